```python
import jax, jax.numpy as jnp
from jax import lax
import numpy as np

D_MODEL = 1024
BATCH = 8
SEQ = 8192
DEPTH = 1

N_META = 16
RET_HEADS = 8
RET_QK_DIM = 128
RET_V_DIM = 128
RET_CHUNK = 128
ROPE_BASE = 10000.0
LRU_WIDTH = D_MODEL
LRU_BLOCKS = 4
LRU_BLOCK = LRU_WIDTH // LRU_BLOCKS
CONV_WIDTH = 4
LRU_C = 8.0
FFN_HIDDEN = ((8 * D_MODEL // 3 + 255) // 256) * 256
RET_QK = RET_HEADS * RET_QK_DIM
RET_V = RET_HEADS * RET_V_DIM
IN_COLS = 2 * RET_QK + 2 * RET_V + 2 * LRU_WIDTH + 2 * D_MODEL
NORM_EPS = 1e-6

kernel_name = 'hybrid_retention_rglru_gated_block'


def rmsnorm(x, w):
    xf = x.astype(jnp.float32)
    y = xf * lax.rsqrt(jnp.mean(xf * xf, axis=-1, keepdims=True) + NORM_EPS)
    return (y * w.astype(jnp.float32)).astype(x.dtype)


def rotary(x, pos):
    d = x.shape[-1]
    inv_freq = ROPE_BASE ** (-jnp.arange(0, d, 2, dtype=jnp.float32) / d)
    ang = pos.astype(jnp.float32)[:, None] * inv_freq[None, :]
    cos = jnp.cos(ang)[None, :, None, :]
    sin = jnp.sin(ang)[None, :, None, :]
    x1, x2 = x[..., : d // 2], x[..., d // 2:]
    return jnp.concatenate([x1 * cos - x2 * sin, x2 * cos + x1 * sin], axis=-1)


def chunk_retention(q, k, v):
    B, T, H, dk = q.shape
    dv = v.shape[-1]
    C = RET_CHUNK
    pad = C - N_META
    padw = ((0, 0), (pad, 0), (0, 0), (0, 0))
    q, k, v = jnp.pad(q, padw), jnp.pad(k, padw), jnp.pad(v, padw)
    Tp = T + pad
    n_chunks = Tp // C
    log_g = jnp.log(1.0 - 2.0 ** (-5.0 - jnp.arange(H, dtype=jnp.float32)))
    idx = jnp.arange(C, dtype=jnp.float32)
    diff = idx[:, None] - idx[None, :]
    intra = jnp.where(diff[None] >= 0,
                      jnp.exp(jnp.maximum(diff, 0.0)[None] * log_g[:, None, None]), 0.0)
    q_decay = jnp.exp((idx + 1.0)[:, None] * log_g[None, :])
    k_decay = jnp.exp((C - 1.0 - idx)[:, None] * log_g[None, :])
    chunk_decay = jnp.exp(C * log_g)

    def to_chunks(a):
        return a.reshape(B, n_chunks, C, H, a.shape[-1]).transpose(1, 0, 2, 3, 4)

    def step(state, qkv):
        qc, kc, vc = qkv
        s = jnp.einsum('bchd,bmhd->bhcm', qc, kc) * intra[None]
        inner = jnp.einsum('bhcm,bmhe->bche', s, vc)
        cross = jnp.einsum('bchd,bhde->bche', qc, state) * q_decay[None, :, :, None]
        state = state * chunk_decay[None, :, None, None] + jnp.einsum(
            'bmhd,bmhe->bhde', kc * k_decay[None, :, :, None], vc)
        return state, inner + cross

    s0 = jnp.zeros((B, H, dk, dv), jnp.float32)
    _, ys = lax.scan(step, s0, (to_chunks(q), to_chunks(k), to_chunks(v)))
    out = ys.transpose(1, 0, 2, 3, 4).reshape(B, Tp, H, dv)
    return out[:, pad:]


def causal_depthwise_conv(x, w, b):
    T = x.shape[1]
    xp = jnp.pad(x, ((0, 0), (CONV_WIDTH - 1, 0), (0, 0)))
    y = b[None, None, :]
    for j in range(CONV_WIDTH):
        y = y + xp[:, j:j + T] * w[j][None, None, :]
    return y


def rg_lru(x, wa, ba, wx, bx, lam):
    B, T, W = x.shape
    xb = x.reshape(B, T, LRU_BLOCKS, LRU_BLOCK)
    r = jax.nn.sigmoid(jnp.einsum('btgi,gij->btgj', xb, wa).reshape(B, T, W) + ba)
    i = jax.nn.sigmoid(jnp.einsum('btgi,gij->btgj', xb, wx).reshape(B, T, W) + bx)
    log_a = -LRU_C * r * jax.nn.softplus(-lam)
    a = jnp.exp(log_a)
    u = jnp.sqrt(-jnp.expm1(2.0 * log_a)) * (i * x)

    def step(h, au):
        a_t, u_t = au
        h = a_t * h + u_t
        return h, h

    _, hs = lax.scan(step, jnp.zeros((B, W), jnp.float32),
                     (a.transpose(1, 0, 2), u.transpose(1, 0, 2)))
    return hs.transpose(1, 0, 2)


def _fwd_setup_inputs(seed: int = 0) -> dict:
    key = jax.random.key(seed)
    ks = jax.random.split(key, 20)
    f32 = jnp.float32
    nrm = lambda k, s, sc: jax.random.normal(k, s, f32) * sc
    a0 = jax.random.uniform(ks[9], (DEPTH, LRU_WIDTH), f32, minval=0.9, maxval=0.999)
    a0r = a0 ** (1.0 / LRU_C)
    return {
        'x': nrm(ks[0], (BATCH, SEQ, D_MODEL), 1.0),
        'meta_tokens': nrm(ks[1], (N_META, D_MODEL), 1.0),
        'mix_norm_w': 1.0 + nrm(ks[2], (DEPTH, D_MODEL), 0.02),
        'w_in': nrm(ks[3], (DEPTH, D_MODEL, IN_COLS), D_MODEL ** -0.5),
        'conv_w': nrm(ks[4], (DEPTH, CONV_WIDTH, LRU_WIDTH), CONV_WIDTH ** -0.5),
        'conv_b': nrm(ks[5], (DEPTH, LRU_WIDTH), 0.01),
        'lru_wa': nrm(ks[6], (DEPTH, LRU_BLOCKS, LRU_BLOCK, LRU_BLOCK), LRU_BLOCK ** -0.5),
        'lru_ba': nrm(ks[7], (DEPTH, LRU_WIDTH), 0.01),
        'lru_wx': nrm(ks[8], (DEPTH, LRU_BLOCKS, LRU_BLOCK, LRU_BLOCK), LRU_BLOCK ** -0.5),
        'lru_bx': nrm(ks[10], (DEPTH, LRU_WIDTH), 0.01),
        'lru_lambda': jnp.log(a0r) - jnp.log1p(-a0r),
        'w_branch_ret': nrm(ks[11], (DEPTH, RET_V, D_MODEL), RET_V ** -0.5),
        'w_branch_lru': nrm(ks[12], (DEPTH, LRU_WIDTH, D_MODEL), LRU_WIDTH ** -0.5),
        'w_out': nrm(ks[13], (DEPTH, D_MODEL, D_MODEL), D_MODEL ** -0.5),
        'ffn_norm_w': 1.0 + nrm(ks[14], (DEPTH, D_MODEL), 0.02),
        'w_ffn_in': nrm(ks[15], (DEPTH, D_MODEL, 2 * FFN_HIDDEN), D_MODEL ** -0.5),
        'w_ffn_out': nrm(ks[16], (DEPTH, FFN_HIDDEN, D_MODEL), FFN_HIDDEN ** -0.5),
        'final_norm_w': 1.0 + nrm(ks[17], (D_MODEL,), 0.02),
    }


def _fwd_reference(x, meta_tokens, mix_norm_w, w_in, conv_w, conv_b, lru_wa, lru_ba, lru_wx,
              lru_bx, lru_lambda, w_branch_ret, w_branch_lru, w_out, ffn_norm_w,
              w_ffn_in, w_ffn_out, final_norm_w):
    B = x.shape[0]
    f32 = jnp.float32
    meta = jnp.broadcast_to(meta_tokens.astype(x.dtype)[None], (B, N_META, D_MODEL))
    h = jnp.concatenate([meta, x], axis=1)
    T = h.shape[1]
    pos = jnp.arange(T)
    sizes = (RET_QK, RET_QK, RET_V, RET_V, LRU_WIDTH, LRU_WIDTH, D_MODEL, D_MODEL)
    split_at = [int(s) for s in np.cumsum(sizes)[:-1]]
    for l in range(DEPTH):
        u = rmsnorm(h, mix_norm_w[l])
        proj = u @ w_in[l]
        q, k, v, g_ret, lru_in, lru_gate, gate_a, gate_b = jnp.split(proj, split_at, axis=-1)
        q = rotary(q.reshape(B, T, RET_HEADS, RET_QK_DIM).astype(f32), pos)
        k = rotary(k.reshape(B, T, RET_HEADS, RET_QK_DIM).astype(f32), pos) * (RET_QK_DIM ** -0.5)
        v = v.reshape(B, T, RET_HEADS, RET_V_DIM).astype(f32)
        o = chunk_retention(q, k, v)
        o = o * lax.rsqrt(jnp.mean(o * o, axis=-1, keepdims=True) + NORM_EPS)
        o = o.reshape(B, T, RET_V).astype(h.dtype)
        y_ret = (jax.nn.silu(g_ret) * o) @ w_branch_ret[l]
        c = causal_depthwise_conv(lru_in.astype(f32), conv_w[l].astype(f32), conv_b[l].astype(f32))
        r = rg_lru(c, lru_wa[l].astype(f32), lru_ba[l].astype(f32), lru_wx[l].astype(f32),
                   lru_bx[l].astype(f32), lru_lambda[l].astype(f32)).astype(h.dtype)
        y_lru = (jax.nn.gelu(lru_gate) * r) @ w_branch_lru[l]
        mixed = jax.nn.sigmoid(gate_a) * y_ret + jax.nn.sigmoid(gate_b) * y_lru
        h = h + mixed @ w_out[l]
        u = rmsnorm(h, ffn_norm_w[l])
        gu = u @ w_ffn_in[l]
        g, up = gu[..., :FFN_HIDDEN], gu[..., FFN_HIDDEN:]
        h = h + (jax.nn.silu(g) * up) @ w_ffn_out[l]
    h = rmsnorm(h, final_norm_w)
    return h[:, N_META:]


import jax as _jax
import jax.numpy as _jnp

TWIN_FORMAT = 'train_step'
FWD_PARAMS = ['x', 'meta_tokens', 'mix_norm_w', 'w_in', 'conv_w', 'conv_b', 'lru_wa', 'lru_ba', 'lru_wx', 'lru_bx', 'lru_lambda', 'w_branch_ret', 'w_branch_lru', 'w_out', 'ffn_norm_w', 'w_ffn_in', 'w_ffn_out', 'final_norm_w']
TWIN_WEIGHTS = ['meta_tokens', 'mix_norm_w', 'w_in', 'conv_w', 'conv_b', 'lru_wa', 'lru_ba', 'lru_wx', 'lru_bx', 'lru_lambda', 'w_branch_ret', 'w_branch_lru', 'w_out', 'ffn_norm_w', 'w_ffn_in', 'w_ffn_out', 'final_norm_w']
TWIN_DIFF_INPUT = 'x'
TWIN_INPUTS = ['x', 'meta_tokens', 'mix_norm_w', 'w_in', 'conv_w', 'conv_b', 'lru_wa', 'lru_ba', 'lru_wx', 'lru_bx', 'lru_lambda', 'w_branch_ret', 'w_branch_lru', 'w_out', 'ffn_norm_w', 'w_ffn_in', 'w_ffn_out', 'final_norm_w', 'loss_target', 'm_meta_tokens', 'm_mix_norm_w', 'm_w_in', 'm_conv_w', 'm_conv_b', 'm_lru_wa', 'm_lru_ba', 'm_lru_wx', 'm_lru_bx', 'm_lru_lambda', 'm_w_branch_ret', 'm_w_branch_lru', 'm_w_out', 'm_ffn_norm_w', 'm_w_ffn_in', 'm_w_ffn_out', 'm_final_norm_w', 'v_meta_tokens', 'v_mix_norm_w', 'v_w_in', 'v_conv_w', 'v_conv_b', 'v_lru_wa', 'v_lru_ba', 'v_lru_wx', 'v_lru_bx', 'v_lru_lambda', 'v_w_branch_ret', 'v_w_branch_lru', 'v_w_out', 'v_ffn_norm_w', 'v_w_ffn_in', 'v_w_ffn_out', 'v_final_norm_w']
TWIN_OUTPUTS = ['loss', 'grad_x', 'grad_meta_tokens', 'grad_mix_norm_w', 'grad_w_in', 'grad_conv_w', 'grad_conv_b', 'grad_lru_wa', 'grad_lru_ba', 'grad_lru_wx', 'grad_lru_bx', 'grad_lru_lambda', 'grad_w_branch_ret', 'grad_w_branch_lru', 'grad_w_out', 'grad_ffn_norm_w', 'grad_w_ffn_in', 'grad_w_ffn_out', 'grad_final_norm_w', 'delta_meta_tokens', 'delta_mix_norm_w', 'delta_w_in', 'delta_conv_w', 'delta_conv_b', 'delta_lru_wa', 'delta_lru_ba', 'delta_lru_wx', 'delta_lru_bx', 'delta_lru_lambda', 'delta_w_branch_ret', 'delta_w_branch_lru', 'delta_w_out', 'delta_ffn_norm_w', 'delta_w_ffn_in', 'delta_w_ffn_out', 'delta_final_norm_w', 'new_m_meta_tokens', 'new_m_mix_norm_w', 'new_m_w_in', 'new_m_conv_w', 'new_m_conv_b', 'new_m_lru_wa', 'new_m_lru_ba', 'new_m_lru_wx', 'new_m_lru_bx', 'new_m_lru_lambda', 'new_m_w_branch_ret', 'new_m_w_branch_lru', 'new_m_w_out', 'new_m_ffn_norm_w', 'new_m_w_ffn_in', 'new_m_w_ffn_out', 'new_m_final_norm_w', 'new_v_meta_tokens', 'new_v_mix_norm_w', 'new_v_w_in', 'new_v_conv_w', 'new_v_conv_b', 'new_v_lru_wa', 'new_v_lru_ba', 'new_v_lru_wx', 'new_v_lru_bx', 'new_v_lru_lambda', 'new_v_w_branch_ret', 'new_v_w_branch_lru', 'new_v_w_out', 'new_v_ffn_norm_w', 'new_v_w_ffn_in', 'new_v_w_ffn_out', 'new_v_final_norm_w']
TWIN_LEAF_KINDS = {'loss': 'loss', 'grad_x': 'grad_x', 'grad_meta_tokens': 'grad_w', 'grad_mix_norm_w': 'grad_w', 'grad_w_in': 'grad_w', 'grad_conv_w': 'grad_w', 'grad_conv_b': 'grad_w', 'grad_lru_wa': 'grad_w', 'grad_lru_ba': 'grad_w', 'grad_lru_wx': 'grad_w', 'grad_lru_bx': 'grad_w', 'grad_lru_lambda': 'grad_w', 'grad_w_branch_ret': 'grad_w', 'grad_w_branch_lru': 'grad_w', 'grad_w_out': 'grad_w', 'grad_ffn_norm_w': 'grad_w', 'grad_w_ffn_in': 'grad_w', 'grad_w_ffn_out': 'grad_w', 'grad_final_norm_w': 'grad_w', 'delta_meta_tokens': 'delta_w', 'delta_mix_norm_w': 'delta_w', 'delta_w_in': 'delta_w', 'delta_conv_w': 'delta_w', 'delta_conv_b': 'delta_w', 'delta_lru_wa': 'delta_w', 'delta_lru_ba': 'delta_w', 'delta_lru_wx': 'delta_w', 'delta_lru_bx': 'delta_w', 'delta_lru_lambda': 'delta_w', 'delta_w_branch_ret': 'delta_w', 'delta_w_branch_lru': 'delta_w', 'delta_w_out': 'delta_w', 'delta_ffn_norm_w': 'delta_w', 'delta_w_ffn_in': 'delta_w', 'delta_w_ffn_out': 'delta_w', 'delta_final_norm_w': 'delta_w', 'new_m_meta_tokens': 'new_m', 'new_m_mix_norm_w': 'new_m', 'new_m_w_in': 'new_m', 'new_m_conv_w': 'new_m', 'new_m_conv_b': 'new_m', 'new_m_lru_wa': 'new_m', 'new_m_lru_ba': 'new_m', 'new_m_lru_wx': 'new_m', 'new_m_lru_bx': 'new_m', 'new_m_lru_lambda': 'new_m', 'new_m_w_branch_ret': 'new_m', 'new_m_w_branch_lru': 'new_m', 'new_m_w_out': 'new_m', 'new_m_ffn_norm_w': 'new_m', 'new_m_w_ffn_in': 'new_m', 'new_m_w_ffn_out': 'new_m', 'new_m_final_norm_w': 'new_m', 'new_v_meta_tokens': 'new_v', 'new_v_mix_norm_w': 'new_v', 'new_v_w_in': 'new_v', 'new_v_conv_w': 'new_v', 'new_v_conv_b': 'new_v', 'new_v_lru_wa': 'new_v', 'new_v_lru_ba': 'new_v', 'new_v_lru_wx': 'new_v', 'new_v_lru_bx': 'new_v', 'new_v_lru_lambda': 'new_v', 'new_v_w_branch_ret': 'new_v', 'new_v_w_branch_lru': 'new_v', 'new_v_w_out': 'new_v', 'new_v_ffn_norm_w': 'new_v', 'new_v_w_ffn_in': 'new_v', 'new_v_w_ffn_out': 'new_v', 'new_v_final_norm_w': 'new_v'}


def _forward(args):
    return _fwd_reference(*[args[k] for k in FWD_PARAMS])


def _output_shape():
    out = _jax.eval_shape(lambda: _forward(_fwd_setup_inputs(0)))
    return out.shape, out.dtype

N_MICROBATCH = 1
ADAM_LR = 0.001
ADAM_B1 = 0.9
ADAM_B2 = 0.999
ADAM_EPS = 1e-08
ADAM_WD = 0.01
ADAM_STEP = 10
PER_EXAMPLE_BATCH_AXIS = {'x': 0, 'loss_target': 0}
SHARED_INPUTS = []
_WEIGHT_DTYPES = {'meta_tokens': _jnp.float32, 'mix_norm_w': _jnp.float32, 'w_in': _jnp.float32, 'conv_w': _jnp.float32, 'conv_b': _jnp.float32, 'lru_wa': _jnp.float32, 'lru_ba': _jnp.float32, 'lru_wx': _jnp.float32, 'lru_bx': _jnp.float32, 'lru_lambda': _jnp.float32, 'w_branch_ret': _jnp.float32, 'w_branch_lru': _jnp.float32, 'w_out': _jnp.float32, 'ffn_norm_w': _jnp.float32, 'w_ffn_in': _jnp.float32, 'w_ffn_out': _jnp.float32, 'final_norm_w': _jnp.float32}
MOMENT_SCALE = {'meta_tokens': 1.099513e-02, 'mix_norm_w': 1.918462e-01, 'w_in': 6.872838e-02, 'conv_w': 6.713436e-02, 'conv_b': 7.676715e-01, 'lru_wa': 1.798613e-02, 'lru_ba': 1.884773e-02, 'lru_wx': 3.225532e-02, 'lru_bx': 2.250529e-02, 'lru_lambda': 3.821707e-02, 'w_branch_ret': 8.464297e-02, 'w_branch_lru': 6.116056e-02, 'w_out': 1.032831e-01, 'ffn_norm_w': 1.807560e-01, 'w_ffn_in': 7.491673e-02, 'w_ffn_out': 1.224321e-01, 'final_norm_w': 6.392210e+01}


def _to_microbatches(a, axis):
    t = _jnp.moveaxis(a, axis, 0)
    t = t.reshape((N_MICROBATCH, t.shape[0] // N_MICROBATCH) + t.shape[1:])
    return _jnp.moveaxis(t, 1, axis + 1)


def setup_inputs(seed: int = 0) -> dict:
    inp = _fwd_setup_inputs(seed)
    key = _jax.random.fold_in(_jax.random.key(seed), 7919)
    shape, _ = _output_shape()
    out = dict(inp)
    out["loss_target"] = _jax.random.normal(_jax.random.fold_in(key, 0), shape, _jnp.float32)
    for i, name in enumerate(TWIN_WEIGHTS):
        w = inp[name].astype(_jnp.float32)
        if MOMENT_SCALE is None:
            s = _jnp.sqrt(_jnp.mean(_jnp.square(w)) + 1e-30)
        else:
            s = MOMENT_SCALE[name]
        km, kv = _jax.random.split(_jax.random.fold_in(key, i + 1))
        out[name] = w
        out["m_" + name] = s * _jax.random.normal(km, w.shape, _jnp.float32)
        out["v_" + name] = (s * s) * _jax.random.uniform(kv, w.shape, _jnp.float32, 0.5, 1.5)
    if N_MICROBATCH > 1:
        for name, axis in PER_EXAMPLE_BATCH_AXIS.items():
            out[name] = _to_microbatches(out[name], axis)
    return {'x': out['x'], 'meta_tokens': out['meta_tokens'], 'mix_norm_w': out['mix_norm_w'], 'w_in': out['w_in'], 'conv_w': out['conv_w'], 'conv_b': out['conv_b'], 'lru_wa': out['lru_wa'], 'lru_ba': out['lru_ba'], 'lru_wx': out['lru_wx'], 'lru_bx': out['lru_bx'], 'lru_lambda': out['lru_lambda'], 'w_branch_ret': out['w_branch_ret'], 'w_branch_lru': out['w_branch_lru'], 'w_out': out['w_out'], 'ffn_norm_w': out['ffn_norm_w'], 'w_ffn_in': out['w_ffn_in'], 'w_ffn_out': out['w_ffn_out'], 'final_norm_w': out['final_norm_w'], 'loss_target': out['loss_target'], 'm_meta_tokens': out['m_meta_tokens'], 'm_mix_norm_w': out['m_mix_norm_w'], 'm_w_in': out['m_w_in'], 'm_conv_w': out['m_conv_w'], 'm_conv_b': out['m_conv_b'], 'm_lru_wa': out['m_lru_wa'], 'm_lru_ba': out['m_lru_ba'], 'm_lru_wx': out['m_lru_wx'], 'm_lru_bx': out['m_lru_bx'], 'm_lru_lambda': out['m_lru_lambda'], 'm_w_branch_ret': out['m_w_branch_ret'], 'm_w_branch_lru': out['m_w_branch_lru'], 'm_w_out': out['m_w_out'], 'm_ffn_norm_w': out['m_ffn_norm_w'], 'm_w_ffn_in': out['m_w_ffn_in'], 'm_w_ffn_out': out['m_w_ffn_out'], 'm_final_norm_w': out['m_final_norm_w'], 'v_meta_tokens': out['v_meta_tokens'], 'v_mix_norm_w': out['v_mix_norm_w'], 'v_w_in': out['v_w_in'], 'v_conv_w': out['v_conv_w'], 'v_conv_b': out['v_conv_b'], 'v_lru_wa': out['v_lru_wa'], 'v_lru_ba': out['v_lru_ba'], 'v_lru_wx': out['v_lru_wx'], 'v_lru_bx': out['v_lru_bx'], 'v_lru_lambda': out['v_lru_lambda'], 'v_w_branch_ret': out['v_w_branch_ret'], 'v_w_branch_lru': out['v_w_branch_lru'], 'v_w_out': out['v_w_out'], 'v_ffn_norm_w': out['v_ffn_norm_w'], 'v_w_ffn_in': out['v_w_ffn_in'], 'v_w_ffn_out': out['v_w_ffn_out'], 'v_final_norm_w': out['v_final_norm_w']}


def _loss(weights, diff, rest, loss_target):
    with _jax.named_scope("forward"):
        args = {**rest, TWIN_DIFF_INPUT: diff, **{k: w.astype(_WEIGHT_DTYPES[k]) for k, w in weights.items()}}
        y = _forward(args)
    with _jax.named_scope("loss_head"):
        err = _jnp.square(y.astype(_jnp.float32) - loss_target)
        return 0.5 * _jnp.sum(_jnp.mean(err, axis=-1)) if err.ndim else 0.5 * err


def _adamw(w, g, m, v):
    m = ADAM_B1 * m + (1.0 - ADAM_B1) * g
    v = ADAM_B2 * v + (1.0 - ADAM_B2) * _jnp.square(g)
    m_hat = m / (1.0 - ADAM_B1 ** ADAM_STEP)
    v_hat = v / (1.0 - ADAM_B2 ** ADAM_STEP)
    delta = -ADAM_LR * (m_hat / (_jnp.sqrt(v_hat) + ADAM_EPS) + ADAM_WD * w)
    return delta, m, v


def reference(x, meta_tokens, mix_norm_w, w_in, conv_w, conv_b, lru_wa, lru_ba, lru_wx, lru_bx, lru_lambda, w_branch_ret, w_branch_lru, w_out, ffn_norm_w, w_ffn_in, w_ffn_out, final_norm_w, loss_target, m_meta_tokens, m_mix_norm_w, m_w_in, m_conv_w, m_conv_b, m_lru_wa, m_lru_ba, m_lru_wx, m_lru_bx, m_lru_lambda, m_w_branch_ret, m_w_branch_lru, m_w_out, m_ffn_norm_w, m_w_ffn_in, m_w_ffn_out, m_final_norm_w, v_meta_tokens, v_mix_norm_w, v_w_in, v_conv_w, v_conv_b, v_lru_wa, v_lru_ba, v_lru_wx, v_lru_bx, v_lru_lambda, v_w_branch_ret, v_w_branch_lru, v_w_out, v_ffn_norm_w, v_w_ffn_in, v_w_ffn_out, v_final_norm_w):
    given = dict(x=x, meta_tokens=meta_tokens, mix_norm_w=mix_norm_w, w_in=w_in, conv_w=conv_w, conv_b=conv_b, lru_wa=lru_wa, lru_ba=lru_ba, lru_wx=lru_wx, lru_bx=lru_bx, lru_lambda=lru_lambda, w_branch_ret=w_branch_ret, w_branch_lru=w_branch_lru, w_out=w_out, ffn_norm_w=ffn_norm_w, w_ffn_in=w_ffn_in, w_ffn_out=w_ffn_out, final_norm_w=final_norm_w, loss_target=loss_target, m_meta_tokens=m_meta_tokens, m_mix_norm_w=m_mix_norm_w, m_w_in=m_w_in, m_conv_w=m_conv_w, m_conv_b=m_conv_b, m_lru_wa=m_lru_wa, m_lru_ba=m_lru_ba, m_lru_wx=m_lru_wx, m_lru_bx=m_lru_bx, m_lru_lambda=m_lru_lambda, m_w_branch_ret=m_w_branch_ret, m_w_branch_lru=m_w_branch_lru, m_w_out=m_w_out, m_ffn_norm_w=m_ffn_norm_w, m_w_ffn_in=m_w_ffn_in, m_w_ffn_out=m_w_ffn_out, m_final_norm_w=m_final_norm_w, v_meta_tokens=v_meta_tokens, v_mix_norm_w=v_mix_norm_w, v_w_in=v_w_in, v_conv_w=v_conv_w, v_conv_b=v_conv_b, v_lru_wa=v_lru_wa, v_lru_ba=v_lru_ba, v_lru_wx=v_lru_wx, v_lru_bx=v_lru_bx, v_lru_lambda=v_lru_lambda, v_w_branch_ret=v_w_branch_ret, v_w_branch_lru=v_w_branch_lru, v_w_out=v_w_out, v_ffn_norm_w=v_ffn_norm_w, v_w_ffn_in=v_w_ffn_in, v_w_ffn_out=v_w_ffn_out, v_final_norm_w=v_final_norm_w)
    weights = {n: given[n] for n in TWIN_WEIGHTS}
    shared = {n: given[n] for n in SHARED_INPUTS}
    per_example = {n: given[n] for n in ['x']}
    grad_fn = _jax.value_and_grad(_loss, argnums=(0, 1))

    def one_microbatch(ex, loss_target):
        ex = dict(ex)
        diff = ex.pop(TWIN_DIFF_INPUT)
        return grad_fn(weights, diff, {**shared, **ex}, loss_target)

    if N_MICROBATCH == 1:
        loss, (grad_w, grad_x) = one_microbatch(per_example, given["loss_target"])
    else:
        def body(carry, xs):
            loss_sum, grad_sum = carry
            l_k, (gw_k, gx_k) = one_microbatch(xs[0], xs[1])
            with _jax.named_scope("update"):
                return (loss_sum + l_k, _jax.tree.map(_jnp.add, grad_sum, gw_k)), gx_k

        init = (_jnp.zeros((), _jnp.float32), _jax.tree.map(_jnp.zeros_like, weights))
        (loss, grad_w), grad_x = _jax.lax.scan(body, init, (per_example, given["loss_target"]))
    with _jax.named_scope("update"):
        delta_w, new_m, new_v = {}, {}, {}
        for n in TWIN_WEIGHTS:
            delta_w[n], new_m[n], new_v[n] = _adamw(weights[n], grad_w[n], given["m_" + n], given["v_" + n])
    return (loss, grad_x, *[grad_w[n] for n in TWIN_WEIGHTS], *[delta_w[n] for n in TWIN_WEIGHTS],
            *[new_m[n] for n in TWIN_WEIGHTS], *[new_v[n] for n in TWIN_WEIGHTS])
```

```python
import functools
import math

import jax
import jax.numpy as jnp
from jax import lax
from jax.experimental import pallas as pl
from jax.experimental.pallas import tpu as pltpu

F32 = jnp.float32
BF16 = jnp.bfloat16

D = 1024
HEADS = 8
DH = 128
CHUNK = 128
N_META = 16
FRONT = 256
PAD_ROWS = FRONT - N_META
LRU_BLOCKS = 4
LRU_BLOCK = 256
LRU_C = 8.0
FFN = 2816
FFN_HALF = FFN // 2
IN_COLS = 8 * D
ROPE_BASE = 10000.0
EPS = 1e-6
QK_SCALE = DH ** -0.5

ADAM_LR = 0.001
ADAM_B1 = 0.9
ADAM_B2 = 0.999
ADAM_EPS = 1e-08
ADAM_WD = 0.01
ADAM_STEP = 10

TM = 256
TM_MIX_BWD = 128
VMEM_LIMIT = 56 * 1024 * 1024

NT_DIMS = (((1,), (1,)), ((), ()))
TN_DIMS = (((0,), (0,)), ((), ()))
MESH = pl.DeviceIdType.MESH


def _params(sem=None):
    if sem is None:
        return pltpu.CompilerParams(vmem_limit_bytes=VMEM_LIMIT)
    return pltpu.CompilerParams(dimension_semantics=sem, vmem_limit_bytes=VMEM_LIMIT)


def _dot(a, b):
    return jnp.dot(a, b, preferred_element_type=F32)


def _dot_nt(a, b):
    return lax.dot_general(a, b, NT_DIMS, preferred_element_type=F32)


def _dot_tn(a, b):
    return lax.dot_general(a, b, TN_DIMS, preferred_element_type=F32)


def _sigmoid(z):
    return 1.0 / (1.0 + jnp.exp(-z))


def _log1p(x):
    return jnp.where(x < 1e-3, x * (1.0 - x * (0.5 - x * (1.0 / 3.0))), jnp.log(1.0 + x))


def _softplus(x):
    return jnp.maximum(x, 0.0) + _log1p(jnp.exp(-jnp.abs(x)))


def _neg_expm1(x):
    series = -x * (1.0 + x * (0.5 + x * (1.0 / 6.0 + x * (1.0 / 24.0 + x * (1.0 / 120.0)))))
    return jnp.where(x > -0.05, series, 1.0 - jnp.exp(x))


_GELU_K = math.sqrt(2.0 / math.pi)


def _gelu_and_grad(x):
    inner = _GELU_K * (x + 0.044715 * x * x * x)
    t = jnp.tanh(inner)
    val = 0.5 * x * (1.0 + t)
    grad = 0.5 * (1.0 + t) + 0.5 * x * (1.0 - t * t) * _GELU_K * (1.0 + 3.0 * 0.044715 * x * x)
    return val, grad


def _row_ids(i, rows, shape):
    return i * rows + lax.broadcasted_iota(jnp.int32, shape, 0)


def _rope_tables(rows):
    pos = jnp.arange(rows, dtype=jnp.int32) - PAD_ROWS
    inv_freq = ROPE_BASE ** (-jnp.arange(0, DH, 2, dtype=F32) / DH)
    ang = pos.astype(F32)[:, None] * inv_freq[None, :]
    cos, sin = jnp.cos(ang), jnp.sin(ang)
    return jnp.concatenate([cos, cos], axis=1), jnp.concatenate([-sin, sin], axis=1)


def _decay_tables():
    log_g = jnp.log(1.0 - 2.0 ** (-5.0 - jnp.arange(HEADS, dtype=F32)))
    idx = jnp.arange(CHUNK, dtype=F32)
    diff = idx[:, None] - idx[None, :]
    intra = jnp.where(diff[None] >= 0, jnp.exp(jnp.maximum(diff, 0.0)[None] * log_g[:, None, None]), 0.0)
    q_decay = jnp.exp((idx + 1.0)[:, None] * log_g[None, :])
    k_decay = jnp.exp((CHUNK - 1.0 - idx)[:, None] * log_g[None, :])
    chunk_decay = jnp.exp(CHUNK * log_g)
    wide = lambda a: jnp.repeat(a, DH, axis=-1)
    return intra, jnp.swapaxes(intra, 1, 2), wide(q_decay), wide(k_decay), wide(chunk_decay[None, :])


def _in_proj(head, x2d, norm_w, w_in, cos_t, sin_t):
    rows = FRONT + x2d.shape[0]
    nt = rows // TM

    def body(head_ref, x_ref, nw_ref, w_ref, cos_ref, sin_ref, pbf_ref, pf_ref, u_ref, rstd_ref, u_sc):
        i = pl.program_id(0)
        g = pl.program_id(1)

        def norm(hv):
            rs = lax.rsqrt(jnp.mean(hv * hv, axis=-1, keepdims=True) + EPS)
            u = ((hv * rs) * nw_ref[...]).astype(BF16)
            u_sc[...] = u
            u_ref[...] = u
            rstd_ref[...] = rs

        @pl.when(jnp.logical_and(g == 0, i == 0))
        def _():
            norm(head_ref[...])

        @pl.when(jnp.logical_and(g == 0, i > 0))
        def _():
            norm(x_ref[...])

        acc = _dot(u_sc[...], w_ref[...])

        @pl.when(g < 2)
        def _():
            scale = jnp.where(g == 1, QK_SCALE, 1.0).astype(F32)
            for h in range(HEADS):
                sl = slice(h * DH, (h + 1) * DH)
                blk = acc[:, sl]
                out = (blk * cos_ref[...] + pltpu.roll(blk, DH // 2, axis=1) * sin_ref[...]) * scale
                pbf_ref[:, sl] = out.astype(BF16)

        @pl.when(g == 2)
        def _():
            pbf_ref[...] = acc.astype(BF16)

        @pl.when(g >= 3)
        def _():
            pf_ref[...] = acc

    return pl.pallas_call(
        body,
        name="in_proj",
        grid=(nt, 8),
        in_specs=[
            pl.BlockSpec((FRONT, D), lambda i, g: (0, 0)),
            pl.BlockSpec((TM, D), lambda i, g: (jnp.maximum(i - 1, 0), 0)),
            pl.BlockSpec((1, D), lambda i, g: (0, 0)),
            pl.BlockSpec((D, D), lambda i, g: (0, g)),
            pl.BlockSpec((TM, DH), lambda i, g: (i, 0)),
            pl.BlockSpec((TM, DH), lambda i, g: (i, 0)),
        ],
        out_specs=[
            pl.BlockSpec((TM, D), lambda i, g: (i, jnp.minimum(g, 2))),
            pl.BlockSpec((TM, D), lambda i, g: (i, jnp.maximum(g - 3, 0))),
            pl.BlockSpec((TM, D), lambda i, g: (i, 0)),
            pl.BlockSpec((TM, 1), lambda i, g: (i, 0)),
        ],
        out_shape=[
            jax.ShapeDtypeStruct((rows, 3 * D), BF16),
            jax.ShapeDtypeStruct((rows, 5 * D), F32),
            jax.ShapeDtypeStruct((rows, D), BF16),
            jax.ShapeDtypeStruct((rows, 1), F32),
        ],
        scratch_shapes=[pltpu.VMEM((TM, D), BF16)],
        compiler_params=_params(("arbitrary", "arbitrary")),
    )(head, x2d, norm_w, w_in, cos_t, sin_t)


def _retention_fwd(proj_bf, intra, q_dec, k_dec, c_dec):
    rows = proj_bf.shape[0]
    nc = rows // CHUNK

    def body(q_ref, k_ref, v_ref, m_ref, qd_ref, kd_ref, cd_ref, o_ref, sprev_ref, s_sc):
        @pl.when(pl.program_id(0) == 0)
        def _():
            s_sc[...] = jnp.zeros_like(s_sc)

        for h in range(HEADS):
            sl = slice(h * DH, (h + 1) * DH)
            q, k, v = q_ref[:, sl], k_ref[:, sl], v_ref[:, sl]
            state = s_sc[h]
            state_b = state.astype(BF16)
            sprev_ref[0, h] = state_b
            s = _dot_nt(q, k) * m_ref[h]
            inner = _dot(s.astype(BF16), v)
            cross = _dot(q, state_b) * qd_ref[:, sl]
            o_ref[:, sl] = inner + cross
            k_scaled = (k.astype(F32) * kd_ref[:, sl]).astype(BF16)
            s_sc[h] = state * cd_ref[:, sl] + _dot_tn(k_scaled, v)

    chunk_spec = lambda col: pl.BlockSpec((CHUNK, D), lambda c: (c, col))
    const2 = lambda shape: pl.BlockSpec(shape, lambda c: (0, 0))
    return pl.pallas_call(
        body,
        name="retention_fwd",
        grid=(nc,),
        in_specs=[
            chunk_spec(0), chunk_spec(1), chunk_spec(2),
            pl.BlockSpec((HEADS, CHUNK, CHUNK), lambda c: (0, 0, 0)),
            const2((CHUNK, D)), const2((CHUNK, D)), const2((1, D)),
        ],
        out_specs=[
            pl.BlockSpec((CHUNK, D), lambda c: (c, 0)),
            pl.BlockSpec((1, HEADS, DH, DH), lambda c: (c, 0, 0, 0)),
        ],
        out_shape=[
            jax.ShapeDtypeStruct((rows, D), F32),
            jax.ShapeDtypeStruct((nc, HEADS, DH, DH), BF16),
        ],
        scratch_shapes=[pltpu.VMEM((HEADS, DH, DH), F32)],
        compiler_params=_params(("arbitrary",)),
    )(proj_bf, proj_bf, proj_bf, intra, q_dec, k_dec, c_dec)


def _shift_down(x, first8_prev, d):
    rolled = pltpu.roll(x, d, axis=0)
    head = pltpu.roll(jnp.concatenate([first8_prev, x[0:8]], axis=0), d, axis=0)[8:16]
    return rolled, head


def _conv_and_gates(x, prev8, cw_ref, cb_ref, wa_ref, wx_ref, ba_ref, bx_ref, lam_ref, c_sc):
    cw = cw_ref[...]
    conv = cb_ref[...] + cw[3:4] * x
    head = cb_ref[...] + cw[3:4] * x[0:8]
    shifted = []
    for d in (1, 2, 3):
        rolled, hd = _shift_down(x, prev8, d)
        conv = conv + cw[3 - d:4 - d] * rolled
        head = head + cw[3 - d:4 - d] * hd
        shifted.append((rolled, hd))
    c_sc[...] = conv
    c_sc[0:8, :] = head
    c = c_sc[...]
    zr, zi = [], []
    for g in range(LRU_BLOCKS):
        sl = slice(g * LRU_BLOCK, (g + 1) * LRU_BLOCK)
        cg = c[:, sl].astype(BF16)
        zr.append(_dot(cg, wa_ref[g]))
        zi.append(_dot(cg, wx_ref[g]))
    r = _sigmoid(jnp.concatenate(zr, axis=1) + ba_ref[...])
    gate_i = _sigmoid(jnp.concatenate(zi, axis=1) + bx_ref[...])
    sp = _softplus(-lam_ref[...])
    log_a = (-LRU_C) * r * sp
    a = jnp.exp(log_a)
    mult = jnp.sqrt(_neg_expm1(2.0 * log_a))
    return c, r, gate_i, a, mult, sp, shifted


def _lru_fwd(proj_f, conv_w, conv_b, wa, wx, ba, bx, lam):
    rows = proj_f.shape[0]
    nt = rows // TM

    def body(x_ref, cw_ref, cb_ref, wa_ref, wx_ref, ba_ref, bx_ref, lam_ref, h_ref, prev_sc, carry_sc, c_sc, a_sc, u_sc):
        i = pl.program_id(0)

        @pl.when(i == 0)
        def _():
            prev_sc[...] = jnp.zeros_like(prev_sc)
            carry_sc[...] = jnp.zeros_like(carry_sc)

        x = x_ref[...]
        c, r, gate_i, a, mult, _, _ = _conv_and_gates(x, prev_sc[...], cw_ref, cb_ref, wa_ref, wx_ref, ba_ref, bx_ref, lam_ref, c_sc)
        prev_sc[...] = x[TM - 8:TM]
        valid = _row_ids(i, TM, (TM, D)) >= PAD_ROWS
        a_sc[...] = a
        u_sc[...] = jnp.where(valid, mult * gate_i * c, 0.0)
        row8 = lax.broadcasted_iota(jnp.int32, (8, D), 0)

        def group(gi, hprev):
            r0 = pl.multiple_of(gi * 8, 8)
            aa = a_sc[pl.ds(r0, 8), :]
            uu = u_sc[pl.ds(r0, 8), :]
            for d in (1, 2, 4):
                a_sh = jnp.where(row8 >= d, pltpu.roll(aa, d, axis=0), 1.0)
                u_sh = jnp.where(row8 >= d, pltpu.roll(uu, d, axis=0), 0.0)
                uu = uu + aa * u_sh
                aa = aa * a_sh
            hb = aa * hprev + uu
            h_ref[pl.ds(r0, 8), :] = hb
            return hb[7:8, :]

        hlast = lax.fori_loop(0, TM // 8, group, carry_sc[0:1, :])
        carry_sc[0:1, :] = hlast

    vec = pl.BlockSpec((1, D), lambda i: (0, 0))
    wspec = pl.BlockSpec((LRU_BLOCKS, LRU_BLOCK, LRU_BLOCK), lambda i: (0, 0, 0))
    return pl.pallas_call(
        body,
        name="lru_fwd",
        grid=(nt,),
        in_specs=[
            pl.BlockSpec((TM, D), lambda i: (i, 1)),
            pl.BlockSpec((4, D), lambda i: (0, 0)),
            vec, wspec, wspec, vec, vec, vec,
        ],
        out_specs=pl.BlockSpec((TM, D), lambda i: (i, 0)),
        out_shape=jax.ShapeDtypeStruct((rows, D), F32),
        scratch_shapes=[
            pltpu.VMEM((8, D), F32), pltpu.VMEM((8, D), F32),
            pltpu.VMEM((TM, D), F32), pltpu.VMEM((TM, D), F32), pltpu.VMEM((TM, D), F32),
        ],
        compiler_params=_params(("arbitrary",)),
    )(proj_f, conv_w, conv_b, wa, wx, ba, bx, lam)


def _group_norm(o):
    outs, rstds = [], []
    for h in range(HEADS):
        oh = o[:, h * DH:(h + 1) * DH]
        rs = lax.rsqrt(jnp.mean(oh * oh, axis=-1, keepdims=True) + EPS)
        outs.append(oh * rs)
        rstds.append(rs)
    return jnp.concatenate(outs, axis=1), rstds


def _mix_fwd(head, x2d, o, proj_f, h_lru, w_br, w_bl, w_o, ffn_norm_w):
    rows = o.shape[0]
    nt = rows // TM

    def body(head_ref, x_ref, o_ref, gret_ref, lgate_ref, ga_ref, gb_ref, hl_ref, wbr_ref, wbl_ref, wo_ref, nw_ref,
             yret_ref, ylru_ref, h1_ref, u2_ref, rstd_ref):
        i = pl.program_id(0)
        on, _ = _group_norm(o_ref[...])
        gret = gret_ref[...]
        a_ret = (gret * _sigmoid(gret) * on).astype(BF16)
        y_ret = _dot(a_ret, wbr_ref[...])
        gl, _ = _gelu_and_grad(lgate_ref[...])
        a_lru = (gl * hl_ref[...]).astype(BF16)
        y_lru = _dot(a_lru, wbl_ref[...])
        mixed = (_sigmoid(ga_ref[...]) * y_ret + _sigmoid(gb_ref[...]) * y_lru).astype(BF16)
        delta = _dot(mixed, wo_ref[...])
        yret_ref[...] = y_ret
        ylru_ref[...] = y_lru

        def finish(h0):
            h1 = h0 + delta
            rs = lax.rsqrt(jnp.mean(h1 * h1, axis=-1, keepdims=True) + EPS)
            h1_ref[...] = h1
            u2_ref[...] = ((h1 * rs) * nw_ref[...]).astype(BF16)
            rstd_ref[...] = rs

        @pl.when(i == 0)
        def _():
            finish(head_ref[...])

        @pl.when(i > 0)
        def _():
            finish(x_ref[...])

    tile = lambda col: pl.BlockSpec((TM, D), lambda i: (i, col))
    wspec = pl.BlockSpec((D, D), lambda i: (0, 0))
    return pl.pallas_call(
        body,
        name="mix_fwd",
        grid=(nt,),
        in_specs=[
            pl.BlockSpec((FRONT, D), lambda i: (0, 0)),
            pl.BlockSpec((TM, D), lambda i: (jnp.maximum(i - 1, 0), 0)),
            tile(0), tile(0), tile(2), tile(3), tile(4), tile(0),
            wspec, wspec, wspec,
            pl.BlockSpec((1, D), lambda i: (0, 0)),
        ],
        out_specs=[tile(0), tile(0), tile(0), tile(0), pl.BlockSpec((TM, 1), lambda i: (i, 0))],
        out_shape=[
            jax.ShapeDtypeStruct((rows, D), F32), jax.ShapeDtypeStruct((rows, D), F32),
            jax.ShapeDtypeStruct((rows, D), F32), jax.ShapeDtypeStruct((rows, D), BF16),
            jax.ShapeDtypeStruct((rows, 1), F32),
        ],
        compiler_params=_params(("arbitrary",)),
    )(head, x2d, o, proj_f, proj_f, proj_f, proj_f, h_lru, w_br, w_bl, w_o, ffn_norm_w)


def _ffn_fwd_loss(u2, h1, target, w_ffn_in, w_ffn_out, final_norm_w):
    rows = u2.shape[0]
    nt = rows // TM

    def body(u2_ref, h1_ref, tgt_ref, wg_ref, wup_ref, wo_ref, fnw_ref, g_ref, up_ref, dh2_ref, stats_ref, acc_sc):
        i = pl.program_id(0)
        j = pl.program_id(1)

        @pl.when(jnp.logical_and(i == 0, j == 0))
        def _():
            stats_ref[...] = jnp.zeros_like(stats_ref)

        u2 = u2_ref[...]
        g = _dot(u2, wg_ref[...])
        up = _dot(u2, wup_ref[...])
        g_ref[...] = g
        up_ref[...] = up
        act = (g * _sigmoid(g) * up).astype(BF16)
        part = _dot(act, wo_ref[...])

        @pl.when(j == 0)
        def _():
            acc_sc[...] = h1_ref[...] + part

        @pl.when(j == 1)
        def _():
            h2 = acc_sc[...] + part
            rs = lax.rsqrt(jnp.mean(h2 * h2, axis=-1, keepdims=True) + EPS)
            n = h2 * rs
            fnw = fnw_ref[...]
            valid = _row_ids(i, TM, (TM, D)) >= FRONT
            diff = jnp.where(valid, n * fnw - tgt_ref[...], 0.0)
            dy = diff * (1.0 / D)
            stats_ref[0:1, :] += (0.5 / D) * jnp.sum(diff * diff, axis=0, keepdims=True)
            stats_ref[1:2, :] += jnp.sum(dy * n, axis=0, keepdims=True)
            dn = dy * fnw
            dh2_ref[...] = rs * (dn - n * jnp.mean(dn * n, axis=-1, keepdims=True))

    return pl.pallas_call(
        body,
        name="ffn_fwd_loss",
        grid=(nt, 2),
        in_specs=[
            pl.BlockSpec((TM, D), lambda i, j: (i, 0)),
            pl.BlockSpec((TM, D), lambda i, j: (i, 0)),
            pl.BlockSpec((TM, D), lambda i, j: (jnp.maximum(i - 1, 0), 0)),
            pl.BlockSpec((D, FFN_HALF), lambda i, j: (0, j)),
            pl.BlockSpec((D, FFN_HALF), lambda i, j: (0, 2 + j)),
            pl.BlockSpec((FFN_HALF, D), lambda i, j: (j, 0)),
            pl.BlockSpec((1, D), lambda i, j: (0, 0)),
        ],
        out_specs=[
            pl.BlockSpec((TM, FFN_HALF), lambda i, j: (i, j)),
            pl.BlockSpec((TM, FFN_HALF), lambda i, j: (i, j)),
            pl.BlockSpec((TM, D), lambda i, j: (i, 0)),
            pl.BlockSpec((8, D), lambda i, j: (0, 0)),
        ],
        out_shape=[
            jax.ShapeDtypeStruct((rows, FFN), F32), jax.ShapeDtypeStruct((rows, FFN), F32),
            jax.ShapeDtypeStruct((rows, D), F32), jax.ShapeDtypeStruct((8, D), F32),
        ],
        scratch_shapes=[pltpu.VMEM((TM, D), F32)],
        compiler_params=_params(("arbitrary", "arbitrary")),
    )(u2, h1, target, w_ffn_in, w_ffn_in, w_ffn_out, final_norm_w)


def _ffn_bwd(dh2, g, up, h1, rstd2, w_ffn_in, w_ffn_out, ffn_norm_w):
    rows = dh2.shape[0]
    nt = rows // TM

    def body(dh2_ref, g_ref, up_ref, h1_ref, rstd_ref, wg_ref, wup_ref, wo_ref, nw_ref,
             dg_ref, dup_ref, act_ref, dh1_ref, stats_ref, du_sc):
        i = pl.program_id(0)
        j = pl.program_id(1)

        @pl.when(jnp.logical_and(i == 0, j == 0))
        def _():
            stats_ref[...] = jnp.zeros_like(stats_ref)

        dh2 = dh2_ref[...]
        dact = _dot_nt(dh2.astype(BF16), wo_ref[...])
        g = g_ref[...]
        up = up_ref[...]
        sg = _sigmoid(g)
        silu = g * sg
        act_ref[...] = (silu * up).astype(BF16)
        dup = (dact * silu).astype(BF16)
        dg = (dact * up * (sg * (1.0 + g * (1.0 - sg)))).astype(BF16)
        dg_ref[...] = dg
        dup_ref[...] = dup
        part = _dot_nt(dg, wg_ref[...]) + _dot_nt(dup, wup_ref[...])

        @pl.when(j == 0)
        def _():
            du_sc[...] = part

        @pl.when(j == 1)
        def _():
            du = du_sc[...] + part
            rs = rstd_ref[...]
            n = h1_ref[...] * rs
            stats_ref[0:1, :] += jnp.sum(du * n, axis=0, keepdims=True)
            dn = du * nw_ref[...]
            dh1_ref[...] = dh2 + rs * (dn - n * jnp.mean(dn * n, axis=-1, keepdims=True))

    return pl.pallas_call(
        body,
        name="ffn_bwd",
        grid=(nt, 2),
        in_specs=[
            pl.BlockSpec((TM, D), lambda i, j: (i, 0)),
            pl.BlockSpec((TM, FFN_HALF), lambda i, j: (i, j)),
            pl.BlockSpec((TM, FFN_HALF), lambda i, j: (i, j)),
            pl.BlockSpec((TM, D), lambda i, j: (i, 0)),
            pl.BlockSpec((TM, 1), lambda i, j: (i, 0)),
            pl.BlockSpec((D, FFN_HALF), lambda i, j: (0, j)),
            pl.BlockSpec((D, FFN_HALF), lambda i, j: (0, 2 + j)),
            pl.BlockSpec((FFN_HALF, D), lambda i, j: (j, 0)),
            pl.BlockSpec((1, D), lambda i, j: (0, 0)),
        ],
        out_specs=[
            pl.BlockSpec((TM, FFN_HALF), lambda i, j: (i, j)),
            pl.BlockSpec((TM, FFN_HALF), lambda i, j: (i, j)),
            pl.BlockSpec((TM, FFN_HALF), lambda i, j: (i, j)),
            pl.BlockSpec((TM, D), lambda i, j: (i, 0)),
            pl.BlockSpec((8, D), lambda i, j: (0, 0)),
        ],
        out_shape=[
            jax.ShapeDtypeStruct((rows, FFN), BF16), jax.ShapeDtypeStruct((rows, FFN), BF16),
            jax.ShapeDtypeStruct((rows, FFN), BF16), jax.ShapeDtypeStruct((rows, D), F32),
            jax.ShapeDtypeStruct((8, D), F32),
        ],
        scratch_shapes=[pltpu.VMEM((TM, D), F32)],
        compiler_params=_params(("arbitrary", "arbitrary")),
    )(dh2, g, up, h1, rstd2, w_ffn_in, w_ffn_in, w_ffn_out, ffn_norm_w)


def _mix_bwd(dh1, o, proj_f, h_lru, y_ret, y_lru, w_br, w_bl, w_o):
    rows = dh1.shape[0]
    tm = TM_MIX_BWD
    nt = rows // tm

    def body(dh1_ref, o_ref, gret_ref, lgate_ref, ga_ref, gb_ref, hl_ref, yret_ref, ylru_ref, wbr_ref, wbl_ref, wo_ref,
             dproj_ref, do_ref, dhl_ref, mixed_ref, aret_ref, alru_ref, dyret_ref, dylru_ref):
        dmixed = _dot_nt(dh1_ref[...].astype(BF16), wo_ref[...])
        y_ret, y_lru = yret_ref[...], ylru_ref[...]
        sa, sb = _sigmoid(ga_ref[...]), _sigmoid(gb_ref[...])
        mixed_ref[...] = (sa * y_ret + sb * y_lru).astype(BF16)
        dga = dmixed * y_ret * sa * (1.0 - sa)
        dgb = dmixed * y_lru * sb * (1.0 - sb)
        dy_ret = (dmixed * sa).astype(BF16)
        dy_lru = (dmixed * sb).astype(BF16)
        dyret_ref[...] = dy_ret
        dylru_ref[...] = dy_lru
        da_ret = _dot_nt(dy_ret, wbr_ref[...])
        da_lru = _dot_nt(dy_lru, wbl_ref[...])

        gret = gret_ref[...]
        sg = _sigmoid(gret)
        silu = gret * sg
        on, rstds = _group_norm(o_ref[...])
        aret_ref[...] = (silu * on).astype(BF16)
        dgret = da_ret * on * (sg * (1.0 + gret * (1.0 - sg)))
        don = da_ret * silu
        for h in range(HEADS):
            sl = slice(h * DH, (h + 1) * DH)
            onh, donh = on[:, sl], don[:, sl]
            do_ref[:, sl] = rstds[h] * (donh - onh * jnp.mean(donh * onh, axis=-1, keepdims=True))

        gl, gl_grad = _gelu_and_grad(lgate_ref[...])
        hl = hl_ref[...]
        alru_ref[...] = (gl * hl).astype(BF16)
        dlgate = da_lru * hl * gl_grad
        dhl_ref[...] = da_lru * gl

        zeros = jnp.zeros((tm, D), BF16)
        for col in (0, 1, 2, 4):
            dproj_ref[:, col * D:(col + 1) * D] = zeros
        dproj_ref[:, 3 * D:4 * D] = dgret.astype(BF16)
        dproj_ref[:, 5 * D:6 * D] = dlgate.astype(BF16)
        dproj_ref[:, 6 * D:7 * D] = dga.astype(BF16)
        dproj_ref[:, 7 * D:8 * D] = dgb.astype(BF16)

    tile = lambda col: pl.BlockSpec((tm, D), lambda i: (i, col))
    wspec = pl.BlockSpec((D, D), lambda i: (0, 0))
    bf = lambda: jax.ShapeDtypeStruct((rows, D), BF16)
    return pl.pallas_call(
        body,
        name="mix_bwd",
        grid=(nt,),
        in_specs=[tile(0), tile(0), tile(0), tile(2), tile(3), tile(4), tile(0), tile(0), tile(0), wspec, wspec, wspec],
        out_specs=[pl.BlockSpec((tm, IN_COLS), lambda i: (i, 0))] + [tile(0)] * 7,
        out_shape=[
            jax.ShapeDtypeStruct((rows, IN_COLS), BF16),
            jax.ShapeDtypeStruct((rows, D), F32), jax.ShapeDtypeStruct((rows, D), F32),
            bf(), bf(), bf(), bf(), bf(),
        ],
        compiler_params=_params(("arbitrary",)),
    )(dh1, o, proj_f, proj_f, proj_f, proj_f, h_lru, y_ret, y_lru, w_br, w_bl, w_o)


def _retention_bwd(dproj, proj_bf, do, sprev, intra, intra_t, q_dec, k_dec, c_dec, cos_t, sin_t):
    rows = proj_bf.shape[0]
    nc = rows // CHUNK

    def body(dproj_in_ref, q_ref, k_ref, v_ref, do_ref, sprev_ref, m_ref, mt_ref, qd_ref, kd_ref, cd_ref, cos_ref, sin_ref,
             dproj_ref, ds_sc):
        @pl.when(pl.program_id(0) == 0)
        def _():
            ds_sc[...] = jnp.zeros_like(ds_sc)

        cos, sin = cos_ref[...], sin_ref[...]

        def unrotate(dy):
            return dy * cos - pltpu.roll(dy, DH // 2, axis=1) * sin

        for h in range(HEADS):
            sl = slice(h * DH, (h + 1) * DH)
            q, k, v = q_ref[:, sl], k_ref[:, sl], v_ref[:, sl]
            do = do_ref[:, sl]
            dob = do.astype(BF16)
            doq = (do * qd_ref[:, sl]).astype(BF16)
            state_prev = sprev_ref[0, h]
            dstate = ds_sc[h]
            dstate_b = dstate.astype(BF16)
            s_t = (_dot_nt(k, q) * mt_ref[h]).astype(BF16)
            ds_t = (_dot_nt(v, dob) * mt_ref[h]).astype(BF16)
            ds = (_dot_nt(dob, v) * m_ref[h]).astype(BF16)
            kd = kd_ref[:, sl]
            dq = _dot(ds, k) + _dot_nt(doq, state_prev)
            dk = _dot(ds_t, q) + _dot_nt(v, dstate_b) * kd
            k_scaled = (k.astype(F32) * kd).astype(BF16)
            dv = _dot(s_t, dob) + _dot(k_scaled, dstate_b)
            ds_sc[h] = dstate * cd_ref[:, sl] + _dot_tn(q, doq)
            dproj_ref[:, sl] = unrotate(dq).astype(BF16)
            dproj_ref[:, D + h * DH:D + (h + 1) * DH] = (unrotate(dk) * QK_SCALE).astype(BF16)
            dproj_ref[:, 2 * D + h * DH:2 * D + (h + 1) * DH] = dv.astype(BF16)

    rev = lambda c: nc - 1 - c
    chunk_spec = lambda col: pl.BlockSpec((CHUNK, D), lambda c: (rev(c), col))
    const2 = lambda shape: pl.BlockSpec(shape, lambda c: (0, 0))
    const3 = pl.BlockSpec((HEADS, CHUNK, CHUNK), lambda c: (0, 0, 0))
    return pl.pallas_call(
        body,
        name="retention_bwd",
        grid=(nc,),
        in_specs=[
            pl.BlockSpec(memory_space=pl.ANY),
            chunk_spec(0), chunk_spec(1), chunk_spec(2), chunk_spec(0),
            pl.BlockSpec((1, HEADS, DH, DH), lambda c: (rev(c), 0, 0, 0)),
            const3, const3,
            const2((CHUNK, D)), const2((CHUNK, D)), const2((1, D)),
            pl.BlockSpec((CHUNK, DH), lambda c: (rev(c), 0)),
            pl.BlockSpec((CHUNK, DH), lambda c: (rev(c), 0)),
        ],
        out_specs=pl.BlockSpec((CHUNK, 3 * D), lambda c: (rev(c), 0)),
        out_shape=jax.ShapeDtypeStruct(dproj.shape, BF16),
        input_output_aliases={0: 0},
        scratch_shapes=[pltpu.VMEM((HEADS, DH, DH), F32)],
        compiler_params=_params(("arbitrary",)),
    )(dproj, proj_bf, proj_bf, proj_bf, do, sprev, intra, intra_t, q_dec, k_dec, c_dec, cos_t, sin_t)


def _lru_bwd(dproj, proj_f, h_lru, dhl, conv_w, conv_b, wa, wx, ba, bx, lam):
    rows = proj_f.shape[0]
    nt = rows // TM
    per8 = TM // 8

    def body(dproj_in_ref, x_ref, xprev_ref, h_ref, hprev_ref, dhl_ref, cw_ref, cb_ref, wa_ref, wx_ref, ba_ref, bx_ref, lam_ref,
             dproj_ref, dwa_ref, dwx_ref, stats_ref, anext_sc, dhnext_sc, dcnext_sc, c_sc, b_sc, dh_sc):
        step = pl.program_id(0)
        i = nt - 1 - step

        @pl.when(step == 0)
        def _():
            anext_sc[...] = jnp.zeros_like(anext_sc)
            dhnext_sc[...] = jnp.zeros_like(dhnext_sc)
            dcnext_sc[...] = jnp.zeros_like(dcnext_sc)
            dwa_ref[...] = jnp.zeros_like(dwa_ref)
            dwx_ref[...] = jnp.zeros_like(dwx_ref)
            stats_ref[...] = jnp.zeros_like(stats_ref)

        first = i == 0
        x = x_ref[...]
        prev8 = jnp.where(first, 0.0, xprev_ref[...])
        c, r, gate_i, a, mult, sp, shifted = _conv_and_gates(x, prev8, cw_ref, cb_ref, wa_ref, wx_ref, ba_ref, bx_ref, lam_ref, c_sc)

        b_sc[...] = pltpu.roll(a, TM - 1, axis=0)
        b_sc[TM - 1:TM, :] = anext_sc[0:1, :]
        anext_sc[0:1, :] = a[0:1, :]
        row8 = lax.broadcasted_iota(jnp.int32, (8, D), 0)

        def group(gi, dhnext):
            r0 = pl.multiple_of((per8 - 1 - gi) * 8, 8)
            bb = b_sc[pl.ds(r0, 8), :]
            uu = dhl_ref[pl.ds(r0, 8), :]
            for d in (1, 2, 4):
                b_sh = jnp.where(row8 < 8 - d, pltpu.roll(bb, 8 - d, axis=0), 1.0)
                u_sh = jnp.where(row8 < 8 - d, pltpu.roll(uu, 8 - d, axis=0), 0.0)
                uu = uu + bb * u_sh
                bb = bb * b_sh
            dhb = bb * dhnext + uu
            dh_sc[pl.ds(r0, 8), :] = dhb
            return dhb[0:1, :]

        dhfirst = lax.fori_loop(0, per8, group, dhnext_sc[0:1, :])
        dhnext_sc[0:1, :] = dhfirst
        dh = dh_sc[...]

        h = h_ref[...]
        hprev8 = jnp.where(first, 0.0, hprev_ref[...])
        h_dn, h_head = _shift_down(h, hprev8, 1)
        c_sc[...] = h_dn
        c_sc[0:8, :] = h_head
        h_before = c_sc[...]

        valid = _row_ids(i, TM, (TM, D)) >= PAD_ROWS
        da = dh * h_before
        du = jnp.where(valid, dh, 0.0)
        dmult = du * gate_i * c
        dgate_i = du * mult * c
        dc = du * mult * gate_i
        dla = da * a - dmult * (a * a) / mult
        dla = jnp.where(valid, dla, 0.0)
        dr = dla * ((-LRU_C) * sp)
        dzr = dr * r * (1.0 - r)
        dzi = dgate_i * gate_i * (1.0 - gate_i)
        stats_ref[1:2, :] += jnp.sum(dzr, axis=0, keepdims=True)
        stats_ref[2:3, :] += jnp.sum(dzi, axis=0, keepdims=True)
        stats_ref[3:4, :] += jnp.sum(dla * ((-LRU_C) * r), axis=0, keepdims=True)
        dc_gate = []
        for g in range(LRU_BLOCKS):
            sl = slice(g * LRU_BLOCK, (g + 1) * LRU_BLOCK)
            cg = c[:, sl].astype(BF16)
            dzr_g = dzr[:, sl].astype(BF16)
            dzi_g = dzi[:, sl].astype(BF16)
            dc_gate.append(_dot_nt(dzr_g, wa_ref[g]) + _dot_nt(dzi_g, wx_ref[g]))
            dwa_ref[g] += _dot_tn(cg, dzr_g)
            dwx_ref[g] += _dot_tn(cg, dzi_g)
        dc = dc + jnp.concatenate(dc_gate, axis=1)

        cw = cw_ref[...]
        stats_ref[0:1, :] += jnp.sum(dc, axis=0, keepdims=True)
        stats_ref[7:8, :] += jnp.sum(dc * x, axis=0, keepdims=True)
        dx = cw[3:4] * dc
        tail_src = jnp.concatenate([dc[TM - 8:TM], dcnext_sc[...]], axis=0)
        dx_tail = cw[3:4] * dc[TM - 8:TM]
        for d in (1, 2, 3):
            dx = dx + cw[3 - d:4 - d] * pltpu.roll(dc, TM - d, axis=0)
            dx_tail = dx_tail + cw[3 - d:4 - d] * pltpu.roll(tail_src, 16 - d, axis=0)[0:8]
            rolled, hd = shifted[d - 1]
            b_sc[...] = rolled
            b_sc[0:8, :] = hd
            stats_ref[7 - d:8 - d, :] += jnp.sum(dc * b_sc[...], axis=0, keepdims=True)
        dcnext_sc[...] = dc[0:8]
        dproj_ref[...] = dx.astype(BF16)
        dproj_ref[TM - 8:TM, :] = dx_tail.astype(BF16)

    rev = lambda s: nt - 1 - s
    vec = pl.BlockSpec((1, D), lambda s: (0, 0))
    wspec = pl.BlockSpec((LRU_BLOCKS, LRU_BLOCK, LRU_BLOCK), lambda s: (0, 0, 0))
    prev8 = lambda col: pl.BlockSpec((8, D), lambda s: (jnp.maximum(rev(s) * per8 - 1, 0), col))
    return pl.pallas_call(
        body,
        name="lru_bwd",
        grid=(nt,),
        in_specs=[
            pl.BlockSpec(memory_space=pl.ANY),
            pl.BlockSpec((TM, D), lambda s: (rev(s), 1)), prev8(1),
            pl.BlockSpec((TM, D), lambda s: (rev(s), 0)), prev8(0),
            pl.BlockSpec((TM, D), lambda s: (rev(s), 0)),
            pl.BlockSpec((4, D), lambda s: (0, 0)),
            vec, wspec, wspec, vec, vec, vec,
        ],
        out_specs=[
            pl.BlockSpec((TM, D), lambda s: (rev(s), 4)),
            wspec, wspec,
            pl.BlockSpec((8, D), lambda s: (0, 0)),
        ],
        out_shape=[
            jax.ShapeDtypeStruct(dproj.shape, BF16),
            jax.ShapeDtypeStruct((LRU_BLOCKS, LRU_BLOCK, LRU_BLOCK), F32),
            jax.ShapeDtypeStruct((LRU_BLOCKS, LRU_BLOCK, LRU_BLOCK), F32),
            jax.ShapeDtypeStruct((8, D), F32),
        ],
        input_output_aliases={0: 0},
        scratch_shapes=[
            pltpu.VMEM((8, D), F32), pltpu.VMEM((8, D), F32), pltpu.VMEM((8, D), F32),
            pltpu.VMEM((TM, D), F32), pltpu.VMEM((TM, D), F32), pltpu.VMEM((TM, D), F32),
        ],
        compiler_params=_params(("arbitrary",)),
    )(dproj, proj_f, proj_f, h_lru, h_lru, dhl, conv_w, conv_b, wa, wx, ba, bx, lam)


def _in_proj_bwd(dproj, w_in, dh1, head, x2d, rstd1, norm_w):
    rows = dproj.shape[0]
    nt = rows // TM

    def body(dproj_ref, w_ref, dh1_ref, head_ref, x_ref, rstd_ref, nw_ref, gx_ref, ghead_ref, stats_ref, du_sc):
        i = pl.program_id(0)
        g = pl.program_id(1)

        @pl.when(jnp.logical_and(i == 0, g == 0))
        def _():
            stats_ref[...] = jnp.zeros_like(stats_ref)

        part = _dot_nt(dproj_ref[...], w_ref[...])

        @pl.when(g == 0)
        def _():
            du_sc[...] = part

        @pl.when(g > 0)
        def _():
            du_sc[...] += part

        def finish(h0, out_ref):
            du = du_sc[...]
            rs = rstd_ref[...]
            n = h0 * rs
            stats_ref[0:1, :] += jnp.sum(du * n, axis=0, keepdims=True)
            dn = du * nw_ref[...]
            out_ref[...] = dh1_ref[...] + rs * (dn - n * jnp.mean(dn * n, axis=-1, keepdims=True))

        @pl.when(jnp.logical_and(g == 7, i == 0))
        def _():
            finish(head_ref[...], ghead_ref)

        @pl.when(jnp.logical_and(g == 7, i > 0))
        def _():
            finish(x_ref[...], gx_ref)

    return pl.pallas_call(
        body,
        name="in_proj_bwd",
        grid=(nt, 8),
        in_specs=[
            pl.BlockSpec((TM, D), lambda i, g: (i, g)),
            pl.BlockSpec((D, D), lambda i, g: (0, g)),
            pl.BlockSpec((TM, D), lambda i, g: (i, 0)),
            pl.BlockSpec((FRONT, D), lambda i, g: (0, 0)),
            pl.BlockSpec((TM, D), lambda i, g: (jnp.maximum(i - 1, 0), 0)),
            pl.BlockSpec((TM, 1), lambda i, g: (i, 0)),
            pl.BlockSpec((1, D), lambda i, g: (0, 0)),
        ],
        out_specs=[
            pl.BlockSpec((TM, D), lambda i, g: (jnp.maximum(i - 1, 0), 0)),
            pl.BlockSpec((FRONT, D), lambda i, g: (0, 0)),
            pl.BlockSpec((8, D), lambda i, g: (0, 0)),
        ],
        out_shape=[
            jax.ShapeDtypeStruct(x2d.shape, F32),
            jax.ShapeDtypeStruct((FRONT, D), F32),
            jax.ShapeDtypeStruct((8, D), F32),
        ],
        scratch_shapes=[pltpu.VMEM((TM, D), F32)],
        compiler_params=_params(("arbitrary", "arbitrary")),
    )(dproj, w_in, dh1, head, x2d, rstd1, norm_w)


def _matmul_tn(name, x, dy, out_cols, col0=0, prev=None, k_block=None, n_block=None):
    rows, kdim = x.shape
    ndim = dy.shape[1]
    tr = 768 if rows % 768 == 0 else TM
    kb = k_block or kdim
    nb = n_block or ndim
    nr, nk, nn = rows // tr, kdim // kb, ndim // nb
    cb0 = col0 // nb

    def body(*refs):
        x_ref, dy_ref, out_ref = refs[-3], refs[-2], refs[-1]
        part = _dot_tn(x_ref[...].astype(BF16), dy_ref[...].astype(BF16))

        @pl.when(pl.program_id(2) == 0)
        def _():
            out_ref[...] = part

        @pl.when(pl.program_id(2) > 0)
        def _():
            out_ref[...] += part

    in_specs = [
        pl.BlockSpec((tr, kb), lambda n, k, r: (r, k)),
        pl.BlockSpec((tr, nb), lambda n, k, r: (r, n)),
    ]
    args = [x, dy]
    aliases = {}
    if prev is not None:
        in_specs = [pl.BlockSpec(memory_space=pl.ANY)] + in_specs
        args = [prev] + args
        aliases = {0: 0}
    return pl.pallas_call(
        body,
        name=name,
        grid=(nn, nk, nr),
        in_specs=in_specs,
        out_specs=pl.BlockSpec((kb, nb), lambda n, k, r: (k, cb0 + n)),
        out_shape=jax.ShapeDtypeStruct((kdim, out_cols), F32),
        input_output_aliases=aliases,
        compiler_params=_params(("arbitrary", "arbitrary", "arbitrary")),
    )(*args)


def _local_step(x2d, target, meta_full, w):
    rows = FRONT + x2d.shape[0]
    head = jnp.concatenate([jnp.zeros((PAD_ROWS, D), F32), meta_full], axis=0)
    cos_t, sin_t = _rope_tables(rows)
    intra, intra_t, q_dec, k_dec, c_dec = _decay_tables()

    proj_bf, proj_f, u1, rstd1 = _in_proj(head, x2d, w["mix_norm_w"], w["w_in"], cos_t, sin_t)
    o, sprev = _retention_fwd(proj_bf, intra, q_dec, k_dec, c_dec)
    lru_args = (w["conv_w"], w["conv_b"], w["lru_wa"], w["lru_wx"], w["lru_ba"], w["lru_bx"], w["lru_lambda"])
    h_lru = _lru_fwd(proj_f, *lru_args)
    y_ret, y_lru, h1, u2, rstd2 = _mix_fwd(head, x2d, o, proj_f, h_lru, w["w_branch_ret"], w["w_branch_lru"], w["w_out"],
                                           w["ffn_norm_w"])
    g, up, dh2, stats_loss = _ffn_fwd_loss(u2, h1, target, w["w_ffn_in"], w["w_ffn_out"], w["final_norm_w"])

    dg, dup, act, dh1, stats_ffn = _ffn_bwd(dh2, g, up, h1, rstd2, w["w_ffn_in"], w["w_ffn_out"], w["ffn_norm_w"])
    (dproj, do, dhl, mixed, a_ret, a_lru, dy_ret, dy_lru) = _mix_bwd(dh1, o, proj_f, h_lru, y_ret, y_lru, w["w_branch_ret"],
                                                                    w["w_branch_lru"], w["w_out"])
    dproj = _retention_bwd(dproj, proj_bf, do, sprev, intra, intra_t, q_dec, k_dec, c_dec, cos_t, sin_t)
    dproj, dwa, dwx, stats_lru = _lru_bwd(dproj, proj_f, h_lru, dhl, *lru_args)
    grad_x, grad_head, stats_in = _in_proj_bwd(dproj, w["w_in"], dh1, head, x2d, rstd1, w["mix_norm_w"])

    grads = {
        "w_in": _matmul_tn("dw_in", u1, dproj, IN_COLS, n_block=D),
        "w_ffn_in": _matmul_tn("dw_ffn_up", u2, dup, 2 * FFN, col0=FFN, n_block=FFN_HALF,
                               prev=_matmul_tn("dw_ffn_gate", u2, dg, 2 * FFN, n_block=FFN_HALF)),
        "w_ffn_out": _matmul_tn("dw_ffn_out", act, dh2, D, k_block=FFN_HALF),
        "w_out": _matmul_tn("dw_out", mixed, dh1, D),
        "w_branch_ret": _matmul_tn("dw_branch_ret", a_ret, dy_ret, D),
        "w_branch_lru": _matmul_tn("dw_branch_lru", a_lru, dy_lru, D),
        "lru_wa": dwa,
        "lru_wx": dwx,
    }
    return grad_x, grad_head, grads, [stats_loss, stats_ffn, stats_in, stats_lru]


BIG_PIECES = {
    "w_in": ("col", (D, 2 * D)),
    "w_ffn_in": ("col", (D, FFN_HALF)),
    "w_ffn_out": ("row", (FFN // 4, D)),
    "w_branch_ret": ("row", (D // 4, D)),
    "w_branch_lru": ("row", (D // 4, D)),
    "w_out": ("row", (D // 4, D)),
    "lru_wa": ("lru", (LRU_BLOCKS, LRU_BLOCK // 4, LRU_BLOCK)),
    "lru_wx": ("lru", (LRU_BLOCKS, LRU_BLOCK // 4, LRU_BLOCK)),
}
SMALL_PIECES = {"meta_tokens": ("col", (N_META, D // 4)), "conv_w": ("col", (4, D // 4))}


def _full_shape(kind, shard):
    if kind == "col":
        return (shard[0], 4 * shard[1])
    if kind == "row":
        return (4 * shard[0], shard[1])
    return (shard[0], 4 * shard[1], shard[2])


def _half_shape(kind, shard):
    return (shard[0] // 2,) + tuple(shard[1:])


def _aligned(start, multiple):
    return start if isinstance(start, int) else pl.multiple_of(start, multiple)


def _lead(h, size):
    if h is None:
        return pl.ds(0, size)
    return pl.ds(_aligned(h * (size // 2), size // 2), size // 2)


def _full_region(ref, kind, shard, s, h):
    if kind == "col":
        return ref.at[_lead(h, shard[0]), pl.ds(_aligned(s * shard[1], shard[1]), shard[1])]
    if kind == "row":
        size = shard[0] if h is None else shard[0] // 2
        start = s * shard[0] + (0 if h is None else h * (shard[0] // 2))
        return ref.at[pl.ds(_aligned(start, 16), size), :]
    return ref.at[_lead(h, shard[0]), pl.ds(_aligned(s * shard[1], shard[1]), shard[1]), :]


def _shard_region(ref, shard, h):
    return ref.at[_lead(h, shard[0])]


def _place():
    x, y, c = lax.axis_index("x"), lax.axis_index("y"), lax.axis_index("c")
    return x, y, c, 2 * x + y


def _other_chip(s, c, k):
    s2 = jnp.bitwise_xor(s, k)
    return s2, (s2 // 2, s2 % 2, c)


def _remote(src, dst, send_sem, recv_sem, dev):
    return pltpu.make_async_remote_copy(src_ref=src, dst_ref=dst, send_sem=send_sem, recv_sem=recv_sem,
                                        device_id=dev, device_id_type=MESH)


_ANY = pl.BlockSpec(memory_space=pl.ANY)


def _all_gather(shards):
    pieces = [(n, *BIG_PIECES[n], True) for n in BIG_PIECES] + [(n, *SMALL_PIECES[n], False) for n in SMALL_PIECES]
    n = len(pieces)

    def body(*refs):
        ins, outs = refs[:n], refs[n:2 * n]
        local_sem, ici_send, ici_recv, fwd_send, fwd_recv = refs[2 * n:]
        x, y, c, s = _place()
        sibling = (x, y, 1 - c)
        started, local = [], []
        for p, (_, kind, shard, split) in enumerate(pieces):
            cp = pltpu.make_async_copy(ins[p], _full_region(outs[p], kind, shard, s, None), local_sem.at[p])
            cp.start()
            local.append(cp)
            h = c if split else None
            for k in (1, 2, 3):
                _, dev = _other_chip(s, c, k)
                cp = _remote(_shard_region(ins[p], shard, h), _full_region(outs[p], kind, shard, s, h),
                             ici_send.at[p, k - 1], ici_recv.at[p, k - 1], dev)
                cp.start()
                started.append(cp)
        for p, (_, kind, shard, split) in enumerate(pieces):
            h = c if split else None
            for k in (1, 2, 3):
                s2, dev = _other_chip(s, c, k)
                region = _full_region(outs[p], kind, shard, s2, h)
                _remote(region, region, ici_send.at[p, k - 1], ici_recv.at[p, k - 1], dev).wait_recv()
                if split:
                    cp = _remote(region, region, fwd_send.at[p, k - 1], fwd_recv.at[p, k - 1], sibling)
                    cp.start()
                    started.append(cp)
        for p, (_, kind, shard, split) in enumerate(pieces):
            if split:
                for k in (1, 2, 3):
                    s2, _ = _other_chip(s, c, k)
                    region = _full_region(outs[p], kind, shard, s2, 1 - c)
                    _remote(region, region, fwd_send.at[p, k - 1], fwd_recv.at[p, k - 1], sibling).wait_recv()
        for cp in started:
            cp.wait_send()
        for cp in local:
            cp.wait()

    outs = pl.pallas_call(
        body,
        name="gather_weights",
        in_specs=[_ANY] * n,
        out_specs=[_ANY] * n,
        out_shape=[jax.ShapeDtypeStruct(_full_shape(kind, shard), shards[name].dtype) for name, kind, shard, _ in pieces],
        scratch_shapes=[pltpu.SemaphoreType.DMA((n,))] + [pltpu.SemaphoreType.DMA((n, 3))] * 4,
    )(*[shards[name] for name, _, _, _ in pieces])
    return {name: o for (name, _, _, _), o in zip(pieces, outs)}


def _pair_exchange(grads):
    names = list(BIG_PIECES)
    n = len(names)

    def body(*refs):
        ins, outs = refs[:n], refs[n:2 * n]
        send_sem, recv_sem = refs[2 * n:]
        x, y, c, _ = _place()
        sibling = (x, y, 1 - c)
        started = []
        for p, name in enumerate(names):
            kind, shard = BIG_PIECES[name]
            for s2 in range(4):
                cp = _remote(_full_region(ins[p], kind, shard, s2, 1 - c), outs[p].at[s2], send_sem.at[p, s2],
                             recv_sem.at[p, s2], sibling)
                cp.start()
                started.append(cp)
        for p, name in enumerate(names):
            for s2 in range(4):
                _remote(outs[p].at[s2], outs[p].at[s2], send_sem.at[p, s2], recv_sem.at[p, s2], sibling).wait_recv()
        for cp in started:
            cp.wait_send()

    outs = pl.pallas_call(
        body,
        name="grad_pair_exchange",
        in_specs=[_ANY] * n,
        out_specs=[_ANY] * n,
        out_shape=[jax.ShapeDtypeStruct((4,) + _half_shape(*BIG_PIECES[name]), F32) for name in names],
        scratch_shapes=[pltpu.SemaphoreType.DMA((n, 4))] * 2,
    )(*[grads[name] for name in names])
    return dict(zip(names, outs))


def _row_tile(kind, shard):
    half = shard[0] // 2
    return half if kind != "lru" else None


def _half_specs(kind, shard):
    half = shard[0] // 2
    if kind == "col":
        full = pl.BlockSpec((half, shard[1]), lambda j, pr: (pr[1], j))
        buf = pl.BlockSpec((None, half, shard[1]), lambda j, pr: (j, 0, 0))
    elif kind == "row":
        full = pl.BlockSpec((half, shard[1]), lambda j, pr: (2 * j + pr[1], 0))
        buf = pl.BlockSpec((None, half, shard[1]), lambda j, pr: (j, 0, 0))
    else:
        full = pl.BlockSpec((half, shard[1], shard[2]), lambda j, pr: (pr[1], j, 0))
        buf = pl.BlockSpec((None, half, shard[1], shard[2]), lambda j, pr: (j, 0, 0, 0))
    return full, buf


def _pair_sum(name, grad, recv, place):
    kind, shard = BIG_PIECES[name]
    full, buf = _half_specs(kind, shard)

    def body(pr, g_ref, r_ref, o_ref):
        o_ref[...] = (g_ref[...] + r_ref[...]).astype(BF16)

    return pl.pallas_call(
        body,
        name="pair_sum_" + name,
        grid_spec=pltpu.PrefetchScalarGridSpec(num_scalar_prefetch=1, grid=(4,), in_specs=[full, buf], out_specs=buf),
        out_shape=jax.ShapeDtypeStruct((4,) + _half_shape(kind, shard), BF16),
        compiler_params=_params(("arbitrary",)),
    )(place, grad, recv)


def _chip_exchange(sums):
    names = list(BIG_PIECES)
    n = len(names)

    def body(*refs):
        ins, outs = refs[:n], refs[n:2 * n]
        send_sem, recv_sem = refs[2 * n:]
        _, _, c, s = _place()
        started = []
        for p in range(n):
            for k in (1, 2, 3):
                s2, dev = _other_chip(s, c, k)
                cp = _remote(ins[p].at[s2], outs[p].at[k - 1], send_sem.at[p, k - 1], recv_sem.at[p, k - 1], dev)
                cp.start()
                started.append(cp)
        for p in range(n):
            for k in (1, 2, 3):
                _, dev = _other_chip(s, c, k)
                _remote(outs[p].at[k - 1], outs[p].at[k - 1], send_sem.at[p, k - 1], recv_sem.at[p, k - 1], dev).wait_recv()
        for cp in started:
            cp.wait_send()

    outs = pl.pallas_call(
        body,
        name="grad_chip_exchange",
        in_specs=[_ANY] * n,
        out_specs=[_ANY] * n,
        out_shape=[jax.ShapeDtypeStruct((3,) + _half_shape(*BIG_PIECES[name]), BF16) for name in names],
        scratch_shapes=[pltpu.SemaphoreType.DMA((n, 3))] * 2,
    )(*[sums[name] for name in names])
    return dict(zip(names, outs))


def _chip_sum(name, grad, recv_pair, recv_chip, place):
    kind, shard = BIG_PIECES[name]
    half = shard[0] // 2
    tail = tuple(shard[1:])
    zeros = (0,) * len(tail)
    if kind == "col":
        full = pl.BlockSpec((half,) + tail, lambda j, pr: (pr[1], pr[0]))
    elif kind == "row":
        full = pl.BlockSpec((half,) + tail, lambda j, pr: (2 * pr[0] + pr[1], 0))
    else:
        full = pl.BlockSpec((half,) + tail, lambda j, pr: (pr[1], pr[0], 0))
    pair = pl.BlockSpec((None, half) + tail, lambda j, pr: (pr[0], 0) + zeros)
    chip = pl.BlockSpec((3, half) + tail, lambda j, pr: (0, 0) + zeros)
    out = pl.BlockSpec((half,) + tail, lambda j, pr: (pr[1],) + zeros)

    def body(pr, g_ref, rp_ref, rc_ref, o_ref):
        total = g_ref[...] + rp_ref[...]
        for k in range(3):
            total = total + rc_ref[k].astype(F32)
        o_ref[...] = total

    return pl.pallas_call(
        body,
        name="chip_sum_" + name,
        grid_spec=pltpu.PrefetchScalarGridSpec(num_scalar_prefetch=1, grid=(1,), in_specs=[full, pair, chip], out_specs=out),
        out_shape=jax.ShapeDtypeStruct(shard, F32),
        compiler_params=_params(("arbitrary",)),
    )(place, grad, recv_pair, recv_chip)


def _sibling_exchange(halves):
    names = list(BIG_PIECES)
    n = len(names)

    def body(*refs):
        outs = refs[n:2 * n]
        send_sem, recv_sem = refs[2 * n:]
        x, y, c, _ = _place()
        sibling = (x, y, 1 - c)
        started = []
        for p, name in enumerate(names):
            shard = BIG_PIECES[name][1]
            mine = _shard_region(outs[p], shard, c)
            cp = _remote(mine, mine, send_sem.at[p], recv_sem.at[p], sibling)
            cp.start()
            started.append(cp)
        for p, name in enumerate(names):
            theirs = _shard_region(outs[p], BIG_PIECES[name][1], 1 - c)
            _remote(theirs, theirs, send_sem.at[p], recv_sem.at[p], sibling).wait_recv()
        for cp in started:
            cp.wait_send()

    outs = pl.pallas_call(
        body,
        name="grad_sibling_exchange",
        in_specs=[_ANY] * n,
        out_specs=[_ANY] * n,
        out_shape=[jax.ShapeDtypeStruct(BIG_PIECES[name][1], F32) for name in names],
        input_output_aliases={p: p for p in range(n)},
        scratch_shapes=[pltpu.SemaphoreType.DMA((n,))] * 2,
    )(*[halves[name] for name in names])
    return dict(zip(names, outs))


def _reduce_scatter(grads, place):
    recv_pair = _pair_exchange(grads)
    sums = {name: _pair_sum(name, grads[name], recv_pair[name], place) for name in BIG_PIECES}
    recv_chip = _chip_exchange(sums)
    halves = {name: _chip_sum(name, grads[name], recv_pair[name], recv_chip[name], place) for name in BIG_PIECES}
    return _sibling_exchange(halves)


def _adamw_math(w, g, m, v):
    m = ADAM_B1 * m + (1.0 - ADAM_B1) * g
    v = ADAM_B2 * v + (1.0 - ADAM_B2) * (g * g)
    m_hat = m / (1.0 - ADAM_B1 ** ADAM_STEP)
    v_hat = v / (1.0 - ADAM_B2 ** ADAM_STEP)
    delta = -ADAM_LR * (m_hat / (jnp.sqrt(v_hat) + ADAM_EPS) + ADAM_WD * w)
    return delta, m, v


def _adamw(name, g, w, m, v):
    rows, cols = g.shape
    tr = rows // 4 if rows % 32 == 0 else rows

    def body(g_ref, w_ref, m_ref, v_ref, go_ref, d_ref, mo_ref, vo_ref):
        gv = g_ref[...]
        delta, m2, v2 = _adamw_math(w_ref[...], gv, m_ref[...], v_ref[...])
        go_ref[...] = gv
        d_ref[...] = delta
        mo_ref[...] = m2
        vo_ref[...] = v2

    spec = pl.BlockSpec((tr, cols), lambda i: (i, 0))
    return pl.pallas_call(
        body,
        name="adamw_" + name,
        grid=(rows // tr,),
        in_specs=[spec] * 4,
        out_specs=[spec] * 4,
        out_shape=[jax.ShapeDtypeStruct((rows, cols), F32)] * 4,
        compiler_params=_params(("arbitrary",)),
    )(g, w, m, v)


SMALL_ROWS = 48
VEC_ROWS = {"final_norm_w": 1, "ffn_norm_w": 8, "mix_norm_w": 16, "conv_b": 24, "lru_ba": 25, "lru_bx": 26, "lru_lambda": 27}
VEC_NAMES = list(VEC_ROWS)
CONV_W_ROW = 28
META_ROW = 32


def _small_all_reduce(partial):
    def body(in_ref, out_ref, buf, local_sem, send_sem, recv_sem):
        x, y, c, s = _place()
        me = 2 * s + c
        cp = pltpu.make_async_copy(in_ref, buf.at[me], local_sem)
        cp.start()
        started = []
        for k in range(1, 8):
            peer = jnp.bitwise_xor(me, k)
            dev = (peer // 4, (peer // 2) % 2, peer % 2)
            rd = _remote(in_ref, buf.at[me], send_sem.at[k - 1], recv_sem.at[k - 1], dev)
            rd.start()
            started.append(rd)
        for k in range(1, 8):
            peer = jnp.bitwise_xor(me, k)
            dev = (peer // 4, (peer // 2) % 2, peer % 2)
            _remote(in_ref, buf.at[peer], send_sem.at[k - 1], recv_sem.at[k - 1], dev).wait_recv()
        for rd in started:
            rd.wait_send()
        cp.wait()
        total = buf[0]
        for d in range(1, 8):
            total = total + buf[d]
        out_ref[...] = total

    return pl.pallas_call(
        body,
        name="small_all_reduce",
        in_specs=[pl.BlockSpec(memory_space=pltpu.VMEM)],
        out_specs=pl.BlockSpec(memory_space=pltpu.VMEM),
        out_shape=jax.ShapeDtypeStruct((SMALL_ROWS, D), F32),
        scratch_shapes=[pltpu.VMEM((8, SMALL_ROWS, D), F32), pltpu.SemaphoreType.DMA,
                        pltpu.SemaphoreType.DMA((7,)), pltpu.SemaphoreType.DMA((7,))],
    )(partial)


def _small_update(total, place, vecs, conv, meta):
    nvec = len(VEC_NAMES)
    qcols = D // 4

    def body(pr, tot_ref, col_ref, *refs):
        vec_refs = refs[:3 * nvec]
        conv_refs = refs[3 * nvec:3 * nvec + 3]
        meta_refs = refs[3 * nvec + 3:3 * nvec + 6]
        outs = refs[3 * nvec + 6:]
        loss_ref, vec_out, conv_out, meta_out = outs[0], outs[1:5], outs[5:9], outs[9:13]
        loss_ref[...] = jnp.sum(tot_ref[0:1, :], axis=1, keepdims=True)
        for o in vec_out:
            o[...] = jnp.zeros_like(o)
        for j, name in enumerate(VEC_NAMES):
            w, m, v = (r[...] for r in vec_refs[3 * j:3 * j + 3])
            g = tot_ref[VEC_ROWS[name]:VEC_ROWS[name] + 1, :]
            if name == "lru_lambda":
                g = -g / (1.0 + jnp.exp(w))
            for o, val in zip(vec_out, (g,) + _adamw_math(w, g, m, v)):
                o[j:j + 1, :] = val
        for row, n_rows, ins, group in ((CONV_W_ROW, 4, conv_refs, conv_out), (META_ROW, N_META, meta_refs, meta_out)):
            g = col_ref[row:row + n_rows, :]
            w, m, v = (r[...] for r in ins)
            for o, val in zip(group, (g,) + _adamw_math(w, g, m, v)):
                o[...] = val

    whole = lambda shape: pl.BlockSpec(shape, lambda i, pr: (0,) * len(shape))
    in_specs = [whole((SMALL_ROWS, D)), pl.BlockSpec((SMALL_ROWS, qcols), lambda i, pr: (0, pr[0]))]
    in_specs += [whole((1, D))] * (3 * nvec) + [whole((4, qcols))] * 3 + [whole((N_META, qcols))] * 3
    out_shapes = [(1, 1)] + [(8, D)] * 4 + [(4, qcols)] * 4 + [(N_META, qcols)] * 4
    return pl.pallas_call(
        body,
        name="small_update",
        grid_spec=pltpu.PrefetchScalarGridSpec(num_scalar_prefetch=1, grid=(1,), in_specs=in_specs,
                                               out_specs=[whole(s) for s in out_shapes]),
        out_shape=[jax.ShapeDtypeStruct(s, F32) for s in out_shapes],
        compiler_params=_params(("arbitrary",)),
    )(place, total, total, *[a for t in vecs for a in t], *conv, *meta)


WEIGHT_ORDER = ["meta_tokens", "mix_norm_w", "w_in", "conv_w", "conv_b", "lru_wa", "lru_ba", "lru_wx", "lru_bx", "lru_lambda",
                "w_branch_ret", "w_branch_lru", "w_out", "ffn_norm_w", "w_ffn_in", "w_ffn_out", "final_norm_w"]


def kernel(x, meta_tokens, mix_norm_w, w_in, conv_w, conv_b, lru_wa, lru_ba, lru_wx, lru_bx, lru_lambda, w_branch_ret, w_branch_lru, w_out, ffn_norm_w, w_ffn_in, w_ffn_out, final_norm_w, loss_target, m_meta_tokens, m_mix_norm_w, m_w_in, m_conv_w, m_conv_b, m_lru_wa, m_lru_ba, m_lru_wx, m_lru_bx, m_lru_lambda, m_w_branch_ret, m_w_branch_lru, m_w_out, m_ffn_norm_w, m_w_ffn_in, m_w_ffn_out, m_final_norm_w, v_meta_tokens, v_mix_norm_w, v_w_in, v_conv_w, v_conv_b, v_lru_wa, v_lru_ba, v_lru_wx, v_lru_bx, v_lru_lambda, v_w_branch_ret, v_w_branch_lru, v_w_out, v_ffn_norm_w, v_w_ffn_in, v_w_ffn_out, v_final_norm_w):
    args = locals()
    wts = {n: args[n] for n in WEIGHT_ORDER}
    mom = {n: args["m_" + n] for n in WEIGHT_ORDER}
    var = {n: args["v_" + n] for n in WEIGHT_ORDER}
    place = jnp.stack([2 * lax.axis_index("x") + lax.axis_index("y"), lax.axis_index("c")]).astype(jnp.int32)

    shards = {n: wts[n][0].astype(BF16) for n in BIG_PIECES}
    shards["meta_tokens"] = wts["meta_tokens"]
    shards["conv_w"] = wts["conv_w"][0]
    gathered = _all_gather(shards)
    w = {n: gathered[n] for n in BIG_PIECES}
    w["conv_w"] = gathered["conv_w"]
    for n in VEC_NAMES:
        w[n] = wts[n].reshape(1, D)

    grad_x, grad_head, grads, stats = _local_step(x[0], loss_target[0], gathered["meta_tokens"], w)
    shard_grads = _reduce_scatter(grads, place)

    out = {}
    for n in BIG_PIECES:
        shape2d = (-1, wts[n].shape[-1])
        res = _adamw(n, *[a.reshape(shape2d) for a in (shard_grads[n], wts[n], mom[n], var[n])])
        out[n] = [r.reshape(wts[n].shape) for r in res]

    partial = jnp.concatenate(stats + [grad_head[PAD_ROWS:]], axis=0)
    total = _small_all_reduce(partial)
    vecs = [tuple(a[n].reshape(1, D) for a in (wts, mom, var)) for n in VEC_NAMES]
    conv = tuple(a["conv_w"][0] for a in (wts, mom, var))
    meta = tuple(a["meta_tokens"] for a in (wts, mom, var))
    res = _small_update(total, place, vecs, conv, meta)
    loss = res[0].reshape(())
    for j, n in enumerate(VEC_NAMES):
        out[n] = [r[j].reshape(wts[n].shape) for r in res[1:5]]
    out["conv_w"] = [r.reshape(wts["conv_w"].shape) for r in res[5:9]]
    out["meta_tokens"] = list(res[9:13])

    return (loss, grad_x.reshape(x.shape)) + tuple(out[n][kind] for kind in range(4) for n in WEIGHT_ORDER)
```

```python
import functools
import math

import jax
import jax.numpy as jnp
from jax import lax
from jax.experimental import pallas as pl
from jax.experimental.pallas import tpu as pltpu

F32 = jnp.float32
BF16 = jnp.bfloat16

D = 1024
HEADS = 8
DH = 128
CHUNK = 128
N_META = 16
FRONT = 256
PAD_ROWS = FRONT - N_META
LRU_BLOCKS = 4
LRU_BLOCK = 256
LRU_C = 8.0
FFN = 2816
FFN_HALF = FFN // 2
IN_COLS = 8 * D
ROPE_BASE = 10000.0
EPS = 1e-6
QK_SCALE = DH ** -0.5

ADAM_LR = 0.001
ADAM_B1 = 0.9
ADAM_B2 = 0.999
ADAM_EPS = 1e-08
ADAM_WD = 0.01
ADAM_STEP = 10

TM = 256
TM_HEAVY = 768
FFN_BLOCK = 256
TM_MIX_BWD = 128
VMEM_LIMIT = 56 * 1024 * 1024

NT_DIMS = (((1,), (1,)), ((), ()))
TN_DIMS = (((0,), (0,)), ((), ()))
MESH = pl.DeviceIdType.MESH


def _params(sem=None):
    if sem is None:
        return pltpu.CompilerParams(vmem_limit_bytes=VMEM_LIMIT)
    return pltpu.CompilerParams(dimension_semantics=sem, vmem_limit_bytes=VMEM_LIMIT)


def _dot(a, b):
    return jnp.dot(a, b, preferred_element_type=F32)


def _dot_nt(a, b):
    return lax.dot_general(a, b, NT_DIMS, preferred_element_type=F32)


def _dot_tn(a, b):
    return lax.dot_general(a, b, TN_DIMS, preferred_element_type=F32)


def _sigmoid(z):
    return 1.0 / (1.0 + jnp.exp(-z))


def _log1p(x):
    return jnp.where(x < 1e-3, x * (1.0 - x * (0.5 - x * (1.0 / 3.0))), jnp.log(1.0 + x))


def _softplus(x):
    return jnp.maximum(x, 0.0) + _log1p(jnp.exp(-jnp.abs(x)))


def _neg_expm1(x):
    series = -x * (1.0 + x * (0.5 + x * (1.0 / 6.0 + x * (1.0 / 24.0 + x * (1.0 / 120.0)))))
    return jnp.where(x > -0.05, series, 1.0 - jnp.exp(x))


_GELU_K = math.sqrt(2.0 / math.pi)


def _gelu_and_grad(x):
    inner = _GELU_K * (x + 0.044715 * x * x * x)
    t = jnp.tanh(inner)
    val = 0.5 * x * (1.0 + t)
    grad = 0.5 * (1.0 + t) + 0.5 * x * (1.0 - t * t) * _GELU_K * (1.0 + 3.0 * 0.044715 * x * x)
    return val, grad


def _row_ids(i, rows, shape):
    return i * rows + lax.broadcasted_iota(jnp.int32, shape, 0)


def _rope_tables(rows):
    pos = jnp.arange(rows, dtype=jnp.int32) - PAD_ROWS
    inv_freq = ROPE_BASE ** (-jnp.arange(0, DH, 2, dtype=F32) / DH)
    ang = pos.astype(F32)[:, None] * inv_freq[None, :]
    cos, sin = jnp.cos(ang), jnp.sin(ang)
    return jnp.concatenate([cos, cos], axis=1), jnp.concatenate([-sin, sin], axis=1)


def _decay_tables():
    log_g = jnp.log(1.0 - 2.0 ** (-5.0 - jnp.arange(HEADS, dtype=F32)))
    idx = jnp.arange(CHUNK, dtype=F32)
    diff = idx[:, None] - idx[None, :]
    intra = jnp.where(diff[None] >= 0, jnp.exp(jnp.maximum(diff, 0.0)[None] * log_g[:, None, None]), 0.0)
    q_decay = jnp.exp((idx + 1.0)[:, None] * log_g[None, :])
    k_decay = jnp.exp((CHUNK - 1.0 - idx)[:, None] * log_g[None, :])
    chunk_decay = jnp.exp(CHUNK * log_g)
    wide = lambda a: jnp.repeat(a, DH, axis=-1)
    return intra, jnp.swapaxes(intra, 1, 2), wide(q_decay), wide(k_decay), wide(chunk_decay[None, :])


def _norm1(head, x2d, norm_w):
    rows = FRONT + x2d.shape[0]

    def body(head_ref, x_ref, nw_ref, u_ref, rstd_ref):
        def norm(hv):
            rs = lax.rsqrt(jnp.mean(hv * hv, axis=-1, keepdims=True) + EPS)
            u_ref[...] = ((hv * rs) * nw_ref[...]).astype(BF16)
            rstd_ref[...] = rs

        @pl.when(pl.program_id(0) == 0)
        def _():
            norm(head_ref[...])

        @pl.when(pl.program_id(0) > 0)
        def _():
            norm(x_ref[...])

    return pl.pallas_call(
        body,
        name="norm1",
        grid=(rows // TM,),
        in_specs=[
            pl.BlockSpec((FRONT, D), lambda i: (0, 0)),
            pl.BlockSpec((TM, D), lambda i: (jnp.maximum(i - 1, 0), 0)),
            pl.BlockSpec((1, D), lambda i: (0, 0)),
        ],
        out_specs=[pl.BlockSpec((TM, D), lambda i: (i, 0)), pl.BlockSpec((TM, 1), lambda i: (i, 0))],
        out_shape=[jax.ShapeDtypeStruct((rows, D), BF16), jax.ShapeDtypeStruct((rows, 1), F32)],
        compiler_params=_params(("arbitrary",)),
    )(head, x2d, norm_w)


def _heavy_tile(rows):
    return TM_HEAVY if rows % TM_HEAVY == 0 else TM


def _in_proj(u, w_in, cos_t, sin_t):
    rows = u.shape[0]
    tm = _heavy_tile(rows)

    def body(u_ref, w_ref, cos_ref, sin_ref, pbf_ref, pf_ref):
        g = pl.program_id(1)
        acc = _dot(u_ref[...], w_ref[...])

        @pl.when(g < 2)
        def _():
            scale = jnp.where(g == 1, QK_SCALE, 1.0).astype(F32)
            for h in range(HEADS):
                sl = slice(h * DH, (h + 1) * DH)
                blk = acc[:, sl]
                out = (blk * cos_ref[...] + pltpu.roll(blk, DH // 2, axis=1) * sin_ref[...]) * scale
                pbf_ref[:, sl] = out.astype(BF16)

        @pl.when(g == 2)
        def _():
            pbf_ref[...] = acc.astype(BF16)

        @pl.when(g >= 3)
        def _():
            pf_ref[...] = acc

    return pl.pallas_call(
        body,
        name="in_proj",
        grid=(rows // tm, 8),
        in_specs=[
            pl.BlockSpec((tm, D), lambda i, g: (i, 0)),
            pl.BlockSpec((D, D), lambda i, g: (0, g)),
            pl.BlockSpec((tm, DH), lambda i, g: (i, 0)),
            pl.BlockSpec((tm, DH), lambda i, g: (i, 0)),
        ],
        out_specs=[
            pl.BlockSpec((tm, D), lambda i, g: (i, jnp.minimum(g, 2))),
            pl.BlockSpec((tm, D), lambda i, g: (i, jnp.maximum(g - 3, 0))),
        ],
        out_shape=[jax.ShapeDtypeStruct((rows, 3 * D), BF16), jax.ShapeDtypeStruct((rows, 5 * D), F32)],
        compiler_params=_params(("arbitrary", "arbitrary")),
    )(u, w_in, cos_t, sin_t)


def _retention_fwd(proj_bf, intra, q_dec, k_dec, c_dec):
    rows = proj_bf.shape[0]
    nc = rows // CHUNK

    def body(q_ref, k_ref, v_ref, m_ref, qd_ref, kd_ref, cd_ref, o_ref, sprev_ref, s_sc):
        @pl.when(pl.program_id(0) == 0)
        def _():
            s_sc[...] = jnp.zeros_like(s_sc)

        for h in range(HEADS):
            sl = slice(h * DH, (h + 1) * DH)
            q, k, v = q_ref[:, sl], k_ref[:, sl], v_ref[:, sl]
            state = s_sc[h]
            state_b = state.astype(BF16)
            sprev_ref[0, h] = state_b
            s = _dot_nt(q, k) * m_ref[h]
            inner = _dot(s.astype(BF16), v)
            cross = _dot(q, state_b) * qd_ref[:, sl]
            o_ref[:, sl] = inner + cross
            k_scaled = (k.astype(F32) * kd_ref[:, sl]).astype(BF16)
            s_sc[h] = state * cd_ref[:, sl] + _dot_tn(k_scaled, v)

    chunk_spec = lambda col: pl.BlockSpec((CHUNK, D), lambda c: (c, col))
    const2 = lambda shape: pl.BlockSpec(shape, lambda c: (0, 0))
    return pl.pallas_call(
        body,
        name="retention_fwd",
        grid=(nc,),
        in_specs=[
            chunk_spec(0), chunk_spec(1), chunk_spec(2),
            pl.BlockSpec((HEADS, CHUNK, CHUNK), lambda c: (0, 0, 0)),
            const2((CHUNK, D)), const2((CHUNK, D)), const2((1, D)),
        ],
        out_specs=[
            pl.BlockSpec((CHUNK, D), lambda c: (c, 0)),
            pl.BlockSpec((1, HEADS, DH, DH), lambda c: (c, 0, 0, 0)),
        ],
        out_shape=[
            jax.ShapeDtypeStruct((rows, D), F32),
            jax.ShapeDtypeStruct((nc, HEADS, DH, DH), BF16),
        ],
        scratch_shapes=[pltpu.VMEM((HEADS, DH, DH), F32)],
        compiler_params=_params(("arbitrary",)),
    )(proj_bf, proj_bf, proj_bf, intra, q_dec, k_dec, c_dec)


def _shift_down(x, first8_prev, d):
    rolled = pltpu.roll(x, d, axis=0)
    head = pltpu.roll(jnp.concatenate([first8_prev, x[0:8]], axis=0), d, axis=0)[8:16]
    return rolled, head


def _conv_and_gates(x, prev8, cw_ref, cb_ref, wa_ref, wx_ref, ba_ref, bx_ref, lam_ref, c_sc):
    cw = cw_ref[...]
    conv = cb_ref[...] + cw[3:4] * x
    head = cb_ref[...] + cw[3:4] * x[0:8]
    shifted = []
    for d in (1, 2, 3):
        rolled, hd = _shift_down(x, prev8, d)
        conv = conv + cw[3 - d:4 - d] * rolled
        head = head + cw[3 - d:4 - d] * hd
        shifted.append((rolled, hd))
    c_sc[...] = conv
    c_sc[0:8, :] = head
    c = c_sc[...]
    zr, zi = [], []
    for g in range(LRU_BLOCKS):
        sl = slice(g * LRU_BLOCK, (g + 1) * LRU_BLOCK)
        cg = c[:, sl].astype(BF16)
        zr.append(_dot(cg, wa_ref[g]))
        zi.append(_dot(cg, wx_ref[g]))
    r = _sigmoid(jnp.concatenate(zr, axis=1) + ba_ref[...])
    gate_i = _sigmoid(jnp.concatenate(zi, axis=1) + bx_ref[...])
    sp = _softplus(-lam_ref[...])
    log_a = (-LRU_C) * r * sp
    a = jnp.exp(log_a)
    mult = jnp.sqrt(_neg_expm1(2.0 * log_a))
    return c, r, gate_i, a, mult, sp, shifted


def _lru_fwd(proj_f, conv_w, conv_b, wa, wx, ba, bx, lam):
    rows = proj_f.shape[0]
    nt = rows // TM

    def body(x_ref, cw_ref, cb_ref, wa_ref, wx_ref, ba_ref, bx_ref, lam_ref, h_ref, prev_sc, carry_sc, c_sc, a_sc, u_sc):
        i = pl.program_id(0)

        @pl.when(i == 0)
        def _():
            prev_sc[...] = jnp.zeros_like(prev_sc)
            carry_sc[...] = jnp.zeros_like(carry_sc)

        x = x_ref[...]
        c, r, gate_i, a, mult, _, _ = _conv_and_gates(x, prev_sc[...], cw_ref, cb_ref, wa_ref, wx_ref, ba_ref, bx_ref, lam_ref, c_sc)
        prev_sc[...] = x[TM - 8:TM]
        valid = _row_ids(i, TM, (TM, D)) >= PAD_ROWS
        a_sc[...] = a
        u_sc[...] = jnp.where(valid, mult * gate_i * c, 0.0)
        row8 = lax.broadcasted_iota(jnp.int32, (8, D), 0)

        def group(gi, hprev):
            r0 = pl.multiple_of(gi * 8, 8)
            aa = a_sc[pl.ds(r0, 8), :]
            uu = u_sc[pl.ds(r0, 8), :]
            for d in (1, 2, 4):
                a_sh = jnp.where(row8 >= d, pltpu.roll(aa, d, axis=0), 1.0)
                u_sh = jnp.where(row8 >= d, pltpu.roll(uu, d, axis=0), 0.0)
                uu = uu + aa * u_sh
                aa = aa * a_sh
            hb = aa * hprev + uu
            h_ref[pl.ds(r0, 8), :] = hb
            return hb[7:8, :]

        hlast = lax.fori_loop(0, TM // 8, group, carry_sc[0:1, :])
        carry_sc[0:1, :] = hlast

    vec = pl.BlockSpec((1, D), lambda i: (0, 0))
    wspec = pl.BlockSpec((LRU_BLOCKS, LRU_BLOCK, LRU_BLOCK), lambda i: (0, 0, 0))
    return pl.pallas_call(
        body,
        name="lru_fwd",
        grid=(nt,),
        in_specs=[
            pl.BlockSpec((TM, D), lambda i: (i, 1)),
            pl.BlockSpec((4, D), lambda i: (0, 0)),
            vec, wspec, wspec, vec, vec, vec,
        ],
        out_specs=pl.BlockSpec((TM, D), lambda i: (i, 0)),
        out_shape=jax.ShapeDtypeStruct((rows, D), F32),
        scratch_shapes=[
            pltpu.VMEM((8, D), F32), pltpu.VMEM((8, D), F32),
            pltpu.VMEM((TM, D), F32), pltpu.VMEM((TM, D), F32), pltpu.VMEM((TM, D), F32),
        ],
        compiler_params=_params(("arbitrary",)),
    )(proj_f, conv_w, conv_b, wa, wx, ba, bx, lam)


def _group_norm(o):
    outs, rstds = [], []
    for h in range(HEADS):
        oh = o[:, h * DH:(h + 1) * DH]
        rs = lax.rsqrt(jnp.mean(oh * oh, axis=-1, keepdims=True) + EPS)
        outs.append(oh * rs)
        rstds.append(rs)
    return jnp.concatenate(outs, axis=1), rstds


def _mix_fwd(head, x2d, o, proj_f, h_lru, w_br, w_bl, w_o, ffn_norm_w):
    rows = o.shape[0]
    nt = rows // TM

    def body(head_ref, x_ref, o_ref, gret_ref, lgate_ref, ga_ref, gb_ref, hl_ref, wbr_ref, wbl_ref, wo_ref, nw_ref,
             yret_ref, ylru_ref, h1_ref, u2_ref, rstd_ref):
        i = pl.program_id(0)
        on, _ = _group_norm(o_ref[...])
        gret = gret_ref[...]
        a_ret = (gret * _sigmoid(gret) * on).astype(BF16)
        y_ret = _dot(a_ret, wbr_ref[...])
        gl, _ = _gelu_and_grad(lgate_ref[...])
        a_lru = (gl * hl_ref[...]).astype(BF16)
        y_lru = _dot(a_lru, wbl_ref[...])
        mixed = (_sigmoid(ga_ref[...]) * y_ret + _sigmoid(gb_ref[...]) * y_lru).astype(BF16)
        delta = _dot(mixed, wo_ref[...])
        yret_ref[...] = y_ret
        ylru_ref[...] = y_lru

        def finish(h0):
            h1 = h0 + delta
            rs = lax.rsqrt(jnp.mean(h1 * h1, axis=-1, keepdims=True) + EPS)
            h1_ref[...] = h1
            u2_ref[...] = ((h1 * rs) * nw_ref[...]).astype(BF16)
            rstd_ref[...] = rs

        @pl.when(i == 0)
        def _():
            finish(head_ref[...])

        @pl.when(i > 0)
        def _():
            finish(x_ref[...])

    tile = lambda col: pl.BlockSpec((TM, D), lambda i: (i, col))
    wspec = pl.BlockSpec((D, D), lambda i: (0, 0))
    return pl.pallas_call(
        body,
        name="mix_fwd",
        grid=(nt,),
        in_specs=[
            pl.BlockSpec((FRONT, D), lambda i: (0, 0)),
            pl.BlockSpec((TM, D), lambda i: (jnp.maximum(i - 1, 0), 0)),
            tile(0), tile(0), tile(2), tile(3), tile(4), tile(0),
            wspec, wspec, wspec,
            pl.BlockSpec((1, D), lambda i: (0, 0)),
        ],
        out_specs=[tile(0), tile(0), tile(0), tile(0), pl.BlockSpec((TM, 1), lambda i: (i, 0))],
        out_shape=[
            jax.ShapeDtypeStruct((rows, D), F32), jax.ShapeDtypeStruct((rows, D), F32),
            jax.ShapeDtypeStruct((rows, D), F32), jax.ShapeDtypeStruct((rows, D), BF16),
            jax.ShapeDtypeStruct((rows, 1), F32),
        ],
        compiler_params=_params(("arbitrary",)),
    )(head, x2d, o, proj_f, proj_f, proj_f, proj_f, h_lru, w_br, w_bl, w_o, ffn_norm_w)


def _ffn_fwd(u2, h1, w_ffn_in, w_ffn_out):
    rows = u2.shape[0]
    tm = _heavy_tile(rows)
    nb = FFN // FFN_BLOCK

    def body(u2_ref, h1_ref, wg_ref, wup_ref, wo_ref, g_ref, up_ref, h2_ref):
        j = pl.program_id(1)
        u2 = u2_ref[...]
        g = _dot(u2, wg_ref[...])
        up = _dot(u2, wup_ref[...])
        g_ref[...] = g.astype(BF16)
        up_ref[...] = up.astype(BF16)
        part = _dot((g * _sigmoid(g) * up).astype(BF16), wo_ref[...])

        @pl.when(j == 0)
        def _():
            h2_ref[...] = h1_ref[...] + part

        @pl.when(j > 0)
        def _():
            h2_ref[...] += part

    return pl.pallas_call(
        body,
        name="ffn_fwd",
        grid=(rows // tm, nb),
        in_specs=[
            pl.BlockSpec((tm, D), lambda i, j: (i, 0)),
            pl.BlockSpec((tm, D), lambda i, j: (i, 0)),
            pl.BlockSpec((D, FFN_BLOCK), lambda i, j: (0, j)),
            pl.BlockSpec((D, FFN_BLOCK), lambda i, j: (0, nb + j)),
            pl.BlockSpec((FFN_BLOCK, D), lambda i, j: (j, 0)),
        ],
        out_specs=[
            pl.BlockSpec((tm, FFN_BLOCK), lambda i, j: (i, j)),
            pl.BlockSpec((tm, FFN_BLOCK), lambda i, j: (i, j)),
            pl.BlockSpec((tm, D), lambda i, j: (i, 0)),
        ],
        out_shape=[
            jax.ShapeDtypeStruct((rows, FFN), BF16), jax.ShapeDtypeStruct((rows, FFN), BF16),
            jax.ShapeDtypeStruct((rows, D), F32),
        ],
        compiler_params=_params(("arbitrary", "arbitrary")),
    )(u2, h1, w_ffn_in, w_ffn_in, w_ffn_out)


def _final_loss(h2, target, final_norm_w):
    rows = h2.shape[0]

    def body(h2_ref, tgt_ref, fnw_ref, dh2_ref, stats_ref):
        i = pl.program_id(0)

        @pl.when(i == 0)
        def _():
            stats_ref[...] = jnp.zeros_like(stats_ref)

        h2 = h2_ref[...]
        rs = lax.rsqrt(jnp.mean(h2 * h2, axis=-1, keepdims=True) + EPS)
        n = h2 * rs
        fnw = fnw_ref[...]
        valid = _row_ids(i, TM, (TM, D)) >= FRONT
        diff = jnp.where(valid, n * fnw - tgt_ref[...], 0.0)
        dy = diff * (1.0 / D)
        stats_ref[0:1, :] += (0.5 / D) * jnp.sum(diff * diff, axis=0, keepdims=True)
        stats_ref[1:2, :] += jnp.sum(dy * n, axis=0, keepdims=True)
        dn = dy * fnw
        dh2_ref[...] = rs * (dn - n * jnp.mean(dn * n, axis=-1, keepdims=True))

    return pl.pallas_call(
        body,
        name="final_loss",
        grid=(rows // TM,),
        in_specs=[
            pl.BlockSpec((TM, D), lambda i: (i, 0)),
            pl.BlockSpec((TM, D), lambda i: (jnp.maximum(i - 1, 0), 0)),
            pl.BlockSpec((1, D), lambda i: (0, 0)),
        ],
        out_specs=[pl.BlockSpec((TM, D), lambda i: (i, 0)), pl.BlockSpec((8, D), lambda i: (0, 0))],
        out_shape=[jax.ShapeDtypeStruct((rows, D), F32), jax.ShapeDtypeStruct((8, D), F32)],
        compiler_params=_params(("arbitrary",)),
    )(h2, target, final_norm_w)


def _ffn_bwd(dh2, g, up, h1, rstd2, w_ffn_in, w_ffn_out, ffn_norm_w):
    rows = dh2.shape[0]
    tm = _heavy_tile(rows)
    nb = FFN // FFN_BLOCK

    def body(dh2_ref, g_ref, up_ref, h1_ref, rstd_ref, wg_ref, wup_ref, wo_ref, nw_ref,
             dg_ref, dup_ref, act_ref, dh1_ref, stats_ref, du_sc, dh2b_sc):
        i = pl.program_id(0)
        j = pl.program_id(1)

        @pl.when(jnp.logical_and(i == 0, j == 0))
        def _():
            stats_ref[...] = jnp.zeros_like(stats_ref)

        @pl.when(j == 0)
        def _():
            dh2b_sc[...] = dh2_ref[...].astype(BF16)

        dact = _dot_nt(dh2b_sc[...], wo_ref[...])
        g = g_ref[...].astype(F32)
        up = up_ref[...].astype(F32)
        sg = _sigmoid(g)
        silu = g * sg
        act_ref[...] = (silu * up).astype(BF16)
        dup = (dact * silu).astype(BF16)
        dg = (dact * up * (sg * (1.0 + g * (1.0 - sg)))).astype(BF16)
        dg_ref[...] = dg
        dup_ref[...] = dup
        part = _dot_nt(dg, wg_ref[...]) + _dot_nt(dup, wup_ref[...])

        @pl.when(j == 0)
        def _():
            du_sc[...] = part

        @pl.when(j > 0)
        def _():
            du_sc[...] += part

        @pl.when(j == nb - 1)
        def _():
            du = du_sc[...]
            rs = rstd_ref[...]
            n = h1_ref[...] * rs
            stats_ref[0:1, :] += jnp.sum(du * n, axis=0, keepdims=True)
            dn = du * nw_ref[...]
            dh1_ref[...] = dh2_ref[...] + rs * (dn - n * jnp.mean(dn * n, axis=-1, keepdims=True))

    blk = lambda: pl.BlockSpec((tm, FFN_BLOCK), lambda i, j: (i, j))
    return pl.pallas_call(
        body,
        name="ffn_bwd",
        grid=(rows // tm, nb),
        in_specs=[
            pl.BlockSpec((tm, D), lambda i, j: (i, 0)),
            blk(), blk(),
            pl.BlockSpec((tm, D), lambda i, j: (i, 0)),
            pl.BlockSpec((tm, 1), lambda i, j: (i, 0)),
            pl.BlockSpec((D, FFN_BLOCK), lambda i, j: (0, j)),
            pl.BlockSpec((D, FFN_BLOCK), lambda i, j: (0, nb + j)),
            pl.BlockSpec((FFN_BLOCK, D), lambda i, j: (j, 0)),
            pl.BlockSpec((1, D), lambda i, j: (0, 0)),
        ],
        out_specs=[
            blk(), blk(), blk(),
            pl.BlockSpec((tm, D), lambda i, j: (i, 0)),
            pl.BlockSpec((8, D), lambda i, j: (0, 0)),
        ],
        out_shape=[
            jax.ShapeDtypeStruct((rows, FFN), BF16), jax.ShapeDtypeStruct((rows, FFN), BF16),
            jax.ShapeDtypeStruct((rows, FFN), BF16), jax.ShapeDtypeStruct((rows, D), F32),
            jax.ShapeDtypeStruct((8, D), F32),
        ],
        scratch_shapes=[pltpu.VMEM((tm, D), F32), pltpu.VMEM((tm, D), BF16)],
        compiler_params=_params(("arbitrary", "arbitrary")),
    )(dh2, g, up, h1, rstd2, w_ffn_in, w_ffn_in, w_ffn_out, ffn_norm_w)


def _mix_bwd(dh1, o, proj_f, h_lru, y_ret, y_lru, w_br, w_bl, w_o):
    rows = dh1.shape[0]
    tm = TM_MIX_BWD
    nt = rows // tm

    def body(dh1_ref, o_ref, gret_ref, lgate_ref, ga_ref, gb_ref, hl_ref, yret_ref, ylru_ref, wbr_ref, wbl_ref, wo_ref,
             dproj_ref, do_ref, dhl_ref, mixed_ref, aret_ref, alru_ref, dyret_ref, dylru_ref):
        dmixed = _dot_nt(dh1_ref[...].astype(BF16), wo_ref[...])
        y_ret, y_lru = yret_ref[...], ylru_ref[...]
        sa, sb = _sigmoid(ga_ref[...]), _sigmoid(gb_ref[...])
        mixed_ref[...] = (sa * y_ret + sb * y_lru).astype(BF16)
        dga = dmixed * y_ret * sa * (1.0 - sa)
        dgb = dmixed * y_lru * sb * (1.0 - sb)
        dy_ret = (dmixed * sa).astype(BF16)
        dy_lru = (dmixed * sb).astype(BF16)
        dyret_ref[...] = dy_ret
        dylru_ref[...] = dy_lru
        da_ret = _dot_nt(dy_ret, wbr_ref[...])
        da_lru = _dot_nt(dy_lru, wbl_ref[...])

        gret = gret_ref[...]
        sg = _sigmoid(gret)
        silu = gret * sg
        on, rstds = _group_norm(o_ref[...])
        aret_ref[...] = (silu * on).astype(BF16)
        dgret = da_ret * on * (sg * (1.0 + gret * (1.0 - sg)))
        don = da_ret * silu
        for h in range(HEADS):
            sl = slice(h * DH, (h + 1) * DH)
            onh, donh = on[:, sl], don[:, sl]
            do_ref[:, sl] = rstds[h] * (donh - onh * jnp.mean(donh * onh, axis=-1, keepdims=True))

        gl, gl_grad = _gelu_and_grad(lgate_ref[...])
        hl = hl_ref[...]
        alru_ref[...] = (gl * hl).astype(BF16)
        dlgate = da_lru * hl * gl_grad
        dhl_ref[...] = da_lru * gl

        zeros = jnp.zeros((tm, D), BF16)
        for col in (0, 1, 2, 4):
            dproj_ref[:, col * D:(col + 1) * D] = zeros
        dproj_ref[:, 3 * D:4 * D] = dgret.astype(BF16)
        dproj_ref[:, 5 * D:6 * D] = dlgate.astype(BF16)
        dproj_ref[:, 6 * D:7 * D] = dga.astype(BF16)
        dproj_ref[:, 7 * D:8 * D] = dgb.astype(BF16)

    tile = lambda col: pl.BlockSpec((tm, D), lambda i: (i, col))
    wspec = pl.BlockSpec((D, D), lambda i: (0, 0))
    bf = lambda: jax.ShapeDtypeStruct((rows, D), BF16)
    return pl.pallas_call(
        body,
        name="mix_bwd",
        grid=(nt,),
        in_specs=[tile(0), tile(0), tile(0), tile(2), tile(3), tile(4), tile(0), tile(0), tile(0), wspec, wspec, wspec],
        out_specs=[pl.BlockSpec((tm, IN_COLS), lambda i: (i, 0))] + [tile(0)] * 7,
        out_shape=[
            jax.ShapeDtypeStruct((rows, IN_COLS), BF16),
            jax.ShapeDtypeStruct((rows, D), F32), jax.ShapeDtypeStruct((rows, D), F32),
            bf(), bf(), bf(), bf(), bf(),
        ],
        compiler_params=_params(("arbitrary",)),
    )(dh1, o, proj_f, proj_f, proj_f, proj_f, h_lru, y_ret, y_lru, w_br, w_bl, w_o)


def _retention_bwd(dproj, proj_bf, do, sprev, intra, intra_t, q_dec, k_dec, c_dec, cos_t, sin_t):
    rows = proj_bf.shape[0]
    nc = rows // CHUNK

    def body(dproj_in_ref, q_ref, k_ref, v_ref, do_ref, sprev_ref, m_ref, mt_ref, qd_ref, kd_ref, cd_ref, cos_ref, sin_ref,
             dproj_ref, ds_sc):
        @pl.when(pl.program_id(0) == 0)
        def _():
            ds_sc[...] = jnp.zeros_like(ds_sc)

        cos, sin = cos_ref[...], sin_ref[...]

        def unrotate(dy):
            return dy * cos - pltpu.roll(dy, DH // 2, axis=1) * sin

        for h in range(HEADS):
            sl = slice(h * DH, (h + 1) * DH)
            q, k, v = q_ref[:, sl], k_ref[:, sl], v_ref[:, sl]
            do = do_ref[:, sl]
            dob = do.astype(BF16)
            doq = (do * qd_ref[:, sl]).astype(BF16)
            state_prev = sprev_ref[0, h]
            dstate = ds_sc[h]
            dstate_b = dstate.astype(BF16)
            s_t = (_dot_nt(k, q) * mt_ref[h]).astype(BF16)
            ds_t = (_dot_nt(v, dob) * mt_ref[h]).astype(BF16)
            ds = (_dot_nt(dob, v) * m_ref[h]).astype(BF16)
            kd = kd_ref[:, sl]
            dq = _dot(ds, k) + _dot_nt(doq, state_prev)
            dk = _dot(ds_t, q) + _dot_nt(v, dstate_b) * kd
            k_scaled = (k.astype(F32) * kd).astype(BF16)
            dv = _dot(s_t, dob) + _dot(k_scaled, dstate_b)
            ds_sc[h] = dstate * cd_ref[:, sl] + _dot_tn(q, doq)
            dproj_ref[:, sl] = unrotate(dq).astype(BF16)
            dproj_ref[:, D + h * DH:D + (h + 1) * DH] = (unrotate(dk) * QK_SCALE).astype(BF16)
            dproj_ref[:, 2 * D + h * DH:2 * D + (h + 1) * DH] = dv.astype(BF16)

    rev = lambda c: nc - 1 - c
    chunk_spec = lambda col: pl.BlockSpec((CHUNK, D), lambda c: (rev(c), col))
    const2 = lambda shape: pl.BlockSpec(shape, lambda c: (0, 0))
    const3 = pl.BlockSpec((HEADS, CHUNK, CHUNK), lambda c: (0, 0, 0))
    return pl.pallas_call(
        body,
        name="retention_bwd",
        grid=(nc,),
        in_specs=[
            pl.BlockSpec(memory_space=pl.ANY),
            chunk_spec(0), chunk_spec(1), chunk_spec(2), chunk_spec(0),
            pl.BlockSpec((1, HEADS, DH, DH), lambda c: (rev(c), 0, 0, 0)),
            const3, const3,
            const2((CHUNK, D)), const2((CHUNK, D)), const2((1, D)),
            pl.BlockSpec((CHUNK, DH), lambda c: (rev(c), 0)),
            pl.BlockSpec((CHUNK, DH), lambda c: (rev(c), 0)),
        ],
        out_specs=pl.BlockSpec((CHUNK, 3 * D), lambda c: (rev(c), 0)),
        out_shape=jax.ShapeDtypeStruct(dproj.shape, BF16),
        input_output_aliases={0: 0},
        scratch_shapes=[pltpu.VMEM((HEADS, DH, DH), F32)],
        compiler_params=_params(("arbitrary",)),
    )(dproj, proj_bf, proj_bf, proj_bf, do, sprev, intra, intra_t, q_dec, k_dec, c_dec, cos_t, sin_t)


def _lru_bwd(dproj, proj_f, h_lru, dhl, conv_w, conv_b, wa, wx, ba, bx, lam):
    rows = proj_f.shape[0]
    nt = rows // TM
    per8 = TM // 8

    def body(dproj_in_ref, x_ref, xprev_ref, h_ref, hprev_ref, dhl_ref, cw_ref, cb_ref, wa_ref, wx_ref, ba_ref, bx_ref, lam_ref,
             dproj_ref, dwa_ref, dwx_ref, stats_ref, anext_sc, dhnext_sc, dcnext_sc, c_sc, b_sc, dh_sc):
        step = pl.program_id(0)
        i = nt - 1 - step

        @pl.when(step == 0)
        def _():
            anext_sc[...] = jnp.zeros_like(anext_sc)
            dhnext_sc[...] = jnp.zeros_like(dhnext_sc)
            dcnext_sc[...] = jnp.zeros_like(dcnext_sc)
            dwa_ref[...] = jnp.zeros_like(dwa_ref)
            dwx_ref[...] = jnp.zeros_like(dwx_ref)
            stats_ref[...] = jnp.zeros_like(stats_ref)

        first = i == 0
        x = x_ref[...]
        prev8 = jnp.where(first, 0.0, xprev_ref[...])
        c, r, gate_i, a, mult, sp, shifted = _conv_and_gates(x, prev8, cw_ref, cb_ref, wa_ref, wx_ref, ba_ref, bx_ref, lam_ref, c_sc)

        b_sc[...] = pltpu.roll(a, TM - 1, axis=0)
        b_sc[TM - 1:TM, :] = anext_sc[0:1, :]
        anext_sc[0:1, :] = a[0:1, :]
        row8 = lax.broadcasted_iota(jnp.int32, (8, D), 0)

        def group(gi, dhnext):
            r0 = pl.multiple_of((per8 - 1 - gi) * 8, 8)
            bb = b_sc[pl.ds(r0, 8), :]
            uu = dhl_ref[pl.ds(r0, 8), :]
            for d in (1, 2, 4):
                b_sh = jnp.where(row8 < 8 - d, pltpu.roll(bb, 8 - d, axis=0), 1.0)
                u_sh = jnp.where(row8 < 8 - d, pltpu.roll(uu, 8 - d, axis=0), 0.0)
                uu = uu + bb * u_sh
                bb = bb * b_sh
            dhb = bb * dhnext + uu
            dh_sc[pl.ds(r0, 8), :] = dhb
            return dhb[0:1, :]

        dhfirst = lax.fori_loop(0, per8, group, dhnext_sc[0:1, :])
        dhnext_sc[0:1, :] = dhfirst
        dh = dh_sc[...]

        h = h_ref[...]
        hprev8 = jnp.where(first, 0.0, hprev_ref[...])
        h_dn, h_head = _shift_down(h, hprev8, 1)
        c_sc[...] = h_dn
        c_sc[0:8, :] = h_head
        h_before = c_sc[...]

        valid = _row_ids(i, TM, (TM, D)) >= PAD_ROWS
        da = dh * h_before
        du = jnp.where(valid, dh, 0.0)
        dmult = du * gate_i * c
        dgate_i = du * mult * c
        dc = du * mult * gate_i
        dla = da * a - dmult * (a * a) / mult
        dla = jnp.where(valid, dla, 0.0)
        dr = dla * ((-LRU_C) * sp)
        dzr = dr * r * (1.0 - r)
        dzi = dgate_i * gate_i * (1.0 - gate_i)
        stats_ref[1:2, :] += jnp.sum(dzr, axis=0, keepdims=True)
        stats_ref[2:3, :] += jnp.sum(dzi, axis=0, keepdims=True)
        stats_ref[3:4, :] += jnp.sum(dla * ((-LRU_C) * r), axis=0, keepdims=True)
        dc_gate = []
        for g in range(LRU_BLOCKS):
            sl = slice(g * LRU_BLOCK, (g + 1) * LRU_BLOCK)
            cg = c[:, sl].astype(BF16)
            dzr_g = dzr[:, sl].astype(BF16)
            dzi_g = dzi[:, sl].astype(BF16)
            dc_gate.append(_dot_nt(dzr_g, wa_ref[g]) + _dot_nt(dzi_g, wx_ref[g]))
            dwa_ref[g] += _dot_tn(cg, dzr_g)
            dwx_ref[g] += _dot_tn(cg, dzi_g)
        dc = dc + jnp.concatenate(dc_gate, axis=1)

        cw = cw_ref[...]
        stats_ref[0:1, :] += jnp.sum(dc, axis=0, keepdims=True)
        stats_ref[7:8, :] += jnp.sum(dc * x, axis=0, keepdims=True)
        dx = cw[3:4] * dc
        tail_src = jnp.concatenate([dc[TM - 8:TM], dcnext_sc[...]], axis=0)
        dx_tail = cw[3:4] * dc[TM - 8:TM]
        for d in (1, 2, 3):
            dx = dx + cw[3 - d:4 - d] * pltpu.roll(dc, TM - d, axis=0)
            dx_tail = dx_tail + cw[3 - d:4 - d] * pltpu.roll(tail_src, 16 - d, axis=0)[0:8]
            rolled, hd = shifted[d - 1]
            b_sc[...] = rolled
            b_sc[0:8, :] = hd
            stats_ref[7 - d:8 - d, :] += jnp.sum(dc * b_sc[...], axis=0, keepdims=True)
        dcnext_sc[...] = dc[0:8]
        dproj_ref[...] = dx.astype(BF16)
        dproj_ref[TM - 8:TM, :] = dx_tail.astype(BF16)

    rev = lambda s: nt - 1 - s
    vec = pl.BlockSpec((1, D), lambda s: (0, 0))
    wspec = pl.BlockSpec((LRU_BLOCKS, LRU_BLOCK, LRU_BLOCK), lambda s: (0, 0, 0))
    prev8 = lambda col: pl.BlockSpec((8, D), lambda s: (jnp.maximum(rev(s) * per8 - 1, 0), col))
    return pl.pallas_call(
        body,
        name="lru_bwd",
        grid=(nt,),
        in_specs=[
            pl.BlockSpec(memory_space=pl.ANY),
            pl.BlockSpec((TM, D), lambda s: (rev(s), 1)), prev8(1),
            pl.BlockSpec((TM, D), lambda s: (rev(s), 0)), prev8(0),
            pl.BlockSpec((TM, D), lambda s: (rev(s), 0)),
            pl.BlockSpec((4, D), lambda s: (0, 0)),
            vec, wspec, wspec, vec, vec, vec,
        ],
        out_specs=[
            pl.BlockSpec((TM, D), lambda s: (rev(s), 4)),
            wspec, wspec,
            pl.BlockSpec((8, D), lambda s: (0, 0)),
        ],
        out_shape=[
            jax.ShapeDtypeStruct(dproj.shape, BF16),
            jax.ShapeDtypeStruct((LRU_BLOCKS, LRU_BLOCK, LRU_BLOCK), F32),
            jax.ShapeDtypeStruct((LRU_BLOCKS, LRU_BLOCK, LRU_BLOCK), F32),
            jax.ShapeDtypeStruct((8, D), F32),
        ],
        input_output_aliases={0: 0},
        scratch_shapes=[
            pltpu.VMEM((8, D), F32), pltpu.VMEM((8, D), F32), pltpu.VMEM((8, D), F32),
            pltpu.VMEM((TM, D), F32), pltpu.VMEM((TM, D), F32), pltpu.VMEM((TM, D), F32),
        ],
        compiler_params=_params(("arbitrary",)),
    )(dproj, proj_f, proj_f, h_lru, h_lru, dhl, conv_w, conv_b, wa, wx, ba, bx, lam)


def _in_proj_bwd(dproj, w_in):
    rows = dproj.shape[0]
    tm = _heavy_tile(rows)

    def body(dproj_ref, w_ref, du_ref):
        part = _dot_nt(dproj_ref[...], w_ref[...])

        @pl.when(pl.program_id(1) == 0)
        def _():
            du_ref[...] = part

        @pl.when(pl.program_id(1) > 0)
        def _():
            du_ref[...] += part

    return pl.pallas_call(
        body,
        name="in_proj_bwd",
        grid=(rows // tm, 8),
        in_specs=[pl.BlockSpec((tm, D), lambda i, g: (i, g)), pl.BlockSpec((D, D), lambda i, g: (0, g))],
        out_specs=pl.BlockSpec((tm, D), lambda i, g: (i, 0)),
        out_shape=jax.ShapeDtypeStruct((rows, D), F32),
        compiler_params=_params(("arbitrary", "arbitrary")),
    )(dproj, w_in)


def _norm1_bwd(du, dh1, head, x2d, rstd1, norm_w):
    rows = du.shape[0]

    def body(du_ref, dh1_ref, head_ref, x_ref, rstd_ref, nw_ref, gx_ref, ghead_ref, stats_ref):
        i = pl.program_id(0)

        @pl.when(i == 0)
        def _():
            stats_ref[...] = jnp.zeros_like(stats_ref)

        def finish(h0, out_ref):
            du = du_ref[...]
            rs = rstd_ref[...]
            n = h0 * rs
            stats_ref[0:1, :] += jnp.sum(du * n, axis=0, keepdims=True)
            dn = du * nw_ref[...]
            out_ref[...] = dh1_ref[...] + rs * (dn - n * jnp.mean(dn * n, axis=-1, keepdims=True))

        @pl.when(i == 0)
        def _():
            finish(head_ref[...], ghead_ref)

        @pl.when(i > 0)
        def _():
            finish(x_ref[...], gx_ref)

    tile = pl.BlockSpec((TM, D), lambda i: (i, 0))
    return pl.pallas_call(
        body,
        name="norm1_bwd",
        grid=(rows // TM,),
        in_specs=[
            tile, tile,
            pl.BlockSpec((FRONT, D), lambda i: (0, 0)),
            pl.BlockSpec((TM, D), lambda i: (jnp.maximum(i - 1, 0), 0)),
            pl.BlockSpec((TM, 1), lambda i: (i, 0)),
            pl.BlockSpec((1, D), lambda i: (0, 0)),
        ],
        out_specs=[
            pl.BlockSpec((TM, D), lambda i: (jnp.maximum(i - 1, 0), 0)),
            pl.BlockSpec((FRONT, D), lambda i: (0, 0)),
            pl.BlockSpec((8, D), lambda i: (0, 0)),
        ],
        out_shape=[
            jax.ShapeDtypeStruct(x2d.shape, F32),
            jax.ShapeDtypeStruct((FRONT, D), F32),
            jax.ShapeDtypeStruct((8, D), F32),
        ],
        compiler_params=_params(("arbitrary",)),
    )(du, dh1, head, x2d, rstd1, norm_w)


def _matmul_tn(name, x, dy, out_cols, col0=0, prev=None, k_block=None, n_block=None):
    rows, kdim = x.shape
    ndim = dy.shape[1]
    tr = next(t for t in (1408, TM_HEAVY, TM) if rows % t == 0)
    kb = k_block or kdim
    nb = n_block or ndim
    nr, nk, nn = rows // tr, kdim // kb, ndim // nb
    cb0 = col0 // nb

    def body(*refs):
        x_ref, dy_ref, out_ref = refs[-3], refs[-2], refs[-1]
        part = _dot_tn(x_ref[...].astype(BF16), dy_ref[...].astype(BF16))

        @pl.when(pl.program_id(2) == 0)
        def _():
            out_ref[...] = part

        @pl.when(pl.program_id(2) > 0)
        def _():
            out_ref[...] += part

    in_specs = [
        pl.BlockSpec((tr, kb), lambda n, k, r: (r, k)),
        pl.BlockSpec((tr, nb), lambda n, k, r: (r, n)),
    ]
    args = [x, dy]
    aliases = {}
    if prev is not None:
        in_specs = [pl.BlockSpec(memory_space=pl.ANY)] + in_specs
        args = [prev] + args
        aliases = {0: 0}
    return pl.pallas_call(
        body,
        name=name,
        grid=(nn, nk, nr),
        in_specs=in_specs,
        out_specs=pl.BlockSpec((kb, nb), lambda n, k, r: (k, cb0 + n)),
        out_shape=jax.ShapeDtypeStruct((kdim, out_cols), F32),
        input_output_aliases=aliases,
        compiler_params=_params(("arbitrary", "arbitrary", "arbitrary")),
    )(*args)


def _local_step(x2d, target, meta_full, w):
    rows = FRONT + x2d.shape[0]
    head = jnp.concatenate([jnp.zeros((PAD_ROWS, D), F32), meta_full], axis=0)
    cos_t, sin_t = _rope_tables(rows)
    intra, intra_t, q_dec, k_dec, c_dec = _decay_tables()

    u1, rstd1 = _norm1(head, x2d, w["mix_norm_w"])
    proj_bf, proj_f = _in_proj(u1, w["w_in"], cos_t, sin_t)
    o, sprev = _retention_fwd(proj_bf, intra, q_dec, k_dec, c_dec)
    lru_args = (w["conv_w"], w["conv_b"], w["lru_wa"], w["lru_wx"], w["lru_ba"], w["lru_bx"], w["lru_lambda"])
    h_lru = _lru_fwd(proj_f, *lru_args)
    y_ret, y_lru, h1, u2, rstd2 = _mix_fwd(head, x2d, o, proj_f, h_lru, w["w_branch_ret"], w["w_branch_lru"], w["w_out"],
                                           w["ffn_norm_w"])
    g, up, h2 = _ffn_fwd(u2, h1, w["w_ffn_in"], w["w_ffn_out"])
    dh2, stats_loss = _final_loss(h2, target, w["final_norm_w"])

    dg, dup, act, dh1, stats_ffn = _ffn_bwd(dh2, g, up, h1, rstd2, w["w_ffn_in"], w["w_ffn_out"], w["ffn_norm_w"])
    (dproj, do, dhl, mixed, a_ret, a_lru, dy_ret, dy_lru) = _mix_bwd(dh1, o, proj_f, h_lru, y_ret, y_lru, w["w_branch_ret"],
                                                                    w["w_branch_lru"], w["w_out"])
    dproj = _retention_bwd(dproj, proj_bf, do, sprev, intra, intra_t, q_dec, k_dec, c_dec, cos_t, sin_t)
    dproj, dwa, dwx, stats_lru = _lru_bwd(dproj, proj_f, h_lru, dhl, *lru_args)
    du1 = _in_proj_bwd(dproj, w["w_in"])
    grad_x, grad_head, stats_in = _norm1_bwd(du1, dh1, head, x2d, rstd1, w["mix_norm_w"])

    grads = {
        "w_in": _matmul_tn("dw_in", u1, dproj, IN_COLS, n_block=D),
        "w_ffn_in": _matmul_tn("dw_ffn_up", u2, dup, 2 * FFN, col0=FFN, n_block=FFN_HALF,
                               prev=_matmul_tn("dw_ffn_gate", u2, dg, 2 * FFN, n_block=FFN_HALF)),
        "w_ffn_out": _matmul_tn("dw_ffn_out", act, dh2, D, k_block=FFN_HALF),
        "w_out": _matmul_tn("dw_out", mixed, dh1, D),
        "w_branch_ret": _matmul_tn("dw_branch_ret", a_ret, dy_ret, D),
        "w_branch_lru": _matmul_tn("dw_branch_lru", a_lru, dy_lru, D),
        "lru_wa": dwa,
        "lru_wx": dwx,
    }
    return grad_x, grad_head, grads, [stats_loss, stats_ffn, stats_in, stats_lru]


BIG_PIECES = {
    "w_in": ("col", (D, 2 * D)),
    "w_ffn_in": ("col", (D, FFN_HALF)),
    "w_ffn_out": ("row", (FFN // 4, D)),
    "w_branch_ret": ("row", (D // 4, D)),
    "w_branch_lru": ("row", (D // 4, D)),
    "w_out": ("row", (D // 4, D)),
    "lru_wa": ("lru", (LRU_BLOCKS, LRU_BLOCK // 4, LRU_BLOCK)),
    "lru_wx": ("lru", (LRU_BLOCKS, LRU_BLOCK // 4, LRU_BLOCK)),
}
SMALL_PIECES = {"meta_tokens": ("col", (N_META, D // 4)), "conv_w": ("col", (4, D // 4))}


def _full_shape(kind, shard):
    if kind == "col":
        return (shard[0], 4 * shard[1])
    if kind == "row":
        return (4 * shard[0], shard[1])
    return (shard[0], 4 * shard[1], shard[2])


def _half_shape(kind, shard):
    return (shard[0] // 2,) + tuple(shard[1:])


def _aligned(start, multiple):
    return start if isinstance(start, int) else pl.multiple_of(start, multiple)


def _lead(h, size):
    if h is None:
        return pl.ds(0, size)
    return pl.ds(_aligned(h * (size // 2), size // 2), size // 2)


def _full_region(ref, kind, shard, s, h):
    if kind == "col":
        return ref.at[_lead(h, shard[0]), pl.ds(_aligned(s * shard[1], shard[1]), shard[1])]
    if kind == "row":
        size = shard[0] if h is None else shard[0] // 2
        start = s * shard[0] + (0 if h is None else h * (shard[0] // 2))
        return ref.at[pl.ds(_aligned(start, 16), size), :]
    return ref.at[_lead(h, shard[0]), pl.ds(_aligned(s * shard[1], shard[1]), shard[1]), :]


def _shard_region(ref, shard, h):
    return ref.at[_lead(h, shard[0])]


def _place():
    x, y, c = lax.axis_index("x"), lax.axis_index("y"), lax.axis_index("c")
    return x, y, c, 2 * x + y


def _other_chip(s, c, k):
    s2 = jnp.bitwise_xor(s, k)
    return s2, (s2 // 2, s2 % 2, c)


def _remote(src, dst, send_sem, recv_sem, dev):
    return pltpu.make_async_remote_copy(src_ref=src, dst_ref=dst, send_sem=send_sem, recv_sem=recv_sem,
                                        device_id=dev, device_id_type=MESH)


_ANY = pl.BlockSpec(memory_space=pl.ANY)


def _all_gather(shards):
    pieces = [(n, *BIG_PIECES[n], True) for n in BIG_PIECES] + [(n, *SMALL_PIECES[n], False) for n in SMALL_PIECES]
    n = len(pieces)

    def body(*refs):
        ins, outs = refs[:n], refs[n:2 * n]
        local_sem, ici_send, ici_recv, fwd_send, fwd_recv = refs[2 * n:]
        x, y, c, s = _place()
        sibling = (x, y, 1 - c)
        started, local = [], []
        for p, (_, kind, shard, split) in enumerate(pieces):
            cp = pltpu.make_async_copy(ins[p], _full_region(outs[p], kind, shard, s, None), local_sem.at[p])
            cp.start()
            local.append(cp)
            h = c if split else None
            for k in (1, 2, 3):
                _, dev = _other_chip(s, c, k)
                cp = _remote(_shard_region(ins[p], shard, h), _full_region(outs[p], kind, shard, s, h),
                             ici_send.at[p, k - 1], ici_recv.at[p, k - 1], dev)
                cp.start()
                started.append(cp)
        for p, (_, kind, shard, split) in enumerate(pieces):
            h = c if split else None
            for k in (1, 2, 3):
                s2, dev = _other_chip(s, c, k)
                region = _full_region(outs[p], kind, shard, s2, h)
                _remote(region, region, ici_send.at[p, k - 1], ici_recv.at[p, k - 1], dev).wait_recv()
                if split:
                    cp = _remote(region, region, fwd_send.at[p, k - 1], fwd_recv.at[p, k - 1], sibling)
                    cp.start()
                    started.append(cp)
        for p, (_, kind, shard, split) in enumerate(pieces):
            if split:
                for k in (1, 2, 3):
                    s2, _ = _other_chip(s, c, k)
                    region = _full_region(outs[p], kind, shard, s2, 1 - c)
                    _remote(region, region, fwd_send.at[p, k - 1], fwd_recv.at[p, k - 1], sibling).wait_recv()
        for cp in started:
            cp.wait_send()
        for cp in local:
            cp.wait()

    outs = pl.pallas_call(
        body,
        name="gather_weights",
        in_specs=[_ANY] * n,
        out_specs=[_ANY] * n,
        out_shape=[jax.ShapeDtypeStruct(_full_shape(kind, shard), shards[name].dtype) for name, kind, shard, _ in pieces],
        scratch_shapes=[pltpu.SemaphoreType.DMA((n,))] + [pltpu.SemaphoreType.DMA((n, 3))] * 4,
    )(*[shards[name] for name, _, _, _ in pieces])
    return {name: o for (name, _, _, _), o in zip(pieces, outs)}


def _pair_exchange(grads):
    names = list(BIG_PIECES)
    n = len(names)

    def body(*refs):
        ins, outs = refs[:n], refs[n:2 * n]
        send_sem, recv_sem = refs[2 * n:]
        x, y, c, _ = _place()
        sibling = (x, y, 1 - c)
        started = []
        for p, name in enumerate(names):
            kind, shard = BIG_PIECES[name]
            for s2 in range(4):
                cp = _remote(_full_region(ins[p], kind, shard, s2, 1 - c), outs[p].at[s2], send_sem.at[p, s2],
                             recv_sem.at[p, s2], sibling)
                cp.start()
                started.append(cp)
        for p, name in enumerate(names):
            for s2 in range(4):
                _remote(outs[p].at[s2], outs[p].at[s2], send_sem.at[p, s2], recv_sem.at[p, s2], sibling).wait_recv()
        for cp in started:
            cp.wait_send()

    outs = pl.pallas_call(
        body,
        name="grad_pair_exchange",
        in_specs=[_ANY] * n,
        out_specs=[_ANY] * n,
        out_shape=[jax.ShapeDtypeStruct((4,) + _half_shape(*BIG_PIECES[name]), F32) for name in names],
        scratch_shapes=[pltpu.SemaphoreType.DMA((n, 4))] * 2,
    )(*[grads[name] for name in names])
    return dict(zip(names, outs))


def _row_tile(kind, shard):
    half = shard[0] // 2
    return half if kind != "lru" else None


def _half_specs(kind, shard):
    half = shard[0] // 2
    if kind == "col":
        full = pl.BlockSpec((half, shard[1]), lambda j, pr: (pr[1], j))
        buf = pl.BlockSpec((None, half, shard[1]), lambda j, pr: (j, 0, 0))
    elif kind == "row":
        full = pl.BlockSpec((half, shard[1]), lambda j, pr: (2 * j + pr[1], 0))
        buf = pl.BlockSpec((None, half, shard[1]), lambda j, pr: (j, 0, 0))
    else:
        full = pl.BlockSpec((half, shard[1], shard[2]), lambda j, pr: (pr[1], j, 0))
        buf = pl.BlockSpec((None, half, shard[1], shard[2]), lambda j, pr: (j, 0, 0, 0))
    return full, buf


def _pair_sum(name, grad, recv, place):
    kind, shard = BIG_PIECES[name]
    full, buf = _half_specs(kind, shard)

    def body(pr, g_ref, r_ref, o_ref):
        o_ref[...] = (g_ref[...] + r_ref[...]).astype(BF16)

    return pl.pallas_call(
        body,
        name="pair_sum_" + name,
        grid_spec=pltpu.PrefetchScalarGridSpec(num_scalar_prefetch=1, grid=(4,), in_specs=[full, buf], out_specs=buf),
        out_shape=jax.ShapeDtypeStruct((4,) + _half_shape(kind, shard), BF16),
        compiler_params=_params(("arbitrary",)),
    )(place, grad, recv)


def _chip_exchange(sums):
    names = list(BIG_PIECES)
    n = len(names)

    def body(*refs):
        ins, outs = refs[:n], refs[n:2 * n]
        send_sem, recv_sem = refs[2 * n:]
        _, _, c, s = _place()
        started = []
        for p in range(n):
            for k in (1, 2, 3):
                s2, dev = _other_chip(s, c, k)
                cp = _remote(ins[p].at[s2], outs[p].at[k - 1], send_sem.at[p, k - 1], recv_sem.at[p, k - 1], dev)
                cp.start()
                started.append(cp)
        for p in range(n):
            for k in (1, 2, 3):
                _, dev = _other_chip(s, c, k)
                _remote(outs[p].at[k - 1], outs[p].at[k - 1], send_sem.at[p, k - 1], recv_sem.at[p, k - 1], dev).wait_recv()
        for cp in started:
            cp.wait_send()

    outs = pl.pallas_call(
        body,
        name="grad_chip_exchange",
        in_specs=[_ANY] * n,
        out_specs=[_ANY] * n,
        out_shape=[jax.ShapeDtypeStruct((3,) + _half_shape(*BIG_PIECES[name]), BF16) for name in names],
        scratch_shapes=[pltpu.SemaphoreType.DMA((n, 3))] * 2,
    )(*[sums[name] for name in names])
    return dict(zip(names, outs))


def _chip_sum(name, grad, recv_pair, recv_chip, place):
    kind, shard = BIG_PIECES[name]
    half = shard[0] // 2
    tail = tuple(shard[1:])
    zeros = (0,) * len(tail)
    if kind == "col":
        full = pl.BlockSpec((half,) + tail, lambda j, pr: (pr[1], pr[0]))
    elif kind == "row":
        full = pl.BlockSpec((half,) + tail, lambda j, pr: (2 * pr[0] + pr[1], 0))
    else:
        full = pl.BlockSpec((half,) + tail, lambda j, pr: (pr[1], pr[0], 0))
    pair = pl.BlockSpec((None, half) + tail, lambda j, pr: (pr[0], 0) + zeros)
    chip = pl.BlockSpec((3, half) + tail, lambda j, pr: (0, 0) + zeros)
    out = pl.BlockSpec((half,) + tail, lambda j, pr: (pr[1],) + zeros)

    def body(pr, g_ref, rp_ref, rc_ref, o_ref):
        total = g_ref[...] + rp_ref[...]
        for k in range(3):
            total = total + rc_ref[k].astype(F32)
        o_ref[...] = total

    return pl.pallas_call(
        body,
        name="chip_sum_" + name,
        grid_spec=pltpu.PrefetchScalarGridSpec(num_scalar_prefetch=1, grid=(1,), in_specs=[full, pair, chip], out_specs=out),
        out_shape=jax.ShapeDtypeStruct(shard, F32),
        compiler_params=_params(("arbitrary",)),
    )(place, grad, recv_pair, recv_chip)


def _sibling_exchange(halves):
    names = list(BIG_PIECES)
    n = len(names)

    def body(*refs):
        outs = refs[n:2 * n]
        send_sem, recv_sem = refs[2 * n:]
        x, y, c, _ = _place()
        sibling = (x, y, 1 - c)
        started = []
        for p, name in enumerate(names):
            shard = BIG_PIECES[name][1]
            mine = _shard_region(outs[p], shard, c)
            cp = _remote(mine, mine, send_sem.at[p], recv_sem.at[p], sibling)
            cp.start()
            started.append(cp)
        for p, name in enumerate(names):
            theirs = _shard_region(outs[p], BIG_PIECES[name][1], 1 - c)
            _remote(theirs, theirs, send_sem.at[p], recv_sem.at[p], sibling).wait_recv()
        for cp in started:
            cp.wait_send()

    outs = pl.pallas_call(
        body,
        name="grad_sibling_exchange",
        in_specs=[_ANY] * n,
        out_specs=[_ANY] * n,
        out_shape=[jax.ShapeDtypeStruct(BIG_PIECES[name][1], F32) for name in names],
        input_output_aliases={p: p for p in range(n)},
        scratch_shapes=[pltpu.SemaphoreType.DMA((n,))] * 2,
    )(*[halves[name] for name in names])
    return dict(zip(names, outs))


def _reduce_scatter(grads, place):
    recv_pair = _pair_exchange(grads)
    sums = {name: _pair_sum(name, grads[name], recv_pair[name], place) for name in BIG_PIECES}
    recv_chip = _chip_exchange(sums)
    halves = {name: _chip_sum(name, grads[name], recv_pair[name], recv_chip[name], place) for name in BIG_PIECES}
    return _sibling_exchange(halves)


def _adamw_math(w, g, m, v):
    m = ADAM_B1 * m + (1.0 - ADAM_B1) * g
    v = ADAM_B2 * v + (1.0 - ADAM_B2) * (g * g)
    m_hat = m / (1.0 - ADAM_B1 ** ADAM_STEP)
    v_hat = v / (1.0 - ADAM_B2 ** ADAM_STEP)
    delta = -ADAM_LR * (m_hat / (jnp.sqrt(v_hat) + ADAM_EPS) + ADAM_WD * w)
    return delta, m, v


def _adamw(name, g, w, m, v):
    rows, cols = g.shape
    tr = rows // 4 if rows % 32 == 0 else rows

    def body(g_ref, w_ref, m_ref, v_ref, go_ref, d_ref, mo_ref, vo_ref):
        gv = g_ref[...]
        delta, m2, v2 = _adamw_math(w_ref[...], gv, m_ref[...], v_ref[...])
        go_ref[...] = gv
        d_ref[...] = delta
        mo_ref[...] = m2
        vo_ref[...] = v2

    spec = pl.BlockSpec((tr, cols), lambda i: (i, 0))
    return pl.pallas_call(
        body,
        name="adamw_" + name,
        grid=(rows // tr,),
        in_specs=[spec] * 4,
        out_specs=[spec] * 4,
        out_shape=[jax.ShapeDtypeStruct((rows, cols), F32)] * 4,
        compiler_params=_params(("arbitrary",)),
    )(g, w, m, v)


SMALL_ROWS = 48
VEC_ROWS = {"final_norm_w": 1, "ffn_norm_w": 8, "mix_norm_w": 16, "conv_b": 24, "lru_ba": 25, "lru_bx": 26, "lru_lambda": 27}
VEC_NAMES = list(VEC_ROWS)
CONV_W_ROW = 28
META_ROW = 32


def _small_all_reduce(partial):
    def body(in_ref, out_ref, buf, local_sem, send_sem, recv_sem):
        x, y, c, s = _place()
        me = 2 * s + c
        cp = pltpu.make_async_copy(in_ref, buf.at[me], local_sem)
        cp.start()
        started = []
        for k in range(1, 8):
            peer = jnp.bitwise_xor(me, k)
            dev = (peer // 4, (peer // 2) % 2, peer % 2)
            rd = _remote(in_ref, buf.at[me], send_sem.at[k - 1], recv_sem.at[k - 1], dev)
            rd.start()
            started.append(rd)
        for k in range(1, 8):
            peer = jnp.bitwise_xor(me, k)
            dev = (peer // 4, (peer // 2) % 2, peer % 2)
            _remote(in_ref, buf.at[peer], send_sem.at[k - 1], recv_sem.at[k - 1], dev).wait_recv()
        for rd in started:
            rd.wait_send()
        cp.wait()
        total = buf[0]
        for d in range(1, 8):
            total = total + buf[d]
        out_ref[...] = total

    return pl.pallas_call(
        body,
        name="small_all_reduce",
        in_specs=[pl.BlockSpec(memory_space=pltpu.VMEM)],
        out_specs=pl.BlockSpec(memory_space=pltpu.VMEM),
        out_shape=jax.ShapeDtypeStruct((SMALL_ROWS, D), F32),
        scratch_shapes=[pltpu.VMEM((8, SMALL_ROWS, D), F32), pltpu.SemaphoreType.DMA,
                        pltpu.SemaphoreType.DMA((7,)), pltpu.SemaphoreType.DMA((7,))],
    )(partial)


def _small_update(total, place, vecs, conv, meta):
    nvec = len(VEC_NAMES)
    qcols = D // 4

    def body(pr, tot_ref, col_ref, *refs):
        vec_refs = refs[:3 * nvec]
        conv_refs = refs[3 * nvec:3 * nvec + 3]
        meta_refs = refs[3 * nvec + 3:3 * nvec + 6]
        outs = refs[3 * nvec + 6:]
        loss_ref, vec_out, conv_out, meta_out = outs[0], outs[1:5], outs[5:9], outs[9:13]
        loss_ref[...] = jnp.sum(tot_ref[0:1, :], axis=1, keepdims=True)
        for o in vec_out:
            o[...] = jnp.zeros_like(o)
        for j, name in enumerate(VEC_NAMES):
            w, m, v = (r[...] for r in vec_refs[3 * j:3 * j + 3])
            g = tot_ref[VEC_ROWS[name]:VEC_ROWS[name] + 1, :]
            if name == "lru_lambda":
                g = -g / (1.0 + jnp.exp(w))
            for o, val in zip(vec_out, (g,) + _adamw_math(w, g, m, v)):
                o[j:j + 1, :] = val
        for row, n_rows, ins, group in ((CONV_W_ROW, 4, conv_refs, conv_out), (META_ROW, N_META, meta_refs, meta_out)):
            g = col_ref[row:row + n_rows, :]
            w, m, v = (r[...] for r in ins)
            for o, val in zip(group, (g,) + _adamw_math(w, g, m, v)):
                o[...] = val

    whole = lambda shape: pl.BlockSpec(shape, lambda i, pr: (0,) * len(shape))
    in_specs = [whole((SMALL_ROWS, D)), pl.BlockSpec((SMALL_ROWS, qcols), lambda i, pr: (0, pr[0]))]
    in_specs += [whole((1, D))] * (3 * nvec) + [whole((4, qcols))] * 3 + [whole((N_META, qcols))] * 3
    out_shapes = [(1, 1)] + [(8, D)] * 4 + [(4, qcols)] * 4 + [(N_META, qcols)] * 4
    return pl.pallas_call(
        body,
        name="small_update",
        grid_spec=pltpu.PrefetchScalarGridSpec(num_scalar_prefetch=1, grid=(1,), in_specs=in_specs,
                                               out_specs=[whole(s) for s in out_shapes]),
        out_shape=[jax.ShapeDtypeStruct(s, F32) for s in out_shapes],
        compiler_params=_params(("arbitrary",)),
    )(place, total, total, *[a for t in vecs for a in t], *conv, *meta)


WEIGHT_ORDER = ["meta_tokens", "mix_norm_w", "w_in", "conv_w", "conv_b", "lru_wa", "lru_ba", "lru_wx", "lru_bx", "lru_lambda",
                "w_branch_ret", "w_branch_lru", "w_out", "ffn_norm_w", "w_ffn_in", "w_ffn_out", "final_norm_w"]


def kernel(x, meta_tokens, mix_norm_w, w_in, conv_w, conv_b, lru_wa, lru_ba, lru_wx, lru_bx, lru_lambda, w_branch_ret, w_branch_lru, w_out, ffn_norm_w, w_ffn_in, w_ffn_out, final_norm_w, loss_target, m_meta_tokens, m_mix_norm_w, m_w_in, m_conv_w, m_conv_b, m_lru_wa, m_lru_ba, m_lru_wx, m_lru_bx, m_lru_lambda, m_w_branch_ret, m_w_branch_lru, m_w_out, m_ffn_norm_w, m_w_ffn_in, m_w_ffn_out, m_final_norm_w, v_meta_tokens, v_mix_norm_w, v_w_in, v_conv_w, v_conv_b, v_lru_wa, v_lru_ba, v_lru_wx, v_lru_bx, v_lru_lambda, v_w_branch_ret, v_w_branch_lru, v_w_out, v_ffn_norm_w, v_w_ffn_in, v_w_ffn_out, v_final_norm_w):
    args = locals()
    wts = {n: args[n] for n in WEIGHT_ORDER}
    mom = {n: args["m_" + n] for n in WEIGHT_ORDER}
    var = {n: args["v_" + n] for n in WEIGHT_ORDER}
    place = jnp.stack([2 * lax.axis_index("x") + lax.axis_index("y"), lax.axis_index("c")]).astype(jnp.int32)

    shards = {n: wts[n][0].astype(BF16) for n in BIG_PIECES}
    shards["meta_tokens"] = wts["meta_tokens"]
    shards["conv_w"] = wts["conv_w"][0]
    gathered = _all_gather(shards)
    w = {n: gathered[n] for n in BIG_PIECES}
    w["conv_w"] = gathered["conv_w"]
    for n in VEC_NAMES:
        w[n] = wts[n].reshape(1, D)

    grad_x, grad_head, grads, stats = _local_step(x[0], loss_target[0], gathered["meta_tokens"], w)
    shard_grads = _reduce_scatter(grads, place)

    out = {}
    for n in BIG_PIECES:
        shape2d = (-1, wts[n].shape[-1])
        res = _adamw(n, *[a.reshape(shape2d) for a in (shard_grads[n], wts[n], mom[n], var[n])])
        out[n] = [r.reshape(wts[n].shape) for r in res]

    partial = jnp.concatenate(stats + [grad_head[PAD_ROWS:]], axis=0)
    total = _small_all_reduce(partial)
    vecs = [tuple(a[n].reshape(1, D) for a in (wts, mom, var)) for n in VEC_NAMES]
    conv = tuple(a["conv_w"][0] for a in (wts, mom, var))
    meta = tuple(a["meta_tokens"] for a in (wts, mom, var))
    res = _small_update(total, place, vecs, conv, meta)
    loss = res[0].reshape(())
    for j, n in enumerate(VEC_NAMES):
        out[n] = [r[j].reshape(wts[n].shape) for r in res[1:5]]
    out["conv_w"] = [r.reshape(wts["conv_w"].shape) for r in res[5:9]]
    out["meta_tokens"] = list(res[9:13])

    return (loss, grad_x.reshape(x.shape)) + tuple(out[n][kind] for kind in range(4) for n in WEIGHT_ORDER)
```

```python
import functools
import math

import jax
import jax.numpy as jnp
from jax import lax
from jax.experimental import pallas as pl
from jax.experimental.pallas import tpu as pltpu

F32 = jnp.float32
BF16 = jnp.bfloat16

D = 1024
HEADS = 8
DH = 128
CHUNK = 128
N_META = 16
FRONT = 256
PAD_ROWS = FRONT - N_META
LRU_BLOCKS = 4
LRU_BLOCK = 256
LRU_C = 8.0
FFN = 2816
FFN_HALF = FFN // 2
IN_COLS = 8 * D
ROPE_BASE = 10000.0
EPS = 1e-6
QK_SCALE = DH ** -0.5

ADAM_LR = 0.001
ADAM_B1 = 0.9
ADAM_B2 = 0.999
ADAM_EPS = 1e-08
ADAM_WD = 0.01
ADAM_STEP = 10

TM = 256
TM_HEAVY = 768
FFN_BLOCK = 256
TM_MIX_BWD = 128
VMEM_LIMIT = 56 * 1024 * 1024

NT_DIMS = (((1,), (1,)), ((), ()))
TN_DIMS = (((0,), (0,)), ((), ()))
MESH = pl.DeviceIdType.MESH


def _params(sem=None):
    if sem is None:
        return pltpu.CompilerParams(vmem_limit_bytes=VMEM_LIMIT)
    return pltpu.CompilerParams(dimension_semantics=sem, vmem_limit_bytes=VMEM_LIMIT)


def _dot(a, b):
    return jnp.dot(a, b, preferred_element_type=F32)


def _dot_nt(a, b):
    return lax.dot_general(a, b, NT_DIMS, preferred_element_type=F32)


def _dot_tn(a, b):
    return lax.dot_general(a, b, TN_DIMS, preferred_element_type=F32)


def _sigmoid(z):
    return 1.0 / (1.0 + jnp.exp(-z))


def _log1p(x):
    return jnp.where(x < 1e-3, x * (1.0 - x * (0.5 - x * (1.0 / 3.0))), jnp.log(1.0 + x))


def _softplus(x):
    return jnp.maximum(x, 0.0) + _log1p(jnp.exp(-jnp.abs(x)))


def _neg_expm1(x):
    series = -x * (1.0 + x * (0.5 + x * (1.0 / 6.0 + x * (1.0 / 24.0 + x * (1.0 / 120.0)))))
    return jnp.where(x > -0.05, series, 1.0 - jnp.exp(x))


_GELU_K = math.sqrt(2.0 / math.pi)


def _gelu_and_grad(x):
    inner = _GELU_K * (x + 0.044715 * x * x * x)
    t = jnp.tanh(inner)
    val = 0.5 * x * (1.0 + t)
    grad = 0.5 * (1.0 + t) + 0.5 * x * (1.0 - t * t) * _GELU_K * (1.0 + 3.0 * 0.044715 * x * x)
    return val, grad


def _row_ids(i, rows, shape):
    return i * rows + lax.broadcasted_iota(jnp.int32, shape, 0)


def _rope_tables(rows):
    pos = jnp.arange(rows, dtype=jnp.int32) - PAD_ROWS
    inv_freq = ROPE_BASE ** (-jnp.arange(0, DH, 2, dtype=F32) / DH)
    ang = pos.astype(F32)[:, None] * inv_freq[None, :]
    cos, sin = jnp.cos(ang), jnp.sin(ang)
    return jnp.concatenate([cos, cos], axis=1), jnp.concatenate([-sin, sin], axis=1)


def _decay_tables():
    log_g = jnp.log(1.0 - 2.0 ** (-5.0 - jnp.arange(HEADS, dtype=F32)))
    idx = jnp.arange(CHUNK, dtype=F32)
    diff = idx[:, None] - idx[None, :]
    intra = jnp.where(diff[None] >= 0, jnp.exp(jnp.maximum(diff, 0.0)[None] * log_g[:, None, None]), 0.0)
    q_decay = jnp.exp((idx + 1.0)[:, None] * log_g[None, :])
    k_decay = jnp.exp((CHUNK - 1.0 - idx)[:, None] * log_g[None, :])
    chunk_decay = jnp.exp(CHUNK * log_g)
    wide = lambda a: jnp.repeat(a, DH, axis=-1)
    return intra, jnp.swapaxes(intra, 1, 2), wide(q_decay), wide(k_decay), wide(chunk_decay[None, :])


def _norm1(head, x2d, norm_w, riders=()):
    rows = FRONT + x2d.shape[0]

    def body(head_ref, x_ref, nw_ref, u_ref, rstd_ref):
        def norm(hv):
            rs = lax.rsqrt(jnp.mean(hv * hv, axis=-1, keepdims=True) + EPS)
            u_ref[...] = ((hv * rs) * nw_ref[...]).astype(BF16)
            rstd_ref[...] = rs

        @pl.when(pl.program_id(0) == 0)
        def _():
            norm(head_ref[...])

        @pl.when(pl.program_id(0) > 0)
        def _():
            norm(x_ref[...])

    return _hosted_call(
        body,
        name="norm1",
        grid=(rows // TM,),
        in_specs=[
            pl.BlockSpec((FRONT, D), lambda i: (0, 0)),
            pl.BlockSpec((TM, D), lambda i: (jnp.maximum(i - 1, 0), 0)),
            pl.BlockSpec((1, D), lambda i: (0, 0)),
        ],
        out_specs=[pl.BlockSpec((TM, D), lambda i: (i, 0)), pl.BlockSpec((TM, 1), lambda i: (i, 0))],
        out_shape=[jax.ShapeDtypeStruct((rows, D), BF16), jax.ShapeDtypeStruct((rows, 1), F32)],
        scratch_shapes=[],
        args=(head, x2d, norm_w),
        riders=riders,
    )


def _heavy_tile(rows):
    return TM_HEAVY if rows % TM_HEAVY == 0 else TM


def _in_proj(u, w_in, cos_t, sin_t, riders=()):
    rows = u.shape[0]
    tm = _heavy_tile(rows)

    def body(u_ref, w_ref, cos_ref, sin_ref, pbf_ref, pf_ref):
        g = pl.program_id(1)
        acc = _dot(u_ref[...], w_ref[...])

        @pl.when(g < 2)
        def _():
            scale = jnp.where(g == 1, QK_SCALE, 1.0).astype(F32)
            for h in range(HEADS):
                sl = slice(h * DH, (h + 1) * DH)
                blk = acc[:, sl]
                out = (blk * cos_ref[...] + pltpu.roll(blk, DH // 2, axis=1) * sin_ref[...]) * scale
                pbf_ref[:, sl] = out.astype(BF16)

        @pl.when(g == 2)
        def _():
            pbf_ref[...] = acc.astype(BF16)

        @pl.when(g >= 3)
        def _():
            pf_ref[...] = acc

    return _hosted_call(
        body,
        name="in_proj",
        grid=(rows // tm, 8),
        in_specs=[
            pl.BlockSpec((tm, D), lambda i, g: (i, 0)),
            pl.BlockSpec((D, D), lambda i, g: (0, g)),
            pl.BlockSpec((tm, DH), lambda i, g: (i, 0)),
            pl.BlockSpec((tm, DH), lambda i, g: (i, 0)),
        ],
        out_specs=[
            pl.BlockSpec((tm, D), lambda i, g: (i, jnp.minimum(g, 2))),
            pl.BlockSpec((tm, D), lambda i, g: (i, jnp.maximum(g - 3, 0))),
        ],
        out_shape=[jax.ShapeDtypeStruct((rows, 3 * D), BF16), jax.ShapeDtypeStruct((rows, 5 * D), F32)],
        scratch_shapes=[],
        args=(u, w_in, cos_t, sin_t),
        riders=riders,
    )


def _retention_fwd(proj_bf, intra, q_dec, k_dec, c_dec, riders=()):
    rows = proj_bf.shape[0]
    nc = rows // CHUNK

    def body(q_ref, k_ref, v_ref, m_ref, qd_ref, kd_ref, cd_ref, o_ref, sprev_ref, s_sc):
        @pl.when(pl.program_id(0) == 0)
        def _():
            s_sc[...] = jnp.zeros_like(s_sc)

        for h in range(HEADS):
            sl = slice(h * DH, (h + 1) * DH)
            q, k, v = q_ref[:, sl], k_ref[:, sl], v_ref[:, sl]
            state = s_sc[h]
            state_b = state.astype(BF16)
            sprev_ref[0, h] = state_b
            s = _dot_nt(q, k) * m_ref[h]
            inner = _dot(s.astype(BF16), v)
            cross = _dot(q, state_b) * qd_ref[:, sl]
            o_ref[:, sl] = inner + cross
            k_scaled = (k.astype(F32) * kd_ref[:, sl]).astype(BF16)
            s_sc[h] = state * cd_ref[:, sl] + _dot_tn(k_scaled, v)

    chunk_spec = lambda col: pl.BlockSpec((CHUNK, D), lambda c: (c, col))
    const2 = lambda shape: pl.BlockSpec(shape, lambda c: (0, 0))
    return _hosted_call(
        body,
        name="retention_fwd",
        grid=(nc,),
        in_specs=[
            chunk_spec(0), chunk_spec(1), chunk_spec(2),
            pl.BlockSpec((HEADS, CHUNK, CHUNK), lambda c: (0, 0, 0)),
            const2((CHUNK, D)), const2((CHUNK, D)), const2((1, D)),
        ],
        out_specs=[
            pl.BlockSpec((CHUNK, D), lambda c: (c, 0)),
            pl.BlockSpec((1, HEADS, DH, DH), lambda c: (c, 0, 0, 0)),
        ],
        out_shape=[
            jax.ShapeDtypeStruct((rows, D), F32),
            jax.ShapeDtypeStruct((nc, HEADS, DH, DH), BF16),
        ],
        scratch_shapes=[pltpu.VMEM((HEADS, DH, DH), F32)],
        args=(proj_bf, proj_bf, proj_bf, intra, q_dec, k_dec, c_dec),
        riders=riders,
    )


def _shift_down(x, first8_prev, d):
    rolled = pltpu.roll(x, d, axis=0)
    head = pltpu.roll(jnp.concatenate([first8_prev, x[0:8]], axis=0), d, axis=0)[8:16]
    return rolled, head


def _conv_and_gates(x, prev8, cw_ref, cb_ref, wa_ref, wx_ref, ba_ref, bx_ref, lam_ref, c_sc):
    cw = cw_ref[...]
    conv = cb_ref[...] + cw[3:4] * x
    head = cb_ref[...] + cw[3:4] * x[0:8]
    shifted = []
    for d in (1, 2, 3):
        rolled, hd = _shift_down(x, prev8, d)
        conv = conv + cw[3 - d:4 - d] * rolled
        head = head + cw[3 - d:4 - d] * hd
        shifted.append((rolled, hd))
    c_sc[...] = conv
    c_sc[0:8, :] = head
    c = c_sc[...]
    zr, zi = [], []
    for g in range(LRU_BLOCKS):
        sl = slice(g * LRU_BLOCK, (g + 1) * LRU_BLOCK)
        cg = c[:, sl].astype(BF16)
        zr.append(_dot(cg, wa_ref[g]))
        zi.append(_dot(cg, wx_ref[g]))
    r = _sigmoid(jnp.concatenate(zr, axis=1) + ba_ref[...])
    gate_i = _sigmoid(jnp.concatenate(zi, axis=1) + bx_ref[...])
    sp = _softplus(-lam_ref[...])
    log_a = (-LRU_C) * r * sp
    a = jnp.exp(log_a)
    mult = jnp.sqrt(_neg_expm1(2.0 * log_a))
    return c, r, gate_i, a, mult, sp, shifted


def _lru_fwd(proj_f, conv_w, conv_b, wa, wx, ba, bx, lam):
    rows = proj_f.shape[0]
    nt = rows // TM

    def body(x_ref, cw_ref, cb_ref, wa_ref, wx_ref, ba_ref, bx_ref, lam_ref, h_ref, prev_sc, carry_sc, c_sc, a_sc, u_sc):
        i = pl.program_id(0)

        @pl.when(i == 0)
        def _():
            prev_sc[...] = jnp.zeros_like(prev_sc)
            carry_sc[...] = jnp.zeros_like(carry_sc)

        x = x_ref[...]
        c, r, gate_i, a, mult, _, _ = _conv_and_gates(x, prev_sc[...], cw_ref, cb_ref, wa_ref, wx_ref, ba_ref, bx_ref, lam_ref, c_sc)
        prev_sc[...] = x[TM - 8:TM]
        valid = _row_ids(i, TM, (TM, D)) >= PAD_ROWS
        a_sc[...] = a
        u_sc[...] = jnp.where(valid, mult * gate_i * c, 0.0)
        row8 = lax.broadcasted_iota(jnp.int32, (8, D), 0)

        def group(gi, hprev):
            r0 = pl.multiple_of(gi * 8, 8)
            aa = a_sc[pl.ds(r0, 8), :]
            uu = u_sc[pl.ds(r0, 8), :]
            for d in (1, 2, 4):
                a_sh = jnp.where(row8 >= d, pltpu.roll(aa, d, axis=0), 1.0)
                u_sh = jnp.where(row8 >= d, pltpu.roll(uu, d, axis=0), 0.0)
                uu = uu + aa * u_sh
                aa = aa * a_sh
            hb = aa * hprev + uu
            h_ref[pl.ds(r0, 8), :] = hb
            return hb[7:8, :]

        hlast = lax.fori_loop(0, TM // 8, group, carry_sc[0:1, :])
        carry_sc[0:1, :] = hlast

    vec = pl.BlockSpec((1, D), lambda i: (0, 0))
    wspec = pl.BlockSpec((LRU_BLOCKS, LRU_BLOCK, LRU_BLOCK), lambda i: (0, 0, 0))
    return pl.pallas_call(
        body,
        name="lru_fwd",
        grid=(nt,),
        in_specs=[
            pl.BlockSpec((TM, D), lambda i: (i, 1)),
            pl.BlockSpec((4, D), lambda i: (0, 0)),
            vec, wspec, wspec, vec, vec, vec,
        ],
        out_specs=pl.BlockSpec((TM, D), lambda i: (i, 0)),
        out_shape=jax.ShapeDtypeStruct((rows, D), F32),
        scratch_shapes=[
            pltpu.VMEM((8, D), F32), pltpu.VMEM((8, D), F32),
            pltpu.VMEM((TM, D), F32), pltpu.VMEM((TM, D), F32), pltpu.VMEM((TM, D), F32),
        ],
        compiler_params=_params(("arbitrary",)),
    )(proj_f, conv_w, conv_b, wa, wx, ba, bx, lam)


def _group_norm(o):
    outs, rstds = [], []
    for h in range(HEADS):
        oh = o[:, h * DH:(h + 1) * DH]
        rs = lax.rsqrt(jnp.mean(oh * oh, axis=-1, keepdims=True) + EPS)
        outs.append(oh * rs)
        rstds.append(rs)
    return jnp.concatenate(outs, axis=1), rstds


def _mix_fwd(head, x2d, o, proj_f, h_lru, w_br, w_bl, w_o, ffn_norm_w):
    rows = o.shape[0]
    nt = rows // TM

    def body(head_ref, x_ref, o_ref, gret_ref, lgate_ref, ga_ref, gb_ref, hl_ref, wbr_ref, wbl_ref, wo_ref, nw_ref,
             yret_ref, ylru_ref, h1_ref, u2_ref, rstd_ref):
        i = pl.program_id(0)
        on, _ = _group_norm(o_ref[...])
        gret = gret_ref[...]
        a_ret = (gret * _sigmoid(gret) * on).astype(BF16)
        y_ret = _dot(a_ret, wbr_ref[...])
        gl, _ = _gelu_and_grad(lgate_ref[...])
        a_lru = (gl * hl_ref[...]).astype(BF16)
        y_lru = _dot(a_lru, wbl_ref[...])
        mixed = (_sigmoid(ga_ref[...]) * y_ret + _sigmoid(gb_ref[...]) * y_lru).astype(BF16)
        delta = _dot(mixed, wo_ref[...])
        yret_ref[...] = y_ret
        ylru_ref[...] = y_lru

        def finish(h0):
            h1 = h0 + delta
            rs = lax.rsqrt(jnp.mean(h1 * h1, axis=-1, keepdims=True) + EPS)
            h1_ref[...] = h1
            u2_ref[...] = ((h1 * rs) * nw_ref[...]).astype(BF16)
            rstd_ref[...] = rs

        @pl.when(i == 0)
        def _():
            finish(head_ref[...])

        @pl.when(i > 0)
        def _():
            finish(x_ref[...])

    tile = lambda col: pl.BlockSpec((TM, D), lambda i: (i, col))
    wspec = pl.BlockSpec((D, D), lambda i: (0, 0))
    return pl.pallas_call(
        body,
        name="mix_fwd",
        grid=(nt,),
        in_specs=[
            pl.BlockSpec((FRONT, D), lambda i: (0, 0)),
            pl.BlockSpec((TM, D), lambda i: (jnp.maximum(i - 1, 0), 0)),
            tile(0), tile(0), tile(2), tile(3), tile(4), tile(0),
            wspec, wspec, wspec,
            pl.BlockSpec((1, D), lambda i: (0, 0)),
        ],
        out_specs=[tile(0), tile(0), tile(0), tile(0), pl.BlockSpec((TM, 1), lambda i: (i, 0))],
        out_shape=[
            jax.ShapeDtypeStruct((rows, D), F32), jax.ShapeDtypeStruct((rows, D), F32),
            jax.ShapeDtypeStruct((rows, D), F32), jax.ShapeDtypeStruct((rows, D), BF16),
            jax.ShapeDtypeStruct((rows, 1), F32),
        ],
        compiler_params=_params(("arbitrary",)),
    )(head, x2d, o, proj_f, proj_f, proj_f, proj_f, h_lru, w_br, w_bl, w_o, ffn_norm_w)


def _ffn_fwd(u2, h1, w_ffn_in, w_ffn_out):
    rows = u2.shape[0]
    tm = _heavy_tile(rows)
    nb = FFN // FFN_BLOCK

    def body(u2_ref, h1_ref, wg_ref, wup_ref, wo_ref, g_ref, up_ref, h2_ref):
        j = pl.program_id(1)
        u2 = u2_ref[...]
        g = _dot(u2, wg_ref[...])
        up = _dot(u2, wup_ref[...])
        g_ref[...] = g.astype(BF16)
        up_ref[...] = up.astype(BF16)
        part = _dot((g * _sigmoid(g) * up).astype(BF16), wo_ref[...])

        @pl.when(j == 0)
        def _():
            h2_ref[...] = h1_ref[...] + part

        @pl.when(j > 0)
        def _():
            h2_ref[...] += part

    return pl.pallas_call(
        body,
        name="ffn_fwd",
        grid=(rows // tm, nb),
        in_specs=[
            pl.BlockSpec((tm, D), lambda i, j: (i, 0)),
            pl.BlockSpec((tm, D), lambda i, j: (i, 0)),
            pl.BlockSpec((D, FFN_BLOCK), lambda i, j: (0, j)),
            pl.BlockSpec((D, FFN_BLOCK), lambda i, j: (0, nb + j)),
            pl.BlockSpec((FFN_BLOCK, D), lambda i, j: (j, 0)),
        ],
        out_specs=[
            pl.BlockSpec((tm, FFN_BLOCK), lambda i, j: (i, j)),
            pl.BlockSpec((tm, FFN_BLOCK), lambda i, j: (i, j)),
            pl.BlockSpec((tm, D), lambda i, j: (i, 0)),
        ],
        out_shape=[
            jax.ShapeDtypeStruct((rows, FFN), BF16), jax.ShapeDtypeStruct((rows, FFN), BF16),
            jax.ShapeDtypeStruct((rows, D), F32),
        ],
        compiler_params=_params(("arbitrary", "arbitrary")),
    )(u2, h1, w_ffn_in, w_ffn_in, w_ffn_out)


def _final_loss(h2, target, final_norm_w):
    rows = h2.shape[0]

    def body(h2_ref, tgt_ref, fnw_ref, dh2_ref, stats_ref):
        i = pl.program_id(0)

        @pl.when(i == 0)
        def _():
            stats_ref[...] = jnp.zeros_like(stats_ref)

        h2 = h2_ref[...]
        rs = lax.rsqrt(jnp.mean(h2 * h2, axis=-1, keepdims=True) + EPS)
        n = h2 * rs
        fnw = fnw_ref[...]
        valid = _row_ids(i, TM, (TM, D)) >= FRONT
        diff = jnp.where(valid, n * fnw - tgt_ref[...], 0.0)
        dy = diff * (1.0 / D)
        stats_ref[0:1, :] += (0.5 / D) * jnp.sum(diff * diff, axis=0, keepdims=True)
        stats_ref[1:2, :] += jnp.sum(dy * n, axis=0, keepdims=True)
        dn = dy * fnw
        dh2_ref[...] = rs * (dn - n * jnp.mean(dn * n, axis=-1, keepdims=True))

    return pl.pallas_call(
        body,
        name="final_loss",
        grid=(rows // TM,),
        in_specs=[
            pl.BlockSpec((TM, D), lambda i: (i, 0)),
            pl.BlockSpec((TM, D), lambda i: (jnp.maximum(i - 1, 0), 0)),
            pl.BlockSpec((1, D), lambda i: (0, 0)),
        ],
        out_specs=[pl.BlockSpec((TM, D), lambda i: (i, 0)), pl.BlockSpec((8, D), lambda i: (0, 0))],
        out_shape=[jax.ShapeDtypeStruct((rows, D), F32), jax.ShapeDtypeStruct((8, D), F32)],
        compiler_params=_params(("arbitrary",)),
    )(h2, target, final_norm_w)


def _ffn_bwd(dh2, g, up, h1, rstd2, w_ffn_in, w_ffn_out, ffn_norm_w):
    rows = dh2.shape[0]
    tm = _heavy_tile(rows)
    nb = FFN // FFN_BLOCK

    def body(dh2_ref, g_ref, up_ref, h1_ref, rstd_ref, wg_ref, wup_ref, wo_ref, nw_ref,
             dg_ref, dup_ref, act_ref, dh1_ref, stats_ref, du_sc, dh2b_sc):
        i = pl.program_id(0)
        j = pl.program_id(1)

        @pl.when(jnp.logical_and(i == 0, j == 0))
        def _():
            stats_ref[...] = jnp.zeros_like(stats_ref)

        @pl.when(j == 0)
        def _():
            dh2b_sc[...] = dh2_ref[...].astype(BF16)

        dact = _dot_nt(dh2b_sc[...], wo_ref[...])
        g = g_ref[...].astype(F32)
        up = up_ref[...].astype(F32)
        sg = _sigmoid(g)
        silu = g * sg
        act_ref[...] = (silu * up).astype(BF16)
        dup = (dact * silu).astype(BF16)
        dg = (dact * up * (sg * (1.0 + g * (1.0 - sg)))).astype(BF16)
        dg_ref[...] = dg
        dup_ref[...] = dup
        part = _dot_nt(dg, wg_ref[...]) + _dot_nt(dup, wup_ref[...])

        @pl.when(j == 0)
        def _():
            du_sc[...] = part

        @pl.when(j > 0)
        def _():
            du_sc[...] += part

        @pl.when(j == nb - 1)
        def _():
            du = du_sc[...]
            rs = rstd_ref[...]
            n = h1_ref[...] * rs
            stats_ref[0:1, :] += jnp.sum(du * n, axis=0, keepdims=True)
            dn = du * nw_ref[...]
            dh1_ref[...] = dh2_ref[...] + rs * (dn - n * jnp.mean(dn * n, axis=-1, keepdims=True))

    blk = lambda: pl.BlockSpec((tm, FFN_BLOCK), lambda i, j: (i, j))
    return pl.pallas_call(
        body,
        name="ffn_bwd",
        grid=(rows // tm, nb),
        in_specs=[
            pl.BlockSpec((tm, D), lambda i, j: (i, 0)),
            blk(), blk(),
            pl.BlockSpec((tm, D), lambda i, j: (i, 0)),
            pl.BlockSpec((tm, 1), lambda i, j: (i, 0)),
            pl.BlockSpec((D, FFN_BLOCK), lambda i, j: (0, j)),
            pl.BlockSpec((D, FFN_BLOCK), lambda i, j: (0, nb + j)),
            pl.BlockSpec((FFN_BLOCK, D), lambda i, j: (j, 0)),
            pl.BlockSpec((1, D), lambda i, j: (0, 0)),
        ],
        out_specs=[
            blk(), blk(), blk(),
            pl.BlockSpec((tm, D), lambda i, j: (i, 0)),
            pl.BlockSpec((8, D), lambda i, j: (0, 0)),
        ],
        out_shape=[
            jax.ShapeDtypeStruct((rows, FFN), BF16), jax.ShapeDtypeStruct((rows, FFN), BF16),
            jax.ShapeDtypeStruct((rows, FFN), BF16), jax.ShapeDtypeStruct((rows, D), F32),
            jax.ShapeDtypeStruct((8, D), F32),
        ],
        scratch_shapes=[pltpu.VMEM((tm, D), F32), pltpu.VMEM((tm, D), BF16)],
        compiler_params=_params(("arbitrary", "arbitrary")),
    )(dh2, g, up, h1, rstd2, w_ffn_in, w_ffn_in, w_ffn_out, ffn_norm_w)


def _mix_bwd(dh1, o, proj_f, h_lru, y_ret, y_lru, w_br, w_bl, w_o, riders=()):
    rows = dh1.shape[0]
    tm = TM_MIX_BWD
    nt = rows // tm

    def body(dh1_ref, o_ref, gret_ref, lgate_ref, ga_ref, gb_ref, hl_ref, yret_ref, ylru_ref, wbr_ref, wbl_ref, wo_ref,
             dproj_ref, do_ref, dhl_ref, mixed_ref, aret_ref, alru_ref, dyret_ref, dylru_ref):
        dmixed = _dot_nt(dh1_ref[...].astype(BF16), wo_ref[...])
        y_ret, y_lru = yret_ref[...], ylru_ref[...]
        sa, sb = _sigmoid(ga_ref[...]), _sigmoid(gb_ref[...])
        mixed_ref[...] = (sa * y_ret + sb * y_lru).astype(BF16)
        dga = dmixed * y_ret * sa * (1.0 - sa)
        dgb = dmixed * y_lru * sb * (1.0 - sb)
        dy_ret = (dmixed * sa).astype(BF16)
        dy_lru = (dmixed * sb).astype(BF16)
        dyret_ref[...] = dy_ret
        dylru_ref[...] = dy_lru
        da_ret = _dot_nt(dy_ret, wbr_ref[...])
        da_lru = _dot_nt(dy_lru, wbl_ref[...])

        gret = gret_ref[...]
        sg = _sigmoid(gret)
        silu = gret * sg
        on, rstds = _group_norm(o_ref[...])
        aret_ref[...] = (silu * on).astype(BF16)
        dgret = da_ret * on * (sg * (1.0 + gret * (1.0 - sg)))
        don = da_ret * silu
        for h in range(HEADS):
            sl = slice(h * DH, (h + 1) * DH)
            onh, donh = on[:, sl], don[:, sl]
            do_ref[:, sl] = rstds[h] * (donh - onh * jnp.mean(donh * onh, axis=-1, keepdims=True))

        gl, gl_grad = _gelu_and_grad(lgate_ref[...])
        hl = hl_ref[...]
        alru_ref[...] = (gl * hl).astype(BF16)
        dlgate = da_lru * hl * gl_grad
        dhl_ref[...] = da_lru * gl

        zeros = jnp.zeros((tm, D), BF16)
        for col in (0, 1, 2, 4):
            dproj_ref[:, col * D:(col + 1) * D] = zeros
        dproj_ref[:, 3 * D:4 * D] = dgret.astype(BF16)
        dproj_ref[:, 5 * D:6 * D] = dlgate.astype(BF16)
        dproj_ref[:, 6 * D:7 * D] = dga.astype(BF16)
        dproj_ref[:, 7 * D:8 * D] = dgb.astype(BF16)

    tile = lambda col: pl.BlockSpec((tm, D), lambda i: (i, col))
    wspec = pl.BlockSpec((D, D), lambda i: (0, 0))
    bf = lambda: jax.ShapeDtypeStruct((rows, D), BF16)
    return _hosted_call(
        body,
        name="mix_bwd",
        grid=(nt,),
        in_specs=[tile(0), tile(0), tile(0), tile(2), tile(3), tile(4), tile(0), tile(0), tile(0), wspec, wspec, wspec],
        out_specs=[pl.BlockSpec((tm, IN_COLS), lambda i: (i, 0))] + [tile(0)] * 7,
        out_shape=[
            jax.ShapeDtypeStruct((rows, IN_COLS), BF16),
            jax.ShapeDtypeStruct((rows, D), F32), jax.ShapeDtypeStruct((rows, D), F32),
            bf(), bf(), bf(), bf(), bf(),
        ],
        scratch_shapes=[],
        args=(dh1, o, proj_f, proj_f, proj_f, proj_f, h_lru, y_ret, y_lru, w_br, w_bl, w_o),
        riders=riders,
    )


def _retention_bwd(dproj, proj_bf, do, sprev, intra, intra_t, q_dec, k_dec, c_dec, cos_t, sin_t, riders=()):
    rows = proj_bf.shape[0]
    nc = rows // CHUNK

    def body(dproj_in_ref, q_ref, k_ref, v_ref, do_ref, sprev_ref, m_ref, mt_ref, qd_ref, kd_ref, cd_ref, cos_ref, sin_ref,
             dproj_ref, ds_sc):
        @pl.when(pl.program_id(0) == 0)
        def _():
            ds_sc[...] = jnp.zeros_like(ds_sc)

        cos, sin = cos_ref[...], sin_ref[...]

        def unrotate(dy):
            return dy * cos - pltpu.roll(dy, DH // 2, axis=1) * sin

        for h in range(HEADS):
            sl = slice(h * DH, (h + 1) * DH)
            q, k, v = q_ref[:, sl], k_ref[:, sl], v_ref[:, sl]
            do = do_ref[:, sl]
            dob = do.astype(BF16)
            doq = (do * qd_ref[:, sl]).astype(BF16)
            state_prev = sprev_ref[0, h]
            dstate = ds_sc[h]
            dstate_b = dstate.astype(BF16)
            s_t = (_dot_nt(k, q) * mt_ref[h]).astype(BF16)
            ds_t = (_dot_nt(v, dob) * mt_ref[h]).astype(BF16)
            ds = (_dot_nt(dob, v) * m_ref[h]).astype(BF16)
            kd = kd_ref[:, sl]
            dq = _dot(ds, k) + _dot_nt(doq, state_prev)
            dk = _dot(ds_t, q) + _dot_nt(v, dstate_b) * kd
            k_scaled = (k.astype(F32) * kd).astype(BF16)
            dv = _dot(s_t, dob) + _dot(k_scaled, dstate_b)
            ds_sc[h] = dstate * cd_ref[:, sl] + _dot_tn(q, doq)
            dproj_ref[:, sl] = unrotate(dq).astype(BF16)
            dproj_ref[:, D + h * DH:D + (h + 1) * DH] = (unrotate(dk) * QK_SCALE).astype(BF16)
            dproj_ref[:, 2 * D + h * DH:2 * D + (h + 1) * DH] = dv.astype(BF16)

    rev = lambda c: nc - 1 - c
    chunk_spec = lambda col: pl.BlockSpec((CHUNK, D), lambda c: (rev(c), col))
    const2 = lambda shape: pl.BlockSpec(shape, lambda c: (0, 0))
    const3 = pl.BlockSpec((HEADS, CHUNK, CHUNK), lambda c: (0, 0, 0))
    return _hosted_call(
        body,
        name="retention_bwd",
        grid=(nc,),
        in_specs=[
            pl.BlockSpec(memory_space=pl.ANY),
            chunk_spec(0), chunk_spec(1), chunk_spec(2), chunk_spec(0),
            pl.BlockSpec((1, HEADS, DH, DH), lambda c: (rev(c), 0, 0, 0)),
            const3, const3,
            const2((CHUNK, D)), const2((CHUNK, D)), const2((1, D)),
            pl.BlockSpec((CHUNK, DH), lambda c: (rev(c), 0)),
            pl.BlockSpec((CHUNK, DH), lambda c: (rev(c), 0)),
        ],
        out_specs=[pl.BlockSpec((CHUNK, 3 * D), lambda c: (rev(c), 0))],
        out_shape=[jax.ShapeDtypeStruct(dproj.shape, BF16)],
        aliases={0: 0},
        scratch_shapes=[pltpu.VMEM((HEADS, DH, DH), F32)],
        args=(dproj, proj_bf, proj_bf, proj_bf, do, sprev, intra, intra_t, q_dec, k_dec, c_dec, cos_t, sin_t),
        riders=riders,
    )


def _lru_bwd(dproj, proj_f, h_lru, dhl, conv_w, conv_b, wa, wx, ba, bx, lam, riders=()):
    rows = proj_f.shape[0]
    nt = rows // TM
    per8 = TM // 8

    def body(dproj_in_ref, x_ref, xprev_ref, h_ref, hprev_ref, dhl_ref, cw_ref, cb_ref, wa_ref, wx_ref, ba_ref, bx_ref, lam_ref,
             dproj_ref, dwa_ref, dwx_ref, stats_ref, anext_sc, dhnext_sc, dcnext_sc, c_sc, b_sc, dh_sc):
        step = pl.program_id(0)
        i = nt - 1 - step

        @pl.when(step == 0)
        def _():
            anext_sc[...] = jnp.zeros_like(anext_sc)
            dhnext_sc[...] = jnp.zeros_like(dhnext_sc)
            dcnext_sc[...] = jnp.zeros_like(dcnext_sc)
            dwa_ref[...] = jnp.zeros_like(dwa_ref)
            dwx_ref[...] = jnp.zeros_like(dwx_ref)
            stats_ref[...] = jnp.zeros_like(stats_ref)

        first = i == 0
        x = x_ref[...]
        prev8 = jnp.where(first, 0.0, xprev_ref[...])
        c, r, gate_i, a, mult, sp, shifted = _conv_and_gates(x, prev8, cw_ref, cb_ref, wa_ref, wx_ref, ba_ref, bx_ref, lam_ref, c_sc)

        b_sc[...] = pltpu.roll(a, TM - 1, axis=0)
        b_sc[TM - 1:TM, :] = anext_sc[0:1, :]
        anext_sc[0:1, :] = a[0:1, :]
        row8 = lax.broadcasted_iota(jnp.int32, (8, D), 0)

        def group(gi, dhnext):
            r0 = pl.multiple_of((per8 - 1 - gi) * 8, 8)
            bb = b_sc[pl.ds(r0, 8), :]
            uu = dhl_ref[pl.ds(r0, 8), :]
            for d in (1, 2, 4):
                b_sh = jnp.where(row8 < 8 - d, pltpu.roll(bb, 8 - d, axis=0), 1.0)
                u_sh = jnp.where(row8 < 8 - d, pltpu.roll(uu, 8 - d, axis=0), 0.0)
                uu = uu + bb * u_sh
                bb = bb * b_sh
            dhb = bb * dhnext + uu
            dh_sc[pl.ds(r0, 8), :] = dhb
            return dhb[0:1, :]

        dhfirst = lax.fori_loop(0, per8, group, dhnext_sc[0:1, :])
        dhnext_sc[0:1, :] = dhfirst
        dh = dh_sc[...]

        h = h_ref[...]
        hprev8 = jnp.where(first, 0.0, hprev_ref[...])
        h_dn, h_head = _shift_down(h, hprev8, 1)
        c_sc[...] = h_dn
        c_sc[0:8, :] = h_head
        h_before = c_sc[...]

        valid = _row_ids(i, TM, (TM, D)) >= PAD_ROWS
        da = dh * h_before
        du = jnp.where(valid, dh, 0.0)
        dmult = du * gate_i * c
        dgate_i = du * mult * c
        dc = du * mult * gate_i
        dla = da * a - dmult * (a * a) / mult
        dla = jnp.where(valid, dla, 0.0)
        dr = dla * ((-LRU_C) * sp)
        dzr = dr * r * (1.0 - r)
        dzi = dgate_i * gate_i * (1.0 - gate_i)
        stats_ref[1:2, :] += jnp.sum(dzr, axis=0, keepdims=True)
        stats_ref[2:3, :] += jnp.sum(dzi, axis=0, keepdims=True)
        stats_ref[3:4, :] += jnp.sum(dla * ((-LRU_C) * r), axis=0, keepdims=True)
        dc_gate = []
        for g in range(LRU_BLOCKS):
            sl = slice(g * LRU_BLOCK, (g + 1) * LRU_BLOCK)
            cg = c[:, sl].astype(BF16)
            dzr_g = dzr[:, sl].astype(BF16)
            dzi_g = dzi[:, sl].astype(BF16)
            dc_gate.append(_dot_nt(dzr_g, wa_ref[g]) + _dot_nt(dzi_g, wx_ref[g]))
            dwa_ref[g] += _dot_tn(cg, dzr_g)
            dwx_ref[g] += _dot_tn(cg, dzi_g)
        dc = dc + jnp.concatenate(dc_gate, axis=1)

        cw = cw_ref[...]
        stats_ref[0:1, :] += jnp.sum(dc, axis=0, keepdims=True)
        stats_ref[7:8, :] += jnp.sum(dc * x, axis=0, keepdims=True)
        dx = cw[3:4] * dc
        tail_src = jnp.concatenate([dc[TM - 8:TM], dcnext_sc[...]], axis=0)
        dx_tail = cw[3:4] * dc[TM - 8:TM]
        for d in (1, 2, 3):
            dx = dx + cw[3 - d:4 - d] * pltpu.roll(dc, TM - d, axis=0)
            dx_tail = dx_tail + cw[3 - d:4 - d] * pltpu.roll(tail_src, 16 - d, axis=0)[0:8]
            rolled, hd = shifted[d - 1]
            b_sc[...] = rolled
            b_sc[0:8, :] = hd
            stats_ref[7 - d:8 - d, :] += jnp.sum(dc * b_sc[...], axis=0, keepdims=True)
        dcnext_sc[...] = dc[0:8]
        dproj_ref[...] = dx.astype(BF16)
        dproj_ref[TM - 8:TM, :] = dx_tail.astype(BF16)

    rev = lambda s: nt - 1 - s
    vec = pl.BlockSpec((1, D), lambda s: (0, 0))
    wspec = pl.BlockSpec((LRU_BLOCKS, LRU_BLOCK, LRU_BLOCK), lambda s: (0, 0, 0))
    prev8 = lambda col: pl.BlockSpec((8, D), lambda s: (jnp.maximum(rev(s) * per8 - 1, 0), col))
    return _hosted_call(
        body,
        name="lru_bwd",
        grid=(nt,),
        in_specs=[
            pl.BlockSpec(memory_space=pl.ANY),
            pl.BlockSpec((TM, D), lambda s: (rev(s), 1)), prev8(1),
            pl.BlockSpec((TM, D), lambda s: (rev(s), 0)), prev8(0),
            pl.BlockSpec((TM, D), lambda s: (rev(s), 0)),
            pl.BlockSpec((4, D), lambda s: (0, 0)),
            vec, wspec, wspec, vec, vec, vec,
        ],
        out_specs=[
            pl.BlockSpec((TM, D), lambda s: (rev(s), 4)),
            wspec, wspec,
            pl.BlockSpec((8, D), lambda s: (0, 0)),
        ],
        out_shape=[
            jax.ShapeDtypeStruct(dproj.shape, BF16),
            jax.ShapeDtypeStruct((LRU_BLOCKS, LRU_BLOCK, LRU_BLOCK), F32),
            jax.ShapeDtypeStruct((LRU_BLOCKS, LRU_BLOCK, LRU_BLOCK), F32),
            jax.ShapeDtypeStruct((8, D), F32),
        ],
        aliases={0: 0},
        scratch_shapes=[
            pltpu.VMEM((8, D), F32), pltpu.VMEM((8, D), F32), pltpu.VMEM((8, D), F32),
            pltpu.VMEM((TM, D), F32), pltpu.VMEM((TM, D), F32), pltpu.VMEM((TM, D), F32),
        ],
        args=(dproj, proj_f, proj_f, h_lru, h_lru, dhl, conv_w, conv_b, wa, wx, ba, bx, lam),
        riders=riders,
    )


def _in_proj_bwd(dproj, w_in, riders=()):
    rows = dproj.shape[0]
    tm = _heavy_tile(rows)

    def body(dproj_ref, w_ref, du_ref):
        part = _dot_nt(dproj_ref[...], w_ref[...])

        @pl.when(pl.program_id(1) == 0)
        def _():
            du_ref[...] = part

        @pl.when(pl.program_id(1) > 0)
        def _():
            du_ref[...] += part

    return _hosted_call(
        body,
        name="in_proj_bwd",
        grid=(rows // tm, 8),
        in_specs=[pl.BlockSpec((tm, D), lambda i, g: (i, g)), pl.BlockSpec((D, D), lambda i, g: (0, g))],
        out_specs=[pl.BlockSpec((tm, D), lambda i, g: (i, 0))],
        out_shape=[jax.ShapeDtypeStruct((rows, D), F32)],
        scratch_shapes=[],
        args=(dproj, w_in),
        riders=riders,
    )


def _norm1_bwd(du, dh1, head, x2d, rstd1, norm_w, riders=()):
    rows = du.shape[0]

    def body(du_ref, dh1_ref, head_ref, x_ref, rstd_ref, nw_ref, gx_ref, ghead_ref, stats_ref):
        i = pl.program_id(0)

        @pl.when(i == 0)
        def _():
            stats_ref[...] = jnp.zeros_like(stats_ref)

        def finish(h0, out_ref):
            du = du_ref[...]
            rs = rstd_ref[...]
            n = h0 * rs
            stats_ref[0:1, :] += jnp.sum(du * n, axis=0, keepdims=True)
            dn = du * nw_ref[...]
            out_ref[...] = dh1_ref[...] + rs * (dn - n * jnp.mean(dn * n, axis=-1, keepdims=True))

        @pl.when(i == 0)
        def _():
            finish(head_ref[...], ghead_ref)

        @pl.when(i > 0)
        def _():
            finish(x_ref[...], gx_ref)

    tile = pl.BlockSpec((TM, D), lambda i: (i, 0))
    return _hosted_call(
        body,
        name="norm1_bwd",
        grid=(rows // TM,),
        in_specs=[
            tile, tile,
            pl.BlockSpec((FRONT, D), lambda i: (0, 0)),
            pl.BlockSpec((TM, D), lambda i: (jnp.maximum(i - 1, 0), 0)),
            pl.BlockSpec((TM, 1), lambda i: (i, 0)),
            pl.BlockSpec((1, D), lambda i: (0, 0)),
        ],
        out_specs=[
            pl.BlockSpec((TM, D), lambda i: (jnp.maximum(i - 1, 0), 0)),
            pl.BlockSpec((FRONT, D), lambda i: (0, 0)),
            pl.BlockSpec((8, D), lambda i: (0, 0)),
        ],
        out_shape=[
            jax.ShapeDtypeStruct(x2d.shape, F32),
            jax.ShapeDtypeStruct((FRONT, D), F32),
            jax.ShapeDtypeStruct((8, D), F32),
        ],
        scratch_shapes=[],
        args=(du, dh1, head, x2d, rstd1, norm_w),
        riders=riders,
    )


def _matmul_tn(name, x, dy, out_cols, col0=0, prev=None, k_block=None, n_block=None, riders=()):
    rows, kdim = x.shape
    ndim = dy.shape[1]
    tr = next(t for t in (1408, TM_HEAVY, TM) if rows % t == 0)
    kb = k_block or kdim
    nb = n_block or ndim
    nr, nk, nn = rows // tr, kdim // kb, ndim // nb
    cb0 = col0 // nb

    def body(*refs):
        x_ref, dy_ref, out_ref = refs[-3], refs[-2], refs[-1]
        part = _dot_tn(x_ref[...].astype(BF16), dy_ref[...].astype(BF16))

        @pl.when(pl.program_id(2) == 0)
        def _():
            out_ref[...] = part

        @pl.when(pl.program_id(2) > 0)
        def _():
            out_ref[...] += part

    in_specs = [
        pl.BlockSpec((tr, kb), lambda n, k, r: (r, k)),
        pl.BlockSpec((tr, nb), lambda n, k, r: (r, n)),
    ]
    args = [x, dy]
    aliases = {}
    if prev is not None:
        in_specs = [pl.BlockSpec(memory_space=pl.ANY)] + in_specs
        args = [prev] + args
        aliases = {0: 0}
    (out,), rider_outs = _hosted_call(
        body,
        name=name,
        grid=(nn, nk, nr),
        in_specs=in_specs,
        out_specs=[pl.BlockSpec((kb, nb), lambda n, k, r: (k, cb0 + n))],
        out_shape=[jax.ShapeDtypeStruct((kdim, out_cols), F32)],
        scratch_shapes=[],
        aliases=aliases,
        args=args,
        riders=riders,
    )
    return (out, rider_outs) if riders else out


class _NoComm:
    def riders(self, host, w, grads):
        return ()

    def after(self, host, rider_outs, w, grads):
        pass


def _local_step(x2d, target, w, plan):
    rows = FRONT + x2d.shape[0]
    head = jnp.concatenate([jnp.zeros((PAD_ROWS, D), F32), w["meta_tokens"]], axis=0)
    cos_t, sin_t = _rope_tables(rows)
    intra, intra_t, q_dec, k_dec, c_dec = _decay_tables()
    grads = {}

    def hosted(host, fn, *args, **kwargs):
        outs, rider_outs = fn(*args, riders=plan.riders(host, w, grads), **kwargs)
        plan.after(host, rider_outs, w, grads)
        return outs

    u1, rstd1 = hosted("norm1", _norm1, head, x2d, w["mix_norm_w"])
    proj_bf, proj_f = hosted("in_proj", _in_proj, u1, w["w_in"], cos_t, sin_t)
    o, sprev = hosted("retention_fwd", _retention_fwd, proj_bf, intra, q_dec, k_dec, c_dec)
    lru_args = (w["conv_w"], w["conv_b"], w["lru_wa"], w["lru_wx"], w["lru_ba"], w["lru_bx"], w["lru_lambda"])
    h_lru = _lru_fwd(proj_f, *lru_args)
    y_ret, y_lru, h1, u2, rstd2 = _mix_fwd(head, x2d, o, proj_f, h_lru, w["w_branch_ret"], w["w_branch_lru"], w["w_out"],
                                           w["ffn_norm_w"])
    g, up, h2 = _ffn_fwd(u2, h1, w["w_ffn_in"], w["w_ffn_out"])
    dh2, stats_loss = _final_loss(h2, target, w["final_norm_w"])

    dg, dup, act, dh1, stats_ffn = _ffn_bwd(dh2, g, up, h1, rstd2, w["w_ffn_in"], w["w_ffn_out"], w["ffn_norm_w"])
    grads["w_ffn_in"] = _matmul_tn("dw_ffn_up", u2, dup, 2 * FFN, col0=FFN, n_block=FFN_HALF,
                                   prev=_matmul_tn("dw_ffn_gate", u2, dg, 2 * FFN, n_block=FFN_HALF))
    grads["w_ffn_out"] = _matmul_tn("dw_ffn_out", act, dh2, D, k_block=FFN_HALF)
    (dproj, do, dhl, mixed, a_ret, a_lru, dy_ret, dy_lru) = hosted(
        "mix_bwd", _mix_bwd, dh1, o, proj_f, h_lru, y_ret, y_lru, w["w_branch_ret"], w["w_branch_lru"], w["w_out"])
    grads["w_out"] = _matmul_tn("dw_out", mixed, dh1, D)
    grads["w_branch_ret"] = _matmul_tn("dw_branch_ret", a_ret, dy_ret, D)
    grads["w_branch_lru"] = _matmul_tn("dw_branch_lru", a_lru, dy_lru, D)
    (dproj,) = hosted("retention_bwd", _retention_bwd, dproj, proj_bf, do, sprev, intra, intra_t, q_dec, k_dec, c_dec,
                      cos_t, sin_t)
    dproj, grads["lru_wa"], grads["lru_wx"], stats_lru = hosted("lru_bwd", _lru_bwd, dproj, proj_f, h_lru, dhl, *lru_args)
    dw_in_riders = plan.riders("dw_in", w, grads)
    if dw_in_riders:
        grads["w_in"], rider_outs = _matmul_tn("dw_in", u1, dproj, IN_COLS, n_block=D, riders=dw_in_riders)
        plan.after("dw_in", rider_outs, w, grads)
    else:
        grads["w_in"] = _matmul_tn("dw_in", u1, dproj, IN_COLS, n_block=D)
    (du1,) = hosted("in_proj_bwd", _in_proj_bwd, dproj, w["w_in"])
    grad_x, grad_head, stats_in = hosted("norm1_bwd", _norm1_bwd, du1, dh1, head, x2d, rstd1, w["mix_norm_w"])
    return grad_x, grad_head, grads, [stats_loss, stats_ffn, stats_in, stats_lru]


BIG_PIECES = {
    "w_in": ("col", (D, 2 * D)),
    "w_ffn_in": ("col", (D, FFN_HALF)),
    "w_ffn_out": ("row", (FFN // 4, D)),
    "w_branch_ret": ("row", (D // 4, D)),
    "w_branch_lru": ("row", (D // 4, D)),
    "w_out": ("row", (D // 4, D)),
    "lru_wa": ("lru", (LRU_BLOCKS, LRU_BLOCK // 4, LRU_BLOCK)),
    "lru_wx": ("lru", (LRU_BLOCKS, LRU_BLOCK // 4, LRU_BLOCK)),
}
SMALL_PIECES = {"meta_tokens": ("col", (N_META, D // 4)), "conv_w": ("col", (4, D // 4))}


def _full_shape(kind, shard):
    if kind == "col":
        return (shard[0], 4 * shard[1])
    if kind == "row":
        return (4 * shard[0], shard[1])
    return (shard[0], 4 * shard[1], shard[2])


def _half_shape(kind, shard):
    return (shard[0] // 2,) + tuple(shard[1:])


def _aligned(start, multiple):
    return start if isinstance(start, int) else pl.multiple_of(start, multiple)


def _lead(h, size):
    if h is None:
        return pl.ds(0, size)
    return pl.ds(_aligned(h * (size // 2), size // 2), size // 2)


def _full_region(ref, kind, shard, s, h):
    if kind == "col":
        return ref.at[_lead(h, shard[0]), pl.ds(_aligned(s * shard[1], shard[1]), shard[1])]
    if kind == "row":
        size = shard[0] if h is None else shard[0] // 2
        start = s * shard[0] + (0 if h is None else h * (shard[0] // 2))
        return ref.at[pl.ds(_aligned(start, 16), size), :]
    return ref.at[_lead(h, shard[0]), pl.ds(_aligned(s * shard[1], shard[1]), shard[1]), :]


def _shard_region(ref, shard, h):
    return ref.at[_lead(h, shard[0])]


def _place():
    x, y, c = lax.axis_index("x"), lax.axis_index("y"), lax.axis_index("c")
    return x, y, c, 2 * x + y


def _other_chip(s, c, k):
    s2 = jnp.bitwise_xor(s, k)
    return s2, (s2 // 2, s2 % 2, c)


def _remote(src, dst, send_sem, recv_sem, dev):
    return pltpu.make_async_remote_copy(src_ref=src, dst_ref=dst, send_sem=send_sem, recv_sem=recv_sem,
                                        device_id=dev, device_id_type=MESH)


_ANY = pl.BlockSpec(memory_space=pl.ANY)


class _Rider:
    def __init__(self, ins, out_shapes, sem_shapes, build, aliased=False):
        self.ins, self.out_shapes, self.sem_shapes, self.build, self.aliased = ins, out_shapes, sem_shapes, build, aliased


def _hosted_call(body, *, name, grid, in_specs, out_specs, out_shape, scratch_shapes, args, riders=(), aliases=None):
    n_in, n_out, n_sc = len(in_specs), len(out_shape), len(scratch_shapes)
    r_in = [a for r in riders for a in r.ins]
    r_out = [s for r in riders for s in r.out_shapes]
    r_sem = [s for r in riders for s in r.sem_shapes]

    def full_body(*refs):
        ins, rin = refs[:n_in], refs[n_in:n_in + len(r_in)]
        o0 = n_in + len(r_in)
        outs, rout = refs[o0:o0 + n_out], refs[o0 + n_out:o0 + n_out + len(r_out)]
        s0 = o0 + n_out + len(r_out)
        scratch, rsem = refs[s0:s0 + n_sc], refs[s0 + n_sc:]
        starts, waits = [], []
        pi = po = ps = 0
        for r in riders:
            st, wt = r.build(rin[pi:pi + len(r.ins)], rout[po:po + len(r.out_shapes)], rsem[ps:ps + len(r.sem_shapes)])
            starts += st
            waits += wt
            pi, po, ps = pi + len(r.ins), po + len(r.out_shapes), ps + len(r.sem_shapes)
        if riders:
            first = functools.reduce(jnp.logical_and, [pl.program_id(d) == 0 for d in range(len(grid))])
            last = functools.reduce(jnp.logical_and, [pl.program_id(d) == grid[d] - 1 for d in range(len(grid))])

            @pl.when(first)
            def _():
                for cp in starts:
                    cp.start()

        body(*ins, *outs, *scratch)
        if riders:
            @pl.when(last)
            def _():
                for wait in waits:
                    wait()

    io_aliases = dict(aliases or {})
    pi = po = 0
    for r in riders:
        if r.aliased:
            for q in range(len(r.ins)):
                io_aliases[n_in + pi + q] = n_out + po + q
        pi, po = pi + len(r.ins), po + len(r.out_shapes)
    res = pl.pallas_call(
        full_body,
        name=name,
        grid=grid,
        in_specs=list(in_specs) + [_ANY] * len(r_in),
        out_specs=list(out_specs) + [_ANY] * len(r_out),
        out_shape=list(out_shape) + r_out,
        scratch_shapes=list(scratch_shapes) + r_sem,
        input_output_aliases=io_aliases,
        compiler_params=_params(("arbitrary",) * len(grid)),
    )(*args, *r_in)
    rider_outs, po = [], n_out
    for r in riders:
        rider_outs.append(list(res[po:po + len(r.out_shapes)]))
        po += len(r.out_shapes)
    return list(res[:n_out]), rider_outs


def _run_riders(name, riders):
    r_in = [a for r in riders for a in r.ins]
    r_out = [s for r in riders for s in r.out_shapes]
    r_sem = [s for r in riders for s in r.sem_shapes]

    def body(*refs):
        rin, rout, rsem = refs[:len(r_in)], refs[len(r_in):len(r_in) + len(r_out)], refs[len(r_in) + len(r_out):]
        pi = po = ps = 0
        for r in riders:
            starts, waits = r.build(rin[pi:pi + len(r.ins)], rout[po:po + len(r.out_shapes)], rsem[ps:ps + len(r.sem_shapes)])
            for cp in starts:
                cp.start()
            for wait in waits:
                wait()
            pi, po, ps = pi + len(r.ins), po + len(r.out_shapes), ps + len(r.sem_shapes)

    io_aliases = {}
    pi = po = 0
    for r in riders:
        if r.aliased:
            for q in range(len(r.ins)):
                io_aliases[pi + q] = po + q
        pi, po = pi + len(r.ins), po + len(r.out_shapes)
    res = pl.pallas_call(
        body,
        name=name,
        in_specs=[_ANY] * len(r_in),
        out_specs=[_ANY] * len(r_out),
        out_shape=r_out,
        scratch_shapes=r_sem,
        input_output_aliases=io_aliases,
    )(*r_in)
    outs, po = [], 0
    for r in riders:
        outs.append(list(res[po:po + len(r.out_shapes)]))
        po += len(r.out_shapes)
    return outs


def _piece(name):
    if name in BIG_PIECES:
        return (name, *BIG_PIECES[name], True)
    return (name, *SMALL_PIECES[name], False)


def _gather_rider(shards, names):
    pieces = [_piece(n) for n in names]
    n = len(pieces)

    def build(ins, outs, sems):
        local_sem, ici_send, ici_recv = sems
        _, _, c, s = _place()
        starts, waits = [], []
        for p, (_, kind, shard, split) in enumerate(pieces):
            cp = pltpu.make_async_copy(ins[p], _full_region(outs[p], kind, shard, s, None), local_sem.at[p])
            starts.append(cp)
            waits.append(cp.wait)
            h = c if split else None
            for k in (1, 2, 3):
                s2, dev = _other_chip(s, c, k)
                cp = _remote(_shard_region(ins[p], shard, h), _full_region(outs[p], kind, shard, s, h),
                             ici_send.at[p, k - 1], ici_recv.at[p, k - 1], dev)
                starts.append(cp)
                waits.append(cp.wait_send)
                region = _full_region(outs[p], kind, shard, s2, h)
                waits.append(_remote(region, region, ici_send.at[p, k - 1], ici_recv.at[p, k - 1], dev).wait_recv)
        return starts, waits

    return _Rider(
        [shards[name] for name in names],
        [jax.ShapeDtypeStruct(_full_shape(kind, shard), shards[name].dtype) for name, kind, shard, _ in pieces],
        [pltpu.SemaphoreType.DMA((n,)), pltpu.SemaphoreType.DMA((n, 3)), pltpu.SemaphoreType.DMA((n, 3))],
        build)


def _forward_rider(gathered, names):
    pieces = [_piece(n) for n in names]
    n = len(pieces)

    def build(ins, outs, sems):
        fwd_send, fwd_recv = sems
        x, y, c, s = _place()
        sibling = (x, y, 1 - c)
        starts, waits = [], []
        for p, (_, kind, shard, _) in enumerate(pieces):
            for k in (1, 2, 3):
                s2, _ = _other_chip(s, c, k)
                mine = _full_region(outs[p], kind, shard, s2, c)
                theirs = _full_region(outs[p], kind, shard, s2, 1 - c)
                cp = _remote(mine, mine, fwd_send.at[p, k - 1], fwd_recv.at[p, k - 1], sibling)
                starts.append(cp)
                waits.append(cp.wait_send)
                waits.append(_remote(theirs, theirs, fwd_send.at[p, k - 1], fwd_recv.at[p, k - 1], sibling).wait_recv)
        return starts, waits

    return _Rider(
        [gathered[name] for name in names],
        [jax.ShapeDtypeStruct(gathered[name].shape, gathered[name].dtype) for name in names],
        [pltpu.SemaphoreType.DMA((n, 3)), pltpu.SemaphoreType.DMA((n, 3))],
        build, aliased=True)


def _pair_exchange_rider(grads, names):
    n = len(names)

    def build(ins, outs, sems):
        send_sem, recv_sem = sems
        x, y, c, _ = _place()
        sibling = (x, y, 1 - c)
        starts, waits = [], []
        for p, name in enumerate(names):
            kind, shard = BIG_PIECES[name]
            for s2 in range(4):
                cp = _remote(_full_region(ins[p], kind, shard, s2, 1 - c), outs[p].at[s2], send_sem.at[p, s2],
                             recv_sem.at[p, s2], sibling)
                starts.append(cp)
                waits.append(cp.wait_send)
                waits.append(_remote(outs[p].at[s2], outs[p].at[s2], send_sem.at[p, s2], recv_sem.at[p, s2], sibling).wait_recv)
        return starts, waits

    return _Rider(
        [grads[name] for name in names],
        [jax.ShapeDtypeStruct((4,) + _half_shape(*BIG_PIECES[name]), F32) for name in names],
        [pltpu.SemaphoreType.DMA((n, 4))] * 2,
        build)


def _half_specs(kind, shard):
    half = shard[0] // 2
    if kind == "col":
        full = pl.BlockSpec((half, shard[1]), lambda j, pr: (pr[1], j))
        buf = pl.BlockSpec((None, half, shard[1]), lambda j, pr: (j, 0, 0))
    elif kind == "row":
        full = pl.BlockSpec((half, shard[1]), lambda j, pr: (2 * j + pr[1], 0))
        buf = pl.BlockSpec((None, half, shard[1]), lambda j, pr: (j, 0, 0))
    else:
        full = pl.BlockSpec((half, shard[1], shard[2]), lambda j, pr: (pr[1], j, 0))
        buf = pl.BlockSpec((None, half, shard[1], shard[2]), lambda j, pr: (j, 0, 0, 0))
    return full, buf


def _pair_sum(name, grad, recv, place):
    kind, shard = BIG_PIECES[name]
    full, buf = _half_specs(kind, shard)

    def body(pr, g_ref, r_ref, o_ref):
        o_ref[...] = (g_ref[...] + r_ref[...]).astype(BF16)

    return pl.pallas_call(
        body,
        name="pair_sum_" + name,
        grid_spec=pltpu.PrefetchScalarGridSpec(num_scalar_prefetch=1, grid=(4,), in_specs=[full, buf], out_specs=buf),
        out_shape=jax.ShapeDtypeStruct((4,) + _half_shape(kind, shard), BF16),
        compiler_params=_params(("arbitrary",)),
    )(place, grad, recv)


def _chip_exchange_rider(sums, names):
    n = len(names)

    def build(ins, outs, sems):
        send_sem, recv_sem = sems
        _, _, c, s = _place()
        starts, waits = [], []
        for p in range(n):
            for k in (1, 2, 3):
                s2, dev = _other_chip(s, c, k)
                cp = _remote(ins[p].at[s2], outs[p].at[k - 1], send_sem.at[p, k - 1], recv_sem.at[p, k - 1], dev)
                starts.append(cp)
                waits.append(cp.wait_send)
                waits.append(_remote(outs[p].at[k - 1], outs[p].at[k - 1], send_sem.at[p, k - 1], recv_sem.at[p, k - 1],
                                     dev).wait_recv)
        return starts, waits

    return _Rider(
        [sums[name] for name in names],
        [jax.ShapeDtypeStruct((3,) + _half_shape(*BIG_PIECES[name]), BF16) for name in names],
        [pltpu.SemaphoreType.DMA((n, 3))] * 2,
        build)


def _chip_sum(name, grad, recv_pair, recv_chip, place):
    kind, shard = BIG_PIECES[name]
    half = shard[0] // 2
    tail = tuple(shard[1:])
    zeros = (0,) * len(tail)
    if kind == "col":
        full = pl.BlockSpec((half,) + tail, lambda j, pr: (pr[1], pr[0]))
    elif kind == "row":
        full = pl.BlockSpec((half,) + tail, lambda j, pr: (2 * pr[0] + pr[1], 0))
    else:
        full = pl.BlockSpec((half,) + tail, lambda j, pr: (pr[1], pr[0], 0))
    pair = pl.BlockSpec((None, half) + tail, lambda j, pr: (pr[0], 0) + zeros)
    chip = pl.BlockSpec((3, half) + tail, lambda j, pr: (0, 0) + zeros)
    out = pl.BlockSpec((half,) + tail, lambda j, pr: (pr[1],) + zeros)

    def body(pr, g_ref, rp_ref, rc_ref, o_ref):
        total = g_ref[...] + rp_ref[...]
        for k in range(3):
            total = total + rc_ref[k].astype(F32)
        o_ref[...] = total

    return pl.pallas_call(
        body,
        name="chip_sum_" + name,
        grid_spec=pltpu.PrefetchScalarGridSpec(num_scalar_prefetch=1, grid=(1,), in_specs=[full, pair, chip], out_specs=out),
        out_shape=jax.ShapeDtypeStruct(shard, F32),
        compiler_params=_params(("arbitrary",)),
    )(place, grad, recv_pair, recv_chip)


def _sibling_exchange_rider(halves, names):
    n = len(names)

    def build(ins, outs, sems):
        send_sem, recv_sem = sems
        x, y, c, _ = _place()
        sibling = (x, y, 1 - c)
        starts, waits = [], []
        for p, name in enumerate(names):
            shard = BIG_PIECES[name][1]
            mine = _shard_region(outs[p], shard, c)
            theirs = _shard_region(outs[p], shard, 1 - c)
            cp = _remote(mine, mine, send_sem.at[p], recv_sem.at[p], sibling)
            starts.append(cp)
            waits.append(cp.wait_send)
            waits.append(_remote(theirs, theirs, send_sem.at[p], recv_sem.at[p], sibling).wait_recv)
        return starts, waits

    return _Rider(
        [halves[name] for name in names],
        [jax.ShapeDtypeStruct(BIG_PIECES[name][1], F32) for name in names],
        [pltpu.SemaphoreType.DMA((n,))] * 2,
        build, aliased=True)


FIRST_WEIGHTS = ["w_in", "meta_tokens", "conv_w"]
LATE_WEIGHTS = [n for n in BIG_PIECES if n != "w_in"]
GRAD_GROUPS = {
    "ffn": ["w_ffn_in", "w_ffn_out"],
    "mixer": ["w_out", "w_branch_ret", "w_branch_lru", "lru_wa", "lru_wx"],
    "in": ["w_in"],
}
GRAD_SCHEDULE = {
    "mix_bwd": [("pair", "ffn")],
    "retention_bwd": [("chip", "ffn")],
    "lru_bwd": [("sibling", "ffn")],
    "dw_in": [("pair", "mixer")],
    "in_proj_bwd": [("chip", "mixer"), ("pair", "in")],
    "norm1_bwd": [("sibling", "mixer"), ("chip", "in")],
}


class _CommPlan:
    def __init__(self, shards, place):
        self.shards, self.place = shards, place
        self.late = None
        self.recv_pair, self.sums, self.recv_chip, self.halves, self.final = {}, {}, {}, {}, {}

    def _grad_rider(self, stage, group, grads):
        names = GRAD_GROUPS[group]
        if stage == "pair":
            return _pair_exchange_rider(grads, names)
        if stage == "chip":
            return _chip_exchange_rider(self.sums, names)
        return _sibling_exchange_rider(self.halves, names)

    def _grad_after(self, stage, group, outs, grads):
        names = GRAD_GROUPS[group]
        if stage == "pair":
            for n, o in zip(names, outs):
                self.recv_pair[n] = o
                self.sums[n] = _pair_sum(n, grads[n], o, self.place)
        elif stage == "chip":
            for n, o in zip(names, outs):
                self.halves[n] = _chip_sum(n, grads[n], self.recv_pair[n], o, self.place)
        else:
            self.final.update(zip(names, outs))

    def riders(self, host, w, grads):
        if host == "norm1":
            return [_forward_rider(w, ["w_in"])]
        if host == "in_proj":
            return [_gather_rider(self.shards, LATE_WEIGHTS)]
        if host == "retention_fwd":
            return [_forward_rider(self.late, LATE_WEIGHTS)]
        return [self._grad_rider(stage, group, grads) for stage, group in GRAD_SCHEDULE.get(host, [])]

    def after(self, host, rider_outs, w, grads):
        if host == "norm1":
            w["w_in"] = rider_outs[0][0]
        elif host == "in_proj":
            self.late = dict(zip(LATE_WEIGHTS, rider_outs[0]))
        elif host == "retention_fwd":
            w.update(zip(LATE_WEIGHTS, rider_outs[0]))
        else:
            for (stage, group), outs in zip(GRAD_SCHEDULE.get(host, []), rider_outs):
                self._grad_after(stage, group, outs, grads)

    def finish(self):
        (outs,) = _run_riders("grad_tail_exchange", [_sibling_exchange_rider(self.halves, GRAD_GROUPS["in"])])
        self.final.update(zip(GRAD_GROUPS["in"], outs))
        return self.final


def _adamw_math(w, g, m, v):
    m = ADAM_B1 * m + (1.0 - ADAM_B1) * g
    v = ADAM_B2 * v + (1.0 - ADAM_B2) * (g * g)
    m_hat = m / (1.0 - ADAM_B1 ** ADAM_STEP)
    v_hat = v / (1.0 - ADAM_B2 ** ADAM_STEP)
    delta = -ADAM_LR * (m_hat / (jnp.sqrt(v_hat) + ADAM_EPS) + ADAM_WD * w)
    return delta, m, v


def _adamw(name, g, w, m, v):
    rows, cols = g.shape
    tr = rows // 4 if rows % 32 == 0 else rows

    def body(g_ref, w_ref, m_ref, v_ref, go_ref, d_ref, mo_ref, vo_ref):
        gv = g_ref[...]
        delta, m2, v2 = _adamw_math(w_ref[...], gv, m_ref[...], v_ref[...])
        go_ref[...] = gv
        d_ref[...] = delta
        mo_ref[...] = m2
        vo_ref[...] = v2

    spec = pl.BlockSpec((tr, cols), lambda i: (i, 0))
    return pl.pallas_call(
        body,
        name="adamw_" + name,
        grid=(rows // tr,),
        in_specs=[spec] * 4,
        out_specs=[spec] * 4,
        out_shape=[jax.ShapeDtypeStruct((rows, cols), F32)] * 4,
        compiler_params=_params(("arbitrary",)),
    )(g, w, m, v)


SMALL_ROWS = 48
VEC_ROWS = {"final_norm_w": 1, "ffn_norm_w": 8, "mix_norm_w": 16, "conv_b": 24, "lru_ba": 25, "lru_bx": 26, "lru_lambda": 27}
VEC_NAMES = list(VEC_ROWS)
CONV_W_ROW = 28
META_ROW = 32


def _small_all_reduce(partial):
    def body(in_ref, out_ref, buf, local_sem, send_sem, recv_sem):
        x, y, c, s = _place()
        me = 2 * s + c
        cp = pltpu.make_async_copy(in_ref, buf.at[me], local_sem)
        cp.start()
        started = []
        for k in range(1, 8):
            peer = jnp.bitwise_xor(me, k)
            dev = (peer // 4, (peer // 2) % 2, peer % 2)
            rd = _remote(in_ref, buf.at[me], send_sem.at[k - 1], recv_sem.at[k - 1], dev)
            rd.start()
            started.append(rd)
        for k in range(1, 8):
            peer = jnp.bitwise_xor(me, k)
            dev = (peer // 4, (peer // 2) % 2, peer % 2)
            _remote(in_ref, buf.at[peer], send_sem.at[k - 1], recv_sem.at[k - 1], dev).wait_recv()
        for rd in started:
            rd.wait_send()
        cp.wait()
        total = buf[0]
        for d in range(1, 8):
            total = total + buf[d]
        out_ref[...] = total

    return pl.pallas_call(
        body,
        name="small_all_reduce",
        in_specs=[pl.BlockSpec(memory_space=pltpu.VMEM)],
        out_specs=pl.BlockSpec(memory_space=pltpu.VMEM),
        out_shape=jax.ShapeDtypeStruct((SMALL_ROWS, D), F32),
        scratch_shapes=[pltpu.VMEM((8, SMALL_ROWS, D), F32), pltpu.SemaphoreType.DMA,
                        pltpu.SemaphoreType.DMA((7,)), pltpu.SemaphoreType.DMA((7,))],
    )(partial)


def _small_update(total, place, vecs, conv, meta):
    nvec = len(VEC_NAMES)
    qcols = D // 4

    def body(pr, tot_ref, col_ref, *refs):
        vec_refs = refs[:3 * nvec]
        conv_refs = refs[3 * nvec:3 * nvec + 3]
        meta_refs = refs[3 * nvec + 3:3 * nvec + 6]
        outs = refs[3 * nvec + 6:]
        loss_ref, vec_out, conv_out, meta_out = outs[0], outs[1:5], outs[5:9], outs[9:13]
        loss_ref[...] = jnp.sum(tot_ref[0:1, :], axis=1, keepdims=True)
        for o in vec_out:
            o[...] = jnp.zeros_like(o)
        for j, name in enumerate(VEC_NAMES):
            w, m, v = (r[...] for r in vec_refs[3 * j:3 * j + 3])
            g = tot_ref[VEC_ROWS[name]:VEC_ROWS[name] + 1, :]
            if name == "lru_lambda":
                g = -g / (1.0 + jnp.exp(w))
            for o, val in zip(vec_out, (g,) + _adamw_math(w, g, m, v)):
                o[j:j + 1, :] = val
        for row, n_rows, ins, group in ((CONV_W_ROW, 4, conv_refs, conv_out), (META_ROW, N_META, meta_refs, meta_out)):
            g = col_ref[row:row + n_rows, :]
            w, m, v = (r[...] for r in ins)
            for o, val in zip(group, (g,) + _adamw_math(w, g, m, v)):
                o[...] = val

    whole = lambda shape: pl.BlockSpec(shape, lambda i, pr: (0,) * len(shape))
    in_specs = [whole((SMALL_ROWS, D)), pl.BlockSpec((SMALL_ROWS, qcols), lambda i, pr: (0, pr[0]))]
    in_specs += [whole((1, D))] * (3 * nvec) + [whole((4, qcols))] * 3 + [whole((N_META, qcols))] * 3
    out_shapes = [(1, 1)] + [(8, D)] * 4 + [(4, qcols)] * 4 + [(N_META, qcols)] * 4
    return pl.pallas_call(
        body,
        name="small_update",
        grid_spec=pltpu.PrefetchScalarGridSpec(num_scalar_prefetch=1, grid=(1,), in_specs=in_specs,
                                               out_specs=[whole(s) for s in out_shapes]),
        out_shape=[jax.ShapeDtypeStruct(s, F32) for s in out_shapes],
        compiler_params=_params(("arbitrary",)),
    )(place, total, total, *[a for t in vecs for a in t], *conv, *meta)


WEIGHT_ORDER = ["meta_tokens", "mix_norm_w", "w_in", "conv_w", "conv_b", "lru_wa", "lru_ba", "lru_wx", "lru_bx", "lru_lambda",
                "w_branch_ret", "w_branch_lru", "w_out", "ffn_norm_w", "w_ffn_in", "w_ffn_out", "final_norm_w"]


def kernel(x, meta_tokens, mix_norm_w, w_in, conv_w, conv_b, lru_wa, lru_ba, lru_wx, lru_bx, lru_lambda, w_branch_ret, w_branch_lru, w_out, ffn_norm_w, w_ffn_in, w_ffn_out, final_norm_w, loss_target, m_meta_tokens, m_mix_norm_w, m_w_in, m_conv_w, m_conv_b, m_lru_wa, m_lru_ba, m_lru_wx, m_lru_bx, m_lru_lambda, m_w_branch_ret, m_w_branch_lru, m_w_out, m_ffn_norm_w, m_w_ffn_in, m_w_ffn_out, m_final_norm_w, v_meta_tokens, v_mix_norm_w, v_w_in, v_conv_w, v_conv_b, v_lru_wa, v_lru_ba, v_lru_wx, v_lru_bx, v_lru_lambda, v_w_branch_ret, v_w_branch_lru, v_w_out, v_ffn_norm_w, v_w_ffn_in, v_w_ffn_out, v_final_norm_w):
    args = locals()
    wts = {n: args[n] for n in WEIGHT_ORDER}
    mom = {n: args["m_" + n] for n in WEIGHT_ORDER}
    var = {n: args["v_" + n] for n in WEIGHT_ORDER}
    place = jnp.stack([2 * lax.axis_index("x") + lax.axis_index("y"), lax.axis_index("c")]).astype(jnp.int32)

    shards = {n: wts[n][0].astype(BF16) for n in BIG_PIECES}
    shards["meta_tokens"] = wts["meta_tokens"]
    shards["conv_w"] = wts["conv_w"][0]
    plan = _CommPlan(shards, place)
    (first,) = _run_riders("gather_first", [_gather_rider(shards, FIRST_WEIGHTS)])
    w = dict(zip(FIRST_WEIGHTS, first))
    for n in VEC_NAMES:
        w[n] = wts[n].reshape(1, D)

    grad_x, grad_head, _, stats = _local_step(x[0], loss_target[0], w, plan)
    shard_grads = plan.finish()

    out = {}
    for n in BIG_PIECES:
        shape2d = (-1, wts[n].shape[-1])
        res = _adamw(n, *[a.reshape(shape2d) for a in (shard_grads[n], wts[n], mom[n], var[n])])
        out[n] = [r.reshape(wts[n].shape) for r in res]

    partial = jnp.concatenate(stats + [grad_head[PAD_ROWS:]], axis=0)
    total = _small_all_reduce(partial)
    vecs = [tuple(a[n].reshape(1, D) for a in (wts, mom, var)) for n in VEC_NAMES]
    conv = tuple(a["conv_w"][0] for a in (wts, mom, var))
    meta = tuple(a["meta_tokens"] for a in (wts, mom, var))
    res = _small_update(total, place, vecs, conv, meta)
    loss = res[0].reshape(())
    for j, n in enumerate(VEC_NAMES):
        out[n] = [r[j].reshape(wts[n].shape) for r in res[1:5]]
    out["conv_w"] = [r.reshape(wts["conv_w"].shape) for r in res[5:9]]
    out["meta_tokens"] = list(res[9:13])

    return (loss, grad_x.reshape(x.shape)) + tuple(out[n][kind] for kind in range(4) for n in WEIGHT_ORDER)
```

```python
import functools
import math

import jax
import jax.numpy as jnp
from jax import lax
from jax.experimental import pallas as pl
from jax.experimental.pallas import tpu as pltpu

F32 = jnp.float32
BF16 = jnp.bfloat16

D = 1024
HEADS = 8
DH = 128
CHUNK = 128
N_META = 16
FRONT = 256
PAD_ROWS = FRONT - N_META
LRU_BLOCKS = 4
LRU_BLOCK = 256
LRU_C = 8.0
FFN = 2816
FFN_HALF = FFN // 2
IN_COLS = 8 * D
ROPE_BASE = 10000.0
EPS = 1e-6
QK_SCALE = DH ** -0.5

ADAM_LR = 0.001
ADAM_B1 = 0.9
ADAM_B2 = 0.999
ADAM_EPS = 1e-08
ADAM_WD = 0.01
ADAM_STEP = 10

TM = 256
TM_HEAVY = 768
FFN_BLOCK = FFN // 2
TM_MIX_BWD = 128
VMEM_LIMIT = 56 * 1024 * 1024

NT_DIMS = (((1,), (1,)), ((), ()))
TN_DIMS = (((0,), (0,)), ((), ()))
MESH = pl.DeviceIdType.MESH


def _params(sem=None):
    if sem is None:
        return pltpu.CompilerParams(vmem_limit_bytes=VMEM_LIMIT)
    return pltpu.CompilerParams(dimension_semantics=sem, vmem_limit_bytes=VMEM_LIMIT)


def _dot(a, b):
    return jnp.dot(a, b, preferred_element_type=F32)


def _dot_nt(a, b):
    return lax.dot_general(a, b, NT_DIMS, preferred_element_type=F32)


def _dot_tn(a, b):
    return lax.dot_general(a, b, TN_DIMS, preferred_element_type=F32)


def _sigmoid(z):
    return 1.0 / (1.0 + jnp.exp(-z))


def _log1p(x):
    return jnp.where(x < 1e-3, x * (1.0 - x * (0.5 - x * (1.0 / 3.0))), jnp.log(1.0 + x))


def _softplus(x):
    return jnp.maximum(x, 0.0) + _log1p(jnp.exp(-jnp.abs(x)))


def _neg_expm1(x):
    series = -x * (1.0 + x * (0.5 + x * (1.0 / 6.0 + x * (1.0 / 24.0 + x * (1.0 / 120.0)))))
    return jnp.where(x > -0.05, series, 1.0 - jnp.exp(x))


_GELU_K = math.sqrt(2.0 / math.pi)


def _gelu_and_grad(x):
    inner = _GELU_K * (x + 0.044715 * x * x * x)
    t = jnp.tanh(inner)
    val = 0.5 * x * (1.0 + t)
    grad = 0.5 * (1.0 + t) + 0.5 * x * (1.0 - t * t) * _GELU_K * (1.0 + 3.0 * 0.044715 * x * x)
    return val, grad


def _row_ids(i, rows, shape):
    return i * rows + lax.broadcasted_iota(jnp.int32, shape, 0)


def _rope_tables(rows):
    pos = jnp.arange(rows, dtype=jnp.int32) - PAD_ROWS
    inv_freq = ROPE_BASE ** (-jnp.arange(0, DH, 2, dtype=F32) / DH)
    ang = pos.astype(F32)[:, None] * inv_freq[None, :]
    cos, sin = jnp.cos(ang), jnp.sin(ang)
    return jnp.concatenate([cos, cos], axis=1), jnp.concatenate([-sin, sin], axis=1)


def _decay_tables():
    log_g = jnp.log(1.0 - 2.0 ** (-5.0 - jnp.arange(HEADS, dtype=F32)))
    idx = jnp.arange(CHUNK, dtype=F32)
    diff = idx[:, None] - idx[None, :]
    intra = jnp.where(diff[None] >= 0, jnp.exp(jnp.maximum(diff, 0.0)[None] * log_g[:, None, None]), 0.0)
    q_decay = jnp.exp((idx + 1.0)[:, None] * log_g[None, :])
    k_decay = jnp.exp((CHUNK - 1.0 - idx)[:, None] * log_g[None, :])
    chunk_decay = jnp.exp(CHUNK * log_g)
    wide = lambda a: jnp.repeat(a, DH, axis=-1)
    return intra, jnp.swapaxes(intra, 1, 2), wide(q_decay), wide(k_decay), wide(chunk_decay[None, :])


def _norm1(head, x2d, norm_w, riders=()):
    rows = FRONT + x2d.shape[0]

    def body(head_ref, x_ref, nw_ref, u_ref, rstd_ref):
        def norm(hv):
            rs = lax.rsqrt(jnp.mean(hv * hv, axis=-1, keepdims=True) + EPS)
            u_ref[...] = ((hv * rs) * nw_ref[...]).astype(BF16)
            rstd_ref[...] = rs

        @pl.when(pl.program_id(0) == 0)
        def _():
            norm(head_ref[...])

        @pl.when(pl.program_id(0) > 0)
        def _():
            norm(x_ref[...])

    return _hosted_call(
        body,
        name="norm1",
        grid=(rows // TM,),
        in_specs=[
            pl.BlockSpec((FRONT, D), lambda i: (0, 0)),
            pl.BlockSpec((TM, D), lambda i: (jnp.maximum(i - 1, 0), 0)),
            pl.BlockSpec((1, D), lambda i: (0, 0)),
        ],
        out_specs=[pl.BlockSpec((TM, D), lambda i: (i, 0)), pl.BlockSpec((TM, 1), lambda i: (i, 0))],
        out_shape=[jax.ShapeDtypeStruct((rows, D), BF16), jax.ShapeDtypeStruct((rows, 1), F32)],
        scratch_shapes=[],
        args=(head, x2d, norm_w),
        riders=riders,
    )


def _heavy_tile(rows):
    return TM_HEAVY if rows % TM_HEAVY == 0 else TM


def _in_proj(u, w_in, cos_t, sin_t, riders=()):
    rows = u.shape[0]
    tm = _heavy_tile(rows)

    def body(u_ref, w_ref, cos_ref, sin_ref, pbf_ref, pf_ref):
        g = pl.program_id(1)
        acc = _dot(u_ref[...], w_ref[...])

        @pl.when(g < 2)
        def _():
            scale = jnp.where(g == 1, QK_SCALE, 1.0).astype(F32)
            for h in range(HEADS):
                sl = slice(h * DH, (h + 1) * DH)
                blk = acc[:, sl]
                out = (blk * cos_ref[...] + pltpu.roll(blk, DH // 2, axis=1) * sin_ref[...]) * scale
                pbf_ref[:, sl] = out.astype(BF16)

        @pl.when(g == 2)
        def _():
            pbf_ref[...] = acc.astype(BF16)

        @pl.when(g >= 3)
        def _():
            pf_ref[...] = acc

    return _hosted_call(
        body,
        name="in_proj",
        grid=(rows // tm, 8),
        in_specs=[
            pl.BlockSpec((tm, D), lambda i, g: (i, 0)),
            pl.BlockSpec((D, D), lambda i, g: (0, g)),
            pl.BlockSpec((tm, DH), lambda i, g: (i, 0)),
            pl.BlockSpec((tm, DH), lambda i, g: (i, 0)),
        ],
        out_specs=[
            pl.BlockSpec((tm, D), lambda i, g: (i, jnp.minimum(g, 2))),
            pl.BlockSpec((tm, D), lambda i, g: (i, jnp.maximum(g - 3, 0))),
        ],
        out_shape=[jax.ShapeDtypeStruct((rows, 3 * D), BF16), jax.ShapeDtypeStruct((rows, 5 * D), F32)],
        scratch_shapes=[],
        args=(u, w_in, cos_t, sin_t),
        riders=riders,
    )


def _retention_fwd(proj_bf, intra, q_dec, k_dec, c_dec, riders=()):
    rows = proj_bf.shape[0]
    nc = rows // CHUNK

    def body(q_ref, k_ref, v_ref, m_ref, qd_ref, kd_ref, cd_ref, o_ref, sprev_ref, s_sc):
        @pl.when(pl.program_id(0) == 0)
        def _():
            s_sc[...] = jnp.zeros_like(s_sc)

        for h in range(HEADS):
            sl = slice(h * DH, (h + 1) * DH)
            q, k, v = q_ref[:, sl], k_ref[:, sl], v_ref[:, sl]
            state = s_sc[h]
            state_b = state.astype(BF16)
            sprev_ref[0, h] = state_b
            s = _dot_nt(q, k) * m_ref[h]
            inner = _dot(s.astype(BF16), v)
            cross = _dot(q, state_b) * qd_ref[:, sl]
            o_ref[:, sl] = inner + cross
            k_scaled = (k.astype(F32) * kd_ref[:, sl]).astype(BF16)
            s_sc[h] = state * cd_ref[:, sl] + _dot_tn(k_scaled, v)

    chunk_spec = lambda col: pl.BlockSpec((CHUNK, D), lambda c: (c, col))
    const2 = lambda shape: pl.BlockSpec(shape, lambda c: (0, 0))
    return _hosted_call(
        body,
        name="retention_fwd",
        grid=(nc,),
        in_specs=[
            chunk_spec(0), chunk_spec(1), chunk_spec(2),
            pl.BlockSpec((HEADS, CHUNK, CHUNK), lambda c: (0, 0, 0)),
            const2((CHUNK, D)), const2((CHUNK, D)), const2((1, D)),
        ],
        out_specs=[
            pl.BlockSpec((CHUNK, D), lambda c: (c, 0)),
            pl.BlockSpec((1, HEADS, DH, DH), lambda c: (c, 0, 0, 0)),
        ],
        out_shape=[
            jax.ShapeDtypeStruct((rows, D), F32),
            jax.ShapeDtypeStruct((nc, HEADS, DH, DH), BF16),
        ],
        scratch_shapes=[pltpu.VMEM((HEADS, DH, DH), F32)],
        args=(proj_bf, proj_bf, proj_bf, intra, q_dec, k_dec, c_dec),
        riders=riders,
    )


def _shift_down(x, first8_prev, d):
    rolled = pltpu.roll(x, d, axis=0)
    head = pltpu.roll(jnp.concatenate([first8_prev, x[0:8]], axis=0), d, axis=0)[8:16]
    return rolled, head


def _conv_and_gates(x, prev8, cw_ref, cb_ref, wa_ref, wx_ref, ba_ref, bx_ref, lam_ref, c_sc):
    cw = cw_ref[...]
    conv = cb_ref[...] + cw[3:4] * x
    head = cb_ref[...] + cw[3:4] * x[0:8]
    shifted = []
    for d in (1, 2, 3):
        rolled, hd = _shift_down(x, prev8, d)
        conv = conv + cw[3 - d:4 - d] * rolled
        head = head + cw[3 - d:4 - d] * hd
        shifted.append((rolled, hd))
    c_sc[...] = conv
    c_sc[0:8, :] = head
    c = c_sc[...]
    zr, zi = [], []
    for g in range(LRU_BLOCKS):
        sl = slice(g * LRU_BLOCK, (g + 1) * LRU_BLOCK)
        cg = c[:, sl].astype(BF16)
        zr.append(_dot(cg, wa_ref[g]))
        zi.append(_dot(cg, wx_ref[g]))
    r = _sigmoid(jnp.concatenate(zr, axis=1) + ba_ref[...])
    gate_i = _sigmoid(jnp.concatenate(zi, axis=1) + bx_ref[...])
    sp = _softplus(-lam_ref[...])
    log_a = (-LRU_C) * r * sp
    a = jnp.exp(log_a)
    mult = jnp.sqrt(_neg_expm1(2.0 * log_a))
    return c, r, gate_i, a, mult, sp, shifted


def _lru_fwd(proj_f, conv_w, conv_b, wa, wx, ba, bx, lam):
    rows = proj_f.shape[0]
    nt = rows // TM

    def body(x_ref, cw_ref, cb_ref, wa_ref, wx_ref, ba_ref, bx_ref, lam_ref, h_ref, prev_sc, carry_sc, c_sc, a_sc, u_sc):
        i = pl.program_id(0)

        @pl.when(i == 0)
        def _():
            prev_sc[...] = jnp.zeros_like(prev_sc)
            carry_sc[...] = jnp.zeros_like(carry_sc)

        x = x_ref[...]
        c, r, gate_i, a, mult, _, _ = _conv_and_gates(x, prev_sc[...], cw_ref, cb_ref, wa_ref, wx_ref, ba_ref, bx_ref, lam_ref, c_sc)
        prev_sc[...] = x[TM - 8:TM]
        valid = _row_ids(i, TM, (TM, D)) >= PAD_ROWS
        a_sc[...] = a
        u_sc[...] = jnp.where(valid, mult * gate_i * c, 0.0)
        row8 = lax.broadcasted_iota(jnp.int32, (8, D), 0)

        def group(gi, hprev):
            r0 = pl.multiple_of(gi * 8, 8)
            aa = a_sc[pl.ds(r0, 8), :]
            uu = u_sc[pl.ds(r0, 8), :]
            for d in (1, 2, 4):
                a_sh = jnp.where(row8 >= d, pltpu.roll(aa, d, axis=0), 1.0)
                u_sh = jnp.where(row8 >= d, pltpu.roll(uu, d, axis=0), 0.0)
                uu = uu + aa * u_sh
                aa = aa * a_sh
            hb = aa * hprev + uu
            h_ref[pl.ds(r0, 8), :] = hb
            return hb[7:8, :]

        hlast = lax.fori_loop(0, TM // 8, group, carry_sc[0:1, :])
        carry_sc[0:1, :] = hlast

    vec = pl.BlockSpec((1, D), lambda i: (0, 0))
    wspec = pl.BlockSpec((LRU_BLOCKS, LRU_BLOCK, LRU_BLOCK), lambda i: (0, 0, 0))
    return pl.pallas_call(
        body,
        name="lru_fwd",
        grid=(nt,),
        in_specs=[
            pl.BlockSpec((TM, D), lambda i: (i, 1)),
            pl.BlockSpec((4, D), lambda i: (0, 0)),
            vec, wspec, wspec, vec, vec, vec,
        ],
        out_specs=pl.BlockSpec((TM, D), lambda i: (i, 0)),
        out_shape=jax.ShapeDtypeStruct((rows, D), F32),
        scratch_shapes=[
            pltpu.VMEM((8, D), F32), pltpu.VMEM((8, D), F32),
            pltpu.VMEM((TM, D), F32), pltpu.VMEM((TM, D), F32), pltpu.VMEM((TM, D), F32),
        ],
        compiler_params=_params(("arbitrary",)),
    )(proj_f, conv_w, conv_b, wa, wx, ba, bx, lam)


def _group_norm(o):
    outs, rstds = [], []
    for h in range(HEADS):
        oh = o[:, h * DH:(h + 1) * DH]
        rs = lax.rsqrt(jnp.mean(oh * oh, axis=-1, keepdims=True) + EPS)
        outs.append(oh * rs)
        rstds.append(rs)
    return jnp.concatenate(outs, axis=1), rstds


def _mix_fwd(head, x2d, o, proj_f, h_lru, w_br, w_bl, w_o, ffn_norm_w):
    rows = o.shape[0]
    nt = rows // TM

    def body(head_ref, x_ref, o_ref, gret_ref, lgate_ref, ga_ref, gb_ref, hl_ref, wbr_ref, wbl_ref, wo_ref, nw_ref,
             yret_ref, ylru_ref, h1_ref, u2_ref, rstd_ref):
        i = pl.program_id(0)
        on, _ = _group_norm(o_ref[...])
        gret = gret_ref[...]
        a_ret = (gret * _sigmoid(gret) * on).astype(BF16)
        y_ret = _dot(a_ret, wbr_ref[...])
        gl, _ = _gelu_and_grad(lgate_ref[...])
        a_lru = (gl * hl_ref[...]).astype(BF16)
        y_lru = _dot(a_lru, wbl_ref[...])
        mixed = (_sigmoid(ga_ref[...]) * y_ret + _sigmoid(gb_ref[...]) * y_lru).astype(BF16)
        delta = _dot(mixed, wo_ref[...])
        yret_ref[...] = y_ret
        ylru_ref[...] = y_lru

        def finish(h0):
            h1 = h0 + delta
            rs = lax.rsqrt(jnp.mean(h1 * h1, axis=-1, keepdims=True) + EPS)
            h1_ref[...] = h1
            u2_ref[...] = ((h1 * rs) * nw_ref[...]).astype(BF16)
            rstd_ref[...] = rs

        @pl.when(i == 0)
        def _():
            finish(head_ref[...])

        @pl.when(i > 0)
        def _():
            finish(x_ref[...])

    tile = lambda col: pl.BlockSpec((TM, D), lambda i: (i, col))
    wspec = pl.BlockSpec((D, D), lambda i: (0, 0))
    return pl.pallas_call(
        body,
        name="mix_fwd",
        grid=(nt,),
        in_specs=[
            pl.BlockSpec((FRONT, D), lambda i: (0, 0)),
            pl.BlockSpec((TM, D), lambda i: (jnp.maximum(i - 1, 0), 0)),
            tile(0), tile(0), tile(2), tile(3), tile(4), tile(0),
            wspec, wspec, wspec,
            pl.BlockSpec((1, D), lambda i: (0, 0)),
        ],
        out_specs=[tile(0), tile(0), tile(0), tile(0), pl.BlockSpec((TM, 1), lambda i: (i, 0))],
        out_shape=[
            jax.ShapeDtypeStruct((rows, D), F32), jax.ShapeDtypeStruct((rows, D), F32),
            jax.ShapeDtypeStruct((rows, D), F32), jax.ShapeDtypeStruct((rows, D), BF16),
            jax.ShapeDtypeStruct((rows, 1), F32),
        ],
        compiler_params=_params(("arbitrary",)),
    )(head, x2d, o, proj_f, proj_f, proj_f, proj_f, h_lru, w_br, w_bl, w_o, ffn_norm_w)


def _ffn_fwd(u2, h1, w_ffn_in, w_ffn_out):
    rows = u2.shape[0]
    tm = _heavy_tile(rows)
    nb = FFN // FFN_BLOCK
    hid = lambda: pl.BlockSpec((tm, FFN_BLOCK), lambda i, j: (i, j))

    def gate_body(u2_ref, wg_ref, wup_ref, g_ref, up_ref, act_ref):
        u2 = u2_ref[...]
        g = _dot(u2, wg_ref[...])
        up = _dot(u2, wup_ref[...])
        g_ref[...] = g.astype(BF16)
        up_ref[...] = up.astype(BF16)
        act_ref[...] = (g * _sigmoid(g) * up).astype(BF16)

    g, up, act = pl.pallas_call(
        gate_body,
        name="ffn_fwd_gate",
        grid=(rows // tm, nb),
        in_specs=[
            pl.BlockSpec((tm, D), lambda i, j: (i, 0)),
            pl.BlockSpec((D, FFN_BLOCK), lambda i, j: (0, j)),
            pl.BlockSpec((D, FFN_BLOCK), lambda i, j: (0, nb + j)),
        ],
        out_specs=[hid(), hid(), hid()],
        out_shape=[jax.ShapeDtypeStruct((rows, FFN), BF16)] * 3,
        compiler_params=_params(("arbitrary", "arbitrary")),
    )(u2, w_ffn_in, w_ffn_in)

    def out_body(act_ref, h1_ref, wo_ref, h2_ref):
        part = _dot(act_ref[...], wo_ref[...])

        @pl.when(pl.program_id(1) == 0)
        def _():
            h2_ref[...] = h1_ref[...] + part

        @pl.when(pl.program_id(1) > 0)
        def _():
            h2_ref[...] += part

    h2 = pl.pallas_call(
        out_body,
        name="ffn_fwd_out",
        grid=(rows // tm, nb),
        in_specs=[hid(), pl.BlockSpec((tm, D), lambda i, j: (i, 0)), pl.BlockSpec((FFN_BLOCK, D), lambda i, j: (j, 0))],
        out_specs=pl.BlockSpec((tm, D), lambda i, j: (i, 0)),
        out_shape=jax.ShapeDtypeStruct((rows, D), F32),
        compiler_params=_params(("arbitrary", "arbitrary")),
    )(act, h1, w_ffn_out)
    return g, up, act, h2


def _final_loss(h2, target, final_norm_w):
    rows = h2.shape[0]

    def body(h2_ref, tgt_ref, fnw_ref, dh2_ref, stats_ref):
        i = pl.program_id(0)

        @pl.when(i == 0)
        def _():
            stats_ref[...] = jnp.zeros_like(stats_ref)

        h2 = h2_ref[...]
        rs = lax.rsqrt(jnp.mean(h2 * h2, axis=-1, keepdims=True) + EPS)
        n = h2 * rs
        fnw = fnw_ref[...]
        valid = _row_ids(i, TM, (TM, D)) >= FRONT
        diff = jnp.where(valid, n * fnw - tgt_ref[...], 0.0)
        dy = diff * (1.0 / D)
        stats_ref[0:1, :] += (0.5 / D) * jnp.sum(diff * diff, axis=0, keepdims=True)
        stats_ref[1:2, :] += jnp.sum(dy * n, axis=0, keepdims=True)
        dn = dy * fnw
        dh2_ref[...] = rs * (dn - n * jnp.mean(dn * n, axis=-1, keepdims=True))

    return pl.pallas_call(
        body,
        name="final_loss",
        grid=(rows // TM,),
        in_specs=[
            pl.BlockSpec((TM, D), lambda i: (i, 0)),
            pl.BlockSpec((TM, D), lambda i: (jnp.maximum(i - 1, 0), 0)),
            pl.BlockSpec((1, D), lambda i: (0, 0)),
        ],
        out_specs=[pl.BlockSpec((TM, D), lambda i: (i, 0)), pl.BlockSpec((8, D), lambda i: (0, 0))],
        out_shape=[jax.ShapeDtypeStruct((rows, D), F32), jax.ShapeDtypeStruct((8, D), F32)],
        compiler_params=_params(("arbitrary",)),
    )(h2, target, final_norm_w)


def _ffn_bwd(dh2, g, up, h1, rstd2, w_ffn_in, w_ffn_out, ffn_norm_w):
    rows = dh2.shape[0]
    tm = _heavy_tile(rows)
    nb = FFN // FFN_BLOCK
    blk = lambda: pl.BlockSpec((tm, FFN_BLOCK), lambda i, j: (i, j))

    def gate_body(dh2_ref, g_ref, up_ref, wo_ref, dg_ref, dup_ref, dh2b_sc):
        @pl.when(pl.program_id(1) == 0)
        def _():
            dh2b_sc[...] = dh2_ref[...].astype(BF16)

        dact = _dot_nt(dh2b_sc[...], wo_ref[...])
        g = g_ref[...].astype(F32)
        up = up_ref[...].astype(F32)
        sg = _sigmoid(g)
        dup_ref[...] = (dact * (g * sg)).astype(BF16)
        dg_ref[...] = (dact * up * (sg * (1.0 + g * (1.0 - sg)))).astype(BF16)

    dg, dup = pl.pallas_call(
        gate_body,
        name="ffn_bwd_gate",
        grid=(rows // tm, nb),
        in_specs=[
            pl.BlockSpec((tm, D), lambda i, j: (i, 0)),
            blk(), blk(),
            pl.BlockSpec((FFN_BLOCK, D), lambda i, j: (j, 0)),
        ],
        out_specs=[blk(), blk()],
        out_shape=[jax.ShapeDtypeStruct((rows, FFN), BF16)] * 2,
        scratch_shapes=[pltpu.VMEM((tm, D), BF16)],
        compiler_params=_params(("arbitrary", "arbitrary")),
    )(dh2, g, up, w_ffn_out)

    def body(dg_ref, dup_ref, dh2_ref, h1_ref, rstd_ref, wg_ref, wup_ref, nw_ref, dh1_ref, stats_ref, du_sc):
        i = pl.program_id(0)
        j = pl.program_id(1)

        @pl.when(jnp.logical_and(i == 0, j == 0))
        def _():
            stats_ref[...] = jnp.zeros_like(stats_ref)

        part = _dot_nt(dg_ref[...], wg_ref[...]) + _dot_nt(dup_ref[...], wup_ref[...])

        @pl.when(j == 0)
        def _():
            du_sc[...] = part

        @pl.when(j > 0)
        def _():
            du_sc[...] += part

        @pl.when(j == nb - 1)
        def _():
            du = du_sc[...]
            rs = rstd_ref[...]
            n = h1_ref[...] * rs
            stats_ref[0:1, :] += jnp.sum(du * n, axis=0, keepdims=True)
            dn = du * nw_ref[...]
            dh1_ref[...] = dh2_ref[...] + rs * (dn - n * jnp.mean(dn * n, axis=-1, keepdims=True))

    dh1, stats = pl.pallas_call(
        body,
        name="ffn_bwd_in",
        grid=(rows // tm, nb),
        in_specs=[
            blk(), blk(),
            pl.BlockSpec((tm, D), lambda i, j: (i, 0)),
            pl.BlockSpec((tm, D), lambda i, j: (i, 0)),
            pl.BlockSpec((tm, 1), lambda i, j: (i, 0)),
            pl.BlockSpec((D, FFN_BLOCK), lambda i, j: (0, j)),
            pl.BlockSpec((D, FFN_BLOCK), lambda i, j: (0, nb + j)),
            pl.BlockSpec((1, D), lambda i, j: (0, 0)),
        ],
        out_specs=[pl.BlockSpec((tm, D), lambda i, j: (i, 0)), pl.BlockSpec((8, D), lambda i, j: (0, 0))],
        out_shape=[jax.ShapeDtypeStruct((rows, D), F32), jax.ShapeDtypeStruct((8, D), F32)],
        scratch_shapes=[pltpu.VMEM((tm, D), F32)],
        compiler_params=_params(("arbitrary", "arbitrary")),
    )(dg, dup, dh2, h1, rstd2, w_ffn_in, w_ffn_in, ffn_norm_w)
    return dg, dup, dh1, stats


def _mix_bwd(dh1, o, proj_f, h_lru, y_ret, y_lru, w_br, w_bl, w_o, riders=()):
    rows = dh1.shape[0]
    tm = TM_MIX_BWD
    nt = rows // tm

    def body(dh1_ref, o_ref, gret_ref, lgate_ref, ga_ref, gb_ref, hl_ref, yret_ref, ylru_ref, wbr_ref, wbl_ref, wo_ref,
             dproj_ref, do_ref, dhl_ref, mixed_ref, aret_ref, alru_ref, dyret_ref, dylru_ref):
        dmixed = _dot_nt(dh1_ref[...].astype(BF16), wo_ref[...])
        y_ret, y_lru = yret_ref[...], ylru_ref[...]
        sa, sb = _sigmoid(ga_ref[...]), _sigmoid(gb_ref[...])
        mixed_ref[...] = (sa * y_ret + sb * y_lru).astype(BF16)
        dga = dmixed * y_ret * sa * (1.0 - sa)
        dgb = dmixed * y_lru * sb * (1.0 - sb)
        dy_ret = (dmixed * sa).astype(BF16)
        dy_lru = (dmixed * sb).astype(BF16)
        dyret_ref[...] = dy_ret
        dylru_ref[...] = dy_lru
        da_ret = _dot_nt(dy_ret, wbr_ref[...])
        da_lru = _dot_nt(dy_lru, wbl_ref[...])

        gret = gret_ref[...]
        sg = _sigmoid(gret)
        silu = gret * sg
        on, rstds = _group_norm(o_ref[...])
        aret_ref[...] = (silu * on).astype(BF16)
        dgret = da_ret * on * (sg * (1.0 + gret * (1.0 - sg)))
        don = da_ret * silu
        for h in range(HEADS):
            sl = slice(h * DH, (h + 1) * DH)
            onh, donh = on[:, sl], don[:, sl]
            do_ref[:, sl] = rstds[h] * (donh - onh * jnp.mean(donh * onh, axis=-1, keepdims=True))

        gl, gl_grad = _gelu_and_grad(lgate_ref[...])
        hl = hl_ref[...]
        alru_ref[...] = (gl * hl).astype(BF16)
        dlgate = da_lru * hl * gl_grad
        dhl_ref[...] = da_lru * gl

        zeros = jnp.zeros((tm, D), BF16)
        for col in (0, 1, 2, 4):
            dproj_ref[:, col * D:(col + 1) * D] = zeros
        dproj_ref[:, 3 * D:4 * D] = dgret.astype(BF16)
        dproj_ref[:, 5 * D:6 * D] = dlgate.astype(BF16)
        dproj_ref[:, 6 * D:7 * D] = dga.astype(BF16)
        dproj_ref[:, 7 * D:8 * D] = dgb.astype(BF16)

    tile = lambda col: pl.BlockSpec((tm, D), lambda i: (i, col))
    wspec = pl.BlockSpec((D, D), lambda i: (0, 0))
    bf = lambda: jax.ShapeDtypeStruct((rows, D), BF16)
    return _hosted_call(
        body,
        name="mix_bwd",
        grid=(nt,),
        in_specs=[tile(0), tile(0), tile(0), tile(2), tile(3), tile(4), tile(0), tile(0), tile(0), wspec, wspec, wspec],
        out_specs=[pl.BlockSpec((tm, IN_COLS), lambda i: (i, 0))] + [tile(0)] * 7,
        out_shape=[
            jax.ShapeDtypeStruct((rows, IN_COLS), BF16),
            jax.ShapeDtypeStruct((rows, D), F32), jax.ShapeDtypeStruct((rows, D), F32),
            bf(), bf(), bf(), bf(), bf(),
        ],
        scratch_shapes=[],
        args=(dh1, o, proj_f, proj_f, proj_f, proj_f, h_lru, y_ret, y_lru, w_br, w_bl, w_o),
        riders=riders,
    )


def _retention_bwd(dproj, proj_bf, do, sprev, intra, intra_t, q_dec, k_dec, c_dec, cos_t, sin_t, riders=()):
    rows = proj_bf.shape[0]
    nc = rows // CHUNK

    def body(dproj_in_ref, q_ref, k_ref, v_ref, do_ref, sprev_ref, m_ref, mt_ref, qd_ref, kd_ref, cd_ref, cos_ref, sin_ref,
             dproj_ref, ds_sc):
        @pl.when(pl.program_id(0) == 0)
        def _():
            ds_sc[...] = jnp.zeros_like(ds_sc)

        cos, sin = cos_ref[...], sin_ref[...]

        def unrotate(dy):
            return dy * cos - pltpu.roll(dy, DH // 2, axis=1) * sin

        for h in range(HEADS):
            sl = slice(h * DH, (h + 1) * DH)
            q, k, v = q_ref[:, sl], k_ref[:, sl], v_ref[:, sl]
            do = do_ref[:, sl]
            dob = do.astype(BF16)
            doq = (do * qd_ref[:, sl]).astype(BF16)
            state_prev = sprev_ref[0, h]
            dstate = ds_sc[h]
            dstate_b = dstate.astype(BF16)
            s_t = (_dot_nt(k, q) * mt_ref[h]).astype(BF16)
            ds_t = (_dot_nt(v, dob) * mt_ref[h]).astype(BF16)
            ds = (_dot_nt(dob, v) * m_ref[h]).astype(BF16)
            kd = kd_ref[:, sl]
            dq = _dot(ds, k) + _dot_nt(doq, state_prev)
            dk = _dot(ds_t, q) + _dot_nt(v, dstate_b) * kd
            k_scaled = (k.astype(F32) * kd).astype(BF16)
            dv = _dot(s_t, dob) + _dot(k_scaled, dstate_b)
            ds_sc[h] = dstate * cd_ref[:, sl] + _dot_tn(q, doq)
            dproj_ref[:, sl] = unrotate(dq).astype(BF16)
            dproj_ref[:, D + h * DH:D + (h + 1) * DH] = (unrotate(dk) * QK_SCALE).astype(BF16)
            dproj_ref[:, 2 * D + h * DH:2 * D + (h + 1) * DH] = dv.astype(BF16)

    rev = lambda c: nc - 1 - c
    chunk_spec = lambda col: pl.BlockSpec((CHUNK, D), lambda c: (rev(c), col))
    const2 = lambda shape: pl.BlockSpec(shape, lambda c: (0, 0))
    const3 = pl.BlockSpec((HEADS, CHUNK, CHUNK), lambda c: (0, 0, 0))
    return _hosted_call(
        body,
        name="retention_bwd",
        grid=(nc,),
        in_specs=[
            pl.BlockSpec(memory_space=pl.ANY),
            chunk_spec(0), chunk_spec(1), chunk_spec(2), chunk_spec(0),
            pl.BlockSpec((1, HEADS, DH, DH), lambda c: (rev(c), 0, 0, 0)),
            const3, const3,
            const2((CHUNK, D)), const2((CHUNK, D)), const2((1, D)),
            pl.BlockSpec((CHUNK, DH), lambda c: (rev(c), 0)),
            pl.BlockSpec((CHUNK, DH), lambda c: (rev(c), 0)),
        ],
        out_specs=[pl.BlockSpec((CHUNK, 3 * D), lambda c: (rev(c), 0))],
        out_shape=[jax.ShapeDtypeStruct(dproj.shape, BF16)],
        aliases={0: 0},
        scratch_shapes=[pltpu.VMEM((HEADS, DH, DH), F32)],
        args=(dproj, proj_bf, proj_bf, proj_bf, do, sprev, intra, intra_t, q_dec, k_dec, c_dec, cos_t, sin_t),
        riders=riders,
    )


def _lru_bwd(dproj, proj_f, h_lru, dhl, conv_w, conv_b, wa, wx, ba, bx, lam, riders=()):
    rows = proj_f.shape[0]
    nt = rows // TM
    per8 = TM // 8

    def body(dproj_in_ref, x_ref, xprev_ref, h_ref, hprev_ref, dhl_ref, cw_ref, cb_ref, wa_ref, wx_ref, ba_ref, bx_ref, lam_ref,
             dproj_ref, dwa_ref, dwx_ref, stats_ref, anext_sc, dhnext_sc, dcnext_sc, c_sc, b_sc, dh_sc):
        step = pl.program_id(0)
        i = nt - 1 - step

        @pl.when(step == 0)
        def _():
            anext_sc[...] = jnp.zeros_like(anext_sc)
            dhnext_sc[...] = jnp.zeros_like(dhnext_sc)
            dcnext_sc[...] = jnp.zeros_like(dcnext_sc)
            dwa_ref[...] = jnp.zeros_like(dwa_ref)
            dwx_ref[...] = jnp.zeros_like(dwx_ref)
            stats_ref[...] = jnp.zeros_like(stats_ref)

        first = i == 0
        x = x_ref[...]
        prev8 = jnp.where(first, 0.0, xprev_ref[...])
        c, r, gate_i, a, mult, sp, shifted = _conv_and_gates(x, prev8, cw_ref, cb_ref, wa_ref, wx_ref, ba_ref, bx_ref, lam_ref, c_sc)

        b_sc[...] = pltpu.roll(a, TM - 1, axis=0)
        b_sc[TM - 1:TM, :] = anext_sc[0:1, :]
        anext_sc[0:1, :] = a[0:1, :]
        row8 = lax.broadcasted_iota(jnp.int32, (8, D), 0)

        def group(gi, dhnext):
            r0 = pl.multiple_of((per8 - 1 - gi) * 8, 8)
            bb = b_sc[pl.ds(r0, 8), :]
            uu = dhl_ref[pl.ds(r0, 8), :]
            for d in (1, 2, 4):
                b_sh = jnp.where(row8 < 8 - d, pltpu.roll(bb, 8 - d, axis=0), 1.0)
                u_sh = jnp.where(row8 < 8 - d, pltpu.roll(uu, 8 - d, axis=0), 0.0)
                uu = uu + bb * u_sh
                bb = bb * b_sh
            dhb = bb * dhnext + uu
            dh_sc[pl.ds(r0, 8), :] = dhb
            return dhb[0:1, :]

        dhfirst = lax.fori_loop(0, per8, group, dhnext_sc[0:1, :])
        dhnext_sc[0:1, :] = dhfirst
        dh = dh_sc[...]

        h = h_ref[...]
        hprev8 = jnp.where(first, 0.0, hprev_ref[...])
        h_dn, h_head = _shift_down(h, hprev8, 1)
        c_sc[...] = h_dn
        c_sc[0:8, :] = h_head
        h_before = c_sc[...]

        valid = _row_ids(i, TM, (TM, D)) >= PAD_ROWS
        da = dh * h_before
        du = jnp.where(valid, dh, 0.0)
        dmult = du * gate_i * c
        dgate_i = du * mult * c
        dc = du * mult * gate_i
        dla = da * a - dmult * (a * a) / mult
        dla = jnp.where(valid, dla, 0.0)
        dr = dla * ((-LRU_C) * sp)
        dzr = dr * r * (1.0 - r)
        dzi = dgate_i * gate_i * (1.0 - gate_i)
        stats_ref[1:2, :] += jnp.sum(dzr, axis=0, keepdims=True)
        stats_ref[2:3, :] += jnp.sum(dzi, axis=0, keepdims=True)
        stats_ref[3:4, :] += jnp.sum(dla * ((-LRU_C) * r), axis=0, keepdims=True)
        dc_gate = []
        for g in range(LRU_BLOCKS):
            sl = slice(g * LRU_BLOCK, (g + 1) * LRU_BLOCK)
            cg = c[:, sl].astype(BF16)
            dzr_g = dzr[:, sl].astype(BF16)
            dzi_g = dzi[:, sl].astype(BF16)
            dc_gate.append(_dot_nt(dzr_g, wa_ref[g]) + _dot_nt(dzi_g, wx_ref[g]))
            dwa_ref[g] += _dot_tn(cg, dzr_g)
            dwx_ref[g] += _dot_tn(cg, dzi_g)
        dc = dc + jnp.concatenate(dc_gate, axis=1)

        cw = cw_ref[...]
        stats_ref[0:1, :] += jnp.sum(dc, axis=0, keepdims=True)
        stats_ref[7:8, :] += jnp.sum(dc * x, axis=0, keepdims=True)
        dx = cw[3:4] * dc
        tail_src = jnp.concatenate([dc[TM - 8:TM], dcnext_sc[...]], axis=0)
        dx_tail = cw[3:4] * dc[TM - 8:TM]
        for d in (1, 2, 3):
            dx = dx + cw[3 - d:4 - d] * pltpu.roll(dc, TM - d, axis=0)
            dx_tail = dx_tail + cw[3 - d:4 - d] * pltpu.roll(tail_src, 16 - d, axis=0)[0:8]
            rolled, hd = shifted[d - 1]
            b_sc[...] = rolled
            b_sc[0:8, :] = hd
            stats_ref[7 - d:8 - d, :] += jnp.sum(dc * b_sc[...], axis=0, keepdims=True)
        dcnext_sc[...] = dc[0:8]
        dproj_ref[...] = dx.astype(BF16)
        dproj_ref[TM - 8:TM, :] = dx_tail.astype(BF16)

    rev = lambda s: nt - 1 - s
    vec = pl.BlockSpec((1, D), lambda s: (0, 0))
    wspec = pl.BlockSpec((LRU_BLOCKS, LRU_BLOCK, LRU_BLOCK), lambda s: (0, 0, 0))
    prev8 = lambda col: pl.BlockSpec((8, D), lambda s: (jnp.maximum(rev(s) * per8 - 1, 0), col))
    return _hosted_call(
        body,
        name="lru_bwd",
        grid=(nt,),
        in_specs=[
            pl.BlockSpec(memory_space=pl.ANY),
            pl.BlockSpec((TM, D), lambda s: (rev(s), 1)), prev8(1),
            pl.BlockSpec((TM, D), lambda s: (rev(s), 0)), prev8(0),
            pl.BlockSpec((TM, D), lambda s: (rev(s), 0)),
            pl.BlockSpec((4, D), lambda s: (0, 0)),
            vec, wspec, wspec, vec, vec, vec,
        ],
        out_specs=[
            pl.BlockSpec((TM, D), lambda s: (rev(s), 4)),
            wspec, wspec,
            pl.BlockSpec((8, D), lambda s: (0, 0)),
        ],
        out_shape=[
            jax.ShapeDtypeStruct(dproj.shape, BF16),
            jax.ShapeDtypeStruct((LRU_BLOCKS, LRU_BLOCK, LRU_BLOCK), F32),
            jax.ShapeDtypeStruct((LRU_BLOCKS, LRU_BLOCK, LRU_BLOCK), F32),
            jax.ShapeDtypeStruct((8, D), F32),
        ],
        aliases={0: 0},
        scratch_shapes=[
            pltpu.VMEM((8, D), F32), pltpu.VMEM((8, D), F32), pltpu.VMEM((8, D), F32),
            pltpu.VMEM((TM, D), F32), pltpu.VMEM((TM, D), F32), pltpu.VMEM((TM, D), F32),
        ],
        args=(dproj, proj_f, proj_f, h_lru, h_lru, dhl, conv_w, conv_b, wa, wx, ba, bx, lam),
        riders=riders,
    )


def _in_proj_bwd(dproj, w_in, part, prev=None, riders=()):
    rows = dproj.shape[0]
    tm = _heavy_tile(rows)
    nt = rows // tm
    first = 0 if part == 0 else (nt + 1) // 2
    count = (nt + 1) // 2 if part == 0 else nt - first

    def body(*refs):
        dproj_ref, w_ref, du_ref = refs[-3:]
        part = _dot_nt(dproj_ref[...], w_ref[...])

        @pl.when(pl.program_id(1) == 0)
        def _():
            du_ref[...] = part

        @pl.when(pl.program_id(1) > 0)
        def _():
            du_ref[...] += part

    in_specs = [pl.BlockSpec((tm, D), lambda i, g: (first + i, g)), pl.BlockSpec((D, D), lambda i, g: (0, g))]
    args = (dproj, w_in)
    if prev is not None:
        in_specs = [_ANY] + in_specs
        args = (prev,) + args
    return _hosted_call(
        body,
        name="in_proj_bwd_%d" % part,
        grid=(count, 8),
        in_specs=in_specs,
        out_specs=[pl.BlockSpec((tm, D), lambda i, g: (first + i, 0))],
        out_shape=[jax.ShapeDtypeStruct((rows, D), F32)],
        scratch_shapes=[],
        aliases={0: 0} if prev is not None else None,
        args=args,
        riders=riders,
    )


def _norm1_bwd(du, dh1, head, x2d, rstd1, norm_w, riders=()):
    rows = du.shape[0]

    def body(du_ref, dh1_ref, head_ref, x_ref, rstd_ref, nw_ref, gx_ref, ghead_ref, stats_ref):
        i = pl.program_id(0)

        @pl.when(i == 0)
        def _():
            stats_ref[...] = jnp.zeros_like(stats_ref)

        def finish(h0, out_ref):
            du = du_ref[...]
            rs = rstd_ref[...]
            n = h0 * rs
            stats_ref[0:1, :] += jnp.sum(du * n, axis=0, keepdims=True)
            dn = du * nw_ref[...]
            out_ref[...] = dh1_ref[...] + rs * (dn - n * jnp.mean(dn * n, axis=-1, keepdims=True))

        @pl.when(i == 0)
        def _():
            finish(head_ref[...], ghead_ref)

        @pl.when(i > 0)
        def _():
            finish(x_ref[...], gx_ref)

    tile = pl.BlockSpec((TM, D), lambda i: (i, 0))
    return _hosted_call(
        body,
        name="norm1_bwd",
        grid=(rows // TM,),
        in_specs=[
            tile, tile,
            pl.BlockSpec((FRONT, D), lambda i: (0, 0)),
            pl.BlockSpec((TM, D), lambda i: (jnp.maximum(i - 1, 0), 0)),
            pl.BlockSpec((TM, 1), lambda i: (i, 0)),
            pl.BlockSpec((1, D), lambda i: (0, 0)),
        ],
        out_specs=[
            pl.BlockSpec((TM, D), lambda i: (jnp.maximum(i - 1, 0), 0)),
            pl.BlockSpec((FRONT, D), lambda i: (0, 0)),
            pl.BlockSpec((8, D), lambda i: (0, 0)),
        ],
        out_shape=[
            jax.ShapeDtypeStruct(x2d.shape, F32),
            jax.ShapeDtypeStruct((FRONT, D), F32),
            jax.ShapeDtypeStruct((8, D), F32),
        ],
        scratch_shapes=[],
        args=(du, dh1, head, x2d, rstd1, norm_w),
        riders=riders,
    )


def _matmul_tn(name, x, dy, out_cols, col0=0, prev=None, k_block=None, n_block=None, riders=()):
    rows, kdim = x.shape
    ndim = dy.shape[1]
    tr = next(t for t in (1408, TM_HEAVY, TM) if rows % t == 0)
    kb = k_block or kdim
    nb = n_block or ndim
    nr, nk, nn = rows // tr, kdim // kb, ndim // nb
    cb0 = col0 // nb

    def body(*refs):
        x_ref, dy_ref, out_ref = refs[-3], refs[-2], refs[-1]
        part = _dot_tn(x_ref[...].astype(BF16), dy_ref[...].astype(BF16))

        @pl.when(pl.program_id(2) == 0)
        def _():
            out_ref[...] = part

        @pl.when(pl.program_id(2) > 0)
        def _():
            out_ref[...] += part

    in_specs = [
        pl.BlockSpec((tr, kb), lambda n, k, r: (r, k)),
        pl.BlockSpec((tr, nb), lambda n, k, r: (r, n)),
    ]
    args = [x, dy]
    aliases = {}
    if prev is not None:
        in_specs = [pl.BlockSpec(memory_space=pl.ANY)] + in_specs
        args = [prev] + args
        aliases = {0: 0}
    (out,), rider_outs = _hosted_call(
        body,
        name=name,
        grid=(nn, nk, nr),
        in_specs=in_specs,
        out_specs=[pl.BlockSpec((kb, nb), lambda n, k, r: (k, cb0 + n))],
        out_shape=[jax.ShapeDtypeStruct((kdim, out_cols), F32)],
        scratch_shapes=[],
        aliases=aliases,
        args=args,
        riders=riders,
    )
    return (out, rider_outs) if riders else out


class _NoComm:
    def riders(self, host, w, grads):
        return ()

    def after(self, host, rider_outs, w, grads):
        pass


def _local_step(x2d, target, w, plan):
    rows = FRONT + x2d.shape[0]
    head = jnp.concatenate([jnp.zeros((PAD_ROWS, D), F32), w["meta_tokens"]], axis=0)
    cos_t, sin_t = _rope_tables(rows)
    intra, intra_t, q_dec, k_dec, c_dec = _decay_tables()
    grads = {}

    def hosted(host, fn, *args, **kwargs):
        outs, rider_outs = fn(*args, riders=plan.riders(host, w, grads), **kwargs)
        plan.after(host, rider_outs, w, grads)
        return outs

    u1, rstd1 = hosted("norm1", _norm1, head, x2d, w["mix_norm_w"])
    proj_bf, proj_f = hosted("in_proj", _in_proj, u1, w["w_in"], cos_t, sin_t)
    o, sprev = hosted("retention_fwd", _retention_fwd, proj_bf, intra, q_dec, k_dec, c_dec)
    lru_args = (w["conv_w"], w["conv_b"], w["lru_wa"], w["lru_wx"], w["lru_ba"], w["lru_bx"], w["lru_lambda"])
    h_lru = _lru_fwd(proj_f, *lru_args)
    y_ret, y_lru, h1, u2, rstd2 = _mix_fwd(head, x2d, o, proj_f, h_lru, w["w_branch_ret"], w["w_branch_lru"], w["w_out"],
                                           w["ffn_norm_w"])
    g, up, act, h2 = _ffn_fwd(u2, h1, w["w_ffn_in"], w["w_ffn_out"])
    dh2, stats_loss = _final_loss(h2, target, w["final_norm_w"])

    dg, dup, dh1, stats_ffn = _ffn_bwd(dh2, g, up, h1, rstd2, w["w_ffn_in"], w["w_ffn_out"], w["ffn_norm_w"])
    grads["w_ffn_in"] = _matmul_tn("dw_ffn_up", u2, dup, 2 * FFN, col0=FFN, n_block=FFN_HALF,
                                   prev=_matmul_tn("dw_ffn_gate", u2, dg, 2 * FFN, n_block=FFN_HALF))
    grads["w_ffn_out"] = _matmul_tn("dw_ffn_out", act, dh2, D, k_block=FFN_HALF)
    (dproj, do, dhl, mixed, a_ret, a_lru, dy_ret, dy_lru) = hosted(
        "mix_bwd", _mix_bwd, dh1, o, proj_f, h_lru, y_ret, y_lru, w["w_branch_ret"], w["w_branch_lru"], w["w_out"])
    grads["w_out"] = _matmul_tn("dw_out", mixed, dh1, D)
    grads["w_branch_ret"] = _matmul_tn("dw_branch_ret", a_ret, dy_ret, D)
    grads["w_branch_lru"] = _matmul_tn("dw_branch_lru", a_lru, dy_lru, D)
    (dproj,) = hosted("retention_bwd", _retention_bwd, dproj, proj_bf, do, sprev, intra, intra_t, q_dec, k_dec, c_dec,
                      cos_t, sin_t)
    dproj, grads["lru_wa"], grads["lru_wx"], stats_lru = hosted("lru_bwd", _lru_bwd, dproj, proj_f, h_lru, dhl, *lru_args)
    dw_in_riders = plan.riders("dw_in", w, grads)
    if dw_in_riders:
        grads["w_in"], rider_outs = _matmul_tn("dw_in", u1, dproj, IN_COLS, n_block=D, riders=dw_in_riders)
        plan.after("dw_in", rider_outs, w, grads)
    else:
        grads["w_in"] = _matmul_tn("dw_in", u1, dproj, IN_COLS, n_block=D)
    (du1,) = hosted("in_proj_bwd_0", _in_proj_bwd, dproj, w["w_in"], 0)
    (du1,) = hosted("in_proj_bwd_1", _in_proj_bwd, dproj, w["w_in"], 1, du1)
    grad_x, grad_head, stats_in = hosted("norm1_bwd", _norm1_bwd, du1, dh1, head, x2d, rstd1, w["mix_norm_w"])
    return grad_x, grad_head, grads, [stats_loss, stats_ffn, stats_in, stats_lru]


BIG_PIECES = {
    "w_in": ("col", (D, 2 * D)),
    "w_ffn_in": ("col", (D, FFN_HALF)),
    "w_ffn_out": ("row", (FFN // 4, D)),
    "w_branch_ret": ("row", (D // 4, D)),
    "w_branch_lru": ("row", (D // 4, D)),
    "w_out": ("row", (D // 4, D)),
    "lru_wa": ("lru", (LRU_BLOCKS, LRU_BLOCK // 4, LRU_BLOCK)),
    "lru_wx": ("lru", (LRU_BLOCKS, LRU_BLOCK // 4, LRU_BLOCK)),
}
SMALL_PIECES = {"meta_tokens": ("col", (N_META, D // 4)), "conv_w": ("col", (4, D // 4))}


def _full_shape(kind, shard):
    if kind == "col":
        return (shard[0], 4 * shard[1])
    if kind == "row":
        return (4 * shard[0], shard[1])
    return (shard[0], 4 * shard[1], shard[2])


def _half_shape(kind, shard):
    return (shard[0] // 2,) + tuple(shard[1:])


def _aligned(start, multiple):
    return start if isinstance(start, int) else pl.multiple_of(start, multiple)


def _lead(h, size):
    if h is None:
        return pl.ds(0, size)
    return pl.ds(_aligned(h * (size // 2), size // 2), size // 2)


def _full_region(ref, kind, shard, s, h):
    if kind == "col":
        return ref.at[_lead(h, shard[0]), pl.ds(_aligned(s * shard[1], shard[1]), shard[1])]
    if kind == "row":
        size = shard[0] if h is None else shard[0] // 2
        start = s * shard[0] + (0 if h is None else h * (shard[0] // 2))
        return ref.at[pl.ds(_aligned(start, 16), size), :]
    return ref.at[_lead(h, shard[0]), pl.ds(_aligned(s * shard[1], shard[1]), shard[1]), :]


def _shard_region(ref, shard, h):
    return ref.at[_lead(h, shard[0])]


def _place():
    x, y, c = lax.axis_index("x"), lax.axis_index("y"), lax.axis_index("c")
    return x, y, c, 2 * x + y


def _other_chip(s, c, k):
    s2 = jnp.bitwise_xor(s, k)
    return s2, (s2 // 2, s2 % 2, c)


def _remote(src, dst, send_sem, recv_sem, dev):
    return pltpu.make_async_remote_copy(src_ref=src, dst_ref=dst, send_sem=send_sem, recv_sem=recv_sem,
                                        device_id=dev, device_id_type=MESH)


_ANY = pl.BlockSpec(memory_space=pl.ANY)


class _Rider:
    def __init__(self, ins, out_shapes, sem_shapes, build, aliased=False):
        self.ins, self.out_shapes, self.sem_shapes, self.build, self.aliased = ins, out_shapes, sem_shapes, build, aliased


def _hosted_call(body, *, name, grid, in_specs, out_specs, out_shape, scratch_shapes, args, riders=(), aliases=None):
    n_in, n_out, n_sc = len(in_specs), len(out_shape), len(scratch_shapes)
    r_in = [a for r in riders for a in r.ins]
    r_out = [s for r in riders for s in r.out_shapes]
    r_sem = [s for r in riders for s in r.sem_shapes]

    def full_body(*refs):
        ins, rin = refs[:n_in], refs[n_in:n_in + len(r_in)]
        o0 = n_in + len(r_in)
        outs, rout = refs[o0:o0 + n_out], refs[o0 + n_out:o0 + n_out + len(r_out)]
        s0 = o0 + n_out + len(r_out)
        scratch, rsem = refs[s0:s0 + n_sc], refs[s0 + n_sc:]
        starts, waits = [], []
        pi = po = ps = 0
        for r in riders:
            st, wt = r.build(rin[pi:pi + len(r.ins)], rout[po:po + len(r.out_shapes)], rsem[ps:ps + len(r.sem_shapes)])
            starts += st
            waits += wt
            pi, po, ps = pi + len(r.ins), po + len(r.out_shapes), ps + len(r.sem_shapes)
        if riders:
            first = functools.reduce(jnp.logical_and, [pl.program_id(d) == 0 for d in range(len(grid))])
            last = functools.reduce(jnp.logical_and, [pl.program_id(d) == grid[d] - 1 for d in range(len(grid))])

            @pl.when(first)
            def _():
                for cp in starts:
                    cp.start()

        body(*ins, *outs, *scratch)
        if riders:
            @pl.when(last)
            def _():
                for wait in waits:
                    wait()

    io_aliases = dict(aliases or {})
    pi = po = 0
    for r in riders:
        if r.aliased:
            for q in range(len(r.ins)):
                io_aliases[n_in + pi + q] = n_out + po + q
        pi, po = pi + len(r.ins), po + len(r.out_shapes)
    res = pl.pallas_call(
        full_body,
        name=name,
        grid=grid,
        in_specs=list(in_specs) + [_ANY] * len(r_in),
        out_specs=list(out_specs) + [_ANY] * len(r_out),
        out_shape=list(out_shape) + r_out,
        scratch_shapes=list(scratch_shapes) + r_sem,
        input_output_aliases=io_aliases,
        compiler_params=_params(("arbitrary",) * len(grid)),
    )(*args, *r_in)
    rider_outs, po = [], n_out
    for r in riders:
        rider_outs.append(list(res[po:po + len(r.out_shapes)]))
        po += len(r.out_shapes)
    return list(res[:n_out]), rider_outs


def _run_riders(name, riders):
    r_in = [a for r in riders for a in r.ins]
    r_out = [s for r in riders for s in r.out_shapes]
    r_sem = [s for r in riders for s in r.sem_shapes]

    def body(*refs):
        rin, rout, rsem = refs[:len(r_in)], refs[len(r_in):len(r_in) + len(r_out)], refs[len(r_in) + len(r_out):]
        pi = po = ps = 0
        for r in riders:
            starts, waits = r.build(rin[pi:pi + len(r.ins)], rout[po:po + len(r.out_shapes)], rsem[ps:ps + len(r.sem_shapes)])
            for cp in starts:
                cp.start()
            for wait in waits:
                wait()
            pi, po, ps = pi + len(r.ins), po + len(r.out_shapes), ps + len(r.sem_shapes)

    io_aliases = {}
    pi = po = 0
    for r in riders:
        if r.aliased:
            for q in range(len(r.ins)):
                io_aliases[pi + q] = po + q
        pi, po = pi + len(r.ins), po + len(r.out_shapes)
    res = pl.pallas_call(
        body,
        name=name,
        in_specs=[_ANY] * len(r_in),
        out_specs=[_ANY] * len(r_out),
        out_shape=r_out,
        scratch_shapes=r_sem,
        input_output_aliases=io_aliases,
    )(*r_in)
    outs, po = [], 0
    for r in riders:
        outs.append(list(res[po:po + len(r.out_shapes)]))
        po += len(r.out_shapes)
    return outs


def _piece(name):
    if name in BIG_PIECES:
        return (name, *BIG_PIECES[name], True)
    return (name, *SMALL_PIECES[name], False)


def _gather_rider(shards, names):
    pieces = [_piece(n) for n in names]
    n = len(pieces)

    def build(ins, outs, sems):
        local_sem, ici_send, ici_recv = sems
        _, _, c, s = _place()
        starts, waits = [], []
        for p, (_, kind, shard, split) in enumerate(pieces):
            cp = pltpu.make_async_copy(ins[p], _full_region(outs[p], kind, shard, s, None), local_sem.at[p])
            starts.append(cp)
            waits.append(cp.wait)
            h = c if split else None
            for k in (1, 2, 3):
                s2, dev = _other_chip(s, c, k)
                cp = _remote(_shard_region(ins[p], shard, h), _full_region(outs[p], kind, shard, s, h),
                             ici_send.at[p, k - 1], ici_recv.at[p, k - 1], dev)
                starts.append(cp)
                waits.append(cp.wait_send)
                region = _full_region(outs[p], kind, shard, s2, h)
                waits.append(_remote(region, region, ici_send.at[p, k - 1], ici_recv.at[p, k - 1], dev).wait_recv)
        return starts, waits

    return _Rider(
        [shards[name] for name in names],
        [jax.ShapeDtypeStruct(_full_shape(kind, shard), shards[name].dtype) for name, kind, shard, _ in pieces],
        [pltpu.SemaphoreType.DMA((n,)), pltpu.SemaphoreType.DMA((n, 3)), pltpu.SemaphoreType.DMA((n, 3))],
        build)


def _forward_rider(gathered, names):
    pieces = [_piece(n) for n in names]
    n = len(pieces)

    def build(ins, outs, sems):
        fwd_send, fwd_recv = sems
        x, y, c, s = _place()
        sibling = (x, y, 1 - c)
        starts, waits = [], []
        for p, (_, kind, shard, _) in enumerate(pieces):
            for k in (1, 2, 3):
                s2, _ = _other_chip(s, c, k)
                mine = _full_region(outs[p], kind, shard, s2, c)
                theirs = _full_region(outs[p], kind, shard, s2, 1 - c)
                cp = _remote(mine, mine, fwd_send.at[p, k - 1], fwd_recv.at[p, k - 1], sibling)
                starts.append(cp)
                waits.append(cp.wait_send)
                waits.append(_remote(theirs, theirs, fwd_send.at[p, k - 1], fwd_recv.at[p, k - 1], sibling).wait_recv)
        return starts, waits

    return _Rider(
        [gathered[name] for name in names],
        [jax.ShapeDtypeStruct(gathered[name].shape, gathered[name].dtype) for name in names],
        [pltpu.SemaphoreType.DMA((n, 3)), pltpu.SemaphoreType.DMA((n, 3))],
        build, aliased=True)


def _pair_exchange_rider(grads, names):
    n = len(names)

    def build(ins, outs, sems):
        send_sem, recv_sem = sems
        x, y, c, _ = _place()
        sibling = (x, y, 1 - c)
        starts, waits = [], []
        for p, name in enumerate(names):
            kind, shard = BIG_PIECES[name]
            for s2 in range(4):
                cp = _remote(_full_region(ins[p], kind, shard, s2, 1 - c), outs[p].at[s2], send_sem.at[p, s2],
                             recv_sem.at[p, s2], sibling)
                starts.append(cp)
                waits.append(cp.wait_send)
                waits.append(_remote(outs[p].at[s2], outs[p].at[s2], send_sem.at[p, s2], recv_sem.at[p, s2], sibling).wait_recv)
        return starts, waits

    return _Rider(
        [grads[name] for name in names],
        [jax.ShapeDtypeStruct((4,) + _half_shape(*BIG_PIECES[name]), F32) for name in names],
        [pltpu.SemaphoreType.DMA((n, 4))] * 2,
        build)


def _half_specs(kind, shard):
    half = shard[0] // 2
    if kind == "col":
        full = pl.BlockSpec((half, shard[1]), lambda j, pr: (pr[1], j))
        buf = pl.BlockSpec((None, half, shard[1]), lambda j, pr: (j, 0, 0))
    elif kind == "row":
        full = pl.BlockSpec((half, shard[1]), lambda j, pr: (2 * j + pr[1], 0))
        buf = pl.BlockSpec((None, half, shard[1]), lambda j, pr: (j, 0, 0))
    else:
        full = pl.BlockSpec((half, shard[1], shard[2]), lambda j, pr: (pr[1], j, 0))
        buf = pl.BlockSpec((None, half, shard[1], shard[2]), lambda j, pr: (j, 0, 0, 0))
    return full, buf


def _pair_sum(name, grad, recv, place):
    kind, shard = BIG_PIECES[name]
    full, buf = _half_specs(kind, shard)

    def body(pr, g_ref, r_ref, o_ref):
        o_ref[...] = (g_ref[...] + r_ref[...]).astype(BF16)

    return pl.pallas_call(
        body,
        name="pair_sum_" + name,
        grid_spec=pltpu.PrefetchScalarGridSpec(num_scalar_prefetch=1, grid=(4,), in_specs=[full, buf], out_specs=buf),
        out_shape=jax.ShapeDtypeStruct((4,) + _half_shape(kind, shard), BF16),
        compiler_params=_params(("arbitrary",)),
    )(place, grad, recv)


def _chip_exchange_rider(sums, names):
    n = len(names)

    def build(ins, outs, sems):
        send_sem, recv_sem = sems
        _, _, c, s = _place()
        starts, waits = [], []
        for p in range(n):
            for k in (1, 2, 3):
                s2, dev = _other_chip(s, c, k)
                cp = _remote(ins[p].at[s2], outs[p].at[k - 1], send_sem.at[p, k - 1], recv_sem.at[p, k - 1], dev)
                starts.append(cp)
                waits.append(cp.wait_send)
                waits.append(_remote(outs[p].at[k - 1], outs[p].at[k - 1], send_sem.at[p, k - 1], recv_sem.at[p, k - 1],
                                     dev).wait_recv)
        return starts, waits

    return _Rider(
        [sums[name] for name in names],
        [jax.ShapeDtypeStruct((3,) + _half_shape(*BIG_PIECES[name]), BF16) for name in names],
        [pltpu.SemaphoreType.DMA((n, 3))] * 2,
        build)


def _chip_sum(name, grad, recv_pair, recv_chip, place):
    kind, shard = BIG_PIECES[name]
    half = shard[0] // 2
    tail = tuple(shard[1:])
    zeros = (0,) * len(tail)
    if kind == "col":
        full = pl.BlockSpec((half,) + tail, lambda j, pr: (pr[1], pr[0]))
    elif kind == "row":
        full = pl.BlockSpec((half,) + tail, lambda j, pr: (2 * pr[0] + pr[1], 0))
    else:
        full = pl.BlockSpec((half,) + tail, lambda j, pr: (pr[1], pr[0], 0))
    pair = pl.BlockSpec((None, half) + tail, lambda j, pr: (pr[0], 0) + zeros)
    chip = pl.BlockSpec((3, half) + tail, lambda j, pr: (0, 0) + zeros)
    out = pl.BlockSpec((half,) + tail, lambda j, pr: (pr[1],) + zeros)

    def body(pr, g_ref, rp_ref, rc_ref, o_ref):
        total = g_ref[...] + rp_ref[...]
        for k in range(3):
            total = total + rc_ref[k].astype(F32)
        o_ref[...] = total

    return pl.pallas_call(
        body,
        name="chip_sum_" + name,
        grid_spec=pltpu.PrefetchScalarGridSpec(num_scalar_prefetch=1, grid=(1,), in_specs=[full, pair, chip], out_specs=out),
        out_shape=jax.ShapeDtypeStruct(shard, F32),
        compiler_params=_params(("arbitrary",)),
    )(place, grad, recv_pair, recv_chip)


def _sibling_exchange_rider(halves, names):
    n = len(names)

    def build(ins, outs, sems):
        send_sem, recv_sem = sems
        x, y, c, _ = _place()
        sibling = (x, y, 1 - c)
        starts, waits = [], []
        for p, name in enumerate(names):
            shard = BIG_PIECES[name][1]
            mine = _shard_region(outs[p], shard, c)
            theirs = _shard_region(outs[p], shard, 1 - c)
            cp = _remote(mine, mine, send_sem.at[p], recv_sem.at[p], sibling)
            starts.append(cp)
            waits.append(cp.wait_send)
            waits.append(_remote(theirs, theirs, send_sem.at[p], recv_sem.at[p], sibling).wait_recv)
        return starts, waits

    return _Rider(
        [halves[name] for name in names],
        [jax.ShapeDtypeStruct(BIG_PIECES[name][1], F32) for name in names],
        [pltpu.SemaphoreType.DMA((n,))] * 2,
        build, aliased=True)


FIRST_WEIGHTS = ["w_in", "meta_tokens", "conv_w"]
LATE_WEIGHTS = [n for n in BIG_PIECES if n != "w_in"]
GRAD_GROUPS = {
    "ffn": ["w_ffn_in", "w_ffn_out"],
    "mixer": ["w_out", "w_branch_ret", "w_branch_lru", "lru_wa", "lru_wx"],
    "in": ["w_in"],
}
GRAD_SCHEDULE = {
    "mix_bwd": [("pair", "ffn")],
    "retention_bwd": [("chip", "ffn")],
    "lru_bwd": [("sibling", "ffn")],
    "dw_in": [("pair", "mixer")],
    "in_proj_bwd_0": [("chip", "mixer"), ("pair", "in")],
    "in_proj_bwd_1": [("sibling", "mixer"), ("chip", "in")],
}


class _CommPlan:
    def __init__(self, shards, place):
        self.shards, self.place = shards, place
        self.late = None
        self.recv_pair, self.sums, self.recv_chip, self.halves, self.final = {}, {}, {}, {}, {}

    def _grad_rider(self, stage, group, grads):
        names = GRAD_GROUPS[group]
        if stage == "pair":
            return _pair_exchange_rider(grads, names)
        if stage == "chip":
            return _chip_exchange_rider(self.sums, names)
        return _sibling_exchange_rider(self.halves, names)

    def _grad_after(self, stage, group, outs, grads):
        names = GRAD_GROUPS[group]
        if stage == "pair":
            for n, o in zip(names, outs):
                self.recv_pair[n] = o
                self.sums[n] = _pair_sum(n, grads[n], o, self.place)
        elif stage == "chip":
            for n, o in zip(names, outs):
                self.halves[n] = _chip_sum(n, grads[n], self.recv_pair[n], o, self.place)
        else:
            self.final.update(zip(names, outs))

    def riders(self, host, w, grads):
        if host == "norm1":
            return [_forward_rider(w, ["w_in"])]
        if host == "in_proj":
            return [_gather_rider(self.shards, LATE_WEIGHTS)]
        if host == "retention_fwd":
            return [_forward_rider(self.late, LATE_WEIGHTS)]
        return [self._grad_rider(stage, group, grads) for stage, group in GRAD_SCHEDULE.get(host, [])]

    def after(self, host, rider_outs, w, grads):
        if host == "norm1":
            w["w_in"] = rider_outs[0][0]
        elif host == "in_proj":
            self.late = dict(zip(LATE_WEIGHTS, rider_outs[0]))
        elif host == "retention_fwd":
            w.update(zip(LATE_WEIGHTS, rider_outs[0]))
        else:
            for (stage, group), outs in zip(GRAD_SCHEDULE.get(host, []), rider_outs):
                self._grad_after(stage, group, outs, grads)

    def finish(self):
        (outs,) = _run_riders("grad_tail_exchange", [_sibling_exchange_rider(self.halves, GRAD_GROUPS["in"])])
        self.final.update(zip(GRAD_GROUPS["in"], outs))
        return self.final


def _adamw_math(w, g, m, v):
    m = ADAM_B1 * m + (1.0 - ADAM_B1) * g
    v = ADAM_B2 * v + (1.0 - ADAM_B2) * (g * g)
    m_hat = m / (1.0 - ADAM_B1 ** ADAM_STEP)
    v_hat = v / (1.0 - ADAM_B2 ** ADAM_STEP)
    delta = -ADAM_LR * (m_hat / (jnp.sqrt(v_hat) + ADAM_EPS) + ADAM_WD * w)
    return delta, m, v


def _adamw(name, g, w, m, v):
    rows, cols = g.shape
    tr = rows // 4 if rows % 32 == 0 else rows

    def body(g_ref, w_ref, m_ref, v_ref, go_ref, d_ref, mo_ref, vo_ref):
        gv = g_ref[...]
        delta, m2, v2 = _adamw_math(w_ref[...], gv, m_ref[...], v_ref[...])
        go_ref[...] = gv
        d_ref[...] = delta
        mo_ref[...] = m2
        vo_ref[...] = v2

    spec = pl.BlockSpec((tr, cols), lambda i: (i, 0))
    return pl.pallas_call(
        body,
        name="adamw_" + name,
        grid=(rows // tr,),
        in_specs=[spec] * 4,
        out_specs=[spec] * 4,
        out_shape=[jax.ShapeDtypeStruct((rows, cols), F32)] * 4,
        compiler_params=_params(("arbitrary",)),
    )(g, w, m, v)


SMALL_ROWS = 48
VEC_ROWS = {"final_norm_w": 1, "ffn_norm_w": 8, "mix_norm_w": 16, "conv_b": 24, "lru_ba": 25, "lru_bx": 26, "lru_lambda": 27}
VEC_NAMES = list(VEC_ROWS)
CONV_W_ROW = 28
META_ROW = 32


def _small_all_reduce(partial):
    def body(in_ref, out_ref, buf, local_sem, send_sem, recv_sem):
        x, y, c, s = _place()
        me = 2 * s + c
        cp = pltpu.make_async_copy(in_ref, buf.at[me], local_sem)
        cp.start()
        started = []
        for k in range(1, 8):
            peer = jnp.bitwise_xor(me, k)
            dev = (peer // 4, (peer // 2) % 2, peer % 2)
            rd = _remote(in_ref, buf.at[me], send_sem.at[k - 1], recv_sem.at[k - 1], dev)
            rd.start()
            started.append(rd)
        for k in range(1, 8):
            peer = jnp.bitwise_xor(me, k)
            dev = (peer // 4, (peer // 2) % 2, peer % 2)
            _remote(in_ref, buf.at[peer], send_sem.at[k - 1], recv_sem.at[k - 1], dev).wait_recv()
        for rd in started:
            rd.wait_send()
        cp.wait()
        total = buf[0]
        for d in range(1, 8):
            total = total + buf[d]
        out_ref[...] = total

    return pl.pallas_call(
        body,
        name="small_all_reduce",
        in_specs=[pl.BlockSpec(memory_space=pltpu.VMEM)],
        out_specs=pl.BlockSpec(memory_space=pltpu.VMEM),
        out_shape=jax.ShapeDtypeStruct((SMALL_ROWS, D), F32),
        scratch_shapes=[pltpu.VMEM((8, SMALL_ROWS, D), F32), pltpu.SemaphoreType.DMA,
                        pltpu.SemaphoreType.DMA((7,)), pltpu.SemaphoreType.DMA((7,))],
    )(partial)


def _small_update(total, place, vecs, conv, meta):
    nvec = len(VEC_NAMES)
    qcols = D // 4

    def body(pr, tot_ref, col_ref, *refs):
        vec_refs = refs[:3 * nvec]
        conv_refs = refs[3 * nvec:3 * nvec + 3]
        meta_refs = refs[3 * nvec + 3:3 * nvec + 6]
        outs = refs[3 * nvec + 6:]
        loss_ref, vec_out, conv_out, meta_out = outs[0], outs[1:5], outs[5:9], outs[9:13]
        loss_ref[...] = jnp.sum(tot_ref[0:1, :], axis=1, keepdims=True)
        for o in vec_out:
            o[...] = jnp.zeros_like(o)
        for j, name in enumerate(VEC_NAMES):
            w, m, v = (r[...] for r in vec_refs[3 * j:3 * j + 3])
            g = tot_ref[VEC_ROWS[name]:VEC_ROWS[name] + 1, :]
            if name == "lru_lambda":
                g = -g / (1.0 + jnp.exp(w))
            for o, val in zip(vec_out, (g,) + _adamw_math(w, g, m, v)):
                o[j:j + 1, :] = val
        for row, n_rows, ins, group in ((CONV_W_ROW, 4, conv_refs, conv_out), (META_ROW, N_META, meta_refs, meta_out)):
            g = col_ref[row:row + n_rows, :]
            w, m, v = (r[...] for r in ins)
            for o, val in zip(group, (g,) + _adamw_math(w, g, m, v)):
                o[...] = val

    whole = lambda shape: pl.BlockSpec(shape, lambda i, pr: (0,) * len(shape))
    in_specs = [whole((SMALL_ROWS, D)), pl.BlockSpec((SMALL_ROWS, qcols), lambda i, pr: (0, pr[0]))]
    in_specs += [whole((1, D))] * (3 * nvec) + [whole((4, qcols))] * 3 + [whole((N_META, qcols))] * 3
    out_shapes = [(1, 1)] + [(8, D)] * 4 + [(4, qcols)] * 4 + [(N_META, qcols)] * 4
    return pl.pallas_call(
        body,
        name="small_update",
        grid_spec=pltpu.PrefetchScalarGridSpec(num_scalar_prefetch=1, grid=(1,), in_specs=in_specs,
                                               out_specs=[whole(s) for s in out_shapes]),
        out_shape=[jax.ShapeDtypeStruct(s, F32) for s in out_shapes],
        compiler_params=_params(("arbitrary",)),
    )(place, total, total, *[a for t in vecs for a in t], *conv, *meta)


WEIGHT_ORDER = ["meta_tokens", "mix_norm_w", "w_in", "conv_w", "conv_b", "lru_wa", "lru_ba", "lru_wx", "lru_bx", "lru_lambda",
                "w_branch_ret", "w_branch_lru", "w_out", "ffn_norm_w", "w_ffn_in", "w_ffn_out", "final_norm_w"]


def kernel(x, meta_tokens, mix_norm_w, w_in, conv_w, conv_b, lru_wa, lru_ba, lru_wx, lru_bx, lru_lambda, w_branch_ret, w_branch_lru, w_out, ffn_norm_w, w_ffn_in, w_ffn_out, final_norm_w, loss_target, m_meta_tokens, m_mix_norm_w, m_w_in, m_conv_w, m_conv_b, m_lru_wa, m_lru_ba, m_lru_wx, m_lru_bx, m_lru_lambda, m_w_branch_ret, m_w_branch_lru, m_w_out, m_ffn_norm_w, m_w_ffn_in, m_w_ffn_out, m_final_norm_w, v_meta_tokens, v_mix_norm_w, v_w_in, v_conv_w, v_conv_b, v_lru_wa, v_lru_ba, v_lru_wx, v_lru_bx, v_lru_lambda, v_w_branch_ret, v_w_branch_lru, v_w_out, v_ffn_norm_w, v_w_ffn_in, v_w_ffn_out, v_final_norm_w):
    args = locals()
    wts = {n: args[n] for n in WEIGHT_ORDER}
    mom = {n: args["m_" + n] for n in WEIGHT_ORDER}
    var = {n: args["v_" + n] for n in WEIGHT_ORDER}
    place = jnp.stack([2 * lax.axis_index("x") + lax.axis_index("y"), lax.axis_index("c")]).astype(jnp.int32)

    shards = {n: wts[n][0].astype(BF16) for n in BIG_PIECES}
    shards["meta_tokens"] = wts["meta_tokens"]
    shards["conv_w"] = wts["conv_w"][0]
    plan = _CommPlan(shards, place)
    (first,) = _run_riders("gather_first", [_gather_rider(shards, FIRST_WEIGHTS)])
    w = dict(zip(FIRST_WEIGHTS, first))
    for n in VEC_NAMES:
        w[n] = wts[n].reshape(1, D)

    grad_x, grad_head, _, stats = _local_step(x[0], loss_target[0], w, plan)
    shard_grads = plan.finish()

    out = {}
    for n in BIG_PIECES:
        shape2d = (-1, wts[n].shape[-1])
        res = _adamw(n, *[a.reshape(shape2d) for a in (shard_grads[n], wts[n], mom[n], var[n])])
        out[n] = [r.reshape(wts[n].shape) for r in res]

    partial = jnp.concatenate(stats + [grad_head[PAD_ROWS:]], axis=0)
    total = _small_all_reduce(partial)
    vecs = [tuple(a[n].reshape(1, D) for a in (wts, mom, var)) for n in VEC_NAMES]
    conv = tuple(a["conv_w"][0] for a in (wts, mom, var))
    meta = tuple(a["meta_tokens"] for a in (wts, mom, var))
    res = _small_update(total, place, vecs, conv, meta)
    loss = res[0].reshape(())
    for j, n in enumerate(VEC_NAMES):
        out[n] = [r[j].reshape(wts[n].shape) for r in res[1:5]]
    out["conv_w"] = [r.reshape(wts["conv_w"].shape) for r in res[5:9]]
    out["meta_tokens"] = list(res[9:13])

    return (loss, grad_x.reshape(x.shape)) + tuple(out[n][kind] for kind in range(4) for n in WEIGHT_ORDER)
```

```python
import functools
import math

import jax
import jax.numpy as jnp
from jax import lax
from jax.experimental import pallas as pl
from jax.experimental.pallas import tpu as pltpu

F32 = jnp.float32
BF16 = jnp.bfloat16

D = 1024
HEADS = 8
DH = 128
CHUNK = 128
N_META = 16
FRONT = 256
PAD_ROWS = FRONT - N_META
LRU_BLOCKS = 4
LRU_BLOCK = 256
LRU_C = 8.0
FFN = 2816
FFN_HALF = FFN // 2
IN_COLS = 8 * D
ROPE_BASE = 10000.0
EPS = 1e-6
QK_SCALE = DH ** -0.5

ADAM_LR = 0.001
ADAM_B1 = 0.9
ADAM_B2 = 0.999
ADAM_EPS = 1e-08
ADAM_WD = 0.01
ADAM_STEP = 10

TM = 256
TM_HEAVY = 768
FFN_BLOCK = FFN // 2
TM_MIX_BWD = 256
VMEM_LIMIT = 60 * 1024 * 1024

NT_DIMS = (((1,), (1,)), ((), ()))
TN_DIMS = (((0,), (0,)), ((), ()))
MESH = pl.DeviceIdType.MESH


def _params(sem=None):
    if sem is None:
        return pltpu.CompilerParams(vmem_limit_bytes=VMEM_LIMIT)
    return pltpu.CompilerParams(dimension_semantics=sem, vmem_limit_bytes=VMEM_LIMIT)


def _dot(a, b):
    return jnp.dot(a, b, preferred_element_type=F32)


def _dot_nt(a, b):
    return lax.dot_general(a, b, NT_DIMS, preferred_element_type=F32)


def _dot_tn(a, b):
    return lax.dot_general(a, b, TN_DIMS, preferred_element_type=F32)


def _sigmoid(z):
    return 1.0 / (1.0 + jnp.exp(-z))


def _log1p(x):
    return jnp.where(x < 1e-3, x * (1.0 - x * (0.5 - x * (1.0 / 3.0))), jnp.log(1.0 + x))


def _softplus(x):
    return jnp.maximum(x, 0.0) + _log1p(jnp.exp(-jnp.abs(x)))


def _neg_expm1(x):
    series = -x * (1.0 + x * (0.5 + x * (1.0 / 6.0 + x * (1.0 / 24.0 + x * (1.0 / 120.0)))))
    return jnp.where(x > -0.05, series, 1.0 - jnp.exp(x))


_GELU_K = math.sqrt(2.0 / math.pi)


def _gelu_and_grad(x):
    inner = _GELU_K * (x + 0.044715 * x * x * x)
    t = jnp.tanh(inner)
    val = 0.5 * x * (1.0 + t)
    grad = 0.5 * (1.0 + t) + 0.5 * x * (1.0 - t * t) * _GELU_K * (1.0 + 3.0 * 0.044715 * x * x)
    return val, grad


def _row_ids(i, rows, shape):
    return i * rows + lax.broadcasted_iota(jnp.int32, shape, 0)


def _rope_tables(rows):
    pos = jnp.arange(rows, dtype=jnp.int32) - PAD_ROWS
    inv_freq = ROPE_BASE ** (-jnp.arange(0, DH, 2, dtype=F32) / DH)
    ang = pos.astype(F32)[:, None] * inv_freq[None, :]
    cos, sin = jnp.cos(ang), jnp.sin(ang)
    return jnp.concatenate([cos, cos], axis=1), jnp.concatenate([-sin, sin], axis=1)


def _decay_tables():
    log_g = jnp.log(1.0 - 2.0 ** (-5.0 - jnp.arange(HEADS, dtype=F32)))
    idx = jnp.arange(CHUNK, dtype=F32)
    diff = idx[:, None] - idx[None, :]
    intra = jnp.where(diff[None] >= 0, jnp.exp(jnp.maximum(diff, 0.0)[None] * log_g[:, None, None]), 0.0)
    q_decay = jnp.exp((idx + 1.0)[:, None] * log_g[None, :])
    k_decay = jnp.exp((CHUNK - 1.0 - idx)[:, None] * log_g[None, :])
    chunk_decay = jnp.exp(CHUNK * log_g)
    wide = lambda a: jnp.repeat(a, DH, axis=-1)
    return intra, jnp.swapaxes(intra, 1, 2), wide(q_decay), wide(k_decay), wide(chunk_decay[None, :])


def _norm1(head, x2d, norm_w, riders=()):
    rows = FRONT + x2d.shape[0]

    def body(head_ref, x_ref, nw_ref, u_ref, rstd_ref):
        def norm(hv):
            rs = lax.rsqrt(jnp.mean(hv * hv, axis=-1, keepdims=True) + EPS)
            u_ref[...] = ((hv * rs) * nw_ref[...]).astype(BF16)
            rstd_ref[...] = rs

        @pl.when(pl.program_id(0) == 0)
        def _():
            norm(head_ref[...])

        @pl.when(pl.program_id(0) > 0)
        def _():
            norm(x_ref[...])

    return _hosted_call(
        body,
        name="norm1",
        grid=(rows // TM,),
        in_specs=[
            pl.BlockSpec((FRONT, D), lambda i: (0, 0)),
            pl.BlockSpec((TM, D), lambda i: (jnp.maximum(i - 1, 0), 0)),
            pl.BlockSpec((1, D), lambda i: (0, 0)),
        ],
        out_specs=[pl.BlockSpec((TM, D), lambda i: (i, 0)), pl.BlockSpec((TM, 1), lambda i: (i, 0))],
        out_shape=[jax.ShapeDtypeStruct((rows, D), BF16), jax.ShapeDtypeStruct((rows, 1), F32)],
        scratch_shapes=[],
        args=(head, x2d, norm_w),
        riders=riders,
    )


def _heavy_tile(rows):
    return TM_HEAVY if rows % TM_HEAVY == 0 else TM


def _in_proj(u, w_shard, route, cos_t, sin_t, riders=()):
    rows = u.shape[0]
    tm = _heavy_tile(rows)
    nt = rows // tm
    kind, shard = BIG_PIECES["w_in"]

    def body(route_ref, u_hbm, wsh_ref, cos_ref, sin_ref, proj_ref, wfull_ref,
             u_sc, w_sc, u_sem, w_sem, local_sem, ici_send, ici_recv, fwd_send, fwd_recv):
        g, i = pl.program_id(0), pl.program_id(1)
        s, c = route_ref[0], route_ref[1]
        gid = route_ref[2 + g]
        sibling = (s // 2, s % 2, 1 - c)
        local = pltpu.make_async_copy(wsh_ref, _full_region(wfull_ref, kind, shard, s, None), local_sem)
        u_copies = [pltpu.make_async_copy(u_hbm.at[pl.ds(t * tm, tm)], u_sc.at[t], u_sem.at[t]) for t in range(nt)]
        sends, arrivals = [], []
        for k in (1, 2, 3):
            s2, dev = _other_chip(s, c, k)
            sends.append(_remote(_shard_region(wsh_ref, shard, c), _full_region(wfull_ref, kind, shard, s, c),
                                 ici_send.at[k - 1], ici_recv.at[k - 1], dev))
            mine = _full_region(wfull_ref, kind, shard, s2, c)
            theirs = _full_region(wfull_ref, kind, shard, s2, 1 - c)
            arrivals.append((_remote(mine, mine, ici_send.at[k - 1], ici_recv.at[k - 1], dev),
                             _remote(mine, mine, fwd_send.at[k - 1], fwd_recv.at[k - 1], sibling),
                             _remote(theirs, theirs, fwd_send.at[k - 1], fwd_recv.at[k - 1], sibling)))

        @pl.when(jnp.logical_and(g == 0, i == 0))
        def _():
            for cp in sends:
                cp.start()
            local.start()
            for cp in u_copies:
                cp.start()

        for k, (arrived, forward, forwarded) in zip((1, 2, 3), arrivals):
            @pl.when(jnp.logical_and(g == 2 * k, i == 0))
            def _():
                arrived.wait_recv()
                forward.start()
                forwarded.wait_recv()

        @pl.when(jnp.logical_and(i == 0, g < 2))
        def _():
            cp = pltpu.make_async_copy(wsh_ref.at[:, pl.ds(pl.multiple_of(g * D, D), D)], w_sc, w_sem)
            cp.start()
            cp.wait()

        @pl.when(jnp.logical_and(i == 0, g >= 2))
        def _():
            cp = pltpu.make_async_copy(wfull_ref.at[:, pl.ds(pl.multiple_of(gid * D, D), D)], w_sc, w_sem)
            cp.start()
            cp.wait()

        for t in range(nt):
            @pl.when(jnp.logical_and(g == 0, i == t))
            def _():
                u_copies[t].wait()

        acc = _dot(u_sc[i], w_sc[...])

        @pl.when(gid < 2)
        def _():
            scale = jnp.where(gid == 1, QK_SCALE, 1.0).astype(F32)
            for h in range(HEADS):
                sl = slice(h * DH, (h + 1) * DH)
                blk = acc[:, sl]
                out = (blk * cos_ref[...] + pltpu.roll(blk, DH // 2, axis=1) * sin_ref[...]) * scale
                proj_ref[:, sl] = out.astype(BF16)

        @pl.when(gid >= 2)
        def _():
            proj_ref[...] = acc.astype(BF16)

        @pl.when(jnp.logical_and(g == 7, i == nt - 1))
        def _():
            local.wait()
            for cp in sends:
                cp.wait_send()
            for _, forward, _ in arrivals:
                forward.wait_send()

    return _hosted_call(
        body,
        name="in_proj",
        grid=(8, nt),
        in_specs=[
            _ANY, _ANY,
            pl.BlockSpec((tm, DH), lambda g, i, rt: (i, 0)),
            pl.BlockSpec((tm, DH), lambda g, i, rt: (i, 0)),
        ],
        out_specs=[pl.BlockSpec((tm, D), lambda g, i, rt: (i, rt[2 + g])), _ANY],
        out_shape=[jax.ShapeDtypeStruct((rows, IN_COLS), BF16), jax.ShapeDtypeStruct((D, IN_COLS), BF16)],
        scratch_shapes=[
            pltpu.VMEM((nt, tm, D), BF16), pltpu.VMEM((D, D), BF16),
            pltpu.SemaphoreType.DMA((nt,)), pltpu.SemaphoreType.DMA, pltpu.SemaphoreType.DMA,
            pltpu.SemaphoreType.DMA((3,)), pltpu.SemaphoreType.DMA((3,)),
            pltpu.SemaphoreType.DMA((3,)), pltpu.SemaphoreType.DMA((3,)),
        ],
        args=(u, w_shard, cos_t, sin_t),
        riders=riders,
        prefetch=route,
        riders_after_body=True,
    )


def _retention_fwd(proj_bf, intra, q_dec, k_dec, c_dec, riders=()):
    rows = proj_bf.shape[0]
    nc = rows // CHUNK

    def body(q_ref, k_ref, v_ref, m_ref, qd_ref, kd_ref, cd_ref, o_ref, sprev_ref, s_sc):
        @pl.when(pl.program_id(0) == 0)
        def _():
            s_sc[...] = jnp.zeros_like(s_sc)

        for h in range(HEADS):
            sl = slice(h * DH, (h + 1) * DH)
            q, k, v = q_ref[:, sl], k_ref[:, sl], v_ref[:, sl]
            state = s_sc[h]
            state_b = state.astype(BF16)
            sprev_ref[0, h] = state_b
            s = _dot_nt(q, k) * m_ref[h]
            inner = _dot(s.astype(BF16), v)
            cross = _dot(q, state_b) * qd_ref[:, sl]
            o_ref[:, sl] = (inner + cross).astype(BF16)
            k_scaled = (k.astype(F32) * kd_ref[:, sl]).astype(BF16)
            s_sc[h] = state * cd_ref[:, sl] + _dot_tn(k_scaled, v)

    chunk_spec = lambda col: pl.BlockSpec((CHUNK, D), lambda c: (c, col))
    const2 = lambda shape: pl.BlockSpec(shape, lambda c: (0, 0))
    return _hosted_call(
        body,
        name="retention_fwd",
        grid=(nc,),
        in_specs=[
            chunk_spec(0), chunk_spec(1), chunk_spec(2),
            pl.BlockSpec((HEADS, CHUNK, CHUNK), lambda c: (0, 0, 0)),
            const2((CHUNK, D)), const2((CHUNK, D)), const2((1, D)),
        ],
        out_specs=[
            pl.BlockSpec((CHUNK, D), lambda c: (c, 0)),
            pl.BlockSpec((1, HEADS, DH, DH), lambda c: (c, 0, 0, 0)),
        ],
        out_shape=[
            jax.ShapeDtypeStruct((rows, D), BF16),
            jax.ShapeDtypeStruct((nc, HEADS, DH, DH), BF16),
        ],
        scratch_shapes=[pltpu.VMEM((HEADS, DH, DH), F32)],
        args=(proj_bf, proj_bf, proj_bf, intra, q_dec, k_dec, c_dec),
        riders=riders,
    )


def _shift_down(x, first8_prev, d):
    rolled = pltpu.roll(x, d, axis=0)
    head = pltpu.roll(jnp.concatenate([first8_prev, x[0:8]], axis=0), d, axis=0)[8:16]
    return rolled, head


def _conv_and_gates(x, prev8, cw_ref, cb_ref, wa_ref, wx_ref, ba_ref, bx_ref, lam_ref, c_sc):
    cw = cw_ref[...]
    conv = cb_ref[...] + cw[3:4] * x
    head = cb_ref[...] + cw[3:4] * x[0:8]
    shifted = []
    for d in (1, 2, 3):
        rolled, hd = _shift_down(x, prev8, d)
        conv = conv + cw[3 - d:4 - d] * rolled
        head = head + cw[3 - d:4 - d] * hd
        shifted.append((rolled, hd))
    c_sc[...] = conv
    c_sc[0:8, :] = head
    c = c_sc[...]
    zr, zi = [], []
    for g in range(LRU_BLOCKS):
        sl = slice(g * LRU_BLOCK, (g + 1) * LRU_BLOCK)
        cg = c[:, sl].astype(BF16)
        zr.append(_dot(cg, wa_ref[g]))
        zi.append(_dot(cg, wx_ref[g]))
    r = _sigmoid(jnp.concatenate(zr, axis=1) + ba_ref[...])
    gate_i = _sigmoid(jnp.concatenate(zi, axis=1) + bx_ref[...])
    sp = _softplus(-lam_ref[...])
    log_a = (-LRU_C) * r * sp
    a = jnp.exp(log_a)
    mult = jnp.sqrt(_neg_expm1(2.0 * log_a))
    return c, r, gate_i, a, mult, sp, shifted


def _lru_fwd(proj, conv_w, conv_b, wa, wx, ba, bx, lam):
    rows = proj.shape[0]
    nt = rows // TM

    def body(x_ref, cw_ref, cb_ref, wa_ref, wx_ref, ba_ref, bx_ref, lam_ref, h_ref, prev_sc, carry_sc, c_sc, a_sc, u_sc, h_sc):
        i = pl.program_id(0)

        @pl.when(i == 0)
        def _():
            prev_sc[...] = jnp.zeros_like(prev_sc)
            carry_sc[...] = jnp.zeros_like(carry_sc)

        x = x_ref[...].astype(F32)
        c, r, gate_i, a, mult, _, _ = _conv_and_gates(x, prev_sc[...], cw_ref, cb_ref, wa_ref, wx_ref, ba_ref, bx_ref, lam_ref, c_sc)
        prev_sc[...] = x[TM - 8:TM]
        valid = _row_ids(i, TM, (TM, D)) >= PAD_ROWS
        a_sc[...] = a
        u_sc[...] = jnp.where(valid, mult * gate_i * c, 0.0)
        row8 = lax.broadcasted_iota(jnp.int32, (8, D), 0)

        def group(gi, hprev):
            r0 = pl.multiple_of(gi * 8, 8)
            aa = a_sc[pl.ds(r0, 8), :]
            uu = u_sc[pl.ds(r0, 8), :]
            for d in (1, 2, 4):
                a_sh = jnp.where(row8 >= d, pltpu.roll(aa, d, axis=0), 1.0)
                u_sh = jnp.where(row8 >= d, pltpu.roll(uu, d, axis=0), 0.0)
                uu = uu + aa * u_sh
                aa = aa * a_sh
            hb = aa * hprev + uu
            h_sc[pl.ds(r0, 8), :] = hb
            return hb[7:8, :]

        hlast = lax.fori_loop(0, TM // 8, group, carry_sc[0:1, :])
        carry_sc[0:1, :] = hlast
        h_ref[...] = h_sc[...].astype(BF16)

    vec = pl.BlockSpec((1, D), lambda i: (0, 0))
    wspec = pl.BlockSpec((LRU_BLOCKS, LRU_BLOCK, LRU_BLOCK), lambda i: (0, 0, 0))
    return pl.pallas_call(
        body,
        name="lru_fwd",
        grid=(nt,),
        in_specs=[
            pl.BlockSpec((TM, D), lambda i: (i, 4)),
            pl.BlockSpec((4, D), lambda i: (0, 0)),
            vec, wspec, wspec, vec, vec, vec,
        ],
        out_specs=pl.BlockSpec((TM, D), lambda i: (i, 0)),
        out_shape=jax.ShapeDtypeStruct((rows, D), BF16),
        scratch_shapes=[
            pltpu.VMEM((8, D), F32), pltpu.VMEM((8, D), F32),
            pltpu.VMEM((TM, D), F32), pltpu.VMEM((TM, D), F32), pltpu.VMEM((TM, D), F32), pltpu.VMEM((TM, D), F32),
        ],
        compiler_params=_params(("arbitrary",)),
    )(proj, conv_w, conv_b, wa, wx, ba, bx, lam)


def _group_norm(o):
    outs, rstds = [], []
    for h in range(HEADS):
        oh = o[:, h * DH:(h + 1) * DH]
        rs = lax.rsqrt(jnp.mean(oh * oh, axis=-1, keepdims=True) + EPS)
        outs.append(oh * rs)
        rstds.append(rs)
    return jnp.concatenate(outs, axis=1), rstds


def _mix_fwd(head, x2d, o, proj, h_lru, w_br, w_bl, w_o, ffn_norm_w):
    rows = o.shape[0]
    nt = rows // TM

    def body(head_ref, x_ref, o_ref, gret_ref, lgate_ref, ga_ref, gb_ref, hl_ref, wbr_ref, wbl_ref, wo_ref, nw_ref,
             yret_ref, ylru_ref, h1_ref, u2_ref, rstd_ref):
        i = pl.program_id(0)
        on, _ = _group_norm(o_ref[...].astype(F32))
        gret = gret_ref[...].astype(F32)
        a_ret = (gret * _sigmoid(gret) * on).astype(BF16)
        y_ret = _dot(a_ret, wbr_ref[...])
        gl, _ = _gelu_and_grad(lgate_ref[...].astype(F32))
        a_lru = (gl * hl_ref[...].astype(F32)).astype(BF16)
        y_lru = _dot(a_lru, wbl_ref[...])
        mixed = (_sigmoid(ga_ref[...].astype(F32)) * y_ret + _sigmoid(gb_ref[...].astype(F32)) * y_lru).astype(BF16)
        delta = _dot(mixed, wo_ref[...])
        yret_ref[...] = y_ret.astype(BF16)
        ylru_ref[...] = y_lru.astype(BF16)

        def finish(h0):
            h1 = h0 + delta
            rs = lax.rsqrt(jnp.mean(h1 * h1, axis=-1, keepdims=True) + EPS)
            h1_ref[...] = h1
            u2_ref[...] = ((h1 * rs) * nw_ref[...]).astype(BF16)
            rstd_ref[...] = rs

        @pl.when(i == 0)
        def _():
            finish(head_ref[...])

        @pl.when(i > 0)
        def _():
            finish(x_ref[...])

    tile = lambda col: pl.BlockSpec((TM, D), lambda i: (i, col))
    wspec = pl.BlockSpec((D, D), lambda i: (0, 0))
    return pl.pallas_call(
        body,
        name="mix_fwd",
        grid=(nt,),
        in_specs=[
            pl.BlockSpec((FRONT, D), lambda i: (0, 0)),
            pl.BlockSpec((TM, D), lambda i: (jnp.maximum(i - 1, 0), 0)),
            tile(0), tile(3), tile(5), tile(6), tile(7), tile(0),
            wspec, wspec, wspec,
            pl.BlockSpec((1, D), lambda i: (0, 0)),
        ],
        out_specs=[tile(0), tile(0), tile(0), tile(0), pl.BlockSpec((TM, 1), lambda i: (i, 0))],
        out_shape=[
            jax.ShapeDtypeStruct((rows, D), BF16), jax.ShapeDtypeStruct((rows, D), BF16),
            jax.ShapeDtypeStruct((rows, D), F32), jax.ShapeDtypeStruct((rows, D), BF16),
            jax.ShapeDtypeStruct((rows, 1), F32),
        ],
        compiler_params=_params(("arbitrary",)),
    )(head, x2d, o, proj, proj, proj, proj, h_lru, w_br, w_bl, w_o, ffn_norm_w)


def _ffn_fwd(u2, h1, w_ffn_in, w_ffn_out):
    rows = u2.shape[0]
    tm = _heavy_tile(rows)
    nb = FFN // FFN_BLOCK
    hid = lambda: pl.BlockSpec((tm, FFN_BLOCK), lambda i, j: (i, j))

    def gate_body(u2_ref, wg_ref, wup_ref, g_ref, up_ref, act_ref):
        u2 = u2_ref[...]
        g = _dot(u2, wg_ref[...])
        up = _dot(u2, wup_ref[...])
        g_ref[...] = g.astype(BF16)
        up_ref[...] = up.astype(BF16)
        act_ref[...] = (g * _sigmoid(g) * up).astype(BF16)

    g, up, act = pl.pallas_call(
        gate_body,
        name="ffn_fwd_gate",
        grid=(rows // tm, nb),
        in_specs=[
            pl.BlockSpec((tm, D), lambda i, j: (i, 0)),
            pl.BlockSpec((D, FFN_BLOCK), lambda i, j: (0, j)),
            pl.BlockSpec((D, FFN_BLOCK), lambda i, j: (0, nb + j)),
        ],
        out_specs=[hid(), hid(), hid()],
        out_shape=[jax.ShapeDtypeStruct((rows, FFN), BF16)] * 3,
        compiler_params=_params(("arbitrary", "arbitrary")),
    )(u2, w_ffn_in, w_ffn_in)

    def out_body(act_ref, h1_ref, wo_ref, h2_ref):
        part = _dot(act_ref[...], wo_ref[...])

        @pl.when(pl.program_id(1) == 0)
        def _():
            h2_ref[...] = h1_ref[...] + part

        @pl.when(pl.program_id(1) > 0)
        def _():
            h2_ref[...] += part

    h2 = pl.pallas_call(
        out_body,
        name="ffn_fwd_out",
        grid=(rows // tm, nb),
        in_specs=[hid(), pl.BlockSpec((tm, D), lambda i, j: (i, 0)), pl.BlockSpec((FFN_BLOCK, D), lambda i, j: (j, 0))],
        out_specs=pl.BlockSpec((tm, D), lambda i, j: (i, 0)),
        out_shape=jax.ShapeDtypeStruct((rows, D), F32),
        compiler_params=_params(("arbitrary", "arbitrary")),
    )(act, h1, w_ffn_out)
    return g, up, act, h2


def _final_loss(h2, target, final_norm_w):
    rows = h2.shape[0]

    def body(h2_ref, tgt_ref, fnw_ref, dh2_ref, stats_ref):
        i = pl.program_id(0)

        @pl.when(i == 0)
        def _():
            stats_ref[...] = jnp.zeros_like(stats_ref)

        h2 = h2_ref[...]
        rs = lax.rsqrt(jnp.mean(h2 * h2, axis=-1, keepdims=True) + EPS)
        n = h2 * rs
        fnw = fnw_ref[...]
        valid = _row_ids(i, TM, (TM, D)) >= FRONT
        diff = jnp.where(valid, n * fnw - tgt_ref[...], 0.0)
        dy = diff * (1.0 / D)
        stats_ref[0:1, :] += (0.5 / D) * jnp.sum(diff * diff, axis=0, keepdims=True)
        stats_ref[1:2, :] += jnp.sum(dy * n, axis=0, keepdims=True)
        dn = dy * fnw
        dh2_ref[...] = rs * (dn - n * jnp.mean(dn * n, axis=-1, keepdims=True))

    return pl.pallas_call(
        body,
        name="final_loss",
        grid=(rows // TM,),
        in_specs=[
            pl.BlockSpec((TM, D), lambda i: (i, 0)),
            pl.BlockSpec((TM, D), lambda i: (jnp.maximum(i - 1, 0), 0)),
            pl.BlockSpec((1, D), lambda i: (0, 0)),
        ],
        out_specs=[pl.BlockSpec((TM, D), lambda i: (i, 0)), pl.BlockSpec((8, D), lambda i: (0, 0))],
        out_shape=[jax.ShapeDtypeStruct((rows, D), F32), jax.ShapeDtypeStruct((8, D), F32)],
        compiler_params=_params(("arbitrary",)),
    )(h2, target, final_norm_w)


def _ffn_bwd(dh2, g, up, h1, rstd2, w_ffn_in, w_ffn_out, ffn_norm_w):
    rows = dh2.shape[0]
    tm = _heavy_tile(rows)
    nb = FFN // FFN_BLOCK
    blk = lambda: pl.BlockSpec((tm, FFN_BLOCK), lambda i, j: (i, j))

    def gate_body(dh2_ref, g_ref, up_ref, wo_ref, dg_ref, dup_ref, dh2b_sc):
        @pl.when(pl.program_id(1) == 0)
        def _():
            dh2b_sc[...] = dh2_ref[...].astype(BF16)

        dact = _dot_nt(dh2b_sc[...], wo_ref[...])
        g = g_ref[...].astype(F32)
        up = up_ref[...].astype(F32)
        sg = _sigmoid(g)
        dup_ref[...] = (dact * (g * sg)).astype(BF16)
        dg_ref[...] = (dact * up * (sg * (1.0 + g * (1.0 - sg)))).astype(BF16)

    dg, dup = pl.pallas_call(
        gate_body,
        name="ffn_bwd_gate",
        grid=(rows // tm, nb),
        in_specs=[
            pl.BlockSpec((tm, D), lambda i, j: (i, 0)),
            blk(), blk(),
            pl.BlockSpec((FFN_BLOCK, D), lambda i, j: (j, 0)),
        ],
        out_specs=[blk(), blk()],
        out_shape=[jax.ShapeDtypeStruct((rows, FFN), BF16)] * 2,
        scratch_shapes=[pltpu.VMEM((tm, D), BF16)],
        compiler_params=_params(("arbitrary", "arbitrary")),
    )(dh2, g, up, w_ffn_out)

    def body(dg_ref, dup_ref, dh2_ref, h1_ref, rstd_ref, wg_ref, wup_ref, nw_ref, dh1_ref, stats_ref, du_sc):
        i = pl.program_id(0)
        j = pl.program_id(1)

        @pl.when(jnp.logical_and(i == 0, j == 0))
        def _():
            stats_ref[...] = jnp.zeros_like(stats_ref)

        part = _dot_nt(dg_ref[...], wg_ref[...]) + _dot_nt(dup_ref[...], wup_ref[...])

        @pl.when(j == 0)
        def _():
            du_sc[...] = part

        @pl.when(j > 0)
        def _():
            du_sc[...] += part

        @pl.when(j == nb - 1)
        def _():
            du = du_sc[...]
            rs = rstd_ref[...]
            n = h1_ref[...] * rs
            stats_ref[0:1, :] += jnp.sum(du * n, axis=0, keepdims=True)
            dn = du * nw_ref[...]
            dh1_ref[...] = dh2_ref[...] + rs * (dn - n * jnp.mean(dn * n, axis=-1, keepdims=True))

    dh1, stats = pl.pallas_call(
        body,
        name="ffn_bwd_in",
        grid=(rows // tm, nb),
        in_specs=[
            blk(), blk(),
            pl.BlockSpec((tm, D), lambda i, j: (i, 0)),
            pl.BlockSpec((tm, D), lambda i, j: (i, 0)),
            pl.BlockSpec((tm, 1), lambda i, j: (i, 0)),
            pl.BlockSpec((D, FFN_BLOCK), lambda i, j: (0, j)),
            pl.BlockSpec((D, FFN_BLOCK), lambda i, j: (0, nb + j)),
            pl.BlockSpec((1, D), lambda i, j: (0, 0)),
        ],
        out_specs=[pl.BlockSpec((tm, D), lambda i, j: (i, 0)), pl.BlockSpec((8, D), lambda i, j: (0, 0))],
        out_shape=[jax.ShapeDtypeStruct((rows, D), F32), jax.ShapeDtypeStruct((8, D), F32)],
        scratch_shapes=[pltpu.VMEM((tm, D), F32)],
        compiler_params=_params(("arbitrary", "arbitrary")),
    )(dg, dup, dh2, h1, rstd2, w_ffn_in, w_ffn_in, ffn_norm_w)
    return dg, dup, dh1, stats


def _mix_bwd(dh1, o, proj, h_lru, y_ret, y_lru, w_br, w_bl, w_o, riders=()):
    rows = dh1.shape[0]
    tm = TM_MIX_BWD
    nt = rows // tm

    def body(dh1_ref, o_ref, gret_ref, lgate_ref, ga_ref, gb_ref, hl_ref, yret_ref, ylru_ref, wbr_ref, wbl_ref, wo_ref,
             dproj_ref, do_ref, dhl_ref, mixed_ref, aret_ref, alru_ref, dyret_ref, dylru_ref):
        dmixed = _dot_nt(dh1_ref[...].astype(BF16), wo_ref[...])
        y_ret, y_lru = yret_ref[...].astype(F32), ylru_ref[...].astype(F32)
        sa, sb = _sigmoid(ga_ref[...].astype(F32)), _sigmoid(gb_ref[...].astype(F32))
        mixed_ref[...] = (sa * y_ret + sb * y_lru).astype(BF16)
        dga = dmixed * y_ret * sa * (1.0 - sa)
        dgb = dmixed * y_lru * sb * (1.0 - sb)
        dy_ret = (dmixed * sa).astype(BF16)
        dy_lru = (dmixed * sb).astype(BF16)
        dyret_ref[...] = dy_ret
        dylru_ref[...] = dy_lru
        da_ret = _dot_nt(dy_ret, wbr_ref[...])
        da_lru = _dot_nt(dy_lru, wbl_ref[...])

        gret = gret_ref[...].astype(F32)
        sg = _sigmoid(gret)
        silu = gret * sg
        on, rstds = _group_norm(o_ref[...].astype(F32))
        aret_ref[...] = (silu * on).astype(BF16)
        dgret = da_ret * on * (sg * (1.0 + gret * (1.0 - sg)))
        don = da_ret * silu
        for h in range(HEADS):
            sl = slice(h * DH, (h + 1) * DH)
            onh, donh = on[:, sl], don[:, sl]
            do_ref[:, sl] = (rstds[h] * (donh - onh * jnp.mean(donh * onh, axis=-1, keepdims=True))).astype(BF16)

        gl, gl_grad = _gelu_and_grad(lgate_ref[...].astype(F32))
        hl = hl_ref[...].astype(F32)
        alru_ref[...] = (gl * hl).astype(BF16)
        dlgate = da_lru * hl * gl_grad
        dhl_ref[...] = (da_lru * gl).astype(BF16)

        zeros = jnp.zeros((tm, D), BF16)
        for col in (0, 1, 2, 4):
            dproj_ref[:, col * D:(col + 1) * D] = zeros
        dproj_ref[:, 3 * D:4 * D] = dgret.astype(BF16)
        dproj_ref[:, 5 * D:6 * D] = dlgate.astype(BF16)
        dproj_ref[:, 6 * D:7 * D] = dga.astype(BF16)
        dproj_ref[:, 7 * D:8 * D] = dgb.astype(BF16)

    tile = lambda col: pl.BlockSpec((tm, D), lambda i: (i, col))
    wspec = pl.BlockSpec((D, D), lambda i: (0, 0))
    bf = lambda: jax.ShapeDtypeStruct((rows, D), BF16)
    return _hosted_call(
        body,
        name="mix_bwd",
        grid=(nt,),
        in_specs=[tile(0), tile(0), tile(3), tile(5), tile(6), tile(7), tile(0), tile(0), tile(0), wspec, wspec, wspec],
        out_specs=[pl.BlockSpec((tm, IN_COLS), lambda i: (i, 0))] + [tile(0)] * 7,
        out_shape=[jax.ShapeDtypeStruct((rows, IN_COLS), BF16)] + [bf() for _ in range(7)],
        scratch_shapes=[],
        args=(dh1, o, proj, proj, proj, proj, h_lru, y_ret, y_lru, w_br, w_bl, w_o),
        riders=riders,
    )


def _retention_bwd(dproj, proj_bf, do, sprev, intra, intra_t, q_dec, k_dec, c_dec, cos_t, sin_t, riders=()):
    rows = proj_bf.shape[0]
    nc = rows // CHUNK

    def body(dproj_in_ref, q_ref, k_ref, v_ref, do_ref, sprev_ref, m_ref, mt_ref, qd_ref, kd_ref, cd_ref, cos_ref, sin_ref,
             dproj_ref, ds_sc):
        @pl.when(pl.program_id(0) == 0)
        def _():
            ds_sc[...] = jnp.zeros_like(ds_sc)

        cos, sin = cos_ref[...], sin_ref[...]

        def unrotate(dy):
            return dy * cos - pltpu.roll(dy, DH // 2, axis=1) * sin

        for h in range(HEADS):
            sl = slice(h * DH, (h + 1) * DH)
            q, k, v = q_ref[:, sl], k_ref[:, sl], v_ref[:, sl]
            do = do_ref[:, sl]
            dob = do.astype(BF16)
            doq = (do * qd_ref[:, sl]).astype(BF16)
            state_prev = sprev_ref[0, h]
            dstate = ds_sc[h]
            dstate_b = dstate.astype(BF16)
            s_t = (_dot_nt(k, q) * mt_ref[h]).astype(BF16)
            ds_t = (_dot_nt(v, dob) * mt_ref[h]).astype(BF16)
            ds = (_dot_nt(dob, v) * m_ref[h]).astype(BF16)
            kd = kd_ref[:, sl]
            dq = _dot(ds, k) + _dot_nt(doq, state_prev)
            dk = _dot(ds_t, q) + _dot_nt(v, dstate_b) * kd
            k_scaled = (k.astype(F32) * kd).astype(BF16)
            dv = _dot(s_t, dob) + _dot(k_scaled, dstate_b)
            ds_sc[h] = dstate * cd_ref[:, sl] + _dot_tn(q, doq)
            dproj_ref[:, sl] = unrotate(dq).astype(BF16)
            dproj_ref[:, D + h * DH:D + (h + 1) * DH] = (unrotate(dk) * QK_SCALE).astype(BF16)
            dproj_ref[:, 2 * D + h * DH:2 * D + (h + 1) * DH] = dv.astype(BF16)

    rev = lambda c: nc - 1 - c
    chunk_spec = lambda col: pl.BlockSpec((CHUNK, D), lambda c: (rev(c), col))
    const2 = lambda shape: pl.BlockSpec(shape, lambda c: (0, 0))
    const3 = pl.BlockSpec((HEADS, CHUNK, CHUNK), lambda c: (0, 0, 0))
    return _hosted_call(
        body,
        name="retention_bwd",
        grid=(nc,),
        in_specs=[
            pl.BlockSpec(memory_space=pl.ANY),
            chunk_spec(0), chunk_spec(1), chunk_spec(2), chunk_spec(0),
            pl.BlockSpec((1, HEADS, DH, DH), lambda c: (rev(c), 0, 0, 0)),
            const3, const3,
            const2((CHUNK, D)), const2((CHUNK, D)), const2((1, D)),
            pl.BlockSpec((CHUNK, DH), lambda c: (rev(c), 0)),
            pl.BlockSpec((CHUNK, DH), lambda c: (rev(c), 0)),
        ],
        out_specs=[pl.BlockSpec((CHUNK, 3 * D), lambda c: (rev(c), 0))],
        out_shape=[jax.ShapeDtypeStruct(dproj.shape, BF16)],
        aliases={0: 0},
        scratch_shapes=[pltpu.VMEM((HEADS, DH, DH), F32)],
        args=(dproj, proj_bf, proj_bf, proj_bf, do, sprev, intra, intra_t, q_dec, k_dec, c_dec, cos_t, sin_t),
        riders=riders,
    )


def _lru_bwd(dproj, proj, h_lru, dhl, conv_w, conv_b, wa, wx, ba, bx, lam, riders=()):
    rows = proj.shape[0]
    nt = rows // TM
    per8 = TM // 8

    def body(dproj_in_ref, x_ref, xprev_ref, h_ref, hprev_ref, dhl_ref, cw_ref, cb_ref, wa_ref, wx_ref, ba_ref, bx_ref, lam_ref,
             dproj_ref, dwa_ref, dwx_ref, stats_ref, anext_sc, dhnext_sc, dcnext_sc, c_sc, b_sc, dh_sc):
        step = pl.program_id(0)
        i = nt - 1 - step

        @pl.when(step == 0)
        def _():
            anext_sc[...] = jnp.zeros_like(anext_sc)
            dhnext_sc[...] = jnp.zeros_like(dhnext_sc)
            dcnext_sc[...] = jnp.zeros_like(dcnext_sc)
            dwa_ref[...] = jnp.zeros_like(dwa_ref)
            dwx_ref[...] = jnp.zeros_like(dwx_ref)
            stats_ref[...] = jnp.zeros_like(stats_ref)

        first = i == 0
        x = x_ref[...].astype(F32)
        prev8 = jnp.where(first, 0.0, xprev_ref[8:16, :].astype(F32))
        c, r, gate_i, a, mult, sp, shifted = _conv_and_gates(x, prev8, cw_ref, cb_ref, wa_ref, wx_ref, ba_ref, bx_ref, lam_ref, c_sc)
        dh_sc[...] = dhl_ref[...].astype(F32)

        b_sc[...] = pltpu.roll(a, TM - 1, axis=0)
        b_sc[TM - 1:TM, :] = anext_sc[0:1, :]
        anext_sc[0:1, :] = a[0:1, :]
        row8 = lax.broadcasted_iota(jnp.int32, (8, D), 0)

        def group(gi, dhnext):
            r0 = pl.multiple_of((per8 - 1 - gi) * 8, 8)
            bb = b_sc[pl.ds(r0, 8), :]
            uu = dh_sc[pl.ds(r0, 8), :]
            for d in (1, 2, 4):
                b_sh = jnp.where(row8 < 8 - d, pltpu.roll(bb, 8 - d, axis=0), 1.0)
                u_sh = jnp.where(row8 < 8 - d, pltpu.roll(uu, 8 - d, axis=0), 0.0)
                uu = uu + bb * u_sh
                bb = bb * b_sh
            dhb = bb * dhnext + uu
            dh_sc[pl.ds(r0, 8), :] = dhb
            return dhb[0:1, :]

        dhfirst = lax.fori_loop(0, per8, group, dhnext_sc[0:1, :])
        dhnext_sc[0:1, :] = dhfirst
        dh = dh_sc[...]

        h = h_ref[...].astype(F32)
        hprev8 = jnp.where(first, 0.0, hprev_ref[8:16, :].astype(F32))
        h_dn, h_head = _shift_down(h, hprev8, 1)
        c_sc[...] = h_dn
        c_sc[0:8, :] = h_head
        h_before = c_sc[...]

        valid = _row_ids(i, TM, (TM, D)) >= PAD_ROWS
        da = dh * h_before
        du = jnp.where(valid, dh, 0.0)
        dmult = du * gate_i * c
        dgate_i = du * mult * c
        dc = du * mult * gate_i
        dla = da * a - dmult * (a * a) / mult
        dla = jnp.where(valid, dla, 0.0)
        dr = dla * ((-LRU_C) * sp)
        dzr = dr * r * (1.0 - r)
        dzi = dgate_i * gate_i * (1.0 - gate_i)
        stats_ref[1:2, :] += jnp.sum(dzr, axis=0, keepdims=True)
        stats_ref[2:3, :] += jnp.sum(dzi, axis=0, keepdims=True)
        stats_ref[3:4, :] += jnp.sum(dla * ((-LRU_C) * r), axis=0, keepdims=True)
        dc_gate = []
        for g in range(LRU_BLOCKS):
            sl = slice(g * LRU_BLOCK, (g + 1) * LRU_BLOCK)
            cg = c[:, sl].astype(BF16)
            dzr_g = dzr[:, sl].astype(BF16)
            dzi_g = dzi[:, sl].astype(BF16)
            dc_gate.append(_dot_nt(dzr_g, wa_ref[g]) + _dot_nt(dzi_g, wx_ref[g]))
            dwa_ref[g] += _dot_tn(cg, dzr_g)
            dwx_ref[g] += _dot_tn(cg, dzi_g)
        dc = dc + jnp.concatenate(dc_gate, axis=1)

        cw = cw_ref[...]
        stats_ref[0:1, :] += jnp.sum(dc, axis=0, keepdims=True)
        stats_ref[7:8, :] += jnp.sum(dc * x, axis=0, keepdims=True)
        dx = cw[3:4] * dc
        tail_src = jnp.concatenate([dc[TM - 8:TM], dcnext_sc[...]], axis=0)
        dx_tail = cw[3:4] * dc[TM - 8:TM]
        for d in (1, 2, 3):
            dx = dx + cw[3 - d:4 - d] * pltpu.roll(dc, TM - d, axis=0)
            dx_tail = dx_tail + cw[3 - d:4 - d] * pltpu.roll(tail_src, 16 - d, axis=0)[0:8]
            rolled, hd = shifted[d - 1]
            b_sc[...] = rolled
            b_sc[0:8, :] = hd
            stats_ref[7 - d:8 - d, :] += jnp.sum(dc * b_sc[...], axis=0, keepdims=True)
        dcnext_sc[...] = dc[0:8]
        dproj_ref[...] = dx.astype(BF16)
        dproj_ref[TM - 8:TM, :] = dx_tail.astype(BF16)

    rev = lambda s: nt - 1 - s
    vec = pl.BlockSpec((1, D), lambda s: (0, 0))
    wspec = pl.BlockSpec((LRU_BLOCKS, LRU_BLOCK, LRU_BLOCK), lambda s: (0, 0, 0))
    prev16 = lambda col: pl.BlockSpec((16, D), lambda s: (jnp.maximum(rev(s) * (TM // 16) - 1, 0), col))
    return _hosted_call(
        body,
        name="lru_bwd",
        grid=(nt,),
        in_specs=[
            pl.BlockSpec(memory_space=pl.ANY),
            pl.BlockSpec((TM, D), lambda s: (rev(s), 4)), prev16(4),
            pl.BlockSpec((TM, D), lambda s: (rev(s), 0)), prev16(0),
            pl.BlockSpec((TM, D), lambda s: (rev(s), 0)),
            pl.BlockSpec((4, D), lambda s: (0, 0)),
            vec, wspec, wspec, vec, vec, vec,
        ],
        out_specs=[
            pl.BlockSpec((TM, D), lambda s: (rev(s), 4)),
            wspec, wspec,
            pl.BlockSpec((8, D), lambda s: (0, 0)),
        ],
        out_shape=[
            jax.ShapeDtypeStruct(dproj.shape, BF16),
            jax.ShapeDtypeStruct((LRU_BLOCKS, LRU_BLOCK, LRU_BLOCK), F32),
            jax.ShapeDtypeStruct((LRU_BLOCKS, LRU_BLOCK, LRU_BLOCK), F32),
            jax.ShapeDtypeStruct((8, D), F32),
        ],
        aliases={0: 0},
        scratch_shapes=[
            pltpu.VMEM((8, D), F32), pltpu.VMEM((8, D), F32), pltpu.VMEM((8, D), F32),
            pltpu.VMEM((TM, D), F32), pltpu.VMEM((TM, D), F32), pltpu.VMEM((TM, D), F32),
        ],
        args=(dproj, proj, proj, h_lru, h_lru, dhl, conv_w, conv_b, wa, wx, ba, bx, lam),
        riders=riders,
    )


def _in_proj_bwd(dproj, w_in, part, prev=None, riders=()):
    rows = dproj.shape[0]
    tm = _heavy_tile(rows)
    nt = rows // tm
    first = 0 if part == 0 else (nt + 1) // 2
    count = (nt + 1) // 2 if part == 0 else nt - first

    def body(*refs):
        dproj_ref, w_hbm, du_ref, w_sc, w_sem = refs[-5:]

        @pl.when(pl.program_id(0) == 0)
        def _():
            cp = pltpu.make_async_copy(w_hbm, w_sc, w_sem)
            cp.start()
            cp.wait()

        du_ref[...] = _dot_nt(dproj_ref[...], w_sc[...])

    in_specs = [pl.BlockSpec((tm, IN_COLS), lambda i: (first + i, 0)), _ANY]
    args = (dproj, w_in)
    if prev is not None:
        in_specs = [_ANY] + in_specs
        args = (prev,) + args
    return _hosted_call(
        body,
        name="in_proj_bwd_%d" % part,
        grid=(count,),
        in_specs=in_specs,
        out_specs=[pl.BlockSpec((tm, D), lambda i: (first + i, 0))],
        out_shape=[jax.ShapeDtypeStruct((rows, D), F32)],
        scratch_shapes=[pltpu.VMEM((D, IN_COLS), BF16), pltpu.SemaphoreType.DMA],
        aliases={0: 0} if prev is not None else None,
        args=args,
        riders=riders,
    )


def _norm1_bwd(du, dh1, head, x2d, rstd1, norm_w, riders=()):
    rows = du.shape[0]

    def body(du_ref, dh1_ref, head_ref, x_ref, rstd_ref, nw_ref, gx_ref, ghead_ref, stats_ref):
        i = pl.program_id(0)

        @pl.when(i == 0)
        def _():
            stats_ref[...] = jnp.zeros_like(stats_ref)

        def finish(h0, out_ref):
            du = du_ref[...]
            rs = rstd_ref[...]
            n = h0 * rs
            stats_ref[0:1, :] += jnp.sum(du * n, axis=0, keepdims=True)
            dn = du * nw_ref[...]
            out_ref[...] = dh1_ref[...] + rs * (dn - n * jnp.mean(dn * n, axis=-1, keepdims=True))

        @pl.when(i == 0)
        def _():
            finish(head_ref[...], ghead_ref)

        @pl.when(i > 0)
        def _():
            finish(x_ref[...], gx_ref)

    tile = pl.BlockSpec((TM, D), lambda i: (i, 0))
    return _hosted_call(
        body,
        name="norm1_bwd",
        grid=(rows // TM,),
        in_specs=[
            tile, tile,
            pl.BlockSpec((FRONT, D), lambda i: (0, 0)),
            pl.BlockSpec((TM, D), lambda i: (jnp.maximum(i - 1, 0), 0)),
            pl.BlockSpec((TM, 1), lambda i: (i, 0)),
            pl.BlockSpec((1, D), lambda i: (0, 0)),
        ],
        out_specs=[
            pl.BlockSpec((TM, D), lambda i: (jnp.maximum(i - 1, 0), 0)),
            pl.BlockSpec((FRONT, D), lambda i: (0, 0)),
            pl.BlockSpec((8, D), lambda i: (0, 0)),
        ],
        out_shape=[
            jax.ShapeDtypeStruct(x2d.shape, F32),
            jax.ShapeDtypeStruct((FRONT, D), F32),
            jax.ShapeDtypeStruct((8, D), F32),
        ],
        scratch_shapes=[],
        args=(du, dh1, head, x2d, rstd1, norm_w),
        riders=riders,
    )


def _matmul_tn(name, x, dy, out_cols, col0=0, prev=None, k_block=None, n_block=None, riders=()):
    rows, kdim = x.shape
    ndim = dy.shape[1]
    tr = next(t for t in (1408, TM_HEAVY, TM) if rows % t == 0)
    kb = k_block or kdim
    nb = n_block or ndim
    nr, nk, nn = rows // tr, kdim // kb, ndim // nb
    cb0 = col0 // nb

    def body(*refs):
        x_ref, dy_ref, out_ref = refs[-3], refs[-2], refs[-1]
        part = _dot_tn(x_ref[...].astype(BF16), dy_ref[...].astype(BF16))

        @pl.when(pl.program_id(2) == 0)
        def _():
            out_ref[...] = part

        @pl.when(pl.program_id(2) > 0)
        def _():
            out_ref[...] += part

    in_specs = [
        pl.BlockSpec((tr, kb), lambda n, k, r: (r, k)),
        pl.BlockSpec((tr, nb), lambda n, k, r: (r, n)),
    ]
    args = [x, dy]
    aliases = {}
    if prev is not None:
        in_specs = [pl.BlockSpec(memory_space=pl.ANY)] + in_specs
        args = [prev] + args
        aliases = {0: 0}
    (out,), rider_outs = _hosted_call(
        body,
        name=name,
        grid=(nn, nk, nr),
        in_specs=in_specs,
        out_specs=[pl.BlockSpec((kb, nb), lambda n, k, r: (k, cb0 + n))],
        out_shape=[jax.ShapeDtypeStruct((kdim, out_cols), F32)],
        scratch_shapes=[],
        aliases=aliases,
        args=args,
        riders=riders,
    )
    return (out, rider_outs) if riders else out


def _local_step(x2d, target, w, plan):
    rows = FRONT + x2d.shape[0]
    head = jnp.concatenate([jnp.zeros((PAD_ROWS, D), F32), w["meta_tokens"]], axis=0)
    cos_t, sin_t = _rope_tables(rows)
    intra, intra_t, q_dec, k_dec, c_dec = _decay_tables()
    grads = {}

    def hosted(host, fn, *args, **kwargs):
        outs, rider_outs = fn(*args, riders=plan.riders(host, w, grads), **kwargs)
        plan.after(host, rider_outs, w, grads)
        return outs

    (u1, rstd1), _ = _norm1(head, x2d, w["mix_norm_w"])
    proj, w["w_in"] = hosted("in_proj", _in_proj, u1, w["w_in_shard"], w["route"], cos_t, sin_t)
    o, sprev = hosted("retention_fwd", _retention_fwd, proj, intra, q_dec, k_dec, c_dec)
    lru_args = (w["conv_w"], w["conv_b"], w["lru_wa"], w["lru_wx"], w["lru_ba"], w["lru_bx"], w["lru_lambda"])
    h_lru = _lru_fwd(proj, *lru_args)
    y_ret, y_lru, h1, u2, rstd2 = _mix_fwd(head, x2d, o, proj, h_lru, w["w_branch_ret"], w["w_branch_lru"], w["w_out"],
                                           w["ffn_norm_w"])
    g, up, act, h2 = _ffn_fwd(u2, h1, w["w_ffn_in"], w["w_ffn_out"])
    dh2, stats_loss = _final_loss(h2, target, w["final_norm_w"])

    dg, dup, dh1, stats_ffn = _ffn_bwd(dh2, g, up, h1, rstd2, w["w_ffn_in"], w["w_ffn_out"], w["ffn_norm_w"])
    grads["w_ffn_in"] = _matmul_tn("dw_ffn_up", u2, dup, 2 * FFN, col0=FFN, n_block=FFN_HALF,
                                   prev=_matmul_tn("dw_ffn_gate", u2, dg, 2 * FFN, n_block=FFN_HALF))
    grads["w_ffn_out"] = _matmul_tn("dw_ffn_out", act, dh2, D, k_block=FFN_HALF)
    (dproj, do, dhl, mixed, a_ret, a_lru, dy_ret, dy_lru) = hosted(
        "mix_bwd", _mix_bwd, dh1, o, proj, h_lru, y_ret, y_lru, w["w_branch_ret"], w["w_branch_lru"], w["w_out"])
    grads["w_out"] = _matmul_tn("dw_out", mixed, dh1, D)
    grads["w_branch_ret"] = _matmul_tn("dw_branch_ret", a_ret, dy_ret, D)
    grads["w_branch_lru"] = _matmul_tn("dw_branch_lru", a_lru, dy_lru, D)
    (dproj,) = hosted("retention_bwd", _retention_bwd, dproj, proj, do, sprev, intra, intra_t, q_dec, k_dec, c_dec,
                      cos_t, sin_t)
    dproj, grads["lru_wa"], grads["lru_wx"], stats_lru = hosted("lru_bwd", _lru_bwd, dproj, proj, h_lru, dhl, *lru_args)
    grads["w_in"], rider_outs = _matmul_tn("dw_in", u1, dproj, IN_COLS, n_block=D, riders=plan.riders("dw_in", w, grads))
    plan.after("dw_in", rider_outs, w, grads)
    (du1,) = hosted("in_proj_bwd_0", _in_proj_bwd, dproj, w["w_in"], 0)
    (du1,) = hosted("in_proj_bwd_1", _in_proj_bwd, dproj, w["w_in"], 1, du1)
    (grad_x, grad_head, stats_in), _ = _norm1_bwd(du1, dh1, head, x2d, rstd1, w["mix_norm_w"])
    return grad_x, grad_head, grads, [stats_loss, stats_ffn, stats_in, stats_lru]


BIG_PIECES = {
    "w_in": ("col", (D, 2 * D)),
    "w_ffn_in": ("col", (D, FFN_HALF)),
    "w_ffn_out": ("row", (FFN // 4, D)),
    "w_branch_ret": ("row", (D // 4, D)),
    "w_branch_lru": ("row", (D // 4, D)),
    "w_out": ("row", (D // 4, D)),
    "lru_wa": ("lru", (LRU_BLOCKS, LRU_BLOCK // 4, LRU_BLOCK)),
    "lru_wx": ("lru", (LRU_BLOCKS, LRU_BLOCK // 4, LRU_BLOCK)),
}
SMALL_PIECES = {"meta_tokens": ("col", (N_META, D // 4)), "conv_w": ("col", (4, D // 4))}


def _full_shape(kind, shard):
    if kind == "col":
        return (shard[0], 4 * shard[1])
    if kind == "row":
        return (4 * shard[0], shard[1])
    return (shard[0], 4 * shard[1], shard[2])


def _half_shape(kind, shard):
    return (shard[0] // 2,) + tuple(shard[1:])


def _aligned(start, multiple):
    return start if isinstance(start, int) else pl.multiple_of(start, multiple)


def _lead(h, size):
    if h is None:
        return pl.ds(0, size)
    return pl.ds(_aligned(h * (size // 2), size // 2), size // 2)


def _full_region(ref, kind, shard, s, h):
    if kind == "col":
        return ref.at[_lead(h, shard[0]), pl.ds(_aligned(s * shard[1], shard[1]), shard[1])]
    if kind == "row":
        size = shard[0] if h is None else shard[0] // 2
        start = s * shard[0] + (0 if h is None else h * (shard[0] // 2))
        return ref.at[pl.ds(_aligned(start, 16), size), :]
    return ref.at[_lead(h, shard[0]), pl.ds(_aligned(s * shard[1], shard[1]), shard[1]), :]


def _shard_region(ref, shard, h):
    return ref.at[_lead(h, shard[0])]


def _place():
    x, y, c = lax.axis_index("x"), lax.axis_index("y"), lax.axis_index("c")
    return x, y, c, 2 * x + y


def _other_chip(s, c, k):
    s2 = jnp.bitwise_xor(s, k)
    return s2, (s2 // 2, s2 % 2, c)


def _remote(src, dst, send_sem, recv_sem, dev):
    return pltpu.make_async_remote_copy(src_ref=src, dst_ref=dst, send_sem=send_sem, recv_sem=recv_sem,
                                        device_id=dev, device_id_type=MESH)


_ANY = pl.BlockSpec(memory_space=pl.ANY)


class _Rider:
    def __init__(self, ins, out_shapes, sem_shapes, build, aliased=False):
        self.ins, self.out_shapes, self.sem_shapes, self.build, self.aliased = ins, out_shapes, sem_shapes, build, aliased


def _hosted_call(body, *, name, grid, in_specs, out_specs, out_shape, scratch_shapes, args, riders=(), aliases=None,
                 prefetch=None, riders_after_body=False):
    n_in, n_out, n_sc = len(in_specs), len(out_shape), len(scratch_shapes)
    r_in = [a for r in riders for a in r.ins]
    r_out = [s for r in riders for s in r.out_shapes]
    r_sem = [s for r in riders for s in r.sem_shapes]
    lead = () if prefetch is None else (prefetch,)
    assert prefetch is None or not (aliases or any(r.aliased for r in riders))

    def full_body(*refs):
        head, refs = refs[:len(lead)], refs[len(lead):]
        ins, rin = refs[:n_in], refs[n_in:n_in + len(r_in)]
        o0 = n_in + len(r_in)
        outs, rout = refs[o0:o0 + n_out], refs[o0 + n_out:o0 + n_out + len(r_out)]
        s0 = o0 + n_out + len(r_out)
        scratch, rsem = refs[s0:s0 + n_sc], refs[s0 + n_sc:]
        starts, waits = [], []
        pi = po = ps = 0
        for r in riders:
            st, wt = r.build(rin[pi:pi + len(r.ins)], rout[po:po + len(r.out_shapes)], rsem[ps:ps + len(r.sem_shapes)])
            starts += st
            waits += wt
            pi, po, ps = pi + len(r.ins), po + len(r.out_shapes), ps + len(r.sem_shapes)
        first = functools.reduce(jnp.logical_and, [pl.program_id(d) == 0 for d in range(len(grid))])
        last = functools.reduce(jnp.logical_and, [pl.program_id(d) == grid[d] - 1 for d in range(len(grid))])

        def start_riders():
            @pl.when(first)
            def _():
                for cp in starts:
                    cp.start()

        if riders and not riders_after_body:
            start_riders()
        body(*head, *ins, *outs, *scratch)
        if riders and riders_after_body:
            start_riders()
        if riders:
            @pl.when(last)
            def _():
                for wait in waits:
                    wait()

    io_aliases = dict(aliases or {})
    pi = po = 0
    for r in riders:
        if r.aliased:
            for q in range(len(r.ins)):
                io_aliases[n_in + pi + q] = n_out + po + q
        pi, po = pi + len(r.ins), po + len(r.out_shapes)
    specs = dict(
        grid=grid,
        in_specs=list(in_specs) + [_ANY] * len(r_in),
        out_specs=list(out_specs) + [_ANY] * len(r_out),
        scratch_shapes=list(scratch_shapes) + r_sem,
    )
    if prefetch is not None:
        specs = dict(grid_spec=pltpu.PrefetchScalarGridSpec(num_scalar_prefetch=1, **specs))
    res = pl.pallas_call(
        full_body,
        name=name,
        out_shape=list(out_shape) + r_out,
        input_output_aliases=io_aliases,
        compiler_params=_params(("arbitrary",) * len(grid)),
        **specs,
    )(*lead, *args, *r_in)
    rider_outs, po = [], n_out
    for r in riders:
        rider_outs.append(list(res[po:po + len(r.out_shapes)]))
        po += len(r.out_shapes)
    return list(res[:n_out]), rider_outs


def _run_riders(name, riders):
    r_in = [a for r in riders for a in r.ins]
    r_out = [s for r in riders for s in r.out_shapes]
    r_sem = [s for r in riders for s in r.sem_shapes]

    def body(*refs):
        rin, rout, rsem = refs[:len(r_in)], refs[len(r_in):len(r_in) + len(r_out)], refs[len(r_in) + len(r_out):]
        pi = po = ps = 0
        for r in riders:
            starts, waits = r.build(rin[pi:pi + len(r.ins)], rout[po:po + len(r.out_shapes)], rsem[ps:ps + len(r.sem_shapes)])
            for cp in starts:
                cp.start()
            for wait in waits:
                wait()
            pi, po, ps = pi + len(r.ins), po + len(r.out_shapes), ps + len(r.sem_shapes)

    io_aliases = {}
    pi = po = 0
    for r in riders:
        if r.aliased:
            for q in range(len(r.ins)):
                io_aliases[pi + q] = po + q
        pi, po = pi + len(r.ins), po + len(r.out_shapes)
    res = pl.pallas_call(
        body,
        name=name,
        in_specs=[_ANY] * len(r_in),
        out_specs=[_ANY] * len(r_out),
        out_shape=r_out,
        scratch_shapes=r_sem,
        input_output_aliases=io_aliases,
    )(*r_in)
    outs, po = [], 0
    for r in riders:
        outs.append(list(res[po:po + len(r.out_shapes)]))
        po += len(r.out_shapes)
    return outs


def _piece(name):
    if name in BIG_PIECES:
        return (name, *BIG_PIECES[name], True)
    return (name, *SMALL_PIECES[name], False)


def _gather_rider(shards, names):
    pieces = [_piece(n) for n in names]
    n = len(pieces)

    def build(ins, outs, sems):
        local_sem, ici_send, ici_recv = sems
        _, _, c, s = _place()
        starts, waits = [], []
        for p, (_, kind, shard, split) in enumerate(pieces):
            cp = pltpu.make_async_copy(ins[p], _full_region(outs[p], kind, shard, s, None), local_sem.at[p])
            starts.append(cp)
            waits.append(cp.wait)
            h = c if split else None
            for k in (1, 2, 3):
                s2, dev = _other_chip(s, c, k)
                cp = _remote(_shard_region(ins[p], shard, h), _full_region(outs[p], kind, shard, s, h),
                             ici_send.at[p, k - 1], ici_recv.at[p, k - 1], dev)
                starts.append(cp)
                waits.append(cp.wait_send)
                region = _full_region(outs[p], kind, shard, s2, h)
                waits.append(_remote(region, region, ici_send.at[p, k - 1], ici_recv.at[p, k - 1], dev).wait_recv)
        return starts, waits

    return _Rider(
        [shards[name] for name in names],
        [jax.ShapeDtypeStruct(_full_shape(kind, shard), shards[name].dtype) for name, kind, shard, _ in pieces],
        [pltpu.SemaphoreType.DMA((n,)), pltpu.SemaphoreType.DMA((n, 3)), pltpu.SemaphoreType.DMA((n, 3))],
        build)


def _forward_rider(gathered, names):
    pieces = [_piece(n) for n in names]
    n = len(pieces)

    def build(ins, outs, sems):
        fwd_send, fwd_recv = sems
        x, y, c, s = _place()
        sibling = (x, y, 1 - c)
        starts, waits = [], []
        for p, (_, kind, shard, _) in enumerate(pieces):
            for k in (1, 2, 3):
                s2, _ = _other_chip(s, c, k)
                mine = _full_region(outs[p], kind, shard, s2, c)
                theirs = _full_region(outs[p], kind, shard, s2, 1 - c)
                cp = _remote(mine, mine, fwd_send.at[p, k - 1], fwd_recv.at[p, k - 1], sibling)
                starts.append(cp)
                waits.append(cp.wait_send)
                waits.append(_remote(theirs, theirs, fwd_send.at[p, k - 1], fwd_recv.at[p, k - 1], sibling).wait_recv)
        return starts, waits

    return _Rider(
        [gathered[name] for name in names],
        [jax.ShapeDtypeStruct(gathered[name].shape, gathered[name].dtype) for name in names],
        [pltpu.SemaphoreType.DMA((n, 3)), pltpu.SemaphoreType.DMA((n, 3))],
        build, aliased=True)


def _pair_exchange_rider(grads, names):
    n = len(names)

    def build(ins, outs, sems):
        send_sem, recv_sem = sems
        x, y, c, _ = _place()
        sibling = (x, y, 1 - c)
        starts, waits = [], []
        for p, name in enumerate(names):
            kind, shard = BIG_PIECES[name]
            for s2 in range(4):
                cp = _remote(_full_region(ins[p], kind, shard, s2, 1 - c), outs[p].at[s2], send_sem.at[p, s2],
                             recv_sem.at[p, s2], sibling)
                starts.append(cp)
                waits.append(cp.wait_send)
                waits.append(_remote(outs[p].at[s2], outs[p].at[s2], send_sem.at[p, s2], recv_sem.at[p, s2], sibling).wait_recv)
        return starts, waits

    return _Rider(
        [grads[name] for name in names],
        [jax.ShapeDtypeStruct((4,) + _half_shape(*BIG_PIECES[name]), F32) for name in names],
        [pltpu.SemaphoreType.DMA((n, 4))] * 2,
        build)


def _half_specs(kind, shard):
    half = shard[0] // 2
    if kind == "col":
        full = pl.BlockSpec((half, shard[1]), lambda j, pr: (pr[1], j))
        buf = pl.BlockSpec((None, half, shard[1]), lambda j, pr: (j, 0, 0))
    elif kind == "row":
        full = pl.BlockSpec((half, shard[1]), lambda j, pr: (2 * j + pr[1], 0))
        buf = pl.BlockSpec((None, half, shard[1]), lambda j, pr: (j, 0, 0))
    else:
        full = pl.BlockSpec((half, shard[1], shard[2]), lambda j, pr: (pr[1], j, 0))
        buf = pl.BlockSpec((None, half, shard[1], shard[2]), lambda j, pr: (j, 0, 0, 0))
    return full, buf


def _pair_sum(name, grad, recv, place):
    kind, shard = BIG_PIECES[name]
    full, buf = _half_specs(kind, shard)

    def body(pr, g_ref, r_ref, o_ref):
        o_ref[...] = (g_ref[...] + r_ref[...]).astype(BF16)

    return pl.pallas_call(
        body,
        name="pair_sum_" + name,
        grid_spec=pltpu.PrefetchScalarGridSpec(num_scalar_prefetch=1, grid=(4,), in_specs=[full, buf], out_specs=buf),
        out_shape=jax.ShapeDtypeStruct((4,) + _half_shape(kind, shard), BF16),
        compiler_params=_params(("arbitrary",)),
    )(place, grad, recv)


def _chip_exchange_rider(sums, names):
    n = len(names)

    def build(ins, outs, sems):
        send_sem, recv_sem = sems
        _, _, c, s = _place()
        starts, waits = [], []
        for p in range(n):
            for k in (1, 2, 3):
                s2, dev = _other_chip(s, c, k)
                cp = _remote(ins[p].at[s2], outs[p].at[k - 1], send_sem.at[p, k - 1], recv_sem.at[p, k - 1], dev)
                starts.append(cp)
                waits.append(cp.wait_send)
                waits.append(_remote(outs[p].at[k - 1], outs[p].at[k - 1], send_sem.at[p, k - 1], recv_sem.at[p, k - 1],
                                     dev).wait_recv)
        return starts, waits

    return _Rider(
        [sums[name] for name in names],
        [jax.ShapeDtypeStruct((3,) + _half_shape(*BIG_PIECES[name]), BF16) for name in names],
        [pltpu.SemaphoreType.DMA((n, 3))] * 2,
        build)


def _chip_sum(name, grad, recv_pair, recv_chip, place):
    kind, shard = BIG_PIECES[name]
    half = shard[0] // 2
    tail = tuple(shard[1:])
    zeros = (0,) * len(tail)
    if kind == "col":
        full = pl.BlockSpec((half,) + tail, lambda j, pr: (pr[1], pr[0]))
    elif kind == "row":
        full = pl.BlockSpec((half,) + tail, lambda j, pr: (2 * pr[0] + pr[1], 0))
    else:
        full = pl.BlockSpec((half,) + tail, lambda j, pr: (pr[1], pr[0], 0))
    pair = pl.BlockSpec((None, half) + tail, lambda j, pr: (pr[0], 0) + zeros)
    chip = pl.BlockSpec((3, half) + tail, lambda j, pr: (0, 0) + zeros)
    out = pl.BlockSpec((half,) + tail, lambda j, pr: (pr[1],) + zeros)

    def body(pr, g_ref, rp_ref, rc_ref, o_ref):
        total = g_ref[...] + rp_ref[...]
        for k in range(3):
            total = total + rc_ref[k].astype(F32)
        o_ref[...] = total

    return pl.pallas_call(
        body,
        name="chip_sum_" + name,
        grid_spec=pltpu.PrefetchScalarGridSpec(num_scalar_prefetch=1, grid=(1,), in_specs=[full, pair, chip], out_specs=out),
        out_shape=jax.ShapeDtypeStruct(shard, F32),
        compiler_params=_params(("arbitrary",)),
    )(place, grad, recv_pair, recv_chip)


def _sibling_exchange_rider(halves, names):
    n = len(names)

    def build(ins, outs, sems):
        send_sem, recv_sem = sems
        x, y, c, _ = _place()
        sibling = (x, y, 1 - c)
        starts, waits = [], []
        for p, name in enumerate(names):
            shard = BIG_PIECES[name][1]
            mine = _shard_region(outs[p], shard, c)
            theirs = _shard_region(outs[p], shard, 1 - c)
            cp = _remote(mine, mine, send_sem.at[p], recv_sem.at[p], sibling)
            starts.append(cp)
            waits.append(cp.wait_send)
            waits.append(_remote(theirs, theirs, send_sem.at[p], recv_sem.at[p], sibling).wait_recv)
        return starts, waits

    return _Rider(
        [halves[name] for name in names],
        [jax.ShapeDtypeStruct(BIG_PIECES[name][1], F32) for name in names],
        [pltpu.SemaphoreType.DMA((n,))] * 2,
        build, aliased=True)


FIRST_WEIGHTS = ["meta_tokens", "conv_w"]
LATE_WEIGHTS = [n for n in BIG_PIECES if n != "w_in"]
GRAD_GROUPS = {
    "ffn": ["w_ffn_in", "w_ffn_out"],
    "mixer": ["w_out", "w_branch_ret", "w_branch_lru", "lru_wa", "lru_wx"],
    "in": ["w_in"],
}
GRAD_SCHEDULE = {
    "mix_bwd": [("pair", "ffn")],
    "retention_bwd": [("chip", "ffn")],
    "lru_bwd": [("sibling", "ffn")],
    "dw_in": [("pair", "mixer")],
    "in_proj_bwd_0": [("chip", "mixer"), ("pair", "in")],
    "in_proj_bwd_1": [("sibling", "mixer"), ("chip", "in")],
}


class _CommPlan:
    def __init__(self, shards, place):
        self.shards, self.place = shards, place
        self.late = None
        self.recv_pair, self.sums, self.recv_chip, self.halves, self.final = {}, {}, {}, {}, {}

    def _grad_rider(self, stage, group, grads):
        names = GRAD_GROUPS[group]
        if stage == "pair":
            return _pair_exchange_rider(grads, names)
        if stage == "chip":
            return _chip_exchange_rider(self.sums, names)
        return _sibling_exchange_rider(self.halves, names)

    def _grad_after(self, stage, group, outs, grads):
        names = GRAD_GROUPS[group]
        if stage == "pair":
            for n, o in zip(names, outs):
                self.recv_pair[n] = o
                self.sums[n] = _pair_sum(n, grads[n], o, self.place)
        elif stage == "chip":
            for n, o in zip(names, outs):
                self.halves[n] = _chip_sum(n, grads[n], self.recv_pair[n], o, self.place)
        else:
            self.final.update(zip(names, outs))

    def riders(self, host, w, grads):
        if host == "in_proj":
            return [_gather_rider(self.shards, LATE_WEIGHTS)]
        if host == "retention_fwd":
            return [_forward_rider(self.late, LATE_WEIGHTS)]
        return [self._grad_rider(stage, group, grads) for stage, group in GRAD_SCHEDULE.get(host, [])]

    def after(self, host, rider_outs, w, grads):
        if host == "in_proj":
            self.late = dict(zip(LATE_WEIGHTS, rider_outs[0]))
        elif host == "retention_fwd":
            w.update(zip(LATE_WEIGHTS, rider_outs[0]))
        else:
            for (stage, group), outs in zip(GRAD_SCHEDULE.get(host, []), rider_outs):
                self._grad_after(stage, group, outs, grads)

    def finish(self):
        (outs,) = _run_riders("grad_tail_exchange", [_sibling_exchange_rider(self.halves, GRAD_GROUPS["in"])])
        self.final.update(zip(GRAD_GROUPS["in"], outs))
        return self.final


def _adamw_math(w, g, m, v):
    m = ADAM_B1 * m + (1.0 - ADAM_B1) * g
    v = ADAM_B2 * v + (1.0 - ADAM_B2) * (g * g)
    m_hat = m / (1.0 - ADAM_B1 ** ADAM_STEP)
    v_hat = v / (1.0 - ADAM_B2 ** ADAM_STEP)
    delta = -ADAM_LR * (m_hat / (jnp.sqrt(v_hat) + ADAM_EPS) + ADAM_WD * w)
    return delta, m, v


def _adamw(name, g, w, m, v):
    rows, cols = g.shape
    tr = rows // 4 if rows % 32 == 0 else rows

    def body(g_ref, w_ref, m_ref, v_ref, go_ref, d_ref, mo_ref, vo_ref):
        gv = g_ref[...]
        delta, m2, v2 = _adamw_math(w_ref[...], gv, m_ref[...], v_ref[...])
        go_ref[...] = gv
        d_ref[...] = delta
        mo_ref[...] = m2
        vo_ref[...] = v2

    spec = pl.BlockSpec((tr, cols), lambda i: (i, 0))
    return pl.pallas_call(
        body,
        name="adamw_" + name,
        grid=(rows // tr,),
        in_specs=[spec] * 4,
        out_specs=[spec] * 4,
        out_shape=[jax.ShapeDtypeStruct((rows, cols), F32)] * 4,
        compiler_params=_params(("arbitrary",)),
    )(g, w, m, v)


SMALL_ROWS = 48
VEC_ROWS = {"final_norm_w": 1, "ffn_norm_w": 8, "mix_norm_w": 16, "conv_b": 24, "lru_ba": 25, "lru_bx": 26, "lru_lambda": 27}
VEC_NAMES = list(VEC_ROWS)
CONV_W_ROW = 28
META_ROW = 32


def _small_all_reduce(partial):
    def body(in_ref, out_ref, buf, local_sem, send_sem, recv_sem):
        x, y, c, s = _place()
        me = 2 * s + c
        cp = pltpu.make_async_copy(in_ref, buf.at[me], local_sem)
        cp.start()
        started = []
        for k in range(1, 8):
            peer = jnp.bitwise_xor(me, k)
            dev = (peer // 4, (peer // 2) % 2, peer % 2)
            rd = _remote(in_ref, buf.at[me], send_sem.at[k - 1], recv_sem.at[k - 1], dev)
            rd.start()
            started.append(rd)
        for k in range(1, 8):
            peer = jnp.bitwise_xor(me, k)
            dev = (peer // 4, (peer // 2) % 2, peer % 2)
            _remote(in_ref, buf.at[peer], send_sem.at[k - 1], recv_sem.at[k - 1], dev).wait_recv()
        for rd in started:
            rd.wait_send()
        cp.wait()
        total = buf[0]
        for d in range(1, 8):
            total = total + buf[d]
        out_ref[...] = total

    return pl.pallas_call(
        body,
        name="small_all_reduce",
        in_specs=[pl.BlockSpec(memory_space=pltpu.VMEM)],
        out_specs=pl.BlockSpec(memory_space=pltpu.VMEM),
        out_shape=jax.ShapeDtypeStruct((SMALL_ROWS, D), F32),
        scratch_shapes=[pltpu.VMEM((8, SMALL_ROWS, D), F32), pltpu.SemaphoreType.DMA,
                        pltpu.SemaphoreType.DMA((7,)), pltpu.SemaphoreType.DMA((7,))],
    )(partial)


def _small_update(total, place, vecs, conv, meta):
    nvec = len(VEC_NAMES)
    qcols = D // 4

    def body(pr, tot_ref, col_ref, *refs):
        vec_refs = refs[:3 * nvec]
        conv_refs = refs[3 * nvec:3 * nvec + 3]
        meta_refs = refs[3 * nvec + 3:3 * nvec + 6]
        outs = refs[3 * nvec + 6:]
        loss_ref, vec_out, conv_out, meta_out = outs[0], outs[1:5], outs[5:9], outs[9:13]
        loss_ref[...] = jnp.sum(tot_ref[0:1, :], axis=1, keepdims=True)
        for o in vec_out:
            o[...] = jnp.zeros_like(o)
        for j, name in enumerate(VEC_NAMES):
            w, m, v = (r[...] for r in vec_refs[3 * j:3 * j + 3])
            g = tot_ref[VEC_ROWS[name]:VEC_ROWS[name] + 1, :]
            if name == "lru_lambda":
                g = -g / (1.0 + jnp.exp(w))
            for o, val in zip(vec_out, (g,) + _adamw_math(w, g, m, v)):
                o[j:j + 1, :] = val
        for row, n_rows, ins, group in ((CONV_W_ROW, 4, conv_refs, conv_out), (META_ROW, N_META, meta_refs, meta_out)):
            g = col_ref[row:row + n_rows, :]
            w, m, v = (r[...] for r in ins)
            for o, val in zip(group, (g,) + _adamw_math(w, g, m, v)):
                o[...] = val

    whole = lambda shape: pl.BlockSpec(shape, lambda i, pr: (0,) * len(shape))
    in_specs = [whole((SMALL_ROWS, D)), pl.BlockSpec((SMALL_ROWS, qcols), lambda i, pr: (0, pr[0]))]
    in_specs += [whole((1, D))] * (3 * nvec) + [whole((4, qcols))] * 3 + [whole((N_META, qcols))] * 3
    out_shapes = [(1, 1)] + [(8, D)] * 4 + [(4, qcols)] * 4 + [(N_META, qcols)] * 4
    return pl.pallas_call(
        body,
        name="small_update",
        grid_spec=pltpu.PrefetchScalarGridSpec(num_scalar_prefetch=1, grid=(1,), in_specs=in_specs,
                                               out_specs=[whole(s) for s in out_shapes]),
        out_shape=[jax.ShapeDtypeStruct(s, F32) for s in out_shapes],
        compiler_params=_params(("arbitrary",)),
    )(place, total, total, *[a for t in vecs for a in t], *conv, *meta)


WEIGHT_ORDER = ["meta_tokens", "mix_norm_w", "w_in", "conv_w", "conv_b", "lru_wa", "lru_ba", "lru_wx", "lru_bx", "lru_lambda",
                "w_branch_ret", "w_branch_lru", "w_out", "ffn_norm_w", "w_ffn_in", "w_ffn_out", "final_norm_w"]


def kernel(x, meta_tokens, mix_norm_w, w_in, conv_w, conv_b, lru_wa, lru_ba, lru_wx, lru_bx, lru_lambda, w_branch_ret, w_branch_lru, w_out, ffn_norm_w, w_ffn_in, w_ffn_out, final_norm_w, loss_target, m_meta_tokens, m_mix_norm_w, m_w_in, m_conv_w, m_conv_b, m_lru_wa, m_lru_ba, m_lru_wx, m_lru_bx, m_lru_lambda, m_w_branch_ret, m_w_branch_lru, m_w_out, m_ffn_norm_w, m_w_ffn_in, m_w_ffn_out, m_final_norm_w, v_meta_tokens, v_mix_norm_w, v_w_in, v_conv_w, v_conv_b, v_lru_wa, v_lru_ba, v_lru_wx, v_lru_bx, v_lru_lambda, v_w_branch_ret, v_w_branch_lru, v_w_out, v_ffn_norm_w, v_w_ffn_in, v_w_ffn_out, v_final_norm_w):
    args = locals()
    wts = {n: args[n] for n in WEIGHT_ORDER}
    mom = {n: args["m_" + n] for n in WEIGHT_ORDER}
    var = {n: args["v_" + n] for n in WEIGHT_ORDER}
    place = jnp.stack([2 * lax.axis_index("x") + lax.axis_index("y"), lax.axis_index("c")]).astype(jnp.int32)

    shards = {n: wts[n][0].astype(BF16) for n in BIG_PIECES}
    shards["meta_tokens"] = wts["meta_tokens"]
    shards["conv_w"] = wts["conv_w"][0]
    plan = _CommPlan(shards, place)
    (first,) = _run_riders("gather_first", [_gather_rider(shards, FIRST_WEIGHTS)])
    w = dict(zip(FIRST_WEIGHTS, first))
    for n in VEC_NAMES:
        w[n] = wts[n].reshape(1, D)
    chips = jnp.bitwise_xor(place[0], jnp.arange(4, dtype=jnp.int32))
    w["route"] = jnp.concatenate([place, jnp.stack([2 * chips, 2 * chips + 1], axis=1).reshape(8)])
    w["w_in_shard"] = shards["w_in"]

    grad_x, grad_head, _, stats = _local_step(x[0], loss_target[0], w, plan)
    shard_grads = plan.finish()

    out = {}
    for n in BIG_PIECES:
        shape2d = (-1, wts[n].shape[-1])
        res = _adamw(n, *[a.reshape(shape2d) for a in (shard_grads[n], wts[n], mom[n], var[n])])
        out[n] = [r.reshape(wts[n].shape) for r in res]

    partial = jnp.concatenate(stats + [grad_head[PAD_ROWS:]], axis=0)
    total = _small_all_reduce(partial)
    vecs = [tuple(a[n].reshape(1, D) for a in (wts, mom, var)) for n in VEC_NAMES]
    conv = tuple(a["conv_w"][0] for a in (wts, mom, var))
    meta = tuple(a["meta_tokens"] for a in (wts, mom, var))
    res = _small_update(total, place, vecs, conv, meta)
    loss = res[0].reshape(())
    for j, n in enumerate(VEC_NAMES):
        out[n] = [r[j].reshape(wts[n].shape) for r in res[1:5]]
    out["conv_w"] = [r.reshape(wts["conv_w"].shape) for r in res[5:9]]
    out["meta_tokens"] = list(res[9:13])

    return (loss, grad_x.reshape(x.shape)) + tuple(out[n][kind] for kind in range(4) for n in WEIGHT_ORDER)
```

```python
import functools
import math

import jax
import jax.numpy as jnp
from jax import lax
from jax.experimental import pallas as pl
from jax.experimental.pallas import tpu as pltpu

F32 = jnp.float32
BF16 = jnp.bfloat16

D = 1024
HEADS = 8
DH = 128
CHUNK = 128
N_META = 16
FRONT = 256
PAD_ROWS = FRONT - N_META
LRU_BLOCKS = 4
LRU_BLOCK = 256
LRU_C = 8.0
FFN = 2816
FFN_HALF = FFN // 2
IN_COLS = 8 * D
ROPE_BASE = 10000.0
EPS = 1e-6
QK_SCALE = DH ** -0.5

ADAM_LR = 0.001
ADAM_B1 = 0.9
ADAM_B2 = 0.999
ADAM_EPS = 1e-08
ADAM_WD = 0.01
ADAM_STEP = 10

TM = 256
TM_HEAVY = 768
FFN_BLOCK = FFN // 2
TM_MIX_BWD = 256
VMEM_LIMIT = 60 * 1024 * 1024

NT_DIMS = (((1,), (1,)), ((), ()))
TN_DIMS = (((0,), (0,)), ((), ()))
MESH = pl.DeviceIdType.MESH


def _params(sem=None):
    if sem is None:
        return pltpu.CompilerParams(vmem_limit_bytes=VMEM_LIMIT)
    return pltpu.CompilerParams(dimension_semantics=sem, vmem_limit_bytes=VMEM_LIMIT)


def _dot(a, b):
    return jnp.dot(a, b, preferred_element_type=F32)


def _dot_nt(a, b):
    return lax.dot_general(a, b, NT_DIMS, preferred_element_type=F32)


def _dot_tn(a, b):
    return lax.dot_general(a, b, TN_DIMS, preferred_element_type=F32)


def _sigmoid(z):
    return 1.0 / (1.0 + jnp.exp(-z))


def _log1p(x):
    return jnp.where(x < 1e-3, x * (1.0 - x * (0.5 - x * (1.0 / 3.0))), jnp.log(1.0 + x))


def _softplus(x):
    return jnp.maximum(x, 0.0) + _log1p(jnp.exp(-jnp.abs(x)))


def _neg_expm1(x):
    series = -x * (1.0 + x * (0.5 + x * (1.0 / 6.0 + x * (1.0 / 24.0 + x * (1.0 / 120.0)))))
    return jnp.where(x > -0.05, series, 1.0 - jnp.exp(x))


_GELU_K = math.sqrt(2.0 / math.pi)


def _gelu_and_grad(x):
    inner = _GELU_K * (x + 0.044715 * x * x * x)
    t = jnp.tanh(inner)
    val = 0.5 * x * (1.0 + t)
    grad = 0.5 * (1.0 + t) + 0.5 * x * (1.0 - t * t) * _GELU_K * (1.0 + 3.0 * 0.044715 * x * x)
    return val, grad


def _row_ids(i, rows, shape):
    return i * rows + lax.broadcasted_iota(jnp.int32, shape, 0)


def _rope_tables(rows):
    pos = jnp.arange(rows, dtype=jnp.int32) - PAD_ROWS
    inv_freq = ROPE_BASE ** (-jnp.arange(0, DH, 2, dtype=F32) / DH)
    ang = pos.astype(F32)[:, None] * inv_freq[None, :]
    cos, sin = jnp.cos(ang), jnp.sin(ang)
    return jnp.concatenate([cos, cos], axis=1), jnp.concatenate([-sin, sin], axis=1)


def _decay_tables():
    log_g = jnp.log(1.0 - 2.0 ** (-5.0 - jnp.arange(HEADS, dtype=F32)))
    idx = jnp.arange(CHUNK, dtype=F32)
    diff = idx[:, None] - idx[None, :]
    intra = jnp.where(diff[None] >= 0, jnp.exp(jnp.maximum(diff, 0.0)[None] * log_g[:, None, None]), 0.0)
    q_decay = jnp.exp((idx + 1.0)[:, None] * log_g[None, :])
    k_decay = jnp.exp((CHUNK - 1.0 - idx)[:, None] * log_g[None, :])
    chunk_decay = jnp.exp(CHUNK * log_g)
    wide = lambda a: jnp.repeat(a, DH, axis=-1)
    return intra, jnp.swapaxes(intra, 1, 2), wide(q_decay), wide(k_decay), wide(chunk_decay[None, :])


def _norm1(head, x2d, norm_w, riders=()):
    rows = FRONT + x2d.shape[0]

    def body(head_ref, x_ref, nw_ref, u_ref, rstd_ref):
        def norm(hv):
            rs = lax.rsqrt(jnp.mean(hv * hv, axis=-1, keepdims=True) + EPS)
            u_ref[...] = ((hv * rs) * nw_ref[...]).astype(BF16)
            rstd_ref[...] = rs

        @pl.when(pl.program_id(0) == 0)
        def _():
            norm(head_ref[...])

        @pl.when(pl.program_id(0) > 0)
        def _():
            norm(x_ref[...])

    return _hosted_call(
        body,
        name="norm1",
        grid=(rows // TM,),
        in_specs=[
            pl.BlockSpec((FRONT, D), lambda i: (0, 0)),
            pl.BlockSpec((TM, D), lambda i: (jnp.maximum(i - 1, 0), 0)),
            pl.BlockSpec((1, D), lambda i: (0, 0)),
        ],
        out_specs=[pl.BlockSpec((TM, D), lambda i: (i, 0)), pl.BlockSpec((TM, 1), lambda i: (i, 0))],
        out_shape=[jax.ShapeDtypeStruct((rows, D), BF16), jax.ShapeDtypeStruct((rows, 1), F32)],
        scratch_shapes=[],
        args=(head, x2d, norm_w),
        riders=riders,
    )


def _heavy_tile(rows):
    return TM_HEAVY if rows % TM_HEAVY == 0 else TM


def _in_proj(u, w_shard, route, cos_t, sin_t, riders=()):
    rows = u.shape[0]
    tm = _heavy_tile(rows)
    nt = rows // tm
    kind, shard = BIG_PIECES["w_in"]

    def body(route_ref, u_hbm, wsh_ref, cos_ref, sin_ref, proj_ref, wfull_ref,
             u_sc, w_sc, u_sem, w_sem, local_sem, ici_send, ici_recv, fwd_send, fwd_recv):
        g, i = pl.program_id(0), pl.program_id(1)
        s, c = route_ref[0], route_ref[1]
        gid = route_ref[2 + g]
        sibling = (s // 2, s % 2, 1 - c)
        local = pltpu.make_async_copy(wsh_ref, _full_region(wfull_ref, kind, shard, s, None), local_sem)
        u_copies = [pltpu.make_async_copy(u_hbm.at[pl.ds(t * tm, tm)], u_sc.at[t], u_sem.at[t]) for t in range(nt)]
        sends, arrivals = [], []
        for k in (1, 2, 3):
            s2, dev = _other_chip(s, c, k)
            sends.append(_remote(_shard_region(wsh_ref, shard, c), _full_region(wfull_ref, kind, shard, s, c),
                                 ici_send.at[k - 1], ici_recv.at[k - 1], dev))
            mine = _full_region(wfull_ref, kind, shard, s2, c)
            theirs = _full_region(wfull_ref, kind, shard, s2, 1 - c)
            arrivals.append((_remote(mine, mine, ici_send.at[k - 1], ici_recv.at[k - 1], dev),
                             _remote(mine, mine, fwd_send.at[k - 1], fwd_recv.at[k - 1], sibling),
                             _remote(theirs, theirs, fwd_send.at[k - 1], fwd_recv.at[k - 1], sibling)))

        @pl.when(jnp.logical_and(g == 0, i == 0))
        def _():
            for cp in sends:
                cp.start()
            local.start()
            for cp in u_copies:
                cp.start()

        for k, (arrived, forward, forwarded) in zip((1, 2, 3), arrivals):
            @pl.when(jnp.logical_and(g == 2 * k, i == 0))
            def _():
                arrived.wait_recv()
                forward.start()
                forwarded.wait_recv()

        @pl.when(jnp.logical_and(i == 0, g < 2))
        def _():
            cp = pltpu.make_async_copy(wsh_ref.at[:, pl.ds(pl.multiple_of(g * D, D), D)], w_sc, w_sem)
            cp.start()
            cp.wait()

        @pl.when(jnp.logical_and(i == 0, g >= 2))
        def _():
            cp = pltpu.make_async_copy(wfull_ref.at[:, pl.ds(pl.multiple_of(gid * D, D), D)], w_sc, w_sem)
            cp.start()
            cp.wait()

        for t in range(nt):
            @pl.when(jnp.logical_and(g == 0, i == t))
            def _():
                u_copies[t].wait()

        acc = _dot(u_sc[i], w_sc[...])

        @pl.when(gid < 2)
        def _():
            scale = jnp.where(gid == 1, QK_SCALE, 1.0).astype(F32)
            for h in range(HEADS):
                sl = slice(h * DH, (h + 1) * DH)
                blk = acc[:, sl]
                out = (blk * cos_ref[...] + pltpu.roll(blk, DH // 2, axis=1) * sin_ref[...]) * scale
                proj_ref[:, sl] = out.astype(BF16)

        @pl.when(gid >= 2)
        def _():
            proj_ref[...] = acc.astype(BF16)

        @pl.when(jnp.logical_and(g == 7, i == nt - 1))
        def _():
            local.wait()
            for cp in sends:
                cp.wait_send()
            for _, forward, _ in arrivals:
                forward.wait_send()

    return _hosted_call(
        body,
        name="in_proj",
        grid=(8, nt),
        in_specs=[
            _ANY, _ANY,
            pl.BlockSpec((tm, DH), lambda g, i, rt: (i, 0)),
            pl.BlockSpec((tm, DH), lambda g, i, rt: (i, 0)),
        ],
        out_specs=[pl.BlockSpec((tm, D), lambda g, i, rt: (i, rt[2 + g])), _ANY],
        out_shape=[jax.ShapeDtypeStruct((rows, IN_COLS), BF16), jax.ShapeDtypeStruct((D, IN_COLS), BF16)],
        scratch_shapes=[
            pltpu.VMEM((nt, tm, D), BF16), pltpu.VMEM((D, D), BF16),
            pltpu.SemaphoreType.DMA((nt,)), pltpu.SemaphoreType.DMA, pltpu.SemaphoreType.DMA,
            pltpu.SemaphoreType.DMA((3,)), pltpu.SemaphoreType.DMA((3,)),
            pltpu.SemaphoreType.DMA((3,)), pltpu.SemaphoreType.DMA((3,)),
        ],
        args=(u, w_shard, cos_t, sin_t),
        riders=riders,
        prefetch=route,
        riders_after_body=True,
    )


def _retention_fwd(proj_bf, intra, q_dec, k_dec, c_dec, riders=()):
    rows = proj_bf.shape[0]
    nc = rows // CHUNK

    def body(q_ref, k_ref, v_ref, m_ref, qd_ref, kd_ref, cd_ref, o_ref, sprev_ref, s_sc):
        @pl.when(pl.program_id(0) == 0)
        def _():
            s_sc[...] = jnp.zeros_like(s_sc)

        for h in range(HEADS):
            sl = slice(h * DH, (h + 1) * DH)
            q, k, v = q_ref[:, sl], k_ref[:, sl], v_ref[:, sl]
            state = s_sc[h]
            state_b = state.astype(BF16)
            sprev_ref[0, h] = state_b
            s = _dot_nt(q, k) * m_ref[h]
            inner = _dot(s.astype(BF16), v)
            cross = _dot(q, state_b) * qd_ref[:, sl]
            o_ref[:, sl] = (inner + cross).astype(BF16)
            k_scaled = (k.astype(F32) * kd_ref[:, sl]).astype(BF16)
            s_sc[h] = state * cd_ref[:, sl] + _dot_tn(k_scaled, v)

    chunk_spec = lambda col: pl.BlockSpec((CHUNK, D), lambda c: (c, col))
    const2 = lambda shape: pl.BlockSpec(shape, lambda c: (0, 0))
    return _hosted_call(
        body,
        name="retention_fwd",
        grid=(nc,),
        in_specs=[
            chunk_spec(0), chunk_spec(1), chunk_spec(2),
            pl.BlockSpec((HEADS, CHUNK, CHUNK), lambda c: (0, 0, 0)),
            const2((CHUNK, D)), const2((CHUNK, D)), const2((1, D)),
        ],
        out_specs=[
            pl.BlockSpec((CHUNK, D), lambda c: (c, 0)),
            pl.BlockSpec((1, HEADS, DH, DH), lambda c: (c, 0, 0, 0)),
        ],
        out_shape=[
            jax.ShapeDtypeStruct((rows, D), BF16),
            jax.ShapeDtypeStruct((nc, HEADS, DH, DH), BF16),
        ],
        scratch_shapes=[pltpu.VMEM((HEADS, DH, DH), F32)],
        args=(proj_bf, proj_bf, proj_bf, intra, q_dec, k_dec, c_dec),
        riders=riders,
    )


def _shift_down(x, first8_prev, d):
    rolled = pltpu.roll(x, d, axis=0)
    head = pltpu.roll(jnp.concatenate([first8_prev, x[0:8]], axis=0), d, axis=0)[8:16]
    return rolled, head


def _conv_and_gates(x, prev8, cw_ref, cb_ref, wa_ref, wx_ref, ba_ref, bx_ref, lam_ref, c_sc):
    cw = cw_ref[...]
    conv = cb_ref[...] + cw[3:4] * x
    head = cb_ref[...] + cw[3:4] * x[0:8]
    shifted = []
    for d in (1, 2, 3):
        rolled, hd = _shift_down(x, prev8, d)
        conv = conv + cw[3 - d:4 - d] * rolled
        head = head + cw[3 - d:4 - d] * hd
        shifted.append((rolled, hd))
    c_sc[...] = conv
    c_sc[0:8, :] = head
    c = c_sc[...]
    zr, zi = [], []
    for g in range(LRU_BLOCKS):
        sl = slice(g * LRU_BLOCK, (g + 1) * LRU_BLOCK)
        cg = c[:, sl].astype(BF16)
        zr.append(_dot(cg, wa_ref[g]))
        zi.append(_dot(cg, wx_ref[g]))
    r = _sigmoid(jnp.concatenate(zr, axis=1) + ba_ref[...])
    gate_i = _sigmoid(jnp.concatenate(zi, axis=1) + bx_ref[...])
    sp = _softplus(-lam_ref[...])
    log_a = (-LRU_C) * r * sp
    a = jnp.exp(log_a)
    mult = jnp.sqrt(_neg_expm1(2.0 * log_a))
    return c, r, gate_i, a, mult, sp, shifted


def _lru_fwd(proj, conv_w, conv_b, wa, wx, ba, bx, lam, riders=()):
    rows = proj.shape[0]
    nt = rows // TM

    def body(x_ref, cw_ref, cb_ref, wa_ref, wx_ref, ba_ref, bx_ref, lam_ref, h_ref, prev_sc, carry_sc, c_sc, a_sc, u_sc, h_sc):
        i = pl.program_id(0)

        @pl.when(i == 0)
        def _():
            prev_sc[...] = jnp.zeros_like(prev_sc)
            carry_sc[...] = jnp.zeros_like(carry_sc)

        x = x_ref[...].astype(F32)
        c, r, gate_i, a, mult, _, _ = _conv_and_gates(x, prev_sc[...], cw_ref, cb_ref, wa_ref, wx_ref, ba_ref, bx_ref, lam_ref, c_sc)
        prev_sc[...] = x[TM - 8:TM]
        valid = _row_ids(i, TM, (TM, D)) >= PAD_ROWS
        a_sc[...] = a
        u_sc[...] = jnp.where(valid, mult * gate_i * c, 0.0)
        row8 = lax.broadcasted_iota(jnp.int32, (8, D), 0)

        def group(gi, hprev):
            r0 = pl.multiple_of(gi * 8, 8)
            aa = a_sc[pl.ds(r0, 8), :]
            uu = u_sc[pl.ds(r0, 8), :]
            for d in (1, 2, 4):
                a_sh = jnp.where(row8 >= d, pltpu.roll(aa, d, axis=0), 1.0)
                u_sh = jnp.where(row8 >= d, pltpu.roll(uu, d, axis=0), 0.0)
                uu = uu + aa * u_sh
                aa = aa * a_sh
            hb = aa * hprev + uu
            h_sc[pl.ds(r0, 8), :] = hb
            return hb[7:8, :]

        hlast = lax.fori_loop(0, TM // 8, group, carry_sc[0:1, :])
        carry_sc[0:1, :] = hlast
        h_ref[...] = h_sc[...].astype(BF16)

    vec = pl.BlockSpec((1, D), lambda i: (0, 0))
    wspec = pl.BlockSpec((LRU_BLOCKS, LRU_BLOCK, LRU_BLOCK), lambda i: (0, 0, 0))
    return _hosted_call(
        body,
        name="lru_fwd",
        grid=(nt,),
        in_specs=[
            pl.BlockSpec((TM, D), lambda i: (i, 4)),
            pl.BlockSpec((4, D), lambda i: (0, 0)),
            vec, wspec, wspec, vec, vec, vec,
        ],
        out_specs=[pl.BlockSpec((TM, D), lambda i: (i, 0))],
        out_shape=[jax.ShapeDtypeStruct((rows, D), BF16)],
        scratch_shapes=[
            pltpu.VMEM((8, D), F32), pltpu.VMEM((8, D), F32),
            pltpu.VMEM((TM, D), F32), pltpu.VMEM((TM, D), F32), pltpu.VMEM((TM, D), F32), pltpu.VMEM((TM, D), F32),
        ],
        args=(proj, conv_w, conv_b, wa, wx, ba, bx, lam),
        riders=riders,
    )


def _group_norm(o):
    outs, rstds = [], []
    for h in range(HEADS):
        oh = o[:, h * DH:(h + 1) * DH]
        rs = lax.rsqrt(jnp.mean(oh * oh, axis=-1, keepdims=True) + EPS)
        outs.append(oh * rs)
        rstds.append(rs)
    return jnp.concatenate(outs, axis=1), rstds


def _mix_fwd(head, x2d, o, proj, h_lru, w_br, w_bl, w_o, ffn_norm_w, riders=()):
    rows = o.shape[0]
    nt = rows // TM

    def body(head_ref, x_ref, o_ref, gret_ref, lgate_ref, ga_ref, gb_ref, hl_ref, wbr_ref, wbl_ref, wo_ref, nw_ref,
             yret_ref, ylru_ref, h1_ref, u2_ref, rstd_ref):
        i = pl.program_id(0)
        on, _ = _group_norm(o_ref[...].astype(F32))
        gret = gret_ref[...].astype(F32)
        a_ret = (gret * _sigmoid(gret) * on).astype(BF16)
        y_ret = _dot(a_ret, wbr_ref[...])
        gl, _ = _gelu_and_grad(lgate_ref[...].astype(F32))
        a_lru = (gl * hl_ref[...].astype(F32)).astype(BF16)
        y_lru = _dot(a_lru, wbl_ref[...])
        mixed = (_sigmoid(ga_ref[...].astype(F32)) * y_ret + _sigmoid(gb_ref[...].astype(F32)) * y_lru).astype(BF16)
        delta = _dot(mixed, wo_ref[...])
        yret_ref[...] = y_ret.astype(BF16)
        ylru_ref[...] = y_lru.astype(BF16)

        def finish(h0):
            h1 = h0 + delta
            rs = lax.rsqrt(jnp.mean(h1 * h1, axis=-1, keepdims=True) + EPS)
            h1_ref[...] = h1
            u2_ref[...] = ((h1 * rs) * nw_ref[...]).astype(BF16)
            rstd_ref[...] = rs

        @pl.when(i == 0)
        def _():
            finish(head_ref[...])

        @pl.when(i > 0)
        def _():
            finish(x_ref[...])

    tile = lambda col: pl.BlockSpec((TM, D), lambda i: (i, col))
    wspec = pl.BlockSpec((D, D), lambda i: (0, 0))
    return _hosted_call(
        body,
        name="mix_fwd",
        grid=(nt,),
        in_specs=[
            pl.BlockSpec((FRONT, D), lambda i: (0, 0)),
            pl.BlockSpec((TM, D), lambda i: (jnp.maximum(i - 1, 0), 0)),
            tile(0), tile(3), tile(5), tile(6), tile(7), tile(0),
            wspec, wspec, wspec,
            pl.BlockSpec((1, D), lambda i: (0, 0)),
        ],
        out_specs=[tile(0), tile(0), tile(0), tile(0), pl.BlockSpec((TM, 1), lambda i: (i, 0))],
        out_shape=[
            jax.ShapeDtypeStruct((rows, D), BF16), jax.ShapeDtypeStruct((rows, D), BF16),
            jax.ShapeDtypeStruct((rows, D), F32), jax.ShapeDtypeStruct((rows, D), BF16),
            jax.ShapeDtypeStruct((rows, 1), F32),
        ],
        scratch_shapes=[],
        args=(head, x2d, o, proj, proj, proj, proj, h_lru, w_br, w_bl, w_o, ffn_norm_w),
        riders=riders,
    )


def _ffn_fwd(u2, h1, w_ffn_in, w_ffn_out):
    rows = u2.shape[0]
    tm = _heavy_tile(rows)
    nb = FFN // FFN_BLOCK
    hid = lambda: pl.BlockSpec((tm, FFN_BLOCK), lambda i, j: (i, j))

    def gate_body(u2_ref, wg_ref, wup_ref, g_ref, up_ref, act_ref):
        u2 = u2_ref[...]
        g = _dot(u2, wg_ref[...])
        up = _dot(u2, wup_ref[...])
        g_ref[...] = g.astype(BF16)
        up_ref[...] = up.astype(BF16)
        act_ref[...] = (g * _sigmoid(g) * up).astype(BF16)

    g, up, act = pl.pallas_call(
        gate_body,
        name="ffn_fwd_gate",
        grid=(rows // tm, nb),
        in_specs=[
            pl.BlockSpec((tm, D), lambda i, j: (i, 0)),
            pl.BlockSpec((D, FFN_BLOCK), lambda i, j: (0, j)),
            pl.BlockSpec((D, FFN_BLOCK), lambda i, j: (0, nb + j)),
        ],
        out_specs=[hid(), hid(), hid()],
        out_shape=[jax.ShapeDtypeStruct((rows, FFN), BF16)] * 3,
        compiler_params=_params(("arbitrary", "arbitrary")),
    )(u2, w_ffn_in, w_ffn_in)

    def out_body(act_ref, h1_ref, wo_ref, h2_ref):
        part = _dot(act_ref[...], wo_ref[...])

        @pl.when(pl.program_id(1) == 0)
        def _():
            h2_ref[...] = h1_ref[...] + part

        @pl.when(pl.program_id(1) > 0)
        def _():
            h2_ref[...] += part

    h2 = pl.pallas_call(
        out_body,
        name="ffn_fwd_out",
        grid=(rows // tm, nb),
        in_specs=[hid(), pl.BlockSpec((tm, D), lambda i, j: (i, 0)), pl.BlockSpec((FFN_BLOCK, D), lambda i, j: (j, 0))],
        out_specs=pl.BlockSpec((tm, D), lambda i, j: (i, 0)),
        out_shape=jax.ShapeDtypeStruct((rows, D), F32),
        compiler_params=_params(("arbitrary", "arbitrary")),
    )(act, h1, w_ffn_out)
    return g, up, act, h2


def _final_loss(h2, target, final_norm_w):
    rows = h2.shape[0]

    def body(h2_ref, tgt_ref, fnw_ref, dh2_ref, stats_ref):
        i = pl.program_id(0)

        @pl.when(i == 0)
        def _():
            stats_ref[...] = jnp.zeros_like(stats_ref)

        h2 = h2_ref[...]
        rs = lax.rsqrt(jnp.mean(h2 * h2, axis=-1, keepdims=True) + EPS)
        n = h2 * rs
        fnw = fnw_ref[...]
        valid = _row_ids(i, TM, (TM, D)) >= FRONT
        diff = jnp.where(valid, n * fnw - tgt_ref[...], 0.0)
        dy = diff * (1.0 / D)
        stats_ref[0:1, :] += (0.5 / D) * jnp.sum(diff * diff, axis=0, keepdims=True)
        stats_ref[1:2, :] += jnp.sum(dy * n, axis=0, keepdims=True)
        dn = dy * fnw
        dh2_ref[...] = rs * (dn - n * jnp.mean(dn * n, axis=-1, keepdims=True))

    return pl.pallas_call(
        body,
        name="final_loss",
        grid=(rows // TM,),
        in_specs=[
            pl.BlockSpec((TM, D), lambda i: (i, 0)),
            pl.BlockSpec((TM, D), lambda i: (jnp.maximum(i - 1, 0), 0)),
            pl.BlockSpec((1, D), lambda i: (0, 0)),
        ],
        out_specs=[pl.BlockSpec((TM, D), lambda i: (i, 0)), pl.BlockSpec((8, D), lambda i: (0, 0))],
        out_shape=[jax.ShapeDtypeStruct((rows, D), F32), jax.ShapeDtypeStruct((8, D), F32)],
        compiler_params=_params(("arbitrary",)),
    )(h2, target, final_norm_w)


def _ffn_bwd(dh2, g, up, h1, rstd2, w_ffn_in, w_ffn_out, ffn_norm_w):
    rows = dh2.shape[0]
    tm = _heavy_tile(rows)
    nb = FFN // FFN_BLOCK
    blk = lambda: pl.BlockSpec((tm, FFN_BLOCK), lambda i, j: (i, j))

    def gate_body(dh2_ref, g_ref, up_ref, wo_ref, dg_ref, dup_ref, dh2b_sc):
        @pl.when(pl.program_id(1) == 0)
        def _():
            dh2b_sc[...] = dh2_ref[...].astype(BF16)

        dact = _dot_nt(dh2b_sc[...], wo_ref[...])
        g = g_ref[...].astype(F32)
        up = up_ref[...].astype(F32)
        sg = _sigmoid(g)
        dup_ref[...] = (dact * (g * sg)).astype(BF16)
        dg_ref[...] = (dact * up * (sg * (1.0 + g * (1.0 - sg)))).astype(BF16)

    dg, dup = pl.pallas_call(
        gate_body,
        name="ffn_bwd_gate",
        grid=(rows // tm, nb),
        in_specs=[
            pl.BlockSpec((tm, D), lambda i, j: (i, 0)),
            blk(), blk(),
            pl.BlockSpec((FFN_BLOCK, D), lambda i, j: (j, 0)),
        ],
        out_specs=[blk(), blk()],
        out_shape=[jax.ShapeDtypeStruct((rows, FFN), BF16)] * 2,
        scratch_shapes=[pltpu.VMEM((tm, D), BF16)],
        compiler_params=_params(("arbitrary", "arbitrary")),
    )(dh2, g, up, w_ffn_out)

    def body(dg_ref, dup_ref, dh2_ref, h1_ref, rstd_ref, wg_ref, wup_ref, nw_ref, dh1_ref, stats_ref, du_sc):
        i = pl.program_id(0)
        j = pl.program_id(1)

        @pl.when(jnp.logical_and(i == 0, j == 0))
        def _():
            stats_ref[...] = jnp.zeros_like(stats_ref)

        part = _dot_nt(dg_ref[...], wg_ref[...]) + _dot_nt(dup_ref[...], wup_ref[...])

        @pl.when(j == 0)
        def _():
            du_sc[...] = part

        @pl.when(j > 0)
        def _():
            du_sc[...] += part

        @pl.when(j == nb - 1)
        def _():
            du = du_sc[...]
            rs = rstd_ref[...]
            n = h1_ref[...] * rs
            stats_ref[0:1, :] += jnp.sum(du * n, axis=0, keepdims=True)
            dn = du * nw_ref[...]
            dh1_ref[...] = dh2_ref[...] + rs * (dn - n * jnp.mean(dn * n, axis=-1, keepdims=True))

    dh1, stats = pl.pallas_call(
        body,
        name="ffn_bwd_in",
        grid=(rows // tm, nb),
        in_specs=[
            blk(), blk(),
            pl.BlockSpec((tm, D), lambda i, j: (i, 0)),
            pl.BlockSpec((tm, D), lambda i, j: (i, 0)),
            pl.BlockSpec((tm, 1), lambda i, j: (i, 0)),
            pl.BlockSpec((D, FFN_BLOCK), lambda i, j: (0, j)),
            pl.BlockSpec((D, FFN_BLOCK), lambda i, j: (0, nb + j)),
            pl.BlockSpec((1, D), lambda i, j: (0, 0)),
        ],
        out_specs=[pl.BlockSpec((tm, D), lambda i, j: (i, 0)), pl.BlockSpec((8, D), lambda i, j: (0, 0))],
        out_shape=[jax.ShapeDtypeStruct((rows, D), F32), jax.ShapeDtypeStruct((8, D), F32)],
        scratch_shapes=[pltpu.VMEM((tm, D), F32)],
        compiler_params=_params(("arbitrary", "arbitrary")),
    )(dg, dup, dh2, h1, rstd2, w_ffn_in, w_ffn_in, ffn_norm_w)
    return dg, dup, dh1, stats


def _mix_bwd(dh1, o, proj, h_lru, y_ret, y_lru, w_br, w_bl, w_o, riders=()):
    rows = dh1.shape[0]
    tm = TM_MIX_BWD
    nt = rows // tm

    def body(dh1_ref, o_ref, gret_ref, lgate_ref, ga_ref, gb_ref, hl_ref, yret_ref, ylru_ref, wbr_ref, wbl_ref, wo_ref,
             dproj_ref, do_ref, dhl_ref, mixed_ref, aret_ref, alru_ref, dyret_ref, dylru_ref):
        dmixed = _dot_nt(dh1_ref[...].astype(BF16), wo_ref[...])
        y_ret, y_lru = yret_ref[...].astype(F32), ylru_ref[...].astype(F32)
        sa, sb = _sigmoid(ga_ref[...].astype(F32)), _sigmoid(gb_ref[...].astype(F32))
        mixed_ref[...] = (sa * y_ret + sb * y_lru).astype(BF16)
        dga = dmixed * y_ret * sa * (1.0 - sa)
        dgb = dmixed * y_lru * sb * (1.0 - sb)
        dy_ret = (dmixed * sa).astype(BF16)
        dy_lru = (dmixed * sb).astype(BF16)
        dyret_ref[...] = dy_ret
        dylru_ref[...] = dy_lru
        da_ret = _dot_nt(dy_ret, wbr_ref[...])
        da_lru = _dot_nt(dy_lru, wbl_ref[...])

        gret = gret_ref[...].astype(F32)
        sg = _sigmoid(gret)
        silu = gret * sg
        on, rstds = _group_norm(o_ref[...].astype(F32))
        aret_ref[...] = (silu * on).astype(BF16)
        dgret = da_ret * on * (sg * (1.0 + gret * (1.0 - sg)))
        don = da_ret * silu
        for h in range(HEADS):
            sl = slice(h * DH, (h + 1) * DH)
            onh, donh = on[:, sl], don[:, sl]
            do_ref[:, sl] = (rstds[h] * (donh - onh * jnp.mean(donh * onh, axis=-1, keepdims=True))).astype(BF16)

        gl, gl_grad = _gelu_and_grad(lgate_ref[...].astype(F32))
        hl = hl_ref[...].astype(F32)
        alru_ref[...] = (gl * hl).astype(BF16)
        dlgate = da_lru * hl * gl_grad
        dhl_ref[...] = (da_lru * gl).astype(BF16)

        zeros = jnp.zeros((tm, D), BF16)
        for col in (0, 1, 2, 4):
            dproj_ref[:, col * D:(col + 1) * D] = zeros
        dproj_ref[:, 3 * D:4 * D] = dgret.astype(BF16)
        dproj_ref[:, 5 * D:6 * D] = dlgate.astype(BF16)
        dproj_ref[:, 6 * D:7 * D] = dga.astype(BF16)
        dproj_ref[:, 7 * D:8 * D] = dgb.astype(BF16)

    tile = lambda col: pl.BlockSpec((tm, D), lambda i: (i, col))
    wspec = pl.BlockSpec((D, D), lambda i: (0, 0))
    bf = lambda: jax.ShapeDtypeStruct((rows, D), BF16)
    return _hosted_call(
        body,
        name="mix_bwd",
        grid=(nt,),
        in_specs=[tile(0), tile(0), tile(3), tile(5), tile(6), tile(7), tile(0), tile(0), tile(0), wspec, wspec, wspec],
        out_specs=[pl.BlockSpec((tm, IN_COLS), lambda i: (i, 0))] + [tile(0)] * 7,
        out_shape=[jax.ShapeDtypeStruct((rows, IN_COLS), BF16)] + [bf() for _ in range(7)],
        scratch_shapes=[],
        args=(dh1, o, proj, proj, proj, proj, h_lru, y_ret, y_lru, w_br, w_bl, w_o),
        riders=riders,
    )


def _retention_bwd(dproj, proj_bf, do, sprev, intra, intra_t, q_dec, k_dec, c_dec, cos_t, sin_t, riders=()):
    rows = proj_bf.shape[0]
    nc = rows // CHUNK

    def body(dproj_in_ref, q_ref, k_ref, v_ref, do_ref, sprev_ref, m_ref, mt_ref, qd_ref, kd_ref, cd_ref, cos_ref, sin_ref,
             dproj_ref, ds_sc):
        @pl.when(pl.program_id(0) == 0)
        def _():
            ds_sc[...] = jnp.zeros_like(ds_sc)

        cos, sin = cos_ref[...], sin_ref[...]

        def unrotate(dy):
            return dy * cos - pltpu.roll(dy, DH // 2, axis=1) * sin

        for h in range(HEADS):
            sl = slice(h * DH, (h + 1) * DH)
            q, k, v = q_ref[:, sl], k_ref[:, sl], v_ref[:, sl]
            do = do_ref[:, sl]
            dob = do.astype(BF16)
            doq = (do * qd_ref[:, sl]).astype(BF16)
            state_prev = sprev_ref[0, h]
            dstate = ds_sc[h]
            dstate_b = dstate.astype(BF16)
            s_t = (_dot_nt(k, q) * mt_ref[h]).astype(BF16)
            ds_t = (_dot_nt(v, dob) * mt_ref[h]).astype(BF16)
            ds = (_dot_nt(dob, v) * m_ref[h]).astype(BF16)
            kd = kd_ref[:, sl]
            dq = _dot(ds, k) + _dot_nt(doq, state_prev)
            dk = _dot(ds_t, q) + _dot_nt(v, dstate_b) * kd
            k_scaled = (k.astype(F32) * kd).astype(BF16)
            dv = _dot(s_t, dob) + _dot(k_scaled, dstate_b)
            ds_sc[h] = dstate * cd_ref[:, sl] + _dot_tn(q, doq)
            dproj_ref[:, sl] = unrotate(dq).astype(BF16)
            dproj_ref[:, D + h * DH:D + (h + 1) * DH] = (unrotate(dk) * QK_SCALE).astype(BF16)
            dproj_ref[:, 2 * D + h * DH:2 * D + (h + 1) * DH] = dv.astype(BF16)

    rev = lambda c: nc - 1 - c
    chunk_spec = lambda col: pl.BlockSpec((CHUNK, D), lambda c: (rev(c), col))
    const2 = lambda shape: pl.BlockSpec(shape, lambda c: (0, 0))
    const3 = pl.BlockSpec((HEADS, CHUNK, CHUNK), lambda c: (0, 0, 0))
    return _hosted_call(
        body,
        name="retention_bwd",
        grid=(nc,),
        in_specs=[
            pl.BlockSpec(memory_space=pl.ANY),
            chunk_spec(0), chunk_spec(1), chunk_spec(2), chunk_spec(0),
            pl.BlockSpec((1, HEADS, DH, DH), lambda c: (rev(c), 0, 0, 0)),
            const3, const3,
            const2((CHUNK, D)), const2((CHUNK, D)), const2((1, D)),
            pl.BlockSpec((CHUNK, DH), lambda c: (rev(c), 0)),
            pl.BlockSpec((CHUNK, DH), lambda c: (rev(c), 0)),
        ],
        out_specs=[pl.BlockSpec((CHUNK, 3 * D), lambda c: (rev(c), 0))],
        out_shape=[jax.ShapeDtypeStruct(dproj.shape, BF16)],
        aliases={0: 0},
        scratch_shapes=[pltpu.VMEM((HEADS, DH, DH), F32)],
        args=(dproj, proj_bf, proj_bf, proj_bf, do, sprev, intra, intra_t, q_dec, k_dec, c_dec, cos_t, sin_t),
        riders=riders,
    )


def _lru_bwd(dproj, proj, h_lru, dhl, conv_w, conv_b, wa, wx, ba, bx, lam, riders=()):
    rows = proj.shape[0]
    nt = rows // TM
    per8 = TM // 8

    def body(dproj_in_ref, x_ref, xprev_ref, h_ref, hprev_ref, dhl_ref, cw_ref, cb_ref, wa_ref, wx_ref, ba_ref, bx_ref, lam_ref,
             dproj_ref, dwa_ref, dwx_ref, stats_ref, anext_sc, dhnext_sc, dcnext_sc, c_sc, b_sc, dh_sc):
        step = pl.program_id(0)
        i = nt - 1 - step

        @pl.when(step == 0)
        def _():
            anext_sc[...] = jnp.zeros_like(anext_sc)
            dhnext_sc[...] = jnp.zeros_like(dhnext_sc)
            dcnext_sc[...] = jnp.zeros_like(dcnext_sc)
            dwa_ref[...] = jnp.zeros_like(dwa_ref)
            dwx_ref[...] = jnp.zeros_like(dwx_ref)
            stats_ref[...] = jnp.zeros_like(stats_ref)

        first = i == 0
        x = x_ref[...].astype(F32)
        prev8 = jnp.where(first, 0.0, xprev_ref[8:16, :].astype(F32))
        c, r, gate_i, a, mult, sp, shifted = _conv_and_gates(x, prev8, cw_ref, cb_ref, wa_ref, wx_ref, ba_ref, bx_ref, lam_ref, c_sc)
        dh_sc[...] = dhl_ref[...].astype(F32)

        b_sc[...] = pltpu.roll(a, TM - 1, axis=0)
        b_sc[TM - 1:TM, :] = anext_sc[0:1, :]
        anext_sc[0:1, :] = a[0:1, :]
        row8 = lax.broadcasted_iota(jnp.int32, (8, D), 0)

        def group(gi, dhnext):
            r0 = pl.multiple_of((per8 - 1 - gi) * 8, 8)
            bb = b_sc[pl.ds(r0, 8), :]
            uu = dh_sc[pl.ds(r0, 8), :]
            for d in (1, 2, 4):
                b_sh = jnp.where(row8 < 8 - d, pltpu.roll(bb, 8 - d, axis=0), 1.0)
                u_sh = jnp.where(row8 < 8 - d, pltpu.roll(uu, 8 - d, axis=0), 0.0)
                uu = uu + bb * u_sh
                bb = bb * b_sh
            dhb = bb * dhnext + uu
            dh_sc[pl.ds(r0, 8), :] = dhb
            return dhb[0:1, :]

        dhfirst = lax.fori_loop(0, per8, group, dhnext_sc[0:1, :])
        dhnext_sc[0:1, :] = dhfirst
        dh = dh_sc[...]

        h = h_ref[...].astype(F32)
        hprev8 = jnp.where(first, 0.0, hprev_ref[8:16, :].astype(F32))
        h_dn, h_head = _shift_down(h, hprev8, 1)
        c_sc[...] = h_dn
        c_sc[0:8, :] = h_head
        h_before = c_sc[...]

        valid = _row_ids(i, TM, (TM, D)) >= PAD_ROWS
        da = dh * h_before
        du = jnp.where(valid, dh, 0.0)
        dmult = du * gate_i * c
        dgate_i = du * mult * c
        dc = du * mult * gate_i
        dla = da * a - dmult * (a * a) / mult
        dla = jnp.where(valid, dla, 0.0)
        dr = dla * ((-LRU_C) * sp)
        dzr = dr * r * (1.0 - r)
        dzi = dgate_i * gate_i * (1.0 - gate_i)
        stats_ref[1:2, :] += jnp.sum(dzr, axis=0, keepdims=True)
        stats_ref[2:3, :] += jnp.sum(dzi, axis=0, keepdims=True)
        stats_ref[3:4, :] += jnp.sum(dla * ((-LRU_C) * r), axis=0, keepdims=True)
        dc_gate = []
        for g in range(LRU_BLOCKS):
            sl = slice(g * LRU_BLOCK, (g + 1) * LRU_BLOCK)
            cg = c[:, sl].astype(BF16)
            dzr_g = dzr[:, sl].astype(BF16)
            dzi_g = dzi[:, sl].astype(BF16)
            dc_gate.append(_dot_nt(dzr_g, wa_ref[g]) + _dot_nt(dzi_g, wx_ref[g]))
            dwa_ref[g] += _dot_tn(cg, dzr_g)
            dwx_ref[g] += _dot_tn(cg, dzi_g)
        dc = dc + jnp.concatenate(dc_gate, axis=1)

        cw = cw_ref[...]
        stats_ref[0:1, :] += jnp.sum(dc, axis=0, keepdims=True)
        stats_ref[7:8, :] += jnp.sum(dc * x, axis=0, keepdims=True)
        dx = cw[3:4] * dc
        tail_src = jnp.concatenate([dc[TM - 8:TM], dcnext_sc[...]], axis=0)
        dx_tail = cw[3:4] * dc[TM - 8:TM]
        for d in (1, 2, 3):
            dx = dx + cw[3 - d:4 - d] * pltpu.roll(dc, TM - d, axis=0)
            dx_tail = dx_tail + cw[3 - d:4 - d] * pltpu.roll(tail_src, 16 - d, axis=0)[0:8]
            rolled, hd = shifted[d - 1]
            b_sc[...] = rolled
            b_sc[0:8, :] = hd
            stats_ref[7 - d:8 - d, :] += jnp.sum(dc * b_sc[...], axis=0, keepdims=True)
        dcnext_sc[...] = dc[0:8]
        dproj_ref[...] = dx.astype(BF16)
        dproj_ref[TM - 8:TM, :] = dx_tail.astype(BF16)

    rev = lambda s: nt - 1 - s
    vec = pl.BlockSpec((1, D), lambda s: (0, 0))
    wspec = pl.BlockSpec((LRU_BLOCKS, LRU_BLOCK, LRU_BLOCK), lambda s: (0, 0, 0))
    prev16 = lambda col: pl.BlockSpec((16, D), lambda s: (jnp.maximum(rev(s) * (TM // 16) - 1, 0), col))
    return _hosted_call(
        body,
        name="lru_bwd",
        grid=(nt,),
        in_specs=[
            pl.BlockSpec(memory_space=pl.ANY),
            pl.BlockSpec((TM, D), lambda s: (rev(s), 4)), prev16(4),
            pl.BlockSpec((TM, D), lambda s: (rev(s), 0)), prev16(0),
            pl.BlockSpec((TM, D), lambda s: (rev(s), 0)),
            pl.BlockSpec((4, D), lambda s: (0, 0)),
            vec, wspec, wspec, vec, vec, vec,
        ],
        out_specs=[
            pl.BlockSpec((TM, D), lambda s: (rev(s), 4)),
            wspec, wspec,
            pl.BlockSpec((8, D), lambda s: (0, 0)),
        ],
        out_shape=[
            jax.ShapeDtypeStruct(dproj.shape, BF16),
            jax.ShapeDtypeStruct((LRU_BLOCKS, LRU_BLOCK, LRU_BLOCK), F32),
            jax.ShapeDtypeStruct((LRU_BLOCKS, LRU_BLOCK, LRU_BLOCK), F32),
            jax.ShapeDtypeStruct((8, D), F32),
        ],
        aliases={0: 0},
        scratch_shapes=[
            pltpu.VMEM((8, D), F32), pltpu.VMEM((8, D), F32), pltpu.VMEM((8, D), F32),
            pltpu.VMEM((TM, D), F32), pltpu.VMEM((TM, D), F32), pltpu.VMEM((TM, D), F32),
        ],
        args=(dproj, proj, proj, h_lru, h_lru, dhl, conv_w, conv_b, wa, wx, ba, bx, lam),
        riders=riders,
    )


def _in_proj_bwd(dproj, w_in, part, prev=None, riders=()):
    rows = dproj.shape[0]
    tm = _heavy_tile(rows)
    nt = rows // tm
    first = 0 if part == 0 else (nt + 1) // 2
    count = (nt + 1) // 2 if part == 0 else nt - first

    def body(*refs):
        dproj_ref, w_hbm, du_ref, w_sc, w_sem = refs[-5:]

        @pl.when(pl.program_id(0) == 0)
        def _():
            cp = pltpu.make_async_copy(w_hbm, w_sc, w_sem)
            cp.start()
            cp.wait()

        du_ref[...] = _dot_nt(dproj_ref[...], w_sc[...])

    in_specs = [pl.BlockSpec((tm, IN_COLS), lambda i: (first + i, 0)), _ANY]
    args = (dproj, w_in)
    if prev is not None:
        in_specs = [_ANY] + in_specs
        args = (prev,) + args
    return _hosted_call(
        body,
        name="in_proj_bwd_%d" % part,
        grid=(count,),
        in_specs=in_specs,
        out_specs=[pl.BlockSpec((tm, D), lambda i: (first + i, 0))],
        out_shape=[jax.ShapeDtypeStruct((rows, D), F32)],
        scratch_shapes=[pltpu.VMEM((D, IN_COLS), BF16), pltpu.SemaphoreType.DMA],
        aliases={0: 0} if prev is not None else None,
        args=args,
        riders=riders,
    )


def _norm1_bwd(du, dh1, head, x2d, rstd1, norm_w, riders=()):
    rows = du.shape[0]

    def body(du_ref, dh1_ref, head_ref, x_ref, rstd_ref, nw_ref, gx_ref, ghead_ref, stats_ref):
        i = pl.program_id(0)

        @pl.when(i == 0)
        def _():
            stats_ref[...] = jnp.zeros_like(stats_ref)

        def finish(h0, out_ref):
            du = du_ref[...]
            rs = rstd_ref[...]
            n = h0 * rs
            stats_ref[0:1, :] += jnp.sum(du * n, axis=0, keepdims=True)
            dn = du * nw_ref[...]
            out_ref[...] = dh1_ref[...] + rs * (dn - n * jnp.mean(dn * n, axis=-1, keepdims=True))

        @pl.when(i == 0)
        def _():
            finish(head_ref[...], ghead_ref)

        @pl.when(i > 0)
        def _():
            finish(x_ref[...], gx_ref)

    tile = pl.BlockSpec((TM, D), lambda i: (i, 0))
    return _hosted_call(
        body,
        name="norm1_bwd",
        grid=(rows // TM,),
        in_specs=[
            tile, tile,
            pl.BlockSpec((FRONT, D), lambda i: (0, 0)),
            pl.BlockSpec((TM, D), lambda i: (jnp.maximum(i - 1, 0), 0)),
            pl.BlockSpec((TM, 1), lambda i: (i, 0)),
            pl.BlockSpec((1, D), lambda i: (0, 0)),
        ],
        out_specs=[
            pl.BlockSpec((TM, D), lambda i: (jnp.maximum(i - 1, 0), 0)),
            pl.BlockSpec((FRONT, D), lambda i: (0, 0)),
            pl.BlockSpec((8, D), lambda i: (0, 0)),
        ],
        out_shape=[
            jax.ShapeDtypeStruct(x2d.shape, F32),
            jax.ShapeDtypeStruct((FRONT, D), F32),
            jax.ShapeDtypeStruct((8, D), F32),
        ],
        scratch_shapes=[],
        args=(du, dh1, head, x2d, rstd1, norm_w),
        riders=riders,
    )


def _matmul_tn(name, x, dy, out_cols, col0=0, prev=None, k_block=None, n_block=None, riders=()):
    rows, kdim = x.shape
    ndim = dy.shape[1]
    tr = next(t for t in (1408, TM_HEAVY, TM) if rows % t == 0)
    kb = k_block or kdim
    nb = n_block or ndim
    nr, nk, nn = rows // tr, kdim // kb, ndim // nb
    cb0 = col0 // nb

    def body(*refs):
        x_ref, dy_ref, out_ref = refs[-3], refs[-2], refs[-1]
        part = _dot_tn(x_ref[...].astype(BF16), dy_ref[...].astype(BF16))

        @pl.when(pl.program_id(2) == 0)
        def _():
            out_ref[...] = part

        @pl.when(pl.program_id(2) > 0)
        def _():
            out_ref[...] += part

    in_specs = [
        pl.BlockSpec((tr, kb), lambda n, k, r: (r, k)),
        pl.BlockSpec((tr, nb), lambda n, k, r: (r, n)),
    ]
    args = [x, dy]
    aliases = {}
    if prev is not None:
        in_specs = [pl.BlockSpec(memory_space=pl.ANY)] + in_specs
        args = [prev] + args
        aliases = {0: 0}
    (out,), rider_outs = _hosted_call(
        body,
        name=name,
        grid=(nn, nk, nr),
        in_specs=in_specs,
        out_specs=[pl.BlockSpec((kb, nb), lambda n, k, r: (k, cb0 + n))],
        out_shape=[jax.ShapeDtypeStruct((kdim, out_cols), F32)],
        scratch_shapes=[],
        aliases=aliases,
        args=args,
        riders=riders,
    )
    return (out, rider_outs) if riders else out


def _local_step(x2d, target, w, plan):
    rows = FRONT + x2d.shape[0]
    head = jnp.concatenate([jnp.zeros((PAD_ROWS, D), F32), w["meta_tokens"]], axis=0)
    cos_t, sin_t = _rope_tables(rows)
    intra, intra_t, q_dec, k_dec, c_dec = _decay_tables()
    grads = {}

    def hosted(host, fn, *args, **kwargs):
        outs, rider_outs = fn(*args, riders=plan.riders(host, w, grads), **kwargs)
        plan.after(host, rider_outs, w, grads)
        return outs

    (u1, rstd1), _ = _norm1(head, x2d, w["mix_norm_w"])
    proj, w["w_in"] = hosted("in_proj", _in_proj, u1, w["w_in_shard"], w["route"], cos_t, sin_t)
    o, sprev = hosted("retention_fwd", _retention_fwd, proj, intra, q_dec, k_dec, c_dec)
    lru_args = (w["conv_w"], w["conv_b"], w["lru_wa"], w["lru_wx"], w["lru_ba"], w["lru_bx"], w["lru_lambda"])
    (h_lru,) = hosted("lru_fwd", _lru_fwd, proj, *lru_args)
    y_ret, y_lru, h1, u2, rstd2 = hosted("mix_fwd", _mix_fwd, head, x2d, o, proj, h_lru, w["w_branch_ret"],
                                         w["w_branch_lru"], w["w_out"], w["ffn_norm_w"])
    g, up, act, h2 = _ffn_fwd(u2, h1, w["w_ffn_in"], w["w_ffn_out"])
    dh2, stats_loss = _final_loss(h2, target, w["final_norm_w"])

    dg, dup, dh1, stats_ffn = _ffn_bwd(dh2, g, up, h1, rstd2, w["w_ffn_in"], w["w_ffn_out"], w["ffn_norm_w"])
    grads["w_ffn_in"] = _matmul_tn("dw_ffn_up", u2, dup, 2 * FFN, col0=FFN, n_block=FFN_HALF,
                                   prev=_matmul_tn("dw_ffn_gate", u2, dg, 2 * FFN, n_block=FFN_HALF))
    grads["w_ffn_out"] = _matmul_tn("dw_ffn_out", act, dh2, D, k_block=FFN_HALF)
    (dproj, do, dhl, mixed, a_ret, a_lru, dy_ret, dy_lru) = hosted(
        "mix_bwd", _mix_bwd, dh1, o, proj, h_lru, y_ret, y_lru, w["w_branch_ret"], w["w_branch_lru"], w["w_out"])
    grads["w_out"] = _matmul_tn("dw_out", mixed, dh1, D)
    grads["w_branch_ret"] = _matmul_tn("dw_branch_ret", a_ret, dy_ret, D)
    grads["w_branch_lru"] = _matmul_tn("dw_branch_lru", a_lru, dy_lru, D)
    (dproj,) = hosted("retention_bwd", _retention_bwd, dproj, proj, do, sprev, intra, intra_t, q_dec, k_dec, c_dec,
                      cos_t, sin_t)
    dproj, grads["lru_wa"], grads["lru_wx"], stats_lru = hosted("lru_bwd", _lru_bwd, dproj, proj, h_lru, dhl, *lru_args)
    grads["w_in"], rider_outs = _matmul_tn("dw_in", u1, dproj, IN_COLS, n_block=D, riders=plan.riders("dw_in", w, grads))
    plan.after("dw_in", rider_outs, w, grads)
    (du1,) = hosted("in_proj_bwd_0", _in_proj_bwd, dproj, w["w_in"], 0)
    (du1,) = hosted("in_proj_bwd_1", _in_proj_bwd, dproj, w["w_in"], 1, du1)
    (grad_x, grad_head, stats_in), _ = _norm1_bwd(du1, dh1, head, x2d, rstd1, w["mix_norm_w"])
    return grad_x, grad_head, grads, [stats_loss, stats_ffn, stats_in, stats_lru]


BIG_PIECES = {
    "w_in": ("col", (D, 2 * D)),
    "w_ffn_in": ("col", (D, FFN_HALF)),
    "w_ffn_out": ("row", (FFN // 4, D)),
    "w_branch_ret": ("row", (D // 4, D)),
    "w_branch_lru": ("row", (D // 4, D)),
    "w_out": ("row", (D // 4, D)),
    "lru_wa": ("lru", (LRU_BLOCKS, LRU_BLOCK // 4, LRU_BLOCK)),
    "lru_wx": ("lru", (LRU_BLOCKS, LRU_BLOCK // 4, LRU_BLOCK)),
}
SMALL_PIECES = {"meta_tokens": ("col", (N_META, D // 4)), "conv_w": ("col", (4, D // 4))}


def _full_shape(kind, shard):
    if kind == "col":
        return (shard[0], 4 * shard[1])
    if kind == "row":
        return (4 * shard[0], shard[1])
    return (shard[0], 4 * shard[1], shard[2])


def _half_shape(kind, shard):
    return (shard[0] // 2,) + tuple(shard[1:])


def _aligned(start, multiple):
    return start if isinstance(start, int) else pl.multiple_of(start, multiple)


def _lead(h, size):
    if h is None:
        return pl.ds(0, size)
    return pl.ds(_aligned(h * (size // 2), size // 2), size // 2)


def _full_region(ref, kind, shard, s, h):
    if kind == "col":
        return ref.at[_lead(h, shard[0]), pl.ds(_aligned(s * shard[1], shard[1]), shard[1])]
    if kind == "row":
        size = shard[0] if h is None else shard[0] // 2
        start = s * shard[0] + (0 if h is None else h * (shard[0] // 2))
        return ref.at[pl.ds(_aligned(start, 16), size), :]
    return ref.at[_lead(h, shard[0]), pl.ds(_aligned(s * shard[1], shard[1]), shard[1]), :]


def _shard_region(ref, shard, h):
    return ref.at[_lead(h, shard[0])]


def _place():
    x, y, c = lax.axis_index("x"), lax.axis_index("y"), lax.axis_index("c")
    return x, y, c, 2 * x + y


def _other_chip(s, c, k):
    s2 = jnp.bitwise_xor(s, k)
    return s2, (s2 // 2, s2 % 2, c)


def _remote(src, dst, send_sem, recv_sem, dev):
    return pltpu.make_async_remote_copy(src_ref=src, dst_ref=dst, send_sem=send_sem, recv_sem=recv_sem,
                                        device_id=dev, device_id_type=MESH)


_ANY = pl.BlockSpec(memory_space=pl.ANY)


class _Rider:
    def __init__(self, ins, out_shapes, sem_shapes, build, aliased=False):
        self.ins, self.out_shapes, self.sem_shapes, self.build, self.aliased = ins, out_shapes, sem_shapes, build, aliased


def _hosted_call(body, *, name, grid, in_specs, out_specs, out_shape, scratch_shapes, args, riders=(), aliases=None,
                 prefetch=None, riders_after_body=False):
    n_in, n_out, n_sc = len(in_specs), len(out_shape), len(scratch_shapes)
    r_in = [a for r in riders for a in r.ins]
    r_out = [s for r in riders for s in r.out_shapes]
    r_sem = [s for r in riders for s in r.sem_shapes]
    lead = () if prefetch is None else (prefetch,)
    assert prefetch is None or not (aliases or any(r.aliased for r in riders))

    def full_body(*refs):
        head, refs = refs[:len(lead)], refs[len(lead):]
        ins, rin = refs[:n_in], refs[n_in:n_in + len(r_in)]
        o0 = n_in + len(r_in)
        outs, rout = refs[o0:o0 + n_out], refs[o0 + n_out:o0 + n_out + len(r_out)]
        s0 = o0 + n_out + len(r_out)
        scratch, rsem = refs[s0:s0 + n_sc], refs[s0 + n_sc:]
        starts, waits = [], []
        pi = po = ps = 0
        for r in riders:
            st, wt = r.build(rin[pi:pi + len(r.ins)], rout[po:po + len(r.out_shapes)], rsem[ps:ps + len(r.sem_shapes)])
            starts += st
            waits += wt
            pi, po, ps = pi + len(r.ins), po + len(r.out_shapes), ps + len(r.sem_shapes)
        first = functools.reduce(jnp.logical_and, [pl.program_id(d) == 0 for d in range(len(grid))])
        last = functools.reduce(jnp.logical_and, [pl.program_id(d) == grid[d] - 1 for d in range(len(grid))])

        def start_riders():
            @pl.when(first)
            def _():
                for cp in starts:
                    cp.start()

        if riders and not riders_after_body:
            start_riders()
        body(*head, *ins, *outs, *scratch)
        if riders and riders_after_body:
            start_riders()
        if riders:
            @pl.when(last)
            def _():
                for wait in waits:
                    wait()

    io_aliases = dict(aliases or {})
    pi = po = 0
    for r in riders:
        if r.aliased:
            for q in range(len(r.ins)):
                io_aliases[n_in + pi + q] = n_out + po + q
        pi, po = pi + len(r.ins), po + len(r.out_shapes)
    specs = dict(
        grid=grid,
        in_specs=list(in_specs) + [_ANY] * len(r_in),
        out_specs=list(out_specs) + [_ANY] * len(r_out),
        scratch_shapes=list(scratch_shapes) + r_sem,
    )
    if prefetch is not None:
        specs = dict(grid_spec=pltpu.PrefetchScalarGridSpec(num_scalar_prefetch=1, **specs))
    res = pl.pallas_call(
        full_body,
        name=name,
        out_shape=list(out_shape) + r_out,
        input_output_aliases=io_aliases,
        compiler_params=_params(("arbitrary",) * len(grid)),
        **specs,
    )(*lead, *args, *r_in)
    rider_outs, po = [], n_out
    for r in riders:
        rider_outs.append(list(res[po:po + len(r.out_shapes)]))
        po += len(r.out_shapes)
    return list(res[:n_out]), rider_outs


def _run_riders(name, riders):
    r_in = [a for r in riders for a in r.ins]
    r_out = [s for r in riders for s in r.out_shapes]
    r_sem = [s for r in riders for s in r.sem_shapes]

    def body(*refs):
        rin, rout, rsem = refs[:len(r_in)], refs[len(r_in):len(r_in) + len(r_out)], refs[len(r_in) + len(r_out):]
        pi = po = ps = 0
        for r in riders:
            starts, waits = r.build(rin[pi:pi + len(r.ins)], rout[po:po + len(r.out_shapes)], rsem[ps:ps + len(r.sem_shapes)])
            for cp in starts:
                cp.start()
            for wait in waits:
                wait()
            pi, po, ps = pi + len(r.ins), po + len(r.out_shapes), ps + len(r.sem_shapes)

    io_aliases = {}
    pi = po = 0
    for r in riders:
        if r.aliased:
            for q in range(len(r.ins)):
                io_aliases[pi + q] = po + q
        pi, po = pi + len(r.ins), po + len(r.out_shapes)
    res = pl.pallas_call(
        body,
        name=name,
        in_specs=[_ANY] * len(r_in),
        out_specs=[_ANY] * len(r_out),
        out_shape=r_out,
        scratch_shapes=r_sem,
        input_output_aliases=io_aliases,
    )(*r_in)
    outs, po = [], 0
    for r in riders:
        outs.append(list(res[po:po + len(r.out_shapes)]))
        po += len(r.out_shapes)
    return outs


def _piece(name):
    if name in BIG_PIECES:
        return (name, *BIG_PIECES[name], True)
    return (name, *SMALL_PIECES[name], False)


def _gather_rider(shards, names):
    pieces = [_piece(n) for n in names]
    n = len(pieces)

    def build(ins, outs, sems):
        local_sem, ici_send, ici_recv = sems
        _, _, c, s = _place()
        starts, waits = [], []
        for p, (_, kind, shard, split) in enumerate(pieces):
            cp = pltpu.make_async_copy(ins[p], _full_region(outs[p], kind, shard, s, None), local_sem.at[p])
            starts.append(cp)
            waits.append(cp.wait)
            h = c if split else None
            for k in (1, 2, 3):
                s2, dev = _other_chip(s, c, k)
                cp = _remote(_shard_region(ins[p], shard, h), _full_region(outs[p], kind, shard, s, h),
                             ici_send.at[p, k - 1], ici_recv.at[p, k - 1], dev)
                starts.append(cp)
                waits.append(cp.wait_send)
                region = _full_region(outs[p], kind, shard, s2, h)
                waits.append(_remote(region, region, ici_send.at[p, k - 1], ici_recv.at[p, k - 1], dev).wait_recv)
        return starts, waits

    return _Rider(
        [shards[name] for name in names],
        [jax.ShapeDtypeStruct(_full_shape(kind, shard), shards[name].dtype) for name, kind, shard, _ in pieces],
        [pltpu.SemaphoreType.DMA((n,)), pltpu.SemaphoreType.DMA((n, 3)), pltpu.SemaphoreType.DMA((n, 3))],
        build)


def _forward_rider(gathered, names):
    pieces = [_piece(n) for n in names]
    n = len(pieces)

    def build(ins, outs, sems):
        fwd_send, fwd_recv = sems
        x, y, c, s = _place()
        sibling = (x, y, 1 - c)
        starts, waits = [], []
        for p, (_, kind, shard, _) in enumerate(pieces):
            for k in (1, 2, 3):
                s2, _ = _other_chip(s, c, k)
                mine = _full_region(outs[p], kind, shard, s2, c)
                theirs = _full_region(outs[p], kind, shard, s2, 1 - c)
                cp = _remote(mine, mine, fwd_send.at[p, k - 1], fwd_recv.at[p, k - 1], sibling)
                starts.append(cp)
                waits.append(cp.wait_send)
                waits.append(_remote(theirs, theirs, fwd_send.at[p, k - 1], fwd_recv.at[p, k - 1], sibling).wait_recv)
        return starts, waits

    return _Rider(
        [gathered[name] for name in names],
        [jax.ShapeDtypeStruct(gathered[name].shape, gathered[name].dtype) for name in names],
        [pltpu.SemaphoreType.DMA((n, 3)), pltpu.SemaphoreType.DMA((n, 3))],
        build, aliased=True)


def _pair_exchange_rider(grads, names):
    n = len(names)

    def build(ins, outs, sems):
        send_sem, recv_sem = sems
        x, y, c, _ = _place()
        sibling = (x, y, 1 - c)
        starts, waits = [], []
        for p, name in enumerate(names):
            kind, shard = BIG_PIECES[name]
            for s2 in range(4):
                cp = _remote(_full_region(ins[p], kind, shard, s2, 1 - c), outs[p].at[s2], send_sem.at[p, s2],
                             recv_sem.at[p, s2], sibling)
                starts.append(cp)
                waits.append(cp.wait_send)
                waits.append(_remote(outs[p].at[s2], outs[p].at[s2], send_sem.at[p, s2], recv_sem.at[p, s2], sibling).wait_recv)
        return starts, waits

    return _Rider(
        [grads[name] for name in names],
        [jax.ShapeDtypeStruct((4,) + _half_shape(*BIG_PIECES[name]), F32) for name in names],
        [pltpu.SemaphoreType.DMA((n, 4))] * 2,
        build)


def _half_specs(kind, shard):
    half = shard[0] // 2
    if kind == "col":
        full = pl.BlockSpec((half, shard[1]), lambda j, pr: (pr[1], j))
        buf = pl.BlockSpec((None, half, shard[1]), lambda j, pr: (j, 0, 0))
    elif kind == "row":
        full = pl.BlockSpec((half, shard[1]), lambda j, pr: (2 * j + pr[1], 0))
        buf = pl.BlockSpec((None, half, shard[1]), lambda j, pr: (j, 0, 0))
    else:
        full = pl.BlockSpec((half, shard[1], shard[2]), lambda j, pr: (pr[1], j, 0))
        buf = pl.BlockSpec((None, half, shard[1], shard[2]), lambda j, pr: (j, 0, 0, 0))
    return full, buf


def _pair_sum(name, grad, recv, place):
    kind, shard = BIG_PIECES[name]
    full, buf = _half_specs(kind, shard)

    def body(pr, g_ref, r_ref, o_ref):
        o_ref[...] = (g_ref[...] + r_ref[...]).astype(BF16)

    return pl.pallas_call(
        body,
        name="pair_sum_" + name,
        grid_spec=pltpu.PrefetchScalarGridSpec(num_scalar_prefetch=1, grid=(4,), in_specs=[full, buf], out_specs=buf),
        out_shape=jax.ShapeDtypeStruct((4,) + _half_shape(kind, shard), BF16),
        compiler_params=_params(("arbitrary",)),
    )(place, grad, recv)


def _chip_exchange_rider(sums, names):
    n = len(names)

    def build(ins, outs, sems):
        send_sem, recv_sem = sems
        _, _, c, s = _place()
        starts, waits = [], []
        for p in range(n):
            for k in (1, 2, 3):
                s2, dev = _other_chip(s, c, k)
                cp = _remote(ins[p].at[s2], outs[p].at[k - 1], send_sem.at[p, k - 1], recv_sem.at[p, k - 1], dev)
                starts.append(cp)
                waits.append(cp.wait_send)
                waits.append(_remote(outs[p].at[k - 1], outs[p].at[k - 1], send_sem.at[p, k - 1], recv_sem.at[p, k - 1],
                                     dev).wait_recv)
        return starts, waits

    return _Rider(
        [sums[name] for name in names],
        [jax.ShapeDtypeStruct((3,) + _half_shape(*BIG_PIECES[name]), BF16) for name in names],
        [pltpu.SemaphoreType.DMA((n, 3))] * 2,
        build)


def _chip_sum(name, grad, recv_pair, recv_chip, place):
    kind, shard = BIG_PIECES[name]
    half = shard[0] // 2
    tail = tuple(shard[1:])
    zeros = (0,) * len(tail)
    if kind == "col":
        full = pl.BlockSpec((half,) + tail, lambda j, pr: (pr[1], pr[0]))
    elif kind == "row":
        full = pl.BlockSpec((half,) + tail, lambda j, pr: (2 * pr[0] + pr[1], 0))
    else:
        full = pl.BlockSpec((half,) + tail, lambda j, pr: (pr[1], pr[0], 0))
    pair = pl.BlockSpec((None, half) + tail, lambda j, pr: (pr[0], 0) + zeros)
    chip = pl.BlockSpec((3, half) + tail, lambda j, pr: (0, 0) + zeros)
    out = pl.BlockSpec((half,) + tail, lambda j, pr: (pr[1],) + zeros)

    def body(pr, g_ref, rp_ref, rc_ref, o_ref):
        total = g_ref[...] + rp_ref[...]
        for k in range(3):
            total = total + rc_ref[k].astype(F32)
        o_ref[...] = total

    return pl.pallas_call(
        body,
        name="chip_sum_" + name,
        grid_spec=pltpu.PrefetchScalarGridSpec(num_scalar_prefetch=1, grid=(1,), in_specs=[full, pair, chip], out_specs=out),
        out_shape=jax.ShapeDtypeStruct(shard, F32),
        compiler_params=_params(("arbitrary",)),
    )(place, grad, recv_pair, recv_chip)


def _sibling_exchange_rider(halves, names):
    n = len(names)

    def build(ins, outs, sems):
        send_sem, recv_sem = sems
        x, y, c, _ = _place()
        sibling = (x, y, 1 - c)
        starts, waits = [], []
        for p, name in enumerate(names):
            shard = BIG_PIECES[name][1]
            mine = _shard_region(outs[p], shard, c)
            theirs = _shard_region(outs[p], shard, 1 - c)
            cp = _remote(mine, mine, send_sem.at[p], recv_sem.at[p], sibling)
            starts.append(cp)
            waits.append(cp.wait_send)
            waits.append(_remote(theirs, theirs, send_sem.at[p], recv_sem.at[p], sibling).wait_recv)
        return starts, waits

    return _Rider(
        [halves[name] for name in names],
        [jax.ShapeDtypeStruct(BIG_PIECES[name][1], F32) for name in names],
        [pltpu.SemaphoreType.DMA((n,))] * 2,
        build, aliased=True)


FIRST_WEIGHTS = ["meta_tokens", "conv_w"]
WEIGHT_GROUPS = {
    "mixer": ["lru_wa", "lru_wx", "w_branch_ret", "w_branch_lru", "w_out"],
    "ffn_in": ["w_ffn_in"],
    "ffn_out": ["w_ffn_out"],
}
WEIGHT_SCHEDULE = {
    "in_proj": [("gather", "mixer")],
    "retention_fwd": [("forward", "mixer"), ("gather", "ffn_in")],
    "lru_fwd": [("forward", "ffn_in"), ("gather", "ffn_out")],
    "mix_fwd": [("forward", "ffn_out")],
}
GRAD_GROUPS = {
    "ffn": ["w_ffn_in", "w_ffn_out"],
    "mixer": ["w_out", "w_branch_ret", "w_branch_lru", "lru_wa", "lru_wx"],
    "in": ["w_in"],
}
GRAD_SCHEDULE = {
    "mix_bwd": [("pair", "ffn")],
    "retention_bwd": [("chip", "ffn")],
    "lru_bwd": [("sibling", "ffn")],
    "dw_in": [("pair", "mixer")],
    "in_proj_bwd_0": [("chip", "mixer"), ("pair", "in")],
    "in_proj_bwd_1": [("sibling", "mixer"), ("chip", "in")],
}


class _CommPlan:
    def __init__(self, shards, place):
        self.shards, self.place = shards, place
        self.late = {}
        self.recv_pair, self.sums, self.recv_chip, self.halves, self.final = {}, {}, {}, {}, {}

    def _grad_rider(self, stage, group, grads):
        names = GRAD_GROUPS[group]
        if stage == "pair":
            return _pair_exchange_rider(grads, names)
        if stage == "chip":
            return _chip_exchange_rider(self.sums, names)
        return _sibling_exchange_rider(self.halves, names)

    def _grad_after(self, stage, group, outs, grads):
        names = GRAD_GROUPS[group]
        if stage == "pair":
            for n, o in zip(names, outs):
                self.recv_pair[n] = o
                self.sums[n] = _pair_sum(n, grads[n], o, self.place)
        elif stage == "chip":
            for n, o in zip(names, outs):
                self.halves[n] = _chip_sum(n, grads[n], self.recv_pair[n], o, self.place)
        else:
            self.final.update(zip(names, outs))

    def riders(self, host, w, grads):
        if host in WEIGHT_SCHEDULE:
            return [_gather_rider(self.shards, WEIGHT_GROUPS[group]) if stage == "gather"
                    else _forward_rider(self.late, WEIGHT_GROUPS[group]) for stage, group in WEIGHT_SCHEDULE[host]]
        return [self._grad_rider(stage, group, grads) for stage, group in GRAD_SCHEDULE.get(host, [])]

    def after(self, host, rider_outs, w, grads):
        for (stage, group), outs in zip(WEIGHT_SCHEDULE.get(host, []), rider_outs):
            (self.late if stage == "gather" else w).update(zip(WEIGHT_GROUPS[group], outs))
        for (stage, group), outs in zip(GRAD_SCHEDULE.get(host, []), rider_outs):
            self._grad_after(stage, group, outs, grads)

    def finish(self):
        (outs,) = _run_riders("grad_tail_exchange", [_sibling_exchange_rider(self.halves, GRAD_GROUPS["in"])])
        self.final.update(zip(GRAD_GROUPS["in"], outs))
        return self.final


def _adamw_math(w, g, m, v):
    m = ADAM_B1 * m + (1.0 - ADAM_B1) * g
    v = ADAM_B2 * v + (1.0 - ADAM_B2) * (g * g)
    m_hat = m / (1.0 - ADAM_B1 ** ADAM_STEP)
    v_hat = v / (1.0 - ADAM_B2 ** ADAM_STEP)
    delta = -ADAM_LR * (m_hat / (jnp.sqrt(v_hat) + ADAM_EPS) + ADAM_WD * w)
    return delta, m, v


def _adamw(name, g, w, m, v):
    rows, cols = g.shape
    tr = rows // 4 if rows % 32 == 0 else rows

    def body(g_ref, w_ref, m_ref, v_ref, go_ref, d_ref, mo_ref, vo_ref):
        gv = g_ref[...]
        delta, m2, v2 = _adamw_math(w_ref[...], gv, m_ref[...], v_ref[...])
        go_ref[...] = gv
        d_ref[...] = delta
        mo_ref[...] = m2
        vo_ref[...] = v2

    spec = pl.BlockSpec((tr, cols), lambda i: (i, 0))
    return pl.pallas_call(
        body,
        name="adamw_" + name,
        grid=(rows // tr,),
        in_specs=[spec] * 4,
        out_specs=[spec] * 4,
        out_shape=[jax.ShapeDtypeStruct((rows, cols), F32)] * 4,
        compiler_params=_params(("arbitrary",)),
    )(g, w, m, v)


SMALL_ROWS = 48
VEC_ROWS = {"final_norm_w": 1, "ffn_norm_w": 8, "mix_norm_w": 16, "conv_b": 24, "lru_ba": 25, "lru_bx": 26, "lru_lambda": 27}
VEC_NAMES = list(VEC_ROWS)
CONV_W_ROW = 28
META_ROW = 32


def _small_all_reduce(partial):
    def body(in_ref, out_ref, buf, local_sem, send_sem, recv_sem):
        x, y, c, s = _place()
        me = 2 * s + c
        cp = pltpu.make_async_copy(in_ref, buf.at[me], local_sem)
        cp.start()
        started = []
        for k in range(1, 8):
            peer = jnp.bitwise_xor(me, k)
            dev = (peer // 4, (peer // 2) % 2, peer % 2)
            rd = _remote(in_ref, buf.at[me], send_sem.at[k - 1], recv_sem.at[k - 1], dev)
            rd.start()
            started.append(rd)
        for k in range(1, 8):
            peer = jnp.bitwise_xor(me, k)
            dev = (peer // 4, (peer // 2) % 2, peer % 2)
            _remote(in_ref, buf.at[peer], send_sem.at[k - 1], recv_sem.at[k - 1], dev).wait_recv()
        for rd in started:
            rd.wait_send()
        cp.wait()
        total = buf[0]
        for d in range(1, 8):
            total = total + buf[d]
        out_ref[...] = total

    return pl.pallas_call(
        body,
        name="small_all_reduce",
        in_specs=[pl.BlockSpec(memory_space=pltpu.VMEM)],
        out_specs=pl.BlockSpec(memory_space=pltpu.VMEM),
        out_shape=jax.ShapeDtypeStruct((SMALL_ROWS, D), F32),
        scratch_shapes=[pltpu.VMEM((8, SMALL_ROWS, D), F32), pltpu.SemaphoreType.DMA,
                        pltpu.SemaphoreType.DMA((7,)), pltpu.SemaphoreType.DMA((7,))],
    )(partial)


def _small_update(total, place, vecs, conv, meta):
    nvec = len(VEC_NAMES)
    qcols = D // 4

    def body(pr, tot_ref, col_ref, *refs):
        vec_refs = refs[:3 * nvec]
        conv_refs = refs[3 * nvec:3 * nvec + 3]
        meta_refs = refs[3 * nvec + 3:3 * nvec + 6]
        outs = refs[3 * nvec + 6:]
        loss_ref, vec_out, conv_out, meta_out = outs[0], outs[1:5], outs[5:9], outs[9:13]
        loss_ref[...] = jnp.sum(tot_ref[0:1, :], axis=1, keepdims=True)
        for o in vec_out:
            o[...] = jnp.zeros_like(o)
        for j, name in enumerate(VEC_NAMES):
            w, m, v = (r[...] for r in vec_refs[3 * j:3 * j + 3])
            g = tot_ref[VEC_ROWS[name]:VEC_ROWS[name] + 1, :]
            if name == "lru_lambda":
                g = -g / (1.0 + jnp.exp(w))
            for o, val in zip(vec_out, (g,) + _adamw_math(w, g, m, v)):
                o[j:j + 1, :] = val
        for row, n_rows, ins, group in ((CONV_W_ROW, 4, conv_refs, conv_out), (META_ROW, N_META, meta_refs, meta_out)):
            g = col_ref[row:row + n_rows, :]
            w, m, v = (r[...] for r in ins)
            for o, val in zip(group, (g,) + _adamw_math(w, g, m, v)):
                o[...] = val

    whole = lambda shape: pl.BlockSpec(shape, lambda i, pr: (0,) * len(shape))
    in_specs = [whole((SMALL_ROWS, D)), pl.BlockSpec((SMALL_ROWS, qcols), lambda i, pr: (0, pr[0]))]
    in_specs += [whole((1, D))] * (3 * nvec) + [whole((4, qcols))] * 3 + [whole((N_META, qcols))] * 3
    out_shapes = [(1, 1)] + [(8, D)] * 4 + [(4, qcols)] * 4 + [(N_META, qcols)] * 4
    return pl.pallas_call(
        body,
        name="small_update",
        grid_spec=pltpu.PrefetchScalarGridSpec(num_scalar_prefetch=1, grid=(1,), in_specs=in_specs,
                                               out_specs=[whole(s) for s in out_shapes]),
        out_shape=[jax.ShapeDtypeStruct(s, F32) for s in out_shapes],
        compiler_params=_params(("arbitrary",)),
    )(place, total, total, *[a for t in vecs for a in t], *conv, *meta)


WEIGHT_ORDER = ["meta_tokens", "mix_norm_w", "w_in", "conv_w", "conv_b", "lru_wa", "lru_ba", "lru_wx", "lru_bx", "lru_lambda",
                "w_branch_ret", "w_branch_lru", "w_out", "ffn_norm_w", "w_ffn_in", "w_ffn_out", "final_norm_w"]


def kernel(x, meta_tokens, mix_norm_w, w_in, conv_w, conv_b, lru_wa, lru_ba, lru_wx, lru_bx, lru_lambda, w_branch_ret, w_branch_lru, w_out, ffn_norm_w, w_ffn_in, w_ffn_out, final_norm_w, loss_target, m_meta_tokens, m_mix_norm_w, m_w_in, m_conv_w, m_conv_b, m_lru_wa, m_lru_ba, m_lru_wx, m_lru_bx, m_lru_lambda, m_w_branch_ret, m_w_branch_lru, m_w_out, m_ffn_norm_w, m_w_ffn_in, m_w_ffn_out, m_final_norm_w, v_meta_tokens, v_mix_norm_w, v_w_in, v_conv_w, v_conv_b, v_lru_wa, v_lru_ba, v_lru_wx, v_lru_bx, v_lru_lambda, v_w_branch_ret, v_w_branch_lru, v_w_out, v_ffn_norm_w, v_w_ffn_in, v_w_ffn_out, v_final_norm_w):
    args = locals()
    wts = {n: args[n] for n in WEIGHT_ORDER}
    mom = {n: args["m_" + n] for n in WEIGHT_ORDER}
    var = {n: args["v_" + n] for n in WEIGHT_ORDER}
    place = jnp.stack([2 * lax.axis_index("x") + lax.axis_index("y"), lax.axis_index("c")]).astype(jnp.int32)

    shards = {n: wts[n][0].astype(BF16) for n in BIG_PIECES}
    shards["meta_tokens"] = wts["meta_tokens"]
    shards["conv_w"] = wts["conv_w"][0]
    plan = _CommPlan(shards, place)
    (first,) = _run_riders("gather_first", [_gather_rider(shards, FIRST_WEIGHTS)])
    w = dict(zip(FIRST_WEIGHTS, first))
    for n in VEC_NAMES:
        w[n] = wts[n].reshape(1, D)
    chips = jnp.bitwise_xor(place[0], jnp.arange(4, dtype=jnp.int32))
    w["route"] = jnp.concatenate([place, jnp.stack([2 * chips, 2 * chips + 1], axis=1).reshape(8)])
    w["w_in_shard"] = shards["w_in"]

    grad_x, grad_head, _, stats = _local_step(x[0], loss_target[0], w, plan)
    shard_grads = plan.finish()

    out = {}
    for n in BIG_PIECES:
        shape2d = (-1, wts[n].shape[-1])
        res = _adamw(n, *[a.reshape(shape2d) for a in (shard_grads[n], wts[n], mom[n], var[n])])
        out[n] = [r.reshape(wts[n].shape) for r in res]

    partial = jnp.concatenate(stats + [grad_head[PAD_ROWS:]], axis=0)
    total = _small_all_reduce(partial)
    vecs = [tuple(a[n].reshape(1, D) for a in (wts, mom, var)) for n in VEC_NAMES]
    conv = tuple(a["conv_w"][0] for a in (wts, mom, var))
    meta = tuple(a["meta_tokens"] for a in (wts, mom, var))
    res = _small_update(total, place, vecs, conv, meta)
    loss = res[0].reshape(())
    for j, n in enumerate(VEC_NAMES):
        out[n] = [r[j].reshape(wts[n].shape) for r in res[1:5]]
    out["conv_w"] = [r.reshape(wts["conv_w"].shape) for r in res[5:9]]
    out["meta_tokens"] = list(res[9:13])

    return (loss, grad_x.reshape(x.shape)) + tuple(out[n][kind] for kind in range(4) for n in WEIGHT_ORDER)
```

```python
import functools
import math

import jax
import jax.numpy as jnp
from jax import lax
from jax.experimental import pallas as pl
from jax.experimental.pallas import tpu as pltpu

F32 = jnp.float32
BF16 = jnp.bfloat16

D = 1024
HEADS = 8
DH = 128
CHUNK = 256
N_META = 16
FRONT = 256
PAD_ROWS = FRONT - N_META
LRU_BLOCKS = 4
LRU_BLOCK = 256
LRU_C = 8.0
FFN = 2816
FFN_HALF = FFN // 2
IN_COLS = 8 * D
ROPE_BASE = 10000.0
EPS = 1e-6
QK_SCALE = DH ** -0.5

ADAM_LR = 0.001
ADAM_B1 = 0.9
ADAM_B2 = 0.999
ADAM_EPS = 1e-08
ADAM_WD = 0.01
ADAM_STEP = 10

TM = 256
TM_HEAVY = 768
FFN_BLOCK = FFN // 2
TM_MIX_BWD = 256
VMEM_LIMIT = 60 * 1024 * 1024

NT_DIMS = (((1,), (1,)), ((), ()))
TN_DIMS = (((0,), (0,)), ((), ()))
MESH = pl.DeviceIdType.MESH


def _params(sem=None):
    if sem is None:
        return pltpu.CompilerParams(vmem_limit_bytes=VMEM_LIMIT)
    return pltpu.CompilerParams(dimension_semantics=sem, vmem_limit_bytes=VMEM_LIMIT)


def _dot(a, b):
    return jnp.dot(a, b, preferred_element_type=F32)


def _dot_nt(a, b):
    return lax.dot_general(a, b, NT_DIMS, preferred_element_type=F32)


def _dot_tn(a, b):
    return lax.dot_general(a, b, TN_DIMS, preferred_element_type=F32)


def _sigmoid(z):
    return 1.0 / (1.0 + jnp.exp(-z))


def _log1p(x):
    return jnp.where(x < 1e-3, x * (1.0 - x * (0.5 - x * (1.0 / 3.0))), jnp.log(1.0 + x))


def _softplus(x):
    return jnp.maximum(x, 0.0) + _log1p(jnp.exp(-jnp.abs(x)))


def _neg_expm1(x):
    series = -x * (1.0 + x * (0.5 + x * (1.0 / 6.0 + x * (1.0 / 24.0 + x * (1.0 / 120.0)))))
    return jnp.where(x > -0.05, series, 1.0 - jnp.exp(x))


_GELU_K = math.sqrt(2.0 / math.pi)


def _gelu_and_grad(x):
    inner = _GELU_K * (x + 0.044715 * x * x * x)
    t = jnp.tanh(inner)
    val = 0.5 * x * (1.0 + t)
    grad = 0.5 * (1.0 + t) + 0.5 * x * (1.0 - t * t) * _GELU_K * (1.0 + 3.0 * 0.044715 * x * x)
    return val, grad


def _row_ids(i, rows, shape):
    return i * rows + lax.broadcasted_iota(jnp.int32, shape, 0)


def _rope_tables(rows):
    pos = jnp.arange(rows, dtype=jnp.int32) - PAD_ROWS
    inv_freq = ROPE_BASE ** (-jnp.arange(0, DH, 2, dtype=F32) / DH)
    ang = pos.astype(F32)[:, None] * inv_freq[None, :]
    cos, sin = jnp.cos(ang), jnp.sin(ang)
    return jnp.concatenate([cos, cos], axis=1), jnp.concatenate([-sin, sin], axis=1)


def _decay_tables():
    log_g = jnp.log(1.0 - 2.0 ** (-5.0 - jnp.arange(HEADS, dtype=F32)))
    idx = jnp.arange(CHUNK, dtype=F32)
    diff = idx[:, None] - idx[None, :]
    intra = jnp.where(diff[None] >= 0, jnp.exp(jnp.maximum(diff, 0.0)[None] * log_g[:, None, None]), 0.0)
    q_decay = jnp.exp((idx + 1.0)[:, None] * log_g[None, :])
    k_decay = jnp.exp((CHUNK - 1.0 - idx)[:, None] * log_g[None, :])
    chunk_decay = jnp.exp(CHUNK * log_g)
    wide = lambda a: jnp.repeat(a, DH, axis=-1)
    return intra, jnp.swapaxes(intra, 1, 2), wide(q_decay), wide(k_decay), wide(chunk_decay[None, :])


def _norm1(head, x2d, norm_w, riders=()):
    rows = FRONT + x2d.shape[0]

    def body(head_ref, x_ref, nw_ref, u_ref, rstd_ref):
        def norm(hv):
            rs = lax.rsqrt(jnp.mean(hv * hv, axis=-1, keepdims=True) + EPS)
            u_ref[...] = ((hv * rs) * nw_ref[...]).astype(BF16)
            rstd_ref[...] = rs

        @pl.when(pl.program_id(0) == 0)
        def _():
            norm(head_ref[...])

        @pl.when(pl.program_id(0) > 0)
        def _():
            norm(x_ref[...])

    return _hosted_call(
        body,
        name="norm1",
        grid=(rows // TM,),
        in_specs=[
            pl.BlockSpec((FRONT, D), lambda i: (0, 0)),
            pl.BlockSpec((TM, D), lambda i: (jnp.maximum(i - 1, 0), 0)),
            pl.BlockSpec((1, D), lambda i: (0, 0)),
        ],
        out_specs=[pl.BlockSpec((TM, D), lambda i: (i, 0)), pl.BlockSpec((TM, 1), lambda i: (i, 0))],
        out_shape=[jax.ShapeDtypeStruct((rows, D), BF16), jax.ShapeDtypeStruct((rows, 1), F32)],
        scratch_shapes=[],
        args=(head, x2d, norm_w),
        riders=riders,
    )


def _heavy_tile(rows):
    return TM_HEAVY if rows % TM_HEAVY == 0 else TM


def _in_proj(u, w_shard, route, cos_t, sin_t, riders=()):
    rows = u.shape[0]
    tm = _heavy_tile(rows)
    nt = rows // tm
    kind, shard = BIG_PIECES["w_in"]

    def body(route_ref, u_hbm, wsh_ref, cos_ref, sin_ref, proj_ref, wfull_ref,
             u_sc, w_sc, u_sem, w_sem, local_sem, ici_send, ici_recv, fwd_send, fwd_recv):
        g, i = pl.program_id(0), pl.program_id(1)
        s, c = route_ref[0], route_ref[1]
        gid = route_ref[2 + g]
        sibling = (s // 2, s % 2, 1 - c)
        local = pltpu.make_async_copy(wsh_ref, _full_region(wfull_ref, kind, shard, s, None), local_sem)
        u_copies = [pltpu.make_async_copy(u_hbm.at[pl.ds(t * tm, tm)], u_sc.at[t], u_sem.at[t]) for t in range(nt)]
        sends, arrivals = [], []
        for k in (1, 2, 3):
            s2, dev = _other_chip(s, c, k)
            sends.append(_remote(_shard_region(wsh_ref, shard, c), _full_region(wfull_ref, kind, shard, s, c),
                                 ici_send.at[k - 1], ici_recv.at[k - 1], dev))
            mine = _full_region(wfull_ref, kind, shard, s2, c)
            theirs = _full_region(wfull_ref, kind, shard, s2, 1 - c)
            arrivals.append((_remote(mine, mine, ici_send.at[k - 1], ici_recv.at[k - 1], dev),
                             _remote(mine, mine, fwd_send.at[k - 1], fwd_recv.at[k - 1], sibling),
                             _remote(theirs, theirs, fwd_send.at[k - 1], fwd_recv.at[k - 1], sibling)))

        @pl.when(jnp.logical_and(g == 0, i == 0))
        def _():
            for cp in sends:
                cp.start()
            local.start()
            for cp in u_copies:
                cp.start()

        for k, (arrived, forward, forwarded) in zip((1, 2, 3), arrivals):
            @pl.when(jnp.logical_and(g == 2 * k, i == 0))
            def _():
                arrived.wait_recv()
                forward.start()
                forwarded.wait_recv()

        @pl.when(jnp.logical_and(i == 0, g < 2))
        def _():
            cp = pltpu.make_async_copy(wsh_ref.at[:, pl.ds(pl.multiple_of(g * D, D), D)], w_sc, w_sem)
            cp.start()
            cp.wait()

        @pl.when(jnp.logical_and(i == 0, g >= 2))
        def _():
            cp = pltpu.make_async_copy(wfull_ref.at[:, pl.ds(pl.multiple_of(gid * D, D), D)], w_sc, w_sem)
            cp.start()
            cp.wait()

        for t in range(nt):
            @pl.when(jnp.logical_and(g == 0, i == t))
            def _():
                u_copies[t].wait()

        acc = _dot(u_sc[i], w_sc[...])

        @pl.when(gid < 2)
        def _():
            scale = jnp.where(gid == 1, QK_SCALE, 1.0).astype(F32)
            for h in range(HEADS):
                sl = slice(h * DH, (h + 1) * DH)
                blk = acc[:, sl]
                out = (blk * cos_ref[...] + pltpu.roll(blk, DH // 2, axis=1) * sin_ref[...]) * scale
                proj_ref[:, sl] = out.astype(BF16)

        @pl.when(gid >= 2)
        def _():
            proj_ref[...] = acc.astype(BF16)

        @pl.when(jnp.logical_and(g == 7, i == nt - 1))
        def _():
            local.wait()
            for cp in sends:
                cp.wait_send()
            for _, forward, _ in arrivals:
                forward.wait_send()

    return _hosted_call(
        body,
        name="in_proj",
        grid=(8, nt),
        in_specs=[
            _ANY, _ANY,
            pl.BlockSpec((tm, DH), lambda g, i, rt: (i, 0)),
            pl.BlockSpec((tm, DH), lambda g, i, rt: (i, 0)),
        ],
        out_specs=[pl.BlockSpec((tm, D), lambda g, i, rt: (i, rt[2 + g])), _ANY],
        out_shape=[jax.ShapeDtypeStruct((rows, IN_COLS), BF16), jax.ShapeDtypeStruct((D, IN_COLS), BF16)],
        scratch_shapes=[
            pltpu.VMEM((nt, tm, D), BF16), pltpu.VMEM((D, D), BF16),
            pltpu.SemaphoreType.DMA((nt,)), pltpu.SemaphoreType.DMA, pltpu.SemaphoreType.DMA,
            pltpu.SemaphoreType.DMA((3,)), pltpu.SemaphoreType.DMA((3,)),
            pltpu.SemaphoreType.DMA((3,)), pltpu.SemaphoreType.DMA((3,)),
        ],
        args=(u, w_shard, cos_t, sin_t),
        riders=riders,
        prefetch=route,
        riders_after_body=True,
    )


def _retention_fwd(proj_bf, intra, q_dec, k_dec, c_dec, riders=()):
    rows = proj_bf.shape[0]
    nc = rows // CHUNK

    def body(q_ref, k_ref, v_ref, m_ref, qd_ref, kd_ref, cd_ref, o_ref, sprev_ref, s_sc):
        @pl.when(pl.program_id(0) == 0)
        def _():
            s_sc[...] = jnp.zeros_like(s_sc)

        for h in range(HEADS):
            sl = slice(h * DH, (h + 1) * DH)
            q, k, v = q_ref[:, sl], k_ref[:, sl], v_ref[:, sl]
            state = s_sc[h]
            state_b = state.astype(BF16)
            sprev_ref[0, h] = state_b
            s = _dot_nt(q, k) * m_ref[h]
            inner = _dot(s.astype(BF16), v)
            cross = _dot(q, state_b) * qd_ref[:, sl]
            o_ref[:, sl] = (inner + cross).astype(BF16)
            k_scaled = (k.astype(F32) * kd_ref[:, sl]).astype(BF16)
            s_sc[h] = state * cd_ref[:, sl] + _dot_tn(k_scaled, v)

    chunk_spec = lambda col: pl.BlockSpec((CHUNK, D), lambda c: (c, col))
    const2 = lambda shape: pl.BlockSpec(shape, lambda c: (0, 0))
    return _hosted_call(
        body,
        name="retention_fwd",
        grid=(nc,),
        in_specs=[
            chunk_spec(0), chunk_spec(1), chunk_spec(2),
            pl.BlockSpec((HEADS, CHUNK, CHUNK), lambda c: (0, 0, 0)),
            const2((CHUNK, D)), const2((CHUNK, D)), const2((1, D)),
        ],
        out_specs=[
            pl.BlockSpec((CHUNK, D), lambda c: (c, 0)),
            pl.BlockSpec((1, HEADS, DH, DH), lambda c: (c, 0, 0, 0)),
        ],
        out_shape=[
            jax.ShapeDtypeStruct((rows, D), BF16),
            jax.ShapeDtypeStruct((nc, HEADS, DH, DH), BF16),
        ],
        scratch_shapes=[pltpu.VMEM((HEADS, DH, DH), F32)],
        args=(proj_bf, proj_bf, proj_bf, intra, q_dec, k_dec, c_dec),
        riders=riders,
    )


def _shift_down(x, first8_prev, d):
    rolled = pltpu.roll(x, d, axis=0)
    head = pltpu.roll(jnp.concatenate([first8_prev, x[0:8]], axis=0), d, axis=0)[8:16]
    return rolled, head


def _conv_and_gates(x, prev8, cw_ref, cb_ref, wa_ref, wx_ref, ba_ref, bx_ref, lam_ref, c_sc):
    cw = cw_ref[...]
    conv = cb_ref[...] + cw[3:4] * x
    head = cb_ref[...] + cw[3:4] * x[0:8]
    for d in (1, 2, 3):
        rolled, hd = _shift_down(x, prev8, d)
        conv = conv + cw[3 - d:4 - d] * rolled
        head = head + cw[3 - d:4 - d] * hd
    c_sc[...] = conv
    c_sc[0:8, :] = head
    c = c_sc[...]
    zr, zi = [], []
    for g in range(LRU_BLOCKS):
        sl = slice(g * LRU_BLOCK, (g + 1) * LRU_BLOCK)
        cg = c[:, sl].astype(BF16)
        zr.append(_dot(cg, wa_ref[g]))
        zi.append(_dot(cg, wx_ref[g]))
    r = _sigmoid(jnp.concatenate(zr, axis=1) + ba_ref[...])
    gate_i = _sigmoid(jnp.concatenate(zi, axis=1) + bx_ref[...])
    sp = _softplus(-lam_ref[...])
    log_a = (-LRU_C) * r * sp
    a = jnp.exp(log_a)
    mult = jnp.sqrt(_neg_expm1(2.0 * log_a))
    return c, r, gate_i, a, mult, log_a


def _lru_fwd(proj, conv_w, conv_b, wa, wx, ba, bx, lam, riders=()):
    rows = proj.shape[0]
    nt = rows // TM

    def body(x_ref, cw_ref, cb_ref, wa_ref, wx_ref, ba_ref, bx_ref, lam_ref,
             h_ref, c_ref, r_ref, i_ref, la_ref, mult_ref, prev_sc, carry_sc, c_sc, a_sc, u_sc, h_sc):
        i = pl.program_id(0)

        @pl.when(i == 0)
        def _():
            prev_sc[...] = jnp.zeros_like(prev_sc)
            carry_sc[...] = jnp.zeros_like(carry_sc)

        x = x_ref[...].astype(F32)
        c, r, gate_i, a, mult, log_a = _conv_and_gates(x, prev_sc[...], cw_ref, cb_ref, wa_ref, wx_ref, ba_ref, bx_ref,
                                                       lam_ref, c_sc)
        for ref, val in ((c_ref, c), (r_ref, r), (i_ref, gate_i), (la_ref, log_a), (mult_ref, mult)):
            ref[...] = val.astype(BF16)
        prev_sc[...] = x[TM - 8:TM]
        valid = _row_ids(i, TM, (TM, D)) >= PAD_ROWS
        a_sc[...] = a
        u_sc[...] = jnp.where(valid, mult * gate_i * c, 0.0)
        row8 = lax.broadcasted_iota(jnp.int32, (8, D), 0)

        def group(gi, hprev):
            r0 = pl.multiple_of(gi * 8, 8)
            aa = a_sc[pl.ds(r0, 8), :]
            uu = u_sc[pl.ds(r0, 8), :]
            for d in (1, 2, 4):
                a_sh = jnp.where(row8 >= d, pltpu.roll(aa, d, axis=0), 1.0)
                u_sh = jnp.where(row8 >= d, pltpu.roll(uu, d, axis=0), 0.0)
                uu = uu + aa * u_sh
                aa = aa * a_sh
            hb = aa * hprev + uu
            h_sc[pl.ds(r0, 8), :] = hb
            return hb[7:8, :]

        hlast = lax.fori_loop(0, TM // 8, group, carry_sc[0:1, :])
        carry_sc[0:1, :] = hlast
        h_ref[...] = h_sc[...].astype(BF16)

    vec = pl.BlockSpec((1, D), lambda i: (0, 0))
    wspec = pl.BlockSpec((LRU_BLOCKS, LRU_BLOCK, LRU_BLOCK), lambda i: (0, 0, 0))
    return _hosted_call(
        body,
        name="lru_fwd",
        grid=(nt,),
        in_specs=[
            pl.BlockSpec((TM, D), lambda i: (i, 4)),
            pl.BlockSpec((4, D), lambda i: (0, 0)),
            vec, wspec, wspec, vec, vec, vec,
        ],
        out_specs=[pl.BlockSpec((TM, D), lambda i: (i, 0)) for _ in range(6)],
        out_shape=[jax.ShapeDtypeStruct((rows, D), BF16) for _ in range(6)],
        scratch_shapes=[
            pltpu.VMEM((8, D), F32), pltpu.VMEM((8, D), F32),
            pltpu.VMEM((TM, D), F32), pltpu.VMEM((TM, D), F32), pltpu.VMEM((TM, D), F32), pltpu.VMEM((TM, D), F32),
        ],
        args=(proj, conv_w, conv_b, wa, wx, ba, bx, lam),
        riders=riders,
    )


def _group_norm(o):
    outs, rstds = [], []
    for h in range(HEADS):
        oh = o[:, h * DH:(h + 1) * DH]
        rs = lax.rsqrt(jnp.mean(oh * oh, axis=-1, keepdims=True) + EPS)
        outs.append(oh * rs)
        rstds.append(rs)
    return jnp.concatenate(outs, axis=1), rstds


def _mix_fwd(head, x2d, o, proj, h_lru, w_br, w_bl, w_o, ffn_norm_w, riders=()):
    rows = o.shape[0]
    nt = rows // TM

    def body(head_ref, x_ref, o_ref, gret_ref, lgate_ref, ga_ref, gb_ref, hl_ref, wbr_ref, wbl_ref, wo_ref, nw_ref,
             yret_ref, ylru_ref, h1_ref, u2_ref, rstd_ref):
        i = pl.program_id(0)
        on, _ = _group_norm(o_ref[...].astype(F32))
        gret = gret_ref[...].astype(F32)
        a_ret = (gret * _sigmoid(gret) * on).astype(BF16)
        y_ret = _dot(a_ret, wbr_ref[...])
        gl, _ = _gelu_and_grad(lgate_ref[...].astype(F32))
        a_lru = (gl * hl_ref[...].astype(F32)).astype(BF16)
        y_lru = _dot(a_lru, wbl_ref[...])
        mixed = (_sigmoid(ga_ref[...].astype(F32)) * y_ret + _sigmoid(gb_ref[...].astype(F32)) * y_lru).astype(BF16)
        delta = _dot(mixed, wo_ref[...])
        yret_ref[...] = y_ret.astype(BF16)
        ylru_ref[...] = y_lru.astype(BF16)

        def finish(h0):
            h1 = h0 + delta
            rs = lax.rsqrt(jnp.mean(h1 * h1, axis=-1, keepdims=True) + EPS)
            h1_ref[...] = h1
            u2_ref[...] = ((h1 * rs) * nw_ref[...]).astype(BF16)
            rstd_ref[...] = rs

        @pl.when(i == 0)
        def _():
            finish(head_ref[...])

        @pl.when(i > 0)
        def _():
            finish(x_ref[...])

    tile = lambda col: pl.BlockSpec((TM, D), lambda i: (i, col))
    wspec = pl.BlockSpec((D, D), lambda i: (0, 0))
    return _hosted_call(
        body,
        name="mix_fwd",
        grid=(nt,),
        in_specs=[
            pl.BlockSpec((FRONT, D), lambda i: (0, 0)),
            pl.BlockSpec((TM, D), lambda i: (jnp.maximum(i - 1, 0), 0)),
            tile(0), tile(3), tile(5), tile(6), tile(7), tile(0),
            wspec, wspec, wspec,
            pl.BlockSpec((1, D), lambda i: (0, 0)),
        ],
        out_specs=[tile(0), tile(0), tile(0), tile(0), pl.BlockSpec((TM, 1), lambda i: (i, 0))],
        out_shape=[
            jax.ShapeDtypeStruct((rows, D), BF16), jax.ShapeDtypeStruct((rows, D), BF16),
            jax.ShapeDtypeStruct((rows, D), F32), jax.ShapeDtypeStruct((rows, D), BF16),
            jax.ShapeDtypeStruct((rows, 1), F32),
        ],
        scratch_shapes=[],
        args=(head, x2d, o, proj, proj, proj, proj, h_lru, w_br, w_bl, w_o, ffn_norm_w),
        riders=riders,
    )


def _ffn_fwd(u2, h1, w_ffn_in, w_ffn_out):
    rows = u2.shape[0]
    tm = _heavy_tile(rows)
    nb = FFN // FFN_BLOCK
    hid = lambda: pl.BlockSpec((tm, FFN_BLOCK), lambda i, j: (i, j))

    def gate_body(u2_ref, wg_ref, wup_ref, g_ref, up_ref, act_ref):
        u2 = u2_ref[...]
        g = _dot(u2, wg_ref[...])
        up = _dot(u2, wup_ref[...])
        g_ref[...] = g.astype(BF16)
        up_ref[...] = up.astype(BF16)
        act_ref[...] = (g * _sigmoid(g) * up).astype(BF16)

    g, up, act = pl.pallas_call(
        gate_body,
        name="ffn_fwd_gate",
        grid=(rows // tm, nb),
        in_specs=[
            pl.BlockSpec((tm, D), lambda i, j: (i, 0)),
            pl.BlockSpec((D, FFN_BLOCK), lambda i, j: (0, j)),
            pl.BlockSpec((D, FFN_BLOCK), lambda i, j: (0, nb + j)),
        ],
        out_specs=[hid(), hid(), hid()],
        out_shape=[jax.ShapeDtypeStruct((rows, FFN), BF16)] * 3,
        compiler_params=_params(("arbitrary", "arbitrary")),
    )(u2, w_ffn_in, w_ffn_in)

    def out_body(act_ref, h1_ref, wo_ref, h2_ref):
        part = _dot(act_ref[...], wo_ref[...])

        @pl.when(pl.program_id(1) == 0)
        def _():
            h2_ref[...] = h1_ref[...] + part

        @pl.when(pl.program_id(1) > 0)
        def _():
            h2_ref[...] += part

    h2 = pl.pallas_call(
        out_body,
        name="ffn_fwd_out",
        grid=(rows // tm, nb),
        in_specs=[hid(), pl.BlockSpec((tm, D), lambda i, j: (i, 0)), pl.BlockSpec((FFN_BLOCK, D), lambda i, j: (j, 0))],
        out_specs=pl.BlockSpec((tm, D), lambda i, j: (i, 0)),
        out_shape=jax.ShapeDtypeStruct((rows, D), F32),
        compiler_params=_params(("arbitrary", "arbitrary")),
    )(act, h1, w_ffn_out)
    return g, up, act, h2


def _final_loss(h2, target, final_norm_w):
    rows = h2.shape[0]

    def body(h2_ref, tgt_ref, fnw_ref, dh2_ref, stats_ref):
        i = pl.program_id(0)

        @pl.when(i == 0)
        def _():
            stats_ref[...] = jnp.zeros_like(stats_ref)

        h2 = h2_ref[...]
        rs = lax.rsqrt(jnp.mean(h2 * h2, axis=-1, keepdims=True) + EPS)
        n = h2 * rs
        fnw = fnw_ref[...]
        valid = _row_ids(i, TM, (TM, D)) >= FRONT
        diff = jnp.where(valid, n * fnw - tgt_ref[...], 0.0)
        dy = diff * (1.0 / D)
        stats_ref[0:1, :] += (0.5 / D) * jnp.sum(diff * diff, axis=0, keepdims=True)
        stats_ref[1:2, :] += jnp.sum(dy * n, axis=0, keepdims=True)
        dn = dy * fnw
        dh2_ref[...] = rs * (dn - n * jnp.mean(dn * n, axis=-1, keepdims=True))

    return pl.pallas_call(
        body,
        name="final_loss",
        grid=(rows // TM,),
        in_specs=[
            pl.BlockSpec((TM, D), lambda i: (i, 0)),
            pl.BlockSpec((TM, D), lambda i: (jnp.maximum(i - 1, 0), 0)),
            pl.BlockSpec((1, D), lambda i: (0, 0)),
        ],
        out_specs=[pl.BlockSpec((TM, D), lambda i: (i, 0)), pl.BlockSpec((8, D), lambda i: (0, 0))],
        out_shape=[jax.ShapeDtypeStruct((rows, D), F32), jax.ShapeDtypeStruct((8, D), F32)],
        compiler_params=_params(("arbitrary",)),
    )(h2, target, final_norm_w)


def _ffn_bwd(dh2, g, up, h1, rstd2, w_ffn_in, w_ffn_out, ffn_norm_w):
    rows = dh2.shape[0]
    tm = _heavy_tile(rows)
    nb = FFN // FFN_BLOCK
    blk = lambda: pl.BlockSpec((tm, FFN_BLOCK), lambda i, j: (i, j))

    def gate_body(dh2_ref, g_ref, up_ref, wo_ref, dg_ref, dup_ref, dh2b_sc):
        @pl.when(pl.program_id(1) == 0)
        def _():
            dh2b_sc[...] = dh2_ref[...].astype(BF16)

        dact = _dot_nt(dh2b_sc[...], wo_ref[...])
        g = g_ref[...].astype(F32)
        up = up_ref[...].astype(F32)
        sg = _sigmoid(g)
        dup_ref[...] = (dact * (g * sg)).astype(BF16)
        dg_ref[...] = (dact * up * (sg * (1.0 + g * (1.0 - sg)))).astype(BF16)

    dg, dup = pl.pallas_call(
        gate_body,
        name="ffn_bwd_gate",
        grid=(rows // tm, nb),
        in_specs=[
            pl.BlockSpec((tm, D), lambda i, j: (i, 0)),
            blk(), blk(),
            pl.BlockSpec((FFN_BLOCK, D), lambda i, j: (j, 0)),
        ],
        out_specs=[blk(), blk()],
        out_shape=[jax.ShapeDtypeStruct((rows, FFN), BF16)] * 2,
        scratch_shapes=[pltpu.VMEM((tm, D), BF16)],
        compiler_params=_params(("arbitrary", "arbitrary")),
    )(dh2, g, up, w_ffn_out)

    def body(dg_ref, dup_ref, dh2_ref, h1_ref, rstd_ref, wg_ref, wup_ref, nw_ref, dh1_ref, stats_ref, du_sc):
        i = pl.program_id(0)
        j = pl.program_id(1)

        @pl.when(jnp.logical_and(i == 0, j == 0))
        def _():
            stats_ref[...] = jnp.zeros_like(stats_ref)

        part = _dot_nt(dg_ref[...], wg_ref[...]) + _dot_nt(dup_ref[...], wup_ref[...])

        @pl.when(j == 0)
        def _():
            du_sc[...] = part

        @pl.when(j > 0)
        def _():
            du_sc[...] += part

        @pl.when(j == nb - 1)
        def _():
            du = du_sc[...]
            rs = rstd_ref[...]
            n = h1_ref[...] * rs
            stats_ref[0:1, :] += jnp.sum(du * n, axis=0, keepdims=True)
            dn = du * nw_ref[...]
            dh1_ref[...] = dh2_ref[...] + rs * (dn - n * jnp.mean(dn * n, axis=-1, keepdims=True))

    dh1, stats = pl.pallas_call(
        body,
        name="ffn_bwd_in",
        grid=(rows // tm, nb),
        in_specs=[
            blk(), blk(),
            pl.BlockSpec((tm, D), lambda i, j: (i, 0)),
            pl.BlockSpec((tm, D), lambda i, j: (i, 0)),
            pl.BlockSpec((tm, 1), lambda i, j: (i, 0)),
            pl.BlockSpec((D, FFN_BLOCK), lambda i, j: (0, j)),
            pl.BlockSpec((D, FFN_BLOCK), lambda i, j: (0, nb + j)),
            pl.BlockSpec((1, D), lambda i, j: (0, 0)),
        ],
        out_specs=[pl.BlockSpec((tm, D), lambda i, j: (i, 0)), pl.BlockSpec((8, D), lambda i, j: (0, 0))],
        out_shape=[jax.ShapeDtypeStruct((rows, D), F32), jax.ShapeDtypeStruct((8, D), F32)],
        scratch_shapes=[pltpu.VMEM((tm, D), F32)],
        compiler_params=_params(("arbitrary", "arbitrary")),
    )(dg, dup, dh2, h1, rstd2, w_ffn_in, w_ffn_in, ffn_norm_w)
    return dg, dup, dh1, stats


def _mix_bwd(dh1, o, proj, h_lru, y_ret, y_lru, w_br, w_bl, w_o, riders=()):
    rows = dh1.shape[0]
    tm = TM_MIX_BWD
    nt = rows // tm

    def body(dh1_ref, o_ref, gret_ref, lgate_ref, ga_ref, gb_ref, hl_ref, yret_ref, ylru_ref, wbr_ref, wbl_ref, wo_ref,
             dproj_ref, do_ref, dhl_ref, mixed_ref, aret_ref, alru_ref, dyret_ref, dylru_ref):
        dmixed = _dot_nt(dh1_ref[...].astype(BF16), wo_ref[...])
        y_ret, y_lru = yret_ref[...].astype(F32), ylru_ref[...].astype(F32)
        sa, sb = _sigmoid(ga_ref[...].astype(F32)), _sigmoid(gb_ref[...].astype(F32))
        mixed_ref[...] = (sa * y_ret + sb * y_lru).astype(BF16)
        dga = dmixed * y_ret * sa * (1.0 - sa)
        dgb = dmixed * y_lru * sb * (1.0 - sb)
        dy_ret = (dmixed * sa).astype(BF16)
        dy_lru = (dmixed * sb).astype(BF16)
        dyret_ref[...] = dy_ret
        dylru_ref[...] = dy_lru
        da_ret = _dot_nt(dy_ret, wbr_ref[...])
        da_lru = _dot_nt(dy_lru, wbl_ref[...])

        gret = gret_ref[...].astype(F32)
        sg = _sigmoid(gret)
        silu = gret * sg
        on, rstds = _group_norm(o_ref[...].astype(F32))
        aret_ref[...] = (silu * on).astype(BF16)
        dgret = da_ret * on * (sg * (1.0 + gret * (1.0 - sg)))
        don = da_ret * silu
        for h in range(HEADS):
            sl = slice(h * DH, (h + 1) * DH)
            onh, donh = on[:, sl], don[:, sl]
            do_ref[:, sl] = (rstds[h] * (donh - onh * jnp.mean(donh * onh, axis=-1, keepdims=True))).astype(BF16)

        gl, gl_grad = _gelu_and_grad(lgate_ref[...].astype(F32))
        hl = hl_ref[...].astype(F32)
        alru_ref[...] = (gl * hl).astype(BF16)
        dlgate = da_lru * hl * gl_grad
        dhl_ref[...] = (da_lru * gl).astype(BF16)

        zeros = jnp.zeros((tm, D), BF16)
        for col in (0, 1, 2, 4):
            dproj_ref[:, col * D:(col + 1) * D] = zeros
        dproj_ref[:, 3 * D:4 * D] = dgret.astype(BF16)
        dproj_ref[:, 5 * D:6 * D] = dlgate.astype(BF16)
        dproj_ref[:, 6 * D:7 * D] = dga.astype(BF16)
        dproj_ref[:, 7 * D:8 * D] = dgb.astype(BF16)

    tile = lambda col: pl.BlockSpec((tm, D), lambda i: (i, col))
    wspec = pl.BlockSpec((D, D), lambda i: (0, 0))
    bf = lambda: jax.ShapeDtypeStruct((rows, D), BF16)
    return _hosted_call(
        body,
        name="mix_bwd",
        grid=(nt,),
        in_specs=[tile(0), tile(0), tile(3), tile(5), tile(6), tile(7), tile(0), tile(0), tile(0), wspec, wspec, wspec],
        out_specs=[pl.BlockSpec((tm, IN_COLS), lambda i: (i, 0))] + [tile(0)] * 7,
        out_shape=[jax.ShapeDtypeStruct((rows, IN_COLS), BF16)] + [bf() for _ in range(7)],
        scratch_shapes=[],
        args=(dh1, o, proj, proj, proj, proj, h_lru, y_ret, y_lru, w_br, w_bl, w_o),
        riders=riders,
    )


def _retention_bwd(dproj, proj_bf, do, sprev, intra, intra_t, q_dec, k_dec, c_dec, cos_t, sin_t, riders=()):
    rows = proj_bf.shape[0]
    nc = rows // CHUNK

    def body(dproj_in_ref, q_ref, k_ref, v_ref, do_ref, sprev_ref, m_ref, mt_ref, qd_ref, kd_ref, cd_ref, cos_ref, sin_ref,
             dproj_ref, ds_sc):
        @pl.when(pl.program_id(0) == 0)
        def _():
            ds_sc[...] = jnp.zeros_like(ds_sc)

        cos, sin = cos_ref[...], sin_ref[...]

        def unrotate(dy):
            return dy * cos - pltpu.roll(dy, DH // 2, axis=1) * sin

        for h in range(HEADS):
            sl = slice(h * DH, (h + 1) * DH)
            q, k, v = q_ref[:, sl], k_ref[:, sl], v_ref[:, sl]
            do = do_ref[:, sl]
            dob = do.astype(BF16)
            doq = (do * qd_ref[:, sl]).astype(BF16)
            state_prev = sprev_ref[0, h]
            dstate = ds_sc[h]
            dstate_b = dstate.astype(BF16)
            s_t = (_dot_nt(k, q) * mt_ref[h]).astype(BF16)
            ds_t = (_dot_nt(v, dob) * mt_ref[h]).astype(BF16)
            ds = (_dot_nt(dob, v) * m_ref[h]).astype(BF16)
            kd = kd_ref[:, sl]
            dq = _dot(ds, k) + _dot_nt(doq, state_prev)
            dk = _dot(ds_t, q) + _dot_nt(v, dstate_b) * kd
            k_scaled = (k.astype(F32) * kd).astype(BF16)
            dv = _dot(s_t, dob) + _dot(k_scaled, dstate_b)
            ds_sc[h] = dstate * cd_ref[:, sl] + _dot_tn(q, doq)
            dproj_ref[:, sl] = unrotate(dq).astype(BF16)
            dproj_ref[:, D + h * DH:D + (h + 1) * DH] = (unrotate(dk) * QK_SCALE).astype(BF16)
            dproj_ref[:, 2 * D + h * DH:2 * D + (h + 1) * DH] = dv.astype(BF16)

    rev = lambda c: nc - 1 - c
    chunk_spec = lambda col: pl.BlockSpec((CHUNK, D), lambda c: (rev(c), col))
    const2 = lambda shape: pl.BlockSpec(shape, lambda c: (0, 0))
    const3 = pl.BlockSpec((HEADS, CHUNK, CHUNK), lambda c: (0, 0, 0))
    return _hosted_call(
        body,
        name="retention_bwd",
        grid=(nc,),
        in_specs=[
            pl.BlockSpec(memory_space=pl.ANY),
            chunk_spec(0), chunk_spec(1), chunk_spec(2), chunk_spec(0),
            pl.BlockSpec((1, HEADS, DH, DH), lambda c: (rev(c), 0, 0, 0)),
            const3, const3,
            const2((CHUNK, D)), const2((CHUNK, D)), const2((1, D)),
            pl.BlockSpec((CHUNK, DH), lambda c: (rev(c), 0)),
            pl.BlockSpec((CHUNK, DH), lambda c: (rev(c), 0)),
        ],
        out_specs=[pl.BlockSpec((CHUNK, 3 * D), lambda c: (rev(c), 0))],
        out_shape=[jax.ShapeDtypeStruct(dproj.shape, BF16)],
        aliases={0: 0},
        scratch_shapes=[pltpu.VMEM((HEADS, DH, DH), F32)],
        args=(dproj, proj_bf, proj_bf, proj_bf, do, sprev, intra, intra_t, q_dec, k_dec, c_dec, cos_t, sin_t),
        riders=riders,
    )


def _lru_bwd(dproj, proj, saved, dhl, conv_w, wa, wx, lam, riders=()):
    rows = proj.shape[0]
    nt = rows // TM
    per8 = TM // 8

    def body(dproj_in_ref, x_ref, xprev_ref, h_ref, hprev_ref, c_ref, r_ref, i_ref, la_ref, mult_ref, dhl_ref,
             cw_ref, wa_ref, wx_ref, lam_ref,
             dproj_ref, dwa_ref, dwx_ref, stats_ref, anext_sc, dhnext_sc, dcnext_sc, c_sc, b_sc, dh_sc):
        step = pl.program_id(0)
        i = nt - 1 - step

        @pl.when(step == 0)
        def _():
            anext_sc[...] = jnp.zeros_like(anext_sc)
            dhnext_sc[...] = jnp.zeros_like(dhnext_sc)
            dcnext_sc[...] = jnp.zeros_like(dcnext_sc)
            dwa_ref[...] = jnp.zeros_like(dwa_ref)
            dwx_ref[...] = jnp.zeros_like(dwx_ref)
            stats_ref[...] = jnp.zeros_like(stats_ref)

        first = i == 0
        x = x_ref[...].astype(F32)
        prev8 = jnp.where(first, 0.0, xprev_ref[8:16, :].astype(F32))
        c_b = c_ref[...]
        c, r, gate_i, mult = (ref[...].astype(F32) for ref in (c_ref, r_ref, i_ref, mult_ref))
        a = jnp.exp(la_ref[...].astype(F32))
        sp = _softplus(-lam_ref[...])
        dh_sc[...] = dhl_ref[...].astype(F32)

        b_sc[...] = pltpu.roll(a, TM - 1, axis=0)
        b_sc[TM - 1:TM, :] = anext_sc[0:1, :]
        anext_sc[0:1, :] = a[0:1, :]
        row8 = lax.broadcasted_iota(jnp.int32, (8, D), 0)

        def group(gi, dhnext):
            r0 = pl.multiple_of((per8 - 1 - gi) * 8, 8)
            bb = b_sc[pl.ds(r0, 8), :]
            uu = dh_sc[pl.ds(r0, 8), :]
            for d in (1, 2, 4):
                b_sh = jnp.where(row8 < 8 - d, pltpu.roll(bb, 8 - d, axis=0), 1.0)
                u_sh = jnp.where(row8 < 8 - d, pltpu.roll(uu, 8 - d, axis=0), 0.0)
                uu = uu + bb * u_sh
                bb = bb * b_sh
            dhb = bb * dhnext + uu
            dh_sc[pl.ds(r0, 8), :] = dhb
            return dhb[0:1, :]

        dhfirst = lax.fori_loop(0, per8, group, dhnext_sc[0:1, :])
        dhnext_sc[0:1, :] = dhfirst
        dh = dh_sc[...]

        h = h_ref[...].astype(F32)
        hprev8 = jnp.where(first, 0.0, hprev_ref[8:16, :].astype(F32))
        h_dn, h_head = _shift_down(h, hprev8, 1)
        c_sc[...] = h_dn
        c_sc[0:8, :] = h_head
        h_before = c_sc[...]

        valid = _row_ids(i, TM, (TM, D)) >= PAD_ROWS
        da = dh * h_before
        du = jnp.where(valid, dh, 0.0)
        dmult = du * gate_i * c
        dgate_i = du * mult * c
        dc = du * mult * gate_i
        dla = da * a - dmult * (a * a) / mult
        dla = jnp.where(valid, dla, 0.0)
        dr = dla * ((-LRU_C) * sp)
        dzr = dr * r * (1.0 - r)
        dzi = dgate_i * gate_i * (1.0 - gate_i)
        stats_ref[1:2, :] += jnp.sum(dzr, axis=0, keepdims=True)
        stats_ref[2:3, :] += jnp.sum(dzi, axis=0, keepdims=True)
        stats_ref[3:4, :] += jnp.sum(dla * ((-LRU_C) * r), axis=0, keepdims=True)
        dc_gate = []
        for g in range(LRU_BLOCKS):
            sl = slice(g * LRU_BLOCK, (g + 1) * LRU_BLOCK)
            cg = c_b[:, sl]
            dzr_g = dzr[:, sl].astype(BF16)
            dzi_g = dzi[:, sl].astype(BF16)
            dc_gate.append(_dot_nt(dzr_g, wa_ref[g]) + _dot_nt(dzi_g, wx_ref[g]))
            dwa_ref[g] += _dot_tn(cg, dzr_g)
            dwx_ref[g] += _dot_tn(cg, dzi_g)
        dc = dc + jnp.concatenate(dc_gate, axis=1)

        cw = cw_ref[...]
        stats_ref[0:1, :] += jnp.sum(dc, axis=0, keepdims=True)
        stats_ref[7:8, :] += jnp.sum(dc * x, axis=0, keepdims=True)
        dx = cw[3:4] * dc
        tail_src = jnp.concatenate([dc[TM - 8:TM], dcnext_sc[...]], axis=0)
        dx_tail = cw[3:4] * dc[TM - 8:TM]
        for d in (1, 2, 3):
            dx = dx + cw[3 - d:4 - d] * pltpu.roll(dc, TM - d, axis=0)
            dx_tail = dx_tail + cw[3 - d:4 - d] * pltpu.roll(tail_src, 16 - d, axis=0)[0:8]
            rolled, hd = _shift_down(x, prev8, d)
            b_sc[...] = rolled
            b_sc[0:8, :] = hd
            stats_ref[7 - d:8 - d, :] += jnp.sum(dc * b_sc[...], axis=0, keepdims=True)
        dcnext_sc[...] = dc[0:8]
        dproj_ref[...] = dx.astype(BF16)
        dproj_ref[TM - 8:TM, :] = dx_tail.astype(BF16)

    rev = lambda s: nt - 1 - s
    vec = pl.BlockSpec((1, D), lambda s: (0, 0))
    wspec = pl.BlockSpec((LRU_BLOCKS, LRU_BLOCK, LRU_BLOCK), lambda s: (0, 0, 0))
    prev16 = lambda col: pl.BlockSpec((16, D), lambda s: (jnp.maximum(rev(s) * (TM // 16) - 1, 0), col))
    tile = lambda: pl.BlockSpec((TM, D), lambda s: (rev(s), 0))
    h_lru, c_sv, r_sv, i_sv, la_sv, mult_sv = saved
    return _hosted_call(
        body,
        name="lru_bwd",
        grid=(nt,),
        in_specs=[
            pl.BlockSpec(memory_space=pl.ANY),
            pl.BlockSpec((TM, D), lambda s: (rev(s), 4)), prev16(4),
            tile(), prev16(0),
            tile(), tile(), tile(), tile(), tile(), tile(),
            pl.BlockSpec((4, D), lambda s: (0, 0)),
            wspec, wspec, vec,
        ],
        out_specs=[
            pl.BlockSpec((TM, D), lambda s: (rev(s), 4)),
            wspec, wspec,
            pl.BlockSpec((8, D), lambda s: (0, 0)),
        ],
        out_shape=[
            jax.ShapeDtypeStruct(dproj.shape, BF16),
            jax.ShapeDtypeStruct((LRU_BLOCKS, LRU_BLOCK, LRU_BLOCK), F32),
            jax.ShapeDtypeStruct((LRU_BLOCKS, LRU_BLOCK, LRU_BLOCK), F32),
            jax.ShapeDtypeStruct((8, D), F32),
        ],
        aliases={0: 0},
        scratch_shapes=[
            pltpu.VMEM((8, D), F32), pltpu.VMEM((8, D), F32), pltpu.VMEM((8, D), F32),
            pltpu.VMEM((TM, D), F32), pltpu.VMEM((TM, D), F32), pltpu.VMEM((TM, D), F32),
        ],
        args=(dproj, proj, proj, h_lru, h_lru, c_sv, r_sv, i_sv, la_sv, mult_sv, dhl, conv_w, wa, wx, lam),
        riders=riders,
    )


def _in_proj_bwd(dproj, w_in, part, prev=None, riders=()):
    rows = dproj.shape[0]
    tm = _heavy_tile(rows)
    nt = rows // tm
    first = 0 if part == 0 else (nt + 1) // 2
    count = (nt + 1) // 2 if part == 0 else nt - first

    def body(*refs):
        dproj_ref, w_hbm, du_ref, w_sc, w_sem = refs[-5:]

        @pl.when(pl.program_id(0) == 0)
        def _():
            cp = pltpu.make_async_copy(w_hbm, w_sc, w_sem)
            cp.start()
            cp.wait()

        du_ref[...] = _dot_nt(dproj_ref[...], w_sc[...])

    in_specs = [pl.BlockSpec((tm, IN_COLS), lambda i: (first + i, 0)), _ANY]
    args = (dproj, w_in)
    if prev is not None:
        in_specs = [_ANY] + in_specs
        args = (prev,) + args
    return _hosted_call(
        body,
        name="in_proj_bwd_%d" % part,
        grid=(count,),
        in_specs=in_specs,
        out_specs=[pl.BlockSpec((tm, D), lambda i: (first + i, 0))],
        out_shape=[jax.ShapeDtypeStruct((rows, D), F32)],
        scratch_shapes=[pltpu.VMEM((D, IN_COLS), BF16), pltpu.SemaphoreType.DMA],
        aliases={0: 0} if prev is not None else None,
        args=args,
        riders=riders,
    )


def _norm1_bwd(du, dh1, head, x2d, rstd1, norm_w, riders=()):
    rows = du.shape[0]

    def body(du_ref, dh1_ref, head_ref, x_ref, rstd_ref, nw_ref, gx_ref, ghead_ref, stats_ref):
        i = pl.program_id(0)

        @pl.when(i == 0)
        def _():
            stats_ref[...] = jnp.zeros_like(stats_ref)

        def finish(h0, out_ref):
            du = du_ref[...]
            rs = rstd_ref[...]
            n = h0 * rs
            stats_ref[0:1, :] += jnp.sum(du * n, axis=0, keepdims=True)
            dn = du * nw_ref[...]
            out_ref[...] = dh1_ref[...] + rs * (dn - n * jnp.mean(dn * n, axis=-1, keepdims=True))

        @pl.when(i == 0)
        def _():
            finish(head_ref[...], ghead_ref)

        @pl.when(i > 0)
        def _():
            finish(x_ref[...], gx_ref)

    tile = pl.BlockSpec((TM, D), lambda i: (i, 0))
    return _hosted_call(
        body,
        name="norm1_bwd",
        grid=(rows // TM,),
        in_specs=[
            tile, tile,
            pl.BlockSpec((FRONT, D), lambda i: (0, 0)),
            pl.BlockSpec((TM, D), lambda i: (jnp.maximum(i - 1, 0), 0)),
            pl.BlockSpec((TM, 1), lambda i: (i, 0)),
            pl.BlockSpec((1, D), lambda i: (0, 0)),
        ],
        out_specs=[
            pl.BlockSpec((TM, D), lambda i: (jnp.maximum(i - 1, 0), 0)),
            pl.BlockSpec((FRONT, D), lambda i: (0, 0)),
            pl.BlockSpec((8, D), lambda i: (0, 0)),
        ],
        out_shape=[
            jax.ShapeDtypeStruct(x2d.shape, F32),
            jax.ShapeDtypeStruct((FRONT, D), F32),
            jax.ShapeDtypeStruct((8, D), F32),
        ],
        scratch_shapes=[],
        args=(du, dh1, head, x2d, rstd1, norm_w),
        riders=riders,
    )


def _matmul_tn(name, x, dy, out_cols, col0=0, prev=None, k_block=None, n_block=None, riders=()):
    rows, kdim = x.shape
    ndim = dy.shape[1]
    tr = next(t for t in (1408, TM_HEAVY, TM) if rows % t == 0)
    kb = k_block or kdim
    nb = n_block or ndim
    nr, nk, nn = rows // tr, kdim // kb, ndim // nb
    cb0 = col0 // nb

    def body(*refs):
        x_ref, dy_ref, out_ref = refs[-3], refs[-2], refs[-1]
        part = _dot_tn(x_ref[...].astype(BF16), dy_ref[...].astype(BF16))

        @pl.when(pl.program_id(2) == 0)
        def _():
            out_ref[...] = part

        @pl.when(pl.program_id(2) > 0)
        def _():
            out_ref[...] += part

    in_specs = [
        pl.BlockSpec((tr, kb), lambda n, k, r: (r, k)),
        pl.BlockSpec((tr, nb), lambda n, k, r: (r, n)),
    ]
    args = [x, dy]
    aliases = {}
    if prev is not None:
        in_specs = [pl.BlockSpec(memory_space=pl.ANY)] + in_specs
        args = [prev] + args
        aliases = {0: 0}
    (out,), rider_outs = _hosted_call(
        body,
        name=name,
        grid=(nn, nk, nr),
        in_specs=in_specs,
        out_specs=[pl.BlockSpec((kb, nb), lambda n, k, r: (k, cb0 + n))],
        out_shape=[jax.ShapeDtypeStruct((kdim, out_cols), F32)],
        scratch_shapes=[],
        aliases=aliases,
        args=args,
        riders=riders,
    )
    return (out, rider_outs) if riders else out


def _local_step(x2d, target, w, plan):
    rows = FRONT + x2d.shape[0]
    head = jnp.concatenate([jnp.zeros((PAD_ROWS, D), F32), w["meta_tokens"]], axis=0)
    cos_t, sin_t = _rope_tables(rows)
    intra, intra_t, q_dec, k_dec, c_dec = _decay_tables()
    grads = {}

    def hosted(host, fn, *args, **kwargs):
        outs, rider_outs = fn(*args, riders=plan.riders(host, w, grads), **kwargs)
        plan.after(host, rider_outs, w, grads)
        return outs

    (u1, rstd1), _ = _norm1(head, x2d, w["mix_norm_w"])
    proj, w["w_in"] = hosted("in_proj", _in_proj, u1, w["w_in_shard"], w["route"], cos_t, sin_t)
    o, sprev = hosted("retention_fwd", _retention_fwd, proj, intra, q_dec, k_dec, c_dec)
    lru_args = (w["conv_w"], w["conv_b"], w["lru_wa"], w["lru_wx"], w["lru_ba"], w["lru_bx"], w["lru_lambda"])
    lru_saved = hosted("lru_fwd", _lru_fwd, proj, *lru_args)
    h_lru = lru_saved[0]
    y_ret, y_lru, h1, u2, rstd2 = hosted("mix_fwd", _mix_fwd, head, x2d, o, proj, h_lru, w["w_branch_ret"],
                                         w["w_branch_lru"], w["w_out"], w["ffn_norm_w"])
    g, up, act, h2 = _ffn_fwd(u2, h1, w["w_ffn_in"], w["w_ffn_out"])
    dh2, stats_loss = _final_loss(h2, target, w["final_norm_w"])

    dg, dup, dh1, stats_ffn = _ffn_bwd(dh2, g, up, h1, rstd2, w["w_ffn_in"], w["w_ffn_out"], w["ffn_norm_w"])
    grads["w_ffn_in"] = _matmul_tn("dw_ffn_up", u2, dup, 2 * FFN, col0=FFN, n_block=FFN_HALF,
                                   prev=_matmul_tn("dw_ffn_gate", u2, dg, 2 * FFN, n_block=FFN_HALF))
    grads["w_ffn_out"] = _matmul_tn("dw_ffn_out", act, dh2, D, k_block=FFN_HALF)
    (dproj, do, dhl, mixed, a_ret, a_lru, dy_ret, dy_lru) = hosted(
        "mix_bwd", _mix_bwd, dh1, o, proj, h_lru, y_ret, y_lru, w["w_branch_ret"], w["w_branch_lru"], w["w_out"])
    grads["w_out"] = _matmul_tn("dw_out", mixed, dh1, D)
    grads["w_branch_ret"] = _matmul_tn("dw_branch_ret", a_ret, dy_ret, D)
    grads["w_branch_lru"] = _matmul_tn("dw_branch_lru", a_lru, dy_lru, D)
    (dproj,) = hosted("retention_bwd", _retention_bwd, dproj, proj, do, sprev, intra, intra_t, q_dec, k_dec, c_dec,
                      cos_t, sin_t)
    dproj, grads["lru_wa"], grads["lru_wx"], stats_lru = hosted(
        "lru_bwd", _lru_bwd, dproj, proj, lru_saved, dhl, w["conv_w"], w["lru_wa"], w["lru_wx"], w["lru_lambda"])
    grads["w_in"], rider_outs = _matmul_tn("dw_in", u1, dproj, IN_COLS, n_block=D, riders=plan.riders("dw_in", w, grads))
    plan.after("dw_in", rider_outs, w, grads)
    (du1,) = hosted("in_proj_bwd_0", _in_proj_bwd, dproj, w["w_in"], 0)
    (du1,) = hosted("in_proj_bwd_1", _in_proj_bwd, dproj, w["w_in"], 1, du1)
    (grad_x, grad_head, stats_in), _ = _norm1_bwd(du1, dh1, head, x2d, rstd1, w["mix_norm_w"])
    return grad_x, grad_head, grads, [stats_loss, stats_ffn, stats_in, stats_lru]


BIG_PIECES = {
    "w_in": ("col", (D, 2 * D)),
    "w_ffn_in": ("col", (D, FFN_HALF)),
    "w_ffn_out": ("row", (FFN // 4, D)),
    "w_branch_ret": ("row", (D // 4, D)),
    "w_branch_lru": ("row", (D // 4, D)),
    "w_out": ("row", (D // 4, D)),
    "lru_wa": ("lru", (LRU_BLOCKS, LRU_BLOCK // 4, LRU_BLOCK)),
    "lru_wx": ("lru", (LRU_BLOCKS, LRU_BLOCK // 4, LRU_BLOCK)),
}
SMALL_PIECES = {"meta_tokens": ("col", (N_META, D // 4)), "conv_w": ("col", (4, D // 4))}


def _full_shape(kind, shard):
    if kind == "col":
        return (shard[0], 4 * shard[1])
    if kind == "row":
        return (4 * shard[0], shard[1])
    return (shard[0], 4 * shard[1], shard[2])


def _half_shape(kind, shard):
    return (shard[0] // 2,) + tuple(shard[1:])


def _aligned(start, multiple):
    return start if isinstance(start, int) else pl.multiple_of(start, multiple)


def _lead(h, size):
    if h is None:
        return pl.ds(0, size)
    return pl.ds(_aligned(h * (size // 2), size // 2), size // 2)


def _full_region(ref, kind, shard, s, h):
    if kind == "col":
        return ref.at[_lead(h, shard[0]), pl.ds(_aligned(s * shard[1], shard[1]), shard[1])]
    if kind == "row":
        size = shard[0] if h is None else shard[0] // 2
        start = s * shard[0] + (0 if h is None else h * (shard[0] // 2))
        return ref.at[pl.ds(_aligned(start, 16), size), :]
    return ref.at[_lead(h, shard[0]), pl.ds(_aligned(s * shard[1], shard[1]), shard[1]), :]


def _shard_region(ref, shard, h):
    return ref.at[_lead(h, shard[0])]


def _place():
    x, y, c = lax.axis_index("x"), lax.axis_index("y"), lax.axis_index("c")
    return x, y, c, 2 * x + y


def _other_chip(s, c, k):
    s2 = jnp.bitwise_xor(s, k)
    return s2, (s2 // 2, s2 % 2, c)


def _remote(src, dst, send_sem, recv_sem, dev):
    return pltpu.make_async_remote_copy(src_ref=src, dst_ref=dst, send_sem=send_sem, recv_sem=recv_sem,
                                        device_id=dev, device_id_type=MESH)


_ANY = pl.BlockSpec(memory_space=pl.ANY)


class _Rider:
    def __init__(self, ins, out_shapes, sem_shapes, build, aliased=False):
        self.ins, self.out_shapes, self.sem_shapes, self.build, self.aliased = ins, out_shapes, sem_shapes, build, aliased


def _hosted_call(body, *, name, grid, in_specs, out_specs, out_shape, scratch_shapes, args, riders=(), aliases=None,
                 prefetch=None, riders_after_body=False):
    n_in, n_out, n_sc = len(in_specs), len(out_shape), len(scratch_shapes)
    r_in = [a for r in riders for a in r.ins]
    r_out = [s for r in riders for s in r.out_shapes]
    r_sem = [s for r in riders for s in r.sem_shapes]
    lead = () if prefetch is None else (prefetch,)
    assert prefetch is None or not (aliases or any(r.aliased for r in riders))

    def full_body(*refs):
        head, refs = refs[:len(lead)], refs[len(lead):]
        ins, rin = refs[:n_in], refs[n_in:n_in + len(r_in)]
        o0 = n_in + len(r_in)
        outs, rout = refs[o0:o0 + n_out], refs[o0 + n_out:o0 + n_out + len(r_out)]
        s0 = o0 + n_out + len(r_out)
        scratch, rsem = refs[s0:s0 + n_sc], refs[s0 + n_sc:]
        starts, waits = [], []
        pi = po = ps = 0
        for r in riders:
            st, wt = r.build(rin[pi:pi + len(r.ins)], rout[po:po + len(r.out_shapes)], rsem[ps:ps + len(r.sem_shapes)])
            starts += st
            waits += wt
            pi, po, ps = pi + len(r.ins), po + len(r.out_shapes), ps + len(r.sem_shapes)
        first = functools.reduce(jnp.logical_and, [pl.program_id(d) == 0 for d in range(len(grid))])
        last = functools.reduce(jnp.logical_and, [pl.program_id(d) == grid[d] - 1 for d in range(len(grid))])

        def start_riders():
            @pl.when(first)
            def _():
                for cp in starts:
                    cp.start()

        if riders and not riders_after_body:
            start_riders()
        body(*head, *ins, *outs, *scratch)
        if riders and riders_after_body:
            start_riders()
        if riders:
            @pl.when(last)
            def _():
                for wait in waits:
                    wait()

    io_aliases = dict(aliases or {})
    pi = po = 0
    for r in riders:
        if r.aliased:
            for q in range(len(r.ins)):
                io_aliases[n_in + pi + q] = n_out + po + q
        pi, po = pi + len(r.ins), po + len(r.out_shapes)
    specs = dict(
        grid=grid,
        in_specs=list(in_specs) + [_ANY] * len(r_in),
        out_specs=list(out_specs) + [_ANY] * len(r_out),
        scratch_shapes=list(scratch_shapes) + r_sem,
    )
    if prefetch is not None:
        specs = dict(grid_spec=pltpu.PrefetchScalarGridSpec(num_scalar_prefetch=1, **specs))
    res = pl.pallas_call(
        full_body,
        name=name,
        out_shape=list(out_shape) + r_out,
        input_output_aliases=io_aliases,
        compiler_params=_params(("arbitrary",) * len(grid)),
        **specs,
    )(*lead, *args, *r_in)
    rider_outs, po = [], n_out
    for r in riders:
        rider_outs.append(list(res[po:po + len(r.out_shapes)]))
        po += len(r.out_shapes)
    return list(res[:n_out]), rider_outs


def _run_riders(name, riders):
    r_in = [a for r in riders for a in r.ins]
    r_out = [s for r in riders for s in r.out_shapes]
    r_sem = [s for r in riders for s in r.sem_shapes]

    def body(*refs):
        rin, rout, rsem = refs[:len(r_in)], refs[len(r_in):len(r_in) + len(r_out)], refs[len(r_in) + len(r_out):]
        pi = po = ps = 0
        for r in riders:
            starts, waits = r.build(rin[pi:pi + len(r.ins)], rout[po:po + len(r.out_shapes)], rsem[ps:ps + len(r.sem_shapes)])
            for cp in starts:
                cp.start()
            for wait in waits:
                wait()
            pi, po, ps = pi + len(r.ins), po + len(r.out_shapes), ps + len(r.sem_shapes)

    io_aliases = {}
    pi = po = 0
    for r in riders:
        if r.aliased:
            for q in range(len(r.ins)):
                io_aliases[pi + q] = po + q
        pi, po = pi + len(r.ins), po + len(r.out_shapes)
    res = pl.pallas_call(
        body,
        name=name,
        in_specs=[_ANY] * len(r_in),
        out_specs=[_ANY] * len(r_out),
        out_shape=r_out,
        scratch_shapes=r_sem,
        input_output_aliases=io_aliases,
    )(*r_in)
    outs, po = [], 0
    for r in riders:
        outs.append(list(res[po:po + len(r.out_shapes)]))
        po += len(r.out_shapes)
    return outs


def _piece(name):
    if name in BIG_PIECES:
        return (name, *BIG_PIECES[name], True)
    return (name, *SMALL_PIECES[name], False)


def _gather_rider(shards, names):
    pieces = [_piece(n) for n in names]
    n = len(pieces)

    def build(ins, outs, sems):
        local_sem, ici_send, ici_recv = sems
        _, _, c, s = _place()
        starts, waits = [], []
        for p, (_, kind, shard, split) in enumerate(pieces):
            cp = pltpu.make_async_copy(ins[p], _full_region(outs[p], kind, shard, s, None), local_sem.at[p])
            starts.append(cp)
            waits.append(cp.wait)
            h = c if split else None
            for k in (1, 2, 3):
                s2, dev = _other_chip(s, c, k)
                cp = _remote(_shard_region(ins[p], shard, h), _full_region(outs[p], kind, shard, s, h),
                             ici_send.at[p, k - 1], ici_recv.at[p, k - 1], dev)
                starts.append(cp)
                waits.append(cp.wait_send)
                region = _full_region(outs[p], kind, shard, s2, h)
                waits.append(_remote(region, region, ici_send.at[p, k - 1], ici_recv.at[p, k - 1], dev).wait_recv)
        return starts, waits

    return _Rider(
        [shards[name] for name in names],
        [jax.ShapeDtypeStruct(_full_shape(kind, shard), shards[name].dtype) for name, kind, shard, _ in pieces],
        [pltpu.SemaphoreType.DMA((n,)), pltpu.SemaphoreType.DMA((n, 3)), pltpu.SemaphoreType.DMA((n, 3))],
        build)


def _forward_rider(gathered, names):
    pieces = [_piece(n) for n in names]
    n = len(pieces)

    def build(ins, outs, sems):
        fwd_send, fwd_recv = sems
        x, y, c, s = _place()
        sibling = (x, y, 1 - c)
        starts, waits = [], []
        for p, (_, kind, shard, _) in enumerate(pieces):
            for k in (1, 2, 3):
                s2, _ = _other_chip(s, c, k)
                mine = _full_region(outs[p], kind, shard, s2, c)
                theirs = _full_region(outs[p], kind, shard, s2, 1 - c)
                cp = _remote(mine, mine, fwd_send.at[p, k - 1], fwd_recv.at[p, k - 1], sibling)
                starts.append(cp)
                waits.append(cp.wait_send)
                waits.append(_remote(theirs, theirs, fwd_send.at[p, k - 1], fwd_recv.at[p, k - 1], sibling).wait_recv)
        return starts, waits

    return _Rider(
        [gathered[name] for name in names],
        [jax.ShapeDtypeStruct(gathered[name].shape, gathered[name].dtype) for name in names],
        [pltpu.SemaphoreType.DMA((n, 3)), pltpu.SemaphoreType.DMA((n, 3))],
        build, aliased=True)


def _pair_exchange_rider(grads, names):
    n = len(names)

    def build(ins, outs, sems):
        send_sem, recv_sem = sems
        x, y, c, _ = _place()
        sibling = (x, y, 1 - c)
        starts, waits = [], []
        for p, name in enumerate(names):
            kind, shard = BIG_PIECES[name]
            for s2 in range(4):
                cp = _remote(_full_region(ins[p], kind, shard, s2, 1 - c), outs[p].at[s2], send_sem.at[p, s2],
                             recv_sem.at[p, s2], sibling)
                starts.append(cp)
                waits.append(cp.wait_send)
                waits.append(_remote(outs[p].at[s2], outs[p].at[s2], send_sem.at[p, s2], recv_sem.at[p, s2], sibling).wait_recv)
        return starts, waits

    return _Rider(
        [grads[name] for name in names],
        [jax.ShapeDtypeStruct((4,) + _half_shape(*BIG_PIECES[name]), F32) for name in names],
        [pltpu.SemaphoreType.DMA((n, 4))] * 2,
        build)


def _half_specs(kind, shard):
    half = shard[0] // 2
    if kind == "col":
        full = pl.BlockSpec((half, shard[1]), lambda j, pr: (pr[1], j))
        buf = pl.BlockSpec((None, half, shard[1]), lambda j, pr: (j, 0, 0))
    elif kind == "row":
        full = pl.BlockSpec((half, shard[1]), lambda j, pr: (2 * j + pr[1], 0))
        buf = pl.BlockSpec((None, half, shard[1]), lambda j, pr: (j, 0, 0))
    else:
        full = pl.BlockSpec((half, shard[1], shard[2]), lambda j, pr: (pr[1], j, 0))
        buf = pl.BlockSpec((None, half, shard[1], shard[2]), lambda j, pr: (j, 0, 0, 0))
    return full, buf


def _pair_sum(name, grad, recv, place):
    kind, shard = BIG_PIECES[name]
    full, buf = _half_specs(kind, shard)

    def body(pr, g_ref, r_ref, o_ref):
        o_ref[...] = (g_ref[...] + r_ref[...]).astype(BF16)

    return pl.pallas_call(
        body,
        name="pair_sum_" + name,
        grid_spec=pltpu.PrefetchScalarGridSpec(num_scalar_prefetch=1, grid=(4,), in_specs=[full, buf], out_specs=buf),
        out_shape=jax.ShapeDtypeStruct((4,) + _half_shape(kind, shard), BF16),
        compiler_params=_params(("arbitrary",)),
    )(place, grad, recv)


def _chip_exchange_rider(sums, names):
    n = len(names)

    def build(ins, outs, sems):
        send_sem, recv_sem = sems
        _, _, c, s = _place()
        starts, waits = [], []
        for p in range(n):
            for k in (1, 2, 3):
                s2, dev = _other_chip(s, c, k)
                cp = _remote(ins[p].at[s2], outs[p].at[k - 1], send_sem.at[p, k - 1], recv_sem.at[p, k - 1], dev)
                starts.append(cp)
                waits.append(cp.wait_send)
                waits.append(_remote(outs[p].at[k - 1], outs[p].at[k - 1], send_sem.at[p, k - 1], recv_sem.at[p, k - 1],
                                     dev).wait_recv)
        return starts, waits

    return _Rider(
        [sums[name] for name in names],
        [jax.ShapeDtypeStruct((3,) + _half_shape(*BIG_PIECES[name]), BF16) for name in names],
        [pltpu.SemaphoreType.DMA((n, 3))] * 2,
        build)


def _chip_sum(name, grad, recv_pair, recv_chip, place):
    kind, shard = BIG_PIECES[name]
    half = shard[0] // 2
    tail = tuple(shard[1:])
    zeros = (0,) * len(tail)
    if kind == "col":
        full = pl.BlockSpec((half,) + tail, lambda j, pr: (pr[1], pr[0]))
    elif kind == "row":
        full = pl.BlockSpec((half,) + tail, lambda j, pr: (2 * pr[0] + pr[1], 0))
    else:
        full = pl.BlockSpec((half,) + tail, lambda j, pr: (pr[1], pr[0], 0))
    pair = pl.BlockSpec((None, half) + tail, lambda j, pr: (pr[0], 0) + zeros)
    chip = pl.BlockSpec((3, half) + tail, lambda j, pr: (0, 0) + zeros)
    out = pl.BlockSpec((half,) + tail, lambda j, pr: (pr[1],) + zeros)

    def body(pr, g_ref, rp_ref, rc_ref, o_ref):
        total = g_ref[...] + rp_ref[...]
        for k in range(3):
            total = total + rc_ref[k].astype(F32)
        o_ref[...] = total

    return pl.pallas_call(
        body,
        name="chip_sum_" + name,
        grid_spec=pltpu.PrefetchScalarGridSpec(num_scalar_prefetch=1, grid=(1,), in_specs=[full, pair, chip], out_specs=out),
        out_shape=jax.ShapeDtypeStruct(shard, F32),
        compiler_params=_params(("arbitrary",)),
    )(place, grad, recv_pair, recv_chip)


def _sibling_exchange_rider(halves, names):
    n = len(names)

    def build(ins, outs, sems):
        send_sem, recv_sem = sems
        x, y, c, _ = _place()
        sibling = (x, y, 1 - c)
        starts, waits = [], []
        for p, name in enumerate(names):
            shard = BIG_PIECES[name][1]
            mine = _shard_region(outs[p], shard, c)
            theirs = _shard_region(outs[p], shard, 1 - c)
            cp = _remote(mine, mine, send_sem.at[p], recv_sem.at[p], sibling)
            starts.append(cp)
            waits.append(cp.wait_send)
            waits.append(_remote(theirs, theirs, send_sem.at[p], recv_sem.at[p], sibling).wait_recv)
        return starts, waits

    return _Rider(
        [halves[name] for name in names],
        [jax.ShapeDtypeStruct(BIG_PIECES[name][1], F32) for name in names],
        [pltpu.SemaphoreType.DMA((n,))] * 2,
        build, aliased=True)


FIRST_WEIGHTS = ["meta_tokens", "conv_w"]
WEIGHT_GROUPS = {
    "mixer": ["lru_wa", "lru_wx", "w_branch_ret", "w_branch_lru", "w_out"],
    "ffn_in": ["w_ffn_in"],
    "ffn_out": ["w_ffn_out"],
}
WEIGHT_SCHEDULE = {
    "in_proj": [("gather", "mixer")],
    "retention_fwd": [("forward", "mixer"), ("gather", "ffn_in")],
    "lru_fwd": [("forward", "ffn_in"), ("gather", "ffn_out")],
    "mix_fwd": [("forward", "ffn_out")],
}
GRAD_GROUPS = {
    "ffn": ["w_ffn_in", "w_ffn_out"],
    "mixer": ["w_out", "w_branch_ret", "w_branch_lru", "lru_wa", "lru_wx"],
    "in": ["w_in"],
}
GRAD_SCHEDULE = {
    "mix_bwd": [("pair", "ffn")],
    "retention_bwd": [("chip", "ffn")],
    "lru_bwd": [("sibling", "ffn")],
    "dw_in": [("pair", "mixer")],
    "in_proj_bwd_0": [("chip", "mixer"), ("pair", "in")],
    "in_proj_bwd_1": [("sibling", "mixer"), ("chip", "in")],
}


class _CommPlan:
    def __init__(self, shards, place):
        self.shards, self.place = shards, place
        self.late = {}
        self.recv_pair, self.sums, self.recv_chip, self.halves, self.final = {}, {}, {}, {}, {}

    def _grad_rider(self, stage, group, grads):
        names = GRAD_GROUPS[group]
        if stage == "pair":
            return _pair_exchange_rider(grads, names)
        if stage == "chip":
            return _chip_exchange_rider(self.sums, names)
        return _sibling_exchange_rider(self.halves, names)

    def _grad_after(self, stage, group, outs, grads):
        names = GRAD_GROUPS[group]
        if stage == "pair":
            for n, o in zip(names, outs):
                self.recv_pair[n] = o
                self.sums[n] = _pair_sum(n, grads[n], o, self.place)
        elif stage == "chip":
            for n, o in zip(names, outs):
                self.halves[n] = _chip_sum(n, grads[n], self.recv_pair[n], o, self.place)
        else:
            self.final.update(zip(names, outs))

    def riders(self, host, w, grads):
        if host in WEIGHT_SCHEDULE:
            return [_gather_rider(self.shards, WEIGHT_GROUPS[group]) if stage == "gather"
                    else _forward_rider(self.late, WEIGHT_GROUPS[group]) for stage, group in WEIGHT_SCHEDULE[host]]
        return [self._grad_rider(stage, group, grads) for stage, group in GRAD_SCHEDULE.get(host, [])]

    def after(self, host, rider_outs, w, grads):
        for (stage, group), outs in zip(WEIGHT_SCHEDULE.get(host, []), rider_outs):
            (self.late if stage == "gather" else w).update(zip(WEIGHT_GROUPS[group], outs))
        for (stage, group), outs in zip(GRAD_SCHEDULE.get(host, []), rider_outs):
            self._grad_after(stage, group, outs, grads)

    def finish(self):
        (outs,) = _run_riders("grad_tail_exchange", [_sibling_exchange_rider(self.halves, GRAD_GROUPS["in"])])
        self.final.update(zip(GRAD_GROUPS["in"], outs))
        return self.final


def _adamw_math(w, g, m, v):
    m = ADAM_B1 * m + (1.0 - ADAM_B1) * g
    v = ADAM_B2 * v + (1.0 - ADAM_B2) * (g * g)
    m_hat = m / (1.0 - ADAM_B1 ** ADAM_STEP)
    v_hat = v / (1.0 - ADAM_B2 ** ADAM_STEP)
    delta = -ADAM_LR * (m_hat / (jnp.sqrt(v_hat) + ADAM_EPS) + ADAM_WD * w)
    return delta, m, v


def _adamw(name, g, w, m, v):
    rows, cols = g.shape
    tr = rows // 4 if rows % 32 == 0 else rows

    def body(g_ref, w_ref, m_ref, v_ref, go_ref, d_ref, mo_ref, vo_ref):
        gv = g_ref[...]
        delta, m2, v2 = _adamw_math(w_ref[...], gv, m_ref[...], v_ref[...])
        go_ref[...] = gv
        d_ref[...] = delta
        mo_ref[...] = m2
        vo_ref[...] = v2

    spec = pl.BlockSpec((tr, cols), lambda i: (i, 0))
    return pl.pallas_call(
        body,
        name="adamw_" + name,
        grid=(rows // tr,),
        in_specs=[spec] * 4,
        out_specs=[spec] * 4,
        out_shape=[jax.ShapeDtypeStruct((rows, cols), F32)] * 4,
        compiler_params=_params(("arbitrary",)),
    )(g, w, m, v)


SMALL_ROWS = 48
VEC_ROWS = {"final_norm_w": 1, "ffn_norm_w": 8, "mix_norm_w": 16, "conv_b": 24, "lru_ba": 25, "lru_bx": 26, "lru_lambda": 27}
VEC_NAMES = list(VEC_ROWS)
CONV_W_ROW = 28
META_ROW = 32


def _small_all_reduce(partial):
    def body(in_ref, out_ref, buf, local_sem, send_sem, recv_sem):
        x, y, c, s = _place()
        me = 2 * s + c
        cp = pltpu.make_async_copy(in_ref, buf.at[me], local_sem)
        cp.start()
        started = []
        for k in range(1, 8):
            peer = jnp.bitwise_xor(me, k)
            dev = (peer // 4, (peer // 2) % 2, peer % 2)
            rd = _remote(in_ref, buf.at[me], send_sem.at[k - 1], recv_sem.at[k - 1], dev)
            rd.start()
            started.append(rd)
        for k in range(1, 8):
            peer = jnp.bitwise_xor(me, k)
            dev = (peer // 4, (peer // 2) % 2, peer % 2)
            _remote(in_ref, buf.at[peer], send_sem.at[k - 1], recv_sem.at[k - 1], dev).wait_recv()
        for rd in started:
            rd.wait_send()
        cp.wait()
        total = buf[0]
        for d in range(1, 8):
            total = total + buf[d]
        out_ref[...] = total

    return pl.pallas_call(
        body,
        name="small_all_reduce",
        in_specs=[pl.BlockSpec(memory_space=pltpu.VMEM)],
        out_specs=pl.BlockSpec(memory_space=pltpu.VMEM),
        out_shape=jax.ShapeDtypeStruct((SMALL_ROWS, D), F32),
        scratch_shapes=[pltpu.VMEM((8, SMALL_ROWS, D), F32), pltpu.SemaphoreType.DMA,
                        pltpu.SemaphoreType.DMA((7,)), pltpu.SemaphoreType.DMA((7,))],
    )(partial)


def _small_update(total, place, vecs, conv, meta):
    nvec = len(VEC_NAMES)
    qcols = D // 4

    def body(pr, tot_ref, col_ref, *refs):
        vec_refs = refs[:3 * nvec]
        conv_refs = refs[3 * nvec:3 * nvec + 3]
        meta_refs = refs[3 * nvec + 3:3 * nvec + 6]
        outs = refs[3 * nvec + 6:]
        loss_ref, vec_out, conv_out, meta_out = outs[0], outs[1:5], outs[5:9], outs[9:13]
        loss_ref[...] = jnp.sum(tot_ref[0:1, :], axis=1, keepdims=True)
        for o in vec_out:
            o[...] = jnp.zeros_like(o)
        for j, name in enumerate(VEC_NAMES):
            w, m, v = (r[...] for r in vec_refs[3 * j:3 * j + 3])
            g = tot_ref[VEC_ROWS[name]:VEC_ROWS[name] + 1, :]
            if name == "lru_lambda":
                g = -g / (1.0 + jnp.exp(w))
            for o, val in zip(vec_out, (g,) + _adamw_math(w, g, m, v)):
                o[j:j + 1, :] = val
        for row, n_rows, ins, group in ((CONV_W_ROW, 4, conv_refs, conv_out), (META_ROW, N_META, meta_refs, meta_out)):
            g = col_ref[row:row + n_rows, :]
            w, m, v = (r[...] for r in ins)
            for o, val in zip(group, (g,) + _adamw_math(w, g, m, v)):
                o[...] = val

    whole = lambda shape: pl.BlockSpec(shape, lambda i, pr: (0,) * len(shape))
    in_specs = [whole((SMALL_ROWS, D)), pl.BlockSpec((SMALL_ROWS, qcols), lambda i, pr: (0, pr[0]))]
    in_specs += [whole((1, D))] * (3 * nvec) + [whole((4, qcols))] * 3 + [whole((N_META, qcols))] * 3
    out_shapes = [(1, 1)] + [(8, D)] * 4 + [(4, qcols)] * 4 + [(N_META, qcols)] * 4
    return pl.pallas_call(
        body,
        name="small_update",
        grid_spec=pltpu.PrefetchScalarGridSpec(num_scalar_prefetch=1, grid=(1,), in_specs=in_specs,
                                               out_specs=[whole(s) for s in out_shapes]),
        out_shape=[jax.ShapeDtypeStruct(s, F32) for s in out_shapes],
        compiler_params=_params(("arbitrary",)),
    )(place, total, total, *[a for t in vecs for a in t], *conv, *meta)


WEIGHT_ORDER = ["meta_tokens", "mix_norm_w", "w_in", "conv_w", "conv_b", "lru_wa", "lru_ba", "lru_wx", "lru_bx", "lru_lambda",
                "w_branch_ret", "w_branch_lru", "w_out", "ffn_norm_w", "w_ffn_in", "w_ffn_out", "final_norm_w"]


def kernel(x, meta_tokens, mix_norm_w, w_in, conv_w, conv_b, lru_wa, lru_ba, lru_wx, lru_bx, lru_lambda, w_branch_ret, w_branch_lru, w_out, ffn_norm_w, w_ffn_in, w_ffn_out, final_norm_w, loss_target, m_meta_tokens, m_mix_norm_w, m_w_in, m_conv_w, m_conv_b, m_lru_wa, m_lru_ba, m_lru_wx, m_lru_bx, m_lru_lambda, m_w_branch_ret, m_w_branch_lru, m_w_out, m_ffn_norm_w, m_w_ffn_in, m_w_ffn_out, m_final_norm_w, v_meta_tokens, v_mix_norm_w, v_w_in, v_conv_w, v_conv_b, v_lru_wa, v_lru_ba, v_lru_wx, v_lru_bx, v_lru_lambda, v_w_branch_ret, v_w_branch_lru, v_w_out, v_ffn_norm_w, v_w_ffn_in, v_w_ffn_out, v_final_norm_w):
    args = locals()
    wts = {n: args[n] for n in WEIGHT_ORDER}
    mom = {n: args["m_" + n] for n in WEIGHT_ORDER}
    var = {n: args["v_" + n] for n in WEIGHT_ORDER}
    place = jnp.stack([2 * lax.axis_index("x") + lax.axis_index("y"), lax.axis_index("c")]).astype(jnp.int32)

    shards = {n: wts[n][0].astype(BF16) for n in BIG_PIECES}
    shards["meta_tokens"] = wts["meta_tokens"]
    shards["conv_w"] = wts["conv_w"][0]
    plan = _CommPlan(shards, place)
    (first,) = _run_riders("gather_first", [_gather_rider(shards, FIRST_WEIGHTS)])
    w = dict(zip(FIRST_WEIGHTS, first))
    for n in VEC_NAMES:
        w[n] = wts[n].reshape(1, D)
    chips = jnp.bitwise_xor(place[0], jnp.arange(4, dtype=jnp.int32))
    w["route"] = jnp.concatenate([place, jnp.stack([2 * chips, 2 * chips + 1], axis=1).reshape(8)])
    w["w_in_shard"] = shards["w_in"]

    grad_x, grad_head, _, stats = _local_step(x[0], loss_target[0], w, plan)
    shard_grads = plan.finish()

    out = {}
    for n in BIG_PIECES:
        shape2d = (-1, wts[n].shape[-1])
        res = _adamw(n, *[a.reshape(shape2d) for a in (shard_grads[n], wts[n], mom[n], var[n])])
        out[n] = [r.reshape(wts[n].shape) for r in res]

    partial = jnp.concatenate(stats + [grad_head[PAD_ROWS:]], axis=0)
    total = _small_all_reduce(partial)
    vecs = [tuple(a[n].reshape(1, D) for a in (wts, mom, var)) for n in VEC_NAMES]
    conv = tuple(a["conv_w"][0] for a in (wts, mom, var))
    meta = tuple(a["meta_tokens"] for a in (wts, mom, var))
    res = _small_update(total, place, vecs, conv, meta)
    loss = res[0].reshape(())
    for j, n in enumerate(VEC_NAMES):
        out[n] = [r[j].reshape(wts[n].shape) for r in res[1:5]]
    out["conv_w"] = [r.reshape(wts["conv_w"].shape) for r in res[5:9]]
    out["meta_tokens"] = list(res[9:13])

    return (loss, grad_x.reshape(x.shape)) + tuple(out[n][kind] for kind in range(4) for n in WEIGHT_ORDER)
```

```python
import functools
import math

import jax
import jax.numpy as jnp
from jax import lax
from jax.experimental import pallas as pl
from jax.experimental.pallas import tpu as pltpu

F32 = jnp.float32
BF16 = jnp.bfloat16

D = 1024
HEADS = 8
DH = 128
CHUNK = 256
N_META = 16
FRONT = 256
PAD_ROWS = FRONT - N_META
LRU_BLOCKS = 4
LRU_BLOCK = 256
LRU_C = 8.0
FFN = 2816
FFN_HALF = FFN // 2
IN_COLS = 8 * D
ROPE_BASE = 10000.0
EPS = 1e-6
QK_SCALE = DH ** -0.5

ADAM_LR = 0.001
ADAM_B1 = 0.9
ADAM_B2 = 0.999
ADAM_EPS = 1e-08
ADAM_WD = 0.01
ADAM_STEP = 10

TM = 256
TM_HEAVY = 768
FFN_BLOCK = FFN // 2
TM_MIX_BWD = 256
VMEM_LIMIT = 60 * 1024 * 1024

NT_DIMS = (((1,), (1,)), ((), ()))
TN_DIMS = (((0,), (0,)), ((), ()))
MESH = pl.DeviceIdType.MESH


def _params(sem=None):
    if sem is None:
        return pltpu.CompilerParams(vmem_limit_bytes=VMEM_LIMIT)
    return pltpu.CompilerParams(dimension_semantics=sem, vmem_limit_bytes=VMEM_LIMIT)


def _dot(a, b):
    return jnp.dot(a, b, preferred_element_type=F32)


def _dot_nt(a, b):
    return lax.dot_general(a, b, NT_DIMS, preferred_element_type=F32)


def _dot_tn(a, b):
    return lax.dot_general(a, b, TN_DIMS, preferred_element_type=F32)


def _sigmoid(z):
    return 1.0 / (1.0 + jnp.exp(-z))


def _log1p(x):
    return jnp.where(x < 1e-3, x * (1.0 - x * (0.5 - x * (1.0 / 3.0))), jnp.log(1.0 + x))


def _softplus(x):
    return jnp.maximum(x, 0.0) + _log1p(jnp.exp(-jnp.abs(x)))


def _neg_expm1(x):
    series = -x * (1.0 + x * (0.5 + x * (1.0 / 6.0 + x * (1.0 / 24.0 + x * (1.0 / 120.0)))))
    return jnp.where(x > -0.05, series, 1.0 - jnp.exp(x))


_GELU_K = math.sqrt(2.0 / math.pi)


def _gelu_and_grad(x):
    inner = _GELU_K * (x + 0.044715 * x * x * x)
    t = jnp.tanh(inner)
    val = 0.5 * x * (1.0 + t)
    grad = 0.5 * (1.0 + t) + 0.5 * x * (1.0 - t * t) * _GELU_K * (1.0 + 3.0 * 0.044715 * x * x)
    return val, grad


def _row_ids(i, rows, shape):
    return i * rows + lax.broadcasted_iota(jnp.int32, shape, 0)


def _rope_tables(rows):
    pos = jnp.arange(rows, dtype=jnp.int32) - PAD_ROWS
    inv_freq = ROPE_BASE ** (-jnp.arange(0, DH, 2, dtype=F32) / DH)
    ang = pos.astype(F32)[:, None] * inv_freq[None, :]
    cos, sin = jnp.cos(ang), jnp.sin(ang)
    return jnp.concatenate([cos, cos], axis=1), jnp.concatenate([-sin, sin], axis=1)


def _decay_tables():
    log_g = jnp.log(1.0 - 2.0 ** (-5.0 - jnp.arange(HEADS, dtype=F32)))
    idx = jnp.arange(CHUNK, dtype=F32)
    diff = idx[:, None] - idx[None, :]
    intra = jnp.where(diff[None] >= 0, jnp.exp(jnp.maximum(diff, 0.0)[None] * log_g[:, None, None]), 0.0)
    q_decay = jnp.exp((idx + 1.0)[:, None] * log_g[None, :])
    k_decay = jnp.exp((CHUNK - 1.0 - idx)[:, None] * log_g[None, :])
    chunk_decay = jnp.exp(CHUNK * log_g)
    wide = lambda a: jnp.repeat(a, DH, axis=-1)
    return intra, jnp.swapaxes(intra, 1, 2), wide(q_decay), wide(k_decay), wide(chunk_decay[None, :])


def _norm1(head, x2d, norm_w, riders=()):
    rows = FRONT + x2d.shape[0]

    def body(head_ref, x_ref, nw_ref, u_ref, rstd_ref):
        def norm(hv):
            rs = lax.rsqrt(jnp.mean(hv * hv, axis=-1, keepdims=True) + EPS)
            u_ref[...] = ((hv * rs) * nw_ref[...]).astype(BF16)
            rstd_ref[...] = rs

        @pl.when(pl.program_id(0) == 0)
        def _():
            norm(head_ref[...])

        @pl.when(pl.program_id(0) > 0)
        def _():
            norm(x_ref[...])

    return _hosted_call(
        body,
        name="norm1",
        grid=(rows // TM,),
        in_specs=[
            pl.BlockSpec((FRONT, D), lambda i: (0, 0)),
            pl.BlockSpec((TM, D), lambda i: (jnp.maximum(i - 1, 0), 0)),
            pl.BlockSpec((1, D), lambda i: (0, 0)),
        ],
        out_specs=[pl.BlockSpec((TM, D), lambda i: (i, 0)), pl.BlockSpec((TM, 1), lambda i: (i, 0))],
        out_shape=[jax.ShapeDtypeStruct((rows, D), BF16), jax.ShapeDtypeStruct((rows, 1), F32)],
        scratch_shapes=[],
        args=(head, x2d, norm_w),
        riders=riders,
    )


def _heavy_tile(rows):
    return TM_HEAVY if rows % TM_HEAVY == 0 else TM


def _in_proj(u, w_shard, route, cos_t, sin_t, riders=()):
    rows = u.shape[0]
    tm = _heavy_tile(rows)
    nt = rows // tm
    kind, shard = BIG_PIECES["w_in"]

    def body(route_ref, u_hbm, wsh_ref, cos_ref, sin_ref, proj_ref, wfull_ref,
             u_sc, w_sc, u_sem, w_sem, local_sem, ici_send, ici_recv, fwd_send, fwd_recv):
        g, i = pl.program_id(0), pl.program_id(1)
        s, c = route_ref[0], route_ref[1]
        gid = route_ref[2 + g]
        sibling = (s // 2, s % 2, 1 - c)
        local = pltpu.make_async_copy(wsh_ref, _full_region(wfull_ref, kind, shard, s, None), local_sem)
        u_copies = [pltpu.make_async_copy(u_hbm.at[pl.ds(t * tm, tm)], u_sc.at[t], u_sem.at[t]) for t in range(nt)]
        sends, arrivals = [], []
        for k in (1, 2, 3):
            s2, dev = _other_chip(s, c, k)
            sends.append(_remote(_shard_region(wsh_ref, shard, c), _full_region(wfull_ref, kind, shard, s, c),
                                 ici_send.at[k - 1], ici_recv.at[k - 1], dev))
            mine = _full_region(wfull_ref, kind, shard, s2, c)
            theirs = _full_region(wfull_ref, kind, shard, s2, 1 - c)
            arrivals.append((_remote(mine, mine, ici_send.at[k - 1], ici_recv.at[k - 1], dev),
                             _remote(mine, mine, fwd_send.at[k - 1], fwd_recv.at[k - 1], sibling),
                             _remote(theirs, theirs, fwd_send.at[k - 1], fwd_recv.at[k - 1], sibling)))

        @pl.when(jnp.logical_and(g == 0, i == 0))
        def _():
            for cp in sends:
                cp.start()
            local.start()
            for cp in u_copies:
                cp.start()

        for k, (arrived, forward, forwarded) in zip((1, 2, 3), arrivals):
            @pl.when(jnp.logical_and(g == 2 * k, i == 0))
            def _():
                arrived.wait_recv()
                forward.start()
                forwarded.wait_recv()

        @pl.when(jnp.logical_and(i == 0, g < 2))
        def _():
            cp = pltpu.make_async_copy(wsh_ref.at[:, pl.ds(pl.multiple_of(g * D, D), D)], w_sc, w_sem)
            cp.start()
            cp.wait()

        @pl.when(jnp.logical_and(i == 0, g >= 2))
        def _():
            cp = pltpu.make_async_copy(wfull_ref.at[:, pl.ds(pl.multiple_of(gid * D, D), D)], w_sc, w_sem)
            cp.start()
            cp.wait()

        for t in range(nt):
            @pl.when(jnp.logical_and(g == 0, i == t))
            def _():
                u_copies[t].wait()

        acc = _dot(u_sc[i], w_sc[...])

        @pl.when(gid < 2)
        def _():
            scale = jnp.where(gid == 1, QK_SCALE, 1.0).astype(F32)
            for h in range(HEADS):
                sl = slice(h * DH, (h + 1) * DH)
                blk = acc[:, sl]
                out = (blk * cos_ref[...] + pltpu.roll(blk, DH // 2, axis=1) * sin_ref[...]) * scale
                proj_ref[:, sl] = out.astype(BF16)

        @pl.when(gid >= 2)
        def _():
            proj_ref[...] = acc.astype(BF16)

        @pl.when(jnp.logical_and(g == 7, i == nt - 1))
        def _():
            local.wait()
            for cp in sends:
                cp.wait_send()
            for _, forward, _ in arrivals:
                forward.wait_send()

    return _hosted_call(
        body,
        name="in_proj",
        grid=(8, nt),
        in_specs=[
            _ANY, _ANY,
            pl.BlockSpec((tm, DH), lambda g, i, rt: (i, 0)),
            pl.BlockSpec((tm, DH), lambda g, i, rt: (i, 0)),
        ],
        out_specs=[pl.BlockSpec((tm, D), lambda g, i, rt: (i, rt[2 + g])), _ANY],
        out_shape=[jax.ShapeDtypeStruct((rows, IN_COLS), BF16), jax.ShapeDtypeStruct((D, IN_COLS), BF16)],
        scratch_shapes=[
            pltpu.VMEM((nt, tm, D), BF16), pltpu.VMEM((D, D), BF16),
            pltpu.SemaphoreType.DMA((nt,)), pltpu.SemaphoreType.DMA, pltpu.SemaphoreType.DMA,
            pltpu.SemaphoreType.DMA((3,)), pltpu.SemaphoreType.DMA((3,)),
            pltpu.SemaphoreType.DMA((3,)), pltpu.SemaphoreType.DMA((3,)),
        ],
        args=(u, w_shard, cos_t, sin_t),
        riders=riders,
        prefetch=route,
        riders_after_body=True,
    )


def _retention_fwd(proj_bf, intra, q_dec, k_dec, c_dec, riders=()):
    rows = proj_bf.shape[0]
    nc = rows // CHUNK

    def body(q_ref, k_ref, v_ref, m_ref, qd_ref, kd_ref, cd_ref, o_ref, sprev_ref, s_sc):
        @pl.when(pl.program_id(0) == 0)
        def _():
            s_sc[...] = jnp.zeros_like(s_sc)

        for h in range(HEADS):
            sl = slice(h * DH, (h + 1) * DH)
            q, k, v = q_ref[:, sl], k_ref[:, sl], v_ref[:, sl]
            state = s_sc[h]
            state_b = state.astype(BF16)
            sprev_ref[0, h] = state_b
            s = _dot_nt(q, k) * m_ref[h]
            inner = _dot(s.astype(BF16), v)
            cross = _dot(q, state_b) * qd_ref[:, sl]
            o_ref[:, sl] = (inner + cross).astype(BF16)
            k_scaled = (k.astype(F32) * kd_ref[:, sl]).astype(BF16)
            s_sc[h] = state * cd_ref[:, sl] + _dot_tn(k_scaled, v)

    chunk_spec = lambda col: pl.BlockSpec((CHUNK, D), lambda c: (c, col))
    const2 = lambda shape: pl.BlockSpec(shape, lambda c: (0, 0))
    return _hosted_call(
        body,
        name="retention_fwd",
        grid=(nc,),
        in_specs=[
            chunk_spec(0), chunk_spec(1), chunk_spec(2),
            pl.BlockSpec((HEADS, CHUNK, CHUNK), lambda c: (0, 0, 0)),
            const2((CHUNK, D)), const2((CHUNK, D)), const2((1, D)),
        ],
        out_specs=[
            pl.BlockSpec((CHUNK, D), lambda c: (c, 0)),
            pl.BlockSpec((1, HEADS, DH, DH), lambda c: (c, 0, 0, 0)),
        ],
        out_shape=[
            jax.ShapeDtypeStruct((rows, D), BF16),
            jax.ShapeDtypeStruct((nc, HEADS, DH, DH), BF16),
        ],
        scratch_shapes=[pltpu.VMEM((HEADS, DH, DH), F32)],
        args=(proj_bf, proj_bf, proj_bf, intra, q_dec, k_dec, c_dec),
        riders=riders,
    )


def _shift_down(x, first8_prev, d):
    rolled = pltpu.roll(x, d, axis=0)
    head = pltpu.roll(jnp.concatenate([first8_prev, x[0:8]], axis=0), d, axis=0)[8:16]
    return rolled, head


def _conv_and_gates(x, prev8, cw_ref, cb_ref, wa_ref, wx_ref, ba_ref, bx_ref, lam_ref, c_sc):
    cw = cw_ref[...]
    conv = cb_ref[...] + cw[3:4] * x
    head = cb_ref[...] + cw[3:4] * x[0:8]
    for d in (1, 2, 3):
        rolled, hd = _shift_down(x, prev8, d)
        conv = conv + cw[3 - d:4 - d] * rolled
        head = head + cw[3 - d:4 - d] * hd
    c_sc[...] = conv
    c_sc[0:8, :] = head
    c = c_sc[...]
    zr, zi = [], []
    for g in range(LRU_BLOCKS):
        sl = slice(g * LRU_BLOCK, (g + 1) * LRU_BLOCK)
        cg = c[:, sl].astype(BF16)
        zr.append(_dot(cg, wa_ref[g]))
        zi.append(_dot(cg, wx_ref[g]))
    r = _sigmoid(jnp.concatenate(zr, axis=1) + ba_ref[...])
    gate_i = _sigmoid(jnp.concatenate(zi, axis=1) + bx_ref[...])
    sp = _softplus(-lam_ref[...])
    log_a = (-LRU_C) * r * sp
    a = jnp.exp(log_a)
    mult = jnp.sqrt(_neg_expm1(2.0 * log_a))
    return c, r, gate_i, a, mult, log_a


def _lru_fwd(proj, conv_w, conv_b, wa, wx, ba, bx, lam, riders=()):
    rows = proj.shape[0]
    nt = rows // TM

    def body(x_ref, cw_ref, cb_ref, wa_ref, wx_ref, ba_ref, bx_ref, lam_ref,
             h_ref, c_ref, r_ref, i_ref, la_ref, mult_ref, prev_sc, carry_sc, c_sc, a_sc, u_sc, h_sc):
        i = pl.program_id(0)

        @pl.when(i == 0)
        def _():
            prev_sc[...] = jnp.zeros_like(prev_sc)
            carry_sc[...] = jnp.zeros_like(carry_sc)

        x = x_ref[...].astype(F32)
        c, r, gate_i, a, mult, log_a = _conv_and_gates(x, prev_sc[...], cw_ref, cb_ref, wa_ref, wx_ref, ba_ref, bx_ref,
                                                       lam_ref, c_sc)
        for ref, val in ((c_ref, c), (r_ref, r), (i_ref, gate_i), (la_ref, log_a), (mult_ref, mult)):
            ref[...] = val.astype(BF16)
        prev_sc[...] = x[TM - 8:TM]
        valid = _row_ids(i, TM, (TM, D)) >= PAD_ROWS
        a_sc[...] = a
        u_sc[...] = jnp.where(valid, mult * gate_i * c, 0.0)
        row8 = lax.broadcasted_iota(jnp.int32, (8, D), 0)

        def group(gi, hprev):
            r0 = pl.multiple_of(gi * 8, 8)
            aa = a_sc[pl.ds(r0, 8), :]
            uu = u_sc[pl.ds(r0, 8), :]
            for d in (1, 2, 4):
                a_sh = jnp.where(row8 >= d, pltpu.roll(aa, d, axis=0), 1.0)
                u_sh = jnp.where(row8 >= d, pltpu.roll(uu, d, axis=0), 0.0)
                uu = uu + aa * u_sh
                aa = aa * a_sh
            hb = aa * hprev + uu
            h_sc[pl.ds(r0, 8), :] = hb
            return hb[7:8, :]

        hlast = lax.fori_loop(0, TM // 8, group, carry_sc[0:1, :])
        carry_sc[0:1, :] = hlast
        h_ref[...] = h_sc[...].astype(BF16)

    vec = pl.BlockSpec((1, D), lambda i: (0, 0))
    wspec = pl.BlockSpec((LRU_BLOCKS, LRU_BLOCK, LRU_BLOCK), lambda i: (0, 0, 0))
    return _hosted_call(
        body,
        name="lru_fwd",
        grid=(nt,),
        in_specs=[
            pl.BlockSpec((TM, D), lambda i: (i, 4)),
            pl.BlockSpec((4, D), lambda i: (0, 0)),
            vec, wspec, wspec, vec, vec, vec,
        ],
        out_specs=[pl.BlockSpec((TM, D), lambda i: (i, 0)) for _ in range(6)],
        out_shape=[jax.ShapeDtypeStruct((rows, D), BF16) for _ in range(6)],
        scratch_shapes=[
            pltpu.VMEM((8, D), F32), pltpu.VMEM((8, D), F32),
            pltpu.VMEM((TM, D), F32), pltpu.VMEM((TM, D), F32), pltpu.VMEM((TM, D), F32), pltpu.VMEM((TM, D), F32),
        ],
        args=(proj, conv_w, conv_b, wa, wx, ba, bx, lam),
        riders=riders,
    )


def _group_norm(o):
    outs, rstds = [], []
    for h in range(HEADS):
        oh = o[:, h * DH:(h + 1) * DH]
        rs = lax.rsqrt(jnp.mean(oh * oh, axis=-1, keepdims=True) + EPS)
        outs.append(oh * rs)
        rstds.append(rs)
    return jnp.concatenate(outs, axis=1), rstds


def _mix_fwd(head, x2d, o, proj, h_lru, w_br, w_bl, w_o, ffn_norm_w, riders=()):
    rows = o.shape[0]
    nt = rows // TM

    def body(head_ref, x_ref, o_ref, gret_ref, lgate_ref, ga_ref, gb_ref, hl_ref, wbr_ref, wbl_ref, wo_ref, nw_ref,
             yret_ref, ylru_ref, h1_ref, u2_ref, rstd_ref):
        i = pl.program_id(0)
        on, _ = _group_norm(o_ref[...].astype(F32))
        gret = gret_ref[...].astype(F32)
        a_ret = (gret * _sigmoid(gret) * on).astype(BF16)
        y_ret = _dot(a_ret, wbr_ref[...])
        gl, _ = _gelu_and_grad(lgate_ref[...].astype(F32))
        a_lru = (gl * hl_ref[...].astype(F32)).astype(BF16)
        y_lru = _dot(a_lru, wbl_ref[...])
        mixed = (_sigmoid(ga_ref[...].astype(F32)) * y_ret + _sigmoid(gb_ref[...].astype(F32)) * y_lru).astype(BF16)
        delta = _dot(mixed, wo_ref[...])
        yret_ref[...] = y_ret.astype(BF16)
        ylru_ref[...] = y_lru.astype(BF16)

        def finish(h0):
            h1 = h0 + delta
            rs = lax.rsqrt(jnp.mean(h1 * h1, axis=-1, keepdims=True) + EPS)
            h1_ref[...] = h1
            u2_ref[...] = ((h1 * rs) * nw_ref[...]).astype(BF16)
            rstd_ref[...] = rs

        @pl.when(i == 0)
        def _():
            finish(head_ref[...])

        @pl.when(i > 0)
        def _():
            finish(x_ref[...])

    tile = lambda col: pl.BlockSpec((TM, D), lambda i: (i, col))
    wspec = pl.BlockSpec((D, D), lambda i: (0, 0))
    return _hosted_call(
        body,
        name="mix_fwd",
        grid=(nt,),
        in_specs=[
            pl.BlockSpec((FRONT, D), lambda i: (0, 0)),
            pl.BlockSpec((TM, D), lambda i: (jnp.maximum(i - 1, 0), 0)),
            tile(0), tile(3), tile(5), tile(6), tile(7), tile(0),
            wspec, wspec, wspec,
            pl.BlockSpec((1, D), lambda i: (0, 0)),
        ],
        out_specs=[tile(0), tile(0), tile(0), tile(0), pl.BlockSpec((TM, 1), lambda i: (i, 0))],
        out_shape=[
            jax.ShapeDtypeStruct((rows, D), BF16), jax.ShapeDtypeStruct((rows, D), BF16),
            jax.ShapeDtypeStruct((rows, D), F32), jax.ShapeDtypeStruct((rows, D), BF16),
            jax.ShapeDtypeStruct((rows, 1), F32),
        ],
        scratch_shapes=[],
        args=(head, x2d, o, proj, proj, proj, proj, h_lru, w_br, w_bl, w_o, ffn_norm_w),
        riders=riders,
    )


def _ffn_fwd_gate(u2, w_ffn_in, riders=()):
    rows = u2.shape[0]
    tm = _heavy_tile(rows)
    nb = FFN // FFN_BLOCK
    hid = lambda: pl.BlockSpec((tm, FFN_BLOCK), lambda i, j: (i, j))

    def body(u2_ref, wg_ref, wup_ref, silu_ref, dsilu_ref, act_ref):
        u2 = u2_ref[...]
        g = _dot(u2, wg_ref[...])
        up = _dot(u2, wup_ref[...])
        sg = _sigmoid(g)
        silu = g * sg
        silu_ref[...] = silu.astype(BF16)
        dsilu_ref[...] = (up * (sg * (1.0 + g * (1.0 - sg)))).astype(BF16)
        act_ref[...] = (silu * up).astype(BF16)

    return _hosted_call(
        body,
        name="ffn_fwd_gate",
        grid=(rows // tm, nb),
        in_specs=[
            pl.BlockSpec((tm, D), lambda i, j: (i, 0)),
            pl.BlockSpec((D, FFN_BLOCK), lambda i, j: (0, j)),
            pl.BlockSpec((D, FFN_BLOCK), lambda i, j: (0, nb + j)),
        ],
        out_specs=[hid(), hid(), hid()],
        out_shape=[jax.ShapeDtypeStruct((rows, FFN), BF16)] * 3,
        scratch_shapes=[],
        args=(u2, w_ffn_in, w_ffn_in),
        riders=riders,
    )


def _ffn_fwd_out(act, h1, w_ffn_out):
    rows = act.shape[0]
    tm = _heavy_tile(rows)

    def body(act_ref, h1_ref, wo_ref, h2_ref):
        h2_ref[...] = h1_ref[...] + _dot(act_ref[...], wo_ref[...])

    return pl.pallas_call(
        body,
        name="ffn_fwd_out",
        grid=(rows // tm,),
        in_specs=[
            pl.BlockSpec((tm, FFN), lambda i: (i, 0)),
            pl.BlockSpec((tm, D), lambda i: (i, 0)),
            pl.BlockSpec((FFN, D), lambda i: (0, 0)),
        ],
        out_specs=pl.BlockSpec((tm, D), lambda i: (i, 0)),
        out_shape=jax.ShapeDtypeStruct((rows, D), F32),
        compiler_params=_params(("arbitrary",)),
    )(act, h1, w_ffn_out)


def _final_loss(h2, target, final_norm_w):
    rows = h2.shape[0]

    def body(h2_ref, tgt_ref, fnw_ref, dh2_ref, stats_ref):
        i = pl.program_id(0)

        @pl.when(i == 0)
        def _():
            stats_ref[...] = jnp.zeros_like(stats_ref)

        h2 = h2_ref[...]
        rs = lax.rsqrt(jnp.mean(h2 * h2, axis=-1, keepdims=True) + EPS)
        n = h2 * rs
        fnw = fnw_ref[...]
        valid = _row_ids(i, TM, (TM, D)) >= FRONT
        diff = jnp.where(valid, n * fnw - tgt_ref[...], 0.0)
        dy = diff * (1.0 / D)
        stats_ref[0:1, :] += (0.5 / D) * jnp.sum(diff * diff, axis=0, keepdims=True)
        stats_ref[1:2, :] += jnp.sum(dy * n, axis=0, keepdims=True)
        dn = dy * fnw
        dh2_ref[...] = rs * (dn - n * jnp.mean(dn * n, axis=-1, keepdims=True))

    return pl.pallas_call(
        body,
        name="final_loss",
        grid=(rows // TM,),
        in_specs=[
            pl.BlockSpec((TM, D), lambda i: (i, 0)),
            pl.BlockSpec((TM, D), lambda i: (jnp.maximum(i - 1, 0), 0)),
            pl.BlockSpec((1, D), lambda i: (0, 0)),
        ],
        out_specs=[pl.BlockSpec((TM, D), lambda i: (i, 0)), pl.BlockSpec((8, D), lambda i: (0, 0))],
        out_shape=[jax.ShapeDtypeStruct((rows, D), F32), jax.ShapeDtypeStruct((8, D), F32)],
        compiler_params=_params(("arbitrary",)),
    )(h2, target, final_norm_w)


def _ffn_bwd(dh2, silu, dsilu, h1, rstd2, w_ffn_in, w_ffn_out, ffn_norm_w):
    rows = dh2.shape[0]
    tm = _heavy_tile(rows)
    nb = FFN // FFN_BLOCK
    blk = lambda: pl.BlockSpec((tm, FFN_BLOCK), lambda i, j: (i, j))

    def gate_body(dh2_ref, silu_ref, dsilu_ref, wo_ref, dg_ref, dup_ref, dh2b_sc):
        @pl.when(pl.program_id(1) == 0)
        def _():
            dh2b_sc[...] = dh2_ref[...].astype(BF16)

        dact = _dot_nt(dh2b_sc[...], wo_ref[...])
        dup_ref[...] = (dact * silu_ref[...].astype(F32)).astype(BF16)
        dg_ref[...] = (dact * dsilu_ref[...].astype(F32)).astype(BF16)

    dg, dup = pl.pallas_call(
        gate_body,
        name="ffn_bwd_gate",
        grid=(rows // tm, nb),
        in_specs=[
            pl.BlockSpec((tm, D), lambda i, j: (i, 0)),
            blk(), blk(),
            pl.BlockSpec((FFN_BLOCK, D), lambda i, j: (j, 0)),
        ],
        out_specs=[blk(), blk()],
        out_shape=[jax.ShapeDtypeStruct((rows, FFN), BF16)] * 2,
        scratch_shapes=[pltpu.VMEM((tm, D), BF16)],
        compiler_params=_params(("arbitrary", "arbitrary")),
    )(dh2, silu, dsilu, w_ffn_out)

    def body(dg_ref, dup_ref, dh2_ref, h1_ref, rstd_ref, w_hbm, nw_ref, dh1_ref, stats_ref, w_sc, w_sem):
        @pl.when(pl.program_id(0) == 0)
        def _():
            stats_ref[...] = jnp.zeros_like(stats_ref)
            cp = pltpu.make_async_copy(w_hbm, w_sc, w_sem)
            cp.start()
            cp.wait()

        du = _dot_nt(dg_ref[...], w_sc[:, 0:FFN]) + _dot_nt(dup_ref[...], w_sc[:, FFN:2 * FFN])
        rs = rstd_ref[...]
        n = h1_ref[...] * rs
        stats_ref[0:1, :] += jnp.sum(du * n, axis=0, keepdims=True)
        dn = du * nw_ref[...]
        dh1_ref[...] = dh2_ref[...] + rs * (dn - n * jnp.mean(dn * n, axis=-1, keepdims=True))

    row = lambda width: pl.BlockSpec((tm, width), lambda i: (i, 0))
    dh1, stats = pl.pallas_call(
        body,
        name="ffn_bwd_in",
        grid=(rows // tm,),
        in_specs=[row(FFN), row(FFN), row(D), row(D), row(1), _ANY, pl.BlockSpec((1, D), lambda i: (0, 0))],
        out_specs=[row(D), pl.BlockSpec((8, D), lambda i: (0, 0))],
        out_shape=[jax.ShapeDtypeStruct((rows, D), F32), jax.ShapeDtypeStruct((8, D), F32)],
        scratch_shapes=[pltpu.VMEM((D, 2 * FFN), BF16), pltpu.SemaphoreType.DMA],
        compiler_params=_params(("arbitrary",)),
    )(dg, dup, dh2, h1, rstd2, w_ffn_in, ffn_norm_w)
    return dg, dup, dh1, stats


def _mix_bwd(dh1, o, proj, h_lru, y_ret, y_lru, w_br, w_bl, w_o, riders=()):
    rows = dh1.shape[0]
    tm = TM_MIX_BWD
    nt = rows // tm

    def body(dh1_ref, o_ref, gret_ref, lgate_ref, ga_ref, gb_ref, hl_ref, yret_ref, ylru_ref, wbr_ref, wbl_ref, wo_ref,
             dproj_ref, do_ref, dhl_ref, mixed_ref, aret_ref, alru_ref, dyret_ref, dylru_ref):
        dmixed = _dot_nt(dh1_ref[...].astype(BF16), wo_ref[...])
        y_ret, y_lru = yret_ref[...].astype(F32), ylru_ref[...].astype(F32)
        sa, sb = _sigmoid(ga_ref[...].astype(F32)), _sigmoid(gb_ref[...].astype(F32))
        mixed_ref[...] = (sa * y_ret + sb * y_lru).astype(BF16)
        dga = dmixed * y_ret * sa * (1.0 - sa)
        dgb = dmixed * y_lru * sb * (1.0 - sb)
        dy_ret = (dmixed * sa).astype(BF16)
        dy_lru = (dmixed * sb).astype(BF16)
        dyret_ref[...] = dy_ret
        dylru_ref[...] = dy_lru
        da_ret = _dot_nt(dy_ret, wbr_ref[...])
        da_lru = _dot_nt(dy_lru, wbl_ref[...])

        gret = gret_ref[...].astype(F32)
        sg = _sigmoid(gret)
        silu = gret * sg
        on, rstds = _group_norm(o_ref[...].astype(F32))
        aret_ref[...] = (silu * on).astype(BF16)
        dgret = da_ret * on * (sg * (1.0 + gret * (1.0 - sg)))
        don = da_ret * silu
        for h in range(HEADS):
            sl = slice(h * DH, (h + 1) * DH)
            onh, donh = on[:, sl], don[:, sl]
            do_ref[:, sl] = (rstds[h] * (donh - onh * jnp.mean(donh * onh, axis=-1, keepdims=True))).astype(BF16)

        gl, gl_grad = _gelu_and_grad(lgate_ref[...].astype(F32))
        hl = hl_ref[...].astype(F32)
        alru_ref[...] = (gl * hl).astype(BF16)
        dlgate = da_lru * hl * gl_grad
        dhl_ref[...] = (da_lru * gl).astype(BF16)

        zeros = jnp.zeros((tm, D), BF16)
        for col in (0, 1, 2, 4):
            dproj_ref[:, col * D:(col + 1) * D] = zeros
        dproj_ref[:, 3 * D:4 * D] = dgret.astype(BF16)
        dproj_ref[:, 5 * D:6 * D] = dlgate.astype(BF16)
        dproj_ref[:, 6 * D:7 * D] = dga.astype(BF16)
        dproj_ref[:, 7 * D:8 * D] = dgb.astype(BF16)

    tile = lambda col: pl.BlockSpec((tm, D), lambda i: (i, col))
    wspec = pl.BlockSpec((D, D), lambda i: (0, 0))
    bf = lambda: jax.ShapeDtypeStruct((rows, D), BF16)
    return _hosted_call(
        body,
        name="mix_bwd",
        grid=(nt,),
        in_specs=[tile(0), tile(0), tile(3), tile(5), tile(6), tile(7), tile(0), tile(0), tile(0), wspec, wspec, wspec],
        out_specs=[pl.BlockSpec((tm, IN_COLS), lambda i: (i, 0))] + [tile(0)] * 7,
        out_shape=[jax.ShapeDtypeStruct((rows, IN_COLS), BF16)] + [bf() for _ in range(7)],
        scratch_shapes=[],
        args=(dh1, o, proj, proj, proj, proj, h_lru, y_ret, y_lru, w_br, w_bl, w_o),
        riders=riders,
    )


def _retention_bwd(dproj, proj_bf, do, sprev, intra, intra_t, q_dec, k_dec, c_dec, cos_t, sin_t, riders=()):
    rows = proj_bf.shape[0]
    nc = rows // CHUNK

    def body(dproj_in_ref, q_ref, k_ref, v_ref, do_ref, sprev_ref, m_ref, mt_ref, qd_ref, kd_ref, cd_ref, cos_ref, sin_ref,
             dproj_ref, ds_sc):
        @pl.when(pl.program_id(0) == 0)
        def _():
            ds_sc[...] = jnp.zeros_like(ds_sc)

        cos, sin = cos_ref[...], sin_ref[...]

        def unrotate(dy):
            return dy * cos - pltpu.roll(dy, DH // 2, axis=1) * sin

        for h in range(HEADS):
            sl = slice(h * DH, (h + 1) * DH)
            q, k, v = q_ref[:, sl], k_ref[:, sl], v_ref[:, sl]
            do = do_ref[:, sl]
            dob = do.astype(BF16)
            doq = (do * qd_ref[:, sl]).astype(BF16)
            state_prev = sprev_ref[0, h]
            dstate = ds_sc[h]
            dstate_b = dstate.astype(BF16)
            s_t = (_dot_nt(k, q) * mt_ref[h]).astype(BF16)
            ds_t = (_dot_nt(v, dob) * mt_ref[h]).astype(BF16)
            ds = (_dot_nt(dob, v) * m_ref[h]).astype(BF16)
            kd = kd_ref[:, sl]
            dq = _dot(ds, k) + _dot_nt(doq, state_prev)
            dk = _dot(ds_t, q) + _dot_nt(v, dstate_b) * kd
            k_scaled = (k.astype(F32) * kd).astype(BF16)
            dv = _dot(s_t, dob) + _dot(k_scaled, dstate_b)
            ds_sc[h] = dstate * cd_ref[:, sl] + _dot_tn(q, doq)
            dproj_ref[:, sl] = unrotate(dq).astype(BF16)
            dproj_ref[:, D + h * DH:D + (h + 1) * DH] = (unrotate(dk) * QK_SCALE).astype(BF16)
            dproj_ref[:, 2 * D + h * DH:2 * D + (h + 1) * DH] = dv.astype(BF16)

    rev = lambda c: nc - 1 - c
    chunk_spec = lambda col: pl.BlockSpec((CHUNK, D), lambda c: (rev(c), col))
    const2 = lambda shape: pl.BlockSpec(shape, lambda c: (0, 0))
    const3 = pl.BlockSpec((HEADS, CHUNK, CHUNK), lambda c: (0, 0, 0))
    return _hosted_call(
        body,
        name="retention_bwd",
        grid=(nc,),
        in_specs=[
            pl.BlockSpec(memory_space=pl.ANY),
            chunk_spec(0), chunk_spec(1), chunk_spec(2), chunk_spec(0),
            pl.BlockSpec((1, HEADS, DH, DH), lambda c: (rev(c), 0, 0, 0)),
            const3, const3,
            const2((CHUNK, D)), const2((CHUNK, D)), const2((1, D)),
            pl.BlockSpec((CHUNK, DH), lambda c: (rev(c), 0)),
            pl.BlockSpec((CHUNK, DH), lambda c: (rev(c), 0)),
        ],
        out_specs=[pl.BlockSpec((CHUNK, 3 * D), lambda c: (rev(c), 0))],
        out_shape=[jax.ShapeDtypeStruct(dproj.shape, BF16)],
        aliases={0: 0},
        scratch_shapes=[pltpu.VMEM((HEADS, DH, DH), F32)],
        args=(dproj, proj_bf, proj_bf, proj_bf, do, sprev, intra, intra_t, q_dec, k_dec, c_dec, cos_t, sin_t),
        riders=riders,
    )


def _lru_bwd(dproj, proj, saved, dhl, conv_w, wa, wx, lam, riders=()):
    rows = proj.shape[0]
    nt = rows // TM
    per8 = TM // 8

    def body(dproj_in_ref, x_ref, xprev_ref, h_ref, hprev_ref, c_ref, r_ref, i_ref, la_ref, mult_ref, dhl_ref,
             cw_ref, wa_ref, wx_ref, lam_ref,
             dproj_ref, dwa_ref, dwx_ref, stats_ref, anext_sc, dhnext_sc, dcnext_sc, c_sc, b_sc, dh_sc):
        step = pl.program_id(0)
        i = nt - 1 - step

        @pl.when(step == 0)
        def _():
            anext_sc[...] = jnp.zeros_like(anext_sc)
            dhnext_sc[...] = jnp.zeros_like(dhnext_sc)
            dcnext_sc[...] = jnp.zeros_like(dcnext_sc)
            dwa_ref[...] = jnp.zeros_like(dwa_ref)
            dwx_ref[...] = jnp.zeros_like(dwx_ref)
            stats_ref[...] = jnp.zeros_like(stats_ref)

        first = i == 0
        x = x_ref[...].astype(F32)
        prev8 = jnp.where(first, 0.0, xprev_ref[8:16, :].astype(F32))
        c_b = c_ref[...]
        c, r, gate_i, mult = (ref[...].astype(F32) for ref in (c_ref, r_ref, i_ref, mult_ref))
        a = jnp.exp(la_ref[...].astype(F32))
        sp = _softplus(-lam_ref[...])
        dh_sc[...] = dhl_ref[...].astype(F32)

        b_sc[...] = pltpu.roll(a, TM - 1, axis=0)
        b_sc[TM - 1:TM, :] = anext_sc[0:1, :]
        anext_sc[0:1, :] = a[0:1, :]
        row8 = lax.broadcasted_iota(jnp.int32, (8, D), 0)

        def group(gi, dhnext):
            r0 = pl.multiple_of((per8 - 1 - gi) * 8, 8)
            bb = b_sc[pl.ds(r0, 8), :]
            uu = dh_sc[pl.ds(r0, 8), :]
            for d in (1, 2, 4):
                b_sh = jnp.where(row8 < 8 - d, pltpu.roll(bb, 8 - d, axis=0), 1.0)
                u_sh = jnp.where(row8 < 8 - d, pltpu.roll(uu, 8 - d, axis=0), 0.0)
                uu = uu + bb * u_sh
                bb = bb * b_sh
            dhb = bb * dhnext + uu
            dh_sc[pl.ds(r0, 8), :] = dhb
            return dhb[0:1, :]

        dhfirst = lax.fori_loop(0, per8, group, dhnext_sc[0:1, :])
        dhnext_sc[0:1, :] = dhfirst
        dh = dh_sc[...]

        h = h_ref[...].astype(F32)
        hprev8 = jnp.where(first, 0.0, hprev_ref[8:16, :].astype(F32))
        h_dn, h_head = _shift_down(h, hprev8, 1)
        c_sc[...] = h_dn
        c_sc[0:8, :] = h_head
        h_before = c_sc[...]

        valid = _row_ids(i, TM, (TM, D)) >= PAD_ROWS
        da = dh * h_before
        du = jnp.where(valid, dh, 0.0)
        dmult = du * gate_i * c
        dgate_i = du * mult * c
        dc = du * mult * gate_i
        dla = da * a - dmult * (a * a) / mult
        dla = jnp.where(valid, dla, 0.0)
        dr = dla * ((-LRU_C) * sp)
        dzr = dr * r * (1.0 - r)
        dzi = dgate_i * gate_i * (1.0 - gate_i)
        stats_ref[1:2, :] += jnp.sum(dzr, axis=0, keepdims=True)
        stats_ref[2:3, :] += jnp.sum(dzi, axis=0, keepdims=True)
        stats_ref[3:4, :] += jnp.sum(dla * ((-LRU_C) * r), axis=0, keepdims=True)
        dc_gate = []
        for g in range(LRU_BLOCKS):
            sl = slice(g * LRU_BLOCK, (g + 1) * LRU_BLOCK)
            cg = c_b[:, sl]
            dzr_g = dzr[:, sl].astype(BF16)
            dzi_g = dzi[:, sl].astype(BF16)
            dc_gate.append(_dot_nt(dzr_g, wa_ref[g]) + _dot_nt(dzi_g, wx_ref[g]))
            dwa_ref[g] += _dot_tn(cg, dzr_g)
            dwx_ref[g] += _dot_tn(cg, dzi_g)
        dc = dc + jnp.concatenate(dc_gate, axis=1)

        cw = cw_ref[...]
        stats_ref[0:1, :] += jnp.sum(dc, axis=0, keepdims=True)
        stats_ref[7:8, :] += jnp.sum(dc * x, axis=0, keepdims=True)
        dx = cw[3:4] * dc
        tail_src = jnp.concatenate([dc[TM - 8:TM], dcnext_sc[...]], axis=0)
        dx_tail = cw[3:4] * dc[TM - 8:TM]
        for d in (1, 2, 3):
            dx = dx + cw[3 - d:4 - d] * pltpu.roll(dc, TM - d, axis=0)
            dx_tail = dx_tail + cw[3 - d:4 - d] * pltpu.roll(tail_src, 16 - d, axis=0)[0:8]
            rolled, hd = _shift_down(x, prev8, d)
            b_sc[...] = rolled
            b_sc[0:8, :] = hd
            stats_ref[7 - d:8 - d, :] += jnp.sum(dc * b_sc[...], axis=0, keepdims=True)
        dcnext_sc[...] = dc[0:8]
        dproj_ref[...] = dx.astype(BF16)
        dproj_ref[TM - 8:TM, :] = dx_tail.astype(BF16)

    rev = lambda s: nt - 1 - s
    vec = pl.BlockSpec((1, D), lambda s: (0, 0))
    wspec = pl.BlockSpec((LRU_BLOCKS, LRU_BLOCK, LRU_BLOCK), lambda s: (0, 0, 0))
    prev16 = lambda col: pl.BlockSpec((16, D), lambda s: (jnp.maximum(rev(s) * (TM // 16) - 1, 0), col))
    tile = lambda: pl.BlockSpec((TM, D), lambda s: (rev(s), 0))
    h_lru, c_sv, r_sv, i_sv, la_sv, mult_sv = saved
    return _hosted_call(
        body,
        name="lru_bwd",
        grid=(nt,),
        in_specs=[
            pl.BlockSpec(memory_space=pl.ANY),
            pl.BlockSpec((TM, D), lambda s: (rev(s), 4)), prev16(4),
            tile(), prev16(0),
            tile(), tile(), tile(), tile(), tile(), tile(),
            pl.BlockSpec((4, D), lambda s: (0, 0)),
            wspec, wspec, vec,
        ],
        out_specs=[
            pl.BlockSpec((TM, D), lambda s: (rev(s), 4)),
            wspec, wspec,
            pl.BlockSpec((8, D), lambda s: (0, 0)),
        ],
        out_shape=[
            jax.ShapeDtypeStruct(dproj.shape, BF16),
            jax.ShapeDtypeStruct((LRU_BLOCKS, LRU_BLOCK, LRU_BLOCK), F32),
            jax.ShapeDtypeStruct((LRU_BLOCKS, LRU_BLOCK, LRU_BLOCK), F32),
            jax.ShapeDtypeStruct((8, D), F32),
        ],
        aliases={0: 0},
        scratch_shapes=[
            pltpu.VMEM((8, D), F32), pltpu.VMEM((8, D), F32), pltpu.VMEM((8, D), F32),
            pltpu.VMEM((TM, D), F32), pltpu.VMEM((TM, D), F32), pltpu.VMEM((TM, D), F32),
        ],
        args=(dproj, proj, proj, h_lru, h_lru, c_sv, r_sv, i_sv, la_sv, mult_sv, dhl, conv_w, wa, wx, lam),
        riders=riders,
    )


def _in_proj_bwd(dproj, w_in, part, prev=None, riders=()):
    rows = dproj.shape[0]
    tm = _heavy_tile(rows)
    nt = rows // tm
    first = 0 if part == 0 else (nt + 1) // 2
    count = (nt + 1) // 2 if part == 0 else nt - first

    def body(*refs):
        dproj_ref, w_hbm, du_ref, w_sc, w_sem = refs[-5:]

        @pl.when(pl.program_id(0) == 0)
        def _():
            cp = pltpu.make_async_copy(w_hbm, w_sc, w_sem)
            cp.start()
            cp.wait()

        du_ref[...] = _dot_nt(dproj_ref[...], w_sc[...])

    in_specs = [pl.BlockSpec((tm, IN_COLS), lambda i: (first + i, 0)), _ANY]
    args = (dproj, w_in)
    if prev is not None:
        in_specs = [_ANY] + in_specs
        args = (prev,) + args
    return _hosted_call(
        body,
        name="in_proj_bwd_%d" % part,
        grid=(count,),
        in_specs=in_specs,
        out_specs=[pl.BlockSpec((tm, D), lambda i: (first + i, 0))],
        out_shape=[jax.ShapeDtypeStruct((rows, D), F32)],
        scratch_shapes=[pltpu.VMEM((D, IN_COLS), BF16), pltpu.SemaphoreType.DMA],
        aliases={0: 0} if prev is not None else None,
        args=args,
        riders=riders,
    )


def _norm1_bwd(du, dh1, head, x2d, rstd1, norm_w, riders=()):
    rows = du.shape[0]

    def body(du_ref, dh1_ref, head_ref, x_ref, rstd_ref, nw_ref, gx_ref, ghead_ref, stats_ref):
        i = pl.program_id(0)

        @pl.when(i == 0)
        def _():
            stats_ref[...] = jnp.zeros_like(stats_ref)

        def finish(h0, out_ref):
            du = du_ref[...]
            rs = rstd_ref[...]
            n = h0 * rs
            stats_ref[0:1, :] += jnp.sum(du * n, axis=0, keepdims=True)
            dn = du * nw_ref[...]
            out_ref[...] = dh1_ref[...] + rs * (dn - n * jnp.mean(dn * n, axis=-1, keepdims=True))

        @pl.when(i == 0)
        def _():
            finish(head_ref[...], ghead_ref)

        @pl.when(i > 0)
        def _():
            finish(x_ref[...], gx_ref)

    tile = pl.BlockSpec((TM, D), lambda i: (i, 0))
    return _hosted_call(
        body,
        name="norm1_bwd",
        grid=(rows // TM,),
        in_specs=[
            tile, tile,
            pl.BlockSpec((FRONT, D), lambda i: (0, 0)),
            pl.BlockSpec((TM, D), lambda i: (jnp.maximum(i - 1, 0), 0)),
            pl.BlockSpec((TM, 1), lambda i: (i, 0)),
            pl.BlockSpec((1, D), lambda i: (0, 0)),
        ],
        out_specs=[
            pl.BlockSpec((TM, D), lambda i: (jnp.maximum(i - 1, 0), 0)),
            pl.BlockSpec((FRONT, D), lambda i: (0, 0)),
            pl.BlockSpec((8, D), lambda i: (0, 0)),
        ],
        out_shape=[
            jax.ShapeDtypeStruct(x2d.shape, F32),
            jax.ShapeDtypeStruct((FRONT, D), F32),
            jax.ShapeDtypeStruct((8, D), F32),
        ],
        scratch_shapes=[],
        args=(du, dh1, head, x2d, rstd1, norm_w),
        riders=riders,
    )


def _matmul_tn(name, x, dy, out_cols, col0=0, prev=None, k_block=None, n_block=None, riders=()):
    rows, kdim = x.shape
    ndim = dy.shape[1]
    tr = next(t for t in (1408, TM_HEAVY, TM) if rows % t == 0)
    kb = k_block or kdim
    nb = n_block or ndim
    nr, nk, nn = rows // tr, kdim // kb, ndim // nb
    cb0 = col0 // nb

    def body(*refs):
        x_ref, dy_ref, out_ref = refs[-3], refs[-2], refs[-1]
        part = _dot_tn(x_ref[...].astype(BF16), dy_ref[...].astype(BF16))

        @pl.when(pl.program_id(2) == 0)
        def _():
            out_ref[...] = part

        @pl.when(pl.program_id(2) > 0)
        def _():
            out_ref[...] += part

    in_specs = [
        pl.BlockSpec((tr, kb), lambda n, k, r: (r, k)),
        pl.BlockSpec((tr, nb), lambda n, k, r: (r, n)),
    ]
    args = [x, dy]
    aliases = {}
    if prev is not None:
        in_specs = [pl.BlockSpec(memory_space=pl.ANY)] + in_specs
        args = [prev] + args
        aliases = {0: 0}
    (out,), rider_outs = _hosted_call(
        body,
        name=name,
        grid=(nn, nk, nr),
        in_specs=in_specs,
        out_specs=[pl.BlockSpec((kb, nb), lambda n, k, r: (k, cb0 + n))],
        out_shape=[jax.ShapeDtypeStruct((kdim, out_cols), F32)],
        scratch_shapes=[],
        aliases=aliases,
        args=args,
        riders=riders,
    )
    return (out, rider_outs) if riders else out


def _local_step(x2d, target, w, plan):
    rows = FRONT + x2d.shape[0]
    head = jnp.concatenate([jnp.zeros((PAD_ROWS, D), F32), w["meta_tokens"]], axis=0)
    cos_t, sin_t = _rope_tables(rows)
    intra, intra_t, q_dec, k_dec, c_dec = _decay_tables()
    grads = {}

    def hosted(host, fn, *args, **kwargs):
        outs, rider_outs = fn(*args, riders=plan.riders(host, w, grads), **kwargs)
        plan.after(host, rider_outs, w, grads)
        return outs

    (u1, rstd1), _ = _norm1(head, x2d, w["mix_norm_w"])
    proj, w["w_in"] = hosted("in_proj", _in_proj, u1, w["w_in_shard"], w["route"], cos_t, sin_t)
    o, sprev = hosted("retention_fwd", _retention_fwd, proj, intra, q_dec, k_dec, c_dec)
    lru_args = (w["conv_w"], w["conv_b"], w["lru_wa"], w["lru_wx"], w["lru_ba"], w["lru_bx"], w["lru_lambda"])
    lru_saved = hosted("lru_fwd", _lru_fwd, proj, *lru_args)
    h_lru = lru_saved[0]
    y_ret, y_lru, h1, u2, rstd2 = hosted("mix_fwd", _mix_fwd, head, x2d, o, proj, h_lru, w["w_branch_ret"],
                                         w["w_branch_lru"], w["w_out"], w["ffn_norm_w"])
    silu, dsilu, act = hosted("ffn_fwd_gate", _ffn_fwd_gate, u2, w["w_ffn_in"])
    h2 = _ffn_fwd_out(act, h1, w["w_ffn_out"])
    dh2, stats_loss = _final_loss(h2, target, w["final_norm_w"])

    dg, dup, dh1, stats_ffn = _ffn_bwd(dh2, silu, dsilu, h1, rstd2, w["w_ffn_in"], w["w_ffn_out"], w["ffn_norm_w"])
    grads["w_ffn_in"] = _matmul_tn("dw_ffn_up", u2, dup, 2 * FFN, col0=FFN, n_block=FFN_HALF,
                                   prev=_matmul_tn("dw_ffn_gate", u2, dg, 2 * FFN, n_block=FFN_HALF))
    grads["w_ffn_out"] = _matmul_tn("dw_ffn_out", act, dh2, D, k_block=FFN_HALF)
    (dproj, do, dhl, mixed, a_ret, a_lru, dy_ret, dy_lru) = hosted(
        "mix_bwd", _mix_bwd, dh1, o, proj, h_lru, y_ret, y_lru, w["w_branch_ret"], w["w_branch_lru"], w["w_out"])
    grads["w_out"] = _matmul_tn("dw_out", mixed, dh1, D)
    grads["w_branch_ret"] = _matmul_tn("dw_branch_ret", a_ret, dy_ret, D)
    grads["w_branch_lru"] = _matmul_tn("dw_branch_lru", a_lru, dy_lru, D)
    (dproj,) = hosted("retention_bwd", _retention_bwd, dproj, proj, do, sprev, intra, intra_t, q_dec, k_dec, c_dec,
                      cos_t, sin_t)
    dproj, grads["lru_wa"], grads["lru_wx"], stats_lru = hosted(
        "lru_bwd", _lru_bwd, dproj, proj, lru_saved, dhl, w["conv_w"], w["lru_wa"], w["lru_wx"], w["lru_lambda"])
    grads["w_in"], rider_outs = _matmul_tn("dw_in", u1, dproj, IN_COLS, n_block=D, riders=plan.riders("dw_in", w, grads))
    plan.after("dw_in", rider_outs, w, grads)
    (du1,) = hosted("in_proj_bwd_0", _in_proj_bwd, dproj, w["w_in"], 0)
    (du1,) = hosted("in_proj_bwd_1", _in_proj_bwd, dproj, w["w_in"], 1, du1)
    (grad_x, grad_head, stats_in), _ = _norm1_bwd(du1, dh1, head, x2d, rstd1, w["mix_norm_w"])
    return grad_x, grad_head, grads, [stats_loss, stats_ffn, stats_in, stats_lru]


BIG_PIECES = {
    "w_in": ("col", (D, 2 * D)),
    "w_ffn_in": ("col", (D, FFN_HALF)),
    "w_ffn_out": ("row", (FFN // 4, D)),
    "w_branch_ret": ("row", (D // 4, D)),
    "w_branch_lru": ("row", (D // 4, D)),
    "w_out": ("row", (D // 4, D)),
    "lru_wa": ("lru", (LRU_BLOCKS, LRU_BLOCK // 4, LRU_BLOCK)),
    "lru_wx": ("lru", (LRU_BLOCKS, LRU_BLOCK // 4, LRU_BLOCK)),
}
SMALL_PIECES = {"meta_tokens": ("col", (N_META, D // 4)), "conv_w": ("col", (4, D // 4))}


def _full_shape(kind, shard):
    if kind == "col":
        return (shard[0], 4 * shard[1])
    if kind == "row":
        return (4 * shard[0], shard[1])
    return (shard[0], 4 * shard[1], shard[2])


def _half_shape(kind, shard):
    return (shard[0] // 2,) + tuple(shard[1:])


def _aligned(start, multiple):
    return start if isinstance(start, int) else pl.multiple_of(start, multiple)


def _lead(h, size):
    if h is None:
        return pl.ds(0, size)
    return pl.ds(_aligned(h * (size // 2), size // 2), size // 2)


def _full_region(ref, kind, shard, s, h):
    if kind == "col":
        return ref.at[_lead(h, shard[0]), pl.ds(_aligned(s * shard[1], shard[1]), shard[1])]
    if kind == "row":
        size = shard[0] if h is None else shard[0] // 2
        start = s * shard[0] + (0 if h is None else h * (shard[0] // 2))
        return ref.at[pl.ds(_aligned(start, 16), size), :]
    return ref.at[_lead(h, shard[0]), pl.ds(_aligned(s * shard[1], shard[1]), shard[1]), :]


def _shard_region(ref, shard, h):
    return ref.at[_lead(h, shard[0])]


def _place():
    x, y, c = lax.axis_index("x"), lax.axis_index("y"), lax.axis_index("c")
    return x, y, c, 2 * x + y


def _other_chip(s, c, k):
    s2 = jnp.bitwise_xor(s, k)
    return s2, (s2 // 2, s2 % 2, c)


def _remote(src, dst, send_sem, recv_sem, dev):
    return pltpu.make_async_remote_copy(src_ref=src, dst_ref=dst, send_sem=send_sem, recv_sem=recv_sem,
                                        device_id=dev, device_id_type=MESH)


_ANY = pl.BlockSpec(memory_space=pl.ANY)


class _Rider:
    def __init__(self, ins, out_shapes, sem_shapes, build, aliased=False):
        self.ins, self.out_shapes, self.sem_shapes, self.build, self.aliased = ins, out_shapes, sem_shapes, build, aliased


def _hosted_call(body, *, name, grid, in_specs, out_specs, out_shape, scratch_shapes, args, riders=(), aliases=None,
                 prefetch=None, riders_after_body=False):
    n_in, n_out, n_sc = len(in_specs), len(out_shape), len(scratch_shapes)
    r_in = [a for r in riders for a in r.ins]
    r_out = [s for r in riders for s in r.out_shapes]
    r_sem = [s for r in riders for s in r.sem_shapes]
    lead = () if prefetch is None else (prefetch,)
    assert prefetch is None or not (aliases or any(r.aliased for r in riders))

    def full_body(*refs):
        head, refs = refs[:len(lead)], refs[len(lead):]
        ins, rin = refs[:n_in], refs[n_in:n_in + len(r_in)]
        o0 = n_in + len(r_in)
        outs, rout = refs[o0:o0 + n_out], refs[o0 + n_out:o0 + n_out + len(r_out)]
        s0 = o0 + n_out + len(r_out)
        scratch, rsem = refs[s0:s0 + n_sc], refs[s0 + n_sc:]
        starts, waits = [], []
        pi = po = ps = 0
        for r in riders:
            st, wt = r.build(rin[pi:pi + len(r.ins)], rout[po:po + len(r.out_shapes)], rsem[ps:ps + len(r.sem_shapes)])
            starts += st
            waits += wt
            pi, po, ps = pi + len(r.ins), po + len(r.out_shapes), ps + len(r.sem_shapes)
        first = functools.reduce(jnp.logical_and, [pl.program_id(d) == 0 for d in range(len(grid))])
        last = functools.reduce(jnp.logical_and, [pl.program_id(d) == grid[d] - 1 for d in range(len(grid))])

        def start_riders():
            @pl.when(first)
            def _():
                for cp in starts:
                    cp.start()

        if riders and not riders_after_body:
            start_riders()
        body(*head, *ins, *outs, *scratch)
        if riders and riders_after_body:
            start_riders()
        if riders:
            @pl.when(last)
            def _():
                for wait in waits:
                    wait()

    io_aliases = dict(aliases or {})
    pi = po = 0
    for r in riders:
        if r.aliased:
            for q in range(len(r.ins)):
                io_aliases[n_in + pi + q] = n_out + po + q
        pi, po = pi + len(r.ins), po + len(r.out_shapes)
    specs = dict(
        grid=grid,
        in_specs=list(in_specs) + [_ANY] * len(r_in),
        out_specs=list(out_specs) + [_ANY] * len(r_out),
        scratch_shapes=list(scratch_shapes) + r_sem,
    )
    if prefetch is not None:
        specs = dict(grid_spec=pltpu.PrefetchScalarGridSpec(num_scalar_prefetch=1, **specs))
    res = pl.pallas_call(
        full_body,
        name=name,
        out_shape=list(out_shape) + r_out,
        input_output_aliases=io_aliases,
        compiler_params=_params(("arbitrary",) * len(grid)),
        **specs,
    )(*lead, *args, *r_in)
    rider_outs, po = [], n_out
    for r in riders:
        rider_outs.append(list(res[po:po + len(r.out_shapes)]))
        po += len(r.out_shapes)
    return list(res[:n_out]), rider_outs


def _run_riders(name, riders):
    r_in = [a for r in riders for a in r.ins]
    r_out = [s for r in riders for s in r.out_shapes]
    r_sem = [s for r in riders for s in r.sem_shapes]

    def body(*refs):
        rin, rout, rsem = refs[:len(r_in)], refs[len(r_in):len(r_in) + len(r_out)], refs[len(r_in) + len(r_out):]
        pi = po = ps = 0
        for r in riders:
            starts, waits = r.build(rin[pi:pi + len(r.ins)], rout[po:po + len(r.out_shapes)], rsem[ps:ps + len(r.sem_shapes)])
            for cp in starts:
                cp.start()
            for wait in waits:
                wait()
            pi, po, ps = pi + len(r.ins), po + len(r.out_shapes), ps + len(r.sem_shapes)

    io_aliases = {}
    pi = po = 0
    for r in riders:
        if r.aliased:
            for q in range(len(r.ins)):
                io_aliases[pi + q] = po + q
        pi, po = pi + len(r.ins), po + len(r.out_shapes)
    res = pl.pallas_call(
        body,
        name=name,
        in_specs=[_ANY] * len(r_in),
        out_specs=[_ANY] * len(r_out),
        out_shape=r_out,
        scratch_shapes=r_sem,
        input_output_aliases=io_aliases,
    )(*r_in)
    outs, po = [], 0
    for r in riders:
        outs.append(list(res[po:po + len(r.out_shapes)]))
        po += len(r.out_shapes)
    return outs


def _piece(name):
    if name in BIG_PIECES:
        return (name, *BIG_PIECES[name], True)
    return (name, *SMALL_PIECES[name], False)


def _gather_rider(shards, names):
    pieces = [_piece(n) for n in names]
    n = len(pieces)

    def build(ins, outs, sems):
        local_sem, ici_send, ici_recv = sems
        _, _, c, s = _place()
        starts, waits = [], []
        for p, (_, kind, shard, split) in enumerate(pieces):
            cp = pltpu.make_async_copy(ins[p], _full_region(outs[p], kind, shard, s, None), local_sem.at[p])
            starts.append(cp)
            waits.append(cp.wait)
            h = c if split else None
            for k in (1, 2, 3):
                s2, dev = _other_chip(s, c, k)
                cp = _remote(_shard_region(ins[p], shard, h), _full_region(outs[p], kind, shard, s, h),
                             ici_send.at[p, k - 1], ici_recv.at[p, k - 1], dev)
                starts.append(cp)
                waits.append(cp.wait_send)
                region = _full_region(outs[p], kind, shard, s2, h)
                waits.append(_remote(region, region, ici_send.at[p, k - 1], ici_recv.at[p, k - 1], dev).wait_recv)
        return starts, waits

    return _Rider(
        [shards[name] for name in names],
        [jax.ShapeDtypeStruct(_full_shape(kind, shard), shards[name].dtype) for name, kind, shard, _ in pieces],
        [pltpu.SemaphoreType.DMA((n,)), pltpu.SemaphoreType.DMA((n, 3)), pltpu.SemaphoreType.DMA((n, 3))],
        build)


def _forward_rider(gathered, names):
    pieces = [_piece(n) for n in names]
    n = len(pieces)

    def build(ins, outs, sems):
        fwd_send, fwd_recv = sems
        x, y, c, s = _place()
        sibling = (x, y, 1 - c)
        starts, waits = [], []
        for p, (_, kind, shard, _) in enumerate(pieces):
            for k in (1, 2, 3):
                s2, _ = _other_chip(s, c, k)
                mine = _full_region(outs[p], kind, shard, s2, c)
                theirs = _full_region(outs[p], kind, shard, s2, 1 - c)
                cp = _remote(mine, mine, fwd_send.at[p, k - 1], fwd_recv.at[p, k - 1], sibling)
                starts.append(cp)
                waits.append(cp.wait_send)
                waits.append(_remote(theirs, theirs, fwd_send.at[p, k - 1], fwd_recv.at[p, k - 1], sibling).wait_recv)
        return starts, waits

    return _Rider(
        [gathered[name] for name in names],
        [jax.ShapeDtypeStruct(gathered[name].shape, gathered[name].dtype) for name in names],
        [pltpu.SemaphoreType.DMA((n, 3)), pltpu.SemaphoreType.DMA((n, 3))],
        build, aliased=True)


def _pair_exchange_rider(grads, names):
    n = len(names)

    def build(ins, outs, sems):
        send_sem, recv_sem = sems
        x, y, c, _ = _place()
        sibling = (x, y, 1 - c)
        starts, waits = [], []
        for p, name in enumerate(names):
            kind, shard = BIG_PIECES[name]
            for s2 in range(4):
                cp = _remote(_full_region(ins[p], kind, shard, s2, 1 - c), outs[p].at[s2], send_sem.at[p, s2],
                             recv_sem.at[p, s2], sibling)
                starts.append(cp)
                waits.append(cp.wait_send)
                waits.append(_remote(outs[p].at[s2], outs[p].at[s2], send_sem.at[p, s2], recv_sem.at[p, s2], sibling).wait_recv)
        return starts, waits

    return _Rider(
        [grads[name] for name in names],
        [jax.ShapeDtypeStruct((4,) + _half_shape(*BIG_PIECES[name]), F32) for name in names],
        [pltpu.SemaphoreType.DMA((n, 4))] * 2,
        build)


def _half_specs(kind, shard):
    half = shard[0] // 2
    if kind == "col":
        full = pl.BlockSpec((half, shard[1]), lambda j, pr: (pr[1], j))
        buf = pl.BlockSpec((None, half, shard[1]), lambda j, pr: (j, 0, 0))
    elif kind == "row":
        full = pl.BlockSpec((half, shard[1]), lambda j, pr: (2 * j + pr[1], 0))
        buf = pl.BlockSpec((None, half, shard[1]), lambda j, pr: (j, 0, 0))
    else:
        full = pl.BlockSpec((half, shard[1], shard[2]), lambda j, pr: (pr[1], j, 0))
        buf = pl.BlockSpec((None, half, shard[1], shard[2]), lambda j, pr: (j, 0, 0, 0))
    return full, buf


def _pair_sum(name, grad, recv, place):
    kind, shard = BIG_PIECES[name]
    full, buf = _half_specs(kind, shard)

    def body(pr, g_ref, r_ref, o_ref):
        o_ref[...] = (g_ref[...] + r_ref[...]).astype(BF16)

    return pl.pallas_call(
        body,
        name="pair_sum_" + name,
        grid_spec=pltpu.PrefetchScalarGridSpec(num_scalar_prefetch=1, grid=(4,), in_specs=[full, buf], out_specs=buf),
        out_shape=jax.ShapeDtypeStruct((4,) + _half_shape(kind, shard), BF16),
        compiler_params=_params(("arbitrary",)),
    )(place, grad, recv)


def _chip_exchange_rider(sums, names):
    n = len(names)

    def build(ins, outs, sems):
        send_sem, recv_sem = sems
        _, _, c, s = _place()
        starts, waits = [], []
        for p in range(n):
            for k in (1, 2, 3):
                s2, dev = _other_chip(s, c, k)
                cp = _remote(ins[p].at[s2], outs[p].at[k - 1], send_sem.at[p, k - 1], recv_sem.at[p, k - 1], dev)
                starts.append(cp)
                waits.append(cp.wait_send)
                waits.append(_remote(outs[p].at[k - 1], outs[p].at[k - 1], send_sem.at[p, k - 1], recv_sem.at[p, k - 1],
                                     dev).wait_recv)
        return starts, waits

    return _Rider(
        [sums[name] for name in names],
        [jax.ShapeDtypeStruct((3,) + _half_shape(*BIG_PIECES[name]), BF16) for name in names],
        [pltpu.SemaphoreType.DMA((n, 3))] * 2,
        build)


def _chip_sum(name, grad, recv_pair, recv_chip, place):
    kind, shard = BIG_PIECES[name]
    half = shard[0] // 2
    tail = tuple(shard[1:])
    zeros = (0,) * len(tail)
    if kind == "col":
        full = pl.BlockSpec((half,) + tail, lambda j, pr: (pr[1], pr[0]))
    elif kind == "row":
        full = pl.BlockSpec((half,) + tail, lambda j, pr: (2 * pr[0] + pr[1], 0))
    else:
        full = pl.BlockSpec((half,) + tail, lambda j, pr: (pr[1], pr[0], 0))
    pair = pl.BlockSpec((None, half) + tail, lambda j, pr: (pr[0], 0) + zeros)
    chip = pl.BlockSpec((3, half) + tail, lambda j, pr: (0, 0) + zeros)
    out = pl.BlockSpec((half,) + tail, lambda j, pr: (pr[1],) + zeros)

    def body(pr, g_ref, rp_ref, rc_ref, o_ref):
        total = g_ref[...] + rp_ref[...]
        for k in range(3):
            total = total + rc_ref[k].astype(F32)
        o_ref[...] = total

    return pl.pallas_call(
        body,
        name="chip_sum_" + name,
        grid_spec=pltpu.PrefetchScalarGridSpec(num_scalar_prefetch=1, grid=(1,), in_specs=[full, pair, chip], out_specs=out),
        out_shape=jax.ShapeDtypeStruct(shard, F32),
        compiler_params=_params(("arbitrary",)),
    )(place, grad, recv_pair, recv_chip)


def _sibling_exchange_rider(halves, names):
    n = len(names)

    def build(ins, outs, sems):
        send_sem, recv_sem = sems
        x, y, c, _ = _place()
        sibling = (x, y, 1 - c)
        starts, waits = [], []
        for p, name in enumerate(names):
            shard = BIG_PIECES[name][1]
            mine = _shard_region(outs[p], shard, c)
            theirs = _shard_region(outs[p], shard, 1 - c)
            cp = _remote(mine, mine, send_sem.at[p], recv_sem.at[p], sibling)
            starts.append(cp)
            waits.append(cp.wait_send)
            waits.append(_remote(theirs, theirs, send_sem.at[p], recv_sem.at[p], sibling).wait_recv)
        return starts, waits

    return _Rider(
        [halves[name] for name in names],
        [jax.ShapeDtypeStruct(BIG_PIECES[name][1], F32) for name in names],
        [pltpu.SemaphoreType.DMA((n,))] * 2,
        build, aliased=True)


FIRST_WEIGHTS = ["meta_tokens", "conv_w"]
WEIGHT_GROUPS = {
    "mixer": ["lru_wa", "lru_wx", "w_branch_ret", "w_branch_lru", "w_out"],
    "ffn_in": ["w_ffn_in"],
    "ffn_out": ["w_ffn_out"],
}
WEIGHT_SCHEDULE = {
    "in_proj": [("gather", "mixer")],
    "retention_fwd": [("forward", "mixer")],
    "lru_fwd": [("gather", "ffn_in")],
    "mix_fwd": [("forward", "ffn_in"), ("gather", "ffn_out")],
    "ffn_fwd_gate": [("forward", "ffn_out")],
}
GRAD_GROUPS = {
    "ffn": ["w_ffn_in", "w_ffn_out"],
    "mixer": ["w_out", "w_branch_ret", "w_branch_lru", "lru_wa", "lru_wx"],
    "in": ["w_in"],
}
GRAD_SCHEDULE = {
    "mix_bwd": [("pair", "ffn")],
    "retention_bwd": [("chip", "ffn")],
    "lru_bwd": [("sibling", "ffn")],
    "dw_in": [("pair", "mixer")],
    "in_proj_bwd_0": [("chip", "mixer"), ("pair", "in")],
    "in_proj_bwd_1": [("sibling", "mixer"), ("chip", "in")],
}


class _CommPlan:
    def __init__(self, shards, place):
        self.shards, self.place = shards, place
        self.late = {}
        self.recv_pair, self.sums, self.recv_chip, self.halves, self.final = {}, {}, {}, {}, {}

    def _grad_rider(self, stage, group, grads):
        names = GRAD_GROUPS[group]
        if stage == "pair":
            return _pair_exchange_rider(grads, names)
        if stage == "chip":
            return _chip_exchange_rider(self.sums, names)
        return _sibling_exchange_rider(self.halves, names)

    def _grad_after(self, stage, group, outs, grads):
        names = GRAD_GROUPS[group]
        if stage == "pair":
            for n, o in zip(names, outs):
                self.recv_pair[n] = o
                self.sums[n] = _pair_sum(n, grads[n], o, self.place)
        elif stage == "chip":
            for n, o in zip(names, outs):
                self.halves[n] = _chip_sum(n, grads[n], self.recv_pair[n], o, self.place)
        else:
            self.final.update(zip(names, outs))

    def riders(self, host, w, grads):
        if host in WEIGHT_SCHEDULE:
            return [_gather_rider(self.shards, WEIGHT_GROUPS[group]) if stage == "gather"
                    else _forward_rider(self.late, WEIGHT_GROUPS[group]) for stage, group in WEIGHT_SCHEDULE[host]]
        return [self._grad_rider(stage, group, grads) for stage, group in GRAD_SCHEDULE.get(host, [])]

    def after(self, host, rider_outs, w, grads):
        for (stage, group), outs in zip(WEIGHT_SCHEDULE.get(host, []), rider_outs):
            (self.late if stage == "gather" else w).update(zip(WEIGHT_GROUPS[group], outs))
        for (stage, group), outs in zip(GRAD_SCHEDULE.get(host, []), rider_outs):
            self._grad_after(stage, group, outs, grads)

    def finish(self):
        (outs,) = _run_riders("grad_tail_exchange", [_sibling_exchange_rider(self.halves, GRAD_GROUPS["in"])])
        self.final.update(zip(GRAD_GROUPS["in"], outs))
        return self.final


def _adamw_math(w, g, m, v):
    m = ADAM_B1 * m + (1.0 - ADAM_B1) * g
    v = ADAM_B2 * v + (1.0 - ADAM_B2) * (g * g)
    m_hat = m / (1.0 - ADAM_B1 ** ADAM_STEP)
    v_hat = v / (1.0 - ADAM_B2 ** ADAM_STEP)
    delta = -ADAM_LR * (m_hat / (jnp.sqrt(v_hat) + ADAM_EPS) + ADAM_WD * w)
    return delta, m, v


def _adamw(name, g, w, m, v):
    rows, cols = g.shape
    tr = rows // 4 if rows % 32 == 0 else rows

    def body(g_ref, w_ref, m_ref, v_ref, go_ref, d_ref, mo_ref, vo_ref):
        gv = g_ref[...]
        delta, m2, v2 = _adamw_math(w_ref[...], gv, m_ref[...], v_ref[...])
        go_ref[...] = gv
        d_ref[...] = delta
        mo_ref[...] = m2
        vo_ref[...] = v2

    spec = pl.BlockSpec((tr, cols), lambda i: (i, 0))
    return pl.pallas_call(
        body,
        name="adamw_" + name,
        grid=(rows // tr,),
        in_specs=[spec] * 4,
        out_specs=[spec] * 4,
        out_shape=[jax.ShapeDtypeStruct((rows, cols), F32)] * 4,
        compiler_params=_params(("arbitrary",)),
    )(g, w, m, v)


SMALL_ROWS = 48
VEC_ROWS = {"final_norm_w": 1, "ffn_norm_w": 8, "mix_norm_w": 16, "conv_b": 24, "lru_ba": 25, "lru_bx": 26, "lru_lambda": 27}
VEC_NAMES = list(VEC_ROWS)
CONV_W_ROW = 28
META_ROW = 32


def _small_all_reduce(partial):
    def body(in_ref, out_ref, buf, local_sem, send_sem, recv_sem):
        x, y, c, s = _place()
        me = 2 * s + c
        cp = pltpu.make_async_copy(in_ref, buf.at[me], local_sem)
        cp.start()
        started = []
        for k in range(1, 8):
            peer = jnp.bitwise_xor(me, k)
            dev = (peer // 4, (peer // 2) % 2, peer % 2)
            rd = _remote(in_ref, buf.at[me], send_sem.at[k - 1], recv_sem.at[k - 1], dev)
            rd.start()
            started.append(rd)
        for k in range(1, 8):
            peer = jnp.bitwise_xor(me, k)
            dev = (peer // 4, (peer // 2) % 2, peer % 2)
            _remote(in_ref, buf.at[peer], send_sem.at[k - 1], recv_sem.at[k - 1], dev).wait_recv()
        for rd in started:
            rd.wait_send()
        cp.wait()
        total = buf[0]
        for d in range(1, 8):
            total = total + buf[d]
        out_ref[...] = total

    return pl.pallas_call(
        body,
        name="small_all_reduce",
        in_specs=[pl.BlockSpec(memory_space=pltpu.VMEM)],
        out_specs=pl.BlockSpec(memory_space=pltpu.VMEM),
        out_shape=jax.ShapeDtypeStruct((SMALL_ROWS, D), F32),
        scratch_shapes=[pltpu.VMEM((8, SMALL_ROWS, D), F32), pltpu.SemaphoreType.DMA,
                        pltpu.SemaphoreType.DMA((7,)), pltpu.SemaphoreType.DMA((7,))],
    )(partial)


def _small_update(total, place, vecs, conv, meta):
    nvec = len(VEC_NAMES)
    qcols = D // 4

    def body(pr, tot_ref, col_ref, *refs):
        vec_refs = refs[:3 * nvec]
        conv_refs = refs[3 * nvec:3 * nvec + 3]
        meta_refs = refs[3 * nvec + 3:3 * nvec + 6]
        outs = refs[3 * nvec + 6:]
        loss_ref, vec_out, conv_out, meta_out = outs[0], outs[1:5], outs[5:9], outs[9:13]
        loss_ref[...] = jnp.sum(tot_ref[0:1, :], axis=1, keepdims=True)
        for o in vec_out:
            o[...] = jnp.zeros_like(o)
        for j, name in enumerate(VEC_NAMES):
            w, m, v = (r[...] for r in vec_refs[3 * j:3 * j + 3])
            g = tot_ref[VEC_ROWS[name]:VEC_ROWS[name] + 1, :]
            if name == "lru_lambda":
                g = -g / (1.0 + jnp.exp(w))
            for o, val in zip(vec_out, (g,) + _adamw_math(w, g, m, v)):
                o[j:j + 1, :] = val
        for row, n_rows, ins, group in ((CONV_W_ROW, 4, conv_refs, conv_out), (META_ROW, N_META, meta_refs, meta_out)):
            g = col_ref[row:row + n_rows, :]
            w, m, v = (r[...] for r in ins)
            for o, val in zip(group, (g,) + _adamw_math(w, g, m, v)):
                o[...] = val

    whole = lambda shape: pl.BlockSpec(shape, lambda i, pr: (0,) * len(shape))
    in_specs = [whole((SMALL_ROWS, D)), pl.BlockSpec((SMALL_ROWS, qcols), lambda i, pr: (0, pr[0]))]
    in_specs += [whole((1, D))] * (3 * nvec) + [whole((4, qcols))] * 3 + [whole((N_META, qcols))] * 3
    out_shapes = [(1, 1)] + [(8, D)] * 4 + [(4, qcols)] * 4 + [(N_META, qcols)] * 4
    return pl.pallas_call(
        body,
        name="small_update",
        grid_spec=pltpu.PrefetchScalarGridSpec(num_scalar_prefetch=1, grid=(1,), in_specs=in_specs,
                                               out_specs=[whole(s) for s in out_shapes]),
        out_shape=[jax.ShapeDtypeStruct(s, F32) for s in out_shapes],
        compiler_params=_params(("arbitrary",)),
    )(place, total, total, *[a for t in vecs for a in t], *conv, *meta)


WEIGHT_ORDER = ["meta_tokens", "mix_norm_w", "w_in", "conv_w", "conv_b", "lru_wa", "lru_ba", "lru_wx", "lru_bx", "lru_lambda",
                "w_branch_ret", "w_branch_lru", "w_out", "ffn_norm_w", "w_ffn_in", "w_ffn_out", "final_norm_w"]


def kernel(x, meta_tokens, mix_norm_w, w_in, conv_w, conv_b, lru_wa, lru_ba, lru_wx, lru_bx, lru_lambda, w_branch_ret, w_branch_lru, w_out, ffn_norm_w, w_ffn_in, w_ffn_out, final_norm_w, loss_target, m_meta_tokens, m_mix_norm_w, m_w_in, m_conv_w, m_conv_b, m_lru_wa, m_lru_ba, m_lru_wx, m_lru_bx, m_lru_lambda, m_w_branch_ret, m_w_branch_lru, m_w_out, m_ffn_norm_w, m_w_ffn_in, m_w_ffn_out, m_final_norm_w, v_meta_tokens, v_mix_norm_w, v_w_in, v_conv_w, v_conv_b, v_lru_wa, v_lru_ba, v_lru_wx, v_lru_bx, v_lru_lambda, v_w_branch_ret, v_w_branch_lru, v_w_out, v_ffn_norm_w, v_w_ffn_in, v_w_ffn_out, v_final_norm_w):
    args = locals()
    wts = {n: args[n] for n in WEIGHT_ORDER}
    mom = {n: args["m_" + n] for n in WEIGHT_ORDER}
    var = {n: args["v_" + n] for n in WEIGHT_ORDER}
    place = jnp.stack([2 * lax.axis_index("x") + lax.axis_index("y"), lax.axis_index("c")]).astype(jnp.int32)

    shards = {n: wts[n][0].astype(BF16) for n in BIG_PIECES}
    shards["meta_tokens"] = wts["meta_tokens"]
    shards["conv_w"] = wts["conv_w"][0]
    plan = _CommPlan(shards, place)
    (first,) = _run_riders("gather_first", [_gather_rider(shards, FIRST_WEIGHTS)])
    w = dict(zip(FIRST_WEIGHTS, first))
    for n in VEC_NAMES:
        w[n] = wts[n].reshape(1, D)
    chips = jnp.bitwise_xor(place[0], jnp.arange(4, dtype=jnp.int32))
    w["route"] = jnp.concatenate([place, jnp.stack([2 * chips, 2 * chips + 1], axis=1).reshape(8)])
    w["w_in_shard"] = shards["w_in"]

    grad_x, grad_head, _, stats = _local_step(x[0], loss_target[0], w, plan)
    shard_grads = plan.finish()

    out = {}
    for n in BIG_PIECES:
        shape2d = (-1, wts[n].shape[-1])
        res = _adamw(n, *[a.reshape(shape2d) for a in (shard_grads[n], wts[n], mom[n], var[n])])
        out[n] = [r.reshape(wts[n].shape) for r in res]

    partial = jnp.concatenate(stats + [grad_head[PAD_ROWS:]], axis=0)
    total = _small_all_reduce(partial)
    vecs = [tuple(a[n].reshape(1, D) for a in (wts, mom, var)) for n in VEC_NAMES]
    conv = tuple(a["conv_w"][0] for a in (wts, mom, var))
    meta = tuple(a["meta_tokens"] for a in (wts, mom, var))
    res = _small_update(total, place, vecs, conv, meta)
    loss = res[0].reshape(())
    for j, n in enumerate(VEC_NAMES):
        out[n] = [r[j].reshape(wts[n].shape) for r in res[1:5]]
    out["conv_w"] = [r.reshape(wts["conv_w"].shape) for r in res[5:9]]
    out["meta_tokens"] = list(res[9:13])

    return (loss, grad_x.reshape(x.shape)) + tuple(out[n][kind] for kind in range(4) for n in WEIGHT_ORDER)
```

```python
import functools
import math

import jax
import jax.numpy as jnp
from jax import lax
from jax.experimental import pallas as pl
from jax.experimental.pallas import tpu as pltpu

F32 = jnp.float32
BF16 = jnp.bfloat16

D = 1024
HEADS = 8
DH = 128
CHUNK = 256
N_META = 16
FRONT = 256
PAD_ROWS = FRONT - N_META
LRU_BLOCKS = 4
LRU_BLOCK = 256
LRU_C = 8.0
FFN = 2816
FFN_HALF = FFN // 2
IN_COLS = 8 * D
ROPE_BASE = 10000.0
EPS = 1e-6
QK_SCALE = DH ** -0.5

ADAM_LR = 0.001
ADAM_B1 = 0.9
ADAM_B2 = 0.999
ADAM_EPS = 1e-08
ADAM_WD = 0.01
ADAM_STEP = 10

TM = 256
TM_HEAVY = 768
FFN_BLOCK = FFN // 2
TM_MIX_BWD = 256
VMEM_LIMIT = 60 * 1024 * 1024
TN_VMEM_BUDGET = 40 * 1024 * 1024

NT_DIMS = (((1,), (1,)), ((), ()))
TN_DIMS = (((0,), (0,)), ((), ()))
MESH = pl.DeviceIdType.MESH


def _params(sem=None):
    if sem is None:
        return pltpu.CompilerParams(vmem_limit_bytes=VMEM_LIMIT)
    return pltpu.CompilerParams(dimension_semantics=sem, vmem_limit_bytes=VMEM_LIMIT)


def _dot(a, b):
    return jnp.dot(a, b, preferred_element_type=F32)


def _dot_nt(a, b):
    return lax.dot_general(a, b, NT_DIMS, preferred_element_type=F32)


def _dot_tn(a, b):
    return lax.dot_general(a, b, TN_DIMS, preferred_element_type=F32)


def _sigmoid(z):
    return 1.0 / (1.0 + jnp.exp(-z))


def _log1p(x):
    return jnp.where(x < 1e-3, x * (1.0 - x * (0.5 - x * (1.0 / 3.0))), jnp.log(1.0 + x))


def _softplus(x):
    return jnp.maximum(x, 0.0) + _log1p(jnp.exp(-jnp.abs(x)))


def _neg_expm1(x):
    series = -x * (1.0 + x * (0.5 + x * (1.0 / 6.0 + x * (1.0 / 24.0 + x * (1.0 / 120.0)))))
    return jnp.where(x > -0.05, series, 1.0 - jnp.exp(x))


_GELU_K = math.sqrt(2.0 / math.pi)


def _gelu_and_grad(x):
    inner = _GELU_K * (x + 0.044715 * x * x * x)
    t = jnp.tanh(inner)
    val = 0.5 * x * (1.0 + t)
    grad = 0.5 * (1.0 + t) + 0.5 * x * (1.0 - t * t) * _GELU_K * (1.0 + 3.0 * 0.044715 * x * x)
    return val, grad


def _row_ids(i, rows, shape):
    return i * rows + lax.broadcasted_iota(jnp.int32, shape, 0)


def _rope_tables(rows):
    pos = jnp.arange(rows, dtype=jnp.int32) - PAD_ROWS
    inv_freq = ROPE_BASE ** (-jnp.arange(0, DH, 2, dtype=F32) / DH)
    ang = pos.astype(F32)[:, None] * inv_freq[None, :]
    cos, sin = jnp.cos(ang), jnp.sin(ang)
    return jnp.concatenate([cos, cos], axis=1), jnp.concatenate([-sin, sin], axis=1)


def _decay_tables():
    log_g = jnp.log(1.0 - 2.0 ** (-5.0 - jnp.arange(HEADS, dtype=F32)))
    idx = jnp.arange(CHUNK, dtype=F32)
    diff = idx[:, None] - idx[None, :]
    intra = jnp.where(diff[None] >= 0, jnp.exp(jnp.maximum(diff, 0.0)[None] * log_g[:, None, None]), 0.0)
    q_decay = jnp.exp((idx + 1.0)[:, None] * log_g[None, :])
    k_decay = jnp.exp((CHUNK - 1.0 - idx)[:, None] * log_g[None, :])
    chunk_decay = jnp.exp(CHUNK * log_g)
    wide = lambda a: jnp.repeat(a, DH, axis=-1)
    return intra, jnp.swapaxes(intra, 1, 2), wide(q_decay), wide(k_decay), wide(chunk_decay[None, :])


def _norm1(head, x2d, norm_w, riders=()):
    rows = FRONT + x2d.shape[0]

    def body(head_ref, x_ref, nw_ref, u_ref, rstd_ref):
        def norm(hv):
            rs = lax.rsqrt(jnp.mean(hv * hv, axis=-1, keepdims=True) + EPS)
            u_ref[...] = ((hv * rs) * nw_ref[...]).astype(BF16)
            rstd_ref[...] = rs

        @pl.when(pl.program_id(0) == 0)
        def _():
            norm(head_ref[...])

        @pl.when(pl.program_id(0) > 0)
        def _():
            norm(x_ref[...])

    return _hosted_call(
        body,
        name="norm1",
        grid=(rows // TM,),
        in_specs=[
            pl.BlockSpec((FRONT, D), lambda i: (0, 0)),
            pl.BlockSpec((TM, D), lambda i: (jnp.maximum(i - 1, 0), 0)),
            pl.BlockSpec((1, D), lambda i: (0, 0)),
        ],
        out_specs=[pl.BlockSpec((TM, D), lambda i: (i, 0)), pl.BlockSpec((TM, 1), lambda i: (i, 0))],
        out_shape=[jax.ShapeDtypeStruct((rows, D), BF16), jax.ShapeDtypeStruct((rows, 1), F32)],
        scratch_shapes=[],
        args=(head, x2d, norm_w),
        riders=riders,
    )


def _heavy_tile(rows):
    return TM_HEAVY if rows % TM_HEAVY == 0 else TM


def _in_proj(u, w_shard, route, cos_t, sin_t, riders=()):
    rows = u.shape[0]
    tm = _heavy_tile(rows)
    nt = rows // tm
    kind, shard = BIG_PIECES["w_in"]

    def body(route_ref, u_hbm, wsh_ref, cos_ref, sin_ref, proj_ref, wfull_ref,
             u_sc, w_sc, u_sem, w_sem, local_sem, ici_send, ici_recv, fwd_send, fwd_recv):
        g, i = pl.program_id(0), pl.program_id(1)
        s, c = route_ref[0], route_ref[1]
        gid = route_ref[2 + g]
        sibling = (s // 2, s % 2, 1 - c)
        local = pltpu.make_async_copy(wsh_ref, _full_region(wfull_ref, kind, shard, s, None), local_sem)
        u_copies = [pltpu.make_async_copy(u_hbm.at[pl.ds(t * tm, tm)], u_sc.at[t], u_sem.at[t]) for t in range(nt)]
        sends, arrivals = [], []
        for k in (1, 2, 3):
            s2, dev = _other_chip(s, c, k)
            sends.append(_remote(_shard_region(wsh_ref, shard, c), _full_region(wfull_ref, kind, shard, s, c),
                                 ici_send.at[k - 1], ici_recv.at[k - 1], dev))
            mine = _full_region(wfull_ref, kind, shard, s2, c)
            theirs = _full_region(wfull_ref, kind, shard, s2, 1 - c)
            arrivals.append((_remote(mine, mine, ici_send.at[k - 1], ici_recv.at[k - 1], dev),
                             _remote(mine, mine, fwd_send.at[k - 1], fwd_recv.at[k - 1], sibling),
                             _remote(theirs, theirs, fwd_send.at[k - 1], fwd_recv.at[k - 1], sibling)))

        @pl.when(jnp.logical_and(g == 0, i == 0))
        def _():
            for cp in sends:
                cp.start()
            local.start()
            for cp in u_copies:
                cp.start()

        for k, (arrived, forward, forwarded) in zip((1, 2, 3), arrivals):
            @pl.when(jnp.logical_and(g == 2 * k, i == 0))
            def _():
                arrived.wait_recv()
                forward.start()
                forwarded.wait_recv()

        @pl.when(jnp.logical_and(i == 0, g < 2))
        def _():
            cp = pltpu.make_async_copy(wsh_ref.at[:, pl.ds(pl.multiple_of(g * D, D), D)], w_sc, w_sem)
            cp.start()
            cp.wait()

        @pl.when(jnp.logical_and(i == 0, g >= 2))
        def _():
            cp = pltpu.make_async_copy(wfull_ref.at[:, pl.ds(pl.multiple_of(gid * D, D), D)], w_sc, w_sem)
            cp.start()
            cp.wait()

        for t in range(nt):
            @pl.when(jnp.logical_and(g == 0, i == t))
            def _():
                u_copies[t].wait()

        acc = _dot(u_sc[i], w_sc[...])

        @pl.when(gid < 2)
        def _():
            scale = jnp.where(gid == 1, QK_SCALE, 1.0).astype(F32)
            for h in range(HEADS):
                sl = slice(h * DH, (h + 1) * DH)
                blk = acc[:, sl]
                out = (blk * cos_ref[...] + pltpu.roll(blk, DH // 2, axis=1) * sin_ref[...]) * scale
                proj_ref[:, sl] = out.astype(BF16)

        @pl.when(gid >= 2)
        def _():
            proj_ref[...] = acc.astype(BF16)

        @pl.when(jnp.logical_and(g == 7, i == nt - 1))
        def _():
            local.wait()
            for cp in sends:
                cp.wait_send()
            for _, forward, _ in arrivals:
                forward.wait_send()

    return _hosted_call(
        body,
        name="in_proj",
        grid=(8, nt),
        in_specs=[
            _ANY, _ANY,
            pl.BlockSpec((tm, DH), lambda g, i, rt: (i, 0)),
            pl.BlockSpec((tm, DH), lambda g, i, rt: (i, 0)),
        ],
        out_specs=[pl.BlockSpec((tm, D), lambda g, i, rt: (i, rt[2 + g])), _ANY],
        out_shape=[jax.ShapeDtypeStruct((rows, IN_COLS), BF16), jax.ShapeDtypeStruct((D, IN_COLS), BF16)],
        scratch_shapes=[
            pltpu.VMEM((nt, tm, D), BF16), pltpu.VMEM((D, D), BF16),
            pltpu.SemaphoreType.DMA((nt,)), pltpu.SemaphoreType.DMA, pltpu.SemaphoreType.DMA,
            pltpu.SemaphoreType.DMA((3,)), pltpu.SemaphoreType.DMA((3,)),
            pltpu.SemaphoreType.DMA((3,)), pltpu.SemaphoreType.DMA((3,)),
        ],
        args=(u, w_shard, cos_t, sin_t),
        riders=riders,
        prefetch=route,
        riders_after_body=True,
    )


def _retention_fwd(proj_bf, intra, q_dec, k_dec, c_dec, riders=()):
    rows = proj_bf.shape[0]
    nc = rows // CHUNK

    def body(q_ref, k_ref, v_ref, m_ref, qd_ref, kd_ref, cd_ref, o_ref, sprev_ref, s_sc):
        @pl.when(pl.program_id(0) == 0)
        def _():
            s_sc[...] = jnp.zeros_like(s_sc)

        for h in range(HEADS):
            sl = slice(h * DH, (h + 1) * DH)
            q, k, v = q_ref[:, sl], k_ref[:, sl], v_ref[:, sl]
            state = s_sc[h]
            state_b = state.astype(BF16)
            sprev_ref[0, h] = state_b
            s = _dot_nt(q, k) * m_ref[h]
            inner = _dot(s.astype(BF16), v)
            cross = _dot(q, state_b) * qd_ref[:, sl]
            o_ref[:, sl] = (inner + cross).astype(BF16)
            k_scaled = (k.astype(F32) * kd_ref[:, sl]).astype(BF16)
            s_sc[h] = state * cd_ref[:, sl] + _dot_tn(k_scaled, v)

    chunk_spec = lambda col: pl.BlockSpec((CHUNK, D), lambda c: (c, col))
    const2 = lambda shape: pl.BlockSpec(shape, lambda c: (0, 0))
    return _hosted_call(
        body,
        name="retention_fwd",
        grid=(nc,),
        in_specs=[
            chunk_spec(0), chunk_spec(1), chunk_spec(2),
            pl.BlockSpec((HEADS, CHUNK, CHUNK), lambda c: (0, 0, 0)),
            const2((CHUNK, D)), const2((CHUNK, D)), const2((1, D)),
        ],
        out_specs=[
            pl.BlockSpec((CHUNK, D), lambda c: (c, 0)),
            pl.BlockSpec((1, HEADS, DH, DH), lambda c: (c, 0, 0, 0)),
        ],
        out_shape=[
            jax.ShapeDtypeStruct((rows, D), BF16),
            jax.ShapeDtypeStruct((nc, HEADS, DH, DH), BF16),
        ],
        scratch_shapes=[pltpu.VMEM((HEADS, DH, DH), F32)],
        args=(proj_bf, proj_bf, proj_bf, intra, q_dec, k_dec, c_dec),
        riders=riders,
    )


def _shift_down(x, first8_prev, d):
    rolled = pltpu.roll(x, d, axis=0)
    head = pltpu.roll(jnp.concatenate([first8_prev, x[0:8]], axis=0), d, axis=0)[8:16]
    return rolled, head


def _conv_and_gates(x, prev8, cw_ref, cb_ref, wa_ref, wx_ref, ba_ref, bx_ref, lam_ref, c_sc):
    cw = cw_ref[...]
    conv = cb_ref[...] + cw[3:4] * x
    head = cb_ref[...] + cw[3:4] * x[0:8]
    for d in (1, 2, 3):
        rolled, hd = _shift_down(x, prev8, d)
        conv = conv + cw[3 - d:4 - d] * rolled
        head = head + cw[3 - d:4 - d] * hd
    c_sc[...] = conv
    c_sc[0:8, :] = head
    c = c_sc[...]
    zr, zi = [], []
    for g in range(LRU_BLOCKS):
        sl = slice(g * LRU_BLOCK, (g + 1) * LRU_BLOCK)
        cg = c[:, sl].astype(BF16)
        zr.append(_dot(cg, wa_ref[g]))
        zi.append(_dot(cg, wx_ref[g]))
    r = _sigmoid(jnp.concatenate(zr, axis=1) + ba_ref[...])
    gate_i = _sigmoid(jnp.concatenate(zi, axis=1) + bx_ref[...])
    sp = _softplus(-lam_ref[...])
    log_a = (-LRU_C) * r * sp
    a = jnp.exp(log_a)
    mult = jnp.sqrt(_neg_expm1(2.0 * log_a))
    return c, r, gate_i, a, mult, log_a


def _lru_fwd(proj, conv_w, conv_b, wa, wx, ba, bx, lam, riders=()):
    rows = proj.shape[0]
    nt = rows // TM

    def body(x_ref, cw_ref, cb_ref, wa_ref, wx_ref, ba_ref, bx_ref, lam_ref,
             h_ref, c_ref, r_ref, i_ref, la_ref, mult_ref, prev_sc, carry_sc, c_sc, a_sc, u_sc, h_sc):
        i = pl.program_id(0)

        @pl.when(i == 0)
        def _():
            prev_sc[...] = jnp.zeros_like(prev_sc)
            carry_sc[...] = jnp.zeros_like(carry_sc)

        x = x_ref[...].astype(F32)
        c, r, gate_i, a, mult, log_a = _conv_and_gates(x, prev_sc[...], cw_ref, cb_ref, wa_ref, wx_ref, ba_ref, bx_ref,
                                                       lam_ref, c_sc)
        for ref, val in ((c_ref, c), (r_ref, r), (i_ref, gate_i), (la_ref, log_a), (mult_ref, mult)):
            ref[...] = val.astype(BF16)
        prev_sc[...] = x[TM - 8:TM]
        valid = _row_ids(i, TM, (TM, D)) >= PAD_ROWS
        a_sc[...] = a
        u_sc[...] = jnp.where(valid, mult * gate_i * c, 0.0)
        row8 = lax.broadcasted_iota(jnp.int32, (8, D), 0)

        def group(gi, hprev):
            r0 = pl.multiple_of(gi * 8, 8)
            aa = a_sc[pl.ds(r0, 8), :]
            uu = u_sc[pl.ds(r0, 8), :]
            for d in (1, 2, 4):
                a_sh = jnp.where(row8 >= d, pltpu.roll(aa, d, axis=0), 1.0)
                u_sh = jnp.where(row8 >= d, pltpu.roll(uu, d, axis=0), 0.0)
                uu = uu + aa * u_sh
                aa = aa * a_sh
            hb = aa * hprev + uu
            h_sc[pl.ds(r0, 8), :] = hb
            return hb[7:8, :]

        hlast = lax.fori_loop(0, TM // 8, group, carry_sc[0:1, :])
        carry_sc[0:1, :] = hlast
        h_ref[...] = h_sc[...].astype(BF16)

    vec = pl.BlockSpec((1, D), lambda i: (0, 0))
    wspec = pl.BlockSpec((LRU_BLOCKS, LRU_BLOCK, LRU_BLOCK), lambda i: (0, 0, 0))
    return _hosted_call(
        body,
        name="lru_fwd",
        grid=(nt,),
        in_specs=[
            pl.BlockSpec((TM, D), lambda i: (i, 4)),
            pl.BlockSpec((4, D), lambda i: (0, 0)),
            vec, wspec, wspec, vec, vec, vec,
        ],
        out_specs=[pl.BlockSpec((TM, D), lambda i: (i, 0)) for _ in range(6)],
        out_shape=[jax.ShapeDtypeStruct((rows, D), BF16) for _ in range(6)],
        scratch_shapes=[
            pltpu.VMEM((8, D), F32), pltpu.VMEM((8, D), F32),
            pltpu.VMEM((TM, D), F32), pltpu.VMEM((TM, D), F32), pltpu.VMEM((TM, D), F32), pltpu.VMEM((TM, D), F32),
        ],
        args=(proj, conv_w, conv_b, wa, wx, ba, bx, lam),
        riders=riders,
    )


def _group_norm(o):
    outs, rstds = [], []
    for h in range(HEADS):
        oh = o[:, h * DH:(h + 1) * DH]
        rs = lax.rsqrt(jnp.mean(oh * oh, axis=-1, keepdims=True) + EPS)
        outs.append(oh * rs)
        rstds.append(rs)
    return jnp.concatenate(outs, axis=1), rstds


def _mix_fwd(head, x2d, o, proj, h_lru, w_br, w_bl, w_o, ffn_norm_w, riders=()):
    rows = o.shape[0]
    nt = rows // TM

    def body(head_ref, x_ref, o_ref, gret_ref, lgate_ref, ga_ref, gb_ref, hl_ref, wbr_ref, wbl_ref, wo_ref, nw_ref,
             yret_ref, ylru_ref, h1_ref, u2_ref, rstd_ref):
        i = pl.program_id(0)
        on, _ = _group_norm(o_ref[...].astype(F32))
        gret = gret_ref[...].astype(F32)
        a_ret = (gret * _sigmoid(gret) * on).astype(BF16)
        y_ret = _dot(a_ret, wbr_ref[...])
        gl, _ = _gelu_and_grad(lgate_ref[...].astype(F32))
        a_lru = (gl * hl_ref[...].astype(F32)).astype(BF16)
        y_lru = _dot(a_lru, wbl_ref[...])
        mixed = (_sigmoid(ga_ref[...].astype(F32)) * y_ret + _sigmoid(gb_ref[...].astype(F32)) * y_lru).astype(BF16)
        delta = _dot(mixed, wo_ref[...])
        yret_ref[...] = y_ret.astype(BF16)
        ylru_ref[...] = y_lru.astype(BF16)

        def finish(h0):
            h1 = h0 + delta
            rs = lax.rsqrt(jnp.mean(h1 * h1, axis=-1, keepdims=True) + EPS)
            h1_ref[...] = h1
            u2_ref[...] = ((h1 * rs) * nw_ref[...]).astype(BF16)
            rstd_ref[...] = rs

        @pl.when(i == 0)
        def _():
            finish(head_ref[...])

        @pl.when(i > 0)
        def _():
            finish(x_ref[...])

    tile = lambda col: pl.BlockSpec((TM, D), lambda i: (i, col))
    wspec = pl.BlockSpec((D, D), lambda i: (0, 0))
    return _hosted_call(
        body,
        name="mix_fwd",
        grid=(nt,),
        in_specs=[
            pl.BlockSpec((FRONT, D), lambda i: (0, 0)),
            pl.BlockSpec((TM, D), lambda i: (jnp.maximum(i - 1, 0), 0)),
            tile(0), tile(3), tile(5), tile(6), tile(7), tile(0),
            wspec, wspec, wspec,
            pl.BlockSpec((1, D), lambda i: (0, 0)),
        ],
        out_specs=[tile(0), tile(0), tile(0), tile(0), pl.BlockSpec((TM, 1), lambda i: (i, 0))],
        out_shape=[
            jax.ShapeDtypeStruct((rows, D), BF16), jax.ShapeDtypeStruct((rows, D), BF16),
            jax.ShapeDtypeStruct((rows, D), F32), jax.ShapeDtypeStruct((rows, D), BF16),
            jax.ShapeDtypeStruct((rows, 1), F32),
        ],
        scratch_shapes=[],
        args=(head, x2d, o, proj, proj, proj, proj, h_lru, w_br, w_bl, w_o, ffn_norm_w),
        riders=riders,
    )


def _ffn_fwd_gate(u2, w_ffn_in, riders=()):
    rows = u2.shape[0]
    tm = _heavy_tile(rows)
    nb = FFN // FFN_BLOCK
    hid = lambda: pl.BlockSpec((tm, FFN_BLOCK), lambda i, j: (i, j))

    def body(u2_ref, w_hbm, silu_ref, dsilu_ref, act_ref, w_sc, w_sem):
        j = pl.program_id(1)

        @pl.when(jnp.logical_and(pl.program_id(0) == 0, j == 0))
        def _():
            cp = pltpu.make_async_copy(w_hbm, w_sc, w_sem)
            cp.start()
            cp.wait()

        u2 = u2_ref[...]
        g = _dot(u2, w_sc[:, pl.ds(pl.multiple_of(j * FFN_BLOCK, 128), FFN_BLOCK)])
        up = _dot(u2, w_sc[:, pl.ds(pl.multiple_of(FFN + j * FFN_BLOCK, 128), FFN_BLOCK)])
        sg = _sigmoid(g)
        silu = g * sg
        silu_ref[...] = silu.astype(BF16)
        dsilu_ref[...] = (up * (sg * (1.0 + g * (1.0 - sg)))).astype(BF16)
        act_ref[...] = (silu * up).astype(BF16)

    return _hosted_call(
        body,
        name="ffn_fwd_gate",
        grid=(rows // tm, nb),
        in_specs=[pl.BlockSpec((tm, D), lambda i, j: (i, 0)), _ANY],
        out_specs=[hid(), hid(), hid()],
        out_shape=[jax.ShapeDtypeStruct((rows, FFN), BF16)] * 3,
        scratch_shapes=[pltpu.VMEM((D, 2 * FFN), BF16), pltpu.SemaphoreType.DMA],
        args=(u2, w_ffn_in),
        riders=riders,
    )


def _ffn_fwd_out(act, h1, w_ffn_out):
    rows = act.shape[0]
    tm = _heavy_tile(rows)

    def body(act_ref, h1_ref, wo_ref, h2_ref):
        h2_ref[...] = h1_ref[...] + _dot(act_ref[...], wo_ref[...])

    return pl.pallas_call(
        body,
        name="ffn_fwd_out",
        grid=(rows // tm,),
        in_specs=[
            pl.BlockSpec((tm, FFN), lambda i: (i, 0)),
            pl.BlockSpec((tm, D), lambda i: (i, 0)),
            pl.BlockSpec((FFN, D), lambda i: (0, 0)),
        ],
        out_specs=pl.BlockSpec((tm, D), lambda i: (i, 0)),
        out_shape=jax.ShapeDtypeStruct((rows, D), F32),
        compiler_params=_params(("arbitrary",)),
    )(act, h1, w_ffn_out)


def _final_loss(h2, target, final_norm_w):
    rows = h2.shape[0]

    def body(h2_ref, tgt_ref, fnw_ref, dh2_ref, stats_ref):
        i = pl.program_id(0)

        @pl.when(i == 0)
        def _():
            stats_ref[...] = jnp.zeros_like(stats_ref)

        h2 = h2_ref[...]
        rs = lax.rsqrt(jnp.mean(h2 * h2, axis=-1, keepdims=True) + EPS)
        n = h2 * rs
        fnw = fnw_ref[...]
        valid = _row_ids(i, TM, (TM, D)) >= FRONT
        diff = jnp.where(valid, n * fnw - tgt_ref[...], 0.0)
        dy = diff * (1.0 / D)
        stats_ref[0:1, :] += (0.5 / D) * jnp.sum(diff * diff, axis=0, keepdims=True)
        stats_ref[1:2, :] += jnp.sum(dy * n, axis=0, keepdims=True)
        dn = dy * fnw
        dh2_ref[...] = rs * (dn - n * jnp.mean(dn * n, axis=-1, keepdims=True))

    return pl.pallas_call(
        body,
        name="final_loss",
        grid=(rows // TM,),
        in_specs=[
            pl.BlockSpec((TM, D), lambda i: (i, 0)),
            pl.BlockSpec((TM, D), lambda i: (jnp.maximum(i - 1, 0), 0)),
            pl.BlockSpec((1, D), lambda i: (0, 0)),
        ],
        out_specs=[pl.BlockSpec((TM, D), lambda i: (i, 0)), pl.BlockSpec((8, D), lambda i: (0, 0))],
        out_shape=[jax.ShapeDtypeStruct((rows, D), F32), jax.ShapeDtypeStruct((8, D), F32)],
        compiler_params=_params(("arbitrary",)),
    )(h2, target, final_norm_w)


def _ffn_bwd(dh2, silu, dsilu, h1, rstd2, w_ffn_in, w_ffn_out, ffn_norm_w):
    rows = dh2.shape[0]
    tm = _heavy_tile(rows)
    nb = FFN // FFN_BLOCK
    blk = lambda: pl.BlockSpec((tm, FFN_BLOCK), lambda i, j: (i, j))

    def gate_body(dh2_ref, silu_ref, dsilu_ref, wo_hbm, dg_ref, dup_ref, dh2b_sc, wo_sc, wo_sem):
        j = pl.program_id(1)

        @pl.when(jnp.logical_and(pl.program_id(0) == 0, j == 0))
        def _():
            cp = pltpu.make_async_copy(wo_hbm, wo_sc, wo_sem)
            cp.start()
            cp.wait()

        @pl.when(j == 0)
        def _():
            dh2b_sc[...] = dh2_ref[...].astype(BF16)

        dact = _dot_nt(dh2b_sc[...], wo_sc[pl.ds(pl.multiple_of(j * FFN_BLOCK, 128), FFN_BLOCK), :])
        dup_ref[...] = (dact * silu_ref[...].astype(F32)).astype(BF16)
        dg_ref[...] = (dact * dsilu_ref[...].astype(F32)).astype(BF16)

    dg, dup = pl.pallas_call(
        gate_body,
        name="ffn_bwd_gate",
        grid=(rows // tm, nb),
        in_specs=[pl.BlockSpec((tm, D), lambda i, j: (i, 0)), blk(), blk(), _ANY],
        out_specs=[blk(), blk()],
        out_shape=[jax.ShapeDtypeStruct((rows, FFN), BF16)] * 2,
        scratch_shapes=[pltpu.VMEM((tm, D), BF16), pltpu.VMEM((FFN, D), BF16), pltpu.SemaphoreType.DMA],
        compiler_params=_params(("arbitrary", "arbitrary")),
    )(dh2, silu, dsilu, w_ffn_out)

    def body(dg_ref, dup_ref, dh2_ref, h1_ref, rstd_ref, w_hbm, nw_ref, dh1_ref, stats_ref, w_sc, w_sem):
        @pl.when(pl.program_id(0) == 0)
        def _():
            stats_ref[...] = jnp.zeros_like(stats_ref)
            cp = pltpu.make_async_copy(w_hbm, w_sc, w_sem)
            cp.start()
            cp.wait()

        du = _dot_nt(dg_ref[...], w_sc[:, 0:FFN]) + _dot_nt(dup_ref[...], w_sc[:, FFN:2 * FFN])
        rs = rstd_ref[...]
        n = h1_ref[...] * rs
        stats_ref[0:1, :] += jnp.sum(du * n, axis=0, keepdims=True)
        dn = du * nw_ref[...]
        dh1_ref[...] = dh2_ref[...] + rs * (dn - n * jnp.mean(dn * n, axis=-1, keepdims=True))

    row = lambda width: pl.BlockSpec((tm, width), lambda i: (i, 0))
    dh1, stats = pl.pallas_call(
        body,
        name="ffn_bwd_in",
        grid=(rows // tm,),
        in_specs=[row(FFN), row(FFN), row(D), row(D), row(1), _ANY, pl.BlockSpec((1, D), lambda i: (0, 0))],
        out_specs=[row(D), pl.BlockSpec((8, D), lambda i: (0, 0))],
        out_shape=[jax.ShapeDtypeStruct((rows, D), F32), jax.ShapeDtypeStruct((8, D), F32)],
        scratch_shapes=[pltpu.VMEM((D, 2 * FFN), BF16), pltpu.SemaphoreType.DMA],
        compiler_params=_params(("arbitrary",)),
    )(dg, dup, dh2, h1, rstd2, w_ffn_in, ffn_norm_w)
    return dg, dup, dh1, stats


def _mix_bwd(dh1, o, proj, h_lru, y_ret, y_lru, w_br, w_bl, w_o, riders=()):
    rows = dh1.shape[0]
    tm = TM_MIX_BWD
    nt = rows // tm

    def body(dh1_ref, o_ref, gret_ref, lgate_ref, ga_ref, gb_ref, hl_ref, yret_ref, ylru_ref, wbr_ref, wbl_ref, wo_ref,
             dproj_ref, do_ref, dhl_ref, mixed_ref, aret_ref, alru_ref, dyret_ref, dylru_ref):
        dmixed = _dot_nt(dh1_ref[...].astype(BF16), wo_ref[...])
        y_ret, y_lru = yret_ref[...].astype(F32), ylru_ref[...].astype(F32)
        sa, sb = _sigmoid(ga_ref[...].astype(F32)), _sigmoid(gb_ref[...].astype(F32))
        mixed_ref[...] = (sa * y_ret + sb * y_lru).astype(BF16)
        dga = dmixed * y_ret * sa * (1.0 - sa)
        dgb = dmixed * y_lru * sb * (1.0 - sb)
        dy_ret = (dmixed * sa).astype(BF16)
        dy_lru = (dmixed * sb).astype(BF16)
        dyret_ref[...] = dy_ret
        dylru_ref[...] = dy_lru
        da_ret = _dot_nt(dy_ret, wbr_ref[...])
        da_lru = _dot_nt(dy_lru, wbl_ref[...])

        gret = gret_ref[...].astype(F32)
        sg = _sigmoid(gret)
        silu = gret * sg
        on, rstds = _group_norm(o_ref[...].astype(F32))
        aret_ref[...] = (silu * on).astype(BF16)
        dgret = da_ret * on * (sg * (1.0 + gret * (1.0 - sg)))
        don = da_ret * silu
        for h in range(HEADS):
            sl = slice(h * DH, (h + 1) * DH)
            onh, donh = on[:, sl], don[:, sl]
            do_ref[:, sl] = (rstds[h] * (donh - onh * jnp.mean(donh * onh, axis=-1, keepdims=True))).astype(BF16)

        gl, gl_grad = _gelu_and_grad(lgate_ref[...].astype(F32))
        hl = hl_ref[...].astype(F32)
        alru_ref[...] = (gl * hl).astype(BF16)
        dlgate = da_lru * hl * gl_grad
        dhl_ref[...] = (da_lru * gl).astype(BF16)

        zeros = jnp.zeros((tm, D), BF16)
        for col in (0, 1, 2, 4):
            dproj_ref[:, col * D:(col + 1) * D] = zeros
        dproj_ref[:, 3 * D:4 * D] = dgret.astype(BF16)
        dproj_ref[:, 5 * D:6 * D] = dlgate.astype(BF16)
        dproj_ref[:, 6 * D:7 * D] = dga.astype(BF16)
        dproj_ref[:, 7 * D:8 * D] = dgb.astype(BF16)

    tile = lambda col: pl.BlockSpec((tm, D), lambda i: (i, col))
    wspec = pl.BlockSpec((D, D), lambda i: (0, 0))
    bf = lambda: jax.ShapeDtypeStruct((rows, D), BF16)
    return _hosted_call(
        body,
        name="mix_bwd",
        grid=(nt,),
        in_specs=[tile(0), tile(0), tile(3), tile(5), tile(6), tile(7), tile(0), tile(0), tile(0), wspec, wspec, wspec],
        out_specs=[pl.BlockSpec((tm, IN_COLS), lambda i: (i, 0))] + [tile(0)] * 7,
        out_shape=[jax.ShapeDtypeStruct((rows, IN_COLS), BF16)] + [bf() for _ in range(7)],
        scratch_shapes=[],
        args=(dh1, o, proj, proj, proj, proj, h_lru, y_ret, y_lru, w_br, w_bl, w_o),
        riders=riders,
    )


def _retention_bwd(dproj, proj_bf, do, sprev, intra, intra_t, q_dec, k_dec, c_dec, cos_t, sin_t, riders=()):
    rows = proj_bf.shape[0]
    nc = rows // CHUNK

    def body(dproj_in_ref, q_ref, k_ref, v_ref, do_ref, sprev_ref, m_ref, mt_ref, qd_ref, kd_ref, cd_ref, cos_ref, sin_ref,
             dproj_ref, ds_sc):
        @pl.when(pl.program_id(0) == 0)
        def _():
            ds_sc[...] = jnp.zeros_like(ds_sc)

        cos, sin = cos_ref[...], sin_ref[...]

        def unrotate(dy):
            return dy * cos - pltpu.roll(dy, DH // 2, axis=1) * sin

        for h in range(HEADS):
            sl = slice(h * DH, (h + 1) * DH)
            q, k, v = q_ref[:, sl], k_ref[:, sl], v_ref[:, sl]
            do = do_ref[:, sl]
            dob = do.astype(BF16)
            doq = (do * qd_ref[:, sl]).astype(BF16)
            state_prev = sprev_ref[0, h]
            dstate = ds_sc[h]
            dstate_b = dstate.astype(BF16)
            s_t = (_dot_nt(k, q) * mt_ref[h]).astype(BF16)
            ds_t = (_dot_nt(v, dob) * mt_ref[h]).astype(BF16)
            ds = (_dot_nt(dob, v) * m_ref[h]).astype(BF16)
            kd = kd_ref[:, sl]
            dq = _dot(ds, k) + _dot_nt(doq, state_prev)
            dk = _dot(ds_t, q) + _dot_nt(v, dstate_b) * kd
            k_scaled = (k.astype(F32) * kd).astype(BF16)
            dv = _dot(s_t, dob) + _dot(k_scaled, dstate_b)
            ds_sc[h] = dstate * cd_ref[:, sl] + _dot_tn(q, doq)
            dproj_ref[:, sl] = unrotate(dq).astype(BF16)
            dproj_ref[:, D + h * DH:D + (h + 1) * DH] = (unrotate(dk) * QK_SCALE).astype(BF16)
            dproj_ref[:, 2 * D + h * DH:2 * D + (h + 1) * DH] = dv.astype(BF16)

    rev = lambda c: nc - 1 - c
    chunk_spec = lambda col: pl.BlockSpec((CHUNK, D), lambda c: (rev(c), col))
    const2 = lambda shape: pl.BlockSpec(shape, lambda c: (0, 0))
    const3 = pl.BlockSpec((HEADS, CHUNK, CHUNK), lambda c: (0, 0, 0))
    return _hosted_call(
        body,
        name="retention_bwd",
        grid=(nc,),
        in_specs=[
            pl.BlockSpec(memory_space=pl.ANY),
            chunk_spec(0), chunk_spec(1), chunk_spec(2), chunk_spec(0),
            pl.BlockSpec((1, HEADS, DH, DH), lambda c: (rev(c), 0, 0, 0)),
            const3, const3,
            const2((CHUNK, D)), const2((CHUNK, D)), const2((1, D)),
            pl.BlockSpec((CHUNK, DH), lambda c: (rev(c), 0)),
            pl.BlockSpec((CHUNK, DH), lambda c: (rev(c), 0)),
        ],
        out_specs=[pl.BlockSpec((CHUNK, 3 * D), lambda c: (rev(c), 0))],
        out_shape=[jax.ShapeDtypeStruct(dproj.shape, BF16)],
        aliases={0: 0},
        scratch_shapes=[pltpu.VMEM((HEADS, DH, DH), F32)],
        args=(dproj, proj_bf, proj_bf, proj_bf, do, sprev, intra, intra_t, q_dec, k_dec, c_dec, cos_t, sin_t),
        riders=riders,
    )


def _lru_bwd(dproj, proj, saved, dhl, conv_w, wa, wx, lam, riders=()):
    rows = proj.shape[0]
    nt = rows // TM
    per8 = TM // 8

    def body(dproj_in_ref, x_ref, xprev_ref, h_ref, hprev_ref, c_ref, r_ref, i_ref, la_ref, mult_ref, dhl_ref,
             cw_ref, wa_ref, wx_ref, lam_ref,
             dproj_ref, dwa_ref, dwx_ref, stats_ref, anext_sc, dhnext_sc, dcnext_sc, c_sc, b_sc, dh_sc):
        step = pl.program_id(0)
        i = nt - 1 - step

        @pl.when(step == 0)
        def _():
            anext_sc[...] = jnp.zeros_like(anext_sc)
            dhnext_sc[...] = jnp.zeros_like(dhnext_sc)
            dcnext_sc[...] = jnp.zeros_like(dcnext_sc)
            dwa_ref[...] = jnp.zeros_like(dwa_ref)
            dwx_ref[...] = jnp.zeros_like(dwx_ref)
            stats_ref[...] = jnp.zeros_like(stats_ref)

        first = i == 0
        x = x_ref[...].astype(F32)
        prev8 = jnp.where(first, 0.0, xprev_ref[8:16, :].astype(F32))
        c_b = c_ref[...]
        c, r, gate_i, mult = (ref[...].astype(F32) for ref in (c_ref, r_ref, i_ref, mult_ref))
        a = jnp.exp(la_ref[...].astype(F32))
        sp = _softplus(-lam_ref[...])
        dh_sc[...] = dhl_ref[...].astype(F32)

        b_sc[...] = pltpu.roll(a, TM - 1, axis=0)
        b_sc[TM - 1:TM, :] = anext_sc[0:1, :]
        anext_sc[0:1, :] = a[0:1, :]
        row8 = lax.broadcasted_iota(jnp.int32, (8, D), 0)

        def group(gi, dhnext):
            r0 = pl.multiple_of((per8 - 1 - gi) * 8, 8)
            bb = b_sc[pl.ds(r0, 8), :]
            uu = dh_sc[pl.ds(r0, 8), :]
            for d in (1, 2, 4):
                b_sh = jnp.where(row8 < 8 - d, pltpu.roll(bb, 8 - d, axis=0), 1.0)
                u_sh = jnp.where(row8 < 8 - d, pltpu.roll(uu, 8 - d, axis=0), 0.0)
                uu = uu + bb * u_sh
                bb = bb * b_sh
            dhb = bb * dhnext + uu
            dh_sc[pl.ds(r0, 8), :] = dhb
            return dhb[0:1, :]

        dhfirst = lax.fori_loop(0, per8, group, dhnext_sc[0:1, :])
        dhnext_sc[0:1, :] = dhfirst
        dh = dh_sc[...]

        h = h_ref[...].astype(F32)
        hprev8 = jnp.where(first, 0.0, hprev_ref[8:16, :].astype(F32))
        h_dn, h_head = _shift_down(h, hprev8, 1)
        c_sc[...] = h_dn
        c_sc[0:8, :] = h_head
        h_before = c_sc[...]

        valid = _row_ids(i, TM, (TM, D)) >= PAD_ROWS
        da = dh * h_before
        du = jnp.where(valid, dh, 0.0)
        dmult = du * gate_i * c
        dgate_i = du * mult * c
        dc = du * mult * gate_i
        dla = da * a - dmult * (a * a) / mult
        dla = jnp.where(valid, dla, 0.0)
        dr = dla * ((-LRU_C) * sp)
        dzr = dr * r * (1.0 - r)
        dzi = dgate_i * gate_i * (1.0 - gate_i)
        stats_ref[1:2, :] += jnp.sum(dzr, axis=0, keepdims=True)
        stats_ref[2:3, :] += jnp.sum(dzi, axis=0, keepdims=True)
        stats_ref[3:4, :] += jnp.sum(dla * ((-LRU_C) * r), axis=0, keepdims=True)
        dc_gate = []
        for g in range(LRU_BLOCKS):
            sl = slice(g * LRU_BLOCK, (g + 1) * LRU_BLOCK)
            cg = c_b[:, sl]
            dzr_g = dzr[:, sl].astype(BF16)
            dzi_g = dzi[:, sl].astype(BF16)
            dc_gate.append(_dot_nt(dzr_g, wa_ref[g]) + _dot_nt(dzi_g, wx_ref[g]))
            dwa_ref[g] += _dot_tn(cg, dzr_g)
            dwx_ref[g] += _dot_tn(cg, dzi_g)
        dc = dc + jnp.concatenate(dc_gate, axis=1)

        cw = cw_ref[...]
        stats_ref[0:1, :] += jnp.sum(dc, axis=0, keepdims=True)
        stats_ref[7:8, :] += jnp.sum(dc * x, axis=0, keepdims=True)
        dx = cw[3:4] * dc
        tail_src = jnp.concatenate([dc[TM - 8:TM], dcnext_sc[...]], axis=0)
        dx_tail = cw[3:4] * dc[TM - 8:TM]
        for d in (1, 2, 3):
            dx = dx + cw[3 - d:4 - d] * pltpu.roll(dc, TM - d, axis=0)
            dx_tail = dx_tail + cw[3 - d:4 - d] * pltpu.roll(tail_src, 16 - d, axis=0)[0:8]
            rolled, hd = _shift_down(x, prev8, d)
            b_sc[...] = rolled
            b_sc[0:8, :] = hd
            stats_ref[7 - d:8 - d, :] += jnp.sum(dc * b_sc[...], axis=0, keepdims=True)
        dcnext_sc[...] = dc[0:8]
        dproj_ref[...] = dx.astype(BF16)
        dproj_ref[TM - 8:TM, :] = dx_tail.astype(BF16)

    rev = lambda s: nt - 1 - s
    vec = pl.BlockSpec((1, D), lambda s: (0, 0))
    wspec = pl.BlockSpec((LRU_BLOCKS, LRU_BLOCK, LRU_BLOCK), lambda s: (0, 0, 0))
    prev16 = lambda col: pl.BlockSpec((16, D), lambda s: (jnp.maximum(rev(s) * (TM // 16) - 1, 0), col))
    tile = lambda: pl.BlockSpec((TM, D), lambda s: (rev(s), 0))
    h_lru, c_sv, r_sv, i_sv, la_sv, mult_sv = saved
    return _hosted_call(
        body,
        name="lru_bwd",
        grid=(nt,),
        in_specs=[
            pl.BlockSpec(memory_space=pl.ANY),
            pl.BlockSpec((TM, D), lambda s: (rev(s), 4)), prev16(4),
            tile(), prev16(0),
            tile(), tile(), tile(), tile(), tile(), tile(),
            pl.BlockSpec((4, D), lambda s: (0, 0)),
            wspec, wspec, vec,
        ],
        out_specs=[
            pl.BlockSpec((TM, D), lambda s: (rev(s), 4)),
            wspec, wspec,
            pl.BlockSpec((8, D), lambda s: (0, 0)),
        ],
        out_shape=[
            jax.ShapeDtypeStruct(dproj.shape, BF16),
            jax.ShapeDtypeStruct((LRU_BLOCKS, LRU_BLOCK, LRU_BLOCK), F32),
            jax.ShapeDtypeStruct((LRU_BLOCKS, LRU_BLOCK, LRU_BLOCK), F32),
            jax.ShapeDtypeStruct((8, D), F32),
        ],
        aliases={0: 0},
        scratch_shapes=[
            pltpu.VMEM((8, D), F32), pltpu.VMEM((8, D), F32), pltpu.VMEM((8, D), F32),
            pltpu.VMEM((TM, D), F32), pltpu.VMEM((TM, D), F32), pltpu.VMEM((TM, D), F32),
        ],
        args=(dproj, proj, proj, h_lru, h_lru, c_sv, r_sv, i_sv, la_sv, mult_sv, dhl, conv_w, wa, wx, lam),
        riders=riders,
    )


def _in_proj_bwd(dproj, w_in, part, prev=None, riders=()):
    rows = dproj.shape[0]
    tm = _heavy_tile(rows)
    nt = rows // tm
    first = 0 if part == 0 else (nt + 1) // 2
    count = (nt + 1) // 2 if part == 0 else nt - first

    def body(*refs):
        dproj_ref, w_hbm, du_ref, w_sc, w_sem = refs[-5:]

        @pl.when(pl.program_id(0) == 0)
        def _():
            cp = pltpu.make_async_copy(w_hbm, w_sc, w_sem)
            cp.start()
            cp.wait()

        du_ref[...] = _dot_nt(dproj_ref[...], w_sc[...])

    in_specs = [pl.BlockSpec((tm, IN_COLS), lambda i: (first + i, 0)), _ANY]
    args = (dproj, w_in)
    if prev is not None:
        in_specs = [_ANY] + in_specs
        args = (prev,) + args
    return _hosted_call(
        body,
        name="in_proj_bwd_%d" % part,
        grid=(count,),
        in_specs=in_specs,
        out_specs=[pl.BlockSpec((tm, D), lambda i: (first + i, 0))],
        out_shape=[jax.ShapeDtypeStruct((rows, D), F32)],
        scratch_shapes=[pltpu.VMEM((D, IN_COLS), BF16), pltpu.SemaphoreType.DMA],
        aliases={0: 0} if prev is not None else None,
        args=args,
        riders=riders,
    )


def _norm1_bwd(du, dh1, head, x2d, rstd1, norm_w, riders=()):
    rows = du.shape[0]

    def body(du_ref, dh1_ref, head_ref, x_ref, rstd_ref, nw_ref, gx_ref, ghead_ref, stats_ref):
        i = pl.program_id(0)

        @pl.when(i == 0)
        def _():
            stats_ref[...] = jnp.zeros_like(stats_ref)

        def finish(h0, out_ref):
            du = du_ref[...]
            rs = rstd_ref[...]
            n = h0 * rs
            stats_ref[0:1, :] += jnp.sum(du * n, axis=0, keepdims=True)
            dn = du * nw_ref[...]
            out_ref[...] = dh1_ref[...] + rs * (dn - n * jnp.mean(dn * n, axis=-1, keepdims=True))

        @pl.when(i == 0)
        def _():
            finish(head_ref[...], ghead_ref)

        @pl.when(i > 0)
        def _():
            finish(x_ref[...], gx_ref)

    tile = pl.BlockSpec((TM, D), lambda i: (i, 0))
    return _hosted_call(
        body,
        name="norm1_bwd",
        grid=(rows // TM,),
        in_specs=[
            tile, tile,
            pl.BlockSpec((FRONT, D), lambda i: (0, 0)),
            pl.BlockSpec((TM, D), lambda i: (jnp.maximum(i - 1, 0), 0)),
            pl.BlockSpec((TM, 1), lambda i: (i, 0)),
            pl.BlockSpec((1, D), lambda i: (0, 0)),
        ],
        out_specs=[
            pl.BlockSpec((TM, D), lambda i: (jnp.maximum(i - 1, 0), 0)),
            pl.BlockSpec((FRONT, D), lambda i: (0, 0)),
            pl.BlockSpec((8, D), lambda i: (0, 0)),
        ],
        out_shape=[
            jax.ShapeDtypeStruct(x2d.shape, F32),
            jax.ShapeDtypeStruct((FRONT, D), F32),
            jax.ShapeDtypeStruct((8, D), F32),
        ],
        scratch_shapes=[],
        args=(du, dh1, head, x2d, rstd1, norm_w),
        riders=riders,
    )


def _matmul_tn(name, x, dy, out_cols, col0=0, prev=None, k_block=None, n_block=None, riders=()):
    rows, kdim = x.shape
    ndim = dy.shape[1]
    kb = k_block or kdim
    nb = n_block or ndim
    step_bytes = lambda t: 2 * t * (kb * x.dtype.itemsize + nb * dy.dtype.itemsize) + 2 * kb * nb * 4
    tr = next(t for t in (2816, 1408, TM_HEAVY, TM) if rows % t == 0 and (t == TM or step_bytes(t) <= TN_VMEM_BUDGET))
    nr, nk, nn = rows // tr, kdim // kb, ndim // nb
    cb0 = col0 // nb

    def body(*refs):
        x_ref, dy_ref, out_ref = refs[-3], refs[-2], refs[-1]
        part = _dot_tn(x_ref[...].astype(BF16), dy_ref[...].astype(BF16))

        @pl.when(pl.program_id(2) == 0)
        def _():
            out_ref[...] = part

        @pl.when(pl.program_id(2) > 0)
        def _():
            out_ref[...] += part

    in_specs = [
        pl.BlockSpec((tr, kb), lambda n, k, r: (r, k)),
        pl.BlockSpec((tr, nb), lambda n, k, r: (r, n)),
    ]
    args = [x, dy]
    aliases = {}
    if prev is not None:
        in_specs = [pl.BlockSpec(memory_space=pl.ANY)] + in_specs
        args = [prev] + args
        aliases = {0: 0}
    (out,), rider_outs = _hosted_call(
        body,
        name=name,
        grid=(nn, nk, nr),
        in_specs=in_specs,
        out_specs=[pl.BlockSpec((kb, nb), lambda n, k, r: (k, cb0 + n))],
        out_shape=[jax.ShapeDtypeStruct((kdim, out_cols), F32)],
        scratch_shapes=[],
        aliases=aliases,
        args=args,
        riders=riders,
    )
    return (out, rider_outs) if riders else out


def _local_step(x2d, target, w, plan):
    rows = FRONT + x2d.shape[0]
    head = jnp.concatenate([jnp.zeros((PAD_ROWS, D), F32), w["meta_tokens"]], axis=0)
    cos_t, sin_t = _rope_tables(rows)
    intra, intra_t, q_dec, k_dec, c_dec = _decay_tables()
    grads = {}

    def hosted(host, fn, *args, **kwargs):
        outs, rider_outs = fn(*args, riders=plan.riders(host, w, grads), **kwargs)
        plan.after(host, rider_outs, w, grads)
        return outs

    (u1, rstd1), _ = _norm1(head, x2d, w["mix_norm_w"])
    proj, w["w_in"] = hosted("in_proj", _in_proj, u1, w["w_in_shard"], w["route"], cos_t, sin_t)
    o, sprev = hosted("retention_fwd", _retention_fwd, proj, intra, q_dec, k_dec, c_dec)
    lru_args = (w["conv_w"], w["conv_b"], w["lru_wa"], w["lru_wx"], w["lru_ba"], w["lru_bx"], w["lru_lambda"])
    lru_saved = hosted("lru_fwd", _lru_fwd, proj, *lru_args)
    h_lru = lru_saved[0]
    y_ret, y_lru, h1, u2, rstd2 = hosted("mix_fwd", _mix_fwd, head, x2d, o, proj, h_lru, w["w_branch_ret"],
                                         w["w_branch_lru"], w["w_out"], w["ffn_norm_w"])
    silu, dsilu, act = hosted("ffn_fwd_gate", _ffn_fwd_gate, u2, w["w_ffn_in"])
    h2 = _ffn_fwd_out(act, h1, w["w_ffn_out"])
    dh2, stats_loss = _final_loss(h2, target, w["final_norm_w"])

    dg, dup, dh1, stats_ffn = _ffn_bwd(dh2, silu, dsilu, h1, rstd2, w["w_ffn_in"], w["w_ffn_out"], w["ffn_norm_w"])
    grads["w_ffn_in"] = _matmul_tn("dw_ffn_up", u2, dup, 2 * FFN, col0=FFN, n_block=FFN_HALF,
                                   prev=_matmul_tn("dw_ffn_gate", u2, dg, 2 * FFN, n_block=FFN_HALF))
    grads["w_ffn_out"] = _matmul_tn("dw_ffn_out", act, dh2, D, k_block=FFN_HALF)
    (dproj, do, dhl, mixed, a_ret, a_lru, dy_ret, dy_lru) = hosted(
        "mix_bwd", _mix_bwd, dh1, o, proj, h_lru, y_ret, y_lru, w["w_branch_ret"], w["w_branch_lru"], w["w_out"])
    grads["w_out"] = _matmul_tn("dw_out", mixed, dh1, D)
    grads["w_branch_ret"] = _matmul_tn("dw_branch_ret", a_ret, dy_ret, D)
    grads["w_branch_lru"] = _matmul_tn("dw_branch_lru", a_lru, dy_lru, D)
    (dproj,) = hosted("retention_bwd", _retention_bwd, dproj, proj, do, sprev, intra, intra_t, q_dec, k_dec, c_dec,
                      cos_t, sin_t)
    dproj, grads["lru_wa"], grads["lru_wx"], stats_lru = hosted(
        "lru_bwd", _lru_bwd, dproj, proj, lru_saved, dhl, w["conv_w"], w["lru_wa"], w["lru_wx"], w["lru_lambda"])
    grads["w_in"], rider_outs = _matmul_tn("dw_in", u1, dproj, IN_COLS, n_block=D, riders=plan.riders("dw_in", w, grads))
    plan.after("dw_in", rider_outs, w, grads)
    (du1,) = hosted("in_proj_bwd_0", _in_proj_bwd, dproj, w["w_in"], 0)
    (du1,) = hosted("in_proj_bwd_1", _in_proj_bwd, dproj, w["w_in"], 1, du1)
    (grad_x, grad_head, stats_in), _ = _norm1_bwd(du1, dh1, head, x2d, rstd1, w["mix_norm_w"])
    return grad_x, grad_head, grads, [stats_loss, stats_ffn, stats_in, stats_lru]


BIG_PIECES = {
    "w_in": ("col", (D, 2 * D)),
    "w_ffn_in": ("col", (D, FFN_HALF)),
    "w_ffn_out": ("row", (FFN // 4, D)),
    "w_branch_ret": ("row", (D // 4, D)),
    "w_branch_lru": ("row", (D // 4, D)),
    "w_out": ("row", (D // 4, D)),
    "lru_wa": ("lru", (LRU_BLOCKS, LRU_BLOCK // 4, LRU_BLOCK)),
    "lru_wx": ("lru", (LRU_BLOCKS, LRU_BLOCK // 4, LRU_BLOCK)),
}
SMALL_PIECES = {"meta_tokens": ("col", (N_META, D // 4)), "conv_w": ("col", (4, D // 4))}


def _full_shape(kind, shard):
    if kind == "col":
        return (shard[0], 4 * shard[1])
    if kind == "row":
        return (4 * shard[0], shard[1])
    return (shard[0], 4 * shard[1], shard[2])


def _half_shape(kind, shard):
    return (shard[0] // 2,) + tuple(shard[1:])


def _aligned(start, multiple):
    return start if isinstance(start, int) else pl.multiple_of(start, multiple)


def _lead(h, size):
    if h is None:
        return pl.ds(0, size)
    return pl.ds(_aligned(h * (size // 2), size // 2), size // 2)


def _full_region(ref, kind, shard, s, h):
    if kind == "col":
        return ref.at[_lead(h, shard[0]), pl.ds(_aligned(s * shard[1], shard[1]), shard[1])]
    if kind == "row":
        size = shard[0] if h is None else shard[0] // 2
        start = s * shard[0] + (0 if h is None else h * (shard[0] // 2))
        return ref.at[pl.ds(_aligned(start, 16), size), :]
    return ref.at[_lead(h, shard[0]), pl.ds(_aligned(s * shard[1], shard[1]), shard[1]), :]


def _shard_region(ref, shard, h):
    return ref.at[_lead(h, shard[0])]


def _place():
    x, y, c = lax.axis_index("x"), lax.axis_index("y"), lax.axis_index("c")
    return x, y, c, 2 * x + y


def _other_chip(s, c, k):
    s2 = jnp.bitwise_xor(s, k)
    return s2, (s2 // 2, s2 % 2, c)


def _remote(src, dst, send_sem, recv_sem, dev):
    return pltpu.make_async_remote_copy(src_ref=src, dst_ref=dst, send_sem=send_sem, recv_sem=recv_sem,
                                        device_id=dev, device_id_type=MESH)


_ANY = pl.BlockSpec(memory_space=pl.ANY)


class _Rider:
    def __init__(self, ins, out_shapes, sem_shapes, build, aliased=False):
        self.ins, self.out_shapes, self.sem_shapes, self.build, self.aliased = ins, out_shapes, sem_shapes, build, aliased


def _hosted_call(body, *, name, grid, in_specs, out_specs, out_shape, scratch_shapes, args, riders=(), aliases=None,
                 prefetch=None, riders_after_body=False):
    n_in, n_out, n_sc = len(in_specs), len(out_shape), len(scratch_shapes)
    r_in = [a for r in riders for a in r.ins]
    r_out = [s for r in riders for s in r.out_shapes]
    r_sem = [s for r in riders for s in r.sem_shapes]
    lead = () if prefetch is None else (prefetch,)
    assert prefetch is None or not (aliases or any(r.aliased for r in riders))

    def full_body(*refs):
        head, refs = refs[:len(lead)], refs[len(lead):]
        ins, rin = refs[:n_in], refs[n_in:n_in + len(r_in)]
        o0 = n_in + len(r_in)
        outs, rout = refs[o0:o0 + n_out], refs[o0 + n_out:o0 + n_out + len(r_out)]
        s0 = o0 + n_out + len(r_out)
        scratch, rsem = refs[s0:s0 + n_sc], refs[s0 + n_sc:]
        starts, waits = [], []
        pi = po = ps = 0
        for r in riders:
            st, wt = r.build(rin[pi:pi + len(r.ins)], rout[po:po + len(r.out_shapes)], rsem[ps:ps + len(r.sem_shapes)])
            starts += st
            waits += wt
            pi, po, ps = pi + len(r.ins), po + len(r.out_shapes), ps + len(r.sem_shapes)
        first = functools.reduce(jnp.logical_and, [pl.program_id(d) == 0 for d in range(len(grid))])
        last = functools.reduce(jnp.logical_and, [pl.program_id(d) == grid[d] - 1 for d in range(len(grid))])

        def start_riders():
            @pl.when(first)
            def _():
                for cp in starts:
                    cp.start()

        if riders and not riders_after_body:
            start_riders()
        body(*head, *ins, *outs, *scratch)
        if riders and riders_after_body:
            start_riders()
        if riders:
            @pl.when(last)
            def _():
                for wait in waits:
                    wait()

    io_aliases = dict(aliases or {})
    pi = po = 0
    for r in riders:
        if r.aliased:
            for q in range(len(r.ins)):
                io_aliases[n_in + pi + q] = n_out + po + q
        pi, po = pi + len(r.ins), po + len(r.out_shapes)
    specs = dict(
        grid=grid,
        in_specs=list(in_specs) + [_ANY] * len(r_in),
        out_specs=list(out_specs) + [_ANY] * len(r_out),
        scratch_shapes=list(scratch_shapes) + r_sem,
    )
    if prefetch is not None:
        specs = dict(grid_spec=pltpu.PrefetchScalarGridSpec(num_scalar_prefetch=1, **specs))
    res = pl.pallas_call(
        full_body,
        name=name,
        out_shape=list(out_shape) + r_out,
        input_output_aliases=io_aliases,
        compiler_params=_params(("arbitrary",) * len(grid)),
        **specs,
    )(*lead, *args, *r_in)
    rider_outs, po = [], n_out
    for r in riders:
        rider_outs.append(list(res[po:po + len(r.out_shapes)]))
        po += len(r.out_shapes)
    return list(res[:n_out]), rider_outs


def _run_riders(name, riders):
    r_in = [a for r in riders for a in r.ins]
    r_out = [s for r in riders for s in r.out_shapes]
    r_sem = [s for r in riders for s in r.sem_shapes]

    def body(*refs):
        rin, rout, rsem = refs[:len(r_in)], refs[len(r_in):len(r_in) + len(r_out)], refs[len(r_in) + len(r_out):]
        pi = po = ps = 0
        for r in riders:
            starts, waits = r.build(rin[pi:pi + len(r.ins)], rout[po:po + len(r.out_shapes)], rsem[ps:ps + len(r.sem_shapes)])
            for cp in starts:
                cp.start()
            for wait in waits:
                wait()
            pi, po, ps = pi + len(r.ins), po + len(r.out_shapes), ps + len(r.sem_shapes)

    io_aliases = {}
    pi = po = 0
    for r in riders:
        if r.aliased:
            for q in range(len(r.ins)):
                io_aliases[pi + q] = po + q
        pi, po = pi + len(r.ins), po + len(r.out_shapes)
    res = pl.pallas_call(
        body,
        name=name,
        in_specs=[_ANY] * len(r_in),
        out_specs=[_ANY] * len(r_out),
        out_shape=r_out,
        scratch_shapes=r_sem,
        input_output_aliases=io_aliases,
    )(*r_in)
    outs, po = [], 0
    for r in riders:
        outs.append(list(res[po:po + len(r.out_shapes)]))
        po += len(r.out_shapes)
    return outs


def _piece(name):
    if name in BIG_PIECES:
        return (name, *BIG_PIECES[name], True)
    return (name, *SMALL_PIECES[name], False)


def _gather_rider(shards, names):
    pieces = [_piece(n) for n in names]
    n = len(pieces)

    def build(ins, outs, sems):
        local_sem, ici_send, ici_recv = sems
        _, _, c, s = _place()
        starts, waits = [], []
        for p, (_, kind, shard, split) in enumerate(pieces):
            cp = pltpu.make_async_copy(ins[p], _full_region(outs[p], kind, shard, s, None), local_sem.at[p])
            starts.append(cp)
            waits.append(cp.wait)
            h = c if split else None
            for k in (1, 2, 3):
                s2, dev = _other_chip(s, c, k)
                cp = _remote(_shard_region(ins[p], shard, h), _full_region(outs[p], kind, shard, s, h),
                             ici_send.at[p, k - 1], ici_recv.at[p, k - 1], dev)
                starts.append(cp)
                waits.append(cp.wait_send)
                region = _full_region(outs[p], kind, shard, s2, h)
                waits.append(_remote(region, region, ici_send.at[p, k - 1], ici_recv.at[p, k - 1], dev).wait_recv)
        return starts, waits

    return _Rider(
        [shards[name] for name in names],
        [jax.ShapeDtypeStruct(_full_shape(kind, shard), shards[name].dtype) for name, kind, shard, _ in pieces],
        [pltpu.SemaphoreType.DMA((n,)), pltpu.SemaphoreType.DMA((n, 3)), pltpu.SemaphoreType.DMA((n, 3))],
        build)


def _forward_rider(gathered, names):
    pieces = [_piece(n) for n in names]
    n = len(pieces)

    def build(ins, outs, sems):
        fwd_send, fwd_recv = sems
        x, y, c, s = _place()
        sibling = (x, y, 1 - c)
        starts, waits = [], []
        for p, (_, kind, shard, _) in enumerate(pieces):
            for k in (1, 2, 3):
                s2, _ = _other_chip(s, c, k)
                mine = _full_region(outs[p], kind, shard, s2, c)
                theirs = _full_region(outs[p], kind, shard, s2, 1 - c)
                cp = _remote(mine, mine, fwd_send.at[p, k - 1], fwd_recv.at[p, k - 1], sibling)
                starts.append(cp)
                waits.append(cp.wait_send)
                waits.append(_remote(theirs, theirs, fwd_send.at[p, k - 1], fwd_recv.at[p, k - 1], sibling).wait_recv)
        return starts, waits

    return _Rider(
        [gathered[name] for name in names],
        [jax.ShapeDtypeStruct(gathered[name].shape, gathered[name].dtype) for name in names],
        [pltpu.SemaphoreType.DMA((n, 3)), pltpu.SemaphoreType.DMA((n, 3))],
        build, aliased=True)


def _pair_exchange_rider(grads, names):
    n = len(names)

    def build(ins, outs, sems):
        send_sem, recv_sem = sems
        x, y, c, _ = _place()
        sibling = (x, y, 1 - c)
        starts, waits = [], []
        for p, name in enumerate(names):
            kind, shard = BIG_PIECES[name]
            for s2 in range(4):
                cp = _remote(_full_region(ins[p], kind, shard, s2, 1 - c), outs[p].at[s2], send_sem.at[p, s2],
                             recv_sem.at[p, s2], sibling)
                starts.append(cp)
                waits.append(cp.wait_send)
                waits.append(_remote(outs[p].at[s2], outs[p].at[s2], send_sem.at[p, s2], recv_sem.at[p, s2], sibling).wait_recv)
        return starts, waits

    return _Rider(
        [grads[name] for name in names],
        [jax.ShapeDtypeStruct((4,) + _half_shape(*BIG_PIECES[name]), F32) for name in names],
        [pltpu.SemaphoreType.DMA((n, 4))] * 2,
        build)


def _half_specs(kind, shard):
    half = shard[0] // 2
    if kind == "col":
        full = pl.BlockSpec((half, shard[1]), lambda j, pr: (pr[1], j))
        buf = pl.BlockSpec((None, half, shard[1]), lambda j, pr: (j, 0, 0))
    elif kind == "row":
        full = pl.BlockSpec((half, shard[1]), lambda j, pr: (2 * j + pr[1], 0))
        buf = pl.BlockSpec((None, half, shard[1]), lambda j, pr: (j, 0, 0))
    else:
        full = pl.BlockSpec((half, shard[1], shard[2]), lambda j, pr: (pr[1], j, 0))
        buf = pl.BlockSpec((None, half, shard[1], shard[2]), lambda j, pr: (j, 0, 0, 0))
    return full, buf


def _pair_sum(name, grad, recv, place):
    kind, shard = BIG_PIECES[name]
    full, buf = _half_specs(kind, shard)

    def body(pr, g_ref, r_ref, o_ref):
        o_ref[...] = (g_ref[...] + r_ref[...]).astype(BF16)

    return pl.pallas_call(
        body,
        name="pair_sum_" + name,
        grid_spec=pltpu.PrefetchScalarGridSpec(num_scalar_prefetch=1, grid=(4,), in_specs=[full, buf], out_specs=buf),
        out_shape=jax.ShapeDtypeStruct((4,) + _half_shape(kind, shard), BF16),
        compiler_params=_params(("arbitrary",)),
    )(place, grad, recv)


def _chip_exchange_rider(sums, names):
    n = len(names)

    def build(ins, outs, sems):
        send_sem, recv_sem = sems
        _, _, c, s = _place()
        starts, waits = [], []
        for p in range(n):
            for k in (1, 2, 3):
                s2, dev = _other_chip(s, c, k)
                cp = _remote(ins[p].at[s2], outs[p].at[k - 1], send_sem.at[p, k - 1], recv_sem.at[p, k - 1], dev)
                starts.append(cp)
                waits.append(cp.wait_send)
                waits.append(_remote(outs[p].at[k - 1], outs[p].at[k - 1], send_sem.at[p, k - 1], recv_sem.at[p, k - 1],
                                     dev).wait_recv)
        return starts, waits

    return _Rider(
        [sums[name] for name in names],
        [jax.ShapeDtypeStruct((3,) + _half_shape(*BIG_PIECES[name]), BF16) for name in names],
        [pltpu.SemaphoreType.DMA((n, 3))] * 2,
        build)


def _chip_sum(name, grad, recv_pair, recv_chip, place):
    kind, shard = BIG_PIECES[name]
    half = shard[0] // 2
    tail = tuple(shard[1:])
    zeros = (0,) * len(tail)
    if kind == "col":
        full = pl.BlockSpec((half,) + tail, lambda j, pr: (pr[1], pr[0]))
    elif kind == "row":
        full = pl.BlockSpec((half,) + tail, lambda j, pr: (2 * pr[0] + pr[1], 0))
    else:
        full = pl.BlockSpec((half,) + tail, lambda j, pr: (pr[1], pr[0], 0))
    pair = pl.BlockSpec((None, half) + tail, lambda j, pr: (pr[0], 0) + zeros)
    chip = pl.BlockSpec((3, half) + tail, lambda j, pr: (0, 0) + zeros)
    out = pl.BlockSpec((half,) + tail, lambda j, pr: (pr[1],) + zeros)

    def body(pr, g_ref, rp_ref, rc_ref, o_ref):
        total = g_ref[...] + rp_ref[...]
        for k in range(3):
            total = total + rc_ref[k].astype(F32)
        o_ref[...] = total

    return pl.pallas_call(
        body,
        name="chip_sum_" + name,
        grid_spec=pltpu.PrefetchScalarGridSpec(num_scalar_prefetch=1, grid=(1,), in_specs=[full, pair, chip], out_specs=out),
        out_shape=jax.ShapeDtypeStruct(shard, F32),
        compiler_params=_params(("arbitrary",)),
    )(place, grad, recv_pair, recv_chip)


def _sibling_exchange_rider(halves, names):
    n = len(names)

    def build(ins, outs, sems):
        send_sem, recv_sem = sems
        x, y, c, _ = _place()
        sibling = (x, y, 1 - c)
        starts, waits = [], []
        for p, name in enumerate(names):
            shard = BIG_PIECES[name][1]
            mine = _shard_region(outs[p], shard, c)
            theirs = _shard_region(outs[p], shard, 1 - c)
            cp = _remote(mine, mine, send_sem.at[p], recv_sem.at[p], sibling)
            starts.append(cp)
            waits.append(cp.wait_send)
            waits.append(_remote(theirs, theirs, send_sem.at[p], recv_sem.at[p], sibling).wait_recv)
        return starts, waits

    return _Rider(
        [halves[name] for name in names],
        [jax.ShapeDtypeStruct(BIG_PIECES[name][1], F32) for name in names],
        [pltpu.SemaphoreType.DMA((n,))] * 2,
        build, aliased=True)


FIRST_WEIGHTS = ["meta_tokens", "conv_w"]
WEIGHT_GROUPS = {
    "mixer": ["lru_wa", "lru_wx", "w_branch_ret", "w_branch_lru", "w_out"],
    "ffn_in": ["w_ffn_in"],
    "ffn_out": ["w_ffn_out"],
}
WEIGHT_SCHEDULE = {
    "in_proj": [("gather", "mixer")],
    "retention_fwd": [("forward", "mixer")],
    "lru_fwd": [("gather", "ffn_in")],
    "mix_fwd": [("forward", "ffn_in"), ("gather", "ffn_out")],
    "ffn_fwd_gate": [("forward", "ffn_out")],
}
GRAD_GROUPS = {
    "ffn": ["w_ffn_in", "w_ffn_out"],
    "mixer": ["w_out", "w_branch_ret", "w_branch_lru", "lru_wa", "lru_wx"],
    "in": ["w_in"],
}
GRAD_SCHEDULE = {
    "mix_bwd": [("pair", "ffn")],
    "retention_bwd": [("chip", "ffn")],
    "lru_bwd": [("sibling", "ffn")],
    "dw_in": [("pair", "mixer")],
    "in_proj_bwd_0": [("chip", "mixer"), ("pair", "in")],
    "in_proj_bwd_1": [("sibling", "mixer"), ("chip", "in")],
}


class _CommPlan:
    def __init__(self, shards, place):
        self.shards, self.place = shards, place
        self.late = {}
        self.recv_pair, self.sums, self.recv_chip, self.halves, self.final = {}, {}, {}, {}, {}

    def _grad_rider(self, stage, group, grads):
        names = GRAD_GROUPS[group]
        if stage == "pair":
            return _pair_exchange_rider(grads, names)
        if stage == "chip":
            return _chip_exchange_rider(self.sums, names)
        return _sibling_exchange_rider(self.halves, names)

    def _grad_after(self, stage, group, outs, grads):
        names = GRAD_GROUPS[group]
        if stage == "pair":
            for n, o in zip(names, outs):
                self.recv_pair[n] = o
                self.sums[n] = _pair_sum(n, grads[n], o, self.place)
        elif stage == "chip":
            for n, o in zip(names, outs):
                self.halves[n] = _chip_sum(n, grads[n], self.recv_pair[n], o, self.place)
        else:
            self.final.update(zip(names, outs))

    def riders(self, host, w, grads):
        if host in WEIGHT_SCHEDULE:
            return [_gather_rider(self.shards, WEIGHT_GROUPS[group]) if stage == "gather"
                    else _forward_rider(self.late, WEIGHT_GROUPS[group]) for stage, group in WEIGHT_SCHEDULE[host]]
        return [self._grad_rider(stage, group, grads) for stage, group in GRAD_SCHEDULE.get(host, [])]

    def after(self, host, rider_outs, w, grads):
        for (stage, group), outs in zip(WEIGHT_SCHEDULE.get(host, []), rider_outs):
            (self.late if stage == "gather" else w).update(zip(WEIGHT_GROUPS[group], outs))
        for (stage, group), outs in zip(GRAD_SCHEDULE.get(host, []), rider_outs):
            self._grad_after(stage, group, outs, grads)

    def finish(self):
        (outs,) = _run_riders("grad_tail_exchange", [_sibling_exchange_rider(self.halves, GRAD_GROUPS["in"])])
        self.final.update(zip(GRAD_GROUPS["in"], outs))
        return self.final


def _adamw_math(w, g, m, v):
    m = ADAM_B1 * m + (1.0 - ADAM_B1) * g
    v = ADAM_B2 * v + (1.0 - ADAM_B2) * (g * g)
    m_hat = m / (1.0 - ADAM_B1 ** ADAM_STEP)
    v_hat = v / (1.0 - ADAM_B2 ** ADAM_STEP)
    delta = -ADAM_LR * (m_hat / (jnp.sqrt(v_hat) + ADAM_EPS) + ADAM_WD * w)
    return delta, m, v


def _adamw(name, g, w, m, v):
    rows, cols = g.shape
    tr = rows // 4 if rows % 32 == 0 else rows

    def body(g_ref, w_ref, m_ref, v_ref, go_ref, d_ref, mo_ref, vo_ref):
        gv = g_ref[...]
        delta, m2, v2 = _adamw_math(w_ref[...], gv, m_ref[...], v_ref[...])
        go_ref[...] = gv
        d_ref[...] = delta
        mo_ref[...] = m2
        vo_ref[...] = v2

    spec = pl.BlockSpec((tr, cols), lambda i: (i, 0))
    return pl.pallas_call(
        body,
        name="adamw_" + name,
        grid=(rows // tr,),
        in_specs=[spec] * 4,
        out_specs=[spec] * 4,
        out_shape=[jax.ShapeDtypeStruct((rows, cols), F32)] * 4,
        compiler_params=_params(("arbitrary",)),
    )(g, w, m, v)


SMALL_ROWS = 48
VEC_ROWS = {"final_norm_w": 1, "ffn_norm_w": 8, "mix_norm_w": 16, "conv_b": 24, "lru_ba": 25, "lru_bx": 26, "lru_lambda": 27}
VEC_NAMES = list(VEC_ROWS)
CONV_W_ROW = 28
META_ROW = 32


def _small_all_reduce(partial):
    def body(in_ref, out_ref, buf, local_sem, send_sem, recv_sem):
        x, y, c, s = _place()
        me = 2 * s + c
        cp = pltpu.make_async_copy(in_ref, buf.at[me], local_sem)
        cp.start()
        started = []
        for k in range(1, 8):
            peer = jnp.bitwise_xor(me, k)
            dev = (peer // 4, (peer // 2) % 2, peer % 2)
            rd = _remote(in_ref, buf.at[me], send_sem.at[k - 1], recv_sem.at[k - 1], dev)
            rd.start()
            started.append(rd)
        for k in range(1, 8):
            peer = jnp.bitwise_xor(me, k)
            dev = (peer // 4, (peer // 2) % 2, peer % 2)
            _remote(in_ref, buf.at[peer], send_sem.at[k - 1], recv_sem.at[k - 1], dev).wait_recv()
        for rd in started:
            rd.wait_send()
        cp.wait()
        total = buf[0]
        for d in range(1, 8):
            total = total + buf[d]
        out_ref[...] = total

    return pl.pallas_call(
        body,
        name="small_all_reduce",
        in_specs=[pl.BlockSpec(memory_space=pltpu.VMEM)],
        out_specs=pl.BlockSpec(memory_space=pltpu.VMEM),
        out_shape=jax.ShapeDtypeStruct((SMALL_ROWS, D), F32),
        scratch_shapes=[pltpu.VMEM((8, SMALL_ROWS, D), F32), pltpu.SemaphoreType.DMA,
                        pltpu.SemaphoreType.DMA((7,)), pltpu.SemaphoreType.DMA((7,))],
    )(partial)


def _small_update(total, place, vecs, conv, meta):
    nvec = len(VEC_NAMES)
    qcols = D // 4

    def body(pr, tot_ref, col_ref, *refs):
        vec_refs = refs[:3 * nvec]
        conv_refs = refs[3 * nvec:3 * nvec + 3]
        meta_refs = refs[3 * nvec + 3:3 * nvec + 6]
        outs = refs[3 * nvec + 6:]
        loss_ref, vec_out, conv_out, meta_out = outs[0], outs[1:5], outs[5:9], outs[9:13]
        loss_ref[...] = jnp.sum(tot_ref[0:1, :], axis=1, keepdims=True)
        for o in vec_out:
            o[...] = jnp.zeros_like(o)
        for j, name in enumerate(VEC_NAMES):
            w, m, v = (r[...] for r in vec_refs[3 * j:3 * j + 3])
            g = tot_ref[VEC_ROWS[name]:VEC_ROWS[name] + 1, :]
            if name == "lru_lambda":
                g = -g / (1.0 + jnp.exp(w))
            for o, val in zip(vec_out, (g,) + _adamw_math(w, g, m, v)):
                o[j:j + 1, :] = val
        for row, n_rows, ins, group in ((CONV_W_ROW, 4, conv_refs, conv_out), (META_ROW, N_META, meta_refs, meta_out)):
            g = col_ref[row:row + n_rows, :]
            w, m, v = (r[...] for r in ins)
            for o, val in zip(group, (g,) + _adamw_math(w, g, m, v)):
                o[...] = val

    whole = lambda shape: pl.BlockSpec(shape, lambda i, pr: (0,) * len(shape))
    in_specs = [whole((SMALL_ROWS, D)), pl.BlockSpec((SMALL_ROWS, qcols), lambda i, pr: (0, pr[0]))]
    in_specs += [whole((1, D))] * (3 * nvec) + [whole((4, qcols))] * 3 + [whole((N_META, qcols))] * 3
    out_shapes = [(1, 1)] + [(8, D)] * 4 + [(4, qcols)] * 4 + [(N_META, qcols)] * 4
    return pl.pallas_call(
        body,
        name="small_update",
        grid_spec=pltpu.PrefetchScalarGridSpec(num_scalar_prefetch=1, grid=(1,), in_specs=in_specs,
                                               out_specs=[whole(s) for s in out_shapes]),
        out_shape=[jax.ShapeDtypeStruct(s, F32) for s in out_shapes],
        compiler_params=_params(("arbitrary",)),
    )(place, total, total, *[a for t in vecs for a in t], *conv, *meta)


WEIGHT_ORDER = ["meta_tokens", "mix_norm_w", "w_in", "conv_w", "conv_b", "lru_wa", "lru_ba", "lru_wx", "lru_bx", "lru_lambda",
                "w_branch_ret", "w_branch_lru", "w_out", "ffn_norm_w", "w_ffn_in", "w_ffn_out", "final_norm_w"]


def kernel(x, meta_tokens, mix_norm_w, w_in, conv_w, conv_b, lru_wa, lru_ba, lru_wx, lru_bx, lru_lambda, w_branch_ret, w_branch_lru, w_out, ffn_norm_w, w_ffn_in, w_ffn_out, final_norm_w, loss_target, m_meta_tokens, m_mix_norm_w, m_w_in, m_conv_w, m_conv_b, m_lru_wa, m_lru_ba, m_lru_wx, m_lru_bx, m_lru_lambda, m_w_branch_ret, m_w_branch_lru, m_w_out, m_ffn_norm_w, m_w_ffn_in, m_w_ffn_out, m_final_norm_w, v_meta_tokens, v_mix_norm_w, v_w_in, v_conv_w, v_conv_b, v_lru_wa, v_lru_ba, v_lru_wx, v_lru_bx, v_lru_lambda, v_w_branch_ret, v_w_branch_lru, v_w_out, v_ffn_norm_w, v_w_ffn_in, v_w_ffn_out, v_final_norm_w):
    args = locals()
    wts = {n: args[n] for n in WEIGHT_ORDER}
    mom = {n: args["m_" + n] for n in WEIGHT_ORDER}
    var = {n: args["v_" + n] for n in WEIGHT_ORDER}
    place = jnp.stack([2 * lax.axis_index("x") + lax.axis_index("y"), lax.axis_index("c")]).astype(jnp.int32)

    shards = {n: wts[n][0].astype(BF16) for n in BIG_PIECES}
    shards["meta_tokens"] = wts["meta_tokens"]
    shards["conv_w"] = wts["conv_w"][0]
    plan = _CommPlan(shards, place)
    (first,) = _run_riders("gather_first", [_gather_rider(shards, FIRST_WEIGHTS)])
    w = dict(zip(FIRST_WEIGHTS, first))
    for n in VEC_NAMES:
        w[n] = wts[n].reshape(1, D)
    chips = jnp.bitwise_xor(place[0], jnp.arange(4, dtype=jnp.int32))
    w["route"] = jnp.concatenate([place, jnp.stack([2 * chips, 2 * chips + 1], axis=1).reshape(8)])
    w["w_in_shard"] = shards["w_in"]

    grad_x, grad_head, _, stats = _local_step(x[0], loss_target[0], w, plan)
    shard_grads = plan.finish()

    out = {}
    for n in BIG_PIECES:
        shape2d = (-1, wts[n].shape[-1])
        res = _adamw(n, *[a.reshape(shape2d) for a in (shard_grads[n], wts[n], mom[n], var[n])])
        out[n] = [r.reshape(wts[n].shape) for r in res]

    partial = jnp.concatenate(stats + [grad_head[PAD_ROWS:]], axis=0)
    total = _small_all_reduce(partial)
    vecs = [tuple(a[n].reshape(1, D) for a in (wts, mom, var)) for n in VEC_NAMES]
    conv = tuple(a["conv_w"][0] for a in (wts, mom, var))
    meta = tuple(a["meta_tokens"] for a in (wts, mom, var))
    res = _small_update(total, place, vecs, conv, meta)
    loss = res[0].reshape(())
    for j, n in enumerate(VEC_NAMES):
        out[n] = [r[j].reshape(wts[n].shape) for r in res[1:5]]
    out["conv_w"] = [r.reshape(wts["conv_w"].shape) for r in res[5:9]]
    out["meta_tokens"] = list(res[9:13])

    return (loss, grad_x.reshape(x.shape)) + tuple(out[n][kind] for kind in range(4) for n in WEIGHT_ORDER)
```

```python
import functools
import math

import jax
import jax.numpy as jnp
from jax import lax
from jax.experimental import pallas as pl
from jax.experimental.pallas import tpu as pltpu

F32 = jnp.float32
BF16 = jnp.bfloat16

D = 1024
HEADS = 8
DH = 128
CHUNK = 256
N_META = 16
FRONT = 256
PAD_ROWS = FRONT - N_META
LRU_BLOCKS = 4
LRU_BLOCK = 256
LRU_C = 8.0
FFN = 2816
FFN_HALF = FFN // 2
IN_COLS = 8 * D
ROPE_BASE = 10000.0
EPS = 1e-6
QK_SCALE = DH ** -0.5

ADAM_LR = 0.001
ADAM_B1 = 0.9
ADAM_B2 = 0.999
ADAM_EPS = 1e-08
ADAM_WD = 0.01
ADAM_STEP = 10

TM = 256
TM_HEAVY = 768
FFN_BLOCK = FFN // 2
TM_MIX_BWD = 256
VMEM_LIMIT = 60 * 1024 * 1024
TN_VMEM_BUDGET = 40 * 1024 * 1024

NT_DIMS = (((1,), (1,)), ((), ()))
TN_DIMS = (((0,), (0,)), ((), ()))
MESH = pl.DeviceIdType.MESH


def _params(sem=None):
    if sem is None:
        return pltpu.CompilerParams(vmem_limit_bytes=VMEM_LIMIT)
    return pltpu.CompilerParams(dimension_semantics=sem, vmem_limit_bytes=VMEM_LIMIT)


def _dot(a, b):
    return jnp.dot(a, b, preferred_element_type=F32)


def _dot_nt(a, b):
    return lax.dot_general(a, b, NT_DIMS, preferred_element_type=F32)


def _dot_tn(a, b):
    return lax.dot_general(a, b, TN_DIMS, preferred_element_type=F32)


def _sigmoid(z):
    return 1.0 / (1.0 + jnp.exp(-z))


def _log1p(x):
    return jnp.where(x < 1e-3, x * (1.0 - x * (0.5 - x * (1.0 / 3.0))), jnp.log(1.0 + x))


def _softplus(x):
    return jnp.maximum(x, 0.0) + _log1p(jnp.exp(-jnp.abs(x)))


def _neg_expm1(x):
    series = -x * (1.0 + x * (0.5 + x * (1.0 / 6.0 + x * (1.0 / 24.0 + x * (1.0 / 120.0)))))
    return jnp.where(x > -0.05, series, 1.0 - jnp.exp(x))


_GELU_K = math.sqrt(2.0 / math.pi)


def _gelu_and_grad(x):
    inner = _GELU_K * (x + 0.044715 * x * x * x)
    t = jnp.tanh(inner)
    val = 0.5 * x * (1.0 + t)
    grad = 0.5 * (1.0 + t) + 0.5 * x * (1.0 - t * t) * _GELU_K * (1.0 + 3.0 * 0.044715 * x * x)
    return val, grad


def _row_ids(i, rows, shape):
    return i * rows + lax.broadcasted_iota(jnp.int32, shape, 0)


def _rope_tables(rows):
    pos = jnp.arange(rows, dtype=jnp.int32) - PAD_ROWS
    inv_freq = ROPE_BASE ** (-jnp.arange(0, DH, 2, dtype=F32) / DH)
    ang = pos.astype(F32)[:, None] * inv_freq[None, :]
    cos, sin = jnp.cos(ang), jnp.sin(ang)
    return jnp.concatenate([cos, cos], axis=1), jnp.concatenate([-sin, sin], axis=1)


def _decay_tables():
    log_g = jnp.log(1.0 - 2.0 ** (-5.0 - jnp.arange(HEADS, dtype=F32)))
    idx = jnp.arange(CHUNK, dtype=F32)
    diff = idx[:, None] - idx[None, :]
    intra = jnp.where(diff[None] >= 0, jnp.exp(jnp.maximum(diff, 0.0)[None] * log_g[:, None, None]), 0.0)
    q_decay = jnp.exp((idx + 1.0)[:, None] * log_g[None, :])
    k_decay = jnp.exp((CHUNK - 1.0 - idx)[:, None] * log_g[None, :])
    chunk_decay = jnp.exp(CHUNK * log_g)
    wide = lambda a: jnp.repeat(a, DH, axis=-1)
    return intra, jnp.swapaxes(intra, 1, 2), wide(q_decay), wide(k_decay), wide(chunk_decay[None, :])


def _norm1(head, x2d, norm_w, riders=()):
    rows = FRONT + x2d.shape[0]

    def body(head_ref, x_ref, nw_ref, u_ref, rstd_ref):
        def norm(hv):
            rs = lax.rsqrt(jnp.mean(hv * hv, axis=-1, keepdims=True) + EPS)
            u_ref[...] = ((hv * rs) * nw_ref[...]).astype(BF16)
            rstd_ref[...] = rs

        @pl.when(pl.program_id(0) == 0)
        def _():
            norm(head_ref[...])

        @pl.when(pl.program_id(0) > 0)
        def _():
            norm(x_ref[...])

    return _hosted_call(
        body,
        name="norm1",
        grid=(rows // TM,),
        in_specs=[
            pl.BlockSpec((FRONT, D), lambda i: (0, 0)),
            pl.BlockSpec((TM, D), lambda i: (jnp.maximum(i - 1, 0), 0)),
            pl.BlockSpec((1, D), lambda i: (0, 0)),
        ],
        out_specs=[pl.BlockSpec((TM, D), lambda i: (i, 0)), pl.BlockSpec((TM, 1), lambda i: (i, 0))],
        out_shape=[jax.ShapeDtypeStruct((rows, D), BF16), jax.ShapeDtypeStruct((rows, 1), F32)],
        scratch_shapes=[],
        args=(head, x2d, norm_w),
        riders=riders,
    )


def _heavy_tile(rows):
    return TM_HEAVY if rows % TM_HEAVY == 0 else TM


def _in_proj(u, w_shard, route, cos_t, sin_t, riders=()):
    rows = u.shape[0]
    tm = _heavy_tile(rows)
    nt = rows // tm
    kind, shard = BIG_PIECES["w_in"]

    def body(route_ref, u_hbm, wsh_ref, cos_ref, sin_ref, proj_ref, wfull_ref,
             u_sc, w_sc, u_sem, w_sem, local_sem, ici_send, ici_recv, fwd_send, fwd_recv):
        g, i = pl.program_id(0), pl.program_id(1)
        s, c = route_ref[0], route_ref[1]
        gid = route_ref[2 + g]
        sibling = (s // 2, s % 2, 1 - c)
        local = pltpu.make_async_copy(wsh_ref, _full_region(wfull_ref, kind, shard, s, None), local_sem)
        u_copies = [pltpu.make_async_copy(u_hbm.at[pl.ds(t * tm, tm)], u_sc.at[t], u_sem.at[t]) for t in range(nt)]
        sends, arrivals = [], []
        for k in (1, 2, 3):
            s2, dev = _other_chip(s, c, k)
            sends.append(_remote(_shard_region(wsh_ref, shard, c), _full_region(wfull_ref, kind, shard, s, c),
                                 ici_send.at[k - 1], ici_recv.at[k - 1], dev))
            mine = _full_region(wfull_ref, kind, shard, s2, c)
            theirs = _full_region(wfull_ref, kind, shard, s2, 1 - c)
            arrivals.append((_remote(mine, mine, ici_send.at[k - 1], ici_recv.at[k - 1], dev),
                             _remote(mine, mine, fwd_send.at[k - 1], fwd_recv.at[k - 1], sibling),
                             _remote(theirs, theirs, fwd_send.at[k - 1], fwd_recv.at[k - 1], sibling)))

        slot = g % 2
        last_tile = i == nt - 1

        def block_copy(src, col, to_slot):
            return pltpu.make_async_copy(src.at[:, pl.ds(pl.multiple_of(col * D, D), D)], w_sc.at[to_slot],
                                         w_sem.at[to_slot])

        @pl.when(jnp.logical_and(g == 0, i == 0))
        def _():
            for cp in sends:
                cp.start()
            local.start()
            for cp in u_copies:
                cp.start()
            cp = block_copy(wsh_ref, 0, 0)
            cp.start()
            cp.wait()

        @pl.when(jnp.logical_and(last_tile, g == 0))
        def _():
            block_copy(wsh_ref, 1, 1).start()

        for k, (arrived, forward, forwarded) in zip((1, 2, 3), arrivals):
            @pl.when(jnp.logical_and(last_tile, g == 2 * k - 1))
            def _():
                arrived.wait_recv()
                forward.start()
                forwarded.wait_recv()
                block_copy(wfull_ref, route_ref[2 + 2 * k], 0).start()

            @pl.when(jnp.logical_and(last_tile, g == 2 * k))
            def _():
                block_copy(wfull_ref, route_ref[3 + 2 * k], 1).start()

        @pl.when(jnp.logical_and(i == 0, g > 0))
        def _():
            block_copy(wfull_ref, 0, slot).wait()

        for t in range(nt):
            @pl.when(jnp.logical_and(g == 0, i == t))
            def _():
                u_copies[t].wait()

        acc = _dot(u_sc[i], w_sc[slot])

        @pl.when(gid < 2)
        def _():
            scale = jnp.where(gid == 1, QK_SCALE, 1.0).astype(F32)
            for h in range(HEADS):
                sl = slice(h * DH, (h + 1) * DH)
                blk = acc[:, sl]
                out = (blk * cos_ref[...] + pltpu.roll(blk, DH // 2, axis=1) * sin_ref[...]) * scale
                proj_ref[:, sl] = out.astype(BF16)

        @pl.when(gid >= 2)
        def _():
            proj_ref[...] = acc.astype(BF16)

        @pl.when(jnp.logical_and(g == 7, i == nt - 1))
        def _():
            local.wait()
            for cp in sends:
                cp.wait_send()
            for _, forward, _ in arrivals:
                forward.wait_send()

    return _hosted_call(
        body,
        name="in_proj",
        grid=(8, nt),
        in_specs=[
            _ANY, _ANY,
            pl.BlockSpec((tm, DH), lambda g, i, rt: (i, 0)),
            pl.BlockSpec((tm, DH), lambda g, i, rt: (i, 0)),
        ],
        out_specs=[pl.BlockSpec((tm, D), lambda g, i, rt: (i, rt[2 + g])), _ANY],
        out_shape=[jax.ShapeDtypeStruct((rows, IN_COLS), BF16), jax.ShapeDtypeStruct((D, IN_COLS), BF16)],
        scratch_shapes=[
            pltpu.VMEM((nt, tm, D), BF16), pltpu.VMEM((2, D, D), BF16),
            pltpu.SemaphoreType.DMA((nt,)), pltpu.SemaphoreType.DMA((2,)), pltpu.SemaphoreType.DMA,
            pltpu.SemaphoreType.DMA((3,)), pltpu.SemaphoreType.DMA((3,)),
            pltpu.SemaphoreType.DMA((3,)), pltpu.SemaphoreType.DMA((3,)),
        ],
        args=(u, w_shard, cos_t, sin_t),
        riders=riders,
        prefetch=route,
        riders_after_body=True,
    )


def _retention_fwd(proj_bf, intra, q_dec, k_dec, c_dec, riders=()):
    rows = proj_bf.shape[0]
    nc = rows // CHUNK

    def body(q_ref, k_ref, v_ref, m_ref, qd_ref, kd_ref, cd_ref, o_ref, sprev_ref, s_sc):
        @pl.when(pl.program_id(0) == 0)
        def _():
            s_sc[...] = jnp.zeros_like(s_sc)

        for h in range(HEADS):
            sl = slice(h * DH, (h + 1) * DH)
            q, k, v = q_ref[:, sl], k_ref[:, sl], v_ref[:, sl]
            state = s_sc[h]
            state_b = state.astype(BF16)
            sprev_ref[0, h] = state_b
            s = _dot_nt(q, k) * m_ref[h]
            inner = _dot(s.astype(BF16), v)
            cross = _dot(q, state_b) * qd_ref[:, sl]
            o_ref[:, sl] = (inner + cross).astype(BF16)
            k_scaled = (k.astype(F32) * kd_ref[:, sl]).astype(BF16)
            s_sc[h] = state * cd_ref[:, sl] + _dot_tn(k_scaled, v)

    chunk_spec = lambda col: pl.BlockSpec((CHUNK, D), lambda c: (c, col))
    const2 = lambda shape: pl.BlockSpec(shape, lambda c: (0, 0))
    return _hosted_call(
        body,
        name="retention_fwd",
        grid=(nc,),
        in_specs=[
            chunk_spec(0), chunk_spec(1), chunk_spec(2),
            pl.BlockSpec((HEADS, CHUNK, CHUNK), lambda c: (0, 0, 0)),
            const2((CHUNK, D)), const2((CHUNK, D)), const2((1, D)),
        ],
        out_specs=[
            pl.BlockSpec((CHUNK, D), lambda c: (c, 0)),
            pl.BlockSpec((1, HEADS, DH, DH), lambda c: (c, 0, 0, 0)),
        ],
        out_shape=[
            jax.ShapeDtypeStruct((rows, D), BF16),
            jax.ShapeDtypeStruct((nc, HEADS, DH, DH), BF16),
        ],
        scratch_shapes=[pltpu.VMEM((HEADS, DH, DH), F32)],
        args=(proj_bf, proj_bf, proj_bf, intra, q_dec, k_dec, c_dec),
        riders=riders,
    )


def _shift_down(x, first8_prev, d):
    rolled = pltpu.roll(x, d, axis=0)
    head = pltpu.roll(jnp.concatenate([first8_prev, x[0:8]], axis=0), d, axis=0)[8:16]
    return rolled, head


def _conv_and_gates(x, prev8, cw_ref, cb_ref, wa_ref, wx_ref, ba_ref, bx_ref, lam_ref, c_sc):
    cw = cw_ref[...]
    conv = cb_ref[...] + cw[3:4] * x
    head = cb_ref[...] + cw[3:4] * x[0:8]
    for d in (1, 2, 3):
        rolled, hd = _shift_down(x, prev8, d)
        conv = conv + cw[3 - d:4 - d] * rolled
        head = head + cw[3 - d:4 - d] * hd
    c_sc[...] = conv
    c_sc[0:8, :] = head
    c = c_sc[...]
    zr, zi = [], []
    for g in range(LRU_BLOCKS):
        sl = slice(g * LRU_BLOCK, (g + 1) * LRU_BLOCK)
        cg = c[:, sl].astype(BF16)
        zr.append(_dot(cg, wa_ref[g]))
        zi.append(_dot(cg, wx_ref[g]))
    r = _sigmoid(jnp.concatenate(zr, axis=1) + ba_ref[...])
    gate_i = _sigmoid(jnp.concatenate(zi, axis=1) + bx_ref[...])
    sp = _softplus(-lam_ref[...])
    log_a = (-LRU_C) * r * sp
    a = jnp.exp(log_a)
    mult = jnp.sqrt(_neg_expm1(2.0 * log_a))
    return c, r, gate_i, a, mult, log_a


def _lru_fwd(proj, conv_w, conv_b, wa, wx, ba, bx, lam, riders=()):
    rows = proj.shape[0]
    nt = rows // TM

    def body(x_ref, cw_ref, cb_ref, wa_ref, wx_ref, ba_ref, bx_ref, lam_ref,
             h_ref, c_ref, r_ref, i_ref, la_ref, mult_ref, prev_sc, carry_sc, c_sc, a_sc, u_sc, h_sc):
        i = pl.program_id(0)

        @pl.when(i == 0)
        def _():
            prev_sc[...] = jnp.zeros_like(prev_sc)
            carry_sc[...] = jnp.zeros_like(carry_sc)

        x = x_ref[...].astype(F32)
        c, r, gate_i, a, mult, log_a = _conv_and_gates(x, prev_sc[...], cw_ref, cb_ref, wa_ref, wx_ref, ba_ref, bx_ref,
                                                       lam_ref, c_sc)
        for ref, val in ((c_ref, c), (r_ref, r), (i_ref, gate_i), (la_ref, log_a), (mult_ref, mult)):
            ref[...] = val.astype(BF16)
        prev_sc[...] = x[TM - 8:TM]
        valid = _row_ids(i, TM, (TM, D)) >= PAD_ROWS
        a_sc[...] = a
        u_sc[...] = jnp.where(valid, mult * gate_i * c, 0.0)
        row8 = lax.broadcasted_iota(jnp.int32, (8, D), 0)

        def group(gi, hprev):
            r0 = pl.multiple_of(gi * 8, 8)
            aa = a_sc[pl.ds(r0, 8), :]
            uu = u_sc[pl.ds(r0, 8), :]
            for d in (1, 2, 4):
                a_sh = jnp.where(row8 >= d, pltpu.roll(aa, d, axis=0), 1.0)
                u_sh = jnp.where(row8 >= d, pltpu.roll(uu, d, axis=0), 0.0)
                uu = uu + aa * u_sh
                aa = aa * a_sh
            hb = aa * hprev + uu
            h_sc[pl.ds(r0, 8), :] = hb
            return hb[7:8, :]

        hlast = lax.fori_loop(0, TM // 8, group, carry_sc[0:1, :])
        carry_sc[0:1, :] = hlast
        h_ref[...] = h_sc[...].astype(BF16)

    vec = pl.BlockSpec((1, D), lambda i: (0, 0))
    wspec = pl.BlockSpec((LRU_BLOCKS, LRU_BLOCK, LRU_BLOCK), lambda i: (0, 0, 0))
    return _hosted_call(
        body,
        name="lru_fwd",
        grid=(nt,),
        in_specs=[
            pl.BlockSpec((TM, D), lambda i: (i, 4)),
            pl.BlockSpec((4, D), lambda i: (0, 0)),
            vec, wspec, wspec, vec, vec, vec,
        ],
        out_specs=[pl.BlockSpec((TM, D), lambda i: (i, 0)) for _ in range(6)],
        out_shape=[jax.ShapeDtypeStruct((rows, D), BF16) for _ in range(6)],
        scratch_shapes=[
            pltpu.VMEM((8, D), F32), pltpu.VMEM((8, D), F32),
            pltpu.VMEM((TM, D), F32), pltpu.VMEM((TM, D), F32), pltpu.VMEM((TM, D), F32), pltpu.VMEM((TM, D), F32),
        ],
        args=(proj, conv_w, conv_b, wa, wx, ba, bx, lam),
        riders=riders,
    )


def _group_norm(o):
    outs, rstds = [], []
    for h in range(HEADS):
        oh = o[:, h * DH:(h + 1) * DH]
        rs = lax.rsqrt(jnp.mean(oh * oh, axis=-1, keepdims=True) + EPS)
        outs.append(oh * rs)
        rstds.append(rs)
    return jnp.concatenate(outs, axis=1), rstds


def _mix_fwd(head, x2d, o, proj, h_lru, w_br, w_bl, w_o, ffn_norm_w, riders=()):
    rows = o.shape[0]
    nt = rows // TM

    def body(head_ref, x_ref, o_ref, gret_ref, lgate_ref, ga_ref, gb_ref, hl_ref, wbr_ref, wbl_ref, wo_ref, nw_ref,
             yret_ref, ylru_ref, h1_ref, u2_ref, rstd_ref):
        i = pl.program_id(0)
        on, _ = _group_norm(o_ref[...].astype(F32))
        gret = gret_ref[...].astype(F32)
        a_ret = (gret * _sigmoid(gret) * on).astype(BF16)
        y_ret = _dot(a_ret, wbr_ref[...])
        gl, _ = _gelu_and_grad(lgate_ref[...].astype(F32))
        a_lru = (gl * hl_ref[...].astype(F32)).astype(BF16)
        y_lru = _dot(a_lru, wbl_ref[...])
        mixed = (_sigmoid(ga_ref[...].astype(F32)) * y_ret + _sigmoid(gb_ref[...].astype(F32)) * y_lru).astype(BF16)
        delta = _dot(mixed, wo_ref[...])
        yret_ref[...] = y_ret.astype(BF16)
        ylru_ref[...] = y_lru.astype(BF16)

        def finish(h0):
            h1 = h0 + delta
            rs = lax.rsqrt(jnp.mean(h1 * h1, axis=-1, keepdims=True) + EPS)
            h1_ref[...] = h1
            u2_ref[...] = ((h1 * rs) * nw_ref[...]).astype(BF16)
            rstd_ref[...] = rs

        @pl.when(i == 0)
        def _():
            finish(head_ref[...])

        @pl.when(i > 0)
        def _():
            finish(x_ref[...])

    tile = lambda col: pl.BlockSpec((TM, D), lambda i: (i, col))
    wspec = pl.BlockSpec((D, D), lambda i: (0, 0))
    return _hosted_call(
        body,
        name="mix_fwd",
        grid=(nt,),
        in_specs=[
            pl.BlockSpec((FRONT, D), lambda i: (0, 0)),
            pl.BlockSpec((TM, D), lambda i: (jnp.maximum(i - 1, 0), 0)),
            tile(0), tile(3), tile(5), tile(6), tile(7), tile(0),
            wspec, wspec, wspec,
            pl.BlockSpec((1, D), lambda i: (0, 0)),
        ],
        out_specs=[tile(0), tile(0), tile(0), tile(0), pl.BlockSpec((TM, 1), lambda i: (i, 0))],
        out_shape=[
            jax.ShapeDtypeStruct((rows, D), BF16), jax.ShapeDtypeStruct((rows, D), BF16),
            jax.ShapeDtypeStruct((rows, D), F32), jax.ShapeDtypeStruct((rows, D), BF16),
            jax.ShapeDtypeStruct((rows, 1), F32),
        ],
        scratch_shapes=[],
        args=(head, x2d, o, proj, proj, proj, proj, h_lru, w_br, w_bl, w_o, ffn_norm_w),
        riders=riders,
    )


def _ffn_fwd_gate(u2, w_ffn_in, riders=()):
    rows = u2.shape[0]
    tm = _heavy_tile(rows)
    nb = FFN // FFN_BLOCK
    hid = lambda: pl.BlockSpec((tm, FFN_BLOCK), lambda i, j: (i, j))

    def body(u2_ref, w_hbm, silu_ref, dsilu_ref, act_ref, w_sc, w_sem):
        j = pl.program_id(1)

        @pl.when(jnp.logical_and(pl.program_id(0) == 0, j == 0))
        def _():
            cp = pltpu.make_async_copy(w_hbm, w_sc, w_sem)
            cp.start()
            cp.wait()

        u2 = u2_ref[...]
        g = _dot(u2, w_sc[:, pl.ds(pl.multiple_of(j * FFN_BLOCK, 128), FFN_BLOCK)])
        up = _dot(u2, w_sc[:, pl.ds(pl.multiple_of(FFN + j * FFN_BLOCK, 128), FFN_BLOCK)])
        sg = _sigmoid(g)
        silu = g * sg
        silu_ref[...] = silu.astype(BF16)
        dsilu_ref[...] = (up * (sg * (1.0 + g * (1.0 - sg)))).astype(BF16)
        act_ref[...] = (silu * up).astype(BF16)

    return _hosted_call(
        body,
        name="ffn_fwd_gate",
        grid=(rows // tm, nb),
        in_specs=[pl.BlockSpec((tm, D), lambda i, j: (i, 0)), _ANY],
        out_specs=[hid(), hid(), hid()],
        out_shape=[jax.ShapeDtypeStruct((rows, FFN), BF16)] * 3,
        scratch_shapes=[pltpu.VMEM((D, 2 * FFN), BF16), pltpu.SemaphoreType.DMA],
        args=(u2, w_ffn_in),
        riders=riders,
    )


def _ffn_out_loss(act, h1, w_ffn_out, target, final_norm_w):
    rows = act.shape[0]
    tm = _heavy_tile(rows)
    nt = rows // tm

    def body(act_ref, h1_ref, wo_ref, fnw_ref, tgt_hbm, dh2_ref, stats_ref, tgt_sc, tgt_sem):
        i = pl.program_id(0)
        slot = i % 2

        @pl.when(i == 0)
        def _():
            stats_ref[...] = jnp.zeros_like(stats_ref)
            tgt_sc[0, 0:FRONT, :] = jnp.zeros((FRONT, D), F32)
            cp = pltpu.make_async_copy(tgt_hbm.at[pl.ds(0, tm - FRONT)], tgt_sc.at[0, pl.ds(FRONT, tm - FRONT)], tgt_sem.at[0])
            cp.start()
            cp.wait()

        @pl.when(i + 1 < nt)
        def _():
            start = pl.multiple_of((i + 1) * tm - FRONT, FRONT)
            pltpu.make_async_copy(tgt_hbm.at[pl.ds(start, tm)], tgt_sc.at[1 - slot], tgt_sem.at[1 - slot]).start()

        @pl.when(i > 0)
        def _():
            pltpu.make_async_copy(tgt_hbm.at[pl.ds(0, tm)], tgt_sc.at[slot], tgt_sem.at[slot]).wait()

        h2 = h1_ref[...] + _dot(act_ref[...], wo_ref[...])
        rs = lax.rsqrt(jnp.mean(h2 * h2, axis=-1, keepdims=True) + EPS)
        n = h2 * rs
        fnw = fnw_ref[...]
        valid = _row_ids(i, tm, (tm, D)) >= FRONT
        diff = jnp.where(valid, n * fnw - tgt_sc[slot], 0.0)
        dy = diff * (1.0 / D)
        stats_ref[0:1, :] += (0.5 / D) * jnp.sum(diff * diff, axis=0, keepdims=True)
        stats_ref[1:2, :] += jnp.sum(dy * n, axis=0, keepdims=True)
        dn = dy * fnw
        dh2_ref[...] = rs * (dn - n * jnp.mean(dn * n, axis=-1, keepdims=True))

    return pl.pallas_call(
        body,
        name="ffn_out_loss",
        grid=(nt,),
        in_specs=[
            pl.BlockSpec((tm, FFN), lambda i: (i, 0)),
            pl.BlockSpec((tm, D), lambda i: (i, 0)),
            pl.BlockSpec((FFN, D), lambda i: (0, 0)),
            pl.BlockSpec((1, D), lambda i: (0, 0)),
            _ANY,
        ],
        out_specs=[pl.BlockSpec((tm, D), lambda i: (i, 0)), pl.BlockSpec((8, D), lambda i: (0, 0))],
        out_shape=[jax.ShapeDtypeStruct((rows, D), F32), jax.ShapeDtypeStruct((8, D), F32)],
        scratch_shapes=[pltpu.VMEM((2, tm, D), F32), pltpu.SemaphoreType.DMA((2,))],
        compiler_params=_params(("arbitrary",)),
    )(act, h1, w_ffn_out, final_norm_w, target)


def _ffn_bwd(dh2, silu, dsilu, h1, rstd2, w_ffn_in, w_ffn_out, ffn_norm_w):
    rows = dh2.shape[0]
    tm = _heavy_tile(rows)
    nb = FFN // FFN_BLOCK
    blk = lambda: pl.BlockSpec((tm, FFN_BLOCK), lambda i, j: (i, j))

    def gate_body(dh2_ref, silu_ref, dsilu_ref, wo_hbm, dg_ref, dup_ref, dh2b_sc, wo_sc, wo_sem):
        j = pl.program_id(1)

        @pl.when(jnp.logical_and(pl.program_id(0) == 0, j == 0))
        def _():
            cp = pltpu.make_async_copy(wo_hbm, wo_sc, wo_sem)
            cp.start()
            cp.wait()

        @pl.when(j == 0)
        def _():
            dh2b_sc[...] = dh2_ref[...].astype(BF16)

        dact = _dot_nt(dh2b_sc[...], wo_sc[pl.ds(pl.multiple_of(j * FFN_BLOCK, 128), FFN_BLOCK), :])
        dup_ref[...] = (dact * silu_ref[...].astype(F32)).astype(BF16)
        dg_ref[...] = (dact * dsilu_ref[...].astype(F32)).astype(BF16)

    dg, dup = pl.pallas_call(
        gate_body,
        name="ffn_bwd_gate",
        grid=(rows // tm, nb),
        in_specs=[pl.BlockSpec((tm, D), lambda i, j: (i, 0)), blk(), blk(), _ANY],
        out_specs=[blk(), blk()],
        out_shape=[jax.ShapeDtypeStruct((rows, FFN), BF16)] * 2,
        scratch_shapes=[pltpu.VMEM((tm, D), BF16), pltpu.VMEM((FFN, D), BF16), pltpu.SemaphoreType.DMA],
        compiler_params=_params(("arbitrary", "arbitrary")),
    )(dh2, silu, dsilu, w_ffn_out)

    def body(dg_ref, dup_ref, dh2_ref, h1_ref, rstd_ref, w_hbm, nw_ref, dh1_ref, stats_ref, w_sc, w_sem):
        @pl.when(pl.program_id(0) == 0)
        def _():
            stats_ref[...] = jnp.zeros_like(stats_ref)
            cp = pltpu.make_async_copy(w_hbm, w_sc, w_sem)
            cp.start()
            cp.wait()

        du = _dot_nt(dg_ref[...], w_sc[:, 0:FFN]) + _dot_nt(dup_ref[...], w_sc[:, FFN:2 * FFN])
        rs = rstd_ref[...]
        n = h1_ref[...] * rs
        stats_ref[0:1, :] += jnp.sum(du * n, axis=0, keepdims=True)
        dn = du * nw_ref[...]
        dh1_ref[...] = dh2_ref[...] + rs * (dn - n * jnp.mean(dn * n, axis=-1, keepdims=True))

    row = lambda width: pl.BlockSpec((tm, width), lambda i: (i, 0))
    dh1, stats = pl.pallas_call(
        body,
        name="ffn_bwd_in",
        grid=(rows // tm,),
        in_specs=[row(FFN), row(FFN), row(D), row(D), row(1), _ANY, pl.BlockSpec((1, D), lambda i: (0, 0))],
        out_specs=[row(D), pl.BlockSpec((8, D), lambda i: (0, 0))],
        out_shape=[jax.ShapeDtypeStruct((rows, D), F32), jax.ShapeDtypeStruct((8, D), F32)],
        scratch_shapes=[pltpu.VMEM((D, 2 * FFN), BF16), pltpu.SemaphoreType.DMA],
        compiler_params=_params(("arbitrary",)),
    )(dg, dup, dh2, h1, rstd2, w_ffn_in, ffn_norm_w)
    return dg, dup, dh1, stats


def _mix_bwd(dh1, o, proj, h_lru, y_ret, y_lru, w_br, w_bl, w_o, riders=()):
    rows = dh1.shape[0]
    tm = TM_MIX_BWD
    nt = rows // tm

    def body(dh1_ref, o_ref, gret_ref, lgate_ref, ga_ref, gb_ref, hl_ref, yret_ref, ylru_ref, wbr_ref, wbl_ref, wo_ref,
             dproj_ref, do_ref, dhl_ref, mixed_ref, aret_ref, alru_ref, dyret_ref, dylru_ref):
        dmixed = _dot_nt(dh1_ref[...].astype(BF16), wo_ref[...])
        y_ret, y_lru = yret_ref[...].astype(F32), ylru_ref[...].astype(F32)
        sa, sb = _sigmoid(ga_ref[...].astype(F32)), _sigmoid(gb_ref[...].astype(F32))
        mixed_ref[...] = (sa * y_ret + sb * y_lru).astype(BF16)
        dga = dmixed * y_ret * sa * (1.0 - sa)
        dgb = dmixed * y_lru * sb * (1.0 - sb)
        dy_ret = (dmixed * sa).astype(BF16)
        dy_lru = (dmixed * sb).astype(BF16)
        dyret_ref[...] = dy_ret
        dylru_ref[...] = dy_lru
        da_ret = _dot_nt(dy_ret, wbr_ref[...])
        da_lru = _dot_nt(dy_lru, wbl_ref[...])

        gret = gret_ref[...].astype(F32)
        sg = _sigmoid(gret)
        silu = gret * sg
        on, rstds = _group_norm(o_ref[...].astype(F32))
        aret_ref[...] = (silu * on).astype(BF16)
        dgret = da_ret * on * (sg * (1.0 + gret * (1.0 - sg)))
        don = da_ret * silu
        for h in range(HEADS):
            sl = slice(h * DH, (h + 1) * DH)
            onh, donh = on[:, sl], don[:, sl]
            do_ref[:, sl] = (rstds[h] * (donh - onh * jnp.mean(donh * onh, axis=-1, keepdims=True))).astype(BF16)

        gl, gl_grad = _gelu_and_grad(lgate_ref[...].astype(F32))
        hl = hl_ref[...].astype(F32)
        alru_ref[...] = (gl * hl).astype(BF16)
        dlgate = da_lru * hl * gl_grad
        dhl_ref[...] = (da_lru * gl).astype(BF16)

        zeros = jnp.zeros((tm, D), BF16)
        for col in (0, 1, 2, 4):
            dproj_ref[:, col * D:(col + 1) * D] = zeros
        dproj_ref[:, 3 * D:4 * D] = dgret.astype(BF16)
        dproj_ref[:, 5 * D:6 * D] = dlgate.astype(BF16)
        dproj_ref[:, 6 * D:7 * D] = dga.astype(BF16)
        dproj_ref[:, 7 * D:8 * D] = dgb.astype(BF16)

    tile = lambda col: pl.BlockSpec((tm, D), lambda i: (i, col))
    wspec = pl.BlockSpec((D, D), lambda i: (0, 0))
    bf = lambda: jax.ShapeDtypeStruct((rows, D), BF16)
    return _hosted_call(
        body,
        name="mix_bwd",
        grid=(nt,),
        in_specs=[tile(0), tile(0), tile(3), tile(5), tile(6), tile(7), tile(0), tile(0), tile(0), wspec, wspec, wspec],
        out_specs=[pl.BlockSpec((tm, IN_COLS), lambda i: (i, 0))] + [tile(0)] * 7,
        out_shape=[jax.ShapeDtypeStruct((rows, IN_COLS), BF16)] + [bf() for _ in range(7)],
        scratch_shapes=[],
        args=(dh1, o, proj, proj, proj, proj, h_lru, y_ret, y_lru, w_br, w_bl, w_o),
        riders=riders,
    )


def _retention_bwd(dproj, proj_bf, do, sprev, intra, intra_t, q_dec, k_dec, c_dec, cos_t, sin_t, riders=()):
    rows = proj_bf.shape[0]
    nc = rows // CHUNK

    def body(dproj_in_ref, q_ref, k_ref, v_ref, do_ref, sprev_ref, m_ref, mt_ref, qd_ref, kd_ref, cd_ref, cos_ref, sin_ref,
             dproj_ref, ds_sc):
        @pl.when(pl.program_id(0) == 0)
        def _():
            ds_sc[...] = jnp.zeros_like(ds_sc)

        cos, sin = cos_ref[...], sin_ref[...]

        def unrotate(dy):
            return dy * cos - pltpu.roll(dy, DH // 2, axis=1) * sin

        for h in range(HEADS):
            sl = slice(h * DH, (h + 1) * DH)
            q, k, v = q_ref[:, sl], k_ref[:, sl], v_ref[:, sl]
            do = do_ref[:, sl]
            dob = do.astype(BF16)
            doq = (do * qd_ref[:, sl]).astype(BF16)
            state_prev = sprev_ref[0, h]
            dstate = ds_sc[h]
            dstate_b = dstate.astype(BF16)
            s_t = (_dot_nt(k, q) * mt_ref[h]).astype(BF16)
            ds_t = (_dot_nt(v, dob) * mt_ref[h]).astype(BF16)
            ds = (_dot_nt(dob, v) * m_ref[h]).astype(BF16)
            kd = kd_ref[:, sl]
            dq = _dot(ds, k) + _dot_nt(doq, state_prev)
            dk = _dot(ds_t, q) + _dot_nt(v, dstate_b) * kd
            k_scaled = (k.astype(F32) * kd).astype(BF16)
            dv = _dot(s_t, dob) + _dot(k_scaled, dstate_b)
            ds_sc[h] = dstate * cd_ref[:, sl] + _dot_tn(q, doq)
            dproj_ref[:, sl] = unrotate(dq).astype(BF16)
            dproj_ref[:, D + h * DH:D + (h + 1) * DH] = (unrotate(dk) * QK_SCALE).astype(BF16)
            dproj_ref[:, 2 * D + h * DH:2 * D + (h + 1) * DH] = dv.astype(BF16)

    rev = lambda c: nc - 1 - c
    chunk_spec = lambda col: pl.BlockSpec((CHUNK, D), lambda c: (rev(c), col))
    const2 = lambda shape: pl.BlockSpec(shape, lambda c: (0, 0))
    const3 = pl.BlockSpec((HEADS, CHUNK, CHUNK), lambda c: (0, 0, 0))
    return _hosted_call(
        body,
        name="retention_bwd",
        grid=(nc,),
        in_specs=[
            pl.BlockSpec(memory_space=pl.ANY),
            chunk_spec(0), chunk_spec(1), chunk_spec(2), chunk_spec(0),
            pl.BlockSpec((1, HEADS, DH, DH), lambda c: (rev(c), 0, 0, 0)),
            const3, const3,
            const2((CHUNK, D)), const2((CHUNK, D)), const2((1, D)),
            pl.BlockSpec((CHUNK, DH), lambda c: (rev(c), 0)),
            pl.BlockSpec((CHUNK, DH), lambda c: (rev(c), 0)),
        ],
        out_specs=[pl.BlockSpec((CHUNK, 3 * D), lambda c: (rev(c), 0))],
        out_shape=[jax.ShapeDtypeStruct(dproj.shape, BF16)],
        aliases={0: 0},
        scratch_shapes=[pltpu.VMEM((HEADS, DH, DH), F32)],
        args=(dproj, proj_bf, proj_bf, proj_bf, do, sprev, intra, intra_t, q_dec, k_dec, c_dec, cos_t, sin_t),
        riders=riders,
    )


def _lru_bwd(dproj, proj, saved, dhl, conv_w, wa, wx, lam, riders=()):
    rows = proj.shape[0]
    nt = rows // TM
    per8 = TM // 8

    def body(dproj_in_ref, x_ref, xprev_ref, h_ref, hprev_ref, c_ref, r_ref, i_ref, la_ref, mult_ref, dhl_ref,
             cw_ref, wa_ref, wx_ref, lam_ref,
             dproj_ref, dwa_ref, dwx_ref, stats_ref, anext_sc, dhnext_sc, dcnext_sc, c_sc, b_sc, dh_sc):
        step = pl.program_id(0)
        i = nt - 1 - step

        @pl.when(step == 0)
        def _():
            anext_sc[...] = jnp.zeros_like(anext_sc)
            dhnext_sc[...] = jnp.zeros_like(dhnext_sc)
            dcnext_sc[...] = jnp.zeros_like(dcnext_sc)
            dwa_ref[...] = jnp.zeros_like(dwa_ref)
            dwx_ref[...] = jnp.zeros_like(dwx_ref)
            stats_ref[...] = jnp.zeros_like(stats_ref)

        first = i == 0
        x = x_ref[...].astype(F32)
        prev8 = jnp.where(first, 0.0, xprev_ref[8:16, :].astype(F32))
        c_b = c_ref[...]
        c, r, gate_i, mult = (ref[...].astype(F32) for ref in (c_ref, r_ref, i_ref, mult_ref))
        a = jnp.exp(la_ref[...].astype(F32))
        sp = _softplus(-lam_ref[...])
        dh_sc[...] = dhl_ref[...].astype(F32)

        b_sc[...] = pltpu.roll(a, TM - 1, axis=0)
        b_sc[TM - 1:TM, :] = anext_sc[0:1, :]
        anext_sc[0:1, :] = a[0:1, :]
        row8 = lax.broadcasted_iota(jnp.int32, (8, D), 0)

        def group(gi, dhnext):
            r0 = pl.multiple_of((per8 - 1 - gi) * 8, 8)
            bb = b_sc[pl.ds(r0, 8), :]
            uu = dh_sc[pl.ds(r0, 8), :]
            for d in (1, 2, 4):
                b_sh = jnp.where(row8 < 8 - d, pltpu.roll(bb, 8 - d, axis=0), 1.0)
                u_sh = jnp.where(row8 < 8 - d, pltpu.roll(uu, 8 - d, axis=0), 0.0)
                uu = uu + bb * u_sh
                bb = bb * b_sh
            dhb = bb * dhnext + uu
            dh_sc[pl.ds(r0, 8), :] = dhb
            return dhb[0:1, :]

        dhfirst = lax.fori_loop(0, per8, group, dhnext_sc[0:1, :])
        dhnext_sc[0:1, :] = dhfirst
        dh = dh_sc[...]

        h = h_ref[...].astype(F32)
        hprev8 = jnp.where(first, 0.0, hprev_ref[8:16, :].astype(F32))
        h_dn, h_head = _shift_down(h, hprev8, 1)
        c_sc[...] = h_dn
        c_sc[0:8, :] = h_head
        h_before = c_sc[...]

        valid = _row_ids(i, TM, (TM, D)) >= PAD_ROWS
        da = dh * h_before
        du = jnp.where(valid, dh, 0.0)
        dmult = du * gate_i * c
        dgate_i = du * mult * c
        dc = du * mult * gate_i
        dla = da * a - dmult * (a * a) / mult
        dla = jnp.where(valid, dla, 0.0)
        dr = dla * ((-LRU_C) * sp)
        dzr = dr * r * (1.0 - r)
        dzi = dgate_i * gate_i * (1.0 - gate_i)
        stats_ref[1:2, :] += jnp.sum(dzr, axis=0, keepdims=True)
        stats_ref[2:3, :] += jnp.sum(dzi, axis=0, keepdims=True)
        stats_ref[3:4, :] += jnp.sum(dla * ((-LRU_C) * r), axis=0, keepdims=True)
        dc_gate = []
        for g in range(LRU_BLOCKS):
            sl = slice(g * LRU_BLOCK, (g + 1) * LRU_BLOCK)
            cg = c_b[:, sl]
            dzr_g = dzr[:, sl].astype(BF16)
            dzi_g = dzi[:, sl].astype(BF16)
            dc_gate.append(_dot_nt(dzr_g, wa_ref[g]) + _dot_nt(dzi_g, wx_ref[g]))
            dwa_ref[g] += _dot_tn(cg, dzr_g)
            dwx_ref[g] += _dot_tn(cg, dzi_g)
        dc = dc + jnp.concatenate(dc_gate, axis=1)

        cw = cw_ref[...]
        stats_ref[0:1, :] += jnp.sum(dc, axis=0, keepdims=True)
        stats_ref[7:8, :] += jnp.sum(dc * x, axis=0, keepdims=True)
        dx = cw[3:4] * dc
        tail_src = jnp.concatenate([dc[TM - 8:TM], dcnext_sc[...]], axis=0)
        dx_tail = cw[3:4] * dc[TM - 8:TM]
        for d in (1, 2, 3):
            dx = dx + cw[3 - d:4 - d] * pltpu.roll(dc, TM - d, axis=0)
            dx_tail = dx_tail + cw[3 - d:4 - d] * pltpu.roll(tail_src, 16 - d, axis=0)[0:8]
            rolled, hd = _shift_down(x, prev8, d)
            b_sc[...] = rolled
            b_sc[0:8, :] = hd
            stats_ref[7 - d:8 - d, :] += jnp.sum(dc * b_sc[...], axis=0, keepdims=True)
        dcnext_sc[...] = dc[0:8]
        dproj_ref[...] = dx.astype(BF16)
        dproj_ref[TM - 8:TM, :] = dx_tail.astype(BF16)

    rev = lambda s: nt - 1 - s
    vec = pl.BlockSpec((1, D), lambda s: (0, 0))
    wspec = pl.BlockSpec((LRU_BLOCKS, LRU_BLOCK, LRU_BLOCK), lambda s: (0, 0, 0))
    prev16 = lambda col: pl.BlockSpec((16, D), lambda s: (jnp.maximum(rev(s) * (TM // 16) - 1, 0), col))
    tile = lambda: pl.BlockSpec((TM, D), lambda s: (rev(s), 0))
    h_lru, c_sv, r_sv, i_sv, la_sv, mult_sv = saved
    return _hosted_call(
        body,
        name="lru_bwd",
        grid=(nt,),
        in_specs=[
            pl.BlockSpec(memory_space=pl.ANY),
            pl.BlockSpec((TM, D), lambda s: (rev(s), 4)), prev16(4),
            tile(), prev16(0),
            tile(), tile(), tile(), tile(), tile(), tile(),
            pl.BlockSpec((4, D), lambda s: (0, 0)),
            wspec, wspec, vec,
        ],
        out_specs=[
            pl.BlockSpec((TM, D), lambda s: (rev(s), 4)),
            wspec, wspec,
            pl.BlockSpec((8, D), lambda s: (0, 0)),
        ],
        out_shape=[
            jax.ShapeDtypeStruct(dproj.shape, BF16),
            jax.ShapeDtypeStruct((LRU_BLOCKS, LRU_BLOCK, LRU_BLOCK), F32),
            jax.ShapeDtypeStruct((LRU_BLOCKS, LRU_BLOCK, LRU_BLOCK), F32),
            jax.ShapeDtypeStruct((8, D), F32),
        ],
        aliases={0: 0},
        scratch_shapes=[
            pltpu.VMEM((8, D), F32), pltpu.VMEM((8, D), F32), pltpu.VMEM((8, D), F32),
            pltpu.VMEM((TM, D), F32), pltpu.VMEM((TM, D), F32), pltpu.VMEM((TM, D), F32),
        ],
        args=(dproj, proj, proj, h_lru, h_lru, c_sv, r_sv, i_sv, la_sv, mult_sv, dhl, conv_w, wa, wx, lam),
        riders=riders,
    )


def _in_proj_bwd(dproj, w_in, part, prev=None, riders=()):
    rows = dproj.shape[0]
    tm = _heavy_tile(rows)
    nt = rows // tm
    first = 0 if part == 0 else (nt + 1) // 2
    count = (nt + 1) // 2 if part == 0 else nt - first

    def body(*refs):
        dproj_ref, w_hbm, du_ref, w_sc, w_sem = refs[-5:]

        @pl.when(pl.program_id(0) == 0)
        def _():
            cp = pltpu.make_async_copy(w_hbm, w_sc, w_sem)
            cp.start()
            cp.wait()

        du_ref[...] = _dot_nt(dproj_ref[...], w_sc[...])

    in_specs = [pl.BlockSpec((tm, IN_COLS), lambda i: (first + i, 0)), _ANY]
    args = (dproj, w_in)
    if prev is not None:
        in_specs = [_ANY] + in_specs
        args = (prev,) + args
    return _hosted_call(
        body,
        name="in_proj_bwd_%d" % part,
        grid=(count,),
        in_specs=in_specs,
        out_specs=[pl.BlockSpec((tm, D), lambda i: (first + i, 0))],
        out_shape=[jax.ShapeDtypeStruct((rows, D), F32)],
        scratch_shapes=[pltpu.VMEM((D, IN_COLS), BF16), pltpu.SemaphoreType.DMA],
        aliases={0: 0} if prev is not None else None,
        args=args,
        riders=riders,
    )


def _norm1_bwd(du, dh1, head, x2d, rstd1, norm_w, riders=()):
    rows = du.shape[0]

    def body(du_ref, dh1_ref, head_ref, x_ref, rstd_ref, nw_ref, gx_ref, ghead_ref, stats_ref):
        i = pl.program_id(0)

        @pl.when(i == 0)
        def _():
            stats_ref[...] = jnp.zeros_like(stats_ref)

        def finish(h0, out_ref):
            du = du_ref[...]
            rs = rstd_ref[...]
            n = h0 * rs
            stats_ref[0:1, :] += jnp.sum(du * n, axis=0, keepdims=True)
            dn = du * nw_ref[...]
            out_ref[...] = dh1_ref[...] + rs * (dn - n * jnp.mean(dn * n, axis=-1, keepdims=True))

        @pl.when(i == 0)
        def _():
            finish(head_ref[...], ghead_ref)

        @pl.when(i > 0)
        def _():
            finish(x_ref[...], gx_ref)

    tile = pl.BlockSpec((TM, D), lambda i: (i, 0))
    return _hosted_call(
        body,
        name="norm1_bwd",
        grid=(rows // TM,),
        in_specs=[
            tile, tile,
            pl.BlockSpec((FRONT, D), lambda i: (0, 0)),
            pl.BlockSpec((TM, D), lambda i: (jnp.maximum(i - 1, 0), 0)),
            pl.BlockSpec((TM, 1), lambda i: (i, 0)),
            pl.BlockSpec((1, D), lambda i: (0, 0)),
        ],
        out_specs=[
            pl.BlockSpec((TM, D), lambda i: (jnp.maximum(i - 1, 0), 0)),
            pl.BlockSpec((FRONT, D), lambda i: (0, 0)),
            pl.BlockSpec((8, D), lambda i: (0, 0)),
        ],
        out_shape=[
            jax.ShapeDtypeStruct(x2d.shape, F32),
            jax.ShapeDtypeStruct((FRONT, D), F32),
            jax.ShapeDtypeStruct((8, D), F32),
        ],
        scratch_shapes=[],
        args=(du, dh1, head, x2d, rstd1, norm_w),
        riders=riders,
    )


def _matmul_tn(name, x, dy, out_cols, col0=0, prev=None, k_block=None, n_block=None, riders=()):
    rows, kdim = x.shape
    ndim = dy.shape[1]
    kb = k_block or kdim
    nb = n_block or ndim
    step_bytes = lambda t: 2 * t * (kb * x.dtype.itemsize + nb * dy.dtype.itemsize) + 2 * kb * nb * 4
    tr = next(t for t in (2816, 1408, TM_HEAVY, TM) if rows % t == 0 and (t == TM or step_bytes(t) <= TN_VMEM_BUDGET))
    nr, nk, nn = rows // tr, kdim // kb, ndim // nb
    cb0 = col0 // nb

    def body(*refs):
        x_ref, dy_ref, out_ref = refs[-3], refs[-2], refs[-1]
        part = _dot_tn(x_ref[...].astype(BF16), dy_ref[...].astype(BF16))

        @pl.when(pl.program_id(2) == 0)
        def _():
            out_ref[...] = part

        @pl.when(pl.program_id(2) > 0)
        def _():
            out_ref[...] += part

    in_specs = [
        pl.BlockSpec((tr, kb), lambda n, k, r: (r, k)),
        pl.BlockSpec((tr, nb), lambda n, k, r: (r, n)),
    ]
    args = [x, dy]
    aliases = {}
    if prev is not None:
        in_specs = [pl.BlockSpec(memory_space=pl.ANY)] + in_specs
        args = [prev] + args
        aliases = {0: 0}
    (out,), rider_outs = _hosted_call(
        body,
        name=name,
        grid=(nn, nk, nr),
        in_specs=in_specs,
        out_specs=[pl.BlockSpec((kb, nb), lambda n, k, r: (k, cb0 + n))],
        out_shape=[jax.ShapeDtypeStruct((kdim, out_cols), F32)],
        scratch_shapes=[],
        aliases=aliases,
        args=args,
        riders=riders,
    )
    return (out, rider_outs) if riders else out


def _local_step(x2d, target, w, plan):
    rows = FRONT + x2d.shape[0]
    head = jnp.concatenate([jnp.zeros((PAD_ROWS, D), F32), w["meta_tokens"]], axis=0)
    cos_t, sin_t = _rope_tables(rows)
    intra, intra_t, q_dec, k_dec, c_dec = _decay_tables()
    grads = {}

    def hosted(host, fn, *args, **kwargs):
        outs, rider_outs = fn(*args, riders=plan.riders(host, w, grads), **kwargs)
        plan.after(host, rider_outs, w, grads)
        return outs

    (u1, rstd1), _ = _norm1(head, x2d, w["mix_norm_w"])
    proj, w["w_in"] = hosted("in_proj", _in_proj, u1, w["w_in_shard"], w["route"], cos_t, sin_t)
    o, sprev = hosted("retention_fwd", _retention_fwd, proj, intra, q_dec, k_dec, c_dec)
    lru_args = (w["conv_w"], w["conv_b"], w["lru_wa"], w["lru_wx"], w["lru_ba"], w["lru_bx"], w["lru_lambda"])
    lru_saved = hosted("lru_fwd", _lru_fwd, proj, *lru_args)
    h_lru = lru_saved[0]
    y_ret, y_lru, h1, u2, rstd2 = hosted("mix_fwd", _mix_fwd, head, x2d, o, proj, h_lru, w["w_branch_ret"],
                                         w["w_branch_lru"], w["w_out"], w["ffn_norm_w"])
    silu, dsilu, act = hosted("ffn_fwd_gate", _ffn_fwd_gate, u2, w["w_ffn_in"])
    dh2, stats_loss = _ffn_out_loss(act, h1, w["w_ffn_out"], target, w["final_norm_w"])

    dg, dup, dh1, stats_ffn = _ffn_bwd(dh2, silu, dsilu, h1, rstd2, w["w_ffn_in"], w["w_ffn_out"], w["ffn_norm_w"])
    grads["w_ffn_in"] = _matmul_tn("dw_ffn_up", u2, dup, 2 * FFN, col0=FFN, n_block=FFN_HALF,
                                   prev=_matmul_tn("dw_ffn_gate", u2, dg, 2 * FFN, n_block=FFN_HALF))
    grads["w_ffn_out"] = _matmul_tn("dw_ffn_out", act, dh2, D, k_block=FFN_HALF)
    (dproj, do, dhl, mixed, a_ret, a_lru, dy_ret, dy_lru) = hosted(
        "mix_bwd", _mix_bwd, dh1, o, proj, h_lru, y_ret, y_lru, w["w_branch_ret"], w["w_branch_lru"], w["w_out"])
    grads["w_out"] = _matmul_tn("dw_out", mixed, dh1, D)
    grads["w_branch_ret"] = _matmul_tn("dw_branch_ret", a_ret, dy_ret, D)
    grads["w_branch_lru"] = _matmul_tn("dw_branch_lru", a_lru, dy_lru, D)
    (dproj,) = hosted("retention_bwd", _retention_bwd, dproj, proj, do, sprev, intra, intra_t, q_dec, k_dec, c_dec,
                      cos_t, sin_t)
    dproj, grads["lru_wa"], grads["lru_wx"], stats_lru = hosted(
        "lru_bwd", _lru_bwd, dproj, proj, lru_saved, dhl, w["conv_w"], w["lru_wa"], w["lru_wx"], w["lru_lambda"])
    grads["w_in"], rider_outs = _matmul_tn("dw_in", u1, dproj, IN_COLS, n_block=D, riders=plan.riders("dw_in", w, grads))
    plan.after("dw_in", rider_outs, w, grads)
    (du1,) = hosted("in_proj_bwd_0", _in_proj_bwd, dproj, w["w_in"], 0)
    (du1,) = hosted("in_proj_bwd_1", _in_proj_bwd, dproj, w["w_in"], 1, du1)
    (grad_x, grad_head, stats_in), _ = _norm1_bwd(du1, dh1, head, x2d, rstd1, w["mix_norm_w"])
    return grad_x, grad_head, grads, [stats_loss, stats_ffn, stats_in, stats_lru]


BIG_PIECES = {
    "w_in": ("col", (D, 2 * D)),
    "w_ffn_in": ("col", (D, FFN_HALF)),
    "w_ffn_out": ("row", (FFN // 4, D)),
    "w_branch_ret": ("row", (D // 4, D)),
    "w_branch_lru": ("row", (D // 4, D)),
    "w_out": ("row", (D // 4, D)),
    "lru_wa": ("lru", (LRU_BLOCKS, LRU_BLOCK // 4, LRU_BLOCK)),
    "lru_wx": ("lru", (LRU_BLOCKS, LRU_BLOCK // 4, LRU_BLOCK)),
}
SMALL_PIECES = {"meta_tokens": ("col", (N_META, D // 4)), "conv_w": ("col", (4, D // 4))}


def _full_shape(kind, shard):
    if kind == "col":
        return (shard[0], 4 * shard[1])
    if kind == "row":
        return (4 * shard[0], shard[1])
    return (shard[0], 4 * shard[1], shard[2])


def _half_shape(kind, shard):
    return (shard[0] // 2,) + tuple(shard[1:])


def _aligned(start, multiple):
    return start if isinstance(start, int) else pl.multiple_of(start, multiple)


def _lead(h, size):
    if h is None:
        return pl.ds(0, size)
    return pl.ds(_aligned(h * (size // 2), size // 2), size // 2)


def _full_region(ref, kind, shard, s, h):
    if kind == "col":
        return ref.at[_lead(h, shard[0]), pl.ds(_aligned(s * shard[1], shard[1]), shard[1])]
    if kind == "row":
        size = shard[0] if h is None else shard[0] // 2
        start = s * shard[0] + (0 if h is None else h * (shard[0] // 2))
        return ref.at[pl.ds(_aligned(start, 16), size), :]
    return ref.at[_lead(h, shard[0]), pl.ds(_aligned(s * shard[1], shard[1]), shard[1]), :]


def _shard_region(ref, shard, h):
    return ref.at[_lead(h, shard[0])]


def _place():
    x, y, c = lax.axis_index("x"), lax.axis_index("y"), lax.axis_index("c")
    return x, y, c, 2 * x + y


def _other_chip(s, c, k):
    s2 = jnp.bitwise_xor(s, k)
    return s2, (s2 // 2, s2 % 2, c)


def _remote(src, dst, send_sem, recv_sem, dev):
    return pltpu.make_async_remote_copy(src_ref=src, dst_ref=dst, send_sem=send_sem, recv_sem=recv_sem,
                                        device_id=dev, device_id_type=MESH)


_ANY = pl.BlockSpec(memory_space=pl.ANY)


class _Rider:
    def __init__(self, ins, out_shapes, sem_shapes, build, aliased=False):
        self.ins, self.out_shapes, self.sem_shapes, self.build, self.aliased = ins, out_shapes, sem_shapes, build, aliased


def _hosted_call(body, *, name, grid, in_specs, out_specs, out_shape, scratch_shapes, args, riders=(), aliases=None,
                 prefetch=None, riders_after_body=False):
    n_in, n_out, n_sc = len(in_specs), len(out_shape), len(scratch_shapes)
    r_in = [a for r in riders for a in r.ins]
    r_out = [s for r in riders for s in r.out_shapes]
    r_sem = [s for r in riders for s in r.sem_shapes]
    lead = () if prefetch is None else (prefetch,)
    assert prefetch is None or not (aliases or any(r.aliased for r in riders))

    def full_body(*refs):
        head, refs = refs[:len(lead)], refs[len(lead):]
        ins, rin = refs[:n_in], refs[n_in:n_in + len(r_in)]
        o0 = n_in + len(r_in)
        outs, rout = refs[o0:o0 + n_out], refs[o0 + n_out:o0 + n_out + len(r_out)]
        s0 = o0 + n_out + len(r_out)
        scratch, rsem = refs[s0:s0 + n_sc], refs[s0 + n_sc:]
        starts, waits = [], []
        pi = po = ps = 0
        for r in riders:
            st, wt = r.build(rin[pi:pi + len(r.ins)], rout[po:po + len(r.out_shapes)], rsem[ps:ps + len(r.sem_shapes)])
            starts += st
            waits += wt
            pi, po, ps = pi + len(r.ins), po + len(r.out_shapes), ps + len(r.sem_shapes)
        first = functools.reduce(jnp.logical_and, [pl.program_id(d) == 0 for d in range(len(grid))])
        last = functools.reduce(jnp.logical_and, [pl.program_id(d) == grid[d] - 1 for d in range(len(grid))])

        def start_riders():
            @pl.when(first)
            def _():
                for cp in starts:
                    cp.start()

        if riders and not riders_after_body:
            start_riders()
        body(*head, *ins, *outs, *scratch)
        if riders and riders_after_body:
            start_riders()
        if riders:
            @pl.when(last)
            def _():
                for wait in waits:
                    wait()

    io_aliases = dict(aliases or {})
    pi = po = 0
    for r in riders:
        if r.aliased:
            for q in range(len(r.ins)):
                io_aliases[n_in + pi + q] = n_out + po + q
        pi, po = pi + len(r.ins), po + len(r.out_shapes)
    specs = dict(
        grid=grid,
        in_specs=list(in_specs) + [_ANY] * len(r_in),
        out_specs=list(out_specs) + [_ANY] * len(r_out),
        scratch_shapes=list(scratch_shapes) + r_sem,
    )
    if prefetch is not None:
        specs = dict(grid_spec=pltpu.PrefetchScalarGridSpec(num_scalar_prefetch=1, **specs))
    res = pl.pallas_call(
        full_body,
        name=name,
        out_shape=list(out_shape) + r_out,
        input_output_aliases=io_aliases,
        compiler_params=_params(("arbitrary",) * len(grid)),
        **specs,
    )(*lead, *args, *r_in)
    rider_outs, po = [], n_out
    for r in riders:
        rider_outs.append(list(res[po:po + len(r.out_shapes)]))
        po += len(r.out_shapes)
    return list(res[:n_out]), rider_outs


def _run_riders(name, riders):
    r_in = [a for r in riders for a in r.ins]
    r_out = [s for r in riders for s in r.out_shapes]
    r_sem = [s for r in riders for s in r.sem_shapes]

    def body(*refs):
        rin, rout, rsem = refs[:len(r_in)], refs[len(r_in):len(r_in) + len(r_out)], refs[len(r_in) + len(r_out):]
        pi = po = ps = 0
        for r in riders:
            starts, waits = r.build(rin[pi:pi + len(r.ins)], rout[po:po + len(r.out_shapes)], rsem[ps:ps + len(r.sem_shapes)])
            for cp in starts:
                cp.start()
            for wait in waits:
                wait()
            pi, po, ps = pi + len(r.ins), po + len(r.out_shapes), ps + len(r.sem_shapes)

    io_aliases = {}
    pi = po = 0
    for r in riders:
        if r.aliased:
            for q in range(len(r.ins)):
                io_aliases[pi + q] = po + q
        pi, po = pi + len(r.ins), po + len(r.out_shapes)
    res = pl.pallas_call(
        body,
        name=name,
        in_specs=[_ANY] * len(r_in),
        out_specs=[_ANY] * len(r_out),
        out_shape=r_out,
        scratch_shapes=r_sem,
        input_output_aliases=io_aliases,
    )(*r_in)
    outs, po = [], 0
    for r in riders:
        outs.append(list(res[po:po + len(r.out_shapes)]))
        po += len(r.out_shapes)
    return outs


def _piece(name):
    if name in BIG_PIECES:
        return (name, *BIG_PIECES[name], True)
    return (name, *SMALL_PIECES[name], False)


def _gather_rider(shards, names):
    pieces = [_piece(n) for n in names]
    n = len(pieces)

    def build(ins, outs, sems):
        local_sem, ici_send, ici_recv = sems
        _, _, c, s = _place()
        starts, waits = [], []
        for p, (_, kind, shard, split) in enumerate(pieces):
            cp = pltpu.make_async_copy(ins[p], _full_region(outs[p], kind, shard, s, None), local_sem.at[p])
            starts.append(cp)
            waits.append(cp.wait)
            h = c if split else None
            for k in (1, 2, 3):
                s2, dev = _other_chip(s, c, k)
                cp = _remote(_shard_region(ins[p], shard, h), _full_region(outs[p], kind, shard, s, h),
                             ici_send.at[p, k - 1], ici_recv.at[p, k - 1], dev)
                starts.append(cp)
                waits.append(cp.wait_send)
                region = _full_region(outs[p], kind, shard, s2, h)
                waits.append(_remote(region, region, ici_send.at[p, k - 1], ici_recv.at[p, k - 1], dev).wait_recv)
        return starts, waits

    return _Rider(
        [shards[name] for name in names],
        [jax.ShapeDtypeStruct(_full_shape(kind, shard), shards[name].dtype) for name, kind, shard, _ in pieces],
        [pltpu.SemaphoreType.DMA((n,)), pltpu.SemaphoreType.DMA((n, 3)), pltpu.SemaphoreType.DMA((n, 3))],
        build)


def _forward_rider(gathered, names):
    pieces = [_piece(n) for n in names]
    n = len(pieces)

    def build(ins, outs, sems):
        fwd_send, fwd_recv = sems
        x, y, c, s = _place()
        sibling = (x, y, 1 - c)
        starts, waits = [], []
        for p, (_, kind, shard, _) in enumerate(pieces):
            for k in (1, 2, 3):
                s2, _ = _other_chip(s, c, k)
                mine = _full_region(outs[p], kind, shard, s2, c)
                theirs = _full_region(outs[p], kind, shard, s2, 1 - c)
                cp = _remote(mine, mine, fwd_send.at[p, k - 1], fwd_recv.at[p, k - 1], sibling)
                starts.append(cp)
                waits.append(cp.wait_send)
                waits.append(_remote(theirs, theirs, fwd_send.at[p, k - 1], fwd_recv.at[p, k - 1], sibling).wait_recv)
        return starts, waits

    return _Rider(
        [gathered[name] for name in names],
        [jax.ShapeDtypeStruct(gathered[name].shape, gathered[name].dtype) for name in names],
        [pltpu.SemaphoreType.DMA((n, 3)), pltpu.SemaphoreType.DMA((n, 3))],
        build, aliased=True)


def _pair_exchange_rider(grads, names):
    n = len(names)

    def build(ins, outs, sems):
        send_sem, recv_sem = sems
        x, y, c, _ = _place()
        sibling = (x, y, 1 - c)
        starts, waits = [], []
        for p, name in enumerate(names):
            kind, shard = BIG_PIECES[name]
            for s2 in range(4):
                cp = _remote(_full_region(ins[p], kind, shard, s2, 1 - c), outs[p].at[s2], send_sem.at[p, s2],
                             recv_sem.at[p, s2], sibling)
                starts.append(cp)
                waits.append(cp.wait_send)
                waits.append(_remote(outs[p].at[s2], outs[p].at[s2], send_sem.at[p, s2], recv_sem.at[p, s2], sibling).wait_recv)
        return starts, waits

    return _Rider(
        [grads[name] for name in names],
        [jax.ShapeDtypeStruct((4,) + _half_shape(*BIG_PIECES[name]), F32) for name in names],
        [pltpu.SemaphoreType.DMA((n, 4))] * 2,
        build)


def _half_specs(kind, shard):
    half = shard[0] // 2
    if kind == "col":
        full = pl.BlockSpec((half, shard[1]), lambda j, pr: (pr[1], j))
        buf = pl.BlockSpec((None, half, shard[1]), lambda j, pr: (j, 0, 0))
    elif kind == "row":
        full = pl.BlockSpec((half, shard[1]), lambda j, pr: (2 * j + pr[1], 0))
        buf = pl.BlockSpec((None, half, shard[1]), lambda j, pr: (j, 0, 0))
    else:
        full = pl.BlockSpec((half, shard[1], shard[2]), lambda j, pr: (pr[1], j, 0))
        buf = pl.BlockSpec((None, half, shard[1], shard[2]), lambda j, pr: (j, 0, 0, 0))
    return full, buf


def _pair_sum(name, grad, recv, place):
    kind, shard = BIG_PIECES[name]
    full, buf = _half_specs(kind, shard)

    def body(pr, g_ref, r_ref, o_ref):
        o_ref[...] = (g_ref[...] + r_ref[...]).astype(BF16)

    return pl.pallas_call(
        body,
        name="pair_sum_" + name,
        grid_spec=pltpu.PrefetchScalarGridSpec(num_scalar_prefetch=1, grid=(4,), in_specs=[full, buf], out_specs=buf),
        out_shape=jax.ShapeDtypeStruct((4,) + _half_shape(kind, shard), BF16),
        compiler_params=_params(("arbitrary",)),
    )(place, grad, recv)


def _chip_exchange_rider(sums, names):
    n = len(names)

    def build(ins, outs, sems):
        send_sem, recv_sem = sems
        _, _, c, s = _place()
        starts, waits = [], []
        for p in range(n):
            for k in (1, 2, 3):
                s2, dev = _other_chip(s, c, k)
                cp = _remote(ins[p].at[s2], outs[p].at[k - 1], send_sem.at[p, k - 1], recv_sem.at[p, k - 1], dev)
                starts.append(cp)
                waits.append(cp.wait_send)
                waits.append(_remote(outs[p].at[k - 1], outs[p].at[k - 1], send_sem.at[p, k - 1], recv_sem.at[p, k - 1],
                                     dev).wait_recv)
        return starts, waits

    return _Rider(
        [sums[name] for name in names],
        [jax.ShapeDtypeStruct((3,) + _half_shape(*BIG_PIECES[name]), BF16) for name in names],
        [pltpu.SemaphoreType.DMA((n, 3))] * 2,
        build)


def _chip_sum(name, grad, recv_pair, recv_chip, place):
    kind, shard = BIG_PIECES[name]
    half = shard[0] // 2
    tail = tuple(shard[1:])
    zeros = (0,) * len(tail)
    if kind == "col":
        full = pl.BlockSpec((half,) + tail, lambda j, pr: (pr[1], pr[0]))
    elif kind == "row":
        full = pl.BlockSpec((half,) + tail, lambda j, pr: (2 * pr[0] + pr[1], 0))
    else:
        full = pl.BlockSpec((half,) + tail, lambda j, pr: (pr[1], pr[0], 0))
    pair = pl.BlockSpec((None, half) + tail, lambda j, pr: (pr[0], 0) + zeros)
    chip = pl.BlockSpec((3, half) + tail, lambda j, pr: (0, 0) + zeros)
    out = pl.BlockSpec((half,) + tail, lambda j, pr: (pr[1],) + zeros)

    def body(pr, g_ref, rp_ref, rc_ref, o_ref):
        total = g_ref[...] + rp_ref[...]
        for k in range(3):
            total = total + rc_ref[k].astype(F32)
        o_ref[...] = total

    return pl.pallas_call(
        body,
        name="chip_sum_" + name,
        grid_spec=pltpu.PrefetchScalarGridSpec(num_scalar_prefetch=1, grid=(1,), in_specs=[full, pair, chip], out_specs=out),
        out_shape=jax.ShapeDtypeStruct(shard, F32),
        compiler_params=_params(("arbitrary",)),
    )(place, grad, recv_pair, recv_chip)


def _sibling_exchange_rider(halves, names):
    n = len(names)

    def build(ins, outs, sems):
        send_sem, recv_sem = sems
        x, y, c, _ = _place()
        sibling = (x, y, 1 - c)
        starts, waits = [], []
        for p, name in enumerate(names):
            shard = BIG_PIECES[name][1]
            mine = _shard_region(outs[p], shard, c)
            theirs = _shard_region(outs[p], shard, 1 - c)
            cp = _remote(mine, mine, send_sem.at[p], recv_sem.at[p], sibling)
            starts.append(cp)
            waits.append(cp.wait_send)
            waits.append(_remote(theirs, theirs, send_sem.at[p], recv_sem.at[p], sibling).wait_recv)
        return starts, waits

    return _Rider(
        [halves[name] for name in names],
        [jax.ShapeDtypeStruct(BIG_PIECES[name][1], F32) for name in names],
        [pltpu.SemaphoreType.DMA((n,))] * 2,
        build, aliased=True)


FIRST_WEIGHTS = ["meta_tokens", "conv_w"]
WEIGHT_GROUPS = {
    "lru": ["lru_wa", "lru_wx"],
    "branch": ["w_branch_ret", "w_branch_lru", "w_out"],
    "ffn_in": ["w_ffn_in"],
    "ffn_out": ["w_ffn_out"],
}
WEIGHT_SCHEDULE = {
    "in_proj": [("gather", "lru")],
    "retention_fwd": [("forward", "lru"), ("gather", "branch")],
    "lru_fwd": [("forward", "branch"), ("gather", "ffn_in")],
    "mix_fwd": [("forward", "ffn_in"), ("gather", "ffn_out")],
    "ffn_fwd_gate": [("forward", "ffn_out")],
}
GRAD_GROUPS = {
    "ffn": ["w_ffn_in", "w_ffn_out"],
    "mixer": ["w_out", "w_branch_ret", "w_branch_lru", "lru_wa", "lru_wx"],
    "in": ["w_in"],
}
GRAD_SCHEDULE = {
    "mix_bwd": [("pair", "ffn")],
    "retention_bwd": [("chip", "ffn")],
    "lru_bwd": [("sibling", "ffn")],
    "dw_in": [("pair", "mixer")],
    "in_proj_bwd_0": [("chip", "mixer"), ("pair", "in")],
    "in_proj_bwd_1": [("sibling", "mixer"), ("chip", "in")],
}


class _CommPlan:
    def __init__(self, shards, place):
        self.shards, self.place = shards, place
        self.late = {}
        self.recv_pair, self.sums, self.recv_chip, self.halves, self.final = {}, {}, {}, {}, {}

    def _grad_rider(self, stage, group, grads):
        names = GRAD_GROUPS[group]
        if stage == "pair":
            return _pair_exchange_rider(grads, names)
        if stage == "chip":
            return _chip_exchange_rider(self.sums, names)
        return _sibling_exchange_rider(self.halves, names)

    def _grad_after(self, stage, group, outs, grads):
        names = GRAD_GROUPS[group]
        if stage == "pair":
            for n, o in zip(names, outs):
                self.recv_pair[n] = o
                self.sums[n] = _pair_sum(n, grads[n], o, self.place)
        elif stage == "chip":
            for n, o in zip(names, outs):
                self.halves[n] = _chip_sum(n, grads[n], self.recv_pair[n], o, self.place)
        else:
            self.final.update(zip(names, outs))

    def riders(self, host, w, grads):
        if host in WEIGHT_SCHEDULE:
            return [_gather_rider(self.shards, WEIGHT_GROUPS[group]) if stage == "gather"
                    else _forward_rider(self.late, WEIGHT_GROUPS[group]) for stage, group in WEIGHT_SCHEDULE[host]]
        return [self._grad_rider(stage, group, grads) for stage, group in GRAD_SCHEDULE.get(host, [])]

    def after(self, host, rider_outs, w, grads):
        for (stage, group), outs in zip(WEIGHT_SCHEDULE.get(host, []), rider_outs):
            (self.late if stage == "gather" else w).update(zip(WEIGHT_GROUPS[group], outs))
        for (stage, group), outs in zip(GRAD_SCHEDULE.get(host, []), rider_outs):
            self._grad_after(stage, group, outs, grads)

    def finish(self):
        (outs,) = _run_riders("grad_tail_exchange", [_sibling_exchange_rider(self.halves, GRAD_GROUPS["in"])])
        self.final.update(zip(GRAD_GROUPS["in"], outs))
        return self.final


def _adamw_math(w, g, m, v):
    m = ADAM_B1 * m + (1.0 - ADAM_B1) * g
    v = ADAM_B2 * v + (1.0 - ADAM_B2) * (g * g)
    m_hat = m / (1.0 - ADAM_B1 ** ADAM_STEP)
    v_hat = v / (1.0 - ADAM_B2 ** ADAM_STEP)
    delta = -ADAM_LR * (m_hat / (jnp.sqrt(v_hat) + ADAM_EPS) + ADAM_WD * w)
    return delta, m, v


def _adamw(name, g, w, m, v):
    rows, cols = g.shape
    tr = rows // 4 if rows % 32 == 0 else rows

    def body(g_ref, w_ref, m_ref, v_ref, go_ref, d_ref, mo_ref, vo_ref):
        gv = g_ref[...]
        delta, m2, v2 = _adamw_math(w_ref[...], gv, m_ref[...], v_ref[...])
        go_ref[...] = gv
        d_ref[...] = delta
        mo_ref[...] = m2
        vo_ref[...] = v2

    spec = pl.BlockSpec((tr, cols), lambda i: (i, 0))
    return pl.pallas_call(
        body,
        name="adamw_" + name,
        grid=(rows // tr,),
        in_specs=[spec] * 4,
        out_specs=[spec] * 4,
        out_shape=[jax.ShapeDtypeStruct((rows, cols), F32)] * 4,
        compiler_params=_params(("arbitrary",)),
    )(g, w, m, v)


SMALL_ROWS = 48
VEC_ROWS = {"final_norm_w": 1, "ffn_norm_w": 8, "mix_norm_w": 16, "conv_b": 24, "lru_ba": 25, "lru_bx": 26, "lru_lambda": 27}
VEC_NAMES = list(VEC_ROWS)
CONV_W_ROW = 28
META_ROW = 32


def _small_all_reduce(partial):
    def body(in_ref, out_ref, buf, local_sem, send_sem, recv_sem):
        x, y, c, s = _place()
        me = 2 * s + c
        cp = pltpu.make_async_copy(in_ref, buf.at[me], local_sem)
        cp.start()
        started = []
        for k in range(1, 8):
            peer = jnp.bitwise_xor(me, k)
            dev = (peer // 4, (peer // 2) % 2, peer % 2)
            rd = _remote(in_ref, buf.at[me], send_sem.at[k - 1], recv_sem.at[k - 1], dev)
            rd.start()
            started.append(rd)
        for k in range(1, 8):
            peer = jnp.bitwise_xor(me, k)
            dev = (peer // 4, (peer // 2) % 2, peer % 2)
            _remote(in_ref, buf.at[peer], send_sem.at[k - 1], recv_sem.at[k - 1], dev).wait_recv()
        for rd in started:
            rd.wait_send()
        cp.wait()
        total = buf[0]
        for d in range(1, 8):
            total = total + buf[d]
        out_ref[...] = total

    return pl.pallas_call(
        body,
        name="small_all_reduce",
        in_specs=[pl.BlockSpec(memory_space=pltpu.VMEM)],
        out_specs=pl.BlockSpec(memory_space=pltpu.VMEM),
        out_shape=jax.ShapeDtypeStruct((SMALL_ROWS, D), F32),
        scratch_shapes=[pltpu.VMEM((8, SMALL_ROWS, D), F32), pltpu.SemaphoreType.DMA,
                        pltpu.SemaphoreType.DMA((7,)), pltpu.SemaphoreType.DMA((7,))],
    )(partial)


def _small_update(total, place, vecs, conv, meta):
    nvec = len(VEC_NAMES)
    qcols = D // 4

    def body(pr, tot_ref, col_ref, *refs):
        vec_refs = refs[:3 * nvec]
        conv_refs = refs[3 * nvec:3 * nvec + 3]
        meta_refs = refs[3 * nvec + 3:3 * nvec + 6]
        outs = refs[3 * nvec + 6:]
        loss_ref, vec_out, conv_out, meta_out = outs[0], outs[1:5], outs[5:9], outs[9:13]
        loss_ref[...] = jnp.sum(tot_ref[0:1, :], axis=1, keepdims=True)
        for o in vec_out:
            o[...] = jnp.zeros_like(o)
        for j, name in enumerate(VEC_NAMES):
            w, m, v = (r[...] for r in vec_refs[3 * j:3 * j + 3])
            g = tot_ref[VEC_ROWS[name]:VEC_ROWS[name] + 1, :]
            if name == "lru_lambda":
                g = -g / (1.0 + jnp.exp(w))
            for o, val in zip(vec_out, (g,) + _adamw_math(w, g, m, v)):
                o[j:j + 1, :] = val
        for row, n_rows, ins, group in ((CONV_W_ROW, 4, conv_refs, conv_out), (META_ROW, N_META, meta_refs, meta_out)):
            g = col_ref[row:row + n_rows, :]
            w, m, v = (r[...] for r in ins)
            for o, val in zip(group, (g,) + _adamw_math(w, g, m, v)):
                o[...] = val

    whole = lambda shape: pl.BlockSpec(shape, lambda i, pr: (0,) * len(shape))
    in_specs = [whole((SMALL_ROWS, D)), pl.BlockSpec((SMALL_ROWS, qcols), lambda i, pr: (0, pr[0]))]
    in_specs += [whole((1, D))] * (3 * nvec) + [whole((4, qcols))] * 3 + [whole((N_META, qcols))] * 3
    out_shapes = [(1, 1)] + [(8, D)] * 4 + [(4, qcols)] * 4 + [(N_META, qcols)] * 4
    return pl.pallas_call(
        body,
        name="small_update",
        grid_spec=pltpu.PrefetchScalarGridSpec(num_scalar_prefetch=1, grid=(1,), in_specs=in_specs,
                                               out_specs=[whole(s) for s in out_shapes]),
        out_shape=[jax.ShapeDtypeStruct(s, F32) for s in out_shapes],
        compiler_params=_params(("arbitrary",)),
    )(place, total, total, *[a for t in vecs for a in t], *conv, *meta)


WEIGHT_ORDER = ["meta_tokens", "mix_norm_w", "w_in", "conv_w", "conv_b", "lru_wa", "lru_ba", "lru_wx", "lru_bx", "lru_lambda",
                "w_branch_ret", "w_branch_lru", "w_out", "ffn_norm_w", "w_ffn_in", "w_ffn_out", "final_norm_w"]


def kernel(x, meta_tokens, mix_norm_w, w_in, conv_w, conv_b, lru_wa, lru_ba, lru_wx, lru_bx, lru_lambda, w_branch_ret, w_branch_lru, w_out, ffn_norm_w, w_ffn_in, w_ffn_out, final_norm_w, loss_target, m_meta_tokens, m_mix_norm_w, m_w_in, m_conv_w, m_conv_b, m_lru_wa, m_lru_ba, m_lru_wx, m_lru_bx, m_lru_lambda, m_w_branch_ret, m_w_branch_lru, m_w_out, m_ffn_norm_w, m_w_ffn_in, m_w_ffn_out, m_final_norm_w, v_meta_tokens, v_mix_norm_w, v_w_in, v_conv_w, v_conv_b, v_lru_wa, v_lru_ba, v_lru_wx, v_lru_bx, v_lru_lambda, v_w_branch_ret, v_w_branch_lru, v_w_out, v_ffn_norm_w, v_w_ffn_in, v_w_ffn_out, v_final_norm_w):
    args = locals()
    wts = {n: args[n] for n in WEIGHT_ORDER}
    mom = {n: args["m_" + n] for n in WEIGHT_ORDER}
    var = {n: args["v_" + n] for n in WEIGHT_ORDER}
    place = jnp.stack([2 * lax.axis_index("x") + lax.axis_index("y"), lax.axis_index("c")]).astype(jnp.int32)

    shards = {n: wts[n][0].astype(BF16) for n in BIG_PIECES}
    shards["meta_tokens"] = wts["meta_tokens"]
    shards["conv_w"] = wts["conv_w"][0]
    plan = _CommPlan(shards, place)
    (first,) = _run_riders("gather_first", [_gather_rider(shards, FIRST_WEIGHTS)])
    w = dict(zip(FIRST_WEIGHTS, first))
    for n in VEC_NAMES:
        w[n] = wts[n].reshape(1, D)
    chips = jnp.bitwise_xor(place[0], jnp.arange(4, dtype=jnp.int32))
    w["route"] = jnp.concatenate([place, jnp.stack([2 * chips, 2 * chips + 1], axis=1).reshape(8)])
    w["w_in_shard"] = shards["w_in"]

    grad_x, grad_head, _, stats = _local_step(x[0], loss_target[0], w, plan)
    shard_grads = plan.finish()

    out = {}
    for n in BIG_PIECES:
        shape2d = (-1, wts[n].shape[-1])
        res = _adamw(n, *[a.reshape(shape2d) for a in (shard_grads[n], wts[n], mom[n], var[n])])
        out[n] = [r.reshape(wts[n].shape) for r in res]

    partial = jnp.concatenate(stats + [grad_head[PAD_ROWS:]], axis=0)
    total = _small_all_reduce(partial)
    vecs = [tuple(a[n].reshape(1, D) for a in (wts, mom, var)) for n in VEC_NAMES]
    conv = tuple(a["conv_w"][0] for a in (wts, mom, var))
    meta = tuple(a["meta_tokens"] for a in (wts, mom, var))
    res = _small_update(total, place, vecs, conv, meta)
    loss = res[0].reshape(())
    for j, n in enumerate(VEC_NAMES):
        out[n] = [r[j].reshape(wts[n].shape) for r in res[1:5]]
    out["conv_w"] = [r.reshape(wts["conv_w"].shape) for r in res[5:9]]
    out["meta_tokens"] = list(res[9:13])

    return (loss, grad_x.reshape(x.shape)) + tuple(out[n][kind] for kind in range(4) for n in WEIGHT_ORDER)
```

```python
import functools
import math

import jax
import jax.numpy as jnp
from jax import lax
from jax.experimental import pallas as pl
from jax.experimental.pallas import tpu as pltpu

F32 = jnp.float32
BF16 = jnp.bfloat16

D = 1024
HEADS = 8
DH = 128
CHUNK = 256
N_META = 16
FRONT = 256
PAD_ROWS = FRONT - N_META
LRU_BLOCKS = 4
LRU_BLOCK = 256
LRU_C = 8.0
FFN = 2816
FFN_HALF = FFN // 2
IN_COLS = 8 * D
ROPE_BASE = 10000.0
EPS = 1e-6
QK_SCALE = DH ** -0.5

ADAM_LR = 0.001
ADAM_B1 = 0.9
ADAM_B2 = 0.999
ADAM_EPS = 1e-08
ADAM_WD = 0.01
ADAM_STEP = 10

TM = 256
TM_HEAVY = 768
FFN_BLOCK = FFN // 2
TM_MIX_BWD = 256
VMEM_LIMIT = 60 * 1024 * 1024
TN_VMEM_BUDGET = 40 * 1024 * 1024

NT_DIMS = (((1,), (1,)), ((), ()))
TN_DIMS = (((0,), (0,)), ((), ()))
MESH = pl.DeviceIdType.MESH


def _params(sem=None):
    if sem is None:
        return pltpu.CompilerParams(vmem_limit_bytes=VMEM_LIMIT)
    return pltpu.CompilerParams(dimension_semantics=sem, vmem_limit_bytes=VMEM_LIMIT)


def _dot(a, b):
    return jnp.dot(a, b, preferred_element_type=F32)


def _dot_nt(a, b):
    return lax.dot_general(a, b, NT_DIMS, preferred_element_type=F32)


def _dot_tn(a, b):
    return lax.dot_general(a, b, TN_DIMS, preferred_element_type=F32)


def _sigmoid(z):
    return pl.reciprocal(1.0 + jnp.exp(-z), approx=True)


def _log1p(x):
    return jnp.where(x < 1e-3, x * (1.0 - x * (0.5 - x * (1.0 / 3.0))), jnp.log(1.0 + x))


def _softplus(x):
    return jnp.maximum(x, 0.0) + _log1p(jnp.exp(-jnp.abs(x)))


def _neg_expm1(x):
    series = -x * (1.0 + x * (0.5 + x * (1.0 / 6.0 + x * (1.0 / 24.0 + x * (1.0 / 120.0)))))
    return jnp.where(x > -0.05, series, 1.0 - jnp.exp(x))


_GELU_K = math.sqrt(2.0 / math.pi)


def _gelu_and_grad(x):
    inner = _GELU_K * (x + 0.044715 * x * x * x)
    t = jnp.tanh(inner)
    val = 0.5 * x * (1.0 + t)
    grad = 0.5 * (1.0 + t) + 0.5 * x * (1.0 - t * t) * _GELU_K * (1.0 + 3.0 * 0.044715 * x * x)
    return val, grad


def _row_ids(i, rows, shape):
    return i * rows + lax.broadcasted_iota(jnp.int32, shape, 0)


def _rope_tables(rows):
    pos = jnp.arange(rows, dtype=jnp.int32) - PAD_ROWS
    inv_freq = ROPE_BASE ** (-jnp.arange(0, DH, 2, dtype=F32) / DH)
    ang = pos.astype(F32)[:, None] * inv_freq[None, :]
    cos, sin = jnp.cos(ang), jnp.sin(ang)
    return jnp.concatenate([cos, cos], axis=1), jnp.concatenate([-sin, sin], axis=1)


def _decay_tables():
    log_g = jnp.log(1.0 - 2.0 ** (-5.0 - jnp.arange(HEADS, dtype=F32)))
    idx = jnp.arange(CHUNK, dtype=F32)
    diff = idx[:, None] - idx[None, :]
    intra = jnp.where(diff[None] >= 0, jnp.exp(jnp.maximum(diff, 0.0)[None] * log_g[:, None, None]), 0.0)
    q_decay = jnp.exp((idx + 1.0)[:, None] * log_g[None, :])
    k_decay = jnp.exp((CHUNK - 1.0 - idx)[:, None] * log_g[None, :])
    chunk_decay = jnp.exp(CHUNK * log_g)
    wide = lambda a: jnp.repeat(a, DH, axis=-1)
    return intra, jnp.swapaxes(intra, 1, 2), wide(q_decay), wide(k_decay), wide(chunk_decay[None, :])


def _norm1(head, x2d, norm_w, riders=()):
    rows = FRONT + x2d.shape[0]

    def body(head_ref, x_ref, nw_ref, u_ref, rstd_ref):
        def norm(hv):
            rs = lax.rsqrt(jnp.mean(hv * hv, axis=-1, keepdims=True) + EPS)
            u_ref[...] = ((hv * rs) * nw_ref[...]).astype(BF16)
            rstd_ref[...] = rs

        @pl.when(pl.program_id(0) == 0)
        def _():
            norm(head_ref[...])

        @pl.when(pl.program_id(0) > 0)
        def _():
            norm(x_ref[...])

    return _hosted_call(
        body,
        name="norm1",
        grid=(rows // TM,),
        in_specs=[
            pl.BlockSpec((FRONT, D), lambda i: (0, 0)),
            pl.BlockSpec((TM, D), lambda i: (jnp.maximum(i - 1, 0), 0)),
            pl.BlockSpec((1, D), lambda i: (0, 0)),
        ],
        out_specs=[pl.BlockSpec((TM, D), lambda i: (i, 0)), pl.BlockSpec((TM, 1), lambda i: (i, 0))],
        out_shape=[jax.ShapeDtypeStruct((rows, D), BF16), jax.ShapeDtypeStruct((rows, 1), F32)],
        scratch_shapes=[],
        args=(head, x2d, norm_w),
        riders=riders,
    )


def _heavy_tile(rows):
    return TM_HEAVY if rows % TM_HEAVY == 0 else TM


def _in_proj(u, w_shard, route, cos_t, sin_t, riders=()):
    rows = u.shape[0]
    tm = _heavy_tile(rows)
    nt = rows // tm
    kind, shard = BIG_PIECES["w_in"]

    def body(route_ref, u_hbm, wsh_ref, cos_ref, sin_ref, proj_ref, wfull_ref,
             u_sc, w_sc, u_sem, w_sem, local_sem, ici_send, ici_recv, fwd_send, fwd_recv):
        g, i = pl.program_id(0), pl.program_id(1)
        s, c = route_ref[0], route_ref[1]
        gid = route_ref[2 + g]
        sibling = (s // 2, s % 2, 1 - c)
        local = pltpu.make_async_copy(wsh_ref, _full_region(wfull_ref, kind, shard, s, None), local_sem)
        u_copies = [pltpu.make_async_copy(u_hbm.at[pl.ds(t * tm, tm)], u_sc.at[t], u_sem.at[t]) for t in range(nt)]
        sends, arrivals = [], []
        for k in (1, 2, 3):
            s2, dev = _other_chip(s, c, k)
            sends.append(_remote(_shard_region(wsh_ref, shard, c), _full_region(wfull_ref, kind, shard, s, c),
                                 ici_send.at[k - 1], ici_recv.at[k - 1], dev))
            mine = _full_region(wfull_ref, kind, shard, s2, c)
            theirs = _full_region(wfull_ref, kind, shard, s2, 1 - c)
            arrivals.append((_remote(mine, mine, ici_send.at[k - 1], ici_recv.at[k - 1], dev),
                             _remote(mine, mine, fwd_send.at[k - 1], fwd_recv.at[k - 1], sibling),
                             _remote(theirs, theirs, fwd_send.at[k - 1], fwd_recv.at[k - 1], sibling)))

        slot = g % 2
        last_tile = i == nt - 1

        def block_copy(src, col, to_slot):
            return pltpu.make_async_copy(src.at[:, pl.ds(pl.multiple_of(col * D, D), D)], w_sc.at[to_slot],
                                         w_sem.at[to_slot])

        @pl.when(jnp.logical_and(g == 0, i == 0))
        def _():
            for cp in sends:
                cp.start()
            local.start()
            for cp in u_copies:
                cp.start()
            cp = block_copy(wsh_ref, 0, 0)
            cp.start()
            cp.wait()

        @pl.when(jnp.logical_and(last_tile, g == 0))
        def _():
            block_copy(wsh_ref, 1, 1).start()

        for k, (arrived, forward, forwarded) in zip((1, 2, 3), arrivals):
            @pl.when(jnp.logical_and(last_tile, g == 2 * k - 1))
            def _():
                arrived.wait_recv()
                forward.start()
                forwarded.wait_recv()
                block_copy(wfull_ref, route_ref[2 + 2 * k], 0).start()

            @pl.when(jnp.logical_and(last_tile, g == 2 * k))
            def _():
                block_copy(wfull_ref, route_ref[3 + 2 * k], 1).start()

        @pl.when(jnp.logical_and(i == 0, g > 0))
        def _():
            block_copy(wfull_ref, 0, slot).wait()

        for t in range(nt):
            @pl.when(jnp.logical_and(g == 0, i == t))
            def _():
                u_copies[t].wait()

        acc = _dot(u_sc[i], w_sc[slot])

        @pl.when(gid < 2)
        def _():
            scale = jnp.where(gid == 1, QK_SCALE, 1.0).astype(F32)
            for h in range(HEADS):
                sl = slice(h * DH, (h + 1) * DH)
                blk = acc[:, sl]
                out = (blk * cos_ref[...] + pltpu.roll(blk, DH // 2, axis=1) * sin_ref[...]) * scale
                proj_ref[:, sl] = out.astype(BF16)

        @pl.when(gid >= 2)
        def _():
            proj_ref[...] = acc.astype(BF16)

        @pl.when(jnp.logical_and(g == 7, i == nt - 1))
        def _():
            local.wait()
            for cp in sends:
                cp.wait_send()
            for _, forward, _ in arrivals:
                forward.wait_send()

    return _hosted_call(
        body,
        name="in_proj",
        grid=(8, nt),
        in_specs=[
            _ANY, _ANY,
            pl.BlockSpec((tm, DH), lambda g, i, rt: (i, 0)),
            pl.BlockSpec((tm, DH), lambda g, i, rt: (i, 0)),
        ],
        out_specs=[pl.BlockSpec((tm, D), lambda g, i, rt: (i, rt[2 + g])), _ANY],
        out_shape=[jax.ShapeDtypeStruct((rows, IN_COLS), BF16), jax.ShapeDtypeStruct((D, IN_COLS), BF16)],
        scratch_shapes=[
            pltpu.VMEM((nt, tm, D), BF16), pltpu.VMEM((2, D, D), BF16),
            pltpu.SemaphoreType.DMA((nt,)), pltpu.SemaphoreType.DMA((2,)), pltpu.SemaphoreType.DMA,
            pltpu.SemaphoreType.DMA((3,)), pltpu.SemaphoreType.DMA((3,)),
            pltpu.SemaphoreType.DMA((3,)), pltpu.SemaphoreType.DMA((3,)),
        ],
        args=(u, w_shard, cos_t, sin_t),
        riders=riders,
        prefetch=route,
        riders_after_body=True,
    )


def _retention_fwd(proj_bf, intra, q_dec, k_dec, c_dec, riders=()):
    rows = proj_bf.shape[0]
    nc = rows // CHUNK

    def body(q_ref, k_ref, v_ref, m_ref, qd_ref, kd_ref, cd_ref, o_ref, sprev_ref, s_sc):
        @pl.when(pl.program_id(0) == 0)
        def _():
            s_sc[...] = jnp.zeros_like(s_sc)

        for h in range(HEADS):
            sl = slice(h * DH, (h + 1) * DH)
            q, k, v = q_ref[:, sl], k_ref[:, sl], v_ref[:, sl]
            state = s_sc[h]
            state_b = state.astype(BF16)
            sprev_ref[0, h] = state_b
            s = _dot_nt(q, k) * m_ref[h]
            inner = _dot(s.astype(BF16), v)
            cross = _dot(q, state_b) * qd_ref[:, sl]
            o_ref[:, sl] = (inner + cross).astype(BF16)
            k_scaled = (k.astype(F32) * kd_ref[:, sl]).astype(BF16)
            s_sc[h] = state * cd_ref[:, sl] + _dot_tn(k_scaled, v)

    chunk_spec = lambda col: pl.BlockSpec((CHUNK, D), lambda c: (c, col))
    const2 = lambda shape: pl.BlockSpec(shape, lambda c: (0, 0))
    return _hosted_call(
        body,
        name="retention_fwd",
        grid=(nc,),
        in_specs=[
            chunk_spec(0), chunk_spec(1), chunk_spec(2),
            pl.BlockSpec((HEADS, CHUNK, CHUNK), lambda c: (0, 0, 0)),
            const2((CHUNK, D)), const2((CHUNK, D)), const2((1, D)),
        ],
        out_specs=[
            pl.BlockSpec((CHUNK, D), lambda c: (c, 0)),
            pl.BlockSpec((1, HEADS, DH, DH), lambda c: (c, 0, 0, 0)),
        ],
        out_shape=[
            jax.ShapeDtypeStruct((rows, D), BF16),
            jax.ShapeDtypeStruct((nc, HEADS, DH, DH), BF16),
        ],
        scratch_shapes=[pltpu.VMEM((HEADS, DH, DH), F32)],
        args=(proj_bf, proj_bf, proj_bf, intra, q_dec, k_dec, c_dec),
        riders=riders,
    )


def _shift_down(x, first8_prev, d):
    rolled = pltpu.roll(x, d, axis=0)
    head = pltpu.roll(jnp.concatenate([first8_prev, x[0:8]], axis=0), d, axis=0)[8:16]
    return rolled, head


def _conv_and_gates(x, prev8, cw_ref, cb_ref, wa_ref, wx_ref, ba_ref, bx_ref, lam_ref, c_sc):
    cw = cw_ref[...]
    conv = cb_ref[...] + cw[3:4] * x
    head = cb_ref[...] + cw[3:4] * x[0:8]
    for d in (1, 2, 3):
        rolled, hd = _shift_down(x, prev8, d)
        conv = conv + cw[3 - d:4 - d] * rolled
        head = head + cw[3 - d:4 - d] * hd
    c_sc[...] = conv
    c_sc[0:8, :] = head
    c = c_sc[...]
    zr, zi = [], []
    for g in range(LRU_BLOCKS):
        sl = slice(g * LRU_BLOCK, (g + 1) * LRU_BLOCK)
        cg = c[:, sl].astype(BF16)
        zr.append(_dot(cg, wa_ref[g]))
        zi.append(_dot(cg, wx_ref[g]))
    r = _sigmoid(jnp.concatenate(zr, axis=1) + ba_ref[...])
    gate_i = _sigmoid(jnp.concatenate(zi, axis=1) + bx_ref[...])
    sp = _softplus(-lam_ref[...])
    log_a = (-LRU_C) * r * sp
    a = jnp.exp(log_a)
    mult = jnp.sqrt(_neg_expm1(2.0 * log_a))
    return c, r, gate_i, a, mult, log_a


def _lru_fwd(proj, conv_w, conv_b, wa, wx, ba, bx, lam, riders=()):
    rows = proj.shape[0]
    nt = rows // TM

    def body(x_ref, cw_ref, cb_ref, wa_ref, wx_ref, ba_ref, bx_ref, lam_ref,
             h_ref, c_ref, r_ref, i_ref, la_ref, mult_ref, prev_sc, carry_sc, c_sc, a_sc, u_sc, h_sc):
        i = pl.program_id(0)

        @pl.when(i == 0)
        def _():
            prev_sc[...] = jnp.zeros_like(prev_sc)
            carry_sc[...] = jnp.zeros_like(carry_sc)

        x = x_ref[...].astype(F32)
        c, r, gate_i, a, mult, log_a = _conv_and_gates(x, prev_sc[...], cw_ref, cb_ref, wa_ref, wx_ref, ba_ref, bx_ref,
                                                       lam_ref, c_sc)
        for ref, val in ((c_ref, c), (r_ref, r), (i_ref, gate_i), (la_ref, log_a), (mult_ref, mult)):
            ref[...] = val.astype(BF16)
        prev_sc[...] = x[TM - 8:TM]
        valid = _row_ids(i, TM, (TM, D)) >= PAD_ROWS
        a_sc[...] = a
        u_sc[...] = jnp.where(valid, mult * gate_i * c, 0.0)
        row8 = lax.broadcasted_iota(jnp.int32, (8, D), 0)

        def group(gi, hprev):
            r0 = pl.multiple_of(gi * 8, 8)
            aa = a_sc[pl.ds(r0, 8), :]
            uu = u_sc[pl.ds(r0, 8), :]
            for d in (1, 2, 4):
                a_sh = jnp.where(row8 >= d, pltpu.roll(aa, d, axis=0), 1.0)
                u_sh = jnp.where(row8 >= d, pltpu.roll(uu, d, axis=0), 0.0)
                uu = uu + aa * u_sh
                aa = aa * a_sh
            hb = aa * hprev + uu
            h_sc[pl.ds(r0, 8), :] = hb
            return hb[7:8, :]

        hlast = lax.fori_loop(0, TM // 8, group, carry_sc[0:1, :])
        carry_sc[0:1, :] = hlast
        h_ref[...] = h_sc[...].astype(BF16)

    vec = pl.BlockSpec((1, D), lambda i: (0, 0))
    wspec = pl.BlockSpec((LRU_BLOCKS, LRU_BLOCK, LRU_BLOCK), lambda i: (0, 0, 0))
    return _hosted_call(
        body,
        name="lru_fwd",
        grid=(nt,),
        in_specs=[
            pl.BlockSpec((TM, D), lambda i: (i, 4)),
            pl.BlockSpec((4, D), lambda i: (0, 0)),
            vec, wspec, wspec, vec, vec, vec,
        ],
        out_specs=[pl.BlockSpec((TM, D), lambda i: (i, 0)) for _ in range(6)],
        out_shape=[jax.ShapeDtypeStruct((rows, D), BF16) for _ in range(6)],
        scratch_shapes=[
            pltpu.VMEM((8, D), F32), pltpu.VMEM((8, D), F32),
            pltpu.VMEM((TM, D), F32), pltpu.VMEM((TM, D), F32), pltpu.VMEM((TM, D), F32), pltpu.VMEM((TM, D), F32),
        ],
        args=(proj, conv_w, conv_b, wa, wx, ba, bx, lam),
        riders=riders,
    )


def _group_norm(o):
    outs, rstds = [], []
    for h in range(HEADS):
        oh = o[:, h * DH:(h + 1) * DH]
        rs = lax.rsqrt(jnp.mean(oh * oh, axis=-1, keepdims=True) + EPS)
        outs.append(oh * rs)
        rstds.append(rs)
    return jnp.concatenate(outs, axis=1), rstds


def _mix_fwd(head, x2d, o, proj, h_lru, w_br, w_bl, w_o, ffn_norm_w, riders=()):
    rows = o.shape[0]
    nt = rows // TM

    def body(head_ref, x_ref, o_ref, gret_ref, lgate_ref, ga_ref, gb_ref, hl_ref, wbr_ref, wbl_ref, wo_ref, nw_ref,
             yret_ref, ylru_ref, h1_ref, u2_ref, rstd_ref):
        i = pl.program_id(0)
        on, _ = _group_norm(o_ref[...].astype(F32))
        gret = gret_ref[...].astype(F32)
        a_ret = (gret * _sigmoid(gret) * on).astype(BF16)
        y_ret = _dot(a_ret, wbr_ref[...])
        gl, _ = _gelu_and_grad(lgate_ref[...].astype(F32))
        a_lru = (gl * hl_ref[...].astype(F32)).astype(BF16)
        y_lru = _dot(a_lru, wbl_ref[...])
        mixed = (_sigmoid(ga_ref[...].astype(F32)) * y_ret + _sigmoid(gb_ref[...].astype(F32)) * y_lru).astype(BF16)
        delta = _dot(mixed, wo_ref[...])
        yret_ref[...] = y_ret.astype(BF16)
        ylru_ref[...] = y_lru.astype(BF16)

        def finish(h0):
            h1 = h0 + delta
            rs = lax.rsqrt(jnp.mean(h1 * h1, axis=-1, keepdims=True) + EPS)
            h1_ref[...] = h1
            u2_ref[...] = ((h1 * rs) * nw_ref[...]).astype(BF16)
            rstd_ref[...] = rs

        @pl.when(i == 0)
        def _():
            finish(head_ref[...])

        @pl.when(i > 0)
        def _():
            finish(x_ref[...])

    tile = lambda col: pl.BlockSpec((TM, D), lambda i: (i, col))
    wspec = pl.BlockSpec((D, D), lambda i: (0, 0))
    return _hosted_call(
        body,
        name="mix_fwd",
        grid=(nt,),
        in_specs=[
            pl.BlockSpec((FRONT, D), lambda i: (0, 0)),
            pl.BlockSpec((TM, D), lambda i: (jnp.maximum(i - 1, 0), 0)),
            tile(0), tile(3), tile(5), tile(6), tile(7), tile(0),
            wspec, wspec, wspec,
            pl.BlockSpec((1, D), lambda i: (0, 0)),
        ],
        out_specs=[tile(0), tile(0), tile(0), tile(0), pl.BlockSpec((TM, 1), lambda i: (i, 0))],
        out_shape=[
            jax.ShapeDtypeStruct((rows, D), BF16), jax.ShapeDtypeStruct((rows, D), BF16),
            jax.ShapeDtypeStruct((rows, D), F32), jax.ShapeDtypeStruct((rows, D), BF16),
            jax.ShapeDtypeStruct((rows, 1), F32),
        ],
        scratch_shapes=[],
        args=(head, x2d, o, proj, proj, proj, proj, h_lru, w_br, w_bl, w_o, ffn_norm_w),
        riders=riders,
    )


def _ffn_fwd_gate(u2, w_ffn_in, riders=()):
    rows = u2.shape[0]
    tm = _heavy_tile(rows)
    nb = FFN // FFN_BLOCK
    hid = lambda: pl.BlockSpec((tm, FFN_BLOCK), lambda i, j: (i, j))

    def body(u2_ref, w_hbm, silu_ref, dsilu_ref, act_ref, w_sc, w_sem):
        j = pl.program_id(1)

        @pl.when(jnp.logical_and(pl.program_id(0) == 0, j == 0))
        def _():
            cp = pltpu.make_async_copy(w_hbm, w_sc, w_sem)
            cp.start()
            cp.wait()

        u2 = u2_ref[...]
        g = _dot(u2, w_sc[:, pl.ds(pl.multiple_of(j * FFN_BLOCK, 128), FFN_BLOCK)])
        up = _dot(u2, w_sc[:, pl.ds(pl.multiple_of(FFN + j * FFN_BLOCK, 128), FFN_BLOCK)])
        sg = _sigmoid(g)
        silu = g * sg
        silu_ref[...] = silu.astype(BF16)
        dsilu_ref[...] = (up * (sg * (1.0 + g * (1.0 - sg)))).astype(BF16)
        act_ref[...] = (silu * up).astype(BF16)

    return _hosted_call(
        body,
        name="ffn_fwd_gate",
        grid=(rows // tm, nb),
        in_specs=[pl.BlockSpec((tm, D), lambda i, j: (i, 0)), _ANY],
        out_specs=[hid(), hid(), hid()],
        out_shape=[jax.ShapeDtypeStruct((rows, FFN), BF16)] * 3,
        scratch_shapes=[pltpu.VMEM((D, 2 * FFN), BF16), pltpu.SemaphoreType.DMA],
        args=(u2, w_ffn_in),
        riders=riders,
    )


def _ffn_out_loss(act, h1, w_ffn_out, target, final_norm_w):
    rows = act.shape[0]
    tm = _heavy_tile(rows)
    nt = rows // tm

    def body(act_ref, h1_ref, wo_ref, fnw_ref, tgt_hbm, dh2_ref, stats_ref, tgt_sc, tgt_sem):
        i = pl.program_id(0)
        slot = i % 2

        @pl.when(i == 0)
        def _():
            stats_ref[...] = jnp.zeros_like(stats_ref)
            tgt_sc[0, 0:FRONT, :] = jnp.zeros((FRONT, D), F32)
            cp = pltpu.make_async_copy(tgt_hbm.at[pl.ds(0, tm - FRONT)], tgt_sc.at[0, pl.ds(FRONT, tm - FRONT)], tgt_sem.at[0])
            cp.start()
            cp.wait()

        @pl.when(i + 1 < nt)
        def _():
            start = pl.multiple_of((i + 1) * tm - FRONT, FRONT)
            pltpu.make_async_copy(tgt_hbm.at[pl.ds(start, tm)], tgt_sc.at[1 - slot], tgt_sem.at[1 - slot]).start()

        @pl.when(i > 0)
        def _():
            pltpu.make_async_copy(tgt_hbm.at[pl.ds(0, tm)], tgt_sc.at[slot], tgt_sem.at[slot]).wait()

        h2 = h1_ref[...] + _dot(act_ref[...], wo_ref[...])
        rs = lax.rsqrt(jnp.mean(h2 * h2, axis=-1, keepdims=True) + EPS)
        n = h2 * rs
        fnw = fnw_ref[...]
        valid = _row_ids(i, tm, (tm, D)) >= FRONT
        diff = jnp.where(valid, n * fnw - tgt_sc[slot], 0.0)
        dy = diff * (1.0 / D)
        stats_ref[0:1, :] += (0.5 / D) * jnp.sum(diff * diff, axis=0, keepdims=True)
        stats_ref[1:2, :] += jnp.sum(dy * n, axis=0, keepdims=True)
        dn = dy * fnw
        dh2_ref[...] = rs * (dn - n * jnp.mean(dn * n, axis=-1, keepdims=True))

    return pl.pallas_call(
        body,
        name="ffn_out_loss",
        grid=(nt,),
        in_specs=[
            pl.BlockSpec((tm, FFN), lambda i: (i, 0)),
            pl.BlockSpec((tm, D), lambda i: (i, 0)),
            pl.BlockSpec((FFN, D), lambda i: (0, 0)),
            pl.BlockSpec((1, D), lambda i: (0, 0)),
            _ANY,
        ],
        out_specs=[pl.BlockSpec((tm, D), lambda i: (i, 0)), pl.BlockSpec((8, D), lambda i: (0, 0))],
        out_shape=[jax.ShapeDtypeStruct((rows, D), F32), jax.ShapeDtypeStruct((8, D), F32)],
        scratch_shapes=[pltpu.VMEM((2, tm, D), F32), pltpu.SemaphoreType.DMA((2,))],
        compiler_params=_params(("arbitrary",)),
    )(act, h1, w_ffn_out, final_norm_w, target)


def _ffn_bwd(dh2, silu, dsilu, h1, rstd2, w_ffn_in, w_ffn_out, ffn_norm_w):
    rows = dh2.shape[0]
    tm = _heavy_tile(rows)
    nb = FFN // FFN_BLOCK
    blk = lambda: pl.BlockSpec((tm, FFN_BLOCK), lambda i, j: (i, j))

    def gate_body(dh2_ref, silu_ref, dsilu_ref, wo_hbm, dg_ref, dup_ref, dh2b_sc, wo_sc, wo_sem):
        j = pl.program_id(1)

        @pl.when(jnp.logical_and(pl.program_id(0) == 0, j == 0))
        def _():
            cp = pltpu.make_async_copy(wo_hbm, wo_sc, wo_sem)
            cp.start()
            cp.wait()

        @pl.when(j == 0)
        def _():
            dh2b_sc[...] = dh2_ref[...].astype(BF16)

        dact = _dot_nt(dh2b_sc[...], wo_sc[pl.ds(pl.multiple_of(j * FFN_BLOCK, 128), FFN_BLOCK), :])
        dup_ref[...] = (dact * silu_ref[...].astype(F32)).astype(BF16)
        dg_ref[...] = (dact * dsilu_ref[...].astype(F32)).astype(BF16)

    dg, dup = pl.pallas_call(
        gate_body,
        name="ffn_bwd_gate",
        grid=(rows // tm, nb),
        in_specs=[pl.BlockSpec((tm, D), lambda i, j: (i, 0)), blk(), blk(), _ANY],
        out_specs=[blk(), blk()],
        out_shape=[jax.ShapeDtypeStruct((rows, FFN), BF16)] * 2,
        scratch_shapes=[pltpu.VMEM((tm, D), BF16), pltpu.VMEM((FFN, D), BF16), pltpu.SemaphoreType.DMA],
        compiler_params=_params(("arbitrary", "arbitrary")),
    )(dh2, silu, dsilu, w_ffn_out)

    def body(dg_ref, dup_ref, dh2_ref, h1_ref, rstd_ref, w_hbm, nw_ref, dh1_ref, stats_ref, w_sc, w_sem):
        @pl.when(pl.program_id(0) == 0)
        def _():
            stats_ref[...] = jnp.zeros_like(stats_ref)
            cp = pltpu.make_async_copy(w_hbm, w_sc, w_sem)
            cp.start()
            cp.wait()

        du = _dot_nt(dg_ref[...], w_sc[:, 0:FFN]) + _dot_nt(dup_ref[...], w_sc[:, FFN:2 * FFN])
        rs = rstd_ref[...]
        n = h1_ref[...] * rs
        stats_ref[0:1, :] += jnp.sum(du * n, axis=0, keepdims=True)
        dn = du * nw_ref[...]
        dh1_ref[...] = dh2_ref[...] + rs * (dn - n * jnp.mean(dn * n, axis=-1, keepdims=True))

    row = lambda width: pl.BlockSpec((tm, width), lambda i: (i, 0))
    dh1, stats = pl.pallas_call(
        body,
        name="ffn_bwd_in",
        grid=(rows // tm,),
        in_specs=[row(FFN), row(FFN), row(D), row(D), row(1), _ANY, pl.BlockSpec((1, D), lambda i: (0, 0))],
        out_specs=[row(D), pl.BlockSpec((8, D), lambda i: (0, 0))],
        out_shape=[jax.ShapeDtypeStruct((rows, D), F32), jax.ShapeDtypeStruct((8, D), F32)],
        scratch_shapes=[pltpu.VMEM((D, 2 * FFN), BF16), pltpu.SemaphoreType.DMA],
        compiler_params=_params(("arbitrary",)),
    )(dg, dup, dh2, h1, rstd2, w_ffn_in, ffn_norm_w)
    return dg, dup, dh1, stats


def _mix_bwd(dh1, o, proj, h_lru, y_ret, y_lru, w_br, w_bl, w_o, riders=()):
    rows = dh1.shape[0]
    tm = TM_MIX_BWD
    nt = rows // tm

    def body(dh1_ref, o_ref, gret_ref, lgate_ref, ga_ref, gb_ref, hl_ref, yret_ref, ylru_ref, wbr_ref, wbl_ref, wo_ref,
             dproj_ref, do_ref, dhl_ref, mixed_ref, aret_ref, alru_ref, dyret_ref, dylru_ref):
        dmixed = _dot_nt(dh1_ref[...].astype(BF16), wo_ref[...])
        y_ret, y_lru = yret_ref[...].astype(F32), ylru_ref[...].astype(F32)
        sa, sb = _sigmoid(ga_ref[...].astype(F32)), _sigmoid(gb_ref[...].astype(F32))
        mixed_ref[...] = (sa * y_ret + sb * y_lru).astype(BF16)
        dga = dmixed * y_ret * sa * (1.0 - sa)
        dgb = dmixed * y_lru * sb * (1.0 - sb)
        dy_ret = (dmixed * sa).astype(BF16)
        dy_lru = (dmixed * sb).astype(BF16)
        dyret_ref[...] = dy_ret
        dylru_ref[...] = dy_lru
        da_ret = _dot_nt(dy_ret, wbr_ref[...])
        da_lru = _dot_nt(dy_lru, wbl_ref[...])

        gret = gret_ref[...].astype(F32)
        sg = _sigmoid(gret)
        silu = gret * sg
        on, rstds = _group_norm(o_ref[...].astype(F32))
        aret_ref[...] = (silu * on).astype(BF16)
        dgret = da_ret * on * (sg * (1.0 + gret * (1.0 - sg)))
        don = da_ret * silu
        for h in range(HEADS):
            sl = slice(h * DH, (h + 1) * DH)
            onh, donh = on[:, sl], don[:, sl]
            do_ref[:, sl] = (rstds[h] * (donh - onh * jnp.mean(donh * onh, axis=-1, keepdims=True))).astype(BF16)

        gl, gl_grad = _gelu_and_grad(lgate_ref[...].astype(F32))
        hl = hl_ref[...].astype(F32)
        alru_ref[...] = (gl * hl).astype(BF16)
        dlgate = da_lru * hl * gl_grad
        dhl_ref[...] = (da_lru * gl).astype(BF16)

        zeros = jnp.zeros((tm, D), BF16)
        for col in (0, 1, 2, 4):
            dproj_ref[:, col * D:(col + 1) * D] = zeros
        dproj_ref[:, 3 * D:4 * D] = dgret.astype(BF16)
        dproj_ref[:, 5 * D:6 * D] = dlgate.astype(BF16)
        dproj_ref[:, 6 * D:7 * D] = dga.astype(BF16)
        dproj_ref[:, 7 * D:8 * D] = dgb.astype(BF16)

    tile = lambda col: pl.BlockSpec((tm, D), lambda i: (i, col))
    wspec = pl.BlockSpec((D, D), lambda i: (0, 0))
    bf = lambda: jax.ShapeDtypeStruct((rows, D), BF16)
    return _hosted_call(
        body,
        name="mix_bwd",
        grid=(nt,),
        in_specs=[tile(0), tile(0), tile(3), tile(5), tile(6), tile(7), tile(0), tile(0), tile(0), wspec, wspec, wspec],
        out_specs=[pl.BlockSpec((tm, IN_COLS), lambda i: (i, 0))] + [tile(0)] * 7,
        out_shape=[jax.ShapeDtypeStruct((rows, IN_COLS), BF16)] + [bf() for _ in range(7)],
        scratch_shapes=[],
        args=(dh1, o, proj, proj, proj, proj, h_lru, y_ret, y_lru, w_br, w_bl, w_o),
        riders=riders,
    )


def _retention_bwd(dproj, proj_bf, do, sprev, intra, intra_t, q_dec, k_dec, c_dec, cos_t, sin_t, riders=()):
    rows = proj_bf.shape[0]
    nc = rows // CHUNK

    def body(dproj_in_ref, q_ref, k_ref, v_ref, do_ref, sprev_ref, m_ref, mt_ref, qd_ref, kd_ref, cd_ref, cos_ref, sin_ref,
             dproj_ref, ds_sc):
        @pl.when(pl.program_id(0) == 0)
        def _():
            ds_sc[...] = jnp.zeros_like(ds_sc)

        cos, sin = cos_ref[...], sin_ref[...]

        def unrotate(dy):
            return dy * cos - pltpu.roll(dy, DH // 2, axis=1) * sin

        for h in range(HEADS):
            sl = slice(h * DH, (h + 1) * DH)
            q, k, v = q_ref[:, sl], k_ref[:, sl], v_ref[:, sl]
            do = do_ref[:, sl]
            dob = do.astype(BF16)
            doq = (do * qd_ref[:, sl]).astype(BF16)
            state_prev = sprev_ref[0, h]
            dstate = ds_sc[h]
            dstate_b = dstate.astype(BF16)
            s_t = (_dot_nt(k, q) * mt_ref[h]).astype(BF16)
            ds_t = (_dot_nt(v, dob) * mt_ref[h]).astype(BF16)
            ds = (_dot_nt(dob, v) * m_ref[h]).astype(BF16)
            kd = kd_ref[:, sl]
            dq = _dot(ds, k) + _dot_nt(doq, state_prev)
            dk = _dot(ds_t, q) + _dot_nt(v, dstate_b) * kd
            k_scaled = (k.astype(F32) * kd).astype(BF16)
            dv = _dot(s_t, dob) + _dot(k_scaled, dstate_b)
            ds_sc[h] = dstate * cd_ref[:, sl] + _dot_tn(q, doq)
            dproj_ref[:, sl] = unrotate(dq).astype(BF16)
            dproj_ref[:, D + h * DH:D + (h + 1) * DH] = (unrotate(dk) * QK_SCALE).astype(BF16)
            dproj_ref[:, 2 * D + h * DH:2 * D + (h + 1) * DH] = dv.astype(BF16)

    rev = lambda c: nc - 1 - c
    chunk_spec = lambda col: pl.BlockSpec((CHUNK, D), lambda c: (rev(c), col))
    const2 = lambda shape: pl.BlockSpec(shape, lambda c: (0, 0))
    const3 = pl.BlockSpec((HEADS, CHUNK, CHUNK), lambda c: (0, 0, 0))
    return _hosted_call(
        body,
        name="retention_bwd",
        grid=(nc,),
        in_specs=[
            pl.BlockSpec(memory_space=pl.ANY),
            chunk_spec(0), chunk_spec(1), chunk_spec(2), chunk_spec(0),
            pl.BlockSpec((1, HEADS, DH, DH), lambda c: (rev(c), 0, 0, 0)),
            const3, const3,
            const2((CHUNK, D)), const2((CHUNK, D)), const2((1, D)),
            pl.BlockSpec((CHUNK, DH), lambda c: (rev(c), 0)),
            pl.BlockSpec((CHUNK, DH), lambda c: (rev(c), 0)),
        ],
        out_specs=[pl.BlockSpec((CHUNK, 3 * D), lambda c: (rev(c), 0))],
        out_shape=[jax.ShapeDtypeStruct(dproj.shape, BF16)],
        aliases={0: 0},
        scratch_shapes=[pltpu.VMEM((HEADS, DH, DH), F32)],
        args=(dproj, proj_bf, proj_bf, proj_bf, do, sprev, intra, intra_t, q_dec, k_dec, c_dec, cos_t, sin_t),
        riders=riders,
    )


def _lru_bwd(dproj, proj, saved, dhl, conv_w, wa, wx, lam, riders=()):
    rows = proj.shape[0]
    nt = rows // TM
    per8 = TM // 8

    def body(dproj_in_ref, x_ref, xprev_ref, h_ref, hprev_ref, c_ref, r_ref, i_ref, la_ref, mult_ref, dhl_ref,
             cw_ref, wa_ref, wx_ref, lam_ref,
             dproj_ref, dwa_ref, dwx_ref, stats_ref, anext_sc, dhnext_sc, dcnext_sc, c_sc, b_sc, dh_sc):
        step = pl.program_id(0)
        i = nt - 1 - step

        @pl.when(step == 0)
        def _():
            anext_sc[...] = jnp.zeros_like(anext_sc)
            dhnext_sc[...] = jnp.zeros_like(dhnext_sc)
            dcnext_sc[...] = jnp.zeros_like(dcnext_sc)
            dwa_ref[...] = jnp.zeros_like(dwa_ref)
            dwx_ref[...] = jnp.zeros_like(dwx_ref)
            stats_ref[...] = jnp.zeros_like(stats_ref)

        first = i == 0
        x = x_ref[...].astype(F32)
        prev8 = jnp.where(first, 0.0, xprev_ref[8:16, :].astype(F32))
        c_b = c_ref[...]
        c, r, gate_i, mult = (ref[...].astype(F32) for ref in (c_ref, r_ref, i_ref, mult_ref))
        a = jnp.exp(la_ref[...].astype(F32))
        sp = _softplus(-lam_ref[...])
        dh_sc[...] = dhl_ref[...].astype(F32)

        b_sc[...] = pltpu.roll(a, TM - 1, axis=0)
        b_sc[TM - 1:TM, :] = anext_sc[0:1, :]
        anext_sc[0:1, :] = a[0:1, :]
        row8 = lax.broadcasted_iota(jnp.int32, (8, D), 0)

        def group(gi, dhnext):
            r0 = pl.multiple_of((per8 - 1 - gi) * 8, 8)
            bb = b_sc[pl.ds(r0, 8), :]
            uu = dh_sc[pl.ds(r0, 8), :]
            for d in (1, 2, 4):
                b_sh = jnp.where(row8 < 8 - d, pltpu.roll(bb, 8 - d, axis=0), 1.0)
                u_sh = jnp.where(row8 < 8 - d, pltpu.roll(uu, 8 - d, axis=0), 0.0)
                uu = uu + bb * u_sh
                bb = bb * b_sh
            dhb = bb * dhnext + uu
            dh_sc[pl.ds(r0, 8), :] = dhb
            return dhb[0:1, :]

        dhfirst = lax.fori_loop(0, per8, group, dhnext_sc[0:1, :])
        dhnext_sc[0:1, :] = dhfirst
        dh = dh_sc[...]

        h = h_ref[...].astype(F32)
        hprev8 = jnp.where(first, 0.0, hprev_ref[8:16, :].astype(F32))
        h_dn, h_head = _shift_down(h, hprev8, 1)
        c_sc[...] = h_dn
        c_sc[0:8, :] = h_head
        h_before = c_sc[...]

        valid = _row_ids(i, TM, (TM, D)) >= PAD_ROWS
        da = dh * h_before
        du = jnp.where(valid, dh, 0.0)
        dmult = du * gate_i * c
        dgate_i = du * mult * c
        dc = du * mult * gate_i
        dla = da * a - dmult * (a * a) / mult
        dla = jnp.where(valid, dla, 0.0)
        dr = dla * ((-LRU_C) * sp)
        dzr = dr * r * (1.0 - r)
        dzi = dgate_i * gate_i * (1.0 - gate_i)
        stats_ref[1:2, :] += jnp.sum(dzr, axis=0, keepdims=True)
        stats_ref[2:3, :] += jnp.sum(dzi, axis=0, keepdims=True)
        stats_ref[3:4, :] += jnp.sum(dla * ((-LRU_C) * r), axis=0, keepdims=True)
        dc_gate = []
        for g in range(LRU_BLOCKS):
            sl = slice(g * LRU_BLOCK, (g + 1) * LRU_BLOCK)
            cg = c_b[:, sl]
            dzr_g = dzr[:, sl].astype(BF16)
            dzi_g = dzi[:, sl].astype(BF16)
            dc_gate.append(_dot_nt(dzr_g, wa_ref[g]) + _dot_nt(dzi_g, wx_ref[g]))
            dwa_ref[g] += _dot_tn(cg, dzr_g)
            dwx_ref[g] += _dot_tn(cg, dzi_g)
        dc = dc + jnp.concatenate(dc_gate, axis=1)

        cw = cw_ref[...]
        stats_ref[0:1, :] += jnp.sum(dc, axis=0, keepdims=True)
        stats_ref[7:8, :] += jnp.sum(dc * x, axis=0, keepdims=True)
        dx = cw[3:4] * dc
        tail_src = jnp.concatenate([dc[TM - 8:TM], dcnext_sc[...]], axis=0)
        dx_tail = cw[3:4] * dc[TM - 8:TM]
        for d in (1, 2, 3):
            dx = dx + cw[3 - d:4 - d] * pltpu.roll(dc, TM - d, axis=0)
            dx_tail = dx_tail + cw[3 - d:4 - d] * pltpu.roll(tail_src, 16 - d, axis=0)[0:8]
            rolled, hd = _shift_down(x, prev8, d)
            b_sc[...] = rolled
            b_sc[0:8, :] = hd
            stats_ref[7 - d:8 - d, :] += jnp.sum(dc * b_sc[...], axis=0, keepdims=True)
        dcnext_sc[...] = dc[0:8]
        dproj_ref[...] = dx.astype(BF16)
        dproj_ref[TM - 8:TM, :] = dx_tail.astype(BF16)

    rev = lambda s: nt - 1 - s
    vec = pl.BlockSpec((1, D), lambda s: (0, 0))
    wspec = pl.BlockSpec((LRU_BLOCKS, LRU_BLOCK, LRU_BLOCK), lambda s: (0, 0, 0))
    prev16 = lambda col: pl.BlockSpec((16, D), lambda s: (jnp.maximum(rev(s) * (TM // 16) - 1, 0), col))
    tile = lambda: pl.BlockSpec((TM, D), lambda s: (rev(s), 0))
    h_lru, c_sv, r_sv, i_sv, la_sv, mult_sv = saved
    return _hosted_call(
        body,
        name="lru_bwd",
        grid=(nt,),
        in_specs=[
            pl.BlockSpec(memory_space=pl.ANY),
            pl.BlockSpec((TM, D), lambda s: (rev(s), 4)), prev16(4),
            tile(), prev16(0),
            tile(), tile(), tile(), tile(), tile(), tile(),
            pl.BlockSpec((4, D), lambda s: (0, 0)),
            wspec, wspec, vec,
        ],
        out_specs=[
            pl.BlockSpec((TM, D), lambda s: (rev(s), 4)),
            wspec, wspec,
            pl.BlockSpec((8, D), lambda s: (0, 0)),
        ],
        out_shape=[
            jax.ShapeDtypeStruct(dproj.shape, BF16),
            jax.ShapeDtypeStruct((LRU_BLOCKS, LRU_BLOCK, LRU_BLOCK), F32),
            jax.ShapeDtypeStruct((LRU_BLOCKS, LRU_BLOCK, LRU_BLOCK), F32),
            jax.ShapeDtypeStruct((8, D), F32),
        ],
        aliases={0: 0},
        scratch_shapes=[
            pltpu.VMEM((8, D), F32), pltpu.VMEM((8, D), F32), pltpu.VMEM((8, D), F32),
            pltpu.VMEM((TM, D), F32), pltpu.VMEM((TM, D), F32), pltpu.VMEM((TM, D), F32),
        ],
        args=(dproj, proj, proj, h_lru, h_lru, c_sv, r_sv, i_sv, la_sv, mult_sv, dhl, conv_w, wa, wx, lam),
        riders=riders,
    )


def _in_proj_bwd(dproj, w_in, part, prev=None, riders=()):
    rows = dproj.shape[0]
    tm = _heavy_tile(rows)
    nt = rows // tm
    first = 0 if part == 0 else (nt + 1) // 2
    count = (nt + 1) // 2 if part == 0 else nt - first

    def body(*refs):
        dproj_ref, w_hbm, du_ref, w_sc, w_sem = refs[-5:]

        @pl.when(pl.program_id(0) == 0)
        def _():
            cp = pltpu.make_async_copy(w_hbm, w_sc, w_sem)
            cp.start()
            cp.wait()

        du_ref[...] = _dot_nt(dproj_ref[...], w_sc[...])

    in_specs = [pl.BlockSpec((tm, IN_COLS), lambda i: (first + i, 0)), _ANY]
    args = (dproj, w_in)
    if prev is not None:
        in_specs = [_ANY] + in_specs
        args = (prev,) + args
    return _hosted_call(
        body,
        name="in_proj_bwd_%d" % part,
        grid=(count,),
        in_specs=in_specs,
        out_specs=[pl.BlockSpec((tm, D), lambda i: (first + i, 0))],
        out_shape=[jax.ShapeDtypeStruct((rows, D), F32)],
        scratch_shapes=[pltpu.VMEM((D, IN_COLS), BF16), pltpu.SemaphoreType.DMA],
        aliases={0: 0} if prev is not None else None,
        args=args,
        riders=riders,
    )


def _norm1_bwd(du, dh1, head, x2d, rstd1, norm_w, riders=()):
    rows = du.shape[0]

    def body(du_ref, dh1_ref, head_ref, x_ref, rstd_ref, nw_ref, gx_ref, ghead_ref, stats_ref):
        i = pl.program_id(0)

        @pl.when(i == 0)
        def _():
            stats_ref[...] = jnp.zeros_like(stats_ref)

        def finish(h0, out_ref):
            du = du_ref[...]
            rs = rstd_ref[...]
            n = h0 * rs
            stats_ref[0:1, :] += jnp.sum(du * n, axis=0, keepdims=True)
            dn = du * nw_ref[...]
            out_ref[...] = dh1_ref[...] + rs * (dn - n * jnp.mean(dn * n, axis=-1, keepdims=True))

        @pl.when(i == 0)
        def _():
            finish(head_ref[...], ghead_ref)

        @pl.when(i > 0)
        def _():
            finish(x_ref[...], gx_ref)

    tile = pl.BlockSpec((TM, D), lambda i: (i, 0))
    return _hosted_call(
        body,
        name="norm1_bwd",
        grid=(rows // TM,),
        in_specs=[
            tile, tile,
            pl.BlockSpec((FRONT, D), lambda i: (0, 0)),
            pl.BlockSpec((TM, D), lambda i: (jnp.maximum(i - 1, 0), 0)),
            pl.BlockSpec((TM, 1), lambda i: (i, 0)),
            pl.BlockSpec((1, D), lambda i: (0, 0)),
        ],
        out_specs=[
            pl.BlockSpec((TM, D), lambda i: (jnp.maximum(i - 1, 0), 0)),
            pl.BlockSpec((FRONT, D), lambda i: (0, 0)),
            pl.BlockSpec((8, D), lambda i: (0, 0)),
        ],
        out_shape=[
            jax.ShapeDtypeStruct(x2d.shape, F32),
            jax.ShapeDtypeStruct((FRONT, D), F32),
            jax.ShapeDtypeStruct((8, D), F32),
        ],
        scratch_shapes=[],
        args=(du, dh1, head, x2d, rstd1, norm_w),
        riders=riders,
    )


def _matmul_tn(name, x, dy, out_cols, col0=0, prev=None, k_block=None, n_block=None, riders=()):
    rows, kdim = x.shape
    ndim = dy.shape[1]
    kb = k_block or kdim
    nb = n_block or ndim
    step_bytes = lambda t: 2 * t * (kb * x.dtype.itemsize + nb * dy.dtype.itemsize) + 2 * kb * nb * 4
    tr = next(t for t in (2816, 1408, TM_HEAVY, TM) if rows % t == 0 and (t == TM or step_bytes(t) <= TN_VMEM_BUDGET))
    nr, nk, nn = rows // tr, kdim // kb, ndim // nb
    cb0 = col0 // nb

    def body(*refs):
        x_ref, dy_ref, out_ref = refs[-3], refs[-2], refs[-1]
        part = _dot_tn(x_ref[...].astype(BF16), dy_ref[...].astype(BF16))

        @pl.when(pl.program_id(2) == 0)
        def _():
            out_ref[...] = part

        @pl.when(pl.program_id(2) > 0)
        def _():
            out_ref[...] += part

    in_specs = [
        pl.BlockSpec((tr, kb), lambda n, k, r: (r, k)),
        pl.BlockSpec((tr, nb), lambda n, k, r: (r, n)),
    ]
    args = [x, dy]
    aliases = {}
    if prev is not None:
        in_specs = [pl.BlockSpec(memory_space=pl.ANY)] + in_specs
        args = [prev] + args
        aliases = {0: 0}
    (out,), rider_outs = _hosted_call(
        body,
        name=name,
        grid=(nn, nk, nr),
        in_specs=in_specs,
        out_specs=[pl.BlockSpec((kb, nb), lambda n, k, r: (k, cb0 + n))],
        out_shape=[jax.ShapeDtypeStruct((kdim, out_cols), F32)],
        scratch_shapes=[],
        aliases=aliases,
        args=args,
        riders=riders,
    )
    return (out, rider_outs) if riders else out


def _local_step(x2d, target, w, plan):
    rows = FRONT + x2d.shape[0]
    head = jnp.concatenate([jnp.zeros((PAD_ROWS, D), F32), w["meta_tokens"]], axis=0)
    cos_t, sin_t = _rope_tables(rows)
    intra, intra_t, q_dec, k_dec, c_dec = _decay_tables()
    grads = {}

    def hosted(host, fn, *args, **kwargs):
        outs, rider_outs = fn(*args, riders=plan.riders(host, w, grads), **kwargs)
        plan.after(host, rider_outs, w, grads)
        return outs

    (u1, rstd1), _ = _norm1(head, x2d, w["mix_norm_w"])
    proj, w["w_in"] = hosted("in_proj", _in_proj, u1, w["w_in_shard"], w["route"], cos_t, sin_t)
    o, sprev = hosted("retention_fwd", _retention_fwd, proj, intra, q_dec, k_dec, c_dec)
    lru_args = (w["conv_w"], w["conv_b"], w["lru_wa"], w["lru_wx"], w["lru_ba"], w["lru_bx"], w["lru_lambda"])
    lru_saved = hosted("lru_fwd", _lru_fwd, proj, *lru_args)
    h_lru = lru_saved[0]
    y_ret, y_lru, h1, u2, rstd2 = hosted("mix_fwd", _mix_fwd, head, x2d, o, proj, h_lru, w["w_branch_ret"],
                                         w["w_branch_lru"], w["w_out"], w["ffn_norm_w"])
    silu, dsilu, act = hosted("ffn_fwd_gate", _ffn_fwd_gate, u2, w["w_ffn_in"])
    dh2, stats_loss = _ffn_out_loss(act, h1, w["w_ffn_out"], target, w["final_norm_w"])

    dg, dup, dh1, stats_ffn = _ffn_bwd(dh2, silu, dsilu, h1, rstd2, w["w_ffn_in"], w["w_ffn_out"], w["ffn_norm_w"])
    grads["w_ffn_in"] = _matmul_tn("dw_ffn_up", u2, dup, 2 * FFN, col0=FFN, n_block=FFN_HALF,
                                   prev=_matmul_tn("dw_ffn_gate", u2, dg, 2 * FFN, n_block=FFN_HALF))
    grads["w_ffn_out"] = _matmul_tn("dw_ffn_out", act, dh2, D, k_block=FFN_HALF)
    (dproj, do, dhl, mixed, a_ret, a_lru, dy_ret, dy_lru) = hosted(
        "mix_bwd", _mix_bwd, dh1, o, proj, h_lru, y_ret, y_lru, w["w_branch_ret"], w["w_branch_lru"], w["w_out"])
    grads["w_out"] = _matmul_tn("dw_out", mixed, dh1, D)
    grads["w_branch_ret"] = _matmul_tn("dw_branch_ret", a_ret, dy_ret, D)
    grads["w_branch_lru"] = _matmul_tn("dw_branch_lru", a_lru, dy_lru, D)
    (dproj,) = hosted("retention_bwd", _retention_bwd, dproj, proj, do, sprev, intra, intra_t, q_dec, k_dec, c_dec,
                      cos_t, sin_t)
    dproj, grads["lru_wa"], grads["lru_wx"], stats_lru = hosted(
        "lru_bwd", _lru_bwd, dproj, proj, lru_saved, dhl, w["conv_w"], w["lru_wa"], w["lru_wx"], w["lru_lambda"])
    grads["w_in"], rider_outs = _matmul_tn("dw_in", u1, dproj, IN_COLS, n_block=D, riders=plan.riders("dw_in", w, grads))
    plan.after("dw_in", rider_outs, w, grads)
    (du1,) = hosted("in_proj_bwd_0", _in_proj_bwd, dproj, w["w_in"], 0)
    (du1,) = hosted("in_proj_bwd_1", _in_proj_bwd, dproj, w["w_in"], 1, du1)
    (grad_x, grad_head, stats_in), _ = _norm1_bwd(du1, dh1, head, x2d, rstd1, w["mix_norm_w"])
    return grad_x, grad_head, grads, [stats_loss, stats_ffn, stats_in, stats_lru]


BIG_PIECES = {
    "w_in": ("col", (D, 2 * D)),
    "w_ffn_in": ("col", (D, FFN_HALF)),
    "w_ffn_out": ("row", (FFN // 4, D)),
    "w_branch_ret": ("row", (D // 4, D)),
    "w_branch_lru": ("row", (D // 4, D)),
    "w_out": ("row", (D // 4, D)),
    "lru_wa": ("lru", (LRU_BLOCKS, LRU_BLOCK // 4, LRU_BLOCK)),
    "lru_wx": ("lru", (LRU_BLOCKS, LRU_BLOCK // 4, LRU_BLOCK)),
}
SMALL_PIECES = {"meta_tokens": ("col", (N_META, D // 4)), "conv_w": ("col", (4, D // 4))}


def _full_shape(kind, shard):
    if kind == "col":
        return (shard[0], 4 * shard[1])
    if kind == "row":
        return (4 * shard[0], shard[1])
    return (shard[0], 4 * shard[1], shard[2])


def _half_shape(kind, shard):
    return (shard[0] // 2,) + tuple(shard[1:])


def _aligned(start, multiple):
    return start if isinstance(start, int) else pl.multiple_of(start, multiple)


def _lead(h, size):
    if h is None:
        return pl.ds(0, size)
    return pl.ds(_aligned(h * (size // 2), size // 2), size // 2)


def _full_region(ref, kind, shard, s, h):
    if kind == "col":
        return ref.at[_lead(h, shard[0]), pl.ds(_aligned(s * shard[1], shard[1]), shard[1])]
    if kind == "row":
        size = shard[0] if h is None else shard[0] // 2
        start = s * shard[0] + (0 if h is None else h * (shard[0] // 2))
        return ref.at[pl.ds(_aligned(start, 16), size), :]
    return ref.at[_lead(h, shard[0]), pl.ds(_aligned(s * shard[1], shard[1]), shard[1]), :]


def _shard_region(ref, shard, h):
    return ref.at[_lead(h, shard[0])]


def _place():
    x, y, c = lax.axis_index("x"), lax.axis_index("y"), lax.axis_index("c")
    return x, y, c, 2 * x + y


def _other_chip(s, c, k):
    s2 = jnp.bitwise_xor(s, k)
    return s2, (s2 // 2, s2 % 2, c)


def _remote(src, dst, send_sem, recv_sem, dev):
    return pltpu.make_async_remote_copy(src_ref=src, dst_ref=dst, send_sem=send_sem, recv_sem=recv_sem,
                                        device_id=dev, device_id_type=MESH)


_ANY = pl.BlockSpec(memory_space=pl.ANY)


class _Rider:
    def __init__(self, ins, out_shapes, sem_shapes, build, aliased=False):
        self.ins, self.out_shapes, self.sem_shapes, self.build, self.aliased = ins, out_shapes, sem_shapes, build, aliased


def _hosted_call(body, *, name, grid, in_specs, out_specs, out_shape, scratch_shapes, args, riders=(), aliases=None,
                 prefetch=None, riders_after_body=False):
    n_in, n_out, n_sc = len(in_specs), len(out_shape), len(scratch_shapes)
    r_in = [a for r in riders for a in r.ins]
    r_out = [s for r in riders for s in r.out_shapes]
    r_sem = [s for r in riders for s in r.sem_shapes]
    lead = () if prefetch is None else (prefetch,)
    assert prefetch is None or not (aliases or any(r.aliased for r in riders))

    def full_body(*refs):
        head, refs = refs[:len(lead)], refs[len(lead):]
        ins, rin = refs[:n_in], refs[n_in:n_in + len(r_in)]
        o0 = n_in + len(r_in)
        outs, rout = refs[o0:o0 + n_out], refs[o0 + n_out:o0 + n_out + len(r_out)]
        s0 = o0 + n_out + len(r_out)
        scratch, rsem = refs[s0:s0 + n_sc], refs[s0 + n_sc:]
        starts, waits = [], []
        pi = po = ps = 0
        for r in riders:
            st, wt = r.build(rin[pi:pi + len(r.ins)], rout[po:po + len(r.out_shapes)], rsem[ps:ps + len(r.sem_shapes)])
            starts += st
            waits += wt
            pi, po, ps = pi + len(r.ins), po + len(r.out_shapes), ps + len(r.sem_shapes)
        first = functools.reduce(jnp.logical_and, [pl.program_id(d) == 0 for d in range(len(grid))])
        last = functools.reduce(jnp.logical_and, [pl.program_id(d) == grid[d] - 1 for d in range(len(grid))])

        def start_riders():
            @pl.when(first)
            def _():
                for cp in starts:
                    cp.start()

        if riders and not riders_after_body:
            start_riders()
        body(*head, *ins, *outs, *scratch)
        if riders and riders_after_body:
            start_riders()
        if riders:
            @pl.when(last)
            def _():
                for wait in waits:
                    wait()

    io_aliases = dict(aliases or {})
    pi = po = 0
    for r in riders:
        if r.aliased:
            for q in range(len(r.ins)):
                io_aliases[n_in + pi + q] = n_out + po + q
        pi, po = pi + len(r.ins), po + len(r.out_shapes)
    specs = dict(
        grid=grid,
        in_specs=list(in_specs) + [_ANY] * len(r_in),
        out_specs=list(out_specs) + [_ANY] * len(r_out),
        scratch_shapes=list(scratch_shapes) + r_sem,
    )
    if prefetch is not None:
        specs = dict(grid_spec=pltpu.PrefetchScalarGridSpec(num_scalar_prefetch=1, **specs))
    res = pl.pallas_call(
        full_body,
        name=name,
        out_shape=list(out_shape) + r_out,
        input_output_aliases=io_aliases,
        compiler_params=_params(("arbitrary",) * len(grid)),
        **specs,
    )(*lead, *args, *r_in)
    rider_outs, po = [], n_out
    for r in riders:
        rider_outs.append(list(res[po:po + len(r.out_shapes)]))
        po += len(r.out_shapes)
    return list(res[:n_out]), rider_outs


def _run_riders(name, riders):
    r_in = [a for r in riders for a in r.ins]
    r_out = [s for r in riders for s in r.out_shapes]
    r_sem = [s for r in riders for s in r.sem_shapes]

    def body(*refs):
        rin, rout, rsem = refs[:len(r_in)], refs[len(r_in):len(r_in) + len(r_out)], refs[len(r_in) + len(r_out):]
        pi = po = ps = 0
        for r in riders:
            starts, waits = r.build(rin[pi:pi + len(r.ins)], rout[po:po + len(r.out_shapes)], rsem[ps:ps + len(r.sem_shapes)])
            for cp in starts:
                cp.start()
            for wait in waits:
                wait()
            pi, po, ps = pi + len(r.ins), po + len(r.out_shapes), ps + len(r.sem_shapes)

    io_aliases = {}
    pi = po = 0
    for r in riders:
        if r.aliased:
            for q in range(len(r.ins)):
                io_aliases[pi + q] = po + q
        pi, po = pi + len(r.ins), po + len(r.out_shapes)
    res = pl.pallas_call(
        body,
        name=name,
        in_specs=[_ANY] * len(r_in),
        out_specs=[_ANY] * len(r_out),
        out_shape=r_out,
        scratch_shapes=r_sem,
        input_output_aliases=io_aliases,
    )(*r_in)
    outs, po = [], 0
    for r in riders:
        outs.append(list(res[po:po + len(r.out_shapes)]))
        po += len(r.out_shapes)
    return outs


def _piece(name):
    if name in BIG_PIECES:
        return (name, *BIG_PIECES[name], True)
    return (name, *SMALL_PIECES[name], False)


def _gather_rider(shards, names):
    pieces = [_piece(n) for n in names]
    n = len(pieces)

    def build(ins, outs, sems):
        local_sem, ici_send, ici_recv = sems
        _, _, c, s = _place()
        starts, waits = [], []
        for p, (_, kind, shard, split) in enumerate(pieces):
            cp = pltpu.make_async_copy(ins[p], _full_region(outs[p], kind, shard, s, None), local_sem.at[p])
            starts.append(cp)
            waits.append(cp.wait)
            h = c if split else None
            for k in (1, 2, 3):
                s2, dev = _other_chip(s, c, k)
                cp = _remote(_shard_region(ins[p], shard, h), _full_region(outs[p], kind, shard, s, h),
                             ici_send.at[p, k - 1], ici_recv.at[p, k - 1], dev)
                starts.append(cp)
                waits.append(cp.wait_send)
                region = _full_region(outs[p], kind, shard, s2, h)
                waits.append(_remote(region, region, ici_send.at[p, k - 1], ici_recv.at[p, k - 1], dev).wait_recv)
        return starts, waits

    return _Rider(
        [shards[name] for name in names],
        [jax.ShapeDtypeStruct(_full_shape(kind, shard), shards[name].dtype) for name, kind, shard, _ in pieces],
        [pltpu.SemaphoreType.DMA((n,)), pltpu.SemaphoreType.DMA((n, 3)), pltpu.SemaphoreType.DMA((n, 3))],
        build)


def _forward_rider(gathered, names):
    pieces = [_piece(n) for n in names]
    n = len(pieces)

    def build(ins, outs, sems):
        fwd_send, fwd_recv = sems
        x, y, c, s = _place()
        sibling = (x, y, 1 - c)
        starts, waits = [], []
        for p, (_, kind, shard, _) in enumerate(pieces):
            for k in (1, 2, 3):
                s2, _ = _other_chip(s, c, k)
                mine = _full_region(outs[p], kind, shard, s2, c)
                theirs = _full_region(outs[p], kind, shard, s2, 1 - c)
                cp = _remote(mine, mine, fwd_send.at[p, k - 1], fwd_recv.at[p, k - 1], sibling)
                starts.append(cp)
                waits.append(cp.wait_send)
                waits.append(_remote(theirs, theirs, fwd_send.at[p, k - 1], fwd_recv.at[p, k - 1], sibling).wait_recv)
        return starts, waits

    return _Rider(
        [gathered[name] for name in names],
        [jax.ShapeDtypeStruct(gathered[name].shape, gathered[name].dtype) for name in names],
        [pltpu.SemaphoreType.DMA((n, 3)), pltpu.SemaphoreType.DMA((n, 3))],
        build, aliased=True)


def _pair_exchange_rider(grads, names):
    n = len(names)

    def build(ins, outs, sems):
        send_sem, recv_sem = sems
        x, y, c, _ = _place()
        sibling = (x, y, 1 - c)
        starts, waits = [], []
        for p, name in enumerate(names):
            kind, shard = BIG_PIECES[name]
            for s2 in range(4):
                cp = _remote(_full_region(ins[p], kind, shard, s2, 1 - c), outs[p].at[s2], send_sem.at[p, s2],
                             recv_sem.at[p, s2], sibling)
                starts.append(cp)
                waits.append(cp.wait_send)
                waits.append(_remote(outs[p].at[s2], outs[p].at[s2], send_sem.at[p, s2], recv_sem.at[p, s2], sibling).wait_recv)
        return starts, waits

    return _Rider(
        [grads[name] for name in names],
        [jax.ShapeDtypeStruct((4,) + _half_shape(*BIG_PIECES[name]), F32) for name in names],
        [pltpu.SemaphoreType.DMA((n, 4))] * 2,
        build)


def _half_specs(kind, shard):
    half = shard[0] // 2
    if kind == "col":
        full = pl.BlockSpec((half, shard[1]), lambda j, pr: (pr[1], j))
        buf = pl.BlockSpec((None, half, shard[1]), lambda j, pr: (j, 0, 0))
    elif kind == "row":
        full = pl.BlockSpec((half, shard[1]), lambda j, pr: (2 * j + pr[1], 0))
        buf = pl.BlockSpec((None, half, shard[1]), lambda j, pr: (j, 0, 0))
    else:
        full = pl.BlockSpec((half, shard[1], shard[2]), lambda j, pr: (pr[1], j, 0))
        buf = pl.BlockSpec((None, half, shard[1], shard[2]), lambda j, pr: (j, 0, 0, 0))
    return full, buf


def _pair_sum(name, grad, recv, place):
    kind, shard = BIG_PIECES[name]
    full, buf = _half_specs(kind, shard)

    def body(pr, g_ref, r_ref, o_ref):
        o_ref[...] = (g_ref[...] + r_ref[...]).astype(BF16)

    return pl.pallas_call(
        body,
        name="pair_sum_" + name,
        grid_spec=pltpu.PrefetchScalarGridSpec(num_scalar_prefetch=1, grid=(4,), in_specs=[full, buf], out_specs=buf),
        out_shape=jax.ShapeDtypeStruct((4,) + _half_shape(kind, shard), BF16),
        compiler_params=_params(("arbitrary",)),
    )(place, grad, recv)


def _chip_exchange_rider(sums, names):
    n = len(names)

    def build(ins, outs, sems):
        send_sem, recv_sem = sems
        _, _, c, s = _place()
        starts, waits = [], []
        for p in range(n):
            for k in (1, 2, 3):
                s2, dev = _other_chip(s, c, k)
                cp = _remote(ins[p].at[s2], outs[p].at[k - 1], send_sem.at[p, k - 1], recv_sem.at[p, k - 1], dev)
                starts.append(cp)
                waits.append(cp.wait_send)
                waits.append(_remote(outs[p].at[k - 1], outs[p].at[k - 1], send_sem.at[p, k - 1], recv_sem.at[p, k - 1],
                                     dev).wait_recv)
        return starts, waits

    return _Rider(
        [sums[name] for name in names],
        [jax.ShapeDtypeStruct((3,) + _half_shape(*BIG_PIECES[name]), BF16) for name in names],
        [pltpu.SemaphoreType.DMA((n, 3))] * 2,
        build)


def _chip_sum(name, grad, recv_pair, recv_chip, place):
    kind, shard = BIG_PIECES[name]
    half = shard[0] // 2
    tail = tuple(shard[1:])
    zeros = (0,) * len(tail)
    if kind == "col":
        full = pl.BlockSpec((half,) + tail, lambda j, pr: (pr[1], pr[0]))
    elif kind == "row":
        full = pl.BlockSpec((half,) + tail, lambda j, pr: (2 * pr[0] + pr[1], 0))
    else:
        full = pl.BlockSpec((half,) + tail, lambda j, pr: (pr[1], pr[0], 0))
    pair = pl.BlockSpec((None, half) + tail, lambda j, pr: (pr[0], 0) + zeros)
    chip = pl.BlockSpec((3, half) + tail, lambda j, pr: (0, 0) + zeros)
    out = pl.BlockSpec((half,) + tail, lambda j, pr: (pr[1],) + zeros)

    def body(pr, g_ref, rp_ref, rc_ref, o_ref):
        total = g_ref[...] + rp_ref[...]
        for k in range(3):
            total = total + rc_ref[k].astype(F32)
        o_ref[...] = total

    return pl.pallas_call(
        body,
        name="chip_sum_" + name,
        grid_spec=pltpu.PrefetchScalarGridSpec(num_scalar_prefetch=1, grid=(1,), in_specs=[full, pair, chip], out_specs=out),
        out_shape=jax.ShapeDtypeStruct(shard, F32),
        compiler_params=_params(("arbitrary",)),
    )(place, grad, recv_pair, recv_chip)


def _sibling_exchange_rider(halves, names):
    n = len(names)

    def build(ins, outs, sems):
        send_sem, recv_sem = sems
        x, y, c, _ = _place()
        sibling = (x, y, 1 - c)
        starts, waits = [], []
        for p, name in enumerate(names):
            shard = BIG_PIECES[name][1]
            mine = _shard_region(outs[p], shard, c)
            theirs = _shard_region(outs[p], shard, 1 - c)
            cp = _remote(mine, mine, send_sem.at[p], recv_sem.at[p], sibling)
            starts.append(cp)
            waits.append(cp.wait_send)
            waits.append(_remote(theirs, theirs, send_sem.at[p], recv_sem.at[p], sibling).wait_recv)
        return starts, waits

    return _Rider(
        [halves[name] for name in names],
        [jax.ShapeDtypeStruct(BIG_PIECES[name][1], F32) for name in names],
        [pltpu.SemaphoreType.DMA((n,))] * 2,
        build, aliased=True)


FIRST_WEIGHTS = ["meta_tokens", "conv_w"]
WEIGHT_GROUPS = {
    "lru": ["lru_wa", "lru_wx"],
    "branch": ["w_branch_ret", "w_branch_lru", "w_out"],
    "ffn_in": ["w_ffn_in"],
    "ffn_out": ["w_ffn_out"],
}
WEIGHT_SCHEDULE = {
    "in_proj": [("gather", "lru")],
    "retention_fwd": [("forward", "lru"), ("gather", "branch")],
    "lru_fwd": [("forward", "branch"), ("gather", "ffn_in")],
    "mix_fwd": [("forward", "ffn_in"), ("gather", "ffn_out")],
    "ffn_fwd_gate": [("forward", "ffn_out")],
}
GRAD_GROUPS = {
    "ffn": ["w_ffn_in", "w_ffn_out"],
    "mixer": ["w_out", "w_branch_ret", "w_branch_lru", "lru_wa", "lru_wx"],
    "in": ["w_in"],
}
GRAD_SCHEDULE = {
    "mix_bwd": [("pair", "ffn")],
    "retention_bwd": [("chip", "ffn")],
    "lru_bwd": [("sibling", "ffn")],
    "dw_in": [("pair", "mixer")],
    "in_proj_bwd_0": [("chip", "mixer"), ("pair", "in")],
    "in_proj_bwd_1": [("sibling", "mixer"), ("chip", "in")],
}


class _CommPlan:
    def __init__(self, shards, place):
        self.shards, self.place = shards, place
        self.late = {}
        self.recv_pair, self.sums, self.recv_chip, self.halves, self.final = {}, {}, {}, {}, {}

    def _grad_rider(self, stage, group, grads):
        names = GRAD_GROUPS[group]
        if stage == "pair":
            return _pair_exchange_rider(grads, names)
        if stage == "chip":
            return _chip_exchange_rider(self.sums, names)
        return _sibling_exchange_rider(self.halves, names)

    def _grad_after(self, stage, group, outs, grads):
        names = GRAD_GROUPS[group]
        if stage == "pair":
            for n, o in zip(names, outs):
                self.recv_pair[n] = o
                self.sums[n] = _pair_sum(n, grads[n], o, self.place)
        elif stage == "chip":
            for n, o in zip(names, outs):
                self.halves[n] = _chip_sum(n, grads[n], self.recv_pair[n], o, self.place)
        else:
            self.final.update(zip(names, outs))

    def riders(self, host, w, grads):
        if host in WEIGHT_SCHEDULE:
            return [_gather_rider(self.shards, WEIGHT_GROUPS[group]) if stage == "gather"
                    else _forward_rider(self.late, WEIGHT_GROUPS[group]) for stage, group in WEIGHT_SCHEDULE[host]]
        return [self._grad_rider(stage, group, grads) for stage, group in GRAD_SCHEDULE.get(host, [])]

    def after(self, host, rider_outs, w, grads):
        for (stage, group), outs in zip(WEIGHT_SCHEDULE.get(host, []), rider_outs):
            (self.late if stage == "gather" else w).update(zip(WEIGHT_GROUPS[group], outs))
        for (stage, group), outs in zip(GRAD_SCHEDULE.get(host, []), rider_outs):
            self._grad_after(stage, group, outs, grads)

    def finish(self):
        (outs,) = _run_riders("grad_tail_exchange", [_sibling_exchange_rider(self.halves, GRAD_GROUPS["in"])])
        self.final.update(zip(GRAD_GROUPS["in"], outs))
        return self.final


def _adamw_math(w, g, m, v):
    m = ADAM_B1 * m + (1.0 - ADAM_B1) * g
    v = ADAM_B2 * v + (1.0 - ADAM_B2) * (g * g)
    m_hat = m / (1.0 - ADAM_B1 ** ADAM_STEP)
    v_hat = v / (1.0 - ADAM_B2 ** ADAM_STEP)
    delta = -ADAM_LR * (m_hat / (jnp.sqrt(v_hat) + ADAM_EPS) + ADAM_WD * w)
    return delta, m, v


def _adamw(name, g, w, m, v):
    rows, cols = g.shape
    tr = rows // 4 if rows % 32 == 0 else rows

    def body(g_ref, w_ref, m_ref, v_ref, go_ref, d_ref, mo_ref, vo_ref):
        gv = g_ref[...]
        delta, m2, v2 = _adamw_math(w_ref[...], gv, m_ref[...], v_ref[...])
        go_ref[...] = gv
        d_ref[...] = delta
        mo_ref[...] = m2
        vo_ref[...] = v2

    spec = pl.BlockSpec((tr, cols), lambda i: (i, 0))
    return pl.pallas_call(
        body,
        name="adamw_" + name,
        grid=(rows // tr,),
        in_specs=[spec] * 4,
        out_specs=[spec] * 4,
        out_shape=[jax.ShapeDtypeStruct((rows, cols), F32)] * 4,
        compiler_params=_params(("arbitrary",)),
    )(g, w, m, v)


SMALL_ROWS = 48
VEC_ROWS = {"final_norm_w": 1, "ffn_norm_w": 8, "mix_norm_w": 16, "conv_b": 24, "lru_ba": 25, "lru_bx": 26, "lru_lambda": 27}
VEC_NAMES = list(VEC_ROWS)
CONV_W_ROW = 28
META_ROW = 32


def _small_all_reduce(partial):
    def body(in_ref, out_ref, buf, local_sem, send_sem, recv_sem):
        x, y, c, s = _place()
        me = 2 * s + c
        cp = pltpu.make_async_copy(in_ref, buf.at[me], local_sem)
        cp.start()
        started = []
        for k in range(1, 8):
            peer = jnp.bitwise_xor(me, k)
            dev = (peer // 4, (peer // 2) % 2, peer % 2)
            rd = _remote(in_ref, buf.at[me], send_sem.at[k - 1], recv_sem.at[k - 1], dev)
            rd.start()
            started.append(rd)
        for k in range(1, 8):
            peer = jnp.bitwise_xor(me, k)
            dev = (peer // 4, (peer // 2) % 2, peer % 2)
            _remote(in_ref, buf.at[peer], send_sem.at[k - 1], recv_sem.at[k - 1], dev).wait_recv()
        for rd in started:
            rd.wait_send()
        cp.wait()
        total = buf[0]
        for d in range(1, 8):
            total = total + buf[d]
        out_ref[...] = total

    return pl.pallas_call(
        body,
        name="small_all_reduce",
        in_specs=[pl.BlockSpec(memory_space=pltpu.VMEM)],
        out_specs=pl.BlockSpec(memory_space=pltpu.VMEM),
        out_shape=jax.ShapeDtypeStruct((SMALL_ROWS, D), F32),
        scratch_shapes=[pltpu.VMEM((8, SMALL_ROWS, D), F32), pltpu.SemaphoreType.DMA,
                        pltpu.SemaphoreType.DMA((7,)), pltpu.SemaphoreType.DMA((7,))],
    )(partial)


def _small_update(total, place, vecs, conv, meta):
    nvec = len(VEC_NAMES)
    qcols = D // 4

    def body(pr, tot_ref, col_ref, *refs):
        vec_refs = refs[:3 * nvec]
        conv_refs = refs[3 * nvec:3 * nvec + 3]
        meta_refs = refs[3 * nvec + 3:3 * nvec + 6]
        outs = refs[3 * nvec + 6:]
        loss_ref, vec_out, conv_out, meta_out = outs[0], outs[1:5], outs[5:9], outs[9:13]
        loss_ref[...] = jnp.sum(tot_ref[0:1, :], axis=1, keepdims=True)
        for o in vec_out:
            o[...] = jnp.zeros_like(o)
        for j, name in enumerate(VEC_NAMES):
            w, m, v = (r[...] for r in vec_refs[3 * j:3 * j + 3])
            g = tot_ref[VEC_ROWS[name]:VEC_ROWS[name] + 1, :]
            if name == "lru_lambda":
                g = -g / (1.0 + jnp.exp(w))
            for o, val in zip(vec_out, (g,) + _adamw_math(w, g, m, v)):
                o[j:j + 1, :] = val
        for row, n_rows, ins, group in ((CONV_W_ROW, 4, conv_refs, conv_out), (META_ROW, N_META, meta_refs, meta_out)):
            g = col_ref[row:row + n_rows, :]
            w, m, v = (r[...] for r in ins)
            for o, val in zip(group, (g,) + _adamw_math(w, g, m, v)):
                o[...] = val

    whole = lambda shape: pl.BlockSpec(shape, lambda i, pr: (0,) * len(shape))
    in_specs = [whole((SMALL_ROWS, D)), pl.BlockSpec((SMALL_ROWS, qcols), lambda i, pr: (0, pr[0]))]
    in_specs += [whole((1, D))] * (3 * nvec) + [whole((4, qcols))] * 3 + [whole((N_META, qcols))] * 3
    out_shapes = [(1, 1)] + [(8, D)] * 4 + [(4, qcols)] * 4 + [(N_META, qcols)] * 4
    return pl.pallas_call(
        body,
        name="small_update",
        grid_spec=pltpu.PrefetchScalarGridSpec(num_scalar_prefetch=1, grid=(1,), in_specs=in_specs,
                                               out_specs=[whole(s) for s in out_shapes]),
        out_shape=[jax.ShapeDtypeStruct(s, F32) for s in out_shapes],
        compiler_params=_params(("arbitrary",)),
    )(place, total, total, *[a for t in vecs for a in t], *conv, *meta)


WEIGHT_ORDER = ["meta_tokens", "mix_norm_w", "w_in", "conv_w", "conv_b", "lru_wa", "lru_ba", "lru_wx", "lru_bx", "lru_lambda",
                "w_branch_ret", "w_branch_lru", "w_out", "ffn_norm_w", "w_ffn_in", "w_ffn_out", "final_norm_w"]


def kernel(x, meta_tokens, mix_norm_w, w_in, conv_w, conv_b, lru_wa, lru_ba, lru_wx, lru_bx, lru_lambda, w_branch_ret, w_branch_lru, w_out, ffn_norm_w, w_ffn_in, w_ffn_out, final_norm_w, loss_target, m_meta_tokens, m_mix_norm_w, m_w_in, m_conv_w, m_conv_b, m_lru_wa, m_lru_ba, m_lru_wx, m_lru_bx, m_lru_lambda, m_w_branch_ret, m_w_branch_lru, m_w_out, m_ffn_norm_w, m_w_ffn_in, m_w_ffn_out, m_final_norm_w, v_meta_tokens, v_mix_norm_w, v_w_in, v_conv_w, v_conv_b, v_lru_wa, v_lru_ba, v_lru_wx, v_lru_bx, v_lru_lambda, v_w_branch_ret, v_w_branch_lru, v_w_out, v_ffn_norm_w, v_w_ffn_in, v_w_ffn_out, v_final_norm_w):
    args = locals()
    wts = {n: args[n] for n in WEIGHT_ORDER}
    mom = {n: args["m_" + n] for n in WEIGHT_ORDER}
    var = {n: args["v_" + n] for n in WEIGHT_ORDER}
    place = jnp.stack([2 * lax.axis_index("x") + lax.axis_index("y"), lax.axis_index("c")]).astype(jnp.int32)

    shards = {n: wts[n][0].astype(BF16) for n in BIG_PIECES}
    shards["meta_tokens"] = wts["meta_tokens"]
    shards["conv_w"] = wts["conv_w"][0]
    plan = _CommPlan(shards, place)
    (first,) = _run_riders("gather_first", [_gather_rider(shards, FIRST_WEIGHTS)])
    w = dict(zip(FIRST_WEIGHTS, first))
    for n in VEC_NAMES:
        w[n] = wts[n].reshape(1, D)
    chips = jnp.bitwise_xor(place[0], jnp.arange(4, dtype=jnp.int32))
    w["route"] = jnp.concatenate([place, jnp.stack([2 * chips, 2 * chips + 1], axis=1).reshape(8)])
    w["w_in_shard"] = shards["w_in"]

    grad_x, grad_head, _, stats = _local_step(x[0], loss_target[0], w, plan)
    shard_grads = plan.finish()

    out = {}
    for n in BIG_PIECES:
        shape2d = (-1, wts[n].shape[-1])
        res = _adamw(n, *[a.reshape(shape2d) for a in (shard_grads[n], wts[n], mom[n], var[n])])
        out[n] = [r.reshape(wts[n].shape) for r in res]

    partial = jnp.concatenate(stats + [grad_head[PAD_ROWS:]], axis=0)
    total = _small_all_reduce(partial)
    vecs = [tuple(a[n].reshape(1, D) for a in (wts, mom, var)) for n in VEC_NAMES]
    conv = tuple(a["conv_w"][0] for a in (wts, mom, var))
    meta = tuple(a["meta_tokens"] for a in (wts, mom, var))
    res = _small_update(total, place, vecs, conv, meta)
    loss = res[0].reshape(())
    for j, n in enumerate(VEC_NAMES):
        out[n] = [r[j].reshape(wts[n].shape) for r in res[1:5]]
    out["conv_w"] = [r.reshape(wts["conv_w"].shape) for r in res[5:9]]
    out["meta_tokens"] = list(res[9:13])

    return (loss, grad_x.reshape(x.shape)) + tuple(out[n][kind] for kind in range(4) for n in WEIGHT_ORDER)
```

```python
import functools
import math

import jax
import jax.numpy as jnp
from jax import lax
from jax.experimental import pallas as pl
from jax.experimental.pallas import tpu as pltpu

F32 = jnp.float32
BF16 = jnp.bfloat16

D = 1024
HEADS = 8
DH = 128
CHUNK = 256
N_META = 16
FRONT = 256
PAD_ROWS = FRONT - N_META
LRU_BLOCKS = 4
LRU_BLOCK = 256
LRU_C = 8.0
FFN = 2816
FFN_HALF = FFN // 2
IN_COLS = 8 * D
ROPE_BASE = 10000.0
EPS = 1e-6
QK_SCALE = DH ** -0.5

ADAM_LR = 0.001
ADAM_B1 = 0.9
ADAM_B2 = 0.999
ADAM_EPS = 1e-08
ADAM_WD = 0.01
ADAM_STEP = 10

TM = 256
TM_HEAVY = 768
FFN_BLOCK = FFN // 2
TM_MIX_BWD = 256
VMEM_LIMIT = 60 * 1024 * 1024
TN_VMEM_BUDGET = 40 * 1024 * 1024

NT_DIMS = (((1,), (1,)), ((), ()))
TN_DIMS = (((0,), (0,)), ((), ()))
MESH = pl.DeviceIdType.MESH


def _params(sem=None):
    if sem is None:
        return pltpu.CompilerParams(vmem_limit_bytes=VMEM_LIMIT)
    return pltpu.CompilerParams(dimension_semantics=sem, vmem_limit_bytes=VMEM_LIMIT)


def _dot(a, b):
    return jnp.dot(a, b, preferred_element_type=F32)


def _dot_nt(a, b):
    return lax.dot_general(a, b, NT_DIMS, preferred_element_type=F32)


def _dot_tn(a, b):
    return lax.dot_general(a, b, TN_DIMS, preferred_element_type=F32)


def _sigmoid(z):
    return 1.0 / (1.0 + jnp.exp(-z))


def _log1p(x):
    return jnp.where(x < 1e-3, x * (1.0 - x * (0.5 - x * (1.0 / 3.0))), jnp.log(1.0 + x))


def _softplus(x):
    return jnp.maximum(x, 0.0) + _log1p(jnp.exp(-jnp.abs(x)))


def _neg_expm1(x):
    series = -x * (1.0 + x * (0.5 + x * (1.0 / 6.0 + x * (1.0 / 24.0 + x * (1.0 / 120.0)))))
    return jnp.where(x > -0.05, series, 1.0 - jnp.exp(x))


_GELU_K = math.sqrt(2.0 / math.pi)


def _gelu_and_grad(x):
    inner = _GELU_K * (x + 0.044715 * x * x * x)
    t = jnp.tanh(inner)
    val = 0.5 * x * (1.0 + t)
    grad = 0.5 * (1.0 + t) + 0.5 * x * (1.0 - t * t) * _GELU_K * (1.0 + 3.0 * 0.044715 * x * x)
    return val, grad


def _row_ids(i, rows, shape):
    return i * rows + lax.broadcasted_iota(jnp.int32, shape, 0)


def _rope_tables(rows):
    pos = jnp.arange(rows, dtype=jnp.int32) - PAD_ROWS
    inv_freq = ROPE_BASE ** (-jnp.arange(0, DH, 2, dtype=F32) / DH)
    ang = pos.astype(F32)[:, None] * inv_freq[None, :]
    cos, sin = jnp.cos(ang), jnp.sin(ang)
    return jnp.concatenate([cos, cos], axis=1), jnp.concatenate([-sin, sin], axis=1)


def _decay_tables():
    log_g = jnp.log(1.0 - 2.0 ** (-5.0 - jnp.arange(HEADS, dtype=F32)))
    idx = jnp.arange(CHUNK, dtype=F32)
    diff = idx[:, None] - idx[None, :]
    intra = jnp.where(diff[None] >= 0, jnp.exp(jnp.maximum(diff, 0.0)[None] * log_g[:, None, None]), 0.0)
    q_decay = jnp.exp((idx + 1.0)[:, None] * log_g[None, :])
    k_decay = jnp.exp((CHUNK - 1.0 - idx)[:, None] * log_g[None, :])
    chunk_decay = jnp.exp(CHUNK * log_g)
    wide = lambda a: jnp.repeat(a, DH, axis=-1)
    return intra, jnp.swapaxes(intra, 1, 2), wide(q_decay), wide(k_decay), wide(chunk_decay[None, :])


def _norm1(head, x2d, norm_w, riders=()):
    rows = FRONT + x2d.shape[0]

    def body(head_ref, x_ref, nw_ref, u_ref, rstd_ref):
        def norm(hv):
            rs = lax.rsqrt(jnp.mean(hv * hv, axis=-1, keepdims=True) + EPS)
            u_ref[...] = ((hv * rs) * nw_ref[...]).astype(BF16)
            rstd_ref[...] = rs

        @pl.when(pl.program_id(0) == 0)
        def _():
            norm(head_ref[...])

        @pl.when(pl.program_id(0) > 0)
        def _():
            norm(x_ref[...])

    return _hosted_call(
        body,
        name="norm1",
        grid=(rows // TM,),
        in_specs=[
            pl.BlockSpec((FRONT, D), lambda i: (0, 0)),
            pl.BlockSpec((TM, D), lambda i: (jnp.maximum(i - 1, 0), 0)),
            pl.BlockSpec((1, D), lambda i: (0, 0)),
        ],
        out_specs=[pl.BlockSpec((TM, D), lambda i: (i, 0)), pl.BlockSpec((TM, 1), lambda i: (i, 0))],
        out_shape=[jax.ShapeDtypeStruct((rows, D), BF16), jax.ShapeDtypeStruct((rows, 1), F32)],
        scratch_shapes=[],
        args=(head, x2d, norm_w),
        riders=riders,
    )


def _heavy_tile(rows):
    return TM_HEAVY if rows % TM_HEAVY == 0 else TM


def _swap_3_4(group):
    return jnp.where(group == 3, 4, jnp.where(group == 4, 3, group))


LRU_IN_COL = 3
GATES_COL = 1


def _in_proj(u, w_shard, route, cos_t, sin_t, riders=()):
    rows = u.shape[0]
    tm = _heavy_tile(rows)
    nt = rows // tm
    kind, shard = BIG_PIECES["w_in"]

    def body(route_ref, u_hbm, wsh_ref, cos_ref, sin_ref, proj_ref, wfull_ref,
             u_sc, w_sc, u_sem, w_sem, local_sem, ici_send, ici_recv, fwd_send, fwd_recv):
        g, i = pl.program_id(0), pl.program_id(1)
        s, c = route_ref[0], route_ref[1]
        gid = route_ref[2 + g]
        sibling = (s // 2, s % 2, 1 - c)
        local = pltpu.make_async_copy(wsh_ref, _full_region(wfull_ref, kind, shard, s, None), local_sem)
        u_copies = [pltpu.make_async_copy(u_hbm.at[pl.ds(t * tm, tm)], u_sc.at[t], u_sem.at[t]) for t in range(nt)]
        sends, arrivals = [], []
        for k in (1, 2, 3):
            s2, dev = _other_chip(s, c, k)
            sends.append(_remote(_shard_region(wsh_ref, shard, c), _full_region(wfull_ref, kind, shard, s, c),
                                 ici_send.at[k - 1], ici_recv.at[k - 1], dev))
            mine = _full_region(wfull_ref, kind, shard, s2, c)
            theirs = _full_region(wfull_ref, kind, shard, s2, 1 - c)
            arrivals.append((_remote(mine, mine, ici_send.at[k - 1], ici_recv.at[k - 1], dev),
                             _remote(mine, mine, fwd_send.at[k - 1], fwd_recv.at[k - 1], sibling),
                             _remote(theirs, theirs, fwd_send.at[k - 1], fwd_recv.at[k - 1], sibling)))

        slot = g % 2
        last_tile = i == nt - 1

        def block_copy(src, col, to_slot):
            return pltpu.make_async_copy(src.at[:, pl.ds(pl.multiple_of(col * D, D), D)], w_sc.at[to_slot],
                                         w_sem.at[to_slot])

        @pl.when(jnp.logical_and(g == 0, i == 0))
        def _():
            for cp in sends:
                cp.start()
            local.start()
            for cp in u_copies:
                cp.start()
            cp = block_copy(wsh_ref, 0, 0)
            cp.start()
            cp.wait()

        @pl.when(jnp.logical_and(last_tile, g == 0))
        def _():
            block_copy(wsh_ref, 1, 1).start()

        for k, (arrived, forward, forwarded) in zip((1, 2, 3), arrivals):
            @pl.when(jnp.logical_and(last_tile, g == 2 * k - 1))
            def _():
                arrived.wait_recv()
                forward.start()
                forwarded.wait_recv()
                block_copy(wfull_ref, route_ref[2 + 2 * k], 0).start()

            @pl.when(jnp.logical_and(last_tile, g == 2 * k))
            def _():
                block_copy(wfull_ref, route_ref[3 + 2 * k], 1).start()

        @pl.when(jnp.logical_and(i == 0, g > 0))
        def _():
            block_copy(wfull_ref, 0, slot).wait()

        for t in range(nt):
            @pl.when(jnp.logical_and(g == 0, i == t))
            def _():
                u_copies[t].wait()

        acc = _dot(u_sc[i], w_sc[slot])

        @pl.when(gid < 2)
        def _():
            scale = jnp.where(gid == 1, QK_SCALE, 1.0).astype(F32)
            for h in range(HEADS):
                sl = slice(h * DH, (h + 1) * DH)
                blk = acc[:, sl]
                out = (blk * cos_ref[...] + pltpu.roll(blk, DH // 2, axis=1) * sin_ref[...]) * scale
                proj_ref[:, sl] = out.astype(BF16)

        @pl.when(gid >= 2)
        def _():
            proj_ref[...] = acc.astype(BF16)

        @pl.when(jnp.logical_and(g == 7, i == nt - 1))
        def _():
            local.wait()
            for cp in sends:
                cp.wait_send()
            for _, forward, _ in arrivals:
                forward.wait_send()

    return _hosted_call(
        body,
        name="in_proj",
        grid=(8, nt),
        in_specs=[
            _ANY, _ANY,
            pl.BlockSpec((tm, DH), lambda g, i, rt: (i, 0)),
            pl.BlockSpec((tm, DH), lambda g, i, rt: (i, 0)),
        ],
        out_specs=[pl.BlockSpec((tm, D), lambda g, i, rt: (i, _swap_3_4(rt[2 + g]))), _ANY],
        out_shape=[jax.ShapeDtypeStruct((rows, IN_COLS), BF16), jax.ShapeDtypeStruct((D, IN_COLS), BF16)],
        scratch_shapes=[
            pltpu.VMEM((nt, tm, D), BF16), pltpu.VMEM((2, D, D), BF16),
            pltpu.SemaphoreType.DMA((nt,)), pltpu.SemaphoreType.DMA((2,)), pltpu.SemaphoreType.DMA,
            pltpu.SemaphoreType.DMA((3,)), pltpu.SemaphoreType.DMA((3,)),
            pltpu.SemaphoreType.DMA((3,)), pltpu.SemaphoreType.DMA((3,)),
        ],
        args=(u, w_shard, cos_t, sin_t),
        riders=riders,
        prefetch=route,
        riders_after_body=True,
    )


def _retention_fwd(proj_bf, intra, q_dec, k_dec, c_dec, riders=()):
    rows = proj_bf.shape[0]
    nc = rows // CHUNK

    def body(q_ref, k_ref, v_ref, m_ref, qd_ref, kd_ref, cd_ref, o_ref, sprev_ref, s_sc):
        @pl.when(pl.program_id(0) == 0)
        def _():
            s_sc[...] = jnp.zeros_like(s_sc)

        for h in range(HEADS):
            sl = slice(h * DH, (h + 1) * DH)
            q, k, v = q_ref[:, sl], k_ref[:, sl], v_ref[:, sl]
            state = s_sc[h]
            state_b = state.astype(BF16)
            sprev_ref[0, h] = state_b
            s = _dot_nt(q, k) * m_ref[h]
            inner = _dot(s.astype(BF16), v)
            cross = _dot(q, state_b) * qd_ref[:, sl]
            o_ref[:, sl] = (inner + cross).astype(BF16)
            k_scaled = (k.astype(F32) * kd_ref[:, sl]).astype(BF16)
            s_sc[h] = state * cd_ref[:, sl] + _dot_tn(k_scaled, v)

    chunk_spec = lambda col: pl.BlockSpec((CHUNK, D), lambda c: (c, col))
    const2 = lambda shape: pl.BlockSpec(shape, lambda c: (0, 0))
    return _hosted_call(
        body,
        name="retention_fwd",
        grid=(nc,),
        in_specs=[
            chunk_spec(0), chunk_spec(1), chunk_spec(2),
            pl.BlockSpec((HEADS, CHUNK, CHUNK), lambda c: (0, 0, 0)),
            const2((CHUNK, D)), const2((CHUNK, D)), const2((1, D)),
        ],
        out_specs=[
            pl.BlockSpec((CHUNK, D), lambda c: (c, 0)),
            pl.BlockSpec((1, HEADS, DH, DH), lambda c: (c, 0, 0, 0)),
        ],
        out_shape=[
            jax.ShapeDtypeStruct((rows, D), BF16),
            jax.ShapeDtypeStruct((nc, HEADS, DH, DH), BF16),
        ],
        scratch_shapes=[pltpu.VMEM((HEADS, DH, DH), F32)],
        args=(proj_bf, proj_bf, proj_bf, intra, q_dec, k_dec, c_dec),
        riders=riders,
    )


def _shift_down(x, first8_prev, d):
    rolled = pltpu.roll(x, d, axis=0)
    head = pltpu.roll(jnp.concatenate([first8_prev, x[0:8]], axis=0), d, axis=0)[8:16]
    return rolled, head


def _conv_and_gates(x, prev8, cw_ref, cb_ref, wa_ref, wx_ref, ba_ref, bx_ref, lam_ref, c_sc):
    cw = cw_ref[...]
    conv = cb_ref[...] + cw[3:4] * x
    head = cb_ref[...] + cw[3:4] * x[0:8]
    for d in (1, 2, 3):
        rolled, hd = _shift_down(x, prev8, d)
        conv = conv + cw[3 - d:4 - d] * rolled
        head = head + cw[3 - d:4 - d] * hd
    c_sc[...] = conv
    c_sc[0:8, :] = head
    c = c_sc[...]
    zr, zi = [], []
    for g in range(LRU_BLOCKS):
        sl = slice(g * LRU_BLOCK, (g + 1) * LRU_BLOCK)
        cg = c[:, sl].astype(BF16)
        zr.append(_dot(cg, wa_ref[g]))
        zi.append(_dot(cg, wx_ref[g]))
    r = _sigmoid(jnp.concatenate(zr, axis=1) + ba_ref[...])
    gate_i = _sigmoid(jnp.concatenate(zi, axis=1) + bx_ref[...])
    sp = _softplus(-lam_ref[...])
    log_a = (-LRU_C) * r * sp
    a = jnp.exp(log_a)
    mult = jnp.sqrt(_neg_expm1(2.0 * log_a))
    return c, r, gate_i, a, mult, log_a


def _lru_fwd(proj, conv_w, conv_b, wa, wx, ba, bx, lam, riders=()):
    rows = proj.shape[0]
    nt = rows // TM

    def body(x_ref, cw_ref, cb_ref, wa_ref, wx_ref, ba_ref, bx_ref, lam_ref,
             h_ref, c_ref, r_ref, i_ref, la_ref, mult_ref, prev_sc, carry_sc, c_sc, a_sc, u_sc, h_sc):
        i = pl.program_id(0)

        @pl.when(i == 0)
        def _():
            prev_sc[...] = jnp.zeros_like(prev_sc)
            carry_sc[...] = jnp.zeros_like(carry_sc)

        x = x_ref[...].astype(F32)
        c, r, gate_i, a, mult, log_a = _conv_and_gates(x, prev_sc[...], cw_ref, cb_ref, wa_ref, wx_ref, ba_ref, bx_ref,
                                                       lam_ref, c_sc)
        for ref, val in ((c_ref, c), (r_ref, r), (i_ref, gate_i), (la_ref, log_a), (mult_ref, mult)):
            ref[...] = val.astype(BF16)
        prev_sc[...] = x[TM - 8:TM]
        valid = _row_ids(i, TM, (TM, D)) >= PAD_ROWS
        a_sc[...] = a
        u_sc[...] = jnp.where(valid, mult * gate_i * c, 0.0)
        row8 = lax.broadcasted_iota(jnp.int32, (8, D), 0)

        def group(gi, hprev):
            r0 = pl.multiple_of(gi * 8, 8)
            aa = a_sc[pl.ds(r0, 8), :]
            uu = u_sc[pl.ds(r0, 8), :]
            for d in (1, 2, 4):
                a_sh = jnp.where(row8 >= d, pltpu.roll(aa, d, axis=0), 1.0)
                u_sh = jnp.where(row8 >= d, pltpu.roll(uu, d, axis=0), 0.0)
                uu = uu + aa * u_sh
                aa = aa * a_sh
            hb = aa * hprev + uu
            h_sc[pl.ds(r0, 8), :] = hb
            return hb[7:8, :]

        hlast = lax.fori_loop(0, TM // 8, group, carry_sc[0:1, :])
        carry_sc[0:1, :] = hlast
        h_ref[...] = h_sc[...].astype(BF16)

    vec = pl.BlockSpec((1, D), lambda i: (0, 0))
    wspec = pl.BlockSpec((LRU_BLOCKS, LRU_BLOCK, LRU_BLOCK), lambda i: (0, 0, 0))
    return _hosted_call(
        body,
        name="lru_fwd",
        grid=(nt,),
        in_specs=[
            pl.BlockSpec((TM, D), lambda i: (i, LRU_IN_COL)),
            pl.BlockSpec((4, D), lambda i: (0, 0)),
            vec, wspec, wspec, vec, vec, vec,
        ],
        out_specs=[pl.BlockSpec((TM, D), lambda i: (i, 0)) for _ in range(6)],
        out_shape=[jax.ShapeDtypeStruct((rows, D), BF16) for _ in range(6)],
        scratch_shapes=[
            pltpu.VMEM((8, D), F32), pltpu.VMEM((8, D), F32),
            pltpu.VMEM((TM, D), F32), pltpu.VMEM((TM, D), F32), pltpu.VMEM((TM, D), F32), pltpu.VMEM((TM, D), F32),
        ],
        args=(proj, conv_w, conv_b, wa, wx, ba, bx, lam),
        riders=riders,
    )


def _group_norm(o):
    outs, rstds = [], []
    for h in range(HEADS):
        oh = o[:, h * DH:(h + 1) * DH]
        rs = lax.rsqrt(jnp.mean(oh * oh, axis=-1, keepdims=True) + EPS)
        outs.append(oh * rs)
        rstds.append(rs)
    return jnp.concatenate(outs, axis=1), rstds


def _mix_fwd(head, x2d, o, proj, h_lru, w_br, w_bl, w_o, ffn_norm_w, riders=()):
    rows = o.shape[0]
    nt = rows // TM

    def body(head_ref, x_ref, o_ref, gates_ref, hl_ref, wbr_ref, wbl_ref, wo_ref, nw_ref,
             yret_ref, ylru_ref, h1_ref, u2_ref, rstd_ref):
        i = pl.program_id(0)
        gate = lambda j: gates_ref[:, j * D:(j + 1) * D].astype(F32)
        on, _ = _group_norm(o_ref[...].astype(F32))
        gret = gate(0)
        a_ret = (gret * _sigmoid(gret) * on).astype(BF16)
        y_ret = _dot(a_ret, wbr_ref[...])
        gl, _ = _gelu_and_grad(gate(1))
        a_lru = (gl * hl_ref[...].astype(F32)).astype(BF16)
        y_lru = _dot(a_lru, wbl_ref[...])
        mixed = (_sigmoid(gate(2)) * y_ret + _sigmoid(gate(3)) * y_lru).astype(BF16)
        delta = _dot(mixed, wo_ref[...])
        yret_ref[...] = y_ret.astype(BF16)
        ylru_ref[...] = y_lru.astype(BF16)

        def finish(h0):
            h1 = h0 + delta
            rs = lax.rsqrt(jnp.mean(h1 * h1, axis=-1, keepdims=True) + EPS)
            h1_ref[...] = h1
            u2_ref[...] = ((h1 * rs) * nw_ref[...]).astype(BF16)
            rstd_ref[...] = rs

        @pl.when(i == 0)
        def _():
            finish(head_ref[...])

        @pl.when(i > 0)
        def _():
            finish(x_ref[...])

    tile = lambda col: pl.BlockSpec((TM, D), lambda i: (i, col))
    wspec = pl.BlockSpec((D, D), lambda i: (0, 0))
    return _hosted_call(
        body,
        name="mix_fwd",
        grid=(nt,),
        in_specs=[
            pl.BlockSpec((FRONT, D), lambda i: (0, 0)),
            pl.BlockSpec((TM, D), lambda i: (jnp.maximum(i - 1, 0), 0)),
            tile(0), pl.BlockSpec((TM, 4 * D), lambda i: (i, GATES_COL)), tile(0),
            wspec, wspec, wspec,
            pl.BlockSpec((1, D), lambda i: (0, 0)),
        ],
        out_specs=[tile(0), tile(0), tile(0), tile(0), pl.BlockSpec((TM, 1), lambda i: (i, 0))],
        out_shape=[
            jax.ShapeDtypeStruct((rows, D), BF16), jax.ShapeDtypeStruct((rows, D), BF16),
            jax.ShapeDtypeStruct((rows, D), F32), jax.ShapeDtypeStruct((rows, D), BF16),
            jax.ShapeDtypeStruct((rows, 1), F32),
        ],
        scratch_shapes=[],
        args=(head, x2d, o, proj, h_lru, w_br, w_bl, w_o, ffn_norm_w),
        riders=riders,
    )


def _ffn_fwd_gate(u2, w_ffn_in, riders=()):
    rows = u2.shape[0]
    tm = _heavy_tile(rows)
    nb = FFN // FFN_BLOCK
    hid = lambda: pl.BlockSpec((tm, FFN_BLOCK), lambda i, j: (i, j))

    def body(u2_ref, w_hbm, silu_ref, dsilu_ref, act_ref, w_sc, w_sem):
        j = pl.program_id(1)

        @pl.when(jnp.logical_and(pl.program_id(0) == 0, j == 0))
        def _():
            cp = pltpu.make_async_copy(w_hbm, w_sc, w_sem)
            cp.start()
            cp.wait()

        u2 = u2_ref[...]
        g = _dot(u2, w_sc[:, pl.ds(pl.multiple_of(j * FFN_BLOCK, 128), FFN_BLOCK)])
        up = _dot(u2, w_sc[:, pl.ds(pl.multiple_of(FFN + j * FFN_BLOCK, 128), FFN_BLOCK)])
        sg = _sigmoid(g)
        silu = g * sg
        silu_ref[...] = silu.astype(BF16)
        dsilu_ref[...] = (up * (sg * (1.0 + g * (1.0 - sg)))).astype(BF16)
        act_ref[...] = (silu * up).astype(BF16)

    return _hosted_call(
        body,
        name="ffn_fwd_gate",
        grid=(rows // tm, nb),
        in_specs=[pl.BlockSpec((tm, D), lambda i, j: (i, 0)), _ANY],
        out_specs=[hid(), hid(), hid()],
        out_shape=[jax.ShapeDtypeStruct((rows, FFN), BF16)] * 3,
        scratch_shapes=[pltpu.VMEM((D, 2 * FFN), BF16), pltpu.SemaphoreType.DMA],
        args=(u2, w_ffn_in),
        riders=riders,
    )


def _ffn_out_loss(act, h1, w_ffn_out, target, final_norm_w):
    rows = act.shape[0]
    tm = _heavy_tile(rows)
    nt = rows // tm

    def body(act_ref, h1_ref, wo_ref, fnw_ref, tgt_hbm, dh2_ref, stats_ref, tgt_sc, tgt_sem):
        i = pl.program_id(0)
        slot = i % 2

        @pl.when(i == 0)
        def _():
            stats_ref[...] = jnp.zeros_like(stats_ref)
            tgt_sc[0, 0:FRONT, :] = jnp.zeros((FRONT, D), F32)
            cp = pltpu.make_async_copy(tgt_hbm.at[pl.ds(0, tm - FRONT)], tgt_sc.at[0, pl.ds(FRONT, tm - FRONT)], tgt_sem.at[0])
            cp.start()
            cp.wait()

        @pl.when(i + 1 < nt)
        def _():
            start = pl.multiple_of((i + 1) * tm - FRONT, FRONT)
            pltpu.make_async_copy(tgt_hbm.at[pl.ds(start, tm)], tgt_sc.at[1 - slot], tgt_sem.at[1 - slot]).start()

        @pl.when(i > 0)
        def _():
            pltpu.make_async_copy(tgt_hbm.at[pl.ds(0, tm)], tgt_sc.at[slot], tgt_sem.at[slot]).wait()

        h2 = h1_ref[...] + _dot(act_ref[...], wo_ref[...])
        rs = lax.rsqrt(jnp.mean(h2 * h2, axis=-1, keepdims=True) + EPS)
        n = h2 * rs
        fnw = fnw_ref[...]
        valid = _row_ids(i, tm, (tm, D)) >= FRONT
        diff = jnp.where(valid, n * fnw - tgt_sc[slot], 0.0)
        dy = diff * (1.0 / D)
        stats_ref[0:1, :] += (0.5 / D) * jnp.sum(diff * diff, axis=0, keepdims=True)
        stats_ref[1:2, :] += jnp.sum(dy * n, axis=0, keepdims=True)
        dn = dy * fnw
        dh2_ref[...] = rs * (dn - n * jnp.mean(dn * n, axis=-1, keepdims=True))

    return pl.pallas_call(
        body,
        name="ffn_out_loss",
        grid=(nt,),
        in_specs=[
            pl.BlockSpec((tm, FFN), lambda i: (i, 0)),
            pl.BlockSpec((tm, D), lambda i: (i, 0)),
            pl.BlockSpec((FFN, D), lambda i: (0, 0)),
            pl.BlockSpec((1, D), lambda i: (0, 0)),
            _ANY,
        ],
        out_specs=[pl.BlockSpec((tm, D), lambda i: (i, 0)), pl.BlockSpec((8, D), lambda i: (0, 0))],
        out_shape=[jax.ShapeDtypeStruct((rows, D), F32), jax.ShapeDtypeStruct((8, D), F32)],
        scratch_shapes=[pltpu.VMEM((2, tm, D), F32), pltpu.SemaphoreType.DMA((2,))],
        compiler_params=_params(("arbitrary",)),
    )(act, h1, w_ffn_out, final_norm_w, target)


def _ffn_bwd(dh2, silu, dsilu, h1, rstd2, w_ffn_in, w_ffn_out, ffn_norm_w):
    rows = dh2.shape[0]
    tm = _heavy_tile(rows)
    nb = FFN // FFN_BLOCK
    blk = lambda: pl.BlockSpec((tm, FFN_BLOCK), lambda i, j: (i, j))

    def gate_body(dh2_ref, silu_ref, dsilu_ref, wo_hbm, dg_ref, dup_ref, dh2b_ref, wo_sc, wo_sem):
        j = pl.program_id(1)

        @pl.when(jnp.logical_and(pl.program_id(0) == 0, j == 0))
        def _():
            cp = pltpu.make_async_copy(wo_hbm, wo_sc, wo_sem)
            cp.start()
            cp.wait()

        @pl.when(j == 0)
        def _():
            dh2b_ref[...] = dh2_ref[...].astype(BF16)

        dact = _dot_nt(dh2b_ref[...], wo_sc[pl.ds(pl.multiple_of(j * FFN_BLOCK, 128), FFN_BLOCK), :])
        dup_ref[...] = (dact * silu_ref[...].astype(F32)).astype(BF16)
        dg_ref[...] = (dact * dsilu_ref[...].astype(F32)).astype(BF16)

    dg, dup, dh2b = pl.pallas_call(
        gate_body,
        name="ffn_bwd_gate",
        grid=(rows // tm, nb),
        in_specs=[pl.BlockSpec((tm, D), lambda i, j: (i, 0)), blk(), blk(), _ANY],
        out_specs=[blk(), blk(), pl.BlockSpec((tm, D), lambda i, j: (i, 0))],
        out_shape=[jax.ShapeDtypeStruct((rows, FFN), BF16)] * 2 + [jax.ShapeDtypeStruct((rows, D), BF16)],
        scratch_shapes=[pltpu.VMEM((FFN, D), BF16), pltpu.SemaphoreType.DMA],
        compiler_params=_params(("arbitrary", "arbitrary")),
    )(dh2, silu, dsilu, w_ffn_out)

    def body(dg_ref, dup_ref, dh2_ref, h1_ref, rstd_ref, w_hbm, nw_ref, dh1_ref, stats_ref, w_sc, w_sem):
        @pl.when(pl.program_id(0) == 0)
        def _():
            stats_ref[...] = jnp.zeros_like(stats_ref)
            cp = pltpu.make_async_copy(w_hbm, w_sc, w_sem)
            cp.start()
            cp.wait()

        du = _dot_nt(dg_ref[...], w_sc[:, 0:FFN]) + _dot_nt(dup_ref[...], w_sc[:, FFN:2 * FFN])
        rs = rstd_ref[...]
        n = h1_ref[...] * rs
        stats_ref[0:1, :] += jnp.sum(du * n, axis=0, keepdims=True)
        dn = du * nw_ref[...]
        dh1_ref[...] = dh2_ref[...] + rs * (dn - n * jnp.mean(dn * n, axis=-1, keepdims=True))

    row = lambda width: pl.BlockSpec((tm, width), lambda i: (i, 0))
    dh1, stats = pl.pallas_call(
        body,
        name="ffn_bwd_in",
        grid=(rows // tm,),
        in_specs=[row(FFN), row(FFN), row(D), row(D), row(1), _ANY, pl.BlockSpec((1, D), lambda i: (0, 0))],
        out_specs=[row(D), pl.BlockSpec((8, D), lambda i: (0, 0))],
        out_shape=[jax.ShapeDtypeStruct((rows, D), F32), jax.ShapeDtypeStruct((8, D), F32)],
        scratch_shapes=[pltpu.VMEM((D, 2 * FFN), BF16), pltpu.SemaphoreType.DMA],
        compiler_params=_params(("arbitrary",)),
    )(dg, dup, dh2, h1, rstd2, w_ffn_in, ffn_norm_w)
    return dg, dup, dh2b, dh1, stats


def _mix_bwd(dh1, o, proj, h_lru, y_ret, y_lru, w_br, w_bl, w_o, riders=()):
    rows = dh1.shape[0]
    tm = TM_MIX_BWD
    nt = rows // tm

    def body(dh1_ref, o_ref, gates_ref, hl_ref, yret_ref, ylru_ref, wbr_ref, wbl_ref, wo_ref,
             dproj_ref, do_ref, dhl_ref, mixed_ref, aret_ref, alru_ref, dyret_ref, dylru_ref):
        gate = lambda j: gates_ref[:, j * D:(j + 1) * D].astype(F32)
        dmixed = _dot_nt(dh1_ref[...].astype(BF16), wo_ref[...])
        y_ret, y_lru = yret_ref[...].astype(F32), ylru_ref[...].astype(F32)
        sa, sb = _sigmoid(gate(2)), _sigmoid(gate(3))
        mixed_ref[...] = (sa * y_ret + sb * y_lru).astype(BF16)
        dga = dmixed * y_ret * sa * (1.0 - sa)
        dgb = dmixed * y_lru * sb * (1.0 - sb)
        dy_ret = (dmixed * sa).astype(BF16)
        dy_lru = (dmixed * sb).astype(BF16)
        dyret_ref[...] = dy_ret
        dylru_ref[...] = dy_lru
        da_ret = _dot_nt(dy_ret, wbr_ref[...])
        da_lru = _dot_nt(dy_lru, wbl_ref[...])

        gret = gate(0)
        sg = _sigmoid(gret)
        silu = gret * sg
        on, rstds = _group_norm(o_ref[...].astype(F32))
        aret_ref[...] = (silu * on).astype(BF16)
        dgret = da_ret * on * (sg * (1.0 + gret * (1.0 - sg)))
        don = da_ret * silu
        for h in range(HEADS):
            sl = slice(h * DH, (h + 1) * DH)
            onh, donh = on[:, sl], don[:, sl]
            do_ref[:, sl] = (rstds[h] * (donh - onh * jnp.mean(donh * onh, axis=-1, keepdims=True))).astype(BF16)

        gl, gl_grad = _gelu_and_grad(gate(1))
        hl = hl_ref[...].astype(F32)
        alru_ref[...] = (gl * hl).astype(BF16)
        dlgate = da_lru * hl * gl_grad
        dhl_ref[...] = (da_lru * gl).astype(BF16)

        for j, val in enumerate((dgret, dlgate, dga, dgb)):
            dproj_ref[:, j * D:(j + 1) * D] = val.astype(BF16)

    tile = lambda col: pl.BlockSpec((tm, D), lambda i: (i, col))
    gates = pl.BlockSpec((tm, 4 * D), lambda i: (i, GATES_COL))
    wspec = pl.BlockSpec((D, D), lambda i: (0, 0))
    bf = lambda: jax.ShapeDtypeStruct((rows, D), BF16)
    return _hosted_call(
        body,
        name="mix_bwd",
        grid=(nt,),
        in_specs=[tile(0), tile(0), gates, tile(0), tile(0), tile(0), wspec, wspec, wspec],
        out_specs=[pl.BlockSpec((tm, 4 * D), lambda i: (i, GATES_COL))] + [tile(0)] * 7,
        out_shape=[jax.ShapeDtypeStruct((rows, IN_COLS), BF16)] + [bf() for _ in range(7)],
        scratch_shapes=[],
        args=(dh1, o, proj, h_lru, y_ret, y_lru, w_br, w_bl, w_o),
        riders=riders,
    )


def _retention_bwd(dproj, proj_bf, do, sprev, intra, intra_t, q_dec, k_dec, c_dec, cos_t, sin_t, riders=()):
    rows = proj_bf.shape[0]
    nc = rows // CHUNK

    def body(dproj_in_ref, q_ref, k_ref, v_ref, do_ref, sprev_ref, m_ref, mt_ref, qd_ref, kd_ref, cd_ref, cos_ref, sin_ref,
             dproj_ref, ds_sc):
        @pl.when(pl.program_id(0) == 0)
        def _():
            ds_sc[...] = jnp.zeros_like(ds_sc)

        cos, sin = cos_ref[...], sin_ref[...]

        def unrotate(dy):
            return dy * cos - pltpu.roll(dy, DH // 2, axis=1) * sin

        for h in range(HEADS):
            sl = slice(h * DH, (h + 1) * DH)
            q, k, v = q_ref[:, sl], k_ref[:, sl], v_ref[:, sl]
            do = do_ref[:, sl]
            dob = do.astype(BF16)
            doq = (do * qd_ref[:, sl]).astype(BF16)
            state_prev = sprev_ref[0, h]
            dstate = ds_sc[h]
            dstate_b = dstate.astype(BF16)
            s_t = (_dot_nt(k, q) * mt_ref[h]).astype(BF16)
            ds_t = (_dot_nt(v, dob) * mt_ref[h]).astype(BF16)
            ds = (_dot_nt(dob, v) * m_ref[h]).astype(BF16)
            kd = kd_ref[:, sl]
            dq = _dot(ds, k) + _dot_nt(doq, state_prev)
            dk = _dot(ds_t, q) + _dot_nt(v, dstate_b) * kd
            k_scaled = (k.astype(F32) * kd).astype(BF16)
            dv = _dot(s_t, dob) + _dot(k_scaled, dstate_b)
            ds_sc[h] = dstate * cd_ref[:, sl] + _dot_tn(q, doq)
            dproj_ref[:, sl] = unrotate(dq).astype(BF16)
            dproj_ref[:, D + h * DH:D + (h + 1) * DH] = (unrotate(dk) * QK_SCALE).astype(BF16)
            dproj_ref[:, 2 * D + h * DH:2 * D + (h + 1) * DH] = dv.astype(BF16)

    rev = lambda c: nc - 1 - c
    chunk_spec = lambda col: pl.BlockSpec((CHUNK, D), lambda c: (rev(c), col))
    const2 = lambda shape: pl.BlockSpec(shape, lambda c: (0, 0))
    const3 = pl.BlockSpec((HEADS, CHUNK, CHUNK), lambda c: (0, 0, 0))
    return _hosted_call(
        body,
        name="retention_bwd",
        grid=(nc,),
        in_specs=[
            pl.BlockSpec(memory_space=pl.ANY),
            chunk_spec(0), chunk_spec(1), chunk_spec(2), chunk_spec(0),
            pl.BlockSpec((1, HEADS, DH, DH), lambda c: (rev(c), 0, 0, 0)),
            const3, const3,
            const2((CHUNK, D)), const2((CHUNK, D)), const2((1, D)),
            pl.BlockSpec((CHUNK, DH), lambda c: (rev(c), 0)),
            pl.BlockSpec((CHUNK, DH), lambda c: (rev(c), 0)),
        ],
        out_specs=[pl.BlockSpec((CHUNK, 3 * D), lambda c: (rev(c), 0))],
        out_shape=[jax.ShapeDtypeStruct(dproj.shape, BF16)],
        aliases={0: 0},
        scratch_shapes=[pltpu.VMEM((HEADS, DH, DH), F32)],
        args=(dproj, proj_bf, proj_bf, proj_bf, do, sprev, intra, intra_t, q_dec, k_dec, c_dec, cos_t, sin_t),
        riders=riders,
    )


def _lru_bwd(dproj, proj, saved, dhl, conv_w, wa, wx, lam, riders=()):
    rows = proj.shape[0]
    nt = rows // TM
    per8 = TM // 8

    def body(dproj_in_ref, x_ref, xprev_ref, h_ref, hprev_ref, c_ref, r_ref, i_ref, la_ref, mult_ref, dhl_ref,
             cw_ref, wa_ref, wx_ref, lam_ref,
             dproj_ref, dwa_ref, dwx_ref, stats_ref, anext_sc, dhnext_sc, dcnext_sc, c_sc, b_sc, dh_sc):
        step = pl.program_id(0)
        i = nt - 1 - step

        @pl.when(step == 0)
        def _():
            anext_sc[...] = jnp.zeros_like(anext_sc)
            dhnext_sc[...] = jnp.zeros_like(dhnext_sc)
            dcnext_sc[...] = jnp.zeros_like(dcnext_sc)
            dwa_ref[...] = jnp.zeros_like(dwa_ref)
            dwx_ref[...] = jnp.zeros_like(dwx_ref)
            stats_ref[...] = jnp.zeros_like(stats_ref)

        first = i == 0
        x = x_ref[...].astype(F32)
        prev8 = jnp.where(first, 0.0, xprev_ref[8:16, :].astype(F32))
        c_b = c_ref[...]
        c, r, gate_i, mult = (ref[...].astype(F32) for ref in (c_ref, r_ref, i_ref, mult_ref))
        a = jnp.exp(la_ref[...].astype(F32))
        sp = _softplus(-lam_ref[...])
        dh_sc[...] = dhl_ref[...].astype(F32)

        b_sc[...] = pltpu.roll(a, TM - 1, axis=0)
        b_sc[TM - 1:TM, :] = anext_sc[0:1, :]
        anext_sc[0:1, :] = a[0:1, :]
        row8 = lax.broadcasted_iota(jnp.int32, (8, D), 0)

        def group(gi, dhnext):
            r0 = pl.multiple_of((per8 - 1 - gi) * 8, 8)
            bb = b_sc[pl.ds(r0, 8), :]
            uu = dh_sc[pl.ds(r0, 8), :]
            for d in (1, 2, 4):
                b_sh = jnp.where(row8 < 8 - d, pltpu.roll(bb, 8 - d, axis=0), 1.0)
                u_sh = jnp.where(row8 < 8 - d, pltpu.roll(uu, 8 - d, axis=0), 0.0)
                uu = uu + bb * u_sh
                bb = bb * b_sh
            dhb = bb * dhnext + uu
            dh_sc[pl.ds(r0, 8), :] = dhb
            return dhb[0:1, :]

        dhfirst = lax.fori_loop(0, per8, group, dhnext_sc[0:1, :])
        dhnext_sc[0:1, :] = dhfirst
        dh = dh_sc[...]

        h = h_ref[...].astype(F32)
        hprev8 = jnp.where(first, 0.0, hprev_ref[8:16, :].astype(F32))
        h_dn, h_head = _shift_down(h, hprev8, 1)
        c_sc[...] = h_dn
        c_sc[0:8, :] = h_head
        h_before = c_sc[...]

        valid = _row_ids(i, TM, (TM, D)) >= PAD_ROWS
        da = dh * h_before
        du = jnp.where(valid, dh, 0.0)
        dmult = du * gate_i * c
        dgate_i = du * mult * c
        dc = du * mult * gate_i
        dla = da * a - dmult * (a * a) / mult
        dla = jnp.where(valid, dla, 0.0)
        dr = dla * ((-LRU_C) * sp)
        dzr = dr * r * (1.0 - r)
        dzi = dgate_i * gate_i * (1.0 - gate_i)
        stats_ref[1:2, :] += jnp.sum(dzr, axis=0, keepdims=True)
        stats_ref[2:3, :] += jnp.sum(dzi, axis=0, keepdims=True)
        stats_ref[3:4, :] += jnp.sum(dla * ((-LRU_C) * r), axis=0, keepdims=True)
        dc_gate = []
        for g in range(LRU_BLOCKS):
            sl = slice(g * LRU_BLOCK, (g + 1) * LRU_BLOCK)
            cg = c_b[:, sl]
            dzr_g = dzr[:, sl].astype(BF16)
            dzi_g = dzi[:, sl].astype(BF16)
            dc_gate.append(_dot_nt(dzr_g, wa_ref[g]) + _dot_nt(dzi_g, wx_ref[g]))
            dwa_ref[g] += _dot_tn(cg, dzr_g)
            dwx_ref[g] += _dot_tn(cg, dzi_g)
        dc = dc + jnp.concatenate(dc_gate, axis=1)

        cw = cw_ref[...]
        stats_ref[0:1, :] += jnp.sum(dc, axis=0, keepdims=True)
        stats_ref[7:8, :] += jnp.sum(dc * x, axis=0, keepdims=True)
        dx = cw[3:4] * dc
        tail_src = jnp.concatenate([dc[TM - 8:TM], dcnext_sc[...]], axis=0)
        dx_tail = cw[3:4] * dc[TM - 8:TM]
        for d in (1, 2, 3):
            dx = dx + cw[3 - d:4 - d] * pltpu.roll(dc, TM - d, axis=0)
            dx_tail = dx_tail + cw[3 - d:4 - d] * pltpu.roll(tail_src, 16 - d, axis=0)[0:8]
            rolled, hd = _shift_down(x, prev8, d)
            b_sc[...] = rolled
            b_sc[0:8, :] = hd
            stats_ref[7 - d:8 - d, :] += jnp.sum(dc * b_sc[...], axis=0, keepdims=True)
        dcnext_sc[...] = dc[0:8]
        dproj_ref[...] = dx.astype(BF16)
        dproj_ref[TM - 8:TM, :] = dx_tail.astype(BF16)

    rev = lambda s: nt - 1 - s
    vec = pl.BlockSpec((1, D), lambda s: (0, 0))
    wspec = pl.BlockSpec((LRU_BLOCKS, LRU_BLOCK, LRU_BLOCK), lambda s: (0, 0, 0))
    prev16 = lambda col: pl.BlockSpec((16, D), lambda s: (jnp.maximum(rev(s) * (TM // 16) - 1, 0), col))
    tile = lambda: pl.BlockSpec((TM, D), lambda s: (rev(s), 0))
    h_lru, c_sv, r_sv, i_sv, la_sv, mult_sv = saved
    return _hosted_call(
        body,
        name="lru_bwd",
        grid=(nt,),
        in_specs=[
            pl.BlockSpec(memory_space=pl.ANY),
            pl.BlockSpec((TM, D), lambda s: (rev(s), LRU_IN_COL)), prev16(LRU_IN_COL),
            tile(), prev16(0),
            tile(), tile(), tile(), tile(), tile(), tile(),
            pl.BlockSpec((4, D), lambda s: (0, 0)),
            wspec, wspec, vec,
        ],
        out_specs=[
            pl.BlockSpec((TM, D), lambda s: (rev(s), LRU_IN_COL)),
            wspec, wspec,
            pl.BlockSpec((8, D), lambda s: (0, 0)),
        ],
        out_shape=[
            jax.ShapeDtypeStruct(dproj.shape, BF16),
            jax.ShapeDtypeStruct((LRU_BLOCKS, LRU_BLOCK, LRU_BLOCK), F32),
            jax.ShapeDtypeStruct((LRU_BLOCKS, LRU_BLOCK, LRU_BLOCK), F32),
            jax.ShapeDtypeStruct((8, D), F32),
        ],
        aliases={0: 0},
        scratch_shapes=[
            pltpu.VMEM((8, D), F32), pltpu.VMEM((8, D), F32), pltpu.VMEM((8, D), F32),
            pltpu.VMEM((TM, D), F32), pltpu.VMEM((TM, D), F32), pltpu.VMEM((TM, D), F32),
        ],
        args=(dproj, proj, proj, h_lru, h_lru, c_sv, r_sv, i_sv, la_sv, mult_sv, dhl, conv_w, wa, wx, lam),
        riders=riders,
    )


def _in_proj_bwd(dproj, w_in, part, prev=None, riders=()):
    rows = dproj.shape[0]
    tm = _heavy_tile(rows)
    nt = rows // tm
    first = 0 if part == 0 else (nt + 1) // 2
    count = (nt + 1) // 2 if part == 0 else nt - first

    def body(*refs):
        dproj_ref, w_hbm, du_ref, w_sc, w_sem = refs[-5:]

        @pl.when(pl.program_id(0) == 0)
        def _():
            moves = [((0, 3), 0), ((4, 1), 3), ((3, 1), 4), ((5, 3), 5)]
            copies = [pltpu.make_async_copy(w_hbm.at[:, pl.ds(src * D, n * D)], w_sc.at[:, pl.ds(dst * D, n * D)], w_sem.at[q])
                      for q, ((src, n), dst) in enumerate(moves)]
            for cp in copies:
                cp.start()
            for cp in copies:
                cp.wait()

        du_ref[...] = _dot_nt(dproj_ref[...], w_sc[...])

    in_specs = [pl.BlockSpec((tm, IN_COLS), lambda i: (first + i, 0)), _ANY]
    args = (dproj, w_in)
    if prev is not None:
        in_specs = [_ANY] + in_specs
        args = (prev,) + args
    return _hosted_call(
        body,
        name="in_proj_bwd_%d" % part,
        grid=(count,),
        in_specs=in_specs,
        out_specs=[pl.BlockSpec((tm, D), lambda i: (first + i, 0))],
        out_shape=[jax.ShapeDtypeStruct((rows, D), F32)],
        scratch_shapes=[pltpu.VMEM((D, IN_COLS), BF16), pltpu.SemaphoreType.DMA((4,))],
        aliases={0: 0} if prev is not None else None,
        args=args,
        riders=riders,
    )


def _norm1_bwd(du, dh1, head, x2d, rstd1, norm_w, riders=()):
    rows = du.shape[0]

    def body(du_ref, dh1_ref, head_ref, x_ref, rstd_ref, nw_ref, gx_ref, ghead_ref, stats_ref):
        i = pl.program_id(0)

        @pl.when(i == 0)
        def _():
            stats_ref[...] = jnp.zeros_like(stats_ref)

        def finish(h0, out_ref):
            du = du_ref[...]
            rs = rstd_ref[...]
            n = h0 * rs
            stats_ref[0:1, :] += jnp.sum(du * n, axis=0, keepdims=True)
            dn = du * nw_ref[...]
            out_ref[...] = dh1_ref[...] + rs * (dn - n * jnp.mean(dn * n, axis=-1, keepdims=True))

        @pl.when(i == 0)
        def _():
            finish(head_ref[...], ghead_ref)

        @pl.when(i > 0)
        def _():
            finish(x_ref[...], gx_ref)

    tile = pl.BlockSpec((TM, D), lambda i: (i, 0))
    return _hosted_call(
        body,
        name="norm1_bwd",
        grid=(rows // TM,),
        in_specs=[
            tile, tile,
            pl.BlockSpec((FRONT, D), lambda i: (0, 0)),
            pl.BlockSpec((TM, D), lambda i: (jnp.maximum(i - 1, 0), 0)),
            pl.BlockSpec((TM, 1), lambda i: (i, 0)),
            pl.BlockSpec((1, D), lambda i: (0, 0)),
        ],
        out_specs=[
            pl.BlockSpec((TM, D), lambda i: (jnp.maximum(i - 1, 0), 0)),
            pl.BlockSpec((FRONT, D), lambda i: (0, 0)),
            pl.BlockSpec((8, D), lambda i: (0, 0)),
        ],
        out_shape=[
            jax.ShapeDtypeStruct(x2d.shape, F32),
            jax.ShapeDtypeStruct((FRONT, D), F32),
            jax.ShapeDtypeStruct((8, D), F32),
        ],
        scratch_shapes=[],
        args=(du, dh1, head, x2d, rstd1, norm_w),
        riders=riders,
    )


def _matmul_tn(name, x, dy, out_cols, col0=0, prev=None, k_block=None, n_block=None, riders=(), col_map=None):
    col_map = col_map or (lambda n: n)
    rows, kdim = x.shape
    ndim = dy.shape[1]
    kb = k_block or kdim
    nb = n_block or ndim
    step_bytes = lambda t: 2 * t * (kb * x.dtype.itemsize + nb * dy.dtype.itemsize) + 2 * kb * nb * 4
    tr = next(t for t in (2816, 1408, TM_HEAVY, TM) if rows % t == 0 and (t == TM or step_bytes(t) <= TN_VMEM_BUDGET))
    nr, nk, nn = rows // tr, kdim // kb, ndim // nb
    cb0 = col0 // nb

    def body(*refs):
        x_ref, dy_ref, out_ref = refs[-3], refs[-2], refs[-1]
        part = _dot_tn(x_ref[...].astype(BF16), dy_ref[...].astype(BF16))

        @pl.when(pl.program_id(2) == 0)
        def _():
            out_ref[...] = part

        @pl.when(pl.program_id(2) > 0)
        def _():
            out_ref[...] += part

    in_specs = [
        pl.BlockSpec((tr, kb), lambda n, k, r: (r, k)),
        pl.BlockSpec((tr, nb), lambda n, k, r: (r, n)),
    ]
    args = [x, dy]
    aliases = {}
    if prev is not None:
        in_specs = [pl.BlockSpec(memory_space=pl.ANY)] + in_specs
        args = [prev] + args
        aliases = {0: 0}
    (out,), rider_outs = _hosted_call(
        body,
        name=name,
        grid=(nn, nk, nr),
        in_specs=in_specs,
        out_specs=[pl.BlockSpec((kb, nb), lambda n, k, r: (k, cb0 + col_map(n)))],
        out_shape=[jax.ShapeDtypeStruct((kdim, out_cols), F32)],
        scratch_shapes=[],
        aliases=aliases,
        args=args,
        riders=riders,
    )
    return (out, rider_outs) if riders else out


def _local_step(x2d, target, w, plan):
    rows = FRONT + x2d.shape[0]
    head = jnp.concatenate([jnp.zeros((PAD_ROWS, D), F32), w["meta_tokens"]], axis=0)
    cos_t, sin_t = _rope_tables(rows)
    intra, intra_t, q_dec, k_dec, c_dec = _decay_tables()
    grads = {}

    def hosted(host, fn, *args, **kwargs):
        outs, rider_outs = fn(*args, riders=plan.riders(host, w, grads), **kwargs)
        plan.after(host, rider_outs, w, grads)
        return outs

    (u1, rstd1), _ = _norm1(head, x2d, w["mix_norm_w"])
    proj, w["w_in"] = hosted("in_proj", _in_proj, u1, w["w_in_shard"], w["route"], cos_t, sin_t)
    o, sprev = hosted("retention_fwd", _retention_fwd, proj, intra, q_dec, k_dec, c_dec)
    lru_args = (w["conv_w"], w["conv_b"], w["lru_wa"], w["lru_wx"], w["lru_ba"], w["lru_bx"], w["lru_lambda"])
    lru_saved = hosted("lru_fwd", _lru_fwd, proj, *lru_args)
    h_lru = lru_saved[0]
    y_ret, y_lru, h1, u2, rstd2 = hosted("mix_fwd", _mix_fwd, head, x2d, o, proj, h_lru, w["w_branch_ret"],
                                         w["w_branch_lru"], w["w_out"], w["ffn_norm_w"])
    silu, dsilu, act = hosted("ffn_fwd_gate", _ffn_fwd_gate, u2, w["w_ffn_in"])
    dh2, stats_loss = _ffn_out_loss(act, h1, w["w_ffn_out"], target, w["final_norm_w"])

    dg, dup, dh2b, dh1, stats_ffn = _ffn_bwd(dh2, silu, dsilu, h1, rstd2, w["w_ffn_in"], w["w_ffn_out"], w["ffn_norm_w"])
    grads["w_ffn_in"] = _matmul_tn("dw_ffn_up", u2, dup, 2 * FFN, col0=FFN, n_block=FFN_HALF,
                                   prev=_matmul_tn("dw_ffn_gate", u2, dg, 2 * FFN, n_block=FFN_HALF))
    grads["w_ffn_out"] = _matmul_tn("dw_ffn_out", act, dh2b, D, k_block=FFN_HALF)
    (dproj, do, dhl, mixed, a_ret, a_lru, dy_ret, dy_lru) = hosted(
        "mix_bwd", _mix_bwd, dh1, o, proj, h_lru, y_ret, y_lru, w["w_branch_ret"], w["w_branch_lru"], w["w_out"])
    grads["w_out"] = _matmul_tn("dw_out", mixed, dh1, D)
    grads["w_branch_ret"] = _matmul_tn("dw_branch_ret", a_ret, dy_ret, D)
    grads["w_branch_lru"] = _matmul_tn("dw_branch_lru", a_lru, dy_lru, D)
    (dproj,) = hosted("retention_bwd", _retention_bwd, dproj, proj, do, sprev, intra, intra_t, q_dec, k_dec, c_dec,
                      cos_t, sin_t)
    dproj, grads["lru_wa"], grads["lru_wx"], stats_lru = hosted(
        "lru_bwd", _lru_bwd, dproj, proj, lru_saved, dhl, w["conv_w"], w["lru_wa"], w["lru_wx"], w["lru_lambda"])
    grads["w_in"], rider_outs = _matmul_tn("dw_in", u1, dproj, IN_COLS, n_block=D, col_map=_swap_3_4,
                                           riders=plan.riders("dw_in", w, grads))
    plan.after("dw_in", rider_outs, w, grads)
    (du1,) = hosted("in_proj_bwd_0", _in_proj_bwd, dproj, w["w_in"], 0)
    (du1,) = hosted("in_proj_bwd_1", _in_proj_bwd, dproj, w["w_in"], 1, du1)
    (grad_x, grad_head, stats_in), _ = _norm1_bwd(du1, dh1, head, x2d, rstd1, w["mix_norm_w"])
    return grad_x, grad_head, grads, [stats_loss, stats_ffn, stats_in, stats_lru]


BIG_PIECES = {
    "w_in": ("col", (D, 2 * D)),
    "w_ffn_in": ("col", (D, FFN_HALF)),
    "w_ffn_out": ("row", (FFN // 4, D)),
    "w_branch_ret": ("row", (D // 4, D)),
    "w_branch_lru": ("row", (D // 4, D)),
    "w_out": ("row", (D // 4, D)),
    "lru_wa": ("lru", (LRU_BLOCKS, LRU_BLOCK // 4, LRU_BLOCK)),
    "lru_wx": ("lru", (LRU_BLOCKS, LRU_BLOCK // 4, LRU_BLOCK)),
}
SMALL_PIECES = {"meta_tokens": ("col", (N_META, D // 4)), "conv_w": ("col", (4, D // 4))}


def _full_shape(kind, shard):
    if kind == "col":
        return (shard[0], 4 * shard[1])
    if kind == "row":
        return (4 * shard[0], shard[1])
    return (shard[0], 4 * shard[1], shard[2])


def _half_shape(kind, shard):
    return (shard[0] // 2,) + tuple(shard[1:])


def _aligned(start, multiple):
    return start if isinstance(start, int) else pl.multiple_of(start, multiple)


def _lead(h, size):
    if h is None:
        return pl.ds(0, size)
    return pl.ds(_aligned(h * (size // 2), size // 2), size // 2)


def _full_region(ref, kind, shard, s, h):
    if kind == "col":
        return ref.at[_lead(h, shard[0]), pl.ds(_aligned(s * shard[1], shard[1]), shard[1])]
    if kind == "row":
        size = shard[0] if h is None else shard[0] // 2
        start = s * shard[0] + (0 if h is None else h * (shard[0] // 2))
        return ref.at[pl.ds(_aligned(start, 16), size), :]
    return ref.at[_lead(h, shard[0]), pl.ds(_aligned(s * shard[1], shard[1]), shard[1]), :]


def _shard_region(ref, shard, h):
    return ref.at[_lead(h, shard[0])]


def _place():
    x, y, c = lax.axis_index("x"), lax.axis_index("y"), lax.axis_index("c")
    return x, y, c, 2 * x + y


def _other_chip(s, c, k):
    s2 = jnp.bitwise_xor(s, k)
    return s2, (s2 // 2, s2 % 2, c)


def _remote(src, dst, send_sem, recv_sem, dev):
    return pltpu.make_async_remote_copy(src_ref=src, dst_ref=dst, send_sem=send_sem, recv_sem=recv_sem,
                                        device_id=dev, device_id_type=MESH)


_ANY = pl.BlockSpec(memory_space=pl.ANY)


class _Rider:
    def __init__(self, ins, out_shapes, sem_shapes, build, aliased=False):
        self.ins, self.out_shapes, self.sem_shapes, self.build, self.aliased = ins, out_shapes, sem_shapes, build, aliased


def _hosted_call(body, *, name, grid, in_specs, out_specs, out_shape, scratch_shapes, args, riders=(), aliases=None,
                 prefetch=None, riders_after_body=False):
    n_in, n_out, n_sc = len(in_specs), len(out_shape), len(scratch_shapes)
    r_in = [a for r in riders for a in r.ins]
    r_out = [s for r in riders for s in r.out_shapes]
    r_sem = [s for r in riders for s in r.sem_shapes]
    lead = () if prefetch is None else (prefetch,)
    assert prefetch is None or not (aliases or any(r.aliased for r in riders))

    def full_body(*refs):
        head, refs = refs[:len(lead)], refs[len(lead):]
        ins, rin = refs[:n_in], refs[n_in:n_in + len(r_in)]
        o0 = n_in + len(r_in)
        outs, rout = refs[o0:o0 + n_out], refs[o0 + n_out:o0 + n_out + len(r_out)]
        s0 = o0 + n_out + len(r_out)
        scratch, rsem = refs[s0:s0 + n_sc], refs[s0 + n_sc:]
        starts, waits = [], []
        pi = po = ps = 0
        for r in riders:
            st, wt = r.build(rin[pi:pi + len(r.ins)], rout[po:po + len(r.out_shapes)], rsem[ps:ps + len(r.sem_shapes)])
            starts += st
            waits += wt
            pi, po, ps = pi + len(r.ins), po + len(r.out_shapes), ps + len(r.sem_shapes)
        first = functools.reduce(jnp.logical_and, [pl.program_id(d) == 0 for d in range(len(grid))])
        last = functools.reduce(jnp.logical_and, [pl.program_id(d) == grid[d] - 1 for d in range(len(grid))])

        def start_riders():
            @pl.when(first)
            def _():
                for cp in starts:
                    cp.start()

        if riders and not riders_after_body:
            start_riders()
        body(*head, *ins, *outs, *scratch)
        if riders and riders_after_body:
            start_riders()
        if riders:
            @pl.when(last)
            def _():
                for wait in waits:
                    wait()

    io_aliases = dict(aliases or {})
    pi = po = 0
    for r in riders:
        if r.aliased:
            for q in range(len(r.ins)):
                io_aliases[n_in + pi + q] = n_out + po + q
        pi, po = pi + len(r.ins), po + len(r.out_shapes)
    specs = dict(
        grid=grid,
        in_specs=list(in_specs) + [_ANY] * len(r_in),
        out_specs=list(out_specs) + [_ANY] * len(r_out),
        scratch_shapes=list(scratch_shapes) + r_sem,
    )
    if prefetch is not None:
        specs = dict(grid_spec=pltpu.PrefetchScalarGridSpec(num_scalar_prefetch=1, **specs))
    res = pl.pallas_call(
        full_body,
        name=name,
        out_shape=list(out_shape) + r_out,
        input_output_aliases=io_aliases,
        compiler_params=_params(("arbitrary",) * len(grid)),
        **specs,
    )(*lead, *args, *r_in)
    rider_outs, po = [], n_out
    for r in riders:
        rider_outs.append(list(res[po:po + len(r.out_shapes)]))
        po += len(r.out_shapes)
    return list(res[:n_out]), rider_outs


def _run_riders(name, riders):
    r_in = [a for r in riders for a in r.ins]
    r_out = [s for r in riders for s in r.out_shapes]
    r_sem = [s for r in riders for s in r.sem_shapes]

    def body(*refs):
        rin, rout, rsem = refs[:len(r_in)], refs[len(r_in):len(r_in) + len(r_out)], refs[len(r_in) + len(r_out):]
        pi = po = ps = 0
        for r in riders:
            starts, waits = r.build(rin[pi:pi + len(r.ins)], rout[po:po + len(r.out_shapes)], rsem[ps:ps + len(r.sem_shapes)])
            for cp in starts:
                cp.start()
            for wait in waits:
                wait()
            pi, po, ps = pi + len(r.ins), po + len(r.out_shapes), ps + len(r.sem_shapes)

    io_aliases = {}
    pi = po = 0
    for r in riders:
        if r.aliased:
            for q in range(len(r.ins)):
                io_aliases[pi + q] = po + q
        pi, po = pi + len(r.ins), po + len(r.out_shapes)
    res = pl.pallas_call(
        body,
        name=name,
        in_specs=[_ANY] * len(r_in),
        out_specs=[_ANY] * len(r_out),
        out_shape=r_out,
        scratch_shapes=r_sem,
        input_output_aliases=io_aliases,
    )(*r_in)
    outs, po = [], 0
    for r in riders:
        outs.append(list(res[po:po + len(r.out_shapes)]))
        po += len(r.out_shapes)
    return outs


def _piece(name):
    if name in BIG_PIECES:
        return (name, *BIG_PIECES[name], True)
    return (name, *SMALL_PIECES[name], False)


def _gather_rider(shards, names):
    pieces = [_piece(n) for n in names]
    n = len(pieces)

    def build(ins, outs, sems):
        local_sem, ici_send, ici_recv = sems
        _, _, c, s = _place()
        starts, waits = [], []
        for p, (_, kind, shard, split) in enumerate(pieces):
            cp = pltpu.make_async_copy(ins[p], _full_region(outs[p], kind, shard, s, None), local_sem.at[p])
            starts.append(cp)
            waits.append(cp.wait)
            h = c if split else None
            for k in (1, 2, 3):
                s2, dev = _other_chip(s, c, k)
                cp = _remote(_shard_region(ins[p], shard, h), _full_region(outs[p], kind, shard, s, h),
                             ici_send.at[p, k - 1], ici_recv.at[p, k - 1], dev)
                starts.append(cp)
                waits.append(cp.wait_send)
                region = _full_region(outs[p], kind, shard, s2, h)
                waits.append(_remote(region, region, ici_send.at[p, k - 1], ici_recv.at[p, k - 1], dev).wait_recv)
        return starts, waits

    return _Rider(
        [shards[name] for name in names],
        [jax.ShapeDtypeStruct(_full_shape(kind, shard), shards[name].dtype) for name, kind, shard, _ in pieces],
        [pltpu.SemaphoreType.DMA((n,)), pltpu.SemaphoreType.DMA((n, 3)), pltpu.SemaphoreType.DMA((n, 3))],
        build)


def _forward_rider(gathered, names):
    pieces = [_piece(n) for n in names]
    n = len(pieces)

    def build(ins, outs, sems):
        fwd_send, fwd_recv = sems
        x, y, c, s = _place()
        sibling = (x, y, 1 - c)
        starts, waits = [], []
        for p, (_, kind, shard, _) in enumerate(pieces):
            for k in (1, 2, 3):
                s2, _ = _other_chip(s, c, k)
                mine = _full_region(outs[p], kind, shard, s2, c)
                theirs = _full_region(outs[p], kind, shard, s2, 1 - c)
                cp = _remote(mine, mine, fwd_send.at[p, k - 1], fwd_recv.at[p, k - 1], sibling)
                starts.append(cp)
                waits.append(cp.wait_send)
                waits.append(_remote(theirs, theirs, fwd_send.at[p, k - 1], fwd_recv.at[p, k - 1], sibling).wait_recv)
        return starts, waits

    return _Rider(
        [gathered[name] for name in names],
        [jax.ShapeDtypeStruct(gathered[name].shape, gathered[name].dtype) for name in names],
        [pltpu.SemaphoreType.DMA((n, 3)), pltpu.SemaphoreType.DMA((n, 3))],
        build, aliased=True)


def _pair_exchange_rider(grads, names):
    n = len(names)

    def build(ins, outs, sems):
        send_sem, recv_sem = sems
        x, y, c, _ = _place()
        sibling = (x, y, 1 - c)
        starts, waits = [], []
        for p, name in enumerate(names):
            kind, shard = BIG_PIECES[name]
            for s2 in range(4):
                cp = _remote(_full_region(ins[p], kind, shard, s2, 1 - c), outs[p].at[s2], send_sem.at[p, s2],
                             recv_sem.at[p, s2], sibling)
                starts.append(cp)
                waits.append(cp.wait_send)
                waits.append(_remote(outs[p].at[s2], outs[p].at[s2], send_sem.at[p, s2], recv_sem.at[p, s2], sibling).wait_recv)
        return starts, waits

    return _Rider(
        [grads[name] for name in names],
        [jax.ShapeDtypeStruct((4,) + _half_shape(*BIG_PIECES[name]), F32) for name in names],
        [pltpu.SemaphoreType.DMA((n, 4))] * 2,
        build)


def _half_specs(kind, shard):
    half = shard[0] // 2
    if kind == "col":
        full = pl.BlockSpec((half, shard[1]), lambda j, pr: (pr[1], j))
        buf = pl.BlockSpec((None, half, shard[1]), lambda j, pr: (j, 0, 0))
    elif kind == "row":
        full = pl.BlockSpec((half, shard[1]), lambda j, pr: (2 * j + pr[1], 0))
        buf = pl.BlockSpec((None, half, shard[1]), lambda j, pr: (j, 0, 0))
    else:
        full = pl.BlockSpec((half, shard[1], shard[2]), lambda j, pr: (pr[1], j, 0))
        buf = pl.BlockSpec((None, half, shard[1], shard[2]), lambda j, pr: (j, 0, 0, 0))
    return full, buf


def _pair_sum(name, grad, recv, place):
    kind, shard = BIG_PIECES[name]
    full, buf = _half_specs(kind, shard)

    def body(pr, g_ref, r_ref, o_ref):
        o_ref[...] = (g_ref[...] + r_ref[...]).astype(BF16)

    return pl.pallas_call(
        body,
        name="pair_sum_" + name,
        grid_spec=pltpu.PrefetchScalarGridSpec(num_scalar_prefetch=1, grid=(4,), in_specs=[full, buf], out_specs=buf),
        out_shape=jax.ShapeDtypeStruct((4,) + _half_shape(kind, shard), BF16),
        compiler_params=_params(("arbitrary",)),
    )(place, grad, recv)


def _chip_exchange_rider(sums, names):
    n = len(names)

    def build(ins, outs, sems):
        send_sem, recv_sem = sems
        _, _, c, s = _place()
        starts, waits = [], []
        for p in range(n):
            for k in (1, 2, 3):
                s2, dev = _other_chip(s, c, k)
                cp = _remote(ins[p].at[s2], outs[p].at[k - 1], send_sem.at[p, k - 1], recv_sem.at[p, k - 1], dev)
                starts.append(cp)
                waits.append(cp.wait_send)
                waits.append(_remote(outs[p].at[k - 1], outs[p].at[k - 1], send_sem.at[p, k - 1], recv_sem.at[p, k - 1],
                                     dev).wait_recv)
        return starts, waits

    return _Rider(
        [sums[name] for name in names],
        [jax.ShapeDtypeStruct((3,) + _half_shape(*BIG_PIECES[name]), BF16) for name in names],
        [pltpu.SemaphoreType.DMA((n, 3))] * 2,
        build)


def _chip_sum(name, grad, recv_pair, recv_chip, place):
    kind, shard = BIG_PIECES[name]
    half = shard[0] // 2
    tail = tuple(shard[1:])
    zeros = (0,) * len(tail)
    if kind == "col":
        full = pl.BlockSpec((half,) + tail, lambda j, pr: (pr[1], pr[0]))
    elif kind == "row":
        full = pl.BlockSpec((half,) + tail, lambda j, pr: (2 * pr[0] + pr[1], 0))
    else:
        full = pl.BlockSpec((half,) + tail, lambda j, pr: (pr[1], pr[0], 0))
    pair = pl.BlockSpec((None, half) + tail, lambda j, pr: (pr[0], 0) + zeros)
    chip = pl.BlockSpec((3, half) + tail, lambda j, pr: (0, 0) + zeros)
    out = pl.BlockSpec((half,) + tail, lambda j, pr: (pr[1],) + zeros)

    def body(pr, g_ref, rp_ref, rc_ref, o_ref):
        total = g_ref[...] + rp_ref[...]
        for k in range(3):
            total = total + rc_ref[k].astype(F32)
        o_ref[...] = total

    return pl.pallas_call(
        body,
        name="chip_sum_" + name,
        grid_spec=pltpu.PrefetchScalarGridSpec(num_scalar_prefetch=1, grid=(1,), in_specs=[full, pair, chip], out_specs=out),
        out_shape=jax.ShapeDtypeStruct(shard, F32),
        compiler_params=_params(("arbitrary",)),
    )(place, grad, recv_pair, recv_chip)


def _sibling_exchange_rider(halves, names):
    n = len(names)

    def build(ins, outs, sems):
        send_sem, recv_sem = sems
        x, y, c, _ = _place()
        sibling = (x, y, 1 - c)
        starts, waits = [], []
        for p, name in enumerate(names):
            shard = BIG_PIECES[name][1]
            mine = _shard_region(outs[p], shard, c)
            theirs = _shard_region(outs[p], shard, 1 - c)
            cp = _remote(mine, mine, send_sem.at[p], recv_sem.at[p], sibling)
            starts.append(cp)
            waits.append(cp.wait_send)
            waits.append(_remote(theirs, theirs, send_sem.at[p], recv_sem.at[p], sibling).wait_recv)
        return starts, waits

    return _Rider(
        [halves[name] for name in names],
        [jax.ShapeDtypeStruct(BIG_PIECES[name][1], F32) for name in names],
        [pltpu.SemaphoreType.DMA((n,))] * 2,
        build, aliased=True)


FIRST_WEIGHTS = ["meta_tokens", "conv_w"]
WEIGHT_GROUPS = {
    "lru": ["lru_wa", "lru_wx"],
    "branch": ["w_branch_ret", "w_branch_lru", "w_out"],
    "ffn_in": ["w_ffn_in"],
    "ffn_out": ["w_ffn_out"],
}
WEIGHT_SCHEDULE = {
    "in_proj": [("gather", "lru")],
    "retention_fwd": [("forward", "lru"), ("gather", "branch")],
    "lru_fwd": [("forward", "branch"), ("gather", "ffn_in")],
    "mix_fwd": [("forward", "ffn_in"), ("gather", "ffn_out")],
    "ffn_fwd_gate": [("forward", "ffn_out")],
}
GRAD_GROUPS = {
    "ffn": ["w_ffn_in", "w_ffn_out"],
    "mixer": ["w_out", "w_branch_ret", "w_branch_lru", "lru_wa", "lru_wx"],
    "in": ["w_in"],
}
GRAD_SCHEDULE = {
    "mix_bwd": [("pair", "ffn")],
    "retention_bwd": [("chip", "ffn")],
    "lru_bwd": [("sibling", "ffn")],
    "dw_in": [("pair", "mixer")],
    "in_proj_bwd_0": [("chip", "mixer"), ("pair", "in")],
    "in_proj_bwd_1": [("sibling", "mixer"), ("chip", "in")],
}


class _CommPlan:
    def __init__(self, shards, place):
        self.shards, self.place = shards, place
        self.late = {}
        self.recv_pair, self.sums, self.recv_chip, self.halves, self.final = {}, {}, {}, {}, {}

    def _grad_rider(self, stage, group, grads):
        names = GRAD_GROUPS[group]
        if stage == "pair":
            return _pair_exchange_rider(grads, names)
        if stage == "chip":
            return _chip_exchange_rider(self.sums, names)
        return _sibling_exchange_rider(self.halves, names)

    def _grad_after(self, stage, group, outs, grads):
        names = GRAD_GROUPS[group]
        if stage == "pair":
            for n, o in zip(names, outs):
                self.recv_pair[n] = o
                self.sums[n] = _pair_sum(n, grads[n], o, self.place)
        elif stage == "chip":
            for n, o in zip(names, outs):
                self.halves[n] = _chip_sum(n, grads[n], self.recv_pair[n], o, self.place)
        else:
            self.final.update(zip(names, outs))

    def riders(self, host, w, grads):
        if host in WEIGHT_SCHEDULE:
            return [_gather_rider(self.shards, WEIGHT_GROUPS[group]) if stage == "gather"
                    else _forward_rider(self.late, WEIGHT_GROUPS[group]) for stage, group in WEIGHT_SCHEDULE[host]]
        return [self._grad_rider(stage, group, grads) for stage, group in GRAD_SCHEDULE.get(host, [])]

    def after(self, host, rider_outs, w, grads):
        for (stage, group), outs in zip(WEIGHT_SCHEDULE.get(host, []), rider_outs):
            (self.late if stage == "gather" else w).update(zip(WEIGHT_GROUPS[group], outs))
        for (stage, group), outs in zip(GRAD_SCHEDULE.get(host, []), rider_outs):
            self._grad_after(stage, group, outs, grads)

    def finish(self):
        (outs,) = _run_riders("grad_tail_exchange", [_sibling_exchange_rider(self.halves, GRAD_GROUPS["in"])])
        self.final.update(zip(GRAD_GROUPS["in"], outs))
        return self.final


def _adamw_math(w, g, m, v):
    m = ADAM_B1 * m + (1.0 - ADAM_B1) * g
    v = ADAM_B2 * v + (1.0 - ADAM_B2) * (g * g)
    m_hat = m / (1.0 - ADAM_B1 ** ADAM_STEP)
    v_hat = v / (1.0 - ADAM_B2 ** ADAM_STEP)
    delta = -ADAM_LR * (m_hat / (jnp.sqrt(v_hat) + ADAM_EPS) + ADAM_WD * w)
    return delta, m, v


def _adamw(name, g, w, m, v):
    rows, cols = g.shape
    tr = rows // 4 if rows % 32 == 0 else rows

    def body(g_ref, w_ref, m_ref, v_ref, go_ref, d_ref, mo_ref, vo_ref):
        gv = g_ref[...]
        delta, m2, v2 = _adamw_math(w_ref[...], gv, m_ref[...], v_ref[...])
        go_ref[...] = gv
        d_ref[...] = delta
        mo_ref[...] = m2
        vo_ref[...] = v2

    spec = pl.BlockSpec((tr, cols), lambda i: (i, 0))
    return pl.pallas_call(
        body,
        name="adamw_" + name,
        grid=(rows // tr,),
        in_specs=[spec] * 4,
        out_specs=[spec] * 4,
        out_shape=[jax.ShapeDtypeStruct((rows, cols), F32)] * 4,
        compiler_params=_params(("arbitrary",)),
    )(g, w, m, v)


SMALL_ROWS = 48
VEC_ROWS = {"final_norm_w": 1, "ffn_norm_w": 8, "mix_norm_w": 16, "conv_b": 24, "lru_ba": 25, "lru_bx": 26, "lru_lambda": 27}
VEC_NAMES = list(VEC_ROWS)
CONV_W_ROW = 28
META_ROW = 32


def _small_all_reduce(partial):
    def body(in_ref, out_ref, buf, local_sem, send_sem, recv_sem):
        x, y, c, s = _place()
        me = 2 * s + c
        cp = pltpu.make_async_copy(in_ref, buf.at[me], local_sem)
        cp.start()
        started = []
        for k in range(1, 8):
            peer = jnp.bitwise_xor(me, k)
            dev = (peer // 4, (peer // 2) % 2, peer % 2)
            rd = _remote(in_ref, buf.at[me], send_sem.at[k - 1], recv_sem.at[k - 1], dev)
            rd.start()
            started.append(rd)
        for k in range(1, 8):
            peer = jnp.bitwise_xor(me, k)
            dev = (peer // 4, (peer // 2) % 2, peer % 2)
            _remote(in_ref, buf.at[peer], send_sem.at[k - 1], recv_sem.at[k - 1], dev).wait_recv()
        for rd in started:
            rd.wait_send()
        cp.wait()
        total = buf[0]
        for d in range(1, 8):
            total = total + buf[d]
        out_ref[...] = total

    return pl.pallas_call(
        body,
        name="small_all_reduce",
        in_specs=[pl.BlockSpec(memory_space=pltpu.VMEM)],
        out_specs=pl.BlockSpec(memory_space=pltpu.VMEM),
        out_shape=jax.ShapeDtypeStruct((SMALL_ROWS, D), F32),
        scratch_shapes=[pltpu.VMEM((8, SMALL_ROWS, D), F32), pltpu.SemaphoreType.DMA,
                        pltpu.SemaphoreType.DMA((7,)), pltpu.SemaphoreType.DMA((7,))],
    )(partial)


def _small_update(total, place, vecs, conv, meta):
    nvec = len(VEC_NAMES)
    qcols = D // 4

    def body(pr, tot_ref, col_ref, *refs):
        vec_refs = refs[:3 * nvec]
        conv_refs = refs[3 * nvec:3 * nvec + 3]
        meta_refs = refs[3 * nvec + 3:3 * nvec + 6]
        outs = refs[3 * nvec + 6:]
        loss_ref, vec_out, conv_out, meta_out = outs[0], outs[1:5], outs[5:9], outs[9:13]
        loss_ref[...] = jnp.sum(tot_ref[0:1, :], axis=1, keepdims=True)
        for o in vec_out:
            o[...] = jnp.zeros_like(o)
        for j, name in enumerate(VEC_NAMES):
            w, m, v = (r[...] for r in vec_refs[3 * j:3 * j + 3])
            g = tot_ref[VEC_ROWS[name]:VEC_ROWS[name] + 1, :]
            if name == "lru_lambda":
                g = -g / (1.0 + jnp.exp(w))
            for o, val in zip(vec_out, (g,) + _adamw_math(w, g, m, v)):
                o[j:j + 1, :] = val
        for row, n_rows, ins, group in ((CONV_W_ROW, 4, conv_refs, conv_out), (META_ROW, N_META, meta_refs, meta_out)):
            g = col_ref[row:row + n_rows, :]
            w, m, v = (r[...] for r in ins)
            for o, val in zip(group, (g,) + _adamw_math(w, g, m, v)):
                o[...] = val

    whole = lambda shape: pl.BlockSpec(shape, lambda i, pr: (0,) * len(shape))
    in_specs = [whole((SMALL_ROWS, D)), pl.BlockSpec((SMALL_ROWS, qcols), lambda i, pr: (0, pr[0]))]
    in_specs += [whole((1, D))] * (3 * nvec) + [whole((4, qcols))] * 3 + [whole((N_META, qcols))] * 3
    out_shapes = [(1, 1)] + [(8, D)] * 4 + [(4, qcols)] * 4 + [(N_META, qcols)] * 4
    return pl.pallas_call(
        body,
        name="small_update",
        grid_spec=pltpu.PrefetchScalarGridSpec(num_scalar_prefetch=1, grid=(1,), in_specs=in_specs,
                                               out_specs=[whole(s) for s in out_shapes]),
        out_shape=[jax.ShapeDtypeStruct(s, F32) for s in out_shapes],
        compiler_params=_params(("arbitrary",)),
    )(place, total, total, *[a for t in vecs for a in t], *conv, *meta)


WEIGHT_ORDER = ["meta_tokens", "mix_norm_w", "w_in", "conv_w", "conv_b", "lru_wa", "lru_ba", "lru_wx", "lru_bx", "lru_lambda",
                "w_branch_ret", "w_branch_lru", "w_out", "ffn_norm_w", "w_ffn_in", "w_ffn_out", "final_norm_w"]


def kernel(x, meta_tokens, mix_norm_w, w_in, conv_w, conv_b, lru_wa, lru_ba, lru_wx, lru_bx, lru_lambda, w_branch_ret, w_branch_lru, w_out, ffn_norm_w, w_ffn_in, w_ffn_out, final_norm_w, loss_target, m_meta_tokens, m_mix_norm_w, m_w_in, m_conv_w, m_conv_b, m_lru_wa, m_lru_ba, m_lru_wx, m_lru_bx, m_lru_lambda, m_w_branch_ret, m_w_branch_lru, m_w_out, m_ffn_norm_w, m_w_ffn_in, m_w_ffn_out, m_final_norm_w, v_meta_tokens, v_mix_norm_w, v_w_in, v_conv_w, v_conv_b, v_lru_wa, v_lru_ba, v_lru_wx, v_lru_bx, v_lru_lambda, v_w_branch_ret, v_w_branch_lru, v_w_out, v_ffn_norm_w, v_w_ffn_in, v_w_ffn_out, v_final_norm_w):
    args = locals()
    wts = {n: args[n] for n in WEIGHT_ORDER}
    mom = {n: args["m_" + n] for n in WEIGHT_ORDER}
    var = {n: args["v_" + n] for n in WEIGHT_ORDER}
    place = jnp.stack([2 * lax.axis_index("x") + lax.axis_index("y"), lax.axis_index("c")]).astype(jnp.int32)

    shards = {n: wts[n][0].astype(BF16) for n in BIG_PIECES}
    shards["meta_tokens"] = wts["meta_tokens"]
    shards["conv_w"] = wts["conv_w"][0]
    plan = _CommPlan(shards, place)
    (first,) = _run_riders("gather_first", [_gather_rider(shards, FIRST_WEIGHTS)])
    w = dict(zip(FIRST_WEIGHTS, first))
    for n in VEC_NAMES:
        w[n] = wts[n].reshape(1, D)
    chips = jnp.bitwise_xor(place[0], jnp.arange(4, dtype=jnp.int32))
    w["route"] = jnp.concatenate([place, jnp.stack([2 * chips, 2 * chips + 1], axis=1).reshape(8)])
    w["w_in_shard"] = shards["w_in"]

    grad_x, grad_head, _, stats = _local_step(x[0], loss_target[0], w, plan)
    shard_grads = plan.finish()

    out = {}
    for n in BIG_PIECES:
        shape2d = (-1, wts[n].shape[-1])
        res = _adamw(n, *[a.reshape(shape2d) for a in (shard_grads[n], wts[n], mom[n], var[n])])
        out[n] = [r.reshape(wts[n].shape) for r in res]

    partial = jnp.concatenate(stats + [grad_head[PAD_ROWS:]], axis=0)
    total = _small_all_reduce(partial)
    vecs = [tuple(a[n].reshape(1, D) for a in (wts, mom, var)) for n in VEC_NAMES]
    conv = tuple(a["conv_w"][0] for a in (wts, mom, var))
    meta = tuple(a["meta_tokens"] for a in (wts, mom, var))
    res = _small_update(total, place, vecs, conv, meta)
    loss = res[0].reshape(())
    for j, n in enumerate(VEC_NAMES):
        out[n] = [r[j].reshape(wts[n].shape) for r in res[1:5]]
    out["conv_w"] = [r.reshape(wts["conv_w"].shape) for r in res[5:9]]
    out["meta_tokens"] = list(res[9:13])

    return (loss, grad_x.reshape(x.shape)) + tuple(out[n][kind] for kind in range(4) for n in WEIGHT_ORDER)
```

```python
import functools
import math

import jax
import jax.numpy as jnp
from jax import lax
from jax.experimental import pallas as pl
from jax.experimental.pallas import tpu as pltpu

F32 = jnp.float32
BF16 = jnp.bfloat16

D = 1024
HEADS = 8
DH = 128
CHUNK = 384
N_META = 16
FRONT = 256
PAD_ROWS = FRONT - N_META
LRU_BLOCKS = 4
LRU_BLOCK = 256
LRU_C = 8.0
FFN = 2816
FFN_HALF = FFN // 2
IN_COLS = 8 * D
ROPE_BASE = 10000.0
EPS = 1e-6
QK_SCALE = DH ** -0.5

ADAM_LR = 0.001
ADAM_B1 = 0.9
ADAM_B2 = 0.999
ADAM_EPS = 1e-08
ADAM_WD = 0.01
ADAM_STEP = 10

TM = 256
TM_HEAVY = 768
FFN_BLOCK = FFN // 2
FFN_SUB = 768
TM_MIX_BWD = 256
TM_LRU = 384
VMEM_LIMIT = 60 * 1024 * 1024
TN_VMEM_BUDGET = 40 * 1024 * 1024

NT_DIMS = (((1,), (1,)), ((), ()))
TN_DIMS = (((0,), (0,)), ((), ()))
MESH = pl.DeviceIdType.MESH


def _params(sem=None):
    if sem is None:
        return pltpu.CompilerParams(vmem_limit_bytes=VMEM_LIMIT)
    return pltpu.CompilerParams(dimension_semantics=sem, vmem_limit_bytes=VMEM_LIMIT)


def _dot(a, b):
    return jnp.dot(a, b, preferred_element_type=F32)


def _dot_nt(a, b):
    return lax.dot_general(a, b, NT_DIMS, preferred_element_type=F32)


def _dot_tn(a, b):
    return lax.dot_general(a, b, TN_DIMS, preferred_element_type=F32)


def _sigmoid(z):
    return 1.0 / (1.0 + jnp.exp(-z))


def _log1p(x):
    return jnp.where(x < 1e-3, x * (1.0 - x * (0.5 - x * (1.0 / 3.0))), jnp.log(1.0 + x))


def _softplus(x):
    return jnp.maximum(x, 0.0) + _log1p(jnp.exp(-jnp.abs(x)))


def _neg_expm1(x):
    series = -x * (1.0 + x * (0.5 + x * (1.0 / 6.0 + x * (1.0 / 24.0 + x * (1.0 / 120.0)))))
    return jnp.where(x > -0.05, series, 1.0 - jnp.exp(x))


_GELU_K = math.sqrt(2.0 / math.pi)


def _gelu_and_grad(x):
    inner = _GELU_K * (x + 0.044715 * x * x * x)
    t = jnp.tanh(inner)
    val = 0.5 * x * (1.0 + t)
    grad = 0.5 * (1.0 + t) + 0.5 * x * (1.0 - t * t) * _GELU_K * (1.0 + 3.0 * 0.044715 * x * x)
    return val, grad


def _row_ids(i, rows, shape):
    return i * rows + lax.broadcasted_iota(jnp.int32, shape, 0)


def _rope_tables(rows):
    pos = jnp.arange(rows, dtype=jnp.int32) - PAD_ROWS
    inv_freq = ROPE_BASE ** (-jnp.arange(0, DH, 2, dtype=F32) / DH)
    ang = pos.astype(F32)[:, None] * inv_freq[None, :]
    cos, sin = jnp.cos(ang), jnp.sin(ang)
    return jnp.concatenate([cos, cos], axis=1), jnp.concatenate([-sin, sin], axis=1)


def _decay_tables():
    log_g = jnp.log(1.0 - 2.0 ** (-5.0 - jnp.arange(HEADS, dtype=F32)))
    idx = jnp.arange(CHUNK, dtype=F32)
    diff = idx[:, None] - idx[None, :]
    intra = jnp.where(diff[None] >= 0, jnp.exp(jnp.maximum(diff, 0.0)[None] * log_g[:, None, None]), 0.0)
    q_decay = jnp.exp((idx + 1.0)[:, None] * log_g[None, :])
    k_decay = jnp.exp((CHUNK - 1.0 - idx)[:, None] * log_g[None, :])
    chunk_decay = jnp.exp(CHUNK * log_g)
    wide = lambda a: jnp.repeat(a, DH, axis=-1)
    return intra, jnp.swapaxes(intra, 1, 2), wide(q_decay), wide(k_decay), wide(chunk_decay[None, :])


def _norm1(head, x2d, norm_w, riders=()):
    rows = FRONT + x2d.shape[0]

    def body(head_ref, x_ref, nw_ref, u_ref, rstd_ref):
        def norm(hv):
            rs = lax.rsqrt(jnp.mean(hv * hv, axis=-1, keepdims=True) + EPS)
            u_ref[...] = ((hv * rs) * nw_ref[...]).astype(BF16)
            rstd_ref[...] = rs

        @pl.when(pl.program_id(0) == 0)
        def _():
            norm(head_ref[...])

        @pl.when(pl.program_id(0) > 0)
        def _():
            norm(x_ref[...])

    return _hosted_call(
        body,
        name="norm1",
        grid=(rows // TM,),
        in_specs=[
            pl.BlockSpec((FRONT, D), lambda i: (0, 0)),
            pl.BlockSpec((TM, D), lambda i: (jnp.maximum(i - 1, 0), 0)),
            pl.BlockSpec((1, D), lambda i: (0, 0)),
        ],
        out_specs=[pl.BlockSpec((TM, D), lambda i: (i, 0)), pl.BlockSpec((TM, 1), lambda i: (i, 0))],
        out_shape=[jax.ShapeDtypeStruct((rows, D), BF16), jax.ShapeDtypeStruct((rows, 1), F32)],
        scratch_shapes=[],
        args=(head, x2d, norm_w),
        riders=riders,
    )


def _heavy_tile(rows):
    return TM_HEAVY if rows % TM_HEAVY == 0 else TM


def _lru_tile(rows):
    return TM_LRU if rows % TM_LRU == 0 else TM


def _swap_3_4(group):
    return jnp.where(group == 3, 4, jnp.where(group == 4, 3, group))


LRU_IN_COL = 3
GATES_COL = 1


def _in_proj(u, w_shard, route, cos_t, sin_t, riders=()):
    rows = u.shape[0]
    tm = _heavy_tile(rows)
    nt = rows // tm
    kind, shard = BIG_PIECES["w_in"]

    def body(route_ref, u_hbm, wsh_ref, cos_ref, sin_ref, proj_ref, wfull_ref,
             u_sc, w_sc, u_sem, w_sem, local_sem, ici_send, ici_recv, fwd_send, fwd_recv):
        g, i = pl.program_id(0), pl.program_id(1)
        s, c = route_ref[0], route_ref[1]
        gid = route_ref[2 + g]
        sibling = (s // 2, s % 2, 1 - c)
        local = pltpu.make_async_copy(wsh_ref, _full_region(wfull_ref, kind, shard, s, None), local_sem)
        u_copies = [pltpu.make_async_copy(u_hbm.at[pl.ds(t * tm, tm)], u_sc.at[t], u_sem.at[t]) for t in range(nt)]
        half = shard[0] // 2

        def rows_of(h):
            return pl.ds(pl.multiple_of(h * half, half), half)

        def part(chip, h, j):
            return wfull_ref.at[rows_of(h), pl.ds(pl.multiple_of(chip * shard[1] + j * D, D), D)]

        sends, arrivals = [], {}
        for j in (0, 1):
            for k in (1, 2, 3):
                s2, dev = _other_chip(s, c, k)
                q = 2 * (k - 1) + j
                sends.append(_remote(wsh_ref.at[rows_of(c), pl.ds(j * D, D)], part(s, c, j), ici_send.at[q],
                                     ici_recv.at[q], dev))
                mine, theirs = part(s2, c, j), part(s2, 1 - c, j)
                arrivals[k, j] = (_remote(mine, mine, ici_send.at[q], ici_recv.at[q], dev),
                                  _remote(mine, mine, fwd_send.at[q], fwd_recv.at[q], sibling),
                                  _remote(theirs, theirs, fwd_send.at[q], fwd_recv.at[q], sibling))

        slot = g % 2
        last_tile = i == nt - 1

        def block_copy(src, col, to_slot):
            return pltpu.make_async_copy(src.at[:, pl.ds(pl.multiple_of(col * D, D), D)], w_sc.at[to_slot],
                                         w_sem.at[to_slot])

        @pl.when(jnp.logical_and(g == 0, i == 0))
        def _():
            for cp in sends:
                cp.start()
            local.start()
            for cp in u_copies:
                cp.start()
            cp = block_copy(wsh_ref, 0, 0)
            cp.start()
            cp.wait()

        @pl.when(jnp.logical_and(last_tile, g == 0))
        def _():
            block_copy(wsh_ref, 1, 1).start()

        for k in (1, 2, 3):
            for j in (0, 1):
                arrived, forward, forwarded = arrivals[k, j]

                @pl.when(jnp.logical_and(last_tile, g == 2 * k - 1 + j))
                def _():
                    arrived.wait_recv()
                    forward.start()
                    forwarded.wait_recv()
                    block_copy(wfull_ref, route_ref[2 + 2 * k + j], j).start()

        @pl.when(jnp.logical_and(i == 0, g > 0))
        def _():
            block_copy(wfull_ref, 0, slot).wait()

        for t in range(nt):
            @pl.when(jnp.logical_and(g == 0, i == t))
            def _():
                u_copies[t].wait()

        acc = _dot(u_sc[i], w_sc[slot])

        @pl.when(gid < 2)
        def _():
            scale = jnp.where(gid == 1, QK_SCALE, 1.0).astype(F32)
            for h in range(HEADS):
                sl = slice(h * DH, (h + 1) * DH)
                blk = acc[:, sl]
                out = (blk * cos_ref[...] + pltpu.roll(blk, DH // 2, axis=1) * sin_ref[...]) * scale
                proj_ref[:, sl] = out.astype(BF16)

        @pl.when(gid >= 2)
        def _():
            proj_ref[...] = acc.astype(BF16)

        @pl.when(jnp.logical_and(g == 7, i == nt - 1))
        def _():
            local.wait()
            for cp in sends:
                cp.wait_send()
            for _, forward, _ in arrivals.values():
                forward.wait_send()

    return _hosted_call(
        body,
        name="in_proj",
        grid=(8, nt),
        in_specs=[
            _ANY, _ANY,
            pl.BlockSpec((tm, DH), lambda g, i, rt: (i, 0)),
            pl.BlockSpec((tm, DH), lambda g, i, rt: (i, 0)),
        ],
        out_specs=[pl.BlockSpec((tm, D), lambda g, i, rt: (i, _swap_3_4(rt[2 + g]))), _ANY],
        out_shape=[jax.ShapeDtypeStruct((rows, IN_COLS), BF16), jax.ShapeDtypeStruct((D, IN_COLS), BF16)],
        scratch_shapes=[
            pltpu.VMEM((nt, tm, D), BF16), pltpu.VMEM((2, D, D), BF16),
            pltpu.SemaphoreType.DMA((nt,)), pltpu.SemaphoreType.DMA((2,)), pltpu.SemaphoreType.DMA,
            pltpu.SemaphoreType.DMA((6,)), pltpu.SemaphoreType.DMA((6,)),
            pltpu.SemaphoreType.DMA((6,)), pltpu.SemaphoreType.DMA((6,)),
        ],
        args=(u, w_shard, cos_t, sin_t),
        riders=riders,
        prefetch=route,
        riders_after_body=True,
    )


def _retention_fwd(proj_bf, intra, q_dec, k_dec, c_dec, riders=()):
    rows = proj_bf.shape[0]
    nc = rows // CHUNK

    def body(q_ref, k_ref, v_ref, m_ref, qd_ref, kd_ref, cd_ref, o_ref, sprev_ref, s_sc):
        @pl.when(pl.program_id(0) == 0)
        def _():
            s_sc[...] = jnp.zeros_like(s_sc)

        for h in range(HEADS):
            sl = slice(h * DH, (h + 1) * DH)
            q, k, v = q_ref[:, sl], k_ref[:, sl], v_ref[:, sl]
            state = s_sc[h]
            state_b = state.astype(BF16)
            sprev_ref[0, h] = state_b
            s = _dot_nt(q, k) * m_ref[h]
            inner = _dot(s.astype(BF16), v)
            cross = _dot(q, state_b) * qd_ref[:, sl]
            o_ref[:, sl] = (inner + cross).astype(BF16)
            k_scaled = (k.astype(F32) * kd_ref[:, sl]).astype(BF16)
            s_sc[h] = state * cd_ref[:, sl] + _dot_tn(k_scaled, v)

    chunk_spec = lambda col: pl.BlockSpec((CHUNK, D), lambda c: (c, col))
    const2 = lambda shape: pl.BlockSpec(shape, lambda c: (0, 0))
    return _hosted_call(
        body,
        name="retention_fwd",
        grid=(nc,),
        in_specs=[
            chunk_spec(0), chunk_spec(1), chunk_spec(2),
            pl.BlockSpec((HEADS, CHUNK, CHUNK), lambda c: (0, 0, 0)),
            const2((CHUNK, D)), const2((CHUNK, D)), const2((1, D)),
        ],
        out_specs=[
            pl.BlockSpec((CHUNK, D), lambda c: (c, 0)),
            pl.BlockSpec((1, HEADS, DH, DH), lambda c: (c, 0, 0, 0)),
        ],
        out_shape=[
            jax.ShapeDtypeStruct((rows, D), BF16),
            jax.ShapeDtypeStruct((nc, HEADS, DH, DH), BF16),
        ],
        scratch_shapes=[pltpu.VMEM((HEADS, DH, DH), F32)],
        args=(proj_bf, proj_bf, proj_bf, intra, q_dec, k_dec, c_dec),
        riders=riders,
    )


def _shift_down(x, first8_prev, d):
    rolled = pltpu.roll(x, d, axis=0)
    head = pltpu.roll(jnp.concatenate([first8_prev, x[0:8]], axis=0), d, axis=0)[8:16]
    return rolled, head


def _conv_and_gates(x, prev8, cw_ref, cb_ref, wa_ref, wx_ref, ba_ref, bx_ref, lam_ref, c_sc):
    cw = cw_ref[...]
    conv = cb_ref[...] + cw[3:4] * x
    head = cb_ref[...] + cw[3:4] * x[0:8]
    for d in (1, 2, 3):
        rolled, hd = _shift_down(x, prev8, d)
        conv = conv + cw[3 - d:4 - d] * rolled
        head = head + cw[3 - d:4 - d] * hd
    c_sc[...] = conv
    c_sc[0:8, :] = head
    c = c_sc[...]
    zr, zi = [], []
    for g in range(LRU_BLOCKS):
        sl = slice(g * LRU_BLOCK, (g + 1) * LRU_BLOCK)
        cg = c[:, sl].astype(BF16)
        zr.append(_dot(cg, wa_ref[g]))
        zi.append(_dot(cg, wx_ref[g]))
    r = _sigmoid(jnp.concatenate(zr, axis=1) + ba_ref[...])
    gate_i = _sigmoid(jnp.concatenate(zi, axis=1) + bx_ref[...])
    sp = _softplus(-lam_ref[...])
    log_a = (-LRU_C) * r * sp
    a = jnp.exp(log_a)
    mult = jnp.sqrt(_neg_expm1(2.0 * log_a))
    return c, r, gate_i, a, mult, log_a


def _lru_fwd(proj, conv_w, conv_b, wa, wx, ba, bx, lam, riders=()):
    rows = proj.shape[0]
    TM = _lru_tile(rows)
    nt = rows // TM

    def body(x_ref, cw_ref, cb_ref, wa_ref, wx_ref, ba_ref, bx_ref, lam_ref,
             h_ref, c_ref, r_ref, i_ref, la_ref, mult_ref, prev_sc, carry_sc, c_sc, a_sc, u_sc, h_sc):
        i = pl.program_id(0)

        @pl.when(i == 0)
        def _():
            prev_sc[...] = jnp.zeros_like(prev_sc)
            carry_sc[...] = jnp.zeros_like(carry_sc)

        x = x_ref[...].astype(F32)
        c, r, gate_i, a, mult, log_a = _conv_and_gates(x, prev_sc[...], cw_ref, cb_ref, wa_ref, wx_ref, ba_ref, bx_ref,
                                                       lam_ref, c_sc)
        for ref, val in ((c_ref, c), (r_ref, r), (i_ref, gate_i), (la_ref, log_a), (mult_ref, mult)):
            ref[...] = val.astype(BF16)
        prev_sc[...] = x[TM - 8:TM]
        valid = _row_ids(i, TM, (TM, D)) >= PAD_ROWS
        a_sc[...] = a
        u_sc[...] = jnp.where(valid, mult * gate_i * c, 0.0)
        row8 = lax.broadcasted_iota(jnp.int32, (8, D), 0)

        def group(gi, hprev):
            r0 = pl.multiple_of(gi * 8, 8)
            aa = a_sc[pl.ds(r0, 8), :]
            uu = u_sc[pl.ds(r0, 8), :]
            for d in (1, 2, 4):
                a_sh = jnp.where(row8 >= d, pltpu.roll(aa, d, axis=0), 1.0)
                u_sh = jnp.where(row8 >= d, pltpu.roll(uu, d, axis=0), 0.0)
                uu = uu + aa * u_sh
                aa = aa * a_sh
            hb = aa * hprev + uu
            h_sc[pl.ds(r0, 8), :] = hb
            return hb[7:8, :]

        hlast = lax.fori_loop(0, TM // 8, group, carry_sc[0:1, :])
        carry_sc[0:1, :] = hlast
        h_ref[...] = h_sc[...].astype(BF16)

    vec = pl.BlockSpec((1, D), lambda i: (0, 0))
    wspec = pl.BlockSpec((LRU_BLOCKS, LRU_BLOCK, LRU_BLOCK), lambda i: (0, 0, 0))
    return _hosted_call(
        body,
        name="lru_fwd",
        grid=(nt,),
        in_specs=[
            pl.BlockSpec((TM, D), lambda i: (i, LRU_IN_COL)),
            pl.BlockSpec((4, D), lambda i: (0, 0)),
            vec, wspec, wspec, vec, vec, vec,
        ],
        out_specs=[pl.BlockSpec((TM, D), lambda i: (i, 0)) for _ in range(6)],
        out_shape=[jax.ShapeDtypeStruct((rows, D), BF16) for _ in range(6)],
        scratch_shapes=[
            pltpu.VMEM((8, D), F32), pltpu.VMEM((8, D), F32),
            pltpu.VMEM((TM, D), F32), pltpu.VMEM((TM, D), F32), pltpu.VMEM((TM, D), F32), pltpu.VMEM((TM, D), F32),
        ],
        args=(proj, conv_w, conv_b, wa, wx, ba, bx, lam),
        riders=riders,
    )


def _group_norm(o):
    outs, rstds = [], []
    for h in range(HEADS):
        oh = o[:, h * DH:(h + 1) * DH]
        rs = lax.rsqrt(jnp.mean(oh * oh, axis=-1, keepdims=True) + EPS)
        outs.append(oh * rs)
        rstds.append(rs)
    return jnp.concatenate(outs, axis=1), rstds


def _mix_fwd(head, x2d, o, proj, h_lru, w_br, w_bl, w_o, ffn_norm_w, riders=()):
    rows = o.shape[0]
    nt = rows // TM

    def body(head_ref, x_ref, o_ref, gates_ref, hl_ref, wbr_ref, wbl_ref, wo_ref, nw_ref,
             yret_ref, ylru_ref, h1_ref, u2_ref, rstd_ref):
        i = pl.program_id(0)
        gate = lambda j: gates_ref[:, j * D:(j + 1) * D].astype(F32)
        on, _ = _group_norm(o_ref[...].astype(F32))
        gret = gate(0)
        a_ret = (gret * _sigmoid(gret) * on).astype(BF16)
        y_ret = _dot(a_ret, wbr_ref[...])
        gl, _ = _gelu_and_grad(gate(1))
        a_lru = (gl * hl_ref[...].astype(F32)).astype(BF16)
        y_lru = _dot(a_lru, wbl_ref[...])
        mixed = (_sigmoid(gate(2)) * y_ret + _sigmoid(gate(3)) * y_lru).astype(BF16)
        delta = _dot(mixed, wo_ref[...])
        yret_ref[...] = y_ret.astype(BF16)
        ylru_ref[...] = y_lru.astype(BF16)

        def finish(h0):
            h1 = h0 + delta
            rs = lax.rsqrt(jnp.mean(h1 * h1, axis=-1, keepdims=True) + EPS)
            h1_ref[...] = h1
            u2_ref[...] = ((h1 * rs) * nw_ref[...]).astype(BF16)
            rstd_ref[...] = rs

        @pl.when(i == 0)
        def _():
            finish(head_ref[...])

        @pl.when(i > 0)
        def _():
            finish(x_ref[...])

    tile = lambda col: pl.BlockSpec((TM, D), lambda i: (i, col))
    wspec = pl.BlockSpec((D, D), lambda i: (0, 0))
    return _hosted_call(
        body,
        name="mix_fwd",
        grid=(nt,),
        in_specs=[
            pl.BlockSpec((FRONT, D), lambda i: (0, 0)),
            pl.BlockSpec((TM, D), lambda i: (jnp.maximum(i - 1, 0), 0)),
            tile(0), pl.BlockSpec((TM, 4 * D), lambda i: (i, GATES_COL)), tile(0),
            wspec, wspec, wspec,
            pl.BlockSpec((1, D), lambda i: (0, 0)),
        ],
        out_specs=[tile(0), tile(0), tile(0), tile(0), pl.BlockSpec((TM, 1), lambda i: (i, 0))],
        out_shape=[
            jax.ShapeDtypeStruct((rows, D), BF16), jax.ShapeDtypeStruct((rows, D), BF16),
            jax.ShapeDtypeStruct((rows, D), F32), jax.ShapeDtypeStruct((rows, D), BF16),
            jax.ShapeDtypeStruct((rows, 1), F32),
        ],
        scratch_shapes=[],
        args=(head, x2d, o, proj, h_lru, w_br, w_bl, w_o, ffn_norm_w),
        riders=riders,
    )


def _ffn_fwd_gate(u2, w_ffn_in, riders=()):
    rows = u2.shape[0]
    tm = _heavy_tile(rows)
    nb = FFN // FFN_BLOCK
    hid = lambda: pl.BlockSpec((tm, FFN_BLOCK), lambda i, j: (i, j))

    def body(u2_ref, w_hbm, silu_ref, dsilu_ref, act_ref, w_sc, w_sem):
        j = pl.program_id(1)

        @pl.when(jnp.logical_and(pl.program_id(0) == 0, j == 0))
        def _():
            cp = pltpu.make_async_copy(w_hbm, w_sc, w_sem)
            cp.start()
            cp.wait()

        u2 = u2_ref[...]
        for a, b in ((0, FFN_SUB), (FFN_SUB, FFN_BLOCK)):
            col = pl.multiple_of(j * FFN_BLOCK + a, 128)
            g = _dot(u2, w_sc[:, pl.ds(col, b - a)])
            up = _dot(u2, w_sc[:, pl.ds(pl.multiple_of(FFN + col, 128), b - a)])
            sg = _sigmoid(g)
            silu = g * sg
            silu_ref[:, a:b] = silu.astype(BF16)
            dsilu_ref[:, a:b] = (up * (sg * (1.0 + g * (1.0 - sg)))).astype(BF16)
            act_ref[:, a:b] = (silu * up).astype(BF16)

    return _hosted_call(
        body,
        name="ffn_fwd_gate",
        grid=(rows // tm, nb),
        in_specs=[pl.BlockSpec((tm, D), lambda i, j: (i, 0)), _ANY],
        out_specs=[hid(), hid(), hid()],
        out_shape=[jax.ShapeDtypeStruct((rows, FFN), BF16)] * 3,
        scratch_shapes=[pltpu.VMEM((D, 2 * FFN), BF16), pltpu.SemaphoreType.DMA],
        args=(u2, w_ffn_in),
        riders=riders,
    )


def _ffn_out_loss(act, h1, w_ffn_out, target, final_norm_w):
    rows = act.shape[0]
    tm = _heavy_tile(rows)
    nt = rows // tm

    def body(act_ref, h1_ref, wo_ref, fnw_ref, tgt_hbm, dh2_ref, stats_ref, tgt_sc, tgt_sem):
        i = pl.program_id(0)
        slot = i % 2

        @pl.when(i == 0)
        def _():
            stats_ref[...] = jnp.zeros_like(stats_ref)
            tgt_sc[0, 0:FRONT, :] = jnp.zeros((FRONT, D), F32)
            cp = pltpu.make_async_copy(tgt_hbm.at[pl.ds(0, tm - FRONT)], tgt_sc.at[0, pl.ds(FRONT, tm - FRONT)], tgt_sem.at[0])
            cp.start()
            cp.wait()

        @pl.when(i + 1 < nt)
        def _():
            start = pl.multiple_of((i + 1) * tm - FRONT, FRONT)
            pltpu.make_async_copy(tgt_hbm.at[pl.ds(start, tm)], tgt_sc.at[1 - slot], tgt_sem.at[1 - slot]).start()

        @pl.when(i > 0)
        def _():
            pltpu.make_async_copy(tgt_hbm.at[pl.ds(0, tm)], tgt_sc.at[slot], tgt_sem.at[slot]).wait()

        h2 = h1_ref[...] + _dot(act_ref[...], wo_ref[...])
        rs = lax.rsqrt(jnp.mean(h2 * h2, axis=-1, keepdims=True) + EPS)
        n = h2 * rs
        fnw = fnw_ref[...]
        valid = _row_ids(i, tm, (tm, D)) >= FRONT
        diff = jnp.where(valid, n * fnw - tgt_sc[slot], 0.0)
        dy = diff * (1.0 / D)
        stats_ref[0:1, :] += (0.5 / D) * jnp.sum(diff * diff, axis=0, keepdims=True)
        stats_ref[1:2, :] += jnp.sum(dy * n, axis=0, keepdims=True)
        dn = dy * fnw
        dh2_ref[...] = rs * (dn - n * jnp.mean(dn * n, axis=-1, keepdims=True))

    return pl.pallas_call(
        body,
        name="ffn_out_loss",
        grid=(nt,),
        in_specs=[
            pl.BlockSpec((tm, FFN), lambda i: (i, 0)),
            pl.BlockSpec((tm, D), lambda i: (i, 0)),
            pl.BlockSpec((FFN, D), lambda i: (0, 0)),
            pl.BlockSpec((1, D), lambda i: (0, 0)),
            _ANY,
        ],
        out_specs=[pl.BlockSpec((tm, D), lambda i: (i, 0)), pl.BlockSpec((8, D), lambda i: (0, 0))],
        out_shape=[jax.ShapeDtypeStruct((rows, D), F32), jax.ShapeDtypeStruct((8, D), F32)],
        scratch_shapes=[pltpu.VMEM((2, tm, D), F32), pltpu.SemaphoreType.DMA((2,))],
        compiler_params=_params(("arbitrary",)),
    )(act, h1, w_ffn_out, final_norm_w, target)


def _ffn_bwd(dh2, silu, dsilu, h1, rstd2, w_ffn_in, w_ffn_out, ffn_norm_w):
    rows = dh2.shape[0]
    tm = _heavy_tile(rows)
    nb = FFN // FFN_BLOCK
    blk = lambda: pl.BlockSpec((tm, FFN_BLOCK), lambda i, j: (i, j))

    def gate_body(dh2_ref, silu_ref, dsilu_ref, wo_hbm, dg_ref, dup_ref, dh2b_ref, wo_sc, wo_sem):
        j = pl.program_id(1)

        @pl.when(jnp.logical_and(pl.program_id(0) == 0, j == 0))
        def _():
            cp = pltpu.make_async_copy(wo_hbm, wo_sc, wo_sem)
            cp.start()
            cp.wait()

        @pl.when(j == 0)
        def _():
            dh2b_ref[...] = dh2_ref[...].astype(BF16)

        dact = _dot_nt(dh2b_ref[...], wo_sc[pl.ds(pl.multiple_of(j * FFN_BLOCK, 128), FFN_BLOCK), :])
        dup_ref[...] = (dact * silu_ref[...].astype(F32)).astype(BF16)
        dg_ref[...] = (dact * dsilu_ref[...].astype(F32)).astype(BF16)

    dg, dup, dh2b = pl.pallas_call(
        gate_body,
        name="ffn_bwd_gate",
        grid=(rows // tm, nb),
        in_specs=[pl.BlockSpec((tm, D), lambda i, j: (i, 0)), blk(), blk(), _ANY],
        out_specs=[blk(), blk(), pl.BlockSpec((tm, D), lambda i, j: (i, 0))],
        out_shape=[jax.ShapeDtypeStruct((rows, FFN), BF16)] * 2 + [jax.ShapeDtypeStruct((rows, D), BF16)],
        scratch_shapes=[pltpu.VMEM((FFN, D), BF16), pltpu.SemaphoreType.DMA],
        compiler_params=_params(("arbitrary", "arbitrary")),
    )(dh2, silu, dsilu, w_ffn_out)

    def body(dg_ref, dup_ref, dh2_ref, h1_ref, rstd_ref, w_hbm, nw_ref, dh1_ref, stats_ref, w_sc, w_sem):
        @pl.when(pl.program_id(0) == 0)
        def _():
            stats_ref[...] = jnp.zeros_like(stats_ref)
            cp = pltpu.make_async_copy(w_hbm, w_sc, w_sem)
            cp.start()
            cp.wait()

        du = _dot_nt(dg_ref[...], w_sc[:, 0:FFN]) + _dot_nt(dup_ref[...], w_sc[:, FFN:2 * FFN])
        rs = rstd_ref[...]
        n = h1_ref[...] * rs
        stats_ref[0:1, :] += jnp.sum(du * n, axis=0, keepdims=True)
        dn = du * nw_ref[...]
        dh1_ref[...] = dh2_ref[...] + rs * (dn - n * jnp.mean(dn * n, axis=-1, keepdims=True))

    row = lambda width: pl.BlockSpec((tm, width), lambda i: (i, 0))
    dh1, stats = pl.pallas_call(
        body,
        name="ffn_bwd_in",
        grid=(rows // tm,),
        in_specs=[row(FFN), row(FFN), row(D), row(D), row(1), _ANY, pl.BlockSpec((1, D), lambda i: (0, 0))],
        out_specs=[row(D), pl.BlockSpec((8, D), lambda i: (0, 0))],
        out_shape=[jax.ShapeDtypeStruct((rows, D), F32), jax.ShapeDtypeStruct((8, D), F32)],
        scratch_shapes=[pltpu.VMEM((D, 2 * FFN), BF16), pltpu.SemaphoreType.DMA],
        compiler_params=_params(("arbitrary",)),
    )(dg, dup, dh2, h1, rstd2, w_ffn_in, ffn_norm_w)
    return dg, dup, dh2b, dh1, stats


def _mix_bwd(dh1, o, proj, h_lru, y_ret, y_lru, w_br, w_bl, w_o, riders=()):
    rows = dh1.shape[0]
    tm = TM_MIX_BWD
    nt = rows // tm

    def body(dh1_ref, o_ref, gates_ref, hl_ref, yret_ref, ylru_ref, wbr_ref, wbl_ref, wo_ref,
             dproj_ref, do_ref, dhl_ref, mixed_ref, aret_ref, alru_ref, dyret_ref, dylru_ref):
        gate = lambda j: gates_ref[:, j * D:(j + 1) * D].astype(F32)
        dmixed = _dot_nt(dh1_ref[...].astype(BF16), wo_ref[...])
        y_ret, y_lru = yret_ref[...].astype(F32), ylru_ref[...].astype(F32)
        sa, sb = _sigmoid(gate(2)), _sigmoid(gate(3))
        mixed_ref[...] = (sa * y_ret + sb * y_lru).astype(BF16)
        dga = dmixed * y_ret * sa * (1.0 - sa)
        dgb = dmixed * y_lru * sb * (1.0 - sb)
        dy_ret = (dmixed * sa).astype(BF16)
        dy_lru = (dmixed * sb).astype(BF16)
        dyret_ref[...] = dy_ret
        dylru_ref[...] = dy_lru
        da_ret = _dot_nt(dy_ret, wbr_ref[...])
        da_lru = _dot_nt(dy_lru, wbl_ref[...])

        gret = gate(0)
        sg = _sigmoid(gret)
        silu = gret * sg
        on, rstds = _group_norm(o_ref[...].astype(F32))
        aret_ref[...] = (silu * on).astype(BF16)
        dgret = da_ret * on * (sg * (1.0 + gret * (1.0 - sg)))
        don = da_ret * silu
        for h in range(HEADS):
            sl = slice(h * DH, (h + 1) * DH)
            onh, donh = on[:, sl], don[:, sl]
            do_ref[:, sl] = (rstds[h] * (donh - onh * jnp.mean(donh * onh, axis=-1, keepdims=True))).astype(BF16)

        gl, gl_grad = _gelu_and_grad(gate(1))
        hl = hl_ref[...].astype(F32)
        alru_ref[...] = (gl * hl).astype(BF16)
        dlgate = da_lru * hl * gl_grad
        dhl_ref[...] = (da_lru * gl).astype(BF16)

        for j, val in enumerate((dgret, dlgate, dga, dgb)):
            dproj_ref[:, j * D:(j + 1) * D] = val.astype(BF16)

    tile = lambda col: pl.BlockSpec((tm, D), lambda i: (i, col))
    gates = pl.BlockSpec((tm, 4 * D), lambda i: (i, GATES_COL))
    wspec = pl.BlockSpec((D, D), lambda i: (0, 0))
    bf = lambda: jax.ShapeDtypeStruct((rows, D), BF16)
    return _hosted_call(
        body,
        name="mix_bwd",
        grid=(nt,),
        in_specs=[tile(0), tile(0), gates, tile(0), tile(0), tile(0), wspec, wspec, wspec],
        out_specs=[pl.BlockSpec((tm, 4 * D), lambda i: (i, GATES_COL))] + [tile(0)] * 7,
        out_shape=[jax.ShapeDtypeStruct((rows, IN_COLS), BF16)] + [bf() for _ in range(7)],
        scratch_shapes=[],
        args=(dh1, o, proj, h_lru, y_ret, y_lru, w_br, w_bl, w_o),
        riders=riders,
    )


def _retention_bwd(dproj, proj_bf, do, sprev, intra, intra_t, q_dec, k_dec, c_dec, cos_t, sin_t, riders=()):
    rows = proj_bf.shape[0]
    nc = rows // CHUNK

    def body(dproj_in_ref, q_ref, k_ref, v_ref, do_ref, sprev_ref, m_ref, mt_ref, qd_ref, kd_ref, cd_ref, cos_ref, sin_ref,
             dproj_ref, ds_sc):
        @pl.when(pl.program_id(0) == 0)
        def _():
            ds_sc[...] = jnp.zeros_like(ds_sc)

        cos, sin = cos_ref[...], sin_ref[...]

        def unrotate(dy):
            return dy * cos - pltpu.roll(dy, DH // 2, axis=1) * sin

        for h in range(HEADS):
            sl = slice(h * DH, (h + 1) * DH)
            q, k, v = q_ref[:, sl], k_ref[:, sl], v_ref[:, sl]
            do = do_ref[:, sl]
            dob = do.astype(BF16)
            doq = (do * qd_ref[:, sl]).astype(BF16)
            state_prev = sprev_ref[0, h]
            dstate = ds_sc[h]
            dstate_b = dstate.astype(BF16)
            s_t = (_dot_nt(k, q) * mt_ref[h]).astype(BF16)
            ds_t = (_dot_nt(v, dob) * mt_ref[h]).astype(BF16)
            ds = (_dot_nt(dob, v) * m_ref[h]).astype(BF16)
            kd = kd_ref[:, sl]
            dq = _dot(ds, k) + _dot_nt(doq, state_prev)
            dk = _dot(ds_t, q) + _dot_nt(v, dstate_b) * kd
            k_scaled = (k.astype(F32) * kd).astype(BF16)
            dv = _dot(s_t, dob) + _dot(k_scaled, dstate_b)
            ds_sc[h] = dstate * cd_ref[:, sl] + _dot_tn(q, doq)
            dproj_ref[:, sl] = unrotate(dq).astype(BF16)
            dproj_ref[:, D + h * DH:D + (h + 1) * DH] = (unrotate(dk) * QK_SCALE).astype(BF16)
            dproj_ref[:, 2 * D + h * DH:2 * D + (h + 1) * DH] = dv.astype(BF16)

    rev = lambda c: nc - 1 - c
    chunk_spec = lambda col: pl.BlockSpec((CHUNK, D), lambda c: (rev(c), col))
    const2 = lambda shape: pl.BlockSpec(shape, lambda c: (0, 0))
    const3 = pl.BlockSpec((HEADS, CHUNK, CHUNK), lambda c: (0, 0, 0))
    return _hosted_call(
        body,
        name="retention_bwd",
        grid=(nc,),
        in_specs=[
            pl.BlockSpec(memory_space=pl.ANY),
            chunk_spec(0), chunk_spec(1), chunk_spec(2), chunk_spec(0),
            pl.BlockSpec((1, HEADS, DH, DH), lambda c: (rev(c), 0, 0, 0)),
            const3, const3,
            const2((CHUNK, D)), const2((CHUNK, D)), const2((1, D)),
            pl.BlockSpec((CHUNK, DH), lambda c: (rev(c), 0)),
            pl.BlockSpec((CHUNK, DH), lambda c: (rev(c), 0)),
        ],
        out_specs=[pl.BlockSpec((CHUNK, 3 * D), lambda c: (rev(c), 0))],
        out_shape=[jax.ShapeDtypeStruct(dproj.shape, BF16)],
        aliases={0: 0},
        scratch_shapes=[pltpu.VMEM((HEADS, DH, DH), F32)],
        args=(dproj, proj_bf, proj_bf, proj_bf, do, sprev, intra, intra_t, q_dec, k_dec, c_dec, cos_t, sin_t),
        riders=riders,
    )


def _lru_bwd(dproj, proj, saved, dhl, conv_w, wa, wx, lam, riders=()):
    rows = proj.shape[0]
    TM = _lru_tile(rows)
    nt = rows // TM
    per8 = TM // 8

    def body(dproj_in_ref, x_ref, xprev_ref, h_ref, hprev_ref, c_ref, r_ref, i_ref, la_ref, mult_ref, dhl_ref,
             cw_ref, wa_ref, wx_ref, lam_ref,
             dproj_ref, dwa_ref, dwx_ref, stats_ref, anext_sc, dhnext_sc, dcnext_sc, c_sc, b_sc, dh_sc):
        step = pl.program_id(0)
        i = nt - 1 - step

        @pl.when(step == 0)
        def _():
            anext_sc[...] = jnp.zeros_like(anext_sc)
            dhnext_sc[...] = jnp.zeros_like(dhnext_sc)
            dcnext_sc[...] = jnp.zeros_like(dcnext_sc)
            dwa_ref[...] = jnp.zeros_like(dwa_ref)
            dwx_ref[...] = jnp.zeros_like(dwx_ref)
            stats_ref[...] = jnp.zeros_like(stats_ref)

        first = i == 0
        x = x_ref[...].astype(F32)
        prev8 = jnp.where(first, 0.0, xprev_ref[8:16, :].astype(F32))
        c_b = c_ref[...]
        c, r, gate_i, mult = (ref[...].astype(F32) for ref in (c_ref, r_ref, i_ref, mult_ref))
        a = jnp.exp(la_ref[...].astype(F32))
        sp = _softplus(-lam_ref[...])
        dh_sc[...] = dhl_ref[...].astype(F32)

        b_sc[...] = pltpu.roll(a, TM - 1, axis=0)
        b_sc[TM - 1:TM, :] = anext_sc[0:1, :]
        anext_sc[0:1, :] = a[0:1, :]
        row8 = lax.broadcasted_iota(jnp.int32, (8, D), 0)

        def group(gi, dhnext):
            r0 = pl.multiple_of((per8 - 1 - gi) * 8, 8)
            bb = b_sc[pl.ds(r0, 8), :]
            uu = dh_sc[pl.ds(r0, 8), :]
            for d in (1, 2, 4):
                b_sh = jnp.where(row8 < 8 - d, pltpu.roll(bb, 8 - d, axis=0), 1.0)
                u_sh = jnp.where(row8 < 8 - d, pltpu.roll(uu, 8 - d, axis=0), 0.0)
                uu = uu + bb * u_sh
                bb = bb * b_sh
            dhb = bb * dhnext + uu
            dh_sc[pl.ds(r0, 8), :] = dhb
            return dhb[0:1, :]

        dhfirst = lax.fori_loop(0, per8, group, dhnext_sc[0:1, :])
        dhnext_sc[0:1, :] = dhfirst
        dh = dh_sc[...]

        h = h_ref[...].astype(F32)
        hprev8 = jnp.where(first, 0.0, hprev_ref[8:16, :].astype(F32))
        h_dn, h_head = _shift_down(h, hprev8, 1)
        c_sc[...] = h_dn
        c_sc[0:8, :] = h_head
        h_before = c_sc[...]

        valid = _row_ids(i, TM, (TM, D)) >= PAD_ROWS
        da = dh * h_before
        du = jnp.where(valid, dh, 0.0)
        dmult = du * gate_i * c
        dgate_i = du * mult * c
        dc = du * mult * gate_i
        dla = da * a - dmult * (a * a) / mult
        dla = jnp.where(valid, dla, 0.0)
        dr = dla * ((-LRU_C) * sp)
        dzr = dr * r * (1.0 - r)
        dzi = dgate_i * gate_i * (1.0 - gate_i)
        stats_ref[1:2, :] += jnp.sum(dzr, axis=0, keepdims=True)
        stats_ref[2:3, :] += jnp.sum(dzi, axis=0, keepdims=True)
        stats_ref[3:4, :] += jnp.sum(dla * ((-LRU_C) * r), axis=0, keepdims=True)
        dc_gate = []
        for g in range(LRU_BLOCKS):
            sl = slice(g * LRU_BLOCK, (g + 1) * LRU_BLOCK)
            cg = c_b[:, sl]
            dzr_g = dzr[:, sl].astype(BF16)
            dzi_g = dzi[:, sl].astype(BF16)
            dc_gate.append(_dot_nt(dzr_g, wa_ref[g]) + _dot_nt(dzi_g, wx_ref[g]))
            dwa_ref[g] += _dot_tn(cg, dzr_g)
            dwx_ref[g] += _dot_tn(cg, dzi_g)
        dc = dc + jnp.concatenate(dc_gate, axis=1)

        cw = cw_ref[...]
        stats_ref[0:1, :] += jnp.sum(dc, axis=0, keepdims=True)
        stats_ref[7:8, :] += jnp.sum(dc * x, axis=0, keepdims=True)
        dx = cw[3:4] * dc
        tail_src = jnp.concatenate([dc[TM - 8:TM], dcnext_sc[...]], axis=0)
        dx_tail = cw[3:4] * dc[TM - 8:TM]
        for d in (1, 2, 3):
            dx = dx + cw[3 - d:4 - d] * pltpu.roll(dc, TM - d, axis=0)
            dx_tail = dx_tail + cw[3 - d:4 - d] * pltpu.roll(tail_src, 16 - d, axis=0)[0:8]
            rolled, hd = _shift_down(x, prev8, d)
            b_sc[...] = rolled
            b_sc[0:8, :] = hd
            stats_ref[7 - d:8 - d, :] += jnp.sum(dc * b_sc[...], axis=0, keepdims=True)
        dcnext_sc[...] = dc[0:8]
        dproj_ref[...] = dx.astype(BF16)
        dproj_ref[TM - 8:TM, :] = dx_tail.astype(BF16)

    rev = lambda s: nt - 1 - s
    vec = pl.BlockSpec((1, D), lambda s: (0, 0))
    wspec = pl.BlockSpec((LRU_BLOCKS, LRU_BLOCK, LRU_BLOCK), lambda s: (0, 0, 0))
    prev16 = lambda col: pl.BlockSpec((16, D), lambda s: (jnp.maximum(rev(s) * (TM // 16) - 1, 0), col))
    tile = lambda: pl.BlockSpec((TM, D), lambda s: (rev(s), 0))
    h_lru, c_sv, r_sv, i_sv, la_sv, mult_sv = saved
    return _hosted_call(
        body,
        name="lru_bwd",
        grid=(nt,),
        in_specs=[
            pl.BlockSpec(memory_space=pl.ANY),
            pl.BlockSpec((TM, D), lambda s: (rev(s), LRU_IN_COL)), prev16(LRU_IN_COL),
            tile(), prev16(0),
            tile(), tile(), tile(), tile(), tile(), tile(),
            pl.BlockSpec((4, D), lambda s: (0, 0)),
            wspec, wspec, vec,
        ],
        out_specs=[
            pl.BlockSpec((TM, D), lambda s: (rev(s), LRU_IN_COL)),
            wspec, wspec,
            pl.BlockSpec((8, D), lambda s: (0, 0)),
        ],
        out_shape=[
            jax.ShapeDtypeStruct(dproj.shape, BF16),
            jax.ShapeDtypeStruct((LRU_BLOCKS, LRU_BLOCK, LRU_BLOCK), F32),
            jax.ShapeDtypeStruct((LRU_BLOCKS, LRU_BLOCK, LRU_BLOCK), F32),
            jax.ShapeDtypeStruct((8, D), F32),
        ],
        aliases={0: 0},
        scratch_shapes=[
            pltpu.VMEM((8, D), F32), pltpu.VMEM((8, D), F32), pltpu.VMEM((8, D), F32),
            pltpu.VMEM((TM, D), F32), pltpu.VMEM((TM, D), F32), pltpu.VMEM((TM, D), F32),
        ],
        args=(dproj, proj, proj, h_lru, h_lru, c_sv, r_sv, i_sv, la_sv, mult_sv, dhl, conv_w, wa, wx, lam),
        riders=riders,
    )


def _in_proj_bwd(dproj, w_in, part, prev=None, riders=()):
    rows = dproj.shape[0]
    tm = _heavy_tile(rows)
    nt = rows // tm
    first = 0 if part == 0 else (nt + 1) // 2
    count = (nt + 1) // 2 if part == 0 else nt - first

    def body(*refs):
        dproj_ref, w_hbm, du_ref, w_sc, w_sem = refs[-5:]

        @pl.when(pl.program_id(0) == 0)
        def _():
            moves = [((0, 3), 0), ((4, 1), 3), ((3, 1), 4), ((5, 3), 5)]
            copies = [pltpu.make_async_copy(w_hbm.at[:, pl.ds(src * D, n * D)], w_sc.at[:, pl.ds(dst * D, n * D)], w_sem.at[q])
                      for q, ((src, n), dst) in enumerate(moves)]
            for cp in copies:
                cp.start()
            for cp in copies:
                cp.wait()

        du_ref[...] = _dot_nt(dproj_ref[...], w_sc[...])

    in_specs = [pl.BlockSpec((tm, IN_COLS), lambda i: (first + i, 0)), _ANY]
    args = (dproj, w_in)
    if prev is not None:
        in_specs = [_ANY] + in_specs
        args = (prev,) + args
    return _hosted_call(
        body,
        name="in_proj_bwd_%d" % part,
        grid=(count,),
        in_specs=in_specs,
        out_specs=[pl.BlockSpec((tm, D), lambda i: (first + i, 0))],
        out_shape=[jax.ShapeDtypeStruct((rows, D), F32)],
        scratch_shapes=[pltpu.VMEM((D, IN_COLS), BF16), pltpu.SemaphoreType.DMA((4,))],
        aliases={0: 0} if prev is not None else None,
        args=args,
        riders=riders,
    )


def _norm1_bwd(du, dh1, head, x2d, rstd1, norm_w, riders=()):
    rows = du.shape[0]

    def body(du_ref, dh1_ref, head_ref, x_ref, rstd_ref, nw_ref, gx_ref, ghead_ref, stats_ref):
        i = pl.program_id(0)

        @pl.when(i == 0)
        def _():
            stats_ref[...] = jnp.zeros_like(stats_ref)

        def finish(h0, out_ref):
            du = du_ref[...]
            rs = rstd_ref[...]
            n = h0 * rs
            stats_ref[0:1, :] += jnp.sum(du * n, axis=0, keepdims=True)
            dn = du * nw_ref[...]
            out_ref[...] = dh1_ref[...] + rs * (dn - n * jnp.mean(dn * n, axis=-1, keepdims=True))

        @pl.when(i == 0)
        def _():
            finish(head_ref[...], ghead_ref)

        @pl.when(i > 0)
        def _():
            finish(x_ref[...], gx_ref)

    tile = pl.BlockSpec((TM, D), lambda i: (i, 0))
    return _hosted_call(
        body,
        name="norm1_bwd",
        grid=(rows // TM,),
        in_specs=[
            tile, tile,
            pl.BlockSpec((FRONT, D), lambda i: (0, 0)),
            pl.BlockSpec((TM, D), lambda i: (jnp.maximum(i - 1, 0), 0)),
            pl.BlockSpec((TM, 1), lambda i: (i, 0)),
            pl.BlockSpec((1, D), lambda i: (0, 0)),
        ],
        out_specs=[
            pl.BlockSpec((TM, D), lambda i: (jnp.maximum(i - 1, 0), 0)),
            pl.BlockSpec((FRONT, D), lambda i: (0, 0)),
            pl.BlockSpec((8, D), lambda i: (0, 0)),
        ],
        out_shape=[
            jax.ShapeDtypeStruct(x2d.shape, F32),
            jax.ShapeDtypeStruct((FRONT, D), F32),
            jax.ShapeDtypeStruct((8, D), F32),
        ],
        scratch_shapes=[],
        args=(du, dh1, head, x2d, rstd1, norm_w),
        riders=riders,
    )


def _matmul_tn(name, x, dy, out_cols, col0=0, prev=None, k_block=None, n_block=None, riders=(), col_map=None):
    col_map = col_map or (lambda n: n)
    rows, kdim = x.shape
    ndim = dy.shape[1]
    kb = k_block or kdim
    nb = n_block or ndim
    step_bytes = lambda t: 2 * t * (kb * x.dtype.itemsize + nb * dy.dtype.itemsize) + 2 * kb * nb * 4
    tr = next(t for t in (2816, 1408, TM_HEAVY, TM) if rows % t == 0 and (t == TM or step_bytes(t) <= TN_VMEM_BUDGET))
    nr, nk, nn = rows // tr, kdim // kb, ndim // nb
    cb0 = col0 // nb

    def body(*refs):
        x_ref, dy_ref, out_ref = refs[-3], refs[-2], refs[-1]
        part = _dot_tn(x_ref[...].astype(BF16), dy_ref[...].astype(BF16))

        @pl.when(pl.program_id(2) == 0)
        def _():
            out_ref[...] = part

        @pl.when(pl.program_id(2) > 0)
        def _():
            out_ref[...] += part

    in_specs = [
        pl.BlockSpec((tr, kb), lambda n, k, r: (r, k)),
        pl.BlockSpec((tr, nb), lambda n, k, r: (r, n)),
    ]
    args = [x, dy]
    aliases = {}
    if prev is not None:
        in_specs = [pl.BlockSpec(memory_space=pl.ANY)] + in_specs
        args = [prev] + args
        aliases = {0: 0}
    (out,), rider_outs = _hosted_call(
        body,
        name=name,
        grid=(nn, nk, nr),
        in_specs=in_specs,
        out_specs=[pl.BlockSpec((kb, nb), lambda n, k, r: (k, cb0 + col_map(n)))],
        out_shape=[jax.ShapeDtypeStruct((kdim, out_cols), F32)],
        scratch_shapes=[],
        aliases=aliases,
        args=args,
        riders=riders,
    )
    return (out, rider_outs) if riders else out


def _local_step(x2d, target, w, plan):
    rows = FRONT + x2d.shape[0]
    head = jnp.concatenate([jnp.zeros((PAD_ROWS, D), F32), w["meta_tokens"]], axis=0)
    cos_t, sin_t = _rope_tables(rows)
    intra, intra_t, q_dec, k_dec, c_dec = _decay_tables()
    grads = {}

    def hosted(host, fn, *args, **kwargs):
        outs, rider_outs = fn(*args, riders=plan.riders(host, w, grads), **kwargs)
        plan.after(host, rider_outs, w, grads)
        return outs

    (u1, rstd1), _ = _norm1(head, x2d, w["mix_norm_w"])
    proj, w["w_in"] = hosted("in_proj", _in_proj, u1, w["w_in_shard"], w["route"], cos_t, sin_t)
    o, sprev = hosted("retention_fwd", _retention_fwd, proj, intra, q_dec, k_dec, c_dec)
    lru_args = (w["conv_w"], w["conv_b"], w["lru_wa"], w["lru_wx"], w["lru_ba"], w["lru_bx"], w["lru_lambda"])
    lru_saved = hosted("lru_fwd", _lru_fwd, proj, *lru_args)
    h_lru = lru_saved[0]
    y_ret, y_lru, h1, u2, rstd2 = hosted("mix_fwd", _mix_fwd, head, x2d, o, proj, h_lru, w["w_branch_ret"],
                                         w["w_branch_lru"], w["w_out"], w["ffn_norm_w"])
    silu, dsilu, act = hosted("ffn_fwd_gate", _ffn_fwd_gate, u2, w["w_ffn_in"])
    dh2, stats_loss = _ffn_out_loss(act, h1, w["w_ffn_out"], target, w["final_norm_w"])

    dg, dup, dh2b, dh1, stats_ffn = _ffn_bwd(dh2, silu, dsilu, h1, rstd2, w["w_ffn_in"], w["w_ffn_out"], w["ffn_norm_w"])
    grads["w_ffn_in"] = _matmul_tn("dw_ffn_up", u2, dup, 2 * FFN, col0=FFN, n_block=FFN_HALF,
                                   prev=_matmul_tn("dw_ffn_gate", u2, dg, 2 * FFN, n_block=FFN_HALF))
    grads["w_ffn_out"] = _matmul_tn("dw_ffn_out", act, dh2b, D, k_block=FFN_HALF)
    (dproj, do, dhl, mixed, a_ret, a_lru, dy_ret, dy_lru) = hosted(
        "mix_bwd", _mix_bwd, dh1, o, proj, h_lru, y_ret, y_lru, w["w_branch_ret"], w["w_branch_lru"], w["w_out"])
    grads["w_out"] = _matmul_tn("dw_out", mixed, dh1, D)
    grads["w_branch_ret"] = _matmul_tn("dw_branch_ret", a_ret, dy_ret, D)
    grads["w_branch_lru"] = _matmul_tn("dw_branch_lru", a_lru, dy_lru, D)
    (dproj,) = hosted("retention_bwd", _retention_bwd, dproj, proj, do, sprev, intra, intra_t, q_dec, k_dec, c_dec,
                      cos_t, sin_t)
    dproj, grads["lru_wa"], grads["lru_wx"], stats_lru = hosted(
        "lru_bwd", _lru_bwd, dproj, proj, lru_saved, dhl, w["conv_w"], w["lru_wa"], w["lru_wx"], w["lru_lambda"])
    grads["w_in"], rider_outs = _matmul_tn("dw_in", u1, dproj, IN_COLS, n_block=D, col_map=_swap_3_4,
                                           riders=plan.riders("dw_in", w, grads))
    plan.after("dw_in", rider_outs, w, grads)
    (du1,) = hosted("in_proj_bwd_0", _in_proj_bwd, dproj, w["w_in"], 0)
    (du1,) = hosted("in_proj_bwd_1", _in_proj_bwd, dproj, w["w_in"], 1, du1)
    (grad_x, grad_head, stats_in), _ = _norm1_bwd(du1, dh1, head, x2d, rstd1, w["mix_norm_w"])
    return grad_x, grad_head, grads, [stats_loss, stats_ffn, stats_in, stats_lru]


BIG_PIECES = {
    "w_in": ("col", (D, 2 * D)),
    "w_ffn_in": ("col", (D, FFN_HALF)),
    "w_ffn_out": ("row", (FFN // 4, D)),
    "w_branch_ret": ("row", (D // 4, D)),
    "w_branch_lru": ("row", (D // 4, D)),
    "w_out": ("row", (D // 4, D)),
    "lru_wa": ("lru", (LRU_BLOCKS, LRU_BLOCK // 4, LRU_BLOCK)),
    "lru_wx": ("lru", (LRU_BLOCKS, LRU_BLOCK // 4, LRU_BLOCK)),
}
SMALL_PIECES = {"meta_tokens": ("col", (N_META, D // 4)), "conv_w": ("col", (4, D // 4))}


def _full_shape(kind, shard):
    if kind == "col":
        return (shard[0], 4 * shard[1])
    if kind == "row":
        return (4 * shard[0], shard[1])
    return (shard[0], 4 * shard[1], shard[2])


def _half_shape(kind, shard):
    return (shard[0] // 2,) + tuple(shard[1:])


def _aligned(start, multiple):
    return start if isinstance(start, int) else pl.multiple_of(start, multiple)


def _lead(h, size):
    if h is None:
        return pl.ds(0, size)
    return pl.ds(_aligned(h * (size // 2), size // 2), size // 2)


def _full_region(ref, kind, shard, s, h):
    if kind == "col":
        return ref.at[_lead(h, shard[0]), pl.ds(_aligned(s * shard[1], shard[1]), shard[1])]
    if kind == "row":
        size = shard[0] if h is None else shard[0] // 2
        start = s * shard[0] + (0 if h is None else h * (shard[0] // 2))
        return ref.at[pl.ds(_aligned(start, 16), size), :]
    return ref.at[_lead(h, shard[0]), pl.ds(_aligned(s * shard[1], shard[1]), shard[1]), :]


def _shard_region(ref, shard, h):
    return ref.at[_lead(h, shard[0])]


def _place():
    x, y, c = lax.axis_index("x"), lax.axis_index("y"), lax.axis_index("c")
    return x, y, c, 2 * x + y


def _other_chip(s, c, k):
    s2 = jnp.bitwise_xor(s, k)
    return s2, (s2 // 2, s2 % 2, c)


def _remote(src, dst, send_sem, recv_sem, dev):
    return pltpu.make_async_remote_copy(src_ref=src, dst_ref=dst, send_sem=send_sem, recv_sem=recv_sem,
                                        device_id=dev, device_id_type=MESH)


_ANY = pl.BlockSpec(memory_space=pl.ANY)


class _Rider:
    def __init__(self, ins, out_shapes, sem_shapes, build, aliased=False):
        self.ins, self.out_shapes, self.sem_shapes, self.build, self.aliased = ins, out_shapes, sem_shapes, build, aliased


def _hosted_call(body, *, name, grid, in_specs, out_specs, out_shape, scratch_shapes, args, riders=(), aliases=None,
                 prefetch=None, riders_after_body=False):
    n_in, n_out, n_sc = len(in_specs), len(out_shape), len(scratch_shapes)
    r_in = [a for r in riders for a in r.ins]
    r_out = [s for r in riders for s in r.out_shapes]
    r_sem = [s for r in riders for s in r.sem_shapes]
    lead = () if prefetch is None else (prefetch,)
    assert prefetch is None or not (aliases or any(r.aliased for r in riders))

    def full_body(*refs):
        head, refs = refs[:len(lead)], refs[len(lead):]
        ins, rin = refs[:n_in], refs[n_in:n_in + len(r_in)]
        o0 = n_in + len(r_in)
        outs, rout = refs[o0:o0 + n_out], refs[o0 + n_out:o0 + n_out + len(r_out)]
        s0 = o0 + n_out + len(r_out)
        scratch, rsem = refs[s0:s0 + n_sc], refs[s0 + n_sc:]
        starts, waits = [], []
        pi = po = ps = 0
        for r in riders:
            st, wt = r.build(rin[pi:pi + len(r.ins)], rout[po:po + len(r.out_shapes)], rsem[ps:ps + len(r.sem_shapes)])
            starts += st
            waits += wt
            pi, po, ps = pi + len(r.ins), po + len(r.out_shapes), ps + len(r.sem_shapes)
        first = functools.reduce(jnp.logical_and, [pl.program_id(d) == 0 for d in range(len(grid))])
        last = functools.reduce(jnp.logical_and, [pl.program_id(d) == grid[d] - 1 for d in range(len(grid))])

        def start_riders():
            @pl.when(first)
            def _():
                for cp in starts:
                    cp.start()

        if riders and not riders_after_body:
            start_riders()
        body(*head, *ins, *outs, *scratch)
        if riders and riders_after_body:
            start_riders()
        if riders:
            @pl.when(last)
            def _():
                for wait in waits:
                    wait()

    io_aliases = dict(aliases or {})
    pi = po = 0
    for r in riders:
        if r.aliased:
            for q in range(len(r.ins)):
                io_aliases[n_in + pi + q] = n_out + po + q
        pi, po = pi + len(r.ins), po + len(r.out_shapes)
    specs = dict(
        grid=grid,
        in_specs=list(in_specs) + [_ANY] * len(r_in),
        out_specs=list(out_specs) + [_ANY] * len(r_out),
        scratch_shapes=list(scratch_shapes) + r_sem,
    )
    if prefetch is not None:
        specs = dict(grid_spec=pltpu.PrefetchScalarGridSpec(num_scalar_prefetch=1, **specs))
    res = pl.pallas_call(
        full_body,
        name=name,
        out_shape=list(out_shape) + r_out,
        input_output_aliases=io_aliases,
        compiler_params=_params(("arbitrary",) * len(grid)),
        **specs,
    )(*lead, *args, *r_in)
    rider_outs, po = [], n_out
    for r in riders:
        rider_outs.append(list(res[po:po + len(r.out_shapes)]))
        po += len(r.out_shapes)
    return list(res[:n_out]), rider_outs


def _run_riders(name, riders):
    r_in = [a for r in riders for a in r.ins]
    r_out = [s for r in riders for s in r.out_shapes]
    r_sem = [s for r in riders for s in r.sem_shapes]

    def body(*refs):
        rin, rout, rsem = refs[:len(r_in)], refs[len(r_in):len(r_in) + len(r_out)], refs[len(r_in) + len(r_out):]
        pi = po = ps = 0
        for r in riders:
            starts, waits = r.build(rin[pi:pi + len(r.ins)], rout[po:po + len(r.out_shapes)], rsem[ps:ps + len(r.sem_shapes)])
            for cp in starts:
                cp.start()
            for wait in waits:
                wait()
            pi, po, ps = pi + len(r.ins), po + len(r.out_shapes), ps + len(r.sem_shapes)

    io_aliases = {}
    pi = po = 0
    for r in riders:
        if r.aliased:
            for q in range(len(r.ins)):
                io_aliases[pi + q] = po + q
        pi, po = pi + len(r.ins), po + len(r.out_shapes)
    res = pl.pallas_call(
        body,
        name=name,
        in_specs=[_ANY] * len(r_in),
        out_specs=[_ANY] * len(r_out),
        out_shape=r_out,
        scratch_shapes=r_sem,
        input_output_aliases=io_aliases,
    )(*r_in)
    outs, po = [], 0
    for r in riders:
        outs.append(list(res[po:po + len(r.out_shapes)]))
        po += len(r.out_shapes)
    return outs


def _piece(name):
    if name in BIG_PIECES:
        return (name, *BIG_PIECES[name], True)
    return (name, *SMALL_PIECES[name], False)


def _gather_rider(shards, names):
    pieces = [_piece(n) for n in names]
    n = len(pieces)

    def build(ins, outs, sems):
        local_sem, ici_send, ici_recv = sems
        _, _, c, s = _place()
        starts, waits = [], []
        for p, (_, kind, shard, split) in enumerate(pieces):
            cp = pltpu.make_async_copy(ins[p], _full_region(outs[p], kind, shard, s, None), local_sem.at[p])
            starts.append(cp)
            waits.append(cp.wait)
            h = c if split else None
            for k in (1, 2, 3):
                s2, dev = _other_chip(s, c, k)
                cp = _remote(_shard_region(ins[p], shard, h), _full_region(outs[p], kind, shard, s, h),
                             ici_send.at[p, k - 1], ici_recv.at[p, k - 1], dev)
                starts.append(cp)
                waits.append(cp.wait_send)
                region = _full_region(outs[p], kind, shard, s2, h)
                waits.append(_remote(region, region, ici_send.at[p, k - 1], ici_recv.at[p, k - 1], dev).wait_recv)
        return starts, waits

    return _Rider(
        [shards[name] for name in names],
        [jax.ShapeDtypeStruct(_full_shape(kind, shard), shards[name].dtype) for name, kind, shard, _ in pieces],
        [pltpu.SemaphoreType.DMA((n,)), pltpu.SemaphoreType.DMA((n, 3)), pltpu.SemaphoreType.DMA((n, 3))],
        build)


def _forward_rider(gathered, names):
    pieces = [_piece(n) for n in names]
    n = len(pieces)

    def build(ins, outs, sems):
        fwd_send, fwd_recv = sems
        x, y, c, s = _place()
        sibling = (x, y, 1 - c)
        starts, waits = [], []
        for p, (_, kind, shard, _) in enumerate(pieces):
            for k in (1, 2, 3):
                s2, _ = _other_chip(s, c, k)
                mine = _full_region(outs[p], kind, shard, s2, c)
                theirs = _full_region(outs[p], kind, shard, s2, 1 - c)
                cp = _remote(mine, mine, fwd_send.at[p, k - 1], fwd_recv.at[p, k - 1], sibling)
                starts.append(cp)
                waits.append(cp.wait_send)
                waits.append(_remote(theirs, theirs, fwd_send.at[p, k - 1], fwd_recv.at[p, k - 1], sibling).wait_recv)
        return starts, waits

    return _Rider(
        [gathered[name] for name in names],
        [jax.ShapeDtypeStruct(gathered[name].shape, gathered[name].dtype) for name in names],
        [pltpu.SemaphoreType.DMA((n, 3)), pltpu.SemaphoreType.DMA((n, 3))],
        build, aliased=True)


def _pair_exchange_rider(grads, names):
    n = len(names)

    def build(ins, outs, sems):
        send_sem, recv_sem = sems
        x, y, c, _ = _place()
        sibling = (x, y, 1 - c)
        starts, waits = [], []
        for p, name in enumerate(names):
            kind, shard = BIG_PIECES[name]
            for s2 in range(4):
                cp = _remote(_full_region(ins[p], kind, shard, s2, 1 - c), outs[p].at[s2], send_sem.at[p, s2],
                             recv_sem.at[p, s2], sibling)
                starts.append(cp)
                waits.append(cp.wait_send)
                waits.append(_remote(outs[p].at[s2], outs[p].at[s2], send_sem.at[p, s2], recv_sem.at[p, s2], sibling).wait_recv)
        return starts, waits

    return _Rider(
        [grads[name] for name in names],
        [jax.ShapeDtypeStruct((4,) + _half_shape(*BIG_PIECES[name]), F32) for name in names],
        [pltpu.SemaphoreType.DMA((n, 4))] * 2,
        build)


def _half_specs(kind, shard):
    half = shard[0] // 2
    if kind == "col":
        full = pl.BlockSpec((half, shard[1]), lambda j, pr: (pr[1], j))
        buf = pl.BlockSpec((None, half, shard[1]), lambda j, pr: (j, 0, 0))
    elif kind == "row":
        full = pl.BlockSpec((half, shard[1]), lambda j, pr: (2 * j + pr[1], 0))
        buf = pl.BlockSpec((None, half, shard[1]), lambda j, pr: (j, 0, 0))
    else:
        full = pl.BlockSpec((half, shard[1], shard[2]), lambda j, pr: (pr[1], j, 0))
        buf = pl.BlockSpec((None, half, shard[1], shard[2]), lambda j, pr: (j, 0, 0, 0))
    return full, buf


def _pair_sum(name, grad, recv, place):
    kind, shard = BIG_PIECES[name]
    full, buf = _half_specs(kind, shard)

    def body(pr, g_ref, r_ref, o_ref):
        o_ref[...] = (g_ref[...] + r_ref[...]).astype(BF16)

    return pl.pallas_call(
        body,
        name="pair_sum_" + name,
        grid_spec=pltpu.PrefetchScalarGridSpec(num_scalar_prefetch=1, grid=(4,), in_specs=[full, buf], out_specs=buf),
        out_shape=jax.ShapeDtypeStruct((4,) + _half_shape(kind, shard), BF16),
        compiler_params=_params(("arbitrary",)),
    )(place, grad, recv)


def _chip_exchange_rider(sums, names):
    n = len(names)

    def build(ins, outs, sems):
        send_sem, recv_sem = sems
        _, _, c, s = _place()
        starts, waits = [], []
        for p in range(n):
            for k in (1, 2, 3):
                s2, dev = _other_chip(s, c, k)
                cp = _remote(ins[p].at[s2], outs[p].at[k - 1], send_sem.at[p, k - 1], recv_sem.at[p, k - 1], dev)
                starts.append(cp)
                waits.append(cp.wait_send)
                waits.append(_remote(outs[p].at[k - 1], outs[p].at[k - 1], send_sem.at[p, k - 1], recv_sem.at[p, k - 1],
                                     dev).wait_recv)
        return starts, waits

    return _Rider(
        [sums[name] for name in names],
        [jax.ShapeDtypeStruct((3,) + _half_shape(*BIG_PIECES[name]), BF16) for name in names],
        [pltpu.SemaphoreType.DMA((n, 3))] * 2,
        build)


def _chip_sum(name, grad, recv_pair, recv_chip, place):
    kind, shard = BIG_PIECES[name]
    half = shard[0] // 2
    tail = tuple(shard[1:])
    zeros = (0,) * len(tail)
    if kind == "col":
        full = pl.BlockSpec((half,) + tail, lambda j, pr: (pr[1], pr[0]))
    elif kind == "row":
        full = pl.BlockSpec((half,) + tail, lambda j, pr: (2 * pr[0] + pr[1], 0))
    else:
        full = pl.BlockSpec((half,) + tail, lambda j, pr: (pr[1], pr[0], 0))
    pair = pl.BlockSpec((None, half) + tail, lambda j, pr: (pr[0], 0) + zeros)
    chip = pl.BlockSpec((3, half) + tail, lambda j, pr: (0, 0) + zeros)
    out = pl.BlockSpec((half,) + tail, lambda j, pr: (pr[1],) + zeros)

    def body(pr, g_ref, rp_ref, rc_ref, o_ref):
        total = g_ref[...] + rp_ref[...]
        for k in range(3):
            total = total + rc_ref[k].astype(F32)
        o_ref[...] = total

    return pl.pallas_call(
        body,
        name="chip_sum_" + name,
        grid_spec=pltpu.PrefetchScalarGridSpec(num_scalar_prefetch=1, grid=(1,), in_specs=[full, pair, chip], out_specs=out),
        out_shape=jax.ShapeDtypeStruct(shard, F32),
        compiler_params=_params(("arbitrary",)),
    )(place, grad, recv_pair, recv_chip)


def _sibling_exchange_rider(halves, names):
    n = len(names)

    def build(ins, outs, sems):
        send_sem, recv_sem = sems
        x, y, c, _ = _place()
        sibling = (x, y, 1 - c)
        starts, waits = [], []
        for p, name in enumerate(names):
            shard = BIG_PIECES[name][1]
            mine = _shard_region(outs[p], shard, c)
            theirs = _shard_region(outs[p], shard, 1 - c)
            cp = _remote(mine, mine, send_sem.at[p], recv_sem.at[p], sibling)
            starts.append(cp)
            waits.append(cp.wait_send)
            waits.append(_remote(theirs, theirs, send_sem.at[p], recv_sem.at[p], sibling).wait_recv)
        return starts, waits

    return _Rider(
        [halves[name] for name in names],
        [jax.ShapeDtypeStruct(BIG_PIECES[name][1], F32) for name in names],
        [pltpu.SemaphoreType.DMA((n,))] * 2,
        build, aliased=True)


FIRST_WEIGHTS = ["meta_tokens", "conv_w"]
WEIGHT_GROUPS = {
    "lru": ["lru_wa", "lru_wx"],
    "branch": ["w_branch_ret", "w_branch_lru", "w_out"],
    "ffn_in": ["w_ffn_in"],
    "ffn_out": ["w_ffn_out"],
}
WEIGHT_SCHEDULE = {
    "in_proj": [("gather", "lru")],
    "retention_fwd": [("forward", "lru"), ("gather", "branch")],
    "lru_fwd": [("forward", "branch"), ("gather", "ffn_in")],
    "mix_fwd": [("forward", "ffn_in"), ("gather", "ffn_out")],
    "ffn_fwd_gate": [("forward", "ffn_out")],
}
GRAD_GROUPS = {
    "ffn": ["w_ffn_in", "w_ffn_out"],
    "mixer": ["w_out", "w_branch_ret", "w_branch_lru", "lru_wa", "lru_wx"],
    "in": ["w_in"],
}
GRAD_SCHEDULE = {
    "mix_bwd": [("pair", "ffn")],
    "retention_bwd": [("chip", "ffn")],
    "lru_bwd": [("sibling", "ffn")],
    "dw_in": [("pair", "mixer")],
    "in_proj_bwd_0": [("chip", "mixer"), ("pair", "in")],
    "in_proj_bwd_1": [("sibling", "mixer"), ("chip", "in")],
}


class _CommPlan:
    def __init__(self, shards, place):
        self.shards, self.place = shards, place
        self.late = {}
        self.recv_pair, self.sums, self.recv_chip, self.halves, self.final = {}, {}, {}, {}, {}

    def _grad_rider(self, stage, group, grads):
        names = GRAD_GROUPS[group]
        if stage == "pair":
            return _pair_exchange_rider(grads, names)
        if stage == "chip":
            return _chip_exchange_rider(self.sums, names)
        return _sibling_exchange_rider(self.halves, names)

    def _grad_after(self, stage, group, outs, grads):
        names = GRAD_GROUPS[group]
        if stage == "pair":
            for n, o in zip(names, outs):
                self.recv_pair[n] = o
                self.sums[n] = _pair_sum(n, grads[n], o, self.place)
        elif stage == "chip":
            for n, o in zip(names, outs):
                self.halves[n] = _chip_sum(n, grads[n], self.recv_pair[n], o, self.place)
        else:
            self.final.update(zip(names, outs))

    def riders(self, host, w, grads):
        if host in WEIGHT_SCHEDULE:
            return [_gather_rider(self.shards, WEIGHT_GROUPS[group]) if stage == "gather"
                    else _forward_rider(self.late, WEIGHT_GROUPS[group]) for stage, group in WEIGHT_SCHEDULE[host]]
        return [self._grad_rider(stage, group, grads) for stage, group in GRAD_SCHEDULE.get(host, [])]

    def after(self, host, rider_outs, w, grads):
        for (stage, group), outs in zip(WEIGHT_SCHEDULE.get(host, []), rider_outs):
            (self.late if stage == "gather" else w).update(zip(WEIGHT_GROUPS[group], outs))
        for (stage, group), outs in zip(GRAD_SCHEDULE.get(host, []), rider_outs):
            self._grad_after(stage, group, outs, grads)

    def finish(self):
        (outs,) = _run_riders("grad_tail_exchange", [_sibling_exchange_rider(self.halves, GRAD_GROUPS["in"])])
        self.final.update(zip(GRAD_GROUPS["in"], outs))
        return self.final


def _adamw_math(w, g, m, v):
    m = ADAM_B1 * m + (1.0 - ADAM_B1) * g
    v = ADAM_B2 * v + (1.0 - ADAM_B2) * (g * g)
    m_hat = m / (1.0 - ADAM_B1 ** ADAM_STEP)
    v_hat = v / (1.0 - ADAM_B2 ** ADAM_STEP)
    delta = -ADAM_LR * (m_hat / (jnp.sqrt(v_hat) + ADAM_EPS) + ADAM_WD * w)
    return delta, m, v


def _adamw(name, g, w, m, v):
    rows, cols = g.shape
    tr = rows // 4 if rows % 32 == 0 else rows

    def body(g_ref, w_ref, m_ref, v_ref, go_ref, d_ref, mo_ref, vo_ref):
        gv = g_ref[...]
        delta, m2, v2 = _adamw_math(w_ref[...], gv, m_ref[...], v_ref[...])
        go_ref[...] = gv
        d_ref[...] = delta
        mo_ref[...] = m2
        vo_ref[...] = v2

    spec = pl.BlockSpec((tr, cols), lambda i: (i, 0))
    return pl.pallas_call(
        body,
        name="adamw_" + name,
        grid=(rows // tr,),
        in_specs=[spec] * 4,
        out_specs=[spec] * 4,
        out_shape=[jax.ShapeDtypeStruct((rows, cols), F32)] * 4,
        compiler_params=_params(("arbitrary",)),
    )(g, w, m, v)


SMALL_ROWS = 48
VEC_ROWS = {"final_norm_w": 1, "ffn_norm_w": 8, "mix_norm_w": 16, "conv_b": 24, "lru_ba": 25, "lru_bx": 26, "lru_lambda": 27}
VEC_NAMES = list(VEC_ROWS)
CONV_W_ROW = 28
META_ROW = 32


def _small_all_reduce(partial):
    def body(in_ref, out_ref, buf, local_sem, send_sem, recv_sem):
        x, y, c, s = _place()
        me = 2 * s + c
        cp = pltpu.make_async_copy(in_ref, buf.at[me], local_sem)
        cp.start()
        started = []
        for k in range(1, 8):
            peer = jnp.bitwise_xor(me, k)
            dev = (peer // 4, (peer // 2) % 2, peer % 2)
            rd = _remote(in_ref, buf.at[me], send_sem.at[k - 1], recv_sem.at[k - 1], dev)
            rd.start()
            started.append(rd)
        for k in range(1, 8):
            peer = jnp.bitwise_xor(me, k)
            dev = (peer // 4, (peer // 2) % 2, peer % 2)
            _remote(in_ref, buf.at[peer], send_sem.at[k - 1], recv_sem.at[k - 1], dev).wait_recv()
        for rd in started:
            rd.wait_send()
        cp.wait()
        total = buf[0]
        for d in range(1, 8):
            total = total + buf[d]
        out_ref[...] = total

    return pl.pallas_call(
        body,
        name="small_all_reduce",
        in_specs=[pl.BlockSpec(memory_space=pltpu.VMEM)],
        out_specs=pl.BlockSpec(memory_space=pltpu.VMEM),
        out_shape=jax.ShapeDtypeStruct((SMALL_ROWS, D), F32),
        scratch_shapes=[pltpu.VMEM((8, SMALL_ROWS, D), F32), pltpu.SemaphoreType.DMA,
                        pltpu.SemaphoreType.DMA((7,)), pltpu.SemaphoreType.DMA((7,))],
    )(partial)


def _small_update(total, place, vecs, conv, meta):
    nvec = len(VEC_NAMES)
    qcols = D // 4

    def body(pr, tot_ref, col_ref, *refs):
        vec_refs = refs[:3 * nvec]
        conv_refs = refs[3 * nvec:3 * nvec + 3]
        meta_refs = refs[3 * nvec + 3:3 * nvec + 6]
        outs = refs[3 * nvec + 6:]
        loss_ref, vec_out, conv_out, meta_out = outs[0], outs[1:5], outs[5:9], outs[9:13]
        loss_ref[...] = jnp.sum(tot_ref[0:1, :], axis=1, keepdims=True)
        for o in vec_out:
            o[...] = jnp.zeros_like(o)
        for j, name in enumerate(VEC_NAMES):
            w, m, v = (r[...] for r in vec_refs[3 * j:3 * j + 3])
            g = tot_ref[VEC_ROWS[name]:VEC_ROWS[name] + 1, :]
            if name == "lru_lambda":
                g = -g / (1.0 + jnp.exp(w))
            for o, val in zip(vec_out, (g,) + _adamw_math(w, g, m, v)):
                o[j:j + 1, :] = val
        for row, n_rows, ins, group in ((CONV_W_ROW, 4, conv_refs, conv_out), (META_ROW, N_META, meta_refs, meta_out)):
            g = col_ref[row:row + n_rows, :]
            w, m, v = (r[...] for r in ins)
            for o, val in zip(group, (g,) + _adamw_math(w, g, m, v)):
                o[...] = val

    whole = lambda shape: pl.BlockSpec(shape, lambda i, pr: (0,) * len(shape))
    in_specs = [whole((SMALL_ROWS, D)), pl.BlockSpec((SMALL_ROWS, qcols), lambda i, pr: (0, pr[0]))]
    in_specs += [whole((1, D))] * (3 * nvec) + [whole((4, qcols))] * 3 + [whole((N_META, qcols))] * 3
    out_shapes = [(1, 1)] + [(8, D)] * 4 + [(4, qcols)] * 4 + [(N_META, qcols)] * 4
    return pl.pallas_call(
        body,
        name="small_update",
        grid_spec=pltpu.PrefetchScalarGridSpec(num_scalar_prefetch=1, grid=(1,), in_specs=in_specs,
                                               out_specs=[whole(s) for s in out_shapes]),
        out_shape=[jax.ShapeDtypeStruct(s, F32) for s in out_shapes],
        compiler_params=_params(("arbitrary",)),
    )(place, total, total, *[a for t in vecs for a in t], *conv, *meta)


WEIGHT_ORDER = ["meta_tokens", "mix_norm_w", "w_in", "conv_w", "conv_b", "lru_wa", "lru_ba", "lru_wx", "lru_bx", "lru_lambda",
                "w_branch_ret", "w_branch_lru", "w_out", "ffn_norm_w", "w_ffn_in", "w_ffn_out", "final_norm_w"]


def kernel(x, meta_tokens, mix_norm_w, w_in, conv_w, conv_b, lru_wa, lru_ba, lru_wx, lru_bx, lru_lambda, w_branch_ret, w_branch_lru, w_out, ffn_norm_w, w_ffn_in, w_ffn_out, final_norm_w, loss_target, m_meta_tokens, m_mix_norm_w, m_w_in, m_conv_w, m_conv_b, m_lru_wa, m_lru_ba, m_lru_wx, m_lru_bx, m_lru_lambda, m_w_branch_ret, m_w_branch_lru, m_w_out, m_ffn_norm_w, m_w_ffn_in, m_w_ffn_out, m_final_norm_w, v_meta_tokens, v_mix_norm_w, v_w_in, v_conv_w, v_conv_b, v_lru_wa, v_lru_ba, v_lru_wx, v_lru_bx, v_lru_lambda, v_w_branch_ret, v_w_branch_lru, v_w_out, v_ffn_norm_w, v_w_ffn_in, v_w_ffn_out, v_final_norm_w):
    args = locals()
    wts = {n: args[n] for n in WEIGHT_ORDER}
    mom = {n: args["m_" + n] for n in WEIGHT_ORDER}
    var = {n: args["v_" + n] for n in WEIGHT_ORDER}
    place = jnp.stack([2 * lax.axis_index("x") + lax.axis_index("y"), lax.axis_index("c")]).astype(jnp.int32)

    shards = {n: wts[n][0].astype(BF16) for n in BIG_PIECES}
    shards["meta_tokens"] = wts["meta_tokens"]
    shards["conv_w"] = wts["conv_w"][0]
    plan = _CommPlan(shards, place)
    (first,) = _run_riders("gather_first", [_gather_rider(shards, FIRST_WEIGHTS)])
    w = dict(zip(FIRST_WEIGHTS, first))
    for n in VEC_NAMES:
        w[n] = wts[n].reshape(1, D)
    chips = jnp.bitwise_xor(place[0], jnp.arange(4, dtype=jnp.int32))
    w["route"] = jnp.concatenate([place, jnp.stack([2 * chips, 2 * chips + 1], axis=1).reshape(8)])
    w["w_in_shard"] = shards["w_in"]

    grad_x, grad_head, _, stats = _local_step(x[0], loss_target[0], w, plan)
    shard_grads = plan.finish()

    out = {}
    for n in BIG_PIECES:
        shape2d = (-1, wts[n].shape[-1])
        res = _adamw(n, *[a.reshape(shape2d) for a in (shard_grads[n], wts[n], mom[n], var[n])])
        out[n] = [r.reshape(wts[n].shape) for r in res]

    partial = jnp.concatenate(stats + [grad_head[PAD_ROWS:]], axis=0)
    total = _small_all_reduce(partial)
    vecs = [tuple(a[n].reshape(1, D) for a in (wts, mom, var)) for n in VEC_NAMES]
    conv = tuple(a["conv_w"][0] for a in (wts, mom, var))
    meta = tuple(a["meta_tokens"] for a in (wts, mom, var))
    res = _small_update(total, place, vecs, conv, meta)
    loss = res[0].reshape(())
    for j, n in enumerate(VEC_NAMES):
        out[n] = [r[j].reshape(wts[n].shape) for r in res[1:5]]
    out["conv_w"] = [r.reshape(wts["conv_w"].shape) for r in res[5:9]]
    out["meta_tokens"] = list(res[9:13])

    return (loss, grad_x.reshape(x.shape)) + tuple(out[n][kind] for kind in range(4) for n in WEIGHT_ORDER)
```

```python
import functools
import math

import jax
import jax.numpy as jnp
from jax import lax
from jax.experimental import pallas as pl
from jax.experimental.pallas import tpu as pltpu

F32 = jnp.float32
BF16 = jnp.bfloat16

D = 1024
HEADS = 8
DH = 128
CHUNK = 256
N_META = 16
FRONT = 256
PAD_ROWS = FRONT - N_META
LRU_BLOCKS = 4
LRU_BLOCK = 256
LRU_C = 8.0
FFN = 2816
FFN_HALF = FFN // 2
IN_COLS = 8 * D
ROPE_BASE = 10000.0
EPS = 1e-6
QK_SCALE = DH ** -0.5

ADAM_LR = 0.001
ADAM_B1 = 0.9
ADAM_B2 = 0.999
ADAM_EPS = 1e-08
ADAM_WD = 0.01
ADAM_STEP = 10

TM = 256
TM_HEAVY = 768
FFN_BLOCK = FFN // 2
FFN_SUB = 768
TM_MIX_BWD = 256
TM_LRU = 384
VMEM_LIMIT = 60 * 1024 * 1024
TN_VMEM_BUDGET = 40 * 1024 * 1024

NT_DIMS = (((1,), (1,)), ((), ()))
TN_DIMS = (((0,), (0,)), ((), ()))
MESH = pl.DeviceIdType.MESH


def _params(sem=None):
    if sem is None:
        return pltpu.CompilerParams(vmem_limit_bytes=VMEM_LIMIT)
    return pltpu.CompilerParams(dimension_semantics=sem, vmem_limit_bytes=VMEM_LIMIT)


def _dot(a, b):
    return jnp.dot(a, b, preferred_element_type=F32)


def _dot_nt(a, b):
    return lax.dot_general(a, b, NT_DIMS, preferred_element_type=F32)


def _dot_tn(a, b):
    return lax.dot_general(a, b, TN_DIMS, preferred_element_type=F32)


def _sigmoid(z):
    return 1.0 / (1.0 + jnp.exp(-z))


def _log1p(x):
    return jnp.where(x < 1e-3, x * (1.0 - x * (0.5 - x * (1.0 / 3.0))), jnp.log(1.0 + x))


def _softplus(x):
    return jnp.maximum(x, 0.0) + _log1p(jnp.exp(-jnp.abs(x)))


def _neg_expm1(x):
    series = -x * (1.0 + x * (0.5 + x * (1.0 / 6.0 + x * (1.0 / 24.0 + x * (1.0 / 120.0)))))
    return jnp.where(x > -0.05, series, 1.0 - jnp.exp(x))


_GELU_K = math.sqrt(2.0 / math.pi)


def _gelu_and_grad(x):
    inner = _GELU_K * (x + 0.044715 * x * x * x)
    t = jnp.tanh(inner)
    val = 0.5 * x * (1.0 + t)
    grad = 0.5 * (1.0 + t) + 0.5 * x * (1.0 - t * t) * _GELU_K * (1.0 + 3.0 * 0.044715 * x * x)
    return val, grad


def _row_ids(i, rows, shape):
    return i * rows + lax.broadcasted_iota(jnp.int32, shape, 0)


def _rope_tables(rows):
    pos = jnp.arange(rows, dtype=jnp.int32) - PAD_ROWS
    inv_freq = ROPE_BASE ** (-jnp.arange(0, DH, 2, dtype=F32) / DH)
    ang = pos.astype(F32)[:, None] * inv_freq[None, :]
    cos, sin = jnp.cos(ang), jnp.sin(ang)
    return jnp.concatenate([cos, cos], axis=1), jnp.concatenate([-sin, sin], axis=1)


def _decay_tables():
    log_g = jnp.log(1.0 - 2.0 ** (-5.0 - jnp.arange(HEADS, dtype=F32)))
    idx = jnp.arange(CHUNK, dtype=F32)
    diff = idx[:, None] - idx[None, :]
    intra = jnp.where(diff[None] >= 0, jnp.exp(jnp.maximum(diff, 0.0)[None] * log_g[:, None, None]), 0.0)
    q_decay = jnp.exp((idx + 1.0)[:, None] * log_g[None, :])
    k_decay = jnp.exp((CHUNK - 1.0 - idx)[:, None] * log_g[None, :])
    chunk_decay = jnp.exp(CHUNK * log_g)
    wide = lambda a: jnp.repeat(a, DH, axis=-1)
    return intra, jnp.swapaxes(intra, 1, 2), wide(q_decay), wide(k_decay), wide(chunk_decay[None, :])


def _norm1(head, x2d, norm_w, riders=()):
    rows = FRONT + x2d.shape[0]

    def body(head_ref, x_ref, nw_ref, u_ref, rstd_ref):
        def norm(hv):
            rs = lax.rsqrt(jnp.mean(hv * hv, axis=-1, keepdims=True) + EPS)
            u_ref[...] = ((hv * rs) * nw_ref[...]).astype(BF16)
            rstd_ref[...] = rs

        @pl.when(pl.program_id(0) == 0)
        def _():
            norm(head_ref[...])

        @pl.when(pl.program_id(0) > 0)
        def _():
            norm(x_ref[...])

    return _hosted_call(
        body,
        name="norm1",
        grid=(rows // TM,),
        in_specs=[
            pl.BlockSpec((FRONT, D), lambda i: (0, 0)),
            pl.BlockSpec((TM, D), lambda i: (jnp.maximum(i - 1, 0), 0)),
            pl.BlockSpec((1, D), lambda i: (0, 0)),
        ],
        out_specs=[pl.BlockSpec((TM, D), lambda i: (i, 0)), pl.BlockSpec((TM, 1), lambda i: (i, 0))],
        out_shape=[jax.ShapeDtypeStruct((rows, D), BF16), jax.ShapeDtypeStruct((rows, 1), F32)],
        scratch_shapes=[],
        args=(head, x2d, norm_w),
        riders=riders,
    )


def _heavy_tile(rows):
    return TM_HEAVY if rows % TM_HEAVY == 0 else TM


def _lru_tile(rows):
    return TM_LRU if rows % TM_LRU == 0 else TM


def _swap_3_4(group):
    return jnp.where(group == 3, 4, jnp.where(group == 4, 3, group))


LRU_IN_COL = 3
GATES_COL = 1


def _in_proj(u, w_shard, route, cos_t, sin_t, riders=()):
    rows = u.shape[0]
    tm = _heavy_tile(rows)
    nt = rows // tm
    kind, shard = BIG_PIECES["w_in"]

    def body(route_ref, u_hbm, wsh_ref, cos_ref, sin_ref, proj_ref, wfull_ref,
             u_sc, w_sc, u_sem, w_sem, local_sem, ici_send, ici_recv, fwd_send, fwd_recv):
        g, i = pl.program_id(0), pl.program_id(1)
        s, c = route_ref[0], route_ref[1]
        gid = route_ref[2 + g]
        sibling = (s // 2, s % 2, 1 - c)
        local = pltpu.make_async_copy(wsh_ref, _full_region(wfull_ref, kind, shard, s, None), local_sem)
        u_copies = [pltpu.make_async_copy(u_hbm.at[pl.ds(t * tm, tm)], u_sc.at[t], u_sem.at[t]) for t in range(nt)]
        sends, arrivals = [], []
        for k in (1, 2, 3):
            s2, dev = _other_chip(s, c, k)
            sends.append(_remote(_shard_region(wsh_ref, shard, c), _full_region(wfull_ref, kind, shard, s, c),
                                 ici_send.at[k - 1], ici_recv.at[k - 1], dev))
            mine = _full_region(wfull_ref, kind, shard, s2, c)
            theirs = _full_region(wfull_ref, kind, shard, s2, 1 - c)
            arrivals.append((_remote(mine, mine, ici_send.at[k - 1], ici_recv.at[k - 1], dev),
                             _remote(mine, mine, fwd_send.at[k - 1], fwd_recv.at[k - 1], sibling),
                             _remote(theirs, theirs, fwd_send.at[k - 1], fwd_recv.at[k - 1], sibling)))

        slot = g % 2
        last_tile = i == nt - 1

        def block_copy(src, col, to_slot):
            return pltpu.make_async_copy(src.at[:, pl.ds(pl.multiple_of(col * D, D), D)], w_sc.at[to_slot],
                                         w_sem.at[to_slot])

        @pl.when(jnp.logical_and(g == 0, i == 0))
        def _():
            for cp in sends:
                cp.start()
            local.start()
            for cp in u_copies:
                cp.start()
            cp = block_copy(wsh_ref, 0, 0)
            cp.start()
            cp.wait()

        @pl.when(jnp.logical_and(last_tile, g == 0))
        def _():
            block_copy(wsh_ref, 1, 1).start()

        for k, (arrived, forward, forwarded) in zip((1, 2, 3), arrivals):
            @pl.when(jnp.logical_and(last_tile, g == 2 * k - 1))
            def _():
                arrived.wait_recv()
                forward.start()
                forwarded.wait_recv()
                block_copy(wfull_ref, route_ref[2 + 2 * k], 0).start()

            @pl.when(jnp.logical_and(last_tile, g == 2 * k))
            def _():
                block_copy(wfull_ref, route_ref[3 + 2 * k], 1).start()

        @pl.when(jnp.logical_and(i == 0, g > 0))
        def _():
            block_copy(wfull_ref, 0, slot).wait()

        for t in range(nt):
            @pl.when(jnp.logical_and(g == 0, i == t))
            def _():
                u_copies[t].wait()

        acc = _dot(u_sc[i], w_sc[slot])

        @pl.when(gid < 2)
        def _():
            scale = jnp.where(gid == 1, QK_SCALE, 1.0).astype(F32)
            for h in range(HEADS):
                sl = slice(h * DH, (h + 1) * DH)
                blk = acc[:, sl]
                out = (blk * cos_ref[...] + pltpu.roll(blk, DH // 2, axis=1) * sin_ref[...]) * scale
                proj_ref[:, sl] = out.astype(BF16)

        @pl.when(gid >= 2)
        def _():
            proj_ref[...] = acc.astype(BF16)

        @pl.when(jnp.logical_and(g == 7, i == nt - 1))
        def _():
            local.wait()
            for cp in sends:
                cp.wait_send()
            for _, forward, _ in arrivals:
                forward.wait_send()

    return _hosted_call(
        body,
        name="in_proj",
        grid=(8, nt),
        in_specs=[
            _ANY, _ANY,
            pl.BlockSpec((tm, DH), lambda g, i, rt: (i, 0)),
            pl.BlockSpec((tm, DH), lambda g, i, rt: (i, 0)),
        ],
        out_specs=[pl.BlockSpec((tm, D), lambda g, i, rt: (i, _swap_3_4(rt[2 + g]))), _ANY],
        out_shape=[jax.ShapeDtypeStruct((rows, IN_COLS), BF16), jax.ShapeDtypeStruct((D, IN_COLS), BF16)],
        scratch_shapes=[
            pltpu.VMEM((nt, tm, D), BF16), pltpu.VMEM((2, D, D), BF16),
            pltpu.SemaphoreType.DMA((nt,)), pltpu.SemaphoreType.DMA((2,)), pltpu.SemaphoreType.DMA,
            pltpu.SemaphoreType.DMA((3,)), pltpu.SemaphoreType.DMA((3,)),
            pltpu.SemaphoreType.DMA((3,)), pltpu.SemaphoreType.DMA((3,)),
        ],
        args=(u, w_shard, cos_t, sin_t),
        riders=riders,
        prefetch=route,
        riders_after_body=True,
    )


def _retention_fwd(proj_bf, intra, q_dec, k_dec, c_dec, riders=()):
    rows = proj_bf.shape[0]
    nc = rows // CHUNK

    def body(q_ref, k_ref, v_ref, m_ref, qd_ref, kd_ref, cd_ref, o_ref, sprev_ref, s_sc):
        @pl.when(pl.program_id(0) == 0)
        def _():
            s_sc[...] = jnp.zeros_like(s_sc)

        for h in range(HEADS):
            sl = slice(h * DH, (h + 1) * DH)
            q, k, v = q_ref[:, sl], k_ref[:, sl], v_ref[:, sl]
            state = s_sc[h]
            state_b = state.astype(BF16)
            sprev_ref[0, h] = state_b
            s = _dot_nt(q, k) * m_ref[h]
            inner = _dot(s.astype(BF16), v)
            cross = _dot(q, state_b) * qd_ref[:, sl]
            o_ref[:, sl] = (inner + cross).astype(BF16)
            k_scaled = (k.astype(F32) * kd_ref[:, sl]).astype(BF16)
            s_sc[h] = state * cd_ref[:, sl] + _dot_tn(k_scaled, v)

    chunk_spec = lambda col: pl.BlockSpec((CHUNK, D), lambda c: (c, col))
    const2 = lambda shape: pl.BlockSpec(shape, lambda c: (0, 0))
    return _hosted_call(
        body,
        name="retention_fwd",
        grid=(nc,),
        in_specs=[
            chunk_spec(0), chunk_spec(1), chunk_spec(2),
            pl.BlockSpec((HEADS, CHUNK, CHUNK), lambda c: (0, 0, 0)),
            const2((CHUNK, D)), const2((CHUNK, D)), const2((1, D)),
        ],
        out_specs=[
            pl.BlockSpec((CHUNK, D), lambda c: (c, 0)),
            pl.BlockSpec((1, HEADS, DH, DH), lambda c: (c, 0, 0, 0)),
        ],
        out_shape=[
            jax.ShapeDtypeStruct((rows, D), BF16),
            jax.ShapeDtypeStruct((nc, HEADS, DH, DH), BF16),
        ],
        scratch_shapes=[pltpu.VMEM((HEADS, DH, DH), F32)],
        args=(proj_bf, proj_bf, proj_bf, intra, q_dec, k_dec, c_dec),
        riders=riders,
    )


def _shift_down(x, first8_prev, d):
    rolled = pltpu.roll(x, d, axis=0)
    head = pltpu.roll(jnp.concatenate([first8_prev, x[0:8]], axis=0), d, axis=0)[8:16]
    return rolled, head


def _conv_and_gates(x, prev8, cw_ref, cb_ref, wa_ref, wx_ref, ba_ref, bx_ref, lam_ref, c_sc):
    cw = cw_ref[...]
    conv = cb_ref[...] + cw[3:4] * x
    head = cb_ref[...] + cw[3:4] * x[0:8]
    for d in (1, 2, 3):
        rolled, hd = _shift_down(x, prev8, d)
        conv = conv + cw[3 - d:4 - d] * rolled
        head = head + cw[3 - d:4 - d] * hd
    c_sc[...] = conv
    c_sc[0:8, :] = head
    c = c_sc[...]
    zr, zi = [], []
    for g in range(LRU_BLOCKS):
        sl = slice(g * LRU_BLOCK, (g + 1) * LRU_BLOCK)
        cg = c[:, sl].astype(BF16)
        zr.append(_dot(cg, wa_ref[g]))
        zi.append(_dot(cg, wx_ref[g]))
    r = _sigmoid(jnp.concatenate(zr, axis=1) + ba_ref[...])
    gate_i = _sigmoid(jnp.concatenate(zi, axis=1) + bx_ref[...])
    sp = _softplus(-lam_ref[...])
    log_a = (-LRU_C) * r * sp
    a = jnp.exp(log_a)
    mult = jnp.sqrt(_neg_expm1(2.0 * log_a))
    return c, r, gate_i, a, mult, log_a


def _lru_fwd(proj, conv_w, conv_b, wa, wx, ba, bx, lam, riders=()):
    rows = proj.shape[0]
    TM = _lru_tile(rows)
    nt = rows // TM

    def body(x_ref, cw_ref, cb_ref, wa_ref, wx_ref, ba_ref, bx_ref, lam_ref,
             h_ref, c_ref, r_ref, i_ref, la_ref, mult_ref, prev_sc, carry_sc, c_sc, a_sc, u_sc, h_sc):
        i = pl.program_id(0)

        @pl.when(i == 0)
        def _():
            prev_sc[...] = jnp.zeros_like(prev_sc)
            carry_sc[...] = jnp.zeros_like(carry_sc)

        x = x_ref[...].astype(F32)
        c, r, gate_i, a, mult, log_a = _conv_and_gates(x, prev_sc[...], cw_ref, cb_ref, wa_ref, wx_ref, ba_ref, bx_ref,
                                                       lam_ref, c_sc)
        for ref, val in ((c_ref, c), (r_ref, r), (i_ref, gate_i), (la_ref, log_a), (mult_ref, mult)):
            ref[...] = val.astype(BF16)
        prev_sc[...] = x[TM - 8:TM]
        valid = _row_ids(i, TM, (TM, D)) >= PAD_ROWS
        a_sc[...] = a
        u_sc[...] = jnp.where(valid, mult * gate_i * c, 0.0)
        row8 = lax.broadcasted_iota(jnp.int32, (8, D), 0)

        def group(gi, hprev):
            r0 = pl.multiple_of(gi * 8, 8)
            aa = a_sc[pl.ds(r0, 8), :]
            uu = u_sc[pl.ds(r0, 8), :]
            for d in (1, 2, 4):
                a_sh = jnp.where(row8 >= d, pltpu.roll(aa, d, axis=0), 1.0)
                u_sh = jnp.where(row8 >= d, pltpu.roll(uu, d, axis=0), 0.0)
                uu = uu + aa * u_sh
                aa = aa * a_sh
            hb = aa * hprev + uu
            h_sc[pl.ds(r0, 8), :] = hb
            return hb[7:8, :]

        hlast = lax.fori_loop(0, TM // 8, group, carry_sc[0:1, :])
        carry_sc[0:1, :] = hlast
        h_ref[...] = h_sc[...].astype(BF16)

    vec = pl.BlockSpec((1, D), lambda i: (0, 0))
    wspec = pl.BlockSpec((LRU_BLOCKS, LRU_BLOCK, LRU_BLOCK), lambda i: (0, 0, 0))
    return _hosted_call(
        body,
        name="lru_fwd",
        grid=(nt,),
        in_specs=[
            pl.BlockSpec((TM, D), lambda i: (i, LRU_IN_COL)),
            pl.BlockSpec((4, D), lambda i: (0, 0)),
            vec, wspec, wspec, vec, vec, vec,
        ],
        out_specs=[pl.BlockSpec((TM, D), lambda i: (i, 0)) for _ in range(6)],
        out_shape=[jax.ShapeDtypeStruct((rows, D), BF16) for _ in range(6)],
        scratch_shapes=[
            pltpu.VMEM((8, D), F32), pltpu.VMEM((8, D), F32),
            pltpu.VMEM((TM, D), F32), pltpu.VMEM((TM, D), F32), pltpu.VMEM((TM, D), F32), pltpu.VMEM((TM, D), F32),
        ],
        args=(proj, conv_w, conv_b, wa, wx, ba, bx, lam),
        riders=riders,
    )


def _group_norm(o):
    outs, rstds = [], []
    for h in range(HEADS):
        oh = o[:, h * DH:(h + 1) * DH]
        rs = lax.rsqrt(jnp.mean(oh * oh, axis=-1, keepdims=True) + EPS)
        outs.append(oh * rs)
        rstds.append(rs)
    return jnp.concatenate(outs, axis=1), rstds


def _mix_fwd(head, x2d, o, proj, h_lru, w_br, w_bl, w_o, ffn_norm_w, riders=()):
    rows = o.shape[0]
    nt = rows // TM

    def body(head_ref, x_ref, o_ref, gates_ref, hl_ref, wbr_ref, wbl_ref, wo_ref, nw_ref,
             yret_ref, ylru_ref, h1_ref, u2_ref, rstd_ref):
        i = pl.program_id(0)
        gate = lambda j: gates_ref[:, j * D:(j + 1) * D].astype(F32)
        on, _ = _group_norm(o_ref[...].astype(F32))
        gret = gate(0)
        a_ret = (gret * _sigmoid(gret) * on).astype(BF16)
        y_ret = _dot(a_ret, wbr_ref[...])
        gl, _ = _gelu_and_grad(gate(1))
        a_lru = (gl * hl_ref[...].astype(F32)).astype(BF16)
        y_lru = _dot(a_lru, wbl_ref[...])
        mixed = (_sigmoid(gate(2)) * y_ret + _sigmoid(gate(3)) * y_lru).astype(BF16)
        delta = _dot(mixed, wo_ref[...])
        yret_ref[...] = y_ret.astype(BF16)
        ylru_ref[...] = y_lru.astype(BF16)

        def finish(h0):
            h1 = h0 + delta
            rs = lax.rsqrt(jnp.mean(h1 * h1, axis=-1, keepdims=True) + EPS)
            h1_ref[...] = h1
            u2_ref[...] = ((h1 * rs) * nw_ref[...]).astype(BF16)
            rstd_ref[...] = rs

        @pl.when(i == 0)
        def _():
            finish(head_ref[...])

        @pl.when(i > 0)
        def _():
            finish(x_ref[...])

    tile = lambda col: pl.BlockSpec((TM, D), lambda i: (i, col))
    wspec = pl.BlockSpec((D, D), lambda i: (0, 0))
    return _hosted_call(
        body,
        name="mix_fwd",
        grid=(nt,),
        in_specs=[
            pl.BlockSpec((FRONT, D), lambda i: (0, 0)),
            pl.BlockSpec((TM, D), lambda i: (jnp.maximum(i - 1, 0), 0)),
            tile(0), pl.BlockSpec((TM, 4 * D), lambda i: (i, GATES_COL)), tile(0),
            wspec, wspec, wspec,
            pl.BlockSpec((1, D), lambda i: (0, 0)),
        ],
        out_specs=[tile(0), tile(0), tile(0), tile(0), pl.BlockSpec((TM, 1), lambda i: (i, 0))],
        out_shape=[
            jax.ShapeDtypeStruct((rows, D), BF16), jax.ShapeDtypeStruct((rows, D), BF16),
            jax.ShapeDtypeStruct((rows, D), F32), jax.ShapeDtypeStruct((rows, D), BF16),
            jax.ShapeDtypeStruct((rows, 1), F32),
        ],
        scratch_shapes=[],
        args=(head, x2d, o, proj, h_lru, w_br, w_bl, w_o, ffn_norm_w),
        riders=riders,
    )


def _ffn_fwd_gate(u2, w_ffn_in, riders=()):
    rows = u2.shape[0]
    tm = _heavy_tile(rows)
    nb = FFN // FFN_BLOCK
    hid = lambda: pl.BlockSpec((tm, FFN_BLOCK), lambda i, j: (i, j))

    def body(u2_ref, w_hbm, silu_ref, dsilu_ref, act_ref, w_sc, w_sem):
        j = pl.program_id(1)

        @pl.when(jnp.logical_and(pl.program_id(0) == 0, j == 0))
        def _():
            cp = pltpu.make_async_copy(w_hbm, w_sc, w_sem)
            cp.start()
            cp.wait()

        u2 = u2_ref[...]
        for a, b in ((0, FFN_SUB), (FFN_SUB, FFN_BLOCK)):
            col = pl.multiple_of(j * FFN_BLOCK + a, 128)
            g = _dot(u2, w_sc[:, pl.ds(col, b - a)])
            up = _dot(u2, w_sc[:, pl.ds(pl.multiple_of(FFN + col, 128), b - a)])
            sg = _sigmoid(g)
            silu = g * sg
            silu_ref[:, a:b] = silu.astype(BF16)
            dsilu_ref[:, a:b] = (up * (sg * (1.0 + g * (1.0 - sg)))).astype(BF16)
            act_ref[:, a:b] = (silu * up).astype(BF16)

    return _hosted_call(
        body,
        name="ffn_fwd_gate",
        grid=(rows // tm, nb),
        in_specs=[pl.BlockSpec((tm, D), lambda i, j: (i, 0)), _ANY],
        out_specs=[hid(), hid(), hid()],
        out_shape=[jax.ShapeDtypeStruct((rows, FFN), BF16)] * 3,
        scratch_shapes=[pltpu.VMEM((D, 2 * FFN), BF16), pltpu.SemaphoreType.DMA],
        args=(u2, w_ffn_in),
        riders=riders,
    )


def _ffn_out_loss(act, h1, w_ffn_out, target, final_norm_w):
    rows = act.shape[0]
    tm = _heavy_tile(rows)
    nt = rows // tm

    def body(act_ref, h1_ref, wo_ref, fnw_ref, tgt_hbm, dh2_ref, stats_ref, tgt_sc, tgt_sem):
        i = pl.program_id(0)
        slot = i % 2

        @pl.when(i == 0)
        def _():
            stats_ref[...] = jnp.zeros_like(stats_ref)
            tgt_sc[0, 0:FRONT, :] = jnp.zeros((FRONT, D), F32)
            cp = pltpu.make_async_copy(tgt_hbm.at[pl.ds(0, tm - FRONT)], tgt_sc.at[0, pl.ds(FRONT, tm - FRONT)], tgt_sem.at[0])
            cp.start()
            cp.wait()

        @pl.when(i + 1 < nt)
        def _():
            start = pl.multiple_of((i + 1) * tm - FRONT, FRONT)
            pltpu.make_async_copy(tgt_hbm.at[pl.ds(start, tm)], tgt_sc.at[1 - slot], tgt_sem.at[1 - slot]).start()

        @pl.when(i > 0)
        def _():
            pltpu.make_async_copy(tgt_hbm.at[pl.ds(0, tm)], tgt_sc.at[slot], tgt_sem.at[slot]).wait()

        h2 = h1_ref[...] + _dot(act_ref[...], wo_ref[...])
        rs = lax.rsqrt(jnp.mean(h2 * h2, axis=-1, keepdims=True) + EPS)
        n = h2 * rs
        fnw = fnw_ref[...]
        valid = _row_ids(i, tm, (tm, D)) >= FRONT
        diff = jnp.where(valid, n * fnw - tgt_sc[slot], 0.0)
        dy = diff * (1.0 / D)
        stats_ref[0:1, :] += (0.5 / D) * jnp.sum(diff * diff, axis=0, keepdims=True)
        stats_ref[1:2, :] += jnp.sum(dy * n, axis=0, keepdims=True)
        dn = dy * fnw
        dh2_ref[...] = rs * (dn - n * jnp.mean(dn * n, axis=-1, keepdims=True))

    return pl.pallas_call(
        body,
        name="ffn_out_loss",
        grid=(nt,),
        in_specs=[
            pl.BlockSpec((tm, FFN), lambda i: (i, 0)),
            pl.BlockSpec((tm, D), lambda i: (i, 0)),
            pl.BlockSpec((FFN, D), lambda i: (0, 0)),
            pl.BlockSpec((1, D), lambda i: (0, 0)),
            _ANY,
        ],
        out_specs=[pl.BlockSpec((tm, D), lambda i: (i, 0)), pl.BlockSpec((8, D), lambda i: (0, 0))],
        out_shape=[jax.ShapeDtypeStruct((rows, D), F32), jax.ShapeDtypeStruct((8, D), F32)],
        scratch_shapes=[pltpu.VMEM((2, tm, D), F32), pltpu.SemaphoreType.DMA((2,))],
        compiler_params=_params(("arbitrary",)),
    )(act, h1, w_ffn_out, final_norm_w, target)


def _ffn_bwd(dh2, silu, dsilu, h1, rstd2, w_ffn_in, w_ffn_out, ffn_norm_w):
    rows = dh2.shape[0]
    tm = _heavy_tile(rows)
    nb = FFN // FFN_BLOCK
    blk = lambda: pl.BlockSpec((tm, FFN_BLOCK), lambda i, j: (i, j))

    def gate_body(dh2_ref, silu_ref, dsilu_ref, wo_hbm, dg_ref, dup_ref, dh2b_ref, wo_sc, wo_sem):
        j = pl.program_id(1)

        @pl.when(jnp.logical_and(pl.program_id(0) == 0, j == 0))
        def _():
            cp = pltpu.make_async_copy(wo_hbm, wo_sc, wo_sem)
            cp.start()
            cp.wait()

        @pl.when(j == 0)
        def _():
            dh2b_ref[...] = dh2_ref[...].astype(BF16)

        dact = _dot_nt(dh2b_ref[...], wo_sc[pl.ds(pl.multiple_of(j * FFN_BLOCK, 128), FFN_BLOCK), :])
        dup_ref[...] = (dact * silu_ref[...].astype(F32)).astype(BF16)
        dg_ref[...] = (dact * dsilu_ref[...].astype(F32)).astype(BF16)

    dg, dup, dh2b = pl.pallas_call(
        gate_body,
        name="ffn_bwd_gate",
        grid=(rows // tm, nb),
        in_specs=[pl.BlockSpec((tm, D), lambda i, j: (i, 0)), blk(), blk(), _ANY],
        out_specs=[blk(), blk(), pl.BlockSpec((tm, D), lambda i, j: (i, 0))],
        out_shape=[jax.ShapeDtypeStruct((rows, FFN), BF16)] * 2 + [jax.ShapeDtypeStruct((rows, D), BF16)],
        scratch_shapes=[pltpu.VMEM((FFN, D), BF16), pltpu.SemaphoreType.DMA],
        compiler_params=_params(("arbitrary", "arbitrary")),
    )(dh2, silu, dsilu, w_ffn_out)

    def body(dg_ref, dup_ref, dh2_ref, h1_ref, rstd_ref, w_hbm, nw_ref, dh1_ref, stats_ref, w_sc, w_sem):
        @pl.when(pl.program_id(0) == 0)
        def _():
            stats_ref[...] = jnp.zeros_like(stats_ref)
            cp = pltpu.make_async_copy(w_hbm, w_sc, w_sem)
            cp.start()
            cp.wait()

        du = _dot_nt(dg_ref[...], w_sc[:, 0:FFN]) + _dot_nt(dup_ref[...], w_sc[:, FFN:2 * FFN])
        rs = rstd_ref[...]
        n = h1_ref[...] * rs
        stats_ref[0:1, :] += jnp.sum(du * n, axis=0, keepdims=True)
        dn = du * nw_ref[...]
        dh1_ref[...] = dh2_ref[...] + rs * (dn - n * jnp.mean(dn * n, axis=-1, keepdims=True))

    row = lambda width: pl.BlockSpec((tm, width), lambda i: (i, 0))
    dh1, stats = pl.pallas_call(
        body,
        name="ffn_bwd_in",
        grid=(rows // tm,),
        in_specs=[row(FFN), row(FFN), row(D), row(D), row(1), _ANY, pl.BlockSpec((1, D), lambda i: (0, 0))],
        out_specs=[row(D), pl.BlockSpec((8, D), lambda i: (0, 0))],
        out_shape=[jax.ShapeDtypeStruct((rows, D), F32), jax.ShapeDtypeStruct((8, D), F32)],
        scratch_shapes=[pltpu.VMEM((D, 2 * FFN), BF16), pltpu.SemaphoreType.DMA],
        compiler_params=_params(("arbitrary",)),
    )(dg, dup, dh2, h1, rstd2, w_ffn_in, ffn_norm_w)
    return dg, dup, dh2b, dh1, stats


def _mix_bwd(dh1, o, proj, h_lru, y_ret, y_lru, w_br, w_bl, w_o, riders=()):
    rows = dh1.shape[0]
    tm = TM_MIX_BWD
    nt = rows // tm

    def body(dh1_ref, o_ref, gates_ref, hl_ref, yret_ref, ylru_ref, wbr_ref, wbl_ref, wo_ref,
             dproj_ref, do_ref, dhl_ref, mixed_ref, aret_ref, alru_ref, dyret_ref, dylru_ref):
        gate = lambda j: gates_ref[:, j * D:(j + 1) * D].astype(F32)
        dmixed = _dot_nt(dh1_ref[...].astype(BF16), wo_ref[...])
        y_ret, y_lru = yret_ref[...].astype(F32), ylru_ref[...].astype(F32)
        sa, sb = _sigmoid(gate(2)), _sigmoid(gate(3))
        mixed_ref[...] = (sa * y_ret + sb * y_lru).astype(BF16)
        dga = dmixed * y_ret * sa * (1.0 - sa)
        dgb = dmixed * y_lru * sb * (1.0 - sb)
        dy_ret = (dmixed * sa).astype(BF16)
        dy_lru = (dmixed * sb).astype(BF16)
        dyret_ref[...] = dy_ret
        dylru_ref[...] = dy_lru
        da_ret = _dot_nt(dy_ret, wbr_ref[...])
        da_lru = _dot_nt(dy_lru, wbl_ref[...])

        gret = gate(0)
        sg = _sigmoid(gret)
        silu = gret * sg
        on, rstds = _group_norm(o_ref[...].astype(F32))
        aret_ref[...] = (silu * on).astype(BF16)
        dgret = da_ret * on * (sg * (1.0 + gret * (1.0 - sg)))
        don = da_ret * silu
        for h in range(HEADS):
            sl = slice(h * DH, (h + 1) * DH)
            onh, donh = on[:, sl], don[:, sl]
            do_ref[:, sl] = (rstds[h] * (donh - onh * jnp.mean(donh * onh, axis=-1, keepdims=True))).astype(BF16)

        gl, gl_grad = _gelu_and_grad(gate(1))
        hl = hl_ref[...].astype(F32)
        alru_ref[...] = (gl * hl).astype(BF16)
        dlgate = da_lru * hl * gl_grad
        dhl_ref[...] = (da_lru * gl).astype(BF16)

        for j, val in enumerate((dgret, dlgate, dga, dgb)):
            dproj_ref[:, j * D:(j + 1) * D] = val.astype(BF16)

    tile = lambda col: pl.BlockSpec((tm, D), lambda i: (i, col))
    gates = pl.BlockSpec((tm, 4 * D), lambda i: (i, GATES_COL))
    wspec = pl.BlockSpec((D, D), lambda i: (0, 0))
    bf = lambda: jax.ShapeDtypeStruct((rows, D), BF16)
    return _hosted_call(
        body,
        name="mix_bwd",
        grid=(nt,),
        in_specs=[tile(0), tile(0), gates, tile(0), tile(0), tile(0), wspec, wspec, wspec],
        out_specs=[pl.BlockSpec((tm, 4 * D), lambda i: (i, GATES_COL))] + [tile(0)] * 7,
        out_shape=[jax.ShapeDtypeStruct((rows, IN_COLS), BF16)] + [bf() for _ in range(7)],
        scratch_shapes=[],
        args=(dh1, o, proj, h_lru, y_ret, y_lru, w_br, w_bl, w_o),
        riders=riders,
    )


def _retention_bwd(dproj, proj_bf, do, sprev, intra, intra_t, q_dec, k_dec, c_dec, cos_t, sin_t, riders=()):
    rows = proj_bf.shape[0]
    nc = rows // CHUNK

    def body(dproj_in_ref, q_ref, k_ref, v_ref, do_ref, sprev_ref, m_ref, mt_ref, qd_ref, kd_ref, cd_ref, cos_ref, sin_ref,
             dproj_ref, ds_sc):
        @pl.when(pl.program_id(0) == 0)
        def _():
            ds_sc[...] = jnp.zeros_like(ds_sc)

        cos, sin = cos_ref[...], sin_ref[...]

        def unrotate(dy):
            return dy * cos - pltpu.roll(dy, DH // 2, axis=1) * sin

        for h in range(HEADS):
            sl = slice(h * DH, (h + 1) * DH)
            q, k, v = q_ref[:, sl], k_ref[:, sl], v_ref[:, sl]
            do = do_ref[:, sl]
            dob = do.astype(BF16)
            doq = (do * qd_ref[:, sl]).astype(BF16)
            state_prev = sprev_ref[0, h]
            dstate = ds_sc[h]
            dstate_b = dstate.astype(BF16)
            s_t = (_dot_nt(k, q) * mt_ref[h]).astype(BF16)
            ds_t = (_dot_nt(v, dob) * mt_ref[h]).astype(BF16)
            ds = (_dot_nt(dob, v) * m_ref[h]).astype(BF16)
            kd = kd_ref[:, sl]
            dq = _dot(ds, k) + _dot_nt(doq, state_prev)
            dk = _dot(ds_t, q) + _dot_nt(v, dstate_b) * kd
            k_scaled = (k.astype(F32) * kd).astype(BF16)
            dv = _dot(s_t, dob) + _dot(k_scaled, dstate_b)
            ds_sc[h] = dstate * cd_ref[:, sl] + _dot_tn(q, doq)
            dproj_ref[:, sl] = unrotate(dq).astype(BF16)
            dproj_ref[:, D + h * DH:D + (h + 1) * DH] = (unrotate(dk) * QK_SCALE).astype(BF16)
            dproj_ref[:, 2 * D + h * DH:2 * D + (h + 1) * DH] = dv.astype(BF16)

    rev = lambda c: nc - 1 - c
    chunk_spec = lambda col: pl.BlockSpec((CHUNK, D), lambda c: (rev(c), col))
    const2 = lambda shape: pl.BlockSpec(shape, lambda c: (0, 0))
    const3 = pl.BlockSpec((HEADS, CHUNK, CHUNK), lambda c: (0, 0, 0))
    return _hosted_call(
        body,
        name="retention_bwd",
        grid=(nc,),
        in_specs=[
            pl.BlockSpec(memory_space=pl.ANY),
            chunk_spec(0), chunk_spec(1), chunk_spec(2), chunk_spec(0),
            pl.BlockSpec((1, HEADS, DH, DH), lambda c: (rev(c), 0, 0, 0)),
            const3, const3,
            const2((CHUNK, D)), const2((CHUNK, D)), const2((1, D)),
            pl.BlockSpec((CHUNK, DH), lambda c: (rev(c), 0)),
            pl.BlockSpec((CHUNK, DH), lambda c: (rev(c), 0)),
        ],
        out_specs=[pl.BlockSpec((CHUNK, 3 * D), lambda c: (rev(c), 0))],
        out_shape=[jax.ShapeDtypeStruct(dproj.shape, BF16)],
        aliases={0: 0},
        scratch_shapes=[pltpu.VMEM((HEADS, DH, DH), F32)],
        args=(dproj, proj_bf, proj_bf, proj_bf, do, sprev, intra, intra_t, q_dec, k_dec, c_dec, cos_t, sin_t),
        riders=riders,
    )


def _lru_bwd(dproj, proj, saved, dhl, conv_w, wa, wx, lam, riders=()):
    rows = proj.shape[0]
    TM = _lru_tile(rows)
    nt = rows // TM
    per8 = TM // 8

    def body(dproj_in_ref, x_ref, xprev_ref, h_ref, hprev_ref, c_ref, r_ref, i_ref, la_ref, mult_ref, dhl_ref,
             cw_ref, wa_ref, wx_ref, lam_ref,
             dproj_ref, dwa_ref, dwx_ref, stats_ref, anext_sc, dhnext_sc, dcnext_sc, c_sc, b_sc, dh_sc):
        step = pl.program_id(0)
        i = nt - 1 - step

        @pl.when(step == 0)
        def _():
            anext_sc[...] = jnp.zeros_like(anext_sc)
            dhnext_sc[...] = jnp.zeros_like(dhnext_sc)
            dcnext_sc[...] = jnp.zeros_like(dcnext_sc)
            dwa_ref[...] = jnp.zeros_like(dwa_ref)
            dwx_ref[...] = jnp.zeros_like(dwx_ref)
            stats_ref[...] = jnp.zeros_like(stats_ref)

        first = i == 0
        x = x_ref[...].astype(F32)
        prev8 = jnp.where(first, 0.0, xprev_ref[8:16, :].astype(F32))
        c_b = c_ref[...]
        c, r, gate_i, mult = (ref[...].astype(F32) for ref in (c_ref, r_ref, i_ref, mult_ref))
        a = jnp.exp(la_ref[...].astype(F32))
        sp = _softplus(-lam_ref[...])
        dh_sc[...] = dhl_ref[...].astype(F32)

        b_sc[...] = pltpu.roll(a, TM - 1, axis=0)
        b_sc[TM - 1:TM, :] = anext_sc[0:1, :]
        anext_sc[0:1, :] = a[0:1, :]
        row8 = lax.broadcasted_iota(jnp.int32, (8, D), 0)

        def group(gi, dhnext):
            r0 = pl.multiple_of((per8 - 1 - gi) * 8, 8)
            bb = b_sc[pl.ds(r0, 8), :]
            uu = dh_sc[pl.ds(r0, 8), :]
            for d in (1, 2, 4):
                b_sh = jnp.where(row8 < 8 - d, pltpu.roll(bb, 8 - d, axis=0), 1.0)
                u_sh = jnp.where(row8 < 8 - d, pltpu.roll(uu, 8 - d, axis=0), 0.0)
                uu = uu + bb * u_sh
                bb = bb * b_sh
            dhb = bb * dhnext + uu
            dh_sc[pl.ds(r0, 8), :] = dhb
            return dhb[0:1, :]

        dhfirst = lax.fori_loop(0, per8, group, dhnext_sc[0:1, :])
        dhnext_sc[0:1, :] = dhfirst
        dh = dh_sc[...]

        h = h_ref[...].astype(F32)
        hprev8 = jnp.where(first, 0.0, hprev_ref[8:16, :].astype(F32))
        h_dn, h_head = _shift_down(h, hprev8, 1)
        c_sc[...] = h_dn
        c_sc[0:8, :] = h_head
        h_before = c_sc[...]

        valid = _row_ids(i, TM, (TM, D)) >= PAD_ROWS
        da = dh * h_before
        du = jnp.where(valid, dh, 0.0)
        dmult = du * gate_i * c
        dgate_i = du * mult * c
        dc = du * mult * gate_i
        dla = da * a - dmult * (a * a) / mult
        dla = jnp.where(valid, dla, 0.0)
        dr = dla * ((-LRU_C) * sp)
        dzr = dr * r * (1.0 - r)
        dzi = dgate_i * gate_i * (1.0 - gate_i)
        stats_ref[1:2, :] += jnp.sum(dzr, axis=0, keepdims=True)
        stats_ref[2:3, :] += jnp.sum(dzi, axis=0, keepdims=True)
        stats_ref[3:4, :] += jnp.sum(dla * ((-LRU_C) * r), axis=0, keepdims=True)
        dc_gate = []
        for g in range(LRU_BLOCKS):
            sl = slice(g * LRU_BLOCK, (g + 1) * LRU_BLOCK)
            cg = c_b[:, sl]
            dzr_g = dzr[:, sl].astype(BF16)
            dzi_g = dzi[:, sl].astype(BF16)
            dc_gate.append(_dot_nt(dzr_g, wa_ref[g]) + _dot_nt(dzi_g, wx_ref[g]))
            dwa_ref[g] += _dot_tn(cg, dzr_g)
            dwx_ref[g] += _dot_tn(cg, dzi_g)
        dc = dc + jnp.concatenate(dc_gate, axis=1)

        cw = cw_ref[...]
        stats_ref[0:1, :] += jnp.sum(dc, axis=0, keepdims=True)
        stats_ref[7:8, :] += jnp.sum(dc * x, axis=0, keepdims=True)
        dx = cw[3:4] * dc
        tail_src = jnp.concatenate([dc[TM - 8:TM], dcnext_sc[...]], axis=0)
        dx_tail = cw[3:4] * dc[TM - 8:TM]
        for d in (1, 2, 3):
            dx = dx + cw[3 - d:4 - d] * pltpu.roll(dc, TM - d, axis=0)
            dx_tail = dx_tail + cw[3 - d:4 - d] * pltpu.roll(tail_src, 16 - d, axis=0)[0:8]
            rolled, hd = _shift_down(x, prev8, d)
            b_sc[...] = rolled
            b_sc[0:8, :] = hd
            stats_ref[7 - d:8 - d, :] += jnp.sum(dc * b_sc[...], axis=0, keepdims=True)
        dcnext_sc[...] = dc[0:8]
        dproj_ref[...] = dx.astype(BF16)
        dproj_ref[TM - 8:TM, :] = dx_tail.astype(BF16)

    rev = lambda s: nt - 1 - s
    vec = pl.BlockSpec((1, D), lambda s: (0, 0))
    wspec = pl.BlockSpec((LRU_BLOCKS, LRU_BLOCK, LRU_BLOCK), lambda s: (0, 0, 0))
    prev16 = lambda col: pl.BlockSpec((16, D), lambda s: (jnp.maximum(rev(s) * (TM // 16) - 1, 0), col))
    tile = lambda: pl.BlockSpec((TM, D), lambda s: (rev(s), 0))
    h_lru, c_sv, r_sv, i_sv, la_sv, mult_sv = saved
    return _hosted_call(
        body,
        name="lru_bwd",
        grid=(nt,),
        in_specs=[
            pl.BlockSpec(memory_space=pl.ANY),
            pl.BlockSpec((TM, D), lambda s: (rev(s), LRU_IN_COL)), prev16(LRU_IN_COL),
            tile(), prev16(0),
            tile(), tile(), tile(), tile(), tile(), tile(),
            pl.BlockSpec((4, D), lambda s: (0, 0)),
            wspec, wspec, vec,
        ],
        out_specs=[
            pl.BlockSpec((TM, D), lambda s: (rev(s), LRU_IN_COL)),
            wspec, wspec,
            pl.BlockSpec((8, D), lambda s: (0, 0)),
        ],
        out_shape=[
            jax.ShapeDtypeStruct(dproj.shape, BF16),
            jax.ShapeDtypeStruct((LRU_BLOCKS, LRU_BLOCK, LRU_BLOCK), F32),
            jax.ShapeDtypeStruct((LRU_BLOCKS, LRU_BLOCK, LRU_BLOCK), F32),
            jax.ShapeDtypeStruct((8, D), F32),
        ],
        aliases={0: 0},
        scratch_shapes=[
            pltpu.VMEM((8, D), F32), pltpu.VMEM((8, D), F32), pltpu.VMEM((8, D), F32),
            pltpu.VMEM((TM, D), F32), pltpu.VMEM((TM, D), F32), pltpu.VMEM((TM, D), F32),
        ],
        args=(dproj, proj, proj, h_lru, h_lru, c_sv, r_sv, i_sv, la_sv, mult_sv, dhl, conv_w, wa, wx, lam),
        riders=riders,
    )


def _in_proj_bwd(dproj, w_in, part, prev=None, riders=()):
    rows = dproj.shape[0]
    tm = _heavy_tile(rows)
    nt = rows // tm
    first = 0 if part == 0 else (nt + 1) // 2
    count = (nt + 1) // 2 if part == 0 else nt - first

    def body(*refs):
        dproj_ref, w_hbm, du_ref, w_sc, w_sem = refs[-5:]

        @pl.when(pl.program_id(0) == 0)
        def _():
            moves = [((0, 3), 0), ((4, 1), 3), ((3, 1), 4), ((5, 3), 5)]
            copies = [pltpu.make_async_copy(w_hbm.at[pl.ds(src * D, n * D)], w_sc.at[pl.ds(dst * D, n * D)], w_sem.at[q])
                      for q, ((src, n), dst) in enumerate(moves)]
            for cp in copies:
                cp.start()
            for cp in copies:
                cp.wait()

        du_ref[...] = _dot(dproj_ref[...], w_sc[...])

    in_specs = [pl.BlockSpec((tm, IN_COLS), lambda i: (first + i, 0)), _ANY]
    args = (dproj, w_in)
    if prev is not None:
        in_specs = [_ANY] + in_specs
        args = (prev,) + args
    return _hosted_call(
        body,
        name="in_proj_bwd_%d" % part,
        grid=(count,),
        in_specs=in_specs,
        out_specs=[pl.BlockSpec((tm, D), lambda i: (first + i, 0))],
        out_shape=[jax.ShapeDtypeStruct((rows, D), F32)],
        scratch_shapes=[pltpu.VMEM((IN_COLS, D), BF16), pltpu.SemaphoreType.DMA((4,))],
        aliases={0: 0} if prev is not None else None,
        args=args,
        riders=riders,
    )


def _norm1_bwd(du, dh1, head, x2d, rstd1, norm_w, riders=()):
    rows = du.shape[0]

    def body(du_ref, dh1_ref, head_ref, x_ref, rstd_ref, nw_ref, gx_ref, ghead_ref, stats_ref):
        i = pl.program_id(0)

        @pl.when(i == 0)
        def _():
            stats_ref[...] = jnp.zeros_like(stats_ref)

        def finish(h0, out_ref):
            du = du_ref[...]
            rs = rstd_ref[...]
            n = h0 * rs
            stats_ref[0:1, :] += jnp.sum(du * n, axis=0, keepdims=True)
            dn = du * nw_ref[...]
            out_ref[...] = dh1_ref[...] + rs * (dn - n * jnp.mean(dn * n, axis=-1, keepdims=True))

        @pl.when(i == 0)
        def _():
            finish(head_ref[...], ghead_ref)

        @pl.when(i > 0)
        def _():
            finish(x_ref[...], gx_ref)

    tile = pl.BlockSpec((TM, D), lambda i: (i, 0))
    return _hosted_call(
        body,
        name="norm1_bwd",
        grid=(rows // TM,),
        in_specs=[
            tile, tile,
            pl.BlockSpec((FRONT, D), lambda i: (0, 0)),
            pl.BlockSpec((TM, D), lambda i: (jnp.maximum(i - 1, 0), 0)),
            pl.BlockSpec((TM, 1), lambda i: (i, 0)),
            pl.BlockSpec((1, D), lambda i: (0, 0)),
        ],
        out_specs=[
            pl.BlockSpec((TM, D), lambda i: (jnp.maximum(i - 1, 0), 0)),
            pl.BlockSpec((FRONT, D), lambda i: (0, 0)),
            pl.BlockSpec((8, D), lambda i: (0, 0)),
        ],
        out_shape=[
            jax.ShapeDtypeStruct(x2d.shape, F32),
            jax.ShapeDtypeStruct((FRONT, D), F32),
            jax.ShapeDtypeStruct((8, D), F32),
        ],
        scratch_shapes=[],
        args=(du, dh1, head, x2d, rstd1, norm_w),
        riders=riders,
    )


def _matmul_tn(name, x, dy, out_cols, col0=0, prev=None, k_block=None, n_block=None, riders=(), col_map=None):
    col_map = col_map or (lambda n: n)
    rows, kdim = x.shape
    ndim = dy.shape[1]
    kb = k_block or kdim
    nb = n_block or ndim
    step_bytes = lambda t: 2 * t * (kb * x.dtype.itemsize + nb * dy.dtype.itemsize) + 2 * kb * nb * 4
    tr = next(t for t in (2816, 1408, TM_HEAVY, TM) if rows % t == 0 and (t == TM or step_bytes(t) <= TN_VMEM_BUDGET))
    nr, nk, nn = rows // tr, kdim // kb, ndim // nb
    cb0 = col0 // nb

    def body(*refs):
        x_ref, dy_ref, out_ref = refs[-3], refs[-2], refs[-1]
        part = _dot_tn(x_ref[...].astype(BF16), dy_ref[...].astype(BF16))

        @pl.when(pl.program_id(2) == 0)
        def _():
            out_ref[...] = part

        @pl.when(pl.program_id(2) > 0)
        def _():
            out_ref[...] += part

    in_specs = [
        pl.BlockSpec((tr, kb), lambda n, k, r: (r, k)),
        pl.BlockSpec((tr, nb), lambda n, k, r: (r, n)),
    ]
    args = [x, dy]
    aliases = {}
    if prev is not None:
        in_specs = [pl.BlockSpec(memory_space=pl.ANY)] + in_specs
        args = [prev] + args
        aliases = {0: 0}
    (out,), rider_outs = _hosted_call(
        body,
        name=name,
        grid=(nn, nk, nr),
        in_specs=in_specs,
        out_specs=[pl.BlockSpec((kb, nb), lambda n, k, r: (k, cb0 + col_map(n)))],
        out_shape=[jax.ShapeDtypeStruct((kdim, out_cols), F32)],
        scratch_shapes=[],
        aliases=aliases,
        args=args,
        riders=riders,
    )
    return (out, rider_outs) if riders else out


def _local_step(x2d, target, w, plan):
    rows = FRONT + x2d.shape[0]
    head = jnp.concatenate([jnp.zeros((PAD_ROWS, D), F32), w["meta_tokens"]], axis=0)
    cos_t, sin_t = _rope_tables(rows)
    intra, intra_t, q_dec, k_dec, c_dec = _decay_tables()
    grads = {}

    def hosted(host, fn, *args, **kwargs):
        outs, rider_outs = fn(*args, riders=plan.riders(host, w, grads), **kwargs)
        plan.after(host, rider_outs, w, grads)
        return outs

    (u1, rstd1), _ = _norm1(head, x2d, w["mix_norm_w"])
    proj, w["w_in"] = hosted("in_proj", _in_proj, u1, w["w_in_shard"], w["route"], cos_t, sin_t)
    o, sprev = hosted("retention_fwd", _retention_fwd, proj, intra, q_dec, k_dec, c_dec)
    lru_args = (w["conv_w"], w["conv_b"], w["lru_wa"], w["lru_wx"], w["lru_ba"], w["lru_bx"], w["lru_lambda"])
    lru_saved = hosted("lru_fwd", _lru_fwd, proj, *lru_args)
    h_lru = lru_saved[0]
    y_ret, y_lru, h1, u2, rstd2 = hosted("mix_fwd", _mix_fwd, head, x2d, o, proj, h_lru, w["w_branch_ret"],
                                         w["w_branch_lru"], w["w_out"], w["ffn_norm_w"])
    silu, dsilu, act = hosted("ffn_fwd_gate", _ffn_fwd_gate, u2, w["w_ffn_in"])
    dh2, stats_loss = _ffn_out_loss(act, h1, w["w_ffn_out"], target, w["final_norm_w"])

    dg, dup, dh2b, dh1, stats_ffn = _ffn_bwd(dh2, silu, dsilu, h1, rstd2, w["w_ffn_in"], w["w_ffn_out"], w["ffn_norm_w"])
    grads["w_ffn_in"] = _matmul_tn("dw_ffn_up", u2, dup, 2 * FFN, col0=FFN, n_block=FFN_HALF,
                                   prev=_matmul_tn("dw_ffn_gate", u2, dg, 2 * FFN, n_block=FFN_HALF))
    grads["w_ffn_out"] = _matmul_tn("dw_ffn_out", act, dh2b, D, k_block=FFN_HALF)
    (dproj, do, dhl, mixed, a_ret, a_lru, dy_ret, dy_lru) = hosted(
        "mix_bwd", _mix_bwd, dh1, o, proj, h_lru, y_ret, y_lru, w["w_branch_ret"], w["w_branch_lru"], w["w_out"])
    grads["w_out"] = _matmul_tn("dw_out", mixed, dh1, D)
    grads["w_branch_ret"] = _matmul_tn("dw_branch_ret", a_ret, dy_ret, D)
    grads["w_branch_lru"] = _matmul_tn("dw_branch_lru", a_lru, dy_lru, D)
    (dproj,) = hosted("retention_bwd", _retention_bwd, dproj, proj, do, sprev, intra, intra_t, q_dec, k_dec, c_dec,
                      cos_t, sin_t)
    dproj, grads["lru_wa"], grads["lru_wx"], stats_lru = hosted(
        "lru_bwd", _lru_bwd, dproj, proj, lru_saved, dhl, w["conv_w"], w["lru_wa"], w["lru_wx"], w["lru_lambda"])
    grads["w_in"], rider_outs = _matmul_tn("dw_in", u1, dproj, IN_COLS, n_block=D, col_map=_swap_3_4,
                                           riders=plan.riders("dw_in", w, grads))
    plan.after("dw_in", rider_outs, w, grads)
    w_in_t = jnp.transpose(w["w_in"])
    (du1,) = hosted("in_proj_bwd_0", _in_proj_bwd, dproj, w_in_t, 0)
    (du1,) = hosted("in_proj_bwd_1", _in_proj_bwd, dproj, w_in_t, 1, du1)
    (grad_x, grad_head, stats_in), _ = _norm1_bwd(du1, dh1, head, x2d, rstd1, w["mix_norm_w"])
    return grad_x, grad_head, grads, [stats_loss, stats_ffn, stats_in, stats_lru]


BIG_PIECES = {
    "w_in": ("col", (D, 2 * D)),
    "w_ffn_in": ("col", (D, FFN_HALF)),
    "w_ffn_out": ("row", (FFN // 4, D)),
    "w_branch_ret": ("row", (D // 4, D)),
    "w_branch_lru": ("row", (D // 4, D)),
    "w_out": ("row", (D // 4, D)),
    "lru_wa": ("lru", (LRU_BLOCKS, LRU_BLOCK // 4, LRU_BLOCK)),
    "lru_wx": ("lru", (LRU_BLOCKS, LRU_BLOCK // 4, LRU_BLOCK)),
}
SMALL_PIECES = {"meta_tokens": ("col", (N_META, D // 4)), "conv_w": ("col", (4, D // 4))}


def _full_shape(kind, shard):
    if kind == "col":
        return (shard[0], 4 * shard[1])
    if kind == "row":
        return (4 * shard[0], shard[1])
    return (shard[0], 4 * shard[1], shard[2])


def _half_shape(kind, shard):
    return (shard[0] // 2,) + tuple(shard[1:])


def _aligned(start, multiple):
    return start if isinstance(start, int) else pl.multiple_of(start, multiple)


def _lead(h, size):
    if h is None:
        return pl.ds(0, size)
    return pl.ds(_aligned(h * (size // 2), size // 2), size // 2)


def _full_region(ref, kind, shard, s, h):
    if kind == "col":
        return ref.at[_lead(h, shard[0]), pl.ds(_aligned(s * shard[1], shard[1]), shard[1])]
    if kind == "row":
        size = shard[0] if h is None else shard[0] // 2
        start = s * shard[0] + (0 if h is None else h * (shard[0] // 2))
        return ref.at[pl.ds(_aligned(start, 16), size), :]
    return ref.at[_lead(h, shard[0]), pl.ds(_aligned(s * shard[1], shard[1]), shard[1]), :]


def _shard_region(ref, shard, h):
    return ref.at[_lead(h, shard[0])]


def _place():
    x, y, c = lax.axis_index("x"), lax.axis_index("y"), lax.axis_index("c")
    return x, y, c, 2 * x + y


def _other_chip(s, c, k):
    s2 = jnp.bitwise_xor(s, k)
    return s2, (s2 // 2, s2 % 2, c)


def _remote(src, dst, send_sem, recv_sem, dev):
    return pltpu.make_async_remote_copy(src_ref=src, dst_ref=dst, send_sem=send_sem, recv_sem=recv_sem,
                                        device_id=dev, device_id_type=MESH)


_ANY = pl.BlockSpec(memory_space=pl.ANY)


class _Rider:
    def __init__(self, ins, out_shapes, sem_shapes, build, aliased=False):
        self.ins, self.out_shapes, self.sem_shapes, self.build, self.aliased = ins, out_shapes, sem_shapes, build, aliased


def _hosted_call(body, *, name, grid, in_specs, out_specs, out_shape, scratch_shapes, args, riders=(), aliases=None,
                 prefetch=None, riders_after_body=False):
    n_in, n_out, n_sc = len(in_specs), len(out_shape), len(scratch_shapes)
    r_in = [a for r in riders for a in r.ins]
    r_out = [s for r in riders for s in r.out_shapes]
    r_sem = [s for r in riders for s in r.sem_shapes]
    lead = () if prefetch is None else (prefetch,)
    assert prefetch is None or not (aliases or any(r.aliased for r in riders))

    def full_body(*refs):
        head, refs = refs[:len(lead)], refs[len(lead):]
        ins, rin = refs[:n_in], refs[n_in:n_in + len(r_in)]
        o0 = n_in + len(r_in)
        outs, rout = refs[o0:o0 + n_out], refs[o0 + n_out:o0 + n_out + len(r_out)]
        s0 = o0 + n_out + len(r_out)
        scratch, rsem = refs[s0:s0 + n_sc], refs[s0 + n_sc:]
        starts, waits = [], []
        pi = po = ps = 0
        for r in riders:
            st, wt = r.build(rin[pi:pi + len(r.ins)], rout[po:po + len(r.out_shapes)], rsem[ps:ps + len(r.sem_shapes)])
            starts += st
            waits += wt
            pi, po, ps = pi + len(r.ins), po + len(r.out_shapes), ps + len(r.sem_shapes)
        first = functools.reduce(jnp.logical_and, [pl.program_id(d) == 0 for d in range(len(grid))])
        last = functools.reduce(jnp.logical_and, [pl.program_id(d) == grid[d] - 1 for d in range(len(grid))])

        def start_riders():
            @pl.when(first)
            def _():
                for cp in starts:
                    cp.start()

        if riders and not riders_after_body:
            start_riders()
        body(*head, *ins, *outs, *scratch)
        if riders and riders_after_body:
            start_riders()
        if riders:
            @pl.when(last)
            def _():
                for wait in waits:
                    wait()

    io_aliases = dict(aliases or {})
    pi = po = 0
    for r in riders:
        if r.aliased:
            for q in range(len(r.ins)):
                io_aliases[n_in + pi + q] = n_out + po + q
        pi, po = pi + len(r.ins), po + len(r.out_shapes)
    specs = dict(
        grid=grid,
        in_specs=list(in_specs) + [_ANY] * len(r_in),
        out_specs=list(out_specs) + [_ANY] * len(r_out),
        scratch_shapes=list(scratch_shapes) + r_sem,
    )
    if prefetch is not None:
        specs = dict(grid_spec=pltpu.PrefetchScalarGridSpec(num_scalar_prefetch=1, **specs))
    res = pl.pallas_call(
        full_body,
        name=name,
        out_shape=list(out_shape) + r_out,
        input_output_aliases=io_aliases,
        compiler_params=_params(("arbitrary",) * len(grid)),
        **specs,
    )(*lead, *args, *r_in)
    rider_outs, po = [], n_out
    for r in riders:
        rider_outs.append(list(res[po:po + len(r.out_shapes)]))
        po += len(r.out_shapes)
    return list(res[:n_out]), rider_outs


def _run_riders(name, riders):
    r_in = [a for r in riders for a in r.ins]
    r_out = [s for r in riders for s in r.out_shapes]
    r_sem = [s for r in riders for s in r.sem_shapes]

    def body(*refs):
        rin, rout, rsem = refs[:len(r_in)], refs[len(r_in):len(r_in) + len(r_out)], refs[len(r_in) + len(r_out):]
        pi = po = ps = 0
        all_waits = []
        for r in riders:
            starts, waits = r.build(rin[pi:pi + len(r.ins)], rout[po:po + len(r.out_shapes)], rsem[ps:ps + len(r.sem_shapes)])
            for cp in starts:
                cp.start()
            all_waits += waits
            pi, po, ps = pi + len(r.ins), po + len(r.out_shapes), ps + len(r.sem_shapes)
        for wait in all_waits:
            wait()

    io_aliases = {}
    pi = po = 0
    for r in riders:
        if r.aliased:
            for q in range(len(r.ins)):
                io_aliases[pi + q] = po + q
        pi, po = pi + len(r.ins), po + len(r.out_shapes)
    res = pl.pallas_call(
        body,
        name=name,
        in_specs=[_ANY] * len(r_in),
        out_specs=[_ANY] * len(r_out),
        out_shape=r_out,
        scratch_shapes=r_sem,
        input_output_aliases=io_aliases,
    )(*r_in)
    outs, po = [], 0
    for r in riders:
        outs.append(list(res[po:po + len(r.out_shapes)]))
        po += len(r.out_shapes)
    return outs


def _piece(name):
    if name in BIG_PIECES:
        return (name, *BIG_PIECES[name], True)
    return (name, *SMALL_PIECES[name], False)


def _gather_rider(shards, names):
    pieces = [_piece(n) for n in names]
    n = len(pieces)

    def build(ins, outs, sems):
        local_sem, ici_send, ici_recv = sems
        _, _, c, s = _place()
        starts, waits = [], []
        for p, (_, kind, shard, split) in enumerate(pieces):
            cp = pltpu.make_async_copy(ins[p], _full_region(outs[p], kind, shard, s, None), local_sem.at[p])
            starts.append(cp)
            waits.append(cp.wait)
            h = c if split else None
            for k in (1, 2, 3):
                s2, dev = _other_chip(s, c, k)
                cp = _remote(_shard_region(ins[p], shard, h), _full_region(outs[p], kind, shard, s, h),
                             ici_send.at[p, k - 1], ici_recv.at[p, k - 1], dev)
                starts.append(cp)
                waits.append(cp.wait_send)
                region = _full_region(outs[p], kind, shard, s2, h)
                waits.append(_remote(region, region, ici_send.at[p, k - 1], ici_recv.at[p, k - 1], dev).wait_recv)
        return starts, waits

    return _Rider(
        [shards[name] for name in names],
        [jax.ShapeDtypeStruct(_full_shape(kind, shard), shards[name].dtype) for name, kind, shard, _ in pieces],
        [pltpu.SemaphoreType.DMA((n,)), pltpu.SemaphoreType.DMA((n, 3)), pltpu.SemaphoreType.DMA((n, 3))],
        build)


def _forward_rider(gathered, names):
    pieces = [_piece(n) for n in names]
    n = len(pieces)

    def build(ins, outs, sems):
        fwd_send, fwd_recv = sems
        x, y, c, s = _place()
        sibling = (x, y, 1 - c)
        starts, waits = [], []
        for p, (_, kind, shard, _) in enumerate(pieces):
            for k in (1, 2, 3):
                s2, _ = _other_chip(s, c, k)
                mine = _full_region(outs[p], kind, shard, s2, c)
                theirs = _full_region(outs[p], kind, shard, s2, 1 - c)
                cp = _remote(mine, mine, fwd_send.at[p, k - 1], fwd_recv.at[p, k - 1], sibling)
                starts.append(cp)
                waits.append(cp.wait_send)
                waits.append(_remote(theirs, theirs, fwd_send.at[p, k - 1], fwd_recv.at[p, k - 1], sibling).wait_recv)
        return starts, waits

    return _Rider(
        [gathered[name] for name in names],
        [jax.ShapeDtypeStruct(gathered[name].shape, gathered[name].dtype) for name in names],
        [pltpu.SemaphoreType.DMA((n, 3)), pltpu.SemaphoreType.DMA((n, 3))],
        build, aliased=True)


def _pair_exchange_rider(grads, names):
    n = len(names)

    def build(ins, outs, sems):
        send_sem, recv_sem = sems
        x, y, c, _ = _place()
        sibling = (x, y, 1 - c)
        starts, waits = [], []
        for p, name in enumerate(names):
            kind, shard = BIG_PIECES[name]
            for s2 in range(4):
                cp = _remote(_full_region(ins[p], kind, shard, s2, 1 - c), outs[p].at[s2], send_sem.at[p, s2],
                             recv_sem.at[p, s2], sibling)
                starts.append(cp)
                waits.append(cp.wait_send)
                waits.append(_remote(outs[p].at[s2], outs[p].at[s2], send_sem.at[p, s2], recv_sem.at[p, s2], sibling).wait_recv)
        return starts, waits

    return _Rider(
        [grads[name] for name in names],
        [jax.ShapeDtypeStruct((4,) + _half_shape(*BIG_PIECES[name]), F32) for name in names],
        [pltpu.SemaphoreType.DMA((n, 4))] * 2,
        build)


def _half_specs(kind, shard):
    half = shard[0] // 2
    if kind == "col":
        full = pl.BlockSpec((half, shard[1]), lambda j, pr: (pr[1], j))
        buf = pl.BlockSpec((None, half, shard[1]), lambda j, pr: (j, 0, 0))
    elif kind == "row":
        full = pl.BlockSpec((half, shard[1]), lambda j, pr: (2 * j + pr[1], 0))
        buf = pl.BlockSpec((None, half, shard[1]), lambda j, pr: (j, 0, 0))
    else:
        full = pl.BlockSpec((half, shard[1], shard[2]), lambda j, pr: (pr[1], j, 0))
        buf = pl.BlockSpec((None, half, shard[1], shard[2]), lambda j, pr: (j, 0, 0, 0))
    return full, buf


def _pair_sum(name, grad, recv, place):
    kind, shard = BIG_PIECES[name]
    full, buf = _half_specs(kind, shard)

    def body(pr, g_ref, r_ref, o_ref):
        o_ref[...] = (g_ref[...] + r_ref[...]).astype(BF16)

    return pl.pallas_call(
        body,
        name="pair_sum_" + name,
        grid_spec=pltpu.PrefetchScalarGridSpec(num_scalar_prefetch=1, grid=(4,), in_specs=[full, buf], out_specs=buf),
        out_shape=jax.ShapeDtypeStruct((4,) + _half_shape(kind, shard), BF16),
        compiler_params=_params(("arbitrary",)),
    )(place, grad, recv)


def _chip_exchange_rider(sums, names):
    n = len(names)

    def build(ins, outs, sems):
        send_sem, recv_sem = sems
        _, _, c, s = _place()
        starts, waits = [], []
        for p in range(n):
            for k in (1, 2, 3):
                s2, dev = _other_chip(s, c, k)
                cp = _remote(ins[p].at[s2], outs[p].at[k - 1], send_sem.at[p, k - 1], recv_sem.at[p, k - 1], dev)
                starts.append(cp)
                waits.append(cp.wait_send)
                waits.append(_remote(outs[p].at[k - 1], outs[p].at[k - 1], send_sem.at[p, k - 1], recv_sem.at[p, k - 1],
                                     dev).wait_recv)
        return starts, waits

    return _Rider(
        [sums[name] for name in names],
        [jax.ShapeDtypeStruct((3,) + _half_shape(*BIG_PIECES[name]), BF16) for name in names],
        [pltpu.SemaphoreType.DMA((n, 3))] * 2,
        build)


def _chip_sum(name, grad, recv_pair, recv_chip, place):
    kind, shard = BIG_PIECES[name]
    half = shard[0] // 2
    tail = tuple(shard[1:])
    zeros = (0,) * len(tail)
    if kind == "col":
        full = pl.BlockSpec((half,) + tail, lambda j, pr: (pr[1], pr[0]))
    elif kind == "row":
        full = pl.BlockSpec((half,) + tail, lambda j, pr: (2 * pr[0] + pr[1], 0))
    else:
        full = pl.BlockSpec((half,) + tail, lambda j, pr: (pr[1], pr[0], 0))
    pair = pl.BlockSpec((None, half) + tail, lambda j, pr: (pr[0], 0) + zeros)
    chip = pl.BlockSpec((3, half) + tail, lambda j, pr: (0, 0) + zeros)
    out = pl.BlockSpec((half,) + tail, lambda j, pr: (pr[1],) + zeros)

    def body(pr, g_ref, rp_ref, rc_ref, o_ref):
        total = g_ref[...] + rp_ref[...]
        for k in range(3):
            total = total + rc_ref[k].astype(F32)
        o_ref[...] = total

    return pl.pallas_call(
        body,
        name="chip_sum_" + name,
        grid_spec=pltpu.PrefetchScalarGridSpec(num_scalar_prefetch=1, grid=(1,), in_specs=[full, pair, chip], out_specs=out),
        out_shape=jax.ShapeDtypeStruct(shard, F32),
        compiler_params=_params(("arbitrary",)),
    )(place, grad, recv_pair, recv_chip)


def _sibling_exchange_rider(halves, names):
    n = len(names)

    def build(ins, outs, sems):
        send_sem, recv_sem = sems
        x, y, c, _ = _place()
        sibling = (x, y, 1 - c)
        starts, waits = [], []
        for p, name in enumerate(names):
            shard = BIG_PIECES[name][1]
            mine = _shard_region(outs[p], shard, c)
            theirs = _shard_region(outs[p], shard, 1 - c)
            cp = _remote(mine, mine, send_sem.at[p], recv_sem.at[p], sibling)
            starts.append(cp)
            waits.append(cp.wait_send)
            waits.append(_remote(theirs, theirs, send_sem.at[p], recv_sem.at[p], sibling).wait_recv)
        return starts, waits

    return _Rider(
        [halves[name] for name in names],
        [jax.ShapeDtypeStruct(BIG_PIECES[name][1], F32) for name in names],
        [pltpu.SemaphoreType.DMA((n,))] * 2,
        build, aliased=True)


FIRST_WEIGHTS = ["meta_tokens", "conv_w"]
WEIGHT_GROUPS = {
    "lru": ["lru_wa", "lru_wx"],
    "branch": ["w_branch_ret", "w_branch_lru", "w_out"],
    "ffn_in": ["w_ffn_in"],
    "ffn_out": ["w_ffn_out"],
}
WEIGHT_SCHEDULE = {
    "in_proj": [("gather", "lru")],
    "retention_fwd": [("forward", "lru"), ("gather", "branch")],
    "lru_fwd": [("forward", "branch"), ("gather", "ffn_in")],
    "mix_fwd": [("forward", "ffn_in"), ("gather", "ffn_out")],
    "ffn_fwd_gate": [("forward", "ffn_out")],
}
GRAD_GROUPS = {
    "ffn": ["w_ffn_in", "w_ffn_out"],
    "mixer": ["w_out", "w_branch_ret", "w_branch_lru", "lru_wa", "lru_wx"],
    "in": ["w_in"],
}
GRAD_SCHEDULE = {
    "mix_bwd": [("pair", "ffn")],
    "retention_bwd": [("chip", "ffn")],
    "lru_bwd": [("sibling", "ffn")],
    "dw_in": [("pair", "mixer")],
    "in_proj_bwd_0": [("chip", "mixer"), ("pair", "in")],
    "in_proj_bwd_1": [("sibling", "mixer"), ("chip", "in")],
}


class _CommPlan:
    def __init__(self, shards, place):
        self.shards, self.place = shards, place
        self.late = {}
        self.recv_pair, self.sums, self.recv_chip, self.halves, self.final = {}, {}, {}, {}, {}

    def _grad_rider(self, stage, group, grads):
        names = GRAD_GROUPS[group]
        if stage == "pair":
            return _pair_exchange_rider(grads, names)
        if stage == "chip":
            return _chip_exchange_rider(self.sums, names)
        return _sibling_exchange_rider(self.halves, names)

    def _grad_after(self, stage, group, outs, grads):
        names = GRAD_GROUPS[group]
        if stage == "pair":
            for n, o in zip(names, outs):
                self.recv_pair[n] = o
                self.sums[n] = _pair_sum(n, grads[n], o, self.place)
        elif stage == "chip":
            for n, o in zip(names, outs):
                self.halves[n] = _chip_sum(n, grads[n], self.recv_pair[n], o, self.place)
        else:
            self.final.update(zip(names, outs))

    def riders(self, host, w, grads):
        if host in WEIGHT_SCHEDULE:
            return [_gather_rider(self.shards, WEIGHT_GROUPS[group]) if stage == "gather"
                    else _forward_rider(self.late, WEIGHT_GROUPS[group]) for stage, group in WEIGHT_SCHEDULE[host]]
        return [self._grad_rider(stage, group, grads) for stage, group in GRAD_SCHEDULE.get(host, [])]

    def after(self, host, rider_outs, w, grads):
        for (stage, group), outs in zip(WEIGHT_SCHEDULE.get(host, []), rider_outs):
            (self.late if stage == "gather" else w).update(zip(WEIGHT_GROUPS[group], outs))
        for (stage, group), outs in zip(GRAD_SCHEDULE.get(host, []), rider_outs):
            self._grad_after(stage, group, outs, grads)

    def finish(self, partial):
        outs, (blocks,) = _run_riders("tail_exchange", [_sibling_exchange_rider(self.halves, GRAD_GROUPS["in"]),
                                                        _small_exchange_rider(partial)])
        self.final.update(zip(GRAD_GROUPS["in"], outs))
        return self.final, blocks


def _adamw_math(w, g, m, v):
    m = ADAM_B1 * m + (1.0 - ADAM_B1) * g
    v = ADAM_B2 * v + (1.0 - ADAM_B2) * (g * g)
    m_hat = m / (1.0 - ADAM_B1 ** ADAM_STEP)
    v_hat = v / (1.0 - ADAM_B2 ** ADAM_STEP)
    delta = -ADAM_LR * (m_hat / (jnp.sqrt(v_hat) + ADAM_EPS) + ADAM_WD * w)
    return delta, m, v


def _adamw(name, g, w, m, v):
    rows, cols = g.shape
    tr = rows // 4 if rows % 32 == 0 else rows

    def body(g_ref, w_ref, m_ref, v_ref, go_ref, d_ref, mo_ref, vo_ref):
        gv = g_ref[...]
        delta, m2, v2 = _adamw_math(w_ref[...], gv, m_ref[...], v_ref[...])
        go_ref[...] = gv
        d_ref[...] = delta
        mo_ref[...] = m2
        vo_ref[...] = v2

    spec = pl.BlockSpec((tr, cols), lambda i: (i, 0))
    return pl.pallas_call(
        body,
        name="adamw_" + name,
        grid=(rows // tr,),
        in_specs=[spec] * 4,
        out_specs=[spec] * 4,
        out_shape=[jax.ShapeDtypeStruct((rows, cols), F32)] * 4,
        compiler_params=_params(("arbitrary",)),
    )(g, w, m, v)


SMALL_ROWS = 48
VEC_ROWS = {"final_norm_w": 1, "ffn_norm_w": 8, "mix_norm_w": 16, "conv_b": 24, "lru_ba": 25, "lru_bx": 26, "lru_lambda": 27}
VEC_NAMES = list(VEC_ROWS)
CONV_W_ROW = 28
META_ROW = 32


def _small_exchange_rider(partial):
    def build(ins, outs, sems):
        local_sem, send_sem, recv_sem = sems
        _, _, c, s = _place()
        me = 2 * s + c
        cp = pltpu.make_async_copy(ins[0], outs[0].at[me], local_sem)
        starts, waits = [cp], [cp.wait]
        for k in range(1, 8):
            peer = jnp.bitwise_xor(me, k)
            dev = (peer // 4, (peer // 2) % 2, peer % 2)
            rd = _remote(ins[0], outs[0].at[me], send_sem.at[k - 1], recv_sem.at[k - 1], dev)
            starts.append(rd)
            waits.append(rd.wait_send)
            waits.append(_remote(ins[0], outs[0].at[peer], send_sem.at[k - 1], recv_sem.at[k - 1], dev).wait_recv)
        return starts, waits

    return _Rider([partial], [jax.ShapeDtypeStruct((8, SMALL_ROWS, D), F32)],
                  [pltpu.SemaphoreType.DMA, pltpu.SemaphoreType.DMA((7,)), pltpu.SemaphoreType.DMA((7,))], build)


def _small_update(blocks, place, vecs, conv, meta):
    nvec = len(VEC_NAMES)
    qcols = D // 4

    def in_order(ref):
        total = ref[0]
        for d in range(1, 8):
            total = total + ref[d]
        return total

    def body(pr, blocks_ref, cols_ref, *refs):
        vec_refs = refs[:3 * nvec]
        conv_refs = refs[3 * nvec:3 * nvec + 3]
        meta_refs = refs[3 * nvec + 3:3 * nvec + 6]
        outs = refs[3 * nvec + 6:]
        loss_ref, vec_out, conv_out, meta_out = outs[0], outs[1:5], outs[5:9], outs[9:13]
        tot, col = in_order(blocks_ref), in_order(cols_ref)
        loss_ref[...] = jnp.sum(tot[0:1, :], axis=1, keepdims=True)
        for o in vec_out:
            o[...] = jnp.zeros_like(o)
        for j, name in enumerate(VEC_NAMES):
            w, m, v = (r[...] for r in vec_refs[3 * j:3 * j + 3])
            g = tot[VEC_ROWS[name]:VEC_ROWS[name] + 1, :]
            if name == "lru_lambda":
                g = -g / (1.0 + jnp.exp(w))
            for o, val in zip(vec_out, (g,) + _adamw_math(w, g, m, v)):
                o[j:j + 1, :] = val
        for row, n_rows, ins, group in ((CONV_W_ROW, 4, conv_refs, conv_out), (META_ROW, N_META, meta_refs, meta_out)):
            g = col[row:row + n_rows, :]
            w, m, v = (r[...] for r in ins)
            for o, val in zip(group, (g,) + _adamw_math(w, g, m, v)):
                o[...] = val

    whole = lambda shape: pl.BlockSpec(shape, lambda i, pr: (0,) * len(shape))
    in_specs = [whole((8, SMALL_ROWS, D)), pl.BlockSpec((8, SMALL_ROWS, qcols), lambda i, pr: (0, 0, pr[0]))]
    in_specs += [whole((1, D))] * (3 * nvec) + [whole((4, qcols))] * 3 + [whole((N_META, qcols))] * 3
    out_shapes = [(1, 1)] + [(8, D)] * 4 + [(4, qcols)] * 4 + [(N_META, qcols)] * 4
    return pl.pallas_call(
        body,
        name="small_update",
        grid_spec=pltpu.PrefetchScalarGridSpec(num_scalar_prefetch=1, grid=(1,), in_specs=in_specs,
                                               out_specs=[whole(s) for s in out_shapes]),
        out_shape=[jax.ShapeDtypeStruct(s, F32) for s in out_shapes],
        compiler_params=_params(("arbitrary",)),
    )(place, blocks, blocks, *[a for t in vecs for a in t], *conv, *meta)


WEIGHT_ORDER = ["meta_tokens", "mix_norm_w", "w_in", "conv_w", "conv_b", "lru_wa", "lru_ba", "lru_wx", "lru_bx", "lru_lambda",
                "w_branch_ret", "w_branch_lru", "w_out", "ffn_norm_w", "w_ffn_in", "w_ffn_out", "final_norm_w"]


def kernel(x, meta_tokens, mix_norm_w, w_in, conv_w, conv_b, lru_wa, lru_ba, lru_wx, lru_bx, lru_lambda, w_branch_ret, w_branch_lru, w_out, ffn_norm_w, w_ffn_in, w_ffn_out, final_norm_w, loss_target, m_meta_tokens, m_mix_norm_w, m_w_in, m_conv_w, m_conv_b, m_lru_wa, m_lru_ba, m_lru_wx, m_lru_bx, m_lru_lambda, m_w_branch_ret, m_w_branch_lru, m_w_out, m_ffn_norm_w, m_w_ffn_in, m_w_ffn_out, m_final_norm_w, v_meta_tokens, v_mix_norm_w, v_w_in, v_conv_w, v_conv_b, v_lru_wa, v_lru_ba, v_lru_wx, v_lru_bx, v_lru_lambda, v_w_branch_ret, v_w_branch_lru, v_w_out, v_ffn_norm_w, v_w_ffn_in, v_w_ffn_out, v_final_norm_w):
    args = locals()
    wts = {n: args[n] for n in WEIGHT_ORDER}
    mom = {n: args["m_" + n] for n in WEIGHT_ORDER}
    var = {n: args["v_" + n] for n in WEIGHT_ORDER}
    place = jnp.stack([2 * lax.axis_index("x") + lax.axis_index("y"), lax.axis_index("c")]).astype(jnp.int32)

    shards = {n: wts[n][0].astype(BF16) for n in BIG_PIECES}
    shards["meta_tokens"] = wts["meta_tokens"]
    shards["conv_w"] = wts["conv_w"][0]
    plan = _CommPlan(shards, place)
    (first,) = _run_riders("gather_first", [_gather_rider(shards, FIRST_WEIGHTS)])
    w = dict(zip(FIRST_WEIGHTS, first))
    for n in VEC_NAMES:
        w[n] = wts[n].reshape(1, D)
    chips = jnp.bitwise_xor(place[0], jnp.arange(4, dtype=jnp.int32))
    w["route"] = jnp.concatenate([place, jnp.stack([2 * chips, 2 * chips + 1], axis=1).reshape(8)])
    w["w_in_shard"] = shards["w_in"]

    grad_x, grad_head, _, stats = _local_step(x[0], loss_target[0], w, plan)
    partial = jnp.concatenate(stats + [grad_head[PAD_ROWS:]], axis=0)
    shard_grads, blocks = plan.finish(partial)

    out = {}
    for n in BIG_PIECES:
        shape2d = (-1, wts[n].shape[-1])
        res = _adamw(n, *[a.reshape(shape2d) for a in (shard_grads[n], wts[n], mom[n], var[n])])
        out[n] = [r.reshape(wts[n].shape) for r in res]

    vecs = [tuple(a[n].reshape(1, D) for a in (wts, mom, var)) for n in VEC_NAMES]
    conv = tuple(a["conv_w"][0] for a in (wts, mom, var))
    meta = tuple(a["meta_tokens"] for a in (wts, mom, var))
    res = _small_update(blocks, place, vecs, conv, meta)
    loss = res[0].reshape(())
    for j, n in enumerate(VEC_NAMES):
        out[n] = [r[j].reshape(wts[n].shape) for r in res[1:5]]
    out["conv_w"] = [r.reshape(wts["conv_w"].shape) for r in res[5:9]]
    out["meta_tokens"] = list(res[9:13])

    return (loss, grad_x.reshape(x.shape)) + tuple(out[n][kind] for kind in range(4) for n in WEIGHT_ORDER)
```

```python
import functools
import math

import jax
import jax.numpy as jnp
from jax import lax
from jax.experimental import pallas as pl
from jax.experimental.pallas import tpu as pltpu

F32 = jnp.float32
BF16 = jnp.bfloat16

D = 1024
HEADS = 8
DH = 128
CHUNK = 256
N_META = 16
FRONT = 256
PAD_ROWS = FRONT - N_META
LRU_BLOCKS = 4
LRU_BLOCK = 256
LRU_C = 8.0
FFN = 2816
FFN_HALF = FFN // 2
IN_COLS = 8 * D
ROPE_BASE = 10000.0
EPS = 1e-6
QK_SCALE = DH ** -0.5

ADAM_LR = 0.001
ADAM_B1 = 0.9
ADAM_B2 = 0.999
ADAM_EPS = 1e-08
ADAM_WD = 0.01
ADAM_STEP = 10

TM = 256
TM_HEAVY = 768
FFN_BLOCK = FFN // 2
FFN_SUB = 768
TM_MIX_BWD = 256
TM_LRU = 384
TM_IN_PROJ = 1408
VMEM_LIMIT = 60 * 1024 * 1024
TN_VMEM_BUDGET = 40 * 1024 * 1024

NT_DIMS = (((1,), (1,)), ((), ()))
TN_DIMS = (((0,), (0,)), ((), ()))
MESH = pl.DeviceIdType.MESH


def _params(sem=None):
    if sem is None:
        return pltpu.CompilerParams(vmem_limit_bytes=VMEM_LIMIT)
    return pltpu.CompilerParams(dimension_semantics=sem, vmem_limit_bytes=VMEM_LIMIT)


def _dot(a, b):
    return jnp.dot(a, b, preferred_element_type=F32)


def _dot_nt(a, b):
    return lax.dot_general(a, b, NT_DIMS, preferred_element_type=F32)


def _dot_tn(a, b):
    return lax.dot_general(a, b, TN_DIMS, preferred_element_type=F32)


def _sigmoid(z):
    return 1.0 / (1.0 + jnp.exp(-z))


def _log1p(x):
    return jnp.where(x < 1e-3, x * (1.0 - x * (0.5 - x * (1.0 / 3.0))), jnp.log(1.0 + x))


def _softplus(x):
    return jnp.maximum(x, 0.0) + _log1p(jnp.exp(-jnp.abs(x)))


def _neg_expm1(x):
    series = -x * (1.0 + x * (0.5 + x * (1.0 / 6.0 + x * (1.0 / 24.0 + x * (1.0 / 120.0)))))
    return jnp.where(x > -0.05, series, 1.0 - jnp.exp(x))


_GELU_K = math.sqrt(2.0 / math.pi)


def _gelu_and_grad(x):
    inner = _GELU_K * (x + 0.044715 * x * x * x)
    t = jnp.tanh(inner)
    val = 0.5 * x * (1.0 + t)
    grad = 0.5 * (1.0 + t) + 0.5 * x * (1.0 - t * t) * _GELU_K * (1.0 + 3.0 * 0.044715 * x * x)
    return val, grad


def _row_ids(i, rows, shape):
    return i * rows + lax.broadcasted_iota(jnp.int32, shape, 0)


def _rope_tables(rows):
    pos = jnp.arange(rows, dtype=jnp.int32) - PAD_ROWS
    inv_freq = ROPE_BASE ** (-jnp.arange(0, DH, 2, dtype=F32) / DH)
    ang = pos.astype(F32)[:, None] * inv_freq[None, :]
    cos, sin = jnp.cos(ang), jnp.sin(ang)
    return jnp.concatenate([cos, cos], axis=1), jnp.concatenate([-sin, sin], axis=1)


def _decay_tables():
    log_g = jnp.log(1.0 - 2.0 ** (-5.0 - jnp.arange(HEADS, dtype=F32)))
    idx = jnp.arange(CHUNK, dtype=F32)
    diff = idx[:, None] - idx[None, :]
    intra = jnp.where(diff[None] >= 0, jnp.exp(jnp.maximum(diff, 0.0)[None] * log_g[:, None, None]), 0.0)
    q_decay = jnp.exp((idx + 1.0)[:, None] * log_g[None, :])
    k_decay = jnp.exp((CHUNK - 1.0 - idx)[:, None] * log_g[None, :])
    chunk_decay = jnp.exp(CHUNK * log_g)
    wide = lambda a: jnp.repeat(a, DH, axis=-1)
    return intra, jnp.swapaxes(intra, 1, 2), wide(q_decay), wide(k_decay), wide(chunk_decay[None, :])


def _norm1(head, x2d, norm_w, riders=()):
    rows = FRONT + x2d.shape[0]

    def body(head_ref, x_ref, nw_ref, u_ref, rstd_ref):
        def norm(hv):
            rs = lax.rsqrt(jnp.mean(hv * hv, axis=-1, keepdims=True) + EPS)
            u_ref[...] = ((hv * rs) * nw_ref[...]).astype(BF16)
            rstd_ref[...] = rs

        @pl.when(pl.program_id(0) == 0)
        def _():
            norm(head_ref[...])

        @pl.when(pl.program_id(0) > 0)
        def _():
            norm(x_ref[...])

    return _hosted_call(
        body,
        name="norm1",
        grid=(rows // TM,),
        in_specs=[
            pl.BlockSpec((FRONT, D), lambda i: (0, 0)),
            pl.BlockSpec((TM, D), lambda i: (jnp.maximum(i - 1, 0), 0)),
            pl.BlockSpec((1, D), lambda i: (0, 0)),
        ],
        out_specs=[pl.BlockSpec((TM, D), lambda i: (i, 0)), pl.BlockSpec((TM, 1), lambda i: (i, 0))],
        out_shape=[jax.ShapeDtypeStruct((rows, D), BF16), jax.ShapeDtypeStruct((rows, 1), F32)],
        scratch_shapes=[],
        args=(head, x2d, norm_w),
        riders=riders,
    )


def _heavy_tile(rows):
    return TM_HEAVY if rows % TM_HEAVY == 0 else TM


def _lru_tile(rows):
    return TM_LRU if rows % TM_LRU == 0 else TM


def _swap_3_4(group):
    return jnp.where(group == 3, 4, jnp.where(group == 4, 3, group))


LRU_IN_COL = 3
GATES_COL = 1


def _in_proj(u, w_shard, route, cos_t, sin_t, riders=()):
    rows = u.shape[0]
    tm = TM_IN_PROJ if rows % TM_IN_PROJ == 0 else _heavy_tile(rows)
    nt = rows // tm
    kind, shard = BIG_PIECES["w_in"]

    def body(route_ref, u_hbm, wsh_ref, cos_ref, sin_ref, proj_ref, wfull_ref,
             u_sc, w_sc, u_sem, w_sem, local_sem, ici_send, ici_recv, fwd_send, fwd_recv):
        g, i = pl.program_id(0), pl.program_id(1)
        s, c = route_ref[0], route_ref[1]
        gid = route_ref[2 + g]
        sibling = (s // 2, s % 2, 1 - c)
        local = pltpu.make_async_copy(wsh_ref, _full_region(wfull_ref, kind, shard, s, None), local_sem)
        u_copies = [pltpu.make_async_copy(u_hbm.at[pl.ds(t * tm, tm)], u_sc.at[t], u_sem.at[t]) for t in range(nt)]
        sends, arrivals = [], []
        for k in (1, 2, 3):
            s2, dev = _other_chip(s, c, k)
            sends.append(_remote(_shard_region(wsh_ref, shard, c), _full_region(wfull_ref, kind, shard, s, c),
                                 ici_send.at[k - 1], ici_recv.at[k - 1], dev))
            mine = _full_region(wfull_ref, kind, shard, s2, c)
            theirs = _full_region(wfull_ref, kind, shard, s2, 1 - c)
            arrivals.append((_remote(mine, mine, ici_send.at[k - 1], ici_recv.at[k - 1], dev),
                             _remote(mine, mine, fwd_send.at[k - 1], fwd_recv.at[k - 1], sibling),
                             _remote(theirs, theirs, fwd_send.at[k - 1], fwd_recv.at[k - 1], sibling)))

        slot = g % 2
        last_tile = i == nt - 1

        def block_copy(src, col, to_slot):
            return pltpu.make_async_copy(src.at[:, pl.ds(pl.multiple_of(col * D, D), D)], w_sc.at[to_slot],
                                         w_sem.at[to_slot])

        @pl.when(jnp.logical_and(g == 0, i == 0))
        def _():
            for cp in sends:
                cp.start()
            local.start()
            for cp in u_copies:
                cp.start()
            cp = block_copy(wsh_ref, 0, 0)
            cp.start()
            cp.wait()

        @pl.when(jnp.logical_and(last_tile, g == 0))
        def _():
            block_copy(wsh_ref, 1, 1).start()

        for k, (arrived, forward, forwarded) in zip((1, 2, 3), arrivals):
            @pl.when(jnp.logical_and(last_tile, g == 2 * k - 1))
            def _():
                arrived.wait_recv()
                forward.start()
                forwarded.wait_recv()
                block_copy(wfull_ref, route_ref[2 + 2 * k], 0).start()

            @pl.when(jnp.logical_and(last_tile, g == 2 * k))
            def _():
                block_copy(wfull_ref, route_ref[3 + 2 * k], 1).start()

        @pl.when(jnp.logical_and(i == 0, g > 0))
        def _():
            block_copy(wfull_ref, 0, slot).wait()

        for t in range(nt):
            @pl.when(jnp.logical_and(g == 0, i == t))
            def _():
                u_copies[t].wait()

        acc = _dot(u_sc[i], w_sc[slot])

        @pl.when(gid < 2)
        def _():
            scale = jnp.where(gid == 1, QK_SCALE, 1.0).astype(F32)
            for h in range(HEADS):
                sl = slice(h * DH, (h + 1) * DH)
                blk = acc[:, sl]
                out = (blk * cos_ref[...] + pltpu.roll(blk, DH // 2, axis=1) * sin_ref[...]) * scale
                proj_ref[:, sl] = out.astype(BF16)

        @pl.when(gid >= 2)
        def _():
            proj_ref[...] = acc.astype(BF16)

        @pl.when(jnp.logical_and(g == 7, i == nt - 1))
        def _():
            local.wait()
            for cp in sends:
                cp.wait_send()
            for _, forward, _ in arrivals:
                forward.wait_send()

    return _hosted_call(
        body,
        name="in_proj",
        grid=(8, nt),
        in_specs=[
            _ANY, _ANY,
            pl.BlockSpec((tm, DH), lambda g, i, rt: (i, 0)),
            pl.BlockSpec((tm, DH), lambda g, i, rt: (i, 0)),
        ],
        out_specs=[pl.BlockSpec((tm, D), lambda g, i, rt: (i, _swap_3_4(rt[2 + g]))), _ANY],
        out_shape=[jax.ShapeDtypeStruct((rows, IN_COLS), BF16), jax.ShapeDtypeStruct((D, IN_COLS), BF16)],
        scratch_shapes=[
            pltpu.VMEM((nt, tm, D), BF16), pltpu.VMEM((2, D, D), BF16),
            pltpu.SemaphoreType.DMA((nt,)), pltpu.SemaphoreType.DMA((2,)), pltpu.SemaphoreType.DMA,
            pltpu.SemaphoreType.DMA((3,)), pltpu.SemaphoreType.DMA((3,)),
            pltpu.SemaphoreType.DMA((3,)), pltpu.SemaphoreType.DMA((3,)),
        ],
        args=(u, w_shard, cos_t, sin_t),
        riders=riders,
        prefetch=route,
        riders_after_body=True,
    )


def _retention_fwd(proj_bf, intra, q_dec, k_dec, c_dec, riders=()):
    rows = proj_bf.shape[0]
    nc = rows // CHUNK

    def body(q_ref, k_ref, v_ref, m_ref, qd_ref, kd_ref, cd_ref, o_ref, sprev_ref, s_sc):
        @pl.when(pl.program_id(0) == 0)
        def _():
            s_sc[...] = jnp.zeros_like(s_sc)

        for h in range(HEADS):
            sl = slice(h * DH, (h + 1) * DH)
            q, k, v = q_ref[:, sl], k_ref[:, sl], v_ref[:, sl]
            state = s_sc[h]
            state_b = state.astype(BF16)
            sprev_ref[0, h] = state_b
            s = _dot_nt(q, k) * m_ref[h]
            inner = _dot(s.astype(BF16), v)
            cross = _dot(q, state_b) * qd_ref[:, sl]
            o_ref[:, sl] = (inner + cross).astype(BF16)
            k_scaled = (k.astype(F32) * kd_ref[:, sl]).astype(BF16)
            s_sc[h] = state * cd_ref[:, sl] + _dot_tn(k_scaled, v)

    chunk_spec = lambda col: pl.BlockSpec((CHUNK, D), lambda c: (c, col))
    const2 = lambda shape: pl.BlockSpec(shape, lambda c: (0, 0))
    return _hosted_call(
        body,
        name="retention_fwd",
        grid=(nc,),
        in_specs=[
            chunk_spec(0), chunk_spec(1), chunk_spec(2),
            pl.BlockSpec((HEADS, CHUNK, CHUNK), lambda c: (0, 0, 0)),
            const2((CHUNK, D)), const2((CHUNK, D)), const2((1, D)),
        ],
        out_specs=[
            pl.BlockSpec((CHUNK, D), lambda c: (c, 0)),
            pl.BlockSpec((1, HEADS, DH, DH), lambda c: (c, 0, 0, 0)),
        ],
        out_shape=[
            jax.ShapeDtypeStruct((rows, D), BF16),
            jax.ShapeDtypeStruct((nc, HEADS, DH, DH), BF16),
        ],
        scratch_shapes=[pltpu.VMEM((HEADS, DH, DH), F32)],
        args=(proj_bf, proj_bf, proj_bf, intra, q_dec, k_dec, c_dec),
        riders=riders,
    )


def _shift_down(x, first8_prev, d):
    rolled = pltpu.roll(x, d, axis=0)
    head = pltpu.roll(jnp.concatenate([first8_prev, x[0:8]], axis=0), d, axis=0)[8:16]
    return rolled, head


def _conv_and_gates(x, prev8, cw_ref, cb_ref, wa_ref, wx_ref, ba_ref, bx_ref, lam_ref, c_sc):
    cw = cw_ref[...]
    conv = cb_ref[...] + cw[3:4] * x
    head = cb_ref[...] + cw[3:4] * x[0:8]
    for d in (1, 2, 3):
        rolled, hd = _shift_down(x, prev8, d)
        conv = conv + cw[3 - d:4 - d] * rolled
        head = head + cw[3 - d:4 - d] * hd
    c_sc[...] = conv
    c_sc[0:8, :] = head
    c = c_sc[...]
    zr, zi = [], []
    for g in range(LRU_BLOCKS):
        sl = slice(g * LRU_BLOCK, (g + 1) * LRU_BLOCK)
        cg = c[:, sl].astype(BF16)
        zr.append(_dot(cg, wa_ref[g]))
        zi.append(_dot(cg, wx_ref[g]))
    r = _sigmoid(jnp.concatenate(zr, axis=1) + ba_ref[...])
    gate_i = _sigmoid(jnp.concatenate(zi, axis=1) + bx_ref[...])
    sp = _softplus(-lam_ref[...])
    log_a = (-LRU_C) * r * sp
    a = jnp.exp(log_a)
    mult = jnp.sqrt(_neg_expm1(2.0 * log_a))
    return c, r, gate_i, a, mult, log_a


def _lru_fwd(proj, conv_w, conv_b, wa, wx, ba, bx, lam, riders=()):
    rows = proj.shape[0]
    TM = _lru_tile(rows)
    nt = rows // TM

    def body(x_ref, cw_ref, cb_ref, wa_ref, wx_ref, ba_ref, bx_ref, lam_ref,
             h_ref, c_ref, r_ref, i_ref, la_ref, mult_ref, prev_sc, carry_sc, c_sc, a_sc, u_sc, h_sc):
        i = pl.program_id(0)

        @pl.when(i == 0)
        def _():
            prev_sc[...] = jnp.zeros_like(prev_sc)
            carry_sc[...] = jnp.zeros_like(carry_sc)

        x = x_ref[...].astype(F32)
        c, r, gate_i, a, mult, log_a = _conv_and_gates(x, prev_sc[...], cw_ref, cb_ref, wa_ref, wx_ref, ba_ref, bx_ref,
                                                       lam_ref, c_sc)
        for ref, val in ((c_ref, c), (r_ref, r), (i_ref, gate_i), (la_ref, log_a), (mult_ref, mult)):
            ref[...] = val.astype(BF16)
        prev_sc[...] = x[TM - 8:TM]
        valid = _row_ids(i, TM, (TM, D)) >= PAD_ROWS
        a_sc[...] = a
        u_sc[...] = jnp.where(valid, mult * gate_i * c, 0.0)
        row8 = lax.broadcasted_iota(jnp.int32, (8, D), 0)

        def group(gi, hprev):
            r0 = pl.multiple_of(gi * 8, 8)
            aa = a_sc[pl.ds(r0, 8), :]
            uu = u_sc[pl.ds(r0, 8), :]
            for d in (1, 2, 4):
                a_sh = jnp.where(row8 >= d, pltpu.roll(aa, d, axis=0), 1.0)
                u_sh = jnp.where(row8 >= d, pltpu.roll(uu, d, axis=0), 0.0)
                uu = uu + aa * u_sh
                aa = aa * a_sh
            hb = aa * hprev + uu
            h_sc[pl.ds(r0, 8), :] = hb
            return hb[7:8, :]

        hlast = lax.fori_loop(0, TM // 8, group, carry_sc[0:1, :])
        carry_sc[0:1, :] = hlast
        h_ref[...] = h_sc[...].astype(BF16)

    vec = pl.BlockSpec((1, D), lambda i: (0, 0))
    wspec = pl.BlockSpec((LRU_BLOCKS, LRU_BLOCK, LRU_BLOCK), lambda i: (0, 0, 0))
    return _hosted_call(
        body,
        name="lru_fwd",
        grid=(nt,),
        in_specs=[
            pl.BlockSpec((TM, D), lambda i: (i, LRU_IN_COL)),
            pl.BlockSpec((4, D), lambda i: (0, 0)),
            vec, wspec, wspec, vec, vec, vec,
        ],
        out_specs=[pl.BlockSpec((TM, D), lambda i: (i, 0)) for _ in range(6)],
        out_shape=[jax.ShapeDtypeStruct((rows, D), BF16) for _ in range(6)],
        scratch_shapes=[
            pltpu.VMEM((8, D), F32), pltpu.VMEM((8, D), F32),
            pltpu.VMEM((TM, D), F32), pltpu.VMEM((TM, D), F32), pltpu.VMEM((TM, D), F32), pltpu.VMEM((TM, D), F32),
        ],
        args=(proj, conv_w, conv_b, wa, wx, ba, bx, lam),
        riders=riders,
    )


def _group_norm(o):
    outs, rstds = [], []
    for h in range(HEADS):
        oh = o[:, h * DH:(h + 1) * DH]
        rs = lax.rsqrt(jnp.mean(oh * oh, axis=-1, keepdims=True) + EPS)
        outs.append(oh * rs)
        rstds.append(rs)
    return jnp.concatenate(outs, axis=1), rstds


def _mix_fwd(head, x2d, o, proj, h_lru, w_br, w_bl, w_o, ffn_norm_w, riders=()):
    rows = o.shape[0]
    nt = rows // TM

    def body(head_ref, x_ref, o_ref, gates_ref, hl_ref, wbr_ref, wbl_ref, wo_ref, nw_ref,
             yret_ref, ylru_ref, h1_ref, u2_ref, rstd_ref):
        i = pl.program_id(0)
        gate = lambda j: gates_ref[:, j * D:(j + 1) * D].astype(F32)
        on, _ = _group_norm(o_ref[...].astype(F32))
        gret = gate(0)
        a_ret = (gret * _sigmoid(gret) * on).astype(BF16)
        y_ret = _dot(a_ret, wbr_ref[...])
        gl, _ = _gelu_and_grad(gate(1))
        a_lru = (gl * hl_ref[...].astype(F32)).astype(BF16)
        y_lru = _dot(a_lru, wbl_ref[...])
        mixed = (_sigmoid(gate(2)) * y_ret + _sigmoid(gate(3)) * y_lru).astype(BF16)
        delta = _dot(mixed, wo_ref[...])
        yret_ref[...] = y_ret.astype(BF16)
        ylru_ref[...] = y_lru.astype(BF16)

        def finish(h0):
            h1 = h0 + delta
            rs = lax.rsqrt(jnp.mean(h1 * h1, axis=-1, keepdims=True) + EPS)
            h1_ref[...] = h1
            u2_ref[...] = ((h1 * rs) * nw_ref[...]).astype(BF16)
            rstd_ref[...] = rs

        @pl.when(i == 0)
        def _():
            finish(head_ref[...])

        @pl.when(i > 0)
        def _():
            finish(x_ref[...])

    tile = lambda col: pl.BlockSpec((TM, D), lambda i: (i, col))
    wspec = pl.BlockSpec((D, D), lambda i: (0, 0))
    return _hosted_call(
        body,
        name="mix_fwd",
        grid=(nt,),
        in_specs=[
            pl.BlockSpec((FRONT, D), lambda i: (0, 0)),
            pl.BlockSpec((TM, D), lambda i: (jnp.maximum(i - 1, 0), 0)),
            tile(0), pl.BlockSpec((TM, 4 * D), lambda i: (i, GATES_COL)), tile(0),
            wspec, wspec, wspec,
            pl.BlockSpec((1, D), lambda i: (0, 0)),
        ],
        out_specs=[tile(0), tile(0), tile(0), tile(0), pl.BlockSpec((TM, 1), lambda i: (i, 0))],
        out_shape=[
            jax.ShapeDtypeStruct((rows, D), BF16), jax.ShapeDtypeStruct((rows, D), BF16),
            jax.ShapeDtypeStruct((rows, D), F32), jax.ShapeDtypeStruct((rows, D), BF16),
            jax.ShapeDtypeStruct((rows, 1), F32),
        ],
        scratch_shapes=[],
        args=(head, x2d, o, proj, h_lru, w_br, w_bl, w_o, ffn_norm_w),
        riders=riders,
    )


def _ffn_fwd_gate(u2, w_ffn_in, riders=()):
    rows = u2.shape[0]
    tm = _heavy_tile(rows)
    nb = FFN // FFN_BLOCK
    hid = lambda: pl.BlockSpec((tm, FFN_BLOCK), lambda i, j: (i, j))

    def body(u2_ref, w_hbm, silu_ref, dsilu_ref, act_ref, w_sc, w_sem):
        j = pl.program_id(1)

        @pl.when(jnp.logical_and(pl.program_id(0) == 0, j == 0))
        def _():
            cp = pltpu.make_async_copy(w_hbm, w_sc, w_sem)
            cp.start()
            cp.wait()

        u2 = u2_ref[...]
        for a, b in ((0, FFN_SUB), (FFN_SUB, FFN_BLOCK)):
            col = pl.multiple_of(j * FFN_BLOCK + a, 128)
            g = _dot(u2, w_sc[:, pl.ds(col, b - a)])
            up = _dot(u2, w_sc[:, pl.ds(pl.multiple_of(FFN + col, 128), b - a)])
            sg = _sigmoid(g)
            silu = g * sg
            silu_ref[:, a:b] = silu.astype(BF16)
            dsilu_ref[:, a:b] = (up * (sg * (1.0 + g * (1.0 - sg)))).astype(BF16)
            act_ref[:, a:b] = (silu * up).astype(BF16)

    return _hosted_call(
        body,
        name="ffn_fwd_gate",
        grid=(rows // tm, nb),
        in_specs=[pl.BlockSpec((tm, D), lambda i, j: (i, 0)), _ANY],
        out_specs=[hid(), hid(), hid()],
        out_shape=[jax.ShapeDtypeStruct((rows, FFN), BF16)] * 3,
        scratch_shapes=[pltpu.VMEM((D, 2 * FFN), BF16), pltpu.SemaphoreType.DMA],
        args=(u2, w_ffn_in),
        riders=riders,
    )


def _ffn_out_loss(act, h1, w_ffn_out, target, final_norm_w):
    rows = act.shape[0]
    tm = _heavy_tile(rows)
    nt = rows // tm

    def body(act_ref, h1_ref, wo_ref, fnw_ref, tgt_hbm, dh2_ref, stats_ref, tgt_sc, tgt_sem):
        i = pl.program_id(0)
        slot = i % 2

        @pl.when(i == 0)
        def _():
            stats_ref[...] = jnp.zeros_like(stats_ref)
            tgt_sc[0, 0:FRONT, :] = jnp.zeros((FRONT, D), F32)
            cp = pltpu.make_async_copy(tgt_hbm.at[pl.ds(0, tm - FRONT)], tgt_sc.at[0, pl.ds(FRONT, tm - FRONT)], tgt_sem.at[0])
            cp.start()
            cp.wait()

        @pl.when(i + 1 < nt)
        def _():
            start = pl.multiple_of((i + 1) * tm - FRONT, FRONT)
            pltpu.make_async_copy(tgt_hbm.at[pl.ds(start, tm)], tgt_sc.at[1 - slot], tgt_sem.at[1 - slot]).start()

        @pl.when(i > 0)
        def _():
            pltpu.make_async_copy(tgt_hbm.at[pl.ds(0, tm)], tgt_sc.at[slot], tgt_sem.at[slot]).wait()

        h2 = h1_ref[...] + _dot(act_ref[...], wo_ref[...])
        rs = lax.rsqrt(jnp.mean(h2 * h2, axis=-1, keepdims=True) + EPS)
        n = h2 * rs
        fnw = fnw_ref[...]
        valid = _row_ids(i, tm, (tm, D)) >= FRONT
        diff = jnp.where(valid, n * fnw - tgt_sc[slot], 0.0)
        dy = diff * (1.0 / D)
        stats_ref[0:1, :] += (0.5 / D) * jnp.sum(diff * diff, axis=0, keepdims=True)
        stats_ref[1:2, :] += jnp.sum(dy * n, axis=0, keepdims=True)
        dn = dy * fnw
        dh2_ref[...] = rs * (dn - n * jnp.mean(dn * n, axis=-1, keepdims=True))

    return pl.pallas_call(
        body,
        name="ffn_out_loss",
        grid=(nt,),
        in_specs=[
            pl.BlockSpec((tm, FFN), lambda i: (i, 0)),
            pl.BlockSpec((tm, D), lambda i: (i, 0)),
            pl.BlockSpec((FFN, D), lambda i: (0, 0)),
            pl.BlockSpec((1, D), lambda i: (0, 0)),
            _ANY,
        ],
        out_specs=[pl.BlockSpec((tm, D), lambda i: (i, 0)), pl.BlockSpec((8, D), lambda i: (0, 0))],
        out_shape=[jax.ShapeDtypeStruct((rows, D), F32), jax.ShapeDtypeStruct((8, D), F32)],
        scratch_shapes=[pltpu.VMEM((2, tm, D), F32), pltpu.SemaphoreType.DMA((2,))],
        compiler_params=_params(("arbitrary",)),
    )(act, h1, w_ffn_out, final_norm_w, target)


def _ffn_bwd(dh2, silu, dsilu, h1, rstd2, w_ffn_in, w_ffn_out, ffn_norm_w):
    rows = dh2.shape[0]
    tm = _heavy_tile(rows)
    nb = FFN // FFN_BLOCK
    blk = lambda: pl.BlockSpec((tm, FFN_BLOCK), lambda i, j: (i, j))

    def gate_body(dh2_ref, silu_ref, dsilu_ref, wo_hbm, dg_ref, dup_ref, dh2b_ref, wo_sc, wo_sem):
        j = pl.program_id(1)

        @pl.when(jnp.logical_and(pl.program_id(0) == 0, j == 0))
        def _():
            cp = pltpu.make_async_copy(wo_hbm, wo_sc, wo_sem)
            cp.start()
            cp.wait()

        @pl.when(j == 0)
        def _():
            dh2b_ref[...] = dh2_ref[...].astype(BF16)

        dact = _dot_nt(dh2b_ref[...], wo_sc[pl.ds(pl.multiple_of(j * FFN_BLOCK, 128), FFN_BLOCK), :])
        dup_ref[...] = (dact * silu_ref[...].astype(F32)).astype(BF16)
        dg_ref[...] = (dact * dsilu_ref[...].astype(F32)).astype(BF16)

    dg, dup, dh2b = pl.pallas_call(
        gate_body,
        name="ffn_bwd_gate",
        grid=(rows // tm, nb),
        in_specs=[pl.BlockSpec((tm, D), lambda i, j: (i, 0)), blk(), blk(), _ANY],
        out_specs=[blk(), blk(), pl.BlockSpec((tm, D), lambda i, j: (i, 0))],
        out_shape=[jax.ShapeDtypeStruct((rows, FFN), BF16)] * 2 + [jax.ShapeDtypeStruct((rows, D), BF16)],
        scratch_shapes=[pltpu.VMEM((FFN, D), BF16), pltpu.SemaphoreType.DMA],
        compiler_params=_params(("arbitrary", "arbitrary")),
    )(dh2, silu, dsilu, w_ffn_out)

    def body(dg_ref, dup_ref, dh2_ref, h1_ref, rstd_ref, w_hbm, nw_ref, dh1_ref, stats_ref, w_sc, w_sem):
        @pl.when(pl.program_id(0) == 0)
        def _():
            stats_ref[...] = jnp.zeros_like(stats_ref)
            cp = pltpu.make_async_copy(w_hbm, w_sc, w_sem)
            cp.start()
            cp.wait()

        du = _dot_nt(dg_ref[...], w_sc[:, 0:FFN]) + _dot_nt(dup_ref[...], w_sc[:, FFN:2 * FFN])
        rs = rstd_ref[...]
        n = h1_ref[...] * rs
        stats_ref[0:1, :] += jnp.sum(du * n, axis=0, keepdims=True)
        dn = du * nw_ref[...]
        dh1_ref[...] = dh2_ref[...] + rs * (dn - n * jnp.mean(dn * n, axis=-1, keepdims=True))

    row = lambda width: pl.BlockSpec((tm, width), lambda i: (i, 0))
    dh1, stats = pl.pallas_call(
        body,
        name="ffn_bwd_in",
        grid=(rows // tm,),
        in_specs=[row(FFN), row(FFN), row(D), row(D), row(1), _ANY, pl.BlockSpec((1, D), lambda i: (0, 0))],
        out_specs=[row(D), pl.BlockSpec((8, D), lambda i: (0, 0))],
        out_shape=[jax.ShapeDtypeStruct((rows, D), F32), jax.ShapeDtypeStruct((8, D), F32)],
        scratch_shapes=[pltpu.VMEM((D, 2 * FFN), BF16), pltpu.SemaphoreType.DMA],
        compiler_params=_params(("arbitrary",)),
    )(dg, dup, dh2, h1, rstd2, w_ffn_in, ffn_norm_w)
    return dg, dup, dh2b, dh1, stats


def _mix_bwd(dh1, o, proj, h_lru, y_ret, y_lru, w_br, w_bl, w_o, riders=()):
    rows = dh1.shape[0]
    tm = TM_MIX_BWD
    nt = rows // tm

    def body(dh1_ref, o_ref, gates_ref, hl_ref, yret_ref, ylru_ref, wbr_ref, wbl_ref, wo_ref,
             dproj_ref, do_ref, dhl_ref, mixed_ref, aret_ref, alru_ref, dyret_ref, dylru_ref):
        gate = lambda j: gates_ref[:, j * D:(j + 1) * D].astype(F32)
        dmixed = _dot_nt(dh1_ref[...].astype(BF16), wo_ref[...])
        y_ret, y_lru = yret_ref[...].astype(F32), ylru_ref[...].astype(F32)
        sa, sb = _sigmoid(gate(2)), _sigmoid(gate(3))
        mixed_ref[...] = (sa * y_ret + sb * y_lru).astype(BF16)
        dga = dmixed * y_ret * sa * (1.0 - sa)
        dgb = dmixed * y_lru * sb * (1.0 - sb)
        dy_ret = (dmixed * sa).astype(BF16)
        dy_lru = (dmixed * sb).astype(BF16)
        dyret_ref[...] = dy_ret
        dylru_ref[...] = dy_lru
        da_ret = _dot_nt(dy_ret, wbr_ref[...])
        da_lru = _dot_nt(dy_lru, wbl_ref[...])

        gret = gate(0)
        sg = _sigmoid(gret)
        silu = gret * sg
        on, rstds = _group_norm(o_ref[...].astype(F32))
        aret_ref[...] = (silu * on).astype(BF16)
        dgret = da_ret * on * (sg * (1.0 + gret * (1.0 - sg)))
        don = da_ret * silu
        for h in range(HEADS):
            sl = slice(h * DH, (h + 1) * DH)
            onh, donh = on[:, sl], don[:, sl]
            do_ref[:, sl] = (rstds[h] * (donh - onh * jnp.mean(donh * onh, axis=-1, keepdims=True))).astype(BF16)

        gl, gl_grad = _gelu_and_grad(gate(1))
        hl = hl_ref[...].astype(F32)
        alru_ref[...] = (gl * hl).astype(BF16)
        dlgate = da_lru * hl * gl_grad
        dhl_ref[...] = (da_lru * gl).astype(BF16)

        for j, val in enumerate((dgret, dlgate, dga, dgb)):
            dproj_ref[:, j * D:(j + 1) * D] = val.astype(BF16)

    tile = lambda col: pl.BlockSpec((tm, D), lambda i: (i, col))
    gates = pl.BlockSpec((tm, 4 * D), lambda i: (i, GATES_COL))
    wspec = pl.BlockSpec((D, D), lambda i: (0, 0))
    bf = lambda: jax.ShapeDtypeStruct((rows, D), BF16)
    return _hosted_call(
        body,
        name="mix_bwd",
        grid=(nt,),
        in_specs=[tile(0), tile(0), gates, tile(0), tile(0), tile(0), wspec, wspec, wspec],
        out_specs=[pl.BlockSpec((tm, 4 * D), lambda i: (i, GATES_COL))] + [tile(0)] * 7,
        out_shape=[jax.ShapeDtypeStruct((rows, IN_COLS), BF16)] + [bf() for _ in range(7)],
        scratch_shapes=[],
        args=(dh1, o, proj, h_lru, y_ret, y_lru, w_br, w_bl, w_o),
        riders=riders,
    )


def _retention_bwd(dproj, proj_bf, do, sprev, intra, intra_t, q_dec, k_dec, c_dec, cos_t, sin_t, riders=()):
    rows = proj_bf.shape[0]
    nc = rows // CHUNK

    def body(dproj_in_ref, q_ref, k_ref, v_ref, do_ref, sprev_ref, m_ref, mt_ref, qd_ref, kd_ref, cd_ref, cos_ref, sin_ref,
             dproj_ref, ds_sc):
        @pl.when(pl.program_id(0) == 0)
        def _():
            ds_sc[...] = jnp.zeros_like(ds_sc)

        cos, sin = cos_ref[...], sin_ref[...]

        def unrotate(dy):
            return dy * cos - pltpu.roll(dy, DH // 2, axis=1) * sin

        for h in range(HEADS):
            sl = slice(h * DH, (h + 1) * DH)
            q, k, v = q_ref[:, sl], k_ref[:, sl], v_ref[:, sl]
            do = do_ref[:, sl]
            dob = do.astype(BF16)
            doq = (do * qd_ref[:, sl]).astype(BF16)
            state_prev = sprev_ref[0, h]
            dstate = ds_sc[h]
            dstate_b = dstate.astype(BF16)
            s_t = (_dot_nt(k, q) * mt_ref[h]).astype(BF16)
            ds_t = (_dot_nt(v, dob) * mt_ref[h]).astype(BF16)
            ds = (_dot_nt(dob, v) * m_ref[h]).astype(BF16)
            kd = kd_ref[:, sl]
            dq = _dot(ds, k) + _dot_nt(doq, state_prev)
            dk = _dot(ds_t, q) + _dot_nt(v, dstate_b) * kd
            k_scaled = (k.astype(F32) * kd).astype(BF16)
            dv = _dot(s_t, dob) + _dot(k_scaled, dstate_b)
            ds_sc[h] = dstate * cd_ref[:, sl] + _dot_tn(q, doq)
            dproj_ref[:, sl] = unrotate(dq).astype(BF16)
            dproj_ref[:, D + h * DH:D + (h + 1) * DH] = (unrotate(dk) * QK_SCALE).astype(BF16)
            dproj_ref[:, 2 * D + h * DH:2 * D + (h + 1) * DH] = dv.astype(BF16)

    rev = lambda c: nc - 1 - c
    chunk_spec = lambda col: pl.BlockSpec((CHUNK, D), lambda c: (rev(c), col))
    const2 = lambda shape: pl.BlockSpec(shape, lambda c: (0, 0))
    const3 = pl.BlockSpec((HEADS, CHUNK, CHUNK), lambda c: (0, 0, 0))
    return _hosted_call(
        body,
        name="retention_bwd",
        grid=(nc,),
        in_specs=[
            pl.BlockSpec(memory_space=pl.ANY),
            chunk_spec(0), chunk_spec(1), chunk_spec(2), chunk_spec(0),
            pl.BlockSpec((1, HEADS, DH, DH), lambda c: (rev(c), 0, 0, 0)),
            const3, const3,
            const2((CHUNK, D)), const2((CHUNK, D)), const2((1, D)),
            pl.BlockSpec((CHUNK, DH), lambda c: (rev(c), 0)),
            pl.BlockSpec((CHUNK, DH), lambda c: (rev(c), 0)),
        ],
        out_specs=[pl.BlockSpec((CHUNK, 3 * D), lambda c: (rev(c), 0))],
        out_shape=[jax.ShapeDtypeStruct(dproj.shape, BF16)],
        aliases={0: 0},
        scratch_shapes=[pltpu.VMEM((HEADS, DH, DH), F32)],
        args=(dproj, proj_bf, proj_bf, proj_bf, do, sprev, intra, intra_t, q_dec, k_dec, c_dec, cos_t, sin_t),
        riders=riders,
    )


def _lru_bwd(dproj, proj, saved, dhl, conv_w, wa, wx, lam, riders=()):
    rows = proj.shape[0]
    TM = _lru_tile(rows)
    nt = rows // TM
    per8 = TM // 8

    def body(dproj_in_ref, x_ref, xprev_ref, h_ref, hprev_ref, c_ref, r_ref, i_ref, la_ref, mult_ref, dhl_ref,
             cw_ref, wa_ref, wx_ref, lam_ref,
             dproj_ref, dwa_ref, dwx_ref, stats_ref, anext_sc, dhnext_sc, dcnext_sc, c_sc, b_sc, dh_sc):
        step = pl.program_id(0)
        i = nt - 1 - step

        @pl.when(step == 0)
        def _():
            anext_sc[...] = jnp.zeros_like(anext_sc)
            dhnext_sc[...] = jnp.zeros_like(dhnext_sc)
            dcnext_sc[...] = jnp.zeros_like(dcnext_sc)
            dwa_ref[...] = jnp.zeros_like(dwa_ref)
            dwx_ref[...] = jnp.zeros_like(dwx_ref)
            stats_ref[...] = jnp.zeros_like(stats_ref)

        first = i == 0
        x = x_ref[...].astype(F32)
        prev8 = jnp.where(first, 0.0, xprev_ref[8:16, :].astype(F32))
        c_b = c_ref[...]
        c, r, gate_i, mult = (ref[...].astype(F32) for ref in (c_ref, r_ref, i_ref, mult_ref))
        a = jnp.exp(la_ref[...].astype(F32))
        sp = _softplus(-lam_ref[...])
        dh_sc[...] = dhl_ref[...].astype(F32)

        b_sc[...] = pltpu.roll(a, TM - 1, axis=0)
        b_sc[TM - 1:TM, :] = anext_sc[0:1, :]
        anext_sc[0:1, :] = a[0:1, :]
        row8 = lax.broadcasted_iota(jnp.int32, (8, D), 0)

        def group(gi, dhnext):
            r0 = pl.multiple_of((per8 - 1 - gi) * 8, 8)
            bb = b_sc[pl.ds(r0, 8), :]
            uu = dh_sc[pl.ds(r0, 8), :]
            for d in (1, 2, 4):
                b_sh = jnp.where(row8 < 8 - d, pltpu.roll(bb, 8 - d, axis=0), 1.0)
                u_sh = jnp.where(row8 < 8 - d, pltpu.roll(uu, 8 - d, axis=0), 0.0)
                uu = uu + bb * u_sh
                bb = bb * b_sh
            dhb = bb * dhnext + uu
            dh_sc[pl.ds(r0, 8), :] = dhb
            return dhb[0:1, :]

        dhfirst = lax.fori_loop(0, per8, group, dhnext_sc[0:1, :])
        dhnext_sc[0:1, :] = dhfirst
        dh = dh_sc[...]

        h = h_ref[...].astype(F32)
        hprev8 = jnp.where(first, 0.0, hprev_ref[8:16, :].astype(F32))
        h_dn, h_head = _shift_down(h, hprev8, 1)
        c_sc[...] = h_dn
        c_sc[0:8, :] = h_head
        h_before = c_sc[...]

        valid = _row_ids(i, TM, (TM, D)) >= PAD_ROWS
        da = dh * h_before
        du = jnp.where(valid, dh, 0.0)
        dmult = du * gate_i * c
        dgate_i = du * mult * c
        dc = du * mult * gate_i
        dla = da * a - dmult * (a * a) / mult
        dla = jnp.where(valid, dla, 0.0)
        dr = dla * ((-LRU_C) * sp)
        dzr = dr * r * (1.0 - r)
        dzi = dgate_i * gate_i * (1.0 - gate_i)
        stats_ref[1:2, :] += jnp.sum(dzr, axis=0, keepdims=True)
        stats_ref[2:3, :] += jnp.sum(dzi, axis=0, keepdims=True)
        stats_ref[3:4, :] += jnp.sum(dla * ((-LRU_C) * r), axis=0, keepdims=True)
        dc_gate = []
        for g in range(LRU_BLOCKS):
            sl = slice(g * LRU_BLOCK, (g + 1) * LRU_BLOCK)
            cg = c_b[:, sl]
            dzr_g = dzr[:, sl].astype(BF16)
            dzi_g = dzi[:, sl].astype(BF16)
            dc_gate.append(_dot_nt(dzr_g, wa_ref[g]) + _dot_nt(dzi_g, wx_ref[g]))
            dwa_ref[g] += _dot_tn(cg, dzr_g)
            dwx_ref[g] += _dot_tn(cg, dzi_g)
        dc = dc + jnp.concatenate(dc_gate, axis=1)

        cw = cw_ref[...]
        stats_ref[0:1, :] += jnp.sum(dc, axis=0, keepdims=True)
        stats_ref[7:8, :] += jnp.sum(dc * x, axis=0, keepdims=True)
        dx = cw[3:4] * dc
        tail_src = jnp.concatenate([dc[TM - 8:TM], dcnext_sc[...]], axis=0)
        dx_tail = cw[3:4] * dc[TM - 8:TM]
        for d in (1, 2, 3):
            dx = dx + cw[3 - d:4 - d] * pltpu.roll(dc, TM - d, axis=0)
            dx_tail = dx_tail + cw[3 - d:4 - d] * pltpu.roll(tail_src, 16 - d, axis=0)[0:8]
            rolled, hd = _shift_down(x, prev8, d)
            b_sc[...] = rolled
            b_sc[0:8, :] = hd
            stats_ref[7 - d:8 - d, :] += jnp.sum(dc * b_sc[...], axis=0, keepdims=True)
        dcnext_sc[...] = dc[0:8]
        dproj_ref[...] = dx.astype(BF16)
        dproj_ref[TM - 8:TM, :] = dx_tail.astype(BF16)

    rev = lambda s: nt - 1 - s
    vec = pl.BlockSpec((1, D), lambda s: (0, 0))
    wspec = pl.BlockSpec((LRU_BLOCKS, LRU_BLOCK, LRU_BLOCK), lambda s: (0, 0, 0))
    prev16 = lambda col: pl.BlockSpec((16, D), lambda s: (jnp.maximum(rev(s) * (TM // 16) - 1, 0), col))
    tile = lambda: pl.BlockSpec((TM, D), lambda s: (rev(s), 0))
    h_lru, c_sv, r_sv, i_sv, la_sv, mult_sv = saved
    return _hosted_call(
        body,
        name="lru_bwd",
        grid=(nt,),
        in_specs=[
            pl.BlockSpec(memory_space=pl.ANY),
            pl.BlockSpec((TM, D), lambda s: (rev(s), LRU_IN_COL)), prev16(LRU_IN_COL),
            tile(), prev16(0),
            tile(), tile(), tile(), tile(), tile(), tile(),
            pl.BlockSpec((4, D), lambda s: (0, 0)),
            wspec, wspec, vec,
        ],
        out_specs=[
            pl.BlockSpec((TM, D), lambda s: (rev(s), LRU_IN_COL)),
            wspec, wspec,
            pl.BlockSpec((8, D), lambda s: (0, 0)),
        ],
        out_shape=[
            jax.ShapeDtypeStruct(dproj.shape, BF16),
            jax.ShapeDtypeStruct((LRU_BLOCKS, LRU_BLOCK, LRU_BLOCK), F32),
            jax.ShapeDtypeStruct((LRU_BLOCKS, LRU_BLOCK, LRU_BLOCK), F32),
            jax.ShapeDtypeStruct((8, D), F32),
        ],
        aliases={0: 0},
        scratch_shapes=[
            pltpu.VMEM((8, D), F32), pltpu.VMEM((8, D), F32), pltpu.VMEM((8, D), F32),
            pltpu.VMEM((TM, D), F32), pltpu.VMEM((TM, D), F32), pltpu.VMEM((TM, D), F32),
        ],
        args=(dproj, proj, proj, h_lru, h_lru, c_sv, r_sv, i_sv, la_sv, mult_sv, dhl, conv_w, wa, wx, lam),
        riders=riders,
    )


def _in_proj_bwd(dproj, w_in, part, prev=None, riders=()):
    rows = dproj.shape[0]
    tm = _heavy_tile(rows)
    nt = rows // tm
    first = 0 if part == 0 else (nt + 1) // 2
    count = (nt + 1) // 2 if part == 0 else nt - first

    def body(*refs):
        dproj_ref, w_hbm, du_ref, w_sc, w_sem = refs[-5:]

        @pl.when(pl.program_id(0) == 0)
        def _():
            moves = [((0, 3), 0), ((4, 1), 3), ((3, 1), 4), ((5, 3), 5)]
            copies = [pltpu.make_async_copy(w_hbm.at[:, pl.ds(src * D, n * D)], w_sc.at[:, pl.ds(dst * D, n * D)], w_sem.at[q])
                      for q, ((src, n), dst) in enumerate(moves)]
            for cp in copies:
                cp.start()
            for cp in copies:
                cp.wait()

        du_ref[...] = _dot_nt(dproj_ref[...], w_sc[...])

    in_specs = [pl.BlockSpec((tm, IN_COLS), lambda i: (first + i, 0)), _ANY]
    args = (dproj, w_in)
    if prev is not None:
        in_specs = [_ANY] + in_specs
        args = (prev,) + args
    return _hosted_call(
        body,
        name="in_proj_bwd_%d" % part,
        grid=(count,),
        in_specs=in_specs,
        out_specs=[pl.BlockSpec((tm, D), lambda i: (first + i, 0))],
        out_shape=[jax.ShapeDtypeStruct((rows, D), F32)],
        scratch_shapes=[pltpu.VMEM((D, IN_COLS), BF16), pltpu.SemaphoreType.DMA((4,))],
        aliases={0: 0} if prev is not None else None,
        args=args,
        riders=riders,
    )


def _norm1_bwd(du, dh1, head, x2d, rstd1, norm_w, riders=()):
    rows = du.shape[0]

    def body(du_ref, dh1_ref, head_ref, x_ref, rstd_ref, nw_ref, gx_ref, ghead_ref, stats_ref):
        i = pl.program_id(0)

        @pl.when(i == 0)
        def _():
            stats_ref[...] = jnp.zeros_like(stats_ref)

        def finish(h0, out_ref):
            du = du_ref[...]
            rs = rstd_ref[...]
            n = h0 * rs
            stats_ref[0:1, :] += jnp.sum(du * n, axis=0, keepdims=True)
            dn = du * nw_ref[...]
            out_ref[...] = dh1_ref[...] + rs * (dn - n * jnp.mean(dn * n, axis=-1, keepdims=True))

        @pl.when(i == 0)
        def _():
            finish(head_ref[...], ghead_ref)

        @pl.when(i > 0)
        def _():
            finish(x_ref[...], gx_ref)

    tile = pl.BlockSpec((TM, D), lambda i: (i, 0))
    return _hosted_call(
        body,
        name="norm1_bwd",
        grid=(rows // TM,),
        in_specs=[
            tile, tile,
            pl.BlockSpec((FRONT, D), lambda i: (0, 0)),
            pl.BlockSpec((TM, D), lambda i: (jnp.maximum(i - 1, 0), 0)),
            pl.BlockSpec((TM, 1), lambda i: (i, 0)),
            pl.BlockSpec((1, D), lambda i: (0, 0)),
        ],
        out_specs=[
            pl.BlockSpec((TM, D), lambda i: (jnp.maximum(i - 1, 0), 0)),
            pl.BlockSpec((FRONT, D), lambda i: (0, 0)),
            pl.BlockSpec((8, D), lambda i: (0, 0)),
        ],
        out_shape=[
            jax.ShapeDtypeStruct(x2d.shape, F32),
            jax.ShapeDtypeStruct((FRONT, D), F32),
            jax.ShapeDtypeStruct((8, D), F32),
        ],
        scratch_shapes=[],
        args=(du, dh1, head, x2d, rstd1, norm_w),
        riders=riders,
    )


def _matmul_tn(name, x, dy, out_cols, col0=0, prev=None, k_block=None, n_block=None, riders=(), col_map=None):
    col_map = col_map or (lambda n: n)
    rows, kdim = x.shape
    ndim = dy.shape[1]
    kb = k_block or kdim
    nb = n_block or ndim
    step_bytes = lambda t: 2 * t * (kb * x.dtype.itemsize + nb * dy.dtype.itemsize) + 2 * kb * nb * 4
    tr = next(t for t in (2816, 1408, TM_HEAVY, TM) if rows % t == 0 and (t == TM or step_bytes(t) <= TN_VMEM_BUDGET))
    nr, nk, nn = rows // tr, kdim // kb, ndim // nb
    cb0 = col0 // nb

    def body(*refs):
        x_ref, dy_ref, out_ref = refs[-3], refs[-2], refs[-1]
        part = _dot_tn(x_ref[...].astype(BF16), dy_ref[...].astype(BF16))

        @pl.when(pl.program_id(2) == 0)
        def _():
            out_ref[...] = part

        @pl.when(pl.program_id(2) > 0)
        def _():
            out_ref[...] += part

    in_specs = [
        pl.BlockSpec((tr, kb), lambda n, k, r: (r, k)),
        pl.BlockSpec((tr, nb), lambda n, k, r: (r, n)),
    ]
    args = [x, dy]
    aliases = {}
    if prev is not None:
        in_specs = [pl.BlockSpec(memory_space=pl.ANY)] + in_specs
        args = [prev] + args
        aliases = {0: 0}
    (out,), rider_outs = _hosted_call(
        body,
        name=name,
        grid=(nn, nk, nr),
        in_specs=in_specs,
        out_specs=[pl.BlockSpec((kb, nb), lambda n, k, r: (k, cb0 + col_map(n)))],
        out_shape=[jax.ShapeDtypeStruct((kdim, out_cols), F32)],
        scratch_shapes=[],
        aliases=aliases,
        args=args,
        riders=riders,
    )
    return (out, rider_outs) if riders else out


def _local_step(x2d, target, w, plan):
    rows = FRONT + x2d.shape[0]
    head = jnp.concatenate([jnp.zeros((PAD_ROWS, D), F32), w["meta_tokens"]], axis=0)
    cos_t, sin_t = _rope_tables(rows)
    intra, intra_t, q_dec, k_dec, c_dec = _decay_tables()
    grads = {}

    def hosted(host, fn, *args, **kwargs):
        outs, rider_outs = fn(*args, riders=plan.riders(host, w, grads), **kwargs)
        plan.after(host, rider_outs, w, grads)
        return outs

    (u1, rstd1), _ = _norm1(head, x2d, w["mix_norm_w"])
    proj, w["w_in"] = hosted("in_proj", _in_proj, u1, w["w_in_shard"], w["route"], cos_t, sin_t)
    o, sprev = hosted("retention_fwd", _retention_fwd, proj, intra, q_dec, k_dec, c_dec)
    lru_args = (w["conv_w"], w["conv_b"], w["lru_wa"], w["lru_wx"], w["lru_ba"], w["lru_bx"], w["lru_lambda"])
    lru_saved = hosted("lru_fwd", _lru_fwd, proj, *lru_args)
    h_lru = lru_saved[0]
    y_ret, y_lru, h1, u2, rstd2 = hosted("mix_fwd", _mix_fwd, head, x2d, o, proj, h_lru, w["w_branch_ret"],
                                         w["w_branch_lru"], w["w_out"], w["ffn_norm_w"])
    silu, dsilu, act = hosted("ffn_fwd_gate", _ffn_fwd_gate, u2, w["w_ffn_in"])
    dh2, stats_loss = _ffn_out_loss(act, h1, w["w_ffn_out"], target, w["final_norm_w"])

    dg, dup, dh2b, dh1, stats_ffn = _ffn_bwd(dh2, silu, dsilu, h1, rstd2, w["w_ffn_in"], w["w_ffn_out"], w["ffn_norm_w"])
    grads["w_ffn_in"] = _matmul_tn("dw_ffn_up", u2, dup, 2 * FFN, col0=FFN, n_block=FFN_HALF,
                                   prev=_matmul_tn("dw_ffn_gate", u2, dg, 2 * FFN, n_block=FFN_HALF))
    grads["w_ffn_out"] = _matmul_tn("dw_ffn_out", act, dh2b, D, k_block=FFN_HALF)
    (dproj, do, dhl, mixed, a_ret, a_lru, dy_ret, dy_lru) = hosted(
        "mix_bwd", _mix_bwd, dh1, o, proj, h_lru, y_ret, y_lru, w["w_branch_ret"], w["w_branch_lru"], w["w_out"])
    grads["w_out"] = _matmul_tn("dw_out", mixed, dh1, D)
    grads["w_branch_ret"] = _matmul_tn("dw_branch_ret", a_ret, dy_ret, D)
    grads["w_branch_lru"] = _matmul_tn("dw_branch_lru", a_lru, dy_lru, D)
    (dproj,) = hosted("retention_bwd", _retention_bwd, dproj, proj, do, sprev, intra, intra_t, q_dec, k_dec, c_dec,
                      cos_t, sin_t)
    dproj, grads["lru_wa"], grads["lru_wx"], stats_lru = hosted(
        "lru_bwd", _lru_bwd, dproj, proj, lru_saved, dhl, w["conv_w"], w["lru_wa"], w["lru_wx"], w["lru_lambda"])
    grads["w_in"], rider_outs = _matmul_tn("dw_in", u1, dproj, IN_COLS, n_block=D, col_map=_swap_3_4,
                                           riders=plan.riders("dw_in", w, grads))
    plan.after("dw_in", rider_outs, w, grads)
    (du1,) = hosted("in_proj_bwd_0", _in_proj_bwd, dproj, w["w_in"], 0)
    (du1,) = hosted("in_proj_bwd_1", _in_proj_bwd, dproj, w["w_in"], 1, du1)
    (grad_x, grad_head, stats_in), _ = _norm1_bwd(du1, dh1, head, x2d, rstd1, w["mix_norm_w"])
    return grad_x, grad_head, grads, [stats_loss, stats_ffn, stats_in, stats_lru]


BIG_PIECES = {
    "w_in": ("col", (D, 2 * D)),
    "w_ffn_in": ("col", (D, FFN_HALF)),
    "w_ffn_out": ("row", (FFN // 4, D)),
    "w_branch_ret": ("row", (D // 4, D)),
    "w_branch_lru": ("row", (D // 4, D)),
    "w_out": ("row", (D // 4, D)),
    "lru_wa": ("lru", (LRU_BLOCKS, LRU_BLOCK // 4, LRU_BLOCK)),
    "lru_wx": ("lru", (LRU_BLOCKS, LRU_BLOCK // 4, LRU_BLOCK)),
}
SMALL_PIECES = {"meta_tokens": ("col", (N_META, D // 4)), "conv_w": ("col", (4, D // 4))}


def _full_shape(kind, shard):
    if kind == "col":
        return (shard[0], 4 * shard[1])
    if kind == "row":
        return (4 * shard[0], shard[1])
    return (shard[0], 4 * shard[1], shard[2])


def _half_shape(kind, shard):
    return (shard[0] // 2,) + tuple(shard[1:])


def _aligned(start, multiple):
    return start if isinstance(start, int) else pl.multiple_of(start, multiple)


def _lead(h, size):
    if h is None:
        return pl.ds(0, size)
    return pl.ds(_aligned(h * (size // 2), size // 2), size // 2)


def _full_region(ref, kind, shard, s, h):
    if kind == "col":
        return ref.at[_lead(h, shard[0]), pl.ds(_aligned(s * shard[1], shard[1]), shard[1])]
    if kind == "row":
        size = shard[0] if h is None else shard[0] // 2
        start = s * shard[0] + (0 if h is None else h * (shard[0] // 2))
        return ref.at[pl.ds(_aligned(start, 16), size), :]
    return ref.at[_lead(h, shard[0]), pl.ds(_aligned(s * shard[1], shard[1]), shard[1]), :]


def _shard_region(ref, shard, h):
    return ref.at[_lead(h, shard[0])]


def _place():
    x, y, c = lax.axis_index("x"), lax.axis_index("y"), lax.axis_index("c")
    return x, y, c, 2 * x + y


def _other_chip(s, c, k):
    s2 = jnp.bitwise_xor(s, k)
    return s2, (s2 // 2, s2 % 2, c)


def _remote(src, dst, send_sem, recv_sem, dev):
    return pltpu.make_async_remote_copy(src_ref=src, dst_ref=dst, send_sem=send_sem, recv_sem=recv_sem,
                                        device_id=dev, device_id_type=MESH)


_ANY = pl.BlockSpec(memory_space=pl.ANY)


class _Rider:
    def __init__(self, ins, out_shapes, sem_shapes, build, aliased=False):
        self.ins, self.out_shapes, self.sem_shapes, self.build, self.aliased = ins, out_shapes, sem_shapes, build, aliased


def _hosted_call(body, *, name, grid, in_specs, out_specs, out_shape, scratch_shapes, args, riders=(), aliases=None,
                 prefetch=None, riders_after_body=False):
    n_in, n_out, n_sc = len(in_specs), len(out_shape), len(scratch_shapes)
    r_in = [a for r in riders for a in r.ins]
    r_out = [s for r in riders for s in r.out_shapes]
    r_sem = [s for r in riders for s in r.sem_shapes]
    lead = () if prefetch is None else (prefetch,)
    assert prefetch is None or not (aliases or any(r.aliased for r in riders))

    def full_body(*refs):
        head, refs = refs[:len(lead)], refs[len(lead):]
        ins, rin = refs[:n_in], refs[n_in:n_in + len(r_in)]
        o0 = n_in + len(r_in)
        outs, rout = refs[o0:o0 + n_out], refs[o0 + n_out:o0 + n_out + len(r_out)]
        s0 = o0 + n_out + len(r_out)
        scratch, rsem = refs[s0:s0 + n_sc], refs[s0 + n_sc:]
        starts, waits = [], []
        pi = po = ps = 0
        for r in riders:
            st, wt = r.build(rin[pi:pi + len(r.ins)], rout[po:po + len(r.out_shapes)], rsem[ps:ps + len(r.sem_shapes)])
            starts += st
            waits += wt
            pi, po, ps = pi + len(r.ins), po + len(r.out_shapes), ps + len(r.sem_shapes)
        first = functools.reduce(jnp.logical_and, [pl.program_id(d) == 0 for d in range(len(grid))])
        last = functools.reduce(jnp.logical_and, [pl.program_id(d) == grid[d] - 1 for d in range(len(grid))])

        def start_riders():
            @pl.when(first)
            def _():
                for cp in starts:
                    cp.start()

        if riders and not riders_after_body:
            start_riders()
        body(*head, *ins, *outs, *scratch)
        if riders and riders_after_body:
            start_riders()
        if riders:
            @pl.when(last)
            def _():
                for wait in waits:
                    wait()

    io_aliases = dict(aliases or {})
    pi = po = 0
    for r in riders:
        if r.aliased:
            for q in range(len(r.ins)):
                io_aliases[n_in + pi + q] = n_out + po + q
        pi, po = pi + len(r.ins), po + len(r.out_shapes)
    specs = dict(
        grid=grid,
        in_specs=list(in_specs) + [_ANY] * len(r_in),
        out_specs=list(out_specs) + [_ANY] * len(r_out),
        scratch_shapes=list(scratch_shapes) + r_sem,
    )
    if prefetch is not None:
        specs = dict(grid_spec=pltpu.PrefetchScalarGridSpec(num_scalar_prefetch=1, **specs))
    res = pl.pallas_call(
        full_body,
        name=name,
        out_shape=list(out_shape) + r_out,
        input_output_aliases=io_aliases,
        compiler_params=_params(("arbitrary",) * len(grid)),
        **specs,
    )(*lead, *args, *r_in)
    rider_outs, po = [], n_out
    for r in riders:
        rider_outs.append(list(res[po:po + len(r.out_shapes)]))
        po += len(r.out_shapes)
    return list(res[:n_out]), rider_outs


def _run_riders(name, riders):
    r_in = [a for r in riders for a in r.ins]
    r_out = [s for r in riders for s in r.out_shapes]
    r_sem = [s for r in riders for s in r.sem_shapes]

    def body(*refs):
        rin, rout, rsem = refs[:len(r_in)], refs[len(r_in):len(r_in) + len(r_out)], refs[len(r_in) + len(r_out):]
        pi = po = ps = 0
        all_waits = []
        for r in riders:
            starts, waits = r.build(rin[pi:pi + len(r.ins)], rout[po:po + len(r.out_shapes)], rsem[ps:ps + len(r.sem_shapes)])
            for cp in starts:
                cp.start()
            all_waits += waits
            pi, po, ps = pi + len(r.ins), po + len(r.out_shapes), ps + len(r.sem_shapes)
        for wait in all_waits:
            wait()

    io_aliases = {}
    pi = po = 0
    for r in riders:
        if r.aliased:
            for q in range(len(r.ins)):
                io_aliases[pi + q] = po + q
        pi, po = pi + len(r.ins), po + len(r.out_shapes)
    res = pl.pallas_call(
        body,
        name=name,
        in_specs=[_ANY] * len(r_in),
        out_specs=[_ANY] * len(r_out),
        out_shape=r_out,
        scratch_shapes=r_sem,
        input_output_aliases=io_aliases,
    )(*r_in)
    outs, po = [], 0
    for r in riders:
        outs.append(list(res[po:po + len(r.out_shapes)]))
        po += len(r.out_shapes)
    return outs


def _piece(name):
    if name in BIG_PIECES:
        return (name, *BIG_PIECES[name], True)
    return (name, *SMALL_PIECES[name], False)


def _gather_rider(shards, names):
    pieces = [_piece(n) for n in names]
    n = len(pieces)

    def build(ins, outs, sems):
        local_sem, ici_send, ici_recv = sems
        _, _, c, s = _place()
        starts, waits = [], []
        for p, (_, kind, shard, split) in enumerate(pieces):
            cp = pltpu.make_async_copy(ins[p], _full_region(outs[p], kind, shard, s, None), local_sem.at[p])
            starts.append(cp)
            waits.append(cp.wait)
            h = c if split else None
            for k in (1, 2, 3):
                s2, dev = _other_chip(s, c, k)
                cp = _remote(_shard_region(ins[p], shard, h), _full_region(outs[p], kind, shard, s, h),
                             ici_send.at[p, k - 1], ici_recv.at[p, k - 1], dev)
                starts.append(cp)
                waits.append(cp.wait_send)
                region = _full_region(outs[p], kind, shard, s2, h)
                waits.append(_remote(region, region, ici_send.at[p, k - 1], ici_recv.at[p, k - 1], dev).wait_recv)
        return starts, waits

    return _Rider(
        [shards[name] for name in names],
        [jax.ShapeDtypeStruct(_full_shape(kind, shard), shards[name].dtype) for name, kind, shard, _ in pieces],
        [pltpu.SemaphoreType.DMA((n,)), pltpu.SemaphoreType.DMA((n, 3)), pltpu.SemaphoreType.DMA((n, 3))],
        build)


def _forward_rider(gathered, names):
    pieces = [_piece(n) for n in names]
    n = len(pieces)

    def build(ins, outs, sems):
        fwd_send, fwd_recv = sems
        x, y, c, s = _place()
        sibling = (x, y, 1 - c)
        starts, waits = [], []
        for p, (_, kind, shard, _) in enumerate(pieces):
            for k in (1, 2, 3):
                s2, _ = _other_chip(s, c, k)
                mine = _full_region(outs[p], kind, shard, s2, c)
                theirs = _full_region(outs[p], kind, shard, s2, 1 - c)
                cp = _remote(mine, mine, fwd_send.at[p, k - 1], fwd_recv.at[p, k - 1], sibling)
                starts.append(cp)
                waits.append(cp.wait_send)
                waits.append(_remote(theirs, theirs, fwd_send.at[p, k - 1], fwd_recv.at[p, k - 1], sibling).wait_recv)
        return starts, waits

    return _Rider(
        [gathered[name] for name in names],
        [jax.ShapeDtypeStruct(gathered[name].shape, gathered[name].dtype) for name in names],
        [pltpu.SemaphoreType.DMA((n, 3)), pltpu.SemaphoreType.DMA((n, 3))],
        build, aliased=True)


def _pair_exchange_rider(grads, names):
    n = len(names)

    def build(ins, outs, sems):
        send_sem, recv_sem = sems
        x, y, c, _ = _place()
        sibling = (x, y, 1 - c)
        starts, waits = [], []
        for p, name in enumerate(names):
            kind, shard = BIG_PIECES[name]
            for s2 in range(4):
                cp = _remote(_full_region(ins[p], kind, shard, s2, 1 - c), outs[p].at[s2], send_sem.at[p, s2],
                             recv_sem.at[p, s2], sibling)
                starts.append(cp)
                waits.append(cp.wait_send)
                waits.append(_remote(outs[p].at[s2], outs[p].at[s2], send_sem.at[p, s2], recv_sem.at[p, s2], sibling).wait_recv)
        return starts, waits

    return _Rider(
        [grads[name] for name in names],
        [jax.ShapeDtypeStruct((4,) + _half_shape(*BIG_PIECES[name]), F32) for name in names],
        [pltpu.SemaphoreType.DMA((n, 4))] * 2,
        build)


def _half_specs(kind, shard):
    half = shard[0] // 2
    if kind == "col":
        full = pl.BlockSpec((half, shard[1]), lambda j, pr: (pr[1], j))
        buf = pl.BlockSpec((None, half, shard[1]), lambda j, pr: (j, 0, 0))
    elif kind == "row":
        full = pl.BlockSpec((half, shard[1]), lambda j, pr: (2 * j + pr[1], 0))
        buf = pl.BlockSpec((None, half, shard[1]), lambda j, pr: (j, 0, 0))
    else:
        full = pl.BlockSpec((half, shard[1], shard[2]), lambda j, pr: (pr[1], j, 0))
        buf = pl.BlockSpec((None, half, shard[1], shard[2]), lambda j, pr: (j, 0, 0, 0))
    return full, buf


def _pair_sum(name, grad, recv, place):
    kind, shard = BIG_PIECES[name]
    full, buf = _half_specs(kind, shard)

    def body(pr, g_ref, r_ref, o_ref):
        o_ref[...] = (g_ref[...] + r_ref[...]).astype(BF16)

    return pl.pallas_call(
        body,
        name="pair_sum_" + name,
        grid_spec=pltpu.PrefetchScalarGridSpec(num_scalar_prefetch=1, grid=(4,), in_specs=[full, buf], out_specs=buf),
        out_shape=jax.ShapeDtypeStruct((4,) + _half_shape(kind, shard), BF16),
        compiler_params=_params(("arbitrary",)),
    )(place, grad, recv)


def _chip_exchange_rider(sums, names):
    n = len(names)

    def build(ins, outs, sems):
        send_sem, recv_sem = sems
        _, _, c, s = _place()
        starts, waits = [], []
        for p in range(n):
            for k in (1, 2, 3):
                s2, dev = _other_chip(s, c, k)
                cp = _remote(ins[p].at[s2], outs[p].at[k - 1], send_sem.at[p, k - 1], recv_sem.at[p, k - 1], dev)
                starts.append(cp)
                waits.append(cp.wait_send)
                waits.append(_remote(outs[p].at[k - 1], outs[p].at[k - 1], send_sem.at[p, k - 1], recv_sem.at[p, k - 1],
                                     dev).wait_recv)
        return starts, waits

    return _Rider(
        [sums[name] for name in names],
        [jax.ShapeDtypeStruct((3,) + _half_shape(*BIG_PIECES[name]), BF16) for name in names],
        [pltpu.SemaphoreType.DMA((n, 3))] * 2,
        build)


def _chip_sum(name, grad, recv_pair, recv_chip, place):
    kind, shard = BIG_PIECES[name]
    half = shard[0] // 2
    tail = tuple(shard[1:])
    zeros = (0,) * len(tail)
    if kind == "col":
        full = pl.BlockSpec((half,) + tail, lambda j, pr: (pr[1], pr[0]))
    elif kind == "row":
        full = pl.BlockSpec((half,) + tail, lambda j, pr: (2 * pr[0] + pr[1], 0))
    else:
        full = pl.BlockSpec((half,) + tail, lambda j, pr: (pr[1], pr[0], 0))
    pair = pl.BlockSpec((None, half) + tail, lambda j, pr: (pr[0], 0) + zeros)
    chip = pl.BlockSpec((3, half) + tail, lambda j, pr: (0, 0) + zeros)
    out = pl.BlockSpec((half,) + tail, lambda j, pr: (pr[1],) + zeros)

    def body(pr, g_ref, rp_ref, rc_ref, o_ref):
        total = g_ref[...] + rp_ref[...]
        for k in range(3):
            total = total + rc_ref[k].astype(F32)
        o_ref[...] = total

    return pl.pallas_call(
        body,
        name="chip_sum_" + name,
        grid_spec=pltpu.PrefetchScalarGridSpec(num_scalar_prefetch=1, grid=(1,), in_specs=[full, pair, chip], out_specs=out),
        out_shape=jax.ShapeDtypeStruct(shard, F32),
        compiler_params=_params(("arbitrary",)),
    )(place, grad, recv_pair, recv_chip)


def _sibling_exchange_rider(halves, names):
    n = len(names)

    def build(ins, outs, sems):
        send_sem, recv_sem = sems
        x, y, c, _ = _place()
        sibling = (x, y, 1 - c)
        starts, waits = [], []
        for p, name in enumerate(names):
            shard = BIG_PIECES[name][1]
            mine = _shard_region(outs[p], shard, c)
            theirs = _shard_region(outs[p], shard, 1 - c)
            cp = _remote(mine, mine, send_sem.at[p], recv_sem.at[p], sibling)
            starts.append(cp)
            waits.append(cp.wait_send)
            waits.append(_remote(theirs, theirs, send_sem.at[p], recv_sem.at[p], sibling).wait_recv)
        return starts, waits

    return _Rider(
        [halves[name] for name in names],
        [jax.ShapeDtypeStruct(BIG_PIECES[name][1], F32) for name in names],
        [pltpu.SemaphoreType.DMA((n,))] * 2,
        build, aliased=True)


FIRST_WEIGHTS = ["meta_tokens", "conv_w"]
WEIGHT_GROUPS = {
    "lru": ["lru_wa", "lru_wx"],
    "branch": ["w_branch_ret", "w_branch_lru", "w_out"],
    "ffn_in": ["w_ffn_in"],
    "ffn_out": ["w_ffn_out"],
}
WEIGHT_SCHEDULE = {
    "in_proj": [("gather", "lru")],
    "retention_fwd": [("forward", "lru"), ("gather", "branch")],
    "lru_fwd": [("forward", "branch"), ("gather", "ffn_in")],
    "mix_fwd": [("forward", "ffn_in"), ("gather", "ffn_out")],
    "ffn_fwd_gate": [("forward", "ffn_out")],
}
GRAD_GROUPS = {
    "ffn": ["w_ffn_in", "w_ffn_out"],
    "mixer": ["w_out", "w_branch_ret", "w_branch_lru", "lru_wa", "lru_wx"],
    "in": ["w_in"],
}
GRAD_SCHEDULE = {
    "mix_bwd": [("pair", "ffn")],
    "retention_bwd": [("chip", "ffn")],
    "lru_bwd": [("sibling", "ffn")],
    "dw_in": [("pair", "mixer")],
    "in_proj_bwd_0": [("chip", "mixer"), ("pair", "in")],
    "in_proj_bwd_1": [("sibling", "mixer"), ("chip", "in")],
}


class _CommPlan:
    def __init__(self, shards, place):
        self.shards, self.place = shards, place
        self.late = {}
        self.recv_pair, self.sums, self.recv_chip, self.halves, self.final = {}, {}, {}, {}, {}

    def _grad_rider(self, stage, group, grads):
        names = GRAD_GROUPS[group]
        if stage == "pair":
            return _pair_exchange_rider(grads, names)
        if stage == "chip":
            return _chip_exchange_rider(self.sums, names)
        return _sibling_exchange_rider(self.halves, names)

    def _grad_after(self, stage, group, outs, grads):
        names = GRAD_GROUPS[group]
        if stage == "pair":
            for n, o in zip(names, outs):
                self.recv_pair[n] = o
                self.sums[n] = _pair_sum(n, grads[n], o, self.place)
        elif stage == "chip":
            for n, o in zip(names, outs):
                self.halves[n] = _chip_sum(n, grads[n], self.recv_pair[n], o, self.place)
        else:
            self.final.update(zip(names, outs))

    def riders(self, host, w, grads):
        if host in WEIGHT_SCHEDULE:
            return [_gather_rider(self.shards, WEIGHT_GROUPS[group]) if stage == "gather"
                    else _forward_rider(self.late, WEIGHT_GROUPS[group]) for stage, group in WEIGHT_SCHEDULE[host]]
        return [self._grad_rider(stage, group, grads) for stage, group in GRAD_SCHEDULE.get(host, [])]

    def after(self, host, rider_outs, w, grads):
        for (stage, group), outs in zip(WEIGHT_SCHEDULE.get(host, []), rider_outs):
            (self.late if stage == "gather" else w).update(zip(WEIGHT_GROUPS[group], outs))
        for (stage, group), outs in zip(GRAD_SCHEDULE.get(host, []), rider_outs):
            self._grad_after(stage, group, outs, grads)

    def finish(self, partial):
        outs, (blocks,) = _run_riders("tail_exchange", [_sibling_exchange_rider(self.halves, GRAD_GROUPS["in"]),
                                                        _small_exchange_rider(partial)])
        self.final.update(zip(GRAD_GROUPS["in"], outs))
        return self.final, blocks


def _adamw_math(w, g, m, v):
    m = ADAM_B1 * m + (1.0 - ADAM_B1) * g
    v = ADAM_B2 * v + (1.0 - ADAM_B2) * (g * g)
    m_hat = m / (1.0 - ADAM_B1 ** ADAM_STEP)
    v_hat = v / (1.0 - ADAM_B2 ** ADAM_STEP)
    delta = -ADAM_LR * (m_hat / (jnp.sqrt(v_hat) + ADAM_EPS) + ADAM_WD * w)
    return delta, m, v


def _adamw(name, g, w, m, v):
    rows, cols = g.shape
    tr = rows // 4 if rows % 32 == 0 else rows

    def body(g_ref, w_ref, m_ref, v_ref, go_ref, d_ref, mo_ref, vo_ref):
        gv = g_ref[...]
        delta, m2, v2 = _adamw_math(w_ref[...], gv, m_ref[...], v_ref[...])
        go_ref[...] = gv
        d_ref[...] = delta
        mo_ref[...] = m2
        vo_ref[...] = v2

    spec = pl.BlockSpec((tr, cols), lambda i: (i, 0))
    return pl.pallas_call(
        body,
        name="adamw_" + name,
        grid=(rows // tr,),
        in_specs=[spec] * 4,
        out_specs=[spec] * 4,
        out_shape=[jax.ShapeDtypeStruct((rows, cols), F32)] * 4,
        compiler_params=_params(("arbitrary",)),
    )(g, w, m, v)


SMALL_ROWS = 48
VEC_ROWS = {"final_norm_w": 1, "ffn_norm_w": 8, "mix_norm_w": 16, "conv_b": 24, "lru_ba": 25, "lru_bx": 26, "lru_lambda": 27}
VEC_NAMES = list(VEC_ROWS)
CONV_W_ROW = 28
META_ROW = 32


def _small_exchange_rider(partial):
    def build(ins, outs, sems):
        local_sem, send_sem, recv_sem = sems
        _, _, c, s = _place()
        me = 2 * s + c
        cp = pltpu.make_async_copy(ins[0], outs[0].at[me], local_sem)
        starts, waits = [cp], [cp.wait]
        for k in range(1, 8):
            peer = jnp.bitwise_xor(me, k)
            dev = (peer // 4, (peer // 2) % 2, peer % 2)
            rd = _remote(ins[0], outs[0].at[me], send_sem.at[k - 1], recv_sem.at[k - 1], dev)
            starts.append(rd)
            waits.append(rd.wait_send)
            waits.append(_remote(ins[0], outs[0].at[peer], send_sem.at[k - 1], recv_sem.at[k - 1], dev).wait_recv)
        return starts, waits

    return _Rider([partial], [jax.ShapeDtypeStruct((8, SMALL_ROWS, D), F32)],
                  [pltpu.SemaphoreType.DMA, pltpu.SemaphoreType.DMA((7,)), pltpu.SemaphoreType.DMA((7,))], build)


def _small_update(blocks, place, vecs, conv, meta):
    nvec = len(VEC_NAMES)
    qcols = D // 4

    def in_order(ref):
        total = ref[0]
        for d in range(1, 8):
            total = total + ref[d]
        return total

    def body(pr, blocks_ref, cols_ref, *refs):
        vec_refs = refs[:3 * nvec]
        conv_refs = refs[3 * nvec:3 * nvec + 3]
        meta_refs = refs[3 * nvec + 3:3 * nvec + 6]
        outs = refs[3 * nvec + 6:]
        loss_ref, vec_out, conv_out, meta_out = outs[0], outs[1:5], outs[5:9], outs[9:13]
        tot, col = in_order(blocks_ref), in_order(cols_ref)
        loss_ref[...] = jnp.sum(tot[0:1, :], axis=1, keepdims=True)
        for o in vec_out:
            o[...] = jnp.zeros_like(o)
        for j, name in enumerate(VEC_NAMES):
            w, m, v = (r[...] for r in vec_refs[3 * j:3 * j + 3])
            g = tot[VEC_ROWS[name]:VEC_ROWS[name] + 1, :]
            if name == "lru_lambda":
                g = -g / (1.0 + jnp.exp(w))
            for o, val in zip(vec_out, (g,) + _adamw_math(w, g, m, v)):
                o[j:j + 1, :] = val
        for row, n_rows, ins, group in ((CONV_W_ROW, 4, conv_refs, conv_out), (META_ROW, N_META, meta_refs, meta_out)):
            g = col[row:row + n_rows, :]
            w, m, v = (r[...] for r in ins)
            for o, val in zip(group, (g,) + _adamw_math(w, g, m, v)):
                o[...] = val

    whole = lambda shape: pl.BlockSpec(shape, lambda i, pr: (0,) * len(shape))
    in_specs = [whole((8, SMALL_ROWS, D)), pl.BlockSpec((8, SMALL_ROWS, qcols), lambda i, pr: (0, 0, pr[0]))]
    in_specs += [whole((1, D))] * (3 * nvec) + [whole((4, qcols))] * 3 + [whole((N_META, qcols))] * 3
    out_shapes = [(1, 1)] + [(8, D)] * 4 + [(4, qcols)] * 4 + [(N_META, qcols)] * 4
    return pl.pallas_call(
        body,
        name="small_update",
        grid_spec=pltpu.PrefetchScalarGridSpec(num_scalar_prefetch=1, grid=(1,), in_specs=in_specs,
                                               out_specs=[whole(s) for s in out_shapes]),
        out_shape=[jax.ShapeDtypeStruct(s, F32) for s in out_shapes],
        compiler_params=_params(("arbitrary",)),
    )(place, blocks, blocks, *[a for t in vecs for a in t], *conv, *meta)


WEIGHT_ORDER = ["meta_tokens", "mix_norm_w", "w_in", "conv_w", "conv_b", "lru_wa", "lru_ba", "lru_wx", "lru_bx", "lru_lambda",
                "w_branch_ret", "w_branch_lru", "w_out", "ffn_norm_w", "w_ffn_in", "w_ffn_out", "final_norm_w"]


def kernel(x, meta_tokens, mix_norm_w, w_in, conv_w, conv_b, lru_wa, lru_ba, lru_wx, lru_bx, lru_lambda, w_branch_ret, w_branch_lru, w_out, ffn_norm_w, w_ffn_in, w_ffn_out, final_norm_w, loss_target, m_meta_tokens, m_mix_norm_w, m_w_in, m_conv_w, m_conv_b, m_lru_wa, m_lru_ba, m_lru_wx, m_lru_bx, m_lru_lambda, m_w_branch_ret, m_w_branch_lru, m_w_out, m_ffn_norm_w, m_w_ffn_in, m_w_ffn_out, m_final_norm_w, v_meta_tokens, v_mix_norm_w, v_w_in, v_conv_w, v_conv_b, v_lru_wa, v_lru_ba, v_lru_wx, v_lru_bx, v_lru_lambda, v_w_branch_ret, v_w_branch_lru, v_w_out, v_ffn_norm_w, v_w_ffn_in, v_w_ffn_out, v_final_norm_w):
    args = locals()
    wts = {n: args[n] for n in WEIGHT_ORDER}
    mom = {n: args["m_" + n] for n in WEIGHT_ORDER}
    var = {n: args["v_" + n] for n in WEIGHT_ORDER}
    place = jnp.stack([2 * lax.axis_index("x") + lax.axis_index("y"), lax.axis_index("c")]).astype(jnp.int32)

    shards = {n: wts[n][0].astype(BF16) for n in BIG_PIECES}
    shards["meta_tokens"] = wts["meta_tokens"]
    shards["conv_w"] = wts["conv_w"][0]
    plan = _CommPlan(shards, place)
    (first,) = _run_riders("gather_first", [_gather_rider(shards, FIRST_WEIGHTS)])
    w = dict(zip(FIRST_WEIGHTS, first))
    for n in VEC_NAMES:
        w[n] = wts[n].reshape(1, D)
    chips = jnp.bitwise_xor(place[0], jnp.arange(4, dtype=jnp.int32))
    w["route"] = jnp.concatenate([place, jnp.stack([2 * chips, 2 * chips + 1], axis=1).reshape(8)])
    w["w_in_shard"] = shards["w_in"]

    grad_x, grad_head, _, stats = _local_step(x[0], loss_target[0], w, plan)
    partial = jnp.concatenate(stats + [grad_head[PAD_ROWS:]], axis=0)
    shard_grads, blocks = plan.finish(partial)

    out = {}
    for n in BIG_PIECES:
        shape2d = (-1, wts[n].shape[-1])
        res = _adamw(n, *[a.reshape(shape2d) for a in (shard_grads[n], wts[n], mom[n], var[n])])
        out[n] = [r.reshape(wts[n].shape) for r in res]

    vecs = [tuple(a[n].reshape(1, D) for a in (wts, mom, var)) for n in VEC_NAMES]
    conv = tuple(a["conv_w"][0] for a in (wts, mom, var))
    meta = tuple(a["meta_tokens"] for a in (wts, mom, var))
    res = _small_update(blocks, place, vecs, conv, meta)
    loss = res[0].reshape(())
    for j, n in enumerate(VEC_NAMES):
        out[n] = [r[j].reshape(wts[n].shape) for r in res[1:5]]
    out["conv_w"] = [r.reshape(wts["conv_w"].shape) for r in res[5:9]]
    out["meta_tokens"] = list(res[9:13])

    return (loss, grad_x.reshape(x.shape)) + tuple(out[n][kind] for kind in range(4) for n in WEIGHT_ORDER)
```

```python
import functools
import math

import jax
import jax.numpy as jnp
from jax import lax
from jax.experimental import pallas as pl
from jax.experimental.pallas import tpu as pltpu

F32 = jnp.float32
BF16 = jnp.bfloat16

D = 1024
HEADS = 8
DH = 128
CHUNK = 256
N_META = 16
FRONT = 256
PAD_ROWS = FRONT - N_META
LRU_BLOCKS = 4
LRU_BLOCK = 256
LRU_C = 8.0
FFN = 2816
FFN_HALF = FFN // 2
IN_COLS = 8 * D
ROPE_BASE = 10000.0
EPS = 1e-6
QK_SCALE = DH ** -0.5

ADAM_LR = 0.001
ADAM_B1 = 0.9
ADAM_B2 = 0.999
ADAM_EPS = 1e-08
ADAM_WD = 0.01
ADAM_STEP = 10

TM = 256
TM_HEAVY = 768
FFN_BLOCK = FFN // 2
FFN_SUB = 768
TM_MIX_BWD = 256
TM_LRU = 384
TM_MIX_FWD = 384
TM_IN_PROJ = 1408
VMEM_LIMIT = 60 * 1024 * 1024
TN_VMEM_BUDGET = 40 * 1024 * 1024

NT_DIMS = (((1,), (1,)), ((), ()))
TN_DIMS = (((0,), (0,)), ((), ()))
MESH = pl.DeviceIdType.MESH


def _params(sem=None):
    if sem is None:
        return pltpu.CompilerParams(vmem_limit_bytes=VMEM_LIMIT)
    return pltpu.CompilerParams(dimension_semantics=sem, vmem_limit_bytes=VMEM_LIMIT)


def _dot(a, b):
    return jnp.dot(a, b, preferred_element_type=F32)


def _dot_nt(a, b):
    return lax.dot_general(a, b, NT_DIMS, preferred_element_type=F32)


def _dot_tn(a, b):
    return lax.dot_general(a, b, TN_DIMS, preferred_element_type=F32)


def _sigmoid(z):
    return 1.0 / (1.0 + jnp.exp(-z))


def _log1p(x):
    return jnp.where(x < 1e-3, x * (1.0 - x * (0.5 - x * (1.0 / 3.0))), jnp.log(1.0 + x))


def _softplus(x):
    return jnp.maximum(x, 0.0) + _log1p(jnp.exp(-jnp.abs(x)))


def _neg_expm1(x):
    series = -x * (1.0 + x * (0.5 + x * (1.0 / 6.0 + x * (1.0 / 24.0 + x * (1.0 / 120.0)))))
    return jnp.where(x > -0.05, series, 1.0 - jnp.exp(x))


_GELU_K = math.sqrt(2.0 / math.pi)


def _gelu_and_grad(x):
    inner = _GELU_K * (x + 0.044715 * x * x * x)
    t = jnp.tanh(inner)
    val = 0.5 * x * (1.0 + t)
    grad = 0.5 * (1.0 + t) + 0.5 * x * (1.0 - t * t) * _GELU_K * (1.0 + 3.0 * 0.044715 * x * x)
    return val, grad


def _row_ids(i, rows, shape):
    return i * rows + lax.broadcasted_iota(jnp.int32, shape, 0)


def _rope_tables(rows):
    pos = jnp.arange(rows, dtype=jnp.int32) - PAD_ROWS
    inv_freq = ROPE_BASE ** (-jnp.arange(0, DH, 2, dtype=F32) / DH)
    ang = pos.astype(F32)[:, None] * inv_freq[None, :]
    cos, sin = jnp.cos(ang), jnp.sin(ang)
    return jnp.concatenate([cos, cos], axis=1), jnp.concatenate([-sin, sin], axis=1)


def _decay_tables():
    log_g = jnp.log(1.0 - 2.0 ** (-5.0 - jnp.arange(HEADS, dtype=F32)))
    idx = jnp.arange(CHUNK, dtype=F32)
    diff = idx[:, None] - idx[None, :]
    intra = jnp.where(diff[None] >= 0, jnp.exp(jnp.maximum(diff, 0.0)[None] * log_g[:, None, None]), 0.0)
    q_decay = jnp.exp((idx + 1.0)[:, None] * log_g[None, :])
    k_decay = jnp.exp((CHUNK - 1.0 - idx)[:, None] * log_g[None, :])
    chunk_decay = jnp.exp(CHUNK * log_g)
    wide = lambda a: jnp.repeat(a, DH, axis=-1)
    return intra, jnp.swapaxes(intra, 1, 2), wide(q_decay), wide(k_decay), wide(chunk_decay[None, :])


def _norm1(head, x2d, norm_w, riders=()):
    rows = FRONT + x2d.shape[0]

    def body(head_ref, x_ref, nw_ref, u_ref, rstd_ref):
        def norm(hv):
            rs = lax.rsqrt(jnp.mean(hv * hv, axis=-1, keepdims=True) + EPS)
            u_ref[...] = ((hv * rs) * nw_ref[...]).astype(BF16)
            rstd_ref[...] = rs

        @pl.when(pl.program_id(0) == 0)
        def _():
            norm(head_ref[...])

        @pl.when(pl.program_id(0) > 0)
        def _():
            norm(x_ref[...])

    return _hosted_call(
        body,
        name="norm1",
        grid=(rows // TM,),
        in_specs=[
            pl.BlockSpec((FRONT, D), lambda i: (0, 0)),
            pl.BlockSpec((TM, D), lambda i: (jnp.maximum(i - 1, 0), 0)),
            pl.BlockSpec((1, D), lambda i: (0, 0)),
        ],
        out_specs=[pl.BlockSpec((TM, D), lambda i: (i, 0)), pl.BlockSpec((TM, 1), lambda i: (i, 0))],
        out_shape=[jax.ShapeDtypeStruct((rows, D), BF16), jax.ShapeDtypeStruct((rows, 1), F32)],
        scratch_shapes=[],
        args=(head, x2d, norm_w),
        riders=riders,
    )


def _heavy_tile(rows):
    return TM_HEAVY if rows % TM_HEAVY == 0 else TM


def _lru_tile(rows):
    return TM_LRU if rows % TM_LRU == 0 else TM


def _swap_3_4(group):
    return jnp.where(group == 3, 4, jnp.where(group == 4, 3, group))


LRU_IN_COL = 3
GATES_COL = 1


def _in_proj(u, w_shard, route, cos_t, sin_t, riders=()):
    rows = u.shape[0]
    tm = TM_IN_PROJ if rows % TM_IN_PROJ == 0 else _heavy_tile(rows)
    nt = rows // tm
    kind, shard = BIG_PIECES["w_in"]

    def body(route_ref, u_hbm, wsh_ref, cos_ref, sin_ref, proj_ref, wfull_ref,
             u_sc, w_sc, u_sem, w_sem, local_sem, ici_send, ici_recv, fwd_send, fwd_recv):
        g, i = pl.program_id(0), pl.program_id(1)
        s, c = route_ref[0], route_ref[1]
        gid = route_ref[2 + g]
        sibling = (s // 2, s % 2, 1 - c)
        local = pltpu.make_async_copy(wsh_ref, _full_region(wfull_ref, kind, shard, s, None), local_sem)
        u_copies = [pltpu.make_async_copy(u_hbm.at[pl.ds(t * tm, tm)], u_sc.at[t], u_sem.at[t]) for t in range(nt)]
        sends, arrivals = [], []
        for k in (1, 2, 3):
            s2, dev = _other_chip(s, c, k)
            sends.append(_remote(_shard_region(wsh_ref, shard, c), _full_region(wfull_ref, kind, shard, s, c),
                                 ici_send.at[k - 1], ici_recv.at[k - 1], dev))
            mine = _full_region(wfull_ref, kind, shard, s2, c)
            theirs = _full_region(wfull_ref, kind, shard, s2, 1 - c)
            arrivals.append((_remote(mine, mine, ici_send.at[k - 1], ici_recv.at[k - 1], dev),
                             _remote(mine, mine, fwd_send.at[k - 1], fwd_recv.at[k - 1], sibling),
                             _remote(theirs, theirs, fwd_send.at[k - 1], fwd_recv.at[k - 1], sibling)))

        slot = g % 2
        last_tile = i == nt - 1

        def block_copy(src, col, to_slot):
            return pltpu.make_async_copy(src.at[:, pl.ds(pl.multiple_of(col * D, D), D)], w_sc.at[to_slot],
                                         w_sem.at[to_slot])

        @pl.when(jnp.logical_and(g == 0, i == 0))
        def _():
            for cp in sends:
                cp.start()
            local.start()
            for cp in u_copies:
                cp.start()
            cp = block_copy(wsh_ref, 0, 0)
            cp.start()
            cp.wait()

        @pl.when(jnp.logical_and(last_tile, g == 0))
        def _():
            block_copy(wsh_ref, 1, 1).start()

        for k, (arrived, forward, forwarded) in zip((1, 2, 3), arrivals):
            @pl.when(jnp.logical_and(last_tile, g == 2 * k - 1))
            def _():
                arrived.wait_recv()
                forward.start()
                forwarded.wait_recv()
                block_copy(wfull_ref, route_ref[2 + 2 * k], 0).start()

            @pl.when(jnp.logical_and(last_tile, g == 2 * k))
            def _():
                block_copy(wfull_ref, route_ref[3 + 2 * k], 1).start()

        @pl.when(jnp.logical_and(i == 0, g > 0))
        def _():
            block_copy(wfull_ref, 0, slot).wait()

        for t in range(nt):
            @pl.when(jnp.logical_and(g == 0, i == t))
            def _():
                u_copies[t].wait()

        acc = _dot(u_sc[i], w_sc[slot])

        @pl.when(gid < 2)
        def _():
            scale = jnp.where(gid == 1, QK_SCALE, 1.0).astype(F32)
            for h in range(HEADS):
                sl = slice(h * DH, (h + 1) * DH)
                blk = acc[:, sl]
                out = (blk * cos_ref[...] + pltpu.roll(blk, DH // 2, axis=1) * sin_ref[...]) * scale
                proj_ref[:, sl] = out.astype(BF16)

        @pl.when(gid >= 2)
        def _():
            proj_ref[...] = acc.astype(BF16)

        @pl.when(jnp.logical_and(g == 7, i == nt - 1))
        def _():
            local.wait()
            for cp in sends:
                cp.wait_send()
            for _, forward, _ in arrivals:
                forward.wait_send()

    return _hosted_call(
        body,
        name="in_proj",
        grid=(8, nt),
        in_specs=[
            _ANY, _ANY,
            pl.BlockSpec((tm, DH), lambda g, i, rt: (i, 0)),
            pl.BlockSpec((tm, DH), lambda g, i, rt: (i, 0)),
        ],
        out_specs=[pl.BlockSpec((tm, D), lambda g, i, rt: (i, _swap_3_4(rt[2 + g]))), _ANY],
        out_shape=[jax.ShapeDtypeStruct((rows, IN_COLS), BF16), jax.ShapeDtypeStruct((D, IN_COLS), BF16)],
        scratch_shapes=[
            pltpu.VMEM((nt, tm, D), BF16), pltpu.VMEM((2, D, D), BF16),
            pltpu.SemaphoreType.DMA((nt,)), pltpu.SemaphoreType.DMA((2,)), pltpu.SemaphoreType.DMA,
            pltpu.SemaphoreType.DMA((3,)), pltpu.SemaphoreType.DMA((3,)),
            pltpu.SemaphoreType.DMA((3,)), pltpu.SemaphoreType.DMA((3,)),
        ],
        args=(u, w_shard, cos_t, sin_t),
        riders=riders,
        prefetch=route,
        riders_after_body=True,
    )


def _retention_fwd(proj_bf, intra, q_dec, k_dec, c_dec, riders=()):
    rows = proj_bf.shape[0]
    nc = rows // CHUNK

    def body(q_ref, k_ref, v_ref, m_ref, qd_ref, kd_ref, cd_ref, o_ref, sprev_ref, s_sc):
        @pl.when(pl.program_id(0) == 0)
        def _():
            s_sc[...] = jnp.zeros_like(s_sc)

        for h in range(HEADS):
            sl = slice(h * DH, (h + 1) * DH)
            q, k, v = q_ref[:, sl], k_ref[:, sl], v_ref[:, sl]
            state = s_sc[h]
            state_b = state.astype(BF16)
            sprev_ref[0, h] = state_b
            s = _dot_nt(q, k) * m_ref[h]
            inner = _dot(s.astype(BF16), v)
            cross = _dot(q, state_b) * qd_ref[:, sl]
            o_ref[:, sl] = (inner + cross).astype(BF16)
            k_scaled = (k.astype(F32) * kd_ref[:, sl]).astype(BF16)
            s_sc[h] = state * cd_ref[:, sl] + _dot_tn(k_scaled, v)

    chunk_spec = lambda col: pl.BlockSpec((CHUNK, D), lambda c: (c, col))
    const2 = lambda shape: pl.BlockSpec(shape, lambda c: (0, 0))
    return _hosted_call(
        body,
        name="retention_fwd",
        grid=(nc,),
        in_specs=[
            chunk_spec(0), chunk_spec(1), chunk_spec(2),
            pl.BlockSpec((HEADS, CHUNK, CHUNK), lambda c: (0, 0, 0)),
            const2((CHUNK, D)), const2((CHUNK, D)), const2((1, D)),
        ],
        out_specs=[
            pl.BlockSpec((CHUNK, D), lambda c: (c, 0)),
            pl.BlockSpec((1, HEADS, DH, DH), lambda c: (c, 0, 0, 0)),
        ],
        out_shape=[
            jax.ShapeDtypeStruct((rows, D), BF16),
            jax.ShapeDtypeStruct((nc, HEADS, DH, DH), BF16),
        ],
        scratch_shapes=[pltpu.VMEM((HEADS, DH, DH), F32)],
        args=(proj_bf, proj_bf, proj_bf, intra, q_dec, k_dec, c_dec),
        riders=riders,
    )


def _shift_down(x, first8_prev, d):
    rolled = pltpu.roll(x, d, axis=0)
    head = pltpu.roll(jnp.concatenate([first8_prev, x[0:8]], axis=0), d, axis=0)[8:16]
    return rolled, head


def _conv_and_gates(x, prev8, cw_ref, cb_ref, wa_ref, wx_ref, ba_ref, bx_ref, lam_ref, c_sc):
    cw = cw_ref[...]
    conv = cb_ref[...] + cw[3:4] * x
    head = cb_ref[...] + cw[3:4] * x[0:8]
    for d in (1, 2, 3):
        rolled, hd = _shift_down(x, prev8, d)
        conv = conv + cw[3 - d:4 - d] * rolled
        head = head + cw[3 - d:4 - d] * hd
    c_sc[...] = conv
    c_sc[0:8, :] = head
    c = c_sc[...]
    zr, zi = [], []
    for g in range(LRU_BLOCKS):
        sl = slice(g * LRU_BLOCK, (g + 1) * LRU_BLOCK)
        cg = c[:, sl].astype(BF16)
        zr.append(_dot(cg, wa_ref[g]))
        zi.append(_dot(cg, wx_ref[g]))
    r = _sigmoid(jnp.concatenate(zr, axis=1) + ba_ref[...])
    gate_i = _sigmoid(jnp.concatenate(zi, axis=1) + bx_ref[...])
    sp = _softplus(-lam_ref[...])
    log_a = (-LRU_C) * r * sp
    a = jnp.exp(log_a)
    mult = jnp.sqrt(_neg_expm1(2.0 * log_a))
    return c, r, gate_i, a, mult, log_a


def _lru_fwd(proj, conv_w, conv_b, wa, wx, ba, bx, lam, riders=()):
    rows = proj.shape[0]
    TM = _lru_tile(rows)
    nt = rows // TM

    def body(x_ref, cw_ref, cb_ref, wa_ref, wx_ref, ba_ref, bx_ref, lam_ref,
             h_ref, c_ref, r_ref, i_ref, la_ref, mult_ref, prev_sc, carry_sc, c_sc, a_sc, u_sc, h_sc):
        i = pl.program_id(0)

        @pl.when(i == 0)
        def _():
            prev_sc[...] = jnp.zeros_like(prev_sc)
            carry_sc[...] = jnp.zeros_like(carry_sc)

        x = x_ref[...].astype(F32)
        c, r, gate_i, a, mult, log_a = _conv_and_gates(x, prev_sc[...], cw_ref, cb_ref, wa_ref, wx_ref, ba_ref, bx_ref,
                                                       lam_ref, c_sc)
        for ref, val in ((c_ref, c), (r_ref, r), (i_ref, gate_i), (la_ref, log_a), (mult_ref, mult)):
            ref[...] = val.astype(BF16)
        prev_sc[...] = x[TM - 8:TM]
        valid = _row_ids(i, TM, (TM, D)) >= PAD_ROWS
        a_sc[...] = a
        u_sc[...] = jnp.where(valid, mult * gate_i * c, 0.0)
        row8 = lax.broadcasted_iota(jnp.int32, (8, D), 0)

        def group(gi, hprev):
            r0 = pl.multiple_of(gi * 8, 8)
            aa = a_sc[pl.ds(r0, 8), :]
            uu = u_sc[pl.ds(r0, 8), :]
            for d in (1, 2, 4):
                a_sh = jnp.where(row8 >= d, pltpu.roll(aa, d, axis=0), 1.0)
                u_sh = jnp.where(row8 >= d, pltpu.roll(uu, d, axis=0), 0.0)
                uu = uu + aa * u_sh
                aa = aa * a_sh
            hb = aa * hprev + uu
            h_sc[pl.ds(r0, 8), :] = hb
            return hb[7:8, :]

        hlast = lax.fori_loop(0, TM // 8, group, carry_sc[0:1, :])
        carry_sc[0:1, :] = hlast
        h_ref[...] = h_sc[...].astype(BF16)

    vec = pl.BlockSpec((1, D), lambda i: (0, 0))
    wspec = pl.BlockSpec((LRU_BLOCKS, LRU_BLOCK, LRU_BLOCK), lambda i: (0, 0, 0))
    return _hosted_call(
        body,
        name="lru_fwd",
        grid=(nt,),
        in_specs=[
            pl.BlockSpec((TM, D), lambda i: (i, LRU_IN_COL)),
            pl.BlockSpec((4, D), lambda i: (0, 0)),
            vec, wspec, wspec, vec, vec, vec,
        ],
        out_specs=[pl.BlockSpec((TM, D), lambda i: (i, 0)) for _ in range(6)],
        out_shape=[jax.ShapeDtypeStruct((rows, D), BF16) for _ in range(6)],
        scratch_shapes=[
            pltpu.VMEM((8, D), F32), pltpu.VMEM((8, D), F32),
            pltpu.VMEM((TM, D), F32), pltpu.VMEM((TM, D), F32), pltpu.VMEM((TM, D), F32), pltpu.VMEM((TM, D), F32),
        ],
        args=(proj, conv_w, conv_b, wa, wx, ba, bx, lam),
        riders=riders,
    )


def _group_norm(o):
    outs, rstds = [], []
    for h in range(HEADS):
        oh = o[:, h * DH:(h + 1) * DH]
        rs = lax.rsqrt(jnp.mean(oh * oh, axis=-1, keepdims=True) + EPS)
        outs.append(oh * rs)
        rstds.append(rs)
    return jnp.concatenate(outs, axis=1), rstds


def _frame_rows(head_hbm, x_hbm, buf, sems, i, tm, nt):
    slot = i % 2

    @pl.when(i == 0)
    def _():
        first = [pltpu.make_async_copy(head_hbm, buf.at[0, pl.ds(0, FRONT)], sems.at[2]),
                 pltpu.make_async_copy(x_hbm.at[pl.ds(0, tm - FRONT)], buf.at[0, pl.ds(FRONT, tm - FRONT)], sems.at[0])]
        for cp in first:
            cp.start()
        for cp in first:
            cp.wait()

    @pl.when(i + 1 < nt)
    def _():
        start = pl.multiple_of((i + 1) * tm - FRONT, 128)
        pltpu.make_async_copy(x_hbm.at[pl.ds(start, tm)], buf.at[1 - slot], sems.at[1 - slot]).start()

    @pl.when(i > 0)
    def _():
        pltpu.make_async_copy(x_hbm.at[pl.ds(0, tm)], buf.at[slot], sems.at[slot]).wait()

    return slot


def _mix_fwd(head, x2d, o, proj, h_lru, w_br, w_bl, w_o, ffn_norm_w, riders=()):
    rows = o.shape[0]
    tm = TM_MIX_FWD
    assert rows % tm == 0 and tm > FRONT
    nt = rows // tm

    def body(head_hbm, x_hbm, o_ref, gates_ref, hl_ref, wbr_ref, wbl_ref, wo_ref, nw_ref,
             yret_ref, ylru_ref, h1_ref, u2_ref, rstd_ref, h0_sc, h0_sem):
        i = pl.program_id(0)
        slot = _frame_rows(head_hbm, x_hbm, h0_sc, h0_sem, i, tm, nt)
        gate = lambda j: gates_ref[:, j * D:(j + 1) * D].astype(F32)
        on, _ = _group_norm(o_ref[...].astype(F32))
        gret = gate(0)
        a_ret = (gret * _sigmoid(gret) * on).astype(BF16)
        y_ret = _dot(a_ret, wbr_ref[...])
        gl, _ = _gelu_and_grad(gate(1))
        a_lru = (gl * hl_ref[...].astype(F32)).astype(BF16)
        y_lru = _dot(a_lru, wbl_ref[...])
        mixed = (_sigmoid(gate(2)) * y_ret + _sigmoid(gate(3)) * y_lru).astype(BF16)
        delta = _dot(mixed, wo_ref[...])
        yret_ref[...] = y_ret.astype(BF16)
        ylru_ref[...] = y_lru.astype(BF16)
        h1 = h0_sc[slot] + delta
        rs = lax.rsqrt(jnp.mean(h1 * h1, axis=-1, keepdims=True) + EPS)
        h1_ref[...] = h1
        u2_ref[...] = ((h1 * rs) * nw_ref[...]).astype(BF16)
        rstd_ref[...] = rs

    tile = lambda col: pl.BlockSpec((tm, D), lambda i: (i, col))
    wspec = pl.BlockSpec((D, D), lambda i: (0, 0))
    return _hosted_call(
        body,
        name="mix_fwd",
        grid=(nt,),
        in_specs=[
            _ANY, _ANY,
            tile(0), pl.BlockSpec((tm, 4 * D), lambda i: (i, GATES_COL)), tile(0),
            wspec, wspec, wspec,
            pl.BlockSpec((1, D), lambda i: (0, 0)),
        ],
        out_specs=[tile(0), tile(0), tile(0), tile(0), pl.BlockSpec((tm, 1), lambda i: (i, 0))],
        out_shape=[
            jax.ShapeDtypeStruct((rows, D), BF16), jax.ShapeDtypeStruct((rows, D), BF16),
            jax.ShapeDtypeStruct((rows, D), F32), jax.ShapeDtypeStruct((rows, D), BF16),
            jax.ShapeDtypeStruct((rows, 1), F32),
        ],
        scratch_shapes=[pltpu.VMEM((2, tm, D), F32), pltpu.SemaphoreType.DMA((3,))],
        args=(head, x2d, o, proj, h_lru, w_br, w_bl, w_o, ffn_norm_w),
        riders=riders,
    )


def _ffn_fwd_gate(u2, w_ffn_in, riders=()):
    rows = u2.shape[0]
    tm = _heavy_tile(rows)
    nb = FFN // FFN_BLOCK
    hid = lambda: pl.BlockSpec((tm, FFN_BLOCK), lambda i, j: (i, j))

    def body(u2_ref, w_hbm, silu_ref, dsilu_ref, act_ref, w_sc, w_sem):
        j = pl.program_id(1)

        @pl.when(jnp.logical_and(pl.program_id(0) == 0, j == 0))
        def _():
            cp = pltpu.make_async_copy(w_hbm, w_sc, w_sem)
            cp.start()
            cp.wait()

        u2 = u2_ref[...]
        for a, b in ((0, FFN_SUB), (FFN_SUB, FFN_BLOCK)):
            col = pl.multiple_of(j * FFN_BLOCK + a, 128)
            g = _dot(u2, w_sc[:, pl.ds(col, b - a)])
            up = _dot(u2, w_sc[:, pl.ds(pl.multiple_of(FFN + col, 128), b - a)])
            sg = _sigmoid(g)
            silu = g * sg
            silu_ref[:, a:b] = silu.astype(BF16)
            dsilu_ref[:, a:b] = (up * (sg * (1.0 + g * (1.0 - sg)))).astype(BF16)
            act_ref[:, a:b] = (silu * up).astype(BF16)

    return _hosted_call(
        body,
        name="ffn_fwd_gate",
        grid=(rows // tm, nb),
        in_specs=[pl.BlockSpec((tm, D), lambda i, j: (i, 0)), _ANY],
        out_specs=[hid(), hid(), hid()],
        out_shape=[jax.ShapeDtypeStruct((rows, FFN), BF16)] * 3,
        scratch_shapes=[pltpu.VMEM((D, 2 * FFN), BF16), pltpu.SemaphoreType.DMA],
        args=(u2, w_ffn_in),
        riders=riders,
    )


def _ffn_out_loss(act, h1, w_ffn_out, target, final_norm_w):
    rows = act.shape[0]
    tm = _heavy_tile(rows)
    nt = rows // tm

    def body(act_ref, h1_ref, wo_ref, fnw_ref, tgt_hbm, dh2_ref, stats_ref, tgt_sc, tgt_sem):
        i = pl.program_id(0)
        slot = i % 2

        @pl.when(i == 0)
        def _():
            stats_ref[...] = jnp.zeros_like(stats_ref)
            tgt_sc[0, 0:FRONT, :] = jnp.zeros((FRONT, D), F32)
            cp = pltpu.make_async_copy(tgt_hbm.at[pl.ds(0, tm - FRONT)], tgt_sc.at[0, pl.ds(FRONT, tm - FRONT)], tgt_sem.at[0])
            cp.start()
            cp.wait()

        @pl.when(i + 1 < nt)
        def _():
            start = pl.multiple_of((i + 1) * tm - FRONT, FRONT)
            pltpu.make_async_copy(tgt_hbm.at[pl.ds(start, tm)], tgt_sc.at[1 - slot], tgt_sem.at[1 - slot]).start()

        @pl.when(i > 0)
        def _():
            pltpu.make_async_copy(tgt_hbm.at[pl.ds(0, tm)], tgt_sc.at[slot], tgt_sem.at[slot]).wait()

        h2 = h1_ref[...] + _dot(act_ref[...], wo_ref[...])
        rs = lax.rsqrt(jnp.mean(h2 * h2, axis=-1, keepdims=True) + EPS)
        n = h2 * rs
        fnw = fnw_ref[...]
        valid = _row_ids(i, tm, (tm, D)) >= FRONT
        diff = jnp.where(valid, n * fnw - tgt_sc[slot], 0.0)
        dy = diff * (1.0 / D)
        stats_ref[0:1, :] += (0.5 / D) * jnp.sum(diff * diff, axis=0, keepdims=True)
        stats_ref[1:2, :] += jnp.sum(dy * n, axis=0, keepdims=True)
        dn = dy * fnw
        dh2_ref[...] = rs * (dn - n * jnp.mean(dn * n, axis=-1, keepdims=True))

    return pl.pallas_call(
        body,
        name="ffn_out_loss",
        grid=(nt,),
        in_specs=[
            pl.BlockSpec((tm, FFN), lambda i: (i, 0)),
            pl.BlockSpec((tm, D), lambda i: (i, 0)),
            pl.BlockSpec((FFN, D), lambda i: (0, 0)),
            pl.BlockSpec((1, D), lambda i: (0, 0)),
            _ANY,
        ],
        out_specs=[pl.BlockSpec((tm, D), lambda i: (i, 0)), pl.BlockSpec((8, D), lambda i: (0, 0))],
        out_shape=[jax.ShapeDtypeStruct((rows, D), F32), jax.ShapeDtypeStruct((8, D), F32)],
        scratch_shapes=[pltpu.VMEM((2, tm, D), F32), pltpu.SemaphoreType.DMA((2,))],
        compiler_params=_params(("arbitrary",)),
    )(act, h1, w_ffn_out, final_norm_w, target)


def _ffn_bwd(dh2, silu, dsilu, h1, rstd2, w_ffn_in, w_ffn_out, ffn_norm_w):
    rows = dh2.shape[0]
    tm = _heavy_tile(rows)
    nb = FFN // FFN_BLOCK
    blk = lambda: pl.BlockSpec((tm, FFN_BLOCK), lambda i, j: (i, j))

    def gate_body(dh2_ref, silu_ref, dsilu_ref, wo_hbm, dg_ref, dup_ref, dh2b_ref, wo_sc, wo_sem):
        j = pl.program_id(1)

        @pl.when(jnp.logical_and(pl.program_id(0) == 0, j == 0))
        def _():
            cp = pltpu.make_async_copy(wo_hbm, wo_sc, wo_sem)
            cp.start()
            cp.wait()

        @pl.when(j == 0)
        def _():
            dh2b_ref[...] = dh2_ref[...].astype(BF16)

        dact = _dot_nt(dh2b_ref[...], wo_sc[pl.ds(pl.multiple_of(j * FFN_BLOCK, 128), FFN_BLOCK), :])
        dup_ref[...] = (dact * silu_ref[...].astype(F32)).astype(BF16)
        dg_ref[...] = (dact * dsilu_ref[...].astype(F32)).astype(BF16)

    dg, dup, dh2b = pl.pallas_call(
        gate_body,
        name="ffn_bwd_gate",
        grid=(rows // tm, nb),
        in_specs=[pl.BlockSpec((tm, D), lambda i, j: (i, 0)), blk(), blk(), _ANY],
        out_specs=[blk(), blk(), pl.BlockSpec((tm, D), lambda i, j: (i, 0))],
        out_shape=[jax.ShapeDtypeStruct((rows, FFN), BF16)] * 2 + [jax.ShapeDtypeStruct((rows, D), BF16)],
        scratch_shapes=[pltpu.VMEM((FFN, D), BF16), pltpu.SemaphoreType.DMA],
        compiler_params=_params(("arbitrary", "arbitrary")),
    )(dh2, silu, dsilu, w_ffn_out)

    def body(dg_ref, dup_ref, dh2_ref, h1_ref, rstd_ref, w_hbm, nw_ref, dh1_ref, stats_ref, w_sc, w_sem):
        @pl.when(pl.program_id(0) == 0)
        def _():
            stats_ref[...] = jnp.zeros_like(stats_ref)
            cp = pltpu.make_async_copy(w_hbm, w_sc, w_sem)
            cp.start()
            cp.wait()

        du = _dot_nt(dg_ref[...], w_sc[:, 0:FFN]) + _dot_nt(dup_ref[...], w_sc[:, FFN:2 * FFN])
        rs = rstd_ref[...]
        n = h1_ref[...] * rs
        stats_ref[0:1, :] += jnp.sum(du * n, axis=0, keepdims=True)
        dn = du * nw_ref[...]
        dh1_ref[...] = dh2_ref[...] + rs * (dn - n * jnp.mean(dn * n, axis=-1, keepdims=True))

    row = lambda width: pl.BlockSpec((tm, width), lambda i: (i, 0))
    dh1, stats = pl.pallas_call(
        body,
        name="ffn_bwd_in",
        grid=(rows // tm,),
        in_specs=[row(FFN), row(FFN), row(D), row(D), row(1), _ANY, pl.BlockSpec((1, D), lambda i: (0, 0))],
        out_specs=[row(D), pl.BlockSpec((8, D), lambda i: (0, 0))],
        out_shape=[jax.ShapeDtypeStruct((rows, D), F32), jax.ShapeDtypeStruct((8, D), F32)],
        scratch_shapes=[pltpu.VMEM((D, 2 * FFN), BF16), pltpu.SemaphoreType.DMA],
        compiler_params=_params(("arbitrary",)),
    )(dg, dup, dh2, h1, rstd2, w_ffn_in, ffn_norm_w)
    return dg, dup, dh2b, dh1, stats


def _mix_bwd(dh1, o, proj, h_lru, y_ret, y_lru, w_br, w_bl, w_o, riders=()):
    rows = dh1.shape[0]
    tm = TM_MIX_BWD
    nt = rows // tm

    def body(dh1_ref, o_ref, gates_ref, hl_ref, yret_ref, ylru_ref, wbr_ref, wbl_ref, wo_ref,
             dproj_ref, do_ref, dhl_ref, mixed_ref, aret_ref, alru_ref, dyret_ref, dylru_ref):
        gate = lambda j: gates_ref[:, j * D:(j + 1) * D].astype(F32)
        dmixed = _dot_nt(dh1_ref[...].astype(BF16), wo_ref[...])
        y_ret, y_lru = yret_ref[...].astype(F32), ylru_ref[...].astype(F32)
        sa, sb = _sigmoid(gate(2)), _sigmoid(gate(3))
        mixed_ref[...] = (sa * y_ret + sb * y_lru).astype(BF16)
        dga = dmixed * y_ret * sa * (1.0 - sa)
        dgb = dmixed * y_lru * sb * (1.0 - sb)
        dy_ret = (dmixed * sa).astype(BF16)
        dy_lru = (dmixed * sb).astype(BF16)
        dyret_ref[...] = dy_ret
        dylru_ref[...] = dy_lru
        da_ret = _dot_nt(dy_ret, wbr_ref[...])
        da_lru = _dot_nt(dy_lru, wbl_ref[...])

        gret = gate(0)
        sg = _sigmoid(gret)
        silu = gret * sg
        on, rstds = _group_norm(o_ref[...].astype(F32))
        aret_ref[...] = (silu * on).astype(BF16)
        dgret = da_ret * on * (sg * (1.0 + gret * (1.0 - sg)))
        don = da_ret * silu
        for h in range(HEADS):
            sl = slice(h * DH, (h + 1) * DH)
            onh, donh = on[:, sl], don[:, sl]
            do_ref[:, sl] = (rstds[h] * (donh - onh * jnp.mean(donh * onh, axis=-1, keepdims=True))).astype(BF16)

        gl, gl_grad = _gelu_and_grad(gate(1))
        hl = hl_ref[...].astype(F32)
        alru_ref[...] = (gl * hl).astype(BF16)
        dlgate = da_lru * hl * gl_grad
        dhl_ref[...] = (da_lru * gl).astype(BF16)

        for j, val in enumerate((dgret, dlgate, dga, dgb)):
            dproj_ref[:, j * D:(j + 1) * D] = val.astype(BF16)

    tile = lambda col: pl.BlockSpec((tm, D), lambda i: (i, col))
    gates = pl.BlockSpec((tm, 4 * D), lambda i: (i, GATES_COL))
    wspec = pl.BlockSpec((D, D), lambda i: (0, 0))
    bf = lambda: jax.ShapeDtypeStruct((rows, D), BF16)
    return _hosted_call(
        body,
        name="mix_bwd",
        grid=(nt,),
        in_specs=[tile(0), tile(0), gates, tile(0), tile(0), tile(0), wspec, wspec, wspec],
        out_specs=[pl.BlockSpec((tm, 4 * D), lambda i: (i, GATES_COL))] + [tile(0)] * 7,
        out_shape=[jax.ShapeDtypeStruct((rows, IN_COLS), BF16)] + [bf() for _ in range(7)],
        scratch_shapes=[],
        args=(dh1, o, proj, h_lru, y_ret, y_lru, w_br, w_bl, w_o),
        riders=riders,
    )


def _retention_bwd(dproj, proj_bf, do, sprev, intra, intra_t, q_dec, k_dec, c_dec, cos_t, sin_t, riders=()):
    rows = proj_bf.shape[0]
    nc = rows // CHUNK

    def body(dproj_in_ref, q_ref, k_ref, v_ref, do_ref, sprev_ref, m_ref, mt_ref, qd_ref, kd_ref, cd_ref, cos_ref, sin_ref,
             dproj_ref, ds_sc):
        @pl.when(pl.program_id(0) == 0)
        def _():
            ds_sc[...] = jnp.zeros_like(ds_sc)

        cos, sin = cos_ref[...], sin_ref[...]

        def unrotate(dy):
            return dy * cos - pltpu.roll(dy, DH // 2, axis=1) * sin

        for h in range(HEADS):
            sl = slice(h * DH, (h + 1) * DH)
            q, k, v = q_ref[:, sl], k_ref[:, sl], v_ref[:, sl]
            do = do_ref[:, sl]
            dob = do.astype(BF16)
            doq = (do * qd_ref[:, sl]).astype(BF16)
            state_prev = sprev_ref[0, h]
            dstate = ds_sc[h]
            dstate_b = dstate.astype(BF16)
            s_t = (_dot_nt(k, q) * mt_ref[h]).astype(BF16)
            ds_t = (_dot_nt(v, dob) * mt_ref[h]).astype(BF16)
            ds = (_dot_nt(dob, v) * m_ref[h]).astype(BF16)
            kd = kd_ref[:, sl]
            dq = _dot(ds, k) + _dot_nt(doq, state_prev)
            dk = _dot(ds_t, q) + _dot_nt(v, dstate_b) * kd
            k_scaled = (k.astype(F32) * kd).astype(BF16)
            dv = _dot(s_t, dob) + _dot(k_scaled, dstate_b)
            ds_sc[h] = dstate * cd_ref[:, sl] + _dot_tn(q, doq)
            dproj_ref[:, sl] = unrotate(dq).astype(BF16)
            dproj_ref[:, D + h * DH:D + (h + 1) * DH] = (unrotate(dk) * QK_SCALE).astype(BF16)
            dproj_ref[:, 2 * D + h * DH:2 * D + (h + 1) * DH] = dv.astype(BF16)

    rev = lambda c: nc - 1 - c
    chunk_spec = lambda col: pl.BlockSpec((CHUNK, D), lambda c: (rev(c), col))
    const2 = lambda shape: pl.BlockSpec(shape, lambda c: (0, 0))
    const3 = pl.BlockSpec((HEADS, CHUNK, CHUNK), lambda c: (0, 0, 0))
    return _hosted_call(
        body,
        name="retention_bwd",
        grid=(nc,),
        in_specs=[
            pl.BlockSpec(memory_space=pl.ANY),
            chunk_spec(0), chunk_spec(1), chunk_spec(2), chunk_spec(0),
            pl.BlockSpec((1, HEADS, DH, DH), lambda c: (rev(c), 0, 0, 0)),
            const3, const3,
            const2((CHUNK, D)), const2((CHUNK, D)), const2((1, D)),
            pl.BlockSpec((CHUNK, DH), lambda c: (rev(c), 0)),
            pl.BlockSpec((CHUNK, DH), lambda c: (rev(c), 0)),
        ],
        out_specs=[pl.BlockSpec((CHUNK, 3 * D), lambda c: (rev(c), 0))],
        out_shape=[jax.ShapeDtypeStruct(dproj.shape, BF16)],
        aliases={0: 0},
        scratch_shapes=[pltpu.VMEM((HEADS, DH, DH), F32)],
        args=(dproj, proj_bf, proj_bf, proj_bf, do, sprev, intra, intra_t, q_dec, k_dec, c_dec, cos_t, sin_t),
        riders=riders,
    )


def _lru_bwd(dproj, proj, saved, dhl, conv_w, wa, wx, lam, riders=()):
    rows = proj.shape[0]
    TM = _lru_tile(rows)
    nt = rows // TM
    per8 = TM // 8

    def body(dproj_in_ref, x_ref, xprev_ref, h_ref, hprev_ref, c_ref, r_ref, i_ref, la_ref, mult_ref, dhl_ref,
             cw_ref, wa_ref, wx_ref, lam_ref,
             dproj_ref, dwa_ref, dwx_ref, stats_ref, anext_sc, dhnext_sc, dcnext_sc, c_sc, b_sc, dh_sc):
        step = pl.program_id(0)
        i = nt - 1 - step

        @pl.when(step == 0)
        def _():
            anext_sc[...] = jnp.zeros_like(anext_sc)
            dhnext_sc[...] = jnp.zeros_like(dhnext_sc)
            dcnext_sc[...] = jnp.zeros_like(dcnext_sc)
            dwa_ref[...] = jnp.zeros_like(dwa_ref)
            dwx_ref[...] = jnp.zeros_like(dwx_ref)
            stats_ref[...] = jnp.zeros_like(stats_ref)

        first = i == 0
        x = x_ref[...].astype(F32)
        prev8 = jnp.where(first, 0.0, xprev_ref[8:16, :].astype(F32))
        c_b = c_ref[...]
        c, r, gate_i, mult = (ref[...].astype(F32) for ref in (c_ref, r_ref, i_ref, mult_ref))
        a = jnp.exp(la_ref[...].astype(F32))
        sp = _softplus(-lam_ref[...])
        dh_sc[...] = dhl_ref[...].astype(F32)

        b_sc[...] = pltpu.roll(a, TM - 1, axis=0)
        b_sc[TM - 1:TM, :] = anext_sc[0:1, :]
        anext_sc[0:1, :] = a[0:1, :]
        row8 = lax.broadcasted_iota(jnp.int32, (8, D), 0)

        def group(gi, dhnext):
            r0 = pl.multiple_of((per8 - 1 - gi) * 8, 8)
            bb = b_sc[pl.ds(r0, 8), :]
            uu = dh_sc[pl.ds(r0, 8), :]
            for d in (1, 2, 4):
                b_sh = jnp.where(row8 < 8 - d, pltpu.roll(bb, 8 - d, axis=0), 1.0)
                u_sh = jnp.where(row8 < 8 - d, pltpu.roll(uu, 8 - d, axis=0), 0.0)
                uu = uu + bb * u_sh
                bb = bb * b_sh
            dhb = bb * dhnext + uu
            dh_sc[pl.ds(r0, 8), :] = dhb
            return dhb[0:1, :]

        dhfirst = lax.fori_loop(0, per8, group, dhnext_sc[0:1, :])
        dhnext_sc[0:1, :] = dhfirst
        dh = dh_sc[...]

        h = h_ref[...].astype(F32)
        hprev8 = jnp.where(first, 0.0, hprev_ref[8:16, :].astype(F32))
        h_dn, h_head = _shift_down(h, hprev8, 1)
        c_sc[...] = h_dn
        c_sc[0:8, :] = h_head
        h_before = c_sc[...]

        valid = _row_ids(i, TM, (TM, D)) >= PAD_ROWS
        da = dh * h_before
        du = jnp.where(valid, dh, 0.0)
        dmult = du * gate_i * c
        dgate_i = du * mult * c
        dc = du * mult * gate_i
        dla = da * a - dmult * (a * a) / mult
        dla = jnp.where(valid, dla, 0.0)
        dr = dla * ((-LRU_C) * sp)
        dzr = dr * r * (1.0 - r)
        dzi = dgate_i * gate_i * (1.0 - gate_i)
        stats_ref[1:2, :] += jnp.sum(dzr, axis=0, keepdims=True)
        stats_ref[2:3, :] += jnp.sum(dzi, axis=0, keepdims=True)
        stats_ref[3:4, :] += jnp.sum(dla * ((-LRU_C) * r), axis=0, keepdims=True)
        dc_gate = []
        for g in range(LRU_BLOCKS):
            sl = slice(g * LRU_BLOCK, (g + 1) * LRU_BLOCK)
            cg = c_b[:, sl]
            dzr_g = dzr[:, sl].astype(BF16)
            dzi_g = dzi[:, sl].astype(BF16)
            dc_gate.append(_dot_nt(dzr_g, wa_ref[g]) + _dot_nt(dzi_g, wx_ref[g]))
            dwa_ref[g] += _dot_tn(cg, dzr_g)
            dwx_ref[g] += _dot_tn(cg, dzi_g)
        dc = dc + jnp.concatenate(dc_gate, axis=1)

        cw = cw_ref[...]
        stats_ref[0:1, :] += jnp.sum(dc, axis=0, keepdims=True)
        stats_ref[7:8, :] += jnp.sum(dc * x, axis=0, keepdims=True)
        dx = cw[3:4] * dc
        tail_src = jnp.concatenate([dc[TM - 8:TM], dcnext_sc[...]], axis=0)
        dx_tail = cw[3:4] * dc[TM - 8:TM]
        for d in (1, 2, 3):
            dx = dx + cw[3 - d:4 - d] * pltpu.roll(dc, TM - d, axis=0)
            dx_tail = dx_tail + cw[3 - d:4 - d] * pltpu.roll(tail_src, 16 - d, axis=0)[0:8]
            rolled, hd = _shift_down(x, prev8, d)
            b_sc[...] = rolled
            b_sc[0:8, :] = hd
            stats_ref[7 - d:8 - d, :] += jnp.sum(dc * b_sc[...], axis=0, keepdims=True)
        dcnext_sc[...] = dc[0:8]
        dproj_ref[...] = dx.astype(BF16)
        dproj_ref[TM - 8:TM, :] = dx_tail.astype(BF16)

    rev = lambda s: nt - 1 - s
    vec = pl.BlockSpec((1, D), lambda s: (0, 0))
    wspec = pl.BlockSpec((LRU_BLOCKS, LRU_BLOCK, LRU_BLOCK), lambda s: (0, 0, 0))
    prev16 = lambda col: pl.BlockSpec((16, D), lambda s: (jnp.maximum(rev(s) * (TM // 16) - 1, 0), col))
    tile = lambda: pl.BlockSpec((TM, D), lambda s: (rev(s), 0))
    h_lru, c_sv, r_sv, i_sv, la_sv, mult_sv = saved
    return _hosted_call(
        body,
        name="lru_bwd",
        grid=(nt,),
        in_specs=[
            pl.BlockSpec(memory_space=pl.ANY),
            pl.BlockSpec((TM, D), lambda s: (rev(s), LRU_IN_COL)), prev16(LRU_IN_COL),
            tile(), prev16(0),
            tile(), tile(), tile(), tile(), tile(), tile(),
            pl.BlockSpec((4, D), lambda s: (0, 0)),
            wspec, wspec, vec,
        ],
        out_specs=[
            pl.BlockSpec((TM, D), lambda s: (rev(s), LRU_IN_COL)),
            wspec, wspec,
            pl.BlockSpec((8, D), lambda s: (0, 0)),
        ],
        out_shape=[
            jax.ShapeDtypeStruct(dproj.shape, BF16),
            jax.ShapeDtypeStruct((LRU_BLOCKS, LRU_BLOCK, LRU_BLOCK), F32),
            jax.ShapeDtypeStruct((LRU_BLOCKS, LRU_BLOCK, LRU_BLOCK), F32),
            jax.ShapeDtypeStruct((8, D), F32),
        ],
        aliases={0: 0},
        scratch_shapes=[
            pltpu.VMEM((8, D), F32), pltpu.VMEM((8, D), F32), pltpu.VMEM((8, D), F32),
            pltpu.VMEM((TM, D), F32), pltpu.VMEM((TM, D), F32), pltpu.VMEM((TM, D), F32),
        ],
        args=(dproj, proj, proj, h_lru, h_lru, c_sv, r_sv, i_sv, la_sv, mult_sv, dhl, conv_w, wa, wx, lam),
        riders=riders,
    )


def _in_proj_bwd(dproj, w_in, part, prev=None, riders=()):
    rows = dproj.shape[0]
    tm = _heavy_tile(rows)
    nt = rows // tm
    first = 0 if part == 0 else (nt + 1) // 2
    count = (nt + 1) // 2 if part == 0 else nt - first

    def body(*refs):
        dproj_ref, w_hbm, du_ref, w_sc, w_sem = refs[-5:]

        @pl.when(pl.program_id(0) == 0)
        def _():
            moves = [((0, 3), 0), ((4, 1), 3), ((3, 1), 4), ((5, 3), 5)]
            copies = [pltpu.make_async_copy(w_hbm.at[:, pl.ds(src * D, n * D)], w_sc.at[:, pl.ds(dst * D, n * D)], w_sem.at[q])
                      for q, ((src, n), dst) in enumerate(moves)]
            for cp in copies:
                cp.start()
            for cp in copies:
                cp.wait()

        du_ref[...] = _dot_nt(dproj_ref[...], w_sc[...])

    in_specs = [pl.BlockSpec((tm, IN_COLS), lambda i: (first + i, 0)), _ANY]
    args = (dproj, w_in)
    if prev is not None:
        in_specs = [_ANY] + in_specs
        args = (prev,) + args
    return _hosted_call(
        body,
        name="in_proj_bwd_%d" % part,
        grid=(count,),
        in_specs=in_specs,
        out_specs=[pl.BlockSpec((tm, D), lambda i: (first + i, 0))],
        out_shape=[jax.ShapeDtypeStruct((rows, D), F32)],
        scratch_shapes=[pltpu.VMEM((D, IN_COLS), BF16), pltpu.SemaphoreType.DMA((4,))],
        aliases={0: 0} if prev is not None else None,
        args=args,
        riders=riders,
    )


def _norm1_bwd(du, dh1, head, x2d, rstd1, norm_w, riders=()):
    rows = du.shape[0]

    def body(du_ref, dh1_ref, head_ref, x_ref, rstd_ref, nw_ref, gx_ref, ghead_ref, stats_ref):
        i = pl.program_id(0)

        @pl.when(i == 0)
        def _():
            stats_ref[...] = jnp.zeros_like(stats_ref)

        def finish(h0, out_ref):
            du = du_ref[...]
            rs = rstd_ref[...]
            n = h0 * rs
            stats_ref[0:1, :] += jnp.sum(du * n, axis=0, keepdims=True)
            dn = du * nw_ref[...]
            out_ref[...] = dh1_ref[...] + rs * (dn - n * jnp.mean(dn * n, axis=-1, keepdims=True))

        @pl.when(i == 0)
        def _():
            finish(head_ref[...], ghead_ref)

        @pl.when(i > 0)
        def _():
            finish(x_ref[...], gx_ref)

    tile = pl.BlockSpec((TM, D), lambda i: (i, 0))
    return _hosted_call(
        body,
        name="norm1_bwd",
        grid=(rows // TM,),
        in_specs=[
            tile, tile,
            pl.BlockSpec((FRONT, D), lambda i: (0, 0)),
            pl.BlockSpec((TM, D), lambda i: (jnp.maximum(i - 1, 0), 0)),
            pl.BlockSpec((TM, 1), lambda i: (i, 0)),
            pl.BlockSpec((1, D), lambda i: (0, 0)),
        ],
        out_specs=[
            pl.BlockSpec((TM, D), lambda i: (jnp.maximum(i - 1, 0), 0)),
            pl.BlockSpec((FRONT, D), lambda i: (0, 0)),
            pl.BlockSpec((8, D), lambda i: (0, 0)),
        ],
        out_shape=[
            jax.ShapeDtypeStruct(x2d.shape, F32),
            jax.ShapeDtypeStruct((FRONT, D), F32),
            jax.ShapeDtypeStruct((8, D), F32),
        ],
        scratch_shapes=[],
        args=(du, dh1, head, x2d, rstd1, norm_w),
        riders=riders,
    )


def _matmul_tn(name, x, dy, out_cols, col0=0, prev=None, k_block=None, n_block=None, riders=(), col_map=None):
    col_map = col_map or (lambda n: n)
    rows, kdim = x.shape
    ndim = dy.shape[1]
    kb = k_block or kdim
    nb = n_block or ndim
    step_bytes = lambda t: 2 * t * (kb * x.dtype.itemsize + nb * dy.dtype.itemsize) + 2 * kb * nb * 4
    tr = next(t for t in (2816, 1408, TM_HEAVY, TM) if rows % t == 0 and (t == TM or step_bytes(t) <= TN_VMEM_BUDGET))
    nr, nk, nn = rows // tr, kdim // kb, ndim // nb
    cb0 = col0 // nb

    def body(*refs):
        x_ref, dy_ref, out_ref = refs[-3], refs[-2], refs[-1]
        part = _dot_tn(x_ref[...].astype(BF16), dy_ref[...].astype(BF16))

        @pl.when(pl.program_id(2) == 0)
        def _():
            out_ref[...] = part

        @pl.when(pl.program_id(2) > 0)
        def _():
            out_ref[...] += part

    in_specs = [
        pl.BlockSpec((tr, kb), lambda n, k, r: (r, k)),
        pl.BlockSpec((tr, nb), lambda n, k, r: (r, n)),
    ]
    args = [x, dy]
    aliases = {}
    if prev is not None:
        in_specs = [pl.BlockSpec(memory_space=pl.ANY)] + in_specs
        args = [prev] + args
        aliases = {0: 0}
    (out,), rider_outs = _hosted_call(
        body,
        name=name,
        grid=(nn, nk, nr),
        in_specs=in_specs,
        out_specs=[pl.BlockSpec((kb, nb), lambda n, k, r: (k, cb0 + col_map(n)))],
        out_shape=[jax.ShapeDtypeStruct((kdim, out_cols), F32)],
        scratch_shapes=[],
        aliases=aliases,
        args=args,
        riders=riders,
    )
    return (out, rider_outs) if riders else out


def _local_step(x2d, target, w, plan):
    rows = FRONT + x2d.shape[0]
    head = jnp.concatenate([jnp.zeros((PAD_ROWS, D), F32), w["meta_tokens"]], axis=0)
    cos_t, sin_t = _rope_tables(rows)
    intra, intra_t, q_dec, k_dec, c_dec = _decay_tables()
    grads = {}

    def hosted(host, fn, *args, **kwargs):
        outs, rider_outs = fn(*args, riders=plan.riders(host, w, grads), **kwargs)
        plan.after(host, rider_outs, w, grads)
        return outs

    (u1, rstd1), _ = _norm1(head, x2d, w["mix_norm_w"])
    proj, w["w_in"] = hosted("in_proj", _in_proj, u1, w["w_in_shard"], w["route"], cos_t, sin_t)
    o, sprev = hosted("retention_fwd", _retention_fwd, proj, intra, q_dec, k_dec, c_dec)
    lru_args = (w["conv_w"], w["conv_b"], w["lru_wa"], w["lru_wx"], w["lru_ba"], w["lru_bx"], w["lru_lambda"])
    lru_saved = hosted("lru_fwd", _lru_fwd, proj, *lru_args)
    h_lru = lru_saved[0]
    y_ret, y_lru, h1, u2, rstd2 = hosted("mix_fwd", _mix_fwd, head, x2d, o, proj, h_lru, w["w_branch_ret"],
                                         w["w_branch_lru"], w["w_out"], w["ffn_norm_w"])
    silu, dsilu, act = hosted("ffn_fwd_gate", _ffn_fwd_gate, u2, w["w_ffn_in"])
    dh2, stats_loss = _ffn_out_loss(act, h1, w["w_ffn_out"], target, w["final_norm_w"])

    dg, dup, dh2b, dh1, stats_ffn = _ffn_bwd(dh2, silu, dsilu, h1, rstd2, w["w_ffn_in"], w["w_ffn_out"], w["ffn_norm_w"])
    grads["w_ffn_in"] = _matmul_tn("dw_ffn_up", u2, dup, 2 * FFN, col0=FFN, n_block=FFN_HALF,
                                   prev=_matmul_tn("dw_ffn_gate", u2, dg, 2 * FFN, n_block=FFN_HALF))
    grads["w_ffn_out"] = _matmul_tn("dw_ffn_out", act, dh2b, D, k_block=FFN_HALF)
    (dproj, do, dhl, mixed, a_ret, a_lru, dy_ret, dy_lru) = hosted(
        "mix_bwd", _mix_bwd, dh1, o, proj, h_lru, y_ret, y_lru, w["w_branch_ret"], w["w_branch_lru"], w["w_out"])
    grads["w_out"] = _matmul_tn("dw_out", mixed, dh1, D)
    grads["w_branch_ret"] = _matmul_tn("dw_branch_ret", a_ret, dy_ret, D)
    grads["w_branch_lru"] = _matmul_tn("dw_branch_lru", a_lru, dy_lru, D)
    (dproj,) = hosted("retention_bwd", _retention_bwd, dproj, proj, do, sprev, intra, intra_t, q_dec, k_dec, c_dec,
                      cos_t, sin_t)
    dproj, grads["lru_wa"], grads["lru_wx"], stats_lru = hosted(
        "lru_bwd", _lru_bwd, dproj, proj, lru_saved, dhl, w["conv_w"], w["lru_wa"], w["lru_wx"], w["lru_lambda"])
    grads["w_in"], rider_outs = _matmul_tn("dw_in", u1, dproj, IN_COLS, n_block=D, col_map=_swap_3_4,
                                           riders=plan.riders("dw_in", w, grads))
    plan.after("dw_in", rider_outs, w, grads)
    (du1,) = hosted("in_proj_bwd_0", _in_proj_bwd, dproj, w["w_in"], 0)
    (du1,) = hosted("in_proj_bwd_1", _in_proj_bwd, dproj, w["w_in"], 1, du1)
    (grad_x, grad_head, stats_in), _ = _norm1_bwd(du1, dh1, head, x2d, rstd1, w["mix_norm_w"])
    return grad_x, grad_head, grads, [stats_loss, stats_ffn, stats_in, stats_lru]


BIG_PIECES = {
    "w_in": ("col", (D, 2 * D)),
    "w_ffn_in": ("col", (D, FFN_HALF)),
    "w_ffn_out": ("row", (FFN // 4, D)),
    "w_branch_ret": ("row", (D // 4, D)),
    "w_branch_lru": ("row", (D // 4, D)),
    "w_out": ("row", (D // 4, D)),
    "lru_wa": ("lru", (LRU_BLOCKS, LRU_BLOCK // 4, LRU_BLOCK)),
    "lru_wx": ("lru", (LRU_BLOCKS, LRU_BLOCK // 4, LRU_BLOCK)),
}
SMALL_PIECES = {"meta_tokens": ("col", (N_META, D // 4)), "conv_w": ("col", (4, D // 4))}


def _full_shape(kind, shard):
    if kind == "col":
        return (shard[0], 4 * shard[1])
    if kind == "row":
        return (4 * shard[0], shard[1])
    return (shard[0], 4 * shard[1], shard[2])


def _half_shape(kind, shard):
    return (shard[0] // 2,) + tuple(shard[1:])


def _aligned(start, multiple):
    return start if isinstance(start, int) else pl.multiple_of(start, multiple)


def _lead(h, size):
    if h is None:
        return pl.ds(0, size)
    return pl.ds(_aligned(h * (size // 2), size // 2), size // 2)


def _full_region(ref, kind, shard, s, h):
    if kind == "col":
        return ref.at[_lead(h, shard[0]), pl.ds(_aligned(s * shard[1], shard[1]), shard[1])]
    if kind == "row":
        size = shard[0] if h is None else shard[0] // 2
        start = s * shard[0] + (0 if h is None else h * (shard[0] // 2))
        return ref.at[pl.ds(_aligned(start, 16), size), :]
    return ref.at[_lead(h, shard[0]), pl.ds(_aligned(s * shard[1], shard[1]), shard[1]), :]


def _shard_region(ref, shard, h):
    return ref.at[_lead(h, shard[0])]


def _place():
    x, y, c = lax.axis_index("x"), lax.axis_index("y"), lax.axis_index("c")
    return x, y, c, 2 * x + y


def _other_chip(s, c, k):
    s2 = jnp.bitwise_xor(s, k)
    return s2, (s2 // 2, s2 % 2, c)


def _remote(src, dst, send_sem, recv_sem, dev):
    return pltpu.make_async_remote_copy(src_ref=src, dst_ref=dst, send_sem=send_sem, recv_sem=recv_sem,
                                        device_id=dev, device_id_type=MESH)


_ANY = pl.BlockSpec(memory_space=pl.ANY)


class _Rider:
    def __init__(self, ins, out_shapes, sem_shapes, build, aliased=False):
        self.ins, self.out_shapes, self.sem_shapes, self.build, self.aliased = ins, out_shapes, sem_shapes, build, aliased


def _hosted_call(body, *, name, grid, in_specs, out_specs, out_shape, scratch_shapes, args, riders=(), aliases=None,
                 prefetch=None, riders_after_body=False):
    n_in, n_out, n_sc = len(in_specs), len(out_shape), len(scratch_shapes)
    r_in = [a for r in riders for a in r.ins]
    r_out = [s for r in riders for s in r.out_shapes]
    r_sem = [s for r in riders for s in r.sem_shapes]
    lead = () if prefetch is None else (prefetch,)
    assert prefetch is None or not (aliases or any(r.aliased for r in riders))

    def full_body(*refs):
        head, refs = refs[:len(lead)], refs[len(lead):]
        ins, rin = refs[:n_in], refs[n_in:n_in + len(r_in)]
        o0 = n_in + len(r_in)
        outs, rout = refs[o0:o0 + n_out], refs[o0 + n_out:o0 + n_out + len(r_out)]
        s0 = o0 + n_out + len(r_out)
        scratch, rsem = refs[s0:s0 + n_sc], refs[s0 + n_sc:]
        starts, waits = [], []
        pi = po = ps = 0
        for r in riders:
            st, wt = r.build(rin[pi:pi + len(r.ins)], rout[po:po + len(r.out_shapes)], rsem[ps:ps + len(r.sem_shapes)])
            starts += st
            waits += wt
            pi, po, ps = pi + len(r.ins), po + len(r.out_shapes), ps + len(r.sem_shapes)
        first = functools.reduce(jnp.logical_and, [pl.program_id(d) == 0 for d in range(len(grid))])
        last = functools.reduce(jnp.logical_and, [pl.program_id(d) == grid[d] - 1 for d in range(len(grid))])

        def start_riders():
            @pl.when(first)
            def _():
                for cp in starts:
                    cp.start()

        if riders and not riders_after_body:
            start_riders()
        body(*head, *ins, *outs, *scratch)
        if riders and riders_after_body:
            start_riders()
        if riders:
            @pl.when(last)
            def _():
                for wait in waits:
                    wait()

    io_aliases = dict(aliases or {})
    pi = po = 0
    for r in riders:
        if r.aliased:
            for q in range(len(r.ins)):
                io_aliases[n_in + pi + q] = n_out + po + q
        pi, po = pi + len(r.ins), po + len(r.out_shapes)
    specs = dict(
        grid=grid,
        in_specs=list(in_specs) + [_ANY] * len(r_in),
        out_specs=list(out_specs) + [_ANY] * len(r_out),
        scratch_shapes=list(scratch_shapes) + r_sem,
    )
    if prefetch is not None:
        specs = dict(grid_spec=pltpu.PrefetchScalarGridSpec(num_scalar_prefetch=1, **specs))
    res = pl.pallas_call(
        full_body,
        name=name,
        out_shape=list(out_shape) + r_out,
        input_output_aliases=io_aliases,
        compiler_params=_params(("arbitrary",) * len(grid)),
        **specs,
    )(*lead, *args, *r_in)
    rider_outs, po = [], n_out
    for r in riders:
        rider_outs.append(list(res[po:po + len(r.out_shapes)]))
        po += len(r.out_shapes)
    return list(res[:n_out]), rider_outs


def _run_riders(name, riders):
    r_in = [a for r in riders for a in r.ins]
    r_out = [s for r in riders for s in r.out_shapes]
    r_sem = [s for r in riders for s in r.sem_shapes]

    def body(*refs):
        rin, rout, rsem = refs[:len(r_in)], refs[len(r_in):len(r_in) + len(r_out)], refs[len(r_in) + len(r_out):]
        pi = po = ps = 0
        all_waits = []
        for r in riders:
            starts, waits = r.build(rin[pi:pi + len(r.ins)], rout[po:po + len(r.out_shapes)], rsem[ps:ps + len(r.sem_shapes)])
            for cp in starts:
                cp.start()
            all_waits += waits
            pi, po, ps = pi + len(r.ins), po + len(r.out_shapes), ps + len(r.sem_shapes)
        for wait in all_waits:
            wait()

    io_aliases = {}
    pi = po = 0
    for r in riders:
        if r.aliased:
            for q in range(len(r.ins)):
                io_aliases[pi + q] = po + q
        pi, po = pi + len(r.ins), po + len(r.out_shapes)
    res = pl.pallas_call(
        body,
        name=name,
        in_specs=[_ANY] * len(r_in),
        out_specs=[_ANY] * len(r_out),
        out_shape=r_out,
        scratch_shapes=r_sem,
        input_output_aliases=io_aliases,
    )(*r_in)
    outs, po = [], 0
    for r in riders:
        outs.append(list(res[po:po + len(r.out_shapes)]))
        po += len(r.out_shapes)
    return outs


def _piece(name):
    if name in BIG_PIECES:
        return (name, *BIG_PIECES[name], True)
    return (name, *SMALL_PIECES[name], False)


def _gather_rider(shards, names):
    pieces = [_piece(n) for n in names]
    n = len(pieces)

    def build(ins, outs, sems):
        local_sem, ici_send, ici_recv = sems
        _, _, c, s = _place()
        starts, waits = [], []
        for p, (_, kind, shard, split) in enumerate(pieces):
            cp = pltpu.make_async_copy(ins[p], _full_region(outs[p], kind, shard, s, None), local_sem.at[p])
            starts.append(cp)
            waits.append(cp.wait)
            h = c if split else None
            for k in (1, 2, 3):
                s2, dev = _other_chip(s, c, k)
                cp = _remote(_shard_region(ins[p], shard, h), _full_region(outs[p], kind, shard, s, h),
                             ici_send.at[p, k - 1], ici_recv.at[p, k - 1], dev)
                starts.append(cp)
                waits.append(cp.wait_send)
                region = _full_region(outs[p], kind, shard, s2, h)
                waits.append(_remote(region, region, ici_send.at[p, k - 1], ici_recv.at[p, k - 1], dev).wait_recv)
        return starts, waits

    return _Rider(
        [shards[name] for name in names],
        [jax.ShapeDtypeStruct(_full_shape(kind, shard), shards[name].dtype) for name, kind, shard, _ in pieces],
        [pltpu.SemaphoreType.DMA((n,)), pltpu.SemaphoreType.DMA((n, 3)), pltpu.SemaphoreType.DMA((n, 3))],
        build)


def _forward_rider(gathered, names):
    pieces = [_piece(n) for n in names]
    n = len(pieces)

    def build(ins, outs, sems):
        fwd_send, fwd_recv = sems
        x, y, c, s = _place()
        sibling = (x, y, 1 - c)
        starts, waits = [], []
        for p, (_, kind, shard, _) in enumerate(pieces):
            for k in (1, 2, 3):
                s2, _ = _other_chip(s, c, k)
                mine = _full_region(outs[p], kind, shard, s2, c)
                theirs = _full_region(outs[p], kind, shard, s2, 1 - c)
                cp = _remote(mine, mine, fwd_send.at[p, k - 1], fwd_recv.at[p, k - 1], sibling)
                starts.append(cp)
                waits.append(cp.wait_send)
                waits.append(_remote(theirs, theirs, fwd_send.at[p, k - 1], fwd_recv.at[p, k - 1], sibling).wait_recv)
        return starts, waits

    return _Rider(
        [gathered[name] for name in names],
        [jax.ShapeDtypeStruct(gathered[name].shape, gathered[name].dtype) for name in names],
        [pltpu.SemaphoreType.DMA((n, 3)), pltpu.SemaphoreType.DMA((n, 3))],
        build, aliased=True)


def _pair_exchange_rider(grads, names):
    n = len(names)

    def build(ins, outs, sems):
        send_sem, recv_sem = sems
        x, y, c, _ = _place()
        sibling = (x, y, 1 - c)
        starts, waits = [], []
        for p, name in enumerate(names):
            kind, shard = BIG_PIECES[name]
            for s2 in range(4):
                cp = _remote(_full_region(ins[p], kind, shard, s2, 1 - c), outs[p].at[s2], send_sem.at[p, s2],
                             recv_sem.at[p, s2], sibling)
                starts.append(cp)
                waits.append(cp.wait_send)
                waits.append(_remote(outs[p].at[s2], outs[p].at[s2], send_sem.at[p, s2], recv_sem.at[p, s2], sibling).wait_recv)
        return starts, waits

    return _Rider(
        [grads[name] for name in names],
        [jax.ShapeDtypeStruct((4,) + _half_shape(*BIG_PIECES[name]), F32) for name in names],
        [pltpu.SemaphoreType.DMA((n, 4))] * 2,
        build)


def _half_specs(kind, shard):
    half = shard[0] // 2
    if kind == "col":
        full = pl.BlockSpec((half, shard[1]), lambda j, pr: (pr[1], j))
        buf = pl.BlockSpec((None, half, shard[1]), lambda j, pr: (j, 0, 0))
    elif kind == "row":
        full = pl.BlockSpec((half, shard[1]), lambda j, pr: (2 * j + pr[1], 0))
        buf = pl.BlockSpec((None, half, shard[1]), lambda j, pr: (j, 0, 0))
    else:
        full = pl.BlockSpec((half, shard[1], shard[2]), lambda j, pr: (pr[1], j, 0))
        buf = pl.BlockSpec((None, half, shard[1], shard[2]), lambda j, pr: (j, 0, 0, 0))
    return full, buf


def _pair_sum(name, grad, recv, place):
    kind, shard = BIG_PIECES[name]
    full, buf = _half_specs(kind, shard)

    def body(pr, g_ref, r_ref, o_ref):
        o_ref[...] = (g_ref[...] + r_ref[...]).astype(BF16)

    return pl.pallas_call(
        body,
        name="pair_sum_" + name,
        grid_spec=pltpu.PrefetchScalarGridSpec(num_scalar_prefetch=1, grid=(4,), in_specs=[full, buf], out_specs=buf),
        out_shape=jax.ShapeDtypeStruct((4,) + _half_shape(kind, shard), BF16),
        compiler_params=_params(("arbitrary",)),
    )(place, grad, recv)


def _chip_exchange_rider(sums, names):
    n = len(names)

    def build(ins, outs, sems):
        send_sem, recv_sem = sems
        _, _, c, s = _place()
        starts, waits = [], []
        for p in range(n):
            for k in (1, 2, 3):
                s2, dev = _other_chip(s, c, k)
                cp = _remote(ins[p].at[s2], outs[p].at[k - 1], send_sem.at[p, k - 1], recv_sem.at[p, k - 1], dev)
                starts.append(cp)
                waits.append(cp.wait_send)
                waits.append(_remote(outs[p].at[k - 1], outs[p].at[k - 1], send_sem.at[p, k - 1], recv_sem.at[p, k - 1],
                                     dev).wait_recv)
        return starts, waits

    return _Rider(
        [sums[name] for name in names],
        [jax.ShapeDtypeStruct((3,) + _half_shape(*BIG_PIECES[name]), BF16) for name in names],
        [pltpu.SemaphoreType.DMA((n, 3))] * 2,
        build)


def _chip_sum(name, grad, recv_pair, recv_chip, place):
    kind, shard = BIG_PIECES[name]
    half = shard[0] // 2
    tail = tuple(shard[1:])
    zeros = (0,) * len(tail)
    if kind == "col":
        full = pl.BlockSpec((half,) + tail, lambda j, pr: (pr[1], pr[0]))
    elif kind == "row":
        full = pl.BlockSpec((half,) + tail, lambda j, pr: (2 * pr[0] + pr[1], 0))
    else:
        full = pl.BlockSpec((half,) + tail, lambda j, pr: (pr[1], pr[0], 0))
    pair = pl.BlockSpec((None, half) + tail, lambda j, pr: (pr[0], 0) + zeros)
    chip = pl.BlockSpec((3, half) + tail, lambda j, pr: (0, 0) + zeros)
    out = pl.BlockSpec((half,) + tail, lambda j, pr: (pr[1],) + zeros)

    def body(pr, g_ref, rp_ref, rc_ref, o_ref):
        total = g_ref[...] + rp_ref[...]
        for k in range(3):
            total = total + rc_ref[k].astype(F32)
        o_ref[...] = total

    return pl.pallas_call(
        body,
        name="chip_sum_" + name,
        grid_spec=pltpu.PrefetchScalarGridSpec(num_scalar_prefetch=1, grid=(1,), in_specs=[full, pair, chip], out_specs=out),
        out_shape=jax.ShapeDtypeStruct(shard, F32),
        compiler_params=_params(("arbitrary",)),
    )(place, grad, recv_pair, recv_chip)


def _sibling_exchange_rider(halves, names):
    n = len(names)

    def build(ins, outs, sems):
        send_sem, recv_sem = sems
        x, y, c, _ = _place()
        sibling = (x, y, 1 - c)
        starts, waits = [], []
        for p, name in enumerate(names):
            shard = BIG_PIECES[name][1]
            mine = _shard_region(outs[p], shard, c)
            theirs = _shard_region(outs[p], shard, 1 - c)
            cp = _remote(mine, mine, send_sem.at[p], recv_sem.at[p], sibling)
            starts.append(cp)
            waits.append(cp.wait_send)
            waits.append(_remote(theirs, theirs, send_sem.at[p], recv_sem.at[p], sibling).wait_recv)
        return starts, waits

    return _Rider(
        [halves[name] for name in names],
        [jax.ShapeDtypeStruct(BIG_PIECES[name][1], F32) for name in names],
        [pltpu.SemaphoreType.DMA((n,))] * 2,
        build, aliased=True)


FIRST_WEIGHTS = ["meta_tokens", "conv_w"]
WEIGHT_GROUPS = {
    "lru": ["lru_wa", "lru_wx"],
    "branch": ["w_branch_ret", "w_branch_lru", "w_out"],
    "ffn_in": ["w_ffn_in"],
    "ffn_out": ["w_ffn_out"],
}
WEIGHT_SCHEDULE = {
    "in_proj": [("gather", "lru")],
    "retention_fwd": [("forward", "lru"), ("gather", "branch")],
    "lru_fwd": [("forward", "branch"), ("gather", "ffn_in")],
    "mix_fwd": [("forward", "ffn_in"), ("gather", "ffn_out")],
    "ffn_fwd_gate": [("forward", "ffn_out")],
}
GRAD_GROUPS = {
    "ffn": ["w_ffn_in", "w_ffn_out"],
    "mixer": ["w_out", "w_branch_ret", "w_branch_lru", "lru_wa", "lru_wx"],
    "in": ["w_in"],
}
GRAD_SCHEDULE = {
    "mix_bwd": [("pair", "ffn")],
    "retention_bwd": [("chip", "ffn")],
    "lru_bwd": [("sibling", "ffn")],
    "dw_in": [("pair", "mixer")],
    "in_proj_bwd_0": [("chip", "mixer"), ("pair", "in")],
    "in_proj_bwd_1": [("sibling", "mixer"), ("chip", "in")],
}


class _CommPlan:
    def __init__(self, shards, place):
        self.shards, self.place = shards, place
        self.late = {}
        self.recv_pair, self.sums, self.recv_chip, self.halves, self.final = {}, {}, {}, {}, {}

    def _grad_rider(self, stage, group, grads):
        names = GRAD_GROUPS[group]
        if stage == "pair":
            return _pair_exchange_rider(grads, names)
        if stage == "chip":
            return _chip_exchange_rider(self.sums, names)
        return _sibling_exchange_rider(self.halves, names)

    def _grad_after(self, stage, group, outs, grads):
        names = GRAD_GROUPS[group]
        if stage == "pair":
            for n, o in zip(names, outs):
                self.recv_pair[n] = o
                self.sums[n] = _pair_sum(n, grads[n], o, self.place)
        elif stage == "chip":
            for n, o in zip(names, outs):
                self.halves[n] = _chip_sum(n, grads[n], self.recv_pair[n], o, self.place)
        else:
            self.final.update(zip(names, outs))

    def riders(self, host, w, grads):
        if host in WEIGHT_SCHEDULE:
            return [_gather_rider(self.shards, WEIGHT_GROUPS[group]) if stage == "gather"
                    else _forward_rider(self.late, WEIGHT_GROUPS[group]) for stage, group in WEIGHT_SCHEDULE[host]]
        return [self._grad_rider(stage, group, grads) for stage, group in GRAD_SCHEDULE.get(host, [])]

    def after(self, host, rider_outs, w, grads):
        for (stage, group), outs in zip(WEIGHT_SCHEDULE.get(host, []), rider_outs):
            (self.late if stage == "gather" else w).update(zip(WEIGHT_GROUPS[group], outs))
        for (stage, group), outs in zip(GRAD_SCHEDULE.get(host, []), rider_outs):
            self._grad_after(stage, group, outs, grads)

    def finish(self, partial):
        outs, (blocks,) = _run_riders("tail_exchange", [_sibling_exchange_rider(self.halves, GRAD_GROUPS["in"]),
                                                        _small_exchange_rider(partial)])
        self.final.update(zip(GRAD_GROUPS["in"], outs))
        return self.final, blocks


def _adamw_math(w, g, m, v):
    m = ADAM_B1 * m + (1.0 - ADAM_B1) * g
    v = ADAM_B2 * v + (1.0 - ADAM_B2) * (g * g)
    m_hat = m / (1.0 - ADAM_B1 ** ADAM_STEP)
    v_hat = v / (1.0 - ADAM_B2 ** ADAM_STEP)
    delta = -ADAM_LR * (m_hat / (jnp.sqrt(v_hat) + ADAM_EPS) + ADAM_WD * w)
    return delta, m, v


def _adamw(name, g, w, m, v):
    rows, cols = g.shape
    tr = rows // 4 if rows % 32 == 0 else rows

    def body(g_ref, w_ref, m_ref, v_ref, go_ref, d_ref, mo_ref, vo_ref):
        gv = g_ref[...]
        delta, m2, v2 = _adamw_math(w_ref[...], gv, m_ref[...], v_ref[...])
        go_ref[...] = gv
        d_ref[...] = delta
        mo_ref[...] = m2
        vo_ref[...] = v2

    spec = pl.BlockSpec((tr, cols), lambda i: (i, 0))
    return pl.pallas_call(
        body,
        name="adamw_" + name,
        grid=(rows // tr,),
        in_specs=[spec] * 4,
        out_specs=[spec] * 4,
        out_shape=[jax.ShapeDtypeStruct((rows, cols), F32)] * 4,
        compiler_params=_params(("arbitrary",)),
    )(g, w, m, v)


SMALL_ROWS = 48
VEC_ROWS = {"final_norm_w": 1, "ffn_norm_w": 8, "mix_norm_w": 16, "conv_b": 24, "lru_ba": 25, "lru_bx": 26, "lru_lambda": 27}
VEC_NAMES = list(VEC_ROWS)
CONV_W_ROW = 28
META_ROW = 32


def _small_exchange_rider(partial):
    def build(ins, outs, sems):
        local_sem, send_sem, recv_sem = sems
        _, _, c, s = _place()
        me = 2 * s + c
        cp = pltpu.make_async_copy(ins[0], outs[0].at[me], local_sem)
        starts, waits = [cp], [cp.wait]
        for k in range(1, 8):
            peer = jnp.bitwise_xor(me, k)
            dev = (peer // 4, (peer // 2) % 2, peer % 2)
            rd = _remote(ins[0], outs[0].at[me], send_sem.at[k - 1], recv_sem.at[k - 1], dev)
            starts.append(rd)
            waits.append(rd.wait_send)
            waits.append(_remote(ins[0], outs[0].at[peer], send_sem.at[k - 1], recv_sem.at[k - 1], dev).wait_recv)
        return starts, waits

    return _Rider([partial], [jax.ShapeDtypeStruct((8, SMALL_ROWS, D), F32)],
                  [pltpu.SemaphoreType.DMA, pltpu.SemaphoreType.DMA((7,)), pltpu.SemaphoreType.DMA((7,))], build)


def _small_update(blocks, place, vecs, conv, meta):
    nvec = len(VEC_NAMES)
    qcols = D // 4

    def in_order(ref):
        total = ref[0]
        for d in range(1, 8):
            total = total + ref[d]
        return total

    def body(pr, blocks_ref, cols_ref, *refs):
        vec_refs = refs[:3 * nvec]
        conv_refs = refs[3 * nvec:3 * nvec + 3]
        meta_refs = refs[3 * nvec + 3:3 * nvec + 6]
        outs = refs[3 * nvec + 6:]
        loss_ref, vec_out, conv_out, meta_out = outs[0], outs[1:5], outs[5:9], outs[9:13]
        tot, col = in_order(blocks_ref), in_order(cols_ref)
        loss_ref[...] = jnp.sum(tot[0:1, :], axis=1, keepdims=True)
        for o in vec_out:
            o[...] = jnp.zeros_like(o)
        for j, name in enumerate(VEC_NAMES):
            w, m, v = (r[...] for r in vec_refs[3 * j:3 * j + 3])
            g = tot[VEC_ROWS[name]:VEC_ROWS[name] + 1, :]
            if name == "lru_lambda":
                g = -g / (1.0 + jnp.exp(w))
            for o, val in zip(vec_out, (g,) + _adamw_math(w, g, m, v)):
                o[j:j + 1, :] = val
        for row, n_rows, ins, group in ((CONV_W_ROW, 4, conv_refs, conv_out), (META_ROW, N_META, meta_refs, meta_out)):
            g = col[row:row + n_rows, :]
            w, m, v = (r[...] for r in ins)
            for o, val in zip(group, (g,) + _adamw_math(w, g, m, v)):
                o[...] = val

    whole = lambda shape: pl.BlockSpec(shape, lambda i, pr: (0,) * len(shape))
    in_specs = [whole((8, SMALL_ROWS, D)), pl.BlockSpec((8, SMALL_ROWS, qcols), lambda i, pr: (0, 0, pr[0]))]
    in_specs += [whole((1, D))] * (3 * nvec) + [whole((4, qcols))] * 3 + [whole((N_META, qcols))] * 3
    out_shapes = [(1, 1)] + [(8, D)] * 4 + [(4, qcols)] * 4 + [(N_META, qcols)] * 4
    return pl.pallas_call(
        body,
        name="small_update",
        grid_spec=pltpu.PrefetchScalarGridSpec(num_scalar_prefetch=1, grid=(1,), in_specs=in_specs,
                                               out_specs=[whole(s) for s in out_shapes]),
        out_shape=[jax.ShapeDtypeStruct(s, F32) for s in out_shapes],
        compiler_params=_params(("arbitrary",)),
    )(place, blocks, blocks, *[a for t in vecs for a in t], *conv, *meta)


WEIGHT_ORDER = ["meta_tokens", "mix_norm_w", "w_in", "conv_w", "conv_b", "lru_wa", "lru_ba", "lru_wx", "lru_bx", "lru_lambda",
                "w_branch_ret", "w_branch_lru", "w_out", "ffn_norm_w", "w_ffn_in", "w_ffn_out", "final_norm_w"]


def kernel(x, meta_tokens, mix_norm_w, w_in, conv_w, conv_b, lru_wa, lru_ba, lru_wx, lru_bx, lru_lambda, w_branch_ret, w_branch_lru, w_out, ffn_norm_w, w_ffn_in, w_ffn_out, final_norm_w, loss_target, m_meta_tokens, m_mix_norm_w, m_w_in, m_conv_w, m_conv_b, m_lru_wa, m_lru_ba, m_lru_wx, m_lru_bx, m_lru_lambda, m_w_branch_ret, m_w_branch_lru, m_w_out, m_ffn_norm_w, m_w_ffn_in, m_w_ffn_out, m_final_norm_w, v_meta_tokens, v_mix_norm_w, v_w_in, v_conv_w, v_conv_b, v_lru_wa, v_lru_ba, v_lru_wx, v_lru_bx, v_lru_lambda, v_w_branch_ret, v_w_branch_lru, v_w_out, v_ffn_norm_w, v_w_ffn_in, v_w_ffn_out, v_final_norm_w):
    args = locals()
    wts = {n: args[n] for n in WEIGHT_ORDER}
    mom = {n: args["m_" + n] for n in WEIGHT_ORDER}
    var = {n: args["v_" + n] for n in WEIGHT_ORDER}
    place = jnp.stack([2 * lax.axis_index("x") + lax.axis_index("y"), lax.axis_index("c")]).astype(jnp.int32)

    shards = {n: wts[n][0].astype(BF16) for n in BIG_PIECES}
    shards["meta_tokens"] = wts["meta_tokens"]
    shards["conv_w"] = wts["conv_w"][0]
    plan = _CommPlan(shards, place)
    (first,) = _run_riders("gather_first", [_gather_rider(shards, FIRST_WEIGHTS)])
    w = dict(zip(FIRST_WEIGHTS, first))
    for n in VEC_NAMES:
        w[n] = wts[n].reshape(1, D)
    chips = jnp.bitwise_xor(place[0], jnp.arange(4, dtype=jnp.int32))
    w["route"] = jnp.concatenate([place, jnp.stack([2 * chips, 2 * chips + 1], axis=1).reshape(8)])
    w["w_in_shard"] = shards["w_in"]

    grad_x, grad_head, _, stats = _local_step(x[0], loss_target[0], w, plan)
    partial = jnp.concatenate(stats + [grad_head[PAD_ROWS:]], axis=0)
    shard_grads, blocks = plan.finish(partial)

    out = {}
    for n in BIG_PIECES:
        shape2d = (-1, wts[n].shape[-1])
        res = _adamw(n, *[a.reshape(shape2d) for a in (shard_grads[n], wts[n], mom[n], var[n])])
        out[n] = [r.reshape(wts[n].shape) for r in res]

    vecs = [tuple(a[n].reshape(1, D) for a in (wts, mom, var)) for n in VEC_NAMES]
    conv = tuple(a["conv_w"][0] for a in (wts, mom, var))
    meta = tuple(a["meta_tokens"] for a in (wts, mom, var))
    res = _small_update(blocks, place, vecs, conv, meta)
    loss = res[0].reshape(())
    for j, n in enumerate(VEC_NAMES):
        out[n] = [r[j].reshape(wts[n].shape) for r in res[1:5]]
    out["conv_w"] = [r.reshape(wts["conv_w"].shape) for r in res[5:9]]
    out["meta_tokens"] = list(res[9:13])

    return (loss, grad_x.reshape(x.shape)) + tuple(out[n][kind] for kind in range(4) for n in WEIGHT_ORDER)
```

```python
import functools
import math

import jax
import jax.numpy as jnp
from jax import lax
from jax.experimental import pallas as pl
from jax.experimental.pallas import tpu as pltpu

F32 = jnp.float32
BF16 = jnp.bfloat16

D = 1024
HEADS = 8
DH = 128
CHUNK = 256
N_META = 16
FRONT = 256
PAD_ROWS = FRONT - N_META
LRU_BLOCKS = 4
LRU_BLOCK = 256
LRU_C = 8.0
FFN = 2816
FFN_HALF = FFN // 2
IN_COLS = 8 * D
ROPE_BASE = 10000.0
EPS = 1e-6
QK_SCALE = DH ** -0.5

ADAM_LR = 0.001
ADAM_B1 = 0.9
ADAM_B2 = 0.999
ADAM_EPS = 1e-08
ADAM_WD = 0.01
ADAM_STEP = 10

TM = 256
TM_HEAVY = 768
FFN_BLOCK = FFN // 2
FFN_SUB = 768
TM_MIX_BWD = 256
TM_LRU = 384
TM_MIX_FWD = 384
TM_IN_PROJ = 1408
VMEM_LIMIT = 60 * 1024 * 1024
TN_VMEM_BUDGET = 40 * 1024 * 1024

NT_DIMS = (((1,), (1,)), ((), ()))
TN_DIMS = (((0,), (0,)), ((), ()))
MESH = pl.DeviceIdType.MESH


def _params(sem=None):
    if sem is None:
        return pltpu.CompilerParams(vmem_limit_bytes=VMEM_LIMIT)
    return pltpu.CompilerParams(dimension_semantics=sem, vmem_limit_bytes=VMEM_LIMIT)


def _dot(a, b):
    return jnp.dot(a, b, preferred_element_type=F32)


def _dot_nt(a, b):
    return lax.dot_general(a, b, NT_DIMS, preferred_element_type=F32)


def _dot_tn(a, b):
    return lax.dot_general(a, b, TN_DIMS, preferred_element_type=F32)


def _sigmoid(z):
    return 1.0 / (1.0 + jnp.exp(-z))


def _log1p(x):
    return jnp.where(x < 1e-3, x * (1.0 - x * (0.5 - x * (1.0 / 3.0))), jnp.log(1.0 + x))


def _softplus(x):
    return jnp.maximum(x, 0.0) + _log1p(jnp.exp(-jnp.abs(x)))


def _neg_expm1(x):
    series = -x * (1.0 + x * (0.5 + x * (1.0 / 6.0 + x * (1.0 / 24.0 + x * (1.0 / 120.0)))))
    return jnp.where(x > -0.05, series, 1.0 - jnp.exp(x))


_GELU_K = math.sqrt(2.0 / math.pi)


def _gelu_and_grad(x):
    inner = _GELU_K * (x + 0.044715 * x * x * x)
    t = jnp.tanh(inner)
    val = 0.5 * x * (1.0 + t)
    grad = 0.5 * (1.0 + t) + 0.5 * x * (1.0 - t * t) * _GELU_K * (1.0 + 3.0 * 0.044715 * x * x)
    return val, grad


def _row_ids(i, rows, shape):
    return i * rows + lax.broadcasted_iota(jnp.int32, shape, 0)


def _rope_tables(rows):
    pos = jnp.arange(rows, dtype=jnp.int32) - PAD_ROWS
    inv_freq = ROPE_BASE ** (-jnp.arange(0, DH, 2, dtype=F32) / DH)
    ang = pos.astype(F32)[:, None] * inv_freq[None, :]
    cos, sin = jnp.cos(ang), jnp.sin(ang)
    return jnp.concatenate([cos, cos], axis=1), jnp.concatenate([-sin, sin], axis=1)


def _decay_tables():
    log_g = jnp.log(1.0 - 2.0 ** (-5.0 - jnp.arange(HEADS, dtype=F32)))
    idx = jnp.arange(CHUNK, dtype=F32)
    diff = idx[:, None] - idx[None, :]
    intra = jnp.where(diff[None] >= 0, jnp.exp(jnp.maximum(diff, 0.0)[None] * log_g[:, None, None]), 0.0)
    q_decay = jnp.exp((idx + 1.0)[:, None] * log_g[None, :])
    k_decay = jnp.exp((CHUNK - 1.0 - idx)[:, None] * log_g[None, :])
    chunk_decay = jnp.exp(CHUNK * log_g)
    wide = lambda a: jnp.repeat(a, DH, axis=-1)
    return intra, jnp.swapaxes(intra, 1, 2), wide(q_decay), wide(k_decay), wide(chunk_decay[None, :])


def _norm1(head, x2d, norm_w, riders=()):
    rows = FRONT + x2d.shape[0]

    def body(head_ref, x_ref, nw_ref, u_ref, rstd_ref):
        def norm(hv):
            rs = lax.rsqrt(jnp.mean(hv * hv, axis=-1, keepdims=True) + EPS)
            u_ref[...] = ((hv * rs) * nw_ref[...]).astype(BF16)
            rstd_ref[...] = rs

        @pl.when(pl.program_id(0) == 0)
        def _():
            norm(head_ref[...])

        @pl.when(pl.program_id(0) > 0)
        def _():
            norm(x_ref[...])

    return _hosted_call(
        body,
        name="norm1",
        grid=(rows // TM,),
        in_specs=[
            pl.BlockSpec((FRONT, D), lambda i: (0, 0)),
            pl.BlockSpec((TM, D), lambda i: (jnp.maximum(i - 1, 0), 0)),
            pl.BlockSpec((1, D), lambda i: (0, 0)),
        ],
        out_specs=[pl.BlockSpec((TM, D), lambda i: (i, 0)), pl.BlockSpec((TM, 1), lambda i: (i, 0))],
        out_shape=[jax.ShapeDtypeStruct((rows, D), BF16), jax.ShapeDtypeStruct((rows, 1), F32)],
        scratch_shapes=[],
        args=(head, x2d, norm_w),
        riders=riders,
    )


def _heavy_tile(rows):
    return TM_HEAVY if rows % TM_HEAVY == 0 else TM


def _lru_tile(rows):
    return TM_LRU if rows % TM_LRU == 0 else TM


def _swap_3_4(group):
    return jnp.where(group == 3, 4, jnp.where(group == 4, 3, group))


LRU_IN_COL = 3
GATES_COL = 1


def _in_proj(u, w_shard, route, cos_t, sin_t, riders=()):
    rows = u.shape[0]
    tm = TM_IN_PROJ if rows % TM_IN_PROJ == 0 else _heavy_tile(rows)
    nt = rows // tm
    kind, shard = BIG_PIECES["w_in"]

    def body(route_ref, u_hbm, wsh_ref, cos_ref, sin_ref, proj_ref, wfull_ref,
             u_sc, w_sc, u_sem, w_sem, local_sem, ici_send, ici_recv, fwd_send, fwd_recv):
        g, i = pl.program_id(0), pl.program_id(1)
        s, c = route_ref[0], route_ref[1]
        gid = route_ref[2 + g]
        sibling = (s // 2, s % 2, 1 - c)
        local = pltpu.make_async_copy(wsh_ref, _full_region(wfull_ref, kind, shard, s, None), local_sem)
        u_copies = [pltpu.make_async_copy(u_hbm.at[pl.ds(t * tm, tm)], u_sc.at[t], u_sem.at[t]) for t in range(nt)]
        sends, arrivals = [], []
        for k in (1, 2, 3):
            s2, dev = _other_chip(s, c, k)
            sends.append(_remote(_shard_region(wsh_ref, shard, c), _full_region(wfull_ref, kind, shard, s, c),
                                 ici_send.at[k - 1], ici_recv.at[k - 1], dev))
            mine = _full_region(wfull_ref, kind, shard, s2, c)
            theirs = _full_region(wfull_ref, kind, shard, s2, 1 - c)
            arrivals.append((_remote(mine, mine, ici_send.at[k - 1], ici_recv.at[k - 1], dev),
                             _remote(mine, mine, fwd_send.at[k - 1], fwd_recv.at[k - 1], sibling),
                             _remote(theirs, theirs, fwd_send.at[k - 1], fwd_recv.at[k - 1], sibling)))

        slot = g % 2
        last_tile = i == nt - 1

        def block_copy(src, col, to_slot):
            return pltpu.make_async_copy(src.at[:, pl.ds(pl.multiple_of(col * D, D), D)], w_sc.at[to_slot],
                                         w_sem.at[to_slot])

        @pl.when(jnp.logical_and(g == 0, i == 0))
        def _():
            for cp in sends:
                cp.start()
            local.start()
            for cp in u_copies:
                cp.start()
            cp = block_copy(wsh_ref, 0, 0)
            cp.start()
            cp.wait()

        @pl.when(jnp.logical_and(last_tile, g == 0))
        def _():
            block_copy(wsh_ref, 1, 1).start()

        for k, (arrived, forward, forwarded) in zip((1, 2, 3), arrivals):
            @pl.when(jnp.logical_and(last_tile, g == 2 * k - 1))
            def _():
                arrived.wait_recv()
                forward.start()
                forwarded.wait_recv()
                block_copy(wfull_ref, route_ref[2 + 2 * k], 0).start()

            @pl.when(jnp.logical_and(last_tile, g == 2 * k))
            def _():
                block_copy(wfull_ref, route_ref[3 + 2 * k], 1).start()

        @pl.when(jnp.logical_and(i == 0, g > 0))
        def _():
            block_copy(wfull_ref, 0, slot).wait()

        for t in range(nt):
            @pl.when(jnp.logical_and(g == 0, i == t))
            def _():
                u_copies[t].wait()

        acc = _dot(u_sc[i], w_sc[slot])

        @pl.when(gid < 2)
        def _():
            scale = jnp.where(gid == 1, QK_SCALE, 1.0).astype(F32)
            for h in range(HEADS):
                sl = slice(h * DH, (h + 1) * DH)
                blk = acc[:, sl]
                out = (blk * cos_ref[...] + pltpu.roll(blk, DH // 2, axis=1) * sin_ref[...]) * scale
                proj_ref[:, sl] = out.astype(BF16)

        @pl.when(gid >= 2)
        def _():
            proj_ref[...] = acc.astype(BF16)

        @pl.when(jnp.logical_and(g == 7, i == nt - 1))
        def _():
            local.wait()
            for cp in sends:
                cp.wait_send()
            for _, forward, _ in arrivals:
                forward.wait_send()

    return _hosted_call(
        body,
        name="in_proj",
        grid=(8, nt),
        in_specs=[
            _ANY, _ANY,
            pl.BlockSpec((tm, DH), lambda g, i, rt: (i, 0)),
            pl.BlockSpec((tm, DH), lambda g, i, rt: (i, 0)),
        ],
        out_specs=[pl.BlockSpec((tm, D), lambda g, i, rt: (i, _swap_3_4(rt[2 + g]))), _ANY],
        out_shape=[jax.ShapeDtypeStruct((rows, IN_COLS), BF16), jax.ShapeDtypeStruct((D, IN_COLS), BF16)],
        scratch_shapes=[
            pltpu.VMEM((nt, tm, D), BF16), pltpu.VMEM((2, D, D), BF16),
            pltpu.SemaphoreType.DMA((nt,)), pltpu.SemaphoreType.DMA((2,)), pltpu.SemaphoreType.DMA,
            pltpu.SemaphoreType.DMA((3,)), pltpu.SemaphoreType.DMA((3,)),
            pltpu.SemaphoreType.DMA((3,)), pltpu.SemaphoreType.DMA((3,)),
        ],
        args=(u, w_shard, cos_t, sin_t),
        riders=riders,
        prefetch=route,
        riders_after_body=True,
    )


def _retention_fwd(proj_bf, intra, q_dec, k_dec, c_dec, riders=()):
    rows = proj_bf.shape[0]
    nc = rows // CHUNK

    def body(q_ref, k_ref, v_ref, m_ref, qd_ref, kd_ref, cd_ref, o_ref, sprev_ref, s_sc):
        @pl.when(pl.program_id(0) == 0)
        def _():
            s_sc[...] = jnp.zeros_like(s_sc)

        states = [s_sc[h] for h in range(HEADS)]
        results = []
        for h in range(HEADS):
            sl = slice(h * DH, (h + 1) * DH)
            q, k, v = q_ref[:, sl], k_ref[:, sl], v_ref[:, sl]
            state = states[h]
            state_b = state.astype(BF16)
            s = _dot_nt(q, k) * m_ref[h]
            inner = _dot(s.astype(BF16), v)
            cross = _dot(q, state_b) * qd_ref[:, sl]
            k_scaled = (k.astype(F32) * kd_ref[:, sl]).astype(BF16)
            results.append((state_b, (inner + cross).astype(BF16), state * cd_ref[:, sl] + _dot_tn(k_scaled, v)))

        for h, (state_b, o_h, new_state) in enumerate(results):
            sprev_ref[0, h] = state_b
            o_ref[:, h * DH:(h + 1) * DH] = o_h
            s_sc[h] = new_state

    chunk_spec = lambda col: pl.BlockSpec((CHUNK, D), lambda c: (c, col))
    const2 = lambda shape: pl.BlockSpec(shape, lambda c: (0, 0))
    return _hosted_call(
        body,
        name="retention_fwd",
        grid=(nc,),
        in_specs=[
            chunk_spec(0), chunk_spec(1), chunk_spec(2),
            pl.BlockSpec((HEADS, CHUNK, CHUNK), lambda c: (0, 0, 0)),
            const2((CHUNK, D)), const2((CHUNK, D)), const2((1, D)),
        ],
        out_specs=[
            pl.BlockSpec((CHUNK, D), lambda c: (c, 0)),
            pl.BlockSpec((1, HEADS, DH, DH), lambda c: (c, 0, 0, 0)),
        ],
        out_shape=[
            jax.ShapeDtypeStruct((rows, D), BF16),
            jax.ShapeDtypeStruct((nc, HEADS, DH, DH), BF16),
        ],
        scratch_shapes=[pltpu.VMEM((HEADS, DH, DH), F32)],
        args=(proj_bf, proj_bf, proj_bf, intra, q_dec, k_dec, c_dec),
        riders=riders,
    )


def _shift_down(x, first8_prev, d):
    rolled = pltpu.roll(x, d, axis=0)
    head = pltpu.roll(jnp.concatenate([first8_prev, x[0:8]], axis=0), d, axis=0)[8:16]
    return rolled, head


def _conv_and_gates(x, prev8, cw_ref, cb_ref, wa_ref, wx_ref, ba_ref, bx_ref, lam_ref, c_sc):
    cw = cw_ref[...]
    conv = cb_ref[...] + cw[3:4] * x
    head = cb_ref[...] + cw[3:4] * x[0:8]
    for d in (1, 2, 3):
        rolled, hd = _shift_down(x, prev8, d)
        conv = conv + cw[3 - d:4 - d] * rolled
        head = head + cw[3 - d:4 - d] * hd
    c_sc[...] = conv
    c_sc[0:8, :] = head
    c = c_sc[...]
    zr, zi = [], []
    for g in range(LRU_BLOCKS):
        sl = slice(g * LRU_BLOCK, (g + 1) * LRU_BLOCK)
        cg = c[:, sl].astype(BF16)
        zr.append(_dot(cg, wa_ref[g]))
        zi.append(_dot(cg, wx_ref[g]))
    r = _sigmoid(jnp.concatenate(zr, axis=1) + ba_ref[...])
    gate_i = _sigmoid(jnp.concatenate(zi, axis=1) + bx_ref[...])
    sp = _softplus(-lam_ref[...])
    log_a = (-LRU_C) * r * sp
    a = jnp.exp(log_a)
    mult = jnp.sqrt(_neg_expm1(2.0 * log_a))
    return c, r, gate_i, a, mult, log_a


def _lru_fwd(proj, conv_w, conv_b, wa, wx, ba, bx, lam, riders=()):
    rows = proj.shape[0]
    TM = _lru_tile(rows)
    nt = rows // TM

    def body(x_ref, cw_ref, cb_ref, wa_ref, wx_ref, ba_ref, bx_ref, lam_ref,
             h_ref, c_ref, r_ref, i_ref, la_ref, mult_ref, prev_sc, carry_sc, c_sc, a_sc, u_sc, h_sc):
        i = pl.program_id(0)

        @pl.when(i == 0)
        def _():
            prev_sc[...] = jnp.zeros_like(prev_sc)
            carry_sc[...] = jnp.zeros_like(carry_sc)

        x = x_ref[...].astype(F32)
        c, r, gate_i, a, mult, log_a = _conv_and_gates(x, prev_sc[...], cw_ref, cb_ref, wa_ref, wx_ref, ba_ref, bx_ref,
                                                       lam_ref, c_sc)
        for ref, val in ((c_ref, c), (r_ref, r), (i_ref, gate_i), (la_ref, log_a), (mult_ref, mult)):
            ref[...] = val.astype(BF16)
        prev_sc[...] = x[TM - 8:TM]
        valid = _row_ids(i, TM, (TM, D)) >= PAD_ROWS
        a_sc[...] = a
        u_sc[...] = jnp.where(valid, mult * gate_i * c, 0.0)
        row8 = lax.broadcasted_iota(jnp.int32, (8, D), 0)

        def group(gi, hprev):
            r0 = pl.multiple_of(gi * 8, 8)
            aa = a_sc[pl.ds(r0, 8), :]
            uu = u_sc[pl.ds(r0, 8), :]
            for d in (1, 2, 4):
                a_sh = jnp.where(row8 >= d, pltpu.roll(aa, d, axis=0), 1.0)
                u_sh = jnp.where(row8 >= d, pltpu.roll(uu, d, axis=0), 0.0)
                uu = uu + aa * u_sh
                aa = aa * a_sh
            hb = aa * hprev + uu
            h_sc[pl.ds(r0, 8), :] = hb
            return hb[7:8, :]

        hlast = lax.fori_loop(0, TM // 8, group, carry_sc[0:1, :])
        carry_sc[0:1, :] = hlast
        h_ref[...] = h_sc[...].astype(BF16)

    vec = pl.BlockSpec((1, D), lambda i: (0, 0))
    wspec = pl.BlockSpec((LRU_BLOCKS, LRU_BLOCK, LRU_BLOCK), lambda i: (0, 0, 0))
    return _hosted_call(
        body,
        name="lru_fwd",
        grid=(nt,),
        in_specs=[
            pl.BlockSpec((TM, D), lambda i: (i, LRU_IN_COL)),
            pl.BlockSpec((4, D), lambda i: (0, 0)),
            vec, wspec, wspec, vec, vec, vec,
        ],
        out_specs=[pl.BlockSpec((TM, D), lambda i: (i, 0)) for _ in range(6)],
        out_shape=[jax.ShapeDtypeStruct((rows, D), BF16) for _ in range(6)],
        scratch_shapes=[
            pltpu.VMEM((8, D), F32), pltpu.VMEM((8, D), F32),
            pltpu.VMEM((TM, D), F32), pltpu.VMEM((TM, D), F32), pltpu.VMEM((TM, D), F32), pltpu.VMEM((TM, D), F32),
        ],
        args=(proj, conv_w, conv_b, wa, wx, ba, bx, lam),
        riders=riders,
    )


def _group_norm(o):
    outs, rstds = [], []
    for h in range(HEADS):
        oh = o[:, h * DH:(h + 1) * DH]
        rs = lax.rsqrt(jnp.mean(oh * oh, axis=-1, keepdims=True) + EPS)
        outs.append(oh * rs)
        rstds.append(rs)
    return jnp.concatenate(outs, axis=1), rstds


def _frame_rows(head_hbm, x_hbm, buf, sems, i, tm, nt):
    slot = i % 2

    @pl.when(i == 0)
    def _():
        first = [pltpu.make_async_copy(head_hbm, buf.at[0, pl.ds(0, FRONT)], sems.at[2]),
                 pltpu.make_async_copy(x_hbm.at[pl.ds(0, tm - FRONT)], buf.at[0, pl.ds(FRONT, tm - FRONT)], sems.at[0])]
        for cp in first:
            cp.start()
        for cp in first:
            cp.wait()

    @pl.when(i + 1 < nt)
    def _():
        start = pl.multiple_of((i + 1) * tm - FRONT, 128)
        pltpu.make_async_copy(x_hbm.at[pl.ds(start, tm)], buf.at[1 - slot], sems.at[1 - slot]).start()

    @pl.when(i > 0)
    def _():
        pltpu.make_async_copy(x_hbm.at[pl.ds(0, tm)], buf.at[slot], sems.at[slot]).wait()

    return slot


def _mix_fwd(head, x2d, o, proj, h_lru, w_br, w_bl, w_o, ffn_norm_w, riders=()):
    rows = o.shape[0]
    tm = TM_MIX_FWD
    assert rows % tm == 0 and tm > FRONT
    nt = rows // tm

    def body(head_hbm, x_hbm, o_ref, gates_ref, hl_ref, wbr_ref, wbl_ref, wo_ref, nw_ref,
             yret_ref, ylru_ref, h1_ref, u2_ref, rstd_ref, h0_sc, h0_sem):
        i = pl.program_id(0)
        slot = _frame_rows(head_hbm, x_hbm, h0_sc, h0_sem, i, tm, nt)
        gate = lambda j: gates_ref[:, j * D:(j + 1) * D].astype(F32)
        on, _ = _group_norm(o_ref[...].astype(F32))
        gret = gate(0)
        a_ret = (gret * _sigmoid(gret) * on).astype(BF16)
        y_ret = _dot(a_ret, wbr_ref[...])
        gl, _ = _gelu_and_grad(gate(1))
        a_lru = (gl * hl_ref[...].astype(F32)).astype(BF16)
        y_lru = _dot(a_lru, wbl_ref[...])
        mixed = (_sigmoid(gate(2)) * y_ret + _sigmoid(gate(3)) * y_lru).astype(BF16)
        delta = _dot(mixed, wo_ref[...])
        yret_ref[...] = y_ret.astype(BF16)
        ylru_ref[...] = y_lru.astype(BF16)
        h1 = h0_sc[slot] + delta
        rs = lax.rsqrt(jnp.mean(h1 * h1, axis=-1, keepdims=True) + EPS)
        h1_ref[...] = h1
        u2_ref[...] = ((h1 * rs) * nw_ref[...]).astype(BF16)
        rstd_ref[...] = rs

    tile = lambda col: pl.BlockSpec((tm, D), lambda i: (i, col))
    wspec = pl.BlockSpec((D, D), lambda i: (0, 0))
    return _hosted_call(
        body,
        name="mix_fwd",
        grid=(nt,),
        in_specs=[
            _ANY, _ANY,
            tile(0), pl.BlockSpec((tm, 4 * D), lambda i: (i, GATES_COL)), tile(0),
            wspec, wspec, wspec,
            pl.BlockSpec((1, D), lambda i: (0, 0)),
        ],
        out_specs=[tile(0), tile(0), tile(0), tile(0), pl.BlockSpec((tm, 1), lambda i: (i, 0))],
        out_shape=[
            jax.ShapeDtypeStruct((rows, D), BF16), jax.ShapeDtypeStruct((rows, D), BF16),
            jax.ShapeDtypeStruct((rows, D), F32), jax.ShapeDtypeStruct((rows, D), BF16),
            jax.ShapeDtypeStruct((rows, 1), F32),
        ],
        scratch_shapes=[pltpu.VMEM((2, tm, D), F32), pltpu.SemaphoreType.DMA((3,))],
        args=(head, x2d, o, proj, h_lru, w_br, w_bl, w_o, ffn_norm_w),
        riders=riders,
    )


def _ffn_fwd_gate(u2, w_ffn_in, riders=()):
    rows = u2.shape[0]
    tm = _heavy_tile(rows)
    nb = FFN // FFN_BLOCK
    hid = lambda: pl.BlockSpec((tm, FFN_BLOCK), lambda i, j: (i, j))

    def body(u2_ref, w_hbm, silu_ref, dsilu_ref, act_ref, w_sc, w_sem):
        j = pl.program_id(1)

        @pl.when(jnp.logical_and(pl.program_id(0) == 0, j == 0))
        def _():
            cp = pltpu.make_async_copy(w_hbm, w_sc, w_sem)
            cp.start()
            cp.wait()

        u2 = u2_ref[...]
        for a, b in ((0, FFN_SUB), (FFN_SUB, FFN_BLOCK)):
            col = pl.multiple_of(j * FFN_BLOCK + a, 128)
            g = _dot(u2, w_sc[:, pl.ds(col, b - a)])
            up = _dot(u2, w_sc[:, pl.ds(pl.multiple_of(FFN + col, 128), b - a)])
            sg = _sigmoid(g)
            silu = g * sg
            silu_ref[:, a:b] = silu.astype(BF16)
            dsilu_ref[:, a:b] = (up * (sg * (1.0 + g * (1.0 - sg)))).astype(BF16)
            act_ref[:, a:b] = (silu * up).astype(BF16)

    return _hosted_call(
        body,
        name="ffn_fwd_gate",
        grid=(rows // tm, nb),
        in_specs=[pl.BlockSpec((tm, D), lambda i, j: (i, 0)), _ANY],
        out_specs=[hid(), hid(), hid()],
        out_shape=[jax.ShapeDtypeStruct((rows, FFN), BF16)] * 3,
        scratch_shapes=[pltpu.VMEM((D, 2 * FFN), BF16), pltpu.SemaphoreType.DMA],
        args=(u2, w_ffn_in),
        riders=riders,
    )


def _ffn_out_loss(act, h1, w_ffn_out, target, final_norm_w):
    rows = act.shape[0]
    tm = _heavy_tile(rows)
    nt = rows // tm

    def body(act_ref, h1_ref, wo_ref, fnw_ref, tgt_hbm, dh2_ref, stats_ref, tgt_sc, tgt_sem):
        i = pl.program_id(0)
        slot = i % 2

        @pl.when(i == 0)
        def _():
            stats_ref[...] = jnp.zeros_like(stats_ref)
            tgt_sc[0, 0:FRONT, :] = jnp.zeros((FRONT, D), F32)
            cp = pltpu.make_async_copy(tgt_hbm.at[pl.ds(0, tm - FRONT)], tgt_sc.at[0, pl.ds(FRONT, tm - FRONT)], tgt_sem.at[0])
            cp.start()
            cp.wait()

        @pl.when(i + 1 < nt)
        def _():
            start = pl.multiple_of((i + 1) * tm - FRONT, FRONT)
            pltpu.make_async_copy(tgt_hbm.at[pl.ds(start, tm)], tgt_sc.at[1 - slot], tgt_sem.at[1 - slot]).start()

        @pl.when(i > 0)
        def _():
            pltpu.make_async_copy(tgt_hbm.at[pl.ds(0, tm)], tgt_sc.at[slot], tgt_sem.at[slot]).wait()

        h2 = h1_ref[...] + _dot(act_ref[...], wo_ref[...])
        rs = lax.rsqrt(jnp.mean(h2 * h2, axis=-1, keepdims=True) + EPS)
        n = h2 * rs
        fnw = fnw_ref[...]
        valid = _row_ids(i, tm, (tm, D)) >= FRONT
        diff = jnp.where(valid, n * fnw - tgt_sc[slot], 0.0)
        dy = diff * (1.0 / D)
        stats_ref[0:1, :] += (0.5 / D) * jnp.sum(diff * diff, axis=0, keepdims=True)
        stats_ref[1:2, :] += jnp.sum(dy * n, axis=0, keepdims=True)
        dn = dy * fnw
        dh2_ref[...] = rs * (dn - n * jnp.mean(dn * n, axis=-1, keepdims=True))

    return pl.pallas_call(
        body,
        name="ffn_out_loss",
        grid=(nt,),
        in_specs=[
            pl.BlockSpec((tm, FFN), lambda i: (i, 0)),
            pl.BlockSpec((tm, D), lambda i: (i, 0)),
            pl.BlockSpec((FFN, D), lambda i: (0, 0)),
            pl.BlockSpec((1, D), lambda i: (0, 0)),
            _ANY,
        ],
        out_specs=[pl.BlockSpec((tm, D), lambda i: (i, 0)), pl.BlockSpec((8, D), lambda i: (0, 0))],
        out_shape=[jax.ShapeDtypeStruct((rows, D), F32), jax.ShapeDtypeStruct((8, D), F32)],
        scratch_shapes=[pltpu.VMEM((2, tm, D), F32), pltpu.SemaphoreType.DMA((2,))],
        compiler_params=_params(("arbitrary",)),
    )(act, h1, w_ffn_out, final_norm_w, target)


def _ffn_bwd(dh2, silu, dsilu, h1, rstd2, w_ffn_in, w_ffn_out, ffn_norm_w):
    rows = dh2.shape[0]
    tm = _heavy_tile(rows)
    nb = FFN // FFN_BLOCK
    blk = lambda: pl.BlockSpec((tm, FFN_BLOCK), lambda i, j: (i, j))

    def gate_body(dh2_ref, silu_ref, dsilu_ref, wo_hbm, dg_ref, dup_ref, dh2b_ref, wo_sc, wo_sem):
        j = pl.program_id(1)

        @pl.when(jnp.logical_and(pl.program_id(0) == 0, j == 0))
        def _():
            cp = pltpu.make_async_copy(wo_hbm, wo_sc, wo_sem)
            cp.start()
            cp.wait()

        @pl.when(j == 0)
        def _():
            dh2b_ref[...] = dh2_ref[...].astype(BF16)

        dact = _dot_nt(dh2b_ref[...], wo_sc[pl.ds(pl.multiple_of(j * FFN_BLOCK, 128), FFN_BLOCK), :])
        dup_ref[...] = (dact * silu_ref[...].astype(F32)).astype(BF16)
        dg_ref[...] = (dact * dsilu_ref[...].astype(F32)).astype(BF16)

    dg, dup, dh2b = pl.pallas_call(
        gate_body,
        name="ffn_bwd_gate",
        grid=(rows // tm, nb),
        in_specs=[pl.BlockSpec((tm, D), lambda i, j: (i, 0)), blk(), blk(), _ANY],
        out_specs=[blk(), blk(), pl.BlockSpec((tm, D), lambda i, j: (i, 0))],
        out_shape=[jax.ShapeDtypeStruct((rows, FFN), BF16)] * 2 + [jax.ShapeDtypeStruct((rows, D), BF16)],
        scratch_shapes=[pltpu.VMEM((FFN, D), BF16), pltpu.SemaphoreType.DMA],
        compiler_params=_params(("arbitrary", "arbitrary")),
    )(dh2, silu, dsilu, w_ffn_out)

    def body(dg_ref, dup_ref, dh2_ref, h1_ref, rstd_ref, w_hbm, nw_ref, dh1_ref, stats_ref, w_sc, w_sem):
        @pl.when(pl.program_id(0) == 0)
        def _():
            stats_ref[...] = jnp.zeros_like(stats_ref)
            cp = pltpu.make_async_copy(w_hbm, w_sc, w_sem)
            cp.start()
            cp.wait()

        du = _dot_nt(dg_ref[...], w_sc[:, 0:FFN]) + _dot_nt(dup_ref[...], w_sc[:, FFN:2 * FFN])
        rs = rstd_ref[...]
        n = h1_ref[...] * rs
        stats_ref[0:1, :] += jnp.sum(du * n, axis=0, keepdims=True)
        dn = du * nw_ref[...]
        dh1_ref[...] = dh2_ref[...] + rs * (dn - n * jnp.mean(dn * n, axis=-1, keepdims=True))

    row = lambda width: pl.BlockSpec((tm, width), lambda i: (i, 0))
    dh1, stats = pl.pallas_call(
        body,
        name="ffn_bwd_in",
        grid=(rows // tm,),
        in_specs=[row(FFN), row(FFN), row(D), row(D), row(1), _ANY, pl.BlockSpec((1, D), lambda i: (0, 0))],
        out_specs=[row(D), pl.BlockSpec((8, D), lambda i: (0, 0))],
        out_shape=[jax.ShapeDtypeStruct((rows, D), F32), jax.ShapeDtypeStruct((8, D), F32)],
        scratch_shapes=[pltpu.VMEM((D, 2 * FFN), BF16), pltpu.SemaphoreType.DMA],
        compiler_params=_params(("arbitrary",)),
    )(dg, dup, dh2, h1, rstd2, w_ffn_in, ffn_norm_w)
    return dg, dup, dh2b, dh1, stats


def _mix_bwd(dh1, o, proj, h_lru, y_ret, y_lru, w_br, w_bl, w_o, riders=()):
    rows = dh1.shape[0]
    tm = TM_MIX_BWD
    nt = rows // tm

    def body(dh1_ref, o_ref, gates_ref, hl_ref, yret_ref, ylru_ref, wbr_ref, wbl_ref, wo_ref,
             dproj_ref, do_ref, dhl_ref, mixed_ref, aret_ref, alru_ref, dyret_ref, dylru_ref):
        gate = lambda j: gates_ref[:, j * D:(j + 1) * D].astype(F32)
        dmixed = _dot_nt(dh1_ref[...].astype(BF16), wo_ref[...])
        y_ret, y_lru = yret_ref[...].astype(F32), ylru_ref[...].astype(F32)
        sa, sb = _sigmoid(gate(2)), _sigmoid(gate(3))
        mixed_ref[...] = (sa * y_ret + sb * y_lru).astype(BF16)
        dga = dmixed * y_ret * sa * (1.0 - sa)
        dgb = dmixed * y_lru * sb * (1.0 - sb)
        dy_ret = (dmixed * sa).astype(BF16)
        dy_lru = (dmixed * sb).astype(BF16)
        dyret_ref[...] = dy_ret
        dylru_ref[...] = dy_lru
        da_ret = _dot_nt(dy_ret, wbr_ref[...])
        da_lru = _dot_nt(dy_lru, wbl_ref[...])

        gret = gate(0)
        sg = _sigmoid(gret)
        silu = gret * sg
        on, rstds = _group_norm(o_ref[...].astype(F32))
        aret_ref[...] = (silu * on).astype(BF16)
        dgret = da_ret * on * (sg * (1.0 + gret * (1.0 - sg)))
        don = da_ret * silu
        for h in range(HEADS):
            sl = slice(h * DH, (h + 1) * DH)
            onh, donh = on[:, sl], don[:, sl]
            do_ref[:, sl] = (rstds[h] * (donh - onh * jnp.mean(donh * onh, axis=-1, keepdims=True))).astype(BF16)

        gl, gl_grad = _gelu_and_grad(gate(1))
        hl = hl_ref[...].astype(F32)
        alru_ref[...] = (gl * hl).astype(BF16)
        dlgate = da_lru * hl * gl_grad
        dhl_ref[...] = (da_lru * gl).astype(BF16)

        for j, val in enumerate((dgret, dlgate, dga, dgb)):
            dproj_ref[:, j * D:(j + 1) * D] = val.astype(BF16)

    tile = lambda col: pl.BlockSpec((tm, D), lambda i: (i, col))
    gates = pl.BlockSpec((tm, 4 * D), lambda i: (i, GATES_COL))
    wspec = pl.BlockSpec((D, D), lambda i: (0, 0))
    bf = lambda: jax.ShapeDtypeStruct((rows, D), BF16)
    return _hosted_call(
        body,
        name="mix_bwd",
        grid=(nt,),
        in_specs=[tile(0), tile(0), gates, tile(0), tile(0), tile(0), wspec, wspec, wspec],
        out_specs=[pl.BlockSpec((tm, 4 * D), lambda i: (i, GATES_COL))] + [tile(0)] * 7,
        out_shape=[jax.ShapeDtypeStruct((rows, IN_COLS), BF16)] + [bf() for _ in range(7)],
        scratch_shapes=[],
        args=(dh1, o, proj, h_lru, y_ret, y_lru, w_br, w_bl, w_o),
        riders=riders,
    )


def _retention_bwd(dproj, proj_bf, do, sprev, intra, intra_t, q_dec, k_dec, c_dec, cos_t, sin_t, riders=()):
    rows = proj_bf.shape[0]
    nc = rows // CHUNK

    def body(dproj_in_ref, q_ref, k_ref, v_ref, do_ref, sprev_ref, m_ref, mt_ref, qd_ref, kd_ref, cd_ref, cos_ref, sin_ref,
             dproj_ref, ds_sc):
        @pl.when(pl.program_id(0) == 0)
        def _():
            ds_sc[...] = jnp.zeros_like(ds_sc)

        cos, sin = cos_ref[...], sin_ref[...]

        def unrotate(dy):
            return dy * cos - pltpu.roll(dy, DH // 2, axis=1) * sin

        dstates = [ds_sc[h] for h in range(HEADS)]
        results = []
        for h in range(HEADS):
            sl = slice(h * DH, (h + 1) * DH)
            q, k, v = q_ref[:, sl], k_ref[:, sl], v_ref[:, sl]
            do = do_ref[:, sl]
            dob = do.astype(BF16)
            doq = (do * qd_ref[:, sl]).astype(BF16)
            state_prev = sprev_ref[0, h]
            dstate = dstates[h]
            dstate_b = dstate.astype(BF16)
            s_t = (_dot_nt(k, q) * mt_ref[h]).astype(BF16)
            ds_t = (_dot_nt(v, dob) * mt_ref[h]).astype(BF16)
            ds = (_dot_nt(dob, v) * m_ref[h]).astype(BF16)
            kd = kd_ref[:, sl]
            dq = _dot(ds, k) + _dot_nt(doq, state_prev)
            dk = _dot(ds_t, q) + _dot_nt(v, dstate_b) * kd
            k_scaled = (k.astype(F32) * kd).astype(BF16)
            dv = _dot(s_t, dob) + _dot(k_scaled, dstate_b)
            results.append((dstate * cd_ref[:, sl] + _dot_tn(q, doq), unrotate(dq).astype(BF16),
                            (unrotate(dk) * QK_SCALE).astype(BF16), dv.astype(BF16)))

        for h, (new_dstate, dq_b, dk_b, dv_b) in enumerate(results):
            ds_sc[h] = new_dstate
            for part, val in enumerate((dq_b, dk_b, dv_b)):
                dproj_ref[:, part * D + h * DH:part * D + (h + 1) * DH] = val

    rev = lambda c: nc - 1 - c
    chunk_spec = lambda col: pl.BlockSpec((CHUNK, D), lambda c: (rev(c), col))
    const2 = lambda shape: pl.BlockSpec(shape, lambda c: (0, 0))
    const3 = pl.BlockSpec((HEADS, CHUNK, CHUNK), lambda c: (0, 0, 0))
    return _hosted_call(
        body,
        name="retention_bwd",
        grid=(nc,),
        in_specs=[
            pl.BlockSpec(memory_space=pl.ANY),
            chunk_spec(0), chunk_spec(1), chunk_spec(2), chunk_spec(0),
            pl.BlockSpec((1, HEADS, DH, DH), lambda c: (rev(c), 0, 0, 0)),
            const3, const3,
            const2((CHUNK, D)), const2((CHUNK, D)), const2((1, D)),
            pl.BlockSpec((CHUNK, DH), lambda c: (rev(c), 0)),
            pl.BlockSpec((CHUNK, DH), lambda c: (rev(c), 0)),
        ],
        out_specs=[pl.BlockSpec((CHUNK, 3 * D), lambda c: (rev(c), 0))],
        out_shape=[jax.ShapeDtypeStruct(dproj.shape, BF16)],
        aliases={0: 0},
        scratch_shapes=[pltpu.VMEM((HEADS, DH, DH), F32)],
        args=(dproj, proj_bf, proj_bf, proj_bf, do, sprev, intra, intra_t, q_dec, k_dec, c_dec, cos_t, sin_t),
        riders=riders,
    )


def _lru_bwd(dproj, proj, saved, dhl, conv_w, wa, wx, lam, riders=()):
    rows = proj.shape[0]
    TM = _lru_tile(rows)
    nt = rows // TM
    per8 = TM // 8

    def body(dproj_in_ref, x_ref, xprev_ref, h_ref, hprev_ref, c_ref, r_ref, i_ref, la_ref, mult_ref, dhl_ref,
             cw_ref, wa_ref, wx_ref, lam_ref,
             dproj_ref, dwa_ref, dwx_ref, stats_ref, anext_sc, dhnext_sc, dcnext_sc, c_sc, b_sc, dh_sc):
        step = pl.program_id(0)
        i = nt - 1 - step

        @pl.when(step == 0)
        def _():
            anext_sc[...] = jnp.zeros_like(anext_sc)
            dhnext_sc[...] = jnp.zeros_like(dhnext_sc)
            dcnext_sc[...] = jnp.zeros_like(dcnext_sc)
            dwa_ref[...] = jnp.zeros_like(dwa_ref)
            dwx_ref[...] = jnp.zeros_like(dwx_ref)
            stats_ref[...] = jnp.zeros_like(stats_ref)

        first = i == 0
        x = x_ref[...].astype(F32)
        prev8 = jnp.where(first, 0.0, xprev_ref[8:16, :].astype(F32))
        c_b = c_ref[...]
        c, r, gate_i, mult = (ref[...].astype(F32) for ref in (c_ref, r_ref, i_ref, mult_ref))
        a = jnp.exp(la_ref[...].astype(F32))
        sp = _softplus(-lam_ref[...])
        dh_sc[...] = dhl_ref[...].astype(F32)

        b_sc[...] = pltpu.roll(a, TM - 1, axis=0)
        b_sc[TM - 1:TM, :] = anext_sc[0:1, :]
        anext_sc[0:1, :] = a[0:1, :]
        row8 = lax.broadcasted_iota(jnp.int32, (8, D), 0)

        def group(gi, dhnext):
            r0 = pl.multiple_of((per8 - 1 - gi) * 8, 8)
            bb = b_sc[pl.ds(r0, 8), :]
            uu = dh_sc[pl.ds(r0, 8), :]
            for d in (1, 2, 4):
                b_sh = jnp.where(row8 < 8 - d, pltpu.roll(bb, 8 - d, axis=0), 1.0)
                u_sh = jnp.where(row8 < 8 - d, pltpu.roll(uu, 8 - d, axis=0), 0.0)
                uu = uu + bb * u_sh
                bb = bb * b_sh
            dhb = bb * dhnext + uu
            dh_sc[pl.ds(r0, 8), :] = dhb
            return dhb[0:1, :]

        dhfirst = lax.fori_loop(0, per8, group, dhnext_sc[0:1, :])
        dhnext_sc[0:1, :] = dhfirst
        dh = dh_sc[...]

        h = h_ref[...].astype(F32)
        hprev8 = jnp.where(first, 0.0, hprev_ref[8:16, :].astype(F32))
        h_dn, h_head = _shift_down(h, hprev8, 1)
        c_sc[...] = h_dn
        c_sc[0:8, :] = h_head
        h_before = c_sc[...]

        valid = _row_ids(i, TM, (TM, D)) >= PAD_ROWS
        da = dh * h_before
        du = jnp.where(valid, dh, 0.0)
        dmult = du * gate_i * c
        dgate_i = du * mult * c
        dc = du * mult * gate_i
        dla = da * a - dmult * (a * a) / mult
        dla = jnp.where(valid, dla, 0.0)
        dr = dla * ((-LRU_C) * sp)
        dzr = dr * r * (1.0 - r)
        dzi = dgate_i * gate_i * (1.0 - gate_i)
        stats_ref[1:2, :] += jnp.sum(dzr, axis=0, keepdims=True)
        stats_ref[2:3, :] += jnp.sum(dzi, axis=0, keepdims=True)
        stats_ref[3:4, :] += jnp.sum(dla * ((-LRU_C) * r), axis=0, keepdims=True)
        dc_gate = []
        for g in range(LRU_BLOCKS):
            sl = slice(g * LRU_BLOCK, (g + 1) * LRU_BLOCK)
            cg = c_b[:, sl]
            dzr_g = dzr[:, sl].astype(BF16)
            dzi_g = dzi[:, sl].astype(BF16)
            dc_gate.append(_dot_nt(dzr_g, wa_ref[g]) + _dot_nt(dzi_g, wx_ref[g]))
            dwa_ref[g] += _dot_tn(cg, dzr_g)
            dwx_ref[g] += _dot_tn(cg, dzi_g)
        dc = dc + jnp.concatenate(dc_gate, axis=1)

        cw = cw_ref[...]
        stats_ref[0:1, :] += jnp.sum(dc, axis=0, keepdims=True)
        stats_ref[7:8, :] += jnp.sum(dc * x, axis=0, keepdims=True)
        dx = cw[3:4] * dc
        tail_src = jnp.concatenate([dc[TM - 8:TM], dcnext_sc[...]], axis=0)
        dx_tail = cw[3:4] * dc[TM - 8:TM]
        for d in (1, 2, 3):
            dx = dx + cw[3 - d:4 - d] * pltpu.roll(dc, TM - d, axis=0)
            dx_tail = dx_tail + cw[3 - d:4 - d] * pltpu.roll(tail_src, 16 - d, axis=0)[0:8]
            rolled, hd = _shift_down(x, prev8, d)
            b_sc[...] = rolled
            b_sc[0:8, :] = hd
            stats_ref[7 - d:8 - d, :] += jnp.sum(dc * b_sc[...], axis=0, keepdims=True)
        dcnext_sc[...] = dc[0:8]
        dproj_ref[...] = dx.astype(BF16)
        dproj_ref[TM - 8:TM, :] = dx_tail.astype(BF16)

    rev = lambda s: nt - 1 - s
    vec = pl.BlockSpec((1, D), lambda s: (0, 0))
    wspec = pl.BlockSpec((LRU_BLOCKS, LRU_BLOCK, LRU_BLOCK), lambda s: (0, 0, 0))
    prev16 = lambda col: pl.BlockSpec((16, D), lambda s: (jnp.maximum(rev(s) * (TM // 16) - 1, 0), col))
    tile = lambda: pl.BlockSpec((TM, D), lambda s: (rev(s), 0))
    h_lru, c_sv, r_sv, i_sv, la_sv, mult_sv = saved
    return _hosted_call(
        body,
        name="lru_bwd",
        grid=(nt,),
        in_specs=[
            pl.BlockSpec(memory_space=pl.ANY),
            pl.BlockSpec((TM, D), lambda s: (rev(s), LRU_IN_COL)), prev16(LRU_IN_COL),
            tile(), prev16(0),
            tile(), tile(), tile(), tile(), tile(), tile(),
            pl.BlockSpec((4, D), lambda s: (0, 0)),
            wspec, wspec, vec,
        ],
        out_specs=[
            pl.BlockSpec((TM, D), lambda s: (rev(s), LRU_IN_COL)),
            wspec, wspec,
            pl.BlockSpec((8, D), lambda s: (0, 0)),
        ],
        out_shape=[
            jax.ShapeDtypeStruct(dproj.shape, BF16),
            jax.ShapeDtypeStruct((LRU_BLOCKS, LRU_BLOCK, LRU_BLOCK), F32),
            jax.ShapeDtypeStruct((LRU_BLOCKS, LRU_BLOCK, LRU_BLOCK), F32),
            jax.ShapeDtypeStruct((8, D), F32),
        ],
        aliases={0: 0},
        scratch_shapes=[
            pltpu.VMEM((8, D), F32), pltpu.VMEM((8, D), F32), pltpu.VMEM((8, D), F32),
            pltpu.VMEM((TM, D), F32), pltpu.VMEM((TM, D), F32), pltpu.VMEM((TM, D), F32),
        ],
        args=(dproj, proj, proj, h_lru, h_lru, c_sv, r_sv, i_sv, la_sv, mult_sv, dhl, conv_w, wa, wx, lam),
        riders=riders,
    )


def _in_proj_bwd(dproj, w_in, part, prev=None, riders=()):
    rows = dproj.shape[0]
    tm = _heavy_tile(rows)
    nt = rows // tm
    first = 0 if part == 0 else (nt + 1) // 2
    count = (nt + 1) // 2 if part == 0 else nt - first

    def body(*refs):
        dproj_ref, w_hbm, du_ref, w_sc, w_sem = refs[-5:]

        @pl.when(pl.program_id(0) == 0)
        def _():
            moves = [((0, 3), 0), ((4, 1), 3), ((3, 1), 4), ((5, 3), 5)]
            copies = [pltpu.make_async_copy(w_hbm.at[:, pl.ds(src * D, n * D)], w_sc.at[:, pl.ds(dst * D, n * D)], w_sem.at[q])
                      for q, ((src, n), dst) in enumerate(moves)]
            for cp in copies:
                cp.start()
            for cp in copies:
                cp.wait()

        du_ref[...] = _dot_nt(dproj_ref[...], w_sc[...])

    in_specs = [pl.BlockSpec((tm, IN_COLS), lambda i: (first + i, 0)), _ANY]
    args = (dproj, w_in)
    if prev is not None:
        in_specs = [_ANY] + in_specs
        args = (prev,) + args
    return _hosted_call(
        body,
        name="in_proj_bwd_%d" % part,
        grid=(count,),
        in_specs=in_specs,
        out_specs=[pl.BlockSpec((tm, D), lambda i: (first + i, 0))],
        out_shape=[jax.ShapeDtypeStruct((rows, D), F32)],
        scratch_shapes=[pltpu.VMEM((D, IN_COLS), BF16), pltpu.SemaphoreType.DMA((4,))],
        aliases={0: 0} if prev is not None else None,
        args=args,
        riders=riders,
    )


def _norm1_bwd(du, dh1, head, x2d, rstd1, norm_w, riders=()):
    rows = du.shape[0]

    def body(du_ref, dh1_ref, head_ref, x_ref, rstd_ref, nw_ref, gx_ref, ghead_ref, stats_ref):
        i = pl.program_id(0)

        @pl.when(i == 0)
        def _():
            stats_ref[...] = jnp.zeros_like(stats_ref)

        def finish(h0, out_ref):
            du = du_ref[...]
            rs = rstd_ref[...]
            n = h0 * rs
            stats_ref[0:1, :] += jnp.sum(du * n, axis=0, keepdims=True)
            dn = du * nw_ref[...]
            out_ref[...] = dh1_ref[...] + rs * (dn - n * jnp.mean(dn * n, axis=-1, keepdims=True))

        @pl.when(i == 0)
        def _():
            finish(head_ref[...], ghead_ref)

        @pl.when(i > 0)
        def _():
            finish(x_ref[...], gx_ref)

    tile = pl.BlockSpec((TM, D), lambda i: (i, 0))
    return _hosted_call(
        body,
        name="norm1_bwd",
        grid=(rows // TM,),
        in_specs=[
            tile, tile,
            pl.BlockSpec((FRONT, D), lambda i: (0, 0)),
            pl.BlockSpec((TM, D), lambda i: (jnp.maximum(i - 1, 0), 0)),
            pl.BlockSpec((TM, 1), lambda i: (i, 0)),
            pl.BlockSpec((1, D), lambda i: (0, 0)),
        ],
        out_specs=[
            pl.BlockSpec((TM, D), lambda i: (jnp.maximum(i - 1, 0), 0)),
            pl.BlockSpec((FRONT, D), lambda i: (0, 0)),
            pl.BlockSpec((8, D), lambda i: (0, 0)),
        ],
        out_shape=[
            jax.ShapeDtypeStruct(x2d.shape, F32),
            jax.ShapeDtypeStruct((FRONT, D), F32),
            jax.ShapeDtypeStruct((8, D), F32),
        ],
        scratch_shapes=[],
        args=(du, dh1, head, x2d, rstd1, norm_w),
        riders=riders,
    )


def _matmul_tn(name, x, dy, out_cols, col0=0, prev=None, k_block=None, n_block=None, riders=(), col_map=None):
    col_map = col_map or (lambda n: n)
    rows, kdim = x.shape
    ndim = dy.shape[1]
    kb = k_block or kdim
    nb = n_block or ndim
    step_bytes = lambda t: 2 * t * (kb * x.dtype.itemsize + nb * dy.dtype.itemsize) + 2 * kb * nb * 4
    tr = next(t for t in (2816, 1408, TM_HEAVY, TM) if rows % t == 0 and (t == TM or step_bytes(t) <= TN_VMEM_BUDGET))
    nr, nk, nn = rows // tr, kdim // kb, ndim // nb
    cb0 = col0 // nb

    def body(*refs):
        x_ref, dy_ref, out_ref = refs[-3], refs[-2], refs[-1]
        part = _dot_tn(x_ref[...].astype(BF16), dy_ref[...].astype(BF16))

        @pl.when(pl.program_id(2) == 0)
        def _():
            out_ref[...] = part

        @pl.when(pl.program_id(2) > 0)
        def _():
            out_ref[...] += part

    in_specs = [
        pl.BlockSpec((tr, kb), lambda n, k, r: (r, k)),
        pl.BlockSpec((tr, nb), lambda n, k, r: (r, n)),
    ]
    args = [x, dy]
    aliases = {}
    if prev is not None:
        in_specs = [pl.BlockSpec(memory_space=pl.ANY)] + in_specs
        args = [prev] + args
        aliases = {0: 0}
    (out,), rider_outs = _hosted_call(
        body,
        name=name,
        grid=(nn, nk, nr),
        in_specs=in_specs,
        out_specs=[pl.BlockSpec((kb, nb), lambda n, k, r: (k, cb0 + col_map(n)))],
        out_shape=[jax.ShapeDtypeStruct((kdim, out_cols), F32)],
        scratch_shapes=[],
        aliases=aliases,
        args=args,
        riders=riders,
    )
    return (out, rider_outs) if riders else out


def _local_step(x2d, target, w, plan):
    rows = FRONT + x2d.shape[0]
    head = jnp.concatenate([jnp.zeros((PAD_ROWS, D), F32), w["meta_tokens"]], axis=0)
    cos_t, sin_t = _rope_tables(rows)
    intra, intra_t, q_dec, k_dec, c_dec = _decay_tables()
    grads = {}

    def hosted(host, fn, *args, **kwargs):
        outs, rider_outs = fn(*args, riders=plan.riders(host, w, grads), **kwargs)
        plan.after(host, rider_outs, w, grads)
        return outs

    (u1, rstd1), _ = _norm1(head, x2d, w["mix_norm_w"])
    proj, w["w_in"] = hosted("in_proj", _in_proj, u1, w["w_in_shard"], w["route"], cos_t, sin_t)
    o, sprev = hosted("retention_fwd", _retention_fwd, proj, intra, q_dec, k_dec, c_dec)
    lru_args = (w["conv_w"], w["conv_b"], w["lru_wa"], w["lru_wx"], w["lru_ba"], w["lru_bx"], w["lru_lambda"])
    lru_saved = hosted("lru_fwd", _lru_fwd, proj, *lru_args)
    h_lru = lru_saved[0]
    y_ret, y_lru, h1, u2, rstd2 = hosted("mix_fwd", _mix_fwd, head, x2d, o, proj, h_lru, w["w_branch_ret"],
                                         w["w_branch_lru"], w["w_out"], w["ffn_norm_w"])
    silu, dsilu, act = hosted("ffn_fwd_gate", _ffn_fwd_gate, u2, w["w_ffn_in"])
    dh2, stats_loss = _ffn_out_loss(act, h1, w["w_ffn_out"], target, w["final_norm_w"])

    dg, dup, dh2b, dh1, stats_ffn = _ffn_bwd(dh2, silu, dsilu, h1, rstd2, w["w_ffn_in"], w["w_ffn_out"], w["ffn_norm_w"])
    grads["w_ffn_in"] = _matmul_tn("dw_ffn_up", u2, dup, 2 * FFN, col0=FFN, n_block=FFN_HALF,
                                   prev=_matmul_tn("dw_ffn_gate", u2, dg, 2 * FFN, n_block=FFN_HALF))
    grads["w_ffn_out"] = _matmul_tn("dw_ffn_out", act, dh2b, D, k_block=FFN_HALF)
    (dproj, do, dhl, mixed, a_ret, a_lru, dy_ret, dy_lru) = hosted(
        "mix_bwd", _mix_bwd, dh1, o, proj, h_lru, y_ret, y_lru, w["w_branch_ret"], w["w_branch_lru"], w["w_out"])
    grads["w_out"] = _matmul_tn("dw_out", mixed, dh1, D)
    grads["w_branch_ret"] = _matmul_tn("dw_branch_ret", a_ret, dy_ret, D)
    grads["w_branch_lru"] = _matmul_tn("dw_branch_lru", a_lru, dy_lru, D)
    (dproj,) = hosted("retention_bwd", _retention_bwd, dproj, proj, do, sprev, intra, intra_t, q_dec, k_dec, c_dec,
                      cos_t, sin_t)
    dproj, grads["lru_wa"], grads["lru_wx"], stats_lru = hosted(
        "lru_bwd", _lru_bwd, dproj, proj, lru_saved, dhl, w["conv_w"], w["lru_wa"], w["lru_wx"], w["lru_lambda"])
    grads["w_in"], rider_outs = _matmul_tn("dw_in", u1, dproj, IN_COLS, n_block=D, col_map=_swap_3_4,
                                           riders=plan.riders("dw_in", w, grads))
    plan.after("dw_in", rider_outs, w, grads)
    (du1,) = hosted("in_proj_bwd_0", _in_proj_bwd, dproj, w["w_in"], 0)
    (du1,) = hosted("in_proj_bwd_1", _in_proj_bwd, dproj, w["w_in"], 1, du1)
    (grad_x, grad_head, stats_in), _ = _norm1_bwd(du1, dh1, head, x2d, rstd1, w["mix_norm_w"])
    return grad_x, grad_head, grads, [stats_loss, stats_ffn, stats_in, stats_lru]


BIG_PIECES = {
    "w_in": ("col", (D, 2 * D)),
    "w_ffn_in": ("col", (D, FFN_HALF)),
    "w_ffn_out": ("row", (FFN // 4, D)),
    "w_branch_ret": ("row", (D // 4, D)),
    "w_branch_lru": ("row", (D // 4, D)),
    "w_out": ("row", (D // 4, D)),
    "lru_wa": ("lru", (LRU_BLOCKS, LRU_BLOCK // 4, LRU_BLOCK)),
    "lru_wx": ("lru", (LRU_BLOCKS, LRU_BLOCK // 4, LRU_BLOCK)),
}
SMALL_PIECES = {"meta_tokens": ("col", (N_META, D // 4)), "conv_w": ("col", (4, D // 4))}


def _full_shape(kind, shard):
    if kind == "col":
        return (shard[0], 4 * shard[1])
    if kind == "row":
        return (4 * shard[0], shard[1])
    return (shard[0], 4 * shard[1], shard[2])


def _half_shape(kind, shard):
    return (shard[0] // 2,) + tuple(shard[1:])


def _aligned(start, multiple):
    return start if isinstance(start, int) else pl.multiple_of(start, multiple)


def _lead(h, size):
    if h is None:
        return pl.ds(0, size)
    return pl.ds(_aligned(h * (size // 2), size // 2), size // 2)


def _full_region(ref, kind, shard, s, h):
    if kind == "col":
        return ref.at[_lead(h, shard[0]), pl.ds(_aligned(s * shard[1], shard[1]), shard[1])]
    if kind == "row":
        size = shard[0] if h is None else shard[0] // 2
        start = s * shard[0] + (0 if h is None else h * (shard[0] // 2))
        return ref.at[pl.ds(_aligned(start, 16), size), :]
    return ref.at[_lead(h, shard[0]), pl.ds(_aligned(s * shard[1], shard[1]), shard[1]), :]


def _shard_region(ref, shard, h):
    return ref.at[_lead(h, shard[0])]


def _place():
    x, y, c = lax.axis_index("x"), lax.axis_index("y"), lax.axis_index("c")
    return x, y, c, 2 * x + y


def _other_chip(s, c, k):
    s2 = jnp.bitwise_xor(s, k)
    return s2, (s2 // 2, s2 % 2, c)


def _remote(src, dst, send_sem, recv_sem, dev):
    return pltpu.make_async_remote_copy(src_ref=src, dst_ref=dst, send_sem=send_sem, recv_sem=recv_sem,
                                        device_id=dev, device_id_type=MESH)


_ANY = pl.BlockSpec(memory_space=pl.ANY)


class _Rider:
    def __init__(self, ins, out_shapes, sem_shapes, build, aliased=False):
        self.ins, self.out_shapes, self.sem_shapes, self.build, self.aliased = ins, out_shapes, sem_shapes, build, aliased


def _hosted_call(body, *, name, grid, in_specs, out_specs, out_shape, scratch_shapes, args, riders=(), aliases=None,
                 prefetch=None, riders_after_body=False):
    n_in, n_out, n_sc = len(in_specs), len(out_shape), len(scratch_shapes)
    r_in = [a for r in riders for a in r.ins]
    r_out = [s for r in riders for s in r.out_shapes]
    r_sem = [s for r in riders for s in r.sem_shapes]
    lead = () if prefetch is None else (prefetch,)
    assert prefetch is None or not (aliases or any(r.aliased for r in riders))

    def full_body(*refs):
        head, refs = refs[:len(lead)], refs[len(lead):]
        ins, rin = refs[:n_in], refs[n_in:n_in + len(r_in)]
        o0 = n_in + len(r_in)
        outs, rout = refs[o0:o0 + n_out], refs[o0 + n_out:o0 + n_out + len(r_out)]
        s0 = o0 + n_out + len(r_out)
        scratch, rsem = refs[s0:s0 + n_sc], refs[s0 + n_sc:]
        starts, waits = [], []
        pi = po = ps = 0
        for r in riders:
            st, wt = r.build(rin[pi:pi + len(r.ins)], rout[po:po + len(r.out_shapes)], rsem[ps:ps + len(r.sem_shapes)])
            starts += st
            waits += wt
            pi, po, ps = pi + len(r.ins), po + len(r.out_shapes), ps + len(r.sem_shapes)
        first = functools.reduce(jnp.logical_and, [pl.program_id(d) == 0 for d in range(len(grid))])
        last = functools.reduce(jnp.logical_and, [pl.program_id(d) == grid[d] - 1 for d in range(len(grid))])

        def start_riders():
            @pl.when(first)
            def _():
                for cp in starts:
                    cp.start()

        if riders and not riders_after_body:
            start_riders()
        body(*head, *ins, *outs, *scratch)
        if riders and riders_after_body:
            start_riders()
        if riders:
            @pl.when(last)
            def _():
                for wait in waits:
                    wait()

    io_aliases = dict(aliases or {})
    pi = po = 0
    for r in riders:
        if r.aliased:
            for q in range(len(r.ins)):
                io_aliases[n_in + pi + q] = n_out + po + q
        pi, po = pi + len(r.ins), po + len(r.out_shapes)
    specs = dict(
        grid=grid,
        in_specs=list(in_specs) + [_ANY] * len(r_in),
        out_specs=list(out_specs) + [_ANY] * len(r_out),
        scratch_shapes=list(scratch_shapes) + r_sem,
    )
    if prefetch is not None:
        specs = dict(grid_spec=pltpu.PrefetchScalarGridSpec(num_scalar_prefetch=1, **specs))
    res = pl.pallas_call(
        full_body,
        name=name,
        out_shape=list(out_shape) + r_out,
        input_output_aliases=io_aliases,
        compiler_params=_params(("arbitrary",) * len(grid)),
        **specs,
    )(*lead, *args, *r_in)
    rider_outs, po = [], n_out
    for r in riders:
        rider_outs.append(list(res[po:po + len(r.out_shapes)]))
        po += len(r.out_shapes)
    return list(res[:n_out]), rider_outs


def _run_riders(name, riders):
    r_in = [a for r in riders for a in r.ins]
    r_out = [s for r in riders for s in r.out_shapes]
    r_sem = [s for r in riders for s in r.sem_shapes]

    def body(*refs):
        rin, rout, rsem = refs[:len(r_in)], refs[len(r_in):len(r_in) + len(r_out)], refs[len(r_in) + len(r_out):]
        pi = po = ps = 0
        all_waits = []
        for r in riders:
            starts, waits = r.build(rin[pi:pi + len(r.ins)], rout[po:po + len(r.out_shapes)], rsem[ps:ps + len(r.sem_shapes)])
            for cp in starts:
                cp.start()
            all_waits += waits
            pi, po, ps = pi + len(r.ins), po + len(r.out_shapes), ps + len(r.sem_shapes)
        for wait in all_waits:
            wait()

    io_aliases = {}
    pi = po = 0
    for r in riders:
        if r.aliased:
            for q in range(len(r.ins)):
                io_aliases[pi + q] = po + q
        pi, po = pi + len(r.ins), po + len(r.out_shapes)
    res = pl.pallas_call(
        body,
        name=name,
        in_specs=[_ANY] * len(r_in),
        out_specs=[_ANY] * len(r_out),
        out_shape=r_out,
        scratch_shapes=r_sem,
        input_output_aliases=io_aliases,
    )(*r_in)
    outs, po = [], 0
    for r in riders:
        outs.append(list(res[po:po + len(r.out_shapes)]))
        po += len(r.out_shapes)
    return outs


def _piece(name):
    if name in BIG_PIECES:
        return (name, *BIG_PIECES[name], True)
    return (name, *SMALL_PIECES[name], False)


def _gather_rider(shards, names):
    pieces = [_piece(n) for n in names]
    n = len(pieces)

    def build(ins, outs, sems):
        local_sem, ici_send, ici_recv = sems
        _, _, c, s = _place()
        starts, waits = [], []
        for p, (_, kind, shard, split) in enumerate(pieces):
            cp = pltpu.make_async_copy(ins[p], _full_region(outs[p], kind, shard, s, None), local_sem.at[p])
            starts.append(cp)
            waits.append(cp.wait)
            h = c if split else None
            for k in (1, 2, 3):
                s2, dev = _other_chip(s, c, k)
                cp = _remote(_shard_region(ins[p], shard, h), _full_region(outs[p], kind, shard, s, h),
                             ici_send.at[p, k - 1], ici_recv.at[p, k - 1], dev)
                starts.append(cp)
                waits.append(cp.wait_send)
                region = _full_region(outs[p], kind, shard, s2, h)
                waits.append(_remote(region, region, ici_send.at[p, k - 1], ici_recv.at[p, k - 1], dev).wait_recv)
        return starts, waits

    return _Rider(
        [shards[name] for name in names],
        [jax.ShapeDtypeStruct(_full_shape(kind, shard), shards[name].dtype) for name, kind, shard, _ in pieces],
        [pltpu.SemaphoreType.DMA((n,)), pltpu.SemaphoreType.DMA((n, 3)), pltpu.SemaphoreType.DMA((n, 3))],
        build)


def _forward_rider(gathered, names):
    pieces = [_piece(n) for n in names]
    n = len(pieces)

    def build(ins, outs, sems):
        fwd_send, fwd_recv = sems
        x, y, c, s = _place()
        sibling = (x, y, 1 - c)
        starts, waits = [], []
        for p, (_, kind, shard, _) in enumerate(pieces):
            for k in (1, 2, 3):
                s2, _ = _other_chip(s, c, k)
                mine = _full_region(outs[p], kind, shard, s2, c)
                theirs = _full_region(outs[p], kind, shard, s2, 1 - c)
                cp = _remote(mine, mine, fwd_send.at[p, k - 1], fwd_recv.at[p, k - 1], sibling)
                starts.append(cp)
                waits.append(cp.wait_send)
                waits.append(_remote(theirs, theirs, fwd_send.at[p, k - 1], fwd_recv.at[p, k - 1], sibling).wait_recv)
        return starts, waits

    return _Rider(
        [gathered[name] for name in names],
        [jax.ShapeDtypeStruct(gathered[name].shape, gathered[name].dtype) for name in names],
        [pltpu.SemaphoreType.DMA((n, 3)), pltpu.SemaphoreType.DMA((n, 3))],
        build, aliased=True)


def _pair_exchange_rider(grads, names):
    n = len(names)

    def build(ins, outs, sems):
        send_sem, recv_sem = sems
        x, y, c, _ = _place()
        sibling = (x, y, 1 - c)
        starts, waits = [], []
        for p, name in enumerate(names):
            kind, shard = BIG_PIECES[name]
            for s2 in range(4):
                cp = _remote(_full_region(ins[p], kind, shard, s2, 1 - c), outs[p].at[s2], send_sem.at[p, s2],
                             recv_sem.at[p, s2], sibling)
                starts.append(cp)
                waits.append(cp.wait_send)
                waits.append(_remote(outs[p].at[s2], outs[p].at[s2], send_sem.at[p, s2], recv_sem.at[p, s2], sibling).wait_recv)
        return starts, waits

    return _Rider(
        [grads[name] for name in names],
        [jax.ShapeDtypeStruct((4,) + _half_shape(*BIG_PIECES[name]), F32) for name in names],
        [pltpu.SemaphoreType.DMA((n, 4))] * 2,
        build)


def _half_specs(kind, shard):
    half = shard[0] // 2
    if kind == "col":
        full = pl.BlockSpec((half, shard[1]), lambda j, pr: (pr[1], j))
        buf = pl.BlockSpec((None, half, shard[1]), lambda j, pr: (j, 0, 0))
    elif kind == "row":
        full = pl.BlockSpec((half, shard[1]), lambda j, pr: (2 * j + pr[1], 0))
        buf = pl.BlockSpec((None, half, shard[1]), lambda j, pr: (j, 0, 0))
    else:
        full = pl.BlockSpec((half, shard[1], shard[2]), lambda j, pr: (pr[1], j, 0))
        buf = pl.BlockSpec((None, half, shard[1], shard[2]), lambda j, pr: (j, 0, 0, 0))
    return full, buf


def _pair_sum(name, grad, recv, place):
    kind, shard = BIG_PIECES[name]
    full, buf = _half_specs(kind, shard)

    def body(pr, g_ref, r_ref, o_ref):
        o_ref[...] = (g_ref[...] + r_ref[...]).astype(BF16)

    return pl.pallas_call(
        body,
        name="pair_sum_" + name,
        grid_spec=pltpu.PrefetchScalarGridSpec(num_scalar_prefetch=1, grid=(4,), in_specs=[full, buf], out_specs=buf),
        out_shape=jax.ShapeDtypeStruct((4,) + _half_shape(kind, shard), BF16),
        compiler_params=_params(("arbitrary",)),
    )(place, grad, recv)


def _chip_exchange_rider(sums, names):
    n = len(names)

    def build(ins, outs, sems):
        send_sem, recv_sem = sems
        _, _, c, s = _place()
        starts, waits = [], []
        for p in range(n):
            for k in (1, 2, 3):
                s2, dev = _other_chip(s, c, k)
                cp = _remote(ins[p].at[s2], outs[p].at[k - 1], send_sem.at[p, k - 1], recv_sem.at[p, k - 1], dev)
                starts.append(cp)
                waits.append(cp.wait_send)
                waits.append(_remote(outs[p].at[k - 1], outs[p].at[k - 1], send_sem.at[p, k - 1], recv_sem.at[p, k - 1],
                                     dev).wait_recv)
        return starts, waits

    return _Rider(
        [sums[name] for name in names],
        [jax.ShapeDtypeStruct((3,) + _half_shape(*BIG_PIECES[name]), BF16) for name in names],
        [pltpu.SemaphoreType.DMA((n, 3))] * 2,
        build)


def _chip_sum(name, grad, recv_pair, recv_chip, place):
    kind, shard = BIG_PIECES[name]
    half = shard[0] // 2
    tail = tuple(shard[1:])
    zeros = (0,) * len(tail)
    if kind == "col":
        full = pl.BlockSpec((half,) + tail, lambda j, pr: (pr[1], pr[0]))
    elif kind == "row":
        full = pl.BlockSpec((half,) + tail, lambda j, pr: (2 * pr[0] + pr[1], 0))
    else:
        full = pl.BlockSpec((half,) + tail, lambda j, pr: (pr[1], pr[0], 0))
    pair = pl.BlockSpec((None, half) + tail, lambda j, pr: (pr[0], 0) + zeros)
    chip = pl.BlockSpec((3, half) + tail, lambda j, pr: (0, 0) + zeros)
    out = pl.BlockSpec((half,) + tail, lambda j, pr: (pr[1],) + zeros)

    def body(pr, g_ref, rp_ref, rc_ref, o_ref):
        total = g_ref[...] + rp_ref[...]
        for k in range(3):
            total = total + rc_ref[k].astype(F32)
        o_ref[...] = total

    return pl.pallas_call(
        body,
        name="chip_sum_" + name,
        grid_spec=pltpu.PrefetchScalarGridSpec(num_scalar_prefetch=1, grid=(1,), in_specs=[full, pair, chip], out_specs=out),
        out_shape=jax.ShapeDtypeStruct(shard, F32),
        compiler_params=_params(("arbitrary",)),
    )(place, grad, recv_pair, recv_chip)


def _sibling_exchange_rider(halves, names):
    n = len(names)

    def build(ins, outs, sems):
        send_sem, recv_sem = sems
        x, y, c, _ = _place()
        sibling = (x, y, 1 - c)
        starts, waits = [], []
        for p, name in enumerate(names):
            shard = BIG_PIECES[name][1]
            mine = _shard_region(outs[p], shard, c)
            theirs = _shard_region(outs[p], shard, 1 - c)
            cp = _remote(mine, mine, send_sem.at[p], recv_sem.at[p], sibling)
            starts.append(cp)
            waits.append(cp.wait_send)
            waits.append(_remote(theirs, theirs, send_sem.at[p], recv_sem.at[p], sibling).wait_recv)
        return starts, waits

    return _Rider(
        [halves[name] for name in names],
        [jax.ShapeDtypeStruct(BIG_PIECES[name][1], F32) for name in names],
        [pltpu.SemaphoreType.DMA((n,))] * 2,
        build, aliased=True)


FIRST_WEIGHTS = ["meta_tokens", "conv_w"]
WEIGHT_GROUPS = {
    "lru": ["lru_wa", "lru_wx"],
    "branch": ["w_branch_ret", "w_branch_lru", "w_out"],
    "ffn_in": ["w_ffn_in"],
    "ffn_out": ["w_ffn_out"],
}
WEIGHT_SCHEDULE = {
    "in_proj": [("gather", "lru")],
    "retention_fwd": [("forward", "lru"), ("gather", "branch")],
    "lru_fwd": [("forward", "branch"), ("gather", "ffn_in")],
    "mix_fwd": [("forward", "ffn_in"), ("gather", "ffn_out")],
    "ffn_fwd_gate": [("forward", "ffn_out")],
}
GRAD_GROUPS = {
    "ffn": ["w_ffn_in", "w_ffn_out"],
    "mixer": ["w_out", "w_branch_ret", "w_branch_lru", "lru_wa", "lru_wx"],
    "in": ["w_in"],
}
GRAD_SCHEDULE = {
    "mix_bwd": [("pair", "ffn")],
    "retention_bwd": [("chip", "ffn")],
    "lru_bwd": [("sibling", "ffn")],
    "dw_in": [("pair", "mixer")],
    "in_proj_bwd_0": [("chip", "mixer"), ("pair", "in")],
    "in_proj_bwd_1": [("sibling", "mixer"), ("chip", "in")],
}


class _CommPlan:
    def __init__(self, shards, place):
        self.shards, self.place = shards, place
        self.late = {}
        self.recv_pair, self.sums, self.recv_chip, self.halves, self.final = {}, {}, {}, {}, {}

    def _grad_rider(self, stage, group, grads):
        names = GRAD_GROUPS[group]
        if stage == "pair":
            return _pair_exchange_rider(grads, names)
        if stage == "chip":
            return _chip_exchange_rider(self.sums, names)
        return _sibling_exchange_rider(self.halves, names)

    def _grad_after(self, stage, group, outs, grads):
        names = GRAD_GROUPS[group]
        if stage == "pair":
            for n, o in zip(names, outs):
                self.recv_pair[n] = o
                self.sums[n] = _pair_sum(n, grads[n], o, self.place)
        elif stage == "chip":
            for n, o in zip(names, outs):
                self.halves[n] = _chip_sum(n, grads[n], self.recv_pair[n], o, self.place)
        else:
            self.final.update(zip(names, outs))

    def riders(self, host, w, grads):
        if host in WEIGHT_SCHEDULE:
            return [_gather_rider(self.shards, WEIGHT_GROUPS[group]) if stage == "gather"
                    else _forward_rider(self.late, WEIGHT_GROUPS[group]) for stage, group in WEIGHT_SCHEDULE[host]]
        return [self._grad_rider(stage, group, grads) for stage, group in GRAD_SCHEDULE.get(host, [])]

    def after(self, host, rider_outs, w, grads):
        for (stage, group), outs in zip(WEIGHT_SCHEDULE.get(host, []), rider_outs):
            (self.late if stage == "gather" else w).update(zip(WEIGHT_GROUPS[group], outs))
        for (stage, group), outs in zip(GRAD_SCHEDULE.get(host, []), rider_outs):
            self._grad_after(stage, group, outs, grads)

    def finish(self, partial):
        outs, (blocks,) = _run_riders("tail_exchange", [_sibling_exchange_rider(self.halves, GRAD_GROUPS["in"]),
                                                        _small_exchange_rider(partial)])
        self.final.update(zip(GRAD_GROUPS["in"], outs))
        return self.final, blocks


def _adamw_math(w, g, m, v):
    m = ADAM_B1 * m + (1.0 - ADAM_B1) * g
    v = ADAM_B2 * v + (1.0 - ADAM_B2) * (g * g)
    m_hat = m / (1.0 - ADAM_B1 ** ADAM_STEP)
    v_hat = v / (1.0 - ADAM_B2 ** ADAM_STEP)
    delta = -ADAM_LR * (m_hat / (jnp.sqrt(v_hat) + ADAM_EPS) + ADAM_WD * w)
    return delta, m, v


def _adamw(name, g, w, m, v):
    rows, cols = g.shape
    tr = rows // 4 if rows % 32 == 0 else rows

    def body(g_ref, w_ref, m_ref, v_ref, go_ref, d_ref, mo_ref, vo_ref):
        gv = g_ref[...]
        delta, m2, v2 = _adamw_math(w_ref[...], gv, m_ref[...], v_ref[...])
        go_ref[...] = gv
        d_ref[...] = delta
        mo_ref[...] = m2
        vo_ref[...] = v2

    spec = pl.BlockSpec((tr, cols), lambda i: (i, 0))
    return pl.pallas_call(
        body,
        name="adamw_" + name,
        grid=(rows // tr,),
        in_specs=[spec] * 4,
        out_specs=[spec] * 4,
        out_shape=[jax.ShapeDtypeStruct((rows, cols), F32)] * 4,
        compiler_params=_params(("arbitrary",)),
    )(g, w, m, v)


SMALL_ROWS = 48
VEC_ROWS = {"final_norm_w": 1, "ffn_norm_w": 8, "mix_norm_w": 16, "conv_b": 24, "lru_ba": 25, "lru_bx": 26, "lru_lambda": 27}
VEC_NAMES = list(VEC_ROWS)
CONV_W_ROW = 28
META_ROW = 32


def _small_exchange_rider(partial):
    def build(ins, outs, sems):
        local_sem, send_sem, recv_sem = sems
        _, _, c, s = _place()
        me = 2 * s + c
        cp = pltpu.make_async_copy(ins[0], outs[0].at[me], local_sem)
        starts, waits = [cp], [cp.wait]
        for k in range(1, 8):
            peer = jnp.bitwise_xor(me, k)
            dev = (peer // 4, (peer // 2) % 2, peer % 2)
            rd = _remote(ins[0], outs[0].at[me], send_sem.at[k - 1], recv_sem.at[k - 1], dev)
            starts.append(rd)
            waits.append(rd.wait_send)
            waits.append(_remote(ins[0], outs[0].at[peer], send_sem.at[k - 1], recv_sem.at[k - 1], dev).wait_recv)
        return starts, waits

    return _Rider([partial], [jax.ShapeDtypeStruct((8, SMALL_ROWS, D), F32)],
                  [pltpu.SemaphoreType.DMA, pltpu.SemaphoreType.DMA((7,)), pltpu.SemaphoreType.DMA((7,))], build)


def _small_update(blocks, place, vecs, conv, meta):
    nvec = len(VEC_NAMES)
    qcols = D // 4

    def in_order(ref):
        total = ref[0]
        for d in range(1, 8):
            total = total + ref[d]
        return total

    def body(pr, blocks_ref, cols_ref, *refs):
        vec_refs = refs[:3 * nvec]
        conv_refs = refs[3 * nvec:3 * nvec + 3]
        meta_refs = refs[3 * nvec + 3:3 * nvec + 6]
        outs = refs[3 * nvec + 6:]
        loss_ref, vec_out, conv_out, meta_out = outs[0], outs[1:5], outs[5:9], outs[9:13]
        tot, col = in_order(blocks_ref), in_order(cols_ref)
        loss_ref[...] = jnp.sum(tot[0:1, :], axis=1, keepdims=True)
        for o in vec_out:
            o[...] = jnp.zeros_like(o)
        for j, name in enumerate(VEC_NAMES):
            w, m, v = (r[...] for r in vec_refs[3 * j:3 * j + 3])
            g = tot[VEC_ROWS[name]:VEC_ROWS[name] + 1, :]
            if name == "lru_lambda":
                g = -g / (1.0 + jnp.exp(w))
            for o, val in zip(vec_out, (g,) + _adamw_math(w, g, m, v)):
                o[j:j + 1, :] = val
        for row, n_rows, ins, group in ((CONV_W_ROW, 4, conv_refs, conv_out), (META_ROW, N_META, meta_refs, meta_out)):
            g = col[row:row + n_rows, :]
            w, m, v = (r[...] for r in ins)
            for o, val in zip(group, (g,) + _adamw_math(w, g, m, v)):
                o[...] = val

    whole = lambda shape: pl.BlockSpec(shape, lambda i, pr: (0,) * len(shape))
    in_specs = [whole((8, SMALL_ROWS, D)), pl.BlockSpec((8, SMALL_ROWS, qcols), lambda i, pr: (0, 0, pr[0]))]
    in_specs += [whole((1, D))] * (3 * nvec) + [whole((4, qcols))] * 3 + [whole((N_META, qcols))] * 3
    out_shapes = [(1, 1)] + [(8, D)] * 4 + [(4, qcols)] * 4 + [(N_META, qcols)] * 4
    return pl.pallas_call(
        body,
        name="small_update",
        grid_spec=pltpu.PrefetchScalarGridSpec(num_scalar_prefetch=1, grid=(1,), in_specs=in_specs,
                                               out_specs=[whole(s) for s in out_shapes]),
        out_shape=[jax.ShapeDtypeStruct(s, F32) for s in out_shapes],
        compiler_params=_params(("arbitrary",)),
    )(place, blocks, blocks, *[a for t in vecs for a in t], *conv, *meta)


WEIGHT_ORDER = ["meta_tokens", "mix_norm_w", "w_in", "conv_w", "conv_b", "lru_wa", "lru_ba", "lru_wx", "lru_bx", "lru_lambda",
                "w_branch_ret", "w_branch_lru", "w_out", "ffn_norm_w", "w_ffn_in", "w_ffn_out", "final_norm_w"]


def kernel(x, meta_tokens, mix_norm_w, w_in, conv_w, conv_b, lru_wa, lru_ba, lru_wx, lru_bx, lru_lambda, w_branch_ret, w_branch_lru, w_out, ffn_norm_w, w_ffn_in, w_ffn_out, final_norm_w, loss_target, m_meta_tokens, m_mix_norm_w, m_w_in, m_conv_w, m_conv_b, m_lru_wa, m_lru_ba, m_lru_wx, m_lru_bx, m_lru_lambda, m_w_branch_ret, m_w_branch_lru, m_w_out, m_ffn_norm_w, m_w_ffn_in, m_w_ffn_out, m_final_norm_w, v_meta_tokens, v_mix_norm_w, v_w_in, v_conv_w, v_conv_b, v_lru_wa, v_lru_ba, v_lru_wx, v_lru_bx, v_lru_lambda, v_w_branch_ret, v_w_branch_lru, v_w_out, v_ffn_norm_w, v_w_ffn_in, v_w_ffn_out, v_final_norm_w):
    args = locals()
    wts = {n: args[n] for n in WEIGHT_ORDER}
    mom = {n: args["m_" + n] for n in WEIGHT_ORDER}
    var = {n: args["v_" + n] for n in WEIGHT_ORDER}
    place = jnp.stack([2 * lax.axis_index("x") + lax.axis_index("y"), lax.axis_index("c")]).astype(jnp.int32)

    shards = {n: wts[n][0].astype(BF16) for n in BIG_PIECES}
    shards["meta_tokens"] = wts["meta_tokens"]
    shards["conv_w"] = wts["conv_w"][0]
    plan = _CommPlan(shards, place)
    (first,) = _run_riders("gather_first", [_gather_rider(shards, FIRST_WEIGHTS)])
    w = dict(zip(FIRST_WEIGHTS, first))
    for n in VEC_NAMES:
        w[n] = wts[n].reshape(1, D)
    chips = jnp.bitwise_xor(place[0], jnp.arange(4, dtype=jnp.int32))
    w["route"] = jnp.concatenate([place, jnp.stack([2 * chips, 2 * chips + 1], axis=1).reshape(8)])
    w["w_in_shard"] = shards["w_in"]

    grad_x, grad_head, _, stats = _local_step(x[0], loss_target[0], w, plan)
    partial = jnp.concatenate(stats + [grad_head[PAD_ROWS:]], axis=0)
    shard_grads, blocks = plan.finish(partial)

    out = {}
    for n in BIG_PIECES:
        shape2d = (-1, wts[n].shape[-1])
        res = _adamw(n, *[a.reshape(shape2d) for a in (shard_grads[n], wts[n], mom[n], var[n])])
        out[n] = [r.reshape(wts[n].shape) for r in res]

    vecs = [tuple(a[n].reshape(1, D) for a in (wts, mom, var)) for n in VEC_NAMES]
    conv = tuple(a["conv_w"][0] for a in (wts, mom, var))
    meta = tuple(a["meta_tokens"] for a in (wts, mom, var))
    res = _small_update(blocks, place, vecs, conv, meta)
    loss = res[0].reshape(())
    for j, n in enumerate(VEC_NAMES):
        out[n] = [r[j].reshape(wts[n].shape) for r in res[1:5]]
    out["conv_w"] = [r.reshape(wts["conv_w"].shape) for r in res[5:9]]
    out["meta_tokens"] = list(res[9:13])

    return (loss, grad_x.reshape(x.shape)) + tuple(out[n][kind] for kind in range(4) for n in WEIGHT_ORDER)
```

```python
import functools
import math

import jax
import jax.numpy as jnp
from jax import lax
from jax.experimental import pallas as pl
from jax.experimental.pallas import tpu as pltpu

F32 = jnp.float32
BF16 = jnp.bfloat16

D = 1024
HEADS = 8
DH = 128
CHUNK = 256
N_META = 16
FRONT = 256
PAD_ROWS = FRONT - N_META
LRU_BLOCKS = 4
LRU_BLOCK = 256
LRU_C = 8.0
FFN = 2816
FFN_HALF = FFN // 2
IN_COLS = 8 * D
ROPE_BASE = 10000.0
EPS = 1e-6
QK_SCALE = DH ** -0.5

ADAM_LR = 0.001
ADAM_B1 = 0.9
ADAM_B2 = 0.999
ADAM_EPS = 1e-08
ADAM_WD = 0.01
ADAM_STEP = 10

TM = 256
TM_HEAVY = 768
FFN_BLOCK = FFN // 2
FFN_SUB = 768
TM_MIX_BWD = 256
TM_LRU = 384
TM_MIX_FWD = 384
TM_IN_PROJ = 1408
VMEM_LIMIT = 60 * 1024 * 1024
TN_VMEM_BUDGET = 40 * 1024 * 1024

NT_DIMS = (((1,), (1,)), ((), ()))
TN_DIMS = (((0,), (0,)), ((), ()))
MESH = pl.DeviceIdType.MESH


def _params(sem=None):
    if sem is None:
        return pltpu.CompilerParams(vmem_limit_bytes=VMEM_LIMIT)
    return pltpu.CompilerParams(dimension_semantics=sem, vmem_limit_bytes=VMEM_LIMIT)


def _dot(a, b):
    return jnp.dot(a, b, preferred_element_type=F32)


def _dot_nt(a, b):
    return lax.dot_general(a, b, NT_DIMS, preferred_element_type=F32)


def _dot_tn(a, b):
    return lax.dot_general(a, b, TN_DIMS, preferred_element_type=F32)


def _sigmoid(z):
    return 1.0 / (1.0 + jnp.exp(-z))


def _log1p(x):
    return jnp.where(x < 1e-3, x * (1.0 - x * (0.5 - x * (1.0 / 3.0))), jnp.log(1.0 + x))


def _softplus(x):
    return jnp.maximum(x, 0.0) + _log1p(jnp.exp(-jnp.abs(x)))


def _neg_expm1(x):
    series = -x * (1.0 + x * (0.5 + x * (1.0 / 6.0 + x * (1.0 / 24.0 + x * (1.0 / 120.0)))))
    return jnp.where(x > -0.05, series, 1.0 - jnp.exp(x))


_GELU_K = math.sqrt(2.0 / math.pi)


def _gelu_and_grad(x):
    inner = _GELU_K * (x + 0.044715 * x * x * x)
    t = jnp.tanh(inner)
    val = 0.5 * x * (1.0 + t)
    grad = 0.5 * (1.0 + t) + 0.5 * x * (1.0 - t * t) * _GELU_K * (1.0 + 3.0 * 0.044715 * x * x)
    return val, grad


def _row_ids(i, rows, shape):
    return i * rows + lax.broadcasted_iota(jnp.int32, shape, 0)


def _rope_tables(rows):
    inv_freq = ROPE_BASE ** (-jnp.arange(0, DH, 2, dtype=F32) / DH)
    block_pos = jnp.arange(rows // FRONT, dtype=jnp.int32) * FRONT - PAD_ROWS
    coarse = block_pos.astype(F32)[:, None] * inv_freq[None, :]
    fine = jnp.arange(FRONT, dtype=F32)[:, None] * inv_freq[None, :]
    ca, sa = jnp.cos(coarse)[:, None, :], jnp.sin(coarse)[:, None, :]
    cb, sb = jnp.cos(fine)[None, :, :], jnp.sin(fine)[None, :, :]
    cos = (ca * cb - sa * sb).reshape(rows, DH // 2)
    sin = (sa * cb + ca * sb).reshape(rows, DH // 2)
    return jnp.concatenate([cos, cos], axis=1), jnp.concatenate([-sin, sin], axis=1)


def _decay_tables():
    log_g = jnp.log(1.0 - 2.0 ** (-5.0 - jnp.arange(HEADS, dtype=F32)))
    idx = jnp.arange(CHUNK, dtype=F32)
    diff = idx[:, None] - idx[None, :]
    intra = jnp.where(diff[None] >= 0, jnp.exp(jnp.maximum(diff, 0.0)[None] * log_g[:, None, None]), 0.0)
    q_decay = jnp.exp((idx + 1.0)[:, None] * log_g[None, :])
    k_decay = jnp.exp((CHUNK - 1.0 - idx)[:, None] * log_g[None, :])
    chunk_decay = jnp.exp(CHUNK * log_g)
    wide = lambda a: jnp.repeat(a, DH, axis=-1)
    return intra, jnp.swapaxes(intra, 1, 2), wide(q_decay), wide(k_decay), wide(chunk_decay[None, :])


def _norm1(head, x2d, norm_w, riders=()):
    rows = FRONT + x2d.shape[0]
    tm = TM_IN_PROJ if rows % TM_IN_PROJ == 0 else TM_MIX_FWD
    nt = rows // tm

    def body(head_hbm, x_hbm, nw_ref, u_ref, rstd_ref, h0_sc, h0_sem):
        slot = _frame_rows(head_hbm, x_hbm, h0_sc, h0_sem, pl.program_id(0), tm, nt)
        hv = h0_sc[slot]
        rs = lax.rsqrt(jnp.mean(hv * hv, axis=-1, keepdims=True) + EPS)
        u_ref[...] = ((hv * rs) * nw_ref[...]).astype(BF16)
        rstd_ref[...] = rs

    return _hosted_call(
        body,
        name="norm1",
        grid=(nt,),
        in_specs=[_ANY, _ANY, pl.BlockSpec((1, D), lambda i: (0, 0))],
        out_specs=[pl.BlockSpec((tm, D), lambda i: (i, 0)), pl.BlockSpec((tm, 1), lambda i: (i, 0))],
        out_shape=[jax.ShapeDtypeStruct((rows, D), BF16), jax.ShapeDtypeStruct((rows, 1), F32)],
        scratch_shapes=[pltpu.VMEM((2, tm, D), F32), pltpu.SemaphoreType.DMA((3,))],
        args=(head, x2d, norm_w),
        riders=riders,
    )


def _heavy_tile(rows):
    return TM_HEAVY if rows % TM_HEAVY == 0 else TM


def _lru_tile(rows):
    return TM_LRU if rows % TM_LRU == 0 else TM


def _swap_3_4(group):
    return jnp.where(group == 3, 4, jnp.where(group == 4, 3, group))


LRU_IN_COL = 3
GATES_COL = 1


def _in_proj(u, w_shard, route, cos_t, sin_t, riders=()):
    rows = u.shape[0]
    tm = TM_IN_PROJ if rows % TM_IN_PROJ == 0 else _heavy_tile(rows)
    nt = rows // tm
    kind, shard = BIG_PIECES["w_in"]

    def body(route_ref, u_hbm, wsh_ref, cos_ref, sin_ref, proj_ref, wfull_ref,
             u_sc, w_sc, u_sem, w_sem, local_sem, ici_send, ici_recv, fwd_send, fwd_recv):
        g, i = pl.program_id(0), pl.program_id(1)
        s, c = route_ref[0], route_ref[1]
        gid = route_ref[2 + g]
        sibling = (s // 2, s % 2, 1 - c)
        local = pltpu.make_async_copy(wsh_ref, _full_region(wfull_ref, kind, shard, s, None), local_sem)
        u_copies = [pltpu.make_async_copy(u_hbm.at[pl.ds(t * tm, tm)], u_sc.at[t], u_sem.at[t]) for t in range(nt)]
        sends, arrivals = [], []
        for k in (1, 2, 3):
            s2, dev = _other_chip(s, c, k)
            sends.append(_remote(_shard_region(wsh_ref, shard, c), _full_region(wfull_ref, kind, shard, s, c),
                                 ici_send.at[k - 1], ici_recv.at[k - 1], dev))
            mine = _full_region(wfull_ref, kind, shard, s2, c)
            theirs = _full_region(wfull_ref, kind, shard, s2, 1 - c)
            arrivals.append((_remote(mine, mine, ici_send.at[k - 1], ici_recv.at[k - 1], dev),
                             _remote(mine, mine, fwd_send.at[k - 1], fwd_recv.at[k - 1], sibling),
                             _remote(theirs, theirs, fwd_send.at[k - 1], fwd_recv.at[k - 1], sibling)))

        slot = g % 2
        last_tile = i == nt - 1

        def block_copy(src, col, to_slot):
            return pltpu.make_async_copy(src.at[:, pl.ds(pl.multiple_of(col * D, D), D)], w_sc.at[to_slot],
                                         w_sem.at[to_slot])

        @pl.when(jnp.logical_and(g == 0, i == 0))
        def _():
            for cp in sends:
                cp.start()
            local.start()
            for cp in u_copies:
                cp.start()
            cp = block_copy(wsh_ref, 0, 0)
            cp.start()
            cp.wait()

        @pl.when(jnp.logical_and(last_tile, g == 0))
        def _():
            block_copy(wsh_ref, 1, 1).start()

        for k, (arrived, forward, forwarded) in zip((1, 2, 3), arrivals):
            @pl.when(jnp.logical_and(last_tile, g == 2 * k - 1))
            def _():
                arrived.wait_recv()
                forward.start()
                forwarded.wait_recv()
                block_copy(wfull_ref, route_ref[2 + 2 * k], 0).start()

            @pl.when(jnp.logical_and(last_tile, g == 2 * k))
            def _():
                block_copy(wfull_ref, route_ref[3 + 2 * k], 1).start()

        @pl.when(jnp.logical_and(i == 0, g > 0))
        def _():
            block_copy(wfull_ref, 0, slot).wait()

        for t in range(nt):
            @pl.when(jnp.logical_and(g == 0, i == t))
            def _():
                u_copies[t].wait()

        acc = _dot(u_sc[i], w_sc[slot])

        @pl.when(gid < 2)
        def _():
            scale = jnp.where(gid == 1, QK_SCALE, 1.0).astype(F32)
            for h in range(HEADS):
                sl = slice(h * DH, (h + 1) * DH)
                blk = acc[:, sl]
                out = (blk * cos_ref[...] + pltpu.roll(blk, DH // 2, axis=1) * sin_ref[...]) * scale
                proj_ref[:, sl] = out.astype(BF16)

        @pl.when(gid >= 2)
        def _():
            proj_ref[...] = acc.astype(BF16)

        @pl.when(jnp.logical_and(g == 7, i == nt - 1))
        def _():
            local.wait()
            for cp in sends:
                cp.wait_send()
            for _, forward, _ in arrivals:
                forward.wait_send()

    return _hosted_call(
        body,
        name="in_proj",
        grid=(8, nt),
        in_specs=[
            _ANY, _ANY,
            pl.BlockSpec((tm, DH), lambda g, i, rt: (i, 0)),
            pl.BlockSpec((tm, DH), lambda g, i, rt: (i, 0)),
        ],
        out_specs=[pl.BlockSpec((tm, D), lambda g, i, rt: (i, _swap_3_4(rt[2 + g]))), _ANY],
        out_shape=[jax.ShapeDtypeStruct((rows, IN_COLS), BF16), jax.ShapeDtypeStruct((D, IN_COLS), BF16)],
        scratch_shapes=[
            pltpu.VMEM((nt, tm, D), BF16), pltpu.VMEM((2, D, D), BF16),
            pltpu.SemaphoreType.DMA((nt,)), pltpu.SemaphoreType.DMA((2,)), pltpu.SemaphoreType.DMA,
            pltpu.SemaphoreType.DMA((3,)), pltpu.SemaphoreType.DMA((3,)),
            pltpu.SemaphoreType.DMA((3,)), pltpu.SemaphoreType.DMA((3,)),
        ],
        args=(u, w_shard, cos_t, sin_t),
        riders=riders,
        prefetch=route,
        riders_after_body=True,
    )


def _retention_fwd(proj_bf, intra, q_dec, k_dec, c_dec, riders=()):
    rows = proj_bf.shape[0]
    nc = rows // CHUNK

    def body(q_ref, k_ref, v_ref, m_ref, qd_ref, kd_ref, cd_ref, o_ref, sprev_ref, s_sc):
        @pl.when(pl.program_id(0) == 0)
        def _():
            s_sc[...] = jnp.zeros_like(s_sc)

        states = [s_sc[h] for h in range(HEADS)]
        results = []
        for h in range(HEADS):
            sl = slice(h * DH, (h + 1) * DH)
            q, k, v = q_ref[:, sl], k_ref[:, sl], v_ref[:, sl]
            state = states[h]
            state_b = state.astype(BF16)
            s = _dot_nt(q, k) * m_ref[h]
            inner = _dot(s.astype(BF16), v)
            cross = _dot(q, state_b) * qd_ref[:, sl]
            k_scaled = (k.astype(F32) * kd_ref[:, sl]).astype(BF16)
            results.append((state_b, (inner + cross).astype(BF16), state * cd_ref[:, sl] + _dot_tn(k_scaled, v)))

        for h, (state_b, o_h, new_state) in enumerate(results):
            sprev_ref[0, h] = state_b
            o_ref[:, h * DH:(h + 1) * DH] = o_h
            s_sc[h] = new_state

    chunk_spec = lambda col: pl.BlockSpec((CHUNK, D), lambda c: (c, col))
    const2 = lambda shape: pl.BlockSpec(shape, lambda c: (0, 0))
    return _hosted_call(
        body,
        name="retention_fwd",
        grid=(nc,),
        in_specs=[
            chunk_spec(0), chunk_spec(1), chunk_spec(2),
            pl.BlockSpec((HEADS, CHUNK, CHUNK), lambda c: (0, 0, 0)),
            const2((CHUNK, D)), const2((CHUNK, D)), const2((1, D)),
        ],
        out_specs=[
            pl.BlockSpec((CHUNK, D), lambda c: (c, 0)),
            pl.BlockSpec((1, HEADS, DH, DH), lambda c: (c, 0, 0, 0)),
        ],
        out_shape=[
            jax.ShapeDtypeStruct((rows, D), BF16),
            jax.ShapeDtypeStruct((nc, HEADS, DH, DH), BF16),
        ],
        scratch_shapes=[pltpu.VMEM((HEADS, DH, DH), F32)],
        args=(proj_bf, proj_bf, proj_bf, intra, q_dec, k_dec, c_dec),
        riders=riders,
    )


def _shift_down(x, first8_prev, d):
    rolled = pltpu.roll(x, d, axis=0)
    head = pltpu.roll(jnp.concatenate([first8_prev, x[0:8]], axis=0), d, axis=0)[8:16]
    return rolled, head


def _conv_and_gates(x, prev8, cw_ref, cb_ref, wa_ref, wx_ref, ba_ref, bx_ref, lam_ref, c_sc):
    cw = cw_ref[...]
    conv = cb_ref[...] + cw[3:4] * x
    head = cb_ref[...] + cw[3:4] * x[0:8]
    for d in (1, 2, 3):
        rolled, hd = _shift_down(x, prev8, d)
        conv = conv + cw[3 - d:4 - d] * rolled
        head = head + cw[3 - d:4 - d] * hd
    c_sc[...] = conv
    c_sc[0:8, :] = head
    c = c_sc[...]
    zr, zi = [], []
    for g in range(LRU_BLOCKS):
        sl = slice(g * LRU_BLOCK, (g + 1) * LRU_BLOCK)
        cg = c[:, sl].astype(BF16)
        zr.append(_dot(cg, wa_ref[g]))
        zi.append(_dot(cg, wx_ref[g]))
    r = _sigmoid(jnp.concatenate(zr, axis=1) + ba_ref[...])
    gate_i = _sigmoid(jnp.concatenate(zi, axis=1) + bx_ref[...])
    sp = _softplus(-lam_ref[...])
    log_a = (-LRU_C) * r * sp
    a = jnp.exp(log_a)
    mult = jnp.sqrt(_neg_expm1(2.0 * log_a))
    return c, r, gate_i, a, mult, log_a


def _lru_fwd(proj, conv_w, conv_b, wa, wx, ba, bx, lam, riders=()):
    rows = proj.shape[0]
    TM = _lru_tile(rows)
    nt = rows // TM

    def body(x_ref, cw_ref, cb_ref, wa_ref, wx_ref, ba_ref, bx_ref, lam_ref,
             h_ref, c_ref, r_ref, i_ref, la_ref, mult_ref, prev_sc, carry_sc, c_sc, a_sc, u_sc, h_sc):
        i = pl.program_id(0)

        @pl.when(i == 0)
        def _():
            prev_sc[...] = jnp.zeros_like(prev_sc)
            carry_sc[...] = jnp.zeros_like(carry_sc)

        x = x_ref[...].astype(F32)
        c, r, gate_i, a, mult, log_a = _conv_and_gates(x, prev_sc[...], cw_ref, cb_ref, wa_ref, wx_ref, ba_ref, bx_ref,
                                                       lam_ref, c_sc)
        for ref, val in ((c_ref, c), (r_ref, r), (i_ref, gate_i), (la_ref, log_a), (mult_ref, mult)):
            ref[...] = val.astype(BF16)
        prev_sc[...] = x[TM - 8:TM]
        valid = _row_ids(i, TM, (TM, D)) >= PAD_ROWS
        a_sc[...] = a
        u_sc[...] = jnp.where(valid, mult * gate_i * c, 0.0)
        row8 = lax.broadcasted_iota(jnp.int32, (8, D), 0)

        def group(gi, hprev):
            r0 = pl.multiple_of(gi * 8, 8)
            aa = a_sc[pl.ds(r0, 8), :]
            uu = u_sc[pl.ds(r0, 8), :]
            for d in (1, 2, 4):
                a_sh = jnp.where(row8 >= d, pltpu.roll(aa, d, axis=0), 1.0)
                u_sh = jnp.where(row8 >= d, pltpu.roll(uu, d, axis=0), 0.0)
                uu = uu + aa * u_sh
                aa = aa * a_sh
            hb = aa * hprev + uu
            h_sc[pl.ds(r0, 8), :] = hb
            return hb[7:8, :]

        hlast = lax.fori_loop(0, TM // 8, group, carry_sc[0:1, :])
        carry_sc[0:1, :] = hlast
        h_ref[...] = h_sc[...].astype(BF16)

    vec = pl.BlockSpec((1, D), lambda i: (0, 0))
    wspec = pl.BlockSpec((LRU_BLOCKS, LRU_BLOCK, LRU_BLOCK), lambda i: (0, 0, 0))
    return _hosted_call(
        body,
        name="lru_fwd",
        grid=(nt,),
        in_specs=[
            pl.BlockSpec((TM, D), lambda i: (i, LRU_IN_COL)),
            pl.BlockSpec((4, D), lambda i: (0, 0)),
            vec, wspec, wspec, vec, vec, vec,
        ],
        out_specs=[pl.BlockSpec((TM, D), lambda i: (i, 0)) for _ in range(6)],
        out_shape=[jax.ShapeDtypeStruct((rows, D), BF16) for _ in range(6)],
        scratch_shapes=[
            pltpu.VMEM((8, D), F32), pltpu.VMEM((8, D), F32),
            pltpu.VMEM((TM, D), F32), pltpu.VMEM((TM, D), F32), pltpu.VMEM((TM, D), F32), pltpu.VMEM((TM, D), F32),
        ],
        args=(proj, conv_w, conv_b, wa, wx, ba, bx, lam),
        riders=riders,
    )


def _group_norm(o):
    outs, rstds = [], []
    for h in range(HEADS):
        oh = o[:, h * DH:(h + 1) * DH]
        rs = lax.rsqrt(jnp.mean(oh * oh, axis=-1, keepdims=True) + EPS)
        outs.append(oh * rs)
        rstds.append(rs)
    return jnp.concatenate(outs, axis=1), rstds


def _frame_rows(head_hbm, x_hbm, buf, sems, i, tm, nt):
    slot = i % 2

    @pl.when(i == 0)
    def _():
        first = [pltpu.make_async_copy(head_hbm, buf.at[0, pl.ds(0, FRONT)], sems.at[2]),
                 pltpu.make_async_copy(x_hbm.at[pl.ds(0, tm - FRONT)], buf.at[0, pl.ds(FRONT, tm - FRONT)], sems.at[0])]
        for cp in first:
            cp.start()
        for cp in first:
            cp.wait()

    @pl.when(i + 1 < nt)
    def _():
        start = pl.multiple_of((i + 1) * tm - FRONT, 128)
        pltpu.make_async_copy(x_hbm.at[pl.ds(start, tm)], buf.at[1 - slot], sems.at[1 - slot]).start()

    @pl.when(i > 0)
    def _():
        pltpu.make_async_copy(x_hbm.at[pl.ds(0, tm)], buf.at[slot], sems.at[slot]).wait()

    return slot


def _mix_fwd(head, x2d, o, proj, h_lru, w_br, w_bl, w_o, ffn_norm_w, riders=()):
    rows = o.shape[0]
    tm = TM_MIX_FWD
    assert rows % tm == 0 and tm > FRONT
    nt = rows // tm

    def body(head_hbm, x_hbm, o_ref, gates_ref, hl_ref, wbr_ref, wbl_ref, wo_ref, nw_ref,
             yret_ref, ylru_ref, h1_ref, u2_ref, rstd_ref, h0_sc, h0_sem):
        i = pl.program_id(0)
        slot = _frame_rows(head_hbm, x_hbm, h0_sc, h0_sem, i, tm, nt)
        gate = lambda j: gates_ref[:, j * D:(j + 1) * D].astype(F32)
        on, _ = _group_norm(o_ref[...].astype(F32))
        gret = gate(0)
        a_ret = (gret * _sigmoid(gret) * on).astype(BF16)
        y_ret = _dot(a_ret, wbr_ref[...])
        gl, _ = _gelu_and_grad(gate(1))
        a_lru = (gl * hl_ref[...].astype(F32)).astype(BF16)
        y_lru = _dot(a_lru, wbl_ref[...])
        mixed = (_sigmoid(gate(2)) * y_ret + _sigmoid(gate(3)) * y_lru).astype(BF16)
        delta = _dot(mixed, wo_ref[...])
        yret_ref[...] = y_ret.astype(BF16)
        ylru_ref[...] = y_lru.astype(BF16)
        h1 = h0_sc[slot] + delta
        rs = lax.rsqrt(jnp.mean(h1 * h1, axis=-1, keepdims=True) + EPS)
        h1_ref[...] = h1
        u2_ref[...] = ((h1 * rs) * nw_ref[...]).astype(BF16)
        rstd_ref[...] = rs

    tile = lambda col: pl.BlockSpec((tm, D), lambda i: (i, col))
    wspec = pl.BlockSpec((D, D), lambda i: (0, 0))
    return _hosted_call(
        body,
        name="mix_fwd",
        grid=(nt,),
        in_specs=[
            _ANY, _ANY,
            tile(0), pl.BlockSpec((tm, 4 * D), lambda i: (i, GATES_COL)), tile(0),
            wspec, wspec, wspec,
            pl.BlockSpec((1, D), lambda i: (0, 0)),
        ],
        out_specs=[tile(0), tile(0), tile(0), tile(0), pl.BlockSpec((tm, 1), lambda i: (i, 0))],
        out_shape=[
            jax.ShapeDtypeStruct((rows, D), BF16), jax.ShapeDtypeStruct((rows, D), BF16),
            jax.ShapeDtypeStruct((rows, D), F32), jax.ShapeDtypeStruct((rows, D), BF16),
            jax.ShapeDtypeStruct((rows, 1), F32),
        ],
        scratch_shapes=[pltpu.VMEM((2, tm, D), F32), pltpu.SemaphoreType.DMA((3,))],
        args=(head, x2d, o, proj, h_lru, w_br, w_bl, w_o, ffn_norm_w),
        riders=riders,
    )


def _ffn_fwd_gate(u2, w_ffn_in, riders=()):
    rows = u2.shape[0]
    tm = _heavy_tile(rows)
    nb = FFN // FFN_BLOCK
    hid = lambda: pl.BlockSpec((tm, FFN_BLOCK), lambda i, j: (i, j))

    def body(u2_ref, w_hbm, silu_ref, dsilu_ref, act_ref, w_sc, w_sem):
        j = pl.program_id(1)

        @pl.when(jnp.logical_and(pl.program_id(0) == 0, j == 0))
        def _():
            cp = pltpu.make_async_copy(w_hbm, w_sc, w_sem)
            cp.start()
            cp.wait()

        u2 = u2_ref[...]
        for a, b in ((0, FFN_SUB), (FFN_SUB, FFN_BLOCK)):
            col = pl.multiple_of(j * FFN_BLOCK + a, 128)
            g = _dot(u2, w_sc[:, pl.ds(col, b - a)])
            up = _dot(u2, w_sc[:, pl.ds(pl.multiple_of(FFN + col, 128), b - a)])
            sg = _sigmoid(g)
            silu = g * sg
            silu_ref[:, a:b] = silu.astype(BF16)
            dsilu_ref[:, a:b] = (up * (sg * (1.0 + g * (1.0 - sg)))).astype(BF16)
            act_ref[:, a:b] = (silu * up).astype(BF16)

    return _hosted_call(
        body,
        name="ffn_fwd_gate",
        grid=(rows // tm, nb),
        in_specs=[pl.BlockSpec((tm, D), lambda i, j: (i, 0)), _ANY],
        out_specs=[hid(), hid(), hid()],
        out_shape=[jax.ShapeDtypeStruct((rows, FFN), BF16)] * 3,
        scratch_shapes=[pltpu.VMEM((D, 2 * FFN), BF16), pltpu.SemaphoreType.DMA],
        args=(u2, w_ffn_in),
        riders=riders,
    )


def _ffn_out_loss(act, h1, w_ffn_out, target, final_norm_w):
    rows = act.shape[0]
    tm = _heavy_tile(rows)
    nt = rows // tm

    def body(act_ref, h1_ref, wo_ref, fnw_ref, tgt_hbm, dh2_ref, stats_ref, tgt_sc, tgt_sem):
        i = pl.program_id(0)
        slot = i % 2

        @pl.when(i == 0)
        def _():
            stats_ref[...] = jnp.zeros_like(stats_ref)
            tgt_sc[0, 0:FRONT, :] = jnp.zeros((FRONT, D), F32)
            cp = pltpu.make_async_copy(tgt_hbm.at[pl.ds(0, tm - FRONT)], tgt_sc.at[0, pl.ds(FRONT, tm - FRONT)], tgt_sem.at[0])
            cp.start()
            cp.wait()

        @pl.when(i + 1 < nt)
        def _():
            start = pl.multiple_of((i + 1) * tm - FRONT, FRONT)
            pltpu.make_async_copy(tgt_hbm.at[pl.ds(start, tm)], tgt_sc.at[1 - slot], tgt_sem.at[1 - slot]).start()

        @pl.when(i > 0)
        def _():
            pltpu.make_async_copy(tgt_hbm.at[pl.ds(0, tm)], tgt_sc.at[slot], tgt_sem.at[slot]).wait()

        h2 = h1_ref[...] + _dot(act_ref[...], wo_ref[...])
        rs = lax.rsqrt(jnp.mean(h2 * h2, axis=-1, keepdims=True) + EPS)
        n = h2 * rs
        fnw = fnw_ref[...]
        valid = _row_ids(i, tm, (tm, D)) >= FRONT
        diff = jnp.where(valid, n * fnw - tgt_sc[slot], 0.0)
        dy = diff * (1.0 / D)
        stats_ref[0:1, :] += (0.5 / D) * jnp.sum(diff * diff, axis=0, keepdims=True)
        stats_ref[1:2, :] += jnp.sum(dy * n, axis=0, keepdims=True)
        dn = dy * fnw
        dh2_ref[...] = rs * (dn - n * jnp.mean(dn * n, axis=-1, keepdims=True))

    return pl.pallas_call(
        body,
        name="ffn_out_loss",
        grid=(nt,),
        in_specs=[
            pl.BlockSpec((tm, FFN), lambda i: (i, 0)),
            pl.BlockSpec((tm, D), lambda i: (i, 0)),
            pl.BlockSpec((FFN, D), lambda i: (0, 0)),
            pl.BlockSpec((1, D), lambda i: (0, 0)),
            _ANY,
        ],
        out_specs=[pl.BlockSpec((tm, D), lambda i: (i, 0)), pl.BlockSpec((8, D), lambda i: (0, 0))],
        out_shape=[jax.ShapeDtypeStruct((rows, D), F32), jax.ShapeDtypeStruct((8, D), F32)],
        scratch_shapes=[pltpu.VMEM((2, tm, D), F32), pltpu.SemaphoreType.DMA((2,))],
        compiler_params=_params(("arbitrary",)),
    )(act, h1, w_ffn_out, final_norm_w, target)


def _ffn_bwd(dh2, silu, dsilu, h1, rstd2, w_ffn_in, w_ffn_out, ffn_norm_w):
    rows = dh2.shape[0]
    tm = _heavy_tile(rows)
    nb = FFN // FFN_BLOCK
    blk = lambda: pl.BlockSpec((tm, FFN_BLOCK), lambda i, j: (i, j))

    def gate_body(dh2_ref, silu_ref, dsilu_ref, wo_hbm, dg_ref, dup_ref, dh2b_ref, wo_sc, wo_sem):
        j = pl.program_id(1)

        @pl.when(jnp.logical_and(pl.program_id(0) == 0, j == 0))
        def _():
            cp = pltpu.make_async_copy(wo_hbm, wo_sc, wo_sem)
            cp.start()
            cp.wait()

        @pl.when(j == 0)
        def _():
            dh2b_ref[...] = dh2_ref[...].astype(BF16)

        dact = _dot_nt(dh2b_ref[...], wo_sc[pl.ds(pl.multiple_of(j * FFN_BLOCK, 128), FFN_BLOCK), :])
        dup_ref[...] = (dact * silu_ref[...].astype(F32)).astype(BF16)
        dg_ref[...] = (dact * dsilu_ref[...].astype(F32)).astype(BF16)

    dg, dup, dh2b = pl.pallas_call(
        gate_body,
        name="ffn_bwd_gate",
        grid=(rows // tm, nb),
        in_specs=[pl.BlockSpec((tm, D), lambda i, j: (i, 0)), blk(), blk(), _ANY],
        out_specs=[blk(), blk(), pl.BlockSpec((tm, D), lambda i, j: (i, 0))],
        out_shape=[jax.ShapeDtypeStruct((rows, FFN), BF16)] * 2 + [jax.ShapeDtypeStruct((rows, D), BF16)],
        scratch_shapes=[pltpu.VMEM((FFN, D), BF16), pltpu.SemaphoreType.DMA],
        compiler_params=_params(("arbitrary", "arbitrary")),
    )(dh2, silu, dsilu, w_ffn_out)

    def body(dg_ref, dup_ref, dh2_ref, h1_ref, rstd_ref, w_hbm, nw_ref, dh1_ref, stats_ref, w_sc, w_sem):
        @pl.when(pl.program_id(0) == 0)
        def _():
            stats_ref[...] = jnp.zeros_like(stats_ref)
            cp = pltpu.make_async_copy(w_hbm, w_sc, w_sem)
            cp.start()
            cp.wait()

        du = _dot_nt(dg_ref[...], w_sc[:, 0:FFN]) + _dot_nt(dup_ref[...], w_sc[:, FFN:2 * FFN])
        rs = rstd_ref[...]
        n = h1_ref[...] * rs
        stats_ref[0:1, :] += jnp.sum(du * n, axis=0, keepdims=True)
        dn = du * nw_ref[...]
        dh1_ref[...] = dh2_ref[...] + rs * (dn - n * jnp.mean(dn * n, axis=-1, keepdims=True))

    row = lambda width: pl.BlockSpec((tm, width), lambda i: (i, 0))
    dh1, stats = pl.pallas_call(
        body,
        name="ffn_bwd_in",
        grid=(rows // tm,),
        in_specs=[row(FFN), row(FFN), row(D), row(D), row(1), _ANY, pl.BlockSpec((1, D), lambda i: (0, 0))],
        out_specs=[row(D), pl.BlockSpec((8, D), lambda i: (0, 0))],
        out_shape=[jax.ShapeDtypeStruct((rows, D), F32), jax.ShapeDtypeStruct((8, D), F32)],
        scratch_shapes=[pltpu.VMEM((D, 2 * FFN), BF16), pltpu.SemaphoreType.DMA],
        compiler_params=_params(("arbitrary",)),
    )(dg, dup, dh2, h1, rstd2, w_ffn_in, ffn_norm_w)
    return dg, dup, dh2b, dh1, stats


def _mix_bwd(dh1, o, proj, h_lru, y_ret, y_lru, w_br, w_bl, w_o, riders=()):
    rows = dh1.shape[0]
    tm = TM_MIX_BWD
    nt = rows // tm

    def body(dh1_ref, o_ref, gates_ref, hl_ref, yret_ref, ylru_ref, wbr_ref, wbl_ref, wo_ref,
             dproj_ref, do_ref, dhl_ref, mixed_ref, aret_ref, alru_ref, dyret_ref, dylru_ref):
        gate = lambda j: gates_ref[:, j * D:(j + 1) * D].astype(F32)
        dmixed = _dot_nt(dh1_ref[...].astype(BF16), wo_ref[...])
        y_ret, y_lru = yret_ref[...].astype(F32), ylru_ref[...].astype(F32)
        sa, sb = _sigmoid(gate(2)), _sigmoid(gate(3))
        mixed_ref[...] = (sa * y_ret + sb * y_lru).astype(BF16)
        dga = dmixed * y_ret * sa * (1.0 - sa)
        dgb = dmixed * y_lru * sb * (1.0 - sb)
        dy_ret = (dmixed * sa).astype(BF16)
        dy_lru = (dmixed * sb).astype(BF16)
        dyret_ref[...] = dy_ret
        dylru_ref[...] = dy_lru
        da_ret = _dot_nt(dy_ret, wbr_ref[...])
        da_lru = _dot_nt(dy_lru, wbl_ref[...])

        gret = gate(0)
        sg = _sigmoid(gret)
        silu = gret * sg
        on, rstds = _group_norm(o_ref[...].astype(F32))
        aret_ref[...] = (silu * on).astype(BF16)
        dgret = da_ret * on * (sg * (1.0 + gret * (1.0 - sg)))
        don = da_ret * silu
        for h in range(HEADS):
            sl = slice(h * DH, (h + 1) * DH)
            onh, donh = on[:, sl], don[:, sl]
            do_ref[:, sl] = (rstds[h] * (donh - onh * jnp.mean(donh * onh, axis=-1, keepdims=True))).astype(BF16)

        gl, gl_grad = _gelu_and_grad(gate(1))
        hl = hl_ref[...].astype(F32)
        alru_ref[...] = (gl * hl).astype(BF16)
        dlgate = da_lru * hl * gl_grad
        dhl_ref[...] = (da_lru * gl).astype(BF16)

        for j, val in enumerate((dgret, dlgate, dga, dgb)):
            dproj_ref[:, j * D:(j + 1) * D] = val.astype(BF16)

    tile = lambda col: pl.BlockSpec((tm, D), lambda i: (i, col))
    gates = pl.BlockSpec((tm, 4 * D), lambda i: (i, GATES_COL))
    wspec = pl.BlockSpec((D, D), lambda i: (0, 0))
    bf = lambda: jax.ShapeDtypeStruct((rows, D), BF16)
    return _hosted_call(
        body,
        name="mix_bwd",
        grid=(nt,),
        in_specs=[tile(0), tile(0), gates, tile(0), tile(0), tile(0), wspec, wspec, wspec],
        out_specs=[pl.BlockSpec((tm, 4 * D), lambda i: (i, GATES_COL))] + [tile(0)] * 7,
        out_shape=[jax.ShapeDtypeStruct((rows, IN_COLS), BF16)] + [bf() for _ in range(7)],
        scratch_shapes=[],
        args=(dh1, o, proj, h_lru, y_ret, y_lru, w_br, w_bl, w_o),
        riders=riders,
    )


def _retention_bwd(dproj, proj_bf, do, sprev, intra, intra_t, q_dec, k_dec, c_dec, cos_t, sin_t, riders=()):
    rows = proj_bf.shape[0]
    nc = rows // CHUNK

    def body(dproj_in_ref, q_ref, k_ref, v_ref, do_ref, sprev_ref, m_ref, mt_ref, qd_ref, kd_ref, cd_ref, cos_ref, sin_ref,
             dproj_ref, ds_sc):
        @pl.when(pl.program_id(0) == 0)
        def _():
            ds_sc[...] = jnp.zeros_like(ds_sc)

        cos, sin = cos_ref[...], sin_ref[...]

        def unrotate(dy):
            return dy * cos - pltpu.roll(dy, DH // 2, axis=1) * sin

        dstates = [ds_sc[h] for h in range(HEADS)]
        results = []
        for h in range(HEADS):
            sl = slice(h * DH, (h + 1) * DH)
            q, k, v = q_ref[:, sl], k_ref[:, sl], v_ref[:, sl]
            do = do_ref[:, sl]
            dob = do.astype(BF16)
            doq = (do * qd_ref[:, sl]).astype(BF16)
            state_prev = sprev_ref[0, h]
            dstate = dstates[h]
            dstate_b = dstate.astype(BF16)
            s_t = (_dot_nt(k, q) * mt_ref[h]).astype(BF16)
            ds_t = (_dot_nt(v, dob) * mt_ref[h]).astype(BF16)
            ds = (_dot_nt(dob, v) * m_ref[h]).astype(BF16)
            kd = kd_ref[:, sl]
            dq = _dot(ds, k) + _dot_nt(doq, state_prev)
            dk = _dot(ds_t, q) + _dot_nt(v, dstate_b) * kd
            k_scaled = (k.astype(F32) * kd).astype(BF16)
            dv = _dot(s_t, dob) + _dot(k_scaled, dstate_b)
            results.append((dstate * cd_ref[:, sl] + _dot_tn(q, doq), unrotate(dq).astype(BF16),
                            (unrotate(dk) * QK_SCALE).astype(BF16), dv.astype(BF16)))

        for h, (new_dstate, dq_b, dk_b, dv_b) in enumerate(results):
            ds_sc[h] = new_dstate
            for part, val in enumerate((dq_b, dk_b, dv_b)):
                dproj_ref[:, part * D + h * DH:part * D + (h + 1) * DH] = val

    rev = lambda c: nc - 1 - c
    chunk_spec = lambda col: pl.BlockSpec((CHUNK, D), lambda c: (rev(c), col))
    const2 = lambda shape: pl.BlockSpec(shape, lambda c: (0, 0))
    const3 = pl.BlockSpec((HEADS, CHUNK, CHUNK), lambda c: (0, 0, 0))
    return _hosted_call(
        body,
        name="retention_bwd",
        grid=(nc,),
        in_specs=[
            pl.BlockSpec(memory_space=pl.ANY),
            chunk_spec(0), chunk_spec(1), chunk_spec(2), chunk_spec(0),
            pl.BlockSpec((1, HEADS, DH, DH), lambda c: (rev(c), 0, 0, 0)),
            const3, const3,
            const2((CHUNK, D)), const2((CHUNK, D)), const2((1, D)),
            pl.BlockSpec((CHUNK, DH), lambda c: (rev(c), 0)),
            pl.BlockSpec((CHUNK, DH), lambda c: (rev(c), 0)),
        ],
        out_specs=[pl.BlockSpec((CHUNK, 3 * D), lambda c: (rev(c), 0))],
        out_shape=[jax.ShapeDtypeStruct(dproj.shape, BF16)],
        aliases={0: 0},
        scratch_shapes=[pltpu.VMEM((HEADS, DH, DH), F32)],
        args=(dproj, proj_bf, proj_bf, proj_bf, do, sprev, intra, intra_t, q_dec, k_dec, c_dec, cos_t, sin_t),
        riders=riders,
    )


def _lru_bwd(dproj, proj, saved, dhl, conv_w, wa, wx, lam, riders=()):
    rows = proj.shape[0]
    TM = _lru_tile(rows)
    nt = rows // TM
    per8 = TM // 8

    def body(dproj_in_ref, x_ref, xprev_ref, h_ref, hprev_ref, c_ref, r_ref, i_ref, la_ref, mult_ref, dhl_ref,
             cw_ref, wa_ref, wx_ref, lam_ref,
             dproj_ref, dwa_ref, dwx_ref, stats_ref, anext_sc, dhnext_sc, dcnext_sc, c_sc, b_sc, dh_sc):
        step = pl.program_id(0)
        i = nt - 1 - step

        @pl.when(step == 0)
        def _():
            anext_sc[...] = jnp.zeros_like(anext_sc)
            dhnext_sc[...] = jnp.zeros_like(dhnext_sc)
            dcnext_sc[...] = jnp.zeros_like(dcnext_sc)
            dwa_ref[...] = jnp.zeros_like(dwa_ref)
            dwx_ref[...] = jnp.zeros_like(dwx_ref)
            stats_ref[...] = jnp.zeros_like(stats_ref)

        first = i == 0
        x = x_ref[...].astype(F32)
        prev8 = jnp.where(first, 0.0, xprev_ref[8:16, :].astype(F32))
        c_b = c_ref[...]
        c, r, gate_i, mult = (ref[...].astype(F32) for ref in (c_ref, r_ref, i_ref, mult_ref))
        a = jnp.exp(la_ref[...].astype(F32))
        sp = _softplus(-lam_ref[...])
        dh_sc[...] = dhl_ref[...].astype(F32)

        b_sc[...] = pltpu.roll(a, TM - 1, axis=0)
        b_sc[TM - 1:TM, :] = anext_sc[0:1, :]
        anext_sc[0:1, :] = a[0:1, :]
        row8 = lax.broadcasted_iota(jnp.int32, (8, D), 0)

        def group(gi, dhnext):
            r0 = pl.multiple_of((per8 - 1 - gi) * 8, 8)
            bb = b_sc[pl.ds(r0, 8), :]
            uu = dh_sc[pl.ds(r0, 8), :]
            for d in (1, 2, 4):
                b_sh = jnp.where(row8 < 8 - d, pltpu.roll(bb, 8 - d, axis=0), 1.0)
                u_sh = jnp.where(row8 < 8 - d, pltpu.roll(uu, 8 - d, axis=0), 0.0)
                uu = uu + bb * u_sh
                bb = bb * b_sh
            dhb = bb * dhnext + uu
            dh_sc[pl.ds(r0, 8), :] = dhb
            return dhb[0:1, :]

        dhfirst = lax.fori_loop(0, per8, group, dhnext_sc[0:1, :])
        dhnext_sc[0:1, :] = dhfirst
        dh = dh_sc[...]

        h = h_ref[...].astype(F32)
        hprev8 = jnp.where(first, 0.0, hprev_ref[8:16, :].astype(F32))
        h_dn, h_head = _shift_down(h, hprev8, 1)
        c_sc[...] = h_dn
        c_sc[0:8, :] = h_head
        h_before = c_sc[...]

        valid = _row_ids(i, TM, (TM, D)) >= PAD_ROWS
        da = dh * h_before
        du = jnp.where(valid, dh, 0.0)
        dmult = du * gate_i * c
        dgate_i = du * mult * c
        dc = du * mult * gate_i
        dla = da * a - dmult * (a * a) / mult
        dla = jnp.where(valid, dla, 0.0)
        dr = dla * ((-LRU_C) * sp)
        dzr = dr * r * (1.0 - r)
        dzi = dgate_i * gate_i * (1.0 - gate_i)
        stats_ref[1:2, :] += jnp.sum(dzr, axis=0, keepdims=True)
        stats_ref[2:3, :] += jnp.sum(dzi, axis=0, keepdims=True)
        stats_ref[3:4, :] += jnp.sum(dla * ((-LRU_C) * r), axis=0, keepdims=True)
        dc_gate = []
        for g in range(LRU_BLOCKS):
            sl = slice(g * LRU_BLOCK, (g + 1) * LRU_BLOCK)
            cg = c_b[:, sl]
            dzr_g = dzr[:, sl].astype(BF16)
            dzi_g = dzi[:, sl].astype(BF16)
            dc_gate.append(_dot_nt(dzr_g, wa_ref[g]) + _dot_nt(dzi_g, wx_ref[g]))
            dwa_ref[g] += _dot_tn(cg, dzr_g)
            dwx_ref[g] += _dot_tn(cg, dzi_g)
        dc = dc + jnp.concatenate(dc_gate, axis=1)

        cw = cw_ref[...]
        stats_ref[0:1, :] += jnp.sum(dc, axis=0, keepdims=True)
        stats_ref[7:8, :] += jnp.sum(dc * x, axis=0, keepdims=True)
        dx = cw[3:4] * dc
        tail_src = jnp.concatenate([dc[TM - 8:TM], dcnext_sc[...]], axis=0)
        dx_tail = cw[3:4] * dc[TM - 8:TM]
        for d in (1, 2, 3):
            dx = dx + cw[3 - d:4 - d] * pltpu.roll(dc, TM - d, axis=0)
            dx_tail = dx_tail + cw[3 - d:4 - d] * pltpu.roll(tail_src, 16 - d, axis=0)[0:8]
            rolled, hd = _shift_down(x, prev8, d)
            b_sc[...] = rolled
            b_sc[0:8, :] = hd
            stats_ref[7 - d:8 - d, :] += jnp.sum(dc * b_sc[...], axis=0, keepdims=True)
        dcnext_sc[...] = dc[0:8]
        dproj_ref[...] = dx.astype(BF16)
        dproj_ref[TM - 8:TM, :] = dx_tail.astype(BF16)

    rev = lambda s: nt - 1 - s
    vec = pl.BlockSpec((1, D), lambda s: (0, 0))
    wspec = pl.BlockSpec((LRU_BLOCKS, LRU_BLOCK, LRU_BLOCK), lambda s: (0, 0, 0))
    prev16 = lambda col: pl.BlockSpec((16, D), lambda s: (jnp.maximum(rev(s) * (TM // 16) - 1, 0), col))
    tile = lambda: pl.BlockSpec((TM, D), lambda s: (rev(s), 0))
    h_lru, c_sv, r_sv, i_sv, la_sv, mult_sv = saved
    return _hosted_call(
        body,
        name="lru_bwd",
        grid=(nt,),
        in_specs=[
            pl.BlockSpec(memory_space=pl.ANY),
            pl.BlockSpec((TM, D), lambda s: (rev(s), LRU_IN_COL)), prev16(LRU_IN_COL),
            tile(), prev16(0),
            tile(), tile(), tile(), tile(), tile(), tile(),
            pl.BlockSpec((4, D), lambda s: (0, 0)),
            wspec, wspec, vec,
        ],
        out_specs=[
            pl.BlockSpec((TM, D), lambda s: (rev(s), LRU_IN_COL)),
            wspec, wspec,
            pl.BlockSpec((8, D), lambda s: (0, 0)),
        ],
        out_shape=[
            jax.ShapeDtypeStruct(dproj.shape, BF16),
            jax.ShapeDtypeStruct((LRU_BLOCKS, LRU_BLOCK, LRU_BLOCK), F32),
            jax.ShapeDtypeStruct((LRU_BLOCKS, LRU_BLOCK, LRU_BLOCK), F32),
            jax.ShapeDtypeStruct((8, D), F32),
        ],
        aliases={0: 0},
        scratch_shapes=[
            pltpu.VMEM((8, D), F32), pltpu.VMEM((8, D), F32), pltpu.VMEM((8, D), F32),
            pltpu.VMEM((TM, D), F32), pltpu.VMEM((TM, D), F32), pltpu.VMEM((TM, D), F32),
        ],
        args=(dproj, proj, proj, h_lru, h_lru, c_sv, r_sv, i_sv, la_sv, mult_sv, dhl, conv_w, wa, wx, lam),
        riders=riders,
    )


def _in_proj_bwd(dproj, w_in, part, prev=None, riders=()):
    rows = dproj.shape[0]
    tm = _heavy_tile(rows)
    nt = rows // tm
    first = 0 if part == 0 else (nt + 1) // 2
    count = (nt + 1) // 2 if part == 0 else nt - first

    def body(*refs):
        dproj_ref, w_hbm, du_ref, w_sc, w_sem = refs[-5:]

        @pl.when(pl.program_id(0) == 0)
        def _():
            moves = [((0, 3), 0), ((4, 1), 3), ((3, 1), 4), ((5, 3), 5)]
            copies = [pltpu.make_async_copy(w_hbm.at[:, pl.ds(src * D, n * D)], w_sc.at[:, pl.ds(dst * D, n * D)], w_sem.at[q])
                      for q, ((src, n), dst) in enumerate(moves)]
            for cp in copies:
                cp.start()
            for cp in copies:
                cp.wait()

        du_ref[...] = _dot_nt(dproj_ref[...], w_sc[...])

    in_specs = [pl.BlockSpec((tm, IN_COLS), lambda i: (first + i, 0)), _ANY]
    args = (dproj, w_in)
    if prev is not None:
        in_specs = [_ANY] + in_specs
        args = (prev,) + args
    return _hosted_call(
        body,
        name="in_proj_bwd_%d" % part,
        grid=(count,),
        in_specs=in_specs,
        out_specs=[pl.BlockSpec((tm, D), lambda i: (first + i, 0))],
        out_shape=[jax.ShapeDtypeStruct((rows, D), F32)],
        scratch_shapes=[pltpu.VMEM((D, IN_COLS), BF16), pltpu.SemaphoreType.DMA((4,))],
        aliases={0: 0} if prev is not None else None,
        args=args,
        riders=riders,
    )


def _norm1_bwd(du, dh1, head, x2d, rstd1, norm_w, riders=()):
    rows = du.shape[0]

    def body(du_ref, dh1_ref, head_ref, x_ref, rstd_ref, nw_ref, gx_ref, ghead_ref, stats_ref):
        i = pl.program_id(0)

        @pl.when(i == 0)
        def _():
            stats_ref[...] = jnp.zeros_like(stats_ref)

        def finish(h0, out_ref):
            du = du_ref[...]
            rs = rstd_ref[...]
            n = h0 * rs
            stats_ref[0:1, :] += jnp.sum(du * n, axis=0, keepdims=True)
            dn = du * nw_ref[...]
            out_ref[...] = dh1_ref[...] + rs * (dn - n * jnp.mean(dn * n, axis=-1, keepdims=True))

        @pl.when(i == 0)
        def _():
            finish(head_ref[...], ghead_ref)

        @pl.when(i > 0)
        def _():
            finish(x_ref[...], gx_ref)

    tile = pl.BlockSpec((TM, D), lambda i: (i, 0))
    return _hosted_call(
        body,
        name="norm1_bwd",
        grid=(rows // TM,),
        in_specs=[
            tile, tile,
            pl.BlockSpec((FRONT, D), lambda i: (0, 0)),
            pl.BlockSpec((TM, D), lambda i: (jnp.maximum(i - 1, 0), 0)),
            pl.BlockSpec((TM, 1), lambda i: (i, 0)),
            pl.BlockSpec((1, D), lambda i: (0, 0)),
        ],
        out_specs=[
            pl.BlockSpec((TM, D), lambda i: (jnp.maximum(i - 1, 0), 0)),
            pl.BlockSpec((FRONT, D), lambda i: (0, 0)),
            pl.BlockSpec((8, D), lambda i: (0, 0)),
        ],
        out_shape=[
            jax.ShapeDtypeStruct(x2d.shape, F32),
            jax.ShapeDtypeStruct((FRONT, D), F32),
            jax.ShapeDtypeStruct((8, D), F32),
        ],
        scratch_shapes=[],
        args=(du, dh1, head, x2d, rstd1, norm_w),
        riders=riders,
    )


def _matmul_tn(name, x, dy, out_cols, col0=0, prev=None, k_block=None, n_block=None, riders=(), col_map=None):
    col_map = col_map or (lambda n: n)
    rows, kdim = x.shape
    ndim = dy.shape[1]
    kb = k_block or kdim
    nb = n_block or ndim
    step_bytes = lambda t: 2 * t * (kb * x.dtype.itemsize + nb * dy.dtype.itemsize) + 2 * kb * nb * 4
    tr = next(t for t in (2816, 1408, TM_HEAVY, TM) if rows % t == 0 and (t == TM or step_bytes(t) <= TN_VMEM_BUDGET))
    nr, nk, nn = rows // tr, kdim // kb, ndim // nb
    cb0 = col0 // nb

    def body(*refs):
        x_ref, dy_ref, out_ref = refs[-3], refs[-2], refs[-1]
        part = _dot_tn(x_ref[...].astype(BF16), dy_ref[...].astype(BF16))

        @pl.when(pl.program_id(2) == 0)
        def _():
            out_ref[...] = part

        @pl.when(pl.program_id(2) > 0)
        def _():
            out_ref[...] += part

    in_specs = [
        pl.BlockSpec((tr, kb), lambda n, k, r: (r, k)),
        pl.BlockSpec((tr, nb), lambda n, k, r: (r, n)),
    ]
    args = [x, dy]
    aliases = {}
    if prev is not None:
        in_specs = [pl.BlockSpec(memory_space=pl.ANY)] + in_specs
        args = [prev] + args
        aliases = {0: 0}
    (out,), rider_outs = _hosted_call(
        body,
        name=name,
        grid=(nn, nk, nr),
        in_specs=in_specs,
        out_specs=[pl.BlockSpec((kb, nb), lambda n, k, r: (k, cb0 + col_map(n)))],
        out_shape=[jax.ShapeDtypeStruct((kdim, out_cols), F32)],
        scratch_shapes=[],
        aliases=aliases,
        args=args,
        riders=riders,
    )
    return (out, rider_outs) if riders else out


def _local_step(x2d, target, w, plan):
    rows = FRONT + x2d.shape[0]
    head = jnp.concatenate([jnp.zeros((PAD_ROWS, D), F32), w["meta_tokens"]], axis=0)
    cos_t, sin_t = _rope_tables(rows)
    intra, intra_t, q_dec, k_dec, c_dec = _decay_tables()
    grads = {}

    def hosted(host, fn, *args, **kwargs):
        outs, rider_outs = fn(*args, riders=plan.riders(host, w, grads), **kwargs)
        plan.after(host, rider_outs, w, grads)
        return outs

    (u1, rstd1), _ = _norm1(head, x2d, w["mix_norm_w"])
    proj, w["w_in"] = hosted("in_proj", _in_proj, u1, w["w_in_shard"], w["route"], cos_t, sin_t)
    o, sprev = hosted("retention_fwd", _retention_fwd, proj, intra, q_dec, k_dec, c_dec)
    lru_args = (w["conv_w"], w["conv_b"], w["lru_wa"], w["lru_wx"], w["lru_ba"], w["lru_bx"], w["lru_lambda"])
    lru_saved = hosted("lru_fwd", _lru_fwd, proj, *lru_args)
    h_lru = lru_saved[0]
    y_ret, y_lru, h1, u2, rstd2 = hosted("mix_fwd", _mix_fwd, head, x2d, o, proj, h_lru, w["w_branch_ret"],
                                         w["w_branch_lru"], w["w_out"], w["ffn_norm_w"])
    silu, dsilu, act = hosted("ffn_fwd_gate", _ffn_fwd_gate, u2, w["w_ffn_in"])
    dh2, stats_loss = _ffn_out_loss(act, h1, w["w_ffn_out"], target, w["final_norm_w"])

    dg, dup, dh2b, dh1, stats_ffn = _ffn_bwd(dh2, silu, dsilu, h1, rstd2, w["w_ffn_in"], w["w_ffn_out"], w["ffn_norm_w"])
    grads["w_ffn_in"] = _matmul_tn("dw_ffn_up", u2, dup, 2 * FFN, col0=FFN, n_block=FFN_HALF,
                                   prev=_matmul_tn("dw_ffn_gate", u2, dg, 2 * FFN, n_block=FFN_HALF))
    grads["w_ffn_out"] = _matmul_tn("dw_ffn_out", act, dh2b, D, k_block=FFN_HALF)
    (dproj, do, dhl, mixed, a_ret, a_lru, dy_ret, dy_lru) = hosted(
        "mix_bwd", _mix_bwd, dh1, o, proj, h_lru, y_ret, y_lru, w["w_branch_ret"], w["w_branch_lru"], w["w_out"])
    grads["w_out"] = _matmul_tn("dw_out", mixed, dh1, D)
    grads["w_branch_ret"] = _matmul_tn("dw_branch_ret", a_ret, dy_ret, D)
    grads["w_branch_lru"] = _matmul_tn("dw_branch_lru", a_lru, dy_lru, D)
    (dproj,) = hosted("retention_bwd", _retention_bwd, dproj, proj, do, sprev, intra, intra_t, q_dec, k_dec, c_dec,
                      cos_t, sin_t)
    dproj, grads["lru_wa"], grads["lru_wx"], stats_lru = hosted(
        "lru_bwd", _lru_bwd, dproj, proj, lru_saved, dhl, w["conv_w"], w["lru_wa"], w["lru_wx"], w["lru_lambda"])
    grads["w_in"], rider_outs = _matmul_tn("dw_in", u1, dproj, IN_COLS, n_block=D, col_map=_swap_3_4,
                                           riders=plan.riders("dw_in", w, grads))
    plan.after("dw_in", rider_outs, w, grads)
    (du1,) = hosted("in_proj_bwd_0", _in_proj_bwd, dproj, w["w_in"], 0)
    (du1,) = hosted("in_proj_bwd_1", _in_proj_bwd, dproj, w["w_in"], 1, du1)
    (grad_x, grad_head, stats_in), _ = _norm1_bwd(du1, dh1, head, x2d, rstd1, w["mix_norm_w"])
    return grad_x, grad_head, grads, [stats_loss, stats_ffn, stats_in, stats_lru]


BIG_PIECES = {
    "w_in": ("col", (D, 2 * D)),
    "w_ffn_in": ("col", (D, FFN_HALF)),
    "w_ffn_out": ("row", (FFN // 4, D)),
    "w_branch_ret": ("row", (D // 4, D)),
    "w_branch_lru": ("row", (D // 4, D)),
    "w_out": ("row", (D // 4, D)),
    "lru_wa": ("lru", (LRU_BLOCKS, LRU_BLOCK // 4, LRU_BLOCK)),
    "lru_wx": ("lru", (LRU_BLOCKS, LRU_BLOCK // 4, LRU_BLOCK)),
}
SMALL_PIECES = {"meta_tokens": ("col", (N_META, D // 4)), "conv_w": ("col", (4, D // 4))}


def _full_shape(kind, shard):
    if kind == "col":
        return (shard[0], 4 * shard[1])
    if kind == "row":
        return (4 * shard[0], shard[1])
    return (shard[0], 4 * shard[1], shard[2])


def _half_shape(kind, shard):
    return (shard[0] // 2,) + tuple(shard[1:])


def _aligned(start, multiple):
    return start if isinstance(start, int) else pl.multiple_of(start, multiple)


def _lead(h, size):
    if h is None:
        return pl.ds(0, size)
    return pl.ds(_aligned(h * (size // 2), size // 2), size // 2)


def _full_region(ref, kind, shard, s, h):
    if kind == "col":
        return ref.at[_lead(h, shard[0]), pl.ds(_aligned(s * shard[1], shard[1]), shard[1])]
    if kind == "row":
        size = shard[0] if h is None else shard[0] // 2
        start = s * shard[0] + (0 if h is None else h * (shard[0] // 2))
        return ref.at[pl.ds(_aligned(start, 16), size), :]
    return ref.at[_lead(h, shard[0]), pl.ds(_aligned(s * shard[1], shard[1]), shard[1]), :]


def _shard_region(ref, shard, h):
    return ref.at[_lead(h, shard[0])]


def _place():
    x, y, c = lax.axis_index("x"), lax.axis_index("y"), lax.axis_index("c")
    return x, y, c, 2 * x + y


def _other_chip(s, c, k):
    s2 = jnp.bitwise_xor(s, k)
    return s2, (s2 // 2, s2 % 2, c)


def _remote(src, dst, send_sem, recv_sem, dev):
    return pltpu.make_async_remote_copy(src_ref=src, dst_ref=dst, send_sem=send_sem, recv_sem=recv_sem,
                                        device_id=dev, device_id_type=MESH)


_ANY = pl.BlockSpec(memory_space=pl.ANY)


class _Rider:
    def __init__(self, ins, out_shapes, sem_shapes, build, aliased=False):
        self.ins, self.out_shapes, self.sem_shapes, self.build, self.aliased = ins, out_shapes, sem_shapes, build, aliased


def _hosted_call(body, *, name, grid, in_specs, out_specs, out_shape, scratch_shapes, args, riders=(), aliases=None,
                 prefetch=None, riders_after_body=False):
    n_in, n_out, n_sc = len(in_specs), len(out_shape), len(scratch_shapes)
    r_in = [a for r in riders for a in r.ins]
    r_out = [s for r in riders for s in r.out_shapes]
    r_sem = [s for r in riders for s in r.sem_shapes]
    lead = () if prefetch is None else (prefetch,)
    assert prefetch is None or not (aliases or any(r.aliased for r in riders))

    def full_body(*refs):
        head, refs = refs[:len(lead)], refs[len(lead):]
        ins, rin = refs[:n_in], refs[n_in:n_in + len(r_in)]
        o0 = n_in + len(r_in)
        outs, rout = refs[o0:o0 + n_out], refs[o0 + n_out:o0 + n_out + len(r_out)]
        s0 = o0 + n_out + len(r_out)
        scratch, rsem = refs[s0:s0 + n_sc], refs[s0 + n_sc:]
        starts, waits = [], []
        pi = po = ps = 0
        for r in riders:
            st, wt = r.build(rin[pi:pi + len(r.ins)], rout[po:po + len(r.out_shapes)], rsem[ps:ps + len(r.sem_shapes)])
            starts += st
            waits += wt
            pi, po, ps = pi + len(r.ins), po + len(r.out_shapes), ps + len(r.sem_shapes)
        first = functools.reduce(jnp.logical_and, [pl.program_id(d) == 0 for d in range(len(grid))])
        last = functools.reduce(jnp.logical_and, [pl.program_id(d) == grid[d] - 1 for d in range(len(grid))])

        def start_riders():
            @pl.when(first)
            def _():
                for cp in starts:
                    cp.start()

        if riders and not riders_after_body:
            start_riders()
        body(*head, *ins, *outs, *scratch)
        if riders and riders_after_body:
            start_riders()
        if riders:
            @pl.when(last)
            def _():
                for wait in waits:
                    wait()

    io_aliases = dict(aliases or {})
    pi = po = 0
    for r in riders:
        if r.aliased:
            for q in range(len(r.ins)):
                io_aliases[n_in + pi + q] = n_out + po + q
        pi, po = pi + len(r.ins), po + len(r.out_shapes)
    specs = dict(
        grid=grid,
        in_specs=list(in_specs) + [_ANY] * len(r_in),
        out_specs=list(out_specs) + [_ANY] * len(r_out),
        scratch_shapes=list(scratch_shapes) + r_sem,
    )
    if prefetch is not None:
        specs = dict(grid_spec=pltpu.PrefetchScalarGridSpec(num_scalar_prefetch=1, **specs))
    res = pl.pallas_call(
        full_body,
        name=name,
        out_shape=list(out_shape) + r_out,
        input_output_aliases=io_aliases,
        compiler_params=_params(("arbitrary",) * len(grid)),
        **specs,
    )(*lead, *args, *r_in)
    rider_outs, po = [], n_out
    for r in riders:
        rider_outs.append(list(res[po:po + len(r.out_shapes)]))
        po += len(r.out_shapes)
    return list(res[:n_out]), rider_outs


def _run_riders(name, riders):
    r_in = [a for r in riders for a in r.ins]
    r_out = [s for r in riders for s in r.out_shapes]
    r_sem = [s for r in riders for s in r.sem_shapes]

    def body(*refs):
        rin, rout, rsem = refs[:len(r_in)], refs[len(r_in):len(r_in) + len(r_out)], refs[len(r_in) + len(r_out):]
        pi = po = ps = 0
        all_waits = []
        for r in riders:
            starts, waits = r.build(rin[pi:pi + len(r.ins)], rout[po:po + len(r.out_shapes)], rsem[ps:ps + len(r.sem_shapes)])
            for cp in starts:
                cp.start()
            all_waits += waits
            pi, po, ps = pi + len(r.ins), po + len(r.out_shapes), ps + len(r.sem_shapes)
        for wait in all_waits:
            wait()

    io_aliases = {}
    pi = po = 0
    for r in riders:
        if r.aliased:
            for q in range(len(r.ins)):
                io_aliases[pi + q] = po + q
        pi, po = pi + len(r.ins), po + len(r.out_shapes)
    res = pl.pallas_call(
        body,
        name=name,
        in_specs=[_ANY] * len(r_in),
        out_specs=[_ANY] * len(r_out),
        out_shape=r_out,
        scratch_shapes=r_sem,
        input_output_aliases=io_aliases,
    )(*r_in)
    outs, po = [], 0
    for r in riders:
        outs.append(list(res[po:po + len(r.out_shapes)]))
        po += len(r.out_shapes)
    return outs


def _piece(name):
    if name in BIG_PIECES:
        return (name, *BIG_PIECES[name], True)
    return (name, *SMALL_PIECES[name], False)


def _gather_rider(shards, names):
    pieces = [_piece(n) for n in names]
    n = len(pieces)

    def build(ins, outs, sems):
        local_sem, ici_send, ici_recv = sems
        _, _, c, s = _place()
        starts, waits = [], []
        for p, (_, kind, shard, split) in enumerate(pieces):
            cp = pltpu.make_async_copy(ins[p], _full_region(outs[p], kind, shard, s, None), local_sem.at[p])
            starts.append(cp)
            waits.append(cp.wait)
            h = c if split else None
            for k in (1, 2, 3):
                s2, dev = _other_chip(s, c, k)
                cp = _remote(_shard_region(ins[p], shard, h), _full_region(outs[p], kind, shard, s, h),
                             ici_send.at[p, k - 1], ici_recv.at[p, k - 1], dev)
                starts.append(cp)
                waits.append(cp.wait_send)
                region = _full_region(outs[p], kind, shard, s2, h)
                waits.append(_remote(region, region, ici_send.at[p, k - 1], ici_recv.at[p, k - 1], dev).wait_recv)
        return starts, waits

    return _Rider(
        [shards[name] for name in names],
        [jax.ShapeDtypeStruct(_full_shape(kind, shard), shards[name].dtype) for name, kind, shard, _ in pieces],
        [pltpu.SemaphoreType.DMA((n,)), pltpu.SemaphoreType.DMA((n, 3)), pltpu.SemaphoreType.DMA((n, 3))],
        build)


def _forward_rider(gathered, names):
    pieces = [_piece(n) for n in names]
    n = len(pieces)

    def build(ins, outs, sems):
        fwd_send, fwd_recv = sems
        x, y, c, s = _place()
        sibling = (x, y, 1 - c)
        starts, waits = [], []
        for p, (_, kind, shard, _) in enumerate(pieces):
            for k in (1, 2, 3):
                s2, _ = _other_chip(s, c, k)
                mine = _full_region(outs[p], kind, shard, s2, c)
                theirs = _full_region(outs[p], kind, shard, s2, 1 - c)
                cp = _remote(mine, mine, fwd_send.at[p, k - 1], fwd_recv.at[p, k - 1], sibling)
                starts.append(cp)
                waits.append(cp.wait_send)
                waits.append(_remote(theirs, theirs, fwd_send.at[p, k - 1], fwd_recv.at[p, k - 1], sibling).wait_recv)
        return starts, waits

    return _Rider(
        [gathered[name] for name in names],
        [jax.ShapeDtypeStruct(gathered[name].shape, gathered[name].dtype) for name in names],
        [pltpu.SemaphoreType.DMA((n, 3)), pltpu.SemaphoreType.DMA((n, 3))],
        build, aliased=True)


def _pair_exchange_rider(grads, names):
    n = len(names)

    def build(ins, outs, sems):
        send_sem, recv_sem = sems
        x, y, c, _ = _place()
        sibling = (x, y, 1 - c)
        starts, waits = [], []
        for p, name in enumerate(names):
            kind, shard = BIG_PIECES[name]
            for s2 in range(4):
                cp = _remote(_full_region(ins[p], kind, shard, s2, 1 - c), outs[p].at[s2], send_sem.at[p, s2],
                             recv_sem.at[p, s2], sibling)
                starts.append(cp)
                waits.append(cp.wait_send)
                waits.append(_remote(outs[p].at[s2], outs[p].at[s2], send_sem.at[p, s2], recv_sem.at[p, s2], sibling).wait_recv)
        return starts, waits

    return _Rider(
        [grads[name] for name in names],
        [jax.ShapeDtypeStruct((4,) + _half_shape(*BIG_PIECES[name]), F32) for name in names],
        [pltpu.SemaphoreType.DMA((n, 4))] * 2,
        build)


def _half_specs(kind, shard):
    half = shard[0] // 2
    if kind == "col":
        full = pl.BlockSpec((half, shard[1]), lambda j, pr: (pr[1], j))
        buf = pl.BlockSpec((None, half, shard[1]), lambda j, pr: (j, 0, 0))
    elif kind == "row":
        full = pl.BlockSpec((half, shard[1]), lambda j, pr: (2 * j + pr[1], 0))
        buf = pl.BlockSpec((None, half, shard[1]), lambda j, pr: (j, 0, 0))
    else:
        full = pl.BlockSpec((half, shard[1], shard[2]), lambda j, pr: (pr[1], j, 0))
        buf = pl.BlockSpec((None, half, shard[1], shard[2]), lambda j, pr: (j, 0, 0, 0))
    return full, buf


def _pair_sum(name, grad, recv, place):
    kind, shard = BIG_PIECES[name]
    full, buf = _half_specs(kind, shard)

    def body(pr, g_ref, r_ref, o_ref):
        o_ref[...] = (g_ref[...] + r_ref[...]).astype(BF16)

    return pl.pallas_call(
        body,
        name="pair_sum_" + name,
        grid_spec=pltpu.PrefetchScalarGridSpec(num_scalar_prefetch=1, grid=(4,), in_specs=[full, buf], out_specs=buf),
        out_shape=jax.ShapeDtypeStruct((4,) + _half_shape(kind, shard), BF16),
        compiler_params=_params(("arbitrary",)),
    )(place, grad, recv)


def _chip_exchange_rider(sums, names):
    n = len(names)

    def build(ins, outs, sems):
        send_sem, recv_sem = sems
        _, _, c, s = _place()
        starts, waits = [], []
        for p in range(n):
            for k in (1, 2, 3):
                s2, dev = _other_chip(s, c, k)
                cp = _remote(ins[p].at[s2], outs[p].at[k - 1], send_sem.at[p, k - 1], recv_sem.at[p, k - 1], dev)
                starts.append(cp)
                waits.append(cp.wait_send)
                waits.append(_remote(outs[p].at[k - 1], outs[p].at[k - 1], send_sem.at[p, k - 1], recv_sem.at[p, k - 1],
                                     dev).wait_recv)
        return starts, waits

    return _Rider(
        [sums[name] for name in names],
        [jax.ShapeDtypeStruct((3,) + _half_shape(*BIG_PIECES[name]), BF16) for name in names],
        [pltpu.SemaphoreType.DMA((n, 3))] * 2,
        build)


def _chip_sum(name, grad, recv_pair, recv_chip, place):
    kind, shard = BIG_PIECES[name]
    half = shard[0] // 2
    tail = tuple(shard[1:])
    zeros = (0,) * len(tail)
    if kind == "col":
        full = pl.BlockSpec((half,) + tail, lambda j, pr: (pr[1], pr[0]))
    elif kind == "row":
        full = pl.BlockSpec((half,) + tail, lambda j, pr: (2 * pr[0] + pr[1], 0))
    else:
        full = pl.BlockSpec((half,) + tail, lambda j, pr: (pr[1], pr[0], 0))
    pair = pl.BlockSpec((None, half) + tail, lambda j, pr: (pr[0], 0) + zeros)
    chip = pl.BlockSpec((3, half) + tail, lambda j, pr: (0, 0) + zeros)
    out = pl.BlockSpec((half,) + tail, lambda j, pr: (pr[1],) + zeros)

    def body(pr, g_ref, rp_ref, rc_ref, o_ref):
        total = g_ref[...] + rp_ref[...]
        for k in range(3):
            total = total + rc_ref[k].astype(F32)
        o_ref[...] = total

    return pl.pallas_call(
        body,
        name="chip_sum_" + name,
        grid_spec=pltpu.PrefetchScalarGridSpec(num_scalar_prefetch=1, grid=(1,), in_specs=[full, pair, chip], out_specs=out),
        out_shape=jax.ShapeDtypeStruct(shard, F32),
        compiler_params=_params(("arbitrary",)),
    )(place, grad, recv_pair, recv_chip)


def _sibling_exchange_rider(halves, names):
    n = len(names)

    def build(ins, outs, sems):
        send_sem, recv_sem = sems
        x, y, c, _ = _place()
        sibling = (x, y, 1 - c)
        starts, waits = [], []
        for p, name in enumerate(names):
            shard = BIG_PIECES[name][1]
            mine = _shard_region(outs[p], shard, c)
            theirs = _shard_region(outs[p], shard, 1 - c)
            cp = _remote(mine, mine, send_sem.at[p], recv_sem.at[p], sibling)
            starts.append(cp)
            waits.append(cp.wait_send)
            waits.append(_remote(theirs, theirs, send_sem.at[p], recv_sem.at[p], sibling).wait_recv)
        return starts, waits

    return _Rider(
        [halves[name] for name in names],
        [jax.ShapeDtypeStruct(BIG_PIECES[name][1], F32) for name in names],
        [pltpu.SemaphoreType.DMA((n,))] * 2,
        build, aliased=True)


FIRST_WEIGHTS = ["meta_tokens", "conv_w"]
WEIGHT_GROUPS = {
    "lru": ["lru_wa", "lru_wx"],
    "branch": ["w_branch_ret", "w_branch_lru", "w_out"],
    "ffn_in": ["w_ffn_in"],
    "ffn_out": ["w_ffn_out"],
}
WEIGHT_SCHEDULE = {
    "in_proj": [("gather", "lru")],
    "retention_fwd": [("forward", "lru"), ("gather", "branch")],
    "lru_fwd": [("forward", "branch"), ("gather", "ffn_in")],
    "mix_fwd": [("forward", "ffn_in"), ("gather", "ffn_out")],
    "ffn_fwd_gate": [("forward", "ffn_out")],
}
GRAD_GROUPS = {
    "ffn": ["w_ffn_in", "w_ffn_out"],
    "mixer": ["w_out", "w_branch_ret", "w_branch_lru", "lru_wa", "lru_wx"],
    "in": ["w_in"],
}
GRAD_SCHEDULE = {
    "mix_bwd": [("pair", "ffn")],
    "retention_bwd": [("chip", "ffn")],
    "lru_bwd": [("sibling", "ffn")],
    "dw_in": [("pair", "mixer")],
    "in_proj_bwd_0": [("chip", "mixer"), ("pair", "in")],
    "in_proj_bwd_1": [("sibling", "mixer"), ("chip", "in")],
}


class _CommPlan:
    def __init__(self, shards, place):
        self.shards, self.place = shards, place
        self.late = {}
        self.recv_pair, self.sums, self.recv_chip, self.halves, self.final = {}, {}, {}, {}, {}

    def _grad_rider(self, stage, group, grads):
        names = GRAD_GROUPS[group]
        if stage == "pair":
            return _pair_exchange_rider(grads, names)
        if stage == "chip":
            return _chip_exchange_rider(self.sums, names)
        return _sibling_exchange_rider(self.halves, names)

    def _grad_after(self, stage, group, outs, grads):
        names = GRAD_GROUPS[group]
        if stage == "pair":
            for n, o in zip(names, outs):
                self.recv_pair[n] = o
                self.sums[n] = _pair_sum(n, grads[n], o, self.place)
        elif stage == "chip":
            for n, o in zip(names, outs):
                self.halves[n] = _chip_sum(n, grads[n], self.recv_pair[n], o, self.place)
        else:
            self.final.update(zip(names, outs))

    def riders(self, host, w, grads):
        if host in WEIGHT_SCHEDULE:
            return [_gather_rider(self.shards, WEIGHT_GROUPS[group]) if stage == "gather"
                    else _forward_rider(self.late, WEIGHT_GROUPS[group]) for stage, group in WEIGHT_SCHEDULE[host]]
        return [self._grad_rider(stage, group, grads) for stage, group in GRAD_SCHEDULE.get(host, [])]

    def after(self, host, rider_outs, w, grads):
        for (stage, group), outs in zip(WEIGHT_SCHEDULE.get(host, []), rider_outs):
            (self.late if stage == "gather" else w).update(zip(WEIGHT_GROUPS[group], outs))
        for (stage, group), outs in zip(GRAD_SCHEDULE.get(host, []), rider_outs):
            self._grad_after(stage, group, outs, grads)

    def finish(self, partial):
        outs, (blocks,) = _run_riders("tail_exchange", [_sibling_exchange_rider(self.halves, GRAD_GROUPS["in"]),
                                                        _small_exchange_rider(partial)])
        self.final.update(zip(GRAD_GROUPS["in"], outs))
        return self.final, blocks


def _adamw_math(w, g, m, v):
    m = ADAM_B1 * m + (1.0 - ADAM_B1) * g
    v = ADAM_B2 * v + (1.0 - ADAM_B2) * (g * g)
    m_hat = m / (1.0 - ADAM_B1 ** ADAM_STEP)
    v_hat = v / (1.0 - ADAM_B2 ** ADAM_STEP)
    delta = -ADAM_LR * (m_hat / (jnp.sqrt(v_hat) + ADAM_EPS) + ADAM_WD * w)
    return delta, m, v


def _adamw(name, g, w, m, v):
    rows, cols = g.shape
    tr = rows // 4 if rows % 32 == 0 else rows

    def body(g_ref, w_ref, m_ref, v_ref, go_ref, d_ref, mo_ref, vo_ref):
        gv = g_ref[...]
        delta, m2, v2 = _adamw_math(w_ref[...], gv, m_ref[...], v_ref[...])
        go_ref[...] = gv
        d_ref[...] = delta
        mo_ref[...] = m2
        vo_ref[...] = v2

    spec = pl.BlockSpec((tr, cols), lambda i: (i, 0))
    return pl.pallas_call(
        body,
        name="adamw_" + name,
        grid=(rows // tr,),
        in_specs=[spec] * 4,
        out_specs=[spec] * 4,
        out_shape=[jax.ShapeDtypeStruct((rows, cols), F32)] * 4,
        compiler_params=_params(("arbitrary",)),
    )(g, w, m, v)


SMALL_ROWS = 48
VEC_ROWS = {"final_norm_w": 1, "ffn_norm_w": 8, "mix_norm_w": 16, "conv_b": 24, "lru_ba": 25, "lru_bx": 26, "lru_lambda": 27}
VEC_NAMES = list(VEC_ROWS)
CONV_W_ROW = 28
META_ROW = 32


def _small_exchange_rider(partial):
    def build(ins, outs, sems):
        local_sem, send_sem, recv_sem = sems
        _, _, c, s = _place()
        me = 2 * s + c
        cp = pltpu.make_async_copy(ins[0], outs[0].at[me], local_sem)
        starts, waits = [cp], [cp.wait]
        for k in range(1, 8):
            peer = jnp.bitwise_xor(me, k)
            dev = (peer // 4, (peer // 2) % 2, peer % 2)
            rd = _remote(ins[0], outs[0].at[me], send_sem.at[k - 1], recv_sem.at[k - 1], dev)
            starts.append(rd)
            waits.append(rd.wait_send)
            waits.append(_remote(ins[0], outs[0].at[peer], send_sem.at[k - 1], recv_sem.at[k - 1], dev).wait_recv)
        return starts, waits

    return _Rider([partial], [jax.ShapeDtypeStruct((8, SMALL_ROWS, D), F32)],
                  [pltpu.SemaphoreType.DMA, pltpu.SemaphoreType.DMA((7,)), pltpu.SemaphoreType.DMA((7,))], build)


def _small_update(blocks, place, vecs, conv, meta):
    nvec = len(VEC_NAMES)
    qcols = D // 4

    def in_order(ref):
        total = ref[0]
        for d in range(1, 8):
            total = total + ref[d]
        return total

    def body(pr, blocks_ref, cols_ref, *refs):
        vec_refs = refs[:3 * nvec]
        conv_refs = refs[3 * nvec:3 * nvec + 3]
        meta_refs = refs[3 * nvec + 3:3 * nvec + 6]
        outs = refs[3 * nvec + 6:]
        loss_ref, vec_out, conv_out, meta_out = outs[0], outs[1:5], outs[5:9], outs[9:13]
        tot, col = in_order(blocks_ref), in_order(cols_ref)
        loss_ref[...] = jnp.sum(tot[0:1, :], axis=1, keepdims=True)
        for o in vec_out:
            o[...] = jnp.zeros_like(o)
        for j, name in enumerate(VEC_NAMES):
            w, m, v = (r[...] for r in vec_refs[3 * j:3 * j + 3])
            g = tot[VEC_ROWS[name]:VEC_ROWS[name] + 1, :]
            if name == "lru_lambda":
                g = -g / (1.0 + jnp.exp(w))
            for o, val in zip(vec_out, (g,) + _adamw_math(w, g, m, v)):
                o[j:j + 1, :] = val
        for row, n_rows, ins, group in ((CONV_W_ROW, 4, conv_refs, conv_out), (META_ROW, N_META, meta_refs, meta_out)):
            g = col[row:row + n_rows, :]
            w, m, v = (r[...] for r in ins)
            for o, val in zip(group, (g,) + _adamw_math(w, g, m, v)):
                o[...] = val

    whole = lambda shape: pl.BlockSpec(shape, lambda i, pr: (0,) * len(shape))
    in_specs = [whole((8, SMALL_ROWS, D)), pl.BlockSpec((8, SMALL_ROWS, qcols), lambda i, pr: (0, 0, pr[0]))]
    in_specs += [whole((1, D))] * (3 * nvec) + [whole((4, qcols))] * 3 + [whole((N_META, qcols))] * 3
    out_shapes = [(1, 1)] + [(8, D)] * 4 + [(4, qcols)] * 4 + [(N_META, qcols)] * 4
    return pl.pallas_call(
        body,
        name="small_update",
        grid_spec=pltpu.PrefetchScalarGridSpec(num_scalar_prefetch=1, grid=(1,), in_specs=in_specs,
                                               out_specs=[whole(s) for s in out_shapes]),
        out_shape=[jax.ShapeDtypeStruct(s, F32) for s in out_shapes],
        compiler_params=_params(("arbitrary",)),
    )(place, blocks, blocks, *[a for t in vecs for a in t], *conv, *meta)


WEIGHT_ORDER = ["meta_tokens", "mix_norm_w", "w_in", "conv_w", "conv_b", "lru_wa", "lru_ba", "lru_wx", "lru_bx", "lru_lambda",
                "w_branch_ret", "w_branch_lru", "w_out", "ffn_norm_w", "w_ffn_in", "w_ffn_out", "final_norm_w"]


def kernel(x, meta_tokens, mix_norm_w, w_in, conv_w, conv_b, lru_wa, lru_ba, lru_wx, lru_bx, lru_lambda, w_branch_ret, w_branch_lru, w_out, ffn_norm_w, w_ffn_in, w_ffn_out, final_norm_w, loss_target, m_meta_tokens, m_mix_norm_w, m_w_in, m_conv_w, m_conv_b, m_lru_wa, m_lru_ba, m_lru_wx, m_lru_bx, m_lru_lambda, m_w_branch_ret, m_w_branch_lru, m_w_out, m_ffn_norm_w, m_w_ffn_in, m_w_ffn_out, m_final_norm_w, v_meta_tokens, v_mix_norm_w, v_w_in, v_conv_w, v_conv_b, v_lru_wa, v_lru_ba, v_lru_wx, v_lru_bx, v_lru_lambda, v_w_branch_ret, v_w_branch_lru, v_w_out, v_ffn_norm_w, v_w_ffn_in, v_w_ffn_out, v_final_norm_w):
    args = locals()
    wts = {n: args[n] for n in WEIGHT_ORDER}
    mom = {n: args["m_" + n] for n in WEIGHT_ORDER}
    var = {n: args["v_" + n] for n in WEIGHT_ORDER}
    place = jnp.stack([2 * lax.axis_index("x") + lax.axis_index("y"), lax.axis_index("c")]).astype(jnp.int32)

    shards = {n: wts[n][0].astype(BF16) for n in BIG_PIECES}
    shards["meta_tokens"] = wts["meta_tokens"]
    shards["conv_w"] = wts["conv_w"][0]
    plan = _CommPlan(shards, place)
    (first,) = _run_riders("gather_first", [_gather_rider(shards, FIRST_WEIGHTS)])
    w = dict(zip(FIRST_WEIGHTS, first))
    for n in VEC_NAMES:
        w[n] = wts[n].reshape(1, D)
    chips = jnp.bitwise_xor(place[0], jnp.arange(4, dtype=jnp.int32))
    w["route"] = jnp.concatenate([place, jnp.stack([2 * chips, 2 * chips + 1], axis=1).reshape(8)])
    w["w_in_shard"] = shards["w_in"]

    grad_x, grad_head, _, stats = _local_step(x[0], loss_target[0], w, plan)
    partial = jnp.concatenate(stats + [grad_head[PAD_ROWS:]], axis=0)
    shard_grads, blocks = plan.finish(partial)

    out = {}
    for n in BIG_PIECES:
        shape2d = (-1, wts[n].shape[-1])
        res = _adamw(n, *[a.reshape(shape2d) for a in (shard_grads[n], wts[n], mom[n], var[n])])
        out[n] = [r.reshape(wts[n].shape) for r in res]

    vecs = [tuple(a[n].reshape(1, D) for a in (wts, mom, var)) for n in VEC_NAMES]
    conv = tuple(a["conv_w"][0] for a in (wts, mom, var))
    meta = tuple(a["meta_tokens"] for a in (wts, mom, var))
    res = _small_update(blocks, place, vecs, conv, meta)
    loss = res[0].reshape(())
    for j, n in enumerate(VEC_NAMES):
        out[n] = [r[j].reshape(wts[n].shape) for r in res[1:5]]
    out["conv_w"] = [r.reshape(wts["conv_w"].shape) for r in res[5:9]]
    out["meta_tokens"] = list(res[9:13])

    return (loss, grad_x.reshape(x.shape)) + tuple(out[n][kind] for kind in range(4) for n in WEIGHT_ORDER)
```

```python
import functools
import math

import jax
import jax.numpy as jnp
from jax import lax
from jax.experimental import pallas as pl
from jax.experimental.pallas import tpu as pltpu

F32 = jnp.float32
BF16 = jnp.bfloat16

LANES = 128
BF16_ROWS = 16

D = 1024
HEADS = 8
DH = 128
CHUNK = 256
N_META = 16
FRONT = 256
PAD_ROWS = FRONT - N_META
LRU_BLOCKS = 4
LRU_BLOCK = 256
LRU_C = 8.0
FFN = 2816
FFN_HALF = FFN // 2
IN_COLS = 8 * D
ROPE_BASE = 10000.0
EPS = 1e-6
QK_SCALE = DH ** -0.5

ADAM_LR = 0.001
ADAM_B1 = 0.9
ADAM_B2 = 0.999
ADAM_EPS = 1e-08
ADAM_WD = 0.01
ADAM_STEP = 10

TM = 256
TM_HEAVY = 768
FFN_BLOCK = FFN // 2
FFN_SUB = 768
TM_MIX_BWD = 256
TM_LRU = 384
TM_MIX_FWD = 384
TM_IN_PROJ = 1408
VMEM_LIMIT = 60 * 1024 * 1024
TN_VMEM_BUDGET = 40 * 1024 * 1024

NT_DIMS = (((1,), (1,)), ((), ()))
TN_DIMS = (((0,), (0,)), ((), ()))
MESH = pl.DeviceIdType.MESH


def _params(sem=None):
    if sem is None:
        return pltpu.CompilerParams(vmem_limit_bytes=VMEM_LIMIT)
    return pltpu.CompilerParams(dimension_semantics=sem, vmem_limit_bytes=VMEM_LIMIT)


def _dot(a, b):
    return jnp.dot(a, b, preferred_element_type=F32)


def _dot_nt(a, b):
    return lax.dot_general(a, b, NT_DIMS, preferred_element_type=F32)


def _dot_tn(a, b):
    return lax.dot_general(a, b, TN_DIMS, preferred_element_type=F32)


def _sigmoid(z):
    return 1.0 / (1.0 + jnp.exp(-z))


def _log1p(x):
    return jnp.where(x < 1e-3, x * (1.0 - x * (0.5 - x * (1.0 / 3.0))), jnp.log(1.0 + x))


def _softplus(x):
    return jnp.maximum(x, 0.0) + _log1p(jnp.exp(-jnp.abs(x)))


def _one_minus_square(a, log_a):
    x = 2.0 * log_a
    series = -x * (1.0 + x * (0.5 + x * (1.0 / 6.0)))
    return jnp.where(x > -0.02, series, 1.0 - a * a)


_GELU_K = math.sqrt(2.0 / math.pi)


def _gelu_and_grad(x):
    inner = _GELU_K * (x + 0.044715 * x * x * x)
    t = jnp.tanh(inner)
    val = 0.5 * x * (1.0 + t)
    grad = 0.5 * (1.0 + t) + 0.5 * x * (1.0 - t * t) * _GELU_K * (1.0 + 3.0 * 0.044715 * x * x)
    return val, grad


def _row_ids(i, rows, shape):
    return i * rows + lax.broadcasted_iota(jnp.int32, shape, 0)


def _rope_tables(rows):
    inv_freq = ROPE_BASE ** (-jnp.arange(0, DH, 2, dtype=F32) / DH)
    block_pos = jnp.arange(rows // FRONT, dtype=jnp.int32) * FRONT - PAD_ROWS
    coarse = block_pos.astype(F32)[:, None] * inv_freq[None, :]
    fine = jnp.arange(FRONT, dtype=F32)[:, None] * inv_freq[None, :]
    ca, sa = jnp.cos(coarse)[:, None, :], jnp.sin(coarse)[:, None, :]
    cb, sb = jnp.cos(fine)[None, :, :], jnp.sin(fine)[None, :, :]
    cos = (ca * cb - sa * sb).reshape(rows, DH // 2)
    sin = (sa * cb + ca * sb).reshape(rows, DH // 2)
    return jnp.concatenate([cos, cos], axis=1), jnp.concatenate([-sin, sin], axis=1)


def _decay_tables():
    log_g = jnp.log(1.0 - 2.0 ** (-5.0 - jnp.arange(HEADS, dtype=F32)))
    idx = jnp.arange(CHUNK, dtype=F32)
    diff = idx[:, None] - idx[None, :]
    intra = jnp.where(diff[None] >= 0, jnp.exp(jnp.maximum(diff, 0.0)[None] * log_g[:, None, None]), 0.0)
    q_decay = jnp.exp((idx + 1.0)[:, None] * log_g[None, :])
    k_decay = jnp.exp((CHUNK - 1.0 - idx)[:, None] * log_g[None, :])
    chunk_decay = jnp.exp(CHUNK * log_g)
    wide = lambda a: jnp.repeat(a, DH, axis=-1)
    return intra, jnp.swapaxes(intra, 1, 2), wide(q_decay), wide(k_decay), wide(chunk_decay[None, :])


def _norm1(head, x2d, norm_w, riders=()):
    rows = FRONT + x2d.shape[0]
    tm = TM_IN_PROJ if rows % TM_IN_PROJ == 0 else TM_MIX_FWD
    nt = rows // tm

    def body(head_hbm, x_hbm, nw_ref, u_ref, rstd_ref, h0_sc, h0_sem):
        slot = _frame_rows(head_hbm, x_hbm, h0_sc, h0_sem, pl.program_id(0), tm, nt)
        hv = h0_sc[slot]
        rs = lax.rsqrt(jnp.mean(hv * hv, axis=-1, keepdims=True) + EPS)
        u_ref[...] = ((hv * rs) * nw_ref[...]).astype(BF16)
        rstd_ref[...] = rs

    return _hosted_call(
        body,
        name="norm1",
        grid=(nt,),
        in_specs=[_ANY, _ANY, pl.BlockSpec((1, D), lambda i: (0, 0))],
        out_specs=[pl.BlockSpec((tm, D), lambda i: (i, 0)), pl.BlockSpec((tm, 1), lambda i: (i, 0))],
        out_shape=[jax.ShapeDtypeStruct((rows, D), BF16), jax.ShapeDtypeStruct((rows, 1), F32)],
        scratch_shapes=[pltpu.VMEM((2, tm, D), F32), pltpu.SemaphoreType.DMA((3,))],
        args=(head, x2d, norm_w),
        riders=riders,
    )


def _heavy_tile(rows):
    return TM_HEAVY if rows % TM_HEAVY == 0 else TM


def _lru_tile(rows):
    return TM_LRU if rows % TM_LRU == 0 else TM


def _swap_3_4(group):
    return jnp.where(group == 3, 4, jnp.where(group == 4, 3, group))


LRU_IN_COL = 3
GATES_COL = 1


def _in_proj(u, w_shard, route, cos_t, sin_t, riders=()):
    rows = u.shape[0]
    tm = TM_IN_PROJ if rows % TM_IN_PROJ == 0 else _heavy_tile(rows)
    nt = rows // tm
    kind, shard = BIG_PIECES["w_in"]

    def body(route_ref, u_hbm, wsh_ref, cos_ref, sin_ref, proj_ref, wfull_ref,
             u_sc, w_sc, u_sem, w_sem, local_sem, ici_send, ici_recv, fwd_send, fwd_recv):
        g, i = pl.program_id(0), pl.program_id(1)
        s, c = route_ref[0], route_ref[1]
        gid = route_ref[2 + g]
        sibling = (s // 2, s % 2, 1 - c)
        local = pltpu.make_async_copy(wsh_ref, _full_region(wfull_ref, kind, shard, s, None), local_sem)
        u_copies = [pltpu.make_async_copy(u_hbm.at[pl.ds(t * tm, tm)], u_sc.at[t], u_sem.at[t]) for t in range(nt)]
        sends, arrivals = [], []
        for k in (1, 2, 3):
            s2, dev = _other_chip(s, c, k)
            sends.append(_remote(_shard_region(wsh_ref, shard, c), _full_region(wfull_ref, kind, shard, s, c),
                                 ici_send.at[k - 1], ici_recv.at[k - 1], dev))
            mine = _full_region(wfull_ref, kind, shard, s2, c)
            theirs = _full_region(wfull_ref, kind, shard, s2, 1 - c)
            arrivals.append((_remote(mine, mine, ici_send.at[k - 1], ici_recv.at[k - 1], dev),
                             _remote(mine, mine, fwd_send.at[k - 1], fwd_recv.at[k - 1], sibling),
                             _remote(theirs, theirs, fwd_send.at[k - 1], fwd_recv.at[k - 1], sibling)))

        slot = g % 2
        last_tile = i == nt - 1

        def block_copy(src, col, to_slot):
            return pltpu.make_async_copy(src.at[:, pl.ds(pl.multiple_of(col * D, D), D)], w_sc.at[to_slot],
                                         w_sem.at[to_slot])

        @pl.when(jnp.logical_and(g == 0, i == 0))
        def _():
            for cp in sends:
                cp.start()
            local.start()
            for cp in u_copies:
                cp.start()
            cp = block_copy(wsh_ref, 0, 0)
            cp.start()
            cp.wait()

        @pl.when(jnp.logical_and(last_tile, g == 0))
        def _():
            block_copy(wsh_ref, 1, 1).start()

        for k, (arrived, forward, forwarded) in zip((1, 2, 3), arrivals):
            @pl.when(jnp.logical_and(last_tile, g == 2 * k - 1))
            def _():
                arrived.wait_recv()
                forward.start()
                forwarded.wait_recv()
                block_copy(wfull_ref, route_ref[2 + 2 * k], 0).start()

            @pl.when(jnp.logical_and(last_tile, g == 2 * k))
            def _():
                block_copy(wfull_ref, route_ref[3 + 2 * k], 1).start()

        @pl.when(jnp.logical_and(i == 0, g > 0))
        def _():
            block_copy(wfull_ref, 0, slot).wait()

        for t in range(nt):
            @pl.when(jnp.logical_and(g == 0, i == t))
            def _():
                u_copies[t].wait()

        acc = _dot(u_sc[i], w_sc[slot])

        @pl.when(gid < 2)
        def _():
            scale = jnp.where(gid == 1, QK_SCALE, 1.0).astype(F32)
            for h in range(HEADS):
                sl = slice(h * DH, (h + 1) * DH)
                blk = acc[:, sl]
                out = (blk * cos_ref[...] + pltpu.roll(blk, DH // 2, axis=1) * sin_ref[...]) * scale
                proj_ref[:, sl] = out.astype(BF16)

        @pl.when(gid >= 2)
        def _():
            proj_ref[...] = acc.astype(BF16)

        @pl.when(jnp.logical_and(g == 7, i == nt - 1))
        def _():
            local.wait()
            for cp in sends:
                cp.wait_send()
            for _, forward, _ in arrivals:
                forward.wait_send()

    return _hosted_call(
        body,
        name="in_proj",
        grid=(8, nt),
        in_specs=[
            _ANY, _ANY,
            pl.BlockSpec((tm, DH), lambda g, i, rt: (i, 0)),
            pl.BlockSpec((tm, DH), lambda g, i, rt: (i, 0)),
        ],
        out_specs=[pl.BlockSpec((tm, D), lambda g, i, rt: (i, _swap_3_4(rt[2 + g]))), _ANY],
        out_shape=[jax.ShapeDtypeStruct((rows, IN_COLS), BF16), jax.ShapeDtypeStruct((D, IN_COLS), BF16)],
        scratch_shapes=[
            pltpu.VMEM((nt, tm, D), BF16), pltpu.VMEM((2, D, D), BF16),
            pltpu.SemaphoreType.DMA((nt,)), pltpu.SemaphoreType.DMA((2,)), pltpu.SemaphoreType.DMA,
            pltpu.SemaphoreType.DMA((3,)), pltpu.SemaphoreType.DMA((3,)),
            pltpu.SemaphoreType.DMA((3,)), pltpu.SemaphoreType.DMA((3,)),
        ],
        args=(u, w_shard, cos_t, sin_t),
        riders=riders,
        prefetch=route,
        riders_after_body=True,
    )


def _retention_fwd(proj_bf, intra, q_dec, k_dec, c_dec, riders=()):
    rows = proj_bf.shape[0]
    nc = rows // CHUNK

    def body(q_ref, k_ref, v_ref, m_ref, qd_ref, kd_ref, cd_ref, o_ref, sprev_ref, s_sc):
        @pl.when(pl.program_id(0) == 0)
        def _():
            s_sc[...] = jnp.zeros_like(s_sc)

        for h in range(HEADS):
            sl = slice(h * DH, (h + 1) * DH)
            q, k, v = q_ref[:, sl], k_ref[:, sl], v_ref[:, sl]
            state = s_sc[h]
            state_b = state.astype(BF16)
            sprev_ref[0, h] = state_b
            s = _dot_nt(q, k) * m_ref[h]
            inner = _dot(s.astype(BF16), v)
            cross = _dot(q, state_b) * qd_ref[:, sl]
            o_ref[:, sl] = (inner + cross).astype(BF16)
            k_scaled = (k.astype(F32) * kd_ref[:, sl]).astype(BF16)
            s_sc[h] = state * cd_ref[:, sl] + _dot_tn(k_scaled, v)

    chunk_spec = lambda col: pl.BlockSpec((CHUNK, D), lambda c: (c, col))
    const2 = lambda shape: pl.BlockSpec(shape, lambda c: (0, 0))
    return _hosted_call(
        body,
        name="retention_fwd",
        grid=(nc,),
        in_specs=[
            chunk_spec(0), chunk_spec(1), chunk_spec(2),
            pl.BlockSpec((HEADS, CHUNK, CHUNK), lambda c: (0, 0, 0)),
            const2((CHUNK, D)), const2((CHUNK, D)), const2((1, D)),
        ],
        out_specs=[
            pl.BlockSpec((CHUNK, D), lambda c: (c, 0)),
            pl.BlockSpec((1, HEADS, DH, DH), lambda c: (c, 0, 0, 0)),
        ],
        out_shape=[
            jax.ShapeDtypeStruct((rows, D), BF16),
            jax.ShapeDtypeStruct((nc, HEADS, DH, DH), BF16),
        ],
        scratch_shapes=[pltpu.VMEM((HEADS, DH, DH), F32)],
        args=(proj_bf, proj_bf, proj_bf, intra, q_dec, k_dec, c_dec),
        riders=riders,
    )


def _shift_down(x, first8_prev, d):
    rolled = pltpu.roll(x, d, axis=0)
    head = pltpu.roll(jnp.concatenate([first8_prev, x[0:8]], axis=0), d, axis=0)[8:16]
    return rolled, head


def _conv_and_gates(x, prev8, cw_ref, cb_ref, wa_ref, wx_ref, ba_ref, bx_ref, lam_ref, c_sc):
    cw = cw_ref[...]
    conv = cb_ref[...] + cw[3:4] * x
    head = cb_ref[...] + cw[3:4] * x[0:8]
    for d in (1, 2, 3):
        rolled, hd = _shift_down(x, prev8, d)
        conv = conv + cw[3 - d:4 - d] * rolled
        head = head + cw[3 - d:4 - d] * hd
    c_sc[...] = conv
    c_sc[0:8, :] = head
    c = c_sc[...]
    zr, zi = [], []
    for g in range(LRU_BLOCKS):
        sl = slice(g * LRU_BLOCK, (g + 1) * LRU_BLOCK)
        cg = c[:, sl].astype(BF16)
        zr.append(_dot(cg, wa_ref[g]))
        zi.append(_dot(cg, wx_ref[g]))
    r = _sigmoid(jnp.concatenate(zr, axis=1) + ba_ref[...])
    gate_i = _sigmoid(jnp.concatenate(zi, axis=1) + bx_ref[...])
    sp = _softplus(-lam_ref[...])
    log_a = (-LRU_C) * r * sp
    a = jnp.exp(log_a)
    mult = jnp.sqrt(_one_minus_square(a, log_a))
    return c, r, gate_i, a, mult, log_a


def _lru_fwd(proj, conv_w, conv_b, wa, wx, ba, bx, lam, riders=()):
    rows = proj.shape[0]
    TM = _lru_tile(rows)
    nt = rows // TM

    def body(x_ref, cw_ref, cb_ref, wa_ref, wx_ref, ba_ref, bx_ref, lam_ref,
             h_ref, c_ref, r_ref, i_ref, la_ref, mult_ref, prev_sc, carry_sc, c_sc, a_sc, u_sc, h_sc):
        i = pl.program_id(0)

        @pl.when(i == 0)
        def _():
            prev_sc[...] = jnp.zeros_like(prev_sc)
            carry_sc[...] = jnp.zeros_like(carry_sc)

        x = x_ref[...].astype(F32)
        c, r, gate_i, a, mult, log_a = _conv_and_gates(x, prev_sc[...], cw_ref, cb_ref, wa_ref, wx_ref, ba_ref, bx_ref,
                                                       lam_ref, c_sc)
        for ref, val in ((c_ref, c), (r_ref, r), (i_ref, gate_i), (la_ref, log_a), (mult_ref, mult)):
            ref[...] = val.astype(BF16)
        prev_sc[...] = x[TM - 8:TM]
        valid = _row_ids(i, TM, (TM, D)) >= PAD_ROWS
        a_sc[...] = a
        u_sc[...] = jnp.where(valid, mult * gate_i * c, 0.0)
        row8 = lax.broadcasted_iota(jnp.int32, (8, D), 0)

        def group(gi, hprev):
            r0 = pl.multiple_of(gi * 8, 8)
            aa = a_sc[pl.ds(r0, 8), :]
            uu = u_sc[pl.ds(r0, 8), :]
            for d in (1, 2, 4):
                a_sh = jnp.where(row8 >= d, pltpu.roll(aa, d, axis=0), 1.0)
                u_sh = jnp.where(row8 >= d, pltpu.roll(uu, d, axis=0), 0.0)
                uu = uu + aa * u_sh
                aa = aa * a_sh
            hb = aa * hprev + uu
            h_sc[pl.ds(r0, 8), :] = hb
            return hb[7:8, :]

        hlast = lax.fori_loop(0, TM // 8, group, carry_sc[0:1, :])
        carry_sc[0:1, :] = hlast
        h_ref[...] = h_sc[...].astype(BF16)

    vec = pl.BlockSpec((1, D), lambda i: (0, 0))
    wspec = pl.BlockSpec((LRU_BLOCKS, LRU_BLOCK, LRU_BLOCK), lambda i: (0, 0, 0))
    return _hosted_call(
        body,
        name="lru_fwd",
        grid=(nt,),
        in_specs=[
            pl.BlockSpec((TM, D), lambda i: (i, LRU_IN_COL)),
            pl.BlockSpec((4, D), lambda i: (0, 0)),
            vec, wspec, wspec, vec, vec, vec,
        ],
        out_specs=[pl.BlockSpec((TM, D), lambda i: (i, 0)) for _ in range(6)],
        out_shape=[jax.ShapeDtypeStruct((rows, D), BF16) for _ in range(6)],
        scratch_shapes=[
            pltpu.VMEM((8, D), F32), pltpu.VMEM((8, D), F32),
            pltpu.VMEM((TM, D), F32), pltpu.VMEM((TM, D), F32), pltpu.VMEM((TM, D), F32), pltpu.VMEM((TM, D), F32),
        ],
        args=(proj, conv_w, conv_b, wa, wx, ba, bx, lam),
        riders=riders,
    )


def _group_norm(o):
    outs, rstds = [], []
    for h in range(HEADS):
        oh = o[:, h * DH:(h + 1) * DH]
        rs = lax.rsqrt(jnp.mean(oh * oh, axis=-1, keepdims=True) + EPS)
        outs.append(oh * rs)
        rstds.append(rs)
    return jnp.concatenate(outs, axis=1), rstds


def _frame_rows(head_hbm, x_hbm, buf, sems, i, tm, nt):
    slot = i % 2

    @pl.when(i == 0)
    def _():
        first = [pltpu.make_async_copy(head_hbm, buf.at[0, pl.ds(0, FRONT)], sems.at[2]),
                 pltpu.make_async_copy(x_hbm.at[pl.ds(0, tm - FRONT)], buf.at[0, pl.ds(FRONT, tm - FRONT)], sems.at[0])]
        for cp in first:
            cp.start()
        for cp in first:
            cp.wait()

    @pl.when(i + 1 < nt)
    def _():
        start = pl.multiple_of((i + 1) * tm - FRONT, LANES)
        pltpu.make_async_copy(x_hbm.at[pl.ds(start, tm)], buf.at[1 - slot], sems.at[1 - slot]).start()

    @pl.when(i > 0)
    def _():
        pltpu.make_async_copy(x_hbm.at[pl.ds(0, tm)], buf.at[slot], sems.at[slot]).wait()

    return slot


def _mix_fwd(head, x2d, o, proj, h_lru, w_br, w_bl, w_o, ffn_norm_w, riders=()):
    rows = o.shape[0]
    tm = TM_MIX_FWD
    assert rows % tm == 0 and tm > FRONT
    nt = rows // tm

    def body(head_hbm, x_hbm, o_ref, gates_ref, hl_ref, wbr_ref, wbl_ref, wo_ref, nw_ref,
             yret_ref, ylru_ref, h1_ref, u2_ref, rstd_ref, h0_sc, h0_sem):
        i = pl.program_id(0)
        slot = _frame_rows(head_hbm, x_hbm, h0_sc, h0_sem, i, tm, nt)
        gate = lambda j: gates_ref[:, j * D:(j + 1) * D].astype(F32)
        on, _ = _group_norm(o_ref[...].astype(F32))
        gret = gate(0)
        a_ret = (gret * _sigmoid(gret) * on).astype(BF16)
        y_ret = _dot(a_ret, wbr_ref[...])
        gl, _ = _gelu_and_grad(gate(1))
        a_lru = (gl * hl_ref[...].astype(F32)).astype(BF16)
        y_lru = _dot(a_lru, wbl_ref[...])
        mixed = (_sigmoid(gate(2)) * y_ret + _sigmoid(gate(3)) * y_lru).astype(BF16)
        delta = _dot(mixed, wo_ref[...])
        yret_ref[...] = y_ret.astype(BF16)
        ylru_ref[...] = y_lru.astype(BF16)
        h1 = h0_sc[slot] + delta
        rs = lax.rsqrt(jnp.mean(h1 * h1, axis=-1, keepdims=True) + EPS)
        h1_ref[...] = h1
        u2_ref[...] = ((h1 * rs) * nw_ref[...]).astype(BF16)
        rstd_ref[...] = rs

    tile = lambda col: pl.BlockSpec((tm, D), lambda i: (i, col))
    wspec = pl.BlockSpec((D, D), lambda i: (0, 0))
    return _hosted_call(
        body,
        name="mix_fwd",
        grid=(nt,),
        in_specs=[
            _ANY, _ANY,
            tile(0), pl.BlockSpec((tm, 4 * D), lambda i: (i, GATES_COL)), tile(0),
            wspec, wspec, wspec,
            pl.BlockSpec((1, D), lambda i: (0, 0)),
        ],
        out_specs=[tile(0), tile(0), tile(0), tile(0), pl.BlockSpec((tm, 1), lambda i: (i, 0))],
        out_shape=[
            jax.ShapeDtypeStruct((rows, D), BF16), jax.ShapeDtypeStruct((rows, D), BF16),
            jax.ShapeDtypeStruct((rows, D), F32), jax.ShapeDtypeStruct((rows, D), BF16),
            jax.ShapeDtypeStruct((rows, 1), F32),
        ],
        scratch_shapes=[pltpu.VMEM((2, tm, D), F32), pltpu.SemaphoreType.DMA((3,))],
        args=(head, x2d, o, proj, h_lru, w_br, w_bl, w_o, ffn_norm_w),
        riders=riders,
    )


def _ffn_fwd_gate(u2, w_ffn_in, riders=()):
    rows = u2.shape[0]
    tm = _heavy_tile(rows)
    nb = FFN // FFN_BLOCK
    hid = lambda: pl.BlockSpec((tm, FFN_BLOCK), lambda i, j: (i, j))

    def body(u2_ref, w_hbm, silu_ref, dsilu_ref, act_ref, w_sc, w_sem):
        j = pl.program_id(1)

        @pl.when(jnp.logical_and(pl.program_id(0) == 0, j == 0))
        def _():
            cp = pltpu.make_async_copy(w_hbm, w_sc, w_sem)
            cp.start()
            cp.wait()

        u2 = u2_ref[...]
        for a, b in ((0, FFN_SUB), (FFN_SUB, FFN_BLOCK)):
            col = pl.multiple_of(j * FFN_BLOCK + a, LANES)
            g = _dot(u2, w_sc[:, pl.ds(col, b - a)])
            up = _dot(u2, w_sc[:, pl.ds(pl.multiple_of(FFN + col, LANES), b - a)])
            sg = _sigmoid(g)
            silu = g * sg
            silu_ref[:, a:b] = silu.astype(BF16)
            dsilu_ref[:, a:b] = (up * (sg * (1.0 + g * (1.0 - sg)))).astype(BF16)
            act_ref[:, a:b] = (silu * up).astype(BF16)

    return _hosted_call(
        body,
        name="ffn_fwd_gate",
        grid=(rows // tm, nb),
        in_specs=[pl.BlockSpec((tm, D), lambda i, j: (i, 0)), _ANY],
        out_specs=[hid(), hid(), hid()],
        out_shape=[jax.ShapeDtypeStruct((rows, FFN), BF16)] * 3,
        scratch_shapes=[pltpu.VMEM((D, 2 * FFN), BF16), pltpu.SemaphoreType.DMA],
        args=(u2, w_ffn_in),
        riders=riders,
    )


def _ffn_out_loss(act, h1, w_ffn_out, target, final_norm_w):
    rows = act.shape[0]
    tm = _heavy_tile(rows)
    nt = rows // tm

    def body(act_ref, h1_ref, wo_ref, fnw_ref, tgt_hbm, dh2_ref, stats_ref, tgt_sc, tgt_sem):
        i = pl.program_id(0)
        slot = i % 2

        @pl.when(i == 0)
        def _():
            stats_ref[...] = jnp.zeros_like(stats_ref)
            tgt_sc[0, 0:FRONT, :] = jnp.zeros((FRONT, D), F32)
            cp = pltpu.make_async_copy(tgt_hbm.at[pl.ds(0, tm - FRONT)], tgt_sc.at[0, pl.ds(FRONT, tm - FRONT)], tgt_sem.at[0])
            cp.start()
            cp.wait()

        @pl.when(i + 1 < nt)
        def _():
            start = pl.multiple_of((i + 1) * tm - FRONT, FRONT)
            pltpu.make_async_copy(tgt_hbm.at[pl.ds(start, tm)], tgt_sc.at[1 - slot], tgt_sem.at[1 - slot]).start()

        @pl.when(i > 0)
        def _():
            pltpu.make_async_copy(tgt_hbm.at[pl.ds(0, tm)], tgt_sc.at[slot], tgt_sem.at[slot]).wait()

        h2 = h1_ref[...] + _dot(act_ref[...], wo_ref[...])
        rs = lax.rsqrt(jnp.mean(h2 * h2, axis=-1, keepdims=True) + EPS)
        n = h2 * rs
        fnw = fnw_ref[...]
        valid = _row_ids(i, tm, (tm, D)) >= FRONT
        diff = jnp.where(valid, n * fnw - tgt_sc[slot], 0.0)
        dy = diff * (1.0 / D)
        stats_ref[0:1, :] += (0.5 / D) * jnp.sum(diff * diff, axis=0, keepdims=True)
        stats_ref[1:2, :] += jnp.sum(dy * n, axis=0, keepdims=True)
        dn = dy * fnw
        dh2_ref[...] = rs * (dn - n * jnp.mean(dn * n, axis=-1, keepdims=True))

    return pl.pallas_call(
        body,
        name="ffn_out_loss",
        grid=(nt,),
        in_specs=[
            pl.BlockSpec((tm, FFN), lambda i: (i, 0)),
            pl.BlockSpec((tm, D), lambda i: (i, 0)),
            pl.BlockSpec((FFN, D), lambda i: (0, 0)),
            pl.BlockSpec((1, D), lambda i: (0, 0)),
            _ANY,
        ],
        out_specs=[pl.BlockSpec((tm, D), lambda i: (i, 0)), pl.BlockSpec((8, D), lambda i: (0, 0))],
        out_shape=[jax.ShapeDtypeStruct((rows, D), F32), jax.ShapeDtypeStruct((8, D), F32)],
        scratch_shapes=[pltpu.VMEM((2, tm, D), F32), pltpu.SemaphoreType.DMA((2,))],
        compiler_params=_params(("arbitrary",)),
    )(act, h1, w_ffn_out, final_norm_w, target)


def _ffn_bwd(dh2, silu, dsilu, h1, rstd2, w_ffn_in, w_ffn_out, ffn_norm_w):
    rows = dh2.shape[0]
    tm = _heavy_tile(rows)
    nb = FFN // FFN_BLOCK
    blk = lambda: pl.BlockSpec((tm, FFN_BLOCK), lambda i, j: (i, j))

    def gate_body(dh2_ref, silu_ref, dsilu_ref, wo_hbm, dg_ref, dup_ref, dh2b_ref, wo_sc, wo_sem):
        j = pl.program_id(1)

        @pl.when(jnp.logical_and(pl.program_id(0) == 0, j == 0))
        def _():
            cp = pltpu.make_async_copy(wo_hbm, wo_sc, wo_sem)
            cp.start()
            cp.wait()

        @pl.when(j == 0)
        def _():
            dh2b_ref[...] = dh2_ref[...].astype(BF16)

        dact = _dot_nt(dh2b_ref[...], wo_sc[pl.ds(pl.multiple_of(j * FFN_BLOCK, LANES), FFN_BLOCK), :])
        dup_ref[...] = (dact * silu_ref[...].astype(F32)).astype(BF16)
        dg_ref[...] = (dact * dsilu_ref[...].astype(F32)).astype(BF16)

    dg, dup, dh2b = pl.pallas_call(
        gate_body,
        name="ffn_bwd_gate",
        grid=(rows // tm, nb),
        in_specs=[pl.BlockSpec((tm, D), lambda i, j: (i, 0)), blk(), blk(), _ANY],
        out_specs=[blk(), blk(), pl.BlockSpec((tm, D), lambda i, j: (i, 0))],
        out_shape=[jax.ShapeDtypeStruct((rows, FFN), BF16)] * 2 + [jax.ShapeDtypeStruct((rows, D), BF16)],
        scratch_shapes=[pltpu.VMEM((FFN, D), BF16), pltpu.SemaphoreType.DMA],
        compiler_params=_params(("arbitrary", "arbitrary")),
    )(dh2, silu, dsilu, w_ffn_out)

    def body(dg_ref, dup_ref, dh2_ref, h1_ref, rstd_ref, w_hbm, nw_ref, dh1_ref, stats_ref, w_sc, w_sem):
        @pl.when(pl.program_id(0) == 0)
        def _():
            stats_ref[...] = jnp.zeros_like(stats_ref)
            cp = pltpu.make_async_copy(w_hbm, w_sc, w_sem)
            cp.start()
            cp.wait()

        du = _dot_nt(dg_ref[...], w_sc[:, 0:FFN]) + _dot_nt(dup_ref[...], w_sc[:, FFN:2 * FFN])
        rs = rstd_ref[...]
        n = h1_ref[...] * rs
        stats_ref[0:1, :] += jnp.sum(du * n, axis=0, keepdims=True)
        dn = du * nw_ref[...]
        dh1_ref[...] = dh2_ref[...] + rs * (dn - n * jnp.mean(dn * n, axis=-1, keepdims=True))

    row = lambda width: pl.BlockSpec((tm, width), lambda i: (i, 0))
    dh1, stats = pl.pallas_call(
        body,
        name="ffn_bwd_in",
        grid=(rows // tm,),
        in_specs=[row(FFN), row(FFN), row(D), row(D), row(1), _ANY, pl.BlockSpec((1, D), lambda i: (0, 0))],
        out_specs=[row(D), pl.BlockSpec((8, D), lambda i: (0, 0))],
        out_shape=[jax.ShapeDtypeStruct((rows, D), F32), jax.ShapeDtypeStruct((8, D), F32)],
        scratch_shapes=[pltpu.VMEM((D, 2 * FFN), BF16), pltpu.SemaphoreType.DMA],
        compiler_params=_params(("arbitrary",)),
    )(dg, dup, dh2, h1, rstd2, w_ffn_in, ffn_norm_w)
    return dg, dup, dh2b, dh1, stats


def _mix_bwd(dh1, o, proj, h_lru, y_ret, y_lru, w_br, w_bl, w_o, riders=()):
    rows = dh1.shape[0]
    tm = TM_MIX_BWD
    nt = rows // tm

    def body(dh1_ref, o_ref, gates_ref, hl_ref, yret_ref, ylru_ref, wbr_ref, wbl_ref, wo_ref,
             dproj_ref, do_ref, dhl_ref, mixed_ref, aret_ref, alru_ref, dyret_ref, dylru_ref):
        gate = lambda j: gates_ref[:, j * D:(j + 1) * D].astype(F32)
        dmixed = _dot_nt(dh1_ref[...].astype(BF16), wo_ref[...])
        y_ret, y_lru = yret_ref[...].astype(F32), ylru_ref[...].astype(F32)
        sa, sb = _sigmoid(gate(2)), _sigmoid(gate(3))
        mixed_ref[...] = (sa * y_ret + sb * y_lru).astype(BF16)
        dga = dmixed * y_ret * sa * (1.0 - sa)
        dgb = dmixed * y_lru * sb * (1.0 - sb)
        dy_ret = (dmixed * sa).astype(BF16)
        dy_lru = (dmixed * sb).astype(BF16)
        dyret_ref[...] = dy_ret
        dylru_ref[...] = dy_lru
        da_ret = _dot_nt(dy_ret, wbr_ref[...])
        da_lru = _dot_nt(dy_lru, wbl_ref[...])

        gret = gate(0)
        sg = _sigmoid(gret)
        silu = gret * sg
        on, rstds = _group_norm(o_ref[...].astype(F32))
        aret_ref[...] = (silu * on).astype(BF16)
        dgret = da_ret * on * (sg * (1.0 + gret * (1.0 - sg)))
        don = da_ret * silu
        for h in range(HEADS):
            sl = slice(h * DH, (h + 1) * DH)
            onh, donh = on[:, sl], don[:, sl]
            do_ref[:, sl] = (rstds[h] * (donh - onh * jnp.mean(donh * onh, axis=-1, keepdims=True))).astype(BF16)

        gl, gl_grad = _gelu_and_grad(gate(1))
        hl = hl_ref[...].astype(F32)
        alru_ref[...] = (gl * hl).astype(BF16)
        dlgate = da_lru * hl * gl_grad
        dhl_ref[...] = (da_lru * gl).astype(BF16)

        for j, val in enumerate((dgret, dlgate, dga, dgb)):
            dproj_ref[:, j * D:(j + 1) * D] = val.astype(BF16)

    tile = lambda col: pl.BlockSpec((tm, D), lambda i: (i, col))
    gates = pl.BlockSpec((tm, 4 * D), lambda i: (i, GATES_COL))
    wspec = pl.BlockSpec((D, D), lambda i: (0, 0))
    bf = lambda: jax.ShapeDtypeStruct((rows, D), BF16)
    return _hosted_call(
        body,
        name="mix_bwd",
        grid=(nt,),
        in_specs=[tile(0), tile(0), gates, tile(0), tile(0), tile(0), wspec, wspec, wspec],
        out_specs=[pl.BlockSpec((tm, 4 * D), lambda i: (i, GATES_COL))] + [tile(0)] * 7,
        out_shape=[jax.ShapeDtypeStruct((rows, IN_COLS), BF16)] + [bf() for _ in range(7)],
        scratch_shapes=[],
        args=(dh1, o, proj, h_lru, y_ret, y_lru, w_br, w_bl, w_o),
        riders=riders,
    )


def _retention_bwd(dproj, proj_bf, do, sprev, intra, intra_t, q_dec, k_dec, c_dec, cos_t, sin_t, riders=()):
    rows = proj_bf.shape[0]
    nc = rows // CHUNK

    def body(dproj_in_ref, q_ref, k_ref, v_ref, do_ref, sprev_ref, m_ref, mt_ref, qd_ref, kd_ref, cd_ref, cos_ref, sin_ref,
             dproj_ref, ds_sc):
        @pl.when(pl.program_id(0) == 0)
        def _():
            ds_sc[...] = jnp.zeros_like(ds_sc)

        cos, sin = cos_ref[...], sin_ref[...]

        def unrotate(dy):
            return dy * cos - pltpu.roll(dy, DH // 2, axis=1) * sin

        for h in range(HEADS):
            sl = slice(h * DH, (h + 1) * DH)
            q, k, v = q_ref[:, sl], k_ref[:, sl], v_ref[:, sl]
            do = do_ref[:, sl]
            dob = do.astype(BF16)
            doq = (do * qd_ref[:, sl]).astype(BF16)
            state_prev = sprev_ref[0, h]
            dstate = ds_sc[h]
            dstate_b = dstate.astype(BF16)
            s_t = (_dot_nt(k, q) * mt_ref[h]).astype(BF16)
            ds_t = (_dot_nt(v, dob) * mt_ref[h]).astype(BF16)
            ds = (_dot_nt(dob, v) * m_ref[h]).astype(BF16)
            kd = kd_ref[:, sl]
            dq = _dot(ds, k) + _dot_nt(doq, state_prev)
            dk = _dot(ds_t, q) + _dot_nt(v, dstate_b) * kd
            k_scaled = (k.astype(F32) * kd).astype(BF16)
            dv = _dot(s_t, dob) + _dot(k_scaled, dstate_b)
            ds_sc[h] = dstate * cd_ref[:, sl] + _dot_tn(q, doq)
            for part, val in enumerate((unrotate(dq), unrotate(dk) * QK_SCALE, dv)):
                dproj_ref[:, part * D + h * DH:part * D + (h + 1) * DH] = val.astype(BF16)

    rev = lambda c: nc - 1 - c
    chunk_spec = lambda col: pl.BlockSpec((CHUNK, D), lambda c: (rev(c), col))
    const2 = lambda shape: pl.BlockSpec(shape, lambda c: (0, 0))
    const3 = pl.BlockSpec((HEADS, CHUNK, CHUNK), lambda c: (0, 0, 0))
    return _hosted_call(
        body,
        name="retention_bwd",
        grid=(nc,),
        in_specs=[
            pl.BlockSpec(memory_space=pl.ANY),
            chunk_spec(0), chunk_spec(1), chunk_spec(2), chunk_spec(0),
            pl.BlockSpec((1, HEADS, DH, DH), lambda c: (rev(c), 0, 0, 0)),
            const3, const3,
            const2((CHUNK, D)), const2((CHUNK, D)), const2((1, D)),
            pl.BlockSpec((CHUNK, DH), lambda c: (rev(c), 0)),
            pl.BlockSpec((CHUNK, DH), lambda c: (rev(c), 0)),
        ],
        out_specs=[pl.BlockSpec((CHUNK, 3 * D), lambda c: (rev(c), 0))],
        out_shape=[jax.ShapeDtypeStruct(dproj.shape, BF16)],
        aliases={0: 0},
        scratch_shapes=[pltpu.VMEM((HEADS, DH, DH), F32)],
        args=(dproj, proj_bf, proj_bf, proj_bf, do, sprev, intra, intra_t, q_dec, k_dec, c_dec, cos_t, sin_t),
        riders=riders,
    )


def _lru_bwd(dproj, proj, saved, dhl, conv_w, wa, wx, lam, riders=()):
    rows = proj.shape[0]
    TM = _lru_tile(rows)
    nt = rows // TM
    per8 = TM // 8

    def body(dproj_in_ref, x_ref, xprev_ref, h_ref, hprev_ref, c_ref, r_ref, i_ref, la_ref, mult_ref, dhl_ref,
             cw_ref, wa_ref, wx_ref, lam_ref,
             dproj_ref, dwa_ref, dwx_ref, stats_ref, anext_sc, dhnext_sc, dcnext_sc, c_sc, b_sc, dh_sc):
        step = pl.program_id(0)
        i = nt - 1 - step

        @pl.when(step == 0)
        def _():
            anext_sc[...] = jnp.zeros_like(anext_sc)
            dhnext_sc[...] = jnp.zeros_like(dhnext_sc)
            dcnext_sc[...] = jnp.zeros_like(dcnext_sc)
            dwa_ref[...] = jnp.zeros_like(dwa_ref)
            dwx_ref[...] = jnp.zeros_like(dwx_ref)
            stats_ref[...] = jnp.zeros_like(stats_ref)

        first = i == 0
        x = x_ref[...].astype(F32)
        prev8 = jnp.where(first, 0.0, xprev_ref[8:16, :].astype(F32))
        c_b = c_ref[...]
        c, r, gate_i, mult = (ref[...].astype(F32) for ref in (c_ref, r_ref, i_ref, mult_ref))
        a = jnp.exp(la_ref[...].astype(F32))
        sp = _softplus(-lam_ref[...])
        dh_sc[...] = dhl_ref[...].astype(F32)

        b_sc[...] = pltpu.roll(a, TM - 1, axis=0)
        b_sc[TM - 1:TM, :] = anext_sc[0:1, :]
        anext_sc[0:1, :] = a[0:1, :]
        row8 = lax.broadcasted_iota(jnp.int32, (8, D), 0)

        def group(gi, dhnext):
            r0 = pl.multiple_of((per8 - 1 - gi) * 8, 8)
            bb = b_sc[pl.ds(r0, 8), :]
            uu = dh_sc[pl.ds(r0, 8), :]
            for d in (1, 2, 4):
                b_sh = jnp.where(row8 < 8 - d, pltpu.roll(bb, 8 - d, axis=0), 1.0)
                u_sh = jnp.where(row8 < 8 - d, pltpu.roll(uu, 8 - d, axis=0), 0.0)
                uu = uu + bb * u_sh
                bb = bb * b_sh
            dhb = bb * dhnext + uu
            dh_sc[pl.ds(r0, 8), :] = dhb
            return dhb[0:1, :]

        dhfirst = lax.fori_loop(0, per8, group, dhnext_sc[0:1, :])
        dhnext_sc[0:1, :] = dhfirst
        dh = dh_sc[...]

        h = h_ref[...].astype(F32)
        hprev8 = jnp.where(first, 0.0, hprev_ref[8:16, :].astype(F32))
        h_dn, h_head = _shift_down(h, hprev8, 1)
        c_sc[...] = h_dn
        c_sc[0:8, :] = h_head
        h_before = c_sc[...]

        valid = _row_ids(i, TM, (TM, D)) >= PAD_ROWS
        da = dh * h_before
        du = jnp.where(valid, dh, 0.0)
        dmult = du * gate_i * c
        dgate_i = du * mult * c
        dc = du * mult * gate_i
        dla = da * a - dmult * (a * a) / mult
        dla = jnp.where(valid, dla, 0.0)
        dr = dla * ((-LRU_C) * sp)
        dzr = dr * r * (1.0 - r)
        dzi = dgate_i * gate_i * (1.0 - gate_i)
        stats_ref[1:2, :] += jnp.sum(dzr, axis=0, keepdims=True)
        stats_ref[2:3, :] += jnp.sum(dzi, axis=0, keepdims=True)
        stats_ref[3:4, :] += jnp.sum(dla * ((-LRU_C) * r), axis=0, keepdims=True)
        dc_gate = []
        for g in range(LRU_BLOCKS):
            sl = slice(g * LRU_BLOCK, (g + 1) * LRU_BLOCK)
            cg = c_b[:, sl]
            dzr_g = dzr[:, sl].astype(BF16)
            dzi_g = dzi[:, sl].astype(BF16)
            dc_gate.append(_dot_nt(dzr_g, wa_ref[g]) + _dot_nt(dzi_g, wx_ref[g]))
            dwa_ref[g] += _dot_tn(cg, dzr_g)
            dwx_ref[g] += _dot_tn(cg, dzi_g)
        dc = dc + jnp.concatenate(dc_gate, axis=1)

        cw = cw_ref[...]
        stats_ref[0:1, :] += jnp.sum(dc, axis=0, keepdims=True)
        stats_ref[7:8, :] += jnp.sum(dc * x, axis=0, keepdims=True)
        dx = cw[3:4] * dc
        tail_src = jnp.concatenate([dc[TM - 8:TM], dcnext_sc[...]], axis=0)
        dx_tail = cw[3:4] * dc[TM - 8:TM]
        for d in (1, 2, 3):
            dx = dx + cw[3 - d:4 - d] * pltpu.roll(dc, TM - d, axis=0)
            dx_tail = dx_tail + cw[3 - d:4 - d] * pltpu.roll(tail_src, 16 - d, axis=0)[0:8]
            rolled, hd = _shift_down(x, prev8, d)
            b_sc[...] = rolled
            b_sc[0:8, :] = hd
            stats_ref[7 - d:8 - d, :] += jnp.sum(dc * b_sc[...], axis=0, keepdims=True)
        dcnext_sc[...] = dc[0:8]
        dproj_ref[...] = dx.astype(BF16)
        dproj_ref[TM - 8:TM, :] = dx_tail.astype(BF16)

    rev = lambda s: nt - 1 - s
    vec = pl.BlockSpec((1, D), lambda s: (0, 0))
    wspec = pl.BlockSpec((LRU_BLOCKS, LRU_BLOCK, LRU_BLOCK), lambda s: (0, 0, 0))
    prev16 = lambda col: pl.BlockSpec((BF16_ROWS, D), lambda s: (jnp.maximum(rev(s) * (TM // BF16_ROWS) - 1, 0), col))
    tile = lambda: pl.BlockSpec((TM, D), lambda s: (rev(s), 0))
    h_lru, c_sv, r_sv, i_sv, la_sv, mult_sv = saved
    return _hosted_call(
        body,
        name="lru_bwd",
        grid=(nt,),
        in_specs=[
            pl.BlockSpec(memory_space=pl.ANY),
            pl.BlockSpec((TM, D), lambda s: (rev(s), LRU_IN_COL)), prev16(LRU_IN_COL),
            tile(), prev16(0),
            tile(), tile(), tile(), tile(), tile(), tile(),
            pl.BlockSpec((4, D), lambda s: (0, 0)),
            wspec, wspec, vec,
        ],
        out_specs=[
            pl.BlockSpec((TM, D), lambda s: (rev(s), LRU_IN_COL)),
            wspec, wspec,
            pl.BlockSpec((8, D), lambda s: (0, 0)),
        ],
        out_shape=[
            jax.ShapeDtypeStruct(dproj.shape, BF16),
            jax.ShapeDtypeStruct((LRU_BLOCKS, LRU_BLOCK, LRU_BLOCK), F32),
            jax.ShapeDtypeStruct((LRU_BLOCKS, LRU_BLOCK, LRU_BLOCK), F32),
            jax.ShapeDtypeStruct((8, D), F32),
        ],
        aliases={0: 0},
        scratch_shapes=[
            pltpu.VMEM((8, D), F32), pltpu.VMEM((8, D), F32), pltpu.VMEM((8, D), F32),
            pltpu.VMEM((TM, D), F32), pltpu.VMEM((TM, D), F32), pltpu.VMEM((TM, D), F32),
        ],
        args=(dproj, proj, proj, h_lru, h_lru, c_sv, r_sv, i_sv, la_sv, mult_sv, dhl, conv_w, wa, wx, lam),
        riders=riders,
    )


def _in_proj_bwd(dproj, w_in, part, prev=None, riders=()):
    rows = dproj.shape[0]
    tm = _heavy_tile(rows)
    nt = rows // tm
    first = 0 if part == 0 else (nt + 1) // 2
    count = (nt + 1) // 2 if part == 0 else nt - first

    def body(*refs):
        dproj_ref, w_hbm, du_ref, w_sc, w_sem = refs[-5:]

        @pl.when(pl.program_id(0) == 0)
        def _():
            moves = [((0, 3), 0), ((4, 1), 3), ((3, 1), 4), ((5, 3), 5)]
            copies = [pltpu.make_async_copy(w_hbm.at[:, pl.ds(src * D, n * D)], w_sc.at[:, pl.ds(dst * D, n * D)], w_sem.at[q])
                      for q, ((src, n), dst) in enumerate(moves)]
            for cp in copies:
                cp.start()
            for cp in copies:
                cp.wait()

        du_ref[...] = _dot_nt(dproj_ref[...], w_sc[...])

    in_specs = [pl.BlockSpec((tm, IN_COLS), lambda i: (first + i, 0)), _ANY]
    args = (dproj, w_in)
    if prev is not None:
        in_specs = [_ANY] + in_specs
        args = (prev,) + args
    return _hosted_call(
        body,
        name="in_proj_bwd_%d" % part,
        grid=(count,),
        in_specs=in_specs,
        out_specs=[pl.BlockSpec((tm, D), lambda i: (first + i, 0))],
        out_shape=[jax.ShapeDtypeStruct((rows, D), F32)],
        scratch_shapes=[pltpu.VMEM((D, IN_COLS), BF16), pltpu.SemaphoreType.DMA((4,))],
        aliases={0: 0} if prev is not None else None,
        args=args,
        riders=riders,
    )


def _norm1_bwd(du, dh1, head, x2d, rstd1, norm_w, riders=()):
    rows = du.shape[0]

    def body(du_ref, dh1_ref, head_ref, x_ref, rstd_ref, nw_ref, gx_ref, ghead_ref, stats_ref):
        i = pl.program_id(0)

        @pl.when(i == 0)
        def _():
            stats_ref[...] = jnp.zeros_like(stats_ref)

        def finish(h0, out_ref):
            du = du_ref[...]
            rs = rstd_ref[...]
            n = h0 * rs
            stats_ref[0:1, :] += jnp.sum(du * n, axis=0, keepdims=True)
            dn = du * nw_ref[...]
            out_ref[...] = dh1_ref[...] + rs * (dn - n * jnp.mean(dn * n, axis=-1, keepdims=True))

        @pl.when(i == 0)
        def _():
            finish(head_ref[...], ghead_ref)

        @pl.when(i > 0)
        def _():
            finish(x_ref[...], gx_ref)

    tile = pl.BlockSpec((TM, D), lambda i: (i, 0))
    return _hosted_call(
        body,
        name="norm1_bwd",
        grid=(rows // TM,),
        in_specs=[
            tile, tile,
            pl.BlockSpec((FRONT, D), lambda i: (0, 0)),
            pl.BlockSpec((TM, D), lambda i: (jnp.maximum(i - 1, 0), 0)),
            pl.BlockSpec((TM, 1), lambda i: (i, 0)),
            pl.BlockSpec((1, D), lambda i: (0, 0)),
        ],
        out_specs=[
            pl.BlockSpec((TM, D), lambda i: (jnp.maximum(i - 1, 0), 0)),
            pl.BlockSpec((FRONT, D), lambda i: (0, 0)),
            pl.BlockSpec((8, D), lambda i: (0, 0)),
        ],
        out_shape=[
            jax.ShapeDtypeStruct(x2d.shape, F32),
            jax.ShapeDtypeStruct((FRONT, D), F32),
            jax.ShapeDtypeStruct((8, D), F32),
        ],
        scratch_shapes=[],
        args=(du, dh1, head, x2d, rstd1, norm_w),
        riders=riders,
    )


def _matmul_tn(name, x, dy, out_cols, col0=0, prev=None, k_block=None, n_block=None, riders=(), col_map=None):
    col_map = col_map or (lambda n: n)
    rows, kdim = x.shape
    ndim = dy.shape[1]
    kb = k_block or kdim
    nb = n_block or ndim
    step_bytes = lambda t: 2 * t * (kb * x.dtype.itemsize + nb * dy.dtype.itemsize) + 2 * kb * nb * 4
    tr = next(t for t in (2816, 1408, TM_HEAVY, TM) if rows % t == 0 and (t == TM or step_bytes(t) <= TN_VMEM_BUDGET))
    nr, nk, nn = rows // tr, kdim // kb, ndim // nb
    cb0 = col0 // nb

    def body(*refs):
        x_ref, dy_ref, out_ref = refs[-3], refs[-2], refs[-1]
        part = _dot_tn(x_ref[...].astype(BF16), dy_ref[...].astype(BF16))

        @pl.when(pl.program_id(2) == 0)
        def _():
            out_ref[...] = part

        @pl.when(pl.program_id(2) > 0)
        def _():
            out_ref[...] += part

    in_specs = [
        pl.BlockSpec((tr, kb), lambda n, k, r: (r, k)),
        pl.BlockSpec((tr, nb), lambda n, k, r: (r, n)),
    ]
    args = [x, dy]
    aliases = {}
    if prev is not None:
        in_specs = [pl.BlockSpec(memory_space=pl.ANY)] + in_specs
        args = [prev] + args
        aliases = {0: 0}
    (out,), rider_outs = _hosted_call(
        body,
        name=name,
        grid=(nn, nk, nr),
        in_specs=in_specs,
        out_specs=[pl.BlockSpec((kb, nb), lambda n, k, r: (k, cb0 + col_map(n)))],
        out_shape=[jax.ShapeDtypeStruct((kdim, out_cols), F32)],
        scratch_shapes=[],
        aliases=aliases,
        args=args,
        riders=riders,
    )
    return (out, rider_outs) if riders else out


def _local_step(x2d, target, w, plan):
    rows = FRONT + x2d.shape[0]
    head = jnp.concatenate([jnp.zeros((PAD_ROWS, D), F32), w["meta_tokens"]], axis=0)
    cos_t, sin_t = _rope_tables(rows)
    intra, intra_t, q_dec, k_dec, c_dec = _decay_tables()
    grads = {}

    def hosted(host, fn, *args, **kwargs):
        outs, rider_outs = fn(*args, riders=plan.riders(host, w, grads), **kwargs)
        plan.after(host, rider_outs, w, grads)
        return outs

    (u1, rstd1), _ = _norm1(head, x2d, w["mix_norm_w"])
    proj, w["w_in"] = hosted("in_proj", _in_proj, u1, w["w_in_shard"], w["route"], cos_t, sin_t)
    o, sprev = hosted("retention_fwd", _retention_fwd, proj, intra, q_dec, k_dec, c_dec)
    lru_args = (w["conv_w"], w["conv_b"], w["lru_wa"], w["lru_wx"], w["lru_ba"], w["lru_bx"], w["lru_lambda"])
    lru_saved = hosted("lru_fwd", _lru_fwd, proj, *lru_args)
    h_lru = lru_saved[0]
    y_ret, y_lru, h1, u2, rstd2 = hosted("mix_fwd", _mix_fwd, head, x2d, o, proj, h_lru, w["w_branch_ret"],
                                         w["w_branch_lru"], w["w_out"], w["ffn_norm_w"])
    silu, dsilu, act = hosted("ffn_fwd_gate", _ffn_fwd_gate, u2, w["w_ffn_in"])
    dh2, stats_loss = _ffn_out_loss(act, h1, w["w_ffn_out"], target, w["final_norm_w"])

    dg, dup, dh2b, dh1, stats_ffn = _ffn_bwd(dh2, silu, dsilu, h1, rstd2, w["w_ffn_in"], w["w_ffn_out"], w["ffn_norm_w"])
    grads["w_ffn_in"] = _matmul_tn("dw_ffn_up", u2, dup, 2 * FFN, col0=FFN, n_block=FFN_HALF,
                                   prev=_matmul_tn("dw_ffn_gate", u2, dg, 2 * FFN, n_block=FFN_HALF))
    grads["w_ffn_out"] = _matmul_tn("dw_ffn_out", act, dh2b, D, k_block=FFN_HALF)
    (dproj, do, dhl, mixed, a_ret, a_lru, dy_ret, dy_lru) = hosted(
        "mix_bwd", _mix_bwd, dh1, o, proj, h_lru, y_ret, y_lru, w["w_branch_ret"], w["w_branch_lru"], w["w_out"])
    grads["w_out"] = _matmul_tn("dw_out", mixed, dh1, D)
    grads["w_branch_ret"] = _matmul_tn("dw_branch_ret", a_ret, dy_ret, D)
    grads["w_branch_lru"] = _matmul_tn("dw_branch_lru", a_lru, dy_lru, D)
    (dproj,) = hosted("retention_bwd", _retention_bwd, dproj, proj, do, sprev, intra, intra_t, q_dec, k_dec, c_dec,
                      cos_t, sin_t)
    dproj, grads["lru_wa"], grads["lru_wx"], stats_lru = hosted(
        "lru_bwd", _lru_bwd, dproj, proj, lru_saved, dhl, w["conv_w"], w["lru_wa"], w["lru_wx"], w["lru_lambda"])
    grads["w_in"], rider_outs = _matmul_tn("dw_in", u1, dproj, IN_COLS, n_block=D, col_map=_swap_3_4,
                                           riders=plan.riders("dw_in", w, grads))
    plan.after("dw_in", rider_outs, w, grads)
    (du1,) = hosted("in_proj_bwd_0", _in_proj_bwd, dproj, w["w_in"], 0)
    (du1,) = hosted("in_proj_bwd_1", _in_proj_bwd, dproj, w["w_in"], 1, du1)
    (grad_x, grad_head, stats_in), _ = _norm1_bwd(du1, dh1, head, x2d, rstd1, w["mix_norm_w"])
    return grad_x, grad_head, grads, [stats_loss, stats_ffn, stats_in, stats_lru]


BIG_PIECES = {
    "w_in": ("col", (D, 2 * D)),
    "w_ffn_in": ("col", (D, FFN_HALF)),
    "w_ffn_out": ("row", (FFN // 4, D)),
    "w_branch_ret": ("row", (D // 4, D)),
    "w_branch_lru": ("row", (D // 4, D)),
    "w_out": ("row", (D // 4, D)),
    "lru_wa": ("lru", (LRU_BLOCKS, LRU_BLOCK // 4, LRU_BLOCK)),
    "lru_wx": ("lru", (LRU_BLOCKS, LRU_BLOCK // 4, LRU_BLOCK)),
}
SMALL_PIECES = {"meta_tokens": ("col", (N_META, D // 4)), "conv_w": ("col", (4, D // 4))}


def _full_shape(kind, shard):
    if kind == "col":
        return (shard[0], 4 * shard[1])
    if kind == "row":
        return (4 * shard[0], shard[1])
    return (shard[0], 4 * shard[1], shard[2])


def _half_shape(kind, shard):
    return (shard[0] // 2,) + tuple(shard[1:])


def _aligned(start, multiple):
    return start if isinstance(start, int) else pl.multiple_of(start, multiple)


def _lead(h, size):
    if h is None:
        return pl.ds(0, size)
    return pl.ds(_aligned(h * (size // 2), size // 2), size // 2)


def _full_region(ref, kind, shard, s, h):
    if kind == "col":
        return ref.at[_lead(h, shard[0]), pl.ds(_aligned(s * shard[1], shard[1]), shard[1])]
    if kind == "row":
        size = shard[0] if h is None else shard[0] // 2
        start = s * shard[0] + (0 if h is None else h * (shard[0] // 2))
        return ref.at[pl.ds(_aligned(start, BF16_ROWS), size), :]
    return ref.at[_lead(h, shard[0]), pl.ds(_aligned(s * shard[1], shard[1]), shard[1]), :]


def _shard_region(ref, shard, h):
    return ref.at[_lead(h, shard[0])]


def _place():
    x, y, c = lax.axis_index("x"), lax.axis_index("y"), lax.axis_index("c")
    return x, y, c, 2 * x + y


def _other_chip(s, c, k):
    s2 = jnp.bitwise_xor(s, k)
    return s2, (s2 // 2, s2 % 2, c)


def _remote(src, dst, send_sem, recv_sem, dev):
    return pltpu.make_async_remote_copy(src_ref=src, dst_ref=dst, send_sem=send_sem, recv_sem=recv_sem,
                                        device_id=dev, device_id_type=MESH)


_ANY = pl.BlockSpec(memory_space=pl.ANY)


class _Rider:
    def __init__(self, ins, out_shapes, sem_shapes, build, aliased=False):
        self.ins, self.out_shapes, self.sem_shapes, self.build, self.aliased = ins, out_shapes, sem_shapes, build, aliased


def _hosted_call(body, *, name, grid, in_specs, out_specs, out_shape, scratch_shapes, args, riders=(), aliases=None,
                 prefetch=None, riders_after_body=False):
    n_in, n_out, n_sc = len(in_specs), len(out_shape), len(scratch_shapes)
    r_in = [a for r in riders for a in r.ins]
    r_out = [s for r in riders for s in r.out_shapes]
    r_sem = [s for r in riders for s in r.sem_shapes]
    lead = () if prefetch is None else (prefetch,)
    assert prefetch is None or not (aliases or any(r.aliased for r in riders))

    def full_body(*refs):
        head, refs = refs[:len(lead)], refs[len(lead):]
        ins, rin = refs[:n_in], refs[n_in:n_in + len(r_in)]
        o0 = n_in + len(r_in)
        outs, rout = refs[o0:o0 + n_out], refs[o0 + n_out:o0 + n_out + len(r_out)]
        s0 = o0 + n_out + len(r_out)
        scratch, rsem = refs[s0:s0 + n_sc], refs[s0 + n_sc:]
        starts, waits = [], []
        pi = po = ps = 0
        for r in riders:
            st, wt = r.build(rin[pi:pi + len(r.ins)], rout[po:po + len(r.out_shapes)], rsem[ps:ps + len(r.sem_shapes)])
            starts += st
            waits += wt
            pi, po, ps = pi + len(r.ins), po + len(r.out_shapes), ps + len(r.sem_shapes)
        first = functools.reduce(jnp.logical_and, [pl.program_id(d) == 0 for d in range(len(grid))])
        last = functools.reduce(jnp.logical_and, [pl.program_id(d) == grid[d] - 1 for d in range(len(grid))])

        def start_riders():
            @pl.when(first)
            def _():
                for cp in starts:
                    cp.start()

        if riders and not riders_after_body:
            start_riders()
        body(*head, *ins, *outs, *scratch)
        if riders and riders_after_body:
            start_riders()
        if riders:
            @pl.when(last)
            def _():
                for wait in waits:
                    wait()

    io_aliases = dict(aliases or {})
    pi = po = 0
    for r in riders:
        if r.aliased:
            for q in range(len(r.ins)):
                io_aliases[n_in + pi + q] = n_out + po + q
        pi, po = pi + len(r.ins), po + len(r.out_shapes)
    specs = dict(
        grid=grid,
        in_specs=list(in_specs) + [_ANY] * len(r_in),
        out_specs=list(out_specs) + [_ANY] * len(r_out),
        scratch_shapes=list(scratch_shapes) + r_sem,
    )
    if prefetch is not None:
        specs = dict(grid_spec=pltpu.PrefetchScalarGridSpec(num_scalar_prefetch=1, **specs))
    res = pl.pallas_call(
        full_body,
        name=name,
        out_shape=list(out_shape) + r_out,
        input_output_aliases=io_aliases,
        compiler_params=_params(("arbitrary",) * len(grid)),
        **specs,
    )(*lead, *args, *r_in)
    rider_outs, po = [], n_out
    for r in riders:
        rider_outs.append(list(res[po:po + len(r.out_shapes)]))
        po += len(r.out_shapes)
    return list(res[:n_out]), rider_outs


def _run_riders(name, riders):
    r_in = [a for r in riders for a in r.ins]
    r_out = [s for r in riders for s in r.out_shapes]
    r_sem = [s for r in riders for s in r.sem_shapes]

    def body(*refs):
        rin, rout, rsem = refs[:len(r_in)], refs[len(r_in):len(r_in) + len(r_out)], refs[len(r_in) + len(r_out):]
        pi = po = ps = 0
        all_waits = []
        for r in riders:
            starts, waits = r.build(rin[pi:pi + len(r.ins)], rout[po:po + len(r.out_shapes)], rsem[ps:ps + len(r.sem_shapes)])
            for cp in starts:
                cp.start()
            all_waits += waits
            pi, po, ps = pi + len(r.ins), po + len(r.out_shapes), ps + len(r.sem_shapes)
        for wait in all_waits:
            wait()

    io_aliases = {}
    pi = po = 0
    for r in riders:
        if r.aliased:
            for q in range(len(r.ins)):
                io_aliases[pi + q] = po + q
        pi, po = pi + len(r.ins), po + len(r.out_shapes)
    res = pl.pallas_call(
        body,
        name=name,
        in_specs=[_ANY] * len(r_in),
        out_specs=[_ANY] * len(r_out),
        out_shape=r_out,
        scratch_shapes=r_sem,
        input_output_aliases=io_aliases,
    )(*r_in)
    outs, po = [], 0
    for r in riders:
        outs.append(list(res[po:po + len(r.out_shapes)]))
        po += len(r.out_shapes)
    return outs


def _piece(name):
    if name in BIG_PIECES:
        return (name, *BIG_PIECES[name], True)
    return (name, *SMALL_PIECES[name], False)


def _gather_rider(shards, names):
    pieces = [_piece(n) for n in names]
    n = len(pieces)

    def build(ins, outs, sems):
        local_sem, ici_send, ici_recv = sems
        _, _, c, s = _place()
        starts, waits = [], []
        for p, (_, kind, shard, split) in enumerate(pieces):
            cp = pltpu.make_async_copy(ins[p], _full_region(outs[p], kind, shard, s, None), local_sem.at[p])
            starts.append(cp)
            waits.append(cp.wait)
            h = c if split else None
            for k in (1, 2, 3):
                s2, dev = _other_chip(s, c, k)
                cp = _remote(_shard_region(ins[p], shard, h), _full_region(outs[p], kind, shard, s, h),
                             ici_send.at[p, k - 1], ici_recv.at[p, k - 1], dev)
                starts.append(cp)
                waits.append(cp.wait_send)
                region = _full_region(outs[p], kind, shard, s2, h)
                waits.append(_remote(region, region, ici_send.at[p, k - 1], ici_recv.at[p, k - 1], dev).wait_recv)
        return starts, waits

    return _Rider(
        [shards[name] for name in names],
        [jax.ShapeDtypeStruct(_full_shape(kind, shard), shards[name].dtype) for name, kind, shard, _ in pieces],
        [pltpu.SemaphoreType.DMA((n,)), pltpu.SemaphoreType.DMA((n, 3)), pltpu.SemaphoreType.DMA((n, 3))],
        build)


def _forward_rider(gathered, names):
    pieces = [_piece(n) for n in names]
    n = len(pieces)

    def build(ins, outs, sems):
        fwd_send, fwd_recv = sems
        x, y, c, s = _place()
        sibling = (x, y, 1 - c)
        starts, waits = [], []
        for p, (_, kind, shard, _) in enumerate(pieces):
            for k in (1, 2, 3):
                s2, _ = _other_chip(s, c, k)
                mine = _full_region(outs[p], kind, shard, s2, c)
                theirs = _full_region(outs[p], kind, shard, s2, 1 - c)
                cp = _remote(mine, mine, fwd_send.at[p, k - 1], fwd_recv.at[p, k - 1], sibling)
                starts.append(cp)
                waits.append(cp.wait_send)
                waits.append(_remote(theirs, theirs, fwd_send.at[p, k - 1], fwd_recv.at[p, k - 1], sibling).wait_recv)
        return starts, waits

    return _Rider(
        [gathered[name] for name in names],
        [jax.ShapeDtypeStruct(gathered[name].shape, gathered[name].dtype) for name in names],
        [pltpu.SemaphoreType.DMA((n, 3)), pltpu.SemaphoreType.DMA((n, 3))],
        build, aliased=True)


def _pair_exchange_rider(grads, names):
    n = len(names)

    def build(ins, outs, sems):
        send_sem, recv_sem = sems
        x, y, c, _ = _place()
        sibling = (x, y, 1 - c)
        starts, waits = [], []
        for p, name in enumerate(names):
            kind, shard = BIG_PIECES[name]
            for s2 in range(4):
                cp = _remote(_full_region(ins[p], kind, shard, s2, 1 - c), outs[p].at[s2], send_sem.at[p, s2],
                             recv_sem.at[p, s2], sibling)
                starts.append(cp)
                waits.append(cp.wait_send)
                waits.append(_remote(outs[p].at[s2], outs[p].at[s2], send_sem.at[p, s2], recv_sem.at[p, s2], sibling).wait_recv)
        return starts, waits

    return _Rider(
        [grads[name] for name in names],
        [jax.ShapeDtypeStruct((4,) + _half_shape(*BIG_PIECES[name]), F32) for name in names],
        [pltpu.SemaphoreType.DMA((n, 4))] * 2,
        build)


def _half_specs(kind, shard):
    half = shard[0] // 2
    if kind == "col":
        full = pl.BlockSpec((half, shard[1]), lambda j, pr: (pr[1], j))
        buf = pl.BlockSpec((None, half, shard[1]), lambda j, pr: (j, 0, 0))
    elif kind == "row":
        full = pl.BlockSpec((half, shard[1]), lambda j, pr: (2 * j + pr[1], 0))
        buf = pl.BlockSpec((None, half, shard[1]), lambda j, pr: (j, 0, 0))
    else:
        full = pl.BlockSpec((half, shard[1], shard[2]), lambda j, pr: (pr[1], j, 0))
        buf = pl.BlockSpec((None, half, shard[1], shard[2]), lambda j, pr: (j, 0, 0, 0))
    return full, buf


def _pair_sum(name, grad, recv, place):
    kind, shard = BIG_PIECES[name]
    full, buf = _half_specs(kind, shard)

    def body(pr, g_ref, r_ref, o_ref):
        o_ref[...] = (g_ref[...] + r_ref[...]).astype(BF16)

    return pl.pallas_call(
        body,
        name="pair_sum_" + name,
        grid_spec=pltpu.PrefetchScalarGridSpec(num_scalar_prefetch=1, grid=(4,), in_specs=[full, buf], out_specs=buf),
        out_shape=jax.ShapeDtypeStruct((4,) + _half_shape(kind, shard), BF16),
        compiler_params=_params(("arbitrary",)),
    )(place, grad, recv)


def _chip_exchange_rider(sums, names):
    n = len(names)

    def build(ins, outs, sems):
        send_sem, recv_sem = sems
        _, _, c, s = _place()
        starts, waits = [], []
        for p in range(n):
            for k in (1, 2, 3):
                s2, dev = _other_chip(s, c, k)
                cp = _remote(ins[p].at[s2], outs[p].at[k - 1], send_sem.at[p, k - 1], recv_sem.at[p, k - 1], dev)
                starts.append(cp)
                waits.append(cp.wait_send)
                waits.append(_remote(outs[p].at[k - 1], outs[p].at[k - 1], send_sem.at[p, k - 1], recv_sem.at[p, k - 1],
                                     dev).wait_recv)
        return starts, waits

    return _Rider(
        [sums[name] for name in names],
        [jax.ShapeDtypeStruct((3,) + _half_shape(*BIG_PIECES[name]), BF16) for name in names],
        [pltpu.SemaphoreType.DMA((n, 3))] * 2,
        build)


def _chip_sum(name, grad, recv_pair, recv_chip, place):
    kind, shard = BIG_PIECES[name]
    half = shard[0] // 2
    tail = tuple(shard[1:])
    zeros = (0,) * len(tail)
    nt = 4 if kind != "lru" and half % (4 * BF16_ROWS) == 0 else 1
    rows = half // nt
    if kind == "col":
        full = pl.BlockSpec((rows,) + tail, lambda j, pr: (pr[1] * nt + j, pr[0]))
    elif kind == "row":
        full = pl.BlockSpec((rows,) + tail, lambda j, pr: ((2 * pr[0] + pr[1]) * nt + j, 0))
    else:
        full = pl.BlockSpec((rows,) + tail, lambda j, pr: (pr[1], pr[0], 0))
    pair = pl.BlockSpec((None, rows) + tail, lambda j, pr: (pr[0], j) + zeros)
    chip = pl.BlockSpec((3, rows) + tail, lambda j, pr: (0, j) + zeros)
    out = pl.BlockSpec((rows,) + tail, lambda j, pr: (pr[1] * nt + j,) + zeros)

    def body(pr, g_ref, rp_ref, rc_ref, o_ref):
        total = g_ref[...] + rp_ref[...]
        for k in range(3):
            total = total + rc_ref[k].astype(F32)
        o_ref[...] = total

    return pl.pallas_call(
        body,
        name="chip_sum_" + name,
        grid_spec=pltpu.PrefetchScalarGridSpec(num_scalar_prefetch=1, grid=(nt,), in_specs=[full, pair, chip], out_specs=out),
        out_shape=jax.ShapeDtypeStruct(shard, F32),
        compiler_params=_params(("arbitrary",)),
    )(place, grad, recv_pair, recv_chip)


def _sibling_exchange_rider(halves, names):
    n = len(names)

    def build(ins, outs, sems):
        send_sem, recv_sem = sems
        x, y, c, _ = _place()
        sibling = (x, y, 1 - c)
        starts, waits = [], []
        for p, name in enumerate(names):
            shard = BIG_PIECES[name][1]
            mine = _shard_region(outs[p], shard, c)
            theirs = _shard_region(outs[p], shard, 1 - c)
            cp = _remote(mine, mine, send_sem.at[p], recv_sem.at[p], sibling)
            starts.append(cp)
            waits.append(cp.wait_send)
            waits.append(_remote(theirs, theirs, send_sem.at[p], recv_sem.at[p], sibling).wait_recv)
        return starts, waits

    return _Rider(
        [halves[name] for name in names],
        [jax.ShapeDtypeStruct(BIG_PIECES[name][1], F32) for name in names],
        [pltpu.SemaphoreType.DMA((n,))] * 2,
        build, aliased=True)


FIRST_WEIGHTS = ["meta_tokens", "conv_w"]
WEIGHT_GROUPS = {
    "lru": ["lru_wa", "lru_wx"],
    "branch": ["w_branch_ret", "w_branch_lru", "w_out"],
    "ffn_in": ["w_ffn_in"],
    "ffn_out": ["w_ffn_out"],
}
WEIGHT_SCHEDULE = {
    "in_proj": [("gather", "lru")],
    "retention_fwd": [("forward", "lru"), ("gather", "branch")],
    "lru_fwd": [("forward", "branch"), ("gather", "ffn_in")],
    "mix_fwd": [("forward", "ffn_in"), ("gather", "ffn_out")],
    "ffn_fwd_gate": [("forward", "ffn_out")],
}
GRAD_GROUPS = {
    "ffn": ["w_ffn_in", "w_ffn_out"],
    "mixer": ["w_out", "w_branch_ret", "w_branch_lru", "lru_wa", "lru_wx"],
    "in": ["w_in"],
}
GRAD_SCHEDULE = {
    "mix_bwd": [("pair", "ffn")],
    "retention_bwd": [("chip", "ffn")],
    "lru_bwd": [("sibling", "ffn")],
    "dw_in": [("pair", "mixer")],
    "in_proj_bwd_0": [("chip", "mixer"), ("pair", "in")],
    "in_proj_bwd_1": [("sibling", "mixer"), ("chip", "in")],
}


class _CommPlan:
    def __init__(self, shards, place):
        self.shards, self.place = shards, place
        self.late = {}
        self.recv_pair, self.sums, self.recv_chip, self.halves, self.final = {}, {}, {}, {}, {}

    def _grad_rider(self, stage, group, grads):
        names = GRAD_GROUPS[group]
        if stage == "pair":
            return _pair_exchange_rider(grads, names)
        if stage == "chip":
            return _chip_exchange_rider(self.sums, names)
        return _sibling_exchange_rider(self.halves, names)

    def _grad_after(self, stage, group, outs, grads):
        names = GRAD_GROUPS[group]
        if stage == "pair":
            for n, o in zip(names, outs):
                self.recv_pair[n] = o
                self.sums[n] = _pair_sum(n, grads[n], o, self.place)
        elif stage == "chip":
            for n, o in zip(names, outs):
                self.halves[n] = _chip_sum(n, grads[n], self.recv_pair[n], o, self.place)
        else:
            self.final.update(zip(names, outs))

    def riders(self, host, w, grads):
        if host in WEIGHT_SCHEDULE:
            return [_gather_rider(self.shards, WEIGHT_GROUPS[group]) if stage == "gather"
                    else _forward_rider(self.late, WEIGHT_GROUPS[group]) for stage, group in WEIGHT_SCHEDULE[host]]
        return [self._grad_rider(stage, group, grads) for stage, group in GRAD_SCHEDULE.get(host, [])]

    def after(self, host, rider_outs, w, grads):
        for (stage, group), outs in zip(WEIGHT_SCHEDULE.get(host, []), rider_outs):
            (self.late if stage == "gather" else w).update(zip(WEIGHT_GROUPS[group], outs))
        for (stage, group), outs in zip(GRAD_SCHEDULE.get(host, []), rider_outs):
            self._grad_after(stage, group, outs, grads)

    def finish(self, partial):
        outs, (blocks,) = _run_riders("tail_exchange", [_sibling_exchange_rider(self.halves, GRAD_GROUPS["in"]),
                                                        _small_exchange_rider(partial)])
        self.final.update(zip(GRAD_GROUPS["in"], outs))
        return self.final, blocks


def _adamw_math(w, g, m, v):
    m = ADAM_B1 * m + (1.0 - ADAM_B1) * g
    v = ADAM_B2 * v + (1.0 - ADAM_B2) * (g * g)
    m_hat = m / (1.0 - ADAM_B1 ** ADAM_STEP)
    v_hat = v / (1.0 - ADAM_B2 ** ADAM_STEP)
    delta = -ADAM_LR * (m_hat / (jnp.sqrt(v_hat) + ADAM_EPS) + ADAM_WD * w)
    return delta, m, v


def _adamw(name, g, w, m, v):
    rows, cols = g.shape
    tr = rows // 4 if rows % 32 == 0 else rows

    def body(g_ref, w_ref, m_ref, v_ref, go_ref, d_ref, mo_ref, vo_ref):
        gv = g_ref[...]
        delta, m2, v2 = _adamw_math(w_ref[...], gv, m_ref[...], v_ref[...])
        go_ref[...] = gv
        d_ref[...] = delta
        mo_ref[...] = m2
        vo_ref[...] = v2

    spec = pl.BlockSpec((tr, cols), lambda i: (i, 0))
    return pl.pallas_call(
        body,
        name="adamw_" + name,
        grid=(rows // tr,),
        in_specs=[spec] * 4,
        out_specs=[spec] * 4,
        out_shape=[jax.ShapeDtypeStruct((rows, cols), F32)] * 4,
        compiler_params=_params(("arbitrary",)),
    )(g, w, m, v)


SMALL_ROWS = 48
VEC_ROWS = {"final_norm_w": 1, "ffn_norm_w": 8, "mix_norm_w": 16, "conv_b": 24, "lru_ba": 25, "lru_bx": 26, "lru_lambda": 27}
VEC_NAMES = list(VEC_ROWS)
CONV_W_ROW = 28
META_ROW = 32


def _small_exchange_rider(partial):
    def build(ins, outs, sems):
        local_sem, send_sem, recv_sem = sems
        _, _, c, s = _place()
        me = 2 * s + c
        cp = pltpu.make_async_copy(ins[0], outs[0].at[me], local_sem)
        starts, waits = [cp], [cp.wait]
        for k in range(1, 8):
            peer = jnp.bitwise_xor(me, k)
            dev = (peer // 4, (peer // 2) % 2, peer % 2)
            rd = _remote(ins[0], outs[0].at[me], send_sem.at[k - 1], recv_sem.at[k - 1], dev)
            starts.append(rd)
            waits.append(rd.wait_send)
            waits.append(_remote(ins[0], outs[0].at[peer], send_sem.at[k - 1], recv_sem.at[k - 1], dev).wait_recv)
        return starts, waits

    return _Rider([partial], [jax.ShapeDtypeStruct((8, SMALL_ROWS, D), F32)],
                  [pltpu.SemaphoreType.DMA, pltpu.SemaphoreType.DMA((7,)), pltpu.SemaphoreType.DMA((7,))], build)


def _small_update(blocks, place, vecs, conv, meta):
    nvec = len(VEC_NAMES)
    qcols = D // 4

    def in_order(ref):
        total = ref[0]
        for d in range(1, 8):
            total = total + ref[d]
        return total

    def body(pr, blocks_ref, cols_ref, *refs):
        vec_refs = refs[:3 * nvec]
        conv_refs = refs[3 * nvec:3 * nvec + 3]
        meta_refs = refs[3 * nvec + 3:3 * nvec + 6]
        outs = refs[3 * nvec + 6:]
        loss_ref, vec_out, conv_out, meta_out = outs[0], outs[1:5], outs[5:9], outs[9:13]
        tot, col = in_order(blocks_ref), in_order(cols_ref)
        loss_ref[...] = jnp.sum(tot[0:1, :], axis=1, keepdims=True)
        for o in vec_out:
            o[...] = jnp.zeros_like(o)
        for j, name in enumerate(VEC_NAMES):
            w, m, v = (r[...] for r in vec_refs[3 * j:3 * j + 3])
            g = tot[VEC_ROWS[name]:VEC_ROWS[name] + 1, :]
            if name == "lru_lambda":
                g = -g / (1.0 + jnp.exp(w))
            for o, val in zip(vec_out, (g,) + _adamw_math(w, g, m, v)):
                o[j:j + 1, :] = val
        for row, n_rows, ins, group in ((CONV_W_ROW, 4, conv_refs, conv_out), (META_ROW, N_META, meta_refs, meta_out)):
            g = col[row:row + n_rows, :]
            w, m, v = (r[...] for r in ins)
            for o, val in zip(group, (g,) + _adamw_math(w, g, m, v)):
                o[...] = val

    whole = lambda shape: pl.BlockSpec(shape, lambda i, pr: (0,) * len(shape))
    in_specs = [whole((8, SMALL_ROWS, D)), pl.BlockSpec((8, SMALL_ROWS, qcols), lambda i, pr: (0, 0, pr[0]))]
    in_specs += [whole((1, D))] * (3 * nvec) + [whole((4, qcols))] * 3 + [whole((N_META, qcols))] * 3
    out_shapes = [(1, 1)] + [(8, D)] * 4 + [(4, qcols)] * 4 + [(N_META, qcols)] * 4
    return pl.pallas_call(
        body,
        name="small_update",
        grid_spec=pltpu.PrefetchScalarGridSpec(num_scalar_prefetch=1, grid=(1,), in_specs=in_specs,
                                               out_specs=[whole(s) for s in out_shapes]),
        out_shape=[jax.ShapeDtypeStruct(s, F32) for s in out_shapes],
        compiler_params=_params(("arbitrary",)),
    )(place, blocks, blocks, *[a for t in vecs for a in t], *conv, *meta)


WEIGHT_ORDER = ["meta_tokens", "mix_norm_w", "w_in", "conv_w", "conv_b", "lru_wa", "lru_ba", "lru_wx", "lru_bx", "lru_lambda",
                "w_branch_ret", "w_branch_lru", "w_out", "ffn_norm_w", "w_ffn_in", "w_ffn_out", "final_norm_w"]


def kernel(x, meta_tokens, mix_norm_w, w_in, conv_w, conv_b, lru_wa, lru_ba, lru_wx, lru_bx, lru_lambda, w_branch_ret, w_branch_lru, w_out, ffn_norm_w, w_ffn_in, w_ffn_out, final_norm_w, loss_target, m_meta_tokens, m_mix_norm_w, m_w_in, m_conv_w, m_conv_b, m_lru_wa, m_lru_ba, m_lru_wx, m_lru_bx, m_lru_lambda, m_w_branch_ret, m_w_branch_lru, m_w_out, m_ffn_norm_w, m_w_ffn_in, m_w_ffn_out, m_final_norm_w, v_meta_tokens, v_mix_norm_w, v_w_in, v_conv_w, v_conv_b, v_lru_wa, v_lru_ba, v_lru_wx, v_lru_bx, v_lru_lambda, v_w_branch_ret, v_w_branch_lru, v_w_out, v_ffn_norm_w, v_w_ffn_in, v_w_ffn_out, v_final_norm_w):
    args = locals()
    wts = {n: args[n] for n in WEIGHT_ORDER}
    mom = {n: args["m_" + n] for n in WEIGHT_ORDER}
    var = {n: args["v_" + n] for n in WEIGHT_ORDER}
    place = jnp.stack([2 * lax.axis_index("x") + lax.axis_index("y"), lax.axis_index("c")]).astype(jnp.int32)

    shards = {n: wts[n][0].astype(BF16) for n in BIG_PIECES}
    shards["meta_tokens"] = wts["meta_tokens"]
    shards["conv_w"] = wts["conv_w"][0]
    plan = _CommPlan(shards, place)
    (first,) = _run_riders("gather_first", [_gather_rider(shards, FIRST_WEIGHTS)])
    w = dict(zip(FIRST_WEIGHTS, first))
    for n in VEC_NAMES:
        w[n] = wts[n].reshape(1, D)
    chips = jnp.bitwise_xor(place[0], jnp.arange(4, dtype=jnp.int32))
    w["route"] = jnp.concatenate([place, jnp.stack([2 * chips, 2 * chips + 1], axis=1).reshape(8)])
    w["w_in_shard"] = shards["w_in"]

    grad_x, grad_head, _, stats = _local_step(x[0], loss_target[0], w, plan)
    partial = jnp.concatenate(stats + [grad_head[PAD_ROWS:]], axis=0)
    shard_grads, blocks = plan.finish(partial)

    out = {}
    for n in BIG_PIECES:
        shape2d = (-1, wts[n].shape[-1])
        res = _adamw(n, *[a.reshape(shape2d) for a in (shard_grads[n], wts[n], mom[n], var[n])])
        out[n] = [r.reshape(wts[n].shape) for r in res]

    vecs = [tuple(a[n].reshape(1, D) for a in (wts, mom, var)) for n in VEC_NAMES]
    conv = tuple(a["conv_w"][0] for a in (wts, mom, var))
    meta = tuple(a["meta_tokens"] for a in (wts, mom, var))
    res = _small_update(blocks, place, vecs, conv, meta)
    loss = res[0].reshape(())
    for j, n in enumerate(VEC_NAMES):
        out[n] = [r[j].reshape(wts[n].shape) for r in res[1:5]]
    out["conv_w"] = [r.reshape(wts["conv_w"].shape) for r in res[5:9]]
    out["meta_tokens"] = list(res[9:13])

    return (loss, grad_x.reshape(x.shape)) + tuple(out[n][kind] for kind in range(4) for n in WEIGHT_ORDER)
```

```python
import functools
import math

import jax
import jax.numpy as jnp
from jax import lax
from jax.experimental import pallas as pl
from jax.experimental.pallas import tpu as pltpu

F32 = jnp.float32
BF16 = jnp.bfloat16

LANES = 128
BF16_ROWS = 16

D = 1024
HEADS = 8
DH = 128
CHUNK = 256
N_META = 16
FRONT = 256
PAD_ROWS = FRONT - N_META
LRU_BLOCKS = 4
LRU_BLOCK = 256
LRU_C = 8.0
FFN = 2816
FFN_HALF = FFN // 2
IN_COLS = 8 * D
ROPE_BASE = 10000.0
EPS = 1e-6
QK_SCALE = DH ** -0.5

ADAM_LR = 0.001
ADAM_B1 = 0.9
ADAM_B2 = 0.999
ADAM_EPS = 1e-08
ADAM_WD = 0.01
ADAM_STEP = 10

TM = 256
TM_HEAVY = 768
MXU_TILE = 256
FFN_SUB = 2 * MXU_TILE
_FFN_SUBS = [(a, min(a + FFN_SUB, FFN)) for a in range(0, FFN, FFN_SUB)]
TM_MIX_BWD = 256
TM_LRU = 384
TM_MIX_FWD = 384
TM_IN_PROJ = 1408
VMEM_LIMIT = 60 * 1024 * 1024
TN_VMEM_BUDGET = 40 * 1024 * 1024

NT_DIMS = (((1,), (1,)), ((), ()))
TN_DIMS = (((0,), (0,)), ((), ()))
MESH = pl.DeviceIdType.MESH


def _params(sem=None):
    if sem is None:
        return pltpu.CompilerParams(vmem_limit_bytes=VMEM_LIMIT)
    return pltpu.CompilerParams(dimension_semantics=sem, vmem_limit_bytes=VMEM_LIMIT)


def _dot(a, b):
    return jnp.dot(a, b, preferred_element_type=F32)


def _dot_nt(a, b):
    return lax.dot_general(a, b, NT_DIMS, preferred_element_type=F32)


def _dot_tn(a, b):
    return lax.dot_general(a, b, TN_DIMS, preferred_element_type=F32)


def _sigmoid(z):
    return 1.0 / (1.0 + jnp.exp(-z))


def _log1p(x):
    return jnp.where(x < 1e-3, x * (1.0 - x * (0.5 - x * (1.0 / 3.0))), jnp.log(1.0 + x))


def _softplus(x):
    return jnp.maximum(x, 0.0) + _log1p(jnp.exp(-jnp.abs(x)))


def _one_minus_square(a, log_a):
    x = 2.0 * log_a
    series = -x * (1.0 + x * (0.5 + x * (1.0 / 6.0)))
    return jnp.where(x > -0.02, series, 1.0 - a * a)


_GELU_K = math.sqrt(2.0 / math.pi)


def _gelu_and_grad(x):
    inner = _GELU_K * (x + 0.044715 * x * x * x)
    t = jnp.tanh(inner)
    val = 0.5 * x * (1.0 + t)
    grad = 0.5 * (1.0 + t) + 0.5 * x * (1.0 - t * t) * _GELU_K * (1.0 + 3.0 * 0.044715 * x * x)
    return val, grad


def _row_ids(i, rows, shape):
    return i * rows + lax.broadcasted_iota(jnp.int32, shape, 0)


def _rope_tables(rows):
    inv_freq = ROPE_BASE ** (-jnp.arange(0, DH, 2, dtype=F32) / DH)
    block_pos = jnp.arange(rows // FRONT, dtype=jnp.int32) * FRONT - PAD_ROWS
    coarse = block_pos.astype(F32)[:, None] * inv_freq[None, :]
    fine = jnp.arange(FRONT, dtype=F32)[:, None] * inv_freq[None, :]
    ca, sa = jnp.cos(coarse)[:, None, :], jnp.sin(coarse)[:, None, :]
    cb, sb = jnp.cos(fine)[None, :, :], jnp.sin(fine)[None, :, :]
    cos = (ca * cb - sa * sb).reshape(rows, DH // 2)
    sin = (sa * cb + ca * sb).reshape(rows, DH // 2)
    return jnp.concatenate([cos, cos], axis=1), jnp.concatenate([-sin, sin], axis=1)


def _decay_tables():
    log_g = jnp.log(1.0 - 2.0 ** (-5.0 - jnp.arange(HEADS, dtype=F32)))
    idx = jnp.arange(CHUNK, dtype=F32)
    diff = idx[:, None] - idx[None, :]
    intra = jnp.where(diff[None] >= 0, jnp.exp(jnp.maximum(diff, 0.0)[None] * log_g[:, None, None]), 0.0)
    q_decay = jnp.exp((idx + 1.0)[:, None] * log_g[None, :])
    k_decay = jnp.exp((CHUNK - 1.0 - idx)[:, None] * log_g[None, :])
    chunk_decay = jnp.exp(CHUNK * log_g)
    wide = lambda a: jnp.repeat(a, DH, axis=-1)
    return intra, jnp.swapaxes(intra, 1, 2), wide(q_decay), wide(k_decay), wide(chunk_decay[None, :])


def _norm1(head, x2d, norm_w, riders=()):
    rows = FRONT + x2d.shape[0]
    tm = TM_IN_PROJ if rows % TM_IN_PROJ == 0 else TM_MIX_FWD
    nt = rows // tm

    def body(head_hbm, x_hbm, nw_ref, u_ref, rstd_ref, h0_sc, h0_sem):
        slot = _frame_rows(head_hbm, x_hbm, h0_sc, h0_sem, pl.program_id(0), tm, nt)
        hv = h0_sc[slot]
        rs = lax.rsqrt(jnp.mean(hv * hv, axis=-1, keepdims=True) + EPS)
        u_ref[...] = ((hv * rs) * nw_ref[...]).astype(BF16)
        rstd_ref[...] = rs

    return _hosted_call(
        body,
        name="norm1",
        grid=(nt,),
        in_specs=[_ANY, _ANY, pl.BlockSpec((1, D), lambda i: (0, 0))],
        out_specs=[pl.BlockSpec((tm, D), lambda i: (i, 0)), pl.BlockSpec((tm, 1), lambda i: (i, 0))],
        out_shape=[jax.ShapeDtypeStruct((rows, D), BF16), jax.ShapeDtypeStruct((rows, 1), F32)],
        scratch_shapes=[pltpu.VMEM((2, tm, D), F32), pltpu.SemaphoreType.DMA((3,))],
        args=(head, x2d, norm_w),
        riders=riders,
    )


def _heavy_tile(rows):
    return TM_HEAVY if rows % TM_HEAVY == 0 else TM


def _lru_tile(rows):
    return TM_LRU if rows % TM_LRU == 0 else TM


def _swap_3_4(group):
    return jnp.where(group == 3, 4, jnp.where(group == 4, 3, group))


LRU_IN_COL = 3
GATES_COL = 1


def _in_proj(u, w_shard, route, cos_t, sin_t, riders=()):
    rows = u.shape[0]
    tm = TM_IN_PROJ if rows % TM_IN_PROJ == 0 else _heavy_tile(rows)
    nt = rows // tm
    kind, shard = BIG_PIECES["w_in"]

    def body(route_ref, u_hbm, wsh_ref, cos_ref, sin_ref, proj_ref, wfull_ref,
             u_sc, w_sc, u_sem, w_sem, local_sem, ici_send, ici_recv, fwd_send, fwd_recv):
        g, i = pl.program_id(0), pl.program_id(1)
        s, c = route_ref[0], route_ref[1]
        gid = route_ref[2 + g]
        sibling = (s // 2, s % 2, 1 - c)
        local = pltpu.make_async_copy(wsh_ref, _full_region(wfull_ref, kind, shard, s, None), local_sem)
        u_copies = [pltpu.make_async_copy(u_hbm.at[pl.ds(t * tm, tm)], u_sc.at[t], u_sem.at[t]) for t in range(nt)]
        sends, arrivals = [], []
        for k in (1, 2, 3):
            s2, dev = _other_chip(s, c, k)
            sends.append(_remote(_shard_region(wsh_ref, shard, c), _full_region(wfull_ref, kind, shard, s, c),
                                 ici_send.at[k - 1], ici_recv.at[k - 1], dev))
            mine = _full_region(wfull_ref, kind, shard, s2, c)
            theirs = _full_region(wfull_ref, kind, shard, s2, 1 - c)
            arrivals.append((_remote(mine, mine, ici_send.at[k - 1], ici_recv.at[k - 1], dev),
                             _remote(mine, mine, fwd_send.at[k - 1], fwd_recv.at[k - 1], sibling),
                             _remote(theirs, theirs, fwd_send.at[k - 1], fwd_recv.at[k - 1], sibling)))

        slot = g % 2
        last_tile = i == nt - 1

        def block_copy(src, col, to_slot):
            return pltpu.make_async_copy(src.at[:, pl.ds(pl.multiple_of(col * D, D), D)], w_sc.at[to_slot],
                                         w_sem.at[to_slot])

        @pl.when(jnp.logical_and(g == 0, i == 0))
        def _():
            for cp in sends:
                cp.start()
            local.start()
            for cp in u_copies:
                cp.start()
            cp = block_copy(wsh_ref, 0, 0)
            cp.start()
            cp.wait()

        @pl.when(jnp.logical_and(last_tile, g == 0))
        def _():
            block_copy(wsh_ref, 1, 1).start()

        for k, (arrived, forward, forwarded) in zip((1, 2, 3), arrivals):
            @pl.when(jnp.logical_and(last_tile, g == 2 * k - 1))
            def _():
                arrived.wait_recv()
                forward.start()
                forwarded.wait_recv()
                block_copy(wfull_ref, route_ref[2 + 2 * k], 0).start()

            @pl.when(jnp.logical_and(last_tile, g == 2 * k))
            def _():
                block_copy(wfull_ref, route_ref[3 + 2 * k], 1).start()

        @pl.when(jnp.logical_and(i == 0, g > 0))
        def _():
            block_copy(wfull_ref, 0, slot).wait()

        for t in range(nt):
            @pl.when(jnp.logical_and(g == 0, i == t))
            def _():
                u_copies[t].wait()

        acc = _dot(u_sc[i], w_sc[slot])

        @pl.when(gid < 2)
        def _():
            scale = jnp.where(gid == 1, QK_SCALE, 1.0).astype(F32)
            for h in range(HEADS):
                sl = slice(h * DH, (h + 1) * DH)
                blk = acc[:, sl]
                out = (blk * cos_ref[...] + pltpu.roll(blk, DH // 2, axis=1) * sin_ref[...]) * scale
                proj_ref[:, sl] = out.astype(BF16)

        @pl.when(gid >= 2)
        def _():
            proj_ref[...] = acc.astype(BF16)

        @pl.when(jnp.logical_and(g == 7, i == nt - 1))
        def _():
            local.wait()
            for cp in sends:
                cp.wait_send()
            for _, forward, _ in arrivals:
                forward.wait_send()

    return _hosted_call(
        body,
        name="in_proj",
        grid=(8, nt),
        in_specs=[
            _ANY, _ANY,
            pl.BlockSpec((tm, DH), lambda g, i, rt: (i, 0)),
            pl.BlockSpec((tm, DH), lambda g, i, rt: (i, 0)),
        ],
        out_specs=[pl.BlockSpec((tm, D), lambda g, i, rt: (i, _swap_3_4(rt[2 + g]))), _ANY],
        out_shape=[jax.ShapeDtypeStruct((rows, IN_COLS), BF16), jax.ShapeDtypeStruct((D, IN_COLS), BF16)],
        scratch_shapes=[
            pltpu.VMEM((nt, tm, D), BF16), pltpu.VMEM((2, D, D), BF16),
            pltpu.SemaphoreType.DMA((nt,)), pltpu.SemaphoreType.DMA((2,)), pltpu.SemaphoreType.DMA,
            pltpu.SemaphoreType.DMA((3,)), pltpu.SemaphoreType.DMA((3,)),
            pltpu.SemaphoreType.DMA((3,)), pltpu.SemaphoreType.DMA((3,)),
        ],
        args=(u, w_shard, cos_t, sin_t),
        riders=riders,
        prefetch=route,
        riders_after_body=True,
    )


def _retention_fwd(proj_bf, intra, q_dec, k_dec, c_dec, riders=()):
    rows = proj_bf.shape[0]
    nc = rows // CHUNK

    def body(q_ref, k_ref, v_ref, m_ref, qd_ref, kd_ref, cd_ref, o_ref, sprev_ref, s_sc):
        @pl.when(pl.program_id(0) == 0)
        def _():
            s_sc[...] = jnp.zeros_like(s_sc)

        for h in range(HEADS):
            sl = slice(h * DH, (h + 1) * DH)
            q, k, v = q_ref[:, sl], k_ref[:, sl], v_ref[:, sl]
            state = s_sc[h]
            state_b = state.astype(BF16)
            sprev_ref[0, h] = state_b
            s = _dot_nt(q, k) * m_ref[h]
            inner = _dot(s.astype(BF16), v)
            cross = _dot(q, state_b) * qd_ref[:, sl]
            o_ref[:, sl] = (inner + cross).astype(BF16)
            k_scaled = (k.astype(F32) * kd_ref[:, sl]).astype(BF16)
            s_sc[h] = state * cd_ref[:, sl] + _dot_tn(k_scaled, v)

    chunk_spec = lambda col: pl.BlockSpec((CHUNK, D), lambda c: (c, col))
    const2 = lambda shape: pl.BlockSpec(shape, lambda c: (0, 0))
    return _hosted_call(
        body,
        name="retention_fwd",
        grid=(nc,),
        in_specs=[
            chunk_spec(0), chunk_spec(1), chunk_spec(2),
            pl.BlockSpec((HEADS, CHUNK, CHUNK), lambda c: (0, 0, 0)),
            const2((CHUNK, D)), const2((CHUNK, D)), const2((1, D)),
        ],
        out_specs=[
            pl.BlockSpec((CHUNK, D), lambda c: (c, 0)),
            pl.BlockSpec((1, HEADS, DH, DH), lambda c: (c, 0, 0, 0)),
        ],
        out_shape=[
            jax.ShapeDtypeStruct((rows, D), BF16),
            jax.ShapeDtypeStruct((nc, HEADS, DH, DH), BF16),
        ],
        scratch_shapes=[pltpu.VMEM((HEADS, DH, DH), F32)],
        args=(proj_bf, proj_bf, proj_bf, intra, q_dec, k_dec, c_dec),
        riders=riders,
    )


def _shift_down(x, first8_prev, d):
    rolled = pltpu.roll(x, d, axis=0)
    head = pltpu.roll(jnp.concatenate([first8_prev, x[0:8]], axis=0), d, axis=0)[8:16]
    return rolled, head


def _conv_and_gates(x, prev8, cw_ref, cb_ref, wa_ref, wx_ref, ba_ref, bx_ref, lam_ref, c_sc):
    cw = cw_ref[...]
    conv = cb_ref[...] + cw[3:4] * x
    head = cb_ref[...] + cw[3:4] * x[0:8]
    for d in (1, 2, 3):
        rolled, hd = _shift_down(x, prev8, d)
        conv = conv + cw[3 - d:4 - d] * rolled
        head = head + cw[3 - d:4 - d] * hd
    c_sc[...] = conv
    c_sc[0:8, :] = head
    c = c_sc[...]
    zr, zi = [], []
    for g in range(LRU_BLOCKS):
        sl = slice(g * LRU_BLOCK, (g + 1) * LRU_BLOCK)
        cg = c[:, sl].astype(BF16)
        zr.append(_dot(cg, wa_ref[g]))
        zi.append(_dot(cg, wx_ref[g]))
    r = _sigmoid(jnp.concatenate(zr, axis=1) + ba_ref[...])
    gate_i = _sigmoid(jnp.concatenate(zi, axis=1) + bx_ref[...])
    sp = _softplus(-lam_ref[...])
    log_a = (-LRU_C) * r * sp
    a = jnp.exp(log_a)
    mult = jnp.sqrt(_one_minus_square(a, log_a))
    return c, r, gate_i, a, mult, log_a


def _lru_fwd(proj, conv_w, conv_b, wa, wx, ba, bx, lam, riders=()):
    rows = proj.shape[0]
    TM = _lru_tile(rows)
    nt = rows // TM

    def body(x_ref, cw_ref, cb_ref, wa_ref, wx_ref, ba_ref, bx_ref, lam_ref,
             h_ref, c_ref, r_ref, i_ref, la_ref, mult_ref, prev_sc, carry_sc, c_sc, a_sc, u_sc, h_sc):
        i = pl.program_id(0)

        @pl.when(i == 0)
        def _():
            prev_sc[...] = jnp.zeros_like(prev_sc)
            carry_sc[...] = jnp.zeros_like(carry_sc)

        x = x_ref[...].astype(F32)
        c, r, gate_i, a, mult, log_a = _conv_and_gates(x, prev_sc[...], cw_ref, cb_ref, wa_ref, wx_ref, ba_ref, bx_ref,
                                                       lam_ref, c_sc)
        for ref, val in ((c_ref, c), (r_ref, r), (i_ref, gate_i), (la_ref, log_a), (mult_ref, mult)):
            ref[...] = val.astype(BF16)
        prev_sc[...] = x[TM - 8:TM]
        valid = _row_ids(i, TM, (TM, D)) >= PAD_ROWS
        a_sc[...] = a
        u_sc[...] = jnp.where(valid, mult * gate_i * c, 0.0)
        row8 = lax.broadcasted_iota(jnp.int32, (8, D), 0)

        def group(gi, hprev):
            r0 = pl.multiple_of(gi * 8, 8)
            aa = a_sc[pl.ds(r0, 8), :]
            uu = u_sc[pl.ds(r0, 8), :]
            for d in (1, 2, 4):
                a_sh = jnp.where(row8 >= d, pltpu.roll(aa, d, axis=0), 1.0)
                u_sh = jnp.where(row8 >= d, pltpu.roll(uu, d, axis=0), 0.0)
                uu = uu + aa * u_sh
                aa = aa * a_sh
            hb = aa * hprev + uu
            h_sc[pl.ds(r0, 8), :] = hb
            return hb[7:8, :]

        hlast = lax.fori_loop(0, TM // 8, group, carry_sc[0:1, :])
        carry_sc[0:1, :] = hlast
        h_ref[...] = h_sc[...].astype(BF16)

    vec = pl.BlockSpec((1, D), lambda i: (0, 0))
    wspec = pl.BlockSpec((LRU_BLOCKS, LRU_BLOCK, LRU_BLOCK), lambda i: (0, 0, 0))
    return _hosted_call(
        body,
        name="lru_fwd",
        grid=(nt,),
        in_specs=[
            pl.BlockSpec((TM, D), lambda i: (i, LRU_IN_COL)),
            pl.BlockSpec((4, D), lambda i: (0, 0)),
            vec, wspec, wspec, vec, vec, vec,
        ],
        out_specs=[pl.BlockSpec((TM, D), lambda i: (i, 0)) for _ in range(6)],
        out_shape=[jax.ShapeDtypeStruct((rows, D), BF16) for _ in range(6)],
        scratch_shapes=[
            pltpu.VMEM((8, D), F32), pltpu.VMEM((8, D), F32),
            pltpu.VMEM((TM, D), F32), pltpu.VMEM((TM, D), F32), pltpu.VMEM((TM, D), F32), pltpu.VMEM((TM, D), F32),
        ],
        args=(proj, conv_w, conv_b, wa, wx, ba, bx, lam),
        riders=riders,
    )


def _group_norm(o):
    outs, rstds = [], []
    for h in range(HEADS):
        oh = o[:, h * DH:(h + 1) * DH]
        rs = lax.rsqrt(jnp.mean(oh * oh, axis=-1, keepdims=True) + EPS)
        outs.append(oh * rs)
        rstds.append(rs)
    return jnp.concatenate(outs, axis=1), rstds


def _frame_rows(head_hbm, x_hbm, buf, sems, i, tm, nt):
    slot = i % 2

    @pl.when(i == 0)
    def _():
        first = [pltpu.make_async_copy(head_hbm, buf.at[0, pl.ds(0, FRONT)], sems.at[2]),
                 pltpu.make_async_copy(x_hbm.at[pl.ds(0, tm - FRONT)], buf.at[0, pl.ds(FRONT, tm - FRONT)], sems.at[0])]
        for cp in first:
            cp.start()
        for cp in first:
            cp.wait()

    @pl.when(i + 1 < nt)
    def _():
        start = pl.multiple_of((i + 1) * tm - FRONT, LANES)
        pltpu.make_async_copy(x_hbm.at[pl.ds(start, tm)], buf.at[1 - slot], sems.at[1 - slot]).start()

    @pl.when(i > 0)
    def _():
        pltpu.make_async_copy(x_hbm.at[pl.ds(0, tm)], buf.at[slot], sems.at[slot]).wait()

    return slot


def _mix_fwd(head, x2d, o, proj, h_lru, w_br, w_bl, w_o, ffn_norm_w, riders=()):
    rows = o.shape[0]
    tm = TM_MIX_FWD
    assert rows % tm == 0 and tm > FRONT
    nt = rows // tm

    def body(head_hbm, x_hbm, o_ref, gates_ref, hl_ref, wbr_ref, wbl_ref, wo_ref, nw_ref,
             yret_ref, ylru_ref, h1_ref, u2_ref, rstd_ref, h0_sc, h0_sem):
        i = pl.program_id(0)
        slot = _frame_rows(head_hbm, x_hbm, h0_sc, h0_sem, i, tm, nt)
        gate = lambda j: gates_ref[:, j * D:(j + 1) * D].astype(F32)
        on, _ = _group_norm(o_ref[...].astype(F32))
        gret = gate(0)
        a_ret = (gret * _sigmoid(gret) * on).astype(BF16)
        y_ret = _dot(a_ret, wbr_ref[...])
        gl, _ = _gelu_and_grad(gate(1))
        a_lru = (gl * hl_ref[...].astype(F32)).astype(BF16)
        y_lru = _dot(a_lru, wbl_ref[...])
        mixed = (_sigmoid(gate(2)) * y_ret + _sigmoid(gate(3)) * y_lru).astype(BF16)
        delta = _dot(mixed, wo_ref[...])
        yret_ref[...] = y_ret.astype(BF16)
        ylru_ref[...] = y_lru.astype(BF16)
        h1 = h0_sc[slot] + delta
        rs = lax.rsqrt(jnp.mean(h1 * h1, axis=-1, keepdims=True) + EPS)
        h1_ref[...] = h1
        u2_ref[...] = ((h1 * rs) * nw_ref[...]).astype(BF16)
        rstd_ref[...] = rs

    tile = lambda col: pl.BlockSpec((tm, D), lambda i: (i, col))
    wspec = pl.BlockSpec((D, D), lambda i: (0, 0))
    return _hosted_call(
        body,
        name="mix_fwd",
        grid=(nt,),
        in_specs=[
            _ANY, _ANY,
            tile(0), pl.BlockSpec((tm, 4 * D), lambda i: (i, GATES_COL)), tile(0),
            wspec, wspec, wspec,
            pl.BlockSpec((1, D), lambda i: (0, 0)),
        ],
        out_specs=[tile(0), tile(0), tile(0), tile(0), pl.BlockSpec((tm, 1), lambda i: (i, 0))],
        out_shape=[
            jax.ShapeDtypeStruct((rows, D), BF16), jax.ShapeDtypeStruct((rows, D), BF16),
            jax.ShapeDtypeStruct((rows, D), F32), jax.ShapeDtypeStruct((rows, D), BF16),
            jax.ShapeDtypeStruct((rows, 1), F32),
        ],
        scratch_shapes=[pltpu.VMEM((2, tm, D), F32), pltpu.SemaphoreType.DMA((3,))],
        args=(head, x2d, o, proj, h_lru, w_br, w_bl, w_o, ffn_norm_w),
        riders=riders,
    )


def _ffn_fwd_gate(u2, w_ffn_in, riders=()):
    rows = u2.shape[0]
    tm = _heavy_tile(rows)
    hid = lambda: pl.BlockSpec((tm, FFN), lambda i: (i, 0))

    def body(u2_ref, w_hbm, silu_ref, dsilu_ref, act_ref, w_sc, w_sem):
        @pl.when(pl.program_id(0) == 0)
        def _():
            cp = pltpu.make_async_copy(w_hbm, w_sc, w_sem)
            cp.start()
            cp.wait()

        u2 = u2_ref[...]
        for a, b in _FFN_SUBS:
            g = _dot(u2, w_sc[:, a:b])
            up = _dot(u2, w_sc[:, FFN + a:FFN + b])
            sg = _sigmoid(g)
            silu = g * sg
            silu_ref[:, a:b] = silu.astype(BF16)
            dsilu_ref[:, a:b] = (up * (sg * (1.0 + g * (1.0 - sg)))).astype(BF16)
            act_ref[:, a:b] = (silu * up).astype(BF16)

    return _hosted_call(
        body,
        name="ffn_fwd_gate",
        grid=(rows // tm,),
        in_specs=[pl.BlockSpec((tm, D), lambda i: (i, 0)), _ANY],
        out_specs=[hid(), hid(), hid()],
        out_shape=[jax.ShapeDtypeStruct((rows, FFN), BF16)] * 3,
        scratch_shapes=[pltpu.VMEM((D, 2 * FFN), BF16), pltpu.SemaphoreType.DMA],
        args=(u2, w_ffn_in),
        riders=riders,
    )


def _ffn_out_loss(act, h1, w_ffn_out, target, final_norm_w):
    rows = act.shape[0]
    tm = _heavy_tile(rows)
    nt = rows // tm

    def body(act_ref, h1_ref, wo_ref, fnw_ref, tgt_hbm, dh2_ref, stats_ref, tgt_sc, tgt_sem):
        i = pl.program_id(0)
        slot = i % 2

        @pl.when(i == 0)
        def _():
            stats_ref[...] = jnp.zeros_like(stats_ref)
            tgt_sc[0, 0:FRONT, :] = jnp.zeros((FRONT, D), F32)
            cp = pltpu.make_async_copy(tgt_hbm.at[pl.ds(0, tm - FRONT)], tgt_sc.at[0, pl.ds(FRONT, tm - FRONT)], tgt_sem.at[0])
            cp.start()
            cp.wait()

        @pl.when(i + 1 < nt)
        def _():
            start = pl.multiple_of((i + 1) * tm - FRONT, FRONT)
            pltpu.make_async_copy(tgt_hbm.at[pl.ds(start, tm)], tgt_sc.at[1 - slot], tgt_sem.at[1 - slot]).start()

        @pl.when(i > 0)
        def _():
            pltpu.make_async_copy(tgt_hbm.at[pl.ds(0, tm)], tgt_sc.at[slot], tgt_sem.at[slot]).wait()

        h2 = h1_ref[...] + _dot(act_ref[...], wo_ref[...])
        rs = lax.rsqrt(jnp.mean(h2 * h2, axis=-1, keepdims=True) + EPS)
        n = h2 * rs
        fnw = fnw_ref[...]
        valid = _row_ids(i, tm, (tm, D)) >= FRONT
        diff = jnp.where(valid, n * fnw - tgt_sc[slot], 0.0)
        dy = diff * (1.0 / D)
        stats_ref[0:1, :] += (0.5 / D) * jnp.sum(diff * diff, axis=0, keepdims=True)
        stats_ref[1:2, :] += jnp.sum(dy * n, axis=0, keepdims=True)
        dn = dy * fnw
        dh2_ref[...] = rs * (dn - n * jnp.mean(dn * n, axis=-1, keepdims=True))

    return pl.pallas_call(
        body,
        name="ffn_out_loss",
        grid=(nt,),
        in_specs=[
            pl.BlockSpec((tm, FFN), lambda i: (i, 0)),
            pl.BlockSpec((tm, D), lambda i: (i, 0)),
            pl.BlockSpec((FFN, D), lambda i: (0, 0)),
            pl.BlockSpec((1, D), lambda i: (0, 0)),
            _ANY,
        ],
        out_specs=[pl.BlockSpec((tm, D), lambda i: (i, 0)), pl.BlockSpec((8, D), lambda i: (0, 0))],
        out_shape=[jax.ShapeDtypeStruct((rows, D), F32), jax.ShapeDtypeStruct((8, D), F32)],
        scratch_shapes=[pltpu.VMEM((2, tm, D), F32), pltpu.SemaphoreType.DMA((2,))],
        compiler_params=_params(("arbitrary",)),
    )(act, h1, w_ffn_out, final_norm_w, target)


def _ffn_bwd(dh2, silu, dsilu, h1, rstd2, w_ffn_in, w_ffn_out, ffn_norm_w):
    rows = dh2.shape[0]
    tm = _heavy_tile(rows)
    blk = lambda: pl.BlockSpec((tm, FFN), lambda i: (i, 0))

    def gate_body(dh2_ref, silu_ref, dsilu_ref, wo_hbm, dg_ref, dup_ref, dh2b_ref, wo_sc, wo_sem):
        @pl.when(pl.program_id(0) == 0)
        def _():
            cp = pltpu.make_async_copy(wo_hbm, wo_sc, wo_sem)
            cp.start()
            cp.wait()

        dh2b = dh2_ref[...].astype(BF16)
        dh2b_ref[...] = dh2b
        for a, b in _FFN_SUBS:
            dact = _dot_nt(dh2b, wo_sc[a:b, :])
            dup_ref[:, a:b] = (dact * silu_ref[:, a:b].astype(F32)).astype(BF16)
            dg_ref[:, a:b] = (dact * dsilu_ref[:, a:b].astype(F32)).astype(BF16)

    dg, dup, dh2b = pl.pallas_call(
        gate_body,
        name="ffn_bwd_gate",
        grid=(rows // tm,),
        in_specs=[pl.BlockSpec((tm, D), lambda i: (i, 0)), blk(), blk(), _ANY],
        out_specs=[blk(), blk(), pl.BlockSpec((tm, D), lambda i: (i, 0))],
        out_shape=[jax.ShapeDtypeStruct((rows, FFN), BF16)] * 2 + [jax.ShapeDtypeStruct((rows, D), BF16)],
        scratch_shapes=[pltpu.VMEM((FFN, D), BF16), pltpu.SemaphoreType.DMA],
        compiler_params=_params(("arbitrary",)),
    )(dh2, silu, dsilu, w_ffn_out)

    def body(dg_ref, dup_ref, dh2_ref, h1_ref, rstd_ref, w_hbm, nw_ref, dh1_ref, stats_ref, w_sc, w_sem):
        @pl.when(pl.program_id(0) == 0)
        def _():
            stats_ref[...] = jnp.zeros_like(stats_ref)
            cp = pltpu.make_async_copy(w_hbm, w_sc, w_sem)
            cp.start()
            cp.wait()

        du = _dot_nt(dg_ref[...], w_sc[:, 0:FFN]) + _dot_nt(dup_ref[...], w_sc[:, FFN:2 * FFN])
        rs = rstd_ref[...]
        n = h1_ref[...] * rs
        stats_ref[0:1, :] += jnp.sum(du * n, axis=0, keepdims=True)
        dn = du * nw_ref[...]
        dh1_ref[...] = dh2_ref[...] + rs * (dn - n * jnp.mean(dn * n, axis=-1, keepdims=True))

    row = lambda width: pl.BlockSpec((tm, width), lambda i: (i, 0))
    dh1, stats = pl.pallas_call(
        body,
        name="ffn_bwd_in",
        grid=(rows // tm,),
        in_specs=[row(FFN), row(FFN), row(D), row(D), row(1), _ANY, pl.BlockSpec((1, D), lambda i: (0, 0))],
        out_specs=[row(D), pl.BlockSpec((8, D), lambda i: (0, 0))],
        out_shape=[jax.ShapeDtypeStruct((rows, D), F32), jax.ShapeDtypeStruct((8, D), F32)],
        scratch_shapes=[pltpu.VMEM((D, 2 * FFN), BF16), pltpu.SemaphoreType.DMA],
        compiler_params=_params(("arbitrary",)),
    )(dg, dup, dh2, h1, rstd2, w_ffn_in, ffn_norm_w)
    return dg, dup, dh2b, dh1, stats


def _mix_bwd(dh1, o, proj, h_lru, y_ret, y_lru, w_br, w_bl, w_o, riders=()):
    rows = dh1.shape[0]
    tm = TM_MIX_BWD
    nt = rows // tm

    def body(dh1_ref, o_ref, gates_ref, hl_ref, yret_ref, ylru_ref, wbr_ref, wbl_ref, wo_ref,
             dproj_ref, do_ref, dhl_ref, mixed_ref, aret_ref, alru_ref, dyret_ref, dylru_ref):
        gate = lambda j: gates_ref[:, j * D:(j + 1) * D].astype(F32)
        dmixed = _dot_nt(dh1_ref[...].astype(BF16), wo_ref[...])
        y_ret, y_lru = yret_ref[...].astype(F32), ylru_ref[...].astype(F32)
        sa, sb = _sigmoid(gate(2)), _sigmoid(gate(3))
        mixed_ref[...] = (sa * y_ret + sb * y_lru).astype(BF16)
        dga = dmixed * y_ret * sa * (1.0 - sa)
        dgb = dmixed * y_lru * sb * (1.0 - sb)
        dy_ret = (dmixed * sa).astype(BF16)
        dy_lru = (dmixed * sb).astype(BF16)
        dyret_ref[...] = dy_ret
        dylru_ref[...] = dy_lru
        da_ret = _dot_nt(dy_ret, wbr_ref[...])
        da_lru = _dot_nt(dy_lru, wbl_ref[...])

        gret = gate(0)
        sg = _sigmoid(gret)
        silu = gret * sg
        on, rstds = _group_norm(o_ref[...].astype(F32))
        aret_ref[...] = (silu * on).astype(BF16)
        dgret = da_ret * on * (sg * (1.0 + gret * (1.0 - sg)))
        don = da_ret * silu
        for h in range(HEADS):
            sl = slice(h * DH, (h + 1) * DH)
            onh, donh = on[:, sl], don[:, sl]
            do_ref[:, sl] = (rstds[h] * (donh - onh * jnp.mean(donh * onh, axis=-1, keepdims=True))).astype(BF16)

        gl, gl_grad = _gelu_and_grad(gate(1))
        hl = hl_ref[...].astype(F32)
        alru_ref[...] = (gl * hl).astype(BF16)
        dlgate = da_lru * hl * gl_grad
        dhl_ref[...] = (da_lru * gl).astype(BF16)

        for j, val in enumerate((dgret, dlgate, dga, dgb)):
            dproj_ref[:, j * D:(j + 1) * D] = val.astype(BF16)

    tile = lambda col: pl.BlockSpec((tm, D), lambda i: (i, col))
    gates = pl.BlockSpec((tm, 4 * D), lambda i: (i, GATES_COL))
    wspec = pl.BlockSpec((D, D), lambda i: (0, 0))
    bf = lambda: jax.ShapeDtypeStruct((rows, D), BF16)
    return _hosted_call(
        body,
        name="mix_bwd",
        grid=(nt,),
        in_specs=[tile(0), tile(0), gates, tile(0), tile(0), tile(0), wspec, wspec, wspec],
        out_specs=[pl.BlockSpec((tm, 4 * D), lambda i: (i, GATES_COL))] + [tile(0)] * 7,
        out_shape=[jax.ShapeDtypeStruct((rows, IN_COLS), BF16)] + [bf() for _ in range(7)],
        scratch_shapes=[],
        args=(dh1, o, proj, h_lru, y_ret, y_lru, w_br, w_bl, w_o),
        riders=riders,
    )


def _retention_bwd(dproj, proj_bf, do, sprev, intra, intra_t, q_dec, k_dec, c_dec, cos_t, sin_t, riders=()):
    rows = proj_bf.shape[0]
    nc = rows // CHUNK

    def body(dproj_in_ref, q_ref, k_ref, v_ref, do_ref, sprev_ref, m_ref, mt_ref, qd_ref, kd_ref, cd_ref, cos_ref, sin_ref,
             dproj_ref, ds_sc):
        @pl.when(pl.program_id(0) == 0)
        def _():
            ds_sc[...] = jnp.zeros_like(ds_sc)

        cos, sin = cos_ref[...], sin_ref[...]

        def unrotate(dy):
            return dy * cos - pltpu.roll(dy, DH // 2, axis=1) * sin

        for h in range(HEADS):
            sl = slice(h * DH, (h + 1) * DH)
            q, k, v = q_ref[:, sl], k_ref[:, sl], v_ref[:, sl]
            do = do_ref[:, sl]
            dob = do.astype(BF16)
            doq = (do * qd_ref[:, sl]).astype(BF16)
            state_prev = sprev_ref[0, h]
            dstate = ds_sc[h]
            dstate_b = dstate.astype(BF16)
            s_t = (_dot_nt(k, q) * mt_ref[h]).astype(BF16)
            ds_t = (_dot_nt(v, dob) * mt_ref[h]).astype(BF16)
            ds = (_dot_nt(dob, v) * m_ref[h]).astype(BF16)
            kd = kd_ref[:, sl]
            dq = _dot(ds, k) + _dot_nt(doq, state_prev)
            dk = _dot(ds_t, q) + _dot_nt(v, dstate_b) * kd
            k_scaled = (k.astype(F32) * kd).astype(BF16)
            dv = _dot(s_t, dob) + _dot(k_scaled, dstate_b)
            ds_sc[h] = dstate * cd_ref[:, sl] + _dot_tn(q, doq)
            for part, val in enumerate((unrotate(dq), unrotate(dk) * QK_SCALE, dv)):
                dproj_ref[:, part * D + h * DH:part * D + (h + 1) * DH] = val.astype(BF16)

    rev = lambda c: nc - 1 - c
    chunk_spec = lambda col: pl.BlockSpec((CHUNK, D), lambda c: (rev(c), col))
    const2 = lambda shape: pl.BlockSpec(shape, lambda c: (0, 0))
    const3 = pl.BlockSpec((HEADS, CHUNK, CHUNK), lambda c: (0, 0, 0))
    return _hosted_call(
        body,
        name="retention_bwd",
        grid=(nc,),
        in_specs=[
            pl.BlockSpec(memory_space=pl.ANY),
            chunk_spec(0), chunk_spec(1), chunk_spec(2), chunk_spec(0),
            pl.BlockSpec((1, HEADS, DH, DH), lambda c: (rev(c), 0, 0, 0)),
            const3, const3,
            const2((CHUNK, D)), const2((CHUNK, D)), const2((1, D)),
            pl.BlockSpec((CHUNK, DH), lambda c: (rev(c), 0)),
            pl.BlockSpec((CHUNK, DH), lambda c: (rev(c), 0)),
        ],
        out_specs=[pl.BlockSpec((CHUNK, 3 * D), lambda c: (rev(c), 0))],
        out_shape=[jax.ShapeDtypeStruct(dproj.shape, BF16)],
        aliases={0: 0},
        scratch_shapes=[pltpu.VMEM((HEADS, DH, DH), F32)],
        args=(dproj, proj_bf, proj_bf, proj_bf, do, sprev, intra, intra_t, q_dec, k_dec, c_dec, cos_t, sin_t),
        riders=riders,
    )


def _lru_bwd(dproj, proj, saved, dhl, conv_w, wa, wx, lam, riders=()):
    rows = proj.shape[0]
    TM = _lru_tile(rows)
    nt = rows // TM
    per8 = TM // 8

    def body(dproj_in_ref, x_ref, xprev_ref, h_ref, hprev_ref, c_ref, r_ref, i_ref, la_ref, mult_ref, dhl_ref,
             cw_ref, wa_ref, wx_ref, lam_ref,
             dproj_ref, dwa_ref, dwx_ref, stats_ref, anext_sc, dhnext_sc, dcnext_sc, c_sc, b_sc, dh_sc):
        step = pl.program_id(0)
        i = nt - 1 - step

        @pl.when(step == 0)
        def _():
            anext_sc[...] = jnp.zeros_like(anext_sc)
            dhnext_sc[...] = jnp.zeros_like(dhnext_sc)
            dcnext_sc[...] = jnp.zeros_like(dcnext_sc)
            dwa_ref[...] = jnp.zeros_like(dwa_ref)
            dwx_ref[...] = jnp.zeros_like(dwx_ref)
            stats_ref[...] = jnp.zeros_like(stats_ref)

        first = i == 0
        x = x_ref[...].astype(F32)
        prev8 = jnp.where(first, 0.0, xprev_ref[8:16, :].astype(F32))
        c_b = c_ref[...]
        c, r, gate_i, mult = (ref[...].astype(F32) for ref in (c_ref, r_ref, i_ref, mult_ref))
        a = jnp.exp(la_ref[...].astype(F32))
        sp = _softplus(-lam_ref[...])
        dh_sc[...] = dhl_ref[...].astype(F32)

        b_sc[...] = pltpu.roll(a, TM - 1, axis=0)
        b_sc[TM - 1:TM, :] = anext_sc[0:1, :]
        anext_sc[0:1, :] = a[0:1, :]
        row8 = lax.broadcasted_iota(jnp.int32, (8, D), 0)

        def group(gi, dhnext):
            r0 = pl.multiple_of((per8 - 1 - gi) * 8, 8)
            bb = b_sc[pl.ds(r0, 8), :]
            uu = dh_sc[pl.ds(r0, 8), :]
            for d in (1, 2, 4):
                b_sh = jnp.where(row8 < 8 - d, pltpu.roll(bb, 8 - d, axis=0), 1.0)
                u_sh = jnp.where(row8 < 8 - d, pltpu.roll(uu, 8 - d, axis=0), 0.0)
                uu = uu + bb * u_sh
                bb = bb * b_sh
            dhb = bb * dhnext + uu
            dh_sc[pl.ds(r0, 8), :] = dhb
            return dhb[0:1, :]

        dhfirst = lax.fori_loop(0, per8, group, dhnext_sc[0:1, :])
        dhnext_sc[0:1, :] = dhfirst
        dh = dh_sc[...]

        h = h_ref[...].astype(F32)
        hprev8 = jnp.where(first, 0.0, hprev_ref[8:16, :].astype(F32))
        h_dn, h_head = _shift_down(h, hprev8, 1)
        c_sc[...] = h_dn
        c_sc[0:8, :] = h_head
        h_before = c_sc[...]

        valid = _row_ids(i, TM, (TM, D)) >= PAD_ROWS
        da = dh * h_before
        du = jnp.where(valid, dh, 0.0)
        dmult = du * gate_i * c
        dgate_i = du * mult * c
        dc = du * mult * gate_i
        dla = da * a - dmult * (a * a) / mult
        dla = jnp.where(valid, dla, 0.0)
        dr = dla * ((-LRU_C) * sp)
        dzr = dr * r * (1.0 - r)
        dzi = dgate_i * gate_i * (1.0 - gate_i)
        stats_ref[1:2, :] += jnp.sum(dzr, axis=0, keepdims=True)
        stats_ref[2:3, :] += jnp.sum(dzi, axis=0, keepdims=True)
        stats_ref[3:4, :] += jnp.sum(dla * ((-LRU_C) * r), axis=0, keepdims=True)
        dc_gate = []
        for g in range(LRU_BLOCKS):
            sl = slice(g * LRU_BLOCK, (g + 1) * LRU_BLOCK)
            cg = c_b[:, sl]
            dzr_g = dzr[:, sl].astype(BF16)
            dzi_g = dzi[:, sl].astype(BF16)
            dc_gate.append(_dot_nt(dzr_g, wa_ref[g]) + _dot_nt(dzi_g, wx_ref[g]))
            dwa_ref[g] += _dot_tn(cg, dzr_g)
            dwx_ref[g] += _dot_tn(cg, dzi_g)
        dc = dc + jnp.concatenate(dc_gate, axis=1)

        cw = cw_ref[...]
        stats_ref[0:1, :] += jnp.sum(dc, axis=0, keepdims=True)
        stats_ref[7:8, :] += jnp.sum(dc * x, axis=0, keepdims=True)
        dx = cw[3:4] * dc
        tail_src = jnp.concatenate([dc[TM - 8:TM], dcnext_sc[...]], axis=0)
        dx_tail = cw[3:4] * dc[TM - 8:TM]
        for d in (1, 2, 3):
            dx = dx + cw[3 - d:4 - d] * pltpu.roll(dc, TM - d, axis=0)
            dx_tail = dx_tail + cw[3 - d:4 - d] * pltpu.roll(tail_src, 16 - d, axis=0)[0:8]
            rolled, hd = _shift_down(x, prev8, d)
            b_sc[...] = rolled
            b_sc[0:8, :] = hd
            stats_ref[7 - d:8 - d, :] += jnp.sum(dc * b_sc[...], axis=0, keepdims=True)
        dcnext_sc[...] = dc[0:8]
        dproj_ref[...] = dx.astype(BF16)
        dproj_ref[TM - 8:TM, :] = dx_tail.astype(BF16)

    rev = lambda s: nt - 1 - s
    vec = pl.BlockSpec((1, D), lambda s: (0, 0))
    wspec = pl.BlockSpec((LRU_BLOCKS, LRU_BLOCK, LRU_BLOCK), lambda s: (0, 0, 0))
    prev16 = lambda col: pl.BlockSpec((BF16_ROWS, D), lambda s: (jnp.maximum(rev(s) * (TM // BF16_ROWS) - 1, 0), col))
    tile = lambda: pl.BlockSpec((TM, D), lambda s: (rev(s), 0))
    h_lru, c_sv, r_sv, i_sv, la_sv, mult_sv = saved
    return _hosted_call(
        body,
        name="lru_bwd",
        grid=(nt,),
        in_specs=[
            pl.BlockSpec(memory_space=pl.ANY),
            pl.BlockSpec((TM, D), lambda s: (rev(s), LRU_IN_COL)), prev16(LRU_IN_COL),
            tile(), prev16(0),
            tile(), tile(), tile(), tile(), tile(), tile(),
            pl.BlockSpec((4, D), lambda s: (0, 0)),
            wspec, wspec, vec,
        ],
        out_specs=[
            pl.BlockSpec((TM, D), lambda s: (rev(s), LRU_IN_COL)),
            wspec, wspec,
            pl.BlockSpec((8, D), lambda s: (0, 0)),
        ],
        out_shape=[
            jax.ShapeDtypeStruct(dproj.shape, BF16),
            jax.ShapeDtypeStruct((LRU_BLOCKS, LRU_BLOCK, LRU_BLOCK), F32),
            jax.ShapeDtypeStruct((LRU_BLOCKS, LRU_BLOCK, LRU_BLOCK), F32),
            jax.ShapeDtypeStruct((8, D), F32),
        ],
        aliases={0: 0},
        scratch_shapes=[
            pltpu.VMEM((8, D), F32), pltpu.VMEM((8, D), F32), pltpu.VMEM((8, D), F32),
            pltpu.VMEM((TM, D), F32), pltpu.VMEM((TM, D), F32), pltpu.VMEM((TM, D), F32),
        ],
        args=(dproj, proj, proj, h_lru, h_lru, c_sv, r_sv, i_sv, la_sv, mult_sv, dhl, conv_w, wa, wx, lam),
        riders=riders,
    )


def _in_proj_bwd(dproj, w_in, part, prev=None, riders=()):
    rows = dproj.shape[0]
    tm = _heavy_tile(rows)
    nt = rows // tm
    first = 0 if part == 0 else (nt + 1) // 2
    count = (nt + 1) // 2 if part == 0 else nt - first

    def body(*refs):
        dproj_ref, w_hbm, du_ref, w_sc, w_sem = refs[-5:]

        @pl.when(pl.program_id(0) == 0)
        def _():
            moves = [((0, 3), 0), ((4, 1), 3), ((3, 1), 4), ((5, 3), 5)]
            copies = [pltpu.make_async_copy(w_hbm.at[:, pl.ds(src * D, n * D)], w_sc.at[:, pl.ds(dst * D, n * D)], w_sem.at[q])
                      for q, ((src, n), dst) in enumerate(moves)]
            for cp in copies:
                cp.start()
            for cp in copies:
                cp.wait()

        du_ref[...] = _dot_nt(dproj_ref[...], w_sc[...])

    in_specs = [pl.BlockSpec((tm, IN_COLS), lambda i: (first + i, 0)), _ANY]
    args = (dproj, w_in)
    if prev is not None:
        in_specs = [_ANY] + in_specs
        args = (prev,) + args
    return _hosted_call(
        body,
        name="in_proj_bwd_%d" % part,
        grid=(count,),
        in_specs=in_specs,
        out_specs=[pl.BlockSpec((tm, D), lambda i: (first + i, 0))],
        out_shape=[jax.ShapeDtypeStruct((rows, D), F32)],
        scratch_shapes=[pltpu.VMEM((D, IN_COLS), BF16), pltpu.SemaphoreType.DMA((4,))],
        aliases={0: 0} if prev is not None else None,
        args=args,
        riders=riders,
    )


def _norm1_bwd(du, dh1, head, x2d, rstd1, norm_w, riders=()):
    rows = du.shape[0]

    def body(du_ref, dh1_ref, head_ref, x_ref, rstd_ref, nw_ref, gx_ref, ghead_ref, stats_ref):
        i = pl.program_id(0)

        @pl.when(i == 0)
        def _():
            stats_ref[...] = jnp.zeros_like(stats_ref)

        def finish(h0, out_ref):
            du = du_ref[...]
            rs = rstd_ref[...]
            n = h0 * rs
            stats_ref[0:1, :] += jnp.sum(du * n, axis=0, keepdims=True)
            dn = du * nw_ref[...]
            out_ref[...] = dh1_ref[...] + rs * (dn - n * jnp.mean(dn * n, axis=-1, keepdims=True))

        @pl.when(i == 0)
        def _():
            finish(head_ref[...], ghead_ref)

        @pl.when(i > 0)
        def _():
            finish(x_ref[...], gx_ref)

    tile = pl.BlockSpec((TM, D), lambda i: (i, 0))
    return _hosted_call(
        body,
        name="norm1_bwd",
        grid=(rows // TM,),
        in_specs=[
            tile, tile,
            pl.BlockSpec((FRONT, D), lambda i: (0, 0)),
            pl.BlockSpec((TM, D), lambda i: (jnp.maximum(i - 1, 0), 0)),
            pl.BlockSpec((TM, 1), lambda i: (i, 0)),
            pl.BlockSpec((1, D), lambda i: (0, 0)),
        ],
        out_specs=[
            pl.BlockSpec((TM, D), lambda i: (jnp.maximum(i - 1, 0), 0)),
            pl.BlockSpec((FRONT, D), lambda i: (0, 0)),
            pl.BlockSpec((8, D), lambda i: (0, 0)),
        ],
        out_shape=[
            jax.ShapeDtypeStruct(x2d.shape, F32),
            jax.ShapeDtypeStruct((FRONT, D), F32),
            jax.ShapeDtypeStruct((8, D), F32),
        ],
        scratch_shapes=[],
        args=(du, dh1, head, x2d, rstd1, norm_w),
        riders=riders,
    )


def _matmul_tn(name, x, dy, out_cols, col0=0, prev=None, k_block=None, n_block=None, riders=(), col_map=None):
    col_map = col_map or (lambda n: n)
    rows, kdim = x.shape
    ndim = dy.shape[1]
    kb = k_block or kdim
    nb = n_block or ndim
    step_bytes = lambda t: 2 * t * (kb * x.dtype.itemsize + nb * dy.dtype.itemsize) + 2 * kb * nb * 4
    tr = next(t for t in (2816, 1408, TM_HEAVY, TM) if rows % t == 0 and (t == TM or step_bytes(t) <= TN_VMEM_BUDGET))
    nr, nk, nn = rows // tr, kdim // kb, ndim // nb
    cb0 = col0 // nb

    def body(*refs):
        x_ref, dy_ref, out_ref = refs[-3], refs[-2], refs[-1]
        part = _dot_tn(x_ref[...].astype(BF16), dy_ref[...].astype(BF16))

        @pl.when(pl.program_id(2) == 0)
        def _():
            out_ref[...] = part

        @pl.when(pl.program_id(2) > 0)
        def _():
            out_ref[...] += part

    in_specs = [
        pl.BlockSpec((tr, kb), lambda n, k, r: (r, k)),
        pl.BlockSpec((tr, nb), lambda n, k, r: (r, n)),
    ]
    args = [x, dy]
    aliases = {}
    if prev is not None:
        in_specs = [pl.BlockSpec(memory_space=pl.ANY)] + in_specs
        args = [prev] + args
        aliases = {0: 0}
    (out,), rider_outs = _hosted_call(
        body,
        name=name,
        grid=(nn, nk, nr),
        in_specs=in_specs,
        out_specs=[pl.BlockSpec((kb, nb), lambda n, k, r: (k, cb0 + col_map(n)))],
        out_shape=[jax.ShapeDtypeStruct((kdim, out_cols), F32)],
        scratch_shapes=[],
        aliases=aliases,
        args=args,
        riders=riders,
    )
    return (out, rider_outs) if riders else out


def _local_step(x2d, target, w, plan):
    rows = FRONT + x2d.shape[0]
    head = jnp.concatenate([jnp.zeros((PAD_ROWS, D), F32), w["meta_tokens"]], axis=0)
    cos_t, sin_t = _rope_tables(rows)
    intra, intra_t, q_dec, k_dec, c_dec = _decay_tables()
    grads = {}

    def hosted(host, fn, *args, **kwargs):
        outs, rider_outs = fn(*args, riders=plan.riders(host, w, grads), **kwargs)
        plan.after(host, rider_outs, w, grads)
        return outs

    (u1, rstd1), _ = _norm1(head, x2d, w["mix_norm_w"])
    proj, w["w_in"] = hosted("in_proj", _in_proj, u1, w["w_in_shard"], w["route"], cos_t, sin_t)
    o, sprev = hosted("retention_fwd", _retention_fwd, proj, intra, q_dec, k_dec, c_dec)
    lru_args = (w["conv_w"], w["conv_b"], w["lru_wa"], w["lru_wx"], w["lru_ba"], w["lru_bx"], w["lru_lambda"])
    lru_saved = hosted("lru_fwd", _lru_fwd, proj, *lru_args)
    h_lru = lru_saved[0]
    y_ret, y_lru, h1, u2, rstd2 = hosted("mix_fwd", _mix_fwd, head, x2d, o, proj, h_lru, w["w_branch_ret"],
                                         w["w_branch_lru"], w["w_out"], w["ffn_norm_w"])
    silu, dsilu, act = hosted("ffn_fwd_gate", _ffn_fwd_gate, u2, w["w_ffn_in"])
    dh2, stats_loss = _ffn_out_loss(act, h1, w["w_ffn_out"], target, w["final_norm_w"])

    dg, dup, dh2b, dh1, stats_ffn = _ffn_bwd(dh2, silu, dsilu, h1, rstd2, w["w_ffn_in"], w["w_ffn_out"], w["ffn_norm_w"])
    grads["w_ffn_in"] = _matmul_tn("dw_ffn_up", u2, dup, 2 * FFN, col0=FFN, n_block=FFN_HALF,
                                   prev=_matmul_tn("dw_ffn_gate", u2, dg, 2 * FFN, n_block=FFN_HALF))
    grads["w_ffn_out"] = _matmul_tn("dw_ffn_out", act, dh2b, D, k_block=FFN_HALF)
    (dproj, do, dhl, mixed, a_ret, a_lru, dy_ret, dy_lru) = hosted(
        "mix_bwd", _mix_bwd, dh1, o, proj, h_lru, y_ret, y_lru, w["w_branch_ret"], w["w_branch_lru"], w["w_out"])
    grads["w_out"] = _matmul_tn("dw_out", mixed, dh1, D)
    grads["w_branch_ret"] = _matmul_tn("dw_branch_ret", a_ret, dy_ret, D)
    grads["w_branch_lru"] = _matmul_tn("dw_branch_lru", a_lru, dy_lru, D)
    (dproj,) = hosted("retention_bwd", _retention_bwd, dproj, proj, do, sprev, intra, intra_t, q_dec, k_dec, c_dec,
                      cos_t, sin_t)
    dproj, grads["lru_wa"], grads["lru_wx"], stats_lru = hosted(
        "lru_bwd", _lru_bwd, dproj, proj, lru_saved, dhl, w["conv_w"], w["lru_wa"], w["lru_wx"], w["lru_lambda"])
    grads["w_in"], rider_outs = _matmul_tn("dw_in", u1, dproj, IN_COLS, n_block=D, col_map=_swap_3_4,
                                           riders=plan.riders("dw_in", w, grads))
    plan.after("dw_in", rider_outs, w, grads)
    (du1,) = hosted("in_proj_bwd_0", _in_proj_bwd, dproj, w["w_in"], 0)
    (du1,) = hosted("in_proj_bwd_1", _in_proj_bwd, dproj, w["w_in"], 1, du1)
    (grad_x, grad_head, stats_in), _ = _norm1_bwd(du1, dh1, head, x2d, rstd1, w["mix_norm_w"])
    return grad_x, grad_head, grads, [stats_loss, stats_ffn, stats_in, stats_lru]


BIG_PIECES = {
    "w_in": ("col", (D, 2 * D)),
    "w_ffn_in": ("col", (D, FFN_HALF)),
    "w_ffn_out": ("row", (FFN // 4, D)),
    "w_branch_ret": ("row", (D // 4, D)),
    "w_branch_lru": ("row", (D // 4, D)),
    "w_out": ("row", (D // 4, D)),
    "lru_wa": ("lru", (LRU_BLOCKS, LRU_BLOCK // 4, LRU_BLOCK)),
    "lru_wx": ("lru", (LRU_BLOCKS, LRU_BLOCK // 4, LRU_BLOCK)),
}
SMALL_PIECES = {"meta_tokens": ("col", (N_META, D // 4)), "conv_w": ("col", (4, D // 4))}


def _full_shape(kind, shard):
    if kind == "col":
        return (shard[0], 4 * shard[1])
    if kind == "row":
        return (4 * shard[0], shard[1])
    return (shard[0], 4 * shard[1], shard[2])


def _half_shape(kind, shard):
    return (shard[0] // 2,) + tuple(shard[1:])


def _aligned(start, multiple):
    return start if isinstance(start, int) else pl.multiple_of(start, multiple)


def _lead(h, size):
    if h is None:
        return pl.ds(0, size)
    return pl.ds(_aligned(h * (size // 2), size // 2), size // 2)


def _full_region(ref, kind, shard, s, h):
    if kind == "col":
        return ref.at[_lead(h, shard[0]), pl.ds(_aligned(s * shard[1], shard[1]), shard[1])]
    if kind == "row":
        size = shard[0] if h is None else shard[0] // 2
        start = s * shard[0] + (0 if h is None else h * (shard[0] // 2))
        return ref.at[pl.ds(_aligned(start, BF16_ROWS), size), :]
    return ref.at[_lead(h, shard[0]), pl.ds(_aligned(s * shard[1], shard[1]), shard[1]), :]


def _shard_region(ref, shard, h):
    return ref.at[_lead(h, shard[0])]


def _place():
    x, y, c = lax.axis_index("x"), lax.axis_index("y"), lax.axis_index("c")
    return x, y, c, 2 * x + y


def _other_chip(s, c, k):
    s2 = jnp.bitwise_xor(s, k)
    return s2, (s2 // 2, s2 % 2, c)


def _remote(src, dst, send_sem, recv_sem, dev):
    return pltpu.make_async_remote_copy(src_ref=src, dst_ref=dst, send_sem=send_sem, recv_sem=recv_sem,
                                        device_id=dev, device_id_type=MESH)


_ANY = pl.BlockSpec(memory_space=pl.ANY)


class _Rider:
    def __init__(self, ins, out_shapes, sem_shapes, build, aliased=False):
        self.ins, self.out_shapes, self.sem_shapes, self.build, self.aliased = ins, out_shapes, sem_shapes, build, aliased


def _hosted_call(body, *, name, grid, in_specs, out_specs, out_shape, scratch_shapes, args, riders=(), aliases=None,
                 prefetch=None, riders_after_body=False):
    n_in, n_out, n_sc = len(in_specs), len(out_shape), len(scratch_shapes)
    r_in = [a for r in riders for a in r.ins]
    r_out = [s for r in riders for s in r.out_shapes]
    r_sem = [s for r in riders for s in r.sem_shapes]
    lead = () if prefetch is None else (prefetch,)
    assert prefetch is None or not (aliases or any(r.aliased for r in riders))

    def full_body(*refs):
        head, refs = refs[:len(lead)], refs[len(lead):]
        ins, rin = refs[:n_in], refs[n_in:n_in + len(r_in)]
        o0 = n_in + len(r_in)
        outs, rout = refs[o0:o0 + n_out], refs[o0 + n_out:o0 + n_out + len(r_out)]
        s0 = o0 + n_out + len(r_out)
        scratch, rsem = refs[s0:s0 + n_sc], refs[s0 + n_sc:]
        starts, waits = [], []
        pi = po = ps = 0
        for r in riders:
            st, wt = r.build(rin[pi:pi + len(r.ins)], rout[po:po + len(r.out_shapes)], rsem[ps:ps + len(r.sem_shapes)])
            starts += st
            waits += wt
            pi, po, ps = pi + len(r.ins), po + len(r.out_shapes), ps + len(r.sem_shapes)
        first = functools.reduce(jnp.logical_and, [pl.program_id(d) == 0 for d in range(len(grid))])
        last = functools.reduce(jnp.logical_and, [pl.program_id(d) == grid[d] - 1 for d in range(len(grid))])

        def start_riders():
            @pl.when(first)
            def _():
                for cp in starts:
                    cp.start()

        if riders and not riders_after_body:
            start_riders()
        body(*head, *ins, *outs, *scratch)
        if riders and riders_after_body:
            start_riders()
        if riders:
            @pl.when(last)
            def _():
                for wait in waits:
                    wait()

    io_aliases = dict(aliases or {})
    pi = po = 0
    for r in riders:
        if r.aliased:
            for q in range(len(r.ins)):
                io_aliases[n_in + pi + q] = n_out + po + q
        pi, po = pi + len(r.ins), po + len(r.out_shapes)
    specs = dict(
        grid=grid,
        in_specs=list(in_specs) + [_ANY] * len(r_in),
        out_specs=list(out_specs) + [_ANY] * len(r_out),
        scratch_shapes=list(scratch_shapes) + r_sem,
    )
    if prefetch is not None:
        specs = dict(grid_spec=pltpu.PrefetchScalarGridSpec(num_scalar_prefetch=1, **specs))
    res = pl.pallas_call(
        full_body,
        name=name,
        out_shape=list(out_shape) + r_out,
        input_output_aliases=io_aliases,
        compiler_params=_params(("arbitrary",) * len(grid)),
        **specs,
    )(*lead, *args, *r_in)
    rider_outs, po = [], n_out
    for r in riders:
        rider_outs.append(list(res[po:po + len(r.out_shapes)]))
        po += len(r.out_shapes)
    return list(res[:n_out]), rider_outs


def _run_riders(name, riders):
    r_in = [a for r in riders for a in r.ins]
    r_out = [s for r in riders for s in r.out_shapes]
    r_sem = [s for r in riders for s in r.sem_shapes]

    def body(*refs):
        rin, rout, rsem = refs[:len(r_in)], refs[len(r_in):len(r_in) + len(r_out)], refs[len(r_in) + len(r_out):]
        pi = po = ps = 0
        all_waits = []
        for r in riders:
            starts, waits = r.build(rin[pi:pi + len(r.ins)], rout[po:po + len(r.out_shapes)], rsem[ps:ps + len(r.sem_shapes)])
            for cp in starts:
                cp.start()
            all_waits += waits
            pi, po, ps = pi + len(r.ins), po + len(r.out_shapes), ps + len(r.sem_shapes)
        for wait in all_waits:
            wait()

    io_aliases = {}
    pi = po = 0
    for r in riders:
        if r.aliased:
            for q in range(len(r.ins)):
                io_aliases[pi + q] = po + q
        pi, po = pi + len(r.ins), po + len(r.out_shapes)
    res = pl.pallas_call(
        body,
        name=name,
        in_specs=[_ANY] * len(r_in),
        out_specs=[_ANY] * len(r_out),
        out_shape=r_out,
        scratch_shapes=r_sem,
        input_output_aliases=io_aliases,
    )(*r_in)
    outs, po = [], 0
    for r in riders:
        outs.append(list(res[po:po + len(r.out_shapes)]))
        po += len(r.out_shapes)
    return outs


def _piece(name):
    if name in BIG_PIECES:
        return (name, *BIG_PIECES[name], True)
    return (name, *SMALL_PIECES[name], False)


def _gather_rider(shards, names):
    pieces = [_piece(n) for n in names]
    n = len(pieces)

    def build(ins, outs, sems):
        local_sem, ici_send, ici_recv = sems
        _, _, c, s = _place()
        starts, waits = [], []
        for p, (_, kind, shard, split) in enumerate(pieces):
            cp = pltpu.make_async_copy(ins[p], _full_region(outs[p], kind, shard, s, None), local_sem.at[p])
            starts.append(cp)
            waits.append(cp.wait)
            h = c if split else None
            for k in (1, 2, 3):
                s2, dev = _other_chip(s, c, k)
                cp = _remote(_shard_region(ins[p], shard, h), _full_region(outs[p], kind, shard, s, h),
                             ici_send.at[p, k - 1], ici_recv.at[p, k - 1], dev)
                starts.append(cp)
                waits.append(cp.wait_send)
                region = _full_region(outs[p], kind, shard, s2, h)
                waits.append(_remote(region, region, ici_send.at[p, k - 1], ici_recv.at[p, k - 1], dev).wait_recv)
        return starts, waits

    return _Rider(
        [shards[name] for name in names],
        [jax.ShapeDtypeStruct(_full_shape(kind, shard), shards[name].dtype) for name, kind, shard, _ in pieces],
        [pltpu.SemaphoreType.DMA((n,)), pltpu.SemaphoreType.DMA((n, 3)), pltpu.SemaphoreType.DMA((n, 3))],
        build)


def _forward_rider(gathered, names):
    pieces = [_piece(n) for n in names]
    n = len(pieces)

    def build(ins, outs, sems):
        fwd_send, fwd_recv = sems
        x, y, c, s = _place()
        sibling = (x, y, 1 - c)
        starts, waits = [], []
        for p, (_, kind, shard, _) in enumerate(pieces):
            for k in (1, 2, 3):
                s2, _ = _other_chip(s, c, k)
                mine = _full_region(outs[p], kind, shard, s2, c)
                theirs = _full_region(outs[p], kind, shard, s2, 1 - c)
                cp = _remote(mine, mine, fwd_send.at[p, k - 1], fwd_recv.at[p, k - 1], sibling)
                starts.append(cp)
                waits.append(cp.wait_send)
                waits.append(_remote(theirs, theirs, fwd_send.at[p, k - 1], fwd_recv.at[p, k - 1], sibling).wait_recv)
        return starts, waits

    return _Rider(
        [gathered[name] for name in names],
        [jax.ShapeDtypeStruct(gathered[name].shape, gathered[name].dtype) for name in names],
        [pltpu.SemaphoreType.DMA((n, 3)), pltpu.SemaphoreType.DMA((n, 3))],
        build, aliased=True)


def _pair_exchange_rider(grads, names):
    n = len(names)

    def build(ins, outs, sems):
        send_sem, recv_sem = sems
        x, y, c, _ = _place()
        sibling = (x, y, 1 - c)
        starts, waits = [], []
        for p, name in enumerate(names):
            kind, shard = BIG_PIECES[name]
            for s2 in range(4):
                cp = _remote(_full_region(ins[p], kind, shard, s2, 1 - c), outs[p].at[s2], send_sem.at[p, s2],
                             recv_sem.at[p, s2], sibling)
                starts.append(cp)
                waits.append(cp.wait_send)
                waits.append(_remote(outs[p].at[s2], outs[p].at[s2], send_sem.at[p, s2], recv_sem.at[p, s2], sibling).wait_recv)
        return starts, waits

    return _Rider(
        [grads[name] for name in names],
        [jax.ShapeDtypeStruct((4,) + _half_shape(*BIG_PIECES[name]), F32) for name in names],
        [pltpu.SemaphoreType.DMA((n, 4))] * 2,
        build)


def _half_specs(kind, shard):
    half = shard[0] // 2
    if kind == "col":
        full = pl.BlockSpec((half, shard[1]), lambda j, pr: (pr[1], j))
        buf = pl.BlockSpec((None, half, shard[1]), lambda j, pr: (j, 0, 0))
    elif kind == "row":
        full = pl.BlockSpec((half, shard[1]), lambda j, pr: (2 * j + pr[1], 0))
        buf = pl.BlockSpec((None, half, shard[1]), lambda j, pr: (j, 0, 0))
    else:
        full = pl.BlockSpec((half, shard[1], shard[2]), lambda j, pr: (pr[1], j, 0))
        buf = pl.BlockSpec((None, half, shard[1], shard[2]), lambda j, pr: (j, 0, 0, 0))
    return full, buf


def _pair_sum(name, grad, recv, place):
    kind, shard = BIG_PIECES[name]
    full, buf = _half_specs(kind, shard)

    def body(pr, g_ref, r_ref, o_ref):
        o_ref[...] = (g_ref[...] + r_ref[...]).astype(BF16)

    return pl.pallas_call(
        body,
        name="pair_sum_" + name,
        grid_spec=pltpu.PrefetchScalarGridSpec(num_scalar_prefetch=1, grid=(4,), in_specs=[full, buf], out_specs=buf),
        out_shape=jax.ShapeDtypeStruct((4,) + _half_shape(kind, shard), BF16),
        compiler_params=_params(("arbitrary",)),
    )(place, grad, recv)


def _chip_exchange_rider(sums, names):
    n = len(names)

    def build(ins, outs, sems):
        send_sem, recv_sem = sems
        _, _, c, s = _place()
        starts, waits = [], []
        for p in range(n):
            for k in (1, 2, 3):
                s2, dev = _other_chip(s, c, k)
                cp = _remote(ins[p].at[s2], outs[p].at[k - 1], send_sem.at[p, k - 1], recv_sem.at[p, k - 1], dev)
                starts.append(cp)
                waits.append(cp.wait_send)
                waits.append(_remote(outs[p].at[k - 1], outs[p].at[k - 1], send_sem.at[p, k - 1], recv_sem.at[p, k - 1],
                                     dev).wait_recv)
        return starts, waits

    return _Rider(
        [sums[name] for name in names],
        [jax.ShapeDtypeStruct((3,) + _half_shape(*BIG_PIECES[name]), BF16) for name in names],
        [pltpu.SemaphoreType.DMA((n, 3))] * 2,
        build)


def _chip_sum(name, grad, recv_pair, recv_chip, place):
    kind, shard = BIG_PIECES[name]
    half = shard[0] // 2
    tail = tuple(shard[1:])
    zeros = (0,) * len(tail)
    nt = 4 if kind != "lru" and half % (4 * BF16_ROWS) == 0 else 1
    rows = half // nt
    if kind == "col":
        full = pl.BlockSpec((rows,) + tail, lambda j, pr: (pr[1] * nt + j, pr[0]))
    elif kind == "row":
        full = pl.BlockSpec((rows,) + tail, lambda j, pr: ((2 * pr[0] + pr[1]) * nt + j, 0))
    else:
        full = pl.BlockSpec((rows,) + tail, lambda j, pr: (pr[1], pr[0], 0))
    pair = pl.BlockSpec((None, rows) + tail, lambda j, pr: (pr[0], j) + zeros)
    chip = pl.BlockSpec((3, rows) + tail, lambda j, pr: (0, j) + zeros)
    out = pl.BlockSpec((rows,) + tail, lambda j, pr: (pr[1] * nt + j,) + zeros)

    def body(pr, g_ref, rp_ref, rc_ref, o_ref):
        total = g_ref[...] + rp_ref[...]
        for k in range(3):
            total = total + rc_ref[k].astype(F32)
        o_ref[...] = total

    return pl.pallas_call(
        body,
        name="chip_sum_" + name,
        grid_spec=pltpu.PrefetchScalarGridSpec(num_scalar_prefetch=1, grid=(nt,), in_specs=[full, pair, chip], out_specs=out),
        out_shape=jax.ShapeDtypeStruct(shard, F32),
        compiler_params=_params(("arbitrary",)),
    )(place, grad, recv_pair, recv_chip)


def _sibling_exchange_rider(halves, names):
    n = len(names)

    def build(ins, outs, sems):
        send_sem, recv_sem = sems
        x, y, c, _ = _place()
        sibling = (x, y, 1 - c)
        starts, waits = [], []
        for p, name in enumerate(names):
            shard = BIG_PIECES[name][1]
            mine = _shard_region(outs[p], shard, c)
            theirs = _shard_region(outs[p], shard, 1 - c)
            cp = _remote(mine, mine, send_sem.at[p], recv_sem.at[p], sibling)
            starts.append(cp)
            waits.append(cp.wait_send)
            waits.append(_remote(theirs, theirs, send_sem.at[p], recv_sem.at[p], sibling).wait_recv)
        return starts, waits

    return _Rider(
        [halves[name] for name in names],
        [jax.ShapeDtypeStruct(BIG_PIECES[name][1], F32) for name in names],
        [pltpu.SemaphoreType.DMA((n,))] * 2,
        build, aliased=True)


FIRST_WEIGHTS = ["meta_tokens", "conv_w"]
WEIGHT_GROUPS = {
    "lru": ["lru_wa", "lru_wx"],
    "branch": ["w_branch_ret", "w_branch_lru", "w_out"],
    "ffn_in": ["w_ffn_in"],
    "ffn_out": ["w_ffn_out"],
}
WEIGHT_SCHEDULE = {
    "in_proj": [("gather", "lru")],
    "retention_fwd": [("forward", "lru"), ("gather", "branch")],
    "lru_fwd": [("forward", "branch"), ("gather", "ffn_in")],
    "mix_fwd": [("forward", "ffn_in"), ("gather", "ffn_out")],
    "ffn_fwd_gate": [("forward", "ffn_out")],
}
GRAD_GROUPS = {
    "ffn": ["w_ffn_in", "w_ffn_out"],
    "mixer": ["w_out", "w_branch_ret", "w_branch_lru", "lru_wa", "lru_wx"],
    "in": ["w_in"],
}
GRAD_SCHEDULE = {
    "mix_bwd": [("pair", "ffn")],
    "retention_bwd": [("chip", "ffn")],
    "lru_bwd": [("sibling", "ffn")],
    "dw_in": [("pair", "mixer")],
    "in_proj_bwd_0": [("chip", "mixer"), ("pair", "in")],
    "in_proj_bwd_1": [("sibling", "mixer"), ("chip", "in")],
}


class _CommPlan:
    def __init__(self, shards, place):
        self.shards, self.place = shards, place
        self.late = {}
        self.recv_pair, self.sums, self.recv_chip, self.halves, self.final = {}, {}, {}, {}, {}

    def _grad_rider(self, stage, group, grads):
        names = GRAD_GROUPS[group]
        if stage == "pair":
            return _pair_exchange_rider(grads, names)
        if stage == "chip":
            return _chip_exchange_rider(self.sums, names)
        return _sibling_exchange_rider(self.halves, names)

    def _grad_after(self, stage, group, outs, grads):
        names = GRAD_GROUPS[group]
        if stage == "pair":
            for n, o in zip(names, outs):
                self.recv_pair[n] = o
                self.sums[n] = _pair_sum(n, grads[n], o, self.place)
        elif stage == "chip":
            for n, o in zip(names, outs):
                self.halves[n] = _chip_sum(n, grads[n], self.recv_pair[n], o, self.place)
        else:
            self.final.update(zip(names, outs))

    def riders(self, host, w, grads):
        if host in WEIGHT_SCHEDULE:
            return [_gather_rider(self.shards, WEIGHT_GROUPS[group]) if stage == "gather"
                    else _forward_rider(self.late, WEIGHT_GROUPS[group]) for stage, group in WEIGHT_SCHEDULE[host]]
        return [self._grad_rider(stage, group, grads) for stage, group in GRAD_SCHEDULE.get(host, [])]

    def after(self, host, rider_outs, w, grads):
        for (stage, group), outs in zip(WEIGHT_SCHEDULE.get(host, []), rider_outs):
            (self.late if stage == "gather" else w).update(zip(WEIGHT_GROUPS[group], outs))
        for (stage, group), outs in zip(GRAD_SCHEDULE.get(host, []), rider_outs):
            self._grad_after(stage, group, outs, grads)

    def finish(self, partial):
        outs, (blocks,) = _run_riders("tail_exchange", [_sibling_exchange_rider(self.halves, GRAD_GROUPS["in"]),
                                                        _small_exchange_rider(partial)])
        self.final.update(zip(GRAD_GROUPS["in"], outs))
        return self.final, blocks


def _adamw_math(w, g, m, v):
    m = ADAM_B1 * m + (1.0 - ADAM_B1) * g
    v = ADAM_B2 * v + (1.0 - ADAM_B2) * (g * g)
    m_hat = m / (1.0 - ADAM_B1 ** ADAM_STEP)
    v_hat = v / (1.0 - ADAM_B2 ** ADAM_STEP)
    delta = -ADAM_LR * (m_hat / (jnp.sqrt(v_hat) + ADAM_EPS) + ADAM_WD * w)
    return delta, m, v


def _adamw(name, g, w, m, v):
    rows, cols = g.shape
    tr = rows // 4 if rows % 32 == 0 else rows

    def body(g_ref, w_ref, m_ref, v_ref, go_ref, d_ref, mo_ref, vo_ref):
        gv = g_ref[...]
        delta, m2, v2 = _adamw_math(w_ref[...], gv, m_ref[...], v_ref[...])
        go_ref[...] = gv
        d_ref[...] = delta
        mo_ref[...] = m2
        vo_ref[...] = v2

    spec = pl.BlockSpec((tr, cols), lambda i: (i, 0))
    return pl.pallas_call(
        body,
        name="adamw_" + name,
        grid=(rows // tr,),
        in_specs=[spec] * 4,
        out_specs=[spec] * 4,
        out_shape=[jax.ShapeDtypeStruct((rows, cols), F32)] * 4,
        compiler_params=_params(("arbitrary",)),
    )(g, w, m, v)


SMALL_ROWS = 48
VEC_ROWS = {"final_norm_w": 1, "ffn_norm_w": 8, "mix_norm_w": 16, "conv_b": 24, "lru_ba": 25, "lru_bx": 26, "lru_lambda": 27}
VEC_NAMES = list(VEC_ROWS)
CONV_W_ROW = 28
META_ROW = 32


def _small_exchange_rider(partial):
    def build(ins, outs, sems):
        local_sem, send_sem, recv_sem = sems
        _, _, c, s = _place()
        me = 2 * s + c
        cp = pltpu.make_async_copy(ins[0], outs[0].at[me], local_sem)
        starts, waits = [cp], [cp.wait]
        for k in range(1, 8):
            peer = jnp.bitwise_xor(me, k)
            dev = (peer // 4, (peer // 2) % 2, peer % 2)
            rd = _remote(ins[0], outs[0].at[me], send_sem.at[k - 1], recv_sem.at[k - 1], dev)
            starts.append(rd)
            waits.append(rd.wait_send)
            waits.append(_remote(ins[0], outs[0].at[peer], send_sem.at[k - 1], recv_sem.at[k - 1], dev).wait_recv)
        return starts, waits

    return _Rider([partial], [jax.ShapeDtypeStruct((8, SMALL_ROWS, D), F32)],
                  [pltpu.SemaphoreType.DMA, pltpu.SemaphoreType.DMA((7,)), pltpu.SemaphoreType.DMA((7,))], build)


def _small_update(blocks, place, vecs, conv, meta):
    nvec = len(VEC_NAMES)
    qcols = D // 4

    def in_order(ref):
        total = ref[0]
        for d in range(1, 8):
            total = total + ref[d]
        return total

    def body(pr, blocks_ref, cols_ref, *refs):
        vec_refs = refs[:3 * nvec]
        conv_refs = refs[3 * nvec:3 * nvec + 3]
        meta_refs = refs[3 * nvec + 3:3 * nvec + 6]
        outs = refs[3 * nvec + 6:]
        loss_ref, vec_out, conv_out, meta_out = outs[0], outs[1:5], outs[5:9], outs[9:13]
        tot, col = in_order(blocks_ref), in_order(cols_ref)
        loss_ref[...] = jnp.sum(tot[0:1, :], axis=1, keepdims=True)
        for o in vec_out:
            o[...] = jnp.zeros_like(o)
        for j, name in enumerate(VEC_NAMES):
            w, m, v = (r[...] for r in vec_refs[3 * j:3 * j + 3])
            g = tot[VEC_ROWS[name]:VEC_ROWS[name] + 1, :]
            if name == "lru_lambda":
                g = -g / (1.0 + jnp.exp(w))
            for o, val in zip(vec_out, (g,) + _adamw_math(w, g, m, v)):
                o[j:j + 1, :] = val
        for row, n_rows, ins, group in ((CONV_W_ROW, 4, conv_refs, conv_out), (META_ROW, N_META, meta_refs, meta_out)):
            g = col[row:row + n_rows, :]
            w, m, v = (r[...] for r in ins)
            for o, val in zip(group, (g,) + _adamw_math(w, g, m, v)):
                o[...] = val

    whole = lambda shape: pl.BlockSpec(shape, lambda i, pr: (0,) * len(shape))
    in_specs = [whole((8, SMALL_ROWS, D)), pl.BlockSpec((8, SMALL_ROWS, qcols), lambda i, pr: (0, 0, pr[0]))]
    in_specs += [whole((1, D))] * (3 * nvec) + [whole((4, qcols))] * 3 + [whole((N_META, qcols))] * 3
    out_shapes = [(1, 1)] + [(8, D)] * 4 + [(4, qcols)] * 4 + [(N_META, qcols)] * 4
    return pl.pallas_call(
        body,
        name="small_update",
        grid_spec=pltpu.PrefetchScalarGridSpec(num_scalar_prefetch=1, grid=(1,), in_specs=in_specs,
                                               out_specs=[whole(s) for s in out_shapes]),
        out_shape=[jax.ShapeDtypeStruct(s, F32) for s in out_shapes],
        compiler_params=_params(("arbitrary",)),
    )(place, blocks, blocks, *[a for t in vecs for a in t], *conv, *meta)


WEIGHT_ORDER = ["meta_tokens", "mix_norm_w", "w_in", "conv_w", "conv_b", "lru_wa", "lru_ba", "lru_wx", "lru_bx", "lru_lambda",
                "w_branch_ret", "w_branch_lru", "w_out", "ffn_norm_w", "w_ffn_in", "w_ffn_out", "final_norm_w"]


def kernel(x, meta_tokens, mix_norm_w, w_in, conv_w, conv_b, lru_wa, lru_ba, lru_wx, lru_bx, lru_lambda, w_branch_ret, w_branch_lru, w_out, ffn_norm_w, w_ffn_in, w_ffn_out, final_norm_w, loss_target, m_meta_tokens, m_mix_norm_w, m_w_in, m_conv_w, m_conv_b, m_lru_wa, m_lru_ba, m_lru_wx, m_lru_bx, m_lru_lambda, m_w_branch_ret, m_w_branch_lru, m_w_out, m_ffn_norm_w, m_w_ffn_in, m_w_ffn_out, m_final_norm_w, v_meta_tokens, v_mix_norm_w, v_w_in, v_conv_w, v_conv_b, v_lru_wa, v_lru_ba, v_lru_wx, v_lru_bx, v_lru_lambda, v_w_branch_ret, v_w_branch_lru, v_w_out, v_ffn_norm_w, v_w_ffn_in, v_w_ffn_out, v_final_norm_w):
    args = locals()
    wts = {n: args[n] for n in WEIGHT_ORDER}
    mom = {n: args["m_" + n] for n in WEIGHT_ORDER}
    var = {n: args["v_" + n] for n in WEIGHT_ORDER}
    place = jnp.stack([2 * lax.axis_index("x") + lax.axis_index("y"), lax.axis_index("c")]).astype(jnp.int32)

    shards = {n: wts[n][0].astype(BF16) for n in BIG_PIECES}
    shards["meta_tokens"] = wts["meta_tokens"]
    shards["conv_w"] = wts["conv_w"][0]
    plan = _CommPlan(shards, place)
    (first,) = _run_riders("gather_first", [_gather_rider(shards, FIRST_WEIGHTS)])
    w = dict(zip(FIRST_WEIGHTS, first))
    for n in VEC_NAMES:
        w[n] = wts[n].reshape(1, D)
    chips = jnp.bitwise_xor(place[0], jnp.arange(4, dtype=jnp.int32))
    w["route"] = jnp.concatenate([place, jnp.stack([2 * chips, 2 * chips + 1], axis=1).reshape(8)])
    w["w_in_shard"] = shards["w_in"]

    grad_x, grad_head, _, stats = _local_step(x[0], loss_target[0], w, plan)
    partial = jnp.concatenate(stats + [grad_head[PAD_ROWS:]], axis=0)
    shard_grads, blocks = plan.finish(partial)

    out = {}
    for n in BIG_PIECES:
        shape2d = (-1, wts[n].shape[-1])
        res = _adamw(n, *[a.reshape(shape2d) for a in (shard_grads[n], wts[n], mom[n], var[n])])
        out[n] = [r.reshape(wts[n].shape) for r in res]

    vecs = [tuple(a[n].reshape(1, D) for a in (wts, mom, var)) for n in VEC_NAMES]
    conv = tuple(a["conv_w"][0] for a in (wts, mom, var))
    meta = tuple(a["meta_tokens"] for a in (wts, mom, var))
    res = _small_update(blocks, place, vecs, conv, meta)
    loss = res[0].reshape(())
    for j, n in enumerate(VEC_NAMES):
        out[n] = [r[j].reshape(wts[n].shape) for r in res[1:5]]
    out["conv_w"] = [r.reshape(wts["conv_w"].shape) for r in res[5:9]]
    out["meta_tokens"] = list(res[9:13])

    return (loss, grad_x.reshape(x.shape)) + tuple(out[n][kind] for kind in range(4) for n in WEIGHT_ORDER)
```

```python
import functools
import math

import jax
import jax.numpy as jnp
from jax import lax
from jax.experimental import pallas as pl
from jax.experimental.pallas import tpu as pltpu

F32 = jnp.float32
BF16 = jnp.bfloat16

LANES = 128
BF16_ROWS = 16

D = 1024
HEADS = 8
DH = 128
CHUNK = 256
N_META = 16
FRONT = 256
PAD_ROWS = FRONT - N_META
LRU_BLOCKS = 4
LRU_BLOCK = 256
LRU_C = 8.0
FFN = 2816
FFN_HALF = FFN // 2
IN_COLS = 8 * D
ROPE_BASE = 10000.0
EPS = 1e-6
QK_SCALE = DH ** -0.5

ADAM_LR = 0.001
ADAM_B1 = 0.9
ADAM_B2 = 0.999
ADAM_EPS = 1e-08
ADAM_WD = 0.01
ADAM_STEP = 10

TM = 256
TM_HEAVY = 768
MXU_TILE = 256
FFN_SUB = 2 * MXU_TILE
_FFN_SUBS = [(a, min(a + FFN_SUB, FFN)) for a in range(0, FFN, FFN_SUB)]
TM_MIX_BWD = 256
TM_LRU = 384
TM_MIX_FWD = 384
TM_IN_PROJ = 1408
VMEM_LIMIT = 60 * 1024 * 1024
TN_VMEM_BUDGET = 40 * 1024 * 1024

NT_DIMS = (((1,), (1,)), ((), ()))
TN_DIMS = (((0,), (0,)), ((), ()))
MESH = pl.DeviceIdType.MESH


def _params(sem=None):
    if sem is None:
        return pltpu.CompilerParams(vmem_limit_bytes=VMEM_LIMIT)
    return pltpu.CompilerParams(dimension_semantics=sem, vmem_limit_bytes=VMEM_LIMIT)


def _dot(a, b):
    return jnp.dot(a, b, preferred_element_type=F32)


def _dot_nt(a, b):
    return lax.dot_general(a, b, NT_DIMS, preferred_element_type=F32)


def _dot_tn(a, b):
    return lax.dot_general(a, b, TN_DIMS, preferred_element_type=F32)


def _sigmoid(z):
    return 1.0 / (1.0 + jnp.exp(-z))


def _log1p(x):
    return jnp.where(x < 1e-3, x * (1.0 - x * (0.5 - x * (1.0 / 3.0))), jnp.log(1.0 + x))


def _softplus(x):
    return jnp.maximum(x, 0.0) + _log1p(jnp.exp(-jnp.abs(x)))


def _one_minus_square(a, log_a):
    x = 2.0 * log_a
    series = -x * (1.0 + x * (0.5 + x * (1.0 / 6.0)))
    return jnp.where(x > -0.02, series, 1.0 - a * a)


_GELU_K = math.sqrt(2.0 / math.pi)


def _gelu_and_grad(x):
    inner = _GELU_K * (x + 0.044715 * x * x * x)
    t = jnp.tanh(inner)
    val = 0.5 * x * (1.0 + t)
    grad = 0.5 * (1.0 + t) + 0.5 * x * (1.0 - t * t) * _GELU_K * (1.0 + 3.0 * 0.044715 * x * x)
    return val, grad


def _row_ids(i, rows, shape):
    return i * rows + lax.broadcasted_iota(jnp.int32, shape, 0)


def _rope_tables(rows):
    inv_freq = ROPE_BASE ** (-jnp.arange(0, DH, 2, dtype=F32) / DH)
    block_pos = jnp.arange(rows // FRONT, dtype=jnp.int32) * FRONT - PAD_ROWS
    coarse = block_pos.astype(F32)[:, None] * inv_freq[None, :]
    fine = jnp.arange(FRONT, dtype=F32)[:, None] * inv_freq[None, :]
    ca, sa = jnp.cos(coarse)[:, None, :], jnp.sin(coarse)[:, None, :]
    cb, sb = jnp.cos(fine)[None, :, :], jnp.sin(fine)[None, :, :]
    cos = (ca * cb - sa * sb).reshape(rows, DH // 2)
    sin = (sa * cb + ca * sb).reshape(rows, DH // 2)
    return jnp.concatenate([cos, cos], axis=1), jnp.concatenate([-sin, sin], axis=1)


def _decay_tables():
    log_g = jnp.log(1.0 - 2.0 ** (-5.0 - jnp.arange(HEADS, dtype=F32)))
    idx = jnp.arange(CHUNK, dtype=F32)
    diff = idx[:, None] - idx[None, :]
    intra = jnp.where(diff[None] >= 0, jnp.exp(jnp.maximum(diff, 0.0)[None] * log_g[:, None, None]), 0.0)
    q_decay = jnp.exp((idx + 1.0)[:, None] * log_g[None, :])
    k_decay = jnp.exp((CHUNK - 1.0 - idx)[:, None] * log_g[None, :])
    chunk_decay = jnp.exp(CHUNK * log_g)
    wide = lambda a: jnp.repeat(a, DH, axis=-1)
    return intra, jnp.swapaxes(intra, 1, 2), wide(q_decay), wide(k_decay), wide(chunk_decay[None, :])


def _norm1(head, x2d, norm_w, riders=()):
    rows = FRONT + x2d.shape[0]
    tm = TM_IN_PROJ if rows % TM_IN_PROJ == 0 else TM_MIX_FWD
    nt = rows // tm

    def body(head_hbm, x_hbm, nw_ref, u_ref, rstd_ref, h0_sc, h0_sem):
        slot = _frame_rows(head_hbm, x_hbm, h0_sc, h0_sem, pl.program_id(0), tm, nt)
        hv = h0_sc[slot]
        rs = lax.rsqrt(jnp.mean(hv * hv, axis=-1, keepdims=True) + EPS)
        u_ref[...] = ((hv * rs) * nw_ref[...]).astype(BF16)
        rstd_ref[...] = rs

    return _hosted_call(
        body,
        name="norm1",
        grid=(nt,),
        in_specs=[_ANY, _ANY, pl.BlockSpec((1, D), lambda i: (0, 0))],
        out_specs=[pl.BlockSpec((tm, D), lambda i: (i, 0)), pl.BlockSpec((tm, 1), lambda i: (i, 0))],
        out_shape=[jax.ShapeDtypeStruct((rows, D), BF16), jax.ShapeDtypeStruct((rows, 1), F32)],
        scratch_shapes=[pltpu.VMEM((2, tm, D), F32), pltpu.SemaphoreType.DMA((3,))],
        args=(head, x2d, norm_w),
        riders=riders,
    )


def _heavy_tile(rows):
    return TM_HEAVY if rows % TM_HEAVY == 0 else TM


def _lru_tile(rows):
    return TM_LRU if rows % TM_LRU == 0 else TM


def _swap_3_4(group):
    return jnp.where(group == 3, 4, jnp.where(group == 4, 3, group))


LRU_IN_COL = 3
GATES_COL = 1


def _in_proj(u, w_shard, route, cos_t, sin_t, riders=()):
    rows = u.shape[0]
    tm = TM_IN_PROJ if rows % TM_IN_PROJ == 0 else _heavy_tile(rows)
    nt = rows // tm
    kind, shard = BIG_PIECES["w_in"]

    def body(route_ref, u_hbm, wsh_ref, cos_ref, sin_ref, proj_ref, wfull_ref,
             u_sc, w_sc, u_sem, w_sem, local_sem, ici_send, ici_recv, fwd_send, fwd_recv):
        g, i = pl.program_id(0), pl.program_id(1)
        s, c = route_ref[0], route_ref[1]
        gid = route_ref[2 + g]
        sibling = (s // 2, s % 2, 1 - c)
        local = pltpu.make_async_copy(wsh_ref, _full_region(wfull_ref, kind, shard, s, None), local_sem)
        u_copies = [pltpu.make_async_copy(u_hbm.at[pl.ds(t * tm, tm)], u_sc.at[t], u_sem.at[t]) for t in range(nt)]
        sends, arrivals = [], []
        for k in (1, 2, 3):
            s2, dev = _other_chip(s, c, k)
            sends.append(_remote(_shard_region(wsh_ref, shard, c), _full_region(wfull_ref, kind, shard, s, c),
                                 ici_send.at[k - 1], ici_recv.at[k - 1], dev))
            mine = _full_region(wfull_ref, kind, shard, s2, c)
            theirs = _full_region(wfull_ref, kind, shard, s2, 1 - c)
            arrivals.append((_remote(mine, mine, ici_send.at[k - 1], ici_recv.at[k - 1], dev),
                             _remote(mine, mine, fwd_send.at[k - 1], fwd_recv.at[k - 1], sibling),
                             _remote(theirs, theirs, fwd_send.at[k - 1], fwd_recv.at[k - 1], sibling)))

        slot = g % 2
        last_tile = i == nt - 1

        def block_copy(src, col, to_slot):
            return pltpu.make_async_copy(src.at[:, pl.ds(pl.multiple_of(col * D, D), D)], w_sc.at[to_slot],
                                         w_sem.at[to_slot])

        @pl.when(jnp.logical_and(g == 0, i == 0))
        def _():
            for cp in sends:
                cp.start()
            local.start()
            for cp in u_copies:
                cp.start()
            cp = block_copy(wsh_ref, 0, 0)
            cp.start()
            cp.wait()

        @pl.when(jnp.logical_and(last_tile, g == 0))
        def _():
            block_copy(wsh_ref, 1, 1).start()

        for k, (arrived, forward, forwarded) in zip((1, 2, 3), arrivals):
            @pl.when(jnp.logical_and(last_tile, g == 2 * k - 1))
            def _():
                arrived.wait_recv()
                forward.start()
                forwarded.wait_recv()
                block_copy(wfull_ref, route_ref[2 + 2 * k], 0).start()

            @pl.when(jnp.logical_and(last_tile, g == 2 * k))
            def _():
                block_copy(wfull_ref, route_ref[3 + 2 * k], 1).start()

        @pl.when(jnp.logical_and(i == 0, g > 0))
        def _():
            block_copy(wfull_ref, 0, slot).wait()

        for t in range(nt):
            @pl.when(jnp.logical_and(g == 0, i == t))
            def _():
                u_copies[t].wait()

        acc = _dot(u_sc[i], w_sc[slot])

        @pl.when(gid < 2)
        def _():
            scale = jnp.where(gid == 1, QK_SCALE, 1.0).astype(F32)
            for h in range(HEADS):
                sl = slice(h * DH, (h + 1) * DH)
                blk = acc[:, sl]
                out = (blk * cos_ref[...] + pltpu.roll(blk, DH // 2, axis=1) * sin_ref[...]) * scale
                proj_ref[:, sl] = out.astype(BF16)

        @pl.when(gid >= 2)
        def _():
            proj_ref[...] = acc.astype(BF16)

        @pl.when(jnp.logical_and(g == 7, i == nt - 1))
        def _():
            local.wait()
            for cp in sends:
                cp.wait_send()
            for _, forward, _ in arrivals:
                forward.wait_send()

    return _hosted_call(
        body,
        name="in_proj",
        grid=(8, nt),
        in_specs=[
            _ANY, _ANY,
            pl.BlockSpec((tm, DH), lambda g, i, rt: (i, 0)),
            pl.BlockSpec((tm, DH), lambda g, i, rt: (i, 0)),
        ],
        out_specs=[pl.BlockSpec((tm, D), lambda g, i, rt: (i, _swap_3_4(rt[2 + g]))), _ANY],
        out_shape=[jax.ShapeDtypeStruct((rows, IN_COLS), BF16), jax.ShapeDtypeStruct((D, IN_COLS), BF16)],
        scratch_shapes=[
            pltpu.VMEM((nt, tm, D), BF16), pltpu.VMEM((2, D, D), BF16),
            pltpu.SemaphoreType.DMA((nt,)), pltpu.SemaphoreType.DMA((2,)), pltpu.SemaphoreType.DMA,
            pltpu.SemaphoreType.DMA((3,)), pltpu.SemaphoreType.DMA((3,)),
            pltpu.SemaphoreType.DMA((3,)), pltpu.SemaphoreType.DMA((3,)),
        ],
        args=(u, w_shard, cos_t, sin_t),
        riders=riders,
        prefetch=route,
        riders_after_body=True,
    )


def _retention_fwd(proj_bf, intra, q_dec, k_dec, c_dec, riders=()):
    rows = proj_bf.shape[0]
    nc = rows // CHUNK

    def body(q_ref, k_ref, v_ref, m_ref, qd_ref, kd_ref, cd_ref, o_ref, sprev_ref, s_sc):
        @pl.when(pl.program_id(0) == 0)
        def _():
            s_sc[...] = jnp.zeros_like(s_sc)

        for h in range(HEADS):
            sl = slice(h * DH, (h + 1) * DH)
            q, k, v = q_ref[:, sl], k_ref[:, sl], v_ref[:, sl]
            state = s_sc[h]
            state_b = state.astype(BF16)
            sprev_ref[0, h] = state_b
            s = _dot_nt(q, k) * m_ref[h]
            inner = _dot(s.astype(BF16), v)
            cross = _dot(q, state_b) * qd_ref[:, sl]
            o_ref[:, sl] = (inner + cross).astype(BF16)
            k_scaled = (k.astype(F32) * kd_ref[:, sl]).astype(BF16)
            s_sc[h] = state * cd_ref[:, sl] + _dot_tn(k_scaled, v)

    chunk_spec = lambda col: pl.BlockSpec((CHUNK, D), lambda c: (c, col))
    const2 = lambda shape: pl.BlockSpec(shape, lambda c: (0, 0))
    return _hosted_call(
        body,
        name="retention_fwd",
        grid=(nc,),
        in_specs=[
            chunk_spec(0), chunk_spec(1), chunk_spec(2),
            pl.BlockSpec((HEADS, CHUNK, CHUNK), lambda c: (0, 0, 0)),
            const2((CHUNK, D)), const2((CHUNK, D)), const2((1, D)),
        ],
        out_specs=[
            pl.BlockSpec((CHUNK, D), lambda c: (c, 0)),
            pl.BlockSpec((1, HEADS, DH, DH), lambda c: (c, 0, 0, 0)),
        ],
        out_shape=[
            jax.ShapeDtypeStruct((rows, D), BF16),
            jax.ShapeDtypeStruct((nc, HEADS, DH, DH), BF16),
        ],
        scratch_shapes=[pltpu.VMEM((HEADS, DH, DH), F32)],
        args=(proj_bf, proj_bf, proj_bf, intra, q_dec, k_dec, c_dec),
        riders=riders,
    )


def _shift_down(x, first8_prev, d):
    rolled = pltpu.roll(x, d, axis=0)
    head = pltpu.roll(jnp.concatenate([first8_prev, x[0:8]], axis=0), d, axis=0)[8:16]
    return rolled, head


def _conv_and_gates(x, prev8, cw_ref, cb_ref, wa_ref, wx_ref, ba_ref, bx_ref, lam_ref, c_sc):
    cw = cw_ref[...]
    conv = cb_ref[...] + cw[3:4] * x
    head = cb_ref[...] + cw[3:4] * x[0:8]
    for d in (1, 2, 3):
        rolled, hd = _shift_down(x, prev8, d)
        conv = conv + cw[3 - d:4 - d] * rolled
        head = head + cw[3 - d:4 - d] * hd
    c_sc[...] = conv
    c_sc[0:8, :] = head
    c = c_sc[...]
    zr, zi = [], []
    for g in range(LRU_BLOCKS):
        sl = slice(g * LRU_BLOCK, (g + 1) * LRU_BLOCK)
        cg = c[:, sl].astype(BF16)
        zr.append(_dot(cg, wa_ref[g]))
        zi.append(_dot(cg, wx_ref[g]))
    r = _sigmoid(jnp.concatenate(zr, axis=1) + ba_ref[...])
    gate_i = _sigmoid(jnp.concatenate(zi, axis=1) + bx_ref[...])
    sp = _softplus(-lam_ref[...])
    log_a = (-LRU_C) * r * sp
    a = jnp.exp(log_a)
    mult = jnp.sqrt(_one_minus_square(a, log_a))
    return c, r, gate_i, a, mult, log_a


def _lru_fwd(proj, conv_w, conv_b, wa, wx, ba, bx, lam, riders=()):
    rows = proj.shape[0]
    TM = _lru_tile(rows)
    nt = rows // TM

    def body(x_ref, cw_ref, cb_ref, wa_ref, wx_ref, ba_ref, bx_ref, lam_ref,
             h_ref, c_ref, r_ref, i_ref, la_ref, mult_ref, prev_sc, carry_sc, c_sc, a_sc, u_sc, h_sc):
        i = pl.program_id(0)

        @pl.when(i == 0)
        def _():
            prev_sc[...] = jnp.zeros_like(prev_sc)
            carry_sc[...] = jnp.zeros_like(carry_sc)

        x = x_ref[...].astype(F32)
        c, r, gate_i, a, mult, log_a = _conv_and_gates(x, prev_sc[...], cw_ref, cb_ref, wa_ref, wx_ref, ba_ref, bx_ref,
                                                       lam_ref, c_sc)
        for ref, val in ((c_ref, c), (r_ref, r), (i_ref, gate_i), (la_ref, log_a), (mult_ref, mult)):
            ref[...] = val.astype(BF16)
        prev_sc[...] = x[TM - 8:TM]
        valid = _row_ids(i, TM, (TM, D)) >= PAD_ROWS
        a_sc[...] = a
        u_sc[...] = jnp.where(valid, mult * gate_i * c, 0.0)
        row8 = lax.broadcasted_iota(jnp.int32, (8, D), 0)

        def group(gi, hprev):
            r0 = pl.multiple_of(gi * 8, 8)
            aa = a_sc[pl.ds(r0, 8), :]
            uu = u_sc[pl.ds(r0, 8), :]
            for d in (1, 2, 4):
                a_sh = jnp.where(row8 >= d, pltpu.roll(aa, d, axis=0), 1.0)
                u_sh = jnp.where(row8 >= d, pltpu.roll(uu, d, axis=0), 0.0)
                uu = uu + aa * u_sh
                aa = aa * a_sh
            hb = aa * hprev + uu
            h_sc[pl.ds(r0, 8), :] = hb
            return hb[7:8, :]

        hlast = lax.fori_loop(0, TM // 8, group, carry_sc[0:1, :])
        carry_sc[0:1, :] = hlast
        h_ref[...] = h_sc[...].astype(BF16)

    vec = pl.BlockSpec((1, D), lambda i: (0, 0))
    wspec = pl.BlockSpec((LRU_BLOCKS, LRU_BLOCK, LRU_BLOCK), lambda i: (0, 0, 0))
    return _hosted_call(
        body,
        name="lru_fwd",
        grid=(nt,),
        in_specs=[
            pl.BlockSpec((TM, D), lambda i: (i, LRU_IN_COL)),
            pl.BlockSpec((4, D), lambda i: (0, 0)),
            vec, wspec, wspec, vec, vec, vec,
        ],
        out_specs=[pl.BlockSpec((TM, D), lambda i: (i, 0)) for _ in range(6)],
        out_shape=[jax.ShapeDtypeStruct((rows, D), BF16) for _ in range(6)],
        scratch_shapes=[
            pltpu.VMEM((8, D), F32), pltpu.VMEM((8, D), F32),
            pltpu.VMEM((TM, D), F32), pltpu.VMEM((TM, D), F32), pltpu.VMEM((TM, D), F32), pltpu.VMEM((TM, D), F32),
        ],
        args=(proj, conv_w, conv_b, wa, wx, ba, bx, lam),
        riders=riders,
    )


def _group_norm(o):
    outs, rstds = [], []
    for h in range(HEADS):
        oh = o[:, h * DH:(h + 1) * DH]
        rs = lax.rsqrt(jnp.mean(oh * oh, axis=-1, keepdims=True) + EPS)
        outs.append(oh * rs)
        rstds.append(rs)
    return jnp.concatenate(outs, axis=1), rstds


def _frame_rows(head_hbm, x_hbm, buf, sems, i, tm, nt):
    slot = i % 2

    @pl.when(i == 0)
    def _():
        first = [pltpu.make_async_copy(head_hbm, buf.at[0, pl.ds(0, FRONT)], sems.at[2]),
                 pltpu.make_async_copy(x_hbm.at[pl.ds(0, tm - FRONT)], buf.at[0, pl.ds(FRONT, tm - FRONT)], sems.at[0])]
        for cp in first:
            cp.start()
        for cp in first:
            cp.wait()

    @pl.when(i + 1 < nt)
    def _():
        start = pl.multiple_of((i + 1) * tm - FRONT, LANES)
        pltpu.make_async_copy(x_hbm.at[pl.ds(start, tm)], buf.at[1 - slot], sems.at[1 - slot]).start()

    @pl.when(i > 0)
    def _():
        pltpu.make_async_copy(x_hbm.at[pl.ds(0, tm)], buf.at[slot], sems.at[slot]).wait()

    return slot


def _mix_fwd(head, x2d, o, proj, h_lru, w_br, w_bl, w_o, ffn_norm_w, riders=()):
    rows = o.shape[0]
    tm = TM_MIX_FWD
    assert rows % tm == 0 and tm > FRONT
    nt = rows // tm

    def body(head_hbm, x_hbm, o_ref, gates_ref, hl_ref, wbr_ref, wbl_ref, wo_ref, nw_ref,
             yret_ref, ylru_ref, h1_ref, u2_ref, rstd_ref, h0_sc, h0_sem):
        i = pl.program_id(0)
        slot = _frame_rows(head_hbm, x_hbm, h0_sc, h0_sem, i, tm, nt)
        gate = lambda j: gates_ref[:, j * D:(j + 1) * D].astype(F32)
        on, _ = _group_norm(o_ref[...].astype(F32))
        gret = gate(0)
        a_ret = (gret * _sigmoid(gret) * on).astype(BF16)
        y_ret = _dot(a_ret, wbr_ref[...])
        gl, _ = _gelu_and_grad(gate(1))
        a_lru = (gl * hl_ref[...].astype(F32)).astype(BF16)
        y_lru = _dot(a_lru, wbl_ref[...])
        mixed = (_sigmoid(gate(2)) * y_ret + _sigmoid(gate(3)) * y_lru).astype(BF16)
        delta = _dot(mixed, wo_ref[...])
        yret_ref[...] = y_ret.astype(BF16)
        ylru_ref[...] = y_lru.astype(BF16)
        h1 = h0_sc[slot] + delta
        rs = lax.rsqrt(jnp.mean(h1 * h1, axis=-1, keepdims=True) + EPS)
        h1_ref[...] = h1
        u2_ref[...] = ((h1 * rs) * nw_ref[...]).astype(BF16)
        rstd_ref[...] = rs

    tile = lambda col: pl.BlockSpec((tm, D), lambda i: (i, col))
    wspec = pl.BlockSpec((D, D), lambda i: (0, 0))
    return _hosted_call(
        body,
        name="mix_fwd",
        grid=(nt,),
        in_specs=[
            _ANY, _ANY,
            tile(0), pl.BlockSpec((tm, 4 * D), lambda i: (i, GATES_COL)), tile(0),
            wspec, wspec, wspec,
            pl.BlockSpec((1, D), lambda i: (0, 0)),
        ],
        out_specs=[tile(0), tile(0), tile(0), tile(0), pl.BlockSpec((tm, 1), lambda i: (i, 0))],
        out_shape=[
            jax.ShapeDtypeStruct((rows, D), BF16), jax.ShapeDtypeStruct((rows, D), BF16),
            jax.ShapeDtypeStruct((rows, D), F32), jax.ShapeDtypeStruct((rows, D), BF16),
            jax.ShapeDtypeStruct((rows, 1), F32),
        ],
        scratch_shapes=[pltpu.VMEM((2, tm, D), F32), pltpu.SemaphoreType.DMA((3,))],
        args=(head, x2d, o, proj, h_lru, w_br, w_bl, w_o, ffn_norm_w),
        riders=riders,
    )


def _ffn_fwd_gate(u2, w_ffn_in, riders=()):
    rows = u2.shape[0]
    tm = _heavy_tile(rows)
    hid = lambda: pl.BlockSpec((tm, FFN), lambda i: (i, 0))

    def body(u2_ref, w_hbm, silu_ref, dsilu_ref, act_ref, w_sc, w_sem):
        @pl.when(pl.program_id(0) == 0)
        def _():
            cp = pltpu.make_async_copy(w_hbm, w_sc, w_sem)
            cp.start()
            cp.wait()

        u2 = u2_ref[...]
        for a, b in _FFN_SUBS:
            g = _dot(u2, w_sc[:, a:b])
            up = _dot(u2, w_sc[:, FFN + a:FFN + b])
            sg = _sigmoid(g)
            silu = g * sg
            silu_ref[:, a:b] = silu.astype(BF16)
            dsilu_ref[:, a:b] = (up * (sg * (1.0 + g * (1.0 - sg)))).astype(BF16)
            act_ref[:, a:b] = (silu * up).astype(BF16)

    return _hosted_call(
        body,
        name="ffn_fwd_gate",
        grid=(rows // tm,),
        in_specs=[pl.BlockSpec((tm, D), lambda i: (i, 0)), _ANY],
        out_specs=[hid(), hid(), hid()],
        out_shape=[jax.ShapeDtypeStruct((rows, FFN), BF16)] * 3,
        scratch_shapes=[pltpu.VMEM((D, 2 * FFN), BF16), pltpu.SemaphoreType.DMA],
        args=(u2, w_ffn_in),
        riders=riders,
    )


def _ffn_out_loss(act, h1, w_ffn_out, target, final_norm_w):
    rows = act.shape[0]
    tm = _heavy_tile(rows)
    nt = rows // tm

    def body(act_ref, h1_ref, wo_ref, fnw_ref, tgt_hbm, dh2_ref, stats_ref, tgt_sc, tgt_sem):
        i = pl.program_id(0)
        slot = i % 2

        @pl.when(i == 0)
        def _():
            stats_ref[...] = jnp.zeros_like(stats_ref)
            tgt_sc[0, 0:FRONT, :] = jnp.zeros((FRONT, D), F32)
            cp = pltpu.make_async_copy(tgt_hbm.at[pl.ds(0, tm - FRONT)], tgt_sc.at[0, pl.ds(FRONT, tm - FRONT)], tgt_sem.at[0])
            cp.start()
            cp.wait()

        @pl.when(i + 1 < nt)
        def _():
            start = pl.multiple_of((i + 1) * tm - FRONT, FRONT)
            pltpu.make_async_copy(tgt_hbm.at[pl.ds(start, tm)], tgt_sc.at[1 - slot], tgt_sem.at[1 - slot]).start()

        @pl.when(i > 0)
        def _():
            pltpu.make_async_copy(tgt_hbm.at[pl.ds(0, tm)], tgt_sc.at[slot], tgt_sem.at[slot]).wait()

        h2 = h1_ref[...] + _dot(act_ref[...], wo_ref[...])
        rs = lax.rsqrt(jnp.mean(h2 * h2, axis=-1, keepdims=True) + EPS)
        n = h2 * rs
        fnw = fnw_ref[...]
        valid = _row_ids(i, tm, (tm, D)) >= FRONT
        diff = jnp.where(valid, n * fnw - tgt_sc[slot], 0.0)
        dy = diff * (1.0 / D)
        stats_ref[0:1, :] += (0.5 / D) * jnp.sum(diff * diff, axis=0, keepdims=True)
        stats_ref[1:2, :] += jnp.sum(dy * n, axis=0, keepdims=True)
        dn = dy * fnw
        dh2_ref[...] = rs * (dn - n * jnp.mean(dn * n, axis=-1, keepdims=True))

    return pl.pallas_call(
        body,
        name="ffn_out_loss",
        grid=(nt,),
        in_specs=[
            pl.BlockSpec((tm, FFN), lambda i: (i, 0)),
            pl.BlockSpec((tm, D), lambda i: (i, 0)),
            pl.BlockSpec((FFN, D), lambda i: (0, 0)),
            pl.BlockSpec((1, D), lambda i: (0, 0)),
            _ANY,
        ],
        out_specs=[pl.BlockSpec((tm, D), lambda i: (i, 0)), pl.BlockSpec((8, D), lambda i: (0, 0))],
        out_shape=[jax.ShapeDtypeStruct((rows, D), F32), jax.ShapeDtypeStruct((8, D), F32)],
        scratch_shapes=[pltpu.VMEM((2, tm, D), F32), pltpu.SemaphoreType.DMA((2,))],
        compiler_params=_params(("arbitrary",)),
    )(act, h1, w_ffn_out, final_norm_w, target)


def _ffn_bwd(dh2, silu, dsilu, h1, rstd2, w_ffn_in, w_ffn_out, ffn_norm_w):
    rows = dh2.shape[0]
    tm = _heavy_tile(rows)
    blk = lambda: pl.BlockSpec((tm, FFN), lambda i: (i, 0))

    def gate_body(dh2_ref, silu_ref, dsilu_ref, wo_hbm, dg_ref, dup_ref, dh2b_ref, wo_sc, wo_sem):
        @pl.when(pl.program_id(0) == 0)
        def _():
            cp = pltpu.make_async_copy(wo_hbm, wo_sc, wo_sem)
            cp.start()
            cp.wait()

        dh2b = dh2_ref[...].astype(BF16)
        dh2b_ref[...] = dh2b
        for a, b in _FFN_SUBS:
            dact = _dot_nt(dh2b, wo_sc[a:b, :])
            dup_ref[:, a:b] = (dact * silu_ref[:, a:b].astype(F32)).astype(BF16)
            dg_ref[:, a:b] = (dact * dsilu_ref[:, a:b].astype(F32)).astype(BF16)

    dg, dup, dh2b = pl.pallas_call(
        gate_body,
        name="ffn_bwd_gate",
        grid=(rows // tm,),
        in_specs=[pl.BlockSpec((tm, D), lambda i: (i, 0)), blk(), blk(), _ANY],
        out_specs=[blk(), blk(), pl.BlockSpec((tm, D), lambda i: (i, 0))],
        out_shape=[jax.ShapeDtypeStruct((rows, FFN), BF16)] * 2 + [jax.ShapeDtypeStruct((rows, D), BF16)],
        scratch_shapes=[pltpu.VMEM((FFN, D), BF16), pltpu.SemaphoreType.DMA],
        compiler_params=_params(("arbitrary",)),
    )(dh2, silu, dsilu, w_ffn_out)

    def body(dg_ref, dup_ref, dh2_ref, h1_ref, rstd_ref, w_hbm, nw_ref, dh1_ref, stats_ref, w_sc, w_sem):
        @pl.when(pl.program_id(0) == 0)
        def _():
            stats_ref[...] = jnp.zeros_like(stats_ref)
            cp = pltpu.make_async_copy(w_hbm, w_sc, w_sem)
            cp.start()
            cp.wait()

        du = _dot_nt(dg_ref[...], w_sc[:, 0:FFN]) + _dot_nt(dup_ref[...], w_sc[:, FFN:2 * FFN])
        rs = rstd_ref[...]
        n = h1_ref[...] * rs
        stats_ref[0:1, :] += jnp.sum(du * n, axis=0, keepdims=True)
        dn = du * nw_ref[...]
        dh1_ref[...] = dh2_ref[...] + rs * (dn - n * jnp.mean(dn * n, axis=-1, keepdims=True))

    row = lambda width: pl.BlockSpec((tm, width), lambda i: (i, 0))
    dh1, stats = pl.pallas_call(
        body,
        name="ffn_bwd_in",
        grid=(rows // tm,),
        in_specs=[row(FFN), row(FFN), row(D), row(D), row(1), _ANY, pl.BlockSpec((1, D), lambda i: (0, 0))],
        out_specs=[row(D), pl.BlockSpec((8, D), lambda i: (0, 0))],
        out_shape=[jax.ShapeDtypeStruct((rows, D), F32), jax.ShapeDtypeStruct((8, D), F32)],
        scratch_shapes=[pltpu.VMEM((D, 2 * FFN), BF16), pltpu.SemaphoreType.DMA],
        compiler_params=_params(("arbitrary",)),
    )(dg, dup, dh2, h1, rstd2, w_ffn_in, ffn_norm_w)
    return dg, dup, dh2b, dh1, stats


def _mix_bwd(dh1, o, proj, h_lru, y_ret, y_lru, w_br, w_bl, w_o, riders=()):
    rows = dh1.shape[0]
    tm = TM_MIX_BWD
    nt = rows // tm

    def body(dh1_ref, o_ref, gates_ref, hl_ref, yret_ref, ylru_ref, wbr_ref, wbl_ref, wo_ref,
             dproj_ref, do_ref, dhl_ref, mixed_ref, aret_ref, alru_ref, dyret_ref, dylru_ref):
        gate = lambda j: gates_ref[:, j * D:(j + 1) * D].astype(F32)
        dmixed = _dot_nt(dh1_ref[...].astype(BF16), wo_ref[...])
        y_ret, y_lru = yret_ref[...].astype(F32), ylru_ref[...].astype(F32)
        sa, sb = _sigmoid(gate(2)), _sigmoid(gate(3))
        mixed_ref[...] = (sa * y_ret + sb * y_lru).astype(BF16)
        dga = dmixed * y_ret * sa * (1.0 - sa)
        dgb = dmixed * y_lru * sb * (1.0 - sb)
        dy_ret = (dmixed * sa).astype(BF16)
        dy_lru = (dmixed * sb).astype(BF16)
        dyret_ref[...] = dy_ret
        dylru_ref[...] = dy_lru
        da_ret = _dot_nt(dy_ret, wbr_ref[...])
        da_lru = _dot_nt(dy_lru, wbl_ref[...])

        gret = gate(0)
        sg = _sigmoid(gret)
        silu = gret * sg
        on, rstds = _group_norm(o_ref[...].astype(F32))
        aret_ref[...] = (silu * on).astype(BF16)
        dgret = da_ret * on * (sg * (1.0 + gret * (1.0 - sg)))
        don = da_ret * silu
        for h in range(HEADS):
            sl = slice(h * DH, (h + 1) * DH)
            onh, donh = on[:, sl], don[:, sl]
            do_ref[:, sl] = (rstds[h] * (donh - onh * jnp.mean(donh * onh, axis=-1, keepdims=True))).astype(BF16)

        gl, gl_grad = _gelu_and_grad(gate(1))
        hl = hl_ref[...].astype(F32)
        alru_ref[...] = (gl * hl).astype(BF16)
        dlgate = da_lru * hl * gl_grad
        dhl_ref[...] = (da_lru * gl).astype(BF16)

        for j, val in enumerate((dgret, dlgate, dga, dgb)):
            dproj_ref[:, j * D:(j + 1) * D] = val.astype(BF16)

    tile = lambda col: pl.BlockSpec((tm, D), lambda i: (i, col))
    gates = pl.BlockSpec((tm, 4 * D), lambda i: (i, GATES_COL))
    wspec = pl.BlockSpec((D, D), lambda i: (0, 0))
    bf = lambda: jax.ShapeDtypeStruct((rows, D), BF16)
    return _hosted_call(
        body,
        name="mix_bwd",
        grid=(nt,),
        in_specs=[tile(0), tile(0), gates, tile(0), tile(0), tile(0), wspec, wspec, wspec],
        out_specs=[pl.BlockSpec((tm, 4 * D), lambda i: (i, GATES_COL))] + [tile(0)] * 7,
        out_shape=[jax.ShapeDtypeStruct((rows, IN_COLS), BF16)] + [bf() for _ in range(7)],
        scratch_shapes=[],
        args=(dh1, o, proj, h_lru, y_ret, y_lru, w_br, w_bl, w_o),
        riders=riders,
    )


def _retention_bwd(dproj, proj_bf, do, sprev, intra, intra_t, q_dec, k_dec, c_dec, cos_t, sin_t, riders=()):
    rows = proj_bf.shape[0]
    nc = rows // CHUNK

    def body(dproj_in_ref, q_ref, k_ref, v_ref, do_ref, sprev_ref, m_ref, mt_ref, qd_ref, kd_ref, cd_ref, cos_ref, sin_ref,
             dproj_ref, ds_sc):
        @pl.when(pl.program_id(0) == 0)
        def _():
            ds_sc[...] = jnp.zeros_like(ds_sc)

        cos, sin = cos_ref[...], sin_ref[...]

        def unrotate(dy):
            return dy * cos - pltpu.roll(dy, DH // 2, axis=1) * sin

        for h in range(HEADS):
            sl = slice(h * DH, (h + 1) * DH)
            q, k, v = q_ref[:, sl], k_ref[:, sl], v_ref[:, sl]
            do = do_ref[:, sl]
            dob = do.astype(BF16)
            doq = (do * qd_ref[:, sl]).astype(BF16)
            state_prev = sprev_ref[0, h]
            dstate = ds_sc[h]
            dstate_b = dstate.astype(BF16)
            s_t = (_dot_nt(k, q) * mt_ref[h]).astype(BF16)
            ds_t = (_dot_nt(v, dob) * mt_ref[h]).astype(BF16)
            ds = (_dot_nt(dob, v) * m_ref[h]).astype(BF16)
            kd = kd_ref[:, sl]
            dq = _dot(ds, k) + _dot_nt(doq, state_prev)
            dk = _dot(ds_t, q) + _dot_nt(v, dstate_b) * kd
            k_scaled = (k.astype(F32) * kd).astype(BF16)
            dv = _dot(s_t, dob) + _dot(k_scaled, dstate_b)
            ds_sc[h] = dstate * cd_ref[:, sl] + _dot_tn(q, doq)
            for part, val in enumerate((unrotate(dq), unrotate(dk) * QK_SCALE, dv)):
                dproj_ref[:, part * D + h * DH:part * D + (h + 1) * DH] = val.astype(BF16)

    rev = lambda c: nc - 1 - c
    chunk_spec = lambda col: pl.BlockSpec((CHUNK, D), lambda c: (rev(c), col))
    const2 = lambda shape: pl.BlockSpec(shape, lambda c: (0, 0))
    const3 = pl.BlockSpec((HEADS, CHUNK, CHUNK), lambda c: (0, 0, 0))
    return _hosted_call(
        body,
        name="retention_bwd",
        grid=(nc,),
        in_specs=[
            pl.BlockSpec(memory_space=pl.ANY),
            chunk_spec(0), chunk_spec(1), chunk_spec(2), chunk_spec(0),
            pl.BlockSpec((1, HEADS, DH, DH), lambda c: (rev(c), 0, 0, 0)),
            const3, const3,
            const2((CHUNK, D)), const2((CHUNK, D)), const2((1, D)),
            pl.BlockSpec((CHUNK, DH), lambda c: (rev(c), 0)),
            pl.BlockSpec((CHUNK, DH), lambda c: (rev(c), 0)),
        ],
        out_specs=[pl.BlockSpec((CHUNK, 3 * D), lambda c: (rev(c), 0))],
        out_shape=[jax.ShapeDtypeStruct(dproj.shape, BF16)],
        aliases={0: 0},
        scratch_shapes=[pltpu.VMEM((HEADS, DH, DH), F32)],
        args=(dproj, proj_bf, proj_bf, proj_bf, do, sprev, intra, intra_t, q_dec, k_dec, c_dec, cos_t, sin_t),
        riders=riders,
    )


def _lru_bwd(dproj, proj, saved, dhl, conv_w, wa, wx, lam, riders=()):
    rows = proj.shape[0]
    TM = _lru_tile(rows)
    nt = rows // TM
    per8 = TM // 8

    def body(dproj_in_ref, x_ref, xprev_ref, h_ref, hprev_ref, c_ref, r_ref, i_ref, la_ref, mult_ref, dhl_ref,
             cw_ref, wa_ref, wx_ref, lam_ref,
             dproj_ref, dwa_ref, dwx_ref, stats_ref, anext_sc, dhnext_sc, dcnext_sc, c_sc, b_sc, dh_sc):
        step = pl.program_id(0)
        i = nt - 1 - step

        @pl.when(step == 0)
        def _():
            anext_sc[...] = jnp.zeros_like(anext_sc)
            dhnext_sc[...] = jnp.zeros_like(dhnext_sc)
            dcnext_sc[...] = jnp.zeros_like(dcnext_sc)
            dwa_ref[...] = jnp.zeros_like(dwa_ref)
            dwx_ref[...] = jnp.zeros_like(dwx_ref)
            stats_ref[...] = jnp.zeros_like(stats_ref)

        first = i == 0
        x = x_ref[...].astype(F32)
        prev8 = jnp.where(first, 0.0, xprev_ref[8:16, :].astype(F32))
        c_b = c_ref[...]
        c, r, gate_i, mult = (ref[...].astype(F32) for ref in (c_ref, r_ref, i_ref, mult_ref))
        a = jnp.exp(la_ref[...].astype(F32))
        sp = _softplus(-lam_ref[...])
        dh_sc[...] = dhl_ref[...].astype(F32)

        b_sc[...] = pltpu.roll(a, TM - 1, axis=0)
        b_sc[TM - 1:TM, :] = anext_sc[0:1, :]
        anext_sc[0:1, :] = a[0:1, :]
        row8 = lax.broadcasted_iota(jnp.int32, (8, D), 0)

        def group(gi, dhnext):
            r0 = pl.multiple_of((per8 - 1 - gi) * 8, 8)
            bb = b_sc[pl.ds(r0, 8), :]
            uu = dh_sc[pl.ds(r0, 8), :]
            for d in (1, 2, 4):
                b_sh = jnp.where(row8 < 8 - d, pltpu.roll(bb, 8 - d, axis=0), 1.0)
                u_sh = jnp.where(row8 < 8 - d, pltpu.roll(uu, 8 - d, axis=0), 0.0)
                uu = uu + bb * u_sh
                bb = bb * b_sh
            dhb = bb * dhnext + uu
            dh_sc[pl.ds(r0, 8), :] = dhb
            return dhb[0:1, :]

        dhfirst = lax.fori_loop(0, per8, group, dhnext_sc[0:1, :])
        dhnext_sc[0:1, :] = dhfirst
        dh = dh_sc[...]

        h = h_ref[...].astype(F32)
        hprev8 = jnp.where(first, 0.0, hprev_ref[8:16, :].astype(F32))
        h_dn, h_head = _shift_down(h, hprev8, 1)
        c_sc[...] = h_dn
        c_sc[0:8, :] = h_head
        h_before = c_sc[...]

        valid = _row_ids(i, TM, (TM, D)) >= PAD_ROWS
        da = dh * h_before
        du = jnp.where(valid, dh, 0.0)
        dmult = du * gate_i * c
        dgate_i = du * mult * c
        dc = du * mult * gate_i
        dla = da * a - dmult * (a * a) / mult
        dla = jnp.where(valid, dla, 0.0)
        dr = dla * ((-LRU_C) * sp)
        dzr = dr * r * (1.0 - r)
        dzi = dgate_i * gate_i * (1.0 - gate_i)
        stats_ref[1:2, :] += jnp.sum(dzr, axis=0, keepdims=True)
        stats_ref[2:3, :] += jnp.sum(dzi, axis=0, keepdims=True)
        stats_ref[3:4, :] += jnp.sum(dla * ((-LRU_C) * r), axis=0, keepdims=True)
        dc_gate = []
        for g in range(LRU_BLOCKS):
            sl = slice(g * LRU_BLOCK, (g + 1) * LRU_BLOCK)
            cg = c_b[:, sl]
            dzr_g = dzr[:, sl].astype(BF16)
            dzi_g = dzi[:, sl].astype(BF16)
            dc_gate.append(_dot_nt(dzr_g, wa_ref[g]) + _dot_nt(dzi_g, wx_ref[g]))
            dwa_ref[g] += _dot_tn(cg, dzr_g)
            dwx_ref[g] += _dot_tn(cg, dzi_g)
        dc = dc + jnp.concatenate(dc_gate, axis=1)

        cw = cw_ref[...]
        stats_ref[0:1, :] += jnp.sum(dc, axis=0, keepdims=True)
        stats_ref[7:8, :] += jnp.sum(dc * x, axis=0, keepdims=True)
        dx = cw[3:4] * dc
        tail_src = jnp.concatenate([dc[TM - 8:TM], dcnext_sc[...]], axis=0)
        dx_tail = cw[3:4] * dc[TM - 8:TM]
        for d in (1, 2, 3):
            dx = dx + cw[3 - d:4 - d] * pltpu.roll(dc, TM - d, axis=0)
            dx_tail = dx_tail + cw[3 - d:4 - d] * pltpu.roll(tail_src, 16 - d, axis=0)[0:8]
            rolled, hd = _shift_down(x, prev8, d)
            b_sc[...] = rolled
            b_sc[0:8, :] = hd
            stats_ref[7 - d:8 - d, :] += jnp.sum(dc * b_sc[...], axis=0, keepdims=True)
        dcnext_sc[...] = dc[0:8]
        dproj_ref[...] = dx.astype(BF16)
        dproj_ref[TM - 8:TM, :] = dx_tail.astype(BF16)

    rev = lambda s: nt - 1 - s
    vec = pl.BlockSpec((1, D), lambda s: (0, 0))
    wspec = pl.BlockSpec((LRU_BLOCKS, LRU_BLOCK, LRU_BLOCK), lambda s: (0, 0, 0))
    prev16 = lambda col: pl.BlockSpec((BF16_ROWS, D), lambda s: (jnp.maximum(rev(s) * (TM // BF16_ROWS) - 1, 0), col))
    tile = lambda: pl.BlockSpec((TM, D), lambda s: (rev(s), 0))
    h_lru, c_sv, r_sv, i_sv, la_sv, mult_sv = saved
    return _hosted_call(
        body,
        name="lru_bwd",
        grid=(nt,),
        in_specs=[
            pl.BlockSpec(memory_space=pl.ANY),
            pl.BlockSpec((TM, D), lambda s: (rev(s), LRU_IN_COL)), prev16(LRU_IN_COL),
            tile(), prev16(0),
            tile(), tile(), tile(), tile(), tile(), tile(),
            pl.BlockSpec((4, D), lambda s: (0, 0)),
            wspec, wspec, vec,
        ],
        out_specs=[
            pl.BlockSpec((TM, D), lambda s: (rev(s), LRU_IN_COL)),
            wspec, wspec,
            pl.BlockSpec((8, D), lambda s: (0, 0)),
        ],
        out_shape=[
            jax.ShapeDtypeStruct(dproj.shape, BF16),
            jax.ShapeDtypeStruct((LRU_BLOCKS, LRU_BLOCK, LRU_BLOCK), F32),
            jax.ShapeDtypeStruct((LRU_BLOCKS, LRU_BLOCK, LRU_BLOCK), F32),
            jax.ShapeDtypeStruct((8, D), F32),
        ],
        aliases={0: 0},
        scratch_shapes=[
            pltpu.VMEM((8, D), F32), pltpu.VMEM((8, D), F32), pltpu.VMEM((8, D), F32),
            pltpu.VMEM((TM, D), F32), pltpu.VMEM((TM, D), F32), pltpu.VMEM((TM, D), F32),
        ],
        args=(dproj, proj, proj, h_lru, h_lru, c_sv, r_sv, i_sv, la_sv, mult_sv, dhl, conv_w, wa, wx, lam),
        riders=riders,
    )


def _in_proj_bwd(dproj, w_in, part, prev=None, riders=()):
    rows = dproj.shape[0]
    tm = _heavy_tile(rows)
    nt = rows // tm
    split = max(1, nt // 4 + 1) if nt > 1 else 1
    first = 0 if part == 0 else split
    count = split if part == 0 else nt - split

    def body(*refs):
        dproj_ref, w_hbm, du_ref, w_sc, w_sem = refs[-5:]

        @pl.when(pl.program_id(0) == 0)
        def _():
            moves = [((0, 3), 0), ((4, 1), 3), ((3, 1), 4), ((5, 3), 5)]
            copies = [pltpu.make_async_copy(w_hbm.at[:, pl.ds(src * D, n * D)], w_sc.at[:, pl.ds(dst * D, n * D)], w_sem.at[q])
                      for q, ((src, n), dst) in enumerate(moves)]
            for cp in copies:
                cp.start()
            for cp in copies:
                cp.wait()

        du_ref[...] = _dot_nt(dproj_ref[...], w_sc[...])

    in_specs = [pl.BlockSpec((tm, IN_COLS), lambda i: (first + i, 0)), _ANY]
    args = (dproj, w_in)
    if prev is not None:
        in_specs = [_ANY] + in_specs
        args = (prev,) + args
    return _hosted_call(
        body,
        name="in_proj_bwd_%d" % part,
        grid=(count,),
        in_specs=in_specs,
        out_specs=[pl.BlockSpec((tm, D), lambda i: (first + i, 0))],
        out_shape=[jax.ShapeDtypeStruct((rows, D), F32)],
        scratch_shapes=[pltpu.VMEM((D, IN_COLS), BF16), pltpu.SemaphoreType.DMA((4,))],
        aliases={0: 0} if prev is not None else None,
        args=args,
        riders=riders,
    )


def _norm1_bwd(du, dh1, head, x2d, rstd1, norm_w, riders=()):
    rows = du.shape[0]

    def body(du_ref, dh1_ref, head_ref, x_ref, rstd_ref, nw_ref, gx_ref, ghead_ref, stats_ref):
        i = pl.program_id(0)

        @pl.when(i == 0)
        def _():
            stats_ref[...] = jnp.zeros_like(stats_ref)

        def finish(h0, out_ref):
            du = du_ref[...]
            rs = rstd_ref[...]
            n = h0 * rs
            stats_ref[0:1, :] += jnp.sum(du * n, axis=0, keepdims=True)
            dn = du * nw_ref[...]
            out_ref[...] = dh1_ref[...] + rs * (dn - n * jnp.mean(dn * n, axis=-1, keepdims=True))

        @pl.when(i == 0)
        def _():
            finish(head_ref[...], ghead_ref)

        @pl.when(i > 0)
        def _():
            finish(x_ref[...], gx_ref)

    tile = pl.BlockSpec((TM, D), lambda i: (i, 0))
    return _hosted_call(
        body,
        name="norm1_bwd",
        grid=(rows // TM,),
        in_specs=[
            tile, tile,
            pl.BlockSpec((FRONT, D), lambda i: (0, 0)),
            pl.BlockSpec((TM, D), lambda i: (jnp.maximum(i - 1, 0), 0)),
            pl.BlockSpec((TM, 1), lambda i: (i, 0)),
            pl.BlockSpec((1, D), lambda i: (0, 0)),
        ],
        out_specs=[
            pl.BlockSpec((TM, D), lambda i: (jnp.maximum(i - 1, 0), 0)),
            pl.BlockSpec((FRONT, D), lambda i: (0, 0)),
            pl.BlockSpec((8, D), lambda i: (0, 0)),
        ],
        out_shape=[
            jax.ShapeDtypeStruct(x2d.shape, F32),
            jax.ShapeDtypeStruct((FRONT, D), F32),
            jax.ShapeDtypeStruct((8, D), F32),
        ],
        scratch_shapes=[],
        args=(du, dh1, head, x2d, rstd1, norm_w),
        riders=riders,
    )


def _matmul_tn(name, x, dy, out_cols, col0=0, prev=None, k_block=None, n_block=None, riders=(), col_map=None):
    col_map = col_map or (lambda n: n)
    rows, kdim = x.shape
    ndim = dy.shape[1]
    kb = k_block or kdim
    nb = n_block or ndim
    step_bytes = lambda t: 2 * t * (kb * x.dtype.itemsize + nb * dy.dtype.itemsize) + 2 * kb * nb * 4
    tr = next(t for t in (2816, 1408, TM_HEAVY, TM) if rows % t == 0 and (t == TM or step_bytes(t) <= TN_VMEM_BUDGET))
    nr, nk, nn = rows // tr, kdim // kb, ndim // nb
    cb0 = col0 // nb

    def body(*refs):
        x_ref, dy_ref, out_ref = refs[-3], refs[-2], refs[-1]
        part = _dot_tn(x_ref[...].astype(BF16), dy_ref[...].astype(BF16))

        @pl.when(pl.program_id(2) == 0)
        def _():
            out_ref[...] = part

        @pl.when(pl.program_id(2) > 0)
        def _():
            out_ref[...] += part

    in_specs = [
        pl.BlockSpec((tr, kb), lambda n, k, r: (r, k)),
        pl.BlockSpec((tr, nb), lambda n, k, r: (r, n)),
    ]
    args = [x, dy]
    aliases = {}
    if prev is not None:
        in_specs = [pl.BlockSpec(memory_space=pl.ANY)] + in_specs
        args = [prev] + args
        aliases = {0: 0}
    (out,), rider_outs = _hosted_call(
        body,
        name=name,
        grid=(nn, nk, nr),
        in_specs=in_specs,
        out_specs=[pl.BlockSpec((kb, nb), lambda n, k, r: (k, cb0 + col_map(n)))],
        out_shape=[jax.ShapeDtypeStruct((kdim, out_cols), F32)],
        scratch_shapes=[],
        aliases=aliases,
        args=args,
        riders=riders,
    )
    return (out, rider_outs) if riders else out


def _local_step(x2d, target, w, plan):
    rows = FRONT + x2d.shape[0]
    head = jnp.concatenate([jnp.zeros((PAD_ROWS, D), F32), w["meta_tokens"]], axis=0)
    cos_t, sin_t = _rope_tables(rows)
    intra, intra_t, q_dec, k_dec, c_dec = _decay_tables()
    grads = {}

    def hosted(host, fn, *args, **kwargs):
        outs, rider_outs = fn(*args, riders=plan.riders(host, w, grads), **kwargs)
        plan.after(host, rider_outs, w, grads)
        return outs

    (u1, rstd1), _ = _norm1(head, x2d, w["mix_norm_w"])
    proj, w["w_in"] = hosted("in_proj", _in_proj, u1, w["w_in_shard"], w["route"], cos_t, sin_t)
    o, sprev = hosted("retention_fwd", _retention_fwd, proj, intra, q_dec, k_dec, c_dec)
    lru_args = (w["conv_w"], w["conv_b"], w["lru_wa"], w["lru_wx"], w["lru_ba"], w["lru_bx"], w["lru_lambda"])
    lru_saved = hosted("lru_fwd", _lru_fwd, proj, *lru_args)
    h_lru = lru_saved[0]
    y_ret, y_lru, h1, u2, rstd2 = hosted("mix_fwd", _mix_fwd, head, x2d, o, proj, h_lru, w["w_branch_ret"],
                                         w["w_branch_lru"], w["w_out"], w["ffn_norm_w"])
    silu, dsilu, act = hosted("ffn_fwd_gate", _ffn_fwd_gate, u2, w["w_ffn_in"])
    dh2, stats_loss = _ffn_out_loss(act, h1, w["w_ffn_out"], target, w["final_norm_w"])

    dg, dup, dh2b, dh1, stats_ffn = _ffn_bwd(dh2, silu, dsilu, h1, rstd2, w["w_ffn_in"], w["w_ffn_out"], w["ffn_norm_w"])
    grads["w_ffn_in"] = _matmul_tn("dw_ffn_up", u2, dup, 2 * FFN, col0=FFN, n_block=FFN_HALF,
                                   prev=_matmul_tn("dw_ffn_gate", u2, dg, 2 * FFN, n_block=FFN_HALF))
    grads["w_ffn_out"] = _matmul_tn("dw_ffn_out", act, dh2b, D, k_block=FFN_HALF)
    (dproj, do, dhl, mixed, a_ret, a_lru, dy_ret, dy_lru) = hosted(
        "mix_bwd", _mix_bwd, dh1, o, proj, h_lru, y_ret, y_lru, w["w_branch_ret"], w["w_branch_lru"], w["w_out"])
    grads["w_out"] = _matmul_tn("dw_out", mixed, dh1, D)
    grads["w_branch_ret"] = _matmul_tn("dw_branch_ret", a_ret, dy_ret, D)
    grads["w_branch_lru"] = _matmul_tn("dw_branch_lru", a_lru, dy_lru, D)
    (dproj,) = hosted("retention_bwd", _retention_bwd, dproj, proj, do, sprev, intra, intra_t, q_dec, k_dec, c_dec,
                      cos_t, sin_t)
    dproj, grads["lru_wa"], grads["lru_wx"], stats_lru = hosted(
        "lru_bwd", _lru_bwd, dproj, proj, lru_saved, dhl, w["conv_w"], w["lru_wa"], w["lru_wx"], w["lru_lambda"])
    grads["w_in"], rider_outs = _matmul_tn("dw_in", u1, dproj, IN_COLS, n_block=D, col_map=_swap_3_4,
                                           riders=plan.riders("dw_in", w, grads))
    plan.after("dw_in", rider_outs, w, grads)
    (du1,) = hosted("in_proj_bwd_0", _in_proj_bwd, dproj, w["w_in"], 0)
    (du1,) = hosted("in_proj_bwd_1", _in_proj_bwd, dproj, w["w_in"], 1, du1)
    (grad_x, grad_head, stats_in), _ = _norm1_bwd(du1, dh1, head, x2d, rstd1, w["mix_norm_w"])
    return grad_x, grad_head, grads, [stats_loss, stats_ffn, stats_in, stats_lru]


BIG_PIECES = {
    "w_in": ("col", (D, 2 * D)),
    "w_ffn_in": ("col", (D, FFN_HALF)),
    "w_ffn_out": ("row", (FFN // 4, D)),
    "w_branch_ret": ("row", (D // 4, D)),
    "w_branch_lru": ("row", (D // 4, D)),
    "w_out": ("row", (D // 4, D)),
    "lru_wa": ("lru", (LRU_BLOCKS, LRU_BLOCK // 4, LRU_BLOCK)),
    "lru_wx": ("lru", (LRU_BLOCKS, LRU_BLOCK // 4, LRU_BLOCK)),
}
SMALL_PIECES = {"meta_tokens": ("col", (N_META, D // 4)), "conv_w": ("col", (4, D // 4))}


def _full_shape(kind, shard):
    if kind == "col":
        return (shard[0], 4 * shard[1])
    if kind == "row":
        return (4 * shard[0], shard[1])
    return (shard[0], 4 * shard[1], shard[2])


def _half_shape(kind, shard):
    return (shard[0] // 2,) + tuple(shard[1:])


def _aligned(start, multiple):
    return start if isinstance(start, int) else pl.multiple_of(start, multiple)


def _lead(h, size):
    if h is None:
        return pl.ds(0, size)
    return pl.ds(_aligned(h * (size // 2), size // 2), size // 2)


def _full_region(ref, kind, shard, s, h):
    if kind == "col":
        return ref.at[_lead(h, shard[0]), pl.ds(_aligned(s * shard[1], shard[1]), shard[1])]
    if kind == "row":
        size = shard[0] if h is None else shard[0] // 2
        start = s * shard[0] + (0 if h is None else h * (shard[0] // 2))
        return ref.at[pl.ds(_aligned(start, BF16_ROWS), size), :]
    return ref.at[_lead(h, shard[0]), pl.ds(_aligned(s * shard[1], shard[1]), shard[1]), :]


def _shard_region(ref, shard, h):
    return ref.at[_lead(h, shard[0])]


def _place():
    x, y, c = lax.axis_index("x"), lax.axis_index("y"), lax.axis_index("c")
    return x, y, c, 2 * x + y


def _other_chip(s, c, k):
    s2 = jnp.bitwise_xor(s, k)
    return s2, (s2 // 2, s2 % 2, c)


def _remote(src, dst, send_sem, recv_sem, dev):
    return pltpu.make_async_remote_copy(src_ref=src, dst_ref=dst, send_sem=send_sem, recv_sem=recv_sem,
                                        device_id=dev, device_id_type=MESH)


_ANY = pl.BlockSpec(memory_space=pl.ANY)


class _Rider:
    def __init__(self, ins, out_shapes, sem_shapes, build, aliased=False):
        self.ins, self.out_shapes, self.sem_shapes, self.build, self.aliased = ins, out_shapes, sem_shapes, build, aliased


def _hosted_call(body, *, name, grid, in_specs, out_specs, out_shape, scratch_shapes, args, riders=(), aliases=None,
                 prefetch=None, riders_after_body=False):
    n_in, n_out, n_sc = len(in_specs), len(out_shape), len(scratch_shapes)
    r_in = [a for r in riders for a in r.ins]
    r_out = [s for r in riders for s in r.out_shapes]
    r_sem = [s for r in riders for s in r.sem_shapes]
    lead = () if prefetch is None else (prefetch,)
    assert prefetch is None or not (aliases or any(r.aliased for r in riders))

    def full_body(*refs):
        head, refs = refs[:len(lead)], refs[len(lead):]
        ins, rin = refs[:n_in], refs[n_in:n_in + len(r_in)]
        o0 = n_in + len(r_in)
        outs, rout = refs[o0:o0 + n_out], refs[o0 + n_out:o0 + n_out + len(r_out)]
        s0 = o0 + n_out + len(r_out)
        scratch, rsem = refs[s0:s0 + n_sc], refs[s0 + n_sc:]
        starts, waits = [], []
        pi = po = ps = 0
        for r in riders:
            st, wt = r.build(rin[pi:pi + len(r.ins)], rout[po:po + len(r.out_shapes)], rsem[ps:ps + len(r.sem_shapes)])
            starts += st
            waits += wt
            pi, po, ps = pi + len(r.ins), po + len(r.out_shapes), ps + len(r.sem_shapes)
        first = functools.reduce(jnp.logical_and, [pl.program_id(d) == 0 for d in range(len(grid))])
        last = functools.reduce(jnp.logical_and, [pl.program_id(d) == grid[d] - 1 for d in range(len(grid))])

        def start_riders():
            @pl.when(first)
            def _():
                for cp in starts:
                    cp.start()

        if riders and not riders_after_body:
            start_riders()
        body(*head, *ins, *outs, *scratch)
        if riders and riders_after_body:
            start_riders()
        if riders:
            @pl.when(last)
            def _():
                for wait in waits:
                    wait()

    io_aliases = dict(aliases or {})
    pi = po = 0
    for r in riders:
        if r.aliased:
            for q in range(len(r.ins)):
                io_aliases[n_in + pi + q] = n_out + po + q
        pi, po = pi + len(r.ins), po + len(r.out_shapes)
    specs = dict(
        grid=grid,
        in_specs=list(in_specs) + [_ANY] * len(r_in),
        out_specs=list(out_specs) + [_ANY] * len(r_out),
        scratch_shapes=list(scratch_shapes) + r_sem,
    )
    if prefetch is not None:
        specs = dict(grid_spec=pltpu.PrefetchScalarGridSpec(num_scalar_prefetch=1, **specs))
    res = pl.pallas_call(
        full_body,
        name=name,
        out_shape=list(out_shape) + r_out,
        input_output_aliases=io_aliases,
        compiler_params=_params(("arbitrary",) * len(grid)),
        **specs,
    )(*lead, *args, *r_in)
    rider_outs, po = [], n_out
    for r in riders:
        rider_outs.append(list(res[po:po + len(r.out_shapes)]))
        po += len(r.out_shapes)
    return list(res[:n_out]), rider_outs


def _run_riders(name, riders):
    r_in = [a for r in riders for a in r.ins]
    r_out = [s for r in riders for s in r.out_shapes]
    r_sem = [s for r in riders for s in r.sem_shapes]

    def body(*refs):
        rin, rout, rsem = refs[:len(r_in)], refs[len(r_in):len(r_in) + len(r_out)], refs[len(r_in) + len(r_out):]
        pi = po = ps = 0
        all_waits = []
        for r in riders:
            starts, waits = r.build(rin[pi:pi + len(r.ins)], rout[po:po + len(r.out_shapes)], rsem[ps:ps + len(r.sem_shapes)])
            for cp in starts:
                cp.start()
            all_waits += waits
            pi, po, ps = pi + len(r.ins), po + len(r.out_shapes), ps + len(r.sem_shapes)
        for wait in all_waits:
            wait()

    io_aliases = {}
    pi = po = 0
    for r in riders:
        if r.aliased:
            for q in range(len(r.ins)):
                io_aliases[pi + q] = po + q
        pi, po = pi + len(r.ins), po + len(r.out_shapes)
    res = pl.pallas_call(
        body,
        name=name,
        in_specs=[_ANY] * len(r_in),
        out_specs=[_ANY] * len(r_out),
        out_shape=r_out,
        scratch_shapes=r_sem,
        input_output_aliases=io_aliases,
    )(*r_in)
    outs, po = [], 0
    for r in riders:
        outs.append(list(res[po:po + len(r.out_shapes)]))
        po += len(r.out_shapes)
    return outs


def _piece(name):
    if name in BIG_PIECES:
        return (name, *BIG_PIECES[name], True)
    return (name, *SMALL_PIECES[name], False)


def _gather_rider(shards, names):
    pieces = [_piece(n) for n in names]
    n = len(pieces)

    def build(ins, outs, sems):
        local_sem, ici_send, ici_recv = sems
        _, _, c, s = _place()
        starts, waits = [], []
        for p, (_, kind, shard, split) in enumerate(pieces):
            cp = pltpu.make_async_copy(ins[p], _full_region(outs[p], kind, shard, s, None), local_sem.at[p])
            starts.append(cp)
            waits.append(cp.wait)
            h = c if split else None
            for k in (1, 2, 3):
                s2, dev = _other_chip(s, c, k)
                cp = _remote(_shard_region(ins[p], shard, h), _full_region(outs[p], kind, shard, s, h),
                             ici_send.at[p, k - 1], ici_recv.at[p, k - 1], dev)
                starts.append(cp)
                waits.append(cp.wait_send)
                region = _full_region(outs[p], kind, shard, s2, h)
                waits.append(_remote(region, region, ici_send.at[p, k - 1], ici_recv.at[p, k - 1], dev).wait_recv)
        return starts, waits

    return _Rider(
        [shards[name] for name in names],
        [jax.ShapeDtypeStruct(_full_shape(kind, shard), shards[name].dtype) for name, kind, shard, _ in pieces],
        [pltpu.SemaphoreType.DMA((n,)), pltpu.SemaphoreType.DMA((n, 3)), pltpu.SemaphoreType.DMA((n, 3))],
        build)


def _forward_rider(gathered, names):
    pieces = [_piece(n) for n in names]
    n = len(pieces)

    def build(ins, outs, sems):
        fwd_send, fwd_recv = sems
        x, y, c, s = _place()
        sibling = (x, y, 1 - c)
        starts, waits = [], []
        for p, (_, kind, shard, _) in enumerate(pieces):
            for k in (1, 2, 3):
                s2, _ = _other_chip(s, c, k)
                mine = _full_region(outs[p], kind, shard, s2, c)
                theirs = _full_region(outs[p], kind, shard, s2, 1 - c)
                cp = _remote(mine, mine, fwd_send.at[p, k - 1], fwd_recv.at[p, k - 1], sibling)
                starts.append(cp)
                waits.append(cp.wait_send)
                waits.append(_remote(theirs, theirs, fwd_send.at[p, k - 1], fwd_recv.at[p, k - 1], sibling).wait_recv)
        return starts, waits

    return _Rider(
        [gathered[name] for name in names],
        [jax.ShapeDtypeStruct(gathered[name].shape, gathered[name].dtype) for name in names],
        [pltpu.SemaphoreType.DMA((n, 3)), pltpu.SemaphoreType.DMA((n, 3))],
        build, aliased=True)


def _pair_exchange_rider(grads, names):
    n = len(names)

    def build(ins, outs, sems):
        send_sem, recv_sem = sems
        x, y, c, _ = _place()
        sibling = (x, y, 1 - c)
        starts, waits = [], []
        for p, name in enumerate(names):
            kind, shard = BIG_PIECES[name]
            for s2 in range(4):
                cp = _remote(_full_region(ins[p], kind, shard, s2, 1 - c), outs[p].at[s2], send_sem.at[p, s2],
                             recv_sem.at[p, s2], sibling)
                starts.append(cp)
                waits.append(cp.wait_send)
                waits.append(_remote(outs[p].at[s2], outs[p].at[s2], send_sem.at[p, s2], recv_sem.at[p, s2], sibling).wait_recv)
        return starts, waits

    return _Rider(
        [grads[name] for name in names],
        [jax.ShapeDtypeStruct((4,) + _half_shape(*BIG_PIECES[name]), F32) for name in names],
        [pltpu.SemaphoreType.DMA((n, 4))] * 2,
        build)


def _half_specs(kind, shard):
    half = shard[0] // 2
    if kind == "col":
        full = pl.BlockSpec((half, shard[1]), lambda j, pr: (pr[1], j))
        buf = pl.BlockSpec((None, half, shard[1]), lambda j, pr: (j, 0, 0))
    elif kind == "row":
        full = pl.BlockSpec((half, shard[1]), lambda j, pr: (2 * j + pr[1], 0))
        buf = pl.BlockSpec((None, half, shard[1]), lambda j, pr: (j, 0, 0))
    else:
        full = pl.BlockSpec((half, shard[1], shard[2]), lambda j, pr: (pr[1], j, 0))
        buf = pl.BlockSpec((None, half, shard[1], shard[2]), lambda j, pr: (j, 0, 0, 0))
    return full, buf


def _pair_sum(name, grad, recv, place):
    kind, shard = BIG_PIECES[name]
    full, buf = _half_specs(kind, shard)

    def body(pr, g_ref, r_ref, o_ref):
        o_ref[...] = (g_ref[...] + r_ref[...]).astype(BF16)

    return pl.pallas_call(
        body,
        name="pair_sum_" + name,
        grid_spec=pltpu.PrefetchScalarGridSpec(num_scalar_prefetch=1, grid=(4,), in_specs=[full, buf], out_specs=buf),
        out_shape=jax.ShapeDtypeStruct((4,) + _half_shape(kind, shard), BF16),
        compiler_params=_params(("arbitrary",)),
    )(place, grad, recv)


def _chip_exchange_rider(sums, names):
    n = len(names)

    def build(ins, outs, sems):
        send_sem, recv_sem = sems
        _, _, c, s = _place()
        starts, waits = [], []
        for p in range(n):
            for k in (1, 2, 3):
                s2, dev = _other_chip(s, c, k)
                cp = _remote(ins[p].at[s2], outs[p].at[k - 1], send_sem.at[p, k - 1], recv_sem.at[p, k - 1], dev)
                starts.append(cp)
                waits.append(cp.wait_send)
                waits.append(_remote(outs[p].at[k - 1], outs[p].at[k - 1], send_sem.at[p, k - 1], recv_sem.at[p, k - 1],
                                     dev).wait_recv)
        return starts, waits

    return _Rider(
        [sums[name] for name in names],
        [jax.ShapeDtypeStruct((3,) + _half_shape(*BIG_PIECES[name]), BF16) for name in names],
        [pltpu.SemaphoreType.DMA((n, 3))] * 2,
        build)


def _chip_sum(name, grad, recv_pair, recv_chip, place):
    kind, shard = BIG_PIECES[name]
    half = shard[0] // 2
    tail = tuple(shard[1:])
    zeros = (0,) * len(tail)
    nt = 4 if kind != "lru" and half % (4 * BF16_ROWS) == 0 else 1
    rows = half // nt
    if kind == "col":
        full = pl.BlockSpec((rows,) + tail, lambda j, pr: (pr[1] * nt + j, pr[0]))
    elif kind == "row":
        full = pl.BlockSpec((rows,) + tail, lambda j, pr: ((2 * pr[0] + pr[1]) * nt + j, 0))
    else:
        full = pl.BlockSpec((rows,) + tail, lambda j, pr: (pr[1], pr[0], 0))
    pair = pl.BlockSpec((None, rows) + tail, lambda j, pr: (pr[0], j) + zeros)
    chip = pl.BlockSpec((3, rows) + tail, lambda j, pr: (0, j) + zeros)
    out = pl.BlockSpec((rows,) + tail, lambda j, pr: (pr[1] * nt + j,) + zeros)

    def body(pr, g_ref, rp_ref, rc_ref, o_ref):
        total = g_ref[...] + rp_ref[...]
        for k in range(3):
            total = total + rc_ref[k].astype(F32)
        o_ref[...] = total

    return pl.pallas_call(
        body,
        name="chip_sum_" + name,
        grid_spec=pltpu.PrefetchScalarGridSpec(num_scalar_prefetch=1, grid=(nt,), in_specs=[full, pair, chip], out_specs=out),
        out_shape=jax.ShapeDtypeStruct(shard, F32),
        compiler_params=_params(("arbitrary",)),
    )(place, grad, recv_pair, recv_chip)


def _sibling_exchange_rider(halves, names):
    n = len(names)

    def build(ins, outs, sems):
        send_sem, recv_sem = sems
        x, y, c, _ = _place()
        sibling = (x, y, 1 - c)
        starts, waits = [], []
        for p, name in enumerate(names):
            shard = BIG_PIECES[name][1]
            mine = _shard_region(outs[p], shard, c)
            theirs = _shard_region(outs[p], shard, 1 - c)
            cp = _remote(mine, mine, send_sem.at[p], recv_sem.at[p], sibling)
            starts.append(cp)
            waits.append(cp.wait_send)
            waits.append(_remote(theirs, theirs, send_sem.at[p], recv_sem.at[p], sibling).wait_recv)
        return starts, waits

    return _Rider(
        [halves[name] for name in names],
        [jax.ShapeDtypeStruct(BIG_PIECES[name][1], F32) for name in names],
        [pltpu.SemaphoreType.DMA((n,))] * 2,
        build, aliased=True)


FIRST_WEIGHTS = ["meta_tokens", "conv_w"]
WEIGHT_GROUPS = {
    "lru": ["lru_wa", "lru_wx"],
    "branch": ["w_branch_ret", "w_branch_lru", "w_out"],
    "ffn_in": ["w_ffn_in"],
    "ffn_out": ["w_ffn_out"],
}
WEIGHT_SCHEDULE = {
    "in_proj": [("gather", "lru")],
    "retention_fwd": [("forward", "lru"), ("gather", "branch")],
    "lru_fwd": [("forward", "branch"), ("gather", "ffn_in")],
    "mix_fwd": [("forward", "ffn_in"), ("gather", "ffn_out")],
    "ffn_fwd_gate": [("forward", "ffn_out")],
}
GRAD_GROUPS = {
    "ffn_in": ["w_ffn_in"],
    "ffn_out": ["w_ffn_out"],
    "mixer": ["w_out", "w_branch_ret", "w_branch_lru", "lru_wa", "lru_wx"],
    "in": ["w_in"],
}
GRAD_SCHEDULE = {
    "mix_bwd": [("pair", "ffn_in"), ("pair", "ffn_out")],
    "retention_bwd": [("chip", "ffn_in")],
    "lru_bwd": [("chip", "ffn_out"), ("sibling", "ffn_in")],
    "dw_in": [("sibling", "ffn_out"), ("pair", "mixer")],
    "in_proj_bwd_0": [("chip", "mixer"), ("pair", "in")],
    "in_proj_bwd_1": [("sibling", "mixer"), ("chip", "in")],
}


class _CommPlan:
    def __init__(self, shards, place):
        self.shards, self.place = shards, place
        self.late = {}
        self.recv_pair, self.sums, self.recv_chip, self.halves, self.final = {}, {}, {}, {}, {}

    def _grad_rider(self, stage, group, grads):
        names = GRAD_GROUPS[group]
        if stage == "pair":
            return _pair_exchange_rider(grads, names)
        if stage == "chip":
            return _chip_exchange_rider(self.sums, names)
        return _sibling_exchange_rider(self.halves, names)

    def _grad_after(self, stage, group, outs, grads):
        names = GRAD_GROUPS[group]
        if stage == "pair":
            for n, o in zip(names, outs):
                self.recv_pair[n] = o
                self.sums[n] = _pair_sum(n, grads[n], o, self.place)
        elif stage == "chip":
            for n, o in zip(names, outs):
                self.halves[n] = _chip_sum(n, grads[n], self.recv_pair[n], o, self.place)
        else:
            self.final.update(zip(names, outs))

    def riders(self, host, w, grads):
        if host in WEIGHT_SCHEDULE:
            return [_gather_rider(self.shards, WEIGHT_GROUPS[group]) if stage == "gather"
                    else _forward_rider(self.late, WEIGHT_GROUPS[group]) for stage, group in WEIGHT_SCHEDULE[host]]
        return [self._grad_rider(stage, group, grads) for stage, group in GRAD_SCHEDULE.get(host, [])]

    def after(self, host, rider_outs, w, grads):
        for (stage, group), outs in zip(WEIGHT_SCHEDULE.get(host, []), rider_outs):
            (self.late if stage == "gather" else w).update(zip(WEIGHT_GROUPS[group], outs))
        for (stage, group), outs in zip(GRAD_SCHEDULE.get(host, []), rider_outs):
            self._grad_after(stage, group, outs, grads)

    def finish(self, partial):
        outs, (blocks,) = _run_riders("tail_exchange", [_sibling_exchange_rider(self.halves, GRAD_GROUPS["in"]),
                                                        _small_exchange_rider(partial)])
        self.final.update(zip(GRAD_GROUPS["in"], outs))
        return self.final, blocks


def _adamw_math(w, g, m, v):
    m = ADAM_B1 * m + (1.0 - ADAM_B1) * g
    v = ADAM_B2 * v + (1.0 - ADAM_B2) * (g * g)
    m_hat = m / (1.0 - ADAM_B1 ** ADAM_STEP)
    v_hat = v / (1.0 - ADAM_B2 ** ADAM_STEP)
    delta = -ADAM_LR * (m_hat / (jnp.sqrt(v_hat) + ADAM_EPS) + ADAM_WD * w)
    return delta, m, v


def _adamw(name, g, w, m, v):
    rows, cols = g.shape
    tr = rows // 4 if rows % 32 == 0 else rows

    def body(g_ref, w_ref, m_ref, v_ref, go_ref, d_ref, mo_ref, vo_ref):
        gv = g_ref[...]
        delta, m2, v2 = _adamw_math(w_ref[...], gv, m_ref[...], v_ref[...])
        go_ref[...] = gv
        d_ref[...] = delta
        mo_ref[...] = m2
        vo_ref[...] = v2

    spec = pl.BlockSpec((tr, cols), lambda i: (i, 0))
    return pl.pallas_call(
        body,
        name="adamw_" + name,
        grid=(rows // tr,),
        in_specs=[spec] * 4,
        out_specs=[spec] * 4,
        out_shape=[jax.ShapeDtypeStruct((rows, cols), F32)] * 4,
        compiler_params=_params(("arbitrary",)),
    )(g, w, m, v)


SMALL_ROWS = 48
VEC_ROWS = {"final_norm_w": 1, "ffn_norm_w": 8, "mix_norm_w": 16, "conv_b": 24, "lru_ba": 25, "lru_bx": 26, "lru_lambda": 27}
VEC_NAMES = list(VEC_ROWS)
CONV_W_ROW = 28
META_ROW = 32


def _small_exchange_rider(partial):
    def build(ins, outs, sems):
        local_sem, send_sem, recv_sem = sems
        _, _, c, s = _place()
        me = 2 * s + c
        cp = pltpu.make_async_copy(ins[0], outs[0].at[me], local_sem)
        starts, waits = [cp], [cp.wait]
        for k in range(1, 8):
            peer = jnp.bitwise_xor(me, k)
            dev = (peer // 4, (peer // 2) % 2, peer % 2)
            rd = _remote(ins[0], outs[0].at[me], send_sem.at[k - 1], recv_sem.at[k - 1], dev)
            starts.append(rd)
            waits.append(rd.wait_send)
            waits.append(_remote(ins[0], outs[0].at[peer], send_sem.at[k - 1], recv_sem.at[k - 1], dev).wait_recv)
        return starts, waits

    return _Rider([partial], [jax.ShapeDtypeStruct((8, SMALL_ROWS, D), F32)],
                  [pltpu.SemaphoreType.DMA, pltpu.SemaphoreType.DMA((7,)), pltpu.SemaphoreType.DMA((7,))], build)


def _small_update(blocks, place, vecs, conv, meta):
    nvec = len(VEC_NAMES)
    qcols = D // 4

    def in_order(ref):
        total = ref[0]
        for d in range(1, 8):
            total = total + ref[d]
        return total

    def body(pr, blocks_ref, cols_ref, *refs):
        vec_refs = refs[:3 * nvec]
        conv_refs = refs[3 * nvec:3 * nvec + 3]
        meta_refs = refs[3 * nvec + 3:3 * nvec + 6]
        outs = refs[3 * nvec + 6:]
        loss_ref, vec_out, conv_out, meta_out = outs[0], outs[1:5], outs[5:9], outs[9:13]
        tot, col = in_order(blocks_ref), in_order(cols_ref)
        loss_ref[...] = jnp.sum(tot[0:1, :], axis=1, keepdims=True)
        for o in vec_out:
            o[...] = jnp.zeros_like(o)
        for j, name in enumerate(VEC_NAMES):
            w, m, v = (r[...] for r in vec_refs[3 * j:3 * j + 3])
            g = tot[VEC_ROWS[name]:VEC_ROWS[name] + 1, :]
            if name == "lru_lambda":
                g = -g / (1.0 + jnp.exp(w))
            for o, val in zip(vec_out, (g,) + _adamw_math(w, g, m, v)):
                o[j:j + 1, :] = val
        for row, n_rows, ins, group in ((CONV_W_ROW, 4, conv_refs, conv_out), (META_ROW, N_META, meta_refs, meta_out)):
            g = col[row:row + n_rows, :]
            w, m, v = (r[...] for r in ins)
            for o, val in zip(group, (g,) + _adamw_math(w, g, m, v)):
                o[...] = val

    whole = lambda shape: pl.BlockSpec(shape, lambda i, pr: (0,) * len(shape))
    in_specs = [whole((8, SMALL_ROWS, D)), pl.BlockSpec((8, SMALL_ROWS, qcols), lambda i, pr: (0, 0, pr[0]))]
    in_specs += [whole((1, D))] * (3 * nvec) + [whole((4, qcols))] * 3 + [whole((N_META, qcols))] * 3
    out_shapes = [(1, 1)] + [(8, D)] * 4 + [(4, qcols)] * 4 + [(N_META, qcols)] * 4
    return pl.pallas_call(
        body,
        name="small_update",
        grid_spec=pltpu.PrefetchScalarGridSpec(num_scalar_prefetch=1, grid=(1,), in_specs=in_specs,
                                               out_specs=[whole(s) for s in out_shapes]),
        out_shape=[jax.ShapeDtypeStruct(s, F32) for s in out_shapes],
        compiler_params=_params(("arbitrary",)),
    )(place, blocks, blocks, *[a for t in vecs for a in t], *conv, *meta)


WEIGHT_ORDER = ["meta_tokens", "mix_norm_w", "w_in", "conv_w", "conv_b", "lru_wa", "lru_ba", "lru_wx", "lru_bx", "lru_lambda",
                "w_branch_ret", "w_branch_lru", "w_out", "ffn_norm_w", "w_ffn_in", "w_ffn_out", "final_norm_w"]


def kernel(x, meta_tokens, mix_norm_w, w_in, conv_w, conv_b, lru_wa, lru_ba, lru_wx, lru_bx, lru_lambda, w_branch_ret, w_branch_lru, w_out, ffn_norm_w, w_ffn_in, w_ffn_out, final_norm_w, loss_target, m_meta_tokens, m_mix_norm_w, m_w_in, m_conv_w, m_conv_b, m_lru_wa, m_lru_ba, m_lru_wx, m_lru_bx, m_lru_lambda, m_w_branch_ret, m_w_branch_lru, m_w_out, m_ffn_norm_w, m_w_ffn_in, m_w_ffn_out, m_final_norm_w, v_meta_tokens, v_mix_norm_w, v_w_in, v_conv_w, v_conv_b, v_lru_wa, v_lru_ba, v_lru_wx, v_lru_bx, v_lru_lambda, v_w_branch_ret, v_w_branch_lru, v_w_out, v_ffn_norm_w, v_w_ffn_in, v_w_ffn_out, v_final_norm_w):
    args = locals()
    wts = {n: args[n] for n in WEIGHT_ORDER}
    mom = {n: args["m_" + n] for n in WEIGHT_ORDER}
    var = {n: args["v_" + n] for n in WEIGHT_ORDER}
    place = jnp.stack([2 * lax.axis_index("x") + lax.axis_index("y"), lax.axis_index("c")]).astype(jnp.int32)

    shards = {n: wts[n][0].astype(BF16) for n in BIG_PIECES}
    shards["meta_tokens"] = wts["meta_tokens"]
    shards["conv_w"] = wts["conv_w"][0]
    plan = _CommPlan(shards, place)
    (first,) = _run_riders("gather_first", [_gather_rider(shards, FIRST_WEIGHTS)])
    w = dict(zip(FIRST_WEIGHTS, first))
    for n in VEC_NAMES:
        w[n] = wts[n].reshape(1, D)
    chips = jnp.bitwise_xor(place[0], jnp.arange(4, dtype=jnp.int32))
    w["route"] = jnp.concatenate([place, jnp.stack([2 * chips, 2 * chips + 1], axis=1).reshape(8)])
    w["w_in_shard"] = shards["w_in"]

    grad_x, grad_head, _, stats = _local_step(x[0], loss_target[0], w, plan)
    partial = jnp.concatenate(stats + [grad_head[PAD_ROWS:]], axis=0)
    shard_grads, blocks = plan.finish(partial)

    out = {}
    for n in BIG_PIECES:
        shape2d = (-1, wts[n].shape[-1])
        res = _adamw(n, *[a.reshape(shape2d) for a in (shard_grads[n], wts[n], mom[n], var[n])])
        out[n] = [r.reshape(wts[n].shape) for r in res]

    vecs = [tuple(a[n].reshape(1, D) for a in (wts, mom, var)) for n in VEC_NAMES]
    conv = tuple(a["conv_w"][0] for a in (wts, mom, var))
    meta = tuple(a["meta_tokens"] for a in (wts, mom, var))
    res = _small_update(blocks, place, vecs, conv, meta)
    loss = res[0].reshape(())
    for j, n in enumerate(VEC_NAMES):
        out[n] = [r[j].reshape(wts[n].shape) for r in res[1:5]]
    out["conv_w"] = [r.reshape(wts["conv_w"].shape) for r in res[5:9]]
    out["meta_tokens"] = list(res[9:13])

    return (loss, grad_x.reshape(x.shape)) + tuple(out[n][kind] for kind in range(4) for n in WEIGHT_ORDER)
```

```python
import functools
import math

import jax
import jax.numpy as jnp
from jax import lax
from jax.experimental import pallas as pl
from jax.experimental.pallas import tpu as pltpu

F32 = jnp.float32
BF16 = jnp.bfloat16

LANES = 128
BF16_ROWS = 16

D = 1024
HEADS = 8
DH = 128
CHUNK = 256
N_META = 16
FRONT = 256
PAD_ROWS = FRONT - N_META
LRU_BLOCKS = 4
LRU_BLOCK = 256
LRU_C = 8.0
FFN = 2816
FFN_HALF = FFN // 2
IN_COLS = 8 * D
ROPE_BASE = 10000.0
EPS = 1e-6
QK_SCALE = DH ** -0.5

ADAM_LR = 0.001
ADAM_B1 = 0.9
ADAM_B2 = 0.999
ADAM_EPS = 1e-08
ADAM_WD = 0.01
ADAM_STEP = 10

TM = 256
TM_HEAVY = 768
MXU_TILE = 256
FFN_SUB = 2 * MXU_TILE
_FFN_SUBS = [(a, min(a + FFN_SUB, FFN)) for a in range(0, FFN, FFN_SUB)]
TM_MIX_BWD = 256
TM_LRU = 384
TM_MIX_FWD = 384
TM_IN_PROJ = 1408
VMEM_LIMIT = 60 * 1024 * 1024
TN_VMEM_BUDGET = 40 * 1024 * 1024

NT_DIMS = (((1,), (1,)), ((), ()))
TN_DIMS = (((0,), (0,)), ((), ()))
MESH = pl.DeviceIdType.MESH


def _params(sem=None):
    if sem is None:
        return pltpu.CompilerParams(vmem_limit_bytes=VMEM_LIMIT)
    return pltpu.CompilerParams(dimension_semantics=sem, vmem_limit_bytes=VMEM_LIMIT)


def _dot(a, b):
    return jnp.dot(a, b, preferred_element_type=F32)


def _dot_nt(a, b):
    return lax.dot_general(a, b, NT_DIMS, preferred_element_type=F32)


def _dot_tn(a, b):
    return lax.dot_general(a, b, TN_DIMS, preferred_element_type=F32)


def _sigmoid(z):
    return 1.0 / (1.0 + jnp.exp(-z))


def _log1p(x):
    return jnp.where(x < 1e-3, x * (1.0 - x * (0.5 - x * (1.0 / 3.0))), jnp.log(1.0 + x))


def _softplus(x):
    return jnp.maximum(x, 0.0) + _log1p(jnp.exp(-jnp.abs(x)))


def _one_minus_square(a, log_a):
    x = 2.0 * log_a
    series = -x * (1.0 + x * (0.5 + x * (1.0 / 6.0)))
    return jnp.where(x > -0.02, series, 1.0 - a * a)


_GELU_K = math.sqrt(2.0 / math.pi)


def _gelu_and_grad(x):
    inner = _GELU_K * (x + 0.044715 * x * x * x)
    t = jnp.tanh(inner)
    val = 0.5 * x * (1.0 + t)
    grad = 0.5 * (1.0 + t) + 0.5 * x * (1.0 - t * t) * _GELU_K * (1.0 + 3.0 * 0.044715 * x * x)
    return val, grad


def _row_ids(i, rows, shape):
    return i * rows + lax.broadcasted_iota(jnp.int32, shape, 0)


def _rope_tables(rows):
    inv_freq = ROPE_BASE ** (-jnp.arange(0, DH, 2, dtype=F32) / DH)
    block_pos = jnp.arange(rows // FRONT, dtype=jnp.int32) * FRONT - PAD_ROWS
    coarse = block_pos.astype(F32)[:, None] * inv_freq[None, :]
    fine = jnp.arange(FRONT, dtype=F32)[:, None] * inv_freq[None, :]
    ca, sa = jnp.cos(coarse)[:, None, :], jnp.sin(coarse)[:, None, :]
    cb, sb = jnp.cos(fine)[None, :, :], jnp.sin(fine)[None, :, :]
    cos = (ca * cb - sa * sb).reshape(rows, DH // 2)
    sin = (sa * cb + ca * sb).reshape(rows, DH // 2)
    return jnp.concatenate([cos, cos], axis=1), jnp.concatenate([-sin, sin], axis=1)


def _decay_tables():
    log_g = jnp.log(1.0 - 2.0 ** (-5.0 - jnp.arange(HEADS, dtype=F32)))
    idx = jnp.arange(CHUNK, dtype=F32)
    diff = idx[:, None] - idx[None, :]
    intra = jnp.where(diff[None] >= 0, jnp.exp(jnp.maximum(diff, 0.0)[None] * log_g[:, None, None]), 0.0)
    q_decay = jnp.exp((idx + 1.0)[:, None] * log_g[None, :])
    k_decay = jnp.exp((CHUNK - 1.0 - idx)[:, None] * log_g[None, :])
    chunk_decay = jnp.exp(CHUNK * log_g)
    wide = lambda a: jnp.repeat(a, DH, axis=-1)
    return intra, jnp.swapaxes(intra, 1, 2), wide(q_decay), wide(k_decay), wide(chunk_decay[None, :])


def _norm1(head, x2d, norm_w, riders=()):
    rows = FRONT + x2d.shape[0]
    tm = TM_IN_PROJ if rows % TM_IN_PROJ == 0 else TM_MIX_FWD
    nt = rows // tm

    def body(head_hbm, x_hbm, nw_ref, u_ref, rstd_ref, h0_sc, h0_sem):
        slot = _frame_rows(head_hbm, x_hbm, h0_sc, h0_sem, pl.program_id(0), tm, nt)
        hv = h0_sc[slot]
        rs = lax.rsqrt(jnp.mean(hv * hv, axis=-1, keepdims=True) + EPS)
        u_ref[...] = ((hv * rs) * nw_ref[...]).astype(BF16)
        rstd_ref[...] = rs

    return _hosted_call(
        body,
        name="norm1",
        grid=(nt,),
        in_specs=[_ANY, _ANY, pl.BlockSpec((1, D), lambda i: (0, 0))],
        out_specs=[pl.BlockSpec((tm, D), lambda i: (i, 0)), pl.BlockSpec((tm, 1), lambda i: (i, 0))],
        out_shape=[jax.ShapeDtypeStruct((rows, D), BF16), jax.ShapeDtypeStruct((rows, 1), F32)],
        scratch_shapes=[pltpu.VMEM((2, tm, D), F32), pltpu.SemaphoreType.DMA((3,))],
        args=(head, x2d, norm_w),
        riders=riders,
    )


def _heavy_tile(rows):
    return TM_HEAVY if rows % TM_HEAVY == 0 else TM


def _lru_tile(rows):
    return TM_LRU if rows % TM_LRU == 0 else TM


def _swap_3_4(group):
    return jnp.where(group == 3, 4, jnp.where(group == 4, 3, group))


LRU_IN_COL = 3
GATES_COL = 1


def _in_proj(u, w_shard, route, cos_t, sin_t, riders=()):
    rows = u.shape[0]
    tm = TM_IN_PROJ if rows % TM_IN_PROJ == 0 else _heavy_tile(rows)
    nt = rows // tm
    kind, shard = BIG_PIECES["w_in"]

    def body(route_ref, u_hbm, wsh_ref, cos_ref, sin_ref, proj_ref, wfull_ref,
             u_sc, w_sc, u_sem, w_sem, local_sem, ici_send, ici_recv, fwd_send, fwd_recv):
        g, i = pl.program_id(0), pl.program_id(1)
        s, c = route_ref[0], route_ref[1]
        gid = route_ref[2 + g]
        sibling = (s // 2, s % 2, 1 - c)
        local = pltpu.make_async_copy(wsh_ref, _full_region(wfull_ref, kind, shard, s, None), local_sem)
        u_copies = [pltpu.make_async_copy(u_hbm.at[pl.ds(t * tm, tm)], u_sc.at[t], u_sem.at[t]) for t in range(nt)]
        sends, arrivals = [], []
        for k in (1, 2, 3):
            s2, dev = _other_chip(s, c, k)
            sends.append(_remote(_shard_region(wsh_ref, shard, c), _full_region(wfull_ref, kind, shard, s, c),
                                 ici_send.at[k - 1], ici_recv.at[k - 1], dev))
            mine = _full_region(wfull_ref, kind, shard, s2, c)
            theirs = _full_region(wfull_ref, kind, shard, s2, 1 - c)
            arrivals.append((_remote(mine, mine, ici_send.at[k - 1], ici_recv.at[k - 1], dev),
                             _remote(mine, mine, fwd_send.at[k - 1], fwd_recv.at[k - 1], sibling),
                             _remote(theirs, theirs, fwd_send.at[k - 1], fwd_recv.at[k - 1], sibling)))

        slot = g % 2
        last_tile = i == nt - 1

        def block_copy(src, col, to_slot):
            return pltpu.make_async_copy(src.at[:, pl.ds(pl.multiple_of(col * D, D), D)], w_sc.at[to_slot],
                                         w_sem.at[to_slot])

        @pl.when(jnp.logical_and(g == 0, i == 0))
        def _():
            for cp in sends:
                cp.start()
            local.start()
            for cp in u_copies:
                cp.start()
            cp = block_copy(wsh_ref, 0, 0)
            cp.start()
            cp.wait()

        @pl.when(jnp.logical_and(last_tile, g == 0))
        def _():
            block_copy(wsh_ref, 1, 1).start()

        for k, (arrived, forward, forwarded) in zip((1, 2, 3), arrivals):
            @pl.when(jnp.logical_and(last_tile, g == 2 * k - 1))
            def _():
                arrived.wait_recv()
                forward.start()
                forwarded.wait_recv()
                block_copy(wfull_ref, route_ref[2 + 2 * k], 0).start()

            @pl.when(jnp.logical_and(last_tile, g == 2 * k))
            def _():
                block_copy(wfull_ref, route_ref[3 + 2 * k], 1).start()

        @pl.when(jnp.logical_and(i == 0, g > 0))
        def _():
            block_copy(wfull_ref, 0, slot).wait()

        for t in range(nt):
            @pl.when(jnp.logical_and(g == 0, i == t))
            def _():
                u_copies[t].wait()

        acc = _dot(u_sc[i], w_sc[slot])

        @pl.when(gid < 2)
        def _():
            scale = jnp.where(gid == 1, QK_SCALE, 1.0).astype(F32)
            for h in range(HEADS):
                sl = slice(h * DH, (h + 1) * DH)
                blk = acc[:, sl]
                out = (blk * cos_ref[...] + pltpu.roll(blk, DH // 2, axis=1) * sin_ref[...]) * scale
                proj_ref[:, sl] = out.astype(BF16)

        @pl.when(gid >= 2)
        def _():
            proj_ref[...] = acc.astype(BF16)

        @pl.when(jnp.logical_and(g == 7, i == nt - 1))
        def _():
            local.wait()
            for cp in sends:
                cp.wait_send()
            for _, forward, _ in arrivals:
                forward.wait_send()

    return _hosted_call(
        body,
        name="in_proj",
        grid=(8, nt),
        in_specs=[
            _ANY, _ANY,
            pl.BlockSpec((tm, DH), lambda g, i, rt: (i, 0)),
            pl.BlockSpec((tm, DH), lambda g, i, rt: (i, 0)),
        ],
        out_specs=[pl.BlockSpec((tm, D), lambda g, i, rt: (i, _swap_3_4(rt[2 + g]))), _ANY],
        out_shape=[jax.ShapeDtypeStruct((rows, IN_COLS), BF16), jax.ShapeDtypeStruct((D, IN_COLS), BF16)],
        scratch_shapes=[
            pltpu.VMEM((nt, tm, D), BF16), pltpu.VMEM((2, D, D), BF16),
            pltpu.SemaphoreType.DMA((nt,)), pltpu.SemaphoreType.DMA((2,)), pltpu.SemaphoreType.DMA,
            pltpu.SemaphoreType.DMA((3,)), pltpu.SemaphoreType.DMA((3,)),
            pltpu.SemaphoreType.DMA((3,)), pltpu.SemaphoreType.DMA((3,)),
        ],
        args=(u, w_shard, cos_t, sin_t),
        riders=riders,
        prefetch=route,
        riders_after_body=True,
    )


def _retention_fwd(proj_bf, intra, q_dec, k_dec, c_dec, riders=()):
    rows = proj_bf.shape[0]
    nc = rows // CHUNK

    def body(q_ref, k_ref, v_ref, m_ref, qd_ref, kd_ref, cd_ref, o_ref, sprev_ref, s_sc):
        @pl.when(pl.program_id(0) == 0)
        def _():
            s_sc[...] = jnp.zeros_like(s_sc)

        for h in range(HEADS):
            sl = slice(h * DH, (h + 1) * DH)
            q, k, v = q_ref[:, sl], k_ref[:, sl], v_ref[:, sl]
            state = s_sc[h]
            state_b = state.astype(BF16)
            sprev_ref[0, h] = state_b
            s = _dot_nt(q, k) * m_ref[h]
            inner = _dot(s.astype(BF16), v)
            cross = _dot(q, state_b) * qd_ref[:, sl]
            o_ref[:, sl] = (inner + cross).astype(BF16)
            k_scaled = (k.astype(F32) * kd_ref[:, sl]).astype(BF16)
            s_sc[h] = state * cd_ref[:, sl] + _dot_tn(k_scaled, v)

    chunk_spec = lambda col: pl.BlockSpec((CHUNK, D), lambda c: (c, col))
    const2 = lambda shape: pl.BlockSpec(shape, lambda c: (0, 0))
    return _hosted_call(
        body,
        name="retention_fwd",
        grid=(nc,),
        in_specs=[
            chunk_spec(0), chunk_spec(1), chunk_spec(2),
            pl.BlockSpec((HEADS, CHUNK, CHUNK), lambda c: (0, 0, 0)),
            const2((CHUNK, D)), const2((CHUNK, D)), const2((1, D)),
        ],
        out_specs=[
            pl.BlockSpec((CHUNK, D), lambda c: (c, 0)),
            pl.BlockSpec((1, HEADS, DH, DH), lambda c: (c, 0, 0, 0)),
        ],
        out_shape=[
            jax.ShapeDtypeStruct((rows, D), BF16),
            jax.ShapeDtypeStruct((nc, HEADS, DH, DH), BF16),
        ],
        scratch_shapes=[pltpu.VMEM((HEADS, DH, DH), F32)],
        args=(proj_bf, proj_bf, proj_bf, intra, q_dec, k_dec, c_dec),
        riders=riders,
    )


def _shift_down(x, first8_prev, d):
    rolled = pltpu.roll(x, d, axis=0)
    head = pltpu.roll(jnp.concatenate([first8_prev, x[0:8]], axis=0), d, axis=0)[8:16]
    return rolled, head


def _conv_and_gates(x, prev8, cw_ref, cb_ref, wa_ref, wx_ref, ba_ref, bx_ref, lam_ref, c_sc):
    cw = cw_ref[...]
    conv = cb_ref[...] + cw[3:4] * x
    head = cb_ref[...] + cw[3:4] * x[0:8]
    for d in (1, 2, 3):
        rolled, hd = _shift_down(x, prev8, d)
        conv = conv + cw[3 - d:4 - d] * rolled
        head = head + cw[3 - d:4 - d] * hd
    c_sc[...] = conv
    c_sc[0:8, :] = head
    c = c_sc[...]
    zr, zi = [], []
    for g in range(LRU_BLOCKS):
        sl = slice(g * LRU_BLOCK, (g + 1) * LRU_BLOCK)
        cg = c[:, sl].astype(BF16)
        zr.append(_dot(cg, wa_ref[g]))
        zi.append(_dot(cg, wx_ref[g]))
    r = _sigmoid(jnp.concatenate(zr, axis=1) + ba_ref[...])
    gate_i = _sigmoid(jnp.concatenate(zi, axis=1) + bx_ref[...])
    sp = _softplus(-lam_ref[...])
    log_a = (-LRU_C) * r * sp
    a = jnp.exp(log_a)
    mult = jnp.sqrt(_one_minus_square(a, log_a))
    return c, r, gate_i, a, mult, log_a


def _lru_fwd(proj, conv_w, conv_b, wa, wx, ba, bx, lam, riders=()):
    rows = proj.shape[0]
    TM = _lru_tile(rows)
    nt = rows // TM

    def body(x_ref, cw_ref, cb_ref, wa_ref, wx_ref, ba_ref, bx_ref, lam_ref,
             h_ref, c_ref, r_ref, i_ref, la_ref, mult_ref, prev_sc, carry_sc, c_sc, a_sc, u_sc, h_sc):
        i = pl.program_id(0)

        @pl.when(i == 0)
        def _():
            prev_sc[...] = jnp.zeros_like(prev_sc)
            carry_sc[...] = jnp.zeros_like(carry_sc)

        x = x_ref[...].astype(F32)
        c, r, gate_i, a, mult, log_a = _conv_and_gates(x, prev_sc[...], cw_ref, cb_ref, wa_ref, wx_ref, ba_ref, bx_ref,
                                                       lam_ref, c_sc)
        for ref, val in ((c_ref, c), (r_ref, r), (i_ref, gate_i), (la_ref, log_a), (mult_ref, mult)):
            ref[...] = val.astype(BF16)
        prev_sc[...] = x[TM - 8:TM]
        valid = _row_ids(i, TM, (TM, D)) >= PAD_ROWS
        a_sc[...] = a
        u_sc[...] = jnp.where(valid, mult * gate_i * c, 0.0)
        row8 = lax.broadcasted_iota(jnp.int32, (8, D), 0)

        def group(gi, hprev):
            r0 = pl.multiple_of(gi * 8, 8)
            aa = a_sc[pl.ds(r0, 8), :]
            uu = u_sc[pl.ds(r0, 8), :]
            for d in (1, 2, 4):
                a_sh = jnp.where(row8 >= d, pltpu.roll(aa, d, axis=0), 1.0)
                u_sh = jnp.where(row8 >= d, pltpu.roll(uu, d, axis=0), 0.0)
                uu = uu + aa * u_sh
                aa = aa * a_sh
            hb = aa * hprev + uu
            h_sc[pl.ds(r0, 8), :] = hb
            return hb[7:8, :]

        hlast = lax.fori_loop(0, TM // 8, group, carry_sc[0:1, :])
        carry_sc[0:1, :] = hlast
        h_ref[...] = h_sc[...].astype(BF16)

    vec = pl.BlockSpec((1, D), lambda i: (0, 0))
    wspec = pl.BlockSpec((LRU_BLOCKS, LRU_BLOCK, LRU_BLOCK), lambda i: (0, 0, 0))
    return _hosted_call(
        body,
        name="lru_fwd",
        grid=(nt,),
        in_specs=[
            pl.BlockSpec((TM, D), lambda i: (i, LRU_IN_COL)),
            pl.BlockSpec((4, D), lambda i: (0, 0)),
            vec, wspec, wspec, vec, vec, vec,
        ],
        out_specs=[pl.BlockSpec((TM, D), lambda i: (i, 0)) for _ in range(6)],
        out_shape=[jax.ShapeDtypeStruct((rows, D), BF16) for _ in range(6)],
        scratch_shapes=[
            pltpu.VMEM((8, D), F32), pltpu.VMEM((8, D), F32),
            pltpu.VMEM((TM, D), F32), pltpu.VMEM((TM, D), F32), pltpu.VMEM((TM, D), F32), pltpu.VMEM((TM, D), F32),
        ],
        args=(proj, conv_w, conv_b, wa, wx, ba, bx, lam),
        riders=riders,
    )


def _group_norm(o):
    outs, rstds = [], []
    for h in range(HEADS):
        oh = o[:, h * DH:(h + 1) * DH]
        rs = lax.rsqrt(jnp.mean(oh * oh, axis=-1, keepdims=True) + EPS)
        outs.append(oh * rs)
        rstds.append(rs)
    return jnp.concatenate(outs, axis=1), rstds


def _frame_rows(head_hbm, x_hbm, buf, sems, i, tm, nt):
    slot = i % 2

    @pl.when(i == 0)
    def _():
        first = [pltpu.make_async_copy(head_hbm, buf.at[0, pl.ds(0, FRONT)], sems.at[2]),
                 pltpu.make_async_copy(x_hbm.at[pl.ds(0, tm - FRONT)], buf.at[0, pl.ds(FRONT, tm - FRONT)], sems.at[0])]
        for cp in first:
            cp.start()
        for cp in first:
            cp.wait()

    @pl.when(i + 1 < nt)
    def _():
        start = pl.multiple_of((i + 1) * tm - FRONT, LANES)
        pltpu.make_async_copy(x_hbm.at[pl.ds(start, tm)], buf.at[1 - slot], sems.at[1 - slot]).start()

    @pl.when(i > 0)
    def _():
        pltpu.make_async_copy(x_hbm.at[pl.ds(0, tm)], buf.at[slot], sems.at[slot]).wait()

    return slot


def _mix_fwd(head, x2d, o, proj, h_lru, w_br, w_bl, w_o, ffn_norm_w, riders=()):
    rows = o.shape[0]
    tm = TM_MIX_FWD
    assert rows % tm == 0 and tm > FRONT
    nt = rows // tm

    def body(head_hbm, x_hbm, o_ref, gates_ref, hl_ref, wbr_ref, wbl_ref, wo_ref, nw_ref,
             yret_ref, ylru_ref, h1_ref, u2_ref, rstd_ref, h0_sc, h0_sem):
        i = pl.program_id(0)
        slot = _frame_rows(head_hbm, x_hbm, h0_sc, h0_sem, i, tm, nt)
        gate = lambda j: gates_ref[:, j * D:(j + 1) * D].astype(F32)
        on, _ = _group_norm(o_ref[...].astype(F32))
        gret = gate(0)
        a_ret = (gret * _sigmoid(gret) * on).astype(BF16)
        y_ret = _dot(a_ret, wbr_ref[...])
        gl, _ = _gelu_and_grad(gate(1))
        a_lru = (gl * hl_ref[...].astype(F32)).astype(BF16)
        y_lru = _dot(a_lru, wbl_ref[...])
        mixed = (_sigmoid(gate(2)) * y_ret + _sigmoid(gate(3)) * y_lru).astype(BF16)
        delta = _dot(mixed, wo_ref[...])
        yret_ref[...] = y_ret.astype(BF16)
        ylru_ref[...] = y_lru.astype(BF16)
        h1 = h0_sc[slot] + delta
        rs = lax.rsqrt(jnp.mean(h1 * h1, axis=-1, keepdims=True) + EPS)
        h1_ref[...] = h1
        u2_ref[...] = ((h1 * rs) * nw_ref[...]).astype(BF16)
        rstd_ref[...] = rs

    tile = lambda col: pl.BlockSpec((tm, D), lambda i: (i, col))
    wspec = pl.BlockSpec((D, D), lambda i: (0, 0))
    return _hosted_call(
        body,
        name="mix_fwd",
        grid=(nt,),
        in_specs=[
            _ANY, _ANY,
            tile(0), pl.BlockSpec((tm, 4 * D), lambda i: (i, GATES_COL)), tile(0),
            wspec, wspec, wspec,
            pl.BlockSpec((1, D), lambda i: (0, 0)),
        ],
        out_specs=[tile(0), tile(0), tile(0), tile(0), pl.BlockSpec((tm, 1), lambda i: (i, 0))],
        out_shape=[
            jax.ShapeDtypeStruct((rows, D), BF16), jax.ShapeDtypeStruct((rows, D), BF16),
            jax.ShapeDtypeStruct((rows, D), F32), jax.ShapeDtypeStruct((rows, D), BF16),
            jax.ShapeDtypeStruct((rows, 1), F32),
        ],
        scratch_shapes=[pltpu.VMEM((2, tm, D), F32), pltpu.SemaphoreType.DMA((3,))],
        args=(head, x2d, o, proj, h_lru, w_br, w_bl, w_o, ffn_norm_w),
        riders=riders,
    )


def _ffn_fwd_gate(u2, w_ffn_in, riders=()):
    rows = u2.shape[0]
    tm = _heavy_tile(rows)
    hid = lambda: pl.BlockSpec((tm, FFN), lambda i: (i, 0))

    def body(u2_ref, w_hbm, silu_ref, dsilu_ref, act_ref, w_sc, w_sem):
        @pl.when(pl.program_id(0) == 0)
        def _():
            cp = pltpu.make_async_copy(w_hbm, w_sc, w_sem)
            cp.start()
            cp.wait()

        u2 = u2_ref[...]
        for a, b in _FFN_SUBS:
            g = _dot(u2, w_sc[:, a:b])
            up = _dot(u2, w_sc[:, FFN + a:FFN + b])
            sg = _sigmoid(g)
            silu = g * sg
            silu_ref[:, a:b] = silu.astype(BF16)
            dsilu_ref[:, a:b] = (up * (sg * (1.0 + g * (1.0 - sg)))).astype(BF16)
            act_ref[:, a:b] = (silu * up).astype(BF16)

    return _hosted_call(
        body,
        name="ffn_fwd_gate",
        grid=(rows // tm,),
        in_specs=[pl.BlockSpec((tm, D), lambda i: (i, 0)), _ANY],
        out_specs=[hid(), hid(), hid()],
        out_shape=[jax.ShapeDtypeStruct((rows, FFN), BF16)] * 3,
        scratch_shapes=[pltpu.VMEM((D, 2 * FFN), BF16), pltpu.SemaphoreType.DMA],
        args=(u2, w_ffn_in),
        riders=riders,
    )


def _ffn_out_loss(act, h1, w_ffn_out, target, final_norm_w):
    rows = act.shape[0]
    tm = _heavy_tile(rows)
    nt = rows // tm

    def body(act_ref, h1_ref, wo_ref, fnw_ref, tgt_hbm, dh2_ref, stats_ref, tgt_sc, tgt_sem):
        i = pl.program_id(0)
        slot = i % 2

        @pl.when(i == 0)
        def _():
            stats_ref[...] = jnp.zeros_like(stats_ref)
            tgt_sc[0, 0:FRONT, :] = jnp.zeros((FRONT, D), F32)
            cp = pltpu.make_async_copy(tgt_hbm.at[pl.ds(0, tm - FRONT)], tgt_sc.at[0, pl.ds(FRONT, tm - FRONT)], tgt_sem.at[0])
            cp.start()
            cp.wait()

        @pl.when(i + 1 < nt)
        def _():
            start = pl.multiple_of((i + 1) * tm - FRONT, FRONT)
            pltpu.make_async_copy(tgt_hbm.at[pl.ds(start, tm)], tgt_sc.at[1 - slot], tgt_sem.at[1 - slot]).start()

        @pl.when(i > 0)
        def _():
            pltpu.make_async_copy(tgt_hbm.at[pl.ds(0, tm)], tgt_sc.at[slot], tgt_sem.at[slot]).wait()

        h2 = h1_ref[...] + _dot(act_ref[...], wo_ref[...])
        rs = lax.rsqrt(jnp.mean(h2 * h2, axis=-1, keepdims=True) + EPS)
        n = h2 * rs
        fnw = fnw_ref[...]
        valid = _row_ids(i, tm, (tm, D)) >= FRONT
        diff = jnp.where(valid, n * fnw - tgt_sc[slot], 0.0)
        dy = diff * (1.0 / D)
        stats_ref[0:1, :] += (0.5 / D) * jnp.sum(diff * diff, axis=0, keepdims=True)
        stats_ref[1:2, :] += jnp.sum(dy * n, axis=0, keepdims=True)
        dn = dy * fnw
        dh2_ref[...] = rs * (dn - n * jnp.mean(dn * n, axis=-1, keepdims=True))

    return pl.pallas_call(
        body,
        name="ffn_out_loss",
        grid=(nt,),
        in_specs=[
            pl.BlockSpec((tm, FFN), lambda i: (i, 0)),
            pl.BlockSpec((tm, D), lambda i: (i, 0)),
            pl.BlockSpec((FFN, D), lambda i: (0, 0)),
            pl.BlockSpec((1, D), lambda i: (0, 0)),
            _ANY,
        ],
        out_specs=[pl.BlockSpec((tm, D), lambda i: (i, 0)), pl.BlockSpec((8, D), lambda i: (0, 0))],
        out_shape=[jax.ShapeDtypeStruct((rows, D), F32), jax.ShapeDtypeStruct((8, D), F32)],
        scratch_shapes=[pltpu.VMEM((2, tm, D), F32), pltpu.SemaphoreType.DMA((2,))],
        compiler_params=_params(("arbitrary",)),
    )(act, h1, w_ffn_out, final_norm_w, target)


def _ffn_bwd(dh2, silu, dsilu, h1, rstd2, w_ffn_in, w_ffn_out, ffn_norm_w):
    rows = dh2.shape[0]
    tm = _heavy_tile(rows)
    blk = lambda: pl.BlockSpec((tm, FFN), lambda i: (i, 0))

    def gate_body(dh2_ref, silu_ref, dsilu_ref, wo_hbm, dg_ref, dup_ref, dh2b_ref, wo_sc, wo_sem):
        @pl.when(pl.program_id(0) == 0)
        def _():
            cp = pltpu.make_async_copy(wo_hbm, wo_sc, wo_sem)
            cp.start()
            cp.wait()

        dh2b = dh2_ref[...].astype(BF16)
        dh2b_ref[...] = dh2b
        for a, b in _FFN_SUBS:
            dact = _dot_nt(dh2b, wo_sc[a:b, :])
            dup_ref[:, a:b] = (dact * silu_ref[:, a:b].astype(F32)).astype(BF16)
            dg_ref[:, a:b] = (dact * dsilu_ref[:, a:b].astype(F32)).astype(BF16)

    dg, dup, dh2b = pl.pallas_call(
        gate_body,
        name="ffn_bwd_gate",
        grid=(rows // tm,),
        in_specs=[pl.BlockSpec((tm, D), lambda i: (i, 0)), blk(), blk(), _ANY],
        out_specs=[blk(), blk(), pl.BlockSpec((tm, D), lambda i: (i, 0))],
        out_shape=[jax.ShapeDtypeStruct((rows, FFN), BF16)] * 2 + [jax.ShapeDtypeStruct((rows, D), BF16)],
        scratch_shapes=[pltpu.VMEM((FFN, D), BF16), pltpu.SemaphoreType.DMA],
        compiler_params=_params(("arbitrary",)),
    )(dh2, silu, dsilu, w_ffn_out)

    def body(dg_ref, dup_ref, dh2_ref, h1_ref, rstd_ref, w_hbm, nw_ref, dh1_ref, stats_ref, w_sc, w_sem):
        @pl.when(pl.program_id(0) == 0)
        def _():
            stats_ref[...] = jnp.zeros_like(stats_ref)
            cp = pltpu.make_async_copy(w_hbm, w_sc, w_sem)
            cp.start()
            cp.wait()

        du = _dot_nt(dg_ref[...], w_sc[:, 0:FFN]) + _dot_nt(dup_ref[...], w_sc[:, FFN:2 * FFN])
        rs = rstd_ref[...]
        n = h1_ref[...] * rs
        stats_ref[0:1, :] += jnp.sum(du * n, axis=0, keepdims=True)
        dn = du * nw_ref[...]
        dh1_ref[...] = dh2_ref[...] + rs * (dn - n * jnp.mean(dn * n, axis=-1, keepdims=True))

    row = lambda width: pl.BlockSpec((tm, width), lambda i: (i, 0))
    dh1, stats = pl.pallas_call(
        body,
        name="ffn_bwd_in",
        grid=(rows // tm,),
        in_specs=[row(FFN), row(FFN), row(D), row(D), row(1), _ANY, pl.BlockSpec((1, D), lambda i: (0, 0))],
        out_specs=[row(D), pl.BlockSpec((8, D), lambda i: (0, 0))],
        out_shape=[jax.ShapeDtypeStruct((rows, D), F32), jax.ShapeDtypeStruct((8, D), F32)],
        scratch_shapes=[pltpu.VMEM((D, 2 * FFN), BF16), pltpu.SemaphoreType.DMA],
        compiler_params=_params(("arbitrary",)),
    )(dg, dup, dh2, h1, rstd2, w_ffn_in, ffn_norm_w)
    return dg, dup, dh2b, dh1, stats


def _mix_bwd(dh1, o, proj, h_lru, y_ret, y_lru, w_br, w_bl, w_o, riders=()):
    rows = dh1.shape[0]
    tm = TM_MIX_BWD
    nt = rows // tm

    def body(dh1_ref, o_ref, gates_ref, hl_ref, yret_ref, ylru_ref, wbr_ref, wbl_ref, wo_ref,
             dproj_ref, do_ref, dhl_ref, mixed_ref, aret_ref, alru_ref, dyret_ref, dylru_ref):
        gate = lambda j: gates_ref[:, j * D:(j + 1) * D].astype(F32)
        dmixed = _dot_nt(dh1_ref[...].astype(BF16), wo_ref[...])
        y_ret, y_lru = yret_ref[...].astype(F32), ylru_ref[...].astype(F32)
        sa, sb = _sigmoid(gate(2)), _sigmoid(gate(3))
        mixed_ref[...] = (sa * y_ret + sb * y_lru).astype(BF16)
        dga = dmixed * y_ret * sa * (1.0 - sa)
        dgb = dmixed * y_lru * sb * (1.0 - sb)
        dy_ret = (dmixed * sa).astype(BF16)
        dy_lru = (dmixed * sb).astype(BF16)
        dyret_ref[...] = dy_ret
        dylru_ref[...] = dy_lru
        da_ret = _dot_nt(dy_ret, wbr_ref[...])
        da_lru = _dot_nt(dy_lru, wbl_ref[...])

        gret = gate(0)
        sg = _sigmoid(gret)
        silu = gret * sg
        on, rstds = _group_norm(o_ref[...].astype(F32))
        aret_ref[...] = (silu * on).astype(BF16)
        dgret = da_ret * on * (sg * (1.0 + gret * (1.0 - sg)))
        don = da_ret * silu
        for h in range(HEADS):
            sl = slice(h * DH, (h + 1) * DH)
            onh, donh = on[:, sl], don[:, sl]
            do_ref[:, sl] = (rstds[h] * (donh - onh * jnp.mean(donh * onh, axis=-1, keepdims=True))).astype(BF16)

        gl, gl_grad = _gelu_and_grad(gate(1))
        hl = hl_ref[...].astype(F32)
        alru_ref[...] = (gl * hl).astype(BF16)
        dlgate = da_lru * hl * gl_grad
        dhl_ref[...] = (da_lru * gl).astype(BF16)

        for j, val in enumerate((dgret, dlgate, dga, dgb)):
            dproj_ref[:, j * D:(j + 1) * D] = val.astype(BF16)

    tile = lambda col: pl.BlockSpec((tm, D), lambda i: (i, col))
    gates = pl.BlockSpec((tm, 4 * D), lambda i: (i, GATES_COL))
    wspec = pl.BlockSpec((D, D), lambda i: (0, 0))
    bf = lambda: jax.ShapeDtypeStruct((rows, D), BF16)
    return _hosted_call(
        body,
        name="mix_bwd",
        grid=(nt,),
        in_specs=[tile(0), tile(0), gates, tile(0), tile(0), tile(0), wspec, wspec, wspec],
        out_specs=[pl.BlockSpec((tm, 4 * D), lambda i: (i, GATES_COL))] + [tile(0)] * 7,
        out_shape=[jax.ShapeDtypeStruct((rows, IN_COLS), BF16)] + [bf() for _ in range(7)],
        scratch_shapes=[],
        args=(dh1, o, proj, h_lru, y_ret, y_lru, w_br, w_bl, w_o),
        riders=riders,
    )


def _retention_bwd(dproj, proj_bf, do, sprev, intra, intra_t, q_dec, k_dec, c_dec, cos_t, sin_t, riders=()):
    rows = proj_bf.shape[0]
    nc = rows // CHUNK

    def body(dproj_in_ref, q_ref, k_ref, v_ref, do_ref, sprev_ref, m_ref, mt_ref, qd_ref, kd_ref, cd_ref, cos_ref, sin_ref,
             dproj_ref, ds_sc):
        @pl.when(pl.program_id(0) == 0)
        def _():
            ds_sc[...] = jnp.zeros_like(ds_sc)

        cos, sin = cos_ref[...], sin_ref[...]

        def unrotate(dy):
            return dy * cos - pltpu.roll(dy, DH // 2, axis=1) * sin

        for h in range(HEADS):
            sl = slice(h * DH, (h + 1) * DH)
            q, k, v = q_ref[:, sl], k_ref[:, sl], v_ref[:, sl]
            do = do_ref[:, sl]
            dob = do.astype(BF16)
            doq = (do * qd_ref[:, sl]).astype(BF16)
            state_prev = sprev_ref[0, h]
            dstate = ds_sc[h]
            dstate_b = dstate.astype(BF16)
            s_t = (_dot_nt(k, q) * mt_ref[h]).astype(BF16)
            ds_t = (_dot_nt(v, dob) * mt_ref[h]).astype(BF16)
            ds = (_dot_nt(dob, v) * m_ref[h]).astype(BF16)
            kd = kd_ref[:, sl]
            dq = _dot(ds, k) + _dot_nt(doq, state_prev)
            dk = _dot(ds_t, q) + _dot_nt(v, dstate_b) * kd
            k_scaled = (k.astype(F32) * kd).astype(BF16)
            dv = _dot(s_t, dob) + _dot(k_scaled, dstate_b)
            ds_sc[h] = dstate * cd_ref[:, sl] + _dot_tn(q, doq)
            for part, val in enumerate((unrotate(dq), unrotate(dk) * QK_SCALE, dv)):
                dproj_ref[:, part * D + h * DH:part * D + (h + 1) * DH] = val.astype(BF16)

    rev = lambda c: nc - 1 - c
    chunk_spec = lambda col: pl.BlockSpec((CHUNK, D), lambda c: (rev(c), col))
    const2 = lambda shape: pl.BlockSpec(shape, lambda c: (0, 0))
    const3 = pl.BlockSpec((HEADS, CHUNK, CHUNK), lambda c: (0, 0, 0))
    return _hosted_call(
        body,
        name="retention_bwd",
        grid=(nc,),
        in_specs=[
            pl.BlockSpec(memory_space=pl.ANY),
            chunk_spec(0), chunk_spec(1), chunk_spec(2), chunk_spec(0),
            pl.BlockSpec((1, HEADS, DH, DH), lambda c: (rev(c), 0, 0, 0)),
            const3, const3,
            const2((CHUNK, D)), const2((CHUNK, D)), const2((1, D)),
            pl.BlockSpec((CHUNK, DH), lambda c: (rev(c), 0)),
            pl.BlockSpec((CHUNK, DH), lambda c: (rev(c), 0)),
        ],
        out_specs=[pl.BlockSpec((CHUNK, 3 * D), lambda c: (rev(c), 0))],
        out_shape=[jax.ShapeDtypeStruct(dproj.shape, BF16)],
        aliases={0: 0},
        scratch_shapes=[pltpu.VMEM((HEADS, DH, DH), F32)],
        args=(dproj, proj_bf, proj_bf, proj_bf, do, sprev, intra, intra_t, q_dec, k_dec, c_dec, cos_t, sin_t),
        riders=riders,
    )


def _lru_bwd(dproj, proj, saved, dhl, conv_w, wa, wx, lam, riders=()):
    rows = proj.shape[0]
    TM = _lru_tile(rows)
    nt = rows // TM
    per8 = TM // 8

    def body(dproj_in_ref, x_ref, xprev_ref, h_ref, hprev_ref, c_ref, r_ref, i_ref, la_ref, mult_ref, dhl_ref,
             cw_ref, wa_ref, wx_ref, lam_ref,
             dproj_ref, dwa_ref, dwx_ref, stats_ref, anext_sc, dhnext_sc, dcnext_sc, c_sc, b_sc, dh_sc):
        step = pl.program_id(0)
        i = nt - 1 - step

        @pl.when(step == 0)
        def _():
            anext_sc[...] = jnp.zeros_like(anext_sc)
            dhnext_sc[...] = jnp.zeros_like(dhnext_sc)
            dcnext_sc[...] = jnp.zeros_like(dcnext_sc)
            dwa_ref[...] = jnp.zeros_like(dwa_ref)
            dwx_ref[...] = jnp.zeros_like(dwx_ref)
            stats_ref[...] = jnp.zeros_like(stats_ref)

        first = i == 0
        x = x_ref[...].astype(F32)
        prev8 = jnp.where(first, 0.0, xprev_ref[8:16, :].astype(F32))
        c_b = c_ref[...]
        c, r, gate_i, mult = (ref[...].astype(F32) for ref in (c_ref, r_ref, i_ref, mult_ref))
        a = jnp.exp(la_ref[...].astype(F32))
        sp = _softplus(-lam_ref[...])
        dh_sc[...] = dhl_ref[...].astype(F32)

        b_sc[...] = pltpu.roll(a, TM - 1, axis=0)
        b_sc[TM - 1:TM, :] = anext_sc[0:1, :]
        anext_sc[0:1, :] = a[0:1, :]
        row8 = lax.broadcasted_iota(jnp.int32, (8, D), 0)

        def group(gi, dhnext):
            r0 = pl.multiple_of((per8 - 1 - gi) * 8, 8)
            bb = b_sc[pl.ds(r0, 8), :]
            uu = dh_sc[pl.ds(r0, 8), :]
            for d in (1, 2, 4):
                b_sh = jnp.where(row8 < 8 - d, pltpu.roll(bb, 8 - d, axis=0), 1.0)
                u_sh = jnp.where(row8 < 8 - d, pltpu.roll(uu, 8 - d, axis=0), 0.0)
                uu = uu + bb * u_sh
                bb = bb * b_sh
            dhb = bb * dhnext + uu
            dh_sc[pl.ds(r0, 8), :] = dhb
            return dhb[0:1, :]

        dhfirst = lax.fori_loop(0, per8, group, dhnext_sc[0:1, :])
        dhnext_sc[0:1, :] = dhfirst
        dh = dh_sc[...]

        h = h_ref[...].astype(F32)
        hprev8 = jnp.where(first, 0.0, hprev_ref[8:16, :].astype(F32))
        h_dn, h_head = _shift_down(h, hprev8, 1)
        c_sc[...] = h_dn
        c_sc[0:8, :] = h_head
        h_before = c_sc[...]

        valid = _row_ids(i, TM, (TM, D)) >= PAD_ROWS
        da = dh * h_before
        du = jnp.where(valid, dh, 0.0)
        dmult = du * gate_i * c
        dgate_i = du * mult * c
        dc = du * mult * gate_i
        dla = da * a - dmult * (a * a) / mult
        dla = jnp.where(valid, dla, 0.0)
        dr = dla * ((-LRU_C) * sp)
        dzr = dr * r * (1.0 - r)
        dzi = dgate_i * gate_i * (1.0 - gate_i)
        stats_ref[1:2, :] += jnp.sum(dzr, axis=0, keepdims=True)
        stats_ref[2:3, :] += jnp.sum(dzi, axis=0, keepdims=True)
        stats_ref[3:4, :] += jnp.sum(dla * ((-LRU_C) * r), axis=0, keepdims=True)
        dc_gate = []
        for g in range(LRU_BLOCKS):
            sl = slice(g * LRU_BLOCK, (g + 1) * LRU_BLOCK)
            cg = c_b[:, sl]
            dzr_g = dzr[:, sl].astype(BF16)
            dzi_g = dzi[:, sl].astype(BF16)
            dc_gate.append(_dot_nt(dzr_g, wa_ref[g]) + _dot_nt(dzi_g, wx_ref[g]))
            dwa_ref[g] += _dot_tn(cg, dzr_g)
            dwx_ref[g] += _dot_tn(cg, dzi_g)
        dc = dc + jnp.concatenate(dc_gate, axis=1)

        cw = cw_ref[...]
        stats_ref[0:1, :] += jnp.sum(dc, axis=0, keepdims=True)
        stats_ref[7:8, :] += jnp.sum(dc * x, axis=0, keepdims=True)
        dx = cw[3:4] * dc
        tail_src = jnp.concatenate([dc[TM - 8:TM], dcnext_sc[...]], axis=0)
        dx_tail = cw[3:4] * dc[TM - 8:TM]
        for d in (1, 2, 3):
            dx = dx + cw[3 - d:4 - d] * pltpu.roll(dc, TM - d, axis=0)
            dx_tail = dx_tail + cw[3 - d:4 - d] * pltpu.roll(tail_src, 16 - d, axis=0)[0:8]
            rolled, hd = _shift_down(x, prev8, d)
            b_sc[...] = rolled
            b_sc[0:8, :] = hd
            stats_ref[7 - d:8 - d, :] += jnp.sum(dc * b_sc[...], axis=0, keepdims=True)
        dcnext_sc[...] = dc[0:8]
        dproj_ref[...] = dx.astype(BF16)
        dproj_ref[TM - 8:TM, :] = dx_tail.astype(BF16)

    rev = lambda s: nt - 1 - s
    vec = pl.BlockSpec((1, D), lambda s: (0, 0))
    wspec = pl.BlockSpec((LRU_BLOCKS, LRU_BLOCK, LRU_BLOCK), lambda s: (0, 0, 0))
    prev16 = lambda col: pl.BlockSpec((BF16_ROWS, D), lambda s: (jnp.maximum(rev(s) * (TM // BF16_ROWS) - 1, 0), col))
    tile = lambda: pl.BlockSpec((TM, D), lambda s: (rev(s), 0))
    h_lru, c_sv, r_sv, i_sv, la_sv, mult_sv = saved
    return _hosted_call(
        body,
        name="lru_bwd",
        grid=(nt,),
        in_specs=[
            pl.BlockSpec(memory_space=pl.ANY),
            pl.BlockSpec((TM, D), lambda s: (rev(s), LRU_IN_COL)), prev16(LRU_IN_COL),
            tile(), prev16(0),
            tile(), tile(), tile(), tile(), tile(), tile(),
            pl.BlockSpec((4, D), lambda s: (0, 0)),
            wspec, wspec, vec,
        ],
        out_specs=[
            pl.BlockSpec((TM, D), lambda s: (rev(s), LRU_IN_COL)),
            wspec, wspec,
            pl.BlockSpec((8, D), lambda s: (0, 0)),
        ],
        out_shape=[
            jax.ShapeDtypeStruct(dproj.shape, BF16),
            jax.ShapeDtypeStruct((LRU_BLOCKS, LRU_BLOCK, LRU_BLOCK), F32),
            jax.ShapeDtypeStruct((LRU_BLOCKS, LRU_BLOCK, LRU_BLOCK), F32),
            jax.ShapeDtypeStruct((8, D), F32),
        ],
        aliases={0: 0},
        scratch_shapes=[
            pltpu.VMEM((8, D), F32), pltpu.VMEM((8, D), F32), pltpu.VMEM((8, D), F32),
            pltpu.VMEM((TM, D), F32), pltpu.VMEM((TM, D), F32), pltpu.VMEM((TM, D), F32),
        ],
        args=(dproj, proj, proj, h_lru, h_lru, c_sv, r_sv, i_sv, la_sv, mult_sv, dhl, conv_w, wa, wx, lam),
        riders=riders,
    )


def _in_proj_bwd(dproj, w_in, part, prev=None, riders=()):
    rows = dproj.shape[0]
    tm = _heavy_tile(rows)
    nt = rows // tm
    split = max(1, nt // 4 + 1) if nt > 1 else 1
    first = 0 if part == 0 else split
    count = split if part == 0 else nt - split

    def body(*refs):
        dproj_ref, w_hbm, du_ref, w_sc, w_sem = refs[-5:]

        @pl.when(pl.program_id(0) == 0)
        def _():
            moves = [((0, 3), 0), ((4, 1), 3), ((3, 1), 4), ((5, 3), 5)]
            copies = [pltpu.make_async_copy(w_hbm.at[:, pl.ds(src * D, n * D)], w_sc.at[:, pl.ds(dst * D, n * D)], w_sem.at[q])
                      for q, ((src, n), dst) in enumerate(moves)]
            for cp in copies:
                cp.start()
            for cp in copies:
                cp.wait()

        du_ref[...] = _dot_nt(dproj_ref[...], w_sc[...])

    in_specs = [pl.BlockSpec((tm, IN_COLS), lambda i: (first + i, 0)), _ANY]
    args = (dproj, w_in)
    if prev is not None:
        in_specs = [_ANY] + in_specs
        args = (prev,) + args
    return _hosted_call(
        body,
        name="in_proj_bwd_%d" % part,
        grid=(count,),
        in_specs=in_specs,
        out_specs=[pl.BlockSpec((tm, D), lambda i: (first + i, 0))],
        out_shape=[jax.ShapeDtypeStruct((rows, D), F32)],
        scratch_shapes=[pltpu.VMEM((D, IN_COLS), BF16), pltpu.SemaphoreType.DMA((4,))],
        aliases={0: 0} if prev is not None else None,
        args=args,
        riders=riders,
    )


def _norm1_bwd(du, dh1, head, x2d, rstd1, norm_w, riders=()):
    rows = du.shape[0]

    def body(du_ref, dh1_ref, head_ref, x_ref, rstd_ref, nw_ref, gx_ref, ghead_ref, stats_ref):
        i = pl.program_id(0)

        @pl.when(i == 0)
        def _():
            stats_ref[...] = jnp.zeros_like(stats_ref)

        def finish(h0, out_ref):
            du = du_ref[...]
            rs = rstd_ref[...]
            n = h0 * rs
            stats_ref[0:1, :] += jnp.sum(du * n, axis=0, keepdims=True)
            dn = du * nw_ref[...]
            out_ref[...] = dh1_ref[...] + rs * (dn - n * jnp.mean(dn * n, axis=-1, keepdims=True))

        @pl.when(i == 0)
        def _():
            finish(head_ref[...], ghead_ref)

        @pl.when(i > 0)
        def _():
            finish(x_ref[...], gx_ref)

    tile = pl.BlockSpec((TM, D), lambda i: (i, 0))
    return _hosted_call(
        body,
        name="norm1_bwd",
        grid=(rows // TM,),
        in_specs=[
            tile, tile,
            pl.BlockSpec((FRONT, D), lambda i: (0, 0)),
            pl.BlockSpec((TM, D), lambda i: (jnp.maximum(i - 1, 0), 0)),
            pl.BlockSpec((TM, 1), lambda i: (i, 0)),
            pl.BlockSpec((1, D), lambda i: (0, 0)),
        ],
        out_specs=[
            pl.BlockSpec((TM, D), lambda i: (jnp.maximum(i - 1, 0), 0)),
            pl.BlockSpec((FRONT, D), lambda i: (0, 0)),
            pl.BlockSpec((8, D), lambda i: (0, 0)),
        ],
        out_shape=[
            jax.ShapeDtypeStruct(x2d.shape, F32),
            jax.ShapeDtypeStruct((FRONT, D), F32),
            jax.ShapeDtypeStruct((8, D), F32),
        ],
        scratch_shapes=[],
        args=(du, dh1, head, x2d, rstd1, norm_w),
        riders=riders,
    )


def _matmul_tn(name, x, dy, out_cols, col0=0, prev=None, k_block=None, n_block=None, riders=(), col_map=None):
    col_map = col_map or (lambda n: n)
    rows, kdim = x.shape
    ndim = dy.shape[1]
    kb = k_block or kdim
    nb = n_block or ndim
    step_bytes = lambda t: 2 * t * (kb * x.dtype.itemsize + nb * dy.dtype.itemsize) + 2 * kb * nb * 4
    tr = next(t for t in (2816, 1408, TM_HEAVY, TM) if rows % t == 0 and (t == TM or step_bytes(t) <= TN_VMEM_BUDGET))
    nr, nk, nn = rows // tr, kdim // kb, ndim // nb
    cb0 = col0 // nb

    def body(*refs):
        x_ref, dy_ref, out_ref = refs[-3], refs[-2], refs[-1]
        part = _dot_tn(x_ref[...].astype(BF16), dy_ref[...].astype(BF16))

        @pl.when(pl.program_id(2) == 0)
        def _():
            out_ref[...] = part

        @pl.when(pl.program_id(2) > 0)
        def _():
            out_ref[...] += part

    in_specs = [
        pl.BlockSpec((tr, kb), lambda n, k, r: (r, k)),
        pl.BlockSpec((tr, nb), lambda n, k, r: (r, n)),
    ]
    args = [x, dy]
    aliases = {}
    if prev is not None:
        in_specs = [pl.BlockSpec(memory_space=pl.ANY)] + in_specs
        args = [prev] + args
        aliases = {0: 0}
    (out,), rider_outs = _hosted_call(
        body,
        name=name,
        grid=(nn, nk, nr),
        in_specs=in_specs,
        out_specs=[pl.BlockSpec((kb, nb), lambda n, k, r: (k, cb0 + col_map(n)))],
        out_shape=[jax.ShapeDtypeStruct((kdim, out_cols), F32)],
        scratch_shapes=[],
        aliases=aliases,
        args=args,
        riders=riders,
    )
    return (out, rider_outs) if riders else out


def _local_step(x2d, target, w, plan):
    rows = FRONT + x2d.shape[0]
    head = jnp.concatenate([jnp.zeros((PAD_ROWS, D), F32), w["meta_tokens"]], axis=0)
    cos_t, sin_t = _rope_tables(rows)
    intra, intra_t, q_dec, k_dec, c_dec = _decay_tables()
    grads = {}

    def hosted(host, fn, *args, **kwargs):
        outs, rider_outs = fn(*args, riders=plan.riders(host, w, grads), **kwargs)
        plan.after(host, rider_outs, w, grads)
        return outs

    (u1, rstd1), _ = _norm1(head, x2d, w["mix_norm_w"])
    proj, w["w_in"] = hosted("in_proj", _in_proj, u1, w["w_in_shard"], w["route"], cos_t, sin_t)
    o, sprev = hosted("retention_fwd", _retention_fwd, proj, intra, q_dec, k_dec, c_dec)
    lru_args = (w["conv_w"], w["conv_b"], w["lru_wa"], w["lru_wx"], w["lru_ba"], w["lru_bx"], w["lru_lambda"])
    lru_saved = hosted("lru_fwd", _lru_fwd, proj, *lru_args)
    h_lru = lru_saved[0]
    y_ret, y_lru, h1, u2, rstd2 = hosted("mix_fwd", _mix_fwd, head, x2d, o, proj, h_lru, w["w_branch_ret"],
                                         w["w_branch_lru"], w["w_out"], w["ffn_norm_w"])
    silu, dsilu, act = hosted("ffn_fwd_gate", _ffn_fwd_gate, u2, w["w_ffn_in"])
    dh2, stats_loss = _ffn_out_loss(act, h1, w["w_ffn_out"], target, w["final_norm_w"])

    dg, dup, dh2b, dh1, stats_ffn = _ffn_bwd(dh2, silu, dsilu, h1, rstd2, w["w_ffn_in"], w["w_ffn_out"], w["ffn_norm_w"])
    grads["w_ffn_in"] = _matmul_tn("dw_ffn_up", u2, dup, 2 * FFN, col0=FFN, n_block=FFN_HALF,
                                   prev=_matmul_tn("dw_ffn_gate", u2, dg, 2 * FFN, n_block=FFN_HALF))
    grads["w_ffn_out"] = _matmul_tn("dw_ffn_out", act, dh2b, D, k_block=FFN_HALF)
    (dproj, do, dhl, mixed, a_ret, a_lru, dy_ret, dy_lru) = hosted(
        "mix_bwd", _mix_bwd, dh1, o, proj, h_lru, y_ret, y_lru, w["w_branch_ret"], w["w_branch_lru"], w["w_out"])
    grads["w_out"] = _matmul_tn("dw_out", mixed, dh1, D)
    grads["w_branch_ret"] = _matmul_tn("dw_branch_ret", a_ret, dy_ret, D)
    grads["w_branch_lru"] = _matmul_tn("dw_branch_lru", a_lru, dy_lru, D)
    (dproj,) = hosted("retention_bwd", _retention_bwd, dproj, proj, do, sprev, intra, intra_t, q_dec, k_dec, c_dec,
                      cos_t, sin_t)
    dproj, grads["lru_wa"], grads["lru_wx"], stats_lru = hosted(
        "lru_bwd", _lru_bwd, dproj, proj, lru_saved, dhl, w["conv_w"], w["lru_wa"], w["lru_wx"], w["lru_lambda"])
    grads["w_in"], rider_outs = _matmul_tn("dw_in", u1, dproj, IN_COLS, n_block=D, col_map=_swap_3_4,
                                           riders=plan.riders("dw_in", w, grads))
    plan.after("dw_in", rider_outs, w, grads)
    (du1,) = hosted("in_proj_bwd_0", _in_proj_bwd, dproj, w["w_in"], 0)
    (du1,) = hosted("in_proj_bwd_1", _in_proj_bwd, dproj, w["w_in"], 1, du1)
    (grad_x, grad_head, stats_in), _ = _norm1_bwd(du1, dh1, head, x2d, rstd1, w["mix_norm_w"])
    return grad_x, grad_head, grads, [stats_loss, stats_ffn, stats_in, stats_lru]


BIG_PIECES = {
    "w_in": ("col", (D, 2 * D)),
    "w_ffn_in": ("col", (D, FFN_HALF)),
    "w_ffn_out": ("row", (FFN // 4, D)),
    "w_branch_ret": ("row", (D // 4, D)),
    "w_branch_lru": ("row", (D // 4, D)),
    "w_out": ("row", (D // 4, D)),
    "lru_wa": ("lru", (LRU_BLOCKS, LRU_BLOCK // 4, LRU_BLOCK)),
    "lru_wx": ("lru", (LRU_BLOCKS, LRU_BLOCK // 4, LRU_BLOCK)),
}
SMALL_PIECES = {"meta_tokens": ("col", (N_META, D // 4)), "conv_w": ("col", (4, D // 4))}


def _full_shape(kind, shard):
    if kind == "col":
        return (shard[0], 4 * shard[1])
    if kind == "row":
        return (4 * shard[0], shard[1])
    return (shard[0], 4 * shard[1], shard[2])


def _half_shape(kind, shard):
    return (shard[0] // 2,) + tuple(shard[1:])


def _aligned(start, multiple):
    return start if isinstance(start, int) else pl.multiple_of(start, multiple)


def _lead(h, size):
    if h is None:
        return pl.ds(0, size)
    return pl.ds(_aligned(h * (size // 2), size // 2), size // 2)


def _full_region(ref, kind, shard, s, h):
    if kind == "col":
        return ref.at[_lead(h, shard[0]), pl.ds(_aligned(s * shard[1], shard[1]), shard[1])]
    if kind == "row":
        size = shard[0] if h is None else shard[0] // 2
        start = s * shard[0] + (0 if h is None else h * (shard[0] // 2))
        return ref.at[pl.ds(_aligned(start, BF16_ROWS), size), :]
    return ref.at[_lead(h, shard[0]), pl.ds(_aligned(s * shard[1], shard[1]), shard[1]), :]


def _shard_region(ref, shard, h):
    return ref.at[_lead(h, shard[0])]


def _place():
    x, y, c = lax.axis_index("x"), lax.axis_index("y"), lax.axis_index("c")
    return x, y, c, 2 * x + y


def _other_chip(s, c, k):
    s2 = jnp.bitwise_xor(s, k)
    return s2, (s2 // 2, s2 % 2, c)


def _remote(src, dst, send_sem, recv_sem, dev):
    return pltpu.make_async_remote_copy(src_ref=src, dst_ref=dst, send_sem=send_sem, recv_sem=recv_sem,
                                        device_id=dev, device_id_type=MESH)


_ANY = pl.BlockSpec(memory_space=pl.ANY)


class _Rider:
    def __init__(self, ins, out_shapes, sem_shapes, build, aliased=False):
        self.ins, self.out_shapes, self.sem_shapes, self.build, self.aliased = ins, out_shapes, sem_shapes, build, aliased


def _hosted_call(body, *, name, grid, in_specs, out_specs, out_shape, scratch_shapes, args, riders=(), aliases=None,
                 prefetch=None, riders_after_body=False):
    n_in, n_out, n_sc = len(in_specs), len(out_shape), len(scratch_shapes)
    r_in = [a for r in riders for a in r.ins]
    r_out = [s for r in riders for s in r.out_shapes]
    r_sem = [s for r in riders for s in r.sem_shapes]
    lead = () if prefetch is None else (prefetch,)
    assert prefetch is None or not (aliases or any(r.aliased for r in riders))

    def full_body(*refs):
        head, refs = refs[:len(lead)], refs[len(lead):]
        ins, rin = refs[:n_in], refs[n_in:n_in + len(r_in)]
        o0 = n_in + len(r_in)
        outs, rout = refs[o0:o0 + n_out], refs[o0 + n_out:o0 + n_out + len(r_out)]
        s0 = o0 + n_out + len(r_out)
        scratch, rsem = refs[s0:s0 + n_sc], refs[s0 + n_sc:]
        starts, waits = [], []
        pi = po = ps = 0
        for r in riders:
            st, wt = r.build(rin[pi:pi + len(r.ins)], rout[po:po + len(r.out_shapes)], rsem[ps:ps + len(r.sem_shapes)])
            starts += st
            waits += wt
            pi, po, ps = pi + len(r.ins), po + len(r.out_shapes), ps + len(r.sem_shapes)
        first = functools.reduce(jnp.logical_and, [pl.program_id(d) == 0 for d in range(len(grid))])
        last = functools.reduce(jnp.logical_and, [pl.program_id(d) == grid[d] - 1 for d in range(len(grid))])

        def start_riders():
            @pl.when(first)
            def _():
                for cp in starts:
                    cp.start()

        if riders and not riders_after_body:
            start_riders()
        body(*head, *ins, *outs, *scratch)
        if riders and riders_after_body:
            start_riders()
        if riders:
            @pl.when(last)
            def _():
                for wait in waits:
                    wait()

    io_aliases = dict(aliases or {})
    pi = po = 0
    for r in riders:
        if r.aliased:
            for q in range(len(r.ins)):
                io_aliases[n_in + pi + q] = n_out + po + q
        pi, po = pi + len(r.ins), po + len(r.out_shapes)
    specs = dict(
        grid=grid,
        in_specs=list(in_specs) + [_ANY] * len(r_in),
        out_specs=list(out_specs) + [_ANY] * len(r_out),
        scratch_shapes=list(scratch_shapes) + r_sem,
    )
    if prefetch is not None:
        specs = dict(grid_spec=pltpu.PrefetchScalarGridSpec(num_scalar_prefetch=1, **specs))
    res = pl.pallas_call(
        full_body,
        name=name,
        out_shape=list(out_shape) + r_out,
        input_output_aliases=io_aliases,
        compiler_params=_params(("arbitrary",) * len(grid)),
        **specs,
    )(*lead, *args, *r_in)
    rider_outs, po = [], n_out
    for r in riders:
        rider_outs.append(list(res[po:po + len(r.out_shapes)]))
        po += len(r.out_shapes)
    return list(res[:n_out]), rider_outs


def _run_riders(name, riders):
    r_in = [a for r in riders for a in r.ins]
    r_out = [s for r in riders for s in r.out_shapes]
    r_sem = [s for r in riders for s in r.sem_shapes]

    def body(*refs):
        rin, rout, rsem = refs[:len(r_in)], refs[len(r_in):len(r_in) + len(r_out)], refs[len(r_in) + len(r_out):]
        pi = po = ps = 0
        all_waits = []
        for r in riders:
            starts, waits = r.build(rin[pi:pi + len(r.ins)], rout[po:po + len(r.out_shapes)], rsem[ps:ps + len(r.sem_shapes)])
            for cp in starts:
                cp.start()
            all_waits += waits
            pi, po, ps = pi + len(r.ins), po + len(r.out_shapes), ps + len(r.sem_shapes)
        for wait in all_waits:
            wait()

    io_aliases = {}
    pi = po = 0
    for r in riders:
        if r.aliased:
            for q in range(len(r.ins)):
                io_aliases[pi + q] = po + q
        pi, po = pi + len(r.ins), po + len(r.out_shapes)
    res = pl.pallas_call(
        body,
        name=name,
        in_specs=[_ANY] * len(r_in),
        out_specs=[_ANY] * len(r_out),
        out_shape=r_out,
        scratch_shapes=r_sem,
        input_output_aliases=io_aliases,
    )(*r_in)
    outs, po = [], 0
    for r in riders:
        outs.append(list(res[po:po + len(r.out_shapes)]))
        po += len(r.out_shapes)
    return outs


def _piece(name):
    if name in BIG_PIECES:
        return (name, *BIG_PIECES[name], True)
    return (name, *SMALL_PIECES[name], False)


def _gather_rider(shards, names):
    pieces = [_piece(n) for n in names]
    n = len(pieces)

    def build(ins, outs, sems):
        local_sem, ici_send, ici_recv = sems
        _, _, c, s = _place()
        starts, waits = [], []
        for p, (_, kind, shard, split) in enumerate(pieces):
            cp = pltpu.make_async_copy(ins[p], _full_region(outs[p], kind, shard, s, None), local_sem.at[p])
            starts.append(cp)
            waits.append(cp.wait)
            h = c if split else None
            for k in (1, 2, 3):
                s2, dev = _other_chip(s, c, k)
                cp = _remote(_shard_region(ins[p], shard, h), _full_region(outs[p], kind, shard, s, h),
                             ici_send.at[p, k - 1], ici_recv.at[p, k - 1], dev)
                starts.append(cp)
                waits.append(cp.wait_send)
                region = _full_region(outs[p], kind, shard, s2, h)
                waits.append(_remote(region, region, ici_send.at[p, k - 1], ici_recv.at[p, k - 1], dev).wait_recv)
        return starts, waits

    return _Rider(
        [shards[name] for name in names],
        [jax.ShapeDtypeStruct(_full_shape(kind, shard), shards[name].dtype) for name, kind, shard, _ in pieces],
        [pltpu.SemaphoreType.DMA((n,)), pltpu.SemaphoreType.DMA((n, 3)), pltpu.SemaphoreType.DMA((n, 3))],
        build)


def _forward_rider(gathered, names):
    pieces = [_piece(n) for n in names]
    n = len(pieces)

    def build(ins, outs, sems):
        fwd_send, fwd_recv = sems
        x, y, c, s = _place()
        sibling = (x, y, 1 - c)
        starts, waits = [], []
        for p, (_, kind, shard, _) in enumerate(pieces):
            for k in (1, 2, 3):
                s2, _ = _other_chip(s, c, k)
                mine = _full_region(outs[p], kind, shard, s2, c)
                theirs = _full_region(outs[p], kind, shard, s2, 1 - c)
                cp = _remote(mine, mine, fwd_send.at[p, k - 1], fwd_recv.at[p, k - 1], sibling)
                starts.append(cp)
                waits.append(cp.wait_send)
                waits.append(_remote(theirs, theirs, fwd_send.at[p, k - 1], fwd_recv.at[p, k - 1], sibling).wait_recv)
        return starts, waits

    return _Rider(
        [gathered[name] for name in names],
        [jax.ShapeDtypeStruct(gathered[name].shape, gathered[name].dtype) for name in names],
        [pltpu.SemaphoreType.DMA((n, 3)), pltpu.SemaphoreType.DMA((n, 3))],
        build, aliased=True)


def _pair_exchange_rider(grads, names):
    n = len(names)

    def build(ins, outs, sems):
        send_sem, recv_sem = sems
        x, y, c, _ = _place()
        sibling = (x, y, 1 - c)
        starts, waits = [], []
        for p, name in enumerate(names):
            kind, shard = BIG_PIECES[name]
            for s2 in range(4):
                cp = _remote(_full_region(ins[p], kind, shard, s2, 1 - c), outs[p].at[s2], send_sem.at[p, s2],
                             recv_sem.at[p, s2], sibling)
                starts.append(cp)
                waits.append(cp.wait_send)
                waits.append(_remote(outs[p].at[s2], outs[p].at[s2], send_sem.at[p, s2], recv_sem.at[p, s2], sibling).wait_recv)
        return starts, waits

    return _Rider(
        [grads[name] for name in names],
        [jax.ShapeDtypeStruct((4,) + _half_shape(*BIG_PIECES[name]), F32) for name in names],
        [pltpu.SemaphoreType.DMA((n, 4))] * 2,
        build)


def _half_specs(kind, shard):
    half = shard[0] // 2
    if kind == "col":
        full = pl.BlockSpec((half, shard[1]), lambda j, pr: (pr[1], j))
        buf = pl.BlockSpec((None, half, shard[1]), lambda j, pr: (j, 0, 0))
    elif kind == "row":
        full = pl.BlockSpec((half, shard[1]), lambda j, pr: (2 * j + pr[1], 0))
        buf = pl.BlockSpec((None, half, shard[1]), lambda j, pr: (j, 0, 0))
    else:
        full = pl.BlockSpec((half, shard[1], shard[2]), lambda j, pr: (pr[1], j, 0))
        buf = pl.BlockSpec((None, half, shard[1], shard[2]), lambda j, pr: (j, 0, 0, 0))
    return full, buf


def _pair_sum(name, grad, recv, place):
    kind, shard = BIG_PIECES[name]
    full, buf = _half_specs(kind, shard)

    def body(pr, g_ref, r_ref, o_ref):
        o_ref[...] = (g_ref[...] + r_ref[...]).astype(BF16)

    return pl.pallas_call(
        body,
        name="pair_sum_" + name,
        grid_spec=pltpu.PrefetchScalarGridSpec(num_scalar_prefetch=1, grid=(4,), in_specs=[full, buf], out_specs=buf),
        out_shape=jax.ShapeDtypeStruct((4,) + _half_shape(kind, shard), BF16),
        compiler_params=_params(("arbitrary",)),
    )(place, grad, recv)


def _chip_exchange_rider(sums, names):
    n = len(names)

    def build(ins, outs, sems):
        send_sem, recv_sem = sems
        _, _, c, s = _place()
        starts, waits = [], []
        for p in range(n):
            for k in (1, 2, 3):
                s2, dev = _other_chip(s, c, k)
                cp = _remote(ins[p].at[s2], outs[p].at[k - 1], send_sem.at[p, k - 1], recv_sem.at[p, k - 1], dev)
                starts.append(cp)
                waits.append(cp.wait_send)
                waits.append(_remote(outs[p].at[k - 1], outs[p].at[k - 1], send_sem.at[p, k - 1], recv_sem.at[p, k - 1],
                                     dev).wait_recv)
        return starts, waits

    return _Rider(
        [sums[name] for name in names],
        [jax.ShapeDtypeStruct((3,) + _half_shape(*BIG_PIECES[name]), BF16) for name in names],
        [pltpu.SemaphoreType.DMA((n, 3))] * 2,
        build)


def _chip_sum(name, grad, recv_pair, recv_chip, place):
    kind, shard = BIG_PIECES[name]
    half = shard[0] // 2
    tail = tuple(shard[1:])
    zeros = (0,) * len(tail)
    nt = 4 if kind != "lru" and half % (4 * BF16_ROWS) == 0 else 1
    rows = half // nt
    if kind == "col":
        full = pl.BlockSpec((rows,) + tail, lambda j, pr: (pr[1] * nt + j, pr[0]))
    elif kind == "row":
        full = pl.BlockSpec((rows,) + tail, lambda j, pr: ((2 * pr[0] + pr[1]) * nt + j, 0))
    else:
        full = pl.BlockSpec((rows,) + tail, lambda j, pr: (pr[1], pr[0], 0))
    pair = pl.BlockSpec((None, rows) + tail, lambda j, pr: (pr[0], j) + zeros)
    chip = pl.BlockSpec((3, rows) + tail, lambda j, pr: (0, j) + zeros)
    out = pl.BlockSpec((rows,) + tail, lambda j, pr: (pr[1] * nt + j,) + zeros)

    def body(pr, g_ref, rp_ref, rc_ref, o_ref):
        total = g_ref[...] + rp_ref[...]
        for k in range(3):
            total = total + rc_ref[k].astype(F32)
        o_ref[...] = total

    return pl.pallas_call(
        body,
        name="chip_sum_" + name,
        grid_spec=pltpu.PrefetchScalarGridSpec(num_scalar_prefetch=1, grid=(nt,), in_specs=[full, pair, chip], out_specs=out),
        out_shape=jax.ShapeDtypeStruct(shard, F32),
        compiler_params=_params(("arbitrary",)),
    )(place, grad, recv_pair, recv_chip)


def _sibling_exchange_rider(halves, names):
    n = len(names)

    def build(ins, outs, sems):
        send_sem, recv_sem = sems
        x, y, c, _ = _place()
        sibling = (x, y, 1 - c)
        starts, waits = [], []
        for p, name in enumerate(names):
            shard = BIG_PIECES[name][1]
            mine = _shard_region(outs[p], shard, c)
            theirs = _shard_region(outs[p], shard, 1 - c)
            cp = _remote(mine, mine, send_sem.at[p], recv_sem.at[p], sibling)
            starts.append(cp)
            waits.append(cp.wait_send)
            waits.append(_remote(theirs, theirs, send_sem.at[p], recv_sem.at[p], sibling).wait_recv)
        return starts, waits

    return _Rider(
        [halves[name] for name in names],
        [jax.ShapeDtypeStruct(BIG_PIECES[name][1], F32) for name in names],
        [pltpu.SemaphoreType.DMA((n,))] * 2,
        build, aliased=True)


FIRST_WEIGHTS = ["meta_tokens", "conv_w"]
WEIGHT_GROUPS = {
    "lru": ["lru_wa", "lru_wx"],
    "branch": ["w_branch_ret", "w_branch_lru", "w_out"],
    "ffn_in": ["w_ffn_in"],
    "ffn_out": ["w_ffn_out"],
}
WEIGHT_SCHEDULE = {
    "in_proj": [("gather", "lru")],
    "retention_fwd": [("forward", "lru"), ("gather", "branch")],
    "lru_fwd": [("forward", "branch"), ("gather", "ffn_in")],
    "mix_fwd": [("forward", "ffn_in"), ("gather", "ffn_out")],
    "ffn_fwd_gate": [("forward", "ffn_out")],
}
GRAD_GROUPS = {
    "ffn_in": ["w_ffn_in"],
    "ffn_out": ["w_ffn_out"],
    "branch": ["w_out", "w_branch_ret", "w_branch_lru"],
    "lru": ["lru_wa", "lru_wx"],
    "in": ["w_in"],
}
GRAD_SCHEDULE = {
    "mix_bwd": [("pair", "ffn_in"), ("pair", "ffn_out")],
    "retention_bwd": [("chip", "ffn_in"), ("pair", "branch")],
    "lru_bwd": [("chip", "ffn_out"), ("sibling", "ffn_in"), ("chip", "branch")],
    "dw_in": [("sibling", "ffn_out"), ("sibling", "branch"), ("pair", "lru")],
    "in_proj_bwd_0": [("chip", "lru"), ("pair", "in")],
    "in_proj_bwd_1": [("sibling", "lru"), ("chip", "in")],
}


class _CommPlan:
    def __init__(self, shards, place):
        self.shards, self.place = shards, place
        self.late = {}
        self.recv_pair, self.sums, self.recv_chip, self.halves, self.final = {}, {}, {}, {}, {}

    def _grad_rider(self, stage, group, grads):
        names = GRAD_GROUPS[group]
        if stage == "pair":
            return _pair_exchange_rider(grads, names)
        if stage == "chip":
            return _chip_exchange_rider(self.sums, names)
        return _sibling_exchange_rider(self.halves, names)

    def _grad_after(self, stage, group, outs, grads):
        names = GRAD_GROUPS[group]
        if stage == "pair":
            for n, o in zip(names, outs):
                self.recv_pair[n] = o
                self.sums[n] = _pair_sum(n, grads[n], o, self.place)
        elif stage == "chip":
            for n, o in zip(names, outs):
                self.halves[n] = _chip_sum(n, grads[n], self.recv_pair[n], o, self.place)
        else:
            self.final.update(zip(names, outs))

    def riders(self, host, w, grads):
        if host in WEIGHT_SCHEDULE:
            return [_gather_rider(self.shards, WEIGHT_GROUPS[group]) if stage == "gather"
                    else _forward_rider(self.late, WEIGHT_GROUPS[group]) for stage, group in WEIGHT_SCHEDULE[host]]
        return [self._grad_rider(stage, group, grads) for stage, group in GRAD_SCHEDULE.get(host, [])]

    def after(self, host, rider_outs, w, grads):
        for (stage, group), outs in zip(WEIGHT_SCHEDULE.get(host, []), rider_outs):
            (self.late if stage == "gather" else w).update(zip(WEIGHT_GROUPS[group], outs))
        for (stage, group), outs in zip(GRAD_SCHEDULE.get(host, []), rider_outs):
            self._grad_after(stage, group, outs, grads)

    def finish(self, partial):
        outs, (blocks,) = _run_riders("tail_exchange", [_sibling_exchange_rider(self.halves, GRAD_GROUPS["in"]),
                                                        _small_exchange_rider(partial)])
        self.final.update(zip(GRAD_GROUPS["in"], outs))
        return self.final, blocks


def _adamw_math(w, g, m, v):
    m = ADAM_B1 * m + (1.0 - ADAM_B1) * g
    v = ADAM_B2 * v + (1.0 - ADAM_B2) * (g * g)
    m_hat = m / (1.0 - ADAM_B1 ** ADAM_STEP)
    v_hat = v / (1.0 - ADAM_B2 ** ADAM_STEP)
    delta = -ADAM_LR * (m_hat / (jnp.sqrt(v_hat) + ADAM_EPS) + ADAM_WD * w)
    return delta, m, v


def _adamw(name, g, w, m, v):
    rows, cols = g.shape
    tr = rows // 4 if rows % 32 == 0 else rows

    def body(g_ref, w_ref, m_ref, v_ref, go_ref, d_ref, mo_ref, vo_ref):
        gv = g_ref[...]
        delta, m2, v2 = _adamw_math(w_ref[...], gv, m_ref[...], v_ref[...])
        go_ref[...] = gv
        d_ref[...] = delta
        mo_ref[...] = m2
        vo_ref[...] = v2

    spec = pl.BlockSpec((tr, cols), lambda i: (i, 0))
    return pl.pallas_call(
        body,
        name="adamw_" + name,
        grid=(rows // tr,),
        in_specs=[spec] * 4,
        out_specs=[spec] * 4,
        out_shape=[jax.ShapeDtypeStruct((rows, cols), F32)] * 4,
        compiler_params=_params(("arbitrary",)),
    )(g, w, m, v)


SMALL_ROWS = 48
VEC_ROWS = {"final_norm_w": 1, "ffn_norm_w": 8, "mix_norm_w": 16, "conv_b": 24, "lru_ba": 25, "lru_bx": 26, "lru_lambda": 27}
VEC_NAMES = list(VEC_ROWS)
CONV_W_ROW = 28
META_ROW = 32


def _small_exchange_rider(partial):
    def build(ins, outs, sems):
        local_sem, send_sem, recv_sem = sems
        _, _, c, s = _place()
        me = 2 * s + c
        cp = pltpu.make_async_copy(ins[0], outs[0].at[me], local_sem)
        starts, waits = [cp], [cp.wait]
        for k in range(1, 8):
            peer = jnp.bitwise_xor(me, k)
            dev = (peer // 4, (peer // 2) % 2, peer % 2)
            rd = _remote(ins[0], outs[0].at[me], send_sem.at[k - 1], recv_sem.at[k - 1], dev)
            starts.append(rd)
            waits.append(rd.wait_send)
            waits.append(_remote(ins[0], outs[0].at[peer], send_sem.at[k - 1], recv_sem.at[k - 1], dev).wait_recv)
        return starts, waits

    return _Rider([partial], [jax.ShapeDtypeStruct((8, SMALL_ROWS, D), F32)],
                  [pltpu.SemaphoreType.DMA, pltpu.SemaphoreType.DMA((7,)), pltpu.SemaphoreType.DMA((7,))], build)


def _small_update(blocks, place, vecs, conv, meta):
    nvec = len(VEC_NAMES)
    qcols = D // 4

    def in_order(ref):
        total = ref[0]
        for d in range(1, 8):
            total = total + ref[d]
        return total

    def body(pr, blocks_ref, cols_ref, *refs):
        vec_refs = refs[:3 * nvec]
        conv_refs = refs[3 * nvec:3 * nvec + 3]
        meta_refs = refs[3 * nvec + 3:3 * nvec + 6]
        outs = refs[3 * nvec + 6:]
        loss_ref, vec_out, conv_out, meta_out = outs[0], outs[1:5], outs[5:9], outs[9:13]
        tot, col = in_order(blocks_ref), in_order(cols_ref)
        loss_ref[...] = jnp.sum(tot[0:1, :], axis=1, keepdims=True)
        for o in vec_out:
            o[...] = jnp.zeros_like(o)
        for j, name in enumerate(VEC_NAMES):
            w, m, v = (r[...] for r in vec_refs[3 * j:3 * j + 3])
            g = tot[VEC_ROWS[name]:VEC_ROWS[name] + 1, :]
            if name == "lru_lambda":
                g = -g / (1.0 + jnp.exp(w))
            for o, val in zip(vec_out, (g,) + _adamw_math(w, g, m, v)):
                o[j:j + 1, :] = val
        for row, n_rows, ins, group in ((CONV_W_ROW, 4, conv_refs, conv_out), (META_ROW, N_META, meta_refs, meta_out)):
            g = col[row:row + n_rows, :]
            w, m, v = (r[...] for r in ins)
            for o, val in zip(group, (g,) + _adamw_math(w, g, m, v)):
                o[...] = val

    whole = lambda shape: pl.BlockSpec(shape, lambda i, pr: (0,) * len(shape))
    in_specs = [whole((8, SMALL_ROWS, D)), pl.BlockSpec((8, SMALL_ROWS, qcols), lambda i, pr: (0, 0, pr[0]))]
    in_specs += [whole((1, D))] * (3 * nvec) + [whole((4, qcols))] * 3 + [whole((N_META, qcols))] * 3
    out_shapes = [(1, 1)] + [(8, D)] * 4 + [(4, qcols)] * 4 + [(N_META, qcols)] * 4
    return pl.pallas_call(
        body,
        name="small_update",
        grid_spec=pltpu.PrefetchScalarGridSpec(num_scalar_prefetch=1, grid=(1,), in_specs=in_specs,
                                               out_specs=[whole(s) for s in out_shapes]),
        out_shape=[jax.ShapeDtypeStruct(s, F32) for s in out_shapes],
        compiler_params=_params(("arbitrary",)),
    )(place, blocks, blocks, *[a for t in vecs for a in t], *conv, *meta)


WEIGHT_ORDER = ["meta_tokens", "mix_norm_w", "w_in", "conv_w", "conv_b", "lru_wa", "lru_ba", "lru_wx", "lru_bx", "lru_lambda",
                "w_branch_ret", "w_branch_lru", "w_out", "ffn_norm_w", "w_ffn_in", "w_ffn_out", "final_norm_w"]


def kernel(x, meta_tokens, mix_norm_w, w_in, conv_w, conv_b, lru_wa, lru_ba, lru_wx, lru_bx, lru_lambda, w_branch_ret, w_branch_lru, w_out, ffn_norm_w, w_ffn_in, w_ffn_out, final_norm_w, loss_target, m_meta_tokens, m_mix_norm_w, m_w_in, m_conv_w, m_conv_b, m_lru_wa, m_lru_ba, m_lru_wx, m_lru_bx, m_lru_lambda, m_w_branch_ret, m_w_branch_lru, m_w_out, m_ffn_norm_w, m_w_ffn_in, m_w_ffn_out, m_final_norm_w, v_meta_tokens, v_mix_norm_w, v_w_in, v_conv_w, v_conv_b, v_lru_wa, v_lru_ba, v_lru_wx, v_lru_bx, v_lru_lambda, v_w_branch_ret, v_w_branch_lru, v_w_out, v_ffn_norm_w, v_w_ffn_in, v_w_ffn_out, v_final_norm_w):
    args = locals()
    wts = {n: args[n] for n in WEIGHT_ORDER}
    mom = {n: args["m_" + n] for n in WEIGHT_ORDER}
    var = {n: args["v_" + n] for n in WEIGHT_ORDER}
    place = jnp.stack([2 * lax.axis_index("x") + lax.axis_index("y"), lax.axis_index("c")]).astype(jnp.int32)

    shards = {n: wts[n][0].astype(BF16) for n in BIG_PIECES}
    shards["meta_tokens"] = wts["meta_tokens"]
    shards["conv_w"] = wts["conv_w"][0]
    plan = _CommPlan(shards, place)
    (first,) = _run_riders("gather_first", [_gather_rider(shards, FIRST_WEIGHTS)])
    w = dict(zip(FIRST_WEIGHTS, first))
    for n in VEC_NAMES:
        w[n] = wts[n].reshape(1, D)
    chips = jnp.bitwise_xor(place[0], jnp.arange(4, dtype=jnp.int32))
    w["route"] = jnp.concatenate([place, jnp.stack([2 * chips, 2 * chips + 1], axis=1).reshape(8)])
    w["w_in_shard"] = shards["w_in"]

    grad_x, grad_head, _, stats = _local_step(x[0], loss_target[0], w, plan)
    partial = jnp.concatenate(stats + [grad_head[PAD_ROWS:]], axis=0)
    shard_grads, blocks = plan.finish(partial)

    out = {}
    for n in BIG_PIECES:
        shape2d = (-1, wts[n].shape[-1])
        res = _adamw(n, *[a.reshape(shape2d) for a in (shard_grads[n], wts[n], mom[n], var[n])])
        out[n] = [r.reshape(wts[n].shape) for r in res]

    vecs = [tuple(a[n].reshape(1, D) for a in (wts, mom, var)) for n in VEC_NAMES]
    conv = tuple(a["conv_w"][0] for a in (wts, mom, var))
    meta = tuple(a["meta_tokens"] for a in (wts, mom, var))
    res = _small_update(blocks, place, vecs, conv, meta)
    loss = res[0].reshape(())
    for j, n in enumerate(VEC_NAMES):
        out[n] = [r[j].reshape(wts[n].shape) for r in res[1:5]]
    out["conv_w"] = [r.reshape(wts["conv_w"].shape) for r in res[5:9]]
    out["meta_tokens"] = list(res[9:13])

    return (loss, grad_x.reshape(x.shape)) + tuple(out[n][kind] for kind in range(4) for n in WEIGHT_ORDER)
```

```python
import functools
import math

import jax
import jax.numpy as jnp
from jax import lax
from jax.experimental import pallas as pl
from jax.experimental.pallas import tpu as pltpu

F32 = jnp.float32
BF16 = jnp.bfloat16

LANES = 128
BF16_ROWS = 16

D = 1024
HEADS = 8
DH = 128
CHUNK = 256
N_META = 16
FRONT = 256
PAD_ROWS = FRONT - N_META
LRU_BLOCKS = 4
LRU_BLOCK = 256
LRU_C = 8.0
FFN = 2816
FFN_HALF = FFN // 2
IN_COLS = 8 * D
ROPE_BASE = 10000.0
EPS = 1e-6
QK_SCALE = DH ** -0.5

ADAM_LR = 0.001
ADAM_B1 = 0.9
ADAM_B2 = 0.999
ADAM_EPS = 1e-08
ADAM_WD = 0.01
ADAM_STEP = 10

TM = 256
TM_HEAVY = 768
MXU_TILE = 256
FFN_SUB = 2 * MXU_TILE
_FFN_SUBS = [(a, min(a + FFN_SUB, FFN)) for a in range(0, FFN, FFN_SUB)]
TM_MIX_BWD = 384
TM_LRU = 384
TM_MIX_FWD = 384
TM_IN_PROJ = 1408
VMEM_LIMIT = 60 * 1024 * 1024
TN_VMEM_BUDGET = 40 * 1024 * 1024

NT_DIMS = (((1,), (1,)), ((), ()))
TN_DIMS = (((0,), (0,)), ((), ()))
MESH = pl.DeviceIdType.MESH


def _params(sem=None):
    if sem is None:
        return pltpu.CompilerParams(vmem_limit_bytes=VMEM_LIMIT)
    return pltpu.CompilerParams(dimension_semantics=sem, vmem_limit_bytes=VMEM_LIMIT)


def _dot(a, b):
    return jnp.dot(a, b, preferred_element_type=F32)


def _dot_nt(a, b):
    return lax.dot_general(a, b, NT_DIMS, preferred_element_type=F32)


def _dot_tn(a, b):
    return lax.dot_general(a, b, TN_DIMS, preferred_element_type=F32)


def _sigmoid(z):
    return 1.0 / (1.0 + jnp.exp(-z))


def _log1p(x):
    return jnp.where(x < 1e-3, x * (1.0 - x * (0.5 - x * (1.0 / 3.0))), jnp.log(1.0 + x))


def _softplus(x):
    return jnp.maximum(x, 0.0) + _log1p(jnp.exp(-jnp.abs(x)))


def _one_minus_square(a, log_a):
    x = 2.0 * log_a
    series = -x * (1.0 + x * (0.5 + x * (1.0 / 6.0)))
    return jnp.where(x > -0.02, series, 1.0 - a * a)


_GELU_K = math.sqrt(2.0 / math.pi)


def _gelu_and_grad(x):
    inner = _GELU_K * (x + 0.044715 * x * x * x)
    t = jnp.tanh(inner)
    val = 0.5 * x * (1.0 + t)
    grad = 0.5 * (1.0 + t) + 0.5 * x * (1.0 - t * t) * _GELU_K * (1.0 + 3.0 * 0.044715 * x * x)
    return val, grad


def _row_ids(i, rows, shape):
    return i * rows + lax.broadcasted_iota(jnp.int32, shape, 0)


def _rope_tables(rows):
    inv_freq = ROPE_BASE ** (-jnp.arange(0, DH, 2, dtype=F32) / DH)
    block_pos = jnp.arange(rows // FRONT, dtype=jnp.int32) * FRONT - PAD_ROWS
    coarse = block_pos.astype(F32)[:, None] * inv_freq[None, :]
    fine = jnp.arange(FRONT, dtype=F32)[:, None] * inv_freq[None, :]
    ca, sa = jnp.cos(coarse)[:, None, :], jnp.sin(coarse)[:, None, :]
    cb, sb = jnp.cos(fine)[None, :, :], jnp.sin(fine)[None, :, :]
    cos = (ca * cb - sa * sb).reshape(rows, DH // 2)
    sin = (sa * cb + ca * sb).reshape(rows, DH // 2)
    return jnp.concatenate([cos, cos], axis=1), jnp.concatenate([-sin, sin], axis=1)


def _decay_tables():
    log_g = jnp.log(1.0 - 2.0 ** (-5.0 - jnp.arange(HEADS, dtype=F32)))
    idx = jnp.arange(CHUNK, dtype=F32)
    diff = idx[:, None] - idx[None, :]
    intra = jnp.where(diff[None] >= 0, jnp.exp(jnp.maximum(diff, 0.0)[None] * log_g[:, None, None]), 0.0)
    q_decay = jnp.exp((idx + 1.0)[:, None] * log_g[None, :])
    k_decay = jnp.exp((CHUNK - 1.0 - idx)[:, None] * log_g[None, :])
    chunk_decay = jnp.exp(CHUNK * log_g)
    wide = lambda a: jnp.repeat(a, DH, axis=-1)
    return intra, jnp.swapaxes(intra, 1, 2), wide(q_decay), wide(k_decay), wide(chunk_decay[None, :])


def _norm1(head, x2d, norm_w, riders=()):
    rows = FRONT + x2d.shape[0]
    tm = TM_IN_PROJ if rows % TM_IN_PROJ == 0 else TM_MIX_FWD
    nt = rows // tm

    def body(head_hbm, x_hbm, nw_ref, u_ref, rstd_ref, h0_sc, h0_sem):
        slot = _frame_rows(head_hbm, x_hbm, h0_sc, h0_sem, pl.program_id(0), tm, nt)
        hv = h0_sc[slot]
        rs = lax.rsqrt(jnp.mean(hv * hv, axis=-1, keepdims=True) + EPS)
        u_ref[...] = ((hv * rs) * nw_ref[...]).astype(BF16)
        rstd_ref[...] = rs

    return _hosted_call(
        body,
        name="norm1",
        grid=(nt,),
        in_specs=[_ANY, _ANY, pl.BlockSpec((1, D), lambda i: (0, 0))],
        out_specs=[pl.BlockSpec((tm, D), lambda i: (i, 0)), pl.BlockSpec((tm, 1), lambda i: (i, 0))],
        out_shape=[jax.ShapeDtypeStruct((rows, D), BF16), jax.ShapeDtypeStruct((rows, 1), F32)],
        scratch_shapes=[pltpu.VMEM((2, tm, D), F32), pltpu.SemaphoreType.DMA((3,))],
        args=(head, x2d, norm_w),
        riders=riders,
    )


def _heavy_tile(rows):
    return TM_HEAVY if rows % TM_HEAVY == 0 else TM


def _lru_tile(rows):
    return TM_LRU if rows % TM_LRU == 0 else TM


def _swap_3_4(group):
    return jnp.where(group == 3, 4, jnp.where(group == 4, 3, group))


LRU_IN_COL = 3
GATES_COL = 1


def _in_proj(u, w_shard, route, cos_t, sin_t, riders=()):
    rows = u.shape[0]
    tm = TM_IN_PROJ if rows % TM_IN_PROJ == 0 else _heavy_tile(rows)
    nt = rows // tm
    kind, shard = BIG_PIECES["w_in"]

    def body(route_ref, u_hbm, wsh_ref, cos_ref, sin_ref, proj_ref, wfull_ref,
             u_sc, w_sc, u_sem, w_sem, local_sem, ici_send, ici_recv, fwd_send, fwd_recv):
        g, i = pl.program_id(0), pl.program_id(1)
        s, c = route_ref[0], route_ref[1]
        gid = route_ref[2 + g]
        sibling = (s // 2, s % 2, 1 - c)
        local = pltpu.make_async_copy(wsh_ref, _full_region(wfull_ref, kind, shard, s, None), local_sem)
        u_copies = [pltpu.make_async_copy(u_hbm.at[pl.ds(t * tm, tm)], u_sc.at[t], u_sem.at[t]) for t in range(nt)]
        sends, arrivals = [], []
        for k in (1, 2, 3):
            s2, dev = _other_chip(s, c, k)
            sends.append(_remote(_shard_region(wsh_ref, shard, c), _full_region(wfull_ref, kind, shard, s, c),
                                 ici_send.at[k - 1], ici_recv.at[k - 1], dev))
            mine = _full_region(wfull_ref, kind, shard, s2, c)
            theirs = _full_region(wfull_ref, kind, shard, s2, 1 - c)
            arrivals.append((_remote(mine, mine, ici_send.at[k - 1], ici_recv.at[k - 1], dev),
                             _remote(mine, mine, fwd_send.at[k - 1], fwd_recv.at[k - 1], sibling),
                             _remote(theirs, theirs, fwd_send.at[k - 1], fwd_recv.at[k - 1], sibling)))

        slot = g % 2
        last_tile = i == nt - 1

        def block_copy(src, col, to_slot):
            return pltpu.make_async_copy(src.at[:, pl.ds(pl.multiple_of(col * D, D), D)], w_sc.at[to_slot],
                                         w_sem.at[to_slot])

        @pl.when(jnp.logical_and(g == 0, i == 0))
        def _():
            for cp in sends:
                cp.start()
            local.start()
            for cp in u_copies:
                cp.start()
            cp = block_copy(wsh_ref, 0, 0)
            cp.start()
            cp.wait()

        @pl.when(jnp.logical_and(last_tile, g == 0))
        def _():
            block_copy(wsh_ref, 1, 1).start()

        for k, (arrived, forward, forwarded) in zip((1, 2, 3), arrivals):
            @pl.when(jnp.logical_and(last_tile, g == 2 * k - 1))
            def _():
                arrived.wait_recv()
                forward.start()
                forwarded.wait_recv()
                block_copy(wfull_ref, route_ref[2 + 2 * k], 0).start()

            @pl.when(jnp.logical_and(last_tile, g == 2 * k))
            def _():
                block_copy(wfull_ref, route_ref[3 + 2 * k], 1).start()

        @pl.when(jnp.logical_and(i == 0, g > 0))
        def _():
            block_copy(wfull_ref, 0, slot).wait()

        for t in range(nt):
            @pl.when(jnp.logical_and(g == 0, i == t))
            def _():
                u_copies[t].wait()

        acc = _dot(u_sc[i], w_sc[slot])

        @pl.when(gid < 2)
        def _():
            scale = jnp.where(gid == 1, QK_SCALE, 1.0).astype(F32)
            for h in range(HEADS):
                sl = slice(h * DH, (h + 1) * DH)
                blk = acc[:, sl]
                out = (blk * cos_ref[...] + pltpu.roll(blk, DH // 2, axis=1) * sin_ref[...]) * scale
                proj_ref[:, sl] = out.astype(BF16)

        @pl.when(gid >= 2)
        def _():
            proj_ref[...] = acc.astype(BF16)

        @pl.when(jnp.logical_and(g == 7, i == nt - 1))
        def _():
            local.wait()
            for cp in sends:
                cp.wait_send()
            for _, forward, _ in arrivals:
                forward.wait_send()

    return _hosted_call(
        body,
        name="in_proj",
        grid=(8, nt),
        in_specs=[
            _ANY, _ANY,
            pl.BlockSpec((tm, DH), lambda g, i, rt: (i, 0)),
            pl.BlockSpec((tm, DH), lambda g, i, rt: (i, 0)),
        ],
        out_specs=[pl.BlockSpec((tm, D), lambda g, i, rt: (i, _swap_3_4(rt[2 + g]))), _ANY],
        out_shape=[jax.ShapeDtypeStruct((rows, IN_COLS), BF16), jax.ShapeDtypeStruct((D, IN_COLS), BF16)],
        scratch_shapes=[
            pltpu.VMEM((nt, tm, D), BF16), pltpu.VMEM((2, D, D), BF16),
            pltpu.SemaphoreType.DMA((nt,)), pltpu.SemaphoreType.DMA((2,)), pltpu.SemaphoreType.DMA,
            pltpu.SemaphoreType.DMA((3,)), pltpu.SemaphoreType.DMA((3,)),
            pltpu.SemaphoreType.DMA((3,)), pltpu.SemaphoreType.DMA((3,)),
        ],
        args=(u, w_shard, cos_t, sin_t),
        riders=riders,
        prefetch=route,
        riders_after_body=True,
    )


def _retention_fwd(proj_bf, intra, q_dec, k_dec, c_dec, riders=()):
    rows = proj_bf.shape[0]
    nc = rows // CHUNK

    def body(q_ref, k_ref, v_ref, m_ref, qd_ref, kd_ref, cd_ref, o_ref, sprev_ref, s_sc):
        @pl.when(pl.program_id(0) == 0)
        def _():
            s_sc[...] = jnp.zeros_like(s_sc)

        for h in range(HEADS):
            sl = slice(h * DH, (h + 1) * DH)
            q, k, v = q_ref[:, sl], k_ref[:, sl], v_ref[:, sl]
            state = s_sc[h]
            state_b = state.astype(BF16)
            sprev_ref[0, h] = state_b
            s = _dot_nt(q, k) * m_ref[h]
            inner = _dot(s.astype(BF16), v)
            cross = _dot(q, state_b) * qd_ref[:, sl]
            o_ref[:, sl] = (inner + cross).astype(BF16)
            k_scaled = (k.astype(F32) * kd_ref[:, sl]).astype(BF16)
            s_sc[h] = state * cd_ref[:, sl] + _dot_tn(k_scaled, v)

    chunk_spec = lambda col: pl.BlockSpec((CHUNK, D), lambda c: (c, col))
    const2 = lambda shape: pl.BlockSpec(shape, lambda c: (0, 0))
    return _hosted_call(
        body,
        name="retention_fwd",
        grid=(nc,),
        in_specs=[
            chunk_spec(0), chunk_spec(1), chunk_spec(2),
            pl.BlockSpec((HEADS, CHUNK, CHUNK), lambda c: (0, 0, 0)),
            const2((CHUNK, D)), const2((CHUNK, D)), const2((1, D)),
        ],
        out_specs=[
            pl.BlockSpec((CHUNK, D), lambda c: (c, 0)),
            pl.BlockSpec((1, HEADS, DH, DH), lambda c: (c, 0, 0, 0)),
        ],
        out_shape=[
            jax.ShapeDtypeStruct((rows, D), BF16),
            jax.ShapeDtypeStruct((nc, HEADS, DH, DH), BF16),
        ],
        scratch_shapes=[pltpu.VMEM((HEADS, DH, DH), F32)],
        args=(proj_bf, proj_bf, proj_bf, intra, q_dec, k_dec, c_dec),
        riders=riders,
    )


def _shift_down(x, first8_prev, d):
    rolled = pltpu.roll(x, d, axis=0)
    head = pltpu.roll(jnp.concatenate([first8_prev, x[0:8]], axis=0), d, axis=0)[8:16]
    return rolled, head


def _conv_and_gates(x, prev8, cw_ref, cb_ref, wa_ref, wx_ref, ba_ref, bx_ref, lam_ref, c_sc):
    cw = cw_ref[...]
    conv = cb_ref[...] + cw[3:4] * x
    head = cb_ref[...] + cw[3:4] * x[0:8]
    for d in (1, 2, 3):
        rolled, hd = _shift_down(x, prev8, d)
        conv = conv + cw[3 - d:4 - d] * rolled
        head = head + cw[3 - d:4 - d] * hd
    c_sc[...] = conv
    c_sc[0:8, :] = head
    c = c_sc[...]
    zr, zi = [], []
    for g in range(LRU_BLOCKS):
        sl = slice(g * LRU_BLOCK, (g + 1) * LRU_BLOCK)
        cg = c[:, sl].astype(BF16)
        zr.append(_dot(cg, wa_ref[g]))
        zi.append(_dot(cg, wx_ref[g]))
    r = _sigmoid(jnp.concatenate(zr, axis=1) + ba_ref[...])
    gate_i = _sigmoid(jnp.concatenate(zi, axis=1) + bx_ref[...])
    sp = _softplus(-lam_ref[...])
    log_a = (-LRU_C) * r * sp
    a = jnp.exp(log_a)
    mult = jnp.sqrt(_one_minus_square(a, log_a))
    return c, r, gate_i, a, mult, log_a


def _lru_fwd(proj, conv_w, conv_b, wa, wx, ba, bx, lam, riders=()):
    rows = proj.shape[0]
    TM = _lru_tile(rows)
    nt = rows // TM

    def body(x_ref, cw_ref, cb_ref, wa_ref, wx_ref, ba_ref, bx_ref, lam_ref,
             h_ref, c_ref, r_ref, i_ref, la_ref, mult_ref, prev_sc, carry_sc, c_sc, a_sc, u_sc, h_sc):
        i = pl.program_id(0)

        @pl.when(i == 0)
        def _():
            prev_sc[...] = jnp.zeros_like(prev_sc)
            carry_sc[...] = jnp.zeros_like(carry_sc)

        x = x_ref[...].astype(F32)
        c, r, gate_i, a, mult, log_a = _conv_and_gates(x, prev_sc[...], cw_ref, cb_ref, wa_ref, wx_ref, ba_ref, bx_ref,
                                                       lam_ref, c_sc)
        for ref, val in ((c_ref, c), (r_ref, r), (i_ref, gate_i), (la_ref, log_a), (mult_ref, mult)):
            ref[...] = val.astype(BF16)
        prev_sc[...] = x[TM - 8:TM]
        valid = _row_ids(i, TM, (TM, D)) >= PAD_ROWS
        a_sc[...] = a
        u_sc[...] = jnp.where(valid, mult * gate_i * c, 0.0)
        row8 = lax.broadcasted_iota(jnp.int32, (8, D), 0)

        def group(gi, hprev):
            r0 = pl.multiple_of(gi * 8, 8)
            aa = a_sc[pl.ds(r0, 8), :]
            uu = u_sc[pl.ds(r0, 8), :]
            for d in (1, 2, 4):
                a_sh = jnp.where(row8 >= d, pltpu.roll(aa, d, axis=0), 1.0)
                u_sh = jnp.where(row8 >= d, pltpu.roll(uu, d, axis=0), 0.0)
                uu = uu + aa * u_sh
                aa = aa * a_sh
            hb = aa * hprev + uu
            h_sc[pl.ds(r0, 8), :] = hb
            return hb[7:8, :]

        hlast = lax.fori_loop(0, TM // 8, group, carry_sc[0:1, :])
        carry_sc[0:1, :] = hlast
        h_ref[...] = h_sc[...].astype(BF16)

    vec = pl.BlockSpec((1, D), lambda i: (0, 0))
    wspec = pl.BlockSpec((LRU_BLOCKS, LRU_BLOCK, LRU_BLOCK), lambda i: (0, 0, 0))
    return _hosted_call(
        body,
        name="lru_fwd",
        grid=(nt,),
        in_specs=[
            pl.BlockSpec((TM, D), lambda i: (i, LRU_IN_COL)),
            pl.BlockSpec((4, D), lambda i: (0, 0)),
            vec, wspec, wspec, vec, vec, vec,
        ],
        out_specs=[pl.BlockSpec((TM, D), lambda i: (i, 0)) for _ in range(6)],
        out_shape=[jax.ShapeDtypeStruct((rows, D), BF16) for _ in range(6)],
        scratch_shapes=[
            pltpu.VMEM((8, D), F32), pltpu.VMEM((8, D), F32),
            pltpu.VMEM((TM, D), F32), pltpu.VMEM((TM, D), F32), pltpu.VMEM((TM, D), F32), pltpu.VMEM((TM, D), F32),
        ],
        args=(proj, conv_w, conv_b, wa, wx, ba, bx, lam),
        riders=riders,
    )


def _group_norm(o):
    outs, rstds = [], []
    for h in range(HEADS):
        oh = o[:, h * DH:(h + 1) * DH]
        rs = lax.rsqrt(jnp.mean(oh * oh, axis=-1, keepdims=True) + EPS)
        outs.append(oh * rs)
        rstds.append(rs)
    return jnp.concatenate(outs, axis=1), rstds


def _frame_rows(head_hbm, x_hbm, buf, sems, i, tm, nt):
    slot = i % 2

    @pl.when(i == 0)
    def _():
        first = [pltpu.make_async_copy(head_hbm, buf.at[0, pl.ds(0, FRONT)], sems.at[2]),
                 pltpu.make_async_copy(x_hbm.at[pl.ds(0, tm - FRONT)], buf.at[0, pl.ds(FRONT, tm - FRONT)], sems.at[0])]
        for cp in first:
            cp.start()
        for cp in first:
            cp.wait()

    @pl.when(i + 1 < nt)
    def _():
        start = pl.multiple_of((i + 1) * tm - FRONT, LANES)
        pltpu.make_async_copy(x_hbm.at[pl.ds(start, tm)], buf.at[1 - slot], sems.at[1 - slot]).start()

    @pl.when(i > 0)
    def _():
        pltpu.make_async_copy(x_hbm.at[pl.ds(0, tm)], buf.at[slot], sems.at[slot]).wait()

    return slot


def _mix_fwd(head, x2d, o, proj, h_lru, w_br, w_bl, w_o, ffn_norm_w, riders=()):
    rows = o.shape[0]
    tm = TM_MIX_FWD
    assert rows % tm == 0 and tm > FRONT
    nt = rows // tm

    def body(head_hbm, x_hbm, o_ref, gates_ref, hl_ref, wbr_ref, wbl_ref, wo_ref, nw_ref,
             yret_ref, ylru_ref, h1_ref, u2_ref, rstd_ref, h0_sc, h0_sem):
        i = pl.program_id(0)
        slot = _frame_rows(head_hbm, x_hbm, h0_sc, h0_sem, i, tm, nt)
        gate = lambda j: gates_ref[:, j * D:(j + 1) * D].astype(F32)
        on, _ = _group_norm(o_ref[...].astype(F32))
        gret = gate(0)
        a_ret = (gret * _sigmoid(gret) * on).astype(BF16)
        y_ret = _dot(a_ret, wbr_ref[...])
        gl, _ = _gelu_and_grad(gate(1))
        a_lru = (gl * hl_ref[...].astype(F32)).astype(BF16)
        y_lru = _dot(a_lru, wbl_ref[...])
        mixed = (_sigmoid(gate(2)) * y_ret + _sigmoid(gate(3)) * y_lru).astype(BF16)
        delta = _dot(mixed, wo_ref[...])
        yret_ref[...] = y_ret.astype(BF16)
        ylru_ref[...] = y_lru.astype(BF16)
        h1 = h0_sc[slot] + delta
        rs = lax.rsqrt(jnp.mean(h1 * h1, axis=-1, keepdims=True) + EPS)
        h1_ref[...] = h1
        u2_ref[...] = ((h1 * rs) * nw_ref[...]).astype(BF16)
        rstd_ref[...] = rs

    tile = lambda col: pl.BlockSpec((tm, D), lambda i: (i, col))
    wspec = pl.BlockSpec((D, D), lambda i: (0, 0))
    return _hosted_call(
        body,
        name="mix_fwd",
        grid=(nt,),
        in_specs=[
            _ANY, _ANY,
            tile(0), pl.BlockSpec((tm, 4 * D), lambda i: (i, GATES_COL)), tile(0),
            wspec, wspec, wspec,
            pl.BlockSpec((1, D), lambda i: (0, 0)),
        ],
        out_specs=[tile(0), tile(0), tile(0), tile(0), pl.BlockSpec((tm, 1), lambda i: (i, 0))],
        out_shape=[
            jax.ShapeDtypeStruct((rows, D), BF16), jax.ShapeDtypeStruct((rows, D), BF16),
            jax.ShapeDtypeStruct((rows, D), F32), jax.ShapeDtypeStruct((rows, D), BF16),
            jax.ShapeDtypeStruct((rows, 1), F32),
        ],
        scratch_shapes=[pltpu.VMEM((2, tm, D), F32), pltpu.SemaphoreType.DMA((3,))],
        args=(head, x2d, o, proj, h_lru, w_br, w_bl, w_o, ffn_norm_w),
        riders=riders,
    )


def _ffn_fwd_gate(u2, w_ffn_in, riders=()):
    rows = u2.shape[0]
    tm = _heavy_tile(rows)
    hid = lambda: pl.BlockSpec((tm, FFN), lambda i: (i, 0))

    def body(u2_ref, w_hbm, silu_ref, dsilu_ref, act_ref, w_sc, w_sem):
        @pl.when(pl.program_id(0) == 0)
        def _():
            cp = pltpu.make_async_copy(w_hbm, w_sc, w_sem)
            cp.start()
            cp.wait()

        u2 = u2_ref[...]
        for a, b in _FFN_SUBS:
            g = _dot(u2, w_sc[:, a:b])
            up = _dot(u2, w_sc[:, FFN + a:FFN + b])
            sg = _sigmoid(g)
            silu = g * sg
            silu_ref[:, a:b] = silu.astype(BF16)
            dsilu_ref[:, a:b] = (up * (sg * (1.0 + g * (1.0 - sg)))).astype(BF16)
            act_ref[:, a:b] = (silu * up).astype(BF16)

    return _hosted_call(
        body,
        name="ffn_fwd_gate",
        grid=(rows // tm,),
        in_specs=[pl.BlockSpec((tm, D), lambda i: (i, 0)), _ANY],
        out_specs=[hid(), hid(), hid()],
        out_shape=[jax.ShapeDtypeStruct((rows, FFN), BF16)] * 3,
        scratch_shapes=[pltpu.VMEM((D, 2 * FFN), BF16), pltpu.SemaphoreType.DMA],
        args=(u2, w_ffn_in),
        riders=riders,
    )


def _ffn_out_loss(act, h1, w_ffn_out, target, final_norm_w):
    rows = act.shape[0]
    tm = _heavy_tile(rows)
    nt = rows // tm

    def body(act_ref, h1_ref, wo_ref, fnw_ref, tgt_hbm, dh2_ref, stats_ref, tgt_sc, tgt_sem):
        i = pl.program_id(0)
        slot = i % 2

        @pl.when(i == 0)
        def _():
            stats_ref[...] = jnp.zeros_like(stats_ref)
            tgt_sc[0, 0:FRONT, :] = jnp.zeros((FRONT, D), F32)
            cp = pltpu.make_async_copy(tgt_hbm.at[pl.ds(0, tm - FRONT)], tgt_sc.at[0, pl.ds(FRONT, tm - FRONT)], tgt_sem.at[0])
            cp.start()
            cp.wait()

        @pl.when(i + 1 < nt)
        def _():
            start = pl.multiple_of((i + 1) * tm - FRONT, FRONT)
            pltpu.make_async_copy(tgt_hbm.at[pl.ds(start, tm)], tgt_sc.at[1 - slot], tgt_sem.at[1 - slot]).start()

        @pl.when(i > 0)
        def _():
            pltpu.make_async_copy(tgt_hbm.at[pl.ds(0, tm)], tgt_sc.at[slot], tgt_sem.at[slot]).wait()

        h2 = h1_ref[...] + _dot(act_ref[...], wo_ref[...])
        rs = lax.rsqrt(jnp.mean(h2 * h2, axis=-1, keepdims=True) + EPS)
        n = h2 * rs
        fnw = fnw_ref[...]
        valid = _row_ids(i, tm, (tm, D)) >= FRONT
        diff = jnp.where(valid, n * fnw - tgt_sc[slot], 0.0)
        dy = diff * (1.0 / D)
        stats_ref[0:1, :] += (0.5 / D) * jnp.sum(diff * diff, axis=0, keepdims=True)
        stats_ref[1:2, :] += jnp.sum(dy * n, axis=0, keepdims=True)
        dn = dy * fnw
        dh2_ref[...] = rs * (dn - n * jnp.mean(dn * n, axis=-1, keepdims=True))

    return pl.pallas_call(
        body,
        name="ffn_out_loss",
        grid=(nt,),
        in_specs=[
            pl.BlockSpec((tm, FFN), lambda i: (i, 0)),
            pl.BlockSpec((tm, D), lambda i: (i, 0)),
            pl.BlockSpec((FFN, D), lambda i: (0, 0)),
            pl.BlockSpec((1, D), lambda i: (0, 0)),
            _ANY,
        ],
        out_specs=[pl.BlockSpec((tm, D), lambda i: (i, 0)), pl.BlockSpec((8, D), lambda i: (0, 0))],
        out_shape=[jax.ShapeDtypeStruct((rows, D), F32), jax.ShapeDtypeStruct((8, D), F32)],
        scratch_shapes=[pltpu.VMEM((2, tm, D), F32), pltpu.SemaphoreType.DMA((2,))],
        compiler_params=_params(("arbitrary",)),
    )(act, h1, w_ffn_out, final_norm_w, target)


def _ffn_bwd(dh2, silu, dsilu, h1, rstd2, w_ffn_in, w_ffn_out, ffn_norm_w):
    rows = dh2.shape[0]
    tm = _heavy_tile(rows)
    blk = lambda: pl.BlockSpec((tm, FFN), lambda i: (i, 0))

    def gate_body(dh2_ref, silu_ref, dsilu_ref, wo_hbm, dg_ref, dup_ref, dh2b_ref, wo_sc, wo_sem):
        @pl.when(pl.program_id(0) == 0)
        def _():
            cp = pltpu.make_async_copy(wo_hbm, wo_sc, wo_sem)
            cp.start()
            cp.wait()

        dh2b = dh2_ref[...].astype(BF16)
        dh2b_ref[...] = dh2b
        for a, b in _FFN_SUBS:
            dact = _dot_nt(dh2b, wo_sc[a:b, :])
            dup_ref[:, a:b] = (dact * silu_ref[:, a:b].astype(F32)).astype(BF16)
            dg_ref[:, a:b] = (dact * dsilu_ref[:, a:b].astype(F32)).astype(BF16)

    dg, dup, dh2b = pl.pallas_call(
        gate_body,
        name="ffn_bwd_gate",
        grid=(rows // tm,),
        in_specs=[pl.BlockSpec((tm, D), lambda i: (i, 0)), blk(), blk(), _ANY],
        out_specs=[blk(), blk(), pl.BlockSpec((tm, D), lambda i: (i, 0))],
        out_shape=[jax.ShapeDtypeStruct((rows, FFN), BF16)] * 2 + [jax.ShapeDtypeStruct((rows, D), BF16)],
        scratch_shapes=[pltpu.VMEM((FFN, D), BF16), pltpu.SemaphoreType.DMA],
        compiler_params=_params(("arbitrary",)),
    )(dh2, silu, dsilu, w_ffn_out)

    def body(dg_ref, dup_ref, dh2_ref, h1_ref, rstd_ref, w_hbm, nw_ref, dh1_ref, stats_ref, w_sc, w_sem):
        @pl.when(pl.program_id(0) == 0)
        def _():
            stats_ref[...] = jnp.zeros_like(stats_ref)
            cp = pltpu.make_async_copy(w_hbm, w_sc, w_sem)
            cp.start()
            cp.wait()

        du = _dot_nt(dg_ref[...], w_sc[:, 0:FFN]) + _dot_nt(dup_ref[...], w_sc[:, FFN:2 * FFN])
        rs = rstd_ref[...]
        n = h1_ref[...] * rs
        stats_ref[0:1, :] += jnp.sum(du * n, axis=0, keepdims=True)
        dn = du * nw_ref[...]
        dh1_ref[...] = dh2_ref[...] + rs * (dn - n * jnp.mean(dn * n, axis=-1, keepdims=True))

    row = lambda width: pl.BlockSpec((tm, width), lambda i: (i, 0))
    dh1, stats = pl.pallas_call(
        body,
        name="ffn_bwd_in",
        grid=(rows // tm,),
        in_specs=[row(FFN), row(FFN), row(D), row(D), row(1), _ANY, pl.BlockSpec((1, D), lambda i: (0, 0))],
        out_specs=[row(D), pl.BlockSpec((8, D), lambda i: (0, 0))],
        out_shape=[jax.ShapeDtypeStruct((rows, D), F32), jax.ShapeDtypeStruct((8, D), F32)],
        scratch_shapes=[pltpu.VMEM((D, 2 * FFN), BF16), pltpu.SemaphoreType.DMA],
        compiler_params=_params(("arbitrary",)),
    )(dg, dup, dh2, h1, rstd2, w_ffn_in, ffn_norm_w)
    return dg, dup, dh2b, dh1, stats


def _mix_bwd(dh1, o, proj, h_lru, y_ret, y_lru, w_br, w_bl, w_o, riders=()):
    rows = dh1.shape[0]
    tm = TM_MIX_BWD
    nt = rows // tm

    def body(dh1_ref, o_ref, gates_ref, hl_ref, yret_ref, ylru_ref, wbr_hbm, wbl_hbm, wo_hbm,
             dproj_ref, do_ref, dhl_ref, mixed_ref, aret_ref, alru_ref, dyret_ref, dylru_ref, w_sc, w_sem):
        @pl.when(pl.program_id(0) == 0)
        def _():
            copies = [pltpu.make_async_copy(src, w_sc.at[q], w_sem.at[q]) for q, src in enumerate((wbr_hbm, wbl_hbm, wo_hbm))]
            for cp in copies:
                cp.start()
            for cp in copies:
                cp.wait()

        wbr_ref, wbl_ref, wo_ref = w_sc.at[0], w_sc.at[1], w_sc.at[2]
        gate = lambda j: gates_ref[:, j * D:(j + 1) * D].astype(F32)
        dmixed = _dot_nt(dh1_ref[...].astype(BF16), wo_ref[...])
        y_ret, y_lru = yret_ref[...].astype(F32), ylru_ref[...].astype(F32)
        sa, sb = _sigmoid(gate(2)), _sigmoid(gate(3))
        mixed_ref[...] = (sa * y_ret + sb * y_lru).astype(BF16)
        dga = dmixed * y_ret * sa * (1.0 - sa)
        dgb = dmixed * y_lru * sb * (1.0 - sb)
        dy_ret = (dmixed * sa).astype(BF16)
        dy_lru = (dmixed * sb).astype(BF16)
        dyret_ref[...] = dy_ret
        dylru_ref[...] = dy_lru
        da_ret = _dot_nt(dy_ret, wbr_ref[...])
        da_lru = _dot_nt(dy_lru, wbl_ref[...])

        gret = gate(0)
        sg = _sigmoid(gret)
        silu = gret * sg
        on, rstds = _group_norm(o_ref[...].astype(F32))
        aret_ref[...] = (silu * on).astype(BF16)
        dgret = da_ret * on * (sg * (1.0 + gret * (1.0 - sg)))
        don = da_ret * silu
        for h in range(HEADS):
            sl = slice(h * DH, (h + 1) * DH)
            onh, donh = on[:, sl], don[:, sl]
            do_ref[:, sl] = (rstds[h] * (donh - onh * jnp.mean(donh * onh, axis=-1, keepdims=True))).astype(BF16)

        gl, gl_grad = _gelu_and_grad(gate(1))
        hl = hl_ref[...].astype(F32)
        alru_ref[...] = (gl * hl).astype(BF16)
        dlgate = da_lru * hl * gl_grad
        dhl_ref[...] = (da_lru * gl).astype(BF16)

        for j, val in enumerate((dgret, dlgate, dga, dgb)):
            dproj_ref[:, j * D:(j + 1) * D] = val.astype(BF16)

    tile = lambda col: pl.BlockSpec((tm, D), lambda i: (i, col))
    gates = pl.BlockSpec((tm, 4 * D), lambda i: (i, GATES_COL))
    bf = lambda: jax.ShapeDtypeStruct((rows, D), BF16)
    return _hosted_call(
        body,
        name="mix_bwd",
        grid=(nt,),
        in_specs=[tile(0), tile(0), gates, tile(0), tile(0), tile(0), _ANY, _ANY, _ANY],
        out_specs=[pl.BlockSpec((tm, 4 * D), lambda i: (i, GATES_COL))] + [tile(0)] * 7,
        out_shape=[jax.ShapeDtypeStruct((rows, IN_COLS), BF16)] + [bf() for _ in range(7)],
        scratch_shapes=[pltpu.VMEM((3, D, D), BF16), pltpu.SemaphoreType.DMA((3,))],
        args=(dh1, o, proj, h_lru, y_ret, y_lru, w_br, w_bl, w_o),
        riders=riders,
    )


def _retention_bwd(dproj, proj_bf, do, sprev, intra, intra_t, q_dec, k_dec, c_dec, cos_t, sin_t, riders=()):
    rows = proj_bf.shape[0]
    nc = rows // CHUNK

    def body(dproj_in_ref, q_ref, k_ref, v_ref, do_ref, sprev_ref, m_ref, mt_ref, qd_ref, kd_ref, cd_ref, cos_ref, sin_ref,
             dproj_ref, ds_sc):
        @pl.when(pl.program_id(0) == 0)
        def _():
            ds_sc[...] = jnp.zeros_like(ds_sc)

        cos, sin = cos_ref[...], sin_ref[...]

        def unrotate(dy):
            return dy * cos - pltpu.roll(dy, DH // 2, axis=1) * sin

        for h in range(HEADS):
            sl = slice(h * DH, (h + 1) * DH)
            q, k, v = q_ref[:, sl], k_ref[:, sl], v_ref[:, sl]
            do = do_ref[:, sl]
            dob = do.astype(BF16)
            doq = (do * qd_ref[:, sl]).astype(BF16)
            state_prev = sprev_ref[0, h]
            dstate = ds_sc[h]
            dstate_b = dstate.astype(BF16)
            s_t = (_dot_nt(k, q) * mt_ref[h]).astype(BF16)
            ds_t = (_dot_nt(v, dob) * mt_ref[h]).astype(BF16)
            ds = (_dot_nt(dob, v) * m_ref[h]).astype(BF16)
            kd = kd_ref[:, sl]
            dq = _dot(ds, k) + _dot_nt(doq, state_prev)
            dk = _dot(ds_t, q) + _dot_nt(v, dstate_b) * kd
            k_scaled = (k.astype(F32) * kd).astype(BF16)
            dv = _dot(s_t, dob) + _dot(k_scaled, dstate_b)
            ds_sc[h] = dstate * cd_ref[:, sl] + _dot_tn(q, doq)
            for part, val in enumerate((unrotate(dq), unrotate(dk) * QK_SCALE, dv)):
                dproj_ref[:, part * D + h * DH:part * D + (h + 1) * DH] = val.astype(BF16)

    rev = lambda c: nc - 1 - c
    chunk_spec = lambda col: pl.BlockSpec((CHUNK, D), lambda c: (rev(c), col))
    const2 = lambda shape: pl.BlockSpec(shape, lambda c: (0, 0))
    const3 = pl.BlockSpec((HEADS, CHUNK, CHUNK), lambda c: (0, 0, 0))
    return _hosted_call(
        body,
        name="retention_bwd",
        grid=(nc,),
        in_specs=[
            pl.BlockSpec(memory_space=pl.ANY),
            chunk_spec(0), chunk_spec(1), chunk_spec(2), chunk_spec(0),
            pl.BlockSpec((1, HEADS, DH, DH), lambda c: (rev(c), 0, 0, 0)),
            const3, const3,
            const2((CHUNK, D)), const2((CHUNK, D)), const2((1, D)),
            pl.BlockSpec((CHUNK, DH), lambda c: (rev(c), 0)),
            pl.BlockSpec((CHUNK, DH), lambda c: (rev(c), 0)),
        ],
        out_specs=[pl.BlockSpec((CHUNK, 3 * D), lambda c: (rev(c), 0))],
        out_shape=[jax.ShapeDtypeStruct(dproj.shape, BF16)],
        aliases={0: 0},
        scratch_shapes=[pltpu.VMEM((HEADS, DH, DH), F32)],
        args=(dproj, proj_bf, proj_bf, proj_bf, do, sprev, intra, intra_t, q_dec, k_dec, c_dec, cos_t, sin_t),
        riders=riders,
    )


def _lru_bwd(dproj, proj, saved, dhl, conv_w, wa, wx, lam, riders=()):
    rows = proj.shape[0]
    TM = _lru_tile(rows)
    nt = rows // TM
    per8 = TM // 8

    def body(dproj_in_ref, x_ref, xprev_ref, h_ref, hprev_ref, c_ref, r_ref, i_ref, la_ref, mult_ref, dhl_ref,
             cw_ref, wa_ref, wx_ref, lam_ref,
             dproj_ref, dwa_ref, dwx_ref, stats_ref, anext_sc, dhnext_sc, dcnext_sc, c_sc, b_sc, dh_sc):
        step = pl.program_id(0)
        i = nt - 1 - step

        @pl.when(step == 0)
        def _():
            anext_sc[...] = jnp.zeros_like(anext_sc)
            dhnext_sc[...] = jnp.zeros_like(dhnext_sc)
            dcnext_sc[...] = jnp.zeros_like(dcnext_sc)
            dwa_ref[...] = jnp.zeros_like(dwa_ref)
            dwx_ref[...] = jnp.zeros_like(dwx_ref)
            stats_ref[...] = jnp.zeros_like(stats_ref)

        first = i == 0
        x = x_ref[...].astype(F32)
        prev8 = jnp.where(first, 0.0, xprev_ref[8:16, :].astype(F32))
        c_b = c_ref[...]
        c, r, gate_i, mult = (ref[...].astype(F32) for ref in (c_ref, r_ref, i_ref, mult_ref))
        a = jnp.exp(la_ref[...].astype(F32))
        sp = _softplus(-lam_ref[...])
        dh_sc[...] = dhl_ref[...].astype(F32)

        b_sc[...] = pltpu.roll(a, TM - 1, axis=0)
        b_sc[TM - 1:TM, :] = anext_sc[0:1, :]
        anext_sc[0:1, :] = a[0:1, :]
        row8 = lax.broadcasted_iota(jnp.int32, (8, D), 0)

        def group(gi, dhnext):
            r0 = pl.multiple_of((per8 - 1 - gi) * 8, 8)
            bb = b_sc[pl.ds(r0, 8), :]
            uu = dh_sc[pl.ds(r0, 8), :]
            for d in (1, 2, 4):
                b_sh = jnp.where(row8 < 8 - d, pltpu.roll(bb, 8 - d, axis=0), 1.0)
                u_sh = jnp.where(row8 < 8 - d, pltpu.roll(uu, 8 - d, axis=0), 0.0)
                uu = uu + bb * u_sh
                bb = bb * b_sh
            dhb = bb * dhnext + uu
            dh_sc[pl.ds(r0, 8), :] = dhb
            return dhb[0:1, :]

        dhfirst = lax.fori_loop(0, per8, group, dhnext_sc[0:1, :])
        dhnext_sc[0:1, :] = dhfirst
        dh = dh_sc[...]

        h = h_ref[...].astype(F32)
        hprev8 = jnp.where(first, 0.0, hprev_ref[8:16, :].astype(F32))
        h_dn, h_head = _shift_down(h, hprev8, 1)
        c_sc[...] = h_dn
        c_sc[0:8, :] = h_head
        h_before = c_sc[...]

        valid = _row_ids(i, TM, (TM, D)) >= PAD_ROWS
        da = dh * h_before
        du = jnp.where(valid, dh, 0.0)
        dmult = du * gate_i * c
        dgate_i = du * mult * c
        dc = du * mult * gate_i
        dla = da * a - dmult * (a * a) / mult
        dla = jnp.where(valid, dla, 0.0)
        dr = dla * ((-LRU_C) * sp)
        dzr = dr * r * (1.0 - r)
        dzi = dgate_i * gate_i * (1.0 - gate_i)
        stats_ref[1:2, :] += jnp.sum(dzr, axis=0, keepdims=True)
        stats_ref[2:3, :] += jnp.sum(dzi, axis=0, keepdims=True)
        stats_ref[3:4, :] += jnp.sum(dla * ((-LRU_C) * r), axis=0, keepdims=True)
        dc_gate = []
        for g in range(LRU_BLOCKS):
            sl = slice(g * LRU_BLOCK, (g + 1) * LRU_BLOCK)
            cg = c_b[:, sl]
            dzr_g = dzr[:, sl].astype(BF16)
            dzi_g = dzi[:, sl].astype(BF16)
            dc_gate.append(_dot_nt(dzr_g, wa_ref[g]) + _dot_nt(dzi_g, wx_ref[g]))
            dwa_ref[g] += _dot_tn(cg, dzr_g)
            dwx_ref[g] += _dot_tn(cg, dzi_g)
        dc = dc + jnp.concatenate(dc_gate, axis=1)

        cw = cw_ref[...]
        stats_ref[0:1, :] += jnp.sum(dc, axis=0, keepdims=True)
        stats_ref[7:8, :] += jnp.sum(dc * x, axis=0, keepdims=True)
        dx = cw[3:4] * dc
        tail_src = jnp.concatenate([dc[TM - 8:TM], dcnext_sc[...]], axis=0)
        dx_tail = cw[3:4] * dc[TM - 8:TM]
        for d in (1, 2, 3):
            dx = dx + cw[3 - d:4 - d] * pltpu.roll(dc, TM - d, axis=0)
            dx_tail = dx_tail + cw[3 - d:4 - d] * pltpu.roll(tail_src, 16 - d, axis=0)[0:8]
            rolled, hd = _shift_down(x, prev8, d)
            b_sc[...] = rolled
            b_sc[0:8, :] = hd
            stats_ref[7 - d:8 - d, :] += jnp.sum(dc * b_sc[...], axis=0, keepdims=True)
        dcnext_sc[...] = dc[0:8]
        dproj_ref[...] = dx.astype(BF16)
        dproj_ref[TM - 8:TM, :] = dx_tail.astype(BF16)

    rev = lambda s: nt - 1 - s
    vec = pl.BlockSpec((1, D), lambda s: (0, 0))
    wspec = pl.BlockSpec((LRU_BLOCKS, LRU_BLOCK, LRU_BLOCK), lambda s: (0, 0, 0))
    prev16 = lambda col: pl.BlockSpec((BF16_ROWS, D), lambda s: (jnp.maximum(rev(s) * (TM // BF16_ROWS) - 1, 0), col))
    tile = lambda: pl.BlockSpec((TM, D), lambda s: (rev(s), 0))
    h_lru, c_sv, r_sv, i_sv, la_sv, mult_sv = saved
    return _hosted_call(
        body,
        name="lru_bwd",
        grid=(nt,),
        in_specs=[
            pl.BlockSpec(memory_space=pl.ANY),
            pl.BlockSpec((TM, D), lambda s: (rev(s), LRU_IN_COL)), prev16(LRU_IN_COL),
            tile(), prev16(0),
            tile(), tile(), tile(), tile(), tile(), tile(),
            pl.BlockSpec((4, D), lambda s: (0, 0)),
            wspec, wspec, vec,
        ],
        out_specs=[
            pl.BlockSpec((TM, D), lambda s: (rev(s), LRU_IN_COL)),
            wspec, wspec,
            pl.BlockSpec((8, D), lambda s: (0, 0)),
        ],
        out_shape=[
            jax.ShapeDtypeStruct(dproj.shape, BF16),
            jax.ShapeDtypeStruct((LRU_BLOCKS, LRU_BLOCK, LRU_BLOCK), F32),
            jax.ShapeDtypeStruct((LRU_BLOCKS, LRU_BLOCK, LRU_BLOCK), F32),
            jax.ShapeDtypeStruct((8, D), F32),
        ],
        aliases={0: 0},
        scratch_shapes=[
            pltpu.VMEM((8, D), F32), pltpu.VMEM((8, D), F32), pltpu.VMEM((8, D), F32),
            pltpu.VMEM((TM, D), F32), pltpu.VMEM((TM, D), F32), pltpu.VMEM((TM, D), F32),
        ],
        args=(dproj, proj, proj, h_lru, h_lru, c_sv, r_sv, i_sv, la_sv, mult_sv, dhl, conv_w, wa, wx, lam),
        riders=riders,
    )


def _in_proj_bwd(dproj, w_in, part, prev=None, riders=()):
    rows = dproj.shape[0]
    tm = _heavy_tile(rows)
    nt = rows // tm
    split = max(1, nt // 4 + 1) if nt > 1 else 1
    first = 0 if part == 0 else split
    count = split if part == 0 else nt - split

    def body(*refs):
        dproj_ref, w_hbm, du_ref, w_sc, w_sem = refs[-5:]

        @pl.when(pl.program_id(0) == 0)
        def _():
            moves = [((0, 3), 0), ((4, 1), 3), ((3, 1), 4), ((5, 3), 5)]
            copies = [pltpu.make_async_copy(w_hbm.at[:, pl.ds(src * D, n * D)], w_sc.at[:, pl.ds(dst * D, n * D)], w_sem.at[q])
                      for q, ((src, n), dst) in enumerate(moves)]
            for cp in copies:
                cp.start()
            for cp in copies:
                cp.wait()

        du_ref[...] = _dot_nt(dproj_ref[...], w_sc[...])

    in_specs = [pl.BlockSpec((tm, IN_COLS), lambda i: (first + i, 0)), _ANY]
    args = (dproj, w_in)
    if prev is not None:
        in_specs = [_ANY] + in_specs
        args = (prev,) + args
    return _hosted_call(
        body,
        name="in_proj_bwd_%d" % part,
        grid=(count,),
        in_specs=in_specs,
        out_specs=[pl.BlockSpec((tm, D), lambda i: (first + i, 0))],
        out_shape=[jax.ShapeDtypeStruct((rows, D), F32)],
        scratch_shapes=[pltpu.VMEM((D, IN_COLS), BF16), pltpu.SemaphoreType.DMA((4,))],
        aliases={0: 0} if prev is not None else None,
        args=args,
        riders=riders,
    )


def _norm1_bwd(du, dh1, head, x2d, rstd1, norm_w, riders=()):
    rows = du.shape[0]

    def body(du_ref, dh1_ref, head_ref, x_ref, rstd_ref, nw_ref, gx_ref, ghead_ref, stats_ref):
        i = pl.program_id(0)

        @pl.when(i == 0)
        def _():
            stats_ref[...] = jnp.zeros_like(stats_ref)

        def finish(h0, out_ref):
            du = du_ref[...]
            rs = rstd_ref[...]
            n = h0 * rs
            stats_ref[0:1, :] += jnp.sum(du * n, axis=0, keepdims=True)
            dn = du * nw_ref[...]
            out_ref[...] = dh1_ref[...] + rs * (dn - n * jnp.mean(dn * n, axis=-1, keepdims=True))

        @pl.when(i == 0)
        def _():
            finish(head_ref[...], ghead_ref)

        @pl.when(i > 0)
        def _():
            finish(x_ref[...], gx_ref)

    tile = pl.BlockSpec((TM, D), lambda i: (i, 0))
    return _hosted_call(
        body,
        name="norm1_bwd",
        grid=(rows // TM,),
        in_specs=[
            tile, tile,
            pl.BlockSpec((FRONT, D), lambda i: (0, 0)),
            pl.BlockSpec((TM, D), lambda i: (jnp.maximum(i - 1, 0), 0)),
            pl.BlockSpec((TM, 1), lambda i: (i, 0)),
            pl.BlockSpec((1, D), lambda i: (0, 0)),
        ],
        out_specs=[
            pl.BlockSpec((TM, D), lambda i: (jnp.maximum(i - 1, 0), 0)),
            pl.BlockSpec((FRONT, D), lambda i: (0, 0)),
            pl.BlockSpec((8, D), lambda i: (0, 0)),
        ],
        out_shape=[
            jax.ShapeDtypeStruct(x2d.shape, F32),
            jax.ShapeDtypeStruct((FRONT, D), F32),
            jax.ShapeDtypeStruct((8, D), F32),
        ],
        scratch_shapes=[],
        args=(du, dh1, head, x2d, rstd1, norm_w),
        riders=riders,
    )


def _matmul_tn(name, x, dy, out_cols, col0=0, prev=None, k_block=None, n_block=None, riders=(), col_map=None):
    col_map = col_map or (lambda n: n)
    rows, kdim = x.shape
    ndim = dy.shape[1]
    kb = k_block or kdim
    nb = n_block or ndim
    step_bytes = lambda t: 2 * t * (kb * x.dtype.itemsize + nb * dy.dtype.itemsize) + 2 * kb * nb * 4
    tr = next(t for t in (2816, 1408, TM_HEAVY, TM) if rows % t == 0 and (t == TM or step_bytes(t) <= TN_VMEM_BUDGET))
    nr, nk, nn = rows // tr, kdim // kb, ndim // nb
    cb0 = col0 // nb

    def body(*refs):
        x_ref, dy_ref, out_ref = refs[-3], refs[-2], refs[-1]
        part = _dot_tn(x_ref[...].astype(BF16), dy_ref[...].astype(BF16))

        @pl.when(pl.program_id(2) == 0)
        def _():
            out_ref[...] = part

        @pl.when(pl.program_id(2) > 0)
        def _():
            out_ref[...] += part

    in_specs = [
        pl.BlockSpec((tr, kb), lambda n, k, r: (r, k)),
        pl.BlockSpec((tr, nb), lambda n, k, r: (r, n)),
    ]
    args = [x, dy]
    aliases = {}
    if prev is not None:
        in_specs = [pl.BlockSpec(memory_space=pl.ANY)] + in_specs
        args = [prev] + args
        aliases = {0: 0}
    (out,), rider_outs = _hosted_call(
        body,
        name=name,
        grid=(nn, nk, nr),
        in_specs=in_specs,
        out_specs=[pl.BlockSpec((kb, nb), lambda n, k, r: (k, cb0 + col_map(n)))],
        out_shape=[jax.ShapeDtypeStruct((kdim, out_cols), F32)],
        scratch_shapes=[],
        aliases=aliases,
        args=args,
        riders=riders,
    )
    return (out, rider_outs) if riders else out


def _local_step(x2d, target, w, plan):
    rows = FRONT + x2d.shape[0]
    head = jnp.concatenate([jnp.zeros((PAD_ROWS, D), F32), w["meta_tokens"]], axis=0)
    cos_t, sin_t = _rope_tables(rows)
    intra, intra_t, q_dec, k_dec, c_dec = _decay_tables()
    grads = {}

    def hosted(host, fn, *args, **kwargs):
        outs, rider_outs = fn(*args, riders=plan.riders(host, w, grads), **kwargs)
        plan.after(host, rider_outs, w, grads)
        return outs

    (u1, rstd1), _ = _norm1(head, x2d, w["mix_norm_w"])
    proj, w["w_in"] = hosted("in_proj", _in_proj, u1, w["w_in_shard"], w["route"], cos_t, sin_t)
    o, sprev = hosted("retention_fwd", _retention_fwd, proj, intra, q_dec, k_dec, c_dec)
    lru_args = (w["conv_w"], w["conv_b"], w["lru_wa"], w["lru_wx"], w["lru_ba"], w["lru_bx"], w["lru_lambda"])
    lru_saved = hosted("lru_fwd", _lru_fwd, proj, *lru_args)
    h_lru = lru_saved[0]
    y_ret, y_lru, h1, u2, rstd2 = hosted("mix_fwd", _mix_fwd, head, x2d, o, proj, h_lru, w["w_branch_ret"],
                                         w["w_branch_lru"], w["w_out"], w["ffn_norm_w"])
    silu, dsilu, act = hosted("ffn_fwd_gate", _ffn_fwd_gate, u2, w["w_ffn_in"])
    dh2, stats_loss = _ffn_out_loss(act, h1, w["w_ffn_out"], target, w["final_norm_w"])

    dg, dup, dh2b, dh1, stats_ffn = _ffn_bwd(dh2, silu, dsilu, h1, rstd2, w["w_ffn_in"], w["w_ffn_out"], w["ffn_norm_w"])
    grads["w_ffn_in"] = _matmul_tn("dw_ffn_up", u2, dup, 2 * FFN, col0=FFN, n_block=FFN_HALF,
                                   prev=_matmul_tn("dw_ffn_gate", u2, dg, 2 * FFN, n_block=FFN_HALF))
    grads["w_ffn_out"] = _matmul_tn("dw_ffn_out", act, dh2b, D, k_block=FFN_HALF)
    (dproj, do, dhl, mixed, a_ret, a_lru, dy_ret, dy_lru) = hosted(
        "mix_bwd", _mix_bwd, dh1, o, proj, h_lru, y_ret, y_lru, w["w_branch_ret"], w["w_branch_lru"], w["w_out"])
    grads["w_out"] = _matmul_tn("dw_out", mixed, dh1, D)
    grads["w_branch_ret"] = _matmul_tn("dw_branch_ret", a_ret, dy_ret, D)
    grads["w_branch_lru"] = _matmul_tn("dw_branch_lru", a_lru, dy_lru, D)
    (dproj,) = hosted("retention_bwd", _retention_bwd, dproj, proj, do, sprev, intra, intra_t, q_dec, k_dec, c_dec,
                      cos_t, sin_t)
    dproj, grads["lru_wa"], grads["lru_wx"], stats_lru = hosted(
        "lru_bwd", _lru_bwd, dproj, proj, lru_saved, dhl, w["conv_w"], w["lru_wa"], w["lru_wx"], w["lru_lambda"])
    grads["w_in"], rider_outs = _matmul_tn("dw_in", u1, dproj, IN_COLS, n_block=D, col_map=_swap_3_4,
                                           riders=plan.riders("dw_in", w, grads))
    plan.after("dw_in", rider_outs, w, grads)
    (du1,) = hosted("in_proj_bwd_0", _in_proj_bwd, dproj, w["w_in"], 0)
    (du1,) = hosted("in_proj_bwd_1", _in_proj_bwd, dproj, w["w_in"], 1, du1)
    (grad_x, grad_head, stats_in), _ = _norm1_bwd(du1, dh1, head, x2d, rstd1, w["mix_norm_w"])
    return grad_x, grad_head, grads, [stats_loss, stats_ffn, stats_in, stats_lru]


BIG_PIECES = {
    "w_in": ("col", (D, 2 * D)),
    "w_ffn_in": ("col", (D, FFN_HALF)),
    "w_ffn_out": ("row", (FFN // 4, D)),
    "w_branch_ret": ("row", (D // 4, D)),
    "w_branch_lru": ("row", (D // 4, D)),
    "w_out": ("row", (D // 4, D)),
    "lru_wa": ("lru", (LRU_BLOCKS, LRU_BLOCK // 4, LRU_BLOCK)),
    "lru_wx": ("lru", (LRU_BLOCKS, LRU_BLOCK // 4, LRU_BLOCK)),
}
SMALL_PIECES = {"meta_tokens": ("col", (N_META, D // 4)), "conv_w": ("col", (4, D // 4))}


def _full_shape(kind, shard):
    if kind == "col":
        return (shard[0], 4 * shard[1])
    if kind == "row":
        return (4 * shard[0], shard[1])
    return (shard[0], 4 * shard[1], shard[2])


def _half_shape(kind, shard):
    return (shard[0] // 2,) + tuple(shard[1:])


def _aligned(start, multiple):
    return start if isinstance(start, int) else pl.multiple_of(start, multiple)


def _lead(h, size):
    if h is None:
        return pl.ds(0, size)
    return pl.ds(_aligned(h * (size // 2), size // 2), size // 2)


def _full_region(ref, kind, shard, s, h):
    if kind == "col":
        return ref.at[_lead(h, shard[0]), pl.ds(_aligned(s * shard[1], shard[1]), shard[1])]
    if kind == "row":
        size = shard[0] if h is None else shard[0] // 2
        start = s * shard[0] + (0 if h is None else h * (shard[0] // 2))
        return ref.at[pl.ds(_aligned(start, BF16_ROWS), size), :]
    return ref.at[_lead(h, shard[0]), pl.ds(_aligned(s * shard[1], shard[1]), shard[1]), :]


def _shard_region(ref, shard, h):
    return ref.at[_lead(h, shard[0])]


def _place():
    x, y, c = lax.axis_index("x"), lax.axis_index("y"), lax.axis_index("c")
    return x, y, c, 2 * x + y


def _other_chip(s, c, k):
    s2 = jnp.bitwise_xor(s, k)
    return s2, (s2 // 2, s2 % 2, c)


def _remote(src, dst, send_sem, recv_sem, dev):
    return pltpu.make_async_remote_copy(src_ref=src, dst_ref=dst, send_sem=send_sem, recv_sem=recv_sem,
                                        device_id=dev, device_id_type=MESH)


_ANY = pl.BlockSpec(memory_space=pl.ANY)


class _Rider:
    def __init__(self, ins, out_shapes, sem_shapes, build, aliased=False):
        self.ins, self.out_shapes, self.sem_shapes, self.build, self.aliased = ins, out_shapes, sem_shapes, build, aliased


def _hosted_call(body, *, name, grid, in_specs, out_specs, out_shape, scratch_shapes, args, riders=(), aliases=None,
                 prefetch=None, riders_after_body=False):
    n_in, n_out, n_sc = len(in_specs), len(out_shape), len(scratch_shapes)
    r_in = [a for r in riders for a in r.ins]
    r_out = [s for r in riders for s in r.out_shapes]
    r_sem = [s for r in riders for s in r.sem_shapes]
    lead = () if prefetch is None else (prefetch,)
    assert prefetch is None or not (aliases or any(r.aliased for r in riders))

    def full_body(*refs):
        head, refs = refs[:len(lead)], refs[len(lead):]
        ins, rin = refs[:n_in], refs[n_in:n_in + len(r_in)]
        o0 = n_in + len(r_in)
        outs, rout = refs[o0:o0 + n_out], refs[o0 + n_out:o0 + n_out + len(r_out)]
        s0 = o0 + n_out + len(r_out)
        scratch, rsem = refs[s0:s0 + n_sc], refs[s0 + n_sc:]
        starts, waits = [], []
        pi = po = ps = 0
        for r in riders:
            st, wt = r.build(rin[pi:pi + len(r.ins)], rout[po:po + len(r.out_shapes)], rsem[ps:ps + len(r.sem_shapes)])
            starts += st
            waits += wt
            pi, po, ps = pi + len(r.ins), po + len(r.out_shapes), ps + len(r.sem_shapes)
        first = functools.reduce(jnp.logical_and, [pl.program_id(d) == 0 for d in range(len(grid))])
        last = functools.reduce(jnp.logical_and, [pl.program_id(d) == grid[d] - 1 for d in range(len(grid))])

        def start_riders():
            @pl.when(first)
            def _():
                for cp in starts:
                    cp.start()

        if riders and not riders_after_body:
            start_riders()
        body(*head, *ins, *outs, *scratch)
        if riders and riders_after_body:
            start_riders()
        if riders:
            @pl.when(last)
            def _():
                for wait in waits:
                    wait()

    io_aliases = dict(aliases or {})
    pi = po = 0
    for r in riders:
        if r.aliased:
            for q in range(len(r.ins)):
                io_aliases[n_in + pi + q] = n_out + po + q
        pi, po = pi + len(r.ins), po + len(r.out_shapes)
    specs = dict(
        grid=grid,
        in_specs=list(in_specs) + [_ANY] * len(r_in),
        out_specs=list(out_specs) + [_ANY] * len(r_out),
        scratch_shapes=list(scratch_shapes) + r_sem,
    )
    if prefetch is not None:
        specs = dict(grid_spec=pltpu.PrefetchScalarGridSpec(num_scalar_prefetch=1, **specs))
    res = pl.pallas_call(
        full_body,
        name=name,
        out_shape=list(out_shape) + r_out,
        input_output_aliases=io_aliases,
        compiler_params=_params(("arbitrary",) * len(grid)),
        **specs,
    )(*lead, *args, *r_in)
    rider_outs, po = [], n_out
    for r in riders:
        rider_outs.append(list(res[po:po + len(r.out_shapes)]))
        po += len(r.out_shapes)
    return list(res[:n_out]), rider_outs


def _run_riders(name, riders):
    r_in = [a for r in riders for a in r.ins]
    r_out = [s for r in riders for s in r.out_shapes]
    r_sem = [s for r in riders for s in r.sem_shapes]

    def body(*refs):
        rin, rout, rsem = refs[:len(r_in)], refs[len(r_in):len(r_in) + len(r_out)], refs[len(r_in) + len(r_out):]
        pi = po = ps = 0
        all_waits = []
        for r in riders:
            starts, waits = r.build(rin[pi:pi + len(r.ins)], rout[po:po + len(r.out_shapes)], rsem[ps:ps + len(r.sem_shapes)])
            for cp in starts:
                cp.start()
            all_waits += waits
            pi, po, ps = pi + len(r.ins), po + len(r.out_shapes), ps + len(r.sem_shapes)
        for wait in all_waits:
            wait()

    io_aliases = {}
    pi = po = 0
    for r in riders:
        if r.aliased:
            for q in range(len(r.ins)):
                io_aliases[pi + q] = po + q
        pi, po = pi + len(r.ins), po + len(r.out_shapes)
    res = pl.pallas_call(
        body,
        name=name,
        in_specs=[_ANY] * len(r_in),
        out_specs=[_ANY] * len(r_out),
        out_shape=r_out,
        scratch_shapes=r_sem,
        input_output_aliases=io_aliases,
    )(*r_in)
    outs, po = [], 0
    for r in riders:
        outs.append(list(res[po:po + len(r.out_shapes)]))
        po += len(r.out_shapes)
    return outs


def _piece(name):
    if name in BIG_PIECES:
        return (name, *BIG_PIECES[name], True)
    return (name, *SMALL_PIECES[name], False)


def _gather_rider(shards, names):
    pieces = [_piece(n) for n in names]
    n = len(pieces)

    def build(ins, outs, sems):
        local_sem, ici_send, ici_recv = sems
        _, _, c, s = _place()
        starts, waits = [], []
        for p, (_, kind, shard, split) in enumerate(pieces):
            cp = pltpu.make_async_copy(ins[p], _full_region(outs[p], kind, shard, s, None), local_sem.at[p])
            starts.append(cp)
            waits.append(cp.wait)
            h = c if split else None
            for k in (1, 2, 3):
                s2, dev = _other_chip(s, c, k)
                cp = _remote(_shard_region(ins[p], shard, h), _full_region(outs[p], kind, shard, s, h),
                             ici_send.at[p, k - 1], ici_recv.at[p, k - 1], dev)
                starts.append(cp)
                waits.append(cp.wait_send)
                region = _full_region(outs[p], kind, shard, s2, h)
                waits.append(_remote(region, region, ici_send.at[p, k - 1], ici_recv.at[p, k - 1], dev).wait_recv)
        return starts, waits

    return _Rider(
        [shards[name] for name in names],
        [jax.ShapeDtypeStruct(_full_shape(kind, shard), shards[name].dtype) for name, kind, shard, _ in pieces],
        [pltpu.SemaphoreType.DMA((n,)), pltpu.SemaphoreType.DMA((n, 3)), pltpu.SemaphoreType.DMA((n, 3))],
        build)


def _forward_rider(gathered, names):
    pieces = [_piece(n) for n in names]
    n = len(pieces)

    def build(ins, outs, sems):
        fwd_send, fwd_recv = sems
        x, y, c, s = _place()
        sibling = (x, y, 1 - c)
        starts, waits = [], []
        for p, (_, kind, shard, _) in enumerate(pieces):
            for k in (1, 2, 3):
                s2, _ = _other_chip(s, c, k)
                mine = _full_region(outs[p], kind, shard, s2, c)
                theirs = _full_region(outs[p], kind, shard, s2, 1 - c)
                cp = _remote(mine, mine, fwd_send.at[p, k - 1], fwd_recv.at[p, k - 1], sibling)
                starts.append(cp)
                waits.append(cp.wait_send)
                waits.append(_remote(theirs, theirs, fwd_send.at[p, k - 1], fwd_recv.at[p, k - 1], sibling).wait_recv)
        return starts, waits

    return _Rider(
        [gathered[name] for name in names],
        [jax.ShapeDtypeStruct(gathered[name].shape, gathered[name].dtype) for name in names],
        [pltpu.SemaphoreType.DMA((n, 3)), pltpu.SemaphoreType.DMA((n, 3))],
        build, aliased=True)


def _pair_exchange_rider(grads, names):
    n = len(names)

    def build(ins, outs, sems):
        send_sem, recv_sem = sems
        x, y, c, _ = _place()
        sibling = (x, y, 1 - c)
        starts, waits = [], []
        for p, name in enumerate(names):
            kind, shard = BIG_PIECES[name]
            for s2 in range(4):
                cp = _remote(_full_region(ins[p], kind, shard, s2, 1 - c), outs[p].at[s2], send_sem.at[p, s2],
                             recv_sem.at[p, s2], sibling)
                starts.append(cp)
                waits.append(cp.wait_send)
                waits.append(_remote(outs[p].at[s2], outs[p].at[s2], send_sem.at[p, s2], recv_sem.at[p, s2], sibling).wait_recv)
        return starts, waits

    return _Rider(
        [grads[name] for name in names],
        [jax.ShapeDtypeStruct((4,) + _half_shape(*BIG_PIECES[name]), F32) for name in names],
        [pltpu.SemaphoreType.DMA((n, 4))] * 2,
        build)


def _half_specs(kind, shard):
    half = shard[0] // 2
    if kind == "col":
        full = pl.BlockSpec((half, shard[1]), lambda j, pr: (pr[1], j))
        buf = pl.BlockSpec((None, half, shard[1]), lambda j, pr: (j, 0, 0))
    elif kind == "row":
        full = pl.BlockSpec((half, shard[1]), lambda j, pr: (2 * j + pr[1], 0))
        buf = pl.BlockSpec((None, half, shard[1]), lambda j, pr: (j, 0, 0))
    else:
        full = pl.BlockSpec((half, shard[1], shard[2]), lambda j, pr: (pr[1], j, 0))
        buf = pl.BlockSpec((None, half, shard[1], shard[2]), lambda j, pr: (j, 0, 0, 0))
    return full, buf


def _pair_sum(name, grad, recv, place):
    kind, shard = BIG_PIECES[name]
    full, buf = _half_specs(kind, shard)

    def body(pr, g_ref, r_ref, o_ref):
        o_ref[...] = (g_ref[...] + r_ref[...]).astype(BF16)

    return pl.pallas_call(
        body,
        name="pair_sum_" + name,
        grid_spec=pltpu.PrefetchScalarGridSpec(num_scalar_prefetch=1, grid=(4,), in_specs=[full, buf], out_specs=buf),
        out_shape=jax.ShapeDtypeStruct((4,) + _half_shape(kind, shard), BF16),
        compiler_params=_params(("arbitrary",)),
    )(place, grad, recv)


def _chip_exchange_rider(sums, names):
    n = len(names)

    def build(ins, outs, sems):
        send_sem, recv_sem = sems
        _, _, c, s = _place()
        starts, waits = [], []
        for p in range(n):
            for k in (1, 2, 3):
                s2, dev = _other_chip(s, c, k)
                cp = _remote(ins[p].at[s2], outs[p].at[k - 1], send_sem.at[p, k - 1], recv_sem.at[p, k - 1], dev)
                starts.append(cp)
                waits.append(cp.wait_send)
                waits.append(_remote(outs[p].at[k - 1], outs[p].at[k - 1], send_sem.at[p, k - 1], recv_sem.at[p, k - 1],
                                     dev).wait_recv)
        return starts, waits

    return _Rider(
        [sums[name] for name in names],
        [jax.ShapeDtypeStruct((3,) + _half_shape(*BIG_PIECES[name]), BF16) for name in names],
        [pltpu.SemaphoreType.DMA((n, 3))] * 2,
        build)


def _chip_sum(name, grad, recv_pair, recv_chip, place):
    kind, shard = BIG_PIECES[name]
    half = shard[0] // 2
    tail = tuple(shard[1:])
    zeros = (0,) * len(tail)
    nt = 4 if kind != "lru" and half % (4 * BF16_ROWS) == 0 else 1
    rows = half // nt
    if kind == "col":
        full = pl.BlockSpec((rows,) + tail, lambda j, pr: (pr[1] * nt + j, pr[0]))
    elif kind == "row":
        full = pl.BlockSpec((rows,) + tail, lambda j, pr: ((2 * pr[0] + pr[1]) * nt + j, 0))
    else:
        full = pl.BlockSpec((rows,) + tail, lambda j, pr: (pr[1], pr[0], 0))
    pair = pl.BlockSpec((None, rows) + tail, lambda j, pr: (pr[0], j) + zeros)
    chip = pl.BlockSpec((3, rows) + tail, lambda j, pr: (0, j) + zeros)
    out = pl.BlockSpec((rows,) + tail, lambda j, pr: (pr[1] * nt + j,) + zeros)

    def body(pr, g_ref, rp_ref, rc_ref, o_ref):
        total = g_ref[...] + rp_ref[...]
        for k in range(3):
            total = total + rc_ref[k].astype(F32)
        o_ref[...] = total

    return pl.pallas_call(
        body,
        name="chip_sum_" + name,
        grid_spec=pltpu.PrefetchScalarGridSpec(num_scalar_prefetch=1, grid=(nt,), in_specs=[full, pair, chip], out_specs=out),
        out_shape=jax.ShapeDtypeStruct(shard, F32),
        compiler_params=_params(("arbitrary",)),
    )(place, grad, recv_pair, recv_chip)


def _sibling_exchange_rider(halves, names):
    n = len(names)

    def build(ins, outs, sems):
        send_sem, recv_sem = sems
        x, y, c, _ = _place()
        sibling = (x, y, 1 - c)
        starts, waits = [], []
        for p, name in enumerate(names):
            shard = BIG_PIECES[name][1]
            mine = _shard_region(outs[p], shard, c)
            theirs = _shard_region(outs[p], shard, 1 - c)
            cp = _remote(mine, mine, send_sem.at[p], recv_sem.at[p], sibling)
            starts.append(cp)
            waits.append(cp.wait_send)
            waits.append(_remote(theirs, theirs, send_sem.at[p], recv_sem.at[p], sibling).wait_recv)
        return starts, waits

    return _Rider(
        [halves[name] for name in names],
        [jax.ShapeDtypeStruct(BIG_PIECES[name][1], F32) for name in names],
        [pltpu.SemaphoreType.DMA((n,))] * 2,
        build, aliased=True)


FIRST_WEIGHTS = ["meta_tokens", "conv_w"]
WEIGHT_GROUPS = {
    "lru": ["lru_wa", "lru_wx"],
    "branch": ["w_branch_ret", "w_branch_lru", "w_out"],
    "ffn_in": ["w_ffn_in"],
    "ffn_out": ["w_ffn_out"],
}
WEIGHT_SCHEDULE = {
    "in_proj": [("gather", "lru")],
    "retention_fwd": [("forward", "lru"), ("gather", "branch")],
    "lru_fwd": [("forward", "branch"), ("gather", "ffn_in")],
    "mix_fwd": [("forward", "ffn_in"), ("gather", "ffn_out")],
    "ffn_fwd_gate": [("forward", "ffn_out")],
}
GRAD_GROUPS = {
    "ffn_in": ["w_ffn_in"],
    "ffn_out": ["w_ffn_out"],
    "mixer": ["w_out", "w_branch_ret", "w_branch_lru", "lru_wa", "lru_wx"],
    "in": ["w_in"],
}
GRAD_SCHEDULE = {
    "mix_bwd": [("pair", "ffn_in"), ("pair", "ffn_out")],
    "retention_bwd": [("chip", "ffn_in")],
    "lru_bwd": [("chip", "ffn_out"), ("sibling", "ffn_in")],
    "dw_in": [("sibling", "ffn_out"), ("pair", "mixer")],
    "in_proj_bwd_0": [("chip", "mixer"), ("pair", "in")],
    "in_proj_bwd_1": [("sibling", "mixer"), ("chip", "in")],
}


class _CommPlan:
    def __init__(self, shards, place):
        self.shards, self.place = shards, place
        self.late = {}
        self.recv_pair, self.sums, self.recv_chip, self.halves, self.final = {}, {}, {}, {}, {}

    def _grad_rider(self, stage, group, grads):
        names = GRAD_GROUPS[group]
        if stage == "pair":
            return _pair_exchange_rider(grads, names)
        if stage == "chip":
            return _chip_exchange_rider(self.sums, names)
        return _sibling_exchange_rider(self.halves, names)

    def _grad_after(self, stage, group, outs, grads):
        names = GRAD_GROUPS[group]
        if stage == "pair":
            for n, o in zip(names, outs):
                self.recv_pair[n] = o
                self.sums[n] = _pair_sum(n, grads[n], o, self.place)
        elif stage == "chip":
            for n, o in zip(names, outs):
                self.halves[n] = _chip_sum(n, grads[n], self.recv_pair[n], o, self.place)
        else:
            self.final.update(zip(names, outs))

    def riders(self, host, w, grads):
        if host in WEIGHT_SCHEDULE:
            return [_gather_rider(self.shards, WEIGHT_GROUPS[group]) if stage == "gather"
                    else _forward_rider(self.late, WEIGHT_GROUPS[group]) for stage, group in WEIGHT_SCHEDULE[host]]
        return [self._grad_rider(stage, group, grads) for stage, group in GRAD_SCHEDULE.get(host, [])]

    def after(self, host, rider_outs, w, grads):
        for (stage, group), outs in zip(WEIGHT_SCHEDULE.get(host, []), rider_outs):
            (self.late if stage == "gather" else w).update(zip(WEIGHT_GROUPS[group], outs))
        for (stage, group), outs in zip(GRAD_SCHEDULE.get(host, []), rider_outs):
            self._grad_after(stage, group, outs, grads)

    def finish(self, partial):
        outs, (blocks,) = _run_riders("tail_exchange", [_sibling_exchange_rider(self.halves, GRAD_GROUPS["in"]),
                                                        _small_exchange_rider(partial)])
        self.final.update(zip(GRAD_GROUPS["in"], outs))
        return self.final, blocks


def _adamw_math(w, g, m, v):
    m = ADAM_B1 * m + (1.0 - ADAM_B1) * g
    v = ADAM_B2 * v + (1.0 - ADAM_B2) * (g * g)
    m_hat = m / (1.0 - ADAM_B1 ** ADAM_STEP)
    v_hat = v / (1.0 - ADAM_B2 ** ADAM_STEP)
    delta = -ADAM_LR * (m_hat / (jnp.sqrt(v_hat) + ADAM_EPS) + ADAM_WD * w)
    return delta, m, v


def _adamw(name, g, w, m, v):
    rows, cols = g.shape
    tr = rows // 4 if rows % 32 == 0 else rows

    def body(g_ref, w_ref, m_ref, v_ref, go_ref, d_ref, mo_ref, vo_ref):
        gv = g_ref[...]
        delta, m2, v2 = _adamw_math(w_ref[...], gv, m_ref[...], v_ref[...])
        go_ref[...] = gv
        d_ref[...] = delta
        mo_ref[...] = m2
        vo_ref[...] = v2

    spec = pl.BlockSpec((tr, cols), lambda i: (i, 0))
    return pl.pallas_call(
        body,
        name="adamw_" + name,
        grid=(rows // tr,),
        in_specs=[spec] * 4,
        out_specs=[spec] * 4,
        out_shape=[jax.ShapeDtypeStruct((rows, cols), F32)] * 4,
        compiler_params=_params(("arbitrary",)),
    )(g, w, m, v)


SMALL_ROWS = 48
VEC_ROWS = {"final_norm_w": 1, "ffn_norm_w": 8, "mix_norm_w": 16, "conv_b": 24, "lru_ba": 25, "lru_bx": 26, "lru_lambda": 27}
VEC_NAMES = list(VEC_ROWS)
CONV_W_ROW = 28
META_ROW = 32


def _small_exchange_rider(partial):
    def build(ins, outs, sems):
        local_sem, send_sem, recv_sem = sems
        _, _, c, s = _place()
        me = 2 * s + c
        cp = pltpu.make_async_copy(ins[0], outs[0].at[me], local_sem)
        starts, waits = [cp], [cp.wait]
        for k in range(1, 8):
            peer = jnp.bitwise_xor(me, k)
            dev = (peer // 4, (peer // 2) % 2, peer % 2)
            rd = _remote(ins[0], outs[0].at[me], send_sem.at[k - 1], recv_sem.at[k - 1], dev)
            starts.append(rd)
            waits.append(rd.wait_send)
            waits.append(_remote(ins[0], outs[0].at[peer], send_sem.at[k - 1], recv_sem.at[k - 1], dev).wait_recv)
        return starts, waits

    return _Rider([partial], [jax.ShapeDtypeStruct((8, SMALL_ROWS, D), F32)],
                  [pltpu.SemaphoreType.DMA, pltpu.SemaphoreType.DMA((7,)), pltpu.SemaphoreType.DMA((7,))], build)


def _small_update(blocks, place, vecs, conv, meta):
    nvec = len(VEC_NAMES)
    qcols = D // 4

    def in_order(ref):
        total = ref[0]
        for d in range(1, 8):
            total = total + ref[d]
        return total

    def body(pr, blocks_ref, cols_ref, *refs):
        vec_refs = refs[:3 * nvec]
        conv_refs = refs[3 * nvec:3 * nvec + 3]
        meta_refs = refs[3 * nvec + 3:3 * nvec + 6]
        outs = refs[3 * nvec + 6:]
        loss_ref, vec_out, conv_out, meta_out = outs[0], outs[1:5], outs[5:9], outs[9:13]
        tot, col = in_order(blocks_ref), in_order(cols_ref)
        loss_ref[...] = jnp.sum(tot[0:1, :], axis=1, keepdims=True)
        for o in vec_out:
            o[...] = jnp.zeros_like(o)
        for j, name in enumerate(VEC_NAMES):
            w, m, v = (r[...] for r in vec_refs[3 * j:3 * j + 3])
            g = tot[VEC_ROWS[name]:VEC_ROWS[name] + 1, :]
            if name == "lru_lambda":
                g = -g / (1.0 + jnp.exp(w))
            for o, val in zip(vec_out, (g,) + _adamw_math(w, g, m, v)):
                o[j:j + 1, :] = val
        for row, n_rows, ins, group in ((CONV_W_ROW, 4, conv_refs, conv_out), (META_ROW, N_META, meta_refs, meta_out)):
            g = col[row:row + n_rows, :]
            w, m, v = (r[...] for r in ins)
            for o, val in zip(group, (g,) + _adamw_math(w, g, m, v)):
                o[...] = val

    whole = lambda shape: pl.BlockSpec(shape, lambda i, pr: (0,) * len(shape))
    in_specs = [whole((8, SMALL_ROWS, D)), pl.BlockSpec((8, SMALL_ROWS, qcols), lambda i, pr: (0, 0, pr[0]))]
    in_specs += [whole((1, D))] * (3 * nvec) + [whole((4, qcols))] * 3 + [whole((N_META, qcols))] * 3
    out_shapes = [(1, 1)] + [(8, D)] * 4 + [(4, qcols)] * 4 + [(N_META, qcols)] * 4
    return pl.pallas_call(
        body,
        name="small_update",
        grid_spec=pltpu.PrefetchScalarGridSpec(num_scalar_prefetch=1, grid=(1,), in_specs=in_specs,
                                               out_specs=[whole(s) for s in out_shapes]),
        out_shape=[jax.ShapeDtypeStruct(s, F32) for s in out_shapes],
        compiler_params=_params(("arbitrary",)),
    )(place, blocks, blocks, *[a for t in vecs for a in t], *conv, *meta)


WEIGHT_ORDER = ["meta_tokens", "mix_norm_w", "w_in", "conv_w", "conv_b", "lru_wa", "lru_ba", "lru_wx", "lru_bx", "lru_lambda",
                "w_branch_ret", "w_branch_lru", "w_out", "ffn_norm_w", "w_ffn_in", "w_ffn_out", "final_norm_w"]


def kernel(x, meta_tokens, mix_norm_w, w_in, conv_w, conv_b, lru_wa, lru_ba, lru_wx, lru_bx, lru_lambda, w_branch_ret, w_branch_lru, w_out, ffn_norm_w, w_ffn_in, w_ffn_out, final_norm_w, loss_target, m_meta_tokens, m_mix_norm_w, m_w_in, m_conv_w, m_conv_b, m_lru_wa, m_lru_ba, m_lru_wx, m_lru_bx, m_lru_lambda, m_w_branch_ret, m_w_branch_lru, m_w_out, m_ffn_norm_w, m_w_ffn_in, m_w_ffn_out, m_final_norm_w, v_meta_tokens, v_mix_norm_w, v_w_in, v_conv_w, v_conv_b, v_lru_wa, v_lru_ba, v_lru_wx, v_lru_bx, v_lru_lambda, v_w_branch_ret, v_w_branch_lru, v_w_out, v_ffn_norm_w, v_w_ffn_in, v_w_ffn_out, v_final_norm_w):
    args = locals()
    wts = {n: args[n] for n in WEIGHT_ORDER}
    mom = {n: args["m_" + n] for n in WEIGHT_ORDER}
    var = {n: args["v_" + n] for n in WEIGHT_ORDER}
    place = jnp.stack([2 * lax.axis_index("x") + lax.axis_index("y"), lax.axis_index("c")]).astype(jnp.int32)

    shards = {n: wts[n][0].astype(BF16) for n in BIG_PIECES}
    shards["meta_tokens"] = wts["meta_tokens"]
    shards["conv_w"] = wts["conv_w"][0]
    plan = _CommPlan(shards, place)
    (first,) = _run_riders("gather_first", [_gather_rider(shards, FIRST_WEIGHTS)])
    w = dict(zip(FIRST_WEIGHTS, first))
    for n in VEC_NAMES:
        w[n] = wts[n].reshape(1, D)
    chips = jnp.bitwise_xor(place[0], jnp.arange(4, dtype=jnp.int32))
    w["route"] = jnp.concatenate([place, jnp.stack([2 * chips, 2 * chips + 1], axis=1).reshape(8)])
    w["w_in_shard"] = shards["w_in"]

    grad_x, grad_head, _, stats = _local_step(x[0], loss_target[0], w, plan)
    partial = jnp.concatenate(stats + [grad_head[PAD_ROWS:]], axis=0)
    shard_grads, blocks = plan.finish(partial)

    out = {}
    for n in BIG_PIECES:
        shape2d = (-1, wts[n].shape[-1])
        res = _adamw(n, *[a.reshape(shape2d) for a in (shard_grads[n], wts[n], mom[n], var[n])])
        out[n] = [r.reshape(wts[n].shape) for r in res]

    vecs = [tuple(a[n].reshape(1, D) for a in (wts, mom, var)) for n in VEC_NAMES]
    conv = tuple(a["conv_w"][0] for a in (wts, mom, var))
    meta = tuple(a["meta_tokens"] for a in (wts, mom, var))
    res = _small_update(blocks, place, vecs, conv, meta)
    loss = res[0].reshape(())
    for j, n in enumerate(VEC_NAMES):
        out[n] = [r[j].reshape(wts[n].shape) for r in res[1:5]]
    out["conv_w"] = [r.reshape(wts["conv_w"].shape) for r in res[5:9]]
    out["meta_tokens"] = list(res[9:13])

    return (loss, grad_x.reshape(x.shape)) + tuple(out[n][kind] for kind in range(4) for n in WEIGHT_ORDER)
```

```python
import functools
import math

import jax
import jax.numpy as jnp
from jax import lax
from jax.experimental import pallas as pl
from jax.experimental.pallas import tpu as pltpu

F32 = jnp.float32
BF16 = jnp.bfloat16

LANES = 128
BF16_ROWS = 16

D = 1024
HEADS = 8
DH = 128
CHUNK = 256
N_META = 16
FRONT = 256
PAD_ROWS = FRONT - N_META
LRU_BLOCKS = 4
LRU_BLOCK = 256
LRU_C = 8.0
FFN = 2816
FFN_HALF = FFN // 2
IN_COLS = 8 * D
ROPE_BASE = 10000.0
EPS = 1e-6
QK_SCALE = DH ** -0.5

ADAM_LR = 0.001
ADAM_B1 = 0.9
ADAM_B2 = 0.999
ADAM_EPS = 1e-08
ADAM_WD = 0.01
ADAM_STEP = 10

TM = 256
TM_HEAVY = 768
MXU_TILE = 256
FFN_SUB = 2 * MXU_TILE
_FFN_SUBS = [(a, min(a + FFN_SUB, FFN)) for a in range(0, FFN, FFN_SUB)]
TM_MIX_BWD = 384
TM_LRU = 384
TM_MIX_FWD = 384
TM_IN_PROJ = 1408
VMEM_LIMIT = 60 * 1024 * 1024
TN_VMEM_BUDGET = 40 * 1024 * 1024

NT_DIMS = (((1,), (1,)), ((), ()))
TN_DIMS = (((0,), (0,)), ((), ()))
MESH = pl.DeviceIdType.MESH


def _params(sem=None):
    if sem is None:
        return pltpu.CompilerParams(vmem_limit_bytes=VMEM_LIMIT)
    return pltpu.CompilerParams(dimension_semantics=sem, vmem_limit_bytes=VMEM_LIMIT)


def _dot(a, b):
    return jnp.dot(a, b, preferred_element_type=F32)


def _dot_nt(a, b):
    return lax.dot_general(a, b, NT_DIMS, preferred_element_type=F32)


def _dot_tn(a, b):
    return lax.dot_general(a, b, TN_DIMS, preferred_element_type=F32)


def _sigmoid(z):
    return 1.0 / (1.0 + jnp.exp(-z))


def _log1p(x):
    return jnp.where(x < 1e-3, x * (1.0 - x * (0.5 - x * (1.0 / 3.0))), jnp.log(1.0 + x))


def _softplus(x):
    return jnp.maximum(x, 0.0) + _log1p(jnp.exp(-jnp.abs(x)))


def _one_minus_square(a, log_a):
    x = 2.0 * log_a
    series = -x * (1.0 + x * (0.5 + x * (1.0 / 6.0)))
    return jnp.where(x > -0.02, series, 1.0 - a * a)


_GELU_K = math.sqrt(2.0 / math.pi)


def _gelu_and_grad(x):
    inner = _GELU_K * (x + 0.044715 * x * x * x)
    t = jnp.tanh(inner)
    val = 0.5 * x * (1.0 + t)
    grad = 0.5 * (1.0 + t) + 0.5 * x * (1.0 - t * t) * _GELU_K * (1.0 + 3.0 * 0.044715 * x * x)
    return val, grad


def _row_ids(i, rows, shape):
    return i * rows + lax.broadcasted_iota(jnp.int32, shape, 0)


def _rope_tables(rows):
    inv_freq = ROPE_BASE ** (-jnp.arange(0, DH, 2, dtype=F32) / DH)
    block_pos = jnp.arange(rows // FRONT, dtype=jnp.int32) * FRONT - PAD_ROWS
    coarse = block_pos.astype(F32)[:, None] * inv_freq[None, :]
    fine = jnp.arange(FRONT, dtype=F32)[:, None] * inv_freq[None, :]
    ca, sa = jnp.cos(coarse)[:, None, :], jnp.sin(coarse)[:, None, :]
    cb, sb = jnp.cos(fine)[None, :, :], jnp.sin(fine)[None, :, :]
    cos = (ca * cb - sa * sb).reshape(rows, DH // 2)
    sin = (sa * cb + ca * sb).reshape(rows, DH // 2)
    return jnp.concatenate([cos, cos], axis=1), jnp.concatenate([-sin, sin], axis=1)


def _decay_tables():
    log_g = jnp.log(1.0 - 2.0 ** (-5.0 - jnp.arange(HEADS, dtype=F32)))
    idx = jnp.arange(CHUNK, dtype=F32)
    diff = idx[:, None] - idx[None, :]
    intra = jnp.where(diff[None] >= 0, jnp.exp(jnp.maximum(diff, 0.0)[None] * log_g[:, None, None]), 0.0)
    q_decay = jnp.exp((idx + 1.0)[:, None] * log_g[None, :])
    k_decay = jnp.exp((CHUNK - 1.0 - idx)[:, None] * log_g[None, :])
    chunk_decay = jnp.exp(CHUNK * log_g)
    wide = lambda a: jnp.repeat(a, DH, axis=-1)
    return intra, jnp.swapaxes(intra, 1, 2), wide(q_decay), wide(k_decay), wide(chunk_decay[None, :])


def _norm1(head, x2d, norm_w, riders=()):
    rows = FRONT + x2d.shape[0]
    tm = TM_IN_PROJ if rows % TM_IN_PROJ == 0 else TM_MIX_FWD
    nt = rows // tm

    def body(head_hbm, x_hbm, nw_ref, u_ref, rstd_ref, h0_sc, h0_sem):
        slot = _frame_rows(head_hbm, x_hbm, h0_sc, h0_sem, pl.program_id(0), tm, nt)
        hv = h0_sc[slot]
        rs = lax.rsqrt(jnp.mean(hv * hv, axis=-1, keepdims=True) + EPS)
        u_ref[...] = ((hv * rs) * nw_ref[...]).astype(BF16)
        rstd_ref[...] = rs

    return _hosted_call(
        body,
        name="norm1",
        grid=(nt,),
        in_specs=[_ANY, _ANY, pl.BlockSpec((1, D), lambda i: (0, 0))],
        out_specs=[pl.BlockSpec((tm, D), lambda i: (i, 0)), pl.BlockSpec((tm, 1), lambda i: (i, 0))],
        out_shape=[jax.ShapeDtypeStruct((rows, D), BF16), jax.ShapeDtypeStruct((rows, 1), F32)],
        scratch_shapes=[pltpu.VMEM((2, tm, D), F32), pltpu.SemaphoreType.DMA((3,))],
        args=(head, x2d, norm_w),
        riders=riders,
    )


def _heavy_tile(rows):
    return TM_HEAVY if rows % TM_HEAVY == 0 else TM


def _lru_tile(rows):
    return TM_LRU if rows % TM_LRU == 0 else TM


def _swap_3_4(group):
    return jnp.where(group == 3, 4, jnp.where(group == 4, 3, group))


SHARD_ORDER = (2, 1, 3)
LRU_IN_COL = 3
GATES_COL = 1


def _in_proj(u, w_shard, route, cos_t, sin_t, riders=()):
    rows = u.shape[0]
    tm = TM_IN_PROJ if rows % TM_IN_PROJ == 0 else _heavy_tile(rows)
    nt = rows // tm
    kind, shard = BIG_PIECES["w_in"]

    def body(route_ref, u_hbm, wsh_ref, cos_ref, sin_ref, proj_ref, wfull_ref,
             u_sc, w_sc, u_sem, w_sem, local_sem, ici_send, ici_recv, fwd_send, fwd_recv):
        g, i = pl.program_id(0), pl.program_id(1)
        s, c = route_ref[0], route_ref[1]
        gid = route_ref[2 + g]
        sibling = (s // 2, s % 2, 1 - c)
        local = pltpu.make_async_copy(wsh_ref, _full_region(wfull_ref, kind, shard, s, None), local_sem)
        u_copies = [pltpu.make_async_copy(u_hbm.at[pl.ds(t * tm, tm)], u_sc.at[t], u_sem.at[t]) for t in range(nt)]
        sends, arrivals = [], []
        for k in (1, 2, 3):
            s2, dev = _other_chip(s, c, k)
            sends.append(_remote(_shard_region(wsh_ref, shard, c), _full_region(wfull_ref, kind, shard, s, c),
                                 ici_send.at[k - 1], ici_recv.at[k - 1], dev))
            mine = _full_region(wfull_ref, kind, shard, s2, c)
            theirs = _full_region(wfull_ref, kind, shard, s2, 1 - c)
            arrivals.append((_remote(mine, mine, ici_send.at[k - 1], ici_recv.at[k - 1], dev),
                             _remote(mine, mine, fwd_send.at[k - 1], fwd_recv.at[k - 1], sibling),
                             _remote(theirs, theirs, fwd_send.at[k - 1], fwd_recv.at[k - 1], sibling)))

        slot = g % 2
        last_tile = i == nt - 1

        def block_copy(src, col, to_slot):
            return pltpu.make_async_copy(src.at[:, pl.ds(pl.multiple_of(col * D, D), D)], w_sc.at[to_slot],
                                         w_sem.at[to_slot])

        @pl.when(jnp.logical_and(g == 0, i == 0))
        def _():
            for cp in sends:
                cp.start()
            local.start()
            for cp in u_copies:
                cp.start()
            cp = block_copy(wsh_ref, 0, 0)
            cp.start()
            cp.wait()

        @pl.when(jnp.logical_and(last_tile, g == 0))
        def _():
            block_copy(wsh_ref, 1, 1).start()

        for pos, k in enumerate(SHARD_ORDER, start=1):
            arrived, forward, forwarded = arrivals[k - 1]

            @pl.when(jnp.logical_and(last_tile, g == 2 * pos - 1))
            def _():
                arrived.wait_recv()
                forward.start()
                forwarded.wait_recv()
                block_copy(wfull_ref, route_ref[2 + 2 * pos], 0).start()

            @pl.when(jnp.logical_and(last_tile, g == 2 * pos))
            def _():
                block_copy(wfull_ref, route_ref[3 + 2 * pos], 1).start()

        @pl.when(jnp.logical_and(i == 0, g > 0))
        def _():
            block_copy(wfull_ref, 0, slot).wait()

        for t in range(nt):
            @pl.when(jnp.logical_and(g == 0, i == t))
            def _():
                u_copies[t].wait()

        acc = _dot(u_sc[i], w_sc[slot])

        @pl.when(gid < 2)
        def _():
            scale = jnp.where(gid == 1, QK_SCALE, 1.0).astype(F32)
            for h in range(HEADS):
                sl = slice(h * DH, (h + 1) * DH)
                blk = acc[:, sl]
                out = (blk * cos_ref[...] + pltpu.roll(blk, DH // 2, axis=1) * sin_ref[...]) * scale
                proj_ref[:, sl] = out.astype(BF16)

        @pl.when(gid >= 2)
        def _():
            proj_ref[...] = acc.astype(BF16)

        @pl.when(jnp.logical_and(g == 7, i == nt - 1))
        def _():
            local.wait()
            for cp in sends:
                cp.wait_send()
            for _, forward, _ in arrivals:
                forward.wait_send()

    return _hosted_call(
        body,
        name="in_proj",
        grid=(8, nt),
        in_specs=[
            _ANY, _ANY,
            pl.BlockSpec((tm, DH), lambda g, i, rt: (i, 0)),
            pl.BlockSpec((tm, DH), lambda g, i, rt: (i, 0)),
        ],
        out_specs=[pl.BlockSpec((tm, D), lambda g, i, rt: (i, _swap_3_4(rt[2 + g]))), _ANY],
        out_shape=[jax.ShapeDtypeStruct((rows, IN_COLS), BF16), jax.ShapeDtypeStruct((D, IN_COLS), BF16)],
        scratch_shapes=[
            pltpu.VMEM((nt, tm, D), BF16), pltpu.VMEM((2, D, D), BF16),
            pltpu.SemaphoreType.DMA((nt,)), pltpu.SemaphoreType.DMA((2,)), pltpu.SemaphoreType.DMA,
            pltpu.SemaphoreType.DMA((3,)), pltpu.SemaphoreType.DMA((3,)),
            pltpu.SemaphoreType.DMA((3,)), pltpu.SemaphoreType.DMA((3,)),
        ],
        args=(u, w_shard, cos_t, sin_t),
        riders=riders,
        prefetch=route,
        riders_after_body=True,
    )


def _retention_fwd(proj_bf, intra, q_dec, k_dec, c_dec, riders=()):
    rows = proj_bf.shape[0]
    nc = rows // CHUNK

    def body(q_ref, k_ref, v_ref, m_ref, qd_ref, kd_ref, cd_ref, o_ref, sprev_ref, s_sc):
        @pl.when(pl.program_id(0) == 0)
        def _():
            s_sc[...] = jnp.zeros_like(s_sc)

        for h in range(HEADS):
            sl = slice(h * DH, (h + 1) * DH)
            q, k, v = q_ref[:, sl], k_ref[:, sl], v_ref[:, sl]
            state = s_sc[h]
            state_b = state.astype(BF16)
            sprev_ref[0, h] = state_b
            s = _dot_nt(q, k) * m_ref[h]
            inner = _dot(s.astype(BF16), v)
            cross = _dot(q, state_b) * qd_ref[:, sl]
            o_ref[:, sl] = (inner + cross).astype(BF16)
            k_scaled = (k.astype(F32) * kd_ref[:, sl]).astype(BF16)
            s_sc[h] = state * cd_ref[:, sl] + _dot_tn(k_scaled, v)

    chunk_spec = lambda col: pl.BlockSpec((CHUNK, D), lambda c: (c, col))
    const2 = lambda shape: pl.BlockSpec(shape, lambda c: (0, 0))
    return _hosted_call(
        body,
        name="retention_fwd",
        grid=(nc,),
        in_specs=[
            chunk_spec(0), chunk_spec(1), chunk_spec(2),
            pl.BlockSpec((HEADS, CHUNK, CHUNK), lambda c: (0, 0, 0)),
            const2((CHUNK, D)), const2((CHUNK, D)), const2((1, D)),
        ],
        out_specs=[
            pl.BlockSpec((CHUNK, D), lambda c: (c, 0)),
            pl.BlockSpec((1, HEADS, DH, DH), lambda c: (c, 0, 0, 0)),
        ],
        out_shape=[
            jax.ShapeDtypeStruct((rows, D), BF16),
            jax.ShapeDtypeStruct((nc, HEADS, DH, DH), BF16),
        ],
        scratch_shapes=[pltpu.VMEM((HEADS, DH, DH), F32)],
        args=(proj_bf, proj_bf, proj_bf, intra, q_dec, k_dec, c_dec),
        riders=riders,
    )


def _shift_down(x, first8_prev, d):
    rolled = pltpu.roll(x, d, axis=0)
    head = pltpu.roll(jnp.concatenate([first8_prev, x[0:8]], axis=0), d, axis=0)[8:16]
    return rolled, head


def _conv_and_gates(x, prev8, cw_ref, cb_ref, wa_ref, wx_ref, ba_ref, bx_ref, lam_ref, c_sc):
    cw = cw_ref[...]
    conv = cb_ref[...] + cw[3:4] * x
    head = cb_ref[...] + cw[3:4] * x[0:8]
    for d in (1, 2, 3):
        rolled, hd = _shift_down(x, prev8, d)
        conv = conv + cw[3 - d:4 - d] * rolled
        head = head + cw[3 - d:4 - d] * hd
    c_sc[...] = conv
    c_sc[0:8, :] = head
    c = c_sc[...]
    zr, zi = [], []
    for g in range(LRU_BLOCKS):
        sl = slice(g * LRU_BLOCK, (g + 1) * LRU_BLOCK)
        cg = c[:, sl].astype(BF16)
        zr.append(_dot(cg, wa_ref[g]))
        zi.append(_dot(cg, wx_ref[g]))
    r = _sigmoid(jnp.concatenate(zr, axis=1) + ba_ref[...])
    gate_i = _sigmoid(jnp.concatenate(zi, axis=1) + bx_ref[...])
    sp = _softplus(-lam_ref[...])
    log_a = (-LRU_C) * r * sp
    a = jnp.exp(log_a)
    mult = jnp.sqrt(_one_minus_square(a, log_a))
    return c, r, gate_i, a, mult, log_a


def _lru_fwd(proj, conv_w, conv_b, wa, wx, ba, bx, lam, riders=()):
    rows = proj.shape[0]
    TM = _lru_tile(rows)
    nt = rows // TM

    def body(x_ref, cw_ref, cb_ref, wa_ref, wx_ref, ba_ref, bx_ref, lam_ref,
             h_ref, c_ref, r_ref, i_ref, la_ref, mult_ref, prev_sc, carry_sc, c_sc, a_sc, u_sc, h_sc):
        i = pl.program_id(0)

        @pl.when(i == 0)
        def _():
            prev_sc[...] = jnp.zeros_like(prev_sc)
            carry_sc[...] = jnp.zeros_like(carry_sc)

        x = x_ref[...].astype(F32)
        c, r, gate_i, a, mult, log_a = _conv_and_gates(x, prev_sc[...], cw_ref, cb_ref, wa_ref, wx_ref, ba_ref, bx_ref,
                                                       lam_ref, c_sc)
        for ref, val in ((c_ref, c), (r_ref, r), (i_ref, gate_i), (la_ref, log_a), (mult_ref, mult)):
            ref[...] = val.astype(BF16)
        prev_sc[...] = x[TM - 8:TM]
        valid = _row_ids(i, TM, (TM, D)) >= PAD_ROWS
        a_sc[...] = a
        u_sc[...] = jnp.where(valid, mult * gate_i * c, 0.0)
        row8 = lax.broadcasted_iota(jnp.int32, (8, D), 0)

        def group(gi, hprev):
            r0 = pl.multiple_of(gi * 8, 8)
            aa = a_sc[pl.ds(r0, 8), :]
            uu = u_sc[pl.ds(r0, 8), :]
            for d in (1, 2, 4):
                a_sh = jnp.where(row8 >= d, pltpu.roll(aa, d, axis=0), 1.0)
                u_sh = jnp.where(row8 >= d, pltpu.roll(uu, d, axis=0), 0.0)
                uu = uu + aa * u_sh
                aa = aa * a_sh
            hb = aa * hprev + uu
            h_sc[pl.ds(r0, 8), :] = hb
            return hb[7:8, :]

        hlast = lax.fori_loop(0, TM // 8, group, carry_sc[0:1, :])
        carry_sc[0:1, :] = hlast
        h_ref[...] = h_sc[...].astype(BF16)

    vec = pl.BlockSpec((1, D), lambda i: (0, 0))
    wspec = pl.BlockSpec((LRU_BLOCKS, LRU_BLOCK, LRU_BLOCK), lambda i: (0, 0, 0))
    return _hosted_call(
        body,
        name="lru_fwd",
        grid=(nt,),
        in_specs=[
            pl.BlockSpec((TM, D), lambda i: (i, LRU_IN_COL)),
            pl.BlockSpec((4, D), lambda i: (0, 0)),
            vec, wspec, wspec, vec, vec, vec,
        ],
        out_specs=[pl.BlockSpec((TM, D), lambda i: (i, 0)) for _ in range(6)],
        out_shape=[jax.ShapeDtypeStruct((rows, D), BF16) for _ in range(6)],
        scratch_shapes=[
            pltpu.VMEM((8, D), F32), pltpu.VMEM((8, D), F32),
            pltpu.VMEM((TM, D), F32), pltpu.VMEM((TM, D), F32), pltpu.VMEM((TM, D), F32), pltpu.VMEM((TM, D), F32),
        ],
        args=(proj, conv_w, conv_b, wa, wx, ba, bx, lam),
        riders=riders,
    )


def _group_norm(o):
    outs, rstds = [], []
    for h in range(HEADS):
        oh = o[:, h * DH:(h + 1) * DH]
        rs = lax.rsqrt(jnp.mean(oh * oh, axis=-1, keepdims=True) + EPS)
        outs.append(oh * rs)
        rstds.append(rs)
    return jnp.concatenate(outs, axis=1), rstds


def _frame_rows(head_hbm, x_hbm, buf, sems, i, tm, nt):
    slot = i % 2

    @pl.when(i == 0)
    def _():
        first = [pltpu.make_async_copy(head_hbm, buf.at[0, pl.ds(0, FRONT)], sems.at[2]),
                 pltpu.make_async_copy(x_hbm.at[pl.ds(0, tm - FRONT)], buf.at[0, pl.ds(FRONT, tm - FRONT)], sems.at[0])]
        for cp in first:
            cp.start()
        for cp in first:
            cp.wait()

    @pl.when(i + 1 < nt)
    def _():
        start = pl.multiple_of((i + 1) * tm - FRONT, LANES)
        pltpu.make_async_copy(x_hbm.at[pl.ds(start, tm)], buf.at[1 - slot], sems.at[1 - slot]).start()

    @pl.when(i > 0)
    def _():
        pltpu.make_async_copy(x_hbm.at[pl.ds(0, tm)], buf.at[slot], sems.at[slot]).wait()

    return slot


def _mix_fwd(head, x2d, o, proj, h_lru, w_br, w_bl, w_o, ffn_norm_w, riders=()):
    rows = o.shape[0]
    tm = TM_MIX_FWD
    assert rows % tm == 0 and tm > FRONT
    nt = rows // tm

    def body(head_hbm, x_hbm, o_ref, gates_ref, hl_ref, wbr_ref, wbl_ref, wo_ref, nw_ref,
             yret_ref, ylru_ref, h1_ref, u2_ref, rstd_ref, h0_sc, h0_sem):
        i = pl.program_id(0)
        slot = _frame_rows(head_hbm, x_hbm, h0_sc, h0_sem, i, tm, nt)
        gate = lambda j: gates_ref[:, j * D:(j + 1) * D].astype(F32)
        on, _ = _group_norm(o_ref[...].astype(F32))
        gret = gate(0)
        a_ret = (gret * _sigmoid(gret) * on).astype(BF16)
        y_ret = _dot(a_ret, wbr_ref[...])
        gl, _ = _gelu_and_grad(gate(1))
        a_lru = (gl * hl_ref[...].astype(F32)).astype(BF16)
        y_lru = _dot(a_lru, wbl_ref[...])
        mixed = (_sigmoid(gate(2)) * y_ret + _sigmoid(gate(3)) * y_lru).astype(BF16)
        delta = _dot(mixed, wo_ref[...])
        yret_ref[...] = y_ret.astype(BF16)
        ylru_ref[...] = y_lru.astype(BF16)
        h1 = h0_sc[slot] + delta
        rs = lax.rsqrt(jnp.mean(h1 * h1, axis=-1, keepdims=True) + EPS)
        h1_ref[...] = h1
        u2_ref[...] = ((h1 * rs) * nw_ref[...]).astype(BF16)
        rstd_ref[...] = rs

    tile = lambda col: pl.BlockSpec((tm, D), lambda i: (i, col))
    wspec = pl.BlockSpec((D, D), lambda i: (0, 0))
    return _hosted_call(
        body,
        name="mix_fwd",
        grid=(nt,),
        in_specs=[
            _ANY, _ANY,
            tile(0), pl.BlockSpec((tm, 4 * D), lambda i: (i, GATES_COL)), tile(0),
            wspec, wspec, wspec,
            pl.BlockSpec((1, D), lambda i: (0, 0)),
        ],
        out_specs=[tile(0), tile(0), tile(0), tile(0), pl.BlockSpec((tm, 1), lambda i: (i, 0))],
        out_shape=[
            jax.ShapeDtypeStruct((rows, D), BF16), jax.ShapeDtypeStruct((rows, D), BF16),
            jax.ShapeDtypeStruct((rows, D), F32), jax.ShapeDtypeStruct((rows, D), BF16),
            jax.ShapeDtypeStruct((rows, 1), F32),
        ],
        scratch_shapes=[pltpu.VMEM((2, tm, D), F32), pltpu.SemaphoreType.DMA((3,))],
        args=(head, x2d, o, proj, h_lru, w_br, w_bl, w_o, ffn_norm_w),
        riders=riders,
    )


def _ffn_fwd_gate(u2, w_ffn_in, riders=()):
    rows = u2.shape[0]
    tm = _heavy_tile(rows)
    hid = lambda: pl.BlockSpec((tm, FFN), lambda i: (i, 0))

    def body(u2_ref, w_hbm, silu_ref, dsilu_ref, act_ref, w_sc, w_sem):
        @pl.when(pl.program_id(0) == 0)
        def _():
            cp = pltpu.make_async_copy(w_hbm, w_sc, w_sem)
            cp.start()
            cp.wait()

        u2 = u2_ref[...]
        for a, b in _FFN_SUBS:
            g = _dot(u2, w_sc[:, a:b])
            up = _dot(u2, w_sc[:, FFN + a:FFN + b])
            sg = _sigmoid(g)
            silu = g * sg
            silu_ref[:, a:b] = silu.astype(BF16)
            dsilu_ref[:, a:b] = (up * (sg * (1.0 + g * (1.0 - sg)))).astype(BF16)
            act_ref[:, a:b] = (silu * up).astype(BF16)

    return _hosted_call(
        body,
        name="ffn_fwd_gate",
        grid=(rows // tm,),
        in_specs=[pl.BlockSpec((tm, D), lambda i: (i, 0)), _ANY],
        out_specs=[hid(), hid(), hid()],
        out_shape=[jax.ShapeDtypeStruct((rows, FFN), BF16)] * 3,
        scratch_shapes=[pltpu.VMEM((D, 2 * FFN), BF16), pltpu.SemaphoreType.DMA],
        args=(u2, w_ffn_in),
        riders=riders,
    )


def _ffn_out_loss(act, h1, w_ffn_out, target, final_norm_w):
    rows = act.shape[0]
    tm = _heavy_tile(rows)
    nt = rows // tm

    def body(act_ref, h1_ref, wo_ref, fnw_ref, tgt_hbm, dh2_ref, stats_ref, tgt_sc, tgt_sem):
        i = pl.program_id(0)
        slot = i % 2

        @pl.when(i == 0)
        def _():
            stats_ref[...] = jnp.zeros_like(stats_ref)
            tgt_sc[0, 0:FRONT, :] = jnp.zeros((FRONT, D), F32)
            cp = pltpu.make_async_copy(tgt_hbm.at[pl.ds(0, tm - FRONT)], tgt_sc.at[0, pl.ds(FRONT, tm - FRONT)], tgt_sem.at[0])
            cp.start()
            cp.wait()

        @pl.when(i + 1 < nt)
        def _():
            start = pl.multiple_of((i + 1) * tm - FRONT, FRONT)
            pltpu.make_async_copy(tgt_hbm.at[pl.ds(start, tm)], tgt_sc.at[1 - slot], tgt_sem.at[1 - slot]).start()

        @pl.when(i > 0)
        def _():
            pltpu.make_async_copy(tgt_hbm.at[pl.ds(0, tm)], tgt_sc.at[slot], tgt_sem.at[slot]).wait()

        h2 = h1_ref[...] + _dot(act_ref[...], wo_ref[...])
        rs = lax.rsqrt(jnp.mean(h2 * h2, axis=-1, keepdims=True) + EPS)
        n = h2 * rs
        fnw = fnw_ref[...]
        valid = _row_ids(i, tm, (tm, D)) >= FRONT
        diff = jnp.where(valid, n * fnw - tgt_sc[slot], 0.0)
        dy = diff * (1.0 / D)
        stats_ref[0:1, :] += (0.5 / D) * jnp.sum(diff * diff, axis=0, keepdims=True)
        stats_ref[1:2, :] += jnp.sum(dy * n, axis=0, keepdims=True)
        dn = dy * fnw
        dh2_ref[...] = rs * (dn - n * jnp.mean(dn * n, axis=-1, keepdims=True))

    return pl.pallas_call(
        body,
        name="ffn_out_loss",
        grid=(nt,),
        in_specs=[
            pl.BlockSpec((tm, FFN), lambda i: (i, 0)),
            pl.BlockSpec((tm, D), lambda i: (i, 0)),
            pl.BlockSpec((FFN, D), lambda i: (0, 0)),
            pl.BlockSpec((1, D), lambda i: (0, 0)),
            _ANY,
        ],
        out_specs=[pl.BlockSpec((tm, D), lambda i: (i, 0)), pl.BlockSpec((8, D), lambda i: (0, 0))],
        out_shape=[jax.ShapeDtypeStruct((rows, D), F32), jax.ShapeDtypeStruct((8, D), F32)],
        scratch_shapes=[pltpu.VMEM((2, tm, D), F32), pltpu.SemaphoreType.DMA((2,))],
        compiler_params=_params(("arbitrary",)),
    )(act, h1, w_ffn_out, final_norm_w, target)


def _ffn_bwd(dh2, silu, dsilu, h1, rstd2, w_ffn_in, w_ffn_out, ffn_norm_w):
    rows = dh2.shape[0]
    tm = _heavy_tile(rows)
    blk = lambda: pl.BlockSpec((tm, FFN), lambda i: (i, 0))

    def gate_body(dh2_ref, silu_ref, dsilu_ref, wo_hbm, dg_ref, dup_ref, dh2b_ref, wo_sc, wo_sem):
        @pl.when(pl.program_id(0) == 0)
        def _():
            cp = pltpu.make_async_copy(wo_hbm, wo_sc, wo_sem)
            cp.start()
            cp.wait()

        dh2b = dh2_ref[...].astype(BF16)
        dh2b_ref[...] = dh2b
        for a, b in _FFN_SUBS:
            dact = _dot_nt(dh2b, wo_sc[a:b, :])
            dup_ref[:, a:b] = (dact * silu_ref[:, a:b].astype(F32)).astype(BF16)
            dg_ref[:, a:b] = (dact * dsilu_ref[:, a:b].astype(F32)).astype(BF16)

    dg, dup, dh2b = pl.pallas_call(
        gate_body,
        name="ffn_bwd_gate",
        grid=(rows // tm,),
        in_specs=[pl.BlockSpec((tm, D), lambda i: (i, 0)), blk(), blk(), _ANY],
        out_specs=[blk(), blk(), pl.BlockSpec((tm, D), lambda i: (i, 0))],
        out_shape=[jax.ShapeDtypeStruct((rows, FFN), BF16)] * 2 + [jax.ShapeDtypeStruct((rows, D), BF16)],
        scratch_shapes=[pltpu.VMEM((FFN, D), BF16), pltpu.SemaphoreType.DMA],
        compiler_params=_params(("arbitrary",)),
    )(dh2, silu, dsilu, w_ffn_out)

    def body(dg_ref, dup_ref, dh2_ref, h1_ref, rstd_ref, w_hbm, nw_ref, dh1_ref, stats_ref, w_sc, w_sem):
        @pl.when(pl.program_id(0) == 0)
        def _():
            stats_ref[...] = jnp.zeros_like(stats_ref)
            cp = pltpu.make_async_copy(w_hbm, w_sc, w_sem)
            cp.start()
            cp.wait()

        du = _dot_nt(dg_ref[...], w_sc[:, 0:FFN]) + _dot_nt(dup_ref[...], w_sc[:, FFN:2 * FFN])
        rs = rstd_ref[...]
        n = h1_ref[...] * rs
        stats_ref[0:1, :] += jnp.sum(du * n, axis=0, keepdims=True)
        dn = du * nw_ref[...]
        dh1_ref[...] = dh2_ref[...] + rs * (dn - n * jnp.mean(dn * n, axis=-1, keepdims=True))

    row = lambda width: pl.BlockSpec((tm, width), lambda i: (i, 0))
    dh1, stats = pl.pallas_call(
        body,
        name="ffn_bwd_in",
        grid=(rows // tm,),
        in_specs=[row(FFN), row(FFN), row(D), row(D), row(1), _ANY, pl.BlockSpec((1, D), lambda i: (0, 0))],
        out_specs=[row(D), pl.BlockSpec((8, D), lambda i: (0, 0))],
        out_shape=[jax.ShapeDtypeStruct((rows, D), F32), jax.ShapeDtypeStruct((8, D), F32)],
        scratch_shapes=[pltpu.VMEM((D, 2 * FFN), BF16), pltpu.SemaphoreType.DMA],
        compiler_params=_params(("arbitrary",)),
    )(dg, dup, dh2, h1, rstd2, w_ffn_in, ffn_norm_w)
    return dg, dup, dh2b, dh1, stats


def _mix_bwd(dh1, o, proj, h_lru, y_ret, y_lru, w_br, w_bl, w_o, riders=()):
    rows = dh1.shape[0]
    tm = TM_MIX_BWD
    nt = rows // tm

    def body(dh1_ref, o_ref, gates_ref, hl_ref, yret_ref, ylru_ref, wbr_hbm, wbl_hbm, wo_hbm,
             dproj_ref, do_ref, dhl_ref, mixed_ref, aret_ref, alru_ref, dyret_ref, dylru_ref, w_sc, w_sem):
        @pl.when(pl.program_id(0) == 0)
        def _():
            copies = [pltpu.make_async_copy(src, w_sc.at[q], w_sem.at[q]) for q, src in enumerate((wbr_hbm, wbl_hbm, wo_hbm))]
            for cp in copies:
                cp.start()
            for cp in copies:
                cp.wait()

        wbr_ref, wbl_ref, wo_ref = w_sc.at[0], w_sc.at[1], w_sc.at[2]
        gate = lambda j: gates_ref[:, j * D:(j + 1) * D].astype(F32)
        dmixed = _dot_nt(dh1_ref[...].astype(BF16), wo_ref[...])
        y_ret, y_lru = yret_ref[...].astype(F32), ylru_ref[...].astype(F32)
        sa, sb = _sigmoid(gate(2)), _sigmoid(gate(3))
        mixed_ref[...] = (sa * y_ret + sb * y_lru).astype(BF16)
        dga = dmixed * y_ret * sa * (1.0 - sa)
        dgb = dmixed * y_lru * sb * (1.0 - sb)
        dy_ret = (dmixed * sa).astype(BF16)
        dy_lru = (dmixed * sb).astype(BF16)
        dyret_ref[...] = dy_ret
        dylru_ref[...] = dy_lru
        da_ret = _dot_nt(dy_ret, wbr_ref[...])
        da_lru = _dot_nt(dy_lru, wbl_ref[...])

        gret = gate(0)
        sg = _sigmoid(gret)
        silu = gret * sg
        on, rstds = _group_norm(o_ref[...].astype(F32))
        aret_ref[...] = (silu * on).astype(BF16)
        dgret = da_ret * on * (sg * (1.0 + gret * (1.0 - sg)))
        don = da_ret * silu
        for h in range(HEADS):
            sl = slice(h * DH, (h + 1) * DH)
            onh, donh = on[:, sl], don[:, sl]
            do_ref[:, sl] = (rstds[h] * (donh - onh * jnp.mean(donh * onh, axis=-1, keepdims=True))).astype(BF16)

        gl, gl_grad = _gelu_and_grad(gate(1))
        hl = hl_ref[...].astype(F32)
        alru_ref[...] = (gl * hl).astype(BF16)
        dlgate = da_lru * hl * gl_grad
        dhl_ref[...] = (da_lru * gl).astype(BF16)

        for j, val in enumerate((dgret, dlgate, dga, dgb)):
            dproj_ref[:, j * D:(j + 1) * D] = val.astype(BF16)

    tile = lambda col: pl.BlockSpec((tm, D), lambda i: (i, col))
    gates = pl.BlockSpec((tm, 4 * D), lambda i: (i, GATES_COL))
    bf = lambda: jax.ShapeDtypeStruct((rows, D), BF16)
    return _hosted_call(
        body,
        name="mix_bwd",
        grid=(nt,),
        in_specs=[tile(0), tile(0), gates, tile(0), tile(0), tile(0), _ANY, _ANY, _ANY],
        out_specs=[pl.BlockSpec((tm, 4 * D), lambda i: (i, GATES_COL))] + [tile(0)] * 7,
        out_shape=[jax.ShapeDtypeStruct((rows, IN_COLS), BF16)] + [bf() for _ in range(7)],
        scratch_shapes=[pltpu.VMEM((3, D, D), BF16), pltpu.SemaphoreType.DMA((3,))],
        args=(dh1, o, proj, h_lru, y_ret, y_lru, w_br, w_bl, w_o),
        riders=riders,
    )


def _retention_bwd(dproj, proj_bf, do, sprev, intra, intra_t, q_dec, k_dec, c_dec, cos_t, sin_t, riders=()):
    rows = proj_bf.shape[0]
    nc = rows // CHUNK

    def body(dproj_in_ref, q_ref, k_ref, v_ref, do_ref, sprev_ref, m_ref, mt_ref, qd_ref, kd_ref, cd_ref, cos_ref, sin_ref,
             dproj_ref, ds_sc):
        @pl.when(pl.program_id(0) == 0)
        def _():
            ds_sc[...] = jnp.zeros_like(ds_sc)

        cos, sin = cos_ref[...], sin_ref[...]

        def unrotate(dy):
            return dy * cos - pltpu.roll(dy, DH // 2, axis=1) * sin

        for h in range(HEADS):
            sl = slice(h * DH, (h + 1) * DH)
            q, k, v = q_ref[:, sl], k_ref[:, sl], v_ref[:, sl]
            do = do_ref[:, sl]
            dob = do.astype(BF16)
            doq = (do * qd_ref[:, sl]).astype(BF16)
            state_prev = sprev_ref[0, h]
            dstate = ds_sc[h]
            dstate_b = dstate.astype(BF16)
            s_t = (_dot_nt(k, q) * mt_ref[h]).astype(BF16)
            ds_t = (_dot_nt(v, dob) * mt_ref[h]).astype(BF16)
            ds = (_dot_nt(dob, v) * m_ref[h]).astype(BF16)
            kd = kd_ref[:, sl]
            dq = _dot(ds, k) + _dot_nt(doq, state_prev)
            dk = _dot(ds_t, q) + _dot_nt(v, dstate_b) * kd
            k_scaled = (k.astype(F32) * kd).astype(BF16)
            dv = _dot(s_t, dob) + _dot(k_scaled, dstate_b)
            ds_sc[h] = dstate * cd_ref[:, sl] + _dot_tn(q, doq)
            for part, val in enumerate((unrotate(dq), unrotate(dk) * QK_SCALE, dv)):
                dproj_ref[:, part * D + h * DH:part * D + (h + 1) * DH] = val.astype(BF16)

    rev = lambda c: nc - 1 - c
    chunk_spec = lambda col: pl.BlockSpec((CHUNK, D), lambda c: (rev(c), col))
    const2 = lambda shape: pl.BlockSpec(shape, lambda c: (0, 0))
    const3 = pl.BlockSpec((HEADS, CHUNK, CHUNK), lambda c: (0, 0, 0))
    return _hosted_call(
        body,
        name="retention_bwd",
        grid=(nc,),
        in_specs=[
            pl.BlockSpec(memory_space=pl.ANY),
            chunk_spec(0), chunk_spec(1), chunk_spec(2), chunk_spec(0),
            pl.BlockSpec((1, HEADS, DH, DH), lambda c: (rev(c), 0, 0, 0)),
            const3, const3,
            const2((CHUNK, D)), const2((CHUNK, D)), const2((1, D)),
            pl.BlockSpec((CHUNK, DH), lambda c: (rev(c), 0)),
            pl.BlockSpec((CHUNK, DH), lambda c: (rev(c), 0)),
        ],
        out_specs=[pl.BlockSpec((CHUNK, 3 * D), lambda c: (rev(c), 0))],
        out_shape=[jax.ShapeDtypeStruct(dproj.shape, BF16)],
        aliases={0: 0},
        scratch_shapes=[pltpu.VMEM((HEADS, DH, DH), F32)],
        args=(dproj, proj_bf, proj_bf, proj_bf, do, sprev, intra, intra_t, q_dec, k_dec, c_dec, cos_t, sin_t),
        riders=riders,
    )


def _lru_bwd(dproj, proj, saved, dhl, conv_w, wa, wx, lam, riders=()):
    rows = proj.shape[0]
    TM = _lru_tile(rows)
    nt = rows // TM
    per8 = TM // 8

    def body(dproj_in_ref, x_ref, xprev_ref, h_ref, hprev_ref, c_ref, r_ref, i_ref, la_ref, mult_ref, dhl_ref,
             cw_ref, wa_ref, wx_ref, lam_ref,
             dproj_ref, dwa_ref, dwx_ref, stats_ref, anext_sc, dhnext_sc, dcnext_sc, c_sc, b_sc, dh_sc):
        step = pl.program_id(0)
        i = nt - 1 - step

        @pl.when(step == 0)
        def _():
            anext_sc[...] = jnp.zeros_like(anext_sc)
            dhnext_sc[...] = jnp.zeros_like(dhnext_sc)
            dcnext_sc[...] = jnp.zeros_like(dcnext_sc)
            dwa_ref[...] = jnp.zeros_like(dwa_ref)
            dwx_ref[...] = jnp.zeros_like(dwx_ref)
            stats_ref[...] = jnp.zeros_like(stats_ref)

        first = i == 0
        x = x_ref[...].astype(F32)
        prev8 = jnp.where(first, 0.0, xprev_ref[8:16, :].astype(F32))
        c_b = c_ref[...]
        c, r, gate_i, mult = (ref[...].astype(F32) for ref in (c_ref, r_ref, i_ref, mult_ref))
        a = jnp.exp(la_ref[...].astype(F32))
        sp = _softplus(-lam_ref[...])
        dh_sc[...] = dhl_ref[...].astype(F32)

        b_sc[...] = pltpu.roll(a, TM - 1, axis=0)
        b_sc[TM - 1:TM, :] = anext_sc[0:1, :]
        anext_sc[0:1, :] = a[0:1, :]
        row8 = lax.broadcasted_iota(jnp.int32, (8, D), 0)

        def group(gi, dhnext):
            r0 = pl.multiple_of((per8 - 1 - gi) * 8, 8)
            bb = b_sc[pl.ds(r0, 8), :]
            uu = dh_sc[pl.ds(r0, 8), :]
            for d in (1, 2, 4):
                b_sh = jnp.where(row8 < 8 - d, pltpu.roll(bb, 8 - d, axis=0), 1.0)
                u_sh = jnp.where(row8 < 8 - d, pltpu.roll(uu, 8 - d, axis=0), 0.0)
                uu = uu + bb * u_sh
                bb = bb * b_sh
            dhb = bb * dhnext + uu
            dh_sc[pl.ds(r0, 8), :] = dhb
            return dhb[0:1, :]

        dhfirst = lax.fori_loop(0, per8, group, dhnext_sc[0:1, :])
        dhnext_sc[0:1, :] = dhfirst
        dh = dh_sc[...]

        h = h_ref[...].astype(F32)
        hprev8 = jnp.where(first, 0.0, hprev_ref[8:16, :].astype(F32))
        h_dn, h_head = _shift_down(h, hprev8, 1)
        c_sc[...] = h_dn
        c_sc[0:8, :] = h_head
        h_before = c_sc[...]

        valid = _row_ids(i, TM, (TM, D)) >= PAD_ROWS
        da = dh * h_before
        du = jnp.where(valid, dh, 0.0)
        dmult = du * gate_i * c
        dgate_i = du * mult * c
        dc = du * mult * gate_i
        dla = da * a - dmult * (a * a) / mult
        dla = jnp.where(valid, dla, 0.0)
        dr = dla * ((-LRU_C) * sp)
        dzr = dr * r * (1.0 - r)
        dzi = dgate_i * gate_i * (1.0 - gate_i)
        stats_ref[1:2, :] += jnp.sum(dzr, axis=0, keepdims=True)
        stats_ref[2:3, :] += jnp.sum(dzi, axis=0, keepdims=True)
        stats_ref[3:4, :] += jnp.sum(dla * ((-LRU_C) * r), axis=0, keepdims=True)
        dc_gate = []
        for g in range(LRU_BLOCKS):
            sl = slice(g * LRU_BLOCK, (g + 1) * LRU_BLOCK)
            cg = c_b[:, sl]
            dzr_g = dzr[:, sl].astype(BF16)
            dzi_g = dzi[:, sl].astype(BF16)
            dc_gate.append(_dot_nt(dzr_g, wa_ref[g]) + _dot_nt(dzi_g, wx_ref[g]))
            dwa_ref[g] += _dot_tn(cg, dzr_g)
            dwx_ref[g] += _dot_tn(cg, dzi_g)
        dc = dc + jnp.concatenate(dc_gate, axis=1)

        cw = cw_ref[...]
        stats_ref[0:1, :] += jnp.sum(dc, axis=0, keepdims=True)
        stats_ref[7:8, :] += jnp.sum(dc * x, axis=0, keepdims=True)
        dx = cw[3:4] * dc
        tail_src = jnp.concatenate([dc[TM - 8:TM], dcnext_sc[...]], axis=0)
        dx_tail = cw[3:4] * dc[TM - 8:TM]
        for d in (1, 2, 3):
            dx = dx + cw[3 - d:4 - d] * pltpu.roll(dc, TM - d, axis=0)
            dx_tail = dx_tail + cw[3 - d:4 - d] * pltpu.roll(tail_src, 16 - d, axis=0)[0:8]
            rolled, hd = _shift_down(x, prev8, d)
            b_sc[...] = rolled
            b_sc[0:8, :] = hd
            stats_ref[7 - d:8 - d, :] += jnp.sum(dc * b_sc[...], axis=0, keepdims=True)
        dcnext_sc[...] = dc[0:8]
        dproj_ref[...] = dx.astype(BF16)
        dproj_ref[TM - 8:TM, :] = dx_tail.astype(BF16)

    rev = lambda s: nt - 1 - s
    vec = pl.BlockSpec((1, D), lambda s: (0, 0))
    wspec = pl.BlockSpec((LRU_BLOCKS, LRU_BLOCK, LRU_BLOCK), lambda s: (0, 0, 0))
    prev16 = lambda col: pl.BlockSpec((BF16_ROWS, D), lambda s: (jnp.maximum(rev(s) * (TM // BF16_ROWS) - 1, 0), col))
    tile = lambda: pl.BlockSpec((TM, D), lambda s: (rev(s), 0))
    h_lru, c_sv, r_sv, i_sv, la_sv, mult_sv = saved
    return _hosted_call(
        body,
        name="lru_bwd",
        grid=(nt,),
        in_specs=[
            pl.BlockSpec(memory_space=pl.ANY),
            pl.BlockSpec((TM, D), lambda s: (rev(s), LRU_IN_COL)), prev16(LRU_IN_COL),
            tile(), prev16(0),
            tile(), tile(), tile(), tile(), tile(), tile(),
            pl.BlockSpec((4, D), lambda s: (0, 0)),
            wspec, wspec, vec,
        ],
        out_specs=[
            pl.BlockSpec((TM, D), lambda s: (rev(s), LRU_IN_COL)),
            wspec, wspec,
            pl.BlockSpec((8, D), lambda s: (0, 0)),
        ],
        out_shape=[
            jax.ShapeDtypeStruct(dproj.shape, BF16),
            jax.ShapeDtypeStruct((LRU_BLOCKS, LRU_BLOCK, LRU_BLOCK), F32),
            jax.ShapeDtypeStruct((LRU_BLOCKS, LRU_BLOCK, LRU_BLOCK), F32),
            jax.ShapeDtypeStruct((8, D), F32),
        ],
        aliases={0: 0},
        scratch_shapes=[
            pltpu.VMEM((8, D), F32), pltpu.VMEM((8, D), F32), pltpu.VMEM((8, D), F32),
            pltpu.VMEM((TM, D), F32), pltpu.VMEM((TM, D), F32), pltpu.VMEM((TM, D), F32),
        ],
        args=(dproj, proj, proj, h_lru, h_lru, c_sv, r_sv, i_sv, la_sv, mult_sv, dhl, conv_w, wa, wx, lam),
        riders=riders,
    )


def _in_proj_bwd(dproj, w_in, part, prev=None, riders=()):
    rows = dproj.shape[0]
    tm = _heavy_tile(rows)
    nt = rows // tm
    split = max(1, nt // 4 + 1) if nt > 1 else 1
    first = 0 if part == 0 else split
    count = split if part == 0 else nt - split

    def body(*refs):
        dproj_ref, w_hbm, du_ref, w_sc, w_sem = refs[-5:]

        @pl.when(pl.program_id(0) == 0)
        def _():
            moves = [((0, 3), 0), ((4, 1), 3), ((3, 1), 4), ((5, 3), 5)]
            copies = [pltpu.make_async_copy(w_hbm.at[:, pl.ds(src * D, n * D)], w_sc.at[:, pl.ds(dst * D, n * D)], w_sem.at[q])
                      for q, ((src, n), dst) in enumerate(moves)]
            for cp in copies:
                cp.start()
            for cp in copies:
                cp.wait()

        du_ref[...] = _dot_nt(dproj_ref[...], w_sc[...])

    in_specs = [pl.BlockSpec((tm, IN_COLS), lambda i: (first + i, 0)), _ANY]
    args = (dproj, w_in)
    if prev is not None:
        in_specs = [_ANY] + in_specs
        args = (prev,) + args
    return _hosted_call(
        body,
        name="in_proj_bwd_%d" % part,
        grid=(count,),
        in_specs=in_specs,
        out_specs=[pl.BlockSpec((tm, D), lambda i: (first + i, 0))],
        out_shape=[jax.ShapeDtypeStruct((rows, D), F32)],
        scratch_shapes=[pltpu.VMEM((D, IN_COLS), BF16), pltpu.SemaphoreType.DMA((4,))],
        aliases={0: 0} if prev is not None else None,
        args=args,
        riders=riders,
    )


def _norm1_bwd(du, dh1, head, x2d, rstd1, norm_w, riders=()):
    rows = du.shape[0]

    def body(du_ref, dh1_ref, head_ref, x_ref, rstd_ref, nw_ref, gx_ref, ghead_ref, stats_ref):
        i = pl.program_id(0)

        @pl.when(i == 0)
        def _():
            stats_ref[...] = jnp.zeros_like(stats_ref)

        def finish(h0, out_ref):
            du = du_ref[...]
            rs = rstd_ref[...]
            n = h0 * rs
            stats_ref[0:1, :] += jnp.sum(du * n, axis=0, keepdims=True)
            dn = du * nw_ref[...]
            out_ref[...] = dh1_ref[...] + rs * (dn - n * jnp.mean(dn * n, axis=-1, keepdims=True))

        @pl.when(i == 0)
        def _():
            finish(head_ref[...], ghead_ref)

        @pl.when(i > 0)
        def _():
            finish(x_ref[...], gx_ref)

    tile = pl.BlockSpec((TM, D), lambda i: (i, 0))
    return _hosted_call(
        body,
        name="norm1_bwd",
        grid=(rows // TM,),
        in_specs=[
            tile, tile,
            pl.BlockSpec((FRONT, D), lambda i: (0, 0)),
            pl.BlockSpec((TM, D), lambda i: (jnp.maximum(i - 1, 0), 0)),
            pl.BlockSpec((TM, 1), lambda i: (i, 0)),
            pl.BlockSpec((1, D), lambda i: (0, 0)),
        ],
        out_specs=[
            pl.BlockSpec((TM, D), lambda i: (jnp.maximum(i - 1, 0), 0)),
            pl.BlockSpec((FRONT, D), lambda i: (0, 0)),
            pl.BlockSpec((8, D), lambda i: (0, 0)),
        ],
        out_shape=[
            jax.ShapeDtypeStruct(x2d.shape, F32),
            jax.ShapeDtypeStruct((FRONT, D), F32),
            jax.ShapeDtypeStruct((8, D), F32),
        ],
        scratch_shapes=[],
        args=(du, dh1, head, x2d, rstd1, norm_w),
        riders=riders,
    )


def _matmul_tn(name, x, dy, out_cols, col0=0, prev=None, k_block=None, n_block=None, riders=(), col_map=None):
    col_map = col_map or (lambda n: n)
    rows, kdim = x.shape
    ndim = dy.shape[1]
    kb = k_block or kdim
    nb = n_block or ndim
    step_bytes = lambda t: 2 * t * (kb * x.dtype.itemsize + nb * dy.dtype.itemsize) + 2 * kb * nb * 4
    tr = next(t for t in (2816, 1408, TM_HEAVY, TM) if rows % t == 0 and (t == TM or step_bytes(t) <= TN_VMEM_BUDGET))
    nr, nk, nn = rows // tr, kdim // kb, ndim // nb
    cb0 = col0 // nb

    def body(*refs):
        x_ref, dy_ref, out_ref = refs[-3], refs[-2], refs[-1]
        part = _dot_tn(x_ref[...].astype(BF16), dy_ref[...].astype(BF16))

        @pl.when(pl.program_id(2) == 0)
        def _():
            out_ref[...] = part

        @pl.when(pl.program_id(2) > 0)
        def _():
            out_ref[...] += part

    in_specs = [
        pl.BlockSpec((tr, kb), lambda n, k, r: (r, k)),
        pl.BlockSpec((tr, nb), lambda n, k, r: (r, n)),
    ]
    args = [x, dy]
    aliases = {}
    if prev is not None:
        in_specs = [pl.BlockSpec(memory_space=pl.ANY)] + in_specs
        args = [prev] + args
        aliases = {0: 0}
    (out,), rider_outs = _hosted_call(
        body,
        name=name,
        grid=(nn, nk, nr),
        in_specs=in_specs,
        out_specs=[pl.BlockSpec((kb, nb), lambda n, k, r: (k, cb0 + col_map(n)))],
        out_shape=[jax.ShapeDtypeStruct((kdim, out_cols), F32)],
        scratch_shapes=[],
        aliases=aliases,
        args=args,
        riders=riders,
    )
    return (out, rider_outs) if riders else out


def _local_step(x2d, target, w, plan):
    rows = FRONT + x2d.shape[0]
    head = jnp.concatenate([jnp.zeros((PAD_ROWS, D), F32), w["meta_tokens"]], axis=0)
    cos_t, sin_t = _rope_tables(rows)
    intra, intra_t, q_dec, k_dec, c_dec = _decay_tables()
    grads = {}

    def hosted(host, fn, *args, **kwargs):
        outs, rider_outs = fn(*args, riders=plan.riders(host, w, grads), **kwargs)
        plan.after(host, rider_outs, w, grads)
        return outs

    (u1, rstd1), _ = _norm1(head, x2d, w["mix_norm_w"])
    proj, w["w_in"] = hosted("in_proj", _in_proj, u1, w["w_in_shard"], w["route"], cos_t, sin_t)
    o, sprev = hosted("retention_fwd", _retention_fwd, proj, intra, q_dec, k_dec, c_dec)
    lru_args = (w["conv_w"], w["conv_b"], w["lru_wa"], w["lru_wx"], w["lru_ba"], w["lru_bx"], w["lru_lambda"])
    lru_saved = hosted("lru_fwd", _lru_fwd, proj, *lru_args)
    h_lru = lru_saved[0]
    y_ret, y_lru, h1, u2, rstd2 = hosted("mix_fwd", _mix_fwd, head, x2d, o, proj, h_lru, w["w_branch_ret"],
                                         w["w_branch_lru"], w["w_out"], w["ffn_norm_w"])
    silu, dsilu, act = hosted("ffn_fwd_gate", _ffn_fwd_gate, u2, w["w_ffn_in"])
    dh2, stats_loss = _ffn_out_loss(act, h1, w["w_ffn_out"], target, w["final_norm_w"])

    dg, dup, dh2b, dh1, stats_ffn = _ffn_bwd(dh2, silu, dsilu, h1, rstd2, w["w_ffn_in"], w["w_ffn_out"], w["ffn_norm_w"])
    grads["w_ffn_in"] = _matmul_tn("dw_ffn_up", u2, dup, 2 * FFN, col0=FFN, n_block=FFN_HALF,
                                   prev=_matmul_tn("dw_ffn_gate", u2, dg, 2 * FFN, n_block=FFN_HALF))
    grads["w_ffn_out"] = _matmul_tn("dw_ffn_out", act, dh2b, D, k_block=FFN_HALF)
    (dproj, do, dhl, mixed, a_ret, a_lru, dy_ret, dy_lru) = hosted(
        "mix_bwd", _mix_bwd, dh1, o, proj, h_lru, y_ret, y_lru, w["w_branch_ret"], w["w_branch_lru"], w["w_out"])
    grads["w_out"] = _matmul_tn("dw_out", mixed, dh1, D)
    grads["w_branch_ret"] = _matmul_tn("dw_branch_ret", a_ret, dy_ret, D)
    grads["w_branch_lru"] = _matmul_tn("dw_branch_lru", a_lru, dy_lru, D)
    (dproj,) = hosted("retention_bwd", _retention_bwd, dproj, proj, do, sprev, intra, intra_t, q_dec, k_dec, c_dec,
                      cos_t, sin_t)
    dproj, grads["lru_wa"], grads["lru_wx"], stats_lru = hosted(
        "lru_bwd", _lru_bwd, dproj, proj, lru_saved, dhl, w["conv_w"], w["lru_wa"], w["lru_wx"], w["lru_lambda"])
    grads["w_in"], rider_outs = _matmul_tn("dw_in", u1, dproj, IN_COLS, n_block=D, col_map=_swap_3_4,
                                           riders=plan.riders("dw_in", w, grads))
    plan.after("dw_in", rider_outs, w, grads)
    (du1,) = hosted("in_proj_bwd_0", _in_proj_bwd, dproj, w["w_in"], 0)
    (du1,) = hosted("in_proj_bwd_1", _in_proj_bwd, dproj, w["w_in"], 1, du1)
    (grad_x, grad_head, stats_in), _ = _norm1_bwd(du1, dh1, head, x2d, rstd1, w["mix_norm_w"])
    return grad_x, grad_head, grads, [stats_loss, stats_ffn, stats_in, stats_lru]


BIG_PIECES = {
    "w_in": ("col", (D, 2 * D)),
    "w_ffn_in": ("col", (D, FFN_HALF)),
    "w_ffn_out": ("row", (FFN // 4, D)),
    "w_branch_ret": ("row", (D // 4, D)),
    "w_branch_lru": ("row", (D // 4, D)),
    "w_out": ("row", (D // 4, D)),
    "lru_wa": ("lru", (LRU_BLOCKS, LRU_BLOCK // 4, LRU_BLOCK)),
    "lru_wx": ("lru", (LRU_BLOCKS, LRU_BLOCK // 4, LRU_BLOCK)),
}
SMALL_PIECES = {"meta_tokens": ("col", (N_META, D // 4)), "conv_w": ("col", (4, D // 4))}


def _full_shape(kind, shard):
    if kind == "col":
        return (shard[0], 4 * shard[1])
    if kind == "row":
        return (4 * shard[0], shard[1])
    return (shard[0], 4 * shard[1], shard[2])


def _half_shape(kind, shard):
    return (shard[0] // 2,) + tuple(shard[1:])


def _aligned(start, multiple):
    return start if isinstance(start, int) else pl.multiple_of(start, multiple)


def _lead(h, size):
    if h is None:
        return pl.ds(0, size)
    return pl.ds(_aligned(h * (size // 2), size // 2), size // 2)


def _full_region(ref, kind, shard, s, h):
    if kind == "col":
        return ref.at[_lead(h, shard[0]), pl.ds(_aligned(s * shard[1], shard[1]), shard[1])]
    if kind == "row":
        size = shard[0] if h is None else shard[0] // 2
        start = s * shard[0] + (0 if h is None else h * (shard[0] // 2))
        return ref.at[pl.ds(_aligned(start, BF16_ROWS), size), :]
    return ref.at[_lead(h, shard[0]), pl.ds(_aligned(s * shard[1], shard[1]), shard[1]), :]


def _shard_region(ref, shard, h):
    return ref.at[_lead(h, shard[0])]


def _place():
    x, y, c = lax.axis_index("x"), lax.axis_index("y"), lax.axis_index("c")
    return x, y, c, 2 * x + y


def _other_chip(s, c, k):
    s2 = jnp.bitwise_xor(s, k)
    return s2, (s2 // 2, s2 % 2, c)


def _remote(src, dst, send_sem, recv_sem, dev):
    return pltpu.make_async_remote_copy(src_ref=src, dst_ref=dst, send_sem=send_sem, recv_sem=recv_sem,
                                        device_id=dev, device_id_type=MESH)


_ANY = pl.BlockSpec(memory_space=pl.ANY)


class _Rider:
    def __init__(self, ins, out_shapes, sem_shapes, build, aliased=False):
        self.ins, self.out_shapes, self.sem_shapes, self.build, self.aliased = ins, out_shapes, sem_shapes, build, aliased


def _hosted_call(body, *, name, grid, in_specs, out_specs, out_shape, scratch_shapes, args, riders=(), aliases=None,
                 prefetch=None, riders_after_body=False):
    n_in, n_out, n_sc = len(in_specs), len(out_shape), len(scratch_shapes)
    r_in = [a for r in riders for a in r.ins]
    r_out = [s for r in riders for s in r.out_shapes]
    r_sem = [s for r in riders for s in r.sem_shapes]
    lead = () if prefetch is None else (prefetch,)
    assert prefetch is None or not (aliases or any(r.aliased for r in riders))

    def full_body(*refs):
        head, refs = refs[:len(lead)], refs[len(lead):]
        ins, rin = refs[:n_in], refs[n_in:n_in + len(r_in)]
        o0 = n_in + len(r_in)
        outs, rout = refs[o0:o0 + n_out], refs[o0 + n_out:o0 + n_out + len(r_out)]
        s0 = o0 + n_out + len(r_out)
        scratch, rsem = refs[s0:s0 + n_sc], refs[s0 + n_sc:]
        starts, waits = [], []
        pi = po = ps = 0
        for r in riders:
            st, wt = r.build(rin[pi:pi + len(r.ins)], rout[po:po + len(r.out_shapes)], rsem[ps:ps + len(r.sem_shapes)])
            starts += st
            waits += wt
            pi, po, ps = pi + len(r.ins), po + len(r.out_shapes), ps + len(r.sem_shapes)
        first = functools.reduce(jnp.logical_and, [pl.program_id(d) == 0 for d in range(len(grid))])
        last = functools.reduce(jnp.logical_and, [pl.program_id(d) == grid[d] - 1 for d in range(len(grid))])

        def start_riders():
            @pl.when(first)
            def _():
                for cp in starts:
                    cp.start()

        if riders and not riders_after_body:
            start_riders()
        body(*head, *ins, *outs, *scratch)
        if riders and riders_after_body:
            start_riders()
        if riders:
            @pl.when(last)
            def _():
                for wait in waits:
                    wait()

    io_aliases = dict(aliases or {})
    pi = po = 0
    for r in riders:
        if r.aliased:
            for q in range(len(r.ins)):
                io_aliases[n_in + pi + q] = n_out + po + q
        pi, po = pi + len(r.ins), po + len(r.out_shapes)
    specs = dict(
        grid=grid,
        in_specs=list(in_specs) + [_ANY] * len(r_in),
        out_specs=list(out_specs) + [_ANY] * len(r_out),
        scratch_shapes=list(scratch_shapes) + r_sem,
    )
    if prefetch is not None:
        specs = dict(grid_spec=pltpu.PrefetchScalarGridSpec(num_scalar_prefetch=1, **specs))
    res = pl.pallas_call(
        full_body,
        name=name,
        out_shape=list(out_shape) + r_out,
        input_output_aliases=io_aliases,
        compiler_params=_params(("arbitrary",) * len(grid)),
        **specs,
    )(*lead, *args, *r_in)
    rider_outs, po = [], n_out
    for r in riders:
        rider_outs.append(list(res[po:po + len(r.out_shapes)]))
        po += len(r.out_shapes)
    return list(res[:n_out]), rider_outs


def _run_riders(name, riders):
    r_in = [a for r in riders for a in r.ins]
    r_out = [s for r in riders for s in r.out_shapes]
    r_sem = [s for r in riders for s in r.sem_shapes]

    def body(*refs):
        rin, rout, rsem = refs[:len(r_in)], refs[len(r_in):len(r_in) + len(r_out)], refs[len(r_in) + len(r_out):]
        pi = po = ps = 0
        all_waits = []
        for r in riders:
            starts, waits = r.build(rin[pi:pi + len(r.ins)], rout[po:po + len(r.out_shapes)], rsem[ps:ps + len(r.sem_shapes)])
            for cp in starts:
                cp.start()
            all_waits += waits
            pi, po, ps = pi + len(r.ins), po + len(r.out_shapes), ps + len(r.sem_shapes)
        for wait in all_waits:
            wait()

    io_aliases = {}
    pi = po = 0
    for r in riders:
        if r.aliased:
            for q in range(len(r.ins)):
                io_aliases[pi + q] = po + q
        pi, po = pi + len(r.ins), po + len(r.out_shapes)
    res = pl.pallas_call(
        body,
        name=name,
        in_specs=[_ANY] * len(r_in),
        out_specs=[_ANY] * len(r_out),
        out_shape=r_out,
        scratch_shapes=r_sem,
        input_output_aliases=io_aliases,
    )(*r_in)
    outs, po = [], 0
    for r in riders:
        outs.append(list(res[po:po + len(r.out_shapes)]))
        po += len(r.out_shapes)
    return outs


def _piece(name):
    if name in BIG_PIECES:
        return (name, *BIG_PIECES[name], True)
    return (name, *SMALL_PIECES[name], False)


def _gather_rider(shards, names):
    pieces = [_piece(n) for n in names]
    n = len(pieces)

    def build(ins, outs, sems):
        local_sem, ici_send, ici_recv = sems
        _, _, c, s = _place()
        starts, waits = [], []
        for p, (_, kind, shard, split) in enumerate(pieces):
            cp = pltpu.make_async_copy(ins[p], _full_region(outs[p], kind, shard, s, None), local_sem.at[p])
            starts.append(cp)
            waits.append(cp.wait)
            h = c if split else None
            for k in (1, 2, 3):
                s2, dev = _other_chip(s, c, k)
                cp = _remote(_shard_region(ins[p], shard, h), _full_region(outs[p], kind, shard, s, h),
                             ici_send.at[p, k - 1], ici_recv.at[p, k - 1], dev)
                starts.append(cp)
                waits.append(cp.wait_send)
                region = _full_region(outs[p], kind, shard, s2, h)
                waits.append(_remote(region, region, ici_send.at[p, k - 1], ici_recv.at[p, k - 1], dev).wait_recv)
        return starts, waits

    return _Rider(
        [shards[name] for name in names],
        [jax.ShapeDtypeStruct(_full_shape(kind, shard), shards[name].dtype) for name, kind, shard, _ in pieces],
        [pltpu.SemaphoreType.DMA((n,)), pltpu.SemaphoreType.DMA((n, 3)), pltpu.SemaphoreType.DMA((n, 3))],
        build)


def _forward_rider(gathered, names):
    pieces = [_piece(n) for n in names]
    n = len(pieces)

    def build(ins, outs, sems):
        fwd_send, fwd_recv = sems
        x, y, c, s = _place()
        sibling = (x, y, 1 - c)
        starts, waits = [], []
        for p, (_, kind, shard, _) in enumerate(pieces):
            for k in (1, 2, 3):
                s2, _ = _other_chip(s, c, k)
                mine = _full_region(outs[p], kind, shard, s2, c)
                theirs = _full_region(outs[p], kind, shard, s2, 1 - c)
                cp = _remote(mine, mine, fwd_send.at[p, k - 1], fwd_recv.at[p, k - 1], sibling)
                starts.append(cp)
                waits.append(cp.wait_send)
                waits.append(_remote(theirs, theirs, fwd_send.at[p, k - 1], fwd_recv.at[p, k - 1], sibling).wait_recv)
        return starts, waits

    return _Rider(
        [gathered[name] for name in names],
        [jax.ShapeDtypeStruct(gathered[name].shape, gathered[name].dtype) for name in names],
        [pltpu.SemaphoreType.DMA((n, 3)), pltpu.SemaphoreType.DMA((n, 3))],
        build, aliased=True)


def _pair_exchange_rider(grads, names):
    n = len(names)

    def build(ins, outs, sems):
        send_sem, recv_sem = sems
        x, y, c, _ = _place()
        sibling = (x, y, 1 - c)
        starts, waits = [], []
        for p, name in enumerate(names):
            kind, shard = BIG_PIECES[name]
            for s2 in range(4):
                cp = _remote(_full_region(ins[p], kind, shard, s2, 1 - c), outs[p].at[s2], send_sem.at[p, s2],
                             recv_sem.at[p, s2], sibling)
                starts.append(cp)
                waits.append(cp.wait_send)
                waits.append(_remote(outs[p].at[s2], outs[p].at[s2], send_sem.at[p, s2], recv_sem.at[p, s2], sibling).wait_recv)
        return starts, waits

    return _Rider(
        [grads[name] for name in names],
        [jax.ShapeDtypeStruct((4,) + _half_shape(*BIG_PIECES[name]), F32) for name in names],
        [pltpu.SemaphoreType.DMA((n, 4))] * 2,
        build)


def _half_specs(kind, shard):
    half = shard[0] // 2
    if kind == "col":
        full = pl.BlockSpec((half, shard[1]), lambda j, pr: (pr[1], j))
        buf = pl.BlockSpec((None, half, shard[1]), lambda j, pr: (j, 0, 0))
    elif kind == "row":
        full = pl.BlockSpec((half, shard[1]), lambda j, pr: (2 * j + pr[1], 0))
        buf = pl.BlockSpec((None, half, shard[1]), lambda j, pr: (j, 0, 0))
    else:
        full = pl.BlockSpec((half, shard[1], shard[2]), lambda j, pr: (pr[1], j, 0))
        buf = pl.BlockSpec((None, half, shard[1], shard[2]), lambda j, pr: (j, 0, 0, 0))
    return full, buf


def _pair_sum(name, grad, recv, place):
    kind, shard = BIG_PIECES[name]
    full, buf = _half_specs(kind, shard)

    def body(pr, g_ref, r_ref, o_ref):
        o_ref[...] = (g_ref[...] + r_ref[...]).astype(BF16)

    return pl.pallas_call(
        body,
        name="pair_sum_" + name,
        grid_spec=pltpu.PrefetchScalarGridSpec(num_scalar_prefetch=1, grid=(4,), in_specs=[full, buf], out_specs=buf),
        out_shape=jax.ShapeDtypeStruct((4,) + _half_shape(kind, shard), BF16),
        compiler_params=_params(("arbitrary",)),
    )(place, grad, recv)


def _chip_exchange_rider(sums, names):
    n = len(names)

    def build(ins, outs, sems):
        send_sem, recv_sem = sems
        _, _, c, s = _place()
        starts, waits = [], []
        for p in range(n):
            for k in (1, 2, 3):
                s2, dev = _other_chip(s, c, k)
                cp = _remote(ins[p].at[s2], outs[p].at[k - 1], send_sem.at[p, k - 1], recv_sem.at[p, k - 1], dev)
                starts.append(cp)
                waits.append(cp.wait_send)
                waits.append(_remote(outs[p].at[k - 1], outs[p].at[k - 1], send_sem.at[p, k - 1], recv_sem.at[p, k - 1],
                                     dev).wait_recv)
        return starts, waits

    return _Rider(
        [sums[name] for name in names],
        [jax.ShapeDtypeStruct((3,) + _half_shape(*BIG_PIECES[name]), BF16) for name in names],
        [pltpu.SemaphoreType.DMA((n, 3))] * 2,
        build)


def _chip_sum(name, grad, recv_pair, recv_chip, place):
    kind, shard = BIG_PIECES[name]
    half = shard[0] // 2
    tail = tuple(shard[1:])
    zeros = (0,) * len(tail)
    nt = 4 if kind != "lru" and half % (4 * BF16_ROWS) == 0 else 1
    rows = half // nt
    if kind == "col":
        full = pl.BlockSpec((rows,) + tail, lambda j, pr: (pr[1] * nt + j, pr[0]))
    elif kind == "row":
        full = pl.BlockSpec((rows,) + tail, lambda j, pr: ((2 * pr[0] + pr[1]) * nt + j, 0))
    else:
        full = pl.BlockSpec((rows,) + tail, lambda j, pr: (pr[1], pr[0], 0))
    pair = pl.BlockSpec((None, rows) + tail, lambda j, pr: (pr[0], j) + zeros)
    chip = pl.BlockSpec((3, rows) + tail, lambda j, pr: (0, j) + zeros)
    out = pl.BlockSpec((rows,) + tail, lambda j, pr: (pr[1] * nt + j,) + zeros)

    def body(pr, g_ref, rp_ref, rc_ref, o_ref):
        total = g_ref[...] + rp_ref[...]
        for k in range(3):
            total = total + rc_ref[k].astype(F32)
        o_ref[...] = total

    return pl.pallas_call(
        body,
        name="chip_sum_" + name,
        grid_spec=pltpu.PrefetchScalarGridSpec(num_scalar_prefetch=1, grid=(nt,), in_specs=[full, pair, chip], out_specs=out),
        out_shape=jax.ShapeDtypeStruct(shard, F32),
        compiler_params=_params(("arbitrary",)),
    )(place, grad, recv_pair, recv_chip)


def _sibling_exchange_rider(halves, names):
    n = len(names)

    def build(ins, outs, sems):
        send_sem, recv_sem = sems
        x, y, c, _ = _place()
        sibling = (x, y, 1 - c)
        starts, waits = [], []
        for p, name in enumerate(names):
            shard = BIG_PIECES[name][1]
            mine = _shard_region(outs[p], shard, c)
            theirs = _shard_region(outs[p], shard, 1 - c)
            cp = _remote(mine, mine, send_sem.at[p], recv_sem.at[p], sibling)
            starts.append(cp)
            waits.append(cp.wait_send)
            waits.append(_remote(theirs, theirs, send_sem.at[p], recv_sem.at[p], sibling).wait_recv)
        return starts, waits

    return _Rider(
        [halves[name] for name in names],
        [jax.ShapeDtypeStruct(BIG_PIECES[name][1], F32) for name in names],
        [pltpu.SemaphoreType.DMA((n,))] * 2,
        build, aliased=True)


FIRST_WEIGHTS = ["meta_tokens", "conv_w"]
WEIGHT_GROUPS = {
    "lru": ["lru_wa", "lru_wx"],
    "branch": ["w_branch_ret", "w_branch_lru", "w_out"],
    "ffn_in": ["w_ffn_in"],
    "ffn_out": ["w_ffn_out"],
}
WEIGHT_SCHEDULE = {
    "in_proj": [("gather", "lru")],
    "retention_fwd": [("forward", "lru"), ("gather", "branch")],
    "lru_fwd": [("forward", "branch"), ("gather", "ffn_in")],
    "mix_fwd": [("forward", "ffn_in"), ("gather", "ffn_out")],
    "ffn_fwd_gate": [("forward", "ffn_out")],
}
GRAD_GROUPS = {
    "ffn_in": ["w_ffn_in"],
    "ffn_out": ["w_ffn_out"],
    "mixer": ["w_out", "w_branch_ret", "w_branch_lru", "lru_wa", "lru_wx"],
    "in": ["w_in"],
}
GRAD_SCHEDULE = {
    "mix_bwd": [("pair", "ffn_in"), ("pair", "ffn_out")],
    "retention_bwd": [("chip", "ffn_in")],
    "lru_bwd": [("chip", "ffn_out"), ("sibling", "ffn_in")],
    "dw_in": [("sibling", "ffn_out"), ("pair", "mixer")],
    "in_proj_bwd_0": [("chip", "mixer"), ("pair", "in")],
    "in_proj_bwd_1": [("sibling", "mixer"), ("chip", "in")],
}


class _CommPlan:
    def __init__(self, shards, place):
        self.shards, self.place = shards, place
        self.late = {}
        self.recv_pair, self.sums, self.recv_chip, self.halves, self.final = {}, {}, {}, {}, {}

    def _grad_rider(self, stage, group, grads):
        names = GRAD_GROUPS[group]
        if stage == "pair":
            return _pair_exchange_rider(grads, names)
        if stage == "chip":
            return _chip_exchange_rider(self.sums, names)
        return _sibling_exchange_rider(self.halves, names)

    def _grad_after(self, stage, group, outs, grads):
        names = GRAD_GROUPS[group]
        if stage == "pair":
            for n, o in zip(names, outs):
                self.recv_pair[n] = o
                self.sums[n] = _pair_sum(n, grads[n], o, self.place)
        elif stage == "chip":
            for n, o in zip(names, outs):
                self.halves[n] = _chip_sum(n, grads[n], self.recv_pair[n], o, self.place)
        else:
            self.final.update(zip(names, outs))

    def riders(self, host, w, grads):
        if host in WEIGHT_SCHEDULE:
            return [_gather_rider(self.shards, WEIGHT_GROUPS[group]) if stage == "gather"
                    else _forward_rider(self.late, WEIGHT_GROUPS[group]) for stage, group in WEIGHT_SCHEDULE[host]]
        return [self._grad_rider(stage, group, grads) for stage, group in GRAD_SCHEDULE.get(host, [])]

    def after(self, host, rider_outs, w, grads):
        for (stage, group), outs in zip(WEIGHT_SCHEDULE.get(host, []), rider_outs):
            (self.late if stage == "gather" else w).update(zip(WEIGHT_GROUPS[group], outs))
        for (stage, group), outs in zip(GRAD_SCHEDULE.get(host, []), rider_outs):
            self._grad_after(stage, group, outs, grads)

    def finish(self, partial):
        outs, (blocks,) = _run_riders("tail_exchange", [_sibling_exchange_rider(self.halves, GRAD_GROUPS["in"]),
                                                        _small_exchange_rider(partial)])
        self.final.update(zip(GRAD_GROUPS["in"], outs))
        return self.final, blocks


def _adamw_math(w, g, m, v):
    m = ADAM_B1 * m + (1.0 - ADAM_B1) * g
    v = ADAM_B2 * v + (1.0 - ADAM_B2) * (g * g)
    m_hat = m / (1.0 - ADAM_B1 ** ADAM_STEP)
    v_hat = v / (1.0 - ADAM_B2 ** ADAM_STEP)
    delta = -ADAM_LR * (m_hat / (jnp.sqrt(v_hat) + ADAM_EPS) + ADAM_WD * w)
    return delta, m, v


def _adamw(name, g, w, m, v):
    rows, cols = g.shape
    tr = rows // 4 if rows % 32 == 0 else rows

    def body(g_ref, w_ref, m_ref, v_ref, go_ref, d_ref, mo_ref, vo_ref):
        gv = g_ref[...]
        delta, m2, v2 = _adamw_math(w_ref[...], gv, m_ref[...], v_ref[...])
        go_ref[...] = gv
        d_ref[...] = delta
        mo_ref[...] = m2
        vo_ref[...] = v2

    spec = pl.BlockSpec((tr, cols), lambda i: (i, 0))
    return pl.pallas_call(
        body,
        name="adamw_" + name,
        grid=(rows // tr,),
        in_specs=[spec] * 4,
        out_specs=[spec] * 4,
        out_shape=[jax.ShapeDtypeStruct((rows, cols), F32)] * 4,
        compiler_params=_params(("arbitrary",)),
    )(g, w, m, v)


SMALL_ROWS = 48
VEC_ROWS = {"final_norm_w": 1, "ffn_norm_w": 8, "mix_norm_w": 16, "conv_b": 24, "lru_ba": 25, "lru_bx": 26, "lru_lambda": 27}
VEC_NAMES = list(VEC_ROWS)
CONV_W_ROW = 28
META_ROW = 32


def _small_exchange_rider(partial):
    def build(ins, outs, sems):
        local_sem, send_sem, recv_sem = sems
        _, _, c, s = _place()
        me = 2 * s + c
        cp = pltpu.make_async_copy(ins[0], outs[0].at[me], local_sem)
        starts, waits = [cp], [cp.wait]
        for k in range(1, 8):
            peer = jnp.bitwise_xor(me, k)
            dev = (peer // 4, (peer // 2) % 2, peer % 2)
            rd = _remote(ins[0], outs[0].at[me], send_sem.at[k - 1], recv_sem.at[k - 1], dev)
            starts.append(rd)
            waits.append(rd.wait_send)
            waits.append(_remote(ins[0], outs[0].at[peer], send_sem.at[k - 1], recv_sem.at[k - 1], dev).wait_recv)
        return starts, waits

    return _Rider([partial], [jax.ShapeDtypeStruct((8, SMALL_ROWS, D), F32)],
                  [pltpu.SemaphoreType.DMA, pltpu.SemaphoreType.DMA((7,)), pltpu.SemaphoreType.DMA((7,))], build)


def _small_update(blocks, place, vecs, conv, meta):
    nvec = len(VEC_NAMES)
    qcols = D // 4

    def in_order(ref):
        total = ref[0]
        for d in range(1, 8):
            total = total + ref[d]
        return total

    def body(pr, blocks_ref, cols_ref, *refs):
        vec_refs = refs[:3 * nvec]
        conv_refs = refs[3 * nvec:3 * nvec + 3]
        meta_refs = refs[3 * nvec + 3:3 * nvec + 6]
        outs = refs[3 * nvec + 6:]
        loss_ref, vec_out, conv_out, meta_out = outs[0], outs[1:5], outs[5:9], outs[9:13]
        tot, col = in_order(blocks_ref), in_order(cols_ref)
        loss_ref[...] = jnp.sum(tot[0:1, :], axis=1, keepdims=True)
        for o in vec_out:
            o[...] = jnp.zeros_like(o)
        for j, name in enumerate(VEC_NAMES):
            w, m, v = (r[...] for r in vec_refs[3 * j:3 * j + 3])
            g = tot[VEC_ROWS[name]:VEC_ROWS[name] + 1, :]
            if name == "lru_lambda":
                g = -g / (1.0 + jnp.exp(w))
            for o, val in zip(vec_out, (g,) + _adamw_math(w, g, m, v)):
                o[j:j + 1, :] = val
        for row, n_rows, ins, group in ((CONV_W_ROW, 4, conv_refs, conv_out), (META_ROW, N_META, meta_refs, meta_out)):
            g = col[row:row + n_rows, :]
            w, m, v = (r[...] for r in ins)
            for o, val in zip(group, (g,) + _adamw_math(w, g, m, v)):
                o[...] = val

    whole = lambda shape: pl.BlockSpec(shape, lambda i, pr: (0,) * len(shape))
    in_specs = [whole((8, SMALL_ROWS, D)), pl.BlockSpec((8, SMALL_ROWS, qcols), lambda i, pr: (0, 0, pr[0]))]
    in_specs += [whole((1, D))] * (3 * nvec) + [whole((4, qcols))] * 3 + [whole((N_META, qcols))] * 3
    out_shapes = [(1, 1)] + [(8, D)] * 4 + [(4, qcols)] * 4 + [(N_META, qcols)] * 4
    return pl.pallas_call(
        body,
        name="small_update",
        grid_spec=pltpu.PrefetchScalarGridSpec(num_scalar_prefetch=1, grid=(1,), in_specs=in_specs,
                                               out_specs=[whole(s) for s in out_shapes]),
        out_shape=[jax.ShapeDtypeStruct(s, F32) for s in out_shapes],
        compiler_params=_params(("arbitrary",)),
    )(place, blocks, blocks, *[a for t in vecs for a in t], *conv, *meta)


WEIGHT_ORDER = ["meta_tokens", "mix_norm_w", "w_in", "conv_w", "conv_b", "lru_wa", "lru_ba", "lru_wx", "lru_bx", "lru_lambda",
                "w_branch_ret", "w_branch_lru", "w_out", "ffn_norm_w", "w_ffn_in", "w_ffn_out", "final_norm_w"]


def kernel(x, meta_tokens, mix_norm_w, w_in, conv_w, conv_b, lru_wa, lru_ba, lru_wx, lru_bx, lru_lambda, w_branch_ret, w_branch_lru, w_out, ffn_norm_w, w_ffn_in, w_ffn_out, final_norm_w, loss_target, m_meta_tokens, m_mix_norm_w, m_w_in, m_conv_w, m_conv_b, m_lru_wa, m_lru_ba, m_lru_wx, m_lru_bx, m_lru_lambda, m_w_branch_ret, m_w_branch_lru, m_w_out, m_ffn_norm_w, m_w_ffn_in, m_w_ffn_out, m_final_norm_w, v_meta_tokens, v_mix_norm_w, v_w_in, v_conv_w, v_conv_b, v_lru_wa, v_lru_ba, v_lru_wx, v_lru_bx, v_lru_lambda, v_w_branch_ret, v_w_branch_lru, v_w_out, v_ffn_norm_w, v_w_ffn_in, v_w_ffn_out, v_final_norm_w):
    args = locals()
    wts = {n: args[n] for n in WEIGHT_ORDER}
    mom = {n: args["m_" + n] for n in WEIGHT_ORDER}
    var = {n: args["v_" + n] for n in WEIGHT_ORDER}
    place = jnp.stack([2 * lax.axis_index("x") + lax.axis_index("y"), lax.axis_index("c")]).astype(jnp.int32)

    shards = {n: wts[n][0].astype(BF16) for n in BIG_PIECES}
    shards["meta_tokens"] = wts["meta_tokens"]
    shards["conv_w"] = wts["conv_w"][0]
    plan = _CommPlan(shards, place)
    (first,) = _run_riders("gather_first", [_gather_rider(shards, FIRST_WEIGHTS)])
    w = dict(zip(FIRST_WEIGHTS, first))
    for n in VEC_NAMES:
        w[n] = wts[n].reshape(1, D)
    chips = jnp.bitwise_xor(place[0], jnp.array((0,) + SHARD_ORDER, dtype=jnp.int32))
    w["route"] = jnp.concatenate([place, jnp.stack([2 * chips, 2 * chips + 1], axis=1).reshape(8)])
    w["w_in_shard"] = shards["w_in"]

    grad_x, grad_head, _, stats = _local_step(x[0], loss_target[0], w, plan)
    partial = jnp.concatenate(stats + [grad_head[PAD_ROWS:]], axis=0)
    shard_grads, blocks = plan.finish(partial)

    out = {}
    for n in BIG_PIECES:
        shape2d = (-1, wts[n].shape[-1])
        res = _adamw(n, *[a.reshape(shape2d) for a in (shard_grads[n], wts[n], mom[n], var[n])])
        out[n] = [r.reshape(wts[n].shape) for r in res]

    vecs = [tuple(a[n].reshape(1, D) for a in (wts, mom, var)) for n in VEC_NAMES]
    conv = tuple(a["conv_w"][0] for a in (wts, mom, var))
    meta = tuple(a["meta_tokens"] for a in (wts, mom, var))
    res = _small_update(blocks, place, vecs, conv, meta)
    loss = res[0].reshape(())
    for j, n in enumerate(VEC_NAMES):
        out[n] = [r[j].reshape(wts[n].shape) for r in res[1:5]]
    out["conv_w"] = [r.reshape(wts["conv_w"].shape) for r in res[5:9]]
    out["meta_tokens"] = list(res[9:13])

    return (loss, grad_x.reshape(x.shape)) + tuple(out[n][kind] for kind in range(4) for n in WEIGHT_ORDER)
```

```python
import functools
import math

import jax
import jax.numpy as jnp
from jax import lax
from jax.experimental import pallas as pl
from jax.experimental.pallas import tpu as pltpu

F32 = jnp.float32
BF16 = jnp.bfloat16

LANES = 128
BF16_ROWS = 16

D = 1024
HEADS = 8
DH = 128
CHUNK = 256
N_META = 16
FRONT = 256
PAD_ROWS = FRONT - N_META
LRU_BLOCKS = 4
LRU_BLOCK = 256
LRU_C = 8.0
FFN = 2816
FFN_HALF = FFN // 2
IN_COLS = 8 * D
ROPE_BASE = 10000.0
EPS = 1e-6
QK_SCALE = DH ** -0.5

ADAM_LR = 0.001
ADAM_B1 = 0.9
ADAM_B2 = 0.999
ADAM_EPS = 1e-08
ADAM_WD = 0.01
ADAM_STEP = 10

TM = 256
TM_HEAVY = 768
MXU_TILE = 256
FFN_SUB = 2 * MXU_TILE
_FFN_SUBS = [(a, min(a + FFN_SUB, FFN)) for a in range(0, FFN, FFN_SUB)]
TM_MIX_BWD = 384
TM_LRU = 384
TM_MIX_FWD = 384
TM_IN_PROJ = 1408
VMEM_LIMIT = 60 * 1024 * 1024
TN_VMEM_BUDGET = 40 * 1024 * 1024

NT_DIMS = (((1,), (1,)), ((), ()))
TN_DIMS = (((0,), (0,)), ((), ()))
MESH = pl.DeviceIdType.MESH


def _params(sem=None):
    if sem is None:
        return pltpu.CompilerParams(vmem_limit_bytes=VMEM_LIMIT)
    return pltpu.CompilerParams(dimension_semantics=sem, vmem_limit_bytes=VMEM_LIMIT)


def _dot(a, b):
    return jnp.dot(a, b, preferred_element_type=F32)


def _dot_nt(a, b):
    return lax.dot_general(a, b, NT_DIMS, preferred_element_type=F32)


def _dot_tn(a, b):
    return lax.dot_general(a, b, TN_DIMS, preferred_element_type=F32)


def _sigmoid(z):
    return 1.0 / (1.0 + jnp.exp(-z))


def _log1p(x):
    return jnp.where(x < 1e-3, x * (1.0 - x * (0.5 - x * (1.0 / 3.0))), jnp.log(1.0 + x))


def _softplus(x):
    return jnp.maximum(x, 0.0) + _log1p(jnp.exp(-jnp.abs(x)))


def _one_minus_square(a, log_a):
    x = 2.0 * log_a
    series = -x * (1.0 + x * (0.5 + x * (1.0 / 6.0)))
    return jnp.where(x > -0.02, series, 1.0 - a * a)


_GELU_K = math.sqrt(2.0 / math.pi)


def _gelu_and_grad(x):
    inner = _GELU_K * (x + 0.044715 * x * x * x)
    t = jnp.tanh(inner)
    val = 0.5 * x * (1.0 + t)
    grad = 0.5 * (1.0 + t) + 0.5 * x * (1.0 - t * t) * _GELU_K * (1.0 + 3.0 * 0.044715 * x * x)
    return val, grad


def _row_ids(i, rows, shape):
    return i * rows + lax.broadcasted_iota(jnp.int32, shape, 0)


def _rope_tables(rows):
    inv_freq = ROPE_BASE ** (-jnp.arange(0, DH, 2, dtype=F32) / DH)
    block_pos = jnp.arange(rows // FRONT, dtype=jnp.int32) * FRONT - PAD_ROWS
    coarse = block_pos.astype(F32)[:, None] * inv_freq[None, :]
    fine = jnp.arange(FRONT, dtype=F32)[:, None] * inv_freq[None, :]
    ca, sa = jnp.cos(coarse)[:, None, :], jnp.sin(coarse)[:, None, :]
    cb, sb = jnp.cos(fine)[None, :, :], jnp.sin(fine)[None, :, :]
    cos = (ca * cb - sa * sb).reshape(rows, DH // 2)
    sin = (sa * cb + ca * sb).reshape(rows, DH // 2)
    return jnp.concatenate([cos, cos], axis=1), jnp.concatenate([-sin, sin], axis=1)


def _decay_tables():
    log_g = jnp.log(1.0 - 2.0 ** (-5.0 - jnp.arange(HEADS, dtype=F32)))
    idx = jnp.arange(CHUNK, dtype=F32)
    diff = idx[:, None] - idx[None, :]
    intra = jnp.where(diff[None] >= 0, jnp.exp(jnp.maximum(diff, 0.0)[None] * log_g[:, None, None]), 0.0)
    q_decay = jnp.exp((idx + 1.0)[:, None] * log_g[None, :])
    k_decay = jnp.exp((CHUNK - 1.0 - idx)[:, None] * log_g[None, :])
    chunk_decay = jnp.exp(CHUNK * log_g)
    wide = lambda a: jnp.repeat(a, DH, axis=-1)
    return intra, jnp.swapaxes(intra, 1, 2), wide(q_decay), wide(k_decay), wide(chunk_decay[None, :])


def _norm1(head, x2d, norm_w, riders=()):
    rows = FRONT + x2d.shape[0]
    tm = TM_IN_PROJ if rows % TM_IN_PROJ == 0 else TM_MIX_FWD
    nt = rows // tm

    def body(head_hbm, x_hbm, nw_ref, u_ref, rstd_ref, h0_sc, h0_sem):
        slot = _frame_rows(head_hbm, x_hbm, h0_sc, h0_sem, pl.program_id(0), tm, nt)
        hv = h0_sc[slot]
        rs = lax.rsqrt(jnp.mean(hv * hv, axis=-1, keepdims=True) + EPS)
        u_ref[...] = ((hv * rs) * nw_ref[...]).astype(BF16)
        rstd_ref[...] = rs

    return _hosted_call(
        body,
        name="norm1",
        grid=(nt,),
        in_specs=[_ANY, _ANY, pl.BlockSpec((1, D), lambda i: (0, 0))],
        out_specs=[pl.BlockSpec((tm, D), lambda i: (i, 0)), pl.BlockSpec((tm, 1), lambda i: (i, 0))],
        out_shape=[jax.ShapeDtypeStruct((rows, D), BF16), jax.ShapeDtypeStruct((rows, 1), F32)],
        scratch_shapes=[pltpu.VMEM((2, tm, D), F32), pltpu.SemaphoreType.DMA((3,))],
        args=(head, x2d, norm_w),
        riders=riders,
    )


def _heavy_tile(rows):
    return TM_HEAVY if rows % TM_HEAVY == 0 else TM


def _lru_tile(rows):
    return TM_LRU if rows % TM_LRU == 0 else TM


def _swap_3_4(group):
    return jnp.where(group == 3, 4, jnp.where(group == 4, 3, group))


SHARD_ORDER = (2, 3, 1)
LRU_IN_COL = 3
GATES_COL = 1


def _in_proj(u, w_shard, route, cos_t, sin_t, riders=()):
    rows = u.shape[0]
    tm = TM_IN_PROJ if rows % TM_IN_PROJ == 0 else _heavy_tile(rows)
    nt = rows // tm
    kind, shard = BIG_PIECES["w_in"]

    def body(route_ref, u_hbm, wsh_ref, cos_ref, sin_ref, proj_ref, wfull_ref,
             u_sc, w_sc, u_sem, w_sem, local_sem, ici_send, ici_recv, fwd_send, fwd_recv):
        g, i = pl.program_id(0), pl.program_id(1)
        s, c = route_ref[0], route_ref[1]
        gid = route_ref[2 + g]
        sibling = (s // 2, s % 2, 1 - c)
        local = pltpu.make_async_copy(wsh_ref, _full_region(wfull_ref, kind, shard, s, None), local_sem)
        u_copies = [pltpu.make_async_copy(u_hbm.at[pl.ds(t * tm, tm)], u_sc.at[t], u_sem.at[t]) for t in range(nt)]
        sends, arrivals = [], []
        for k in (1, 2, 3):
            s2, dev = _other_chip(s, c, k)
            sends.append(_remote(_shard_region(wsh_ref, shard, c), _full_region(wfull_ref, kind, shard, s, c),
                                 ici_send.at[k - 1], ici_recv.at[k - 1], dev))
            mine = _full_region(wfull_ref, kind, shard, s2, c)
            theirs = _full_region(wfull_ref, kind, shard, s2, 1 - c)
            arrivals.append((_remote(mine, mine, ici_send.at[k - 1], ici_recv.at[k - 1], dev),
                             _remote(mine, mine, fwd_send.at[k - 1], fwd_recv.at[k - 1], sibling),
                             _remote(theirs, theirs, fwd_send.at[k - 1], fwd_recv.at[k - 1], sibling)))

        slot = g % 2
        last_tile = i == nt - 1

        def block_copy(src, col, to_slot):
            return pltpu.make_async_copy(src.at[:, pl.ds(pl.multiple_of(col * D, D), D)], w_sc.at[to_slot],
                                         w_sem.at[to_slot])

        @pl.when(jnp.logical_and(g == 0, i == 0))
        def _():
            for cp in sends:
                cp.start()
            local.start()
            for cp in u_copies:
                cp.start()
            cp = block_copy(wsh_ref, 0, 0)
            cp.start()
            cp.wait()

        @pl.when(jnp.logical_and(last_tile, g == 0))
        def _():
            block_copy(wsh_ref, 1, 1).start()

        for pos, k in enumerate(SHARD_ORDER, start=1):
            arrived, forward, forwarded = arrivals[k - 1]

            @pl.when(jnp.logical_and(last_tile, g == 2 * pos - 1))
            def _():
                arrived.wait_recv()
                forward.start()
                forwarded.wait_recv()
                block_copy(wfull_ref, route_ref[2 + 2 * pos], 0).start()

            @pl.when(jnp.logical_and(last_tile, g == 2 * pos))
            def _():
                block_copy(wfull_ref, route_ref[3 + 2 * pos], 1).start()

        @pl.when(jnp.logical_and(i == 0, g > 0))
        def _():
            block_copy(wfull_ref, 0, slot).wait()

        for t in range(nt):
            @pl.when(jnp.logical_and(g == 0, i == t))
            def _():
                u_copies[t].wait()

        acc = _dot(u_sc[i], w_sc[slot])

        @pl.when(gid < 2)
        def _():
            scale = jnp.where(gid == 1, QK_SCALE, 1.0).astype(F32)
            for h in range(HEADS):
                sl = slice(h * DH, (h + 1) * DH)
                blk = acc[:, sl]
                out = (blk * cos_ref[...] + pltpu.roll(blk, DH // 2, axis=1) * sin_ref[...]) * scale
                proj_ref[:, sl] = out.astype(BF16)

        @pl.when(gid >= 2)
        def _():
            proj_ref[...] = acc.astype(BF16)

        @pl.when(jnp.logical_and(g == 7, i == nt - 1))
        def _():
            local.wait()
            for cp in sends:
                cp.wait_send()
            for _, forward, _ in arrivals:
                forward.wait_send()

    return _hosted_call(
        body,
        name="in_proj",
        grid=(8, nt),
        in_specs=[
            _ANY, _ANY,
            pl.BlockSpec((tm, DH), lambda g, i, rt: (i, 0)),
            pl.BlockSpec((tm, DH), lambda g, i, rt: (i, 0)),
        ],
        out_specs=[pl.BlockSpec((tm, D), lambda g, i, rt: (i, _swap_3_4(rt[2 + g]))), _ANY],
        out_shape=[jax.ShapeDtypeStruct((rows, IN_COLS), BF16), jax.ShapeDtypeStruct((D, IN_COLS), BF16)],
        scratch_shapes=[
            pltpu.VMEM((nt, tm, D), BF16), pltpu.VMEM((2, D, D), BF16),
            pltpu.SemaphoreType.DMA((nt,)), pltpu.SemaphoreType.DMA((2,)), pltpu.SemaphoreType.DMA,
            pltpu.SemaphoreType.DMA((3,)), pltpu.SemaphoreType.DMA((3,)),
            pltpu.SemaphoreType.DMA((3,)), pltpu.SemaphoreType.DMA((3,)),
        ],
        args=(u, w_shard, cos_t, sin_t),
        riders=riders,
        prefetch=route,
        riders_after_body=True,
    )


def _retention_fwd(proj_bf, intra, q_dec, k_dec, c_dec, riders=()):
    rows = proj_bf.shape[0]
    nc = rows // CHUNK

    def body(q_ref, k_ref, v_ref, m_ref, qd_ref, kd_ref, cd_ref, o_ref, sprev_ref, s_sc):
        @pl.when(pl.program_id(0) == 0)
        def _():
            s_sc[...] = jnp.zeros_like(s_sc)

        for h in range(HEADS):
            sl = slice(h * DH, (h + 1) * DH)
            q, k, v = q_ref[:, sl], k_ref[:, sl], v_ref[:, sl]
            state = s_sc[h]
            state_b = state.astype(BF16)
            sprev_ref[0, h] = state_b
            s = _dot_nt(q, k) * m_ref[h]
            inner = _dot(s.astype(BF16), v)
            cross = _dot(q, state_b) * qd_ref[:, sl]
            o_ref[:, sl] = (inner + cross).astype(BF16)
            k_scaled = (k.astype(F32) * kd_ref[:, sl]).astype(BF16)
            s_sc[h] = state * cd_ref[:, sl] + _dot_tn(k_scaled, v)

    chunk_spec = lambda col: pl.BlockSpec((CHUNK, D), lambda c: (c, col))
    const2 = lambda shape: pl.BlockSpec(shape, lambda c: (0, 0))
    return _hosted_call(
        body,
        name="retention_fwd",
        grid=(nc,),
        in_specs=[
            chunk_spec(0), chunk_spec(1), chunk_spec(2),
            pl.BlockSpec((HEADS, CHUNK, CHUNK), lambda c: (0, 0, 0)),
            const2((CHUNK, D)), const2((CHUNK, D)), const2((1, D)),
        ],
        out_specs=[
            pl.BlockSpec((CHUNK, D), lambda c: (c, 0)),
            pl.BlockSpec((1, HEADS, DH, DH), lambda c: (c, 0, 0, 0)),
        ],
        out_shape=[
            jax.ShapeDtypeStruct((rows, D), BF16),
            jax.ShapeDtypeStruct((nc, HEADS, DH, DH), BF16),
        ],
        scratch_shapes=[pltpu.VMEM((HEADS, DH, DH), F32)],
        args=(proj_bf, proj_bf, proj_bf, intra, q_dec, k_dec, c_dec),
        riders=riders,
    )


def _shift_down(x, first8_prev, d):
    rolled = pltpu.roll(x, d, axis=0)
    head = pltpu.roll(jnp.concatenate([first8_prev, x[0:8]], axis=0), d, axis=0)[8:16]
    return rolled, head


def _conv_and_gates(x, prev8, cw_ref, cb_ref, wa_ref, wx_ref, ba_ref, bx_ref, lam_ref, c_sc):
    cw = cw_ref[...]
    conv = cb_ref[...] + cw[3:4] * x
    head = cb_ref[...] + cw[3:4] * x[0:8]
    for d in (1, 2, 3):
        rolled, hd = _shift_down(x, prev8, d)
        conv = conv + cw[3 - d:4 - d] * rolled
        head = head + cw[3 - d:4 - d] * hd
    c_sc[...] = conv
    c_sc[0:8, :] = head
    c = c_sc[...]
    zr, zi = [], []
    for g in range(LRU_BLOCKS):
        sl = slice(g * LRU_BLOCK, (g + 1) * LRU_BLOCK)
        cg = c[:, sl].astype(BF16)
        zr.append(_dot(cg, wa_ref[g]))
        zi.append(_dot(cg, wx_ref[g]))
    r = _sigmoid(jnp.concatenate(zr, axis=1) + ba_ref[...])
    gate_i = _sigmoid(jnp.concatenate(zi, axis=1) + bx_ref[...])
    sp = _softplus(-lam_ref[...])
    log_a = (-LRU_C) * r * sp
    a = jnp.exp(log_a)
    mult = jnp.sqrt(_one_minus_square(a, log_a))
    return c, r, gate_i, a, mult, log_a


def _lru_fwd(proj, conv_w, conv_b, wa, wx, ba, bx, lam, riders=()):
    rows = proj.shape[0]
    TM = _lru_tile(rows)
    nt = rows // TM

    def body(x_ref, cw_ref, cb_ref, wa_ref, wx_ref, ba_ref, bx_ref, lam_ref,
             h_ref, c_ref, r_ref, i_ref, la_ref, mult_ref, prev_sc, carry_sc, c_sc, a_sc, u_sc, h_sc):
        i = pl.program_id(0)

        @pl.when(i == 0)
        def _():
            prev_sc[...] = jnp.zeros_like(prev_sc)
            carry_sc[...] = jnp.zeros_like(carry_sc)

        x = x_ref[...].astype(F32)
        c, r, gate_i, a, mult, log_a = _conv_and_gates(x, prev_sc[...], cw_ref, cb_ref, wa_ref, wx_ref, ba_ref, bx_ref,
                                                       lam_ref, c_sc)
        for ref, val in ((c_ref, c), (r_ref, r), (i_ref, gate_i), (la_ref, log_a), (mult_ref, mult)):
            ref[...] = val.astype(BF16)
        prev_sc[...] = x[TM - 8:TM]
        valid = _row_ids(i, TM, (TM, D)) >= PAD_ROWS
        a_sc[...] = a
        u_sc[...] = jnp.where(valid, mult * gate_i * c, 0.0)
        row8 = lax.broadcasted_iota(jnp.int32, (8, D), 0)

        def group(gi, hprev):
            r0 = pl.multiple_of(gi * 8, 8)
            aa = a_sc[pl.ds(r0, 8), :]
            uu = u_sc[pl.ds(r0, 8), :]
            for d in (1, 2, 4):
                a_sh = jnp.where(row8 >= d, pltpu.roll(aa, d, axis=0), 1.0)
                u_sh = jnp.where(row8 >= d, pltpu.roll(uu, d, axis=0), 0.0)
                uu = uu + aa * u_sh
                aa = aa * a_sh
            hb = aa * hprev + uu
            h_sc[pl.ds(r0, 8), :] = hb
            return hb[7:8, :]

        hlast = lax.fori_loop(0, TM // 8, group, carry_sc[0:1, :])
        carry_sc[0:1, :] = hlast
        h_ref[...] = h_sc[...].astype(BF16)

    vec = pl.BlockSpec((1, D), lambda i: (0, 0))
    wspec = pl.BlockSpec((LRU_BLOCKS, LRU_BLOCK, LRU_BLOCK), lambda i: (0, 0, 0))
    return _hosted_call(
        body,
        name="lru_fwd",
        grid=(nt,),
        in_specs=[
            pl.BlockSpec((TM, D), lambda i: (i, LRU_IN_COL)),
            pl.BlockSpec((4, D), lambda i: (0, 0)),
            vec, wspec, wspec, vec, vec, vec,
        ],
        out_specs=[pl.BlockSpec((TM, D), lambda i: (i, 0)) for _ in range(6)],
        out_shape=[jax.ShapeDtypeStruct((rows, D), BF16) for _ in range(6)],
        scratch_shapes=[
            pltpu.VMEM((8, D), F32), pltpu.VMEM((8, D), F32),
            pltpu.VMEM((TM, D), F32), pltpu.VMEM((TM, D), F32), pltpu.VMEM((TM, D), F32), pltpu.VMEM((TM, D), F32),
        ],
        args=(proj, conv_w, conv_b, wa, wx, ba, bx, lam),
        riders=riders,
    )


def _group_norm(o):
    outs, rstds = [], []
    for h in range(HEADS):
        oh = o[:, h * DH:(h + 1) * DH]
        rs = lax.rsqrt(jnp.mean(oh * oh, axis=-1, keepdims=True) + EPS)
        outs.append(oh * rs)
        rstds.append(rs)
    return jnp.concatenate(outs, axis=1), rstds


def _frame_rows(head_hbm, x_hbm, buf, sems, i, tm, nt):
    slot = i % 2

    @pl.when(i == 0)
    def _():
        first = [pltpu.make_async_copy(head_hbm, buf.at[0, pl.ds(0, FRONT)], sems.at[2]),
                 pltpu.make_async_copy(x_hbm.at[pl.ds(0, tm - FRONT)], buf.at[0, pl.ds(FRONT, tm - FRONT)], sems.at[0])]
        for cp in first:
            cp.start()
        for cp in first:
            cp.wait()

    @pl.when(i + 1 < nt)
    def _():
        start = pl.multiple_of((i + 1) * tm - FRONT, LANES)
        pltpu.make_async_copy(x_hbm.at[pl.ds(start, tm)], buf.at[1 - slot], sems.at[1 - slot]).start()

    @pl.when(i > 0)
    def _():
        pltpu.make_async_copy(x_hbm.at[pl.ds(0, tm)], buf.at[slot], sems.at[slot]).wait()

    return slot


def _mix_fwd(head, x2d, o, proj, h_lru, w_br, w_bl, w_o, ffn_norm_w, riders=()):
    rows = o.shape[0]
    tm = TM_MIX_FWD
    assert rows % tm == 0 and tm > FRONT
    nt = rows // tm

    def body(head_hbm, x_hbm, o_ref, gates_ref, hl_ref, wbr_ref, wbl_ref, wo_ref, nw_ref,
             yret_ref, ylru_ref, h1_ref, u2_ref, rstd_ref, h0_sc, h0_sem):
        i = pl.program_id(0)
        slot = _frame_rows(head_hbm, x_hbm, h0_sc, h0_sem, i, tm, nt)
        gate = lambda j: gates_ref[:, j * D:(j + 1) * D].astype(F32)
        on, _ = _group_norm(o_ref[...].astype(F32))
        gret = gate(0)
        a_ret = (gret * _sigmoid(gret) * on).astype(BF16)
        y_ret = _dot(a_ret, wbr_ref[...])
        gl, _ = _gelu_and_grad(gate(1))
        a_lru = (gl * hl_ref[...].astype(F32)).astype(BF16)
        y_lru = _dot(a_lru, wbl_ref[...])
        mixed = (_sigmoid(gate(2)) * y_ret + _sigmoid(gate(3)) * y_lru).astype(BF16)
        delta = _dot(mixed, wo_ref[...])
        yret_ref[...] = y_ret.astype(BF16)
        ylru_ref[...] = y_lru.astype(BF16)
        h1 = h0_sc[slot] + delta
        rs = lax.rsqrt(jnp.mean(h1 * h1, axis=-1, keepdims=True) + EPS)
        h1_ref[...] = h1
        u2_ref[...] = ((h1 * rs) * nw_ref[...]).astype(BF16)
        rstd_ref[...] = rs

    tile = lambda col: pl.BlockSpec((tm, D), lambda i: (i, col))
    wspec = pl.BlockSpec((D, D), lambda i: (0, 0))
    return _hosted_call(
        body,
        name="mix_fwd",
        grid=(nt,),
        in_specs=[
            _ANY, _ANY,
            tile(0), pl.BlockSpec((tm, 4 * D), lambda i: (i, GATES_COL)), tile(0),
            wspec, wspec, wspec,
            pl.BlockSpec((1, D), lambda i: (0, 0)),
        ],
        out_specs=[tile(0), tile(0), tile(0), tile(0), pl.BlockSpec((tm, 1), lambda i: (i, 0))],
        out_shape=[
            jax.ShapeDtypeStruct((rows, D), BF16), jax.ShapeDtypeStruct((rows, D), BF16),
            jax.ShapeDtypeStruct((rows, D), F32), jax.ShapeDtypeStruct((rows, D), BF16),
            jax.ShapeDtypeStruct((rows, 1), F32),
        ],
        scratch_shapes=[pltpu.VMEM((2, tm, D), F32), pltpu.SemaphoreType.DMA((3,))],
        args=(head, x2d, o, proj, h_lru, w_br, w_bl, w_o, ffn_norm_w),
        riders=riders,
    )


def _ffn_fwd_gate(u2, w_ffn_in, riders=()):
    rows = u2.shape[0]
    tm = _heavy_tile(rows)
    hid = lambda: pl.BlockSpec((tm, FFN), lambda i: (i, 0))

    def body(u2_ref, w_hbm, silu_ref, dsilu_ref, act_ref, w_sc, w_sem):
        @pl.when(pl.program_id(0) == 0)
        def _():
            cp = pltpu.make_async_copy(w_hbm, w_sc, w_sem)
            cp.start()
            cp.wait()

        u2 = u2_ref[...]
        for a, b in _FFN_SUBS:
            g = _dot(u2, w_sc[:, a:b])
            up = _dot(u2, w_sc[:, FFN + a:FFN + b])
            sg = _sigmoid(g)
            silu = g * sg
            silu_ref[:, a:b] = silu.astype(BF16)
            dsilu_ref[:, a:b] = (up * (sg * (1.0 + g * (1.0 - sg)))).astype(BF16)
            act_ref[:, a:b] = (silu * up).astype(BF16)

    return _hosted_call(
        body,
        name="ffn_fwd_gate",
        grid=(rows // tm,),
        in_specs=[pl.BlockSpec((tm, D), lambda i: (i, 0)), _ANY],
        out_specs=[hid(), hid(), hid()],
        out_shape=[jax.ShapeDtypeStruct((rows, FFN), BF16)] * 3,
        scratch_shapes=[pltpu.VMEM((D, 2 * FFN), BF16), pltpu.SemaphoreType.DMA],
        args=(u2, w_ffn_in),
        riders=riders,
    )


def _ffn_out_loss(act, h1, w_ffn_out, target, final_norm_w):
    rows = act.shape[0]
    tm = _heavy_tile(rows)
    nt = rows // tm

    def body(act_ref, h1_ref, wo_ref, fnw_ref, tgt_hbm, dh2_ref, stats_ref, tgt_sc, tgt_sem):
        i = pl.program_id(0)
        slot = i % 2

        @pl.when(i == 0)
        def _():
            stats_ref[...] = jnp.zeros_like(stats_ref)
            tgt_sc[0, 0:FRONT, :] = jnp.zeros((FRONT, D), F32)
            cp = pltpu.make_async_copy(tgt_hbm.at[pl.ds(0, tm - FRONT)], tgt_sc.at[0, pl.ds(FRONT, tm - FRONT)], tgt_sem.at[0])
            cp.start()
            cp.wait()

        @pl.when(i + 1 < nt)
        def _():
            start = pl.multiple_of((i + 1) * tm - FRONT, FRONT)
            pltpu.make_async_copy(tgt_hbm.at[pl.ds(start, tm)], tgt_sc.at[1 - slot], tgt_sem.at[1 - slot]).start()

        @pl.when(i > 0)
        def _():
            pltpu.make_async_copy(tgt_hbm.at[pl.ds(0, tm)], tgt_sc.at[slot], tgt_sem.at[slot]).wait()

        h2 = h1_ref[...] + _dot(act_ref[...], wo_ref[...])
        rs = lax.rsqrt(jnp.mean(h2 * h2, axis=-1, keepdims=True) + EPS)
        n = h2 * rs
        fnw = fnw_ref[...]
        valid = _row_ids(i, tm, (tm, D)) >= FRONT
        diff = jnp.where(valid, n * fnw - tgt_sc[slot], 0.0)
        dy = diff * (1.0 / D)
        stats_ref[0:1, :] += (0.5 / D) * jnp.sum(diff * diff, axis=0, keepdims=True)
        stats_ref[1:2, :] += jnp.sum(dy * n, axis=0, keepdims=True)
        dn = dy * fnw
        dh2_ref[...] = rs * (dn - n * jnp.mean(dn * n, axis=-1, keepdims=True))

    return pl.pallas_call(
        body,
        name="ffn_out_loss",
        grid=(nt,),
        in_specs=[
            pl.BlockSpec((tm, FFN), lambda i: (i, 0)),
            pl.BlockSpec((tm, D), lambda i: (i, 0)),
            pl.BlockSpec((FFN, D), lambda i: (0, 0)),
            pl.BlockSpec((1, D), lambda i: (0, 0)),
            _ANY,
        ],
        out_specs=[pl.BlockSpec((tm, D), lambda i: (i, 0)), pl.BlockSpec((8, D), lambda i: (0, 0))],
        out_shape=[jax.ShapeDtypeStruct((rows, D), F32), jax.ShapeDtypeStruct((8, D), F32)],
        scratch_shapes=[pltpu.VMEM((2, tm, D), F32), pltpu.SemaphoreType.DMA((2,))],
        compiler_params=_params(("arbitrary",)),
    )(act, h1, w_ffn_out, final_norm_w, target)


def _ffn_bwd(dh2, silu, dsilu, h1, rstd2, w_ffn_in, w_ffn_out, ffn_norm_w):
    rows = dh2.shape[0]
    tm = _heavy_tile(rows)
    blk = lambda: pl.BlockSpec((tm, FFN), lambda i: (i, 0))

    def gate_body(dh2_ref, silu_ref, dsilu_ref, wo_hbm, dg_ref, dup_ref, dh2b_ref, wo_sc, wo_sem):
        @pl.when(pl.program_id(0) == 0)
        def _():
            cp = pltpu.make_async_copy(wo_hbm, wo_sc, wo_sem)
            cp.start()
            cp.wait()

        dh2b = dh2_ref[...].astype(BF16)
        dh2b_ref[...] = dh2b
        for a, b in _FFN_SUBS:
            dact = _dot_nt(dh2b, wo_sc[a:b, :])
            dup_ref[:, a:b] = (dact * silu_ref[:, a:b].astype(F32)).astype(BF16)
            dg_ref[:, a:b] = (dact * dsilu_ref[:, a:b].astype(F32)).astype(BF16)

    dg, dup, dh2b = pl.pallas_call(
        gate_body,
        name="ffn_bwd_gate",
        grid=(rows // tm,),
        in_specs=[pl.BlockSpec((tm, D), lambda i: (i, 0)), blk(), blk(), _ANY],
        out_specs=[blk(), blk(), pl.BlockSpec((tm, D), lambda i: (i, 0))],
        out_shape=[jax.ShapeDtypeStruct((rows, FFN), BF16)] * 2 + [jax.ShapeDtypeStruct((rows, D), BF16)],
        scratch_shapes=[pltpu.VMEM((FFN, D), BF16), pltpu.SemaphoreType.DMA],
        compiler_params=_params(("arbitrary",)),
    )(dh2, silu, dsilu, w_ffn_out)

    def body(dg_ref, dup_ref, dh2_ref, h1_ref, rstd_ref, w_hbm, nw_ref, dh1_ref, stats_ref, w_sc, w_sem):
        @pl.when(pl.program_id(0) == 0)
        def _():
            stats_ref[...] = jnp.zeros_like(stats_ref)
            cp = pltpu.make_async_copy(w_hbm, w_sc, w_sem)
            cp.start()
            cp.wait()

        du = _dot_nt(dg_ref[...], w_sc[:, 0:FFN]) + _dot_nt(dup_ref[...], w_sc[:, FFN:2 * FFN])
        rs = rstd_ref[...]
        n = h1_ref[...] * rs
        stats_ref[0:1, :] += jnp.sum(du * n, axis=0, keepdims=True)
        dn = du * nw_ref[...]
        dh1_ref[...] = dh2_ref[...] + rs * (dn - n * jnp.mean(dn * n, axis=-1, keepdims=True))

    row = lambda width: pl.BlockSpec((tm, width), lambda i: (i, 0))
    dh1, stats = pl.pallas_call(
        body,
        name="ffn_bwd_in",
        grid=(rows // tm,),
        in_specs=[row(FFN), row(FFN), row(D), row(D), row(1), _ANY, pl.BlockSpec((1, D), lambda i: (0, 0))],
        out_specs=[row(D), pl.BlockSpec((8, D), lambda i: (0, 0))],
        out_shape=[jax.ShapeDtypeStruct((rows, D), F32), jax.ShapeDtypeStruct((8, D), F32)],
        scratch_shapes=[pltpu.VMEM((D, 2 * FFN), BF16), pltpu.SemaphoreType.DMA],
        compiler_params=_params(("arbitrary",)),
    )(dg, dup, dh2, h1, rstd2, w_ffn_in, ffn_norm_w)
    return dg, dup, dh2b, dh1, stats


def _mix_bwd(dh1, o, proj, h_lru, y_ret, y_lru, w_br, w_bl, w_o, riders=()):
    rows = dh1.shape[0]
    tm = TM_MIX_BWD
    nt = rows // tm

    def body(dh1_ref, o_ref, gates_ref, hl_ref, yret_ref, ylru_ref, wbr_hbm, wbl_hbm, wo_hbm,
             dproj_ref, do_ref, dhl_ref, mixed_ref, aret_ref, alru_ref, dyret_ref, dylru_ref, w_sc, w_sem):
        @pl.when(pl.program_id(0) == 0)
        def _():
            copies = [pltpu.make_async_copy(src, w_sc.at[q], w_sem.at[q]) for q, src in enumerate((wbr_hbm, wbl_hbm, wo_hbm))]
            for cp in copies:
                cp.start()
            for cp in copies:
                cp.wait()

        wbr_ref, wbl_ref, wo_ref = w_sc.at[0], w_sc.at[1], w_sc.at[2]
        gate = lambda j: gates_ref[:, j * D:(j + 1) * D].astype(F32)
        dmixed = _dot_nt(dh1_ref[...].astype(BF16), wo_ref[...])
        y_ret, y_lru = yret_ref[...].astype(F32), ylru_ref[...].astype(F32)
        sa, sb = _sigmoid(gate(2)), _sigmoid(gate(3))
        mixed_ref[...] = (sa * y_ret + sb * y_lru).astype(BF16)
        dga = dmixed * y_ret * sa * (1.0 - sa)
        dgb = dmixed * y_lru * sb * (1.0 - sb)
        dy_ret = (dmixed * sa).astype(BF16)
        dy_lru = (dmixed * sb).astype(BF16)
        dyret_ref[...] = dy_ret
        dylru_ref[...] = dy_lru
        da_ret = _dot_nt(dy_ret, wbr_ref[...])
        da_lru = _dot_nt(dy_lru, wbl_ref[...])

        gret = gate(0)
        sg = _sigmoid(gret)
        silu = gret * sg
        on, rstds = _group_norm(o_ref[...].astype(F32))
        aret_ref[...] = (silu * on).astype(BF16)
        dgret = da_ret * on * (sg * (1.0 + gret * (1.0 - sg)))
        don = da_ret * silu
        for h in range(HEADS):
            sl = slice(h * DH, (h + 1) * DH)
            onh, donh = on[:, sl], don[:, sl]
            do_ref[:, sl] = (rstds[h] * (donh - onh * jnp.mean(donh * onh, axis=-1, keepdims=True))).astype(BF16)

        gl, gl_grad = _gelu_and_grad(gate(1))
        hl = hl_ref[...].astype(F32)
        alru_ref[...] = (gl * hl).astype(BF16)
        dlgate = da_lru * hl * gl_grad
        dhl_ref[...] = (da_lru * gl).astype(BF16)

        for j, val in enumerate((dgret, dlgate, dga, dgb)):
            dproj_ref[:, j * D:(j + 1) * D] = val.astype(BF16)

    tile = lambda col: pl.BlockSpec((tm, D), lambda i: (i, col))
    gates = pl.BlockSpec((tm, 4 * D), lambda i: (i, GATES_COL))
    bf = lambda: jax.ShapeDtypeStruct((rows, D), BF16)
    return _hosted_call(
        body,
        name="mix_bwd",
        grid=(nt,),
        in_specs=[tile(0), tile(0), gates, tile(0), tile(0), tile(0), _ANY, _ANY, _ANY],
        out_specs=[pl.BlockSpec((tm, 4 * D), lambda i: (i, GATES_COL))] + [tile(0)] * 7,
        out_shape=[jax.ShapeDtypeStruct((rows, IN_COLS), BF16)] + [bf() for _ in range(7)],
        scratch_shapes=[pltpu.VMEM((3, D, D), BF16), pltpu.SemaphoreType.DMA((3,))],
        args=(dh1, o, proj, h_lru, y_ret, y_lru, w_br, w_bl, w_o),
        riders=riders,
    )


def _retention_bwd(dproj, proj_bf, do, sprev, intra, intra_t, q_dec, k_dec, c_dec, cos_t, sin_t, riders=()):
    rows = proj_bf.shape[0]
    nc = rows // CHUNK

    def body(dproj_in_ref, q_ref, k_ref, v_ref, do_ref, sprev_ref, m_ref, mt_ref, qd_ref, kd_ref, cd_ref, cos_ref, sin_ref,
             dproj_ref, ds_sc):
        @pl.when(pl.program_id(0) == 0)
        def _():
            ds_sc[...] = jnp.zeros_like(ds_sc)

        cos, sin = cos_ref[...], sin_ref[...]

        def unrotate(dy):
            return dy * cos - pltpu.roll(dy, DH // 2, axis=1) * sin

        for h in range(HEADS):
            sl = slice(h * DH, (h + 1) * DH)
            q, k, v = q_ref[:, sl], k_ref[:, sl], v_ref[:, sl]
            do = do_ref[:, sl]
            dob = do.astype(BF16)
            doq = (do * qd_ref[:, sl]).astype(BF16)
            state_prev = sprev_ref[0, h]
            dstate = ds_sc[h]
            dstate_b = dstate.astype(BF16)
            s_t = (_dot_nt(k, q) * mt_ref[h]).astype(BF16)
            ds_t = (_dot_nt(v, dob) * mt_ref[h]).astype(BF16)
            ds = (_dot_nt(dob, v) * m_ref[h]).astype(BF16)
            kd = kd_ref[:, sl]
            dq = _dot(ds, k) + _dot_nt(doq, state_prev)
            dk = _dot(ds_t, q) + _dot_nt(v, dstate_b) * kd
            k_scaled = (k.astype(F32) * kd).astype(BF16)
            dv = _dot(s_t, dob) + _dot(k_scaled, dstate_b)
            ds_sc[h] = dstate * cd_ref[:, sl] + _dot_tn(q, doq)
            for part, val in enumerate((unrotate(dq), unrotate(dk) * QK_SCALE, dv)):
                dproj_ref[:, part * D + h * DH:part * D + (h + 1) * DH] = val.astype(BF16)

    rev = lambda c: nc - 1 - c
    chunk_spec = lambda col: pl.BlockSpec((CHUNK, D), lambda c: (rev(c), col))
    const2 = lambda shape: pl.BlockSpec(shape, lambda c: (0, 0))
    const3 = pl.BlockSpec((HEADS, CHUNK, CHUNK), lambda c: (0, 0, 0))
    return _hosted_call(
        body,
        name="retention_bwd",
        grid=(nc,),
        in_specs=[
            pl.BlockSpec(memory_space=pl.ANY),
            chunk_spec(0), chunk_spec(1), chunk_spec(2), chunk_spec(0),
            pl.BlockSpec((1, HEADS, DH, DH), lambda c: (rev(c), 0, 0, 0)),
            const3, const3,
            const2((CHUNK, D)), const2((CHUNK, D)), const2((1, D)),
            pl.BlockSpec((CHUNK, DH), lambda c: (rev(c), 0)),
            pl.BlockSpec((CHUNK, DH), lambda c: (rev(c), 0)),
        ],
        out_specs=[pl.BlockSpec((CHUNK, 3 * D), lambda c: (rev(c), 0))],
        out_shape=[jax.ShapeDtypeStruct(dproj.shape, BF16)],
        aliases={0: 0},
        scratch_shapes=[pltpu.VMEM((HEADS, DH, DH), F32)],
        args=(dproj, proj_bf, proj_bf, proj_bf, do, sprev, intra, intra_t, q_dec, k_dec, c_dec, cos_t, sin_t),
        riders=riders,
    )


def _lru_bwd(dproj, proj, saved, dhl, conv_w, wa, wx, lam, riders=()):
    rows = proj.shape[0]
    TM = _lru_tile(rows)
    nt = rows // TM
    per8 = TM // 8

    def body(dproj_in_ref, x_ref, xprev_ref, h_ref, hprev_ref, c_ref, r_ref, i_ref, la_ref, mult_ref, dhl_ref,
             cw_ref, wa_ref, wx_ref, lam_ref,
             dproj_ref, dwa_ref, dwx_ref, stats_ref, anext_sc, dhnext_sc, dcnext_sc, c_sc, b_sc, dh_sc):
        step = pl.program_id(0)
        i = nt - 1 - step

        @pl.when(step == 0)
        def _():
            anext_sc[...] = jnp.zeros_like(anext_sc)
            dhnext_sc[...] = jnp.zeros_like(dhnext_sc)
            dcnext_sc[...] = jnp.zeros_like(dcnext_sc)
            dwa_ref[...] = jnp.zeros_like(dwa_ref)
            dwx_ref[...] = jnp.zeros_like(dwx_ref)
            stats_ref[...] = jnp.zeros_like(stats_ref)

        first = i == 0
        x = x_ref[...].astype(F32)
        prev8 = jnp.where(first, 0.0, xprev_ref[8:16, :].astype(F32))
        c_b = c_ref[...]
        c, r, gate_i, mult = (ref[...].astype(F32) for ref in (c_ref, r_ref, i_ref, mult_ref))
        a = jnp.exp(la_ref[...].astype(F32))
        sp = _softplus(-lam_ref[...])
        dh_sc[...] = dhl_ref[...].astype(F32)

        b_sc[...] = pltpu.roll(a, TM - 1, axis=0)
        b_sc[TM - 1:TM, :] = anext_sc[0:1, :]
        anext_sc[0:1, :] = a[0:1, :]
        row8 = lax.broadcasted_iota(jnp.int32, (8, D), 0)

        def group(gi, dhnext):
            r0 = pl.multiple_of((per8 - 1 - gi) * 8, 8)
            bb = b_sc[pl.ds(r0, 8), :]
            uu = dh_sc[pl.ds(r0, 8), :]
            for d in (1, 2, 4):
                b_sh = jnp.where(row8 < 8 - d, pltpu.roll(bb, 8 - d, axis=0), 1.0)
                u_sh = jnp.where(row8 < 8 - d, pltpu.roll(uu, 8 - d, axis=0), 0.0)
                uu = uu + bb * u_sh
                bb = bb * b_sh
            dhb = bb * dhnext + uu
            dh_sc[pl.ds(r0, 8), :] = dhb
            return dhb[0:1, :]

        dhfirst = lax.fori_loop(0, per8, group, dhnext_sc[0:1, :])
        dhnext_sc[0:1, :] = dhfirst
        dh = dh_sc[...]

        h = h_ref[...].astype(F32)
        hprev8 = jnp.where(first, 0.0, hprev_ref[8:16, :].astype(F32))
        h_dn, h_head = _shift_down(h, hprev8, 1)
        c_sc[...] = h_dn
        c_sc[0:8, :] = h_head
        h_before = c_sc[...]

        valid = _row_ids(i, TM, (TM, D)) >= PAD_ROWS
        da = dh * h_before
        du = jnp.where(valid, dh, 0.0)
        dmult = du * gate_i * c
        dgate_i = du * mult * c
        dc = du * mult * gate_i
        dla = da * a - dmult * (a * a) / mult
        dla = jnp.where(valid, dla, 0.0)
        dr = dla * ((-LRU_C) * sp)
        dzr = dr * r * (1.0 - r)
        dzi = dgate_i * gate_i * (1.0 - gate_i)
        stats_ref[1:2, :] += jnp.sum(dzr, axis=0, keepdims=True)
        stats_ref[2:3, :] += jnp.sum(dzi, axis=0, keepdims=True)
        stats_ref[3:4, :] += jnp.sum(dla * ((-LRU_C) * r), axis=0, keepdims=True)
        dc_gate = []
        for g in range(LRU_BLOCKS):
            sl = slice(g * LRU_BLOCK, (g + 1) * LRU_BLOCK)
            cg = c_b[:, sl]
            dzr_g = dzr[:, sl].astype(BF16)
            dzi_g = dzi[:, sl].astype(BF16)
            dc_gate.append(_dot_nt(dzr_g, wa_ref[g]) + _dot_nt(dzi_g, wx_ref[g]))
            dwa_ref[g] += _dot_tn(cg, dzr_g)
            dwx_ref[g] += _dot_tn(cg, dzi_g)
        dc = dc + jnp.concatenate(dc_gate, axis=1)

        cw = cw_ref[...]
        stats_ref[0:1, :] += jnp.sum(dc, axis=0, keepdims=True)
        stats_ref[7:8, :] += jnp.sum(dc * x, axis=0, keepdims=True)
        dx = cw[3:4] * dc
        tail_src = jnp.concatenate([dc[TM - 8:TM], dcnext_sc[...]], axis=0)
        dx_tail = cw[3:4] * dc[TM - 8:TM]
        for d in (1, 2, 3):
            dx = dx + cw[3 - d:4 - d] * pltpu.roll(dc, TM - d, axis=0)
            dx_tail = dx_tail + cw[3 - d:4 - d] * pltpu.roll(tail_src, 16 - d, axis=0)[0:8]
            rolled, hd = _shift_down(x, prev8, d)
            b_sc[...] = rolled
            b_sc[0:8, :] = hd
            stats_ref[7 - d:8 - d, :] += jnp.sum(dc * b_sc[...], axis=0, keepdims=True)
        dcnext_sc[...] = dc[0:8]
        dproj_ref[...] = dx.astype(BF16)
        dproj_ref[TM - 8:TM, :] = dx_tail.astype(BF16)

    rev = lambda s: nt - 1 - s
    vec = pl.BlockSpec((1, D), lambda s: (0, 0))
    wspec = pl.BlockSpec((LRU_BLOCKS, LRU_BLOCK, LRU_BLOCK), lambda s: (0, 0, 0))
    prev16 = lambda col: pl.BlockSpec((BF16_ROWS, D), lambda s: (jnp.maximum(rev(s) * (TM // BF16_ROWS) - 1, 0), col))
    tile = lambda: pl.BlockSpec((TM, D), lambda s: (rev(s), 0))
    h_lru, c_sv, r_sv, i_sv, la_sv, mult_sv = saved
    return _hosted_call(
        body,
        name="lru_bwd",
        grid=(nt,),
        in_specs=[
            pl.BlockSpec(memory_space=pl.ANY),
            pl.BlockSpec((TM, D), lambda s: (rev(s), LRU_IN_COL)), prev16(LRU_IN_COL),
            tile(), prev16(0),
            tile(), tile(), tile(), tile(), tile(), tile(),
            pl.BlockSpec((4, D), lambda s: (0, 0)),
            wspec, wspec, vec,
        ],
        out_specs=[
            pl.BlockSpec((TM, D), lambda s: (rev(s), LRU_IN_COL)),
            wspec, wspec,
            pl.BlockSpec((8, D), lambda s: (0, 0)),
        ],
        out_shape=[
            jax.ShapeDtypeStruct(dproj.shape, BF16),
            jax.ShapeDtypeStruct((LRU_BLOCKS, LRU_BLOCK, LRU_BLOCK), F32),
            jax.ShapeDtypeStruct((LRU_BLOCKS, LRU_BLOCK, LRU_BLOCK), F32),
            jax.ShapeDtypeStruct((8, D), F32),
        ],
        aliases={0: 0},
        scratch_shapes=[
            pltpu.VMEM((8, D), F32), pltpu.VMEM((8, D), F32), pltpu.VMEM((8, D), F32),
            pltpu.VMEM((TM, D), F32), pltpu.VMEM((TM, D), F32), pltpu.VMEM((TM, D), F32),
        ],
        args=(dproj, proj, proj, h_lru, h_lru, c_sv, r_sv, i_sv, la_sv, mult_sv, dhl, conv_w, wa, wx, lam),
        riders=riders,
    )


def _in_proj_bwd(dproj, w_in, part, prev=None, riders=()):
    rows = dproj.shape[0]
    tm = _heavy_tile(rows)
    nt = rows // tm
    split = max(1, nt // 4 + 1) if nt > 1 else 1
    first = 0 if part == 0 else split
    count = split if part == 0 else nt - split

    def body(*refs):
        dproj_ref, w_hbm, du_ref, w_sc, w_sem = refs[-5:]

        @pl.when(pl.program_id(0) == 0)
        def _():
            moves = [((0, 3), 0), ((4, 1), 3), ((3, 1), 4), ((5, 3), 5)]
            copies = [pltpu.make_async_copy(w_hbm.at[:, pl.ds(src * D, n * D)], w_sc.at[:, pl.ds(dst * D, n * D)], w_sem.at[q])
                      for q, ((src, n), dst) in enumerate(moves)]
            for cp in copies:
                cp.start()
            for cp in copies:
                cp.wait()

        du_ref[...] = _dot_nt(dproj_ref[...], w_sc[...])

    in_specs = [pl.BlockSpec((tm, IN_COLS), lambda i: (first + i, 0)), _ANY]
    args = (dproj, w_in)
    if prev is not None:
        in_specs = [_ANY] + in_specs
        args = (prev,) + args
    return _hosted_call(
        body,
        name="in_proj_bwd_%d" % part,
        grid=(count,),
        in_specs=in_specs,
        out_specs=[pl.BlockSpec((tm, D), lambda i: (first + i, 0))],
        out_shape=[jax.ShapeDtypeStruct((rows, D), F32)],
        scratch_shapes=[pltpu.VMEM((D, IN_COLS), BF16), pltpu.SemaphoreType.DMA((4,))],
        aliases={0: 0} if prev is not None else None,
        args=args,
        riders=riders,
    )


def _norm1_bwd(du, dh1, head, x2d, rstd1, norm_w, riders=()):
    rows = du.shape[0]

    def body(du_ref, dh1_ref, head_ref, x_ref, rstd_ref, nw_ref, gx_ref, ghead_ref, stats_ref):
        i = pl.program_id(0)

        @pl.when(i == 0)
        def _():
            stats_ref[...] = jnp.zeros_like(stats_ref)

        def finish(h0, out_ref):
            du = du_ref[...]
            rs = rstd_ref[...]
            n = h0 * rs
            stats_ref[0:1, :] += jnp.sum(du * n, axis=0, keepdims=True)
            dn = du * nw_ref[...]
            out_ref[...] = dh1_ref[...] + rs * (dn - n * jnp.mean(dn * n, axis=-1, keepdims=True))

        @pl.when(i == 0)
        def _():
            finish(head_ref[...], ghead_ref)

        @pl.when(i > 0)
        def _():
            finish(x_ref[...], gx_ref)

    tile = pl.BlockSpec((TM, D), lambda i: (i, 0))
    return _hosted_call(
        body,
        name="norm1_bwd",
        grid=(rows // TM,),
        in_specs=[
            tile, tile,
            pl.BlockSpec((FRONT, D), lambda i: (0, 0)),
            pl.BlockSpec((TM, D), lambda i: (jnp.maximum(i - 1, 0), 0)),
            pl.BlockSpec((TM, 1), lambda i: (i, 0)),
            pl.BlockSpec((1, D), lambda i: (0, 0)),
        ],
        out_specs=[
            pl.BlockSpec((TM, D), lambda i: (jnp.maximum(i - 1, 0), 0)),
            pl.BlockSpec((FRONT, D), lambda i: (0, 0)),
            pl.BlockSpec((8, D), lambda i: (0, 0)),
        ],
        out_shape=[
            jax.ShapeDtypeStruct(x2d.shape, F32),
            jax.ShapeDtypeStruct((FRONT, D), F32),
            jax.ShapeDtypeStruct((8, D), F32),
        ],
        scratch_shapes=[],
        args=(du, dh1, head, x2d, rstd1, norm_w),
        riders=riders,
    )


def _matmul_tn(name, x, dy, out_cols, col0=0, prev=None, k_block=None, n_block=None, riders=(), col_map=None):
    col_map = col_map or (lambda n: n)
    rows, kdim = x.shape
    ndim = dy.shape[1]
    kb = k_block or kdim
    nb = n_block or ndim
    step_bytes = lambda t: 2 * t * (kb * x.dtype.itemsize + nb * dy.dtype.itemsize) + 2 * kb * nb * 4
    tr = next(t for t in (2816, 1408, TM_HEAVY, TM) if rows % t == 0 and (t == TM or step_bytes(t) <= TN_VMEM_BUDGET))
    nr, nk, nn = rows // tr, kdim // kb, ndim // nb
    cb0 = col0 // nb

    def body(*refs):
        x_ref, dy_ref, out_ref = refs[-3], refs[-2], refs[-1]
        part = _dot_tn(x_ref[...].astype(BF16), dy_ref[...].astype(BF16))

        @pl.when(pl.program_id(2) == 0)
        def _():
            out_ref[...] = part

        @pl.when(pl.program_id(2) > 0)
        def _():
            out_ref[...] += part

    in_specs = [
        pl.BlockSpec((tr, kb), lambda n, k, r: (r, k)),
        pl.BlockSpec((tr, nb), lambda n, k, r: (r, n)),
    ]
    args = [x, dy]
    aliases = {}
    if prev is not None:
        in_specs = [pl.BlockSpec(memory_space=pl.ANY)] + in_specs
        args = [prev] + args
        aliases = {0: 0}
    (out,), rider_outs = _hosted_call(
        body,
        name=name,
        grid=(nn, nk, nr),
        in_specs=in_specs,
        out_specs=[pl.BlockSpec((kb, nb), lambda n, k, r: (k, cb0 + col_map(n)))],
        out_shape=[jax.ShapeDtypeStruct((kdim, out_cols), F32)],
        scratch_shapes=[],
        aliases=aliases,
        args=args,
        riders=riders,
    )
    return (out, rider_outs) if riders else out


def _local_step(x2d, target, w, plan):
    rows = FRONT + x2d.shape[0]
    head = jnp.concatenate([jnp.zeros((PAD_ROWS, D), F32), w["meta_tokens"]], axis=0)
    cos_t, sin_t = _rope_tables(rows)
    intra, intra_t, q_dec, k_dec, c_dec = _decay_tables()
    grads = {}

    def hosted(host, fn, *args, **kwargs):
        outs, rider_outs = fn(*args, riders=plan.riders(host, w, grads), **kwargs)
        plan.after(host, rider_outs, w, grads)
        return outs

    (u1, rstd1), _ = _norm1(head, x2d, w["mix_norm_w"])
    proj, w["w_in"] = hosted("in_proj", _in_proj, u1, w["w_in_shard"], w["route"], cos_t, sin_t)
    o, sprev = hosted("retention_fwd", _retention_fwd, proj, intra, q_dec, k_dec, c_dec)
    lru_args = (w["conv_w"], w["conv_b"], w["lru_wa"], w["lru_wx"], w["lru_ba"], w["lru_bx"], w["lru_lambda"])
    lru_saved = hosted("lru_fwd", _lru_fwd, proj, *lru_args)
    h_lru = lru_saved[0]
    y_ret, y_lru, h1, u2, rstd2 = hosted("mix_fwd", _mix_fwd, head, x2d, o, proj, h_lru, w["w_branch_ret"],
                                         w["w_branch_lru"], w["w_out"], w["ffn_norm_w"])
    silu, dsilu, act = hosted("ffn_fwd_gate", _ffn_fwd_gate, u2, w["w_ffn_in"])
    dh2, stats_loss = _ffn_out_loss(act, h1, w["w_ffn_out"], target, w["final_norm_w"])

    dg, dup, dh2b, dh1, stats_ffn = _ffn_bwd(dh2, silu, dsilu, h1, rstd2, w["w_ffn_in"], w["w_ffn_out"], w["ffn_norm_w"])
    grads["w_ffn_in"] = _matmul_tn("dw_ffn_up", u2, dup, 2 * FFN, col0=FFN, n_block=FFN_HALF,
                                   prev=_matmul_tn("dw_ffn_gate", u2, dg, 2 * FFN, n_block=FFN_HALF))
    grads["w_ffn_out"] = _matmul_tn("dw_ffn_out", act, dh2b, D, k_block=FFN_HALF)
    (dproj, do, dhl, mixed, a_ret, a_lru, dy_ret, dy_lru) = hosted(
        "mix_bwd", _mix_bwd, dh1, o, proj, h_lru, y_ret, y_lru, w["w_branch_ret"], w["w_branch_lru"], w["w_out"])
    grads["w_out"] = _matmul_tn("dw_out", mixed, dh1, D)
    grads["w_branch_ret"] = _matmul_tn("dw_branch_ret", a_ret, dy_ret, D)
    grads["w_branch_lru"] = _matmul_tn("dw_branch_lru", a_lru, dy_lru, D)
    (dproj,) = hosted("retention_bwd", _retention_bwd, dproj, proj, do, sprev, intra, intra_t, q_dec, k_dec, c_dec,
                      cos_t, sin_t)
    dproj, grads["lru_wa"], grads["lru_wx"], stats_lru = hosted(
        "lru_bwd", _lru_bwd, dproj, proj, lru_saved, dhl, w["conv_w"], w["lru_wa"], w["lru_wx"], w["lru_lambda"])
    grads["w_in"], rider_outs = _matmul_tn("dw_in", u1, dproj, IN_COLS, n_block=D, col_map=_swap_3_4,
                                           riders=plan.riders("dw_in", w, grads))
    plan.after("dw_in", rider_outs, w, grads)
    (du1,) = hosted("in_proj_bwd_0", _in_proj_bwd, dproj, w["w_in"], 0)
    (du1,) = hosted("in_proj_bwd_1", _in_proj_bwd, dproj, w["w_in"], 1, du1)
    (grad_x, grad_head, stats_in), _ = _norm1_bwd(du1, dh1, head, x2d, rstd1, w["mix_norm_w"])
    return grad_x, grad_head, grads, [stats_loss, stats_ffn, stats_in, stats_lru]


BIG_PIECES = {
    "w_in": ("col", (D, 2 * D)),
    "w_ffn_in": ("col", (D, FFN_HALF)),
    "w_ffn_out": ("row", (FFN // 4, D)),
    "w_branch_ret": ("row", (D // 4, D)),
    "w_branch_lru": ("row", (D // 4, D)),
    "w_out": ("row", (D // 4, D)),
    "lru_wa": ("lru", (LRU_BLOCKS, LRU_BLOCK // 4, LRU_BLOCK)),
    "lru_wx": ("lru", (LRU_BLOCKS, LRU_BLOCK // 4, LRU_BLOCK)),
}
SMALL_PIECES = {"meta_tokens": ("col", (N_META, D // 4)), "conv_w": ("col", (4, D // 4))}


def _full_shape(kind, shard):
    if kind == "col":
        return (shard[0], 4 * shard[1])
    if kind == "row":
        return (4 * shard[0], shard[1])
    return (shard[0], 4 * shard[1], shard[2])


def _half_shape(kind, shard):
    return (shard[0] // 2,) + tuple(shard[1:])


def _aligned(start, multiple):
    return start if isinstance(start, int) else pl.multiple_of(start, multiple)


def _lead(h, size):
    if h is None:
        return pl.ds(0, size)
    return pl.ds(_aligned(h * (size // 2), size // 2), size // 2)


def _full_region(ref, kind, shard, s, h):
    if kind == "col":
        return ref.at[_lead(h, shard[0]), pl.ds(_aligned(s * shard[1], shard[1]), shard[1])]
    if kind == "row":
        size = shard[0] if h is None else shard[0] // 2
        start = s * shard[0] + (0 if h is None else h * (shard[0] // 2))
        return ref.at[pl.ds(_aligned(start, BF16_ROWS), size), :]
    return ref.at[_lead(h, shard[0]), pl.ds(_aligned(s * shard[1], shard[1]), shard[1]), :]


def _shard_region(ref, shard, h):
    return ref.at[_lead(h, shard[0])]


def _place():
    x, y, c = lax.axis_index("x"), lax.axis_index("y"), lax.axis_index("c")
    return x, y, c, 2 * x + y


def _other_chip(s, c, k):
    s2 = jnp.bitwise_xor(s, k)
    return s2, (s2 // 2, s2 % 2, c)


def _remote(src, dst, send_sem, recv_sem, dev):
    return pltpu.make_async_remote_copy(src_ref=src, dst_ref=dst, send_sem=send_sem, recv_sem=recv_sem,
                                        device_id=dev, device_id_type=MESH)


_ANY = pl.BlockSpec(memory_space=pl.ANY)


class _Rider:
    def __init__(self, ins, out_shapes, sem_shapes, build, aliased=False):
        self.ins, self.out_shapes, self.sem_shapes, self.build, self.aliased = ins, out_shapes, sem_shapes, build, aliased


def _hosted_call(body, *, name, grid, in_specs, out_specs, out_shape, scratch_shapes, args, riders=(), aliases=None,
                 prefetch=None, riders_after_body=False):
    n_in, n_out, n_sc = len(in_specs), len(out_shape), len(scratch_shapes)
    r_in = [a for r in riders for a in r.ins]
    r_out = [s for r in riders for s in r.out_shapes]
    r_sem = [s for r in riders for s in r.sem_shapes]
    lead = () if prefetch is None else (prefetch,)
    assert prefetch is None or not (aliases or any(r.aliased for r in riders))

    def full_body(*refs):
        head, refs = refs[:len(lead)], refs[len(lead):]
        ins, rin = refs[:n_in], refs[n_in:n_in + len(r_in)]
        o0 = n_in + len(r_in)
        outs, rout = refs[o0:o0 + n_out], refs[o0 + n_out:o0 + n_out + len(r_out)]
        s0 = o0 + n_out + len(r_out)
        scratch, rsem = refs[s0:s0 + n_sc], refs[s0 + n_sc:]
        starts, waits = [], []
        pi = po = ps = 0
        for r in riders:
            st, wt = r.build(rin[pi:pi + len(r.ins)], rout[po:po + len(r.out_shapes)], rsem[ps:ps + len(r.sem_shapes)])
            starts += st
            waits += wt
            pi, po, ps = pi + len(r.ins), po + len(r.out_shapes), ps + len(r.sem_shapes)
        first = functools.reduce(jnp.logical_and, [pl.program_id(d) == 0 for d in range(len(grid))])
        last = functools.reduce(jnp.logical_and, [pl.program_id(d) == grid[d] - 1 for d in range(len(grid))])

        def start_riders():
            @pl.when(first)
            def _():
                for cp in starts:
                    cp.start()

        if riders and not riders_after_body:
            start_riders()
        body(*head, *ins, *outs, *scratch)
        if riders and riders_after_body:
            start_riders()
        if riders:
            @pl.when(last)
            def _():
                for wait in waits:
                    wait()

    io_aliases = dict(aliases or {})
    pi = po = 0
    for r in riders:
        if r.aliased:
            for q in range(len(r.ins)):
                io_aliases[n_in + pi + q] = n_out + po + q
        pi, po = pi + len(r.ins), po + len(r.out_shapes)
    specs = dict(
        grid=grid,
        in_specs=list(in_specs) + [_ANY] * len(r_in),
        out_specs=list(out_specs) + [_ANY] * len(r_out),
        scratch_shapes=list(scratch_shapes) + r_sem,
    )
    if prefetch is not None:
        specs = dict(grid_spec=pltpu.PrefetchScalarGridSpec(num_scalar_prefetch=1, **specs))
    res = pl.pallas_call(
        full_body,
        name=name,
        out_shape=list(out_shape) + r_out,
        input_output_aliases=io_aliases,
        compiler_params=_params(("arbitrary",) * len(grid)),
        **specs,
    )(*lead, *args, *r_in)
    rider_outs, po = [], n_out
    for r in riders:
        rider_outs.append(list(res[po:po + len(r.out_shapes)]))
        po += len(r.out_shapes)
    return list(res[:n_out]), rider_outs


def _run_riders(name, riders):
    r_in = [a for r in riders for a in r.ins]
    r_out = [s for r in riders for s in r.out_shapes]
    r_sem = [s for r in riders for s in r.sem_shapes]

    def body(*refs):
        rin, rout, rsem = refs[:len(r_in)], refs[len(r_in):len(r_in) + len(r_out)], refs[len(r_in) + len(r_out):]
        pi = po = ps = 0
        all_waits = []
        for r in riders:
            starts, waits = r.build(rin[pi:pi + len(r.ins)], rout[po:po + len(r.out_shapes)], rsem[ps:ps + len(r.sem_shapes)])
            for cp in starts:
                cp.start()
            all_waits += waits
            pi, po, ps = pi + len(r.ins), po + len(r.out_shapes), ps + len(r.sem_shapes)
        for wait in all_waits:
            wait()

    io_aliases = {}
    pi = po = 0
    for r in riders:
        if r.aliased:
            for q in range(len(r.ins)):
                io_aliases[pi + q] = po + q
        pi, po = pi + len(r.ins), po + len(r.out_shapes)
    res = pl.pallas_call(
        body,
        name=name,
        in_specs=[_ANY] * len(r_in),
        out_specs=[_ANY] * len(r_out),
        out_shape=r_out,
        scratch_shapes=r_sem,
        input_output_aliases=io_aliases,
    )(*r_in)
    outs, po = [], 0
    for r in riders:
        outs.append(list(res[po:po + len(r.out_shapes)]))
        po += len(r.out_shapes)
    return outs


def _piece(name):
    if name in BIG_PIECES:
        return (name, *BIG_PIECES[name], True)
    return (name, *SMALL_PIECES[name], False)


def _gather_rider(shards, names):
    pieces = [_piece(n) for n in names]
    n = len(pieces)

    def build(ins, outs, sems):
        local_sem, ici_send, ici_recv = sems
        _, _, c, s = _place()
        starts, waits = [], []
        for p, (_, kind, shard, split) in enumerate(pieces):
            cp = pltpu.make_async_copy(ins[p], _full_region(outs[p], kind, shard, s, None), local_sem.at[p])
            starts.append(cp)
            waits.append(cp.wait)
            h = c if split else None
            for k in (1, 2, 3):
                s2, dev = _other_chip(s, c, k)
                cp = _remote(_shard_region(ins[p], shard, h), _full_region(outs[p], kind, shard, s, h),
                             ici_send.at[p, k - 1], ici_recv.at[p, k - 1], dev)
                starts.append(cp)
                waits.append(cp.wait_send)
                region = _full_region(outs[p], kind, shard, s2, h)
                waits.append(_remote(region, region, ici_send.at[p, k - 1], ici_recv.at[p, k - 1], dev).wait_recv)
        return starts, waits

    return _Rider(
        [shards[name] for name in names],
        [jax.ShapeDtypeStruct(_full_shape(kind, shard), shards[name].dtype) for name, kind, shard, _ in pieces],
        [pltpu.SemaphoreType.DMA((n,)), pltpu.SemaphoreType.DMA((n, 3)), pltpu.SemaphoreType.DMA((n, 3))],
        build)


def _forward_rider(gathered, names):
    pieces = [_piece(n) for n in names]
    n = len(pieces)

    def build(ins, outs, sems):
        fwd_send, fwd_recv = sems
        x, y, c, s = _place()
        sibling = (x, y, 1 - c)
        starts, waits = [], []
        for p, (_, kind, shard, _) in enumerate(pieces):
            for k in (1, 2, 3):
                s2, _ = _other_chip(s, c, k)
                mine = _full_region(outs[p], kind, shard, s2, c)
                theirs = _full_region(outs[p], kind, shard, s2, 1 - c)
                cp = _remote(mine, mine, fwd_send.at[p, k - 1], fwd_recv.at[p, k - 1], sibling)
                starts.append(cp)
                waits.append(cp.wait_send)
                waits.append(_remote(theirs, theirs, fwd_send.at[p, k - 1], fwd_recv.at[p, k - 1], sibling).wait_recv)
        return starts, waits

    return _Rider(
        [gathered[name] for name in names],
        [jax.ShapeDtypeStruct(gathered[name].shape, gathered[name].dtype) for name in names],
        [pltpu.SemaphoreType.DMA((n, 3)), pltpu.SemaphoreType.DMA((n, 3))],
        build, aliased=True)


def _pair_exchange_rider(grads, names):
    n = len(names)

    def build(ins, outs, sems):
        send_sem, recv_sem = sems
        x, y, c, _ = _place()
        sibling = (x, y, 1 - c)
        starts, waits = [], []
        for p, name in enumerate(names):
            kind, shard = BIG_PIECES[name]
            for s2 in range(4):
                cp = _remote(_full_region(ins[p], kind, shard, s2, 1 - c), outs[p].at[s2], send_sem.at[p, s2],
                             recv_sem.at[p, s2], sibling)
                starts.append(cp)
                waits.append(cp.wait_send)
                waits.append(_remote(outs[p].at[s2], outs[p].at[s2], send_sem.at[p, s2], recv_sem.at[p, s2], sibling).wait_recv)
        return starts, waits

    return _Rider(
        [grads[name] for name in names],
        [jax.ShapeDtypeStruct((4,) + _half_shape(*BIG_PIECES[name]), F32) for name in names],
        [pltpu.SemaphoreType.DMA((n, 4))] * 2,
        build)


def _half_specs(kind, shard):
    half = shard[0] // 2
    if kind == "col":
        full = pl.BlockSpec((half, shard[1]), lambda j, pr: (pr[1], j))
        buf = pl.BlockSpec((None, half, shard[1]), lambda j, pr: (j, 0, 0))
    elif kind == "row":
        full = pl.BlockSpec((half, shard[1]), lambda j, pr: (2 * j + pr[1], 0))
        buf = pl.BlockSpec((None, half, shard[1]), lambda j, pr: (j, 0, 0))
    else:
        full = pl.BlockSpec((half, shard[1], shard[2]), lambda j, pr: (pr[1], j, 0))
        buf = pl.BlockSpec((None, half, shard[1], shard[2]), lambda j, pr: (j, 0, 0, 0))
    return full, buf


def _pair_sum(name, grad, recv, place):
    kind, shard = BIG_PIECES[name]
    full, buf = _half_specs(kind, shard)

    def body(pr, g_ref, r_ref, o_ref):
        o_ref[...] = (g_ref[...] + r_ref[...]).astype(BF16)

    return pl.pallas_call(
        body,
        name="pair_sum_" + name,
        grid_spec=pltpu.PrefetchScalarGridSpec(num_scalar_prefetch=1, grid=(4,), in_specs=[full, buf], out_specs=buf),
        out_shape=jax.ShapeDtypeStruct((4,) + _half_shape(kind, shard), BF16),
        compiler_params=_params(("arbitrary",)),
    )(place, grad, recv)


def _chip_exchange_rider(sums, names):
    n = len(names)

    def build(ins, outs, sems):
        send_sem, recv_sem = sems
        _, _, c, s = _place()
        starts, waits = [], []
        for p in range(n):
            for k in (1, 2, 3):
                s2, dev = _other_chip(s, c, k)
                cp = _remote(ins[p].at[s2], outs[p].at[k - 1], send_sem.at[p, k - 1], recv_sem.at[p, k - 1], dev)
                starts.append(cp)
                waits.append(cp.wait_send)
                waits.append(_remote(outs[p].at[k - 1], outs[p].at[k - 1], send_sem.at[p, k - 1], recv_sem.at[p, k - 1],
                                     dev).wait_recv)
        return starts, waits

    return _Rider(
        [sums[name] for name in names],
        [jax.ShapeDtypeStruct((3,) + _half_shape(*BIG_PIECES[name]), BF16) for name in names],
        [pltpu.SemaphoreType.DMA((n, 3))] * 2,
        build)


def _chip_sum(name, grad, recv_pair, recv_chip, place):
    kind, shard = BIG_PIECES[name]
    half = shard[0] // 2
    tail = tuple(shard[1:])
    zeros = (0,) * len(tail)
    nt = 4 if kind != "lru" and half % (4 * BF16_ROWS) == 0 else 1
    rows = half // nt
    if kind == "col":
        full = pl.BlockSpec((rows,) + tail, lambda j, pr: (pr[1] * nt + j, pr[0]))
    elif kind == "row":
        full = pl.BlockSpec((rows,) + tail, lambda j, pr: ((2 * pr[0] + pr[1]) * nt + j, 0))
    else:
        full = pl.BlockSpec((rows,) + tail, lambda j, pr: (pr[1], pr[0], 0))
    pair = pl.BlockSpec((None, rows) + tail, lambda j, pr: (pr[0], j) + zeros)
    chip = pl.BlockSpec((3, rows) + tail, lambda j, pr: (0, j) + zeros)
    out = pl.BlockSpec((rows,) + tail, lambda j, pr: (pr[1] * nt + j,) + zeros)

    def body(pr, g_ref, rp_ref, rc_ref, o_ref):
        total = g_ref[...] + rp_ref[...]
        for k in range(3):
            total = total + rc_ref[k].astype(F32)
        o_ref[...] = total

    return pl.pallas_call(
        body,
        name="chip_sum_" + name,
        grid_spec=pltpu.PrefetchScalarGridSpec(num_scalar_prefetch=1, grid=(nt,), in_specs=[full, pair, chip], out_specs=out),
        out_shape=jax.ShapeDtypeStruct(shard, F32),
        compiler_params=_params(("arbitrary",)),
    )(place, grad, recv_pair, recv_chip)


def _sibling_exchange_rider(halves, names):
    n = len(names)

    def build(ins, outs, sems):
        send_sem, recv_sem = sems
        x, y, c, _ = _place()
        sibling = (x, y, 1 - c)
        starts, waits = [], []
        for p, name in enumerate(names):
            shard = BIG_PIECES[name][1]
            mine = _shard_region(outs[p], shard, c)
            theirs = _shard_region(outs[p], shard, 1 - c)
            cp = _remote(mine, mine, send_sem.at[p], recv_sem.at[p], sibling)
            starts.append(cp)
            waits.append(cp.wait_send)
            waits.append(_remote(theirs, theirs, send_sem.at[p], recv_sem.at[p], sibling).wait_recv)
        return starts, waits

    return _Rider(
        [halves[name] for name in names],
        [jax.ShapeDtypeStruct(BIG_PIECES[name][1], F32) for name in names],
        [pltpu.SemaphoreType.DMA((n,))] * 2,
        build, aliased=True)


FIRST_WEIGHTS = ["meta_tokens", "conv_w"]
WEIGHT_GROUPS = {
    "lru": ["lru_wa", "lru_wx"],
    "branch": ["w_branch_ret", "w_branch_lru", "w_out"],
    "ffn_in": ["w_ffn_in"],
    "ffn_out": ["w_ffn_out"],
}
WEIGHT_SCHEDULE = {
    "in_proj": [("gather", "lru")],
    "retention_fwd": [("forward", "lru"), ("gather", "branch")],
    "lru_fwd": [("forward", "branch"), ("gather", "ffn_in")],
    "mix_fwd": [("forward", "ffn_in"), ("gather", "ffn_out")],
    "ffn_fwd_gate": [("forward", "ffn_out")],
}
GRAD_GROUPS = {
    "ffn_in": ["w_ffn_in"],
    "ffn_out": ["w_ffn_out"],
    "mixer": ["w_out", "w_branch_ret", "w_branch_lru", "lru_wa", "lru_wx"],
    "in": ["w_in"],
}
GRAD_SCHEDULE = {
    "mix_bwd": [("pair", "ffn_in"), ("pair", "ffn_out")],
    "retention_bwd": [("chip", "ffn_in")],
    "lru_bwd": [("chip", "ffn_out"), ("sibling", "ffn_in")],
    "dw_in": [("sibling", "ffn_out"), ("pair", "mixer")],
    "in_proj_bwd_0": [("chip", "mixer"), ("pair", "in")],
    "in_proj_bwd_1": [("sibling", "mixer"), ("chip", "in")],
}


class _CommPlan:
    def __init__(self, shards, place):
        self.shards, self.place = shards, place
        self.late = {}
        self.recv_pair, self.sums, self.recv_chip, self.halves, self.final = {}, {}, {}, {}, {}

    def _grad_rider(self, stage, group, grads):
        names = GRAD_GROUPS[group]
        if stage == "pair":
            return _pair_exchange_rider(grads, names)
        if stage == "chip":
            return _chip_exchange_rider(self.sums, names)
        return _sibling_exchange_rider(self.halves, names)

    def _grad_after(self, stage, group, outs, grads):
        names = GRAD_GROUPS[group]
        if stage == "pair":
            for n, o in zip(names, outs):
                self.recv_pair[n] = o
                self.sums[n] = _pair_sum(n, grads[n], o, self.place)
        elif stage == "chip":
            for n, o in zip(names, outs):
                self.halves[n] = _chip_sum(n, grads[n], self.recv_pair[n], o, self.place)
        else:
            self.final.update(zip(names, outs))

    def riders(self, host, w, grads):
        if host in WEIGHT_SCHEDULE:
            return [_gather_rider(self.shards, WEIGHT_GROUPS[group]) if stage == "gather"
                    else _forward_rider(self.late, WEIGHT_GROUPS[group]) for stage, group in WEIGHT_SCHEDULE[host]]
        return [self._grad_rider(stage, group, grads) for stage, group in GRAD_SCHEDULE.get(host, [])]

    def after(self, host, rider_outs, w, grads):
        for (stage, group), outs in zip(WEIGHT_SCHEDULE.get(host, []), rider_outs):
            (self.late if stage == "gather" else w).update(zip(WEIGHT_GROUPS[group], outs))
        for (stage, group), outs in zip(GRAD_SCHEDULE.get(host, []), rider_outs):
            self._grad_after(stage, group, outs, grads)

    def finish(self, partial):
        outs, (blocks,) = _run_riders("tail_exchange", [_sibling_exchange_rider(self.halves, GRAD_GROUPS["in"]),
                                                        _small_exchange_rider(partial)])
        self.final.update(zip(GRAD_GROUPS["in"], outs))
        return self.final, blocks


def _adamw_math(w, g, m, v):
    m = ADAM_B1 * m + (1.0 - ADAM_B1) * g
    v = ADAM_B2 * v + (1.0 - ADAM_B2) * (g * g)
    m_hat = m / (1.0 - ADAM_B1 ** ADAM_STEP)
    v_hat = v / (1.0 - ADAM_B2 ** ADAM_STEP)
    delta = -ADAM_LR * (m_hat / (jnp.sqrt(v_hat) + ADAM_EPS) + ADAM_WD * w)
    return delta, m, v


def _adamw(name, g, w, m, v):
    rows, cols = g.shape
    tr = rows // 4 if rows % 32 == 0 else rows

    def body(g_ref, w_ref, m_ref, v_ref, go_ref, d_ref, mo_ref, vo_ref):
        gv = g_ref[...]
        delta, m2, v2 = _adamw_math(w_ref[...], gv, m_ref[...], v_ref[...])
        go_ref[...] = gv
        d_ref[...] = delta
        mo_ref[...] = m2
        vo_ref[...] = v2

    spec = pl.BlockSpec((tr, cols), lambda i: (i, 0))
    return pl.pallas_call(
        body,
        name="adamw_" + name,
        grid=(rows // tr,),
        in_specs=[spec] * 4,
        out_specs=[spec] * 4,
        out_shape=[jax.ShapeDtypeStruct((rows, cols), F32)] * 4,
        compiler_params=_params(("arbitrary",)),
    )(g, w, m, v)


SMALL_ROWS = 48
VEC_ROWS = {"final_norm_w": 1, "ffn_norm_w": 8, "mix_norm_w": 16, "conv_b": 24, "lru_ba": 25, "lru_bx": 26, "lru_lambda": 27}
VEC_NAMES = list(VEC_ROWS)
CONV_W_ROW = 28
META_ROW = 32


def _small_exchange_rider(partial):
    def build(ins, outs, sems):
        local_sem, send_sem, recv_sem = sems
        _, _, c, s = _place()
        me = 2 * s + c
        cp = pltpu.make_async_copy(ins[0], outs[0].at[me], local_sem)
        starts, waits = [cp], [cp.wait]
        for k in range(1, 8):
            peer = jnp.bitwise_xor(me, k)
            dev = (peer // 4, (peer // 2) % 2, peer % 2)
            rd = _remote(ins[0], outs[0].at[me], send_sem.at[k - 1], recv_sem.at[k - 1], dev)
            starts.append(rd)
            waits.append(rd.wait_send)
            waits.append(_remote(ins[0], outs[0].at[peer], send_sem.at[k - 1], recv_sem.at[k - 1], dev).wait_recv)
        return starts, waits

    return _Rider([partial], [jax.ShapeDtypeStruct((8, SMALL_ROWS, D), F32)],
                  [pltpu.SemaphoreType.DMA, pltpu.SemaphoreType.DMA((7,)), pltpu.SemaphoreType.DMA((7,))], build)


def _small_update(blocks, place, vecs, conv, meta):
    nvec = len(VEC_NAMES)
    qcols = D // 4

    def in_order(ref):
        total = ref[0]
        for d in range(1, 8):
            total = total + ref[d]
        return total

    def body(pr, blocks_ref, cols_ref, *refs):
        vec_refs = refs[:3 * nvec]
        conv_refs = refs[3 * nvec:3 * nvec + 3]
        meta_refs = refs[3 * nvec + 3:3 * nvec + 6]
        outs = refs[3 * nvec + 6:]
        loss_ref, vec_out, conv_out, meta_out = outs[0], outs[1:5], outs[5:9], outs[9:13]
        tot, col = in_order(blocks_ref), in_order(cols_ref)
        loss_ref[...] = jnp.sum(tot[0:1, :], axis=1, keepdims=True)
        for o in vec_out:
            o[...] = jnp.zeros_like(o)
        for j, name in enumerate(VEC_NAMES):
            w, m, v = (r[...] for r in vec_refs[3 * j:3 * j + 3])
            g = tot[VEC_ROWS[name]:VEC_ROWS[name] + 1, :]
            if name == "lru_lambda":
                g = -g / (1.0 + jnp.exp(w))
            for o, val in zip(vec_out, (g,) + _adamw_math(w, g, m, v)):
                o[j:j + 1, :] = val
        for row, n_rows, ins, group in ((CONV_W_ROW, 4, conv_refs, conv_out), (META_ROW, N_META, meta_refs, meta_out)):
            g = col[row:row + n_rows, :]
            w, m, v = (r[...] for r in ins)
            for o, val in zip(group, (g,) + _adamw_math(w, g, m, v)):
                o[...] = val

    whole = lambda shape: pl.BlockSpec(shape, lambda i, pr: (0,) * len(shape))
    in_specs = [whole((8, SMALL_ROWS, D)), pl.BlockSpec((8, SMALL_ROWS, qcols), lambda i, pr: (0, 0, pr[0]))]
    in_specs += [whole((1, D))] * (3 * nvec) + [whole((4, qcols))] * 3 + [whole((N_META, qcols))] * 3
    out_shapes = [(1, 1)] + [(8, D)] * 4 + [(4, qcols)] * 4 + [(N_META, qcols)] * 4
    return pl.pallas_call(
        body,
        name="small_update",
        grid_spec=pltpu.PrefetchScalarGridSpec(num_scalar_prefetch=1, grid=(1,), in_specs=in_specs,
                                               out_specs=[whole(s) for s in out_shapes]),
        out_shape=[jax.ShapeDtypeStruct(s, F32) for s in out_shapes],
        compiler_params=_params(("arbitrary",)),
    )(place, blocks, blocks, *[a for t in vecs for a in t], *conv, *meta)


WEIGHT_ORDER = ["meta_tokens", "mix_norm_w", "w_in", "conv_w", "conv_b", "lru_wa", "lru_ba", "lru_wx", "lru_bx", "lru_lambda",
                "w_branch_ret", "w_branch_lru", "w_out", "ffn_norm_w", "w_ffn_in", "w_ffn_out", "final_norm_w"]


def kernel(x, meta_tokens, mix_norm_w, w_in, conv_w, conv_b, lru_wa, lru_ba, lru_wx, lru_bx, lru_lambda, w_branch_ret, w_branch_lru, w_out, ffn_norm_w, w_ffn_in, w_ffn_out, final_norm_w, loss_target, m_meta_tokens, m_mix_norm_w, m_w_in, m_conv_w, m_conv_b, m_lru_wa, m_lru_ba, m_lru_wx, m_lru_bx, m_lru_lambda, m_w_branch_ret, m_w_branch_lru, m_w_out, m_ffn_norm_w, m_w_ffn_in, m_w_ffn_out, m_final_norm_w, v_meta_tokens, v_mix_norm_w, v_w_in, v_conv_w, v_conv_b, v_lru_wa, v_lru_ba, v_lru_wx, v_lru_bx, v_lru_lambda, v_w_branch_ret, v_w_branch_lru, v_w_out, v_ffn_norm_w, v_w_ffn_in, v_w_ffn_out, v_final_norm_w):
    args = locals()
    wts = {n: args[n] for n in WEIGHT_ORDER}
    mom = {n: args["m_" + n] for n in WEIGHT_ORDER}
    var = {n: args["v_" + n] for n in WEIGHT_ORDER}
    place = jnp.stack([2 * lax.axis_index("x") + lax.axis_index("y"), lax.axis_index("c")]).astype(jnp.int32)

    shards = {n: wts[n][0].astype(BF16) for n in BIG_PIECES}
    shards["meta_tokens"] = wts["meta_tokens"]
    shards["conv_w"] = wts["conv_w"][0]
    plan = _CommPlan(shards, place)
    (first,) = _run_riders("gather_first", [_gather_rider(shards, FIRST_WEIGHTS)])
    w = dict(zip(FIRST_WEIGHTS, first))
    for n in VEC_NAMES:
        w[n] = wts[n].reshape(1, D)
    chips = jnp.bitwise_xor(place[0], jnp.array((0,) + SHARD_ORDER, dtype=jnp.int32))
    w["route"] = jnp.concatenate([place, jnp.stack([2 * chips, 2 * chips + 1], axis=1).reshape(8)])
    w["w_in_shard"] = shards["w_in"]

    grad_x, grad_head, _, stats = _local_step(x[0], loss_target[0], w, plan)
    partial = jnp.concatenate(stats + [grad_head[PAD_ROWS:]], axis=0)
    shard_grads, blocks = plan.finish(partial)

    out = {}
    for n in BIG_PIECES:
        shape2d = (-1, wts[n].shape[-1])
        res = _adamw(n, *[a.reshape(shape2d) for a in (shard_grads[n], wts[n], mom[n], var[n])])
        out[n] = [r.reshape(wts[n].shape) for r in res]

    vecs = [tuple(a[n].reshape(1, D) for a in (wts, mom, var)) for n in VEC_NAMES]
    conv = tuple(a["conv_w"][0] for a in (wts, mom, var))
    meta = tuple(a["meta_tokens"] for a in (wts, mom, var))
    res = _small_update(blocks, place, vecs, conv, meta)
    loss = res[0].reshape(())
    for j, n in enumerate(VEC_NAMES):
        out[n] = [r[j].reshape(wts[n].shape) for r in res[1:5]]
    out["conv_w"] = [r.reshape(wts["conv_w"].shape) for r in res[5:9]]
    out["meta_tokens"] = list(res[9:13])

    return (loss, grad_x.reshape(x.shape)) + tuple(out[n][kind] for kind in range(4) for n in WEIGHT_ORDER)
```

```python
import functools
import math

import jax
import jax.numpy as jnp
from jax import lax
from jax.experimental import pallas as pl
from jax.experimental.pallas import tpu as pltpu

F32 = jnp.float32
BF16 = jnp.bfloat16

LANES = 128
BF16_ROWS = 16

D = 1024
HEADS = 8
DH = 128
CHUNK = 256
N_META = 16
FRONT = 256
PAD_ROWS = FRONT - N_META
LRU_BLOCKS = 4
LRU_BLOCK = 256
LRU_C = 8.0
FFN = 2816
FFN_HALF = FFN // 2
IN_COLS = 8 * D
ROPE_BASE = 10000.0
EPS = 1e-6
QK_SCALE = DH ** -0.5

ADAM_LR = 0.001
ADAM_B1 = 0.9
ADAM_B2 = 0.999
ADAM_EPS = 1e-08
ADAM_WD = 0.01
ADAM_STEP = 10

TM = 256
TM_HEAVY = 768
MXU_TILE = 256
FFN_SUB = 2 * MXU_TILE
_FFN_SUBS = [(a, min(a + FFN_SUB, FFN)) for a in range(0, FFN, FFN_SUB)]
TM_MIX_BWD = 384
TM_LRU = 384
TM_MIX_FWD = 384
TM_IN_PROJ = 1408
VMEM_LIMIT = 60 * 1024 * 1024
TN_VMEM_BUDGET = 40 * 1024 * 1024

NT_DIMS = (((1,), (1,)), ((), ()))
TN_DIMS = (((0,), (0,)), ((), ()))
MESH = pl.DeviceIdType.MESH


def _params(sem=None):
    if sem is None:
        return pltpu.CompilerParams(vmem_limit_bytes=VMEM_LIMIT)
    return pltpu.CompilerParams(dimension_semantics=sem, vmem_limit_bytes=VMEM_LIMIT)


def _dot(a, b):
    return jnp.dot(a, b, preferred_element_type=F32)


def _dot_nt(a, b):
    return lax.dot_general(a, b, NT_DIMS, preferred_element_type=F32)


def _dot_tn(a, b):
    return lax.dot_general(a, b, TN_DIMS, preferred_element_type=F32)


def _sigmoid(z):
    return 1.0 / (1.0 + jnp.exp(-z))


def _log1p(x):
    return jnp.where(x < 1e-3, x * (1.0 - x * (0.5 - x * (1.0 / 3.0))), jnp.log(1.0 + x))


def _softplus(x):
    return jnp.maximum(x, 0.0) + _log1p(jnp.exp(-jnp.abs(x)))


def _one_minus_square(a, log_a):
    x = 2.0 * log_a
    series = -x * (1.0 + x * (0.5 + x * (1.0 / 6.0)))
    return jnp.where(x > -0.02, series, 1.0 - a * a)


_GELU_K = math.sqrt(2.0 / math.pi)


def _gelu_and_grad(x):
    inner = _GELU_K * (x + 0.044715 * x * x * x)
    t = jnp.tanh(inner)
    val = 0.5 * x * (1.0 + t)
    grad = 0.5 * (1.0 + t) + 0.5 * x * (1.0 - t * t) * _GELU_K * (1.0 + 3.0 * 0.044715 * x * x)
    return val, grad


def _row_ids(i, rows, shape):
    return i * rows + lax.broadcasted_iota(jnp.int32, shape, 0)


def _rope_tables(rows):
    inv_freq = ROPE_BASE ** (-jnp.arange(0, DH, 2, dtype=F32) / DH)
    block_pos = jnp.arange(rows // FRONT, dtype=jnp.int32) * FRONT - PAD_ROWS
    coarse = block_pos.astype(F32)[:, None] * inv_freq[None, :]
    fine = jnp.arange(FRONT, dtype=F32)[:, None] * inv_freq[None, :]
    ca, sa = jnp.cos(coarse)[:, None, :], jnp.sin(coarse)[:, None, :]
    cb, sb = jnp.cos(fine)[None, :, :], jnp.sin(fine)[None, :, :]
    cos = (ca * cb - sa * sb).reshape(rows, DH // 2)
    sin = (sa * cb + ca * sb).reshape(rows, DH // 2)
    return jnp.concatenate([cos, cos], axis=1), jnp.concatenate([-sin, sin], axis=1)


def _decay_tables():
    log_g = jnp.log(1.0 - 2.0 ** (-5.0 - jnp.arange(HEADS, dtype=F32)))
    idx = jnp.arange(CHUNK, dtype=F32)
    diff = idx[:, None] - idx[None, :]
    intra = jnp.where(diff[None] >= 0, jnp.exp(jnp.maximum(diff, 0.0)[None] * log_g[:, None, None]), 0.0)
    q_decay = jnp.exp((idx + 1.0)[:, None] * log_g[None, :])
    k_decay = jnp.exp((CHUNK - 1.0 - idx)[:, None] * log_g[None, :])
    chunk_decay = jnp.exp(CHUNK * log_g)
    wide = lambda a: jnp.repeat(a, DH, axis=-1)
    return intra, jnp.swapaxes(intra, 1, 2), wide(q_decay), wide(k_decay), wide(chunk_decay[None, :])


def _norm1(head, x2d, norm_w, riders=()):
    rows = FRONT + x2d.shape[0]
    tm = TM_IN_PROJ if rows % TM_IN_PROJ == 0 else TM_MIX_FWD
    nt = rows // tm

    def body(head_hbm, x_hbm, nw_ref, u_ref, rstd_ref, h0_sc, h0_sem):
        slot = _frame_rows(head_hbm, x_hbm, h0_sc, h0_sem, pl.program_id(0), tm, nt)
        hv = h0_sc[slot]
        rs = lax.rsqrt(jnp.mean(hv * hv, axis=-1, keepdims=True) + EPS)
        u_ref[...] = ((hv * rs) * nw_ref[...]).astype(BF16)
        rstd_ref[...] = rs

    return _hosted_call(
        body,
        name="norm1",
        grid=(nt,),
        in_specs=[_ANY, _ANY, pl.BlockSpec((1, D), lambda i: (0, 0))],
        out_specs=[pl.BlockSpec((tm, D), lambda i: (i, 0)), pl.BlockSpec((tm, 1), lambda i: (i, 0))],
        out_shape=[jax.ShapeDtypeStruct((rows, D), BF16), jax.ShapeDtypeStruct((rows, 1), F32)],
        scratch_shapes=[pltpu.VMEM((2, tm, D), F32), pltpu.SemaphoreType.DMA((3,))],
        args=(head, x2d, norm_w),
        riders=riders,
    )


def _heavy_tile(rows):
    return TM_HEAVY if rows % TM_HEAVY == 0 else TM


def _lru_tile(rows):
    return TM_LRU if rows % TM_LRU == 0 else TM


def _swap_3_4(group):
    return jnp.where(group == 3, 4, jnp.where(group == 4, 3, group))


SHARD_ORDER = (2, 3, 1)
LRU_IN_COL = 3
GATES_COL = 1


def _in_proj(u, w_shard, route, cos_t, sin_t, riders=()):
    rows = u.shape[0]
    tm = TM_IN_PROJ if rows % TM_IN_PROJ == 0 else _heavy_tile(rows)
    nt = rows // tm
    kind, shard = BIG_PIECES["w_in"]

    def body(route_ref, u_hbm, wsh_ref, cos_ref, sin_ref, proj_ref, wfull_ref,
             u_sc, w_sc, u_sem, w_sem, local_sem, ici_send, ici_recv, fwd_send, fwd_recv):
        g, i = pl.program_id(0), pl.program_id(1)
        s, c = route_ref[0], route_ref[1]
        gid = route_ref[2 + g]
        sibling = (s // 2, s % 2, 1 - c)
        local = pltpu.make_async_copy(wsh_ref, _full_region(wfull_ref, kind, shard, s, None), local_sem)
        u_copies = [pltpu.make_async_copy(u_hbm.at[pl.ds(t * tm, tm)], u_sc.at[t], u_sem.at[t]) for t in range(nt)]
        sends, arrivals = [], []
        for k in (1, 2, 3):
            s2, dev = _other_chip(s, c, k)
            sends.append(_remote(_shard_region(wsh_ref, shard, c), _full_region(wfull_ref, kind, shard, s, c),
                                 ici_send.at[k - 1], ici_recv.at[k - 1], dev))
            mine = _full_region(wfull_ref, kind, shard, s2, c)
            theirs = _full_region(wfull_ref, kind, shard, s2, 1 - c)
            arrivals.append((_remote(mine, mine, ici_send.at[k - 1], ici_recv.at[k - 1], dev),
                             _remote(mine, mine, fwd_send.at[k - 1], fwd_recv.at[k - 1], sibling),
                             _remote(theirs, theirs, fwd_send.at[k - 1], fwd_recv.at[k - 1], sibling)))

        slot = g % 2
        last_tile = i == nt - 1

        def block_copy(src, col, to_slot):
            return pltpu.make_async_copy(src.at[:, pl.ds(pl.multiple_of(col * D, D), D)], w_sc.at[to_slot],
                                         w_sem.at[to_slot])

        @pl.when(jnp.logical_and(g == 0, i == 0))
        def _():
            for k in SHARD_ORDER:
                sends[k - 1].start()
            local.start()
            for cp in u_copies:
                cp.start()
            cp = block_copy(wsh_ref, 0, 0)
            cp.start()
            cp.wait()

        @pl.when(jnp.logical_and(last_tile, g == 0))
        def _():
            block_copy(wsh_ref, 1, 1).start()

        for pos, k in enumerate(SHARD_ORDER, start=1):
            arrived, forward, forwarded = arrivals[k - 1]

            @pl.when(jnp.logical_and(last_tile, g == 2 * pos - 1))
            def _():
                arrived.wait_recv()
                forward.start()
                forwarded.wait_recv()
                block_copy(wfull_ref, route_ref[2 + 2 * pos], 0).start()

            @pl.when(jnp.logical_and(last_tile, g == 2 * pos))
            def _():
                block_copy(wfull_ref, route_ref[3 + 2 * pos], 1).start()

        @pl.when(jnp.logical_and(i == 0, g > 0))
        def _():
            block_copy(wfull_ref, 0, slot).wait()

        for t in range(nt):
            @pl.when(jnp.logical_and(g == 0, i == t))
            def _():
                u_copies[t].wait()

        acc = _dot(u_sc[i], w_sc[slot])

        @pl.when(gid < 2)
        def _():
            scale = jnp.where(gid == 1, QK_SCALE, 1.0).astype(F32)
            for h in range(HEADS):
                sl = slice(h * DH, (h + 1) * DH)
                blk = acc[:, sl]
                out = (blk * cos_ref[...] + pltpu.roll(blk, DH // 2, axis=1) * sin_ref[...]) * scale
                proj_ref[:, sl] = out.astype(BF16)

        @pl.when(gid >= 2)
        def _():
            proj_ref[...] = acc.astype(BF16)

        @pl.when(jnp.logical_and(g == 7, i == nt - 1))
        def _():
            local.wait()
            for cp in sends:
                cp.wait_send()
            for _, forward, _ in arrivals:
                forward.wait_send()

    return _hosted_call(
        body,
        name="in_proj",
        grid=(8, nt),
        in_specs=[
            _ANY, _ANY,
            pl.BlockSpec((tm, DH), lambda g, i, rt: (i, 0)),
            pl.BlockSpec((tm, DH), lambda g, i, rt: (i, 0)),
        ],
        out_specs=[pl.BlockSpec((tm, D), lambda g, i, rt: (i, _swap_3_4(rt[2 + g]))), _ANY],
        out_shape=[jax.ShapeDtypeStruct((rows, IN_COLS), BF16), jax.ShapeDtypeStruct((D, IN_COLS), BF16)],
        scratch_shapes=[
            pltpu.VMEM((nt, tm, D), BF16), pltpu.VMEM((2, D, D), BF16),
            pltpu.SemaphoreType.DMA((nt,)), pltpu.SemaphoreType.DMA((2,)), pltpu.SemaphoreType.DMA,
            pltpu.SemaphoreType.DMA((3,)), pltpu.SemaphoreType.DMA((3,)),
            pltpu.SemaphoreType.DMA((3,)), pltpu.SemaphoreType.DMA((3,)),
        ],
        args=(u, w_shard, cos_t, sin_t),
        riders=riders,
        prefetch=route,
        riders_after_body=True,
    )


def _retention_fwd(proj_bf, intra, q_dec, k_dec, c_dec, riders=()):
    rows = proj_bf.shape[0]
    nc = rows // CHUNK

    def body(q_ref, k_ref, v_ref, m_ref, qd_ref, kd_ref, cd_ref, o_ref, sprev_ref, s_sc):
        @pl.when(pl.program_id(0) == 0)
        def _():
            s_sc[...] = jnp.zeros_like(s_sc)

        for h in range(HEADS):
            sl = slice(h * DH, (h + 1) * DH)
            q, k, v = q_ref[:, sl], k_ref[:, sl], v_ref[:, sl]
            state = s_sc[h]
            state_b = state.astype(BF16)
            sprev_ref[0, h] = state_b
            s = _dot_nt(q, k) * m_ref[h]
            inner = _dot(s.astype(BF16), v)
            cross = _dot(q, state_b) * qd_ref[:, sl]
            o_ref[:, sl] = (inner + cross).astype(BF16)
            k_scaled = (k.astype(F32) * kd_ref[:, sl]).astype(BF16)
            s_sc[h] = state * cd_ref[:, sl] + _dot_tn(k_scaled, v)

    chunk_spec = lambda col: pl.BlockSpec((CHUNK, D), lambda c: (c, col))
    const2 = lambda shape: pl.BlockSpec(shape, lambda c: (0, 0))
    return _hosted_call(
        body,
        name="retention_fwd",
        grid=(nc,),
        in_specs=[
            chunk_spec(0), chunk_spec(1), chunk_spec(2),
            pl.BlockSpec((HEADS, CHUNK, CHUNK), lambda c: (0, 0, 0)),
            const2((CHUNK, D)), const2((CHUNK, D)), const2((1, D)),
        ],
        out_specs=[
            pl.BlockSpec((CHUNK, D), lambda c: (c, 0)),
            pl.BlockSpec((1, HEADS, DH, DH), lambda c: (c, 0, 0, 0)),
        ],
        out_shape=[
            jax.ShapeDtypeStruct((rows, D), BF16),
            jax.ShapeDtypeStruct((nc, HEADS, DH, DH), BF16),
        ],
        scratch_shapes=[pltpu.VMEM((HEADS, DH, DH), F32)],
        args=(proj_bf, proj_bf, proj_bf, intra, q_dec, k_dec, c_dec),
        riders=riders,
    )


def _shift_down(x, first8_prev, d):
    rolled = pltpu.roll(x, d, axis=0)
    head = pltpu.roll(jnp.concatenate([first8_prev, x[0:8]], axis=0), d, axis=0)[8:16]
    return rolled, head


def _conv_and_gates(x, prev8, cw_ref, cb_ref, wa_ref, wx_ref, ba_ref, bx_ref, lam_ref, c_sc):
    cw = cw_ref[...]
    conv = cb_ref[...] + cw[3:4] * x
    head = cb_ref[...] + cw[3:4] * x[0:8]
    for d in (1, 2, 3):
        rolled, hd = _shift_down(x, prev8, d)
        conv = conv + cw[3 - d:4 - d] * rolled
        head = head + cw[3 - d:4 - d] * hd
    c_sc[...] = conv
    c_sc[0:8, :] = head
    c = c_sc[...]
    zr, zi = [], []
    for g in range(LRU_BLOCKS):
        sl = slice(g * LRU_BLOCK, (g + 1) * LRU_BLOCK)
        cg = c[:, sl].astype(BF16)
        zr.append(_dot(cg, wa_ref[g]))
        zi.append(_dot(cg, wx_ref[g]))
    r = _sigmoid(jnp.concatenate(zr, axis=1) + ba_ref[...])
    gate_i = _sigmoid(jnp.concatenate(zi, axis=1) + bx_ref[...])
    sp = _softplus(-lam_ref[...])
    log_a = (-LRU_C) * r * sp
    a = jnp.exp(log_a)
    mult = jnp.sqrt(_one_minus_square(a, log_a))
    return c, r, gate_i, a, mult, log_a


def _lru_fwd(proj, conv_w, conv_b, wa, wx, ba, bx, lam, riders=()):
    rows = proj.shape[0]
    TM = _lru_tile(rows)
    nt = rows // TM

    def body(x_ref, cw_ref, cb_ref, wa_ref, wx_ref, ba_ref, bx_ref, lam_ref,
             h_ref, c_ref, r_ref, i_ref, la_ref, mult_ref, prev_sc, carry_sc, c_sc, a_sc, u_sc, h_sc):
        i = pl.program_id(0)

        @pl.when(i == 0)
        def _():
            prev_sc[...] = jnp.zeros_like(prev_sc)
            carry_sc[...] = jnp.zeros_like(carry_sc)

        x = x_ref[...].astype(F32)
        c, r, gate_i, a, mult, log_a = _conv_and_gates(x, prev_sc[...], cw_ref, cb_ref, wa_ref, wx_ref, ba_ref, bx_ref,
                                                       lam_ref, c_sc)
        for ref, val in ((c_ref, c), (r_ref, r), (i_ref, gate_i), (la_ref, log_a), (mult_ref, mult)):
            ref[...] = val.astype(BF16)
        prev_sc[...] = x[TM - 8:TM]
        valid = _row_ids(i, TM, (TM, D)) >= PAD_ROWS
        a_sc[...] = a
        u_sc[...] = jnp.where(valid, mult * gate_i * c, 0.0)
        row8 = lax.broadcasted_iota(jnp.int32, (8, D), 0)

        def group(gi, hprev):
            r0 = pl.multiple_of(gi * 8, 8)
            aa = a_sc[pl.ds(r0, 8), :]
            uu = u_sc[pl.ds(r0, 8), :]
            for d in (1, 2, 4):
                a_sh = jnp.where(row8 >= d, pltpu.roll(aa, d, axis=0), 1.0)
                u_sh = jnp.where(row8 >= d, pltpu.roll(uu, d, axis=0), 0.0)
                uu = uu + aa * u_sh
                aa = aa * a_sh
            hb = aa * hprev + uu
            h_sc[pl.ds(r0, 8), :] = hb
            return hb[7:8, :]

        hlast = lax.fori_loop(0, TM // 8, group, carry_sc[0:1, :])
        carry_sc[0:1, :] = hlast
        h_ref[...] = h_sc[...].astype(BF16)

    vec = pl.BlockSpec((1, D), lambda i: (0, 0))
    wspec = pl.BlockSpec((LRU_BLOCKS, LRU_BLOCK, LRU_BLOCK), lambda i: (0, 0, 0))
    return _hosted_call(
        body,
        name="lru_fwd",
        grid=(nt,),
        in_specs=[
            pl.BlockSpec((TM, D), lambda i: (i, LRU_IN_COL)),
            pl.BlockSpec((4, D), lambda i: (0, 0)),
            vec, wspec, wspec, vec, vec, vec,
        ],
        out_specs=[pl.BlockSpec((TM, D), lambda i: (i, 0)) for _ in range(6)],
        out_shape=[jax.ShapeDtypeStruct((rows, D), BF16) for _ in range(6)],
        scratch_shapes=[
            pltpu.VMEM((8, D), F32), pltpu.VMEM((8, D), F32),
            pltpu.VMEM((TM, D), F32), pltpu.VMEM((TM, D), F32), pltpu.VMEM((TM, D), F32), pltpu.VMEM((TM, D), F32),
        ],
        args=(proj, conv_w, conv_b, wa, wx, ba, bx, lam),
        riders=riders,
    )


def _group_norm(o):
    outs, rstds = [], []
    for h in range(HEADS):
        oh = o[:, h * DH:(h + 1) * DH]
        rs = lax.rsqrt(jnp.mean(oh * oh, axis=-1, keepdims=True) + EPS)
        outs.append(oh * rs)
        rstds.append(rs)
    return jnp.concatenate(outs, axis=1), rstds


def _frame_rows(head_hbm, x_hbm, buf, sems, i, tm, nt):
    slot = i % 2

    @pl.when(i == 0)
    def _():
        first = [pltpu.make_async_copy(head_hbm, buf.at[0, pl.ds(0, FRONT)], sems.at[2]),
                 pltpu.make_async_copy(x_hbm.at[pl.ds(0, tm - FRONT)], buf.at[0, pl.ds(FRONT, tm - FRONT)], sems.at[0])]
        for cp in first:
            cp.start()
        for cp in first:
            cp.wait()

    @pl.when(i + 1 < nt)
    def _():
        start = pl.multiple_of((i + 1) * tm - FRONT, LANES)
        pltpu.make_async_copy(x_hbm.at[pl.ds(start, tm)], buf.at[1 - slot], sems.at[1 - slot]).start()

    @pl.when(i > 0)
    def _():
        pltpu.make_async_copy(x_hbm.at[pl.ds(0, tm)], buf.at[slot], sems.at[slot]).wait()

    return slot


def _mix_fwd(head, x2d, o, proj, h_lru, w_br, w_bl, w_o, ffn_norm_w, riders=()):
    rows = o.shape[0]
    tm = TM_MIX_FWD
    assert rows % tm == 0 and tm > FRONT
    nt = rows // tm

    def body(head_hbm, x_hbm, o_ref, gates_ref, hl_ref, wbr_ref, wbl_ref, wo_ref, nw_ref,
             yret_ref, ylru_ref, h1_ref, u2_ref, rstd_ref, h0_sc, h0_sem):
        i = pl.program_id(0)
        slot = _frame_rows(head_hbm, x_hbm, h0_sc, h0_sem, i, tm, nt)
        gate = lambda j: gates_ref[:, j * D:(j + 1) * D].astype(F32)
        on, _ = _group_norm(o_ref[...].astype(F32))
        gret = gate(0)
        a_ret = (gret * _sigmoid(gret) * on).astype(BF16)
        y_ret = _dot(a_ret, wbr_ref[...])
        gl, _ = _gelu_and_grad(gate(1))
        a_lru = (gl * hl_ref[...].astype(F32)).astype(BF16)
        y_lru = _dot(a_lru, wbl_ref[...])
        mixed = (_sigmoid(gate(2)) * y_ret + _sigmoid(gate(3)) * y_lru).astype(BF16)
        delta = _dot(mixed, wo_ref[...])
        yret_ref[...] = y_ret.astype(BF16)
        ylru_ref[...] = y_lru.astype(BF16)
        h1 = h0_sc[slot] + delta
        rs = lax.rsqrt(jnp.mean(h1 * h1, axis=-1, keepdims=True) + EPS)
        h1_ref[...] = h1
        u2_ref[...] = ((h1 * rs) * nw_ref[...]).astype(BF16)
        rstd_ref[...] = rs

    tile = lambda col: pl.BlockSpec((tm, D), lambda i: (i, col))
    wspec = pl.BlockSpec((D, D), lambda i: (0, 0))
    return _hosted_call(
        body,
        name="mix_fwd",
        grid=(nt,),
        in_specs=[
            _ANY, _ANY,
            tile(0), pl.BlockSpec((tm, 4 * D), lambda i: (i, GATES_COL)), tile(0),
            wspec, wspec, wspec,
            pl.BlockSpec((1, D), lambda i: (0, 0)),
        ],
        out_specs=[tile(0), tile(0), tile(0), tile(0), pl.BlockSpec((tm, 1), lambda i: (i, 0))],
        out_shape=[
            jax.ShapeDtypeStruct((rows, D), BF16), jax.ShapeDtypeStruct((rows, D), BF16),
            jax.ShapeDtypeStruct((rows, D), F32), jax.ShapeDtypeStruct((rows, D), BF16),
            jax.ShapeDtypeStruct((rows, 1), F32),
        ],
        scratch_shapes=[pltpu.VMEM((2, tm, D), F32), pltpu.SemaphoreType.DMA((3,))],
        args=(head, x2d, o, proj, h_lru, w_br, w_bl, w_o, ffn_norm_w),
        riders=riders,
    )


def _ffn_fwd_gate(u2, w_ffn_in, riders=()):
    rows = u2.shape[0]
    tm = _heavy_tile(rows)
    hid = lambda: pl.BlockSpec((tm, FFN), lambda i: (i, 0))

    def body(u2_ref, w_hbm, silu_ref, dsilu_ref, act_ref, w_sc, w_sem):
        @pl.when(pl.program_id(0) == 0)
        def _():
            cp = pltpu.make_async_copy(w_hbm, w_sc, w_sem)
            cp.start()
            cp.wait()

        u2 = u2_ref[...]
        for a, b in _FFN_SUBS:
            g = _dot(u2, w_sc[:, a:b])
            up = _dot(u2, w_sc[:, FFN + a:FFN + b])
            sg = _sigmoid(g)
            silu = g * sg
            silu_ref[:, a:b] = silu.astype(BF16)
            dsilu_ref[:, a:b] = (up * (sg * (1.0 + g * (1.0 - sg)))).astype(BF16)
            act_ref[:, a:b] = (silu * up).astype(BF16)

    return _hosted_call(
        body,
        name="ffn_fwd_gate",
        grid=(rows // tm,),
        in_specs=[pl.BlockSpec((tm, D), lambda i: (i, 0)), _ANY],
        out_specs=[hid(), hid(), hid()],
        out_shape=[jax.ShapeDtypeStruct((rows, FFN), BF16)] * 3,
        scratch_shapes=[pltpu.VMEM((D, 2 * FFN), BF16), pltpu.SemaphoreType.DMA],
        args=(u2, w_ffn_in),
        riders=riders,
    )


def _ffn_out_loss(act, h1, w_ffn_out, target, final_norm_w):
    rows = act.shape[0]
    tm = _heavy_tile(rows)
    nt = rows // tm

    def body(act_ref, h1_ref, wo_ref, fnw_ref, tgt_hbm, dh2_ref, stats_ref, tgt_sc, tgt_sem):
        i = pl.program_id(0)
        slot = i % 2

        @pl.when(i == 0)
        def _():
            stats_ref[...] = jnp.zeros_like(stats_ref)
            tgt_sc[0, 0:FRONT, :] = jnp.zeros((FRONT, D), F32)
            cp = pltpu.make_async_copy(tgt_hbm.at[pl.ds(0, tm - FRONT)], tgt_sc.at[0, pl.ds(FRONT, tm - FRONT)], tgt_sem.at[0])
            cp.start()
            cp.wait()

        @pl.when(i + 1 < nt)
        def _():
            start = pl.multiple_of((i + 1) * tm - FRONT, FRONT)
            pltpu.make_async_copy(tgt_hbm.at[pl.ds(start, tm)], tgt_sc.at[1 - slot], tgt_sem.at[1 - slot]).start()

        @pl.when(i > 0)
        def _():
            pltpu.make_async_copy(tgt_hbm.at[pl.ds(0, tm)], tgt_sc.at[slot], tgt_sem.at[slot]).wait()

        h2 = h1_ref[...] + _dot(act_ref[...], wo_ref[...])
        rs = lax.rsqrt(jnp.mean(h2 * h2, axis=-1, keepdims=True) + EPS)
        n = h2 * rs
        fnw = fnw_ref[...]
        valid = _row_ids(i, tm, (tm, D)) >= FRONT
        diff = jnp.where(valid, n * fnw - tgt_sc[slot], 0.0)
        dy = diff * (1.0 / D)
        stats_ref[0:1, :] += (0.5 / D) * jnp.sum(diff * diff, axis=0, keepdims=True)
        stats_ref[1:2, :] += jnp.sum(dy * n, axis=0, keepdims=True)
        dn = dy * fnw
        dh2_ref[...] = rs * (dn - n * jnp.mean(dn * n, axis=-1, keepdims=True))

    return pl.pallas_call(
        body,
        name="ffn_out_loss",
        grid=(nt,),
        in_specs=[
            pl.BlockSpec((tm, FFN), lambda i: (i, 0)),
            pl.BlockSpec((tm, D), lambda i: (i, 0)),
            pl.BlockSpec((FFN, D), lambda i: (0, 0)),
            pl.BlockSpec((1, D), lambda i: (0, 0)),
            _ANY,
        ],
        out_specs=[pl.BlockSpec((tm, D), lambda i: (i, 0)), pl.BlockSpec((8, D), lambda i: (0, 0))],
        out_shape=[jax.ShapeDtypeStruct((rows, D), F32), jax.ShapeDtypeStruct((8, D), F32)],
        scratch_shapes=[pltpu.VMEM((2, tm, D), F32), pltpu.SemaphoreType.DMA((2,))],
        compiler_params=_params(("arbitrary",)),
    )(act, h1, w_ffn_out, final_norm_w, target)


def _ffn_bwd(dh2, silu, dsilu, h1, rstd2, w_ffn_in, w_ffn_out, ffn_norm_w):
    rows = dh2.shape[0]
    tm = _heavy_tile(rows)
    blk = lambda: pl.BlockSpec((tm, FFN), lambda i: (i, 0))

    def gate_body(dh2_ref, silu_ref, dsilu_ref, wo_hbm, dg_ref, dup_ref, dh2b_ref, wo_sc, wo_sem):
        @pl.when(pl.program_id(0) == 0)
        def _():
            cp = pltpu.make_async_copy(wo_hbm, wo_sc, wo_sem)
            cp.start()
            cp.wait()

        dh2b = dh2_ref[...].astype(BF16)
        dh2b_ref[...] = dh2b
        for a, b in _FFN_SUBS:
            dact = _dot_nt(dh2b, wo_sc[a:b, :])
            dup_ref[:, a:b] = (dact * silu_ref[:, a:b].astype(F32)).astype(BF16)
            dg_ref[:, a:b] = (dact * dsilu_ref[:, a:b].astype(F32)).astype(BF16)

    dg, dup, dh2b = pl.pallas_call(
        gate_body,
        name="ffn_bwd_gate",
        grid=(rows // tm,),
        in_specs=[pl.BlockSpec((tm, D), lambda i: (i, 0)), blk(), blk(), _ANY],
        out_specs=[blk(), blk(), pl.BlockSpec((tm, D), lambda i: (i, 0))],
        out_shape=[jax.ShapeDtypeStruct((rows, FFN), BF16)] * 2 + [jax.ShapeDtypeStruct((rows, D), BF16)],
        scratch_shapes=[pltpu.VMEM((FFN, D), BF16), pltpu.SemaphoreType.DMA],
        compiler_params=_params(("arbitrary",)),
    )(dh2, silu, dsilu, w_ffn_out)

    def body(dg_ref, dup_ref, dh2_ref, h1_ref, rstd_ref, w_hbm, nw_ref, dh1_ref, stats_ref, w_sc, w_sem):
        @pl.when(pl.program_id(0) == 0)
        def _():
            stats_ref[...] = jnp.zeros_like(stats_ref)
            cp = pltpu.make_async_copy(w_hbm, w_sc, w_sem)
            cp.start()
            cp.wait()

        du = _dot_nt(dg_ref[...], w_sc[:, 0:FFN]) + _dot_nt(dup_ref[...], w_sc[:, FFN:2 * FFN])
        rs = rstd_ref[...]
        n = h1_ref[...] * rs
        stats_ref[0:1, :] += jnp.sum(du * n, axis=0, keepdims=True)
        dn = du * nw_ref[...]
        dh1_ref[...] = dh2_ref[...] + rs * (dn - n * jnp.mean(dn * n, axis=-1, keepdims=True))

    row = lambda width: pl.BlockSpec((tm, width), lambda i: (i, 0))
    dh1, stats = pl.pallas_call(
        body,
        name="ffn_bwd_in",
        grid=(rows // tm,),
        in_specs=[row(FFN), row(FFN), row(D), row(D), row(1), _ANY, pl.BlockSpec((1, D), lambda i: (0, 0))],
        out_specs=[row(D), pl.BlockSpec((8, D), lambda i: (0, 0))],
        out_shape=[jax.ShapeDtypeStruct((rows, D), F32), jax.ShapeDtypeStruct((8, D), F32)],
        scratch_shapes=[pltpu.VMEM((D, 2 * FFN), BF16), pltpu.SemaphoreType.DMA],
        compiler_params=_params(("arbitrary",)),
    )(dg, dup, dh2, h1, rstd2, w_ffn_in, ffn_norm_w)
    return dg, dup, dh2b, dh1, stats


def _mix_bwd(dh1, o, proj, h_lru, y_ret, y_lru, w_br, w_bl, w_o, riders=()):
    rows = dh1.shape[0]
    tm = TM_MIX_BWD
    nt = rows // tm

    def body(dh1_ref, o_ref, gates_ref, hl_ref, yret_ref, ylru_ref, wbr_hbm, wbl_hbm, wo_hbm,
             dproj_ref, do_ref, dhl_ref, mixed_ref, aret_ref, alru_ref, dyret_ref, dylru_ref, w_sc, w_sem):
        @pl.when(pl.program_id(0) == 0)
        def _():
            copies = [pltpu.make_async_copy(src, w_sc.at[q], w_sem.at[q]) for q, src in enumerate((wbr_hbm, wbl_hbm, wo_hbm))]
            for cp in copies:
                cp.start()
            for cp in copies:
                cp.wait()

        wbr_ref, wbl_ref, wo_ref = w_sc.at[0], w_sc.at[1], w_sc.at[2]
        gate = lambda j: gates_ref[:, j * D:(j + 1) * D].astype(F32)
        dmixed = _dot_nt(dh1_ref[...].astype(BF16), wo_ref[...])
        y_ret, y_lru = yret_ref[...].astype(F32), ylru_ref[...].astype(F32)
        sa, sb = _sigmoid(gate(2)), _sigmoid(gate(3))
        mixed_ref[...] = (sa * y_ret + sb * y_lru).astype(BF16)
        dga = dmixed * y_ret * sa * (1.0 - sa)
        dgb = dmixed * y_lru * sb * (1.0 - sb)
        dy_ret = (dmixed * sa).astype(BF16)
        dy_lru = (dmixed * sb).astype(BF16)
        dyret_ref[...] = dy_ret
        dylru_ref[...] = dy_lru
        da_ret = _dot_nt(dy_ret, wbr_ref[...])
        da_lru = _dot_nt(dy_lru, wbl_ref[...])

        gret = gate(0)
        sg = _sigmoid(gret)
        silu = gret * sg
        on, rstds = _group_norm(o_ref[...].astype(F32))
        aret_ref[...] = (silu * on).astype(BF16)
        dgret = da_ret * on * (sg * (1.0 + gret * (1.0 - sg)))
        don = da_ret * silu
        for h in range(HEADS):
            sl = slice(h * DH, (h + 1) * DH)
            onh, donh = on[:, sl], don[:, sl]
            do_ref[:, sl] = (rstds[h] * (donh - onh * jnp.mean(donh * onh, axis=-1, keepdims=True))).astype(BF16)

        gl, gl_grad = _gelu_and_grad(gate(1))
        hl = hl_ref[...].astype(F32)
        alru_ref[...] = (gl * hl).astype(BF16)
        dlgate = da_lru * hl * gl_grad
        dhl_ref[...] = (da_lru * gl).astype(BF16)

        for j, val in enumerate((dgret, dlgate, dga, dgb)):
            dproj_ref[:, j * D:(j + 1) * D] = val.astype(BF16)

    tile = lambda col: pl.BlockSpec((tm, D), lambda i: (i, col))
    gates = pl.BlockSpec((tm, 4 * D), lambda i: (i, GATES_COL))
    bf = lambda: jax.ShapeDtypeStruct((rows, D), BF16)
    return _hosted_call(
        body,
        name="mix_bwd",
        grid=(nt,),
        in_specs=[tile(0), tile(0), gates, tile(0), tile(0), tile(0), _ANY, _ANY, _ANY],
        out_specs=[pl.BlockSpec((tm, 4 * D), lambda i: (i, GATES_COL))] + [tile(0)] * 7,
        out_shape=[jax.ShapeDtypeStruct((rows, IN_COLS), BF16)] + [bf() for _ in range(7)],
        scratch_shapes=[pltpu.VMEM((3, D, D), BF16), pltpu.SemaphoreType.DMA((3,))],
        args=(dh1, o, proj, h_lru, y_ret, y_lru, w_br, w_bl, w_o),
        riders=riders,
    )


def _retention_bwd(dproj, proj_bf, do, sprev, intra, intra_t, q_dec, k_dec, c_dec, cos_t, sin_t, riders=()):
    rows = proj_bf.shape[0]
    nc = rows // CHUNK

    def body(dproj_in_ref, q_ref, k_ref, v_ref, do_ref, sprev_ref, m_ref, mt_ref, qd_ref, kd_ref, cd_ref, cos_ref, sin_ref,
             dproj_ref, ds_sc):
        @pl.when(pl.program_id(0) == 0)
        def _():
            ds_sc[...] = jnp.zeros_like(ds_sc)

        cos, sin = cos_ref[...], sin_ref[...]

        def unrotate(dy):
            return dy * cos - pltpu.roll(dy, DH // 2, axis=1) * sin

        for h in range(HEADS):
            sl = slice(h * DH, (h + 1) * DH)
            q, k, v = q_ref[:, sl], k_ref[:, sl], v_ref[:, sl]
            do = do_ref[:, sl]
            dob = do.astype(BF16)
            doq = (do * qd_ref[:, sl]).astype(BF16)
            state_prev = sprev_ref[0, h]
            dstate = ds_sc[h]
            dstate_b = dstate.astype(BF16)
            s_t = (_dot_nt(k, q) * mt_ref[h]).astype(BF16)
            ds_t = (_dot_nt(v, dob) * mt_ref[h]).astype(BF16)
            ds = (_dot_nt(dob, v) * m_ref[h]).astype(BF16)
            kd = kd_ref[:, sl]
            dq = _dot(ds, k) + _dot_nt(doq, state_prev)
            dk = _dot(ds_t, q) + _dot_nt(v, dstate_b) * kd
            k_scaled = (k.astype(F32) * kd).astype(BF16)
            dv = _dot(s_t, dob) + _dot(k_scaled, dstate_b)
            ds_sc[h] = dstate * cd_ref[:, sl] + _dot_tn(q, doq)
            for part, val in enumerate((unrotate(dq), unrotate(dk) * QK_SCALE, dv)):
                dproj_ref[:, part * D + h * DH:part * D + (h + 1) * DH] = val.astype(BF16)

    rev = lambda c: nc - 1 - c
    chunk_spec = lambda col: pl.BlockSpec((CHUNK, D), lambda c: (rev(c), col))
    const2 = lambda shape: pl.BlockSpec(shape, lambda c: (0, 0))
    const3 = pl.BlockSpec((HEADS, CHUNK, CHUNK), lambda c: (0, 0, 0))
    return _hosted_call(
        body,
        name="retention_bwd",
        grid=(nc,),
        in_specs=[
            pl.BlockSpec(memory_space=pl.ANY),
            chunk_spec(0), chunk_spec(1), chunk_spec(2), chunk_spec(0),
            pl.BlockSpec((1, HEADS, DH, DH), lambda c: (rev(c), 0, 0, 0)),
            const3, const3,
            const2((CHUNK, D)), const2((CHUNK, D)), const2((1, D)),
            pl.BlockSpec((CHUNK, DH), lambda c: (rev(c), 0)),
            pl.BlockSpec((CHUNK, DH), lambda c: (rev(c), 0)),
        ],
        out_specs=[pl.BlockSpec((CHUNK, 3 * D), lambda c: (rev(c), 0))],
        out_shape=[jax.ShapeDtypeStruct(dproj.shape, BF16)],
        aliases={0: 0},
        scratch_shapes=[pltpu.VMEM((HEADS, DH, DH), F32)],
        args=(dproj, proj_bf, proj_bf, proj_bf, do, sprev, intra, intra_t, q_dec, k_dec, c_dec, cos_t, sin_t),
        riders=riders,
    )


def _lru_bwd(dproj, proj, saved, dhl, conv_w, wa, wx, lam, riders=()):
    rows = proj.shape[0]
    TM = _lru_tile(rows)
    nt = rows // TM
    per8 = TM // 8

    def body(dproj_in_ref, x_ref, xprev_ref, h_ref, hprev_ref, c_ref, r_ref, i_ref, la_ref, mult_ref, dhl_ref,
             cw_ref, wa_ref, wx_ref, lam_ref,
             dproj_ref, dwa_ref, dwx_ref, stats_ref, anext_sc, dhnext_sc, dcnext_sc, c_sc, b_sc, dh_sc):
        step = pl.program_id(0)
        i = nt - 1 - step

        @pl.when(step == 0)
        def _():
            anext_sc[...] = jnp.zeros_like(anext_sc)
            dhnext_sc[...] = jnp.zeros_like(dhnext_sc)
            dcnext_sc[...] = jnp.zeros_like(dcnext_sc)
            dwa_ref[...] = jnp.zeros_like(dwa_ref)
            dwx_ref[...] = jnp.zeros_like(dwx_ref)
            stats_ref[...] = jnp.zeros_like(stats_ref)

        first = i == 0
        x = x_ref[...].astype(F32)
        prev8 = jnp.where(first, 0.0, xprev_ref[8:16, :].astype(F32))
        c_b = c_ref[...]
        c, r, gate_i, mult = (ref[...].astype(F32) for ref in (c_ref, r_ref, i_ref, mult_ref))
        a = jnp.exp(la_ref[...].astype(F32))
        sp = _softplus(-lam_ref[...])
        dh_sc[...] = dhl_ref[...].astype(F32)

        b_sc[...] = pltpu.roll(a, TM - 1, axis=0)
        b_sc[TM - 1:TM, :] = anext_sc[0:1, :]
        anext_sc[0:1, :] = a[0:1, :]
        row8 = lax.broadcasted_iota(jnp.int32, (8, D), 0)

        def group(gi, dhnext):
            r0 = pl.multiple_of((per8 - 1 - gi) * 8, 8)
            bb = b_sc[pl.ds(r0, 8), :]
            uu = dh_sc[pl.ds(r0, 8), :]
            for d in (1, 2, 4):
                b_sh = jnp.where(row8 < 8 - d, pltpu.roll(bb, 8 - d, axis=0), 1.0)
                u_sh = jnp.where(row8 < 8 - d, pltpu.roll(uu, 8 - d, axis=0), 0.0)
                uu = uu + bb * u_sh
                bb = bb * b_sh
            dhb = bb * dhnext + uu
            dh_sc[pl.ds(r0, 8), :] = dhb
            return dhb[0:1, :]

        dhfirst = lax.fori_loop(0, per8, group, dhnext_sc[0:1, :])
        dhnext_sc[0:1, :] = dhfirst
        dh = dh_sc[...]

        h = h_ref[...].astype(F32)
        hprev8 = jnp.where(first, 0.0, hprev_ref[8:16, :].astype(F32))
        h_dn, h_head = _shift_down(h, hprev8, 1)
        c_sc[...] = h_dn
        c_sc[0:8, :] = h_head
        h_before = c_sc[...]

        valid = _row_ids(i, TM, (TM, D)) >= PAD_ROWS
        da = dh * h_before
        du = jnp.where(valid, dh, 0.0)
        dmult = du * gate_i * c
        dgate_i = du * mult * c
        dc = du * mult * gate_i
        dla = da * a - dmult * (a * a) / mult
        dla = jnp.where(valid, dla, 0.0)
        dr = dla * ((-LRU_C) * sp)
        dzr = dr * r * (1.0 - r)
        dzi = dgate_i * gate_i * (1.0 - gate_i)
        stats_ref[1:2, :] += jnp.sum(dzr, axis=0, keepdims=True)
        stats_ref[2:3, :] += jnp.sum(dzi, axis=0, keepdims=True)
        stats_ref[3:4, :] += jnp.sum(dla * ((-LRU_C) * r), axis=0, keepdims=True)
        dc_gate = []
        for g in range(LRU_BLOCKS):
            sl = slice(g * LRU_BLOCK, (g + 1) * LRU_BLOCK)
            cg = c_b[:, sl]
            dzr_g = dzr[:, sl].astype(BF16)
            dzi_g = dzi[:, sl].astype(BF16)
            dc_gate.append(_dot_nt(dzr_g, wa_ref[g]) + _dot_nt(dzi_g, wx_ref[g]))
            dwa_ref[g] += _dot_tn(cg, dzr_g)
            dwx_ref[g] += _dot_tn(cg, dzi_g)
        dc = dc + jnp.concatenate(dc_gate, axis=1)

        cw = cw_ref[...]
        stats_ref[0:1, :] += jnp.sum(dc, axis=0, keepdims=True)
        stats_ref[7:8, :] += jnp.sum(dc * x, axis=0, keepdims=True)
        dx = cw[3:4] * dc
        tail_src = jnp.concatenate([dc[TM - 8:TM], dcnext_sc[...]], axis=0)
        dx_tail = cw[3:4] * dc[TM - 8:TM]
        for d in (1, 2, 3):
            dx = dx + cw[3 - d:4 - d] * pltpu.roll(dc, TM - d, axis=0)
            dx_tail = dx_tail + cw[3 - d:4 - d] * pltpu.roll(tail_src, 16 - d, axis=0)[0:8]
            rolled, hd = _shift_down(x, prev8, d)
            b_sc[...] = rolled
            b_sc[0:8, :] = hd
            stats_ref[7 - d:8 - d, :] += jnp.sum(dc * b_sc[...], axis=0, keepdims=True)
        dcnext_sc[...] = dc[0:8]
        dproj_ref[...] = dx.astype(BF16)
        dproj_ref[TM - 8:TM, :] = dx_tail.astype(BF16)

    rev = lambda s: nt - 1 - s
    vec = pl.BlockSpec((1, D), lambda s: (0, 0))
    wspec = pl.BlockSpec((LRU_BLOCKS, LRU_BLOCK, LRU_BLOCK), lambda s: (0, 0, 0))
    prev16 = lambda col: pl.BlockSpec((BF16_ROWS, D), lambda s: (jnp.maximum(rev(s) * (TM // BF16_ROWS) - 1, 0), col))
    tile = lambda: pl.BlockSpec((TM, D), lambda s: (rev(s), 0))
    h_lru, c_sv, r_sv, i_sv, la_sv, mult_sv = saved
    return _hosted_call(
        body,
        name="lru_bwd",
        grid=(nt,),
        in_specs=[
            pl.BlockSpec(memory_space=pl.ANY),
            pl.BlockSpec((TM, D), lambda s: (rev(s), LRU_IN_COL)), prev16(LRU_IN_COL),
            tile(), prev16(0),
            tile(), tile(), tile(), tile(), tile(), tile(),
            pl.BlockSpec((4, D), lambda s: (0, 0)),
            wspec, wspec, vec,
        ],
        out_specs=[
            pl.BlockSpec((TM, D), lambda s: (rev(s), LRU_IN_COL)),
            wspec, wspec,
            pl.BlockSpec((8, D), lambda s: (0, 0)),
        ],
        out_shape=[
            jax.ShapeDtypeStruct(dproj.shape, BF16),
            jax.ShapeDtypeStruct((LRU_BLOCKS, LRU_BLOCK, LRU_BLOCK), F32),
            jax.ShapeDtypeStruct((LRU_BLOCKS, LRU_BLOCK, LRU_BLOCK), F32),
            jax.ShapeDtypeStruct((8, D), F32),
        ],
        aliases={0: 0},
        scratch_shapes=[
            pltpu.VMEM((8, D), F32), pltpu.VMEM((8, D), F32), pltpu.VMEM((8, D), F32),
            pltpu.VMEM((TM, D), F32), pltpu.VMEM((TM, D), F32), pltpu.VMEM((TM, D), F32),
        ],
        args=(dproj, proj, proj, h_lru, h_lru, c_sv, r_sv, i_sv, la_sv, mult_sv, dhl, conv_w, wa, wx, lam),
        riders=riders,
    )


def _in_proj_bwd(dproj, w_in, part, prev=None, riders=()):
    rows = dproj.shape[0]
    tm = _heavy_tile(rows)
    nt = rows // tm
    split = max(1, nt // 4 + 1) if nt > 1 else 1
    first = 0 if part == 0 else split
    count = split if part == 0 else nt - split

    def body(*refs):
        dproj_ref, w_hbm, du_ref, w_sc, w_sem = refs[-5:]

        @pl.when(pl.program_id(0) == 0)
        def _():
            moves = [((0, 3), 0), ((4, 1), 3), ((3, 1), 4), ((5, 3), 5)]
            copies = [pltpu.make_async_copy(w_hbm.at[:, pl.ds(src * D, n * D)], w_sc.at[:, pl.ds(dst * D, n * D)], w_sem.at[q])
                      for q, ((src, n), dst) in enumerate(moves)]
            for cp in copies:
                cp.start()
            for cp in copies:
                cp.wait()

        du_ref[...] = _dot_nt(dproj_ref[...], w_sc[...])

    in_specs = [pl.BlockSpec((tm, IN_COLS), lambda i: (first + i, 0)), _ANY]
    args = (dproj, w_in)
    if prev is not None:
        in_specs = [_ANY] + in_specs
        args = (prev,) + args
    return _hosted_call(
        body,
        name="in_proj_bwd_%d" % part,
        grid=(count,),
        in_specs=in_specs,
        out_specs=[pl.BlockSpec((tm, D), lambda i: (first + i, 0))],
        out_shape=[jax.ShapeDtypeStruct((rows, D), F32)],
        scratch_shapes=[pltpu.VMEM((D, IN_COLS), BF16), pltpu.SemaphoreType.DMA((4,))],
        aliases={0: 0} if prev is not None else None,
        args=args,
        riders=riders,
    )


def _norm1_bwd(du, dh1, head, x2d, rstd1, norm_w, riders=()):
    rows = du.shape[0]

    def body(du_ref, dh1_ref, head_ref, x_ref, rstd_ref, nw_ref, gx_ref, ghead_ref, stats_ref):
        i = pl.program_id(0)

        @pl.when(i == 0)
        def _():
            stats_ref[...] = jnp.zeros_like(stats_ref)

        def finish(h0, out_ref):
            du = du_ref[...]
            rs = rstd_ref[...]
            n = h0 * rs
            stats_ref[0:1, :] += jnp.sum(du * n, axis=0, keepdims=True)
            dn = du * nw_ref[...]
            out_ref[...] = dh1_ref[...] + rs * (dn - n * jnp.mean(dn * n, axis=-1, keepdims=True))

        @pl.when(i == 0)
        def _():
            finish(head_ref[...], ghead_ref)

        @pl.when(i > 0)
        def _():
            finish(x_ref[...], gx_ref)

    tile = pl.BlockSpec((TM, D), lambda i: (i, 0))
    return _hosted_call(
        body,
        name="norm1_bwd",
        grid=(rows // TM,),
        in_specs=[
            tile, tile,
            pl.BlockSpec((FRONT, D), lambda i: (0, 0)),
            pl.BlockSpec((TM, D), lambda i: (jnp.maximum(i - 1, 0), 0)),
            pl.BlockSpec((TM, 1), lambda i: (i, 0)),
            pl.BlockSpec((1, D), lambda i: (0, 0)),
        ],
        out_specs=[
            pl.BlockSpec((TM, D), lambda i: (jnp.maximum(i - 1, 0), 0)),
            pl.BlockSpec((FRONT, D), lambda i: (0, 0)),
            pl.BlockSpec((8, D), lambda i: (0, 0)),
        ],
        out_shape=[
            jax.ShapeDtypeStruct(x2d.shape, F32),
            jax.ShapeDtypeStruct((FRONT, D), F32),
            jax.ShapeDtypeStruct((8, D), F32),
        ],
        scratch_shapes=[],
        args=(du, dh1, head, x2d, rstd1, norm_w),
        riders=riders,
    )


def _matmul_tn(name, x, dy, out_cols, col0=0, prev=None, k_block=None, n_block=None, riders=(), col_map=None):
    col_map = col_map or (lambda n: n)
    rows, kdim = x.shape
    ndim = dy.shape[1]
    kb = k_block or kdim
    nb = n_block or ndim
    step_bytes = lambda t: 2 * t * (kb * x.dtype.itemsize + nb * dy.dtype.itemsize) + 2 * kb * nb * 4
    tr = next(t for t in (2816, 1408, TM_HEAVY, TM) if rows % t == 0 and (t == TM or step_bytes(t) <= TN_VMEM_BUDGET))
    nr, nk, nn = rows // tr, kdim // kb, ndim // nb
    cb0 = col0 // nb

    def body(*refs):
        x_ref, dy_ref, out_ref = refs[-3], refs[-2], refs[-1]
        part = _dot_tn(x_ref[...].astype(BF16), dy_ref[...].astype(BF16))

        @pl.when(pl.program_id(2) == 0)
        def _():
            out_ref[...] = part

        @pl.when(pl.program_id(2) > 0)
        def _():
            out_ref[...] += part

    in_specs = [
        pl.BlockSpec((tr, kb), lambda n, k, r: (r, k)),
        pl.BlockSpec((tr, nb), lambda n, k, r: (r, n)),
    ]
    args = [x, dy]
    aliases = {}
    if prev is not None:
        in_specs = [pl.BlockSpec(memory_space=pl.ANY)] + in_specs
        args = [prev] + args
        aliases = {0: 0}
    (out,), rider_outs = _hosted_call(
        body,
        name=name,
        grid=(nn, nk, nr),
        in_specs=in_specs,
        out_specs=[pl.BlockSpec((kb, nb), lambda n, k, r: (k, cb0 + col_map(n)))],
        out_shape=[jax.ShapeDtypeStruct((kdim, out_cols), F32)],
        scratch_shapes=[],
        aliases=aliases,
        args=args,
        riders=riders,
    )
    return (out, rider_outs) if riders else out


def _local_step(x2d, target, w, plan):
    rows = FRONT + x2d.shape[0]
    head = jnp.concatenate([jnp.zeros((PAD_ROWS, D), F32), w["meta_tokens"]], axis=0)
    cos_t, sin_t = _rope_tables(rows)
    intra, intra_t, q_dec, k_dec, c_dec = _decay_tables()
    grads = {}

    def hosted(host, fn, *args, **kwargs):
        outs, rider_outs = fn(*args, riders=plan.riders(host, w, grads), **kwargs)
        plan.after(host, rider_outs, w, grads)
        return outs

    (u1, rstd1), _ = _norm1(head, x2d, w["mix_norm_w"])
    proj, w["w_in"] = hosted("in_proj", _in_proj, u1, w["w_in_shard"], w["route"], cos_t, sin_t)
    o, sprev = hosted("retention_fwd", _retention_fwd, proj, intra, q_dec, k_dec, c_dec)
    lru_args = (w["conv_w"], w["conv_b"], w["lru_wa"], w["lru_wx"], w["lru_ba"], w["lru_bx"], w["lru_lambda"])
    lru_saved = hosted("lru_fwd", _lru_fwd, proj, *lru_args)
    h_lru = lru_saved[0]
    y_ret, y_lru, h1, u2, rstd2 = hosted("mix_fwd", _mix_fwd, head, x2d, o, proj, h_lru, w["w_branch_ret"],
                                         w["w_branch_lru"], w["w_out"], w["ffn_norm_w"])
    silu, dsilu, act = hosted("ffn_fwd_gate", _ffn_fwd_gate, u2, w["w_ffn_in"])
    dh2, stats_loss = _ffn_out_loss(act, h1, w["w_ffn_out"], target, w["final_norm_w"])

    dg, dup, dh2b, dh1, stats_ffn = _ffn_bwd(dh2, silu, dsilu, h1, rstd2, w["w_ffn_in"], w["w_ffn_out"], w["ffn_norm_w"])
    grads["w_ffn_in"] = _matmul_tn("dw_ffn_up", u2, dup, 2 * FFN, col0=FFN, n_block=FFN_HALF,
                                   prev=_matmul_tn("dw_ffn_gate", u2, dg, 2 * FFN, n_block=FFN_HALF))
    grads["w_ffn_out"] = _matmul_tn("dw_ffn_out", act, dh2b, D, k_block=FFN_HALF)
    (dproj, do, dhl, mixed, a_ret, a_lru, dy_ret, dy_lru) = hosted(
        "mix_bwd", _mix_bwd, dh1, o, proj, h_lru, y_ret, y_lru, w["w_branch_ret"], w["w_branch_lru"], w["w_out"])
    grads["w_out"] = _matmul_tn("dw_out", mixed, dh1, D)
    grads["w_branch_ret"] = _matmul_tn("dw_branch_ret", a_ret, dy_ret, D)
    grads["w_branch_lru"] = _matmul_tn("dw_branch_lru", a_lru, dy_lru, D)
    (dproj,) = hosted("retention_bwd", _retention_bwd, dproj, proj, do, sprev, intra, intra_t, q_dec, k_dec, c_dec,
                      cos_t, sin_t)
    dproj, grads["lru_wa"], grads["lru_wx"], stats_lru = hosted(
        "lru_bwd", _lru_bwd, dproj, proj, lru_saved, dhl, w["conv_w"], w["lru_wa"], w["lru_wx"], w["lru_lambda"])
    grads["w_in"], rider_outs = _matmul_tn("dw_in", u1, dproj, IN_COLS, n_block=D, col_map=_swap_3_4,
                                           riders=plan.riders("dw_in", w, grads))
    plan.after("dw_in", rider_outs, w, grads)
    (du1,) = hosted("in_proj_bwd_0", _in_proj_bwd, dproj, w["w_in"], 0)
    (du1,) = hosted("in_proj_bwd_1", _in_proj_bwd, dproj, w["w_in"], 1, du1)
    (grad_x, grad_head, stats_in), _ = _norm1_bwd(du1, dh1, head, x2d, rstd1, w["mix_norm_w"])
    return grad_x, grad_head, grads, [stats_loss, stats_ffn, stats_in, stats_lru]


BIG_PIECES = {
    "w_in": ("col", (D, 2 * D)),
    "w_ffn_in": ("col", (D, FFN_HALF)),
    "w_ffn_out": ("row", (FFN // 4, D)),
    "w_branch_ret": ("row", (D // 4, D)),
    "w_branch_lru": ("row", (D // 4, D)),
    "w_out": ("row", (D // 4, D)),
    "lru_wa": ("lru", (LRU_BLOCKS, LRU_BLOCK // 4, LRU_BLOCK)),
    "lru_wx": ("lru", (LRU_BLOCKS, LRU_BLOCK // 4, LRU_BLOCK)),
}
SMALL_PIECES = {"meta_tokens": ("col", (N_META, D // 4)), "conv_w": ("col", (4, D // 4))}


def _full_shape(kind, shard):
    if kind == "col":
        return (shard[0], 4 * shard[1])
    if kind == "row":
        return (4 * shard[0], shard[1])
    return (shard[0], 4 * shard[1], shard[2])


def _half_shape(kind, shard):
    return (shard[0] // 2,) + tuple(shard[1:])


def _aligned(start, multiple):
    return start if isinstance(start, int) else pl.multiple_of(start, multiple)


def _lead(h, size):
    if h is None:
        return pl.ds(0, size)
    return pl.ds(_aligned(h * (size // 2), size // 2), size // 2)


def _full_region(ref, kind, shard, s, h):
    if kind == "col":
        return ref.at[_lead(h, shard[0]), pl.ds(_aligned(s * shard[1], shard[1]), shard[1])]
    if kind == "row":
        size = shard[0] if h is None else shard[0] // 2
        start = s * shard[0] + (0 if h is None else h * (shard[0] // 2))
        return ref.at[pl.ds(_aligned(start, BF16_ROWS), size), :]
    return ref.at[_lead(h, shard[0]), pl.ds(_aligned(s * shard[1], shard[1]), shard[1]), :]


def _shard_region(ref, shard, h):
    return ref.at[_lead(h, shard[0])]


def _place():
    x, y, c = lax.axis_index("x"), lax.axis_index("y"), lax.axis_index("c")
    return x, y, c, 2 * x + y


def _other_chip(s, c, k):
    s2 = jnp.bitwise_xor(s, k)
    return s2, (s2 // 2, s2 % 2, c)


def _remote(src, dst, send_sem, recv_sem, dev):
    return pltpu.make_async_remote_copy(src_ref=src, dst_ref=dst, send_sem=send_sem, recv_sem=recv_sem,
                                        device_id=dev, device_id_type=MESH)


_ANY = pl.BlockSpec(memory_space=pl.ANY)


class _Rider:
    def __init__(self, ins, out_shapes, sem_shapes, build, aliased=False):
        self.ins, self.out_shapes, self.sem_shapes, self.build, self.aliased = ins, out_shapes, sem_shapes, build, aliased


def _hosted_call(body, *, name, grid, in_specs, out_specs, out_shape, scratch_shapes, args, riders=(), aliases=None,
                 prefetch=None, riders_after_body=False):
    n_in, n_out, n_sc = len(in_specs), len(out_shape), len(scratch_shapes)
    r_in = [a for r in riders for a in r.ins]
    r_out = [s for r in riders for s in r.out_shapes]
    r_sem = [s for r in riders for s in r.sem_shapes]
    lead = () if prefetch is None else (prefetch,)
    assert prefetch is None or not (aliases or any(r.aliased for r in riders))

    def full_body(*refs):
        head, refs = refs[:len(lead)], refs[len(lead):]
        ins, rin = refs[:n_in], refs[n_in:n_in + len(r_in)]
        o0 = n_in + len(r_in)
        outs, rout = refs[o0:o0 + n_out], refs[o0 + n_out:o0 + n_out + len(r_out)]
        s0 = o0 + n_out + len(r_out)
        scratch, rsem = refs[s0:s0 + n_sc], refs[s0 + n_sc:]
        starts, waits = [], []
        pi = po = ps = 0
        for r in riders:
            st, wt = r.build(rin[pi:pi + len(r.ins)], rout[po:po + len(r.out_shapes)], rsem[ps:ps + len(r.sem_shapes)])
            starts += st
            waits += wt
            pi, po, ps = pi + len(r.ins), po + len(r.out_shapes), ps + len(r.sem_shapes)
        first = functools.reduce(jnp.logical_and, [pl.program_id(d) == 0 for d in range(len(grid))])
        last = functools.reduce(jnp.logical_and, [pl.program_id(d) == grid[d] - 1 for d in range(len(grid))])

        def start_riders():
            @pl.when(first)
            def _():
                for cp in starts:
                    cp.start()

        if riders and not riders_after_body:
            start_riders()
        body(*head, *ins, *outs, *scratch)
        if riders and riders_after_body:
            start_riders()
        if riders:
            @pl.when(last)
            def _():
                for wait in waits:
                    wait()

    io_aliases = dict(aliases or {})
    pi = po = 0
    for r in riders:
        if r.aliased:
            for q in range(len(r.ins)):
                io_aliases[n_in + pi + q] = n_out + po + q
        pi, po = pi + len(r.ins), po + len(r.out_shapes)
    specs = dict(
        grid=grid,
        in_specs=list(in_specs) + [_ANY] * len(r_in),
        out_specs=list(out_specs) + [_ANY] * len(r_out),
        scratch_shapes=list(scratch_shapes) + r_sem,
    )
    if prefetch is not None:
        specs = dict(grid_spec=pltpu.PrefetchScalarGridSpec(num_scalar_prefetch=1, **specs))
    res = pl.pallas_call(
        full_body,
        name=name,
        out_shape=list(out_shape) + r_out,
        input_output_aliases=io_aliases,
        compiler_params=_params(("arbitrary",) * len(grid)),
        **specs,
    )(*lead, *args, *r_in)
    rider_outs, po = [], n_out
    for r in riders:
        rider_outs.append(list(res[po:po + len(r.out_shapes)]))
        po += len(r.out_shapes)
    return list(res[:n_out]), rider_outs


def _run_riders(name, riders):
    r_in = [a for r in riders for a in r.ins]
    r_out = [s for r in riders for s in r.out_shapes]
    r_sem = [s for r in riders for s in r.sem_shapes]

    def body(*refs):
        rin, rout, rsem = refs[:len(r_in)], refs[len(r_in):len(r_in) + len(r_out)], refs[len(r_in) + len(r_out):]
        pi = po = ps = 0
        all_waits = []
        for r in riders:
            starts, waits = r.build(rin[pi:pi + len(r.ins)], rout[po:po + len(r.out_shapes)], rsem[ps:ps + len(r.sem_shapes)])
            for cp in starts:
                cp.start()
            all_waits += waits
            pi, po, ps = pi + len(r.ins), po + len(r.out_shapes), ps + len(r.sem_shapes)
        for wait in all_waits:
            wait()

    io_aliases = {}
    pi = po = 0
    for r in riders:
        if r.aliased:
            for q in range(len(r.ins)):
                io_aliases[pi + q] = po + q
        pi, po = pi + len(r.ins), po + len(r.out_shapes)
    res = pl.pallas_call(
        body,
        name=name,
        in_specs=[_ANY] * len(r_in),
        out_specs=[_ANY] * len(r_out),
        out_shape=r_out,
        scratch_shapes=r_sem,
        input_output_aliases=io_aliases,
    )(*r_in)
    outs, po = [], 0
    for r in riders:
        outs.append(list(res[po:po + len(r.out_shapes)]))
        po += len(r.out_shapes)
    return outs


def _piece(name):
    if name in BIG_PIECES:
        return (name, *BIG_PIECES[name], True)
    return (name, *SMALL_PIECES[name], False)


def _gather_rider(shards, names):
    pieces = [_piece(n) for n in names]
    n = len(pieces)

    def build(ins, outs, sems):
        local_sem, ici_send, ici_recv = sems
        _, _, c, s = _place()
        starts, waits = [], []
        for p, (_, kind, shard, split) in enumerate(pieces):
            cp = pltpu.make_async_copy(ins[p], _full_region(outs[p], kind, shard, s, None), local_sem.at[p])
            starts.append(cp)
            waits.append(cp.wait)
            h = c if split else None
            for k in (1, 2, 3):
                s2, dev = _other_chip(s, c, k)
                cp = _remote(_shard_region(ins[p], shard, h), _full_region(outs[p], kind, shard, s, h),
                             ici_send.at[p, k - 1], ici_recv.at[p, k - 1], dev)
                starts.append(cp)
                waits.append(cp.wait_send)
                region = _full_region(outs[p], kind, shard, s2, h)
                waits.append(_remote(region, region, ici_send.at[p, k - 1], ici_recv.at[p, k - 1], dev).wait_recv)
        return starts, waits

    return _Rider(
        [shards[name] for name in names],
        [jax.ShapeDtypeStruct(_full_shape(kind, shard), shards[name].dtype) for name, kind, shard, _ in pieces],
        [pltpu.SemaphoreType.DMA((n,)), pltpu.SemaphoreType.DMA((n, 3)), pltpu.SemaphoreType.DMA((n, 3))],
        build)


def _forward_rider(gathered, names):
    pieces = [_piece(n) for n in names]
    n = len(pieces)

    def build(ins, outs, sems):
        fwd_send, fwd_recv = sems
        x, y, c, s = _place()
        sibling = (x, y, 1 - c)
        starts, waits = [], []
        for p, (_, kind, shard, _) in enumerate(pieces):
            for k in (1, 2, 3):
                s2, _ = _other_chip(s, c, k)
                mine = _full_region(outs[p], kind, shard, s2, c)
                theirs = _full_region(outs[p], kind, shard, s2, 1 - c)
                cp = _remote(mine, mine, fwd_send.at[p, k - 1], fwd_recv.at[p, k - 1], sibling)
                starts.append(cp)
                waits.append(cp.wait_send)
                waits.append(_remote(theirs, theirs, fwd_send.at[p, k - 1], fwd_recv.at[p, k - 1], sibling).wait_recv)
        return starts, waits

    return _Rider(
        [gathered[name] for name in names],
        [jax.ShapeDtypeStruct(gathered[name].shape, gathered[name].dtype) for name in names],
        [pltpu.SemaphoreType.DMA((n, 3)), pltpu.SemaphoreType.DMA((n, 3))],
        build, aliased=True)


def _pair_exchange_rider(grads, names):
    n = len(names)

    def build(ins, outs, sems):
        send_sem, recv_sem = sems
        x, y, c, _ = _place()
        sibling = (x, y, 1 - c)
        starts, waits = [], []
        for p, name in enumerate(names):
            kind, shard = BIG_PIECES[name]
            for s2 in range(4):
                cp = _remote(_full_region(ins[p], kind, shard, s2, 1 - c), outs[p].at[s2], send_sem.at[p, s2],
                             recv_sem.at[p, s2], sibling)
                starts.append(cp)
                waits.append(cp.wait_send)
                waits.append(_remote(outs[p].at[s2], outs[p].at[s2], send_sem.at[p, s2], recv_sem.at[p, s2], sibling).wait_recv)
        return starts, waits

    return _Rider(
        [grads[name] for name in names],
        [jax.ShapeDtypeStruct((4,) + _half_shape(*BIG_PIECES[name]), F32) for name in names],
        [pltpu.SemaphoreType.DMA((n, 4))] * 2,
        build)


def _half_specs(kind, shard):
    half = shard[0] // 2
    if kind == "col":
        full = pl.BlockSpec((half, shard[1]), lambda j, pr: (pr[1], j))
        buf = pl.BlockSpec((None, half, shard[1]), lambda j, pr: (j, 0, 0))
    elif kind == "row":
        full = pl.BlockSpec((half, shard[1]), lambda j, pr: (2 * j + pr[1], 0))
        buf = pl.BlockSpec((None, half, shard[1]), lambda j, pr: (j, 0, 0))
    else:
        full = pl.BlockSpec((half, shard[1], shard[2]), lambda j, pr: (pr[1], j, 0))
        buf = pl.BlockSpec((None, half, shard[1], shard[2]), lambda j, pr: (j, 0, 0, 0))
    return full, buf


def _pair_sum(name, grad, recv, place):
    kind, shard = BIG_PIECES[name]
    full, buf = _half_specs(kind, shard)

    def body(pr, g_ref, r_ref, o_ref):
        o_ref[...] = (g_ref[...] + r_ref[...]).astype(BF16)

    return pl.pallas_call(
        body,
        name="pair_sum_" + name,
        grid_spec=pltpu.PrefetchScalarGridSpec(num_scalar_prefetch=1, grid=(4,), in_specs=[full, buf], out_specs=buf),
        out_shape=jax.ShapeDtypeStruct((4,) + _half_shape(kind, shard), BF16),
        compiler_params=_params(("arbitrary",)),
    )(place, grad, recv)


def _chip_exchange_rider(sums, names):
    n = len(names)

    def build(ins, outs, sems):
        send_sem, recv_sem = sems
        _, _, c, s = _place()
        starts, waits = [], []
        for p in range(n):
            for k in (1, 2, 3):
                s2, dev = _other_chip(s, c, k)
                cp = _remote(ins[p].at[s2], outs[p].at[k - 1], send_sem.at[p, k - 1], recv_sem.at[p, k - 1], dev)
                starts.append(cp)
                waits.append(cp.wait_send)
                waits.append(_remote(outs[p].at[k - 1], outs[p].at[k - 1], send_sem.at[p, k - 1], recv_sem.at[p, k - 1],
                                     dev).wait_recv)
        return starts, waits

    return _Rider(
        [sums[name] for name in names],
        [jax.ShapeDtypeStruct((3,) + _half_shape(*BIG_PIECES[name]), BF16) for name in names],
        [pltpu.SemaphoreType.DMA((n, 3))] * 2,
        build)


def _chip_sum(name, grad, recv_pair, recv_chip, place):
    kind, shard = BIG_PIECES[name]
    half = shard[0] // 2
    tail = tuple(shard[1:])
    zeros = (0,) * len(tail)
    nt = 4 if kind != "lru" and half % (4 * BF16_ROWS) == 0 else 1
    rows = half // nt
    if kind == "col":
        full = pl.BlockSpec((rows,) + tail, lambda j, pr: (pr[1] * nt + j, pr[0]))
    elif kind == "row":
        full = pl.BlockSpec((rows,) + tail, lambda j, pr: ((2 * pr[0] + pr[1]) * nt + j, 0))
    else:
        full = pl.BlockSpec((rows,) + tail, lambda j, pr: (pr[1], pr[0], 0))
    pair = pl.BlockSpec((None, rows) + tail, lambda j, pr: (pr[0], j) + zeros)
    chip = pl.BlockSpec((3, rows) + tail, lambda j, pr: (0, j) + zeros)
    out = pl.BlockSpec((rows,) + tail, lambda j, pr: (pr[1] * nt + j,) + zeros)

    def body(pr, g_ref, rp_ref, rc_ref, o_ref):
        total = g_ref[...] + rp_ref[...]
        for k in range(3):
            total = total + rc_ref[k].astype(F32)
        o_ref[...] = total

    return pl.pallas_call(
        body,
        name="chip_sum_" + name,
        grid_spec=pltpu.PrefetchScalarGridSpec(num_scalar_prefetch=1, grid=(nt,), in_specs=[full, pair, chip], out_specs=out),
        out_shape=jax.ShapeDtypeStruct(shard, F32),
        compiler_params=_params(("arbitrary",)),
    )(place, grad, recv_pair, recv_chip)


def _sibling_exchange_rider(halves, names):
    n = len(names)

    def build(ins, outs, sems):
        send_sem, recv_sem = sems
        x, y, c, _ = _place()
        sibling = (x, y, 1 - c)
        starts, waits = [], []
        for p, name in enumerate(names):
            shard = BIG_PIECES[name][1]
            mine = _shard_region(outs[p], shard, c)
            theirs = _shard_region(outs[p], shard, 1 - c)
            cp = _remote(mine, mine, send_sem.at[p], recv_sem.at[p], sibling)
            starts.append(cp)
            waits.append(cp.wait_send)
            waits.append(_remote(theirs, theirs, send_sem.at[p], recv_sem.at[p], sibling).wait_recv)
        return starts, waits

    return _Rider(
        [halves[name] for name in names],
        [jax.ShapeDtypeStruct(BIG_PIECES[name][1], F32) for name in names],
        [pltpu.SemaphoreType.DMA((n,))] * 2,
        build, aliased=True)


FIRST_WEIGHTS = ["meta_tokens", "conv_w"]
WEIGHT_GROUPS = {
    "lru": ["lru_wa", "lru_wx"],
    "branch": ["w_branch_ret", "w_branch_lru", "w_out"],
    "ffn_in": ["w_ffn_in"],
    "ffn_out": ["w_ffn_out"],
}
WEIGHT_SCHEDULE = {
    "in_proj": [("gather", "lru")],
    "retention_fwd": [("forward", "lru"), ("gather", "branch")],
    "lru_fwd": [("forward", "branch"), ("gather", "ffn_in")],
    "mix_fwd": [("forward", "ffn_in"), ("gather", "ffn_out")],
    "ffn_fwd_gate": [("forward", "ffn_out")],
}
GRAD_GROUPS = {
    "ffn_in": ["w_ffn_in"],
    "ffn_out": ["w_ffn_out"],
    "mixer": ["w_out", "w_branch_ret", "w_branch_lru", "lru_wa", "lru_wx"],
    "in": ["w_in"],
}
GRAD_SCHEDULE = {
    "mix_bwd": [("pair", "ffn_in"), ("pair", "ffn_out")],
    "retention_bwd": [("chip", "ffn_in")],
    "lru_bwd": [("chip", "ffn_out"), ("sibling", "ffn_in")],
    "dw_in": [("sibling", "ffn_out"), ("pair", "mixer")],
    "in_proj_bwd_0": [("chip", "mixer"), ("pair", "in")],
    "in_proj_bwd_1": [("sibling", "mixer"), ("chip", "in")],
}


class _CommPlan:
    def __init__(self, shards, place):
        self.shards, self.place = shards, place
        self.late = {}
        self.recv_pair, self.sums, self.recv_chip, self.halves, self.final = {}, {}, {}, {}, {}

    def _grad_rider(self, stage, group, grads):
        names = GRAD_GROUPS[group]
        if stage == "pair":
            return _pair_exchange_rider(grads, names)
        if stage == "chip":
            return _chip_exchange_rider(self.sums, names)
        return _sibling_exchange_rider(self.halves, names)

    def _grad_after(self, stage, group, outs, grads):
        names = GRAD_GROUPS[group]
        if stage == "pair":
            for n, o in zip(names, outs):
                self.recv_pair[n] = o
                self.sums[n] = _pair_sum(n, grads[n], o, self.place)
        elif stage == "chip":
            for n, o in zip(names, outs):
                self.halves[n] = _chip_sum(n, grads[n], self.recv_pair[n], o, self.place)
        else:
            self.final.update(zip(names, outs))

    def riders(self, host, w, grads):
        if host in WEIGHT_SCHEDULE:
            return [_gather_rider(self.shards, WEIGHT_GROUPS[group]) if stage == "gather"
                    else _forward_rider(self.late, WEIGHT_GROUPS[group]) for stage, group in WEIGHT_SCHEDULE[host]]
        return [self._grad_rider(stage, group, grads) for stage, group in GRAD_SCHEDULE.get(host, [])]

    def after(self, host, rider_outs, w, grads):
        for (stage, group), outs in zip(WEIGHT_SCHEDULE.get(host, []), rider_outs):
            (self.late if stage == "gather" else w).update(zip(WEIGHT_GROUPS[group], outs))
        for (stage, group), outs in zip(GRAD_SCHEDULE.get(host, []), rider_outs):
            self._grad_after(stage, group, outs, grads)

    def finish(self, partial):
        outs, (blocks,) = _run_riders("tail_exchange", [_sibling_exchange_rider(self.halves, GRAD_GROUPS["in"]),
                                                        _small_exchange_rider(partial)])
        self.final.update(zip(GRAD_GROUPS["in"], outs))
        return self.final, blocks


def _adamw_math(w, g, m, v):
    m = ADAM_B1 * m + (1.0 - ADAM_B1) * g
    v = ADAM_B2 * v + (1.0 - ADAM_B2) * (g * g)
    m_hat = m / (1.0 - ADAM_B1 ** ADAM_STEP)
    v_hat = v / (1.0 - ADAM_B2 ** ADAM_STEP)
    delta = -ADAM_LR * (m_hat / (jnp.sqrt(v_hat) + ADAM_EPS) + ADAM_WD * w)
    return delta, m, v


def _adamw(name, g, w, m, v):
    rows, cols = g.shape
    tr = rows // 4 if rows % 32 == 0 else rows

    def body(g_ref, w_ref, m_ref, v_ref, go_ref, d_ref, mo_ref, vo_ref):
        gv = g_ref[...]
        delta, m2, v2 = _adamw_math(w_ref[...], gv, m_ref[...], v_ref[...])
        go_ref[...] = gv
        d_ref[...] = delta
        mo_ref[...] = m2
        vo_ref[...] = v2

    spec = pl.BlockSpec((tr, cols), lambda i: (i, 0))
    return pl.pallas_call(
        body,
        name="adamw_" + name,
        grid=(rows // tr,),
        in_specs=[spec] * 4,
        out_specs=[spec] * 4,
        out_shape=[jax.ShapeDtypeStruct((rows, cols), F32)] * 4,
        compiler_params=_params(("arbitrary",)),
    )(g, w, m, v)


SMALL_ROWS = 48
VEC_ROWS = {"final_norm_w": 1, "ffn_norm_w": 8, "mix_norm_w": 16, "conv_b": 24, "lru_ba": 25, "lru_bx": 26, "lru_lambda": 27}
VEC_NAMES = list(VEC_ROWS)
CONV_W_ROW = 28
META_ROW = 32


def _small_exchange_rider(partial):
    def build(ins, outs, sems):
        local_sem, send_sem, recv_sem = sems
        _, _, c, s = _place()
        me = 2 * s + c
        cp = pltpu.make_async_copy(ins[0], outs[0].at[me], local_sem)
        starts, waits = [cp], [cp.wait]
        for k in range(1, 8):
            peer = jnp.bitwise_xor(me, k)
            dev = (peer // 4, (peer // 2) % 2, peer % 2)
            rd = _remote(ins[0], outs[0].at[me], send_sem.at[k - 1], recv_sem.at[k - 1], dev)
            starts.append(rd)
            waits.append(rd.wait_send)
            waits.append(_remote(ins[0], outs[0].at[peer], send_sem.at[k - 1], recv_sem.at[k - 1], dev).wait_recv)
        return starts, waits

    return _Rider([partial], [jax.ShapeDtypeStruct((8, SMALL_ROWS, D), F32)],
                  [pltpu.SemaphoreType.DMA, pltpu.SemaphoreType.DMA((7,)), pltpu.SemaphoreType.DMA((7,))], build)


def _small_update(blocks, place, vecs, conv, meta):
    nvec = len(VEC_NAMES)
    qcols = D // 4

    def in_order(ref):
        total = ref[0]
        for d in range(1, 8):
            total = total + ref[d]
        return total

    def body(pr, blocks_ref, cols_ref, *refs):
        vec_refs = refs[:3 * nvec]
        conv_refs = refs[3 * nvec:3 * nvec + 3]
        meta_refs = refs[3 * nvec + 3:3 * nvec + 6]
        outs = refs[3 * nvec + 6:]
        loss_ref, vec_out, conv_out, meta_out = outs[0], outs[1:5], outs[5:9], outs[9:13]
        tot, col = in_order(blocks_ref), in_order(cols_ref)
        loss_ref[...] = jnp.sum(tot[0:1, :], axis=1, keepdims=True)
        for o in vec_out:
            o[...] = jnp.zeros_like(o)
        for j, name in enumerate(VEC_NAMES):
            w, m, v = (r[...] for r in vec_refs[3 * j:3 * j + 3])
            g = tot[VEC_ROWS[name]:VEC_ROWS[name] + 1, :]
            if name == "lru_lambda":
                g = -g / (1.0 + jnp.exp(w))
            for o, val in zip(vec_out, (g,) + _adamw_math(w, g, m, v)):
                o[j:j + 1, :] = val
        for row, n_rows, ins, group in ((CONV_W_ROW, 4, conv_refs, conv_out), (META_ROW, N_META, meta_refs, meta_out)):
            g = col[row:row + n_rows, :]
            w, m, v = (r[...] for r in ins)
            for o, val in zip(group, (g,) + _adamw_math(w, g, m, v)):
                o[...] = val

    whole = lambda shape: pl.BlockSpec(shape, lambda i, pr: (0,) * len(shape))
    in_specs = [whole((8, SMALL_ROWS, D)), pl.BlockSpec((8, SMALL_ROWS, qcols), lambda i, pr: (0, 0, pr[0]))]
    in_specs += [whole((1, D))] * (3 * nvec) + [whole((4, qcols))] * 3 + [whole((N_META, qcols))] * 3
    out_shapes = [(1, 1)] + [(8, D)] * 4 + [(4, qcols)] * 4 + [(N_META, qcols)] * 4
    return pl.pallas_call(
        body,
        name="small_update",
        grid_spec=pltpu.PrefetchScalarGridSpec(num_scalar_prefetch=1, grid=(1,), in_specs=in_specs,
                                               out_specs=[whole(s) for s in out_shapes]),
        out_shape=[jax.ShapeDtypeStruct(s, F32) for s in out_shapes],
        compiler_params=_params(("arbitrary",)),
    )(place, blocks, blocks, *[a for t in vecs for a in t], *conv, *meta)


WEIGHT_ORDER = ["meta_tokens", "mix_norm_w", "w_in", "conv_w", "conv_b", "lru_wa", "lru_ba", "lru_wx", "lru_bx", "lru_lambda",
                "w_branch_ret", "w_branch_lru", "w_out", "ffn_norm_w", "w_ffn_in", "w_ffn_out", "final_norm_w"]


def kernel(x, meta_tokens, mix_norm_w, w_in, conv_w, conv_b, lru_wa, lru_ba, lru_wx, lru_bx, lru_lambda, w_branch_ret, w_branch_lru, w_out, ffn_norm_w, w_ffn_in, w_ffn_out, final_norm_w, loss_target, m_meta_tokens, m_mix_norm_w, m_w_in, m_conv_w, m_conv_b, m_lru_wa, m_lru_ba, m_lru_wx, m_lru_bx, m_lru_lambda, m_w_branch_ret, m_w_branch_lru, m_w_out, m_ffn_norm_w, m_w_ffn_in, m_w_ffn_out, m_final_norm_w, v_meta_tokens, v_mix_norm_w, v_w_in, v_conv_w, v_conv_b, v_lru_wa, v_lru_ba, v_lru_wx, v_lru_bx, v_lru_lambda, v_w_branch_ret, v_w_branch_lru, v_w_out, v_ffn_norm_w, v_w_ffn_in, v_w_ffn_out, v_final_norm_w):
    args = locals()
    wts = {n: args[n] for n in WEIGHT_ORDER}
    mom = {n: args["m_" + n] for n in WEIGHT_ORDER}
    var = {n: args["v_" + n] for n in WEIGHT_ORDER}
    place = jnp.stack([2 * lax.axis_index("x") + lax.axis_index("y"), lax.axis_index("c")]).astype(jnp.int32)

    shards = {n: wts[n][0].astype(BF16) for n in BIG_PIECES}
    shards["meta_tokens"] = wts["meta_tokens"]
    shards["conv_w"] = wts["conv_w"][0]
    plan = _CommPlan(shards, place)
    (first,) = _run_riders("gather_first", [_gather_rider(shards, FIRST_WEIGHTS)])
    w = dict(zip(FIRST_WEIGHTS, first))
    for n in VEC_NAMES:
        w[n] = wts[n].reshape(1, D)
    chips = jnp.bitwise_xor(place[0], jnp.array((0,) + SHARD_ORDER, dtype=jnp.int32))
    w["route"] = jnp.concatenate([place, jnp.stack([2 * chips, 2 * chips + 1], axis=1).reshape(8)])
    w["w_in_shard"] = shards["w_in"]

    grad_x, grad_head, _, stats = _local_step(x[0], loss_target[0], w, plan)
    partial = jnp.concatenate(stats + [grad_head[PAD_ROWS:]], axis=0)
    shard_grads, blocks = plan.finish(partial)

    out = {}
    for n in BIG_PIECES:
        shape2d = (-1, wts[n].shape[-1])
        res = _adamw(n, *[a.reshape(shape2d) for a in (shard_grads[n], wts[n], mom[n], var[n])])
        out[n] = [r.reshape(wts[n].shape) for r in res]

    vecs = [tuple(a[n].reshape(1, D) for a in (wts, mom, var)) for n in VEC_NAMES]
    conv = tuple(a["conv_w"][0] for a in (wts, mom, var))
    meta = tuple(a["meta_tokens"] for a in (wts, mom, var))
    res = _small_update(blocks, place, vecs, conv, meta)
    loss = res[0].reshape(())
    for j, n in enumerate(VEC_NAMES):
        out[n] = [r[j].reshape(wts[n].shape) for r in res[1:5]]
    out["conv_w"] = [r.reshape(wts["conv_w"].shape) for r in res[5:9]]
    out["meta_tokens"] = list(res[9:13])

    return (loss, grad_x.reshape(x.shape)) + tuple(out[n][kind] for kind in range(4) for n in WEIGHT_ORDER)
```

```python
import functools
import math

import jax
import jax.numpy as jnp
from jax import lax
from jax.experimental import pallas as pl
from jax.experimental.pallas import tpu as pltpu

F32 = jnp.float32
BF16 = jnp.bfloat16

LANES = 128
BF16_ROWS = 16

D = 1024
HEADS = 8
DH = 128
CHUNK = 256
N_META = 16
FRONT = 256
PAD_ROWS = FRONT - N_META
LRU_BLOCKS = 4
LRU_BLOCK = 256
LRU_C = 8.0
FFN = 2816
FFN_HALF = FFN // 2
IN_COLS = 8 * D
ROPE_BASE = 10000.0
EPS = 1e-6
QK_SCALE = DH ** -0.5

ADAM_LR = 0.001
ADAM_B1 = 0.9
ADAM_B2 = 0.999
ADAM_EPS = 1e-08
ADAM_WD = 0.01
ADAM_STEP = 10

TM = 256
TM_HEAVY = 768
MXU_TILE = 256
FFN_SUB = 2 * MXU_TILE
_FFN_SUBS = [(a, min(a + FFN_SUB, FFN)) for a in range(0, FFN, FFN_SUB)]
TM_MIX_BWD = 384
TM_LRU = 384
TM_MIX_FWD = 384
TM_IN_PROJ = 1408
VMEM_LIMIT = 60 * 1024 * 1024
TN_VMEM_BUDGET = 40 * 1024 * 1024

NT_DIMS = (((1,), (1,)), ((), ()))
TN_DIMS = (((0,), (0,)), ((), ()))
MESH = pl.DeviceIdType.MESH


def _params(sem=None):
    if sem is None:
        return pltpu.CompilerParams(vmem_limit_bytes=VMEM_LIMIT)
    return pltpu.CompilerParams(dimension_semantics=sem, vmem_limit_bytes=VMEM_LIMIT)


def _dot(a, b):
    return jnp.dot(a, b, preferred_element_type=F32)


def _dot_nt(a, b):
    return lax.dot_general(a, b, NT_DIMS, preferred_element_type=F32)


def _dot_tn(a, b):
    return lax.dot_general(a, b, TN_DIMS, preferred_element_type=F32)


def _sigmoid(z):
    return 1.0 / (1.0 + jnp.exp(-z))


def _log1p(x):
    return jnp.where(x < 1e-3, x * (1.0 - x * (0.5 - x * (1.0 / 3.0))), jnp.log(1.0 + x))


def _softplus(x):
    return jnp.maximum(x, 0.0) + _log1p(jnp.exp(-jnp.abs(x)))


def _one_minus_square(a, log_a):
    x = 2.0 * log_a
    series = -x * (1.0 + x * (0.5 + x * (1.0 / 6.0)))
    return jnp.where(x > -0.02, series, 1.0 - a * a)


_GELU_K = math.sqrt(2.0 / math.pi)


def _gelu_and_grad(x):
    inner = _GELU_K * (x + 0.044715 * x * x * x)
    t = jnp.tanh(inner)
    val = 0.5 * x * (1.0 + t)
    grad = 0.5 * (1.0 + t) + 0.5 * x * (1.0 - t * t) * _GELU_K * (1.0 + 3.0 * 0.044715 * x * x)
    return val, grad


def _row_ids(i, rows, shape):
    return i * rows + lax.broadcasted_iota(jnp.int32, shape, 0)


def _rope_tables(rows):
    inv_freq = ROPE_BASE ** (-jnp.arange(0, DH, 2, dtype=F32) / DH)
    block_pos = jnp.arange(rows // FRONT, dtype=jnp.int32) * FRONT - PAD_ROWS
    coarse = block_pos.astype(F32)[:, None] * inv_freq[None, :]
    fine = jnp.arange(FRONT, dtype=F32)[:, None] * inv_freq[None, :]
    ca, sa = jnp.cos(coarse)[:, None, :], jnp.sin(coarse)[:, None, :]
    cb, sb = jnp.cos(fine)[None, :, :], jnp.sin(fine)[None, :, :]
    cos = (ca * cb - sa * sb).reshape(rows, DH // 2)
    sin = (sa * cb + ca * sb).reshape(rows, DH // 2)
    return jnp.concatenate([cos, cos], axis=1), jnp.concatenate([-sin, sin], axis=1)


def _decay_tables():
    log_g = jnp.log(1.0 - 2.0 ** (-5.0 - jnp.arange(HEADS, dtype=F32)))
    idx = jnp.arange(CHUNK, dtype=F32)
    diff = idx[:, None] - idx[None, :]
    intra = jnp.where(diff[None] >= 0, jnp.exp(jnp.maximum(diff, 0.0)[None] * log_g[:, None, None]), 0.0)
    q_decay = jnp.exp((idx + 1.0)[:, None] * log_g[None, :])
    k_decay = jnp.exp((CHUNK - 1.0 - idx)[:, None] * log_g[None, :])
    chunk_decay = jnp.exp(CHUNK * log_g)
    wide = lambda a: jnp.repeat(a, DH, axis=-1)
    return intra, jnp.swapaxes(intra, 1, 2), wide(q_decay), wide(k_decay), wide(chunk_decay[None, :])


def _norm1(head, x2d, norm_w, riders=()):
    rows = FRONT + x2d.shape[0]
    tm = TM_IN_PROJ if rows % TM_IN_PROJ == 0 else TM_MIX_FWD
    nt = rows // tm

    def body(head_hbm, x_hbm, nw_ref, u_ref, rstd_ref, h0_sc, h0_sem):
        slot = _frame_rows(head_hbm, x_hbm, h0_sc, h0_sem, pl.program_id(0), tm, nt)
        hv = h0_sc[slot]
        rs = lax.rsqrt(jnp.mean(hv * hv, axis=-1, keepdims=True) + EPS)
        u_ref[...] = ((hv * rs) * nw_ref[...]).astype(BF16)
        rstd_ref[...] = rs

    return _hosted_call(
        body,
        name="norm1",
        grid=(nt,),
        in_specs=[_ANY, _ANY, pl.BlockSpec((1, D), lambda i: (0, 0))],
        out_specs=[pl.BlockSpec((tm, D), lambda i: (i, 0)), pl.BlockSpec((tm, 1), lambda i: (i, 0))],
        out_shape=[jax.ShapeDtypeStruct((rows, D), BF16), jax.ShapeDtypeStruct((rows, 1), F32)],
        scratch_shapes=[pltpu.VMEM((2, tm, D), F32), pltpu.SemaphoreType.DMA((3,))],
        args=(head, x2d, norm_w),
        riders=riders,
    )


def _heavy_tile(rows):
    return TM_HEAVY if rows % TM_HEAVY == 0 else TM


def _lru_tile(rows):
    return TM_LRU if rows % TM_LRU == 0 else TM


def _swap_3_4(group):
    return jnp.where(group == 3, 4, jnp.where(group == 4, 3, group))


SHARD_ORDER = (2, 3, 1)
LRU_IN_COL = 3
GATES_COL = 1


def _in_proj(u, w_shard, route, cos_t, sin_t, riders=()):
    rows = u.shape[0]
    tm = TM_IN_PROJ if rows % TM_IN_PROJ == 0 else _heavy_tile(rows)
    nt = rows // tm
    kind, shard = BIG_PIECES["w_in"]

    def body(route_ref, u_hbm, wsh_ref, cos_ref, sin_ref, proj_ref, wfull_ref,
             u_sc, w_sc, u_sem, w_sem, local_sem, ici_send, ici_recv, fwd_send, fwd_recv):
        g, i = pl.program_id(0), pl.program_id(1)
        s, c = route_ref[0], route_ref[1]
        gid = route_ref[2 + g]
        sibling = (s // 2, s % 2, 1 - c)
        local = pltpu.make_async_copy(wsh_ref, _full_region(wfull_ref, kind, shard, s, None), local_sem)
        u_copies = [pltpu.make_async_copy(u_hbm.at[pl.ds(t * tm, tm)], u_sc.at[t], u_sem.at[t]) for t in range(nt)]
        sends, arrivals = [], []
        for k in (1, 2, 3):
            s2, dev = _other_chip(s, c, k)
            sends.append(_remote(_shard_region(wsh_ref, shard, c), _full_region(wfull_ref, kind, shard, s, c),
                                 ici_send.at[k - 1], ici_recv.at[k - 1], dev))
            mine = _full_region(wfull_ref, kind, shard, s2, c)
            theirs = _full_region(wfull_ref, kind, shard, s2, 1 - c)
            arrivals.append((_remote(mine, mine, ici_send.at[k - 1], ici_recv.at[k - 1], dev),
                             _remote(mine, mine, fwd_send.at[k - 1], fwd_recv.at[k - 1], sibling),
                             _remote(theirs, theirs, fwd_send.at[k - 1], fwd_recv.at[k - 1], sibling)))

        slot = g % 2
        last_tile = i == nt - 1

        def block_copy(src, col, to_slot):
            return pltpu.make_async_copy(src.at[:, pl.ds(pl.multiple_of(col * D, D), D)], w_sc.at[to_slot],
                                         w_sem.at[to_slot])

        @pl.when(jnp.logical_and(g == 0, i == 0))
        def _():
            for k in SHARD_ORDER:
                sends[k - 1].start()
            local.start()
            for cp in u_copies:
                cp.start()
            cp = block_copy(wsh_ref, 0, 0)
            cp.start()
            cp.wait()

        @pl.when(jnp.logical_and(last_tile, g == 0))
        def _():
            block_copy(wsh_ref, 1, 1).start()

        for pos, k in enumerate(SHARD_ORDER, start=1):
            arrived, forward, forwarded = arrivals[k - 1]

            @pl.when(jnp.logical_and(last_tile, g == 2 * pos - 1))
            def _():
                arrived.wait_recv()
                forward.start()
                forwarded.wait_recv()
                block_copy(wfull_ref, route_ref[2 + 2 * pos], 0).start()

            @pl.when(jnp.logical_and(last_tile, g == 2 * pos))
            def _():
                block_copy(wfull_ref, route_ref[3 + 2 * pos], 1).start()

        @pl.when(jnp.logical_and(i == 0, g > 0))
        def _():
            block_copy(wfull_ref, 0, slot).wait()

        for t in range(nt):
            @pl.when(jnp.logical_and(g == 0, i == t))
            def _():
                u_copies[t].wait()

        acc = _dot(u_sc[i], w_sc[slot])

        @pl.when(gid < 2)
        def _():
            scale = jnp.where(gid == 1, QK_SCALE, 1.0).astype(F32)
            for h in range(HEADS):
                sl = slice(h * DH, (h + 1) * DH)
                blk = acc[:, sl]
                out = (blk * cos_ref[...] + pltpu.roll(blk, DH // 2, axis=1) * sin_ref[...]) * scale
                proj_ref[:, sl] = out.astype(BF16)

        @pl.when(gid >= 2)
        def _():
            proj_ref[...] = acc.astype(BF16)

        @pl.when(jnp.logical_and(g == 7, i == nt - 1))
        def _():
            local.wait()
            for cp in sends:
                cp.wait_send()
            for _, forward, _ in arrivals:
                forward.wait_send()

    return _hosted_call(
        body,
        name="in_proj",
        grid=(8, nt),
        in_specs=[
            _ANY, _ANY,
            pl.BlockSpec((tm, DH), lambda g, i, rt: (i, 0)),
            pl.BlockSpec((tm, DH), lambda g, i, rt: (i, 0)),
        ],
        out_specs=[pl.BlockSpec((tm, D), lambda g, i, rt: (i, _swap_3_4(rt[2 + g]))), _ANY],
        out_shape=[jax.ShapeDtypeStruct((rows, IN_COLS), BF16), jax.ShapeDtypeStruct((D, IN_COLS), BF16)],
        scratch_shapes=[
            pltpu.VMEM((nt, tm, D), BF16), pltpu.VMEM((2, D, D), BF16),
            pltpu.SemaphoreType.DMA((nt,)), pltpu.SemaphoreType.DMA((2,)), pltpu.SemaphoreType.DMA,
            pltpu.SemaphoreType.DMA((3,)), pltpu.SemaphoreType.DMA((3,)),
            pltpu.SemaphoreType.DMA((3,)), pltpu.SemaphoreType.DMA((3,)),
        ],
        args=(u, w_shard, cos_t, sin_t),
        riders=riders,
        prefetch=route,
        riders_after_body=True,
    )


def _retention_fwd(proj_bf, intra, q_dec, k_dec, c_dec, riders=()):
    rows = proj_bf.shape[0]
    nc = rows // CHUNK

    def body(q_ref, k_ref, v_ref, m_ref, qd_ref, kd_ref, cd_ref, o_ref, sprev_ref, s_sc):
        @pl.when(pl.program_id(0) == 0)
        def _():
            s_sc[...] = jnp.zeros_like(s_sc)

        for h in range(HEADS):
            sl = slice(h * DH, (h + 1) * DH)
            q, k, v = q_ref[:, sl], k_ref[:, sl], v_ref[:, sl]
            state = s_sc[h]
            state_b = state.astype(BF16)
            sprev_ref[0, h] = state_b
            s = _dot_nt(q, k) * m_ref[h]
            inner = _dot(s.astype(BF16), v)
            cross = _dot(q, state_b) * qd_ref[:, sl]
            o_ref[:, sl] = (inner + cross).astype(BF16)
            k_scaled = (k.astype(F32) * kd_ref[:, sl]).astype(BF16)
            s_sc[h] = state * cd_ref[:, sl] + _dot_tn(k_scaled, v)

    chunk_spec = lambda col: pl.BlockSpec((CHUNK, D), lambda c: (c, col))
    const2 = lambda shape: pl.BlockSpec(shape, lambda c: (0, 0))
    return _hosted_call(
        body,
        name="retention_fwd",
        grid=(nc,),
        in_specs=[
            chunk_spec(0), chunk_spec(1), chunk_spec(2),
            pl.BlockSpec((HEADS, CHUNK, CHUNK), lambda c: (0, 0, 0)),
            const2((CHUNK, D)), const2((CHUNK, D)), const2((1, D)),
        ],
        out_specs=[
            pl.BlockSpec((CHUNK, D), lambda c: (c, 0)),
            pl.BlockSpec((1, HEADS, DH, DH), lambda c: (c, 0, 0, 0)),
        ],
        out_shape=[
            jax.ShapeDtypeStruct((rows, D), BF16),
            jax.ShapeDtypeStruct((nc, HEADS, DH, DH), BF16),
        ],
        scratch_shapes=[pltpu.VMEM((HEADS, DH, DH), F32)],
        args=(proj_bf, proj_bf, proj_bf, intra, q_dec, k_dec, c_dec),
        riders=riders,
    )


def _shift_down(x, first8_prev, d):
    rolled = pltpu.roll(x, d, axis=0)
    head = pltpu.roll(jnp.concatenate([first8_prev, x[0:8]], axis=0), d, axis=0)[8:16]
    return rolled, head


def _conv_and_gates(x, prev8, cw_ref, cb_ref, wa_ref, wx_ref, ba_ref, bx_ref, lam_ref, c_sc):
    cw = cw_ref[...]
    conv = cb_ref[...] + cw[3:4] * x
    head = cb_ref[...] + cw[3:4] * x[0:8]
    for d in (1, 2, 3):
        rolled, hd = _shift_down(x, prev8, d)
        conv = conv + cw[3 - d:4 - d] * rolled
        head = head + cw[3 - d:4 - d] * hd
    c_sc[...] = conv
    c_sc[0:8, :] = head
    c = c_sc[...]
    zr, zi = [], []
    for g in range(LRU_BLOCKS):
        sl = slice(g * LRU_BLOCK, (g + 1) * LRU_BLOCK)
        cg = c[:, sl].astype(BF16)
        zr.append(_dot(cg, wa_ref[g]))
        zi.append(_dot(cg, wx_ref[g]))
    r = _sigmoid(jnp.concatenate(zr, axis=1) + ba_ref[...])
    gate_i = _sigmoid(jnp.concatenate(zi, axis=1) + bx_ref[...])
    sp = _softplus(-lam_ref[...])
    log_a = (-LRU_C) * r * sp
    a = jnp.exp(log_a)
    mult = jnp.sqrt(_one_minus_square(a, log_a))
    return c, r, gate_i, a, mult, log_a


def _lru_fwd(proj, conv_w, conv_b, wa, wx, ba, bx, lam, riders=()):
    rows = proj.shape[0]
    TM = _lru_tile(rows)
    nt = rows // TM

    def body(x_ref, cw_ref, cb_ref, wa_ref, wx_ref, ba_ref, bx_ref, lam_ref,
             h_ref, c_ref, r_ref, i_ref, la_ref, mult_ref, prev_sc, carry_sc, c_sc, a_sc, u_sc, h_sc):
        i = pl.program_id(0)

        @pl.when(i == 0)
        def _():
            prev_sc[...] = jnp.zeros_like(prev_sc)
            carry_sc[...] = jnp.zeros_like(carry_sc)

        x = x_ref[...].astype(F32)
        c, r, gate_i, a, mult, log_a = _conv_and_gates(x, prev_sc[...], cw_ref, cb_ref, wa_ref, wx_ref, ba_ref, bx_ref,
                                                       lam_ref, c_sc)
        for ref, val in ((c_ref, c), (r_ref, r), (i_ref, gate_i), (la_ref, log_a), (mult_ref, mult)):
            ref[...] = val.astype(BF16)
        prev_sc[...] = x[TM - 8:TM]
        valid = _row_ids(i, TM, (TM, D)) >= PAD_ROWS
        a_sc[...] = a
        u_sc[...] = jnp.where(valid, mult * gate_i * c, 0.0)
        row8 = lax.broadcasted_iota(jnp.int32, (8, D), 0)

        def group(gi, hprev):
            r0 = pl.multiple_of(gi * 8, 8)
            aa = a_sc[pl.ds(r0, 8), :]
            uu = u_sc[pl.ds(r0, 8), :]
            for d in (1, 2, 4):
                a_sh = jnp.where(row8 >= d, pltpu.roll(aa, d, axis=0), 1.0)
                u_sh = jnp.where(row8 >= d, pltpu.roll(uu, d, axis=0), 0.0)
                uu = uu + aa * u_sh
                aa = aa * a_sh
            hb = aa * hprev + uu
            h_sc[pl.ds(r0, 8), :] = hb
            return hb[7:8, :]

        hlast = lax.fori_loop(0, TM // 8, group, carry_sc[0:1, :])
        carry_sc[0:1, :] = hlast
        h_ref[...] = h_sc[...].astype(BF16)

    vec = pl.BlockSpec((1, D), lambda i: (0, 0))
    wspec = pl.BlockSpec((LRU_BLOCKS, LRU_BLOCK, LRU_BLOCK), lambda i: (0, 0, 0))
    return _hosted_call(
        body,
        name="lru_fwd",
        grid=(nt,),
        in_specs=[
            pl.BlockSpec((TM, D), lambda i: (i, LRU_IN_COL)),
            pl.BlockSpec((4, D), lambda i: (0, 0)),
            vec, wspec, wspec, vec, vec, vec,
        ],
        out_specs=[pl.BlockSpec((TM, D), lambda i: (i, 0)) for _ in range(6)],
        out_shape=[jax.ShapeDtypeStruct((rows, D), BF16) for _ in range(6)],
        scratch_shapes=[
            pltpu.VMEM((8, D), F32), pltpu.VMEM((8, D), F32),
            pltpu.VMEM((TM, D), F32), pltpu.VMEM((TM, D), F32), pltpu.VMEM((TM, D), F32), pltpu.VMEM((TM, D), F32),
        ],
        args=(proj, conv_w, conv_b, wa, wx, ba, bx, lam),
        riders=riders,
    )


def _group_norm(o):
    outs, rstds = [], []
    for h in range(HEADS):
        oh = o[:, h * DH:(h + 1) * DH]
        rs = lax.rsqrt(jnp.mean(oh * oh, axis=-1, keepdims=True) + EPS)
        outs.append(oh * rs)
        rstds.append(rs)
    return jnp.concatenate(outs, axis=1), rstds


def _frame_rows(head_hbm, x_hbm, buf, sems, i, tm, nt):
    slot = i % 2

    @pl.when(i == 0)
    def _():
        first = [pltpu.make_async_copy(head_hbm, buf.at[0, pl.ds(0, FRONT)], sems.at[2]),
                 pltpu.make_async_copy(x_hbm.at[pl.ds(0, tm - FRONT)], buf.at[0, pl.ds(FRONT, tm - FRONT)], sems.at[0])]
        for cp in first:
            cp.start()
        for cp in first:
            cp.wait()

    @pl.when(i + 1 < nt)
    def _():
        start = pl.multiple_of((i + 1) * tm - FRONT, LANES)
        pltpu.make_async_copy(x_hbm.at[pl.ds(start, tm)], buf.at[1 - slot], sems.at[1 - slot]).start()

    @pl.when(i > 0)
    def _():
        pltpu.make_async_copy(x_hbm.at[pl.ds(0, tm)], buf.at[slot], sems.at[slot]).wait()

    return slot


def _mix_fwd(head, x2d, o, proj, h_lru, w_br, w_bl, w_o, ffn_norm_w, riders=()):
    rows = o.shape[0]
    tm = TM_MIX_FWD
    assert rows % tm == 0 and tm > FRONT
    nt = rows // tm

    def body(head_hbm, x_hbm, o_ref, gates_ref, hl_ref, wbr_ref, wbl_ref, wo_ref, nw_ref,
             yret_ref, ylru_ref, h1_ref, u2_ref, rstd_ref, h0_sc, h0_sem):
        i = pl.program_id(0)
        slot = _frame_rows(head_hbm, x_hbm, h0_sc, h0_sem, i, tm, nt)
        gate = lambda j: gates_ref[:, j * D:(j + 1) * D].astype(F32)
        on, _ = _group_norm(o_ref[...].astype(F32))
        gret = gate(0)
        a_ret = (gret * _sigmoid(gret) * on).astype(BF16)
        y_ret = _dot(a_ret, wbr_ref[...])
        gl, _ = _gelu_and_grad(gate(1))
        a_lru = (gl * hl_ref[...].astype(F32)).astype(BF16)
        y_lru = _dot(a_lru, wbl_ref[...])
        mixed = (_sigmoid(gate(2)) * y_ret + _sigmoid(gate(3)) * y_lru).astype(BF16)
        delta = _dot(mixed, wo_ref[...])
        yret_ref[...] = y_ret.astype(BF16)
        ylru_ref[...] = y_lru.astype(BF16)
        h1 = h0_sc[slot] + delta
        rs = lax.rsqrt(jnp.mean(h1 * h1, axis=-1, keepdims=True) + EPS)
        h1_ref[...] = h1
        u2_ref[...] = ((h1 * rs) * nw_ref[...]).astype(BF16)
        rstd_ref[...] = rs

    tile = lambda col: pl.BlockSpec((tm, D), lambda i: (i, col))
    wspec = pl.BlockSpec((D, D), lambda i: (0, 0))
    return _hosted_call(
        body,
        name="mix_fwd",
        grid=(nt,),
        in_specs=[
            _ANY, _ANY,
            tile(0), pl.BlockSpec((tm, 4 * D), lambda i: (i, GATES_COL)), tile(0),
            wspec, wspec, wspec,
            pl.BlockSpec((1, D), lambda i: (0, 0)),
        ],
        out_specs=[tile(0), tile(0), tile(0), tile(0), pl.BlockSpec((tm, 1), lambda i: (i, 0))],
        out_shape=[
            jax.ShapeDtypeStruct((rows, D), BF16), jax.ShapeDtypeStruct((rows, D), BF16),
            jax.ShapeDtypeStruct((rows, D), F32), jax.ShapeDtypeStruct((rows, D), BF16),
            jax.ShapeDtypeStruct((rows, 1), F32),
        ],
        scratch_shapes=[pltpu.VMEM((2, tm, D), F32), pltpu.SemaphoreType.DMA((3,))],
        args=(head, x2d, o, proj, h_lru, w_br, w_bl, w_o, ffn_norm_w),
        riders=riders,
    )


def _ffn_fwd_gate(u2, w_ffn_in, riders=()):
    rows = u2.shape[0]
    tm = _heavy_tile(rows)
    hid = lambda: pl.BlockSpec((tm, FFN), lambda i: (i, 0))

    def body(u2_ref, w_hbm, silu_ref, dsilu_ref, act_ref, w_sc, w_sem):
        @pl.when(pl.program_id(0) == 0)
        def _():
            cp = pltpu.make_async_copy(w_hbm, w_sc, w_sem)
            cp.start()
            cp.wait()

        u2 = u2_ref[...]
        for a, b in _FFN_SUBS:
            g = _dot(u2, w_sc[:, a:b])
            up = _dot(u2, w_sc[:, FFN + a:FFN + b])
            sg = _sigmoid(g)
            silu = g * sg
            silu_ref[:, a:b] = silu.astype(BF16)
            dsilu_ref[:, a:b] = (up * (sg * (1.0 + g * (1.0 - sg)))).astype(BF16)
            act_ref[:, a:b] = (silu * up).astype(BF16)

    return _hosted_call(
        body,
        name="ffn_fwd_gate",
        grid=(rows // tm,),
        in_specs=[pl.BlockSpec((tm, D), lambda i: (i, 0)), _ANY],
        out_specs=[hid(), hid(), hid()],
        out_shape=[jax.ShapeDtypeStruct((rows, FFN), BF16)] * 3,
        scratch_shapes=[pltpu.VMEM((D, 2 * FFN), BF16), pltpu.SemaphoreType.DMA],
        args=(u2, w_ffn_in),
        riders=riders,
    )


def _ffn_out_loss(act, h1, w_ffn_out, target, final_norm_w):
    rows = act.shape[0]
    tm = _heavy_tile(rows)
    nt = rows // tm

    def body(act_ref, h1_ref, wo_ref, fnw_ref, tgt_hbm, dh2_ref, stats_ref, tgt_sc, tgt_sem):
        i = pl.program_id(0)
        slot = i % 2

        @pl.when(i == 0)
        def _():
            stats_ref[...] = jnp.zeros_like(stats_ref)
            tgt_sc[0, 0:FRONT, :] = jnp.zeros((FRONT, D), F32)
            cp = pltpu.make_async_copy(tgt_hbm.at[pl.ds(0, tm - FRONT)], tgt_sc.at[0, pl.ds(FRONT, tm - FRONT)], tgt_sem.at[0])
            cp.start()
            cp.wait()

        @pl.when(i + 1 < nt)
        def _():
            start = pl.multiple_of((i + 1) * tm - FRONT, FRONT)
            pltpu.make_async_copy(tgt_hbm.at[pl.ds(start, tm)], tgt_sc.at[1 - slot], tgt_sem.at[1 - slot]).start()

        @pl.when(i > 0)
        def _():
            pltpu.make_async_copy(tgt_hbm.at[pl.ds(0, tm)], tgt_sc.at[slot], tgt_sem.at[slot]).wait()

        h2 = h1_ref[...] + _dot(act_ref[...], wo_ref[...])
        rs = lax.rsqrt(jnp.mean(h2 * h2, axis=-1, keepdims=True) + EPS)
        n = h2 * rs
        fnw = fnw_ref[...]
        valid = _row_ids(i, tm, (tm, D)) >= FRONT
        diff = jnp.where(valid, n * fnw - tgt_sc[slot], 0.0)
        dy = diff * (1.0 / D)
        stats_ref[0:1, :] += (0.5 / D) * jnp.sum(diff * diff, axis=0, keepdims=True)
        stats_ref[1:2, :] += jnp.sum(dy * n, axis=0, keepdims=True)
        dn = dy * fnw
        dh2_ref[...] = rs * (dn - n * jnp.mean(dn * n, axis=-1, keepdims=True))

    return pl.pallas_call(
        body,
        name="ffn_out_loss",
        grid=(nt,),
        in_specs=[
            pl.BlockSpec((tm, FFN), lambda i: (i, 0)),
            pl.BlockSpec((tm, D), lambda i: (i, 0)),
            pl.BlockSpec((FFN, D), lambda i: (0, 0)),
            pl.BlockSpec((1, D), lambda i: (0, 0)),
            _ANY,
        ],
        out_specs=[pl.BlockSpec((tm, D), lambda i: (i, 0)), pl.BlockSpec((8, D), lambda i: (0, 0))],
        out_shape=[jax.ShapeDtypeStruct((rows, D), F32), jax.ShapeDtypeStruct((8, D), F32)],
        scratch_shapes=[pltpu.VMEM((2, tm, D), F32), pltpu.SemaphoreType.DMA((2,))],
        compiler_params=_params(("arbitrary",)),
    )(act, h1, w_ffn_out, final_norm_w, target)


def _ffn_bwd(dh2, silu, dsilu, h1, rstd2, w_ffn_in, w_ffn_out, ffn_norm_w):
    rows = dh2.shape[0]
    tm = _heavy_tile(rows)
    blk = lambda: pl.BlockSpec((tm, FFN), lambda i: (i, 0))

    def gate_body(dh2_ref, silu_ref, dsilu_ref, wo_hbm, dg_ref, dup_ref, dh2b_ref, wo_sc, wo_sem):
        @pl.when(pl.program_id(0) == 0)
        def _():
            cp = pltpu.make_async_copy(wo_hbm, wo_sc, wo_sem)
            cp.start()
            cp.wait()

        dh2b = dh2_ref[...].astype(BF16)
        dh2b_ref[...] = dh2b
        for a, b in _FFN_SUBS:
            dact = _dot_nt(dh2b, wo_sc[a:b, :])
            dup_ref[:, a:b] = (dact * silu_ref[:, a:b].astype(F32)).astype(BF16)
            dg_ref[:, a:b] = (dact * dsilu_ref[:, a:b].astype(F32)).astype(BF16)

    dg, dup, dh2b = pl.pallas_call(
        gate_body,
        name="ffn_bwd_gate",
        grid=(rows // tm,),
        in_specs=[pl.BlockSpec((tm, D), lambda i: (i, 0)), blk(), blk(), _ANY],
        out_specs=[blk(), blk(), pl.BlockSpec((tm, D), lambda i: (i, 0))],
        out_shape=[jax.ShapeDtypeStruct((rows, FFN), BF16)] * 2 + [jax.ShapeDtypeStruct((rows, D), BF16)],
        scratch_shapes=[pltpu.VMEM((FFN, D), BF16), pltpu.SemaphoreType.DMA],
        compiler_params=_params(("arbitrary",)),
    )(dh2, silu, dsilu, w_ffn_out)

    def body(dg_ref, dup_ref, dh2_ref, h1_ref, rstd_ref, w_hbm, nw_ref, dh1_ref, stats_ref, w_sc, w_sem):
        @pl.when(pl.program_id(0) == 0)
        def _():
            stats_ref[...] = jnp.zeros_like(stats_ref)
            cp = pltpu.make_async_copy(w_hbm, w_sc, w_sem)
            cp.start()
            cp.wait()

        du = _dot_nt(dg_ref[...], w_sc[:, 0:FFN]) + _dot_nt(dup_ref[...], w_sc[:, FFN:2 * FFN])
        rs = rstd_ref[...]
        n = h1_ref[...] * rs
        stats_ref[0:1, :] += jnp.sum(du * n, axis=0, keepdims=True)
        dn = du * nw_ref[...]
        dh1_ref[...] = dh2_ref[...] + rs * (dn - n * jnp.mean(dn * n, axis=-1, keepdims=True))

    row = lambda width: pl.BlockSpec((tm, width), lambda i: (i, 0))
    dh1, stats = pl.pallas_call(
        body,
        name="ffn_bwd_in",
        grid=(rows // tm,),
        in_specs=[row(FFN), row(FFN), row(D), row(D), row(1), _ANY, pl.BlockSpec((1, D), lambda i: (0, 0))],
        out_specs=[row(D), pl.BlockSpec((8, D), lambda i: (0, 0))],
        out_shape=[jax.ShapeDtypeStruct((rows, D), F32), jax.ShapeDtypeStruct((8, D), F32)],
        scratch_shapes=[pltpu.VMEM((D, 2 * FFN), BF16), pltpu.SemaphoreType.DMA],
        compiler_params=_params(("arbitrary",)),
    )(dg, dup, dh2, h1, rstd2, w_ffn_in, ffn_norm_w)
    return dg, dup, dh2b, dh1, stats


def _mix_bwd(dh1, o, proj, h_lru, y_ret, y_lru, w_br, w_bl, w_o, riders=()):
    rows = dh1.shape[0]
    tm = TM_MIX_BWD
    nt = rows // tm

    def body(dh1_ref, o_ref, gates_ref, hl_ref, yret_ref, ylru_ref, wbr_hbm, wbl_hbm, wo_hbm,
             dproj_ref, do_ref, dhl_ref, mixed_ref, aret_ref, alru_ref, dyret_ref, dylru_ref, w_sc, w_sem):
        @pl.when(pl.program_id(0) == 0)
        def _():
            copies = [pltpu.make_async_copy(src, w_sc.at[q], w_sem.at[q]) for q, src in enumerate((wbr_hbm, wbl_hbm, wo_hbm))]
            for cp in copies:
                cp.start()
            for cp in copies:
                cp.wait()

        wbr_ref, wbl_ref, wo_ref = w_sc.at[0], w_sc.at[1], w_sc.at[2]
        gate = lambda j: gates_ref[:, j * D:(j + 1) * D].astype(F32)
        dmixed = _dot_nt(dh1_ref[...].astype(BF16), wo_ref[...])
        y_ret, y_lru = yret_ref[...].astype(F32), ylru_ref[...].astype(F32)
        sa, sb = _sigmoid(gate(2)), _sigmoid(gate(3))
        mixed_ref[...] = (sa * y_ret + sb * y_lru).astype(BF16)
        dga = dmixed * y_ret * sa * (1.0 - sa)
        dgb = dmixed * y_lru * sb * (1.0 - sb)
        dy_ret = (dmixed * sa).astype(BF16)
        dy_lru = (dmixed * sb).astype(BF16)
        dyret_ref[...] = dy_ret
        dylru_ref[...] = dy_lru
        da_ret = _dot_nt(dy_ret, wbr_ref[...])
        da_lru = _dot_nt(dy_lru, wbl_ref[...])

        gret = gate(0)
        sg = _sigmoid(gret)
        silu = gret * sg
        on, rstds = _group_norm(o_ref[...].astype(F32))
        aret_ref[...] = (silu * on).astype(BF16)
        dgret = da_ret * on * (sg * (1.0 + gret * (1.0 - sg)))
        don = da_ret * silu
        for h in range(HEADS):
            sl = slice(h * DH, (h + 1) * DH)
            onh, donh = on[:, sl], don[:, sl]
            do_ref[:, sl] = (rstds[h] * (donh - onh * jnp.mean(donh * onh, axis=-1, keepdims=True))).astype(BF16)

        gl, gl_grad = _gelu_and_grad(gate(1))
        hl = hl_ref[...].astype(F32)
        alru_ref[...] = (gl * hl).astype(BF16)
        dlgate = da_lru * hl * gl_grad
        dhl_ref[...] = (da_lru * gl).astype(BF16)

        for j, val in enumerate((dgret, dlgate, dga, dgb)):
            dproj_ref[:, j * D:(j + 1) * D] = val.astype(BF16)

    tile = lambda col: pl.BlockSpec((tm, D), lambda i: (i, col))
    gates = pl.BlockSpec((tm, 4 * D), lambda i: (i, GATES_COL))
    bf = lambda: jax.ShapeDtypeStruct((rows, D), BF16)
    return _hosted_call(
        body,
        name="mix_bwd",
        grid=(nt,),
        in_specs=[tile(0), tile(0), gates, tile(0), tile(0), tile(0), _ANY, _ANY, _ANY],
        out_specs=[pl.BlockSpec((tm, 4 * D), lambda i: (i, GATES_COL))] + [tile(0)] * 7,
        out_shape=[jax.ShapeDtypeStruct((rows, IN_COLS), BF16)] + [bf() for _ in range(7)],
        scratch_shapes=[pltpu.VMEM((3, D, D), BF16), pltpu.SemaphoreType.DMA((3,))],
        args=(dh1, o, proj, h_lru, y_ret, y_lru, w_br, w_bl, w_o),
        riders=riders,
    )


def _retention_bwd(dproj, proj_bf, do, sprev, intra, intra_t, q_dec, k_dec, c_dec, cos_t, sin_t, riders=()):
    rows = proj_bf.shape[0]
    nc = rows // CHUNK

    def body(dproj_in_ref, q_ref, k_ref, v_ref, do_ref, sprev_ref, m_ref, mt_ref, qd_ref, kd_ref, cd_ref, cos_ref, sin_ref,
             dproj_ref, ds_sc):
        @pl.when(pl.program_id(0) == 0)
        def _():
            ds_sc[...] = jnp.zeros_like(ds_sc)

        cos, sin = cos_ref[...], sin_ref[...]

        def unrotate(dy):
            return dy * cos - pltpu.roll(dy, DH // 2, axis=1) * sin

        for h in range(HEADS):
            sl = slice(h * DH, (h + 1) * DH)
            q, k, v = q_ref[:, sl], k_ref[:, sl], v_ref[:, sl]
            do = do_ref[:, sl]
            dob = do.astype(BF16)
            doq = (do * qd_ref[:, sl]).astype(BF16)
            state_prev = sprev_ref[0, h]
            dstate = ds_sc[h]
            dstate_b = dstate.astype(BF16)
            s_t = (_dot_nt(k, q) * mt_ref[h]).astype(BF16)
            ds_t = (_dot_nt(v, dob) * mt_ref[h]).astype(BF16)
            ds = (_dot_nt(dob, v) * m_ref[h]).astype(BF16)
            kd = kd_ref[:, sl]
            dq = _dot(ds, k) + _dot_nt(doq, state_prev)
            dk = _dot(ds_t, q) + _dot_nt(v, dstate_b) * kd
            k_scaled = (k.astype(F32) * kd).astype(BF16)
            dv = _dot(s_t, dob) + _dot(k_scaled, dstate_b)
            ds_sc[h] = dstate * cd_ref[:, sl] + _dot_tn(q, doq)
            for part, val in enumerate((unrotate(dq), unrotate(dk) * QK_SCALE, dv)):
                dproj_ref[:, part * D + h * DH:part * D + (h + 1) * DH] = val.astype(BF16)

    rev = lambda c: nc - 1 - c
    chunk_spec = lambda col: pl.BlockSpec((CHUNK, D), lambda c: (rev(c), col))
    const2 = lambda shape: pl.BlockSpec(shape, lambda c: (0, 0))
    const3 = pl.BlockSpec((HEADS, CHUNK, CHUNK), lambda c: (0, 0, 0))
    return _hosted_call(
        body,
        name="retention_bwd",
        grid=(nc,),
        in_specs=[
            pl.BlockSpec(memory_space=pl.ANY),
            chunk_spec(0), chunk_spec(1), chunk_spec(2), chunk_spec(0),
            pl.BlockSpec((1, HEADS, DH, DH), lambda c: (rev(c), 0, 0, 0)),
            const3, const3,
            const2((CHUNK, D)), const2((CHUNK, D)), const2((1, D)),
            pl.BlockSpec((CHUNK, DH), lambda c: (rev(c), 0)),
            pl.BlockSpec((CHUNK, DH), lambda c: (rev(c), 0)),
        ],
        out_specs=[pl.BlockSpec((CHUNK, 3 * D), lambda c: (rev(c), 0))],
        out_shape=[jax.ShapeDtypeStruct(dproj.shape, BF16)],
        aliases={0: 0},
        scratch_shapes=[pltpu.VMEM((HEADS, DH, DH), F32)],
        args=(dproj, proj_bf, proj_bf, proj_bf, do, sprev, intra, intra_t, q_dec, k_dec, c_dec, cos_t, sin_t),
        riders=riders,
    )


def _lru_bwd(dproj, proj, saved, dhl, conv_w, wa, wx, lam, riders=()):
    rows = proj.shape[0]
    TM = _lru_tile(rows)
    nt = rows // TM
    per8 = TM // 8

    def body(dproj_in_ref, x_ref, xprev_ref, h_ref, hprev_ref, c_ref, r_ref, i_ref, la_ref, mult_ref, dhl_ref,
             cw_ref, wa_ref, wx_ref, lam_ref,
             dproj_ref, dwa_ref, dwx_ref, stats_ref, anext_sc, dhnext_sc, dcnext_sc, c_sc, b_sc, dh_sc):
        step = pl.program_id(0)
        i = nt - 1 - step

        @pl.when(step == 0)
        def _():
            anext_sc[...] = jnp.zeros_like(anext_sc)
            dhnext_sc[...] = jnp.zeros_like(dhnext_sc)
            dcnext_sc[...] = jnp.zeros_like(dcnext_sc)
            dwa_ref[...] = jnp.zeros_like(dwa_ref)
            dwx_ref[...] = jnp.zeros_like(dwx_ref)
            stats_ref[...] = jnp.zeros_like(stats_ref)

        first = i == 0
        x = x_ref[...].astype(F32)
        prev8 = jnp.where(first, 0.0, xprev_ref[8:16, :].astype(F32))
        c_b = c_ref[...]
        c, r, gate_i, mult = (ref[...].astype(F32) for ref in (c_ref, r_ref, i_ref, mult_ref))
        a = jnp.exp(la_ref[...].astype(F32))
        sp = _softplus(-lam_ref[...])
        dh_sc[...] = dhl_ref[...].astype(F32)

        b_sc[...] = pltpu.roll(a, TM - 1, axis=0)
        b_sc[TM - 1:TM, :] = anext_sc[0:1, :]
        anext_sc[0:1, :] = a[0:1, :]
        row8 = lax.broadcasted_iota(jnp.int32, (8, D), 0)

        def group(gi, dhnext):
            r0 = pl.multiple_of((per8 - 1 - gi) * 8, 8)
            bb = b_sc[pl.ds(r0, 8), :]
            uu = dh_sc[pl.ds(r0, 8), :]
            for d in (1, 2, 4):
                b_sh = jnp.where(row8 < 8 - d, pltpu.roll(bb, 8 - d, axis=0), 1.0)
                u_sh = jnp.where(row8 < 8 - d, pltpu.roll(uu, 8 - d, axis=0), 0.0)
                uu = uu + bb * u_sh
                bb = bb * b_sh
            dhb = bb * dhnext + uu
            dh_sc[pl.ds(r0, 8), :] = dhb
            return dhb[0:1, :]

        dhfirst = lax.fori_loop(0, per8, group, dhnext_sc[0:1, :])
        dhnext_sc[0:1, :] = dhfirst
        dh = dh_sc[...]

        h = h_ref[...].astype(F32)
        hprev8 = jnp.where(first, 0.0, hprev_ref[8:16, :].astype(F32))
        h_dn, h_head = _shift_down(h, hprev8, 1)
        c_sc[...] = h_dn
        c_sc[0:8, :] = h_head
        h_before = c_sc[...]

        valid = _row_ids(i, TM, (TM, D)) >= PAD_ROWS
        da = dh * h_before
        du = jnp.where(valid, dh, 0.0)
        dmult = du * gate_i * c
        dgate_i = du * mult * c
        dc = du * mult * gate_i
        dla = da * a - dmult * (a * a) / mult
        dla = jnp.where(valid, dla, 0.0)
        dr = dla * ((-LRU_C) * sp)
        dzr = dr * r * (1.0 - r)
        dzi = dgate_i * gate_i * (1.0 - gate_i)
        stats_ref[1:2, :] += jnp.sum(dzr, axis=0, keepdims=True)
        stats_ref[2:3, :] += jnp.sum(dzi, axis=0, keepdims=True)
        stats_ref[3:4, :] += jnp.sum(dla * ((-LRU_C) * r), axis=0, keepdims=True)
        dc_gate = []
        for g in range(LRU_BLOCKS):
            sl = slice(g * LRU_BLOCK, (g + 1) * LRU_BLOCK)
            cg = c_b[:, sl]
            dzr_g = dzr[:, sl].astype(BF16)
            dzi_g = dzi[:, sl].astype(BF16)
            dc_gate.append(_dot_nt(dzr_g, wa_ref[g]) + _dot_nt(dzi_g, wx_ref[g]))
            dwa_ref[g] += _dot_tn(cg, dzr_g)
            dwx_ref[g] += _dot_tn(cg, dzi_g)
        dc = dc + jnp.concatenate(dc_gate, axis=1)

        cw = cw_ref[...]
        stats_ref[0:1, :] += jnp.sum(dc, axis=0, keepdims=True)
        stats_ref[7:8, :] += jnp.sum(dc * x, axis=0, keepdims=True)
        dx = cw[3:4] * dc
        tail_src = jnp.concatenate([dc[TM - 8:TM], dcnext_sc[...]], axis=0)
        dx_tail = cw[3:4] * dc[TM - 8:TM]
        for d in (1, 2, 3):
            dx = dx + cw[3 - d:4 - d] * pltpu.roll(dc, TM - d, axis=0)
            dx_tail = dx_tail + cw[3 - d:4 - d] * pltpu.roll(tail_src, 16 - d, axis=0)[0:8]
            rolled, hd = _shift_down(x, prev8, d)
            b_sc[...] = rolled
            b_sc[0:8, :] = hd
            stats_ref[7 - d:8 - d, :] += jnp.sum(dc * b_sc[...], axis=0, keepdims=True)
        dcnext_sc[...] = dc[0:8]
        dproj_ref[...] = dx.astype(BF16)
        dproj_ref[TM - 8:TM, :] = dx_tail.astype(BF16)

    rev = lambda s: nt - 1 - s
    vec = pl.BlockSpec((1, D), lambda s: (0, 0))
    wspec = pl.BlockSpec((LRU_BLOCKS, LRU_BLOCK, LRU_BLOCK), lambda s: (0, 0, 0))
    prev16 = lambda col: pl.BlockSpec((BF16_ROWS, D), lambda s: (jnp.maximum(rev(s) * (TM // BF16_ROWS) - 1, 0), col))
    tile = lambda: pl.BlockSpec((TM, D), lambda s: (rev(s), 0))
    h_lru, c_sv, r_sv, i_sv, la_sv, mult_sv = saved
    return _hosted_call(
        body,
        name="lru_bwd",
        grid=(nt,),
        in_specs=[
            pl.BlockSpec(memory_space=pl.ANY),
            pl.BlockSpec((TM, D), lambda s: (rev(s), LRU_IN_COL)), prev16(LRU_IN_COL),
            tile(), prev16(0),
            tile(), tile(), tile(), tile(), tile(), tile(),
            pl.BlockSpec((4, D), lambda s: (0, 0)),
            wspec, wspec, vec,
        ],
        out_specs=[
            pl.BlockSpec((TM, D), lambda s: (rev(s), LRU_IN_COL)),
            wspec, wspec,
            pl.BlockSpec((8, D), lambda s: (0, 0)),
        ],
        out_shape=[
            jax.ShapeDtypeStruct(dproj.shape, BF16),
            jax.ShapeDtypeStruct((LRU_BLOCKS, LRU_BLOCK, LRU_BLOCK), F32),
            jax.ShapeDtypeStruct((LRU_BLOCKS, LRU_BLOCK, LRU_BLOCK), F32),
            jax.ShapeDtypeStruct((8, D), F32),
        ],
        aliases={0: 0},
        scratch_shapes=[
            pltpu.VMEM((8, D), F32), pltpu.VMEM((8, D), F32), pltpu.VMEM((8, D), F32),
            pltpu.VMEM((TM, D), F32), pltpu.VMEM((TM, D), F32), pltpu.VMEM((TM, D), F32),
        ],
        args=(dproj, proj, proj, h_lru, h_lru, c_sv, r_sv, i_sv, la_sv, mult_sv, dhl, conv_w, wa, wx, lam),
        riders=riders,
    )


def _in_proj_bwd(dproj, w_in, part, prev=None, riders=()):
    rows = dproj.shape[0]
    tm = _heavy_tile(rows)
    nt = rows // tm
    split = max(1, nt // 4 + 1) if nt > 1 else 1
    first = 0 if part == 0 else split
    count = split if part == 0 else nt - split

    def body(*refs):
        dproj_ref, w_hbm, du_ref, w_sc, w_sem = refs[-5:]

        @pl.when(pl.program_id(0) == 0)
        def _():
            moves = [((0, 3), 0), ((4, 1), 3), ((3, 1), 4), ((5, 3), 5)]
            copies = [pltpu.make_async_copy(w_hbm.at[:, pl.ds(src * D, n * D)], w_sc.at[:, pl.ds(dst * D, n * D)], w_sem.at[q])
                      for q, ((src, n), dst) in enumerate(moves)]
            for cp in copies:
                cp.start()
            for cp in copies:
                cp.wait()

        du_ref[...] = _dot_nt(dproj_ref[...], w_sc[...])

    in_specs = [pl.BlockSpec((tm, IN_COLS), lambda i: (first + i, 0)), _ANY]
    args = (dproj, w_in)
    if prev is not None:
        in_specs = [_ANY] + in_specs
        args = (prev,) + args
    return _hosted_call(
        body,
        name="in_proj_bwd_%d" % part,
        grid=(count,),
        in_specs=in_specs,
        out_specs=[pl.BlockSpec((tm, D), lambda i: (first + i, 0))],
        out_shape=[jax.ShapeDtypeStruct((rows, D), F32)],
        scratch_shapes=[pltpu.VMEM((D, IN_COLS), BF16), pltpu.SemaphoreType.DMA((4,))],
        aliases={0: 0} if prev is not None else None,
        args=args,
        riders=riders,
    )


def _norm1_bwd(du, dh1, head, x2d, rstd1, norm_w, riders=()):
    rows = du.shape[0]

    def body(du_ref, dh1_ref, head_ref, x_ref, rstd_ref, nw_ref, gx_ref, ghead_ref, stats_ref):
        i = pl.program_id(0)

        @pl.when(i == 0)
        def _():
            stats_ref[...] = jnp.zeros_like(stats_ref)

        def finish(h0, out_ref):
            du = du_ref[...]
            rs = rstd_ref[...]
            n = h0 * rs
            stats_ref[0:1, :] += jnp.sum(du * n, axis=0, keepdims=True)
            dn = du * nw_ref[...]
            out_ref[...] = dh1_ref[...] + rs * (dn - n * jnp.mean(dn * n, axis=-1, keepdims=True))

        @pl.when(i == 0)
        def _():
            finish(head_ref[...], ghead_ref)

        @pl.when(i > 0)
        def _():
            finish(x_ref[...], gx_ref)

    tile = pl.BlockSpec((TM, D), lambda i: (i, 0))
    return _hosted_call(
        body,
        name="norm1_bwd",
        grid=(rows // TM,),
        in_specs=[
            tile, tile,
            pl.BlockSpec((FRONT, D), lambda i: (0, 0)),
            pl.BlockSpec((TM, D), lambda i: (jnp.maximum(i - 1, 0), 0)),
            pl.BlockSpec((TM, 1), lambda i: (i, 0)),
            pl.BlockSpec((1, D), lambda i: (0, 0)),
        ],
        out_specs=[
            pl.BlockSpec((TM, D), lambda i: (jnp.maximum(i - 1, 0), 0)),
            pl.BlockSpec((FRONT, D), lambda i: (0, 0)),
            pl.BlockSpec((8, D), lambda i: (0, 0)),
        ],
        out_shape=[
            jax.ShapeDtypeStruct(x2d.shape, F32),
            jax.ShapeDtypeStruct((FRONT, D), F32),
            jax.ShapeDtypeStruct((8, D), F32),
        ],
        scratch_shapes=[],
        args=(du, dh1, head, x2d, rstd1, norm_w),
        riders=riders,
    )


def _matmul_tn(name, x, dy, out_cols, col0=0, prev=None, k_block=None, n_block=None, riders=(), col_map=None):
    col_map = col_map or (lambda n: n)
    rows, kdim = x.shape
    ndim = dy.shape[1]
    kb = k_block or kdim
    nb = n_block or ndim
    step_bytes = lambda t: 2 * t * (kb * x.dtype.itemsize + nb * dy.dtype.itemsize) + 2 * kb * nb * 4
    tr = next(t for t in (2816, 1408, TM_HEAVY, TM) if rows % t == 0 and (t == TM or step_bytes(t) <= TN_VMEM_BUDGET))
    nr, nk, nn = rows // tr, kdim // kb, ndim // nb
    cb0 = col0 // nb

    def body(*refs):
        x_ref, dy_ref, out_ref = refs[-3], refs[-2], refs[-1]
        part = _dot_tn(x_ref[...].astype(BF16), dy_ref[...].astype(BF16))

        @pl.when(pl.program_id(2) == 0)
        def _():
            out_ref[...] = part

        @pl.when(pl.program_id(2) > 0)
        def _():
            out_ref[...] += part

    in_specs = [
        pl.BlockSpec((tr, kb), lambda n, k, r: (r, k)),
        pl.BlockSpec((tr, nb), lambda n, k, r: (r, n)),
    ]
    args = [x, dy]
    aliases = {}
    if prev is not None:
        in_specs = [pl.BlockSpec(memory_space=pl.ANY)] + in_specs
        args = [prev] + args
        aliases = {0: 0}
    (out,), rider_outs = _hosted_call(
        body,
        name=name,
        grid=(nn, nk, nr),
        in_specs=in_specs,
        out_specs=[pl.BlockSpec((kb, nb), lambda n, k, r: (k, cb0 + col_map(n)))],
        out_shape=[jax.ShapeDtypeStruct((kdim, out_cols), F32)],
        scratch_shapes=[],
        aliases=aliases,
        args=args,
        riders=riders,
    )
    return (out, rider_outs) if riders else out


def _local_step(x2d, target, w, plan):
    rows = FRONT + x2d.shape[0]
    head = jnp.concatenate([jnp.zeros((PAD_ROWS, D), F32), w["meta_tokens"]], axis=0)
    cos_t, sin_t = _rope_tables(rows)
    intra, intra_t, q_dec, k_dec, c_dec = _decay_tables()
    grads = {}

    def hosted(host, fn, *args, **kwargs):
        outs, rider_outs = fn(*args, riders=plan.riders(host, w, grads), **kwargs)
        plan.after(host, rider_outs, w, grads)
        return outs

    (u1, rstd1), _ = _norm1(head, x2d, w["mix_norm_w"])
    proj, w["w_in"] = hosted("in_proj", _in_proj, u1, w["w_in_shard"], w["route"], cos_t, sin_t)
    o, sprev = hosted("retention_fwd", _retention_fwd, proj, intra, q_dec, k_dec, c_dec)
    lru_args = (w["conv_w"], w["conv_b"], w["lru_wa"], w["lru_wx"], w["lru_ba"], w["lru_bx"], w["lru_lambda"])
    lru_saved = hosted("lru_fwd", _lru_fwd, proj, *lru_args)
    h_lru = lru_saved[0]
    y_ret, y_lru, h1, u2, rstd2 = hosted("mix_fwd", _mix_fwd, head, x2d, o, proj, h_lru, w["w_branch_ret"],
                                         w["w_branch_lru"], w["w_out"], w["ffn_norm_w"])
    silu, dsilu, act = hosted("ffn_fwd_gate", _ffn_fwd_gate, u2, w["w_ffn_in"])
    dh2, stats_loss = _ffn_out_loss(act, h1, w["w_ffn_out"], target, w["final_norm_w"])

    dg, dup, dh2b, dh1, stats_ffn = _ffn_bwd(dh2, silu, dsilu, h1, rstd2, w["w_ffn_in"], w["w_ffn_out"], w["ffn_norm_w"])
    grads["w_ffn_in"] = _matmul_tn("dw_ffn_up", u2, dup, 2 * FFN, col0=FFN, n_block=FFN_HALF,
                                   prev=_matmul_tn("dw_ffn_gate", u2, dg, 2 * FFN, n_block=FFN_HALF))
    grads["w_ffn_out"] = _matmul_tn("dw_ffn_out", act, dh2b, D, k_block=FFN_HALF)
    (dproj, do, dhl, mixed, a_ret, a_lru, dy_ret, dy_lru) = hosted(
        "mix_bwd", _mix_bwd, dh1, o, proj, h_lru, y_ret, y_lru, w["w_branch_ret"], w["w_branch_lru"], w["w_out"])
    grads["w_out"] = _matmul_tn("dw_out", mixed, dh1, D)
    grads["w_branch_ret"] = _matmul_tn("dw_branch_ret", a_ret, dy_ret, D)
    grads["w_branch_lru"] = _matmul_tn("dw_branch_lru", a_lru, dy_lru, D)
    (dproj,) = hosted("retention_bwd", _retention_bwd, dproj, proj, do, sprev, intra, intra_t, q_dec, k_dec, c_dec,
                      cos_t, sin_t)
    dproj, grads["lru_wa"], grads["lru_wx"], stats_lru = hosted(
        "lru_bwd", _lru_bwd, dproj, proj, lru_saved, dhl, w["conv_w"], w["lru_wa"], w["lru_wx"], w["lru_lambda"])
    grads["w_in"], rider_outs = _matmul_tn("dw_in", u1, dproj, IN_COLS, n_block=D, col_map=_swap_3_4,
                                           riders=plan.riders("dw_in", w, grads))
    plan.after("dw_in", rider_outs, w, grads)
    (du1,) = hosted("in_proj_bwd_0", _in_proj_bwd, dproj, w["w_in"], 0)
    (du1,) = hosted("in_proj_bwd_1", _in_proj_bwd, dproj, w["w_in"], 1, du1)
    (grad_x, grad_head, stats_in), _ = _norm1_bwd(du1, dh1, head, x2d, rstd1, w["mix_norm_w"])
    return grad_x, grad_head, grads, [stats_loss, stats_ffn, stats_in, stats_lru]


BIG_PIECES = {
    "w_in": ("col", (D, 2 * D)),
    "w_ffn_in": ("col", (D, FFN_HALF)),
    "w_ffn_out": ("row", (FFN // 4, D)),
    "w_branch_ret": ("row", (D // 4, D)),
    "w_branch_lru": ("row", (D // 4, D)),
    "w_out": ("row", (D // 4, D)),
    "lru_wa": ("lru", (LRU_BLOCKS, LRU_BLOCK // 4, LRU_BLOCK)),
    "lru_wx": ("lru", (LRU_BLOCKS, LRU_BLOCK // 4, LRU_BLOCK)),
}
SMALL_PIECES = {"meta_tokens": ("col", (N_META, D // 4)), "conv_w": ("col", (4, D // 4))}


def _full_shape(kind, shard):
    if kind == "col":
        return (shard[0], 4 * shard[1])
    if kind == "row":
        return (4 * shard[0], shard[1])
    return (shard[0], 4 * shard[1], shard[2])


def _half_shape(kind, shard):
    return (shard[0] // 2,) + tuple(shard[1:])


def _aligned(start, multiple):
    return start if isinstance(start, int) else pl.multiple_of(start, multiple)


def _lead(h, size):
    if h is None:
        return pl.ds(0, size)
    return pl.ds(_aligned(h * (size // 2), size // 2), size // 2)


def _full_region(ref, kind, shard, s, h):
    if kind == "col":
        return ref.at[_lead(h, shard[0]), pl.ds(_aligned(s * shard[1], shard[1]), shard[1])]
    if kind == "row":
        size = shard[0] if h is None else shard[0] // 2
        start = s * shard[0] + (0 if h is None else h * (shard[0] // 2))
        return ref.at[pl.ds(_aligned(start, BF16_ROWS), size), :]
    return ref.at[_lead(h, shard[0]), pl.ds(_aligned(s * shard[1], shard[1]), shard[1]), :]


def _shard_region(ref, shard, h):
    return ref.at[_lead(h, shard[0])]


def _place():
    x, y, c = lax.axis_index("x"), lax.axis_index("y"), lax.axis_index("c")
    return x, y, c, 2 * x + y


def _other_chip(s, c, k):
    s2 = jnp.bitwise_xor(s, k)
    return s2, (s2 // 2, s2 % 2, c)


def _remote(src, dst, send_sem, recv_sem, dev):
    return pltpu.make_async_remote_copy(src_ref=src, dst_ref=dst, send_sem=send_sem, recv_sem=recv_sem,
                                        device_id=dev, device_id_type=MESH)


_ANY = pl.BlockSpec(memory_space=pl.ANY)


class _Rider:
    def __init__(self, ins, out_shapes, sem_shapes, build, aliased=False):
        self.ins, self.out_shapes, self.sem_shapes, self.build, self.aliased = ins, out_shapes, sem_shapes, build, aliased


def _hosted_call(body, *, name, grid, in_specs, out_specs, out_shape, scratch_shapes, args, riders=(), aliases=None,
                 prefetch=None, riders_after_body=False):
    n_in, n_out, n_sc = len(in_specs), len(out_shape), len(scratch_shapes)
    r_in = [a for r in riders for a in r.ins]
    r_out = [s for r in riders for s in r.out_shapes]
    r_sem = [s for r in riders for s in r.sem_shapes]
    lead = () if prefetch is None else (prefetch,)
    assert prefetch is None or not (aliases or any(r.aliased for r in riders))

    def full_body(*refs):
        head, refs = refs[:len(lead)], refs[len(lead):]
        ins, rin = refs[:n_in], refs[n_in:n_in + len(r_in)]
        o0 = n_in + len(r_in)
        outs, rout = refs[o0:o0 + n_out], refs[o0 + n_out:o0 + n_out + len(r_out)]
        s0 = o0 + n_out + len(r_out)
        scratch, rsem = refs[s0:s0 + n_sc], refs[s0 + n_sc:]
        starts, waits = [], []
        pi = po = ps = 0
        for r in riders:
            st, wt = r.build(rin[pi:pi + len(r.ins)], rout[po:po + len(r.out_shapes)], rsem[ps:ps + len(r.sem_shapes)])
            starts += st
            waits += wt
            pi, po, ps = pi + len(r.ins), po + len(r.out_shapes), ps + len(r.sem_shapes)
        first = functools.reduce(jnp.logical_and, [pl.program_id(d) == 0 for d in range(len(grid))])
        last = functools.reduce(jnp.logical_and, [pl.program_id(d) == grid[d] - 1 for d in range(len(grid))])

        def start_riders():
            @pl.when(first)
            def _():
                for cp in starts:
                    cp.start()

        if riders and not riders_after_body:
            start_riders()
        body(*head, *ins, *outs, *scratch)
        if riders and riders_after_body:
            start_riders()
        if riders:
            @pl.when(last)
            def _():
                for wait in waits:
                    wait()

    io_aliases = dict(aliases or {})
    pi = po = 0
    for r in riders:
        if r.aliased:
            for q in range(len(r.ins)):
                io_aliases[n_in + pi + q] = n_out + po + q
        pi, po = pi + len(r.ins), po + len(r.out_shapes)
    specs = dict(
        grid=grid,
        in_specs=list(in_specs) + [_ANY] * len(r_in),
        out_specs=list(out_specs) + [_ANY] * len(r_out),
        scratch_shapes=list(scratch_shapes) + r_sem,
    )
    if prefetch is not None:
        specs = dict(grid_spec=pltpu.PrefetchScalarGridSpec(num_scalar_prefetch=1, **specs))
    res = pl.pallas_call(
        full_body,
        name=name,
        out_shape=list(out_shape) + r_out,
        input_output_aliases=io_aliases,
        compiler_params=_params(("arbitrary",) * len(grid)),
        **specs,
    )(*lead, *args, *r_in)
    rider_outs, po = [], n_out
    for r in riders:
        rider_outs.append(list(res[po:po + len(r.out_shapes)]))
        po += len(r.out_shapes)
    return list(res[:n_out]), rider_outs


def _run_riders(name, riders):
    r_in = [a for r in riders for a in r.ins]
    r_out = [s for r in riders for s in r.out_shapes]
    r_sem = [s for r in riders for s in r.sem_shapes]

    def body(*refs):
        rin, rout, rsem = refs[:len(r_in)], refs[len(r_in):len(r_in) + len(r_out)], refs[len(r_in) + len(r_out):]
        pi = po = ps = 0
        all_waits = []
        for r in riders:
            starts, waits = r.build(rin[pi:pi + len(r.ins)], rout[po:po + len(r.out_shapes)], rsem[ps:ps + len(r.sem_shapes)])
            for cp in starts:
                cp.start()
            all_waits += waits
            pi, po, ps = pi + len(r.ins), po + len(r.out_shapes), ps + len(r.sem_shapes)
        for wait in all_waits:
            wait()

    io_aliases = {}
    pi = po = 0
    for r in riders:
        if r.aliased:
            for q in range(len(r.ins)):
                io_aliases[pi + q] = po + q
        pi, po = pi + len(r.ins), po + len(r.out_shapes)
    res = pl.pallas_call(
        body,
        name=name,
        in_specs=[_ANY] * len(r_in),
        out_specs=[_ANY] * len(r_out),
        out_shape=r_out,
        scratch_shapes=r_sem,
        input_output_aliases=io_aliases,
    )(*r_in)
    outs, po = [], 0
    for r in riders:
        outs.append(list(res[po:po + len(r.out_shapes)]))
        po += len(r.out_shapes)
    return outs


def _piece(name):
    if name in BIG_PIECES:
        return (name, *BIG_PIECES[name], True)
    return (name, *SMALL_PIECES[name], False)


def _gather_rider(shards, names):
    pieces = [_piece(n) for n in names]
    n = len(pieces)

    def build(ins, outs, sems):
        local_sem, ici_send, ici_recv = sems
        _, _, c, s = _place()
        starts, waits = [], []
        for p, (_, kind, shard, split) in enumerate(pieces):
            cp = pltpu.make_async_copy(ins[p], _full_region(outs[p], kind, shard, s, None), local_sem.at[p])
            starts.append(cp)
            waits.append(cp.wait)
            h = c if split else None
            for k in (1, 2, 3):
                s2, dev = _other_chip(s, c, k)
                cp = _remote(_shard_region(ins[p], shard, h), _full_region(outs[p], kind, shard, s, h),
                             ici_send.at[p, k - 1], ici_recv.at[p, k - 1], dev)
                starts.append(cp)
                waits.append(cp.wait_send)
                region = _full_region(outs[p], kind, shard, s2, h)
                waits.append(_remote(region, region, ici_send.at[p, k - 1], ici_recv.at[p, k - 1], dev).wait_recv)
        return starts, waits

    return _Rider(
        [shards[name] for name in names],
        [jax.ShapeDtypeStruct(_full_shape(kind, shard), shards[name].dtype) for name, kind, shard, _ in pieces],
        [pltpu.SemaphoreType.DMA((n,)), pltpu.SemaphoreType.DMA((n, 3)), pltpu.SemaphoreType.DMA((n, 3))],
        build)


def _forward_rider(gathered, names):
    pieces = [_piece(n) for n in names]
    n = len(pieces)

    def build(ins, outs, sems):
        fwd_send, fwd_recv = sems
        x, y, c, s = _place()
        sibling = (x, y, 1 - c)
        starts, waits = [], []
        for p, (_, kind, shard, _) in enumerate(pieces):
            for k in (1, 2, 3):
                s2, _ = _other_chip(s, c, k)
                mine = _full_region(outs[p], kind, shard, s2, c)
                theirs = _full_region(outs[p], kind, shard, s2, 1 - c)
                cp = _remote(mine, mine, fwd_send.at[p, k - 1], fwd_recv.at[p, k - 1], sibling)
                starts.append(cp)
                waits.append(cp.wait_send)
                waits.append(_remote(theirs, theirs, fwd_send.at[p, k - 1], fwd_recv.at[p, k - 1], sibling).wait_recv)
        return starts, waits

    return _Rider(
        [gathered[name] for name in names],
        [jax.ShapeDtypeStruct(gathered[name].shape, gathered[name].dtype) for name in names],
        [pltpu.SemaphoreType.DMA((n, 3)), pltpu.SemaphoreType.DMA((n, 3))],
        build, aliased=True)


def _pair_exchange_rider(grads, names):
    n = len(names)

    def build(ins, outs, sems):
        send_sem, recv_sem = sems
        x, y, c, _ = _place()
        sibling = (x, y, 1 - c)
        starts, waits = [], []
        for p, name in enumerate(names):
            kind, shard = BIG_PIECES[name]
            for s2 in range(4):
                cp = _remote(_full_region(ins[p], kind, shard, s2, 1 - c), outs[p].at[s2], send_sem.at[p, s2],
                             recv_sem.at[p, s2], sibling)
                starts.append(cp)
                waits.append(cp.wait_send)
                waits.append(_remote(outs[p].at[s2], outs[p].at[s2], send_sem.at[p, s2], recv_sem.at[p, s2], sibling).wait_recv)
        return starts, waits

    return _Rider(
        [grads[name] for name in names],
        [jax.ShapeDtypeStruct((4,) + _half_shape(*BIG_PIECES[name]), F32) for name in names],
        [pltpu.SemaphoreType.DMA((n, 4))] * 2,
        build)


def _half_specs(kind, shard):
    half = shard[0] // 2
    if kind == "col":
        full = pl.BlockSpec((half, shard[1]), lambda j, pr: (pr[1], j))
        buf = pl.BlockSpec((None, half, shard[1]), lambda j, pr: (j, 0, 0))
    elif kind == "row":
        full = pl.BlockSpec((half, shard[1]), lambda j, pr: (2 * j + pr[1], 0))
        buf = pl.BlockSpec((None, half, shard[1]), lambda j, pr: (j, 0, 0))
    else:
        full = pl.BlockSpec((half, shard[1], shard[2]), lambda j, pr: (pr[1], j, 0))
        buf = pl.BlockSpec((None, half, shard[1], shard[2]), lambda j, pr: (j, 0, 0, 0))
    return full, buf


def _pair_sum(name, grad, recv, place):
    kind, shard = BIG_PIECES[name]
    full, buf = _half_specs(kind, shard)

    def body(pr, g_ref, r_ref, o_ref):
        o_ref[...] = (g_ref[...] + r_ref[...]).astype(BF16)

    return pl.pallas_call(
        body,
        name="pair_sum_" + name,
        grid_spec=pltpu.PrefetchScalarGridSpec(num_scalar_prefetch=1, grid=(4,), in_specs=[full, buf], out_specs=buf),
        out_shape=jax.ShapeDtypeStruct((4,) + _half_shape(kind, shard), BF16),
        compiler_params=_params(("arbitrary",)),
    )(place, grad, recv)


def _chip_exchange_rider(sums, names):
    n = len(names)

    def build(ins, outs, sems):
        send_sem, recv_sem = sems
        _, _, c, s = _place()
        starts, waits = [], []
        for p in range(n):
            for k in (1, 2, 3):
                s2, dev = _other_chip(s, c, k)
                cp = _remote(ins[p].at[s2], outs[p].at[k - 1], send_sem.at[p, k - 1], recv_sem.at[p, k - 1], dev)
                starts.append(cp)
                waits.append(cp.wait_send)
                waits.append(_remote(outs[p].at[k - 1], outs[p].at[k - 1], send_sem.at[p, k - 1], recv_sem.at[p, k - 1],
                                     dev).wait_recv)
        return starts, waits

    return _Rider(
        [sums[name] for name in names],
        [jax.ShapeDtypeStruct((3,) + _half_shape(*BIG_PIECES[name]), BF16) for name in names],
        [pltpu.SemaphoreType.DMA((n, 3))] * 2,
        build)


def _chip_sum(name, grad, recv_pair, recv_chip, place):
    kind, shard = BIG_PIECES[name]
    half = shard[0] // 2
    tail = tuple(shard[1:])
    zeros = (0,) * len(tail)
    nt = 4 if kind != "lru" and half % (4 * BF16_ROWS) == 0 else 1
    rows = half // nt
    if kind == "col":
        full = pl.BlockSpec((rows,) + tail, lambda j, pr: (pr[1] * nt + j, pr[0]))
    elif kind == "row":
        full = pl.BlockSpec((rows,) + tail, lambda j, pr: ((2 * pr[0] + pr[1]) * nt + j, 0))
    else:
        full = pl.BlockSpec((rows,) + tail, lambda j, pr: (pr[1], pr[0], 0))
    pair = pl.BlockSpec((None, rows) + tail, lambda j, pr: (pr[0], j) + zeros)
    chip = pl.BlockSpec((3, rows) + tail, lambda j, pr: (0, j) + zeros)
    out = pl.BlockSpec((rows,) + tail, lambda j, pr: (pr[1] * nt + j,) + zeros)

    def body(pr, g_ref, rp_ref, rc_ref, o_ref):
        total = g_ref[...] + rp_ref[...]
        for k in range(3):
            total = total + rc_ref[k].astype(F32)
        o_ref[...] = total

    return pl.pallas_call(
        body,
        name="chip_sum_" + name,
        grid_spec=pltpu.PrefetchScalarGridSpec(num_scalar_prefetch=1, grid=(nt,), in_specs=[full, pair, chip], out_specs=out),
        out_shape=jax.ShapeDtypeStruct(shard, F32),
        compiler_params=_params(("arbitrary",)),
    )(place, grad, recv_pair, recv_chip)


def _sibling_exchange_rider(halves, names):
    n = len(names)

    def build(ins, outs, sems):
        send_sem, recv_sem = sems
        x, y, c, _ = _place()
        sibling = (x, y, 1 - c)
        starts, waits = [], []
        for p, name in enumerate(names):
            shard = BIG_PIECES[name][1]
            mine = _shard_region(outs[p], shard, c)
            theirs = _shard_region(outs[p], shard, 1 - c)
            cp = _remote(mine, mine, send_sem.at[p], recv_sem.at[p], sibling)
            starts.append(cp)
            waits.append(cp.wait_send)
            waits.append(_remote(theirs, theirs, send_sem.at[p], recv_sem.at[p], sibling).wait_recv)
        return starts, waits

    return _Rider(
        [halves[name] for name in names],
        [jax.ShapeDtypeStruct(BIG_PIECES[name][1], F32) for name in names],
        [pltpu.SemaphoreType.DMA((n,))] * 2,
        build, aliased=True)


FIRST_WEIGHTS = ["meta_tokens", "conv_w"]
WEIGHT_GROUPS = {
    "lru": ["lru_wa", "lru_wx"],
    "branch": ["w_branch_ret", "w_branch_lru", "w_out"],
    "ffn_in": ["w_ffn_in"],
    "ffn_out": ["w_ffn_out"],
}
WEIGHT_SCHEDULE = {
    "in_proj": [("gather", "lru")],
    "retention_fwd": [("forward", "lru"), ("gather", "branch")],
    "lru_fwd": [("forward", "branch"), ("gather", "ffn_in")],
    "mix_fwd": [("forward", "ffn_in"), ("gather", "ffn_out")],
    "ffn_fwd_gate": [("forward", "ffn_out")],
}
GRAD_GROUPS = {
    "ffn_in": ["w_ffn_in"],
    "ffn_out": ["w_ffn_out"],
    "mixer": ["w_out", "w_branch_ret", "w_branch_lru", "lru_wa", "lru_wx"],
    "in": ["w_in"],
}
GRAD_SCHEDULE = {
    "mix_bwd": [("pair", "ffn_in"), ("pair", "ffn_out")],
    "retention_bwd": [("chip", "ffn_out")],
    "lru_bwd": [("chip", "ffn_in"), ("sibling", "ffn_out")],
    "dw_in": [("sibling", "ffn_in"), ("pair", "mixer")],
    "in_proj_bwd_0": [("chip", "mixer"), ("pair", "in")],
    "in_proj_bwd_1": [("sibling", "mixer"), ("chip", "in")],
}


class _CommPlan:
    def __init__(self, shards, place):
        self.shards, self.place = shards, place
        self.late = {}
        self.recv_pair, self.sums, self.recv_chip, self.halves, self.final = {}, {}, {}, {}, {}

    def _grad_rider(self, stage, group, grads):
        names = GRAD_GROUPS[group]
        if stage == "pair":
            return _pair_exchange_rider(grads, names)
        if stage == "chip":
            return _chip_exchange_rider(self.sums, names)
        return _sibling_exchange_rider(self.halves, names)

    def _grad_after(self, stage, group, outs, grads):
        names = GRAD_GROUPS[group]
        if stage == "pair":
            for n, o in zip(names, outs):
                self.recv_pair[n] = o
                self.sums[n] = _pair_sum(n, grads[n], o, self.place)
        elif stage == "chip":
            for n, o in zip(names, outs):
                self.halves[n] = _chip_sum(n, grads[n], self.recv_pair[n], o, self.place)
        else:
            self.final.update(zip(names, outs))

    def riders(self, host, w, grads):
        if host in WEIGHT_SCHEDULE:
            return [_gather_rider(self.shards, WEIGHT_GROUPS[group]) if stage == "gather"
                    else _forward_rider(self.late, WEIGHT_GROUPS[group]) for stage, group in WEIGHT_SCHEDULE[host]]
        return [self._grad_rider(stage, group, grads) for stage, group in GRAD_SCHEDULE.get(host, [])]

    def after(self, host, rider_outs, w, grads):
        for (stage, group), outs in zip(WEIGHT_SCHEDULE.get(host, []), rider_outs):
            (self.late if stage == "gather" else w).update(zip(WEIGHT_GROUPS[group], outs))
        for (stage, group), outs in zip(GRAD_SCHEDULE.get(host, []), rider_outs):
            self._grad_after(stage, group, outs, grads)

    def finish(self, partial):
        outs, (blocks,) = _run_riders("tail_exchange", [_sibling_exchange_rider(self.halves, GRAD_GROUPS["in"]),
                                                        _small_exchange_rider(partial)])
        self.final.update(zip(GRAD_GROUPS["in"], outs))
        return self.final, blocks


def _adamw_math(w, g, m, v):
    m = ADAM_B1 * m + (1.0 - ADAM_B1) * g
    v = ADAM_B2 * v + (1.0 - ADAM_B2) * (g * g)
    m_hat = m / (1.0 - ADAM_B1 ** ADAM_STEP)
    v_hat = v / (1.0 - ADAM_B2 ** ADAM_STEP)
    delta = -ADAM_LR * (m_hat / (jnp.sqrt(v_hat) + ADAM_EPS) + ADAM_WD * w)
    return delta, m, v


def _adamw(name, g, w, m, v):
    rows, cols = g.shape
    tr = rows // 4 if rows % 32 == 0 else rows

    def body(g_ref, w_ref, m_ref, v_ref, go_ref, d_ref, mo_ref, vo_ref):
        gv = g_ref[...]
        delta, m2, v2 = _adamw_math(w_ref[...], gv, m_ref[...], v_ref[...])
        go_ref[...] = gv
        d_ref[...] = delta
        mo_ref[...] = m2
        vo_ref[...] = v2

    spec = pl.BlockSpec((tr, cols), lambda i: (i, 0))
    return pl.pallas_call(
        body,
        name="adamw_" + name,
        grid=(rows // tr,),
        in_specs=[spec] * 4,
        out_specs=[spec] * 4,
        out_shape=[jax.ShapeDtypeStruct((rows, cols), F32)] * 4,
        compiler_params=_params(("arbitrary",)),
    )(g, w, m, v)


SMALL_ROWS = 48
VEC_ROWS = {"final_norm_w": 1, "ffn_norm_w": 8, "mix_norm_w": 16, "conv_b": 24, "lru_ba": 25, "lru_bx": 26, "lru_lambda": 27}
VEC_NAMES = list(VEC_ROWS)
CONV_W_ROW = 28
META_ROW = 32


def _small_exchange_rider(partial):
    def build(ins, outs, sems):
        local_sem, send_sem, recv_sem = sems
        _, _, c, s = _place()
        me = 2 * s + c
        cp = pltpu.make_async_copy(ins[0], outs[0].at[me], local_sem)
        starts, waits = [cp], [cp.wait]
        for k in range(1, 8):
            peer = jnp.bitwise_xor(me, k)
            dev = (peer // 4, (peer // 2) % 2, peer % 2)
            rd = _remote(ins[0], outs[0].at[me], send_sem.at[k - 1], recv_sem.at[k - 1], dev)
            starts.append(rd)
            waits.append(rd.wait_send)
            waits.append(_remote(ins[0], outs[0].at[peer], send_sem.at[k - 1], recv_sem.at[k - 1], dev).wait_recv)
        return starts, waits

    return _Rider([partial], [jax.ShapeDtypeStruct((8, SMALL_ROWS, D), F32)],
                  [pltpu.SemaphoreType.DMA, pltpu.SemaphoreType.DMA((7,)), pltpu.SemaphoreType.DMA((7,))], build)


def _small_update(blocks, place, vecs, conv, meta):
    nvec = len(VEC_NAMES)
    qcols = D // 4

    def in_order(ref):
        total = ref[0]
        for d in range(1, 8):
            total = total + ref[d]
        return total

    def body(pr, blocks_ref, cols_ref, *refs):
        vec_refs = refs[:3 * nvec]
        conv_refs = refs[3 * nvec:3 * nvec + 3]
        meta_refs = refs[3 * nvec + 3:3 * nvec + 6]
        outs = refs[3 * nvec + 6:]
        loss_ref, vec_out, conv_out, meta_out = outs[0], outs[1:5], outs[5:9], outs[9:13]
        tot, col = in_order(blocks_ref), in_order(cols_ref)
        loss_ref[...] = jnp.sum(tot[0:1, :], axis=1, keepdims=True)
        for o in vec_out:
            o[...] = jnp.zeros_like(o)
        for j, name in enumerate(VEC_NAMES):
            w, m, v = (r[...] for r in vec_refs[3 * j:3 * j + 3])
            g = tot[VEC_ROWS[name]:VEC_ROWS[name] + 1, :]
            if name == "lru_lambda":
                g = -g / (1.0 + jnp.exp(w))
            for o, val in zip(vec_out, (g,) + _adamw_math(w, g, m, v)):
                o[j:j + 1, :] = val
        for row, n_rows, ins, group in ((CONV_W_ROW, 4, conv_refs, conv_out), (META_ROW, N_META, meta_refs, meta_out)):
            g = col[row:row + n_rows, :]
            w, m, v = (r[...] for r in ins)
            for o, val in zip(group, (g,) + _adamw_math(w, g, m, v)):
                o[...] = val

    whole = lambda shape: pl.BlockSpec(shape, lambda i, pr: (0,) * len(shape))
    in_specs = [whole((8, SMALL_ROWS, D)), pl.BlockSpec((8, SMALL_ROWS, qcols), lambda i, pr: (0, 0, pr[0]))]
    in_specs += [whole((1, D))] * (3 * nvec) + [whole((4, qcols))] * 3 + [whole((N_META, qcols))] * 3
    out_shapes = [(1, 1)] + [(8, D)] * 4 + [(4, qcols)] * 4 + [(N_META, qcols)] * 4
    return pl.pallas_call(
        body,
        name="small_update",
        grid_spec=pltpu.PrefetchScalarGridSpec(num_scalar_prefetch=1, grid=(1,), in_specs=in_specs,
                                               out_specs=[whole(s) for s in out_shapes]),
        out_shape=[jax.ShapeDtypeStruct(s, F32) for s in out_shapes],
        compiler_params=_params(("arbitrary",)),
    )(place, blocks, blocks, *[a for t in vecs for a in t], *conv, *meta)


WEIGHT_ORDER = ["meta_tokens", "mix_norm_w", "w_in", "conv_w", "conv_b", "lru_wa", "lru_ba", "lru_wx", "lru_bx", "lru_lambda",
                "w_branch_ret", "w_branch_lru", "w_out", "ffn_norm_w", "w_ffn_in", "w_ffn_out", "final_norm_w"]


def kernel(x, meta_tokens, mix_norm_w, w_in, conv_w, conv_b, lru_wa, lru_ba, lru_wx, lru_bx, lru_lambda, w_branch_ret, w_branch_lru, w_out, ffn_norm_w, w_ffn_in, w_ffn_out, final_norm_w, loss_target, m_meta_tokens, m_mix_norm_w, m_w_in, m_conv_w, m_conv_b, m_lru_wa, m_lru_ba, m_lru_wx, m_lru_bx, m_lru_lambda, m_w_branch_ret, m_w_branch_lru, m_w_out, m_ffn_norm_w, m_w_ffn_in, m_w_ffn_out, m_final_norm_w, v_meta_tokens, v_mix_norm_w, v_w_in, v_conv_w, v_conv_b, v_lru_wa, v_lru_ba, v_lru_wx, v_lru_bx, v_lru_lambda, v_w_branch_ret, v_w_branch_lru, v_w_out, v_ffn_norm_w, v_w_ffn_in, v_w_ffn_out, v_final_norm_w):
    args = locals()
    wts = {n: args[n] for n in WEIGHT_ORDER}
    mom = {n: args["m_" + n] for n in WEIGHT_ORDER}
    var = {n: args["v_" + n] for n in WEIGHT_ORDER}
    place = jnp.stack([2 * lax.axis_index("x") + lax.axis_index("y"), lax.axis_index("c")]).astype(jnp.int32)

    shards = {n: wts[n][0].astype(BF16) for n in BIG_PIECES}
    shards["meta_tokens"] = wts["meta_tokens"]
    shards["conv_w"] = wts["conv_w"][0]
    plan = _CommPlan(shards, place)
    (first,) = _run_riders("gather_first", [_gather_rider(shards, FIRST_WEIGHTS)])
    w = dict(zip(FIRST_WEIGHTS, first))
    for n in VEC_NAMES:
        w[n] = wts[n].reshape(1, D)
    chips = jnp.bitwise_xor(place[0], jnp.array((0,) + SHARD_ORDER, dtype=jnp.int32))
    w["route"] = jnp.concatenate([place, jnp.stack([2 * chips, 2 * chips + 1], axis=1).reshape(8)])
    w["w_in_shard"] = shards["w_in"]

    grad_x, grad_head, _, stats = _local_step(x[0], loss_target[0], w, plan)
    partial = jnp.concatenate(stats + [grad_head[PAD_ROWS:]], axis=0)
    shard_grads, blocks = plan.finish(partial)

    out = {}
    for n in BIG_PIECES:
        shape2d = (-1, wts[n].shape[-1])
        res = _adamw(n, *[a.reshape(shape2d) for a in (shard_grads[n], wts[n], mom[n], var[n])])
        out[n] = [r.reshape(wts[n].shape) for r in res]

    vecs = [tuple(a[n].reshape(1, D) for a in (wts, mom, var)) for n in VEC_NAMES]
    conv = tuple(a["conv_w"][0] for a in (wts, mom, var))
    meta = tuple(a["meta_tokens"] for a in (wts, mom, var))
    res = _small_update(blocks, place, vecs, conv, meta)
    loss = res[0].reshape(())
    for j, n in enumerate(VEC_NAMES):
        out[n] = [r[j].reshape(wts[n].shape) for r in res[1:5]]
    out["conv_w"] = [r.reshape(wts["conv_w"].shape) for r in res[5:9]]
    out["meta_tokens"] = list(res[9:13])

    return (loss, grad_x.reshape(x.shape)) + tuple(out[n][kind] for kind in range(4) for n in WEIGHT_ORDER)
```

```python
import functools
import math

import jax
import jax.numpy as jnp
from jax import lax
from jax.experimental import pallas as pl
from jax.experimental.pallas import tpu as pltpu

F32 = jnp.float32
BF16 = jnp.bfloat16

LANES = 128
BF16_ROWS = 16

D = 1024
HEADS = 8
DH = 128
CHUNK = 256
N_META = 16
FRONT = 256
PAD_ROWS = FRONT - N_META
LRU_BLOCKS = 4
LRU_BLOCK = 256
LRU_C = 8.0
FFN = 2816
FFN_HALF = FFN // 2
IN_COLS = 8 * D
ROPE_BASE = 10000.0
EPS = 1e-6
QK_SCALE = DH ** -0.5

ADAM_LR = 0.001
ADAM_B1 = 0.9
ADAM_B2 = 0.999
ADAM_EPS = 1e-08
ADAM_WD = 0.01
ADAM_STEP = 10

TM = 256
TM_HEAVY = 768
MXU_TILE = 256
FFN_SUB = 3 * MXU_TILE
_FFN_SUBS = [(a, min(a + FFN_SUB, FFN)) for a in range(0, FFN, FFN_SUB)]
TM_MIX_BWD = 384
TM_LRU = 384
TM_MIX_FWD = 384
TM_IN_PROJ = 1408
VMEM_LIMIT = 60 * 1024 * 1024
TN_VMEM_BUDGET = 40 * 1024 * 1024

NT_DIMS = (((1,), (1,)), ((), ()))
TN_DIMS = (((0,), (0,)), ((), ()))
MESH = pl.DeviceIdType.MESH


def _params(sem=None):
    if sem is None:
        return pltpu.CompilerParams(vmem_limit_bytes=VMEM_LIMIT)
    return pltpu.CompilerParams(dimension_semantics=sem, vmem_limit_bytes=VMEM_LIMIT)


def _dot(a, b):
    return jnp.dot(a, b, preferred_element_type=F32)


def _dot_nt(a, b):
    return lax.dot_general(a, b, NT_DIMS, preferred_element_type=F32)


def _dot_tn(a, b):
    return lax.dot_general(a, b, TN_DIMS, preferred_element_type=F32)


def _sigmoid(z):
    return 1.0 / (1.0 + jnp.exp(-z))


def _log1p(x):
    return jnp.where(x < 1e-3, x * (1.0 - x * (0.5 - x * (1.0 / 3.0))), jnp.log(1.0 + x))


def _softplus(x):
    return jnp.maximum(x, 0.0) + _log1p(jnp.exp(-jnp.abs(x)))


def _one_minus_square(a, log_a):
    x = 2.0 * log_a
    series = -x * (1.0 + x * (0.5 + x * (1.0 / 6.0)))
    return jnp.where(x > -0.02, series, 1.0 - a * a)


_GELU_K = math.sqrt(2.0 / math.pi)


def _gelu_and_grad(x):
    inner = _GELU_K * (x + 0.044715 * x * x * x)
    t = jnp.tanh(inner)
    val = 0.5 * x * (1.0 + t)
    grad = 0.5 * (1.0 + t) + 0.5 * x * (1.0 - t * t) * _GELU_K * (1.0 + 3.0 * 0.044715 * x * x)
    return val, grad


def _row_ids(i, rows, shape):
    return i * rows + lax.broadcasted_iota(jnp.int32, shape, 0)


def _rope_tables(rows):
    inv_freq = ROPE_BASE ** (-jnp.arange(0, DH, 2, dtype=F32) / DH)
    block_pos = jnp.arange(rows // FRONT, dtype=jnp.int32) * FRONT - PAD_ROWS
    coarse = block_pos.astype(F32)[:, None] * inv_freq[None, :]
    fine = jnp.arange(FRONT, dtype=F32)[:, None] * inv_freq[None, :]
    ca, sa = jnp.cos(coarse)[:, None, :], jnp.sin(coarse)[:, None, :]
    cb, sb = jnp.cos(fine)[None, :, :], jnp.sin(fine)[None, :, :]
    cos = (ca * cb - sa * sb).reshape(rows, DH // 2)
    sin = (sa * cb + ca * sb).reshape(rows, DH // 2)
    return jnp.concatenate([cos, cos], axis=1), jnp.concatenate([-sin, sin], axis=1)


def _decay_tables():
    log_g = jnp.log(1.0 - 2.0 ** (-5.0 - jnp.arange(HEADS, dtype=F32)))
    idx = jnp.arange(CHUNK, dtype=F32)
    diff = idx[:, None] - idx[None, :]
    intra = jnp.where(diff[None] >= 0, jnp.exp(jnp.maximum(diff, 0.0)[None] * log_g[:, None, None]), 0.0)
    q_decay = jnp.exp((idx + 1.0)[:, None] * log_g[None, :])
    k_decay = jnp.exp((CHUNK - 1.0 - idx)[:, None] * log_g[None, :])
    chunk_decay = jnp.exp(CHUNK * log_g)
    wide = lambda a: jnp.repeat(a, DH, axis=-1)
    return intra, jnp.swapaxes(intra, 1, 2), wide(q_decay), wide(k_decay), wide(chunk_decay[None, :])


def _norm1(head, x2d, norm_w, riders=()):
    rows = FRONT + x2d.shape[0]
    tm = TM_IN_PROJ if rows % TM_IN_PROJ == 0 else TM_MIX_FWD
    nt = rows // tm

    def body(head_hbm, x_hbm, nw_ref, u_ref, rstd_ref, h0_sc, h0_sem):
        slot = _frame_rows(head_hbm, x_hbm, h0_sc, h0_sem, pl.program_id(0), tm, nt)
        hv = h0_sc[slot]
        rs = lax.rsqrt(jnp.mean(hv * hv, axis=-1, keepdims=True) + EPS)
        u_ref[...] = ((hv * rs) * nw_ref[...]).astype(BF16)
        rstd_ref[...] = rs

    return _hosted_call(
        body,
        name="norm1",
        grid=(nt,),
        in_specs=[_ANY, _ANY, pl.BlockSpec((1, D), lambda i: (0, 0))],
        out_specs=[pl.BlockSpec((tm, D), lambda i: (i, 0)), pl.BlockSpec((tm, 1), lambda i: (i, 0))],
        out_shape=[jax.ShapeDtypeStruct((rows, D), BF16), jax.ShapeDtypeStruct((rows, 1), F32)],
        scratch_shapes=[pltpu.VMEM((2, tm, D), F32), pltpu.SemaphoreType.DMA((3,))],
        args=(head, x2d, norm_w),
        riders=riders,
    )


def _heavy_tile(rows):
    return TM_HEAVY if rows % TM_HEAVY == 0 else TM


def _lru_tile(rows):
    return TM_LRU if rows % TM_LRU == 0 else TM


def _swap_3_4(group):
    return jnp.where(group == 3, 4, jnp.where(group == 4, 3, group))


SHARD_ORDER = (2, 3, 1)
LRU_IN_COL = 3
GATES_COL = 1


def _in_proj(u, w_shard, route, cos_t, sin_t, riders=()):
    rows = u.shape[0]
    tm = TM_IN_PROJ if rows % TM_IN_PROJ == 0 else _heavy_tile(rows)
    nt = rows // tm
    kind, shard = BIG_PIECES["w_in"]

    def body(route_ref, u_hbm, wsh_ref, cos_ref, sin_ref, proj_ref, wfull_ref,
             u_sc, w_sc, u_sem, w_sem, local_sem, ici_send, ici_recv, fwd_send, fwd_recv):
        g, i = pl.program_id(0), pl.program_id(1)
        s, c = route_ref[0], route_ref[1]
        gid = route_ref[2 + g]
        sibling = (s // 2, s % 2, 1 - c)
        local = pltpu.make_async_copy(wsh_ref, _full_region(wfull_ref, kind, shard, s, None), local_sem)
        u_copies = [pltpu.make_async_copy(u_hbm.at[pl.ds(t * tm, tm)], u_sc.at[t], u_sem.at[t]) for t in range(nt)]
        sends, arrivals = [], []
        for k in (1, 2, 3):
            s2, dev = _other_chip(s, c, k)
            sends.append(_remote(_shard_region(wsh_ref, shard, c), _full_region(wfull_ref, kind, shard, s, c),
                                 ici_send.at[k - 1], ici_recv.at[k - 1], dev))
            mine = _full_region(wfull_ref, kind, shard, s2, c)
            theirs = _full_region(wfull_ref, kind, shard, s2, 1 - c)
            arrivals.append((_remote(mine, mine, ici_send.at[k - 1], ici_recv.at[k - 1], dev),
                             _remote(mine, mine, fwd_send.at[k - 1], fwd_recv.at[k - 1], sibling),
                             _remote(theirs, theirs, fwd_send.at[k - 1], fwd_recv.at[k - 1], sibling)))

        slot = g % 2
        last_tile = i == nt - 1

        def block_copy(src, col, to_slot):
            return pltpu.make_async_copy(src.at[:, pl.ds(pl.multiple_of(col * D, D), D)], w_sc.at[to_slot],
                                         w_sem.at[to_slot])

        @pl.when(jnp.logical_and(g == 0, i == 0))
        def _():
            for k in SHARD_ORDER:
                sends[k - 1].start()
            local.start()
            for cp in u_copies:
                cp.start()
            cp = block_copy(wsh_ref, 0, 0)
            cp.start()
            cp.wait()

        @pl.when(jnp.logical_and(last_tile, g == 0))
        def _():
            block_copy(wsh_ref, 1, 1).start()

        for pos, k in enumerate(SHARD_ORDER, start=1):
            arrived, forward, forwarded = arrivals[k - 1]

            @pl.when(jnp.logical_and(last_tile, g == 2 * pos - 1))
            def _():
                arrived.wait_recv()
                forward.start()
                forwarded.wait_recv()
                block_copy(wfull_ref, route_ref[2 + 2 * pos], 0).start()

            @pl.when(jnp.logical_and(last_tile, g == 2 * pos))
            def _():
                block_copy(wfull_ref, route_ref[3 + 2 * pos], 1).start()

        @pl.when(jnp.logical_and(i == 0, g > 0))
        def _():
            block_copy(wfull_ref, 0, slot).wait()

        for t in range(nt):
            @pl.when(jnp.logical_and(g == 0, i == t))
            def _():
                u_copies[t].wait()

        acc = _dot(u_sc[i], w_sc[slot])

        @pl.when(gid < 2)
        def _():
            scale = jnp.where(gid == 1, QK_SCALE, 1.0).astype(F32)
            for h in range(HEADS):
                sl = slice(h * DH, (h + 1) * DH)
                blk = acc[:, sl]
                out = (blk * cos_ref[...] + pltpu.roll(blk, DH // 2, axis=1) * sin_ref[...]) * scale
                proj_ref[:, sl] = out.astype(BF16)

        @pl.when(gid >= 2)
        def _():
            proj_ref[...] = acc.astype(BF16)

        @pl.when(jnp.logical_and(g == 7, i == nt - 1))
        def _():
            local.wait()
            for cp in sends:
                cp.wait_send()
            for _, forward, _ in arrivals:
                forward.wait_send()

    return _hosted_call(
        body,
        name="in_proj",
        grid=(8, nt),
        in_specs=[
            _ANY, _ANY,
            pl.BlockSpec((tm, DH), lambda g, i, rt: (i, 0)),
            pl.BlockSpec((tm, DH), lambda g, i, rt: (i, 0)),
        ],
        out_specs=[pl.BlockSpec((tm, D), lambda g, i, rt: (i, _swap_3_4(rt[2 + g]))), _ANY],
        out_shape=[jax.ShapeDtypeStruct((rows, IN_COLS), BF16), jax.ShapeDtypeStruct((D, IN_COLS), BF16)],
        scratch_shapes=[
            pltpu.VMEM((nt, tm, D), BF16), pltpu.VMEM((2, D, D), BF16),
            pltpu.SemaphoreType.DMA((nt,)), pltpu.SemaphoreType.DMA((2,)), pltpu.SemaphoreType.DMA,
            pltpu.SemaphoreType.DMA((3,)), pltpu.SemaphoreType.DMA((3,)),
            pltpu.SemaphoreType.DMA((3,)), pltpu.SemaphoreType.DMA((3,)),
        ],
        args=(u, w_shard, cos_t, sin_t),
        riders=riders,
        prefetch=route,
        riders_after_body=True,
    )


def _retention_fwd(proj_bf, intra, q_dec, k_dec, c_dec, riders=()):
    rows = proj_bf.shape[0]
    nc = rows // CHUNK

    def body(q_ref, k_ref, v_ref, m_ref, qd_ref, kd_ref, cd_ref, o_ref, sprev_ref, s_sc):
        @pl.when(pl.program_id(0) == 0)
        def _():
            s_sc[...] = jnp.zeros_like(s_sc)

        for h in range(HEADS):
            sl = slice(h * DH, (h + 1) * DH)
            q, k, v = q_ref[:, sl], k_ref[:, sl], v_ref[:, sl]
            state = s_sc[h]
            state_b = state.astype(BF16)
            sprev_ref[0, h] = state_b
            s = _dot_nt(q, k) * m_ref[h]
            inner = _dot(s.astype(BF16), v)
            cross = _dot(q, state_b) * qd_ref[:, sl]
            o_ref[:, sl] = (inner + cross).astype(BF16)
            k_scaled = (k.astype(F32) * kd_ref[:, sl]).astype(BF16)
            s_sc[h] = state * cd_ref[:, sl] + _dot_tn(k_scaled, v)

    chunk_spec = lambda col: pl.BlockSpec((CHUNK, D), lambda c: (c, col))
    const2 = lambda shape: pl.BlockSpec(shape, lambda c: (0, 0))
    return _hosted_call(
        body,
        name="retention_fwd",
        grid=(nc,),
        in_specs=[
            chunk_spec(0), chunk_spec(1), chunk_spec(2),
            pl.BlockSpec((HEADS, CHUNK, CHUNK), lambda c: (0, 0, 0)),
            const2((CHUNK, D)), const2((CHUNK, D)), const2((1, D)),
        ],
        out_specs=[
            pl.BlockSpec((CHUNK, D), lambda c: (c, 0)),
            pl.BlockSpec((1, HEADS, DH, DH), lambda c: (c, 0, 0, 0)),
        ],
        out_shape=[
            jax.ShapeDtypeStruct((rows, D), BF16),
            jax.ShapeDtypeStruct((nc, HEADS, DH, DH), BF16),
        ],
        scratch_shapes=[pltpu.VMEM((HEADS, DH, DH), F32)],
        args=(proj_bf, proj_bf, proj_bf, intra, q_dec, k_dec, c_dec),
        riders=riders,
    )


def _shift_down(x, first8_prev, d):
    rolled = pltpu.roll(x, d, axis=0)
    head = pltpu.roll(jnp.concatenate([first8_prev, x[0:8]], axis=0), d, axis=0)[8:16]
    return rolled, head


def _conv_and_gates(x, prev8, cw_ref, cb_ref, wa_ref, wx_ref, ba_ref, bx_ref, lam_ref, c_sc):
    cw = cw_ref[...]
    conv = cb_ref[...] + cw[3:4] * x
    head = cb_ref[...] + cw[3:4] * x[0:8]
    for d in (1, 2, 3):
        rolled, hd = _shift_down(x, prev8, d)
        conv = conv + cw[3 - d:4 - d] * rolled
        head = head + cw[3 - d:4 - d] * hd
    c_sc[...] = conv
    c_sc[0:8, :] = head
    c = c_sc[...]
    zr, zi = [], []
    for g in range(LRU_BLOCKS):
        sl = slice(g * LRU_BLOCK, (g + 1) * LRU_BLOCK)
        cg = c[:, sl].astype(BF16)
        zr.append(_dot(cg, wa_ref[g]))
        zi.append(_dot(cg, wx_ref[g]))
    r = _sigmoid(jnp.concatenate(zr, axis=1) + ba_ref[...])
    gate_i = _sigmoid(jnp.concatenate(zi, axis=1) + bx_ref[...])
    sp = _softplus(-lam_ref[...])
    log_a = (-LRU_C) * r * sp
    a = jnp.exp(log_a)
    mult = jnp.sqrt(_one_minus_square(a, log_a))
    return c, r, gate_i, a, mult, log_a


def _lru_fwd(proj, conv_w, conv_b, wa, wx, ba, bx, lam, riders=()):
    rows = proj.shape[0]
    TM = _lru_tile(rows)
    nt = rows // TM

    def body(x_ref, cw_ref, cb_ref, wa_ref, wx_ref, ba_ref, bx_ref, lam_ref,
             h_ref, c_ref, r_ref, i_ref, la_ref, mult_ref, prev_sc, carry_sc, c_sc, a_sc, u_sc, h_sc):
        i = pl.program_id(0)

        @pl.when(i == 0)
        def _():
            prev_sc[...] = jnp.zeros_like(prev_sc)
            carry_sc[...] = jnp.zeros_like(carry_sc)

        x = x_ref[...].astype(F32)
        c, r, gate_i, a, mult, log_a = _conv_and_gates(x, prev_sc[...], cw_ref, cb_ref, wa_ref, wx_ref, ba_ref, bx_ref,
                                                       lam_ref, c_sc)
        for ref, val in ((c_ref, c), (r_ref, r), (i_ref, gate_i), (la_ref, log_a), (mult_ref, mult)):
            ref[...] = val.astype(BF16)
        prev_sc[...] = x[TM - 8:TM]
        valid = _row_ids(i, TM, (TM, D)) >= PAD_ROWS
        a_sc[...] = a
        u_sc[...] = jnp.where(valid, mult * gate_i * c, 0.0)
        row8 = lax.broadcasted_iota(jnp.int32, (8, D), 0)

        def group(gi, hprev):
            r0 = pl.multiple_of(gi * 8, 8)
            aa = a_sc[pl.ds(r0, 8), :]
            uu = u_sc[pl.ds(r0, 8), :]
            for d in (1, 2, 4):
                a_sh = jnp.where(row8 >= d, pltpu.roll(aa, d, axis=0), 1.0)
                u_sh = jnp.where(row8 >= d, pltpu.roll(uu, d, axis=0), 0.0)
                uu = uu + aa * u_sh
                aa = aa * a_sh
            hb = aa * hprev + uu
            h_sc[pl.ds(r0, 8), :] = hb
            return hb[7:8, :]

        hlast = lax.fori_loop(0, TM // 8, group, carry_sc[0:1, :])
        carry_sc[0:1, :] = hlast
        h_ref[...] = h_sc[...].astype(BF16)

    vec = pl.BlockSpec((1, D), lambda i: (0, 0))
    wspec = pl.BlockSpec((LRU_BLOCKS, LRU_BLOCK, LRU_BLOCK), lambda i: (0, 0, 0))
    return _hosted_call(
        body,
        name="lru_fwd",
        grid=(nt,),
        in_specs=[
            pl.BlockSpec((TM, D), lambda i: (i, LRU_IN_COL)),
            pl.BlockSpec((4, D), lambda i: (0, 0)),
            vec, wspec, wspec, vec, vec, vec,
        ],
        out_specs=[pl.BlockSpec((TM, D), lambda i: (i, 0)) for _ in range(6)],
        out_shape=[jax.ShapeDtypeStruct((rows, D), BF16) for _ in range(6)],
        scratch_shapes=[
            pltpu.VMEM((8, D), F32), pltpu.VMEM((8, D), F32),
            pltpu.VMEM((TM, D), F32), pltpu.VMEM((TM, D), F32), pltpu.VMEM((TM, D), F32), pltpu.VMEM((TM, D), F32),
        ],
        args=(proj, conv_w, conv_b, wa, wx, ba, bx, lam),
        riders=riders,
    )


def _group_norm(o):
    outs, rstds = [], []
    for h in range(HEADS):
        oh = o[:, h * DH:(h + 1) * DH]
        rs = lax.rsqrt(jnp.mean(oh * oh, axis=-1, keepdims=True) + EPS)
        outs.append(oh * rs)
        rstds.append(rs)
    return jnp.concatenate(outs, axis=1), rstds


def _frame_rows(head_hbm, x_hbm, buf, sems, i, tm, nt):
    slot = i % 2

    @pl.when(i == 0)
    def _():
        first = [pltpu.make_async_copy(head_hbm, buf.at[0, pl.ds(0, FRONT)], sems.at[2]),
                 pltpu.make_async_copy(x_hbm.at[pl.ds(0, tm - FRONT)], buf.at[0, pl.ds(FRONT, tm - FRONT)], sems.at[0])]
        for cp in first:
            cp.start()
        for cp in first:
            cp.wait()

    @pl.when(i + 1 < nt)
    def _():
        start = pl.multiple_of((i + 1) * tm - FRONT, LANES)
        pltpu.make_async_copy(x_hbm.at[pl.ds(start, tm)], buf.at[1 - slot], sems.at[1 - slot]).start()

    @pl.when(i > 0)
    def _():
        pltpu.make_async_copy(x_hbm.at[pl.ds(0, tm)], buf.at[slot], sems.at[slot]).wait()

    return slot


def _mix_fwd(head, x2d, o, proj, h_lru, w_br, w_bl, w_o, ffn_norm_w, riders=()):
    rows = o.shape[0]
    tm = TM_MIX_FWD
    assert rows % tm == 0 and tm > FRONT
    nt = rows // tm

    def body(head_hbm, x_hbm, o_ref, gates_ref, hl_ref, wbr_ref, wbl_ref, wo_ref, nw_ref,
             yret_ref, ylru_ref, h1_ref, u2_ref, rstd_ref, h0_sc, h0_sem):
        i = pl.program_id(0)
        slot = _frame_rows(head_hbm, x_hbm, h0_sc, h0_sem, i, tm, nt)
        gate = lambda j: gates_ref[:, j * D:(j + 1) * D].astype(F32)
        on, _ = _group_norm(o_ref[...].astype(F32))
        gret = gate(0)
        a_ret = (gret * _sigmoid(gret) * on).astype(BF16)
        y_ret = _dot(a_ret, wbr_ref[...])
        gl, _ = _gelu_and_grad(gate(1))
        a_lru = (gl * hl_ref[...].astype(F32)).astype(BF16)
        y_lru = _dot(a_lru, wbl_ref[...])
        mixed = (_sigmoid(gate(2)) * y_ret + _sigmoid(gate(3)) * y_lru).astype(BF16)
        delta = _dot(mixed, wo_ref[...])
        yret_ref[...] = y_ret.astype(BF16)
        ylru_ref[...] = y_lru.astype(BF16)
        h1 = h0_sc[slot] + delta
        rs = lax.rsqrt(jnp.mean(h1 * h1, axis=-1, keepdims=True) + EPS)
        h1_ref[...] = h1
        u2_ref[...] = ((h1 * rs) * nw_ref[...]).astype(BF16)
        rstd_ref[...] = rs

    tile = lambda col: pl.BlockSpec((tm, D), lambda i: (i, col))
    wspec = pl.BlockSpec((D, D), lambda i: (0, 0))
    return _hosted_call(
        body,
        name="mix_fwd",
        grid=(nt,),
        in_specs=[
            _ANY, _ANY,
            tile(0), pl.BlockSpec((tm, 4 * D), lambda i: (i, GATES_COL)), tile(0),
            wspec, wspec, wspec,
            pl.BlockSpec((1, D), lambda i: (0, 0)),
        ],
        out_specs=[tile(0), tile(0), tile(0), tile(0), pl.BlockSpec((tm, 1), lambda i: (i, 0))],
        out_shape=[
            jax.ShapeDtypeStruct((rows, D), BF16), jax.ShapeDtypeStruct((rows, D), BF16),
            jax.ShapeDtypeStruct((rows, D), F32), jax.ShapeDtypeStruct((rows, D), BF16),
            jax.ShapeDtypeStruct((rows, 1), F32),
        ],
        scratch_shapes=[pltpu.VMEM((2, tm, D), F32), pltpu.SemaphoreType.DMA((3,))],
        args=(head, x2d, o, proj, h_lru, w_br, w_bl, w_o, ffn_norm_w),
        riders=riders,
    )


def _ffn_fwd_gate(u2, w_ffn_in, riders=()):
    rows = u2.shape[0]
    tm = _heavy_tile(rows)
    hid = lambda: pl.BlockSpec((tm, FFN), lambda i: (i, 0))

    def body(u2_ref, w_hbm, silu_ref, dsilu_ref, act_ref, w_sc, w_sem):
        @pl.when(pl.program_id(0) == 0)
        def _():
            cp = pltpu.make_async_copy(w_hbm, w_sc, w_sem)
            cp.start()
            cp.wait()

        u2 = u2_ref[...]
        for a, b in _FFN_SUBS:
            g = _dot(u2, w_sc[:, a:b])
            up = _dot(u2, w_sc[:, FFN + a:FFN + b])
            sg = _sigmoid(g)
            silu = g * sg
            silu_ref[:, a:b] = silu.astype(BF16)
            dsilu_ref[:, a:b] = (up * (sg * (1.0 + g * (1.0 - sg)))).astype(BF16)
            act_ref[:, a:b] = (silu * up).astype(BF16)

    return _hosted_call(
        body,
        name="ffn_fwd_gate",
        grid=(rows // tm,),
        in_specs=[pl.BlockSpec((tm, D), lambda i: (i, 0)), _ANY],
        out_specs=[hid(), hid(), hid()],
        out_shape=[jax.ShapeDtypeStruct((rows, FFN), BF16)] * 3,
        scratch_shapes=[pltpu.VMEM((D, 2 * FFN), BF16), pltpu.SemaphoreType.DMA],
        args=(u2, w_ffn_in),
        riders=riders,
    )


def _ffn_out_loss(act, h1, w_ffn_out, target, final_norm_w):
    rows = act.shape[0]
    tm = _heavy_tile(rows)
    nt = rows // tm

    def body(act_ref, h1_ref, wo_ref, fnw_ref, tgt_hbm, dh2_ref, stats_ref, tgt_sc, tgt_sem):
        i = pl.program_id(0)
        slot = i % 2

        @pl.when(i == 0)
        def _():
            stats_ref[...] = jnp.zeros_like(stats_ref)
            tgt_sc[0, 0:FRONT, :] = jnp.zeros((FRONT, D), F32)
            cp = pltpu.make_async_copy(tgt_hbm.at[pl.ds(0, tm - FRONT)], tgt_sc.at[0, pl.ds(FRONT, tm - FRONT)], tgt_sem.at[0])
            cp.start()
            cp.wait()

        @pl.when(i + 1 < nt)
        def _():
            start = pl.multiple_of((i + 1) * tm - FRONT, FRONT)
            pltpu.make_async_copy(tgt_hbm.at[pl.ds(start, tm)], tgt_sc.at[1 - slot], tgt_sem.at[1 - slot]).start()

        @pl.when(i > 0)
        def _():
            pltpu.make_async_copy(tgt_hbm.at[pl.ds(0, tm)], tgt_sc.at[slot], tgt_sem.at[slot]).wait()

        h2 = h1_ref[...] + _dot(act_ref[...], wo_ref[...])
        rs = lax.rsqrt(jnp.mean(h2 * h2, axis=-1, keepdims=True) + EPS)
        n = h2 * rs
        fnw = fnw_ref[...]
        valid = _row_ids(i, tm, (tm, D)) >= FRONT
        diff = jnp.where(valid, n * fnw - tgt_sc[slot], 0.0)
        dy = diff * (1.0 / D)
        stats_ref[0:1, :] += (0.5 / D) * jnp.sum(diff * diff, axis=0, keepdims=True)
        stats_ref[1:2, :] += jnp.sum(dy * n, axis=0, keepdims=True)
        dn = dy * fnw
        dh2_ref[...] = rs * (dn - n * jnp.mean(dn * n, axis=-1, keepdims=True))

    return pl.pallas_call(
        body,
        name="ffn_out_loss",
        grid=(nt,),
        in_specs=[
            pl.BlockSpec((tm, FFN), lambda i: (i, 0)),
            pl.BlockSpec((tm, D), lambda i: (i, 0)),
            pl.BlockSpec((FFN, D), lambda i: (0, 0)),
            pl.BlockSpec((1, D), lambda i: (0, 0)),
            _ANY,
        ],
        out_specs=[pl.BlockSpec((tm, D), lambda i: (i, 0)), pl.BlockSpec((8, D), lambda i: (0, 0))],
        out_shape=[jax.ShapeDtypeStruct((rows, D), F32), jax.ShapeDtypeStruct((8, D), F32)],
        scratch_shapes=[pltpu.VMEM((2, tm, D), F32), pltpu.SemaphoreType.DMA((2,))],
        compiler_params=_params(("arbitrary",)),
    )(act, h1, w_ffn_out, final_norm_w, target)


def _ffn_bwd(dh2, silu, dsilu, h1, rstd2, w_ffn_in, w_ffn_out, ffn_norm_w):
    rows = dh2.shape[0]
    tm = _heavy_tile(rows)
    blk = lambda: pl.BlockSpec((tm, FFN), lambda i: (i, 0))

    def gate_body(dh2_ref, silu_ref, dsilu_ref, wo_hbm, dg_ref, dup_ref, dh2b_ref, wo_sc, wo_sem):
        @pl.when(pl.program_id(0) == 0)
        def _():
            cp = pltpu.make_async_copy(wo_hbm, wo_sc, wo_sem)
            cp.start()
            cp.wait()

        dh2b = dh2_ref[...].astype(BF16)
        dh2b_ref[...] = dh2b
        for a, b in _FFN_SUBS:
            dact = _dot_nt(dh2b, wo_sc[a:b, :])
            dup_ref[:, a:b] = (dact * silu_ref[:, a:b].astype(F32)).astype(BF16)
            dg_ref[:, a:b] = (dact * dsilu_ref[:, a:b].astype(F32)).astype(BF16)

    dg, dup, dh2b = pl.pallas_call(
        gate_body,
        name="ffn_bwd_gate",
        grid=(rows // tm,),
        in_specs=[pl.BlockSpec((tm, D), lambda i: (i, 0)), blk(), blk(), _ANY],
        out_specs=[blk(), blk(), pl.BlockSpec((tm, D), lambda i: (i, 0))],
        out_shape=[jax.ShapeDtypeStruct((rows, FFN), BF16)] * 2 + [jax.ShapeDtypeStruct((rows, D), BF16)],
        scratch_shapes=[pltpu.VMEM((FFN, D), BF16), pltpu.SemaphoreType.DMA],
        compiler_params=_params(("arbitrary",)),
    )(dh2, silu, dsilu, w_ffn_out)

    def body(dg_ref, dup_ref, dh2_ref, h1_ref, rstd_ref, w_hbm, nw_ref, dh1_ref, stats_ref, w_sc, w_sem):
        @pl.when(pl.program_id(0) == 0)
        def _():
            stats_ref[...] = jnp.zeros_like(stats_ref)
            cp = pltpu.make_async_copy(w_hbm, w_sc, w_sem)
            cp.start()
            cp.wait()

        du = _dot_nt(dg_ref[...], w_sc[:, 0:FFN]) + _dot_nt(dup_ref[...], w_sc[:, FFN:2 * FFN])
        rs = rstd_ref[...]
        n = h1_ref[...] * rs
        stats_ref[0:1, :] += jnp.sum(du * n, axis=0, keepdims=True)
        dn = du * nw_ref[...]
        dh1_ref[...] = dh2_ref[...] + rs * (dn - n * jnp.mean(dn * n, axis=-1, keepdims=True))

    row = lambda width: pl.BlockSpec((tm, width), lambda i: (i, 0))
    dh1, stats = pl.pallas_call(
        body,
        name="ffn_bwd_in",
        grid=(rows // tm,),
        in_specs=[row(FFN), row(FFN), row(D), row(D), row(1), _ANY, pl.BlockSpec((1, D), lambda i: (0, 0))],
        out_specs=[row(D), pl.BlockSpec((8, D), lambda i: (0, 0))],
        out_shape=[jax.ShapeDtypeStruct((rows, D), F32), jax.ShapeDtypeStruct((8, D), F32)],
        scratch_shapes=[pltpu.VMEM((D, 2 * FFN), BF16), pltpu.SemaphoreType.DMA],
        compiler_params=_params(("arbitrary",)),
    )(dg, dup, dh2, h1, rstd2, w_ffn_in, ffn_norm_w)
    return dg, dup, dh2b, dh1, stats


def _mix_bwd(dh1, o, proj, h_lru, y_ret, y_lru, w_br, w_bl, w_o, riders=()):
    rows = dh1.shape[0]
    tm = TM_MIX_BWD
    nt = rows // tm

    def body(dh1_ref, o_ref, gates_ref, hl_ref, yret_ref, ylru_ref, wbr_hbm, wbl_hbm, wo_hbm,
             dproj_ref, do_ref, dhl_ref, mixed_ref, aret_ref, alru_ref, dyret_ref, dylru_ref, w_sc, w_sem):
        @pl.when(pl.program_id(0) == 0)
        def _():
            copies = [pltpu.make_async_copy(src, w_sc.at[q], w_sem.at[q]) for q, src in enumerate((wbr_hbm, wbl_hbm, wo_hbm))]
            for cp in copies:
                cp.start()
            for cp in copies:
                cp.wait()

        wbr_ref, wbl_ref, wo_ref = w_sc.at[0], w_sc.at[1], w_sc.at[2]
        gate = lambda j: gates_ref[:, j * D:(j + 1) * D].astype(F32)
        dmixed = _dot_nt(dh1_ref[...].astype(BF16), wo_ref[...])
        y_ret, y_lru = yret_ref[...].astype(F32), ylru_ref[...].astype(F32)
        sa, sb = _sigmoid(gate(2)), _sigmoid(gate(3))
        mixed_ref[...] = (sa * y_ret + sb * y_lru).astype(BF16)
        dga = dmixed * y_ret * sa * (1.0 - sa)
        dgb = dmixed * y_lru * sb * (1.0 - sb)
        dy_ret = (dmixed * sa).astype(BF16)
        dy_lru = (dmixed * sb).astype(BF16)
        dyret_ref[...] = dy_ret
        dylru_ref[...] = dy_lru
        da_ret = _dot_nt(dy_ret, wbr_ref[...])
        da_lru = _dot_nt(dy_lru, wbl_ref[...])

        gret = gate(0)
        sg = _sigmoid(gret)
        silu = gret * sg
        on, rstds = _group_norm(o_ref[...].astype(F32))
        aret_ref[...] = (silu * on).astype(BF16)
        dgret = da_ret * on * (sg * (1.0 + gret * (1.0 - sg)))
        don = da_ret * silu
        for h in range(HEADS):
            sl = slice(h * DH, (h + 1) * DH)
            onh, donh = on[:, sl], don[:, sl]
            do_ref[:, sl] = (rstds[h] * (donh - onh * jnp.mean(donh * onh, axis=-1, keepdims=True))).astype(BF16)

        gl, gl_grad = _gelu_and_grad(gate(1))
        hl = hl_ref[...].astype(F32)
        alru_ref[...] = (gl * hl).astype(BF16)
        dlgate = da_lru * hl * gl_grad
        dhl_ref[...] = (da_lru * gl).astype(BF16)

        for j, val in enumerate((dgret, dlgate, dga, dgb)):
            dproj_ref[:, j * D:(j + 1) * D] = val.astype(BF16)

    tile = lambda col: pl.BlockSpec((tm, D), lambda i: (i, col))
    gates = pl.BlockSpec((tm, 4 * D), lambda i: (i, GATES_COL))
    bf = lambda: jax.ShapeDtypeStruct((rows, D), BF16)
    return _hosted_call(
        body,
        name="mix_bwd",
        grid=(nt,),
        in_specs=[tile(0), tile(0), gates, tile(0), tile(0), tile(0), _ANY, _ANY, _ANY],
        out_specs=[pl.BlockSpec((tm, 4 * D), lambda i: (i, GATES_COL))] + [tile(0)] * 7,
        out_shape=[jax.ShapeDtypeStruct((rows, IN_COLS), BF16)] + [bf() for _ in range(7)],
        scratch_shapes=[pltpu.VMEM((3, D, D), BF16), pltpu.SemaphoreType.DMA((3,))],
        args=(dh1, o, proj, h_lru, y_ret, y_lru, w_br, w_bl, w_o),
        riders=riders,
    )


def _retention_bwd(dproj, proj_bf, do, sprev, intra, intra_t, q_dec, k_dec, c_dec, cos_t, sin_t, riders=()):
    rows = proj_bf.shape[0]
    nc = rows // CHUNK

    def body(dproj_in_ref, q_ref, k_ref, v_ref, do_ref, sprev_ref, m_ref, mt_ref, qd_ref, kd_ref, cd_ref, cos_ref, sin_ref,
             dproj_ref, ds_sc):
        @pl.when(pl.program_id(0) == 0)
        def _():
            ds_sc[...] = jnp.zeros_like(ds_sc)

        cos, sin = cos_ref[...], sin_ref[...]

        def unrotate(dy):
            return dy * cos - pltpu.roll(dy, DH // 2, axis=1) * sin

        for h in range(HEADS):
            sl = slice(h * DH, (h + 1) * DH)
            q, k, v = q_ref[:, sl], k_ref[:, sl], v_ref[:, sl]
            do = do_ref[:, sl]
            dob = do.astype(BF16)
            doq = (do * qd_ref[:, sl]).astype(BF16)
            state_prev = sprev_ref[0, h]
            dstate = ds_sc[h]
            dstate_b = dstate.astype(BF16)
            s_t = (_dot_nt(k, q) * mt_ref[h]).astype(BF16)
            ds_t = (_dot_nt(v, dob) * mt_ref[h]).astype(BF16)
            ds = (_dot_nt(dob, v) * m_ref[h]).astype(BF16)
            kd = kd_ref[:, sl]
            dq = _dot(ds, k) + _dot_nt(doq, state_prev)
            dk = _dot(ds_t, q) + _dot_nt(v, dstate_b) * kd
            k_scaled = (k.astype(F32) * kd).astype(BF16)
            dv = _dot(s_t, dob) + _dot(k_scaled, dstate_b)
            ds_sc[h] = dstate * cd_ref[:, sl] + _dot_tn(q, doq)
            for part, val in enumerate((unrotate(dq), unrotate(dk) * QK_SCALE, dv)):
                dproj_ref[:, part * D + h * DH:part * D + (h + 1) * DH] = val.astype(BF16)

    rev = lambda c: nc - 1 - c
    chunk_spec = lambda col: pl.BlockSpec((CHUNK, D), lambda c: (rev(c), col))
    const2 = lambda shape: pl.BlockSpec(shape, lambda c: (0, 0))
    const3 = pl.BlockSpec((HEADS, CHUNK, CHUNK), lambda c: (0, 0, 0))
    return _hosted_call(
        body,
        name="retention_bwd",
        grid=(nc,),
        in_specs=[
            pl.BlockSpec(memory_space=pl.ANY),
            chunk_spec(0), chunk_spec(1), chunk_spec(2), chunk_spec(0),
            pl.BlockSpec((1, HEADS, DH, DH), lambda c: (rev(c), 0, 0, 0)),
            const3, const3,
            const2((CHUNK, D)), const2((CHUNK, D)), const2((1, D)),
            pl.BlockSpec((CHUNK, DH), lambda c: (rev(c), 0)),
            pl.BlockSpec((CHUNK, DH), lambda c: (rev(c), 0)),
        ],
        out_specs=[pl.BlockSpec((CHUNK, 3 * D), lambda c: (rev(c), 0))],
        out_shape=[jax.ShapeDtypeStruct(dproj.shape, BF16)],
        aliases={0: 0},
        scratch_shapes=[pltpu.VMEM((HEADS, DH, DH), F32)],
        args=(dproj, proj_bf, proj_bf, proj_bf, do, sprev, intra, intra_t, q_dec, k_dec, c_dec, cos_t, sin_t),
        riders=riders,
    )


def _lru_bwd(dproj, proj, saved, dhl, conv_w, wa, wx, lam, riders=()):
    rows = proj.shape[0]
    TM = _lru_tile(rows)
    nt = rows // TM
    per8 = TM // 8

    def body(dproj_in_ref, x_ref, xprev_ref, h_ref, hprev_ref, c_ref, r_ref, i_ref, la_ref, mult_ref, dhl_ref,
             cw_ref, wa_ref, wx_ref, lam_ref,
             dproj_ref, dwa_ref, dwx_ref, stats_ref, anext_sc, dhnext_sc, dcnext_sc, c_sc, b_sc, dh_sc):
        step = pl.program_id(0)
        i = nt - 1 - step

        @pl.when(step == 0)
        def _():
            anext_sc[...] = jnp.zeros_like(anext_sc)
            dhnext_sc[...] = jnp.zeros_like(dhnext_sc)
            dcnext_sc[...] = jnp.zeros_like(dcnext_sc)
            dwa_ref[...] = jnp.zeros_like(dwa_ref)
            dwx_ref[...] = jnp.zeros_like(dwx_ref)
            stats_ref[...] = jnp.zeros_like(stats_ref)

        first = i == 0
        x = x_ref[...].astype(F32)
        prev8 = jnp.where(first, 0.0, xprev_ref[8:16, :].astype(F32))
        c_b = c_ref[...]
        c, r, gate_i, mult = (ref[...].astype(F32) for ref in (c_ref, r_ref, i_ref, mult_ref))
        a = jnp.exp(la_ref[...].astype(F32))
        sp = _softplus(-lam_ref[...])
        dh_sc[...] = dhl_ref[...].astype(F32)

        b_sc[...] = pltpu.roll(a, TM - 1, axis=0)
        b_sc[TM - 1:TM, :] = anext_sc[0:1, :]
        anext_sc[0:1, :] = a[0:1, :]
        row8 = lax.broadcasted_iota(jnp.int32, (8, D), 0)

        def group(gi, dhnext):
            r0 = pl.multiple_of((per8 - 1 - gi) * 8, 8)
            bb = b_sc[pl.ds(r0, 8), :]
            uu = dh_sc[pl.ds(r0, 8), :]
            for d in (1, 2, 4):
                b_sh = jnp.where(row8 < 8 - d, pltpu.roll(bb, 8 - d, axis=0), 1.0)
                u_sh = jnp.where(row8 < 8 - d, pltpu.roll(uu, 8 - d, axis=0), 0.0)
                uu = uu + bb * u_sh
                bb = bb * b_sh
            dhb = bb * dhnext + uu
            dh_sc[pl.ds(r0, 8), :] = dhb
            return dhb[0:1, :]

        dhfirst = lax.fori_loop(0, per8, group, dhnext_sc[0:1, :])
        dhnext_sc[0:1, :] = dhfirst
        dh = dh_sc[...]

        h = h_ref[...].astype(F32)
        hprev8 = jnp.where(first, 0.0, hprev_ref[8:16, :].astype(F32))
        h_dn, h_head = _shift_down(h, hprev8, 1)
        c_sc[...] = h_dn
        c_sc[0:8, :] = h_head
        h_before = c_sc[...]

        valid = _row_ids(i, TM, (TM, D)) >= PAD_ROWS
        da = dh * h_before
        du = jnp.where(valid, dh, 0.0)
        dmult = du * gate_i * c
        dgate_i = du * mult * c
        dc = du * mult * gate_i
        dla = da * a - dmult * (a * a) / mult
        dla = jnp.where(valid, dla, 0.0)
        dr = dla * ((-LRU_C) * sp)
        dzr = dr * r * (1.0 - r)
        dzi = dgate_i * gate_i * (1.0 - gate_i)
        stats_ref[1:2, :] += jnp.sum(dzr, axis=0, keepdims=True)
        stats_ref[2:3, :] += jnp.sum(dzi, axis=0, keepdims=True)
        stats_ref[3:4, :] += jnp.sum(dla * ((-LRU_C) * r), axis=0, keepdims=True)
        dc_gate = []
        for g in range(LRU_BLOCKS):
            sl = slice(g * LRU_BLOCK, (g + 1) * LRU_BLOCK)
            cg = c_b[:, sl]
            dzr_g = dzr[:, sl].astype(BF16)
            dzi_g = dzi[:, sl].astype(BF16)
            dc_gate.append(_dot_nt(dzr_g, wa_ref[g]) + _dot_nt(dzi_g, wx_ref[g]))
            dwa_ref[g] += _dot_tn(cg, dzr_g)
            dwx_ref[g] += _dot_tn(cg, dzi_g)
        dc = dc + jnp.concatenate(dc_gate, axis=1)

        cw = cw_ref[...]
        stats_ref[0:1, :] += jnp.sum(dc, axis=0, keepdims=True)
        stats_ref[7:8, :] += jnp.sum(dc * x, axis=0, keepdims=True)
        dx = cw[3:4] * dc
        tail_src = jnp.concatenate([dc[TM - 8:TM], dcnext_sc[...]], axis=0)
        dx_tail = cw[3:4] * dc[TM - 8:TM]
        for d in (1, 2, 3):
            dx = dx + cw[3 - d:4 - d] * pltpu.roll(dc, TM - d, axis=0)
            dx_tail = dx_tail + cw[3 - d:4 - d] * pltpu.roll(tail_src, 16 - d, axis=0)[0:8]
            rolled, hd = _shift_down(x, prev8, d)
            b_sc[...] = rolled
            b_sc[0:8, :] = hd
            stats_ref[7 - d:8 - d, :] += jnp.sum(dc * b_sc[...], axis=0, keepdims=True)
        dcnext_sc[...] = dc[0:8]
        dproj_ref[...] = dx.astype(BF16)
        dproj_ref[TM - 8:TM, :] = dx_tail.astype(BF16)

    rev = lambda s: nt - 1 - s
    vec = pl.BlockSpec((1, D), lambda s: (0, 0))
    wspec = pl.BlockSpec((LRU_BLOCKS, LRU_BLOCK, LRU_BLOCK), lambda s: (0, 0, 0))
    prev16 = lambda col: pl.BlockSpec((BF16_ROWS, D), lambda s: (jnp.maximum(rev(s) * (TM // BF16_ROWS) - 1, 0), col))
    tile = lambda: pl.BlockSpec((TM, D), lambda s: (rev(s), 0))
    h_lru, c_sv, r_sv, i_sv, la_sv, mult_sv = saved
    return _hosted_call(
        body,
        name="lru_bwd",
        grid=(nt,),
        in_specs=[
            pl.BlockSpec(memory_space=pl.ANY),
            pl.BlockSpec((TM, D), lambda s: (rev(s), LRU_IN_COL)), prev16(LRU_IN_COL),
            tile(), prev16(0),
            tile(), tile(), tile(), tile(), tile(), tile(),
            pl.BlockSpec((4, D), lambda s: (0, 0)),
            wspec, wspec, vec,
        ],
        out_specs=[
            pl.BlockSpec((TM, D), lambda s: (rev(s), LRU_IN_COL)),
            wspec, wspec,
            pl.BlockSpec((8, D), lambda s: (0, 0)),
        ],
        out_shape=[
            jax.ShapeDtypeStruct(dproj.shape, BF16),
            jax.ShapeDtypeStruct((LRU_BLOCKS, LRU_BLOCK, LRU_BLOCK), F32),
            jax.ShapeDtypeStruct((LRU_BLOCKS, LRU_BLOCK, LRU_BLOCK), F32),
            jax.ShapeDtypeStruct((8, D), F32),
        ],
        aliases={0: 0},
        scratch_shapes=[
            pltpu.VMEM((8, D), F32), pltpu.VMEM((8, D), F32), pltpu.VMEM((8, D), F32),
            pltpu.VMEM((TM, D), F32), pltpu.VMEM((TM, D), F32), pltpu.VMEM((TM, D), F32),
        ],
        args=(dproj, proj, proj, h_lru, h_lru, c_sv, r_sv, i_sv, la_sv, mult_sv, dhl, conv_w, wa, wx, lam),
        riders=riders,
    )


def _in_proj_bwd(dproj, w_in, part, prev=None, riders=()):
    rows = dproj.shape[0]
    tm = _heavy_tile(rows)
    nt = rows // tm
    split = max(1, nt // 4 + 1) if nt > 1 else 1
    first = 0 if part == 0 else split
    count = split if part == 0 else nt - split

    def body(*refs):
        dproj_ref, w_hbm, du_ref, w_sc, w_sem = refs[-5:]

        @pl.when(pl.program_id(0) == 0)
        def _():
            moves = [((0, 3), 0), ((4, 1), 3), ((3, 1), 4), ((5, 3), 5)]
            copies = [pltpu.make_async_copy(w_hbm.at[:, pl.ds(src * D, n * D)], w_sc.at[:, pl.ds(dst * D, n * D)], w_sem.at[q])
                      for q, ((src, n), dst) in enumerate(moves)]
            for cp in copies:
                cp.start()
            for cp in copies:
                cp.wait()

        du_ref[...] = _dot_nt(dproj_ref[...], w_sc[...])

    in_specs = [pl.BlockSpec((tm, IN_COLS), lambda i: (first + i, 0)), _ANY]
    args = (dproj, w_in)
    if prev is not None:
        in_specs = [_ANY] + in_specs
        args = (prev,) + args
    return _hosted_call(
        body,
        name="in_proj_bwd_%d" % part,
        grid=(count,),
        in_specs=in_specs,
        out_specs=[pl.BlockSpec((tm, D), lambda i: (first + i, 0))],
        out_shape=[jax.ShapeDtypeStruct((rows, D), F32)],
        scratch_shapes=[pltpu.VMEM((D, IN_COLS), BF16), pltpu.SemaphoreType.DMA((4,))],
        aliases={0: 0} if prev is not None else None,
        args=args,
        riders=riders,
    )


def _norm1_bwd(du, dh1, head, x2d, rstd1, norm_w, riders=()):
    rows = du.shape[0]

    def body(du_ref, dh1_ref, head_ref, x_ref, rstd_ref, nw_ref, gx_ref, ghead_ref, stats_ref):
        i = pl.program_id(0)

        @pl.when(i == 0)
        def _():
            stats_ref[...] = jnp.zeros_like(stats_ref)

        def finish(h0, out_ref):
            du = du_ref[...]
            rs = rstd_ref[...]
            n = h0 * rs
            stats_ref[0:1, :] += jnp.sum(du * n, axis=0, keepdims=True)
            dn = du * nw_ref[...]
            out_ref[...] = dh1_ref[...] + rs * (dn - n * jnp.mean(dn * n, axis=-1, keepdims=True))

        @pl.when(i == 0)
        def _():
            finish(head_ref[...], ghead_ref)

        @pl.when(i > 0)
        def _():
            finish(x_ref[...], gx_ref)

    tile = pl.BlockSpec((TM, D), lambda i: (i, 0))
    return _hosted_call(
        body,
        name="norm1_bwd",
        grid=(rows // TM,),
        in_specs=[
            tile, tile,
            pl.BlockSpec((FRONT, D), lambda i: (0, 0)),
            pl.BlockSpec((TM, D), lambda i: (jnp.maximum(i - 1, 0), 0)),
            pl.BlockSpec((TM, 1), lambda i: (i, 0)),
            pl.BlockSpec((1, D), lambda i: (0, 0)),
        ],
        out_specs=[
            pl.BlockSpec((TM, D), lambda i: (jnp.maximum(i - 1, 0), 0)),
            pl.BlockSpec((FRONT, D), lambda i: (0, 0)),
            pl.BlockSpec((8, D), lambda i: (0, 0)),
        ],
        out_shape=[
            jax.ShapeDtypeStruct(x2d.shape, F32),
            jax.ShapeDtypeStruct((FRONT, D), F32),
            jax.ShapeDtypeStruct((8, D), F32),
        ],
        scratch_shapes=[],
        args=(du, dh1, head, x2d, rstd1, norm_w),
        riders=riders,
    )


def _matmul_tn(name, x, dy, out_cols, col0=0, prev=None, k_block=None, n_block=None, riders=(), col_map=None):
    col_map = col_map or (lambda n: n)
    rows, kdim = x.shape
    ndim = dy.shape[1]
    kb = k_block or kdim
    nb = n_block or ndim
    step_bytes = lambda t: 2 * t * (kb * x.dtype.itemsize + nb * dy.dtype.itemsize) + 2 * kb * nb * 4
    tr = next(t for t in (2816, 1408, TM_HEAVY, TM) if rows % t == 0 and (t == TM or step_bytes(t) <= TN_VMEM_BUDGET))
    nr, nk, nn = rows // tr, kdim // kb, ndim // nb
    cb0 = col0 // nb

    def body(*refs):
        x_ref, dy_ref, out_ref = refs[-3], refs[-2], refs[-1]
        part = _dot_tn(x_ref[...].astype(BF16), dy_ref[...].astype(BF16))

        @pl.when(pl.program_id(2) == 0)
        def _():
            out_ref[...] = part

        @pl.when(pl.program_id(2) > 0)
        def _():
            out_ref[...] += part

    in_specs = [
        pl.BlockSpec((tr, kb), lambda n, k, r: (r, k)),
        pl.BlockSpec((tr, nb), lambda n, k, r: (r, n)),
    ]
    args = [x, dy]
    aliases = {}
    if prev is not None:
        in_specs = [pl.BlockSpec(memory_space=pl.ANY)] + in_specs
        args = [prev] + args
        aliases = {0: 0}
    (out,), rider_outs = _hosted_call(
        body,
        name=name,
        grid=(nn, nk, nr),
        in_specs=in_specs,
        out_specs=[pl.BlockSpec((kb, nb), lambda n, k, r: (k, cb0 + col_map(n)))],
        out_shape=[jax.ShapeDtypeStruct((kdim, out_cols), F32)],
        scratch_shapes=[],
        aliases=aliases,
        args=args,
        riders=riders,
    )
    return (out, rider_outs) if riders else out


def _local_step(x2d, target, w, plan):
    rows = FRONT + x2d.shape[0]
    head = jnp.concatenate([jnp.zeros((PAD_ROWS, D), F32), w["meta_tokens"]], axis=0)
    cos_t, sin_t = _rope_tables(rows)
    intra, intra_t, q_dec, k_dec, c_dec = _decay_tables()
    grads = {}

    def hosted(host, fn, *args, **kwargs):
        outs, rider_outs = fn(*args, riders=plan.riders(host, w, grads), **kwargs)
        plan.after(host, rider_outs, w, grads)
        return outs

    (u1, rstd1), _ = _norm1(head, x2d, w["mix_norm_w"])
    proj, w["w_in"] = hosted("in_proj", _in_proj, u1, w["w_in_shard"], w["route"], cos_t, sin_t)
    o, sprev = hosted("retention_fwd", _retention_fwd, proj, intra, q_dec, k_dec, c_dec)
    lru_args = (w["conv_w"], w["conv_b"], w["lru_wa"], w["lru_wx"], w["lru_ba"], w["lru_bx"], w["lru_lambda"])
    lru_saved = hosted("lru_fwd", _lru_fwd, proj, *lru_args)
    h_lru = lru_saved[0]
    y_ret, y_lru, h1, u2, rstd2 = hosted("mix_fwd", _mix_fwd, head, x2d, o, proj, h_lru, w["w_branch_ret"],
                                         w["w_branch_lru"], w["w_out"], w["ffn_norm_w"])
    silu, dsilu, act = hosted("ffn_fwd_gate", _ffn_fwd_gate, u2, w["w_ffn_in"])
    dh2, stats_loss = _ffn_out_loss(act, h1, w["w_ffn_out"], target, w["final_norm_w"])

    dg, dup, dh2b, dh1, stats_ffn = _ffn_bwd(dh2, silu, dsilu, h1, rstd2, w["w_ffn_in"], w["w_ffn_out"], w["ffn_norm_w"])
    grads["w_ffn_in"] = _matmul_tn("dw_ffn_up", u2, dup, 2 * FFN, col0=FFN, n_block=FFN_HALF,
                                   prev=_matmul_tn("dw_ffn_gate", u2, dg, 2 * FFN, n_block=FFN_HALF))
    grads["w_ffn_out"] = _matmul_tn("dw_ffn_out", act, dh2b, D, k_block=FFN_HALF)
    (dproj, do, dhl, mixed, a_ret, a_lru, dy_ret, dy_lru) = hosted(
        "mix_bwd", _mix_bwd, dh1, o, proj, h_lru, y_ret, y_lru, w["w_branch_ret"], w["w_branch_lru"], w["w_out"])
    grads["w_out"] = _matmul_tn("dw_out", mixed, dh1, D)
    grads["w_branch_ret"] = _matmul_tn("dw_branch_ret", a_ret, dy_ret, D)
    grads["w_branch_lru"] = _matmul_tn("dw_branch_lru", a_lru, dy_lru, D)
    (dproj,) = hosted("retention_bwd", _retention_bwd, dproj, proj, do, sprev, intra, intra_t, q_dec, k_dec, c_dec,
                      cos_t, sin_t)
    dproj, grads["lru_wa"], grads["lru_wx"], stats_lru = hosted(
        "lru_bwd", _lru_bwd, dproj, proj, lru_saved, dhl, w["conv_w"], w["lru_wa"], w["lru_wx"], w["lru_lambda"])
    grads["w_in"], rider_outs = _matmul_tn("dw_in", u1, dproj, IN_COLS, n_block=D, col_map=_swap_3_4,
                                           riders=plan.riders("dw_in", w, grads))
    plan.after("dw_in", rider_outs, w, grads)
    (du1,) = hosted("in_proj_bwd_0", _in_proj_bwd, dproj, w["w_in"], 0)
    (du1,) = hosted("in_proj_bwd_1", _in_proj_bwd, dproj, w["w_in"], 1, du1)
    (grad_x, grad_head, stats_in), _ = _norm1_bwd(du1, dh1, head, x2d, rstd1, w["mix_norm_w"])
    return grad_x, grad_head, grads, [stats_loss, stats_ffn, stats_in, stats_lru]


BIG_PIECES = {
    "w_in": ("col", (D, 2 * D)),
    "w_ffn_in": ("col", (D, FFN_HALF)),
    "w_ffn_out": ("row", (FFN // 4, D)),
    "w_branch_ret": ("row", (D // 4, D)),
    "w_branch_lru": ("row", (D // 4, D)),
    "w_out": ("row", (D // 4, D)),
    "lru_wa": ("lru", (LRU_BLOCKS, LRU_BLOCK // 4, LRU_BLOCK)),
    "lru_wx": ("lru", (LRU_BLOCKS, LRU_BLOCK // 4, LRU_BLOCK)),
}
SMALL_PIECES = {"meta_tokens": ("col", (N_META, D // 4)), "conv_w": ("col", (4, D // 4))}


def _full_shape(kind, shard):
    if kind == "col":
        return (shard[0], 4 * shard[1])
    if kind == "row":
        return (4 * shard[0], shard[1])
    return (shard[0], 4 * shard[1], shard[2])


def _half_shape(kind, shard):
    return (shard[0] // 2,) + tuple(shard[1:])


def _aligned(start, multiple):
    return start if isinstance(start, int) else pl.multiple_of(start, multiple)


def _lead(h, size):
    if h is None:
        return pl.ds(0, size)
    return pl.ds(_aligned(h * (size // 2), size // 2), size // 2)


def _full_region(ref, kind, shard, s, h):
    if kind == "col":
        return ref.at[_lead(h, shard[0]), pl.ds(_aligned(s * shard[1], shard[1]), shard[1])]
    if kind == "row":
        size = shard[0] if h is None else shard[0] // 2
        start = s * shard[0] + (0 if h is None else h * (shard[0] // 2))
        return ref.at[pl.ds(_aligned(start, BF16_ROWS), size), :]
    return ref.at[_lead(h, shard[0]), pl.ds(_aligned(s * shard[1], shard[1]), shard[1]), :]


def _shard_region(ref, shard, h):
    return ref.at[_lead(h, shard[0])]


def _place():
    x, y, c = lax.axis_index("x"), lax.axis_index("y"), lax.axis_index("c")
    return x, y, c, 2 * x + y


def _other_chip(s, c, k):
    s2 = jnp.bitwise_xor(s, k)
    return s2, (s2 // 2, s2 % 2, c)


def _remote(src, dst, send_sem, recv_sem, dev):
    return pltpu.make_async_remote_copy(src_ref=src, dst_ref=dst, send_sem=send_sem, recv_sem=recv_sem,
                                        device_id=dev, device_id_type=MESH)


_ANY = pl.BlockSpec(memory_space=pl.ANY)


class _Rider:
    def __init__(self, ins, out_shapes, sem_shapes, build, aliased=False):
        self.ins, self.out_shapes, self.sem_shapes, self.build, self.aliased = ins, out_shapes, sem_shapes, build, aliased


def _hosted_call(body, *, name, grid, in_specs, out_specs, out_shape, scratch_shapes, args, riders=(), aliases=None,
                 prefetch=None, riders_after_body=False):
    n_in, n_out, n_sc = len(in_specs), len(out_shape), len(scratch_shapes)
    r_in = [a for r in riders for a in r.ins]
    r_out = [s for r in riders for s in r.out_shapes]
    r_sem = [s for r in riders for s in r.sem_shapes]
    lead = () if prefetch is None else (prefetch,)
    assert prefetch is None or not (aliases or any(r.aliased for r in riders))

    def full_body(*refs):
        head, refs = refs[:len(lead)], refs[len(lead):]
        ins, rin = refs[:n_in], refs[n_in:n_in + len(r_in)]
        o0 = n_in + len(r_in)
        outs, rout = refs[o0:o0 + n_out], refs[o0 + n_out:o0 + n_out + len(r_out)]
        s0 = o0 + n_out + len(r_out)
        scratch, rsem = refs[s0:s0 + n_sc], refs[s0 + n_sc:]
        starts, waits = [], []
        pi = po = ps = 0
        for r in riders:
            st, wt = r.build(rin[pi:pi + len(r.ins)], rout[po:po + len(r.out_shapes)], rsem[ps:ps + len(r.sem_shapes)])
            starts += st
            waits += wt
            pi, po, ps = pi + len(r.ins), po + len(r.out_shapes), ps + len(r.sem_shapes)
        first = functools.reduce(jnp.logical_and, [pl.program_id(d) == 0 for d in range(len(grid))])
        last = functools.reduce(jnp.logical_and, [pl.program_id(d) == grid[d] - 1 for d in range(len(grid))])

        def start_riders():
            @pl.when(first)
            def _():
                for cp in starts:
                    cp.start()

        if riders and not riders_after_body:
            start_riders()
        body(*head, *ins, *outs, *scratch)
        if riders and riders_after_body:
            start_riders()
        if riders:
            @pl.when(last)
            def _():
                for wait in waits:
                    wait()

    io_aliases = dict(aliases or {})
    pi = po = 0
    for r in riders:
        if r.aliased:
            for q in range(len(r.ins)):
                io_aliases[n_in + pi + q] = n_out + po + q
        pi, po = pi + len(r.ins), po + len(r.out_shapes)
    specs = dict(
        grid=grid,
        in_specs=list(in_specs) + [_ANY] * len(r_in),
        out_specs=list(out_specs) + [_ANY] * len(r_out),
        scratch_shapes=list(scratch_shapes) + r_sem,
    )
    if prefetch is not None:
        specs = dict(grid_spec=pltpu.PrefetchScalarGridSpec(num_scalar_prefetch=1, **specs))
    res = pl.pallas_call(
        full_body,
        name=name,
        out_shape=list(out_shape) + r_out,
        input_output_aliases=io_aliases,
        compiler_params=_params(("arbitrary",) * len(grid)),
        **specs,
    )(*lead, *args, *r_in)
    rider_outs, po = [], n_out
    for r in riders:
        rider_outs.append(list(res[po:po + len(r.out_shapes)]))
        po += len(r.out_shapes)
    return list(res[:n_out]), rider_outs


def _run_riders(name, riders):
    r_in = [a for r in riders for a in r.ins]
    r_out = [s for r in riders for s in r.out_shapes]
    r_sem = [s for r in riders for s in r.sem_shapes]

    def body(*refs):
        rin, rout, rsem = refs[:len(r_in)], refs[len(r_in):len(r_in) + len(r_out)], refs[len(r_in) + len(r_out):]
        pi = po = ps = 0
        all_waits = []
        for r in riders:
            starts, waits = r.build(rin[pi:pi + len(r.ins)], rout[po:po + len(r.out_shapes)], rsem[ps:ps + len(r.sem_shapes)])
            for cp in starts:
                cp.start()
            all_waits += waits
            pi, po, ps = pi + len(r.ins), po + len(r.out_shapes), ps + len(r.sem_shapes)
        for wait in all_waits:
            wait()

    io_aliases = {}
    pi = po = 0
    for r in riders:
        if r.aliased:
            for q in range(len(r.ins)):
                io_aliases[pi + q] = po + q
        pi, po = pi + len(r.ins), po + len(r.out_shapes)
    res = pl.pallas_call(
        body,
        name=name,
        in_specs=[_ANY] * len(r_in),
        out_specs=[_ANY] * len(r_out),
        out_shape=r_out,
        scratch_shapes=r_sem,
        input_output_aliases=io_aliases,
    )(*r_in)
    outs, po = [], 0
    for r in riders:
        outs.append(list(res[po:po + len(r.out_shapes)]))
        po += len(r.out_shapes)
    return outs


def _piece(name):
    if name in BIG_PIECES:
        return (name, *BIG_PIECES[name], True)
    return (name, *SMALL_PIECES[name], False)


def _gather_rider(shards, names):
    pieces = [_piece(n) for n in names]
    n = len(pieces)

    def build(ins, outs, sems):
        local_sem, ici_send, ici_recv = sems
        _, _, c, s = _place()
        starts, waits = [], []
        for p, (_, kind, shard, split) in enumerate(pieces):
            cp = pltpu.make_async_copy(ins[p], _full_region(outs[p], kind, shard, s, None), local_sem.at[p])
            starts.append(cp)
            waits.append(cp.wait)
            h = c if split else None
            for k in (1, 2, 3):
                s2, dev = _other_chip(s, c, k)
                cp = _remote(_shard_region(ins[p], shard, h), _full_region(outs[p], kind, shard, s, h),
                             ici_send.at[p, k - 1], ici_recv.at[p, k - 1], dev)
                starts.append(cp)
                waits.append(cp.wait_send)
                region = _full_region(outs[p], kind, shard, s2, h)
                waits.append(_remote(region, region, ici_send.at[p, k - 1], ici_recv.at[p, k - 1], dev).wait_recv)
        return starts, waits

    return _Rider(
        [shards[name] for name in names],
        [jax.ShapeDtypeStruct(_full_shape(kind, shard), shards[name].dtype) for name, kind, shard, _ in pieces],
        [pltpu.SemaphoreType.DMA((n,)), pltpu.SemaphoreType.DMA((n, 3)), pltpu.SemaphoreType.DMA((n, 3))],
        build)


def _forward_rider(gathered, names):
    pieces = [_piece(n) for n in names]
    n = len(pieces)

    def build(ins, outs, sems):
        fwd_send, fwd_recv = sems
        x, y, c, s = _place()
        sibling = (x, y, 1 - c)
        starts, waits = [], []
        for p, (_, kind, shard, _) in enumerate(pieces):
            for k in (1, 2, 3):
                s2, _ = _other_chip(s, c, k)
                mine = _full_region(outs[p], kind, shard, s2, c)
                theirs = _full_region(outs[p], kind, shard, s2, 1 - c)
                cp = _remote(mine, mine, fwd_send.at[p, k - 1], fwd_recv.at[p, k - 1], sibling)
                starts.append(cp)
                waits.append(cp.wait_send)
                waits.append(_remote(theirs, theirs, fwd_send.at[p, k - 1], fwd_recv.at[p, k - 1], sibling).wait_recv)
        return starts, waits

    return _Rider(
        [gathered[name] for name in names],
        [jax.ShapeDtypeStruct(gathered[name].shape, gathered[name].dtype) for name in names],
        [pltpu.SemaphoreType.DMA((n, 3)), pltpu.SemaphoreType.DMA((n, 3))],
        build, aliased=True)


def _pair_exchange_rider(grads, names):
    n = len(names)

    def build(ins, outs, sems):
        send_sem, recv_sem = sems
        x, y, c, _ = _place()
        sibling = (x, y, 1 - c)
        starts, waits = [], []
        for p, name in enumerate(names):
            kind, shard = BIG_PIECES[name]
            for s2 in range(4):
                cp = _remote(_full_region(ins[p], kind, shard, s2, 1 - c), outs[p].at[s2], send_sem.at[p, s2],
                             recv_sem.at[p, s2], sibling)
                starts.append(cp)
                waits.append(cp.wait_send)
                waits.append(_remote(outs[p].at[s2], outs[p].at[s2], send_sem.at[p, s2], recv_sem.at[p, s2], sibling).wait_recv)
        return starts, waits

    return _Rider(
        [grads[name] for name in names],
        [jax.ShapeDtypeStruct((4,) + _half_shape(*BIG_PIECES[name]), F32) for name in names],
        [pltpu.SemaphoreType.DMA((n, 4))] * 2,
        build)


def _half_specs(kind, shard):
    half = shard[0] // 2
    if kind == "col":
        full = pl.BlockSpec((half, shard[1]), lambda j, pr: (pr[1], j))
        buf = pl.BlockSpec((None, half, shard[1]), lambda j, pr: (j, 0, 0))
    elif kind == "row":
        full = pl.BlockSpec((half, shard[1]), lambda j, pr: (2 * j + pr[1], 0))
        buf = pl.BlockSpec((None, half, shard[1]), lambda j, pr: (j, 0, 0))
    else:
        full = pl.BlockSpec((half, shard[1], shard[2]), lambda j, pr: (pr[1], j, 0))
        buf = pl.BlockSpec((None, half, shard[1], shard[2]), lambda j, pr: (j, 0, 0, 0))
    return full, buf


def _pair_sum(name, grad, recv, place):
    kind, shard = BIG_PIECES[name]
    full, buf = _half_specs(kind, shard)

    def body(pr, g_ref, r_ref, o_ref):
        o_ref[...] = (g_ref[...] + r_ref[...]).astype(BF16)

    return pl.pallas_call(
        body,
        name="pair_sum_" + name,
        grid_spec=pltpu.PrefetchScalarGridSpec(num_scalar_prefetch=1, grid=(4,), in_specs=[full, buf], out_specs=buf),
        out_shape=jax.ShapeDtypeStruct((4,) + _half_shape(kind, shard), BF16),
        compiler_params=_params(("arbitrary",)),
    )(place, grad, recv)


def _chip_exchange_rider(sums, names):
    n = len(names)

    def build(ins, outs, sems):
        send_sem, recv_sem = sems
        _, _, c, s = _place()
        starts, waits = [], []
        for p in range(n):
            for k in (1, 2, 3):
                s2, dev = _other_chip(s, c, k)
                cp = _remote(ins[p].at[s2], outs[p].at[k - 1], send_sem.at[p, k - 1], recv_sem.at[p, k - 1], dev)
                starts.append(cp)
                waits.append(cp.wait_send)
                waits.append(_remote(outs[p].at[k - 1], outs[p].at[k - 1], send_sem.at[p, k - 1], recv_sem.at[p, k - 1],
                                     dev).wait_recv)
        return starts, waits

    return _Rider(
        [sums[name] for name in names],
        [jax.ShapeDtypeStruct((3,) + _half_shape(*BIG_PIECES[name]), BF16) for name in names],
        [pltpu.SemaphoreType.DMA((n, 3))] * 2,
        build)


def _chip_sum(name, grad, recv_pair, recv_chip, place):
    kind, shard = BIG_PIECES[name]
    half = shard[0] // 2
    tail = tuple(shard[1:])
    zeros = (0,) * len(tail)
    nt = 4 if kind != "lru" and half % (4 * BF16_ROWS) == 0 else 1
    rows = half // nt
    if kind == "col":
        full = pl.BlockSpec((rows,) + tail, lambda j, pr: (pr[1] * nt + j, pr[0]))
    elif kind == "row":
        full = pl.BlockSpec((rows,) + tail, lambda j, pr: ((2 * pr[0] + pr[1]) * nt + j, 0))
    else:
        full = pl.BlockSpec((rows,) + tail, lambda j, pr: (pr[1], pr[0], 0))
    pair = pl.BlockSpec((None, rows) + tail, lambda j, pr: (pr[0], j) + zeros)
    chip = pl.BlockSpec((3, rows) + tail, lambda j, pr: (0, j) + zeros)
    out = pl.BlockSpec((rows,) + tail, lambda j, pr: (pr[1] * nt + j,) + zeros)

    def body(pr, g_ref, rp_ref, rc_ref, o_ref):
        total = g_ref[...] + rp_ref[...]
        for k in range(3):
            total = total + rc_ref[k].astype(F32)
        o_ref[...] = total

    return pl.pallas_call(
        body,
        name="chip_sum_" + name,
        grid_spec=pltpu.PrefetchScalarGridSpec(num_scalar_prefetch=1, grid=(nt,), in_specs=[full, pair, chip], out_specs=out),
        out_shape=jax.ShapeDtypeStruct(shard, F32),
        compiler_params=_params(("arbitrary",)),
    )(place, grad, recv_pair, recv_chip)


def _sibling_exchange_rider(halves, names):
    n = len(names)

    def build(ins, outs, sems):
        send_sem, recv_sem = sems
        x, y, c, _ = _place()
        sibling = (x, y, 1 - c)
        starts, waits = [], []
        for p, name in enumerate(names):
            shard = BIG_PIECES[name][1]
            mine = _shard_region(outs[p], shard, c)
            theirs = _shard_region(outs[p], shard, 1 - c)
            cp = _remote(mine, mine, send_sem.at[p], recv_sem.at[p], sibling)
            starts.append(cp)
            waits.append(cp.wait_send)
            waits.append(_remote(theirs, theirs, send_sem.at[p], recv_sem.at[p], sibling).wait_recv)
        return starts, waits

    return _Rider(
        [halves[name] for name in names],
        [jax.ShapeDtypeStruct(BIG_PIECES[name][1], F32) for name in names],
        [pltpu.SemaphoreType.DMA((n,))] * 2,
        build, aliased=True)


FIRST_WEIGHTS = ["meta_tokens", "conv_w"]
WEIGHT_GROUPS = {
    "lru": ["lru_wa", "lru_wx"],
    "branch": ["w_branch_ret", "w_branch_lru", "w_out"],
    "ffn_in": ["w_ffn_in"],
    "ffn_out": ["w_ffn_out"],
}
WEIGHT_SCHEDULE = {
    "in_proj": [("gather", "lru")],
    "retention_fwd": [("forward", "lru"), ("gather", "branch")],
    "lru_fwd": [("forward", "branch"), ("gather", "ffn_in")],
    "mix_fwd": [("forward", "ffn_in"), ("gather", "ffn_out")],
    "ffn_fwd_gate": [("forward", "ffn_out")],
}
GRAD_GROUPS = {
    "ffn_in": ["w_ffn_in"],
    "ffn_out": ["w_ffn_out"],
    "mixer": ["w_out", "w_branch_ret", "w_branch_lru", "lru_wa", "lru_wx"],
    "in": ["w_in"],
}
GRAD_SCHEDULE = {
    "mix_bwd": [("pair", "ffn_in"), ("pair", "ffn_out")],
    "retention_bwd": [("chip", "ffn_out")],
    "lru_bwd": [("chip", "ffn_in"), ("sibling", "ffn_out")],
    "dw_in": [("sibling", "ffn_in"), ("pair", "mixer")],
    "in_proj_bwd_0": [("chip", "mixer"), ("pair", "in")],
    "in_proj_bwd_1": [("sibling", "mixer"), ("chip", "in")],
}


class _CommPlan:
    def __init__(self, shards, place):
        self.shards, self.place = shards, place
        self.late = {}
        self.recv_pair, self.sums, self.recv_chip, self.halves, self.final = {}, {}, {}, {}, {}

    def _grad_rider(self, stage, group, grads):
        names = GRAD_GROUPS[group]
        if stage == "pair":
            return _pair_exchange_rider(grads, names)
        if stage == "chip":
            return _chip_exchange_rider(self.sums, names)
        return _sibling_exchange_rider(self.halves, names)

    def _grad_after(self, stage, group, outs, grads):
        names = GRAD_GROUPS[group]
        if stage == "pair":
            for n, o in zip(names, outs):
                self.recv_pair[n] = o
                self.sums[n] = _pair_sum(n, grads[n], o, self.place)
        elif stage == "chip":
            for n, o in zip(names, outs):
                self.halves[n] = _chip_sum(n, grads[n], self.recv_pair[n], o, self.place)
        else:
            self.final.update(zip(names, outs))

    def riders(self, host, w, grads):
        if host in WEIGHT_SCHEDULE:
            return [_gather_rider(self.shards, WEIGHT_GROUPS[group]) if stage == "gather"
                    else _forward_rider(self.late, WEIGHT_GROUPS[group]) for stage, group in WEIGHT_SCHEDULE[host]]
        return [self._grad_rider(stage, group, grads) for stage, group in GRAD_SCHEDULE.get(host, [])]

    def after(self, host, rider_outs, w, grads):
        for (stage, group), outs in zip(WEIGHT_SCHEDULE.get(host, []), rider_outs):
            (self.late if stage == "gather" else w).update(zip(WEIGHT_GROUPS[group], outs))
        for (stage, group), outs in zip(GRAD_SCHEDULE.get(host, []), rider_outs):
            self._grad_after(stage, group, outs, grads)

    def finish(self, partial):
        outs, (blocks,) = _run_riders("tail_exchange", [_sibling_exchange_rider(self.halves, GRAD_GROUPS["in"]),
                                                        _small_exchange_rider(partial)])
        self.final.update(zip(GRAD_GROUPS["in"], outs))
        return self.final, blocks


def _adamw_math(w, g, m, v):
    m = ADAM_B1 * m + (1.0 - ADAM_B1) * g
    v = ADAM_B2 * v + (1.0 - ADAM_B2) * (g * g)
    m_hat = m / (1.0 - ADAM_B1 ** ADAM_STEP)
    v_hat = v / (1.0 - ADAM_B2 ** ADAM_STEP)
    delta = -ADAM_LR * (m_hat / (jnp.sqrt(v_hat) + ADAM_EPS) + ADAM_WD * w)
    return delta, m, v


def _adamw(name, g, w, m, v):
    rows, cols = g.shape
    tr = rows // 4 if rows % 32 == 0 else rows

    def body(g_ref, w_ref, m_ref, v_ref, go_ref, d_ref, mo_ref, vo_ref):
        gv = g_ref[...]
        delta, m2, v2 = _adamw_math(w_ref[...], gv, m_ref[...], v_ref[...])
        go_ref[...] = gv
        d_ref[...] = delta
        mo_ref[...] = m2
        vo_ref[...] = v2

    spec = pl.BlockSpec((tr, cols), lambda i: (i, 0))
    return pl.pallas_call(
        body,
        name="adamw_" + name,
        grid=(rows // tr,),
        in_specs=[spec] * 4,
        out_specs=[spec] * 4,
        out_shape=[jax.ShapeDtypeStruct((rows, cols), F32)] * 4,
        compiler_params=_params(("arbitrary",)),
    )(g, w, m, v)


SMALL_ROWS = 48
VEC_ROWS = {"final_norm_w": 1, "ffn_norm_w": 8, "mix_norm_w": 16, "conv_b": 24, "lru_ba": 25, "lru_bx": 26, "lru_lambda": 27}
VEC_NAMES = list(VEC_ROWS)
CONV_W_ROW = 28
META_ROW = 32


def _small_exchange_rider(partial):
    def build(ins, outs, sems):
        local_sem, send_sem, recv_sem = sems
        _, _, c, s = _place()
        me = 2 * s + c
        cp = pltpu.make_async_copy(ins[0], outs[0].at[me], local_sem)
        starts, waits = [cp], [cp.wait]
        for k in range(1, 8):
            peer = jnp.bitwise_xor(me, k)
            dev = (peer // 4, (peer // 2) % 2, peer % 2)
            rd = _remote(ins[0], outs[0].at[me], send_sem.at[k - 1], recv_sem.at[k - 1], dev)
            starts.append(rd)
            waits.append(rd.wait_send)
            waits.append(_remote(ins[0], outs[0].at[peer], send_sem.at[k - 1], recv_sem.at[k - 1], dev).wait_recv)
        return starts, waits

    return _Rider([partial], [jax.ShapeDtypeStruct((8, SMALL_ROWS, D), F32)],
                  [pltpu.SemaphoreType.DMA, pltpu.SemaphoreType.DMA((7,)), pltpu.SemaphoreType.DMA((7,))], build)


def _small_update(blocks, place, vecs, conv, meta):
    nvec = len(VEC_NAMES)
    qcols = D // 4

    def in_order(ref):
        total = ref[0]
        for d in range(1, 8):
            total = total + ref[d]
        return total

    def body(pr, blocks_ref, cols_ref, *refs):
        vec_refs = refs[:3 * nvec]
        conv_refs = refs[3 * nvec:3 * nvec + 3]
        meta_refs = refs[3 * nvec + 3:3 * nvec + 6]
        outs = refs[3 * nvec + 6:]
        loss_ref, vec_out, conv_out, meta_out = outs[0], outs[1:5], outs[5:9], outs[9:13]
        tot, col = in_order(blocks_ref), in_order(cols_ref)
        loss_ref[...] = jnp.sum(tot[0:1, :], axis=1, keepdims=True)
        for o in vec_out:
            o[...] = jnp.zeros_like(o)
        for j, name in enumerate(VEC_NAMES):
            w, m, v = (r[...] for r in vec_refs[3 * j:3 * j + 3])
            g = tot[VEC_ROWS[name]:VEC_ROWS[name] + 1, :]
            if name == "lru_lambda":
                g = -g / (1.0 + jnp.exp(w))
            for o, val in zip(vec_out, (g,) + _adamw_math(w, g, m, v)):
                o[j:j + 1, :] = val
        for row, n_rows, ins, group in ((CONV_W_ROW, 4, conv_refs, conv_out), (META_ROW, N_META, meta_refs, meta_out)):
            g = col[row:row + n_rows, :]
            w, m, v = (r[...] for r in ins)
            for o, val in zip(group, (g,) + _adamw_math(w, g, m, v)):
                o[...] = val

    whole = lambda shape: pl.BlockSpec(shape, lambda i, pr: (0,) * len(shape))
    in_specs = [whole((8, SMALL_ROWS, D)), pl.BlockSpec((8, SMALL_ROWS, qcols), lambda i, pr: (0, 0, pr[0]))]
    in_specs += [whole((1, D))] * (3 * nvec) + [whole((4, qcols))] * 3 + [whole((N_META, qcols))] * 3
    out_shapes = [(1, 1)] + [(8, D)] * 4 + [(4, qcols)] * 4 + [(N_META, qcols)] * 4
    return pl.pallas_call(
        body,
        name="small_update",
        grid_spec=pltpu.PrefetchScalarGridSpec(num_scalar_prefetch=1, grid=(1,), in_specs=in_specs,
                                               out_specs=[whole(s) for s in out_shapes]),
        out_shape=[jax.ShapeDtypeStruct(s, F32) for s in out_shapes],
        compiler_params=_params(("arbitrary",)),
    )(place, blocks, blocks, *[a for t in vecs for a in t], *conv, *meta)


WEIGHT_ORDER = ["meta_tokens", "mix_norm_w", "w_in", "conv_w", "conv_b", "lru_wa", "lru_ba", "lru_wx", "lru_bx", "lru_lambda",
                "w_branch_ret", "w_branch_lru", "w_out", "ffn_norm_w", "w_ffn_in", "w_ffn_out", "final_norm_w"]


def kernel(x, meta_tokens, mix_norm_w, w_in, conv_w, conv_b, lru_wa, lru_ba, lru_wx, lru_bx, lru_lambda, w_branch_ret, w_branch_lru, w_out, ffn_norm_w, w_ffn_in, w_ffn_out, final_norm_w, loss_target, m_meta_tokens, m_mix_norm_w, m_w_in, m_conv_w, m_conv_b, m_lru_wa, m_lru_ba, m_lru_wx, m_lru_bx, m_lru_lambda, m_w_branch_ret, m_w_branch_lru, m_w_out, m_ffn_norm_w, m_w_ffn_in, m_w_ffn_out, m_final_norm_w, v_meta_tokens, v_mix_norm_w, v_w_in, v_conv_w, v_conv_b, v_lru_wa, v_lru_ba, v_lru_wx, v_lru_bx, v_lru_lambda, v_w_branch_ret, v_w_branch_lru, v_w_out, v_ffn_norm_w, v_w_ffn_in, v_w_ffn_out, v_final_norm_w):
    args = locals()
    wts = {n: args[n] for n in WEIGHT_ORDER}
    mom = {n: args["m_" + n] for n in WEIGHT_ORDER}
    var = {n: args["v_" + n] for n in WEIGHT_ORDER}
    place = jnp.stack([2 * lax.axis_index("x") + lax.axis_index("y"), lax.axis_index("c")]).astype(jnp.int32)

    shards = {n: wts[n][0].astype(BF16) for n in BIG_PIECES}
    shards["meta_tokens"] = wts["meta_tokens"]
    shards["conv_w"] = wts["conv_w"][0]
    plan = _CommPlan(shards, place)
    (first,) = _run_riders("gather_first", [_gather_rider(shards, FIRST_WEIGHTS)])
    w = dict(zip(FIRST_WEIGHTS, first))
    for n in VEC_NAMES:
        w[n] = wts[n].reshape(1, D)
    chips = jnp.bitwise_xor(place[0], jnp.array((0,) + SHARD_ORDER, dtype=jnp.int32))
    w["route"] = jnp.concatenate([place, jnp.stack([2 * chips, 2 * chips + 1], axis=1).reshape(8)])
    w["w_in_shard"] = shards["w_in"]

    grad_x, grad_head, _, stats = _local_step(x[0], loss_target[0], w, plan)
    partial = jnp.concatenate(stats + [grad_head[PAD_ROWS:]], axis=0)
    shard_grads, blocks = plan.finish(partial)

    out = {}
    for n in BIG_PIECES:
        shape2d = (-1, wts[n].shape[-1])
        res = _adamw(n, *[a.reshape(shape2d) for a in (shard_grads[n], wts[n], mom[n], var[n])])
        out[n] = [r.reshape(wts[n].shape) for r in res]

    vecs = [tuple(a[n].reshape(1, D) for a in (wts, mom, var)) for n in VEC_NAMES]
    conv = tuple(a["conv_w"][0] for a in (wts, mom, var))
    meta = tuple(a["meta_tokens"] for a in (wts, mom, var))
    res = _small_update(blocks, place, vecs, conv, meta)
    loss = res[0].reshape(())
    for j, n in enumerate(VEC_NAMES):
        out[n] = [r[j].reshape(wts[n].shape) for r in res[1:5]]
    out["conv_w"] = [r.reshape(wts["conv_w"].shape) for r in res[5:9]]
    out["meta_tokens"] = list(res[9:13])

    return (loss, grad_x.reshape(x.shape)) + tuple(out[n][kind] for kind in range(4) for n in WEIGHT_ORDER)
```

```python
import functools
import math

import jax
import jax.numpy as jnp
from jax import lax
from jax.experimental import pallas as pl
from jax.experimental.pallas import tpu as pltpu

F32 = jnp.float32
BF16 = jnp.bfloat16

LANES = 128
BF16_ROWS = 16

D = 1024
HEADS = 8
DH = 128
CHUNK = 256
N_META = 16
FRONT = 256
PAD_ROWS = FRONT - N_META
LRU_BLOCKS = 4
LRU_BLOCK = 256
LRU_C = 8.0
FFN = 2816
FFN_HALF = FFN // 2
IN_COLS = 8 * D
ROPE_BASE = 10000.0
EPS = 1e-6
QK_SCALE = DH ** -0.5

ADAM_LR = 0.001
ADAM_B1 = 0.9
ADAM_B2 = 0.999
ADAM_EPS = 1e-08
ADAM_WD = 0.01
ADAM_STEP = 10

TM = 256
TM_HEAVY = 768
MXU_TILE = 256
FFN_SUB = 4 * MXU_TILE
_FFN_SUBS = [(a, min(a + FFN_SUB, FFN)) for a in range(0, FFN, FFN_SUB)]
TM_MIX_BWD = 384
TM_LRU = 384
TM_MIX_FWD = 384
TM_IN_PROJ = 1408
VMEM_LIMIT = 60 * 1024 * 1024
TN_VMEM_BUDGET = 40 * 1024 * 1024

NT_DIMS = (((1,), (1,)), ((), ()))
TN_DIMS = (((0,), (0,)), ((), ()))
MESH = pl.DeviceIdType.MESH


def _params(sem=None):
    if sem is None:
        return pltpu.CompilerParams(vmem_limit_bytes=VMEM_LIMIT)
    return pltpu.CompilerParams(dimension_semantics=sem, vmem_limit_bytes=VMEM_LIMIT)


def _dot(a, b):
    return jnp.dot(a, b, preferred_element_type=F32)


def _dot_nt(a, b):
    return lax.dot_general(a, b, NT_DIMS, preferred_element_type=F32)


def _dot_tn(a, b):
    return lax.dot_general(a, b, TN_DIMS, preferred_element_type=F32)


def _sigmoid(z):
    return 1.0 / (1.0 + jnp.exp(-z))


def _log1p(x):
    return jnp.where(x < 1e-3, x * (1.0 - x * (0.5 - x * (1.0 / 3.0))), jnp.log(1.0 + x))


def _softplus(x):
    return jnp.maximum(x, 0.0) + _log1p(jnp.exp(-jnp.abs(x)))


def _one_minus_square(a, log_a):
    x = 2.0 * log_a
    series = -x * (1.0 + x * (0.5 + x * (1.0 / 6.0)))
    return jnp.where(x > -0.02, series, 1.0 - a * a)


_GELU_K = math.sqrt(2.0 / math.pi)


def _gelu_and_grad(x):
    inner = _GELU_K * (x + 0.044715 * x * x * x)
    t = jnp.tanh(inner)
    val = 0.5 * x * (1.0 + t)
    grad = 0.5 * (1.0 + t) + 0.5 * x * (1.0 - t * t) * _GELU_K * (1.0 + 3.0 * 0.044715 * x * x)
    return val, grad


def _row_ids(i, rows, shape):
    return i * rows + lax.broadcasted_iota(jnp.int32, shape, 0)


def _rope_tables(rows):
    inv_freq = ROPE_BASE ** (-jnp.arange(0, DH, 2, dtype=F32) / DH)
    block_pos = jnp.arange(rows // FRONT, dtype=jnp.int32) * FRONT - PAD_ROWS
    coarse = block_pos.astype(F32)[:, None] * inv_freq[None, :]
    fine = jnp.arange(FRONT, dtype=F32)[:, None] * inv_freq[None, :]
    ca, sa = jnp.cos(coarse)[:, None, :], jnp.sin(coarse)[:, None, :]
    cb, sb = jnp.cos(fine)[None, :, :], jnp.sin(fine)[None, :, :]
    cos = (ca * cb - sa * sb).reshape(rows, DH // 2)
    sin = (sa * cb + ca * sb).reshape(rows, DH // 2)
    return jnp.concatenate([cos, cos], axis=1), jnp.concatenate([-sin, sin], axis=1)


def _decay_tables():
    log_g = jnp.log(1.0 - 2.0 ** (-5.0 - jnp.arange(HEADS, dtype=F32)))
    idx = jnp.arange(CHUNK, dtype=F32)
    diff = idx[:, None] - idx[None, :]
    intra = jnp.where(diff[None] >= 0, jnp.exp(jnp.maximum(diff, 0.0)[None] * log_g[:, None, None]), 0.0)
    q_decay = jnp.exp((idx + 1.0)[:, None] * log_g[None, :])
    k_decay = jnp.exp((CHUNK - 1.0 - idx)[:, None] * log_g[None, :])
    chunk_decay = jnp.exp(CHUNK * log_g)
    wide = lambda a: jnp.repeat(a, DH, axis=-1)
    return intra, jnp.swapaxes(intra, 1, 2), wide(q_decay), wide(k_decay), wide(chunk_decay[None, :])


def _norm1(head, x2d, norm_w, riders=()):
    rows = FRONT + x2d.shape[0]
    tm = TM_IN_PROJ if rows % TM_IN_PROJ == 0 else TM_MIX_FWD
    nt = rows // tm

    def body(head_hbm, x_hbm, nw_ref, u_ref, rstd_ref, h0_sc, h0_sem):
        slot = _frame_rows(head_hbm, x_hbm, h0_sc, h0_sem, pl.program_id(0), tm, nt)
        hv = h0_sc[slot]
        rs = lax.rsqrt(jnp.mean(hv * hv, axis=-1, keepdims=True) + EPS)
        u_ref[...] = ((hv * rs) * nw_ref[...]).astype(BF16)
        rstd_ref[...] = rs

    return _hosted_call(
        body,
        name="norm1",
        grid=(nt,),
        in_specs=[_ANY, _ANY, pl.BlockSpec((1, D), lambda i: (0, 0))],
        out_specs=[pl.BlockSpec((tm, D), lambda i: (i, 0)), pl.BlockSpec((tm, 1), lambda i: (i, 0))],
        out_shape=[jax.ShapeDtypeStruct((rows, D), BF16), jax.ShapeDtypeStruct((rows, 1), F32)],
        scratch_shapes=[pltpu.VMEM((2, tm, D), F32), pltpu.SemaphoreType.DMA((3,))],
        args=(head, x2d, norm_w),
        riders=riders,
    )


def _heavy_tile(rows):
    return TM_HEAVY if rows % TM_HEAVY == 0 else TM


def _lru_tile(rows):
    return TM_LRU if rows % TM_LRU == 0 else TM


def _swap_3_4(group):
    return jnp.where(group == 3, 4, jnp.where(group == 4, 3, group))


SHARD_ORDER = (2, 3, 1)
LRU_IN_COL = 3
GATES_COL = 1


def _in_proj(u, w_shard, route, cos_t, sin_t, riders=()):
    rows = u.shape[0]
    tm = TM_IN_PROJ if rows % TM_IN_PROJ == 0 else _heavy_tile(rows)
    nt = rows // tm
    kind, shard = BIG_PIECES["w_in"]

    def body(route_ref, u_hbm, wsh_ref, cos_ref, sin_ref, proj_ref, wfull_ref,
             u_sc, w_sc, u_sem, w_sem, local_sem, ici_send, ici_recv, fwd_send, fwd_recv):
        g, i = pl.program_id(0), pl.program_id(1)
        s, c = route_ref[0], route_ref[1]
        gid = route_ref[2 + g]
        sibling = (s // 2, s % 2, 1 - c)
        local = pltpu.make_async_copy(wsh_ref, _full_region(wfull_ref, kind, shard, s, None), local_sem)
        u_copies = [pltpu.make_async_copy(u_hbm.at[pl.ds(t * tm, tm)], u_sc.at[t], u_sem.at[t]) for t in range(nt)]
        sends, arrivals = [], []
        for k in (1, 2, 3):
            s2, dev = _other_chip(s, c, k)
            sends.append(_remote(_shard_region(wsh_ref, shard, c), _full_region(wfull_ref, kind, shard, s, c),
                                 ici_send.at[k - 1], ici_recv.at[k - 1], dev))
            mine = _full_region(wfull_ref, kind, shard, s2, c)
            theirs = _full_region(wfull_ref, kind, shard, s2, 1 - c)
            arrivals.append((_remote(mine, mine, ici_send.at[k - 1], ici_recv.at[k - 1], dev),
                             _remote(mine, mine, fwd_send.at[k - 1], fwd_recv.at[k - 1], sibling),
                             _remote(theirs, theirs, fwd_send.at[k - 1], fwd_recv.at[k - 1], sibling)))

        slot = g % 2
        last_tile = i == nt - 1

        def block_copy(src, col, to_slot):
            return pltpu.make_async_copy(src.at[:, pl.ds(pl.multiple_of(col * D, D), D)], w_sc.at[to_slot],
                                         w_sem.at[to_slot])

        @pl.when(jnp.logical_and(g == 0, i == 0))
        def _():
            for k in SHARD_ORDER:
                sends[k - 1].start()
            local.start()
            for cp in u_copies:
                cp.start()
            cp = block_copy(wsh_ref, 0, 0)
            cp.start()
            cp.wait()

        @pl.when(jnp.logical_and(last_tile, g == 0))
        def _():
            block_copy(wsh_ref, 1, 1).start()

        for pos, k in enumerate(SHARD_ORDER, start=1):
            arrived, forward, forwarded = arrivals[k - 1]

            @pl.when(jnp.logical_and(last_tile, g == 2 * pos - 1))
            def _():
                arrived.wait_recv()
                forward.start()
                forwarded.wait_recv()
                block_copy(wfull_ref, route_ref[2 + 2 * pos], 0).start()

            @pl.when(jnp.logical_and(last_tile, g == 2 * pos))
            def _():
                block_copy(wfull_ref, route_ref[3 + 2 * pos], 1).start()

        @pl.when(jnp.logical_and(i == 0, g > 0))
        def _():
            block_copy(wfull_ref, 0, slot).wait()

        for t in range(nt):
            @pl.when(jnp.logical_and(g == 0, i == t))
            def _():
                u_copies[t].wait()

        acc = _dot(u_sc[i], w_sc[slot])

        @pl.when(gid < 2)
        def _():
            scale = jnp.where(gid == 1, QK_SCALE, 1.0).astype(F32)
            for h in range(HEADS):
                sl = slice(h * DH, (h + 1) * DH)
                blk = acc[:, sl]
                out = (blk * cos_ref[...] + pltpu.roll(blk, DH // 2, axis=1) * sin_ref[...]) * scale
                proj_ref[:, sl] = out.astype(BF16)

        @pl.when(gid >= 2)
        def _():
            proj_ref[...] = acc.astype(BF16)

        @pl.when(jnp.logical_and(g == 7, i == nt - 1))
        def _():
            local.wait()
            for cp in sends:
                cp.wait_send()
            for _, forward, _ in arrivals:
                forward.wait_send()

    return _hosted_call(
        body,
        name="in_proj",
        grid=(8, nt),
        in_specs=[
            _ANY, _ANY,
            pl.BlockSpec((tm, DH), lambda g, i, rt: (i, 0)),
            pl.BlockSpec((tm, DH), lambda g, i, rt: (i, 0)),
        ],
        out_specs=[pl.BlockSpec((tm, D), lambda g, i, rt: (i, _swap_3_4(rt[2 + g]))), _ANY],
        out_shape=[jax.ShapeDtypeStruct((rows, IN_COLS), BF16), jax.ShapeDtypeStruct((D, IN_COLS), BF16)],
        scratch_shapes=[
            pltpu.VMEM((nt, tm, D), BF16), pltpu.VMEM((2, D, D), BF16),
            pltpu.SemaphoreType.DMA((nt,)), pltpu.SemaphoreType.DMA((2,)), pltpu.SemaphoreType.DMA,
            pltpu.SemaphoreType.DMA((3,)), pltpu.SemaphoreType.DMA((3,)),
            pltpu.SemaphoreType.DMA((3,)), pltpu.SemaphoreType.DMA((3,)),
        ],
        args=(u, w_shard, cos_t, sin_t),
        riders=riders,
        prefetch=route,
        riders_after_body=True,
    )


def _retention_fwd(proj_bf, intra, q_dec, k_dec, c_dec, riders=()):
    rows = proj_bf.shape[0]
    nc = rows // CHUNK

    def body(q_ref, k_ref, v_ref, m_ref, qd_ref, kd_ref, cd_ref, o_ref, sprev_ref, s_sc):
        @pl.when(pl.program_id(0) == 0)
        def _():
            s_sc[...] = jnp.zeros_like(s_sc)

        for h in range(HEADS):
            sl = slice(h * DH, (h + 1) * DH)
            q, k, v = q_ref[:, sl], k_ref[:, sl], v_ref[:, sl]
            state = s_sc[h]
            state_b = state.astype(BF16)
            sprev_ref[0, h] = state_b
            s = _dot_nt(q, k) * m_ref[h]
            inner = _dot(s.astype(BF16), v)
            cross = _dot(q, state_b) * qd_ref[:, sl]
            o_ref[:, sl] = (inner + cross).astype(BF16)
            k_scaled = (k.astype(F32) * kd_ref[:, sl]).astype(BF16)
            s_sc[h] = state * cd_ref[:, sl] + _dot_tn(k_scaled, v)

    chunk_spec = lambda col: pl.BlockSpec((CHUNK, D), lambda c: (c, col))
    const2 = lambda shape: pl.BlockSpec(shape, lambda c: (0, 0))
    return _hosted_call(
        body,
        name="retention_fwd",
        grid=(nc,),
        in_specs=[
            chunk_spec(0), chunk_spec(1), chunk_spec(2),
            pl.BlockSpec((HEADS, CHUNK, CHUNK), lambda c: (0, 0, 0)),
            const2((CHUNK, D)), const2((CHUNK, D)), const2((1, D)),
        ],
        out_specs=[
            pl.BlockSpec((CHUNK, D), lambda c: (c, 0)),
            pl.BlockSpec((1, HEADS, DH, DH), lambda c: (c, 0, 0, 0)),
        ],
        out_shape=[
            jax.ShapeDtypeStruct((rows, D), BF16),
            jax.ShapeDtypeStruct((nc, HEADS, DH, DH), BF16),
        ],
        scratch_shapes=[pltpu.VMEM((HEADS, DH, DH), F32)],
        args=(proj_bf, proj_bf, proj_bf, intra, q_dec, k_dec, c_dec),
        riders=riders,
    )


def _shift_down(x, first8_prev, d):
    rolled = pltpu.roll(x, d, axis=0)
    head = pltpu.roll(jnp.concatenate([first8_prev, x[0:8]], axis=0), d, axis=0)[8:16]
    return rolled, head


def _conv_and_gates(x, prev8, cw_ref, cb_ref, wa_ref, wx_ref, ba_ref, bx_ref, lam_ref, c_sc):
    cw = cw_ref[...]
    conv = cb_ref[...] + cw[3:4] * x
    head = cb_ref[...] + cw[3:4] * x[0:8]
    for d in (1, 2, 3):
        rolled, hd = _shift_down(x, prev8, d)
        conv = conv + cw[3 - d:4 - d] * rolled
        head = head + cw[3 - d:4 - d] * hd
    c_sc[...] = conv
    c_sc[0:8, :] = head
    c = c_sc[...]
    zr, zi = [], []
    for g in range(LRU_BLOCKS):
        sl = slice(g * LRU_BLOCK, (g + 1) * LRU_BLOCK)
        cg = c[:, sl].astype(BF16)
        zr.append(_dot(cg, wa_ref[g]))
        zi.append(_dot(cg, wx_ref[g]))
    r = _sigmoid(jnp.concatenate(zr, axis=1) + ba_ref[...])
    gate_i = _sigmoid(jnp.concatenate(zi, axis=1) + bx_ref[...])
    sp = _softplus(-lam_ref[...])
    log_a = (-LRU_C) * r * sp
    a = jnp.exp(log_a)
    mult = jnp.sqrt(_one_minus_square(a, log_a))
    return c, r, gate_i, a, mult, log_a


def _lru_fwd(proj, conv_w, conv_b, wa, wx, ba, bx, lam, riders=()):
    rows = proj.shape[0]
    TM = _lru_tile(rows)
    nt = rows // TM

    def body(x_ref, cw_ref, cb_ref, wa_ref, wx_ref, ba_ref, bx_ref, lam_ref,
             h_ref, c_ref, r_ref, i_ref, la_ref, mult_ref, prev_sc, carry_sc, c_sc, a_sc, u_sc, h_sc):
        i = pl.program_id(0)

        @pl.when(i == 0)
        def _():
            prev_sc[...] = jnp.zeros_like(prev_sc)
            carry_sc[...] = jnp.zeros_like(carry_sc)

        x = x_ref[...].astype(F32)
        c, r, gate_i, a, mult, log_a = _conv_and_gates(x, prev_sc[...], cw_ref, cb_ref, wa_ref, wx_ref, ba_ref, bx_ref,
                                                       lam_ref, c_sc)
        for ref, val in ((c_ref, c), (r_ref, r), (i_ref, gate_i), (la_ref, log_a), (mult_ref, mult)):
            ref[...] = val.astype(BF16)
        prev_sc[...] = x[TM - 8:TM]
        valid = _row_ids(i, TM, (TM, D)) >= PAD_ROWS
        a_sc[...] = a
        u_sc[...] = jnp.where(valid, mult * gate_i * c, 0.0)
        row8 = lax.broadcasted_iota(jnp.int32, (8, D), 0)

        def group(gi, hprev):
            r0 = pl.multiple_of(gi * 8, 8)
            aa = a_sc[pl.ds(r0, 8), :]
            uu = u_sc[pl.ds(r0, 8), :]
            for d in (1, 2, 4):
                a_sh = jnp.where(row8 >= d, pltpu.roll(aa, d, axis=0), 1.0)
                u_sh = jnp.where(row8 >= d, pltpu.roll(uu, d, axis=0), 0.0)
                uu = uu + aa * u_sh
                aa = aa * a_sh
            hb = aa * hprev + uu
            h_sc[pl.ds(r0, 8), :] = hb
            return hb[7:8, :]

        hlast = lax.fori_loop(0, TM // 8, group, carry_sc[0:1, :])
        carry_sc[0:1, :] = hlast
        h_ref[...] = h_sc[...].astype(BF16)

    vec = pl.BlockSpec((1, D), lambda i: (0, 0))
    wspec = pl.BlockSpec((LRU_BLOCKS, LRU_BLOCK, LRU_BLOCK), lambda i: (0, 0, 0))
    return _hosted_call(
        body,
        name="lru_fwd",
        grid=(nt,),
        in_specs=[
            pl.BlockSpec((TM, D), lambda i: (i, LRU_IN_COL)),
            pl.BlockSpec((4, D), lambda i: (0, 0)),
            vec, wspec, wspec, vec, vec, vec,
        ],
        out_specs=[pl.BlockSpec((TM, D), lambda i: (i, 0)) for _ in range(6)],
        out_shape=[jax.ShapeDtypeStruct((rows, D), BF16) for _ in range(6)],
        scratch_shapes=[
            pltpu.VMEM((8, D), F32), pltpu.VMEM((8, D), F32),
            pltpu.VMEM((TM, D), F32), pltpu.VMEM((TM, D), F32), pltpu.VMEM((TM, D), F32), pltpu.VMEM((TM, D), F32),
        ],
        args=(proj, conv_w, conv_b, wa, wx, ba, bx, lam),
        riders=riders,
    )


def _group_norm(o):
    outs, rstds = [], []
    for h in range(HEADS):
        oh = o[:, h * DH:(h + 1) * DH]
        rs = lax.rsqrt(jnp.mean(oh * oh, axis=-1, keepdims=True) + EPS)
        outs.append(oh * rs)
        rstds.append(rs)
    return jnp.concatenate(outs, axis=1), rstds


def _frame_rows(head_hbm, x_hbm, buf, sems, i, tm, nt):
    slot = i % 2

    @pl.when(i == 0)
    def _():
        first = [pltpu.make_async_copy(head_hbm, buf.at[0, pl.ds(0, FRONT)], sems.at[2]),
                 pltpu.make_async_copy(x_hbm.at[pl.ds(0, tm - FRONT)], buf.at[0, pl.ds(FRONT, tm - FRONT)], sems.at[0])]
        for cp in first:
            cp.start()
        for cp in first:
            cp.wait()

    @pl.when(i + 1 < nt)
    def _():
        start = pl.multiple_of((i + 1) * tm - FRONT, LANES)
        pltpu.make_async_copy(x_hbm.at[pl.ds(start, tm)], buf.at[1 - slot], sems.at[1 - slot]).start()

    @pl.when(i > 0)
    def _():
        pltpu.make_async_copy(x_hbm.at[pl.ds(0, tm)], buf.at[slot], sems.at[slot]).wait()

    return slot


def _mix_fwd(head, x2d, o, proj, h_lru, w_br, w_bl, w_o, ffn_norm_w, riders=()):
    rows = o.shape[0]
    tm = TM_MIX_FWD
    assert rows % tm == 0 and tm > FRONT
    nt = rows // tm

    def body(head_hbm, x_hbm, o_ref, gates_ref, hl_ref, wbr_ref, wbl_ref, wo_ref, nw_ref,
             yret_ref, ylru_ref, h1_ref, u2_ref, rstd_ref, h0_sc, h0_sem):
        i = pl.program_id(0)
        slot = _frame_rows(head_hbm, x_hbm, h0_sc, h0_sem, i, tm, nt)
        gate = lambda j: gates_ref[:, j * D:(j + 1) * D].astype(F32)
        on, _ = _group_norm(o_ref[...].astype(F32))
        gret = gate(0)
        a_ret = (gret * _sigmoid(gret) * on).astype(BF16)
        y_ret = _dot(a_ret, wbr_ref[...])
        gl, _ = _gelu_and_grad(gate(1))
        a_lru = (gl * hl_ref[...].astype(F32)).astype(BF16)
        y_lru = _dot(a_lru, wbl_ref[...])
        mixed = (_sigmoid(gate(2)) * y_ret + _sigmoid(gate(3)) * y_lru).astype(BF16)
        delta = _dot(mixed, wo_ref[...])
        yret_ref[...] = y_ret.astype(BF16)
        ylru_ref[...] = y_lru.astype(BF16)
        h1 = h0_sc[slot] + delta
        rs = lax.rsqrt(jnp.mean(h1 * h1, axis=-1, keepdims=True) + EPS)
        h1_ref[...] = h1
        u2_ref[...] = ((h1 * rs) * nw_ref[...]).astype(BF16)
        rstd_ref[...] = rs

    tile = lambda col: pl.BlockSpec((tm, D), lambda i: (i, col))
    wspec = pl.BlockSpec((D, D), lambda i: (0, 0))
    return _hosted_call(
        body,
        name="mix_fwd",
        grid=(nt,),
        in_specs=[
            _ANY, _ANY,
            tile(0), pl.BlockSpec((tm, 4 * D), lambda i: (i, GATES_COL)), tile(0),
            wspec, wspec, wspec,
            pl.BlockSpec((1, D), lambda i: (0, 0)),
        ],
        out_specs=[tile(0), tile(0), tile(0), tile(0), pl.BlockSpec((tm, 1), lambda i: (i, 0))],
        out_shape=[
            jax.ShapeDtypeStruct((rows, D), BF16), jax.ShapeDtypeStruct((rows, D), BF16),
            jax.ShapeDtypeStruct((rows, D), F32), jax.ShapeDtypeStruct((rows, D), BF16),
            jax.ShapeDtypeStruct((rows, 1), F32),
        ],
        scratch_shapes=[pltpu.VMEM((2, tm, D), F32), pltpu.SemaphoreType.DMA((3,))],
        args=(head, x2d, o, proj, h_lru, w_br, w_bl, w_o, ffn_norm_w),
        riders=riders,
    )


def _ffn_fwd_gate(u2, w_ffn_in, riders=()):
    rows = u2.shape[0]
    tm = _heavy_tile(rows)
    hid = lambda: pl.BlockSpec((tm, FFN), lambda i: (i, 0))

    def body(u2_ref, w_hbm, silu_ref, dsilu_ref, act_ref, w_sc, w_sem):
        @pl.when(pl.program_id(0) == 0)
        def _():
            cp = pltpu.make_async_copy(w_hbm, w_sc, w_sem)
            cp.start()
            cp.wait()

        u2 = u2_ref[...]
        for a, b in _FFN_SUBS:
            g = _dot(u2, w_sc[:, a:b])
            up = _dot(u2, w_sc[:, FFN + a:FFN + b])
            sg = _sigmoid(g)
            silu = g * sg
            silu_ref[:, a:b] = silu.astype(BF16)
            dsilu_ref[:, a:b] = (up * (sg * (1.0 + g * (1.0 - sg)))).astype(BF16)
            act_ref[:, a:b] = (silu * up).astype(BF16)

    return _hosted_call(
        body,
        name="ffn_fwd_gate",
        grid=(rows // tm,),
        in_specs=[pl.BlockSpec((tm, D), lambda i: (i, 0)), _ANY],
        out_specs=[hid(), hid(), hid()],
        out_shape=[jax.ShapeDtypeStruct((rows, FFN), BF16)] * 3,
        scratch_shapes=[pltpu.VMEM((D, 2 * FFN), BF16), pltpu.SemaphoreType.DMA],
        args=(u2, w_ffn_in),
        riders=riders,
    )


def _ffn_out_loss(act, h1, w_ffn_out, target, final_norm_w):
    rows = act.shape[0]
    tm = _heavy_tile(rows)
    nt = rows // tm

    def body(act_ref, h1_ref, wo_ref, fnw_ref, tgt_hbm, dh2_ref, stats_ref, tgt_sc, tgt_sem):
        i = pl.program_id(0)
        slot = i % 2

        @pl.when(i == 0)
        def _():
            stats_ref[...] = jnp.zeros_like(stats_ref)
            tgt_sc[0, 0:FRONT, :] = jnp.zeros((FRONT, D), F32)
            cp = pltpu.make_async_copy(tgt_hbm.at[pl.ds(0, tm - FRONT)], tgt_sc.at[0, pl.ds(FRONT, tm - FRONT)], tgt_sem.at[0])
            cp.start()
            cp.wait()

        @pl.when(i + 1 < nt)
        def _():
            start = pl.multiple_of((i + 1) * tm - FRONT, FRONT)
            pltpu.make_async_copy(tgt_hbm.at[pl.ds(start, tm)], tgt_sc.at[1 - slot], tgt_sem.at[1 - slot]).start()

        @pl.when(i > 0)
        def _():
            pltpu.make_async_copy(tgt_hbm.at[pl.ds(0, tm)], tgt_sc.at[slot], tgt_sem.at[slot]).wait()

        h2 = h1_ref[...] + _dot(act_ref[...], wo_ref[...])
        rs = lax.rsqrt(jnp.mean(h2 * h2, axis=-1, keepdims=True) + EPS)
        n = h2 * rs
        fnw = fnw_ref[...]
        valid = _row_ids(i, tm, (tm, D)) >= FRONT
        diff = jnp.where(valid, n * fnw - tgt_sc[slot], 0.0)
        dy = diff * (1.0 / D)
        stats_ref[0:1, :] += (0.5 / D) * jnp.sum(diff * diff, axis=0, keepdims=True)
        stats_ref[1:2, :] += jnp.sum(dy * n, axis=0, keepdims=True)
        dn = dy * fnw
        dh2_ref[...] = rs * (dn - n * jnp.mean(dn * n, axis=-1, keepdims=True))

    return pl.pallas_call(
        body,
        name="ffn_out_loss",
        grid=(nt,),
        in_specs=[
            pl.BlockSpec((tm, FFN), lambda i: (i, 0)),
            pl.BlockSpec((tm, D), lambda i: (i, 0)),
            pl.BlockSpec((FFN, D), lambda i: (0, 0)),
            pl.BlockSpec((1, D), lambda i: (0, 0)),
            _ANY,
        ],
        out_specs=[pl.BlockSpec((tm, D), lambda i: (i, 0)), pl.BlockSpec((8, D), lambda i: (0, 0))],
        out_shape=[jax.ShapeDtypeStruct((rows, D), F32), jax.ShapeDtypeStruct((8, D), F32)],
        scratch_shapes=[pltpu.VMEM((2, tm, D), F32), pltpu.SemaphoreType.DMA((2,))],
        compiler_params=_params(("arbitrary",)),
    )(act, h1, w_ffn_out, final_norm_w, target)


def _ffn_bwd(dh2, silu, dsilu, h1, rstd2, w_ffn_in, w_ffn_out, ffn_norm_w):
    rows = dh2.shape[0]
    tm = _heavy_tile(rows)
    blk = lambda: pl.BlockSpec((tm, FFN), lambda i: (i, 0))

    def gate_body(dh2_ref, silu_ref, dsilu_ref, wo_hbm, dg_ref, dup_ref, dh2b_ref, wo_sc, wo_sem):
        @pl.when(pl.program_id(0) == 0)
        def _():
            cp = pltpu.make_async_copy(wo_hbm, wo_sc, wo_sem)
            cp.start()
            cp.wait()

        dh2b = dh2_ref[...].astype(BF16)
        dh2b_ref[...] = dh2b
        for a, b in _FFN_SUBS:
            dact = _dot_nt(dh2b, wo_sc[a:b, :])
            dup_ref[:, a:b] = (dact * silu_ref[:, a:b].astype(F32)).astype(BF16)
            dg_ref[:, a:b] = (dact * dsilu_ref[:, a:b].astype(F32)).astype(BF16)

    dg, dup, dh2b = pl.pallas_call(
        gate_body,
        name="ffn_bwd_gate",
        grid=(rows // tm,),
        in_specs=[pl.BlockSpec((tm, D), lambda i: (i, 0)), blk(), blk(), _ANY],
        out_specs=[blk(), blk(), pl.BlockSpec((tm, D), lambda i: (i, 0))],
        out_shape=[jax.ShapeDtypeStruct((rows, FFN), BF16)] * 2 + [jax.ShapeDtypeStruct((rows, D), BF16)],
        scratch_shapes=[pltpu.VMEM((FFN, D), BF16), pltpu.SemaphoreType.DMA],
        compiler_params=_params(("arbitrary",)),
    )(dh2, silu, dsilu, w_ffn_out)

    def body(dg_ref, dup_ref, dh2_ref, h1_ref, rstd_ref, w_hbm, nw_ref, dh1_ref, stats_ref, w_sc, w_sem):
        @pl.when(pl.program_id(0) == 0)
        def _():
            stats_ref[...] = jnp.zeros_like(stats_ref)
            cp = pltpu.make_async_copy(w_hbm, w_sc, w_sem)
            cp.start()
            cp.wait()

        du = _dot_nt(dg_ref[...], w_sc[:, 0:FFN]) + _dot_nt(dup_ref[...], w_sc[:, FFN:2 * FFN])
        rs = rstd_ref[...]
        n = h1_ref[...] * rs
        stats_ref[0:1, :] += jnp.sum(du * n, axis=0, keepdims=True)
        dn = du * nw_ref[...]
        dh1_ref[...] = dh2_ref[...] + rs * (dn - n * jnp.mean(dn * n, axis=-1, keepdims=True))

    row = lambda width: pl.BlockSpec((tm, width), lambda i: (i, 0))
    dh1, stats = pl.pallas_call(
        body,
        name="ffn_bwd_in",
        grid=(rows // tm,),
        in_specs=[row(FFN), row(FFN), row(D), row(D), row(1), _ANY, pl.BlockSpec((1, D), lambda i: (0, 0))],
        out_specs=[row(D), pl.BlockSpec((8, D), lambda i: (0, 0))],
        out_shape=[jax.ShapeDtypeStruct((rows, D), F32), jax.ShapeDtypeStruct((8, D), F32)],
        scratch_shapes=[pltpu.VMEM((D, 2 * FFN), BF16), pltpu.SemaphoreType.DMA],
        compiler_params=_params(("arbitrary",)),
    )(dg, dup, dh2, h1, rstd2, w_ffn_in, ffn_norm_w)
    return dg, dup, dh2b, dh1, stats


def _mix_bwd(dh1, o, proj, h_lru, y_ret, y_lru, w_br, w_bl, w_o, riders=()):
    rows = dh1.shape[0]
    tm = TM_MIX_BWD
    nt = rows // tm

    def body(dh1_ref, o_ref, gates_ref, hl_ref, yret_ref, ylru_ref, wbr_hbm, wbl_hbm, wo_hbm,
             dproj_ref, do_ref, dhl_ref, mixed_ref, aret_ref, alru_ref, dyret_ref, dylru_ref, w_sc, w_sem):
        @pl.when(pl.program_id(0) == 0)
        def _():
            copies = [pltpu.make_async_copy(src, w_sc.at[q], w_sem.at[q]) for q, src in enumerate((wbr_hbm, wbl_hbm, wo_hbm))]
            for cp in copies:
                cp.start()
            for cp in copies:
                cp.wait()

        wbr_ref, wbl_ref, wo_ref = w_sc.at[0], w_sc.at[1], w_sc.at[2]
        gate = lambda j: gates_ref[:, j * D:(j + 1) * D].astype(F32)
        dmixed = _dot_nt(dh1_ref[...].astype(BF16), wo_ref[...])
        y_ret, y_lru = yret_ref[...].astype(F32), ylru_ref[...].astype(F32)
        sa, sb = _sigmoid(gate(2)), _sigmoid(gate(3))
        mixed_ref[...] = (sa * y_ret + sb * y_lru).astype(BF16)
        dga = dmixed * y_ret * sa * (1.0 - sa)
        dgb = dmixed * y_lru * sb * (1.0 - sb)
        dy_ret = (dmixed * sa).astype(BF16)
        dy_lru = (dmixed * sb).astype(BF16)
        dyret_ref[...] = dy_ret
        dylru_ref[...] = dy_lru
        da_ret = _dot_nt(dy_ret, wbr_ref[...])
        da_lru = _dot_nt(dy_lru, wbl_ref[...])

        gret = gate(0)
        sg = _sigmoid(gret)
        silu = gret * sg
        on, rstds = _group_norm(o_ref[...].astype(F32))
        aret_ref[...] = (silu * on).astype(BF16)
        dgret = da_ret * on * (sg * (1.0 + gret * (1.0 - sg)))
        don = da_ret * silu
        for h in range(HEADS):
            sl = slice(h * DH, (h + 1) * DH)
            onh, donh = on[:, sl], don[:, sl]
            do_ref[:, sl] = (rstds[h] * (donh - onh * jnp.mean(donh * onh, axis=-1, keepdims=True))).astype(BF16)

        gl, gl_grad = _gelu_and_grad(gate(1))
        hl = hl_ref[...].astype(F32)
        alru_ref[...] = (gl * hl).astype(BF16)
        dlgate = da_lru * hl * gl_grad
        dhl_ref[...] = (da_lru * gl).astype(BF16)

        for j, val in enumerate((dgret, dlgate, dga, dgb)):
            dproj_ref[:, j * D:(j + 1) * D] = val.astype(BF16)

    tile = lambda col: pl.BlockSpec((tm, D), lambda i: (i, col))
    gates = pl.BlockSpec((tm, 4 * D), lambda i: (i, GATES_COL))
    bf = lambda: jax.ShapeDtypeStruct((rows, D), BF16)
    return _hosted_call(
        body,
        name="mix_bwd",
        grid=(nt,),
        in_specs=[tile(0), tile(0), gates, tile(0), tile(0), tile(0), _ANY, _ANY, _ANY],
        out_specs=[pl.BlockSpec((tm, 4 * D), lambda i: (i, GATES_COL))] + [tile(0)] * 7,
        out_shape=[jax.ShapeDtypeStruct((rows, IN_COLS), BF16)] + [bf() for _ in range(7)],
        scratch_shapes=[pltpu.VMEM((3, D, D), BF16), pltpu.SemaphoreType.DMA((3,))],
        args=(dh1, o, proj, h_lru, y_ret, y_lru, w_br, w_bl, w_o),
        riders=riders,
    )


def _retention_bwd(dproj, proj_bf, do, sprev, intra, intra_t, q_dec, k_dec, c_dec, cos_t, sin_t, riders=()):
    rows = proj_bf.shape[0]
    nc = rows // CHUNK

    def body(dproj_in_ref, q_ref, k_ref, v_ref, do_ref, sprev_ref, m_ref, mt_ref, qd_ref, kd_ref, cd_ref, cos_ref, sin_ref,
             dproj_ref, ds_sc):
        @pl.when(pl.program_id(0) == 0)
        def _():
            ds_sc[...] = jnp.zeros_like(ds_sc)

        cos, sin = cos_ref[...], sin_ref[...]

        def unrotate(dy):
            return dy * cos - pltpu.roll(dy, DH // 2, axis=1) * sin

        for h in range(HEADS):
            sl = slice(h * DH, (h + 1) * DH)
            q, k, v = q_ref[:, sl], k_ref[:, sl], v_ref[:, sl]
            do = do_ref[:, sl]
            dob = do.astype(BF16)
            doq = (do * qd_ref[:, sl]).astype(BF16)
            state_prev = sprev_ref[0, h]
            dstate = ds_sc[h]
            dstate_b = dstate.astype(BF16)
            s_t = (_dot_nt(k, q) * mt_ref[h]).astype(BF16)
            ds_t = (_dot_nt(v, dob) * mt_ref[h]).astype(BF16)
            ds = (_dot_nt(dob, v) * m_ref[h]).astype(BF16)
            kd = kd_ref[:, sl]
            dq = _dot(ds, k) + _dot_nt(doq, state_prev)
            dk = _dot(ds_t, q) + _dot_nt(v, dstate_b) * kd
            k_scaled = (k.astype(F32) * kd).astype(BF16)
            dv = _dot(s_t, dob) + _dot(k_scaled, dstate_b)
            ds_sc[h] = dstate * cd_ref[:, sl] + _dot_tn(q, doq)
            for part, val in enumerate((unrotate(dq), unrotate(dk) * QK_SCALE, dv)):
                dproj_ref[:, part * D + h * DH:part * D + (h + 1) * DH] = val.astype(BF16)

    rev = lambda c: nc - 1 - c
    chunk_spec = lambda col: pl.BlockSpec((CHUNK, D), lambda c: (rev(c), col))
    const2 = lambda shape: pl.BlockSpec(shape, lambda c: (0, 0))
    const3 = pl.BlockSpec((HEADS, CHUNK, CHUNK), lambda c: (0, 0, 0))
    return _hosted_call(
        body,
        name="retention_bwd",
        grid=(nc,),
        in_specs=[
            pl.BlockSpec(memory_space=pl.ANY),
            chunk_spec(0), chunk_spec(1), chunk_spec(2), chunk_spec(0),
            pl.BlockSpec((1, HEADS, DH, DH), lambda c: (rev(c), 0, 0, 0)),
            const3, const3,
            const2((CHUNK, D)), const2((CHUNK, D)), const2((1, D)),
            pl.BlockSpec((CHUNK, DH), lambda c: (rev(c), 0)),
            pl.BlockSpec((CHUNK, DH), lambda c: (rev(c), 0)),
        ],
        out_specs=[pl.BlockSpec((CHUNK, 3 * D), lambda c: (rev(c), 0))],
        out_shape=[jax.ShapeDtypeStruct(dproj.shape, BF16)],
        aliases={0: 0},
        scratch_shapes=[pltpu.VMEM((HEADS, DH, DH), F32)],
        args=(dproj, proj_bf, proj_bf, proj_bf, do, sprev, intra, intra_t, q_dec, k_dec, c_dec, cos_t, sin_t),
        riders=riders,
    )


def _lru_bwd(dproj, proj, saved, dhl, conv_w, wa, wx, lam, riders=()):
    rows = proj.shape[0]
    TM = _lru_tile(rows)
    nt = rows // TM
    per8 = TM // 8

    def body(dproj_in_ref, x_ref, xprev_ref, h_ref, hprev_ref, c_ref, r_ref, i_ref, la_ref, mult_ref, dhl_ref,
             cw_ref, wa_ref, wx_ref, lam_ref,
             dproj_ref, dwa_ref, dwx_ref, stats_ref, anext_sc, dhnext_sc, dcnext_sc, c_sc, b_sc, dh_sc):
        step = pl.program_id(0)
        i = nt - 1 - step

        @pl.when(step == 0)
        def _():
            anext_sc[...] = jnp.zeros_like(anext_sc)
            dhnext_sc[...] = jnp.zeros_like(dhnext_sc)
            dcnext_sc[...] = jnp.zeros_like(dcnext_sc)
            dwa_ref[...] = jnp.zeros_like(dwa_ref)
            dwx_ref[...] = jnp.zeros_like(dwx_ref)
            stats_ref[...] = jnp.zeros_like(stats_ref)

        first = i == 0
        x = x_ref[...].astype(F32)
        prev8 = jnp.where(first, 0.0, xprev_ref[8:16, :].astype(F32))
        c_b = c_ref[...]
        c, r, gate_i, mult = (ref[...].astype(F32) for ref in (c_ref, r_ref, i_ref, mult_ref))
        a = jnp.exp(la_ref[...].astype(F32))
        sp = _softplus(-lam_ref[...])
        dh_sc[...] = dhl_ref[...].astype(F32)

        b_sc[...] = pltpu.roll(a, TM - 1, axis=0)
        b_sc[TM - 1:TM, :] = anext_sc[0:1, :]
        anext_sc[0:1, :] = a[0:1, :]
        row8 = lax.broadcasted_iota(jnp.int32, (8, D), 0)

        def group(gi, dhnext):
            r0 = pl.multiple_of((per8 - 1 - gi) * 8, 8)
            bb = b_sc[pl.ds(r0, 8), :]
            uu = dh_sc[pl.ds(r0, 8), :]
            for d in (1, 2, 4):
                b_sh = jnp.where(row8 < 8 - d, pltpu.roll(bb, 8 - d, axis=0), 1.0)
                u_sh = jnp.where(row8 < 8 - d, pltpu.roll(uu, 8 - d, axis=0), 0.0)
                uu = uu + bb * u_sh
                bb = bb * b_sh
            dhb = bb * dhnext + uu
            dh_sc[pl.ds(r0, 8), :] = dhb
            return dhb[0:1, :]

        dhfirst = lax.fori_loop(0, per8, group, dhnext_sc[0:1, :])
        dhnext_sc[0:1, :] = dhfirst
        dh = dh_sc[...]

        h = h_ref[...].astype(F32)
        hprev8 = jnp.where(first, 0.0, hprev_ref[8:16, :].astype(F32))
        h_dn, h_head = _shift_down(h, hprev8, 1)
        c_sc[...] = h_dn
        c_sc[0:8, :] = h_head
        h_before = c_sc[...]

        valid = _row_ids(i, TM, (TM, D)) >= PAD_ROWS
        da = dh * h_before
        du = jnp.where(valid, dh, 0.0)
        dmult = du * gate_i * c
        dgate_i = du * mult * c
        dc = du * mult * gate_i
        dla = da * a - dmult * (a * a) / mult
        dla = jnp.where(valid, dla, 0.0)
        dr = dla * ((-LRU_C) * sp)
        dzr = dr * r * (1.0 - r)
        dzi = dgate_i * gate_i * (1.0 - gate_i)
        stats_ref[1:2, :] += jnp.sum(dzr, axis=0, keepdims=True)
        stats_ref[2:3, :] += jnp.sum(dzi, axis=0, keepdims=True)
        stats_ref[3:4, :] += jnp.sum(dla * ((-LRU_C) * r), axis=0, keepdims=True)
        dc_gate = []
        for g in range(LRU_BLOCKS):
            sl = slice(g * LRU_BLOCK, (g + 1) * LRU_BLOCK)
            cg = c_b[:, sl]
            dzr_g = dzr[:, sl].astype(BF16)
            dzi_g = dzi[:, sl].astype(BF16)
            dc_gate.append(_dot_nt(dzr_g, wa_ref[g]) + _dot_nt(dzi_g, wx_ref[g]))
            dwa_ref[g] += _dot_tn(cg, dzr_g)
            dwx_ref[g] += _dot_tn(cg, dzi_g)
        dc = dc + jnp.concatenate(dc_gate, axis=1)

        cw = cw_ref[...]
        stats_ref[0:1, :] += jnp.sum(dc, axis=0, keepdims=True)
        stats_ref[7:8, :] += jnp.sum(dc * x, axis=0, keepdims=True)
        dx = cw[3:4] * dc
        tail_src = jnp.concatenate([dc[TM - 8:TM], dcnext_sc[...]], axis=0)
        dx_tail = cw[3:4] * dc[TM - 8:TM]
        for d in (1, 2, 3):
            dx = dx + cw[3 - d:4 - d] * pltpu.roll(dc, TM - d, axis=0)
            dx_tail = dx_tail + cw[3 - d:4 - d] * pltpu.roll(tail_src, 16 - d, axis=0)[0:8]
            rolled, hd = _shift_down(x, prev8, d)
            b_sc[...] = rolled
            b_sc[0:8, :] = hd
            stats_ref[7 - d:8 - d, :] += jnp.sum(dc * b_sc[...], axis=0, keepdims=True)
        dcnext_sc[...] = dc[0:8]
        dproj_ref[...] = dx.astype(BF16)
        dproj_ref[TM - 8:TM, :] = dx_tail.astype(BF16)

    rev = lambda s: nt - 1 - s
    vec = pl.BlockSpec((1, D), lambda s: (0, 0))
    wspec = pl.BlockSpec((LRU_BLOCKS, LRU_BLOCK, LRU_BLOCK), lambda s: (0, 0, 0))
    prev16 = lambda col: pl.BlockSpec((BF16_ROWS, D), lambda s: (jnp.maximum(rev(s) * (TM // BF16_ROWS) - 1, 0), col))
    tile = lambda: pl.BlockSpec((TM, D), lambda s: (rev(s), 0))
    h_lru, c_sv, r_sv, i_sv, la_sv, mult_sv = saved
    return _hosted_call(
        body,
        name="lru_bwd",
        grid=(nt,),
        in_specs=[
            pl.BlockSpec(memory_space=pl.ANY),
            pl.BlockSpec((TM, D), lambda s: (rev(s), LRU_IN_COL)), prev16(LRU_IN_COL),
            tile(), prev16(0),
            tile(), tile(), tile(), tile(), tile(), tile(),
            pl.BlockSpec((4, D), lambda s: (0, 0)),
            wspec, wspec, vec,
        ],
        out_specs=[
            pl.BlockSpec((TM, D), lambda s: (rev(s), LRU_IN_COL)),
            wspec, wspec,
            pl.BlockSpec((8, D), lambda s: (0, 0)),
        ],
        out_shape=[
            jax.ShapeDtypeStruct(dproj.shape, BF16),
            jax.ShapeDtypeStruct((LRU_BLOCKS, LRU_BLOCK, LRU_BLOCK), F32),
            jax.ShapeDtypeStruct((LRU_BLOCKS, LRU_BLOCK, LRU_BLOCK), F32),
            jax.ShapeDtypeStruct((8, D), F32),
        ],
        aliases={0: 0},
        scratch_shapes=[
            pltpu.VMEM((8, D), F32), pltpu.VMEM((8, D), F32), pltpu.VMEM((8, D), F32),
            pltpu.VMEM((TM, D), F32), pltpu.VMEM((TM, D), F32), pltpu.VMEM((TM, D), F32),
        ],
        args=(dproj, proj, proj, h_lru, h_lru, c_sv, r_sv, i_sv, la_sv, mult_sv, dhl, conv_w, wa, wx, lam),
        riders=riders,
    )


def _in_proj_bwd(dproj, w_in, part, prev=None, riders=()):
    rows = dproj.shape[0]
    tm = _heavy_tile(rows)
    nt = rows // tm
    split = max(1, nt // 4 + 1) if nt > 1 else 1
    first = 0 if part == 0 else split
    count = split if part == 0 else nt - split

    def body(*refs):
        dproj_ref, w_hbm, du_ref, w_sc, w_sem = refs[-5:]

        @pl.when(pl.program_id(0) == 0)
        def _():
            moves = [((0, 3), 0), ((4, 1), 3), ((3, 1), 4), ((5, 3), 5)]
            copies = [pltpu.make_async_copy(w_hbm.at[:, pl.ds(src * D, n * D)], w_sc.at[:, pl.ds(dst * D, n * D)], w_sem.at[q])
                      for q, ((src, n), dst) in enumerate(moves)]
            for cp in copies:
                cp.start()
            for cp in copies:
                cp.wait()

        du_ref[...] = _dot_nt(dproj_ref[...], w_sc[...])

    in_specs = [pl.BlockSpec((tm, IN_COLS), lambda i: (first + i, 0)), _ANY]
    args = (dproj, w_in)
    if prev is not None:
        in_specs = [_ANY] + in_specs
        args = (prev,) + args
    return _hosted_call(
        body,
        name="in_proj_bwd_%d" % part,
        grid=(count,),
        in_specs=in_specs,
        out_specs=[pl.BlockSpec((tm, D), lambda i: (first + i, 0))],
        out_shape=[jax.ShapeDtypeStruct((rows, D), F32)],
        scratch_shapes=[pltpu.VMEM((D, IN_COLS), BF16), pltpu.SemaphoreType.DMA((4,))],
        aliases={0: 0} if prev is not None else None,
        args=args,
        riders=riders,
    )


def _norm1_bwd(du, dh1, head, x2d, rstd1, norm_w, riders=()):
    rows = du.shape[0]

    def body(du_ref, dh1_ref, head_ref, x_ref, rstd_ref, nw_ref, gx_ref, ghead_ref, stats_ref):
        i = pl.program_id(0)

        @pl.when(i == 0)
        def _():
            stats_ref[...] = jnp.zeros_like(stats_ref)

        def finish(h0, out_ref):
            du = du_ref[...]
            rs = rstd_ref[...]
            n = h0 * rs
            stats_ref[0:1, :] += jnp.sum(du * n, axis=0, keepdims=True)
            dn = du * nw_ref[...]
            out_ref[...] = dh1_ref[...] + rs * (dn - n * jnp.mean(dn * n, axis=-1, keepdims=True))

        @pl.when(i == 0)
        def _():
            finish(head_ref[...], ghead_ref)

        @pl.when(i > 0)
        def _():
            finish(x_ref[...], gx_ref)

    tile = pl.BlockSpec((TM, D), lambda i: (i, 0))
    return _hosted_call(
        body,
        name="norm1_bwd",
        grid=(rows // TM,),
        in_specs=[
            tile, tile,
            pl.BlockSpec((FRONT, D), lambda i: (0, 0)),
            pl.BlockSpec((TM, D), lambda i: (jnp.maximum(i - 1, 0), 0)),
            pl.BlockSpec((TM, 1), lambda i: (i, 0)),
            pl.BlockSpec((1, D), lambda i: (0, 0)),
        ],
        out_specs=[
            pl.BlockSpec((TM, D), lambda i: (jnp.maximum(i - 1, 0), 0)),
            pl.BlockSpec((FRONT, D), lambda i: (0, 0)),
            pl.BlockSpec((8, D), lambda i: (0, 0)),
        ],
        out_shape=[
            jax.ShapeDtypeStruct(x2d.shape, F32),
            jax.ShapeDtypeStruct((FRONT, D), F32),
            jax.ShapeDtypeStruct((8, D), F32),
        ],
        scratch_shapes=[],
        args=(du, dh1, head, x2d, rstd1, norm_w),
        riders=riders,
    )


def _matmul_tn(name, x, dy, out_cols, col0=0, prev=None, k_block=None, n_block=None, riders=(), col_map=None):
    col_map = col_map or (lambda n: n)
    rows, kdim = x.shape
    ndim = dy.shape[1]
    kb = k_block or kdim
    nb = n_block or ndim
    step_bytes = lambda t: 2 * t * (kb * x.dtype.itemsize + nb * dy.dtype.itemsize) + 2 * kb * nb * 4
    tr = next(t for t in (2816, 1408, TM_HEAVY, TM) if rows % t == 0 and (t == TM or step_bytes(t) <= TN_VMEM_BUDGET))
    nr, nk, nn = rows // tr, kdim // kb, ndim // nb
    cb0 = col0 // nb

    def body(*refs):
        x_ref, dy_ref, out_ref = refs[-3], refs[-2], refs[-1]
        part = _dot_tn(x_ref[...].astype(BF16), dy_ref[...].astype(BF16))

        @pl.when(pl.program_id(2) == 0)
        def _():
            out_ref[...] = part

        @pl.when(pl.program_id(2) > 0)
        def _():
            out_ref[...] += part

    in_specs = [
        pl.BlockSpec((tr, kb), lambda n, k, r: (r, k)),
        pl.BlockSpec((tr, nb), lambda n, k, r: (r, n)),
    ]
    args = [x, dy]
    aliases = {}
    if prev is not None:
        in_specs = [pl.BlockSpec(memory_space=pl.ANY)] + in_specs
        args = [prev] + args
        aliases = {0: 0}
    (out,), rider_outs = _hosted_call(
        body,
        name=name,
        grid=(nn, nk, nr),
        in_specs=in_specs,
        out_specs=[pl.BlockSpec((kb, nb), lambda n, k, r: (k, cb0 + col_map(n)))],
        out_shape=[jax.ShapeDtypeStruct((kdim, out_cols), F32)],
        scratch_shapes=[],
        aliases=aliases,
        args=args,
        riders=riders,
    )
    return (out, rider_outs) if riders else out


def _local_step(x2d, target, w, plan):
    rows = FRONT + x2d.shape[0]
    head = jnp.concatenate([jnp.zeros((PAD_ROWS, D), F32), w["meta_tokens"]], axis=0)
    cos_t, sin_t = _rope_tables(rows)
    intra, intra_t, q_dec, k_dec, c_dec = _decay_tables()
    grads = {}

    def hosted(host, fn, *args, **kwargs):
        outs, rider_outs = fn(*args, riders=plan.riders(host, w, grads), **kwargs)
        plan.after(host, rider_outs, w, grads)
        return outs

    (u1, rstd1), _ = _norm1(head, x2d, w["mix_norm_w"])
    proj, w["w_in"] = hosted("in_proj", _in_proj, u1, w["w_in_shard"], w["route"], cos_t, sin_t)
    o, sprev = hosted("retention_fwd", _retention_fwd, proj, intra, q_dec, k_dec, c_dec)
    lru_args = (w["conv_w"], w["conv_b"], w["lru_wa"], w["lru_wx"], w["lru_ba"], w["lru_bx"], w["lru_lambda"])
    lru_saved = hosted("lru_fwd", _lru_fwd, proj, *lru_args)
    h_lru = lru_saved[0]
    y_ret, y_lru, h1, u2, rstd2 = hosted("mix_fwd", _mix_fwd, head, x2d, o, proj, h_lru, w["w_branch_ret"],
                                         w["w_branch_lru"], w["w_out"], w["ffn_norm_w"])
    silu, dsilu, act = hosted("ffn_fwd_gate", _ffn_fwd_gate, u2, w["w_ffn_in"])
    dh2, stats_loss = _ffn_out_loss(act, h1, w["w_ffn_out"], target, w["final_norm_w"])

    dg, dup, dh2b, dh1, stats_ffn = _ffn_bwd(dh2, silu, dsilu, h1, rstd2, w["w_ffn_in"], w["w_ffn_out"], w["ffn_norm_w"])
    grads["w_ffn_in"] = _matmul_tn("dw_ffn_up", u2, dup, 2 * FFN, col0=FFN, n_block=FFN_HALF,
                                   prev=_matmul_tn("dw_ffn_gate", u2, dg, 2 * FFN, n_block=FFN_HALF))
    grads["w_ffn_out"] = _matmul_tn("dw_ffn_out", act, dh2b, D, k_block=FFN_HALF)
    (dproj, do, dhl, mixed, a_ret, a_lru, dy_ret, dy_lru) = hosted(
        "mix_bwd", _mix_bwd, dh1, o, proj, h_lru, y_ret, y_lru, w["w_branch_ret"], w["w_branch_lru"], w["w_out"])
    grads["w_out"] = _matmul_tn("dw_out", mixed, dh1, D)
    grads["w_branch_ret"] = _matmul_tn("dw_branch_ret", a_ret, dy_ret, D)
    grads["w_branch_lru"] = _matmul_tn("dw_branch_lru", a_lru, dy_lru, D)
    (dproj,) = hosted("retention_bwd", _retention_bwd, dproj, proj, do, sprev, intra, intra_t, q_dec, k_dec, c_dec,
                      cos_t, sin_t)
    dproj, grads["lru_wa"], grads["lru_wx"], stats_lru = hosted(
        "lru_bwd", _lru_bwd, dproj, proj, lru_saved, dhl, w["conv_w"], w["lru_wa"], w["lru_wx"], w["lru_lambda"])
    grads["w_in"], rider_outs = _matmul_tn("dw_in", u1, dproj, IN_COLS, n_block=D, col_map=_swap_3_4,
                                           riders=plan.riders("dw_in", w, grads))
    plan.after("dw_in", rider_outs, w, grads)
    (du1,) = hosted("in_proj_bwd_0", _in_proj_bwd, dproj, w["w_in"], 0)
    (du1,) = hosted("in_proj_bwd_1", _in_proj_bwd, dproj, w["w_in"], 1, du1)
    (grad_x, grad_head, stats_in), _ = _norm1_bwd(du1, dh1, head, x2d, rstd1, w["mix_norm_w"])
    return grad_x, grad_head, grads, [stats_loss, stats_ffn, stats_in, stats_lru]


BIG_PIECES = {
    "w_in": ("col", (D, 2 * D)),
    "w_ffn_in": ("col", (D, FFN_HALF)),
    "w_ffn_out": ("row", (FFN // 4, D)),
    "w_branch_ret": ("row", (D // 4, D)),
    "w_branch_lru": ("row", (D // 4, D)),
    "w_out": ("row", (D // 4, D)),
    "lru_wa": ("lru", (LRU_BLOCKS, LRU_BLOCK // 4, LRU_BLOCK)),
    "lru_wx": ("lru", (LRU_BLOCKS, LRU_BLOCK // 4, LRU_BLOCK)),
}
SMALL_PIECES = {"meta_tokens": ("col", (N_META, D // 4)), "conv_w": ("col", (4, D // 4))}


def _full_shape(kind, shard):
    if kind == "col":
        return (shard[0], 4 * shard[1])
    if kind == "row":
        return (4 * shard[0], shard[1])
    return (shard[0], 4 * shard[1], shard[2])


def _half_shape(kind, shard):
    return (shard[0] // 2,) + tuple(shard[1:])


def _aligned(start, multiple):
    return start if isinstance(start, int) else pl.multiple_of(start, multiple)


def _lead(h, size):
    if h is None:
        return pl.ds(0, size)
    return pl.ds(_aligned(h * (size // 2), size // 2), size // 2)


def _full_region(ref, kind, shard, s, h):
    if kind == "col":
        return ref.at[_lead(h, shard[0]), pl.ds(_aligned(s * shard[1], shard[1]), shard[1])]
    if kind == "row":
        size = shard[0] if h is None else shard[0] // 2
        start = s * shard[0] + (0 if h is None else h * (shard[0] // 2))
        return ref.at[pl.ds(_aligned(start, BF16_ROWS), size), :]
    return ref.at[_lead(h, shard[0]), pl.ds(_aligned(s * shard[1], shard[1]), shard[1]), :]


def _shard_region(ref, shard, h):
    return ref.at[_lead(h, shard[0])]


def _place():
    x, y, c = lax.axis_index("x"), lax.axis_index("y"), lax.axis_index("c")
    return x, y, c, 2 * x + y


def _other_chip(s, c, k):
    s2 = jnp.bitwise_xor(s, k)
    return s2, (s2 // 2, s2 % 2, c)


def _remote(src, dst, send_sem, recv_sem, dev):
    return pltpu.make_async_remote_copy(src_ref=src, dst_ref=dst, send_sem=send_sem, recv_sem=recv_sem,
                                        device_id=dev, device_id_type=MESH)


_ANY = pl.BlockSpec(memory_space=pl.ANY)


class _Rider:
    def __init__(self, ins, out_shapes, sem_shapes, build, aliased=False):
        self.ins, self.out_shapes, self.sem_shapes, self.build, self.aliased = ins, out_shapes, sem_shapes, build, aliased


def _hosted_call(body, *, name, grid, in_specs, out_specs, out_shape, scratch_shapes, args, riders=(), aliases=None,
                 prefetch=None, riders_after_body=False):
    n_in, n_out, n_sc = len(in_specs), len(out_shape), len(scratch_shapes)
    r_in = [a for r in riders for a in r.ins]
    r_out = [s for r in riders for s in r.out_shapes]
    r_sem = [s for r in riders for s in r.sem_shapes]
    lead = () if prefetch is None else (prefetch,)
    assert prefetch is None or not (aliases or any(r.aliased for r in riders))

    def full_body(*refs):
        head, refs = refs[:len(lead)], refs[len(lead):]
        ins, rin = refs[:n_in], refs[n_in:n_in + len(r_in)]
        o0 = n_in + len(r_in)
        outs, rout = refs[o0:o0 + n_out], refs[o0 + n_out:o0 + n_out + len(r_out)]
        s0 = o0 + n_out + len(r_out)
        scratch, rsem = refs[s0:s0 + n_sc], refs[s0 + n_sc:]
        starts, waits = [], []
        pi = po = ps = 0
        for r in riders:
            st, wt = r.build(rin[pi:pi + len(r.ins)], rout[po:po + len(r.out_shapes)], rsem[ps:ps + len(r.sem_shapes)])
            starts += st
            waits += wt
            pi, po, ps = pi + len(r.ins), po + len(r.out_shapes), ps + len(r.sem_shapes)
        first = functools.reduce(jnp.logical_and, [pl.program_id(d) == 0 for d in range(len(grid))])
        last = functools.reduce(jnp.logical_and, [pl.program_id(d) == grid[d] - 1 for d in range(len(grid))])

        def start_riders():
            @pl.when(first)
            def _():
                for cp in starts:
                    cp.start()

        if riders and not riders_after_body:
            start_riders()
        body(*head, *ins, *outs, *scratch)
        if riders and riders_after_body:
            start_riders()
        if riders:
            @pl.when(last)
            def _():
                for wait in waits:
                    wait()

    io_aliases = dict(aliases or {})
    pi = po = 0
    for r in riders:
        if r.aliased:
            for q in range(len(r.ins)):
                io_aliases[n_in + pi + q] = n_out + po + q
        pi, po = pi + len(r.ins), po + len(r.out_shapes)
    specs = dict(
        grid=grid,
        in_specs=list(in_specs) + [_ANY] * len(r_in),
        out_specs=list(out_specs) + [_ANY] * len(r_out),
        scratch_shapes=list(scratch_shapes) + r_sem,
    )
    if prefetch is not None:
        specs = dict(grid_spec=pltpu.PrefetchScalarGridSpec(num_scalar_prefetch=1, **specs))
    res = pl.pallas_call(
        full_body,
        name=name,
        out_shape=list(out_shape) + r_out,
        input_output_aliases=io_aliases,
        compiler_params=_params(("arbitrary",) * len(grid)),
        **specs,
    )(*lead, *args, *r_in)
    rider_outs, po = [], n_out
    for r in riders:
        rider_outs.append(list(res[po:po + len(r.out_shapes)]))
        po += len(r.out_shapes)
    return list(res[:n_out]), rider_outs


def _run_riders(name, riders):
    r_in = [a for r in riders for a in r.ins]
    r_out = [s for r in riders for s in r.out_shapes]
    r_sem = [s for r in riders for s in r.sem_shapes]

    def body(*refs):
        rin, rout, rsem = refs[:len(r_in)], refs[len(r_in):len(r_in) + len(r_out)], refs[len(r_in) + len(r_out):]
        pi = po = ps = 0
        all_waits = []
        for r in riders:
            starts, waits = r.build(rin[pi:pi + len(r.ins)], rout[po:po + len(r.out_shapes)], rsem[ps:ps + len(r.sem_shapes)])
            for cp in starts:
                cp.start()
            all_waits += waits
            pi, po, ps = pi + len(r.ins), po + len(r.out_shapes), ps + len(r.sem_shapes)
        for wait in all_waits:
            wait()

    io_aliases = {}
    pi = po = 0
    for r in riders:
        if r.aliased:
            for q in range(len(r.ins)):
                io_aliases[pi + q] = po + q
        pi, po = pi + len(r.ins), po + len(r.out_shapes)
    res = pl.pallas_call(
        body,
        name=name,
        in_specs=[_ANY] * len(r_in),
        out_specs=[_ANY] * len(r_out),
        out_shape=r_out,
        scratch_shapes=r_sem,
        input_output_aliases=io_aliases,
    )(*r_in)
    outs, po = [], 0
    for r in riders:
        outs.append(list(res[po:po + len(r.out_shapes)]))
        po += len(r.out_shapes)
    return outs


def _piece(name):
    if name in BIG_PIECES:
        return (name, *BIG_PIECES[name], True)
    return (name, *SMALL_PIECES[name], False)


def _gather_rider(shards, names):
    pieces = [_piece(n) for n in names]
    n = len(pieces)

    def build(ins, outs, sems):
        local_sem, ici_send, ici_recv = sems
        _, _, c, s = _place()
        starts, waits = [], []
        for p, (_, kind, shard, split) in enumerate(pieces):
            cp = pltpu.make_async_copy(ins[p], _full_region(outs[p], kind, shard, s, None), local_sem.at[p])
            starts.append(cp)
            waits.append(cp.wait)
            h = c if split else None
            for k in (1, 2, 3):
                s2, dev = _other_chip(s, c, k)
                cp = _remote(_shard_region(ins[p], shard, h), _full_region(outs[p], kind, shard, s, h),
                             ici_send.at[p, k - 1], ici_recv.at[p, k - 1], dev)
                starts.append(cp)
                waits.append(cp.wait_send)
                region = _full_region(outs[p], kind, shard, s2, h)
                waits.append(_remote(region, region, ici_send.at[p, k - 1], ici_recv.at[p, k - 1], dev).wait_recv)
        return starts, waits

    return _Rider(
        [shards[name] for name in names],
        [jax.ShapeDtypeStruct(_full_shape(kind, shard), shards[name].dtype) for name, kind, shard, _ in pieces],
        [pltpu.SemaphoreType.DMA((n,)), pltpu.SemaphoreType.DMA((n, 3)), pltpu.SemaphoreType.DMA((n, 3))],
        build)


def _forward_rider(gathered, names):
    pieces = [_piece(n) for n in names]
    n = len(pieces)

    def build(ins, outs, sems):
        fwd_send, fwd_recv = sems
        x, y, c, s = _place()
        sibling = (x, y, 1 - c)
        starts, waits = [], []
        for p, (_, kind, shard, _) in enumerate(pieces):
            for k in (1, 2, 3):
                s2, _ = _other_chip(s, c, k)
                mine = _full_region(outs[p], kind, shard, s2, c)
                theirs = _full_region(outs[p], kind, shard, s2, 1 - c)
                cp = _remote(mine, mine, fwd_send.at[p, k - 1], fwd_recv.at[p, k - 1], sibling)
                starts.append(cp)
                waits.append(cp.wait_send)
                waits.append(_remote(theirs, theirs, fwd_send.at[p, k - 1], fwd_recv.at[p, k - 1], sibling).wait_recv)
        return starts, waits

    return _Rider(
        [gathered[name] for name in names],
        [jax.ShapeDtypeStruct(gathered[name].shape, gathered[name].dtype) for name in names],
        [pltpu.SemaphoreType.DMA((n, 3)), pltpu.SemaphoreType.DMA((n, 3))],
        build, aliased=True)


def _pair_exchange_rider(grads, names):
    n = len(names)

    def build(ins, outs, sems):
        send_sem, recv_sem = sems
        x, y, c, _ = _place()
        sibling = (x, y, 1 - c)
        starts, waits = [], []
        for p, name in enumerate(names):
            kind, shard = BIG_PIECES[name]
            for s2 in range(4):
                cp = _remote(_full_region(ins[p], kind, shard, s2, 1 - c), outs[p].at[s2], send_sem.at[p, s2],
                             recv_sem.at[p, s2], sibling)
                starts.append(cp)
                waits.append(cp.wait_send)
                waits.append(_remote(outs[p].at[s2], outs[p].at[s2], send_sem.at[p, s2], recv_sem.at[p, s2], sibling).wait_recv)
        return starts, waits

    return _Rider(
        [grads[name] for name in names],
        [jax.ShapeDtypeStruct((4,) + _half_shape(*BIG_PIECES[name]), F32) for name in names],
        [pltpu.SemaphoreType.DMA((n, 4))] * 2,
        build)


def _half_specs(kind, shard):
    half = shard[0] // 2
    if kind == "col":
        full = pl.BlockSpec((half, shard[1]), lambda j, pr: (pr[1], j))
        buf = pl.BlockSpec((None, half, shard[1]), lambda j, pr: (j, 0, 0))
    elif kind == "row":
        full = pl.BlockSpec((half, shard[1]), lambda j, pr: (2 * j + pr[1], 0))
        buf = pl.BlockSpec((None, half, shard[1]), lambda j, pr: (j, 0, 0))
    else:
        full = pl.BlockSpec((half, shard[1], shard[2]), lambda j, pr: (pr[1], j, 0))
        buf = pl.BlockSpec((None, half, shard[1], shard[2]), lambda j, pr: (j, 0, 0, 0))
    return full, buf


def _pair_sum(name, grad, recv, place):
    kind, shard = BIG_PIECES[name]
    full, buf = _half_specs(kind, shard)

    def body(pr, g_ref, r_ref, o_ref):
        o_ref[...] = (g_ref[...] + r_ref[...]).astype(BF16)

    return pl.pallas_call(
        body,
        name="pair_sum_" + name,
        grid_spec=pltpu.PrefetchScalarGridSpec(num_scalar_prefetch=1, grid=(4,), in_specs=[full, buf], out_specs=buf),
        out_shape=jax.ShapeDtypeStruct((4,) + _half_shape(kind, shard), BF16),
        compiler_params=_params(("arbitrary",)),
    )(place, grad, recv)


def _chip_exchange_rider(sums, names):
    n = len(names)

    def build(ins, outs, sems):
        send_sem, recv_sem = sems
        _, _, c, s = _place()
        starts, waits = [], []
        for p in range(n):
            for k in (1, 2, 3):
                s2, dev = _other_chip(s, c, k)
                cp = _remote(ins[p].at[s2], outs[p].at[k - 1], send_sem.at[p, k - 1], recv_sem.at[p, k - 1], dev)
                starts.append(cp)
                waits.append(cp.wait_send)
                waits.append(_remote(outs[p].at[k - 1], outs[p].at[k - 1], send_sem.at[p, k - 1], recv_sem.at[p, k - 1],
                                     dev).wait_recv)
        return starts, waits

    return _Rider(
        [sums[name] for name in names],
        [jax.ShapeDtypeStruct((3,) + _half_shape(*BIG_PIECES[name]), BF16) for name in names],
        [pltpu.SemaphoreType.DMA((n, 3))] * 2,
        build)


def _chip_sum(name, grad, recv_pair, recv_chip, place):
    kind, shard = BIG_PIECES[name]
    half = shard[0] // 2
    tail = tuple(shard[1:])
    zeros = (0,) * len(tail)
    nt = 4 if kind != "lru" and half % (4 * BF16_ROWS) == 0 else 1
    rows = half // nt
    if kind == "col":
        full = pl.BlockSpec((rows,) + tail, lambda j, pr: (pr[1] * nt + j, pr[0]))
    elif kind == "row":
        full = pl.BlockSpec((rows,) + tail, lambda j, pr: ((2 * pr[0] + pr[1]) * nt + j, 0))
    else:
        full = pl.BlockSpec((rows,) + tail, lambda j, pr: (pr[1], pr[0], 0))
    pair = pl.BlockSpec((None, rows) + tail, lambda j, pr: (pr[0], j) + zeros)
    chip = pl.BlockSpec((3, rows) + tail, lambda j, pr: (0, j) + zeros)
    out = pl.BlockSpec((rows,) + tail, lambda j, pr: (pr[1] * nt + j,) + zeros)

    def body(pr, g_ref, rp_ref, rc_ref, o_ref):
        total = g_ref[...] + rp_ref[...]
        for k in range(3):
            total = total + rc_ref[k].astype(F32)
        o_ref[...] = total

    return pl.pallas_call(
        body,
        name="chip_sum_" + name,
        grid_spec=pltpu.PrefetchScalarGridSpec(num_scalar_prefetch=1, grid=(nt,), in_specs=[full, pair, chip], out_specs=out),
        out_shape=jax.ShapeDtypeStruct(shard, F32),
        compiler_params=_params(("arbitrary",)),
    )(place, grad, recv_pair, recv_chip)


def _sibling_exchange_rider(halves, names):
    n = len(names)

    def build(ins, outs, sems):
        send_sem, recv_sem = sems
        x, y, c, _ = _place()
        sibling = (x, y, 1 - c)
        starts, waits = [], []
        for p, name in enumerate(names):
            shard = BIG_PIECES[name][1]
            mine = _shard_region(outs[p], shard, c)
            theirs = _shard_region(outs[p], shard, 1 - c)
            cp = _remote(mine, mine, send_sem.at[p], recv_sem.at[p], sibling)
            starts.append(cp)
            waits.append(cp.wait_send)
            waits.append(_remote(theirs, theirs, send_sem.at[p], recv_sem.at[p], sibling).wait_recv)
        return starts, waits

    return _Rider(
        [halves[name] for name in names],
        [jax.ShapeDtypeStruct(BIG_PIECES[name][1], F32) for name in names],
        [pltpu.SemaphoreType.DMA((n,))] * 2,
        build, aliased=True)


FIRST_WEIGHTS = ["meta_tokens", "conv_w"]
WEIGHT_GROUPS = {
    "lru": ["lru_wa", "lru_wx"],
    "branch": ["w_branch_ret", "w_branch_lru", "w_out"],
    "ffn_in": ["w_ffn_in"],
    "ffn_out": ["w_ffn_out"],
}
WEIGHT_SCHEDULE = {
    "in_proj": [("gather", "lru")],
    "retention_fwd": [("forward", "lru"), ("gather", "branch")],
    "lru_fwd": [("forward", "branch"), ("gather", "ffn_in")],
    "mix_fwd": [("forward", "ffn_in"), ("gather", "ffn_out")],
    "ffn_fwd_gate": [("forward", "ffn_out")],
}
GRAD_GROUPS = {
    "ffn_in": ["w_ffn_in"],
    "ffn_out": ["w_ffn_out"],
    "mixer": ["w_out", "w_branch_ret", "w_branch_lru", "lru_wa", "lru_wx"],
    "in": ["w_in"],
}
GRAD_SCHEDULE = {
    "mix_bwd": [("pair", "ffn_in"), ("pair", "ffn_out")],
    "retention_bwd": [("chip", "ffn_out")],
    "lru_bwd": [("chip", "ffn_in"), ("sibling", "ffn_out")],
    "dw_in": [("sibling", "ffn_in"), ("pair", "mixer")],
    "in_proj_bwd_0": [("chip", "mixer"), ("pair", "in")],
    "in_proj_bwd_1": [("sibling", "mixer"), ("chip", "in")],
}


class _CommPlan:
    def __init__(self, shards, place):
        self.shards, self.place = shards, place
        self.late = {}
        self.recv_pair, self.sums, self.recv_chip, self.halves, self.final = {}, {}, {}, {}, {}

    def _grad_rider(self, stage, group, grads):
        names = GRAD_GROUPS[group]
        if stage == "pair":
            return _pair_exchange_rider(grads, names)
        if stage == "chip":
            return _chip_exchange_rider(self.sums, names)
        return _sibling_exchange_rider(self.halves, names)

    def _grad_after(self, stage, group, outs, grads):
        names = GRAD_GROUPS[group]
        if stage == "pair":
            for n, o in zip(names, outs):
                self.recv_pair[n] = o
                self.sums[n] = _pair_sum(n, grads[n], o, self.place)
        elif stage == "chip":
            for n, o in zip(names, outs):
                self.halves[n] = _chip_sum(n, grads[n], self.recv_pair[n], o, self.place)
        else:
            self.final.update(zip(names, outs))

    def riders(self, host, w, grads):
        if host in WEIGHT_SCHEDULE:
            return [_gather_rider(self.shards, WEIGHT_GROUPS[group]) if stage == "gather"
                    else _forward_rider(self.late, WEIGHT_GROUPS[group]) for stage, group in WEIGHT_SCHEDULE[host]]
        return [self._grad_rider(stage, group, grads) for stage, group in GRAD_SCHEDULE.get(host, [])]

    def after(self, host, rider_outs, w, grads):
        for (stage, group), outs in zip(WEIGHT_SCHEDULE.get(host, []), rider_outs):
            (self.late if stage == "gather" else w).update(zip(WEIGHT_GROUPS[group], outs))
        for (stage, group), outs in zip(GRAD_SCHEDULE.get(host, []), rider_outs):
            self._grad_after(stage, group, outs, grads)

    def finish(self, partial):
        outs, (blocks,) = _run_riders("tail_exchange", [_sibling_exchange_rider(self.halves, GRAD_GROUPS["in"]),
                                                        _small_exchange_rider(partial)])
        self.final.update(zip(GRAD_GROUPS["in"], outs))
        return self.final, blocks


def _adamw_math(w, g, m, v):
    m = ADAM_B1 * m + (1.0 - ADAM_B1) * g
    v = ADAM_B2 * v + (1.0 - ADAM_B2) * (g * g)
    m_hat = m / (1.0 - ADAM_B1 ** ADAM_STEP)
    v_hat = v / (1.0 - ADAM_B2 ** ADAM_STEP)
    delta = -ADAM_LR * (m_hat / (jnp.sqrt(v_hat) + ADAM_EPS) + ADAM_WD * w)
    return delta, m, v


def _adamw(name, g, w, m, v):
    rows, cols = g.shape
    tr = rows // 4 if rows % 32 == 0 else rows

    def body(g_ref, w_ref, m_ref, v_ref, go_ref, d_ref, mo_ref, vo_ref):
        gv = g_ref[...]
        delta, m2, v2 = _adamw_math(w_ref[...], gv, m_ref[...], v_ref[...])
        go_ref[...] = gv
        d_ref[...] = delta
        mo_ref[...] = m2
        vo_ref[...] = v2

    spec = pl.BlockSpec((tr, cols), lambda i: (i, 0))
    return pl.pallas_call(
        body,
        name="adamw_" + name,
        grid=(rows // tr,),
        in_specs=[spec] * 4,
        out_specs=[spec] * 4,
        out_shape=[jax.ShapeDtypeStruct((rows, cols), F32)] * 4,
        compiler_params=_params(("arbitrary",)),
    )(g, w, m, v)


SMALL_ROWS = 48
VEC_ROWS = {"final_norm_w": 1, "ffn_norm_w": 8, "mix_norm_w": 16, "conv_b": 24, "lru_ba": 25, "lru_bx": 26, "lru_lambda": 27}
VEC_NAMES = list(VEC_ROWS)
CONV_W_ROW = 28
META_ROW = 32


def _small_exchange_rider(partial):
    def build(ins, outs, sems):
        local_sem, send_sem, recv_sem = sems
        _, _, c, s = _place()
        me = 2 * s + c
        cp = pltpu.make_async_copy(ins[0], outs[0].at[me], local_sem)
        starts, waits = [cp], [cp.wait]
        for k in range(1, 8):
            peer = jnp.bitwise_xor(me, k)
            dev = (peer // 4, (peer // 2) % 2, peer % 2)
            rd = _remote(ins[0], outs[0].at[me], send_sem.at[k - 1], recv_sem.at[k - 1], dev)
            starts.append(rd)
            waits.append(rd.wait_send)
            waits.append(_remote(ins[0], outs[0].at[peer], send_sem.at[k - 1], recv_sem.at[k - 1], dev).wait_recv)
        return starts, waits

    return _Rider([partial], [jax.ShapeDtypeStruct((8, SMALL_ROWS, D), F32)],
                  [pltpu.SemaphoreType.DMA, pltpu.SemaphoreType.DMA((7,)), pltpu.SemaphoreType.DMA((7,))], build)


def _small_update(blocks, place, vecs, conv, meta):
    nvec = len(VEC_NAMES)
    qcols = D // 4

    def in_order(ref):
        total = ref[0]
        for d in range(1, 8):
            total = total + ref[d]
        return total

    def body(pr, blocks_ref, cols_ref, *refs):
        vec_refs = refs[:3 * nvec]
        conv_refs = refs[3 * nvec:3 * nvec + 3]
        meta_refs = refs[3 * nvec + 3:3 * nvec + 6]
        outs = refs[3 * nvec + 6:]
        loss_ref, vec_out, conv_out, meta_out = outs[0], outs[1:5], outs[5:9], outs[9:13]
        tot, col = in_order(blocks_ref), in_order(cols_ref)
        loss_ref[...] = jnp.sum(tot[0:1, :], axis=1, keepdims=True)
        for o in vec_out:
            o[...] = jnp.zeros_like(o)
        for j, name in enumerate(VEC_NAMES):
            w, m, v = (r[...] for r in vec_refs[3 * j:3 * j + 3])
            g = tot[VEC_ROWS[name]:VEC_ROWS[name] + 1, :]
            if name == "lru_lambda":
                g = -g / (1.0 + jnp.exp(w))
            for o, val in zip(vec_out, (g,) + _adamw_math(w, g, m, v)):
                o[j:j + 1, :] = val
        for row, n_rows, ins, group in ((CONV_W_ROW, 4, conv_refs, conv_out), (META_ROW, N_META, meta_refs, meta_out)):
            g = col[row:row + n_rows, :]
            w, m, v = (r[...] for r in ins)
            for o, val in zip(group, (g,) + _adamw_math(w, g, m, v)):
                o[...] = val

    whole = lambda shape: pl.BlockSpec(shape, lambda i, pr: (0,) * len(shape))
    in_specs = [whole((8, SMALL_ROWS, D)), pl.BlockSpec((8, SMALL_ROWS, qcols), lambda i, pr: (0, 0, pr[0]))]
    in_specs += [whole((1, D))] * (3 * nvec) + [whole((4, qcols))] * 3 + [whole((N_META, qcols))] * 3
    out_shapes = [(1, 1)] + [(8, D)] * 4 + [(4, qcols)] * 4 + [(N_META, qcols)] * 4
    return pl.pallas_call(
        body,
        name="small_update",
        grid_spec=pltpu.PrefetchScalarGridSpec(num_scalar_prefetch=1, grid=(1,), in_specs=in_specs,
                                               out_specs=[whole(s) for s in out_shapes]),
        out_shape=[jax.ShapeDtypeStruct(s, F32) for s in out_shapes],
        compiler_params=_params(("arbitrary",)),
    )(place, blocks, blocks, *[a for t in vecs for a in t], *conv, *meta)


WEIGHT_ORDER = ["meta_tokens", "mix_norm_w", "w_in", "conv_w", "conv_b", "lru_wa", "lru_ba", "lru_wx", "lru_bx", "lru_lambda",
                "w_branch_ret", "w_branch_lru", "w_out", "ffn_norm_w", "w_ffn_in", "w_ffn_out", "final_norm_w"]


def kernel(x, meta_tokens, mix_norm_w, w_in, conv_w, conv_b, lru_wa, lru_ba, lru_wx, lru_bx, lru_lambda, w_branch_ret, w_branch_lru, w_out, ffn_norm_w, w_ffn_in, w_ffn_out, final_norm_w, loss_target, m_meta_tokens, m_mix_norm_w, m_w_in, m_conv_w, m_conv_b, m_lru_wa, m_lru_ba, m_lru_wx, m_lru_bx, m_lru_lambda, m_w_branch_ret, m_w_branch_lru, m_w_out, m_ffn_norm_w, m_w_ffn_in, m_w_ffn_out, m_final_norm_w, v_meta_tokens, v_mix_norm_w, v_w_in, v_conv_w, v_conv_b, v_lru_wa, v_lru_ba, v_lru_wx, v_lru_bx, v_lru_lambda, v_w_branch_ret, v_w_branch_lru, v_w_out, v_ffn_norm_w, v_w_ffn_in, v_w_ffn_out, v_final_norm_w):
    args = locals()
    wts = {n: args[n] for n in WEIGHT_ORDER}
    mom = {n: args["m_" + n] for n in WEIGHT_ORDER}
    var = {n: args["v_" + n] for n in WEIGHT_ORDER}
    place = jnp.stack([2 * lax.axis_index("x") + lax.axis_index("y"), lax.axis_index("c")]).astype(jnp.int32)

    shards = {n: wts[n][0].astype(BF16) for n in BIG_PIECES}
    shards["meta_tokens"] = wts["meta_tokens"]
    shards["conv_w"] = wts["conv_w"][0]
    plan = _CommPlan(shards, place)
    (first,) = _run_riders("gather_first", [_gather_rider(shards, FIRST_WEIGHTS)])
    w = dict(zip(FIRST_WEIGHTS, first))
    for n in VEC_NAMES:
        w[n] = wts[n].reshape(1, D)
    chips = jnp.bitwise_xor(place[0], jnp.array((0,) + SHARD_ORDER, dtype=jnp.int32))
    w["route"] = jnp.concatenate([place, jnp.stack([2 * chips, 2 * chips + 1], axis=1).reshape(8)])
    w["w_in_shard"] = shards["w_in"]

    grad_x, grad_head, _, stats = _local_step(x[0], loss_target[0], w, plan)
    partial = jnp.concatenate(stats + [grad_head[PAD_ROWS:]], axis=0)
    shard_grads, blocks = plan.finish(partial)

    out = {}
    for n in BIG_PIECES:
        shape2d = (-1, wts[n].shape[-1])
        res = _adamw(n, *[a.reshape(shape2d) for a in (shard_grads[n], wts[n], mom[n], var[n])])
        out[n] = [r.reshape(wts[n].shape) for r in res]

    vecs = [tuple(a[n].reshape(1, D) for a in (wts, mom, var)) for n in VEC_NAMES]
    conv = tuple(a["conv_w"][0] for a in (wts, mom, var))
    meta = tuple(a["meta_tokens"] for a in (wts, mom, var))
    res = _small_update(blocks, place, vecs, conv, meta)
    loss = res[0].reshape(())
    for j, n in enumerate(VEC_NAMES):
        out[n] = [r[j].reshape(wts[n].shape) for r in res[1:5]]
    out["conv_w"] = [r.reshape(wts["conv_w"].shape) for r in res[5:9]]
    out["meta_tokens"] = list(res[9:13])

    return (loss, grad_x.reshape(x.shape)) + tuple(out[n][kind] for kind in range(4) for n in WEIGHT_ORDER)
```

```python
import functools
import math

import jax
import jax.numpy as jnp
from jax import lax
from jax.experimental import pallas as pl
from jax.experimental.pallas import tpu as pltpu

F32 = jnp.float32
BF16 = jnp.bfloat16

LANES = 128
BF16_ROWS = 16

D = 1024
HEADS = 8
DH = 128
CHUNK = 256
N_META = 16
FRONT = 256
PAD_ROWS = FRONT - N_META
LRU_BLOCKS = 4
LRU_BLOCK = 256
LRU_C = 8.0
FFN = 2816
FFN_HALF = FFN // 2
IN_COLS = 8 * D
ROPE_BASE = 10000.0
EPS = 1e-6
QK_SCALE = DH ** -0.5

ADAM_LR = 0.001
ADAM_B1 = 0.9
ADAM_B2 = 0.999
ADAM_EPS = 1e-08
ADAM_WD = 0.01
ADAM_STEP = 10

TM = 256
TM_HEAVY = 768
MXU_TILE = 256
FFN_SUB = 2 * MXU_TILE
_FFN_SUBS = [(a, min(a + FFN_SUB, FFN)) for a in range(0, FFN, FFN_SUB)]
TM_MIX_BWD = 384
TM_LRU = 384
TM_MIX_FWD = 384
TM_IN_PROJ = 1408
VMEM_LIMIT = 60 * 1024 * 1024
TN_VMEM_BUDGET = 40 * 1024 * 1024

NT_DIMS = (((1,), (1,)), ((), ()))
TN_DIMS = (((0,), (0,)), ((), ()))
MESH = pl.DeviceIdType.MESH


def _params(sem=None):
    if sem is None:
        return pltpu.CompilerParams(vmem_limit_bytes=VMEM_LIMIT)
    return pltpu.CompilerParams(dimension_semantics=sem, vmem_limit_bytes=VMEM_LIMIT)


def _dot(a, b):
    return jnp.dot(a, b, preferred_element_type=F32)


def _dot_nt(a, b):
    return lax.dot_general(a, b, NT_DIMS, preferred_element_type=F32)


def _dot_tn(a, b):
    return lax.dot_general(a, b, TN_DIMS, preferred_element_type=F32)


def _sigmoid(z):
    return 1.0 / (1.0 + jnp.exp(-z))


def _log1p(x):
    return jnp.where(x < 1e-3, x * (1.0 - x * (0.5 - x * (1.0 / 3.0))), jnp.log(1.0 + x))


def _softplus(x):
    return jnp.maximum(x, 0.0) + _log1p(jnp.exp(-jnp.abs(x)))


def _one_minus_square(a, log_a):
    x = 2.0 * log_a
    series = -x * (1.0 + x * (0.5 + x * (1.0 / 6.0)))
    return jnp.where(x > -0.02, series, 1.0 - a * a)


_GELU_K = math.sqrt(2.0 / math.pi)


def _gelu_and_grad(x):
    inner = _GELU_K * (x + 0.044715 * x * x * x)
    t = jnp.tanh(inner)
    val = 0.5 * x * (1.0 + t)
    grad = 0.5 * (1.0 + t) + 0.5 * x * (1.0 - t * t) * _GELU_K * (1.0 + 3.0 * 0.044715 * x * x)
    return val, grad


def _row_ids(i, rows, shape):
    return i * rows + lax.broadcasted_iota(jnp.int32, shape, 0)


def _rope_tables(rows):
    inv_freq = ROPE_BASE ** (-jnp.arange(0, DH, 2, dtype=F32) / DH)
    block_pos = jnp.arange(rows // FRONT, dtype=jnp.int32) * FRONT - PAD_ROWS
    coarse = block_pos.astype(F32)[:, None] * inv_freq[None, :]
    fine = jnp.arange(FRONT, dtype=F32)[:, None] * inv_freq[None, :]
    ca, sa = jnp.cos(coarse)[:, None, :], jnp.sin(coarse)[:, None, :]
    cb, sb = jnp.cos(fine)[None, :, :], jnp.sin(fine)[None, :, :]
    cos = (ca * cb - sa * sb).reshape(rows, DH // 2)
    sin = (sa * cb + ca * sb).reshape(rows, DH // 2)
    return jnp.concatenate([cos, cos], axis=1), jnp.concatenate([-sin, sin], axis=1)


def _decay_tables():
    log_g = jnp.log(1.0 - 2.0 ** (-5.0 - jnp.arange(HEADS, dtype=F32)))
    idx = jnp.arange(CHUNK, dtype=F32)
    diff = idx[:, None] - idx[None, :]
    intra = jnp.where(diff[None] >= 0, jnp.exp(jnp.maximum(diff, 0.0)[None] * log_g[:, None, None]), 0.0)
    q_decay = jnp.exp((idx + 1.0)[:, None] * log_g[None, :])
    k_decay = jnp.exp((CHUNK - 1.0 - idx)[:, None] * log_g[None, :])
    chunk_decay = jnp.exp(CHUNK * log_g)
    wide = lambda a: jnp.repeat(a, DH, axis=-1)
    return intra, jnp.swapaxes(intra, 1, 2), wide(q_decay), wide(k_decay), wide(chunk_decay[None, :])


def _norm1(head, x2d, norm_w, riders=()):
    rows = FRONT + x2d.shape[0]
    tm = TM_IN_PROJ if rows % TM_IN_PROJ == 0 else TM_MIX_FWD
    nt = rows // tm

    def body(head_hbm, x_hbm, nw_ref, u_ref, rstd_ref, h0_sc, h0_sem):
        slot = _frame_rows(head_hbm, x_hbm, h0_sc, h0_sem, pl.program_id(0), tm, nt)
        hv = h0_sc[slot]
        rs = lax.rsqrt(jnp.mean(hv * hv, axis=-1, keepdims=True) + EPS)
        u_ref[...] = ((hv * rs) * nw_ref[...]).astype(BF16)
        rstd_ref[...] = rs

    return _hosted_call(
        body,
        name="norm1",
        grid=(nt,),
        in_specs=[_ANY, _ANY, pl.BlockSpec((1, D), lambda i: (0, 0))],
        out_specs=[pl.BlockSpec((tm, D), lambda i: (i, 0)), pl.BlockSpec((tm, 1), lambda i: (i, 0))],
        out_shape=[jax.ShapeDtypeStruct((rows, D), BF16), jax.ShapeDtypeStruct((rows, 1), F32)],
        scratch_shapes=[pltpu.VMEM((2, tm, D), F32), pltpu.SemaphoreType.DMA((3,))],
        args=(head, x2d, norm_w),
        riders=riders,
    )


def _heavy_tile(rows):
    return TM_HEAVY if rows % TM_HEAVY == 0 else TM


def _lru_tile(rows):
    return TM_LRU if rows % TM_LRU == 0 else TM


def _swap_3_4(group):
    return jnp.where(group == 3, 4, jnp.where(group == 4, 3, group))


SHARD_ORDER = (2, 3, 1)
LRU_IN_COL = 3
GATES_COL = 1


def _in_proj(u, w_shard, route, cos_t, sin_t, riders=()):
    rows = u.shape[0]
    tm = TM_IN_PROJ if rows % TM_IN_PROJ == 0 else _heavy_tile(rows)
    nt = rows // tm
    kind, shard = BIG_PIECES["w_in"]

    def body(route_ref, u_hbm, wsh_ref, cos_ref, sin_ref, proj_ref, wfull_ref,
             u_sc, w_sc, u_sem, w_sem, local_sem, ici_send, ici_recv, fwd_send, fwd_recv):
        g, i = pl.program_id(0), pl.program_id(1)
        s, c = route_ref[0], route_ref[1]
        gid = route_ref[2 + g]
        sibling = (s // 2, s % 2, 1 - c)
        local = pltpu.make_async_copy(wsh_ref, _full_region(wfull_ref, kind, shard, s, None), local_sem)
        u_copies = [pltpu.make_async_copy(u_hbm.at[pl.ds(t * tm, tm)], u_sc.at[t], u_sem.at[t]) for t in range(nt)]
        sends, arrivals = [], []
        for k in (1, 2, 3):
            s2, dev = _other_chip(s, c, k)
            sends.append(_remote(_shard_region(wsh_ref, shard, c), _full_region(wfull_ref, kind, shard, s, c),
                                 ici_send.at[k - 1], ici_recv.at[k - 1], dev))
            mine = _full_region(wfull_ref, kind, shard, s2, c)
            theirs = _full_region(wfull_ref, kind, shard, s2, 1 - c)
            arrivals.append((_remote(mine, mine, ici_send.at[k - 1], ici_recv.at[k - 1], dev),
                             _remote(mine, mine, fwd_send.at[k - 1], fwd_recv.at[k - 1], sibling),
                             _remote(theirs, theirs, fwd_send.at[k - 1], fwd_recv.at[k - 1], sibling)))

        slot = g % 2
        last_tile = i == nt - 1

        def block_copy(src, col, to_slot):
            return pltpu.make_async_copy(src.at[:, pl.ds(pl.multiple_of(col * D, D), D)], w_sc.at[to_slot],
                                         w_sem.at[to_slot])

        @pl.when(jnp.logical_and(g == 0, i == 0))
        def _():
            for k in SHARD_ORDER:
                sends[k - 1].start()
            local.start()
            for cp in u_copies:
                cp.start()
            cp = block_copy(wsh_ref, 0, 0)
            cp.start()
            cp.wait()

        @pl.when(jnp.logical_and(last_tile, g == 0))
        def _():
            block_copy(wsh_ref, 1, 1).start()

        for pos, k in enumerate(SHARD_ORDER, start=1):
            arrived, forward, forwarded = arrivals[k - 1]

            @pl.when(jnp.logical_and(last_tile, g == 2 * pos - 1))
            def _():
                arrived.wait_recv()
                forward.start()
                forwarded.wait_recv()
                block_copy(wfull_ref, route_ref[2 + 2 * pos], 0).start()

            @pl.when(jnp.logical_and(last_tile, g == 2 * pos))
            def _():
                block_copy(wfull_ref, route_ref[3 + 2 * pos], 1).start()

        @pl.when(jnp.logical_and(i == 0, g > 0))
        def _():
            block_copy(wfull_ref, 0, slot).wait()

        for t in range(nt):
            @pl.when(jnp.logical_and(g == 0, i == t))
            def _():
                u_copies[t].wait()

        halves = range(0, D, D // 2)

        @pl.when(gid < 2)
        def _():
            scale = jnp.where(gid == 1, QK_SCALE, 1.0).astype(F32)
            for a in halves:
                acc = _dot(u_sc[i], w_sc[slot, :, a:a + D // 2])
                for h in range(HEADS // 2):
                    blk = acc[:, h * DH:(h + 1) * DH]
                    out = (blk * cos_ref[...] + pltpu.roll(blk, DH // 2, axis=1) * sin_ref[...]) * scale
                    proj_ref[:, a + h * DH:a + (h + 1) * DH] = out.astype(BF16)

        @pl.when(gid >= 2)
        def _():
            for a in halves:
                proj_ref[:, a:a + D // 2] = _dot(u_sc[i], w_sc[slot, :, a:a + D // 2]).astype(BF16)

        @pl.when(jnp.logical_and(g == 7, i == nt - 1))
        def _():
            local.wait()
            for cp in sends:
                cp.wait_send()
            for _, forward, _ in arrivals:
                forward.wait_send()

    return _hosted_call(
        body,
        name="in_proj",
        grid=(8, nt),
        in_specs=[
            _ANY, _ANY,
            pl.BlockSpec((tm, DH), lambda g, i, rt: (i, 0)),
            pl.BlockSpec((tm, DH), lambda g, i, rt: (i, 0)),
        ],
        out_specs=[pl.BlockSpec((tm, D), lambda g, i, rt: (i, _swap_3_4(rt[2 + g]))), _ANY],
        out_shape=[jax.ShapeDtypeStruct((rows, IN_COLS), BF16), jax.ShapeDtypeStruct((D, IN_COLS), BF16)],
        scratch_shapes=[
            pltpu.VMEM((nt, tm, D), BF16), pltpu.VMEM((2, D, D), BF16),
            pltpu.SemaphoreType.DMA((nt,)), pltpu.SemaphoreType.DMA((2,)), pltpu.SemaphoreType.DMA,
            pltpu.SemaphoreType.DMA((3,)), pltpu.SemaphoreType.DMA((3,)),
            pltpu.SemaphoreType.DMA((3,)), pltpu.SemaphoreType.DMA((3,)),
        ],
        args=(u, w_shard, cos_t, sin_t),
        riders=riders,
        prefetch=route,
        riders_after_body=True,
    )


def _retention_fwd(proj_bf, intra, q_dec, k_dec, c_dec, riders=()):
    rows = proj_bf.shape[0]
    nc = rows // CHUNK

    def body(q_ref, k_ref, v_ref, m_ref, qd_ref, kd_ref, cd_ref, o_ref, sprev_ref, s_sc):
        @pl.when(pl.program_id(0) == 0)
        def _():
            s_sc[...] = jnp.zeros_like(s_sc)

        for h in range(HEADS):
            sl = slice(h * DH, (h + 1) * DH)
            q, k, v = q_ref[:, sl], k_ref[:, sl], v_ref[:, sl]
            state = s_sc[h]
            state_b = state.astype(BF16)
            sprev_ref[0, h] = state_b
            s = _dot_nt(q, k) * m_ref[h]
            inner = _dot(s.astype(BF16), v)
            cross = _dot(q, state_b) * qd_ref[:, sl]
            o_ref[:, sl] = (inner + cross).astype(BF16)
            k_scaled = (k.astype(F32) * kd_ref[:, sl]).astype(BF16)
            s_sc[h] = state * cd_ref[:, sl] + _dot_tn(k_scaled, v)

    chunk_spec = lambda col: pl.BlockSpec((CHUNK, D), lambda c: (c, col))
    const2 = lambda shape: pl.BlockSpec(shape, lambda c: (0, 0))
    return _hosted_call(
        body,
        name="retention_fwd",
        grid=(nc,),
        in_specs=[
            chunk_spec(0), chunk_spec(1), chunk_spec(2),
            pl.BlockSpec((HEADS, CHUNK, CHUNK), lambda c: (0, 0, 0)),
            const2((CHUNK, D)), const2((CHUNK, D)), const2((1, D)),
        ],
        out_specs=[
            pl.BlockSpec((CHUNK, D), lambda c: (c, 0)),
            pl.BlockSpec((1, HEADS, DH, DH), lambda c: (c, 0, 0, 0)),
        ],
        out_shape=[
            jax.ShapeDtypeStruct((rows, D), BF16),
            jax.ShapeDtypeStruct((nc, HEADS, DH, DH), BF16),
        ],
        scratch_shapes=[pltpu.VMEM((HEADS, DH, DH), F32)],
        args=(proj_bf, proj_bf, proj_bf, intra, q_dec, k_dec, c_dec),
        riders=riders,
    )


def _shift_down(x, first8_prev, d):
    rolled = pltpu.roll(x, d, axis=0)
    head = pltpu.roll(jnp.concatenate([first8_prev, x[0:8]], axis=0), d, axis=0)[8:16]
    return rolled, head


def _conv_and_gates(x, prev8, cw_ref, cb_ref, wa_ref, wx_ref, ba_ref, bx_ref, lam_ref, c_sc):
    cw = cw_ref[...]
    conv = cb_ref[...] + cw[3:4] * x
    head = cb_ref[...] + cw[3:4] * x[0:8]
    for d in (1, 2, 3):
        rolled, hd = _shift_down(x, prev8, d)
        conv = conv + cw[3 - d:4 - d] * rolled
        head = head + cw[3 - d:4 - d] * hd
    c_sc[...] = conv
    c_sc[0:8, :] = head
    c = c_sc[...]
    zr, zi = [], []
    for g in range(LRU_BLOCKS):
        sl = slice(g * LRU_BLOCK, (g + 1) * LRU_BLOCK)
        cg = c[:, sl].astype(BF16)
        zr.append(_dot(cg, wa_ref[g]))
        zi.append(_dot(cg, wx_ref[g]))
    r = _sigmoid(jnp.concatenate(zr, axis=1) + ba_ref[...])
    gate_i = _sigmoid(jnp.concatenate(zi, axis=1) + bx_ref[...])
    sp = _softplus(-lam_ref[...])
    log_a = (-LRU_C) * r * sp
    a = jnp.exp(log_a)
    mult = jnp.sqrt(_one_minus_square(a, log_a))
    return c, r, gate_i, a, mult, log_a


def _lru_fwd(proj, conv_w, conv_b, wa, wx, ba, bx, lam, riders=()):
    rows = proj.shape[0]
    TM = _lru_tile(rows)
    nt = rows // TM

    def body(x_ref, cw_ref, cb_ref, wa_ref, wx_ref, ba_ref, bx_ref, lam_ref,
             h_ref, c_ref, r_ref, i_ref, la_ref, mult_ref, prev_sc, carry_sc, c_sc, a_sc, u_sc, h_sc):
        i = pl.program_id(0)

        @pl.when(i == 0)
        def _():
            prev_sc[...] = jnp.zeros_like(prev_sc)
            carry_sc[...] = jnp.zeros_like(carry_sc)

        x = x_ref[...].astype(F32)
        c, r, gate_i, a, mult, log_a = _conv_and_gates(x, prev_sc[...], cw_ref, cb_ref, wa_ref, wx_ref, ba_ref, bx_ref,
                                                       lam_ref, c_sc)
        for ref, val in ((c_ref, c), (r_ref, r), (i_ref, gate_i), (la_ref, log_a), (mult_ref, mult)):
            ref[...] = val.astype(BF16)
        prev_sc[...] = x[TM - 8:TM]
        valid = _row_ids(i, TM, (TM, D)) >= PAD_ROWS
        a_sc[...] = a
        u_sc[...] = jnp.where(valid, mult * gate_i * c, 0.0)
        row8 = lax.broadcasted_iota(jnp.int32, (8, D), 0)

        def group(gi, hprev):
            r0 = pl.multiple_of(gi * 8, 8)
            aa = a_sc[pl.ds(r0, 8), :]
            uu = u_sc[pl.ds(r0, 8), :]
            for d in (1, 2, 4):
                a_sh = jnp.where(row8 >= d, pltpu.roll(aa, d, axis=0), 1.0)
                u_sh = jnp.where(row8 >= d, pltpu.roll(uu, d, axis=0), 0.0)
                uu = uu + aa * u_sh
                aa = aa * a_sh
            hb = aa * hprev + uu
            h_sc[pl.ds(r0, 8), :] = hb
            return hb[7:8, :]

        hlast = lax.fori_loop(0, TM // 8, group, carry_sc[0:1, :])
        carry_sc[0:1, :] = hlast
        h_ref[...] = h_sc[...].astype(BF16)

    vec = pl.BlockSpec((1, D), lambda i: (0, 0))
    wspec = pl.BlockSpec((LRU_BLOCKS, LRU_BLOCK, LRU_BLOCK), lambda i: (0, 0, 0))
    return _hosted_call(
        body,
        name="lru_fwd",
        grid=(nt,),
        in_specs=[
            pl.BlockSpec((TM, D), lambda i: (i, LRU_IN_COL)),
            pl.BlockSpec((4, D), lambda i: (0, 0)),
            vec, wspec, wspec, vec, vec, vec,
        ],
        out_specs=[pl.BlockSpec((TM, D), lambda i: (i, 0)) for _ in range(6)],
        out_shape=[jax.ShapeDtypeStruct((rows, D), BF16) for _ in range(6)],
        scratch_shapes=[
            pltpu.VMEM((8, D), F32), pltpu.VMEM((8, D), F32),
            pltpu.VMEM((TM, D), F32), pltpu.VMEM((TM, D), F32), pltpu.VMEM((TM, D), F32), pltpu.VMEM((TM, D), F32),
        ],
        args=(proj, conv_w, conv_b, wa, wx, ba, bx, lam),
        riders=riders,
    )


def _group_norm(o):
    outs, rstds = [], []
    for h in range(HEADS):
        oh = o[:, h * DH:(h + 1) * DH]
        rs = lax.rsqrt(jnp.mean(oh * oh, axis=-1, keepdims=True) + EPS)
        outs.append(oh * rs)
        rstds.append(rs)
    return jnp.concatenate(outs, axis=1), rstds


def _frame_rows(head_hbm, x_hbm, buf, sems, i, tm, nt):
    slot = i % 2

    @pl.when(i == 0)
    def _():
        first = [pltpu.make_async_copy(head_hbm, buf.at[0, pl.ds(0, FRONT)], sems.at[2]),
                 pltpu.make_async_copy(x_hbm.at[pl.ds(0, tm - FRONT)], buf.at[0, pl.ds(FRONT, tm - FRONT)], sems.at[0])]
        for cp in first:
            cp.start()
        for cp in first:
            cp.wait()

    @pl.when(i + 1 < nt)
    def _():
        start = pl.multiple_of((i + 1) * tm - FRONT, LANES)
        pltpu.make_async_copy(x_hbm.at[pl.ds(start, tm)], buf.at[1 - slot], sems.at[1 - slot]).start()

    @pl.when(i > 0)
    def _():
        pltpu.make_async_copy(x_hbm.at[pl.ds(0, tm)], buf.at[slot], sems.at[slot]).wait()

    return slot


def _mix_fwd(head, x2d, o, proj, h_lru, w_br, w_bl, w_o, ffn_norm_w, riders=()):
    rows = o.shape[0]
    tm = TM_MIX_FWD
    assert rows % tm == 0 and tm > FRONT
    nt = rows // tm

    def body(head_hbm, x_hbm, o_ref, gates_ref, hl_ref, wbr_ref, wbl_ref, wo_ref, nw_ref,
             yret_ref, ylru_ref, h1_ref, u2_ref, rstd_ref, h0_sc, h0_sem):
        i = pl.program_id(0)
        slot = _frame_rows(head_hbm, x_hbm, h0_sc, h0_sem, i, tm, nt)
        gate = lambda j: gates_ref[:, j * D:(j + 1) * D].astype(F32)
        on, _ = _group_norm(o_ref[...].astype(F32))
        gret = gate(0)
        a_ret = (gret * _sigmoid(gret) * on).astype(BF16)
        y_ret = _dot(a_ret, wbr_ref[...])
        gl, _ = _gelu_and_grad(gate(1))
        a_lru = (gl * hl_ref[...].astype(F32)).astype(BF16)
        y_lru = _dot(a_lru, wbl_ref[...])
        mixed = (_sigmoid(gate(2)) * y_ret + _sigmoid(gate(3)) * y_lru).astype(BF16)
        delta = _dot(mixed, wo_ref[...])
        yret_ref[...] = y_ret.astype(BF16)
        ylru_ref[...] = y_lru.astype(BF16)
        h1 = h0_sc[slot] + delta
        rs = lax.rsqrt(jnp.mean(h1 * h1, axis=-1, keepdims=True) + EPS)
        h1_ref[...] = h1
        u2_ref[...] = ((h1 * rs) * nw_ref[...]).astype(BF16)
        rstd_ref[...] = rs

    tile = lambda col: pl.BlockSpec((tm, D), lambda i: (i, col))
    wspec = pl.BlockSpec((D, D), lambda i: (0, 0))
    return _hosted_call(
        body,
        name="mix_fwd",
        grid=(nt,),
        in_specs=[
            _ANY, _ANY,
            tile(0), pl.BlockSpec((tm, 4 * D), lambda i: (i, GATES_COL)), tile(0),
            wspec, wspec, wspec,
            pl.BlockSpec((1, D), lambda i: (0, 0)),
        ],
        out_specs=[tile(0), tile(0), tile(0), tile(0), pl.BlockSpec((tm, 1), lambda i: (i, 0))],
        out_shape=[
            jax.ShapeDtypeStruct((rows, D), BF16), jax.ShapeDtypeStruct((rows, D), BF16),
            jax.ShapeDtypeStruct((rows, D), F32), jax.ShapeDtypeStruct((rows, D), BF16),
            jax.ShapeDtypeStruct((rows, 1), F32),
        ],
        scratch_shapes=[pltpu.VMEM((2, tm, D), F32), pltpu.SemaphoreType.DMA((3,))],
        args=(head, x2d, o, proj, h_lru, w_br, w_bl, w_o, ffn_norm_w),
        riders=riders,
    )


def _ffn_fwd_gate(u2, w_ffn_in, riders=()):
    rows = u2.shape[0]
    tm = _heavy_tile(rows)
    hid = lambda: pl.BlockSpec((tm, FFN), lambda i: (i, 0))

    def body(u2_ref, w_hbm, silu_ref, dsilu_ref, act_ref, w_sc, w_sem):
        @pl.when(pl.program_id(0) == 0)
        def _():
            cp = pltpu.make_async_copy(w_hbm, w_sc, w_sem)
            cp.start()
            cp.wait()

        u2 = u2_ref[...]
        for a, b in _FFN_SUBS:
            g = _dot(u2, w_sc[:, a:b])
            up = _dot(u2, w_sc[:, FFN + a:FFN + b])
            sg = _sigmoid(g)
            silu = g * sg
            silu_ref[:, a:b] = silu.astype(BF16)
            dsilu_ref[:, a:b] = (up * (sg * (1.0 + g * (1.0 - sg)))).astype(BF16)
            act_ref[:, a:b] = (silu * up).astype(BF16)

    return _hosted_call(
        body,
        name="ffn_fwd_gate",
        grid=(rows // tm,),
        in_specs=[pl.BlockSpec((tm, D), lambda i: (i, 0)), _ANY],
        out_specs=[hid(), hid(), hid()],
        out_shape=[jax.ShapeDtypeStruct((rows, FFN), BF16)] * 3,
        scratch_shapes=[pltpu.VMEM((D, 2 * FFN), BF16), pltpu.SemaphoreType.DMA],
        args=(u2, w_ffn_in),
        riders=riders,
    )


def _ffn_out_loss(act, h1, w_ffn_out, target, final_norm_w):
    rows = act.shape[0]
    tm = _heavy_tile(rows)
    nt = rows // tm

    def body(act_ref, h1_ref, wo_ref, fnw_ref, tgt_hbm, dh2_ref, stats_ref, tgt_sc, tgt_sem):
        i = pl.program_id(0)
        slot = i % 2

        @pl.when(i == 0)
        def _():
            stats_ref[...] = jnp.zeros_like(stats_ref)
            tgt_sc[0, 0:FRONT, :] = jnp.zeros((FRONT, D), F32)
            cp = pltpu.make_async_copy(tgt_hbm.at[pl.ds(0, tm - FRONT)], tgt_sc.at[0, pl.ds(FRONT, tm - FRONT)], tgt_sem.at[0])
            cp.start()
            cp.wait()

        @pl.when(i + 1 < nt)
        def _():
            start = pl.multiple_of((i + 1) * tm - FRONT, FRONT)
            pltpu.make_async_copy(tgt_hbm.at[pl.ds(start, tm)], tgt_sc.at[1 - slot], tgt_sem.at[1 - slot]).start()

        @pl.when(i > 0)
        def _():
            pltpu.make_async_copy(tgt_hbm.at[pl.ds(0, tm)], tgt_sc.at[slot], tgt_sem.at[slot]).wait()

        h2 = h1_ref[...] + _dot(act_ref[...], wo_ref[...])
        rs = lax.rsqrt(jnp.mean(h2 * h2, axis=-1, keepdims=True) + EPS)
        n = h2 * rs
        fnw = fnw_ref[...]
        valid = _row_ids(i, tm, (tm, D)) >= FRONT
        diff = jnp.where(valid, n * fnw - tgt_sc[slot], 0.0)
        dy = diff * (1.0 / D)
        stats_ref[0:1, :] += (0.5 / D) * jnp.sum(diff * diff, axis=0, keepdims=True)
        stats_ref[1:2, :] += jnp.sum(dy * n, axis=0, keepdims=True)
        dn = dy * fnw
        dh2_ref[...] = rs * (dn - n * jnp.mean(dn * n, axis=-1, keepdims=True))

    return pl.pallas_call(
        body,
        name="ffn_out_loss",
        grid=(nt,),
        in_specs=[
            pl.BlockSpec((tm, FFN), lambda i: (i, 0)),
            pl.BlockSpec((tm, D), lambda i: (i, 0)),
            pl.BlockSpec((FFN, D), lambda i: (0, 0)),
            pl.BlockSpec((1, D), lambda i: (0, 0)),
            _ANY,
        ],
        out_specs=[pl.BlockSpec((tm, D), lambda i: (i, 0)), pl.BlockSpec((8, D), lambda i: (0, 0))],
        out_shape=[jax.ShapeDtypeStruct((rows, D), F32), jax.ShapeDtypeStruct((8, D), F32)],
        scratch_shapes=[pltpu.VMEM((2, tm, D), F32), pltpu.SemaphoreType.DMA((2,))],
        compiler_params=_params(("arbitrary",)),
    )(act, h1, w_ffn_out, final_norm_w, target)


def _ffn_bwd(dh2, silu, dsilu, h1, rstd2, w_ffn_in, w_ffn_out, ffn_norm_w):
    rows = dh2.shape[0]
    tm = _heavy_tile(rows)
    blk = lambda: pl.BlockSpec((tm, FFN), lambda i: (i, 0))

    def gate_body(dh2_ref, silu_ref, dsilu_ref, wo_hbm, dg_ref, dup_ref, dh2b_ref, wo_sc, wo_sem):
        @pl.when(pl.program_id(0) == 0)
        def _():
            cp = pltpu.make_async_copy(wo_hbm, wo_sc, wo_sem)
            cp.start()
            cp.wait()

        dh2b = dh2_ref[...].astype(BF16)
        dh2b_ref[...] = dh2b
        for a, b in _FFN_SUBS:
            dact = _dot_nt(dh2b, wo_sc[a:b, :])
            dup_ref[:, a:b] = (dact * silu_ref[:, a:b].astype(F32)).astype(BF16)
            dg_ref[:, a:b] = (dact * dsilu_ref[:, a:b].astype(F32)).astype(BF16)

    dg, dup, dh2b = pl.pallas_call(
        gate_body,
        name="ffn_bwd_gate",
        grid=(rows // tm,),
        in_specs=[pl.BlockSpec((tm, D), lambda i: (i, 0)), blk(), blk(), _ANY],
        out_specs=[blk(), blk(), pl.BlockSpec((tm, D), lambda i: (i, 0))],
        out_shape=[jax.ShapeDtypeStruct((rows, FFN), BF16)] * 2 + [jax.ShapeDtypeStruct((rows, D), BF16)],
        scratch_shapes=[pltpu.VMEM((FFN, D), BF16), pltpu.SemaphoreType.DMA],
        compiler_params=_params(("arbitrary",)),
    )(dh2, silu, dsilu, w_ffn_out)

    def body(dg_ref, dup_ref, dh2_ref, h1_ref, rstd_ref, w_hbm, nw_ref, dh1_ref, stats_ref, w_sc, w_sem):
        @pl.when(pl.program_id(0) == 0)
        def _():
            stats_ref[...] = jnp.zeros_like(stats_ref)
            cp = pltpu.make_async_copy(w_hbm, w_sc, w_sem)
            cp.start()
            cp.wait()

        du = _dot_nt(dg_ref[...], w_sc[:, 0:FFN]) + _dot_nt(dup_ref[...], w_sc[:, FFN:2 * FFN])
        rs = rstd_ref[...]
        n = h1_ref[...] * rs
        stats_ref[0:1, :] += jnp.sum(du * n, axis=0, keepdims=True)
        dn = du * nw_ref[...]
        dh1_ref[...] = dh2_ref[...] + rs * (dn - n * jnp.mean(dn * n, axis=-1, keepdims=True))

    row = lambda width: pl.BlockSpec((tm, width), lambda i: (i, 0))
    dh1, stats = pl.pallas_call(
        body,
        name="ffn_bwd_in",
        grid=(rows // tm,),
        in_specs=[row(FFN), row(FFN), row(D), row(D), row(1), _ANY, pl.BlockSpec((1, D), lambda i: (0, 0))],
        out_specs=[row(D), pl.BlockSpec((8, D), lambda i: (0, 0))],
        out_shape=[jax.ShapeDtypeStruct((rows, D), F32), jax.ShapeDtypeStruct((8, D), F32)],
        scratch_shapes=[pltpu.VMEM((D, 2 * FFN), BF16), pltpu.SemaphoreType.DMA],
        compiler_params=_params(("arbitrary",)),
    )(dg, dup, dh2, h1, rstd2, w_ffn_in, ffn_norm_w)
    return dg, dup, dh2b, dh1, stats


def _mix_bwd(dh1, o, proj, h_lru, y_ret, y_lru, w_br, w_bl, w_o, riders=()):
    rows = dh1.shape[0]
    tm = TM_MIX_BWD
    nt = rows // tm

    def body(dh1_ref, o_ref, gates_ref, hl_ref, yret_ref, ylru_ref, wbr_hbm, wbl_hbm, wo_hbm,
             dproj_ref, do_ref, dhl_ref, mixed_ref, aret_ref, alru_ref, dyret_ref, dylru_ref, w_sc, w_sem):
        @pl.when(pl.program_id(0) == 0)
        def _():
            copies = [pltpu.make_async_copy(src, w_sc.at[q], w_sem.at[q]) for q, src in enumerate((wbr_hbm, wbl_hbm, wo_hbm))]
            for cp in copies:
                cp.start()
            for cp in copies:
                cp.wait()

        wbr_ref, wbl_ref, wo_ref = w_sc.at[0], w_sc.at[1], w_sc.at[2]
        gate = lambda j: gates_ref[:, j * D:(j + 1) * D].astype(F32)
        dmixed = _dot_nt(dh1_ref[...].astype(BF16), wo_ref[...])
        y_ret, y_lru = yret_ref[...].astype(F32), ylru_ref[...].astype(F32)
        sa, sb = _sigmoid(gate(2)), _sigmoid(gate(3))
        mixed_ref[...] = (sa * y_ret + sb * y_lru).astype(BF16)
        dga = dmixed * y_ret * sa * (1.0 - sa)
        dgb = dmixed * y_lru * sb * (1.0 - sb)
        dy_ret = (dmixed * sa).astype(BF16)
        dy_lru = (dmixed * sb).astype(BF16)
        dyret_ref[...] = dy_ret
        dylru_ref[...] = dy_lru
        da_ret = _dot_nt(dy_ret, wbr_ref[...])
        da_lru = _dot_nt(dy_lru, wbl_ref[...])

        gret = gate(0)
        sg = _sigmoid(gret)
        silu = gret * sg
        on, rstds = _group_norm(o_ref[...].astype(F32))
        aret_ref[...] = (silu * on).astype(BF16)
        dgret = da_ret * on * (sg * (1.0 + gret * (1.0 - sg)))
        don = da_ret * silu
        for h in range(HEADS):
            sl = slice(h * DH, (h + 1) * DH)
            onh, donh = on[:, sl], don[:, sl]
            do_ref[:, sl] = (rstds[h] * (donh - onh * jnp.mean(donh * onh, axis=-1, keepdims=True))).astype(BF16)

        gl, gl_grad = _gelu_and_grad(gate(1))
        hl = hl_ref[...].astype(F32)
        alru_ref[...] = (gl * hl).astype(BF16)
        dlgate = da_lru * hl * gl_grad
        dhl_ref[...] = (da_lru * gl).astype(BF16)

        for j, val in enumerate((dgret, dlgate, dga, dgb)):
            dproj_ref[:, j * D:(j + 1) * D] = val.astype(BF16)

    tile = lambda col: pl.BlockSpec((tm, D), lambda i: (i, col))
    gates = pl.BlockSpec((tm, 4 * D), lambda i: (i, GATES_COL))
    bf = lambda: jax.ShapeDtypeStruct((rows, D), BF16)
    return _hosted_call(
        body,
        name="mix_bwd",
        grid=(nt,),
        in_specs=[tile(0), tile(0), gates, tile(0), tile(0), tile(0), _ANY, _ANY, _ANY],
        out_specs=[pl.BlockSpec((tm, 4 * D), lambda i: (i, GATES_COL))] + [tile(0)] * 7,
        out_shape=[jax.ShapeDtypeStruct((rows, IN_COLS), BF16)] + [bf() for _ in range(7)],
        scratch_shapes=[pltpu.VMEM((3, D, D), BF16), pltpu.SemaphoreType.DMA((3,))],
        args=(dh1, o, proj, h_lru, y_ret, y_lru, w_br, w_bl, w_o),
        riders=riders,
    )


def _retention_bwd(dproj, proj_bf, do, sprev, intra, intra_t, q_dec, k_dec, c_dec, cos_t, sin_t, riders=()):
    rows = proj_bf.shape[0]
    nc = rows // CHUNK

    def body(dproj_in_ref, q_ref, k_ref, v_ref, do_ref, sprev_ref, m_ref, mt_ref, qd_ref, kd_ref, cd_ref, cos_ref, sin_ref,
             dproj_ref, ds_sc):
        @pl.when(pl.program_id(0) == 0)
        def _():
            ds_sc[...] = jnp.zeros_like(ds_sc)

        cos, sin = cos_ref[...], sin_ref[...]

        def unrotate(dy):
            return dy * cos - pltpu.roll(dy, DH // 2, axis=1) * sin

        for h in range(HEADS):
            sl = slice(h * DH, (h + 1) * DH)
            q, k, v = q_ref[:, sl], k_ref[:, sl], v_ref[:, sl]
            do = do_ref[:, sl]
            dob = do.astype(BF16)
            doq = (do * qd_ref[:, sl]).astype(BF16)
            state_prev = sprev_ref[0, h]
            dstate = ds_sc[h]
            dstate_b = dstate.astype(BF16)
            s_t = (_dot_nt(k, q) * mt_ref[h]).astype(BF16)
            ds_t = (_dot_nt(v, dob) * mt_ref[h]).astype(BF16)
            ds = (_dot_nt(dob, v) * m_ref[h]).astype(BF16)
            kd = kd_ref[:, sl]
            dq = _dot(ds, k) + _dot_nt(doq, state_prev)
            dk = _dot(ds_t, q) + _dot_nt(v, dstate_b) * kd
            k_scaled = (k.astype(F32) * kd).astype(BF16)
            dv = _dot(s_t, dob) + _dot(k_scaled, dstate_b)
            ds_sc[h] = dstate * cd_ref[:, sl] + _dot_tn(q, doq)
            for part, val in enumerate((unrotate(dq), unrotate(dk) * QK_SCALE, dv)):
                dproj_ref[:, part * D + h * DH:part * D + (h + 1) * DH] = val.astype(BF16)

    rev = lambda c: nc - 1 - c
    chunk_spec = lambda col: pl.BlockSpec((CHUNK, D), lambda c: (rev(c), col))
    const2 = lambda shape: pl.BlockSpec(shape, lambda c: (0, 0))
    const3 = pl.BlockSpec((HEADS, CHUNK, CHUNK), lambda c: (0, 0, 0))
    return _hosted_call(
        body,
        name="retention_bwd",
        grid=(nc,),
        in_specs=[
            pl.BlockSpec(memory_space=pl.ANY),
            chunk_spec(0), chunk_spec(1), chunk_spec(2), chunk_spec(0),
            pl.BlockSpec((1, HEADS, DH, DH), lambda c: (rev(c), 0, 0, 0)),
            const3, const3,
            const2((CHUNK, D)), const2((CHUNK, D)), const2((1, D)),
            pl.BlockSpec((CHUNK, DH), lambda c: (rev(c), 0)),
            pl.BlockSpec((CHUNK, DH), lambda c: (rev(c), 0)),
        ],
        out_specs=[pl.BlockSpec((CHUNK, 3 * D), lambda c: (rev(c), 0))],
        out_shape=[jax.ShapeDtypeStruct(dproj.shape, BF16)],
        aliases={0: 0},
        scratch_shapes=[pltpu.VMEM((HEADS, DH, DH), F32)],
        args=(dproj, proj_bf, proj_bf, proj_bf, do, sprev, intra, intra_t, q_dec, k_dec, c_dec, cos_t, sin_t),
        riders=riders,
    )


def _lru_bwd(dproj, proj, saved, dhl, conv_w, wa, wx, lam, riders=()):
    rows = proj.shape[0]
    TM = _lru_tile(rows)
    nt = rows // TM
    per8 = TM // 8

    def body(dproj_in_ref, x_ref, xprev_ref, h_ref, hprev_ref, c_ref, r_ref, i_ref, la_ref, mult_ref, dhl_ref,
             cw_ref, wa_ref, wx_ref, lam_ref,
             dproj_ref, dwa_ref, dwx_ref, stats_ref, anext_sc, dhnext_sc, dcnext_sc, c_sc, b_sc, dh_sc):
        step = pl.program_id(0)
        i = nt - 1 - step

        @pl.when(step == 0)
        def _():
            anext_sc[...] = jnp.zeros_like(anext_sc)
            dhnext_sc[...] = jnp.zeros_like(dhnext_sc)
            dcnext_sc[...] = jnp.zeros_like(dcnext_sc)
            dwa_ref[...] = jnp.zeros_like(dwa_ref)
            dwx_ref[...] = jnp.zeros_like(dwx_ref)
            stats_ref[...] = jnp.zeros_like(stats_ref)

        first = i == 0
        x = x_ref[...].astype(F32)
        prev8 = jnp.where(first, 0.0, xprev_ref[8:16, :].astype(F32))
        c_b = c_ref[...]
        c, r, gate_i, mult = (ref[...].astype(F32) for ref in (c_ref, r_ref, i_ref, mult_ref))
        a = jnp.exp(la_ref[...].astype(F32))
        sp = _softplus(-lam_ref[...])
        dh_sc[...] = dhl_ref[...].astype(F32)

        b_sc[...] = pltpu.roll(a, TM - 1, axis=0)
        b_sc[TM - 1:TM, :] = anext_sc[0:1, :]
        anext_sc[0:1, :] = a[0:1, :]
        row8 = lax.broadcasted_iota(jnp.int32, (8, D), 0)

        def group(gi, dhnext):
            r0 = pl.multiple_of((per8 - 1 - gi) * 8, 8)
            bb = b_sc[pl.ds(r0, 8), :]
            uu = dh_sc[pl.ds(r0, 8), :]
            for d in (1, 2, 4):
                b_sh = jnp.where(row8 < 8 - d, pltpu.roll(bb, 8 - d, axis=0), 1.0)
                u_sh = jnp.where(row8 < 8 - d, pltpu.roll(uu, 8 - d, axis=0), 0.0)
                uu = uu + bb * u_sh
                bb = bb * b_sh
            dhb = bb * dhnext + uu
            dh_sc[pl.ds(r0, 8), :] = dhb
            return dhb[0:1, :]

        dhfirst = lax.fori_loop(0, per8, group, dhnext_sc[0:1, :])
        dhnext_sc[0:1, :] = dhfirst
        dh = dh_sc[...]

        h = h_ref[...].astype(F32)
        hprev8 = jnp.where(first, 0.0, hprev_ref[8:16, :].astype(F32))
        h_dn, h_head = _shift_down(h, hprev8, 1)
        c_sc[...] = h_dn
        c_sc[0:8, :] = h_head
        h_before = c_sc[...]

        valid = _row_ids(i, TM, (TM, D)) >= PAD_ROWS
        da = dh * h_before
        du = jnp.where(valid, dh, 0.0)
        dmult = du * gate_i * c
        dgate_i = du * mult * c
        dc = du * mult * gate_i
        dla = da * a - dmult * (a * a) / mult
        dla = jnp.where(valid, dla, 0.0)
        dr = dla * ((-LRU_C) * sp)
        dzr = dr * r * (1.0 - r)
        dzi = dgate_i * gate_i * (1.0 - gate_i)
        stats_ref[1:2, :] += jnp.sum(dzr, axis=0, keepdims=True)
        stats_ref[2:3, :] += jnp.sum(dzi, axis=0, keepdims=True)
        stats_ref[3:4, :] += jnp.sum(dla * ((-LRU_C) * r), axis=0, keepdims=True)
        dc_gate = []
        for g in range(LRU_BLOCKS):
            sl = slice(g * LRU_BLOCK, (g + 1) * LRU_BLOCK)
            cg = c_b[:, sl]
            dzr_g = dzr[:, sl].astype(BF16)
            dzi_g = dzi[:, sl].astype(BF16)
            dc_gate.append(_dot_nt(dzr_g, wa_ref[g]) + _dot_nt(dzi_g, wx_ref[g]))
            dwa_ref[g] += _dot_tn(cg, dzr_g)
            dwx_ref[g] += _dot_tn(cg, dzi_g)
        dc = dc + jnp.concatenate(dc_gate, axis=1)

        cw = cw_ref[...]
        stats_ref[0:1, :] += jnp.sum(dc, axis=0, keepdims=True)
        stats_ref[7:8, :] += jnp.sum(dc * x, axis=0, keepdims=True)
        dx = cw[3:4] * dc
        tail_src = jnp.concatenate([dc[TM - 8:TM], dcnext_sc[...]], axis=0)
        dx_tail = cw[3:4] * dc[TM - 8:TM]
        for d in (1, 2, 3):
            dx = dx + cw[3 - d:4 - d] * pltpu.roll(dc, TM - d, axis=0)
            dx_tail = dx_tail + cw[3 - d:4 - d] * pltpu.roll(tail_src, 16 - d, axis=0)[0:8]
            rolled, hd = _shift_down(x, prev8, d)
            b_sc[...] = rolled
            b_sc[0:8, :] = hd
            stats_ref[7 - d:8 - d, :] += jnp.sum(dc * b_sc[...], axis=0, keepdims=True)
        dcnext_sc[...] = dc[0:8]
        dproj_ref[...] = dx.astype(BF16)
        dproj_ref[TM - 8:TM, :] = dx_tail.astype(BF16)

    rev = lambda s: nt - 1 - s
    vec = pl.BlockSpec((1, D), lambda s: (0, 0))
    wspec = pl.BlockSpec((LRU_BLOCKS, LRU_BLOCK, LRU_BLOCK), lambda s: (0, 0, 0))
    prev16 = lambda col: pl.BlockSpec((BF16_ROWS, D), lambda s: (jnp.maximum(rev(s) * (TM // BF16_ROWS) - 1, 0), col))
    tile = lambda: pl.BlockSpec((TM, D), lambda s: (rev(s), 0))
    h_lru, c_sv, r_sv, i_sv, la_sv, mult_sv = saved
    return _hosted_call(
        body,
        name="lru_bwd",
        grid=(nt,),
        in_specs=[
            pl.BlockSpec(memory_space=pl.ANY),
            pl.BlockSpec((TM, D), lambda s: (rev(s), LRU_IN_COL)), prev16(LRU_IN_COL),
            tile(), prev16(0),
            tile(), tile(), tile(), tile(), tile(), tile(),
            pl.BlockSpec((4, D), lambda s: (0, 0)),
            wspec, wspec, vec,
        ],
        out_specs=[
            pl.BlockSpec((TM, D), lambda s: (rev(s), LRU_IN_COL)),
            wspec, wspec,
            pl.BlockSpec((8, D), lambda s: (0, 0)),
        ],
        out_shape=[
            jax.ShapeDtypeStruct(dproj.shape, BF16),
            jax.ShapeDtypeStruct((LRU_BLOCKS, LRU_BLOCK, LRU_BLOCK), F32),
            jax.ShapeDtypeStruct((LRU_BLOCKS, LRU_BLOCK, LRU_BLOCK), F32),
            jax.ShapeDtypeStruct((8, D), F32),
        ],
        aliases={0: 0},
        scratch_shapes=[
            pltpu.VMEM((8, D), F32), pltpu.VMEM((8, D), F32), pltpu.VMEM((8, D), F32),
            pltpu.VMEM((TM, D), F32), pltpu.VMEM((TM, D), F32), pltpu.VMEM((TM, D), F32),
        ],
        args=(dproj, proj, proj, h_lru, h_lru, c_sv, r_sv, i_sv, la_sv, mult_sv, dhl, conv_w, wa, wx, lam),
        riders=riders,
    )


def _in_proj_bwd(dproj, w_in, part, prev=None, riders=()):
    rows = dproj.shape[0]
    tm = _heavy_tile(rows)
    nt = rows // tm
    split = max(1, nt // 4 + 1) if nt > 1 else 1
    first = 0 if part == 0 else split
    count = split if part == 0 else nt - split

    def body(*refs):
        dproj_ref, w_hbm, du_ref, w_sc, w_sem = refs[-5:]

        @pl.when(pl.program_id(0) == 0)
        def _():
            moves = [((0, 3), 0), ((4, 1), 3), ((3, 1), 4), ((5, 3), 5)]
            copies = [pltpu.make_async_copy(w_hbm.at[:, pl.ds(src * D, n * D)], w_sc.at[:, pl.ds(dst * D, n * D)], w_sem.at[q])
                      for q, ((src, n), dst) in enumerate(moves)]
            for cp in copies:
                cp.start()
            for cp in copies:
                cp.wait()

        du_ref[...] = _dot_nt(dproj_ref[...], w_sc[...])

    in_specs = [pl.BlockSpec((tm, IN_COLS), lambda i: (first + i, 0)), _ANY]
    args = (dproj, w_in)
    if prev is not None:
        in_specs = [_ANY] + in_specs
        args = (prev,) + args
    return _hosted_call(
        body,
        name="in_proj_bwd_%d" % part,
        grid=(count,),
        in_specs=in_specs,
        out_specs=[pl.BlockSpec((tm, D), lambda i: (first + i, 0))],
        out_shape=[jax.ShapeDtypeStruct((rows, D), F32)],
        scratch_shapes=[pltpu.VMEM((D, IN_COLS), BF16), pltpu.SemaphoreType.DMA((4,))],
        aliases={0: 0} if prev is not None else None,
        args=args,
        riders=riders,
    )


def _norm1_bwd(du, dh1, head, x2d, rstd1, norm_w, riders=()):
    rows = du.shape[0]

    def body(du_ref, dh1_ref, head_ref, x_ref, rstd_ref, nw_ref, gx_ref, ghead_ref, stats_ref):
        i = pl.program_id(0)

        @pl.when(i == 0)
        def _():
            stats_ref[...] = jnp.zeros_like(stats_ref)

        def finish(h0, out_ref):
            du = du_ref[...]
            rs = rstd_ref[...]
            n = h0 * rs
            stats_ref[0:1, :] += jnp.sum(du * n, axis=0, keepdims=True)
            dn = du * nw_ref[...]
            out_ref[...] = dh1_ref[...] + rs * (dn - n * jnp.mean(dn * n, axis=-1, keepdims=True))

        @pl.when(i == 0)
        def _():
            finish(head_ref[...], ghead_ref)

        @pl.when(i > 0)
        def _():
            finish(x_ref[...], gx_ref)

    tile = pl.BlockSpec((TM, D), lambda i: (i, 0))
    return _hosted_call(
        body,
        name="norm1_bwd",
        grid=(rows // TM,),
        in_specs=[
            tile, tile,
            pl.BlockSpec((FRONT, D), lambda i: (0, 0)),
            pl.BlockSpec((TM, D), lambda i: (jnp.maximum(i - 1, 0), 0)),
            pl.BlockSpec((TM, 1), lambda i: (i, 0)),
            pl.BlockSpec((1, D), lambda i: (0, 0)),
        ],
        out_specs=[
            pl.BlockSpec((TM, D), lambda i: (jnp.maximum(i - 1, 0), 0)),
            pl.BlockSpec((FRONT, D), lambda i: (0, 0)),
            pl.BlockSpec((8, D), lambda i: (0, 0)),
        ],
        out_shape=[
            jax.ShapeDtypeStruct(x2d.shape, F32),
            jax.ShapeDtypeStruct((FRONT, D), F32),
            jax.ShapeDtypeStruct((8, D), F32),
        ],
        scratch_shapes=[],
        args=(du, dh1, head, x2d, rstd1, norm_w),
        riders=riders,
    )


def _matmul_tn(name, x, dy, out_cols, col0=0, prev=None, k_block=None, n_block=None, riders=(), col_map=None):
    col_map = col_map or (lambda n: n)
    rows, kdim = x.shape
    ndim = dy.shape[1]
    kb = k_block or kdim
    nb = n_block or ndim
    step_bytes = lambda t: 2 * t * (kb * x.dtype.itemsize + nb * dy.dtype.itemsize) + 2 * kb * nb * 4
    tr = next(t for t in (2816, 1408, TM_HEAVY, TM) if rows % t == 0 and (t == TM or step_bytes(t) <= TN_VMEM_BUDGET))
    nr, nk, nn = rows // tr, kdim // kb, ndim // nb
    cb0 = col0 // nb

    def body(*refs):
        x_ref, dy_ref, out_ref = refs[-3], refs[-2], refs[-1]
        cut = (nb // 2) // LANES * LANES
        halves = ((0, cut), (cut, nb)) if cut else ((0, nb),)

        @pl.when(pl.program_id(2) == 0)
        def _():
            for a, b in halves:
                out_ref[:, a:b] = _dot_tn(x_ref[...].astype(BF16), dy_ref[:, a:b].astype(BF16))

        @pl.when(pl.program_id(2) > 0)
        def _():
            for a, b in halves:
                out_ref[:, a:b] += _dot_tn(x_ref[...].astype(BF16), dy_ref[:, a:b].astype(BF16))

    in_specs = [
        pl.BlockSpec((tr, kb), lambda n, k, r: (r, k)),
        pl.BlockSpec((tr, nb), lambda n, k, r: (r, n)),
    ]
    args = [x, dy]
    aliases = {}
    if prev is not None:
        in_specs = [pl.BlockSpec(memory_space=pl.ANY)] + in_specs
        args = [prev] + args
        aliases = {0: 0}
    (out,), rider_outs = _hosted_call(
        body,
        name=name,
        grid=(nn, nk, nr),
        in_specs=in_specs,
        out_specs=[pl.BlockSpec((kb, nb), lambda n, k, r: (k, cb0 + col_map(n)))],
        out_shape=[jax.ShapeDtypeStruct((kdim, out_cols), F32)],
        scratch_shapes=[],
        aliases=aliases,
        args=args,
        riders=riders,
    )
    return (out, rider_outs) if riders else out


def _local_step(x2d, target, w, plan):
    rows = FRONT + x2d.shape[0]
    head = jnp.concatenate([jnp.zeros((PAD_ROWS, D), F32), w["meta_tokens"]], axis=0)
    cos_t, sin_t = _rope_tables(rows)
    intra, intra_t, q_dec, k_dec, c_dec = _decay_tables()
    grads = {}

    def hosted(host, fn, *args, **kwargs):
        outs, rider_outs = fn(*args, riders=plan.riders(host, w, grads), **kwargs)
        plan.after(host, rider_outs, w, grads)
        return outs

    (u1, rstd1), _ = _norm1(head, x2d, w["mix_norm_w"])
    proj, w["w_in"] = hosted("in_proj", _in_proj, u1, w["w_in_shard"], w["route"], cos_t, sin_t)
    o, sprev = hosted("retention_fwd", _retention_fwd, proj, intra, q_dec, k_dec, c_dec)
    lru_args = (w["conv_w"], w["conv_b"], w["lru_wa"], w["lru_wx"], w["lru_ba"], w["lru_bx"], w["lru_lambda"])
    lru_saved = hosted("lru_fwd", _lru_fwd, proj, *lru_args)
    h_lru = lru_saved[0]
    y_ret, y_lru, h1, u2, rstd2 = hosted("mix_fwd", _mix_fwd, head, x2d, o, proj, h_lru, w["w_branch_ret"],
                                         w["w_branch_lru"], w["w_out"], w["ffn_norm_w"])
    silu, dsilu, act = hosted("ffn_fwd_gate", _ffn_fwd_gate, u2, w["w_ffn_in"])
    dh2, stats_loss = _ffn_out_loss(act, h1, w["w_ffn_out"], target, w["final_norm_w"])

    dg, dup, dh2b, dh1, stats_ffn = _ffn_bwd(dh2, silu, dsilu, h1, rstd2, w["w_ffn_in"], w["w_ffn_out"], w["ffn_norm_w"])
    grads["w_ffn_in"] = _matmul_tn("dw_ffn_up", u2, dup, 2 * FFN, col0=FFN, n_block=FFN_HALF,
                                   prev=_matmul_tn("dw_ffn_gate", u2, dg, 2 * FFN, n_block=FFN_HALF))
    grads["w_ffn_out"] = _matmul_tn("dw_ffn_out", act, dh2b, D, k_block=FFN_HALF)
    (dproj, do, dhl, mixed, a_ret, a_lru, dy_ret, dy_lru) = hosted(
        "mix_bwd", _mix_bwd, dh1, o, proj, h_lru, y_ret, y_lru, w["w_branch_ret"], w["w_branch_lru"], w["w_out"])
    grads["w_out"] = _matmul_tn("dw_out", mixed, dh1, D)
    grads["w_branch_ret"] = _matmul_tn("dw_branch_ret", a_ret, dy_ret, D)
    grads["w_branch_lru"] = _matmul_tn("dw_branch_lru", a_lru, dy_lru, D)
    (dproj,) = hosted("retention_bwd", _retention_bwd, dproj, proj, do, sprev, intra, intra_t, q_dec, k_dec, c_dec,
                      cos_t, sin_t)
    dproj, grads["lru_wa"], grads["lru_wx"], stats_lru = hosted(
        "lru_bwd", _lru_bwd, dproj, proj, lru_saved, dhl, w["conv_w"], w["lru_wa"], w["lru_wx"], w["lru_lambda"])
    grads["w_in"], rider_outs = _matmul_tn("dw_in", u1, dproj, IN_COLS, n_block=D, col_map=_swap_3_4,
                                           riders=plan.riders("dw_in", w, grads))
    plan.after("dw_in", rider_outs, w, grads)
    (du1,) = hosted("in_proj_bwd_0", _in_proj_bwd, dproj, w["w_in"], 0)
    (du1,) = hosted("in_proj_bwd_1", _in_proj_bwd, dproj, w["w_in"], 1, du1)
    (grad_x, grad_head, stats_in), _ = _norm1_bwd(du1, dh1, head, x2d, rstd1, w["mix_norm_w"])
    return grad_x, grad_head, grads, [stats_loss, stats_ffn, stats_in, stats_lru]


BIG_PIECES = {
    "w_in": ("col", (D, 2 * D)),
    "w_ffn_in": ("col", (D, FFN_HALF)),
    "w_ffn_out": ("row", (FFN // 4, D)),
    "w_branch_ret": ("row", (D // 4, D)),
    "w_branch_lru": ("row", (D // 4, D)),
    "w_out": ("row", (D // 4, D)),
    "lru_wa": ("lru", (LRU_BLOCKS, LRU_BLOCK // 4, LRU_BLOCK)),
    "lru_wx": ("lru", (LRU_BLOCKS, LRU_BLOCK // 4, LRU_BLOCK)),
}
SMALL_PIECES = {"meta_tokens": ("col", (N_META, D // 4)), "conv_w": ("col", (4, D // 4))}


def _full_shape(kind, shard):
    if kind == "col":
        return (shard[0], 4 * shard[1])
    if kind == "row":
        return (4 * shard[0], shard[1])
    return (shard[0], 4 * shard[1], shard[2])


def _half_shape(kind, shard):
    return (shard[0] // 2,) + tuple(shard[1:])


def _aligned(start, multiple):
    return start if isinstance(start, int) else pl.multiple_of(start, multiple)


def _lead(h, size):
    if h is None:
        return pl.ds(0, size)
    return pl.ds(_aligned(h * (size // 2), size // 2), size // 2)


def _full_region(ref, kind, shard, s, h):
    if kind == "col":
        return ref.at[_lead(h, shard[0]), pl.ds(_aligned(s * shard[1], shard[1]), shard[1])]
    if kind == "row":
        size = shard[0] if h is None else shard[0] // 2
        start = s * shard[0] + (0 if h is None else h * (shard[0] // 2))
        return ref.at[pl.ds(_aligned(start, BF16_ROWS), size), :]
    return ref.at[_lead(h, shard[0]), pl.ds(_aligned(s * shard[1], shard[1]), shard[1]), :]


def _shard_region(ref, shard, h):
    return ref.at[_lead(h, shard[0])]


def _place():
    x, y, c = lax.axis_index("x"), lax.axis_index("y"), lax.axis_index("c")
    return x, y, c, 2 * x + y


def _other_chip(s, c, k):
    s2 = jnp.bitwise_xor(s, k)
    return s2, (s2 // 2, s2 % 2, c)


def _remote(src, dst, send_sem, recv_sem, dev):
    return pltpu.make_async_remote_copy(src_ref=src, dst_ref=dst, send_sem=send_sem, recv_sem=recv_sem,
                                        device_id=dev, device_id_type=MESH)


_ANY = pl.BlockSpec(memory_space=pl.ANY)


class _Rider:
    def __init__(self, ins, out_shapes, sem_shapes, build, aliased=False):
        self.ins, self.out_shapes, self.sem_shapes, self.build, self.aliased = ins, out_shapes, sem_shapes, build, aliased


def _hosted_call(body, *, name, grid, in_specs, out_specs, out_shape, scratch_shapes, args, riders=(), aliases=None,
                 prefetch=None, riders_after_body=False):
    n_in, n_out, n_sc = len(in_specs), len(out_shape), len(scratch_shapes)
    r_in = [a for r in riders for a in r.ins]
    r_out = [s for r in riders for s in r.out_shapes]
    r_sem = [s for r in riders for s in r.sem_shapes]
    lead = () if prefetch is None else (prefetch,)
    assert prefetch is None or not (aliases or any(r.aliased for r in riders))

    def full_body(*refs):
        head, refs = refs[:len(lead)], refs[len(lead):]
        ins, rin = refs[:n_in], refs[n_in:n_in + len(r_in)]
        o0 = n_in + len(r_in)
        outs, rout = refs[o0:o0 + n_out], refs[o0 + n_out:o0 + n_out + len(r_out)]
        s0 = o0 + n_out + len(r_out)
        scratch, rsem = refs[s0:s0 + n_sc], refs[s0 + n_sc:]
        starts, waits = [], []
        pi = po = ps = 0
        for r in riders:
            st, wt = r.build(rin[pi:pi + len(r.ins)], rout[po:po + len(r.out_shapes)], rsem[ps:ps + len(r.sem_shapes)])
            starts += st
            waits += wt
            pi, po, ps = pi + len(r.ins), po + len(r.out_shapes), ps + len(r.sem_shapes)
        first = functools.reduce(jnp.logical_and, [pl.program_id(d) == 0 for d in range(len(grid))])
        last = functools.reduce(jnp.logical_and, [pl.program_id(d) == grid[d] - 1 for d in range(len(grid))])

        def start_riders():
            @pl.when(first)
            def _():
                for cp in starts:
                    cp.start()

        if riders and not riders_after_body:
            start_riders()
        body(*head, *ins, *outs, *scratch)
        if riders and riders_after_body:
            start_riders()
        if riders:
            @pl.when(last)
            def _():
                for wait in waits:
                    wait()

    io_aliases = dict(aliases or {})
    pi = po = 0
    for r in riders:
        if r.aliased:
            for q in range(len(r.ins)):
                io_aliases[n_in + pi + q] = n_out + po + q
        pi, po = pi + len(r.ins), po + len(r.out_shapes)
    specs = dict(
        grid=grid,
        in_specs=list(in_specs) + [_ANY] * len(r_in),
        out_specs=list(out_specs) + [_ANY] * len(r_out),
        scratch_shapes=list(scratch_shapes) + r_sem,
    )
    if prefetch is not None:
        specs = dict(grid_spec=pltpu.PrefetchScalarGridSpec(num_scalar_prefetch=1, **specs))
    res = pl.pallas_call(
        full_body,
        name=name,
        out_shape=list(out_shape) + r_out,
        input_output_aliases=io_aliases,
        compiler_params=_params(("arbitrary",) * len(grid)),
        **specs,
    )(*lead, *args, *r_in)
    rider_outs, po = [], n_out
    for r in riders:
        rider_outs.append(list(res[po:po + len(r.out_shapes)]))
        po += len(r.out_shapes)
    return list(res[:n_out]), rider_outs


def _run_riders(name, riders):
    r_in = [a for r in riders for a in r.ins]
    r_out = [s for r in riders for s in r.out_shapes]
    r_sem = [s for r in riders for s in r.sem_shapes]

    def body(*refs):
        rin, rout, rsem = refs[:len(r_in)], refs[len(r_in):len(r_in) + len(r_out)], refs[len(r_in) + len(r_out):]
        pi = po = ps = 0
        all_waits = []
        for r in riders:
            starts, waits = r.build(rin[pi:pi + len(r.ins)], rout[po:po + len(r.out_shapes)], rsem[ps:ps + len(r.sem_shapes)])
            for cp in starts:
                cp.start()
            all_waits += waits
            pi, po, ps = pi + len(r.ins), po + len(r.out_shapes), ps + len(r.sem_shapes)
        for wait in all_waits:
            wait()

    io_aliases = {}
    pi = po = 0
    for r in riders:
        if r.aliased:
            for q in range(len(r.ins)):
                io_aliases[pi + q] = po + q
        pi, po = pi + len(r.ins), po + len(r.out_shapes)
    res = pl.pallas_call(
        body,
        name=name,
        in_specs=[_ANY] * len(r_in),
        out_specs=[_ANY] * len(r_out),
        out_shape=r_out,
        scratch_shapes=r_sem,
        input_output_aliases=io_aliases,
    )(*r_in)
    outs, po = [], 0
    for r in riders:
        outs.append(list(res[po:po + len(r.out_shapes)]))
        po += len(r.out_shapes)
    return outs


def _piece(name):
    if name in BIG_PIECES:
        return (name, *BIG_PIECES[name], True)
    return (name, *SMALL_PIECES[name], False)


def _gather_rider(shards, names):
    pieces = [_piece(n) for n in names]
    n = len(pieces)

    def build(ins, outs, sems):
        local_sem, ici_send, ici_recv = sems
        _, _, c, s = _place()
        starts, waits = [], []
        for p, (_, kind, shard, split) in enumerate(pieces):
            cp = pltpu.make_async_copy(ins[p], _full_region(outs[p], kind, shard, s, None), local_sem.at[p])
            starts.append(cp)
            waits.append(cp.wait)
            h = c if split else None
            for k in (1, 2, 3):
                s2, dev = _other_chip(s, c, k)
                cp = _remote(_shard_region(ins[p], shard, h), _full_region(outs[p], kind, shard, s, h),
                             ici_send.at[p, k - 1], ici_recv.at[p, k - 1], dev)
                starts.append(cp)
                waits.append(cp.wait_send)
                region = _full_region(outs[p], kind, shard, s2, h)
                waits.append(_remote(region, region, ici_send.at[p, k - 1], ici_recv.at[p, k - 1], dev).wait_recv)
        return starts, waits

    return _Rider(
        [shards[name] for name in names],
        [jax.ShapeDtypeStruct(_full_shape(kind, shard), shards[name].dtype) for name, kind, shard, _ in pieces],
        [pltpu.SemaphoreType.DMA((n,)), pltpu.SemaphoreType.DMA((n, 3)), pltpu.SemaphoreType.DMA((n, 3))],
        build)


def _forward_rider(gathered, names):
    pieces = [_piece(n) for n in names]
    n = len(pieces)

    def build(ins, outs, sems):
        fwd_send, fwd_recv = sems
        x, y, c, s = _place()
        sibling = (x, y, 1 - c)
        starts, waits = [], []
        for p, (_, kind, shard, _) in enumerate(pieces):
            for k in (1, 2, 3):
                s2, _ = _other_chip(s, c, k)
                mine = _full_region(outs[p], kind, shard, s2, c)
                theirs = _full_region(outs[p], kind, shard, s2, 1 - c)
                cp = _remote(mine, mine, fwd_send.at[p, k - 1], fwd_recv.at[p, k - 1], sibling)
                starts.append(cp)
                waits.append(cp.wait_send)
                waits.append(_remote(theirs, theirs, fwd_send.at[p, k - 1], fwd_recv.at[p, k - 1], sibling).wait_recv)
        return starts, waits

    return _Rider(
        [gathered[name] for name in names],
        [jax.ShapeDtypeStruct(gathered[name].shape, gathered[name].dtype) for name in names],
        [pltpu.SemaphoreType.DMA((n, 3)), pltpu.SemaphoreType.DMA((n, 3))],
        build, aliased=True)


def _pair_exchange_rider(grads, names):
    n = len(names)

    def build(ins, outs, sems):
        send_sem, recv_sem = sems
        x, y, c, _ = _place()
        sibling = (x, y, 1 - c)
        starts, waits = [], []
        for p, name in enumerate(names):
            kind, shard = BIG_PIECES[name]
            for s2 in range(4):
                cp = _remote(_full_region(ins[p], kind, shard, s2, 1 - c), outs[p].at[s2], send_sem.at[p, s2],
                             recv_sem.at[p, s2], sibling)
                starts.append(cp)
                waits.append(cp.wait_send)
                waits.append(_remote(outs[p].at[s2], outs[p].at[s2], send_sem.at[p, s2], recv_sem.at[p, s2], sibling).wait_recv)
        return starts, waits

    return _Rider(
        [grads[name] for name in names],
        [jax.ShapeDtypeStruct((4,) + _half_shape(*BIG_PIECES[name]), F32) for name in names],
        [pltpu.SemaphoreType.DMA((n, 4))] * 2,
        build)


def _half_specs(kind, shard):
    half = shard[0] // 2
    if kind == "col":
        full = pl.BlockSpec((half, shard[1]), lambda j, pr: (pr[1], j))
        buf = pl.BlockSpec((None, half, shard[1]), lambda j, pr: (j, 0, 0))
    elif kind == "row":
        full = pl.BlockSpec((half, shard[1]), lambda j, pr: (2 * j + pr[1], 0))
        buf = pl.BlockSpec((None, half, shard[1]), lambda j, pr: (j, 0, 0))
    else:
        full = pl.BlockSpec((half, shard[1], shard[2]), lambda j, pr: (pr[1], j, 0))
        buf = pl.BlockSpec((None, half, shard[1], shard[2]), lambda j, pr: (j, 0, 0, 0))
    return full, buf


def _pair_sum(name, grad, recv, place):
    kind, shard = BIG_PIECES[name]
    full, buf = _half_specs(kind, shard)

    def body(pr, g_ref, r_ref, o_ref):
        o_ref[...] = (g_ref[...] + r_ref[...]).astype(BF16)

    return pl.pallas_call(
        body,
        name="pair_sum_" + name,
        grid_spec=pltpu.PrefetchScalarGridSpec(num_scalar_prefetch=1, grid=(4,), in_specs=[full, buf], out_specs=buf),
        out_shape=jax.ShapeDtypeStruct((4,) + _half_shape(kind, shard), BF16),
        compiler_params=_params(("arbitrary",)),
    )(place, grad, recv)


def _chip_exchange_rider(sums, names):
    n = len(names)

    def build(ins, outs, sems):
        send_sem, recv_sem = sems
        _, _, c, s = _place()
        starts, waits = [], []
        for p in range(n):
            for k in (1, 2, 3):
                s2, dev = _other_chip(s, c, k)
                cp = _remote(ins[p].at[s2], outs[p].at[k - 1], send_sem.at[p, k - 1], recv_sem.at[p, k - 1], dev)
                starts.append(cp)
                waits.append(cp.wait_send)
                waits.append(_remote(outs[p].at[k - 1], outs[p].at[k - 1], send_sem.at[p, k - 1], recv_sem.at[p, k - 1],
                                     dev).wait_recv)
        return starts, waits

    return _Rider(
        [sums[name] for name in names],
        [jax.ShapeDtypeStruct((3,) + _half_shape(*BIG_PIECES[name]), BF16) for name in names],
        [pltpu.SemaphoreType.DMA((n, 3))] * 2,
        build)


def _chip_sum(name, grad, recv_pair, recv_chip, place):
    kind, shard = BIG_PIECES[name]
    half = shard[0] // 2
    tail = tuple(shard[1:])
    zeros = (0,) * len(tail)
    nt = 4 if kind != "lru" and half % (4 * BF16_ROWS) == 0 else 1
    rows = half // nt
    if kind == "col":
        full = pl.BlockSpec((rows,) + tail, lambda j, pr: (pr[1] * nt + j, pr[0]))
    elif kind == "row":
        full = pl.BlockSpec((rows,) + tail, lambda j, pr: ((2 * pr[0] + pr[1]) * nt + j, 0))
    else:
        full = pl.BlockSpec((rows,) + tail, lambda j, pr: (pr[1], pr[0], 0))
    pair = pl.BlockSpec((None, rows) + tail, lambda j, pr: (pr[0], j) + zeros)
    chip = pl.BlockSpec((3, rows) + tail, lambda j, pr: (0, j) + zeros)
    out = pl.BlockSpec((rows,) + tail, lambda j, pr: (pr[1] * nt + j,) + zeros)

    def body(pr, g_ref, rp_ref, rc_ref, o_ref):
        total = g_ref[...] + rp_ref[...]
        for k in range(3):
            total = total + rc_ref[k].astype(F32)
        o_ref[...] = total

    return pl.pallas_call(
        body,
        name="chip_sum_" + name,
        grid_spec=pltpu.PrefetchScalarGridSpec(num_scalar_prefetch=1, grid=(nt,), in_specs=[full, pair, chip], out_specs=out),
        out_shape=jax.ShapeDtypeStruct(shard, F32),
        compiler_params=_params(("arbitrary",)),
    )(place, grad, recv_pair, recv_chip)


def _sibling_exchange_rider(halves, names):
    n = len(names)

    def build(ins, outs, sems):
        send_sem, recv_sem = sems
        x, y, c, _ = _place()
        sibling = (x, y, 1 - c)
        starts, waits = [], []
        for p, name in enumerate(names):
            shard = BIG_PIECES[name][1]
            mine = _shard_region(outs[p], shard, c)
            theirs = _shard_region(outs[p], shard, 1 - c)
            cp = _remote(mine, mine, send_sem.at[p], recv_sem.at[p], sibling)
            starts.append(cp)
            waits.append(cp.wait_send)
            waits.append(_remote(theirs, theirs, send_sem.at[p], recv_sem.at[p], sibling).wait_recv)
        return starts, waits

    return _Rider(
        [halves[name] for name in names],
        [jax.ShapeDtypeStruct(BIG_PIECES[name][1], F32) for name in names],
        [pltpu.SemaphoreType.DMA((n,))] * 2,
        build, aliased=True)


FIRST_WEIGHTS = ["meta_tokens", "conv_w"]
WEIGHT_GROUPS = {
    "lru": ["lru_wa", "lru_wx"],
    "branch": ["w_branch_ret", "w_branch_lru", "w_out"],
    "ffn_in": ["w_ffn_in"],
    "ffn_out": ["w_ffn_out"],
}
WEIGHT_SCHEDULE = {
    "in_proj": [("gather", "lru")],
    "retention_fwd": [("forward", "lru"), ("gather", "branch")],
    "lru_fwd": [("forward", "branch"), ("gather", "ffn_in")],
    "mix_fwd": [("forward", "ffn_in"), ("gather", "ffn_out")],
    "ffn_fwd_gate": [("forward", "ffn_out")],
}
GRAD_GROUPS = {
    "ffn_in": ["w_ffn_in"],
    "ffn_out": ["w_ffn_out"],
    "mixer": ["w_out", "w_branch_ret", "w_branch_lru", "lru_wa", "lru_wx"],
    "in": ["w_in"],
}
GRAD_SCHEDULE = {
    "mix_bwd": [("pair", "ffn_in"), ("pair", "ffn_out")],
    "retention_bwd": [("chip", "ffn_out")],
    "lru_bwd": [("chip", "ffn_in"), ("sibling", "ffn_out")],
    "dw_in": [("sibling", "ffn_in"), ("pair", "mixer")],
    "in_proj_bwd_0": [("chip", "mixer"), ("pair", "in")],
    "in_proj_bwd_1": [("sibling", "mixer"), ("chip", "in")],
}


class _CommPlan:
    def __init__(self, shards, place):
        self.shards, self.place = shards, place
        self.late = {}
        self.recv_pair, self.sums, self.recv_chip, self.halves, self.final = {}, {}, {}, {}, {}

    def _grad_rider(self, stage, group, grads):
        names = GRAD_GROUPS[group]
        if stage == "pair":
            return _pair_exchange_rider(grads, names)
        if stage == "chip":
            return _chip_exchange_rider(self.sums, names)
        return _sibling_exchange_rider(self.halves, names)

    def _grad_after(self, stage, group, outs, grads):
        names = GRAD_GROUPS[group]
        if stage == "pair":
            for n, o in zip(names, outs):
                self.recv_pair[n] = o
                self.sums[n] = _pair_sum(n, grads[n], o, self.place)
        elif stage == "chip":
            for n, o in zip(names, outs):
                self.halves[n] = _chip_sum(n, grads[n], self.recv_pair[n], o, self.place)
        else:
            self.final.update(zip(names, outs))

    def riders(self, host, w, grads):
        if host in WEIGHT_SCHEDULE:
            return [_gather_rider(self.shards, WEIGHT_GROUPS[group]) if stage == "gather"
                    else _forward_rider(self.late, WEIGHT_GROUPS[group]) for stage, group in WEIGHT_SCHEDULE[host]]
        return [self._grad_rider(stage, group, grads) for stage, group in GRAD_SCHEDULE.get(host, [])]

    def after(self, host, rider_outs, w, grads):
        for (stage, group), outs in zip(WEIGHT_SCHEDULE.get(host, []), rider_outs):
            (self.late if stage == "gather" else w).update(zip(WEIGHT_GROUPS[group], outs))
        for (stage, group), outs in zip(GRAD_SCHEDULE.get(host, []), rider_outs):
            self._grad_after(stage, group, outs, grads)

    def finish(self, partial):
        outs, (blocks,) = _run_riders("tail_exchange", [_sibling_exchange_rider(self.halves, GRAD_GROUPS["in"]),
                                                        _small_exchange_rider(partial)])
        self.final.update(zip(GRAD_GROUPS["in"], outs))
        return self.final, blocks


def _adamw_math(w, g, m, v):
    m = ADAM_B1 * m + (1.0 - ADAM_B1) * g
    v = ADAM_B2 * v + (1.0 - ADAM_B2) * (g * g)
    m_hat = m / (1.0 - ADAM_B1 ** ADAM_STEP)
    v_hat = v / (1.0 - ADAM_B2 ** ADAM_STEP)
    delta = -ADAM_LR * (m_hat / (jnp.sqrt(v_hat) + ADAM_EPS) + ADAM_WD * w)
    return delta, m, v


def _adamw(name, g, w, m, v):
    rows, cols = g.shape
    tr = rows // 4 if rows % 32 == 0 else rows

    def body(g_ref, w_ref, m_ref, v_ref, go_ref, d_ref, mo_ref, vo_ref):
        gv = g_ref[...]
        delta, m2, v2 = _adamw_math(w_ref[...], gv, m_ref[...], v_ref[...])
        go_ref[...] = gv
        d_ref[...] = delta
        mo_ref[...] = m2
        vo_ref[...] = v2

    spec = pl.BlockSpec((tr, cols), lambda i: (i, 0))
    return pl.pallas_call(
        body,
        name="adamw_" + name,
        grid=(rows // tr,),
        in_specs=[spec] * 4,
        out_specs=[spec] * 4,
        out_shape=[jax.ShapeDtypeStruct((rows, cols), F32)] * 4,
        compiler_params=_params(("arbitrary",)),
    )(g, w, m, v)


SMALL_ROWS = 48
VEC_ROWS = {"final_norm_w": 1, "ffn_norm_w": 8, "mix_norm_w": 16, "conv_b": 24, "lru_ba": 25, "lru_bx": 26, "lru_lambda": 27}
VEC_NAMES = list(VEC_ROWS)
CONV_W_ROW = 28
META_ROW = 32


def _small_exchange_rider(partial):
    def build(ins, outs, sems):
        local_sem, send_sem, recv_sem = sems
        _, _, c, s = _place()
        me = 2 * s + c
        cp = pltpu.make_async_copy(ins[0], outs[0].at[me], local_sem)
        starts, waits = [cp], [cp.wait]
        for k in range(1, 8):
            peer = jnp.bitwise_xor(me, k)
            dev = (peer // 4, (peer // 2) % 2, peer % 2)
            rd = _remote(ins[0], outs[0].at[me], send_sem.at[k - 1], recv_sem.at[k - 1], dev)
            starts.append(rd)
            waits.append(rd.wait_send)
            waits.append(_remote(ins[0], outs[0].at[peer], send_sem.at[k - 1], recv_sem.at[k - 1], dev).wait_recv)
        return starts, waits

    return _Rider([partial], [jax.ShapeDtypeStruct((8, SMALL_ROWS, D), F32)],
                  [pltpu.SemaphoreType.DMA, pltpu.SemaphoreType.DMA((7,)), pltpu.SemaphoreType.DMA((7,))], build)


def _small_update(blocks, place, vecs, conv, meta):
    nvec = len(VEC_NAMES)
    qcols = D // 4

    def in_order(ref):
        total = ref[0]
        for d in range(1, 8):
            total = total + ref[d]
        return total

    def body(pr, blocks_ref, cols_ref, *refs):
        vec_refs = refs[:3 * nvec]
        conv_refs = refs[3 * nvec:3 * nvec + 3]
        meta_refs = refs[3 * nvec + 3:3 * nvec + 6]
        outs = refs[3 * nvec + 6:]
        loss_ref, vec_out, conv_out, meta_out = outs[0], outs[1:5], outs[5:9], outs[9:13]
        tot, col = in_order(blocks_ref), in_order(cols_ref)
        loss_ref[...] = jnp.sum(tot[0:1, :], axis=1, keepdims=True)
        for o in vec_out:
            o[...] = jnp.zeros_like(o)
        for j, name in enumerate(VEC_NAMES):
            w, m, v = (r[...] for r in vec_refs[3 * j:3 * j + 3])
            g = tot[VEC_ROWS[name]:VEC_ROWS[name] + 1, :]
            if name == "lru_lambda":
                g = -g / (1.0 + jnp.exp(w))
            for o, val in zip(vec_out, (g,) + _adamw_math(w, g, m, v)):
                o[j:j + 1, :] = val
        for row, n_rows, ins, group in ((CONV_W_ROW, 4, conv_refs, conv_out), (META_ROW, N_META, meta_refs, meta_out)):
            g = col[row:row + n_rows, :]
            w, m, v = (r[...] for r in ins)
            for o, val in zip(group, (g,) + _adamw_math(w, g, m, v)):
                o[...] = val

    whole = lambda shape: pl.BlockSpec(shape, lambda i, pr: (0,) * len(shape))
    in_specs = [whole((8, SMALL_ROWS, D)), pl.BlockSpec((8, SMALL_ROWS, qcols), lambda i, pr: (0, 0, pr[0]))]
    in_specs += [whole((1, D))] * (3 * nvec) + [whole((4, qcols))] * 3 + [whole((N_META, qcols))] * 3
    out_shapes = [(1, 1)] + [(8, D)] * 4 + [(4, qcols)] * 4 + [(N_META, qcols)] * 4
    return pl.pallas_call(
        body,
        name="small_update",
        grid_spec=pltpu.PrefetchScalarGridSpec(num_scalar_prefetch=1, grid=(1,), in_specs=in_specs,
                                               out_specs=[whole(s) for s in out_shapes]),
        out_shape=[jax.ShapeDtypeStruct(s, F32) for s in out_shapes],
        compiler_params=_params(("arbitrary",)),
    )(place, blocks, blocks, *[a for t in vecs for a in t], *conv, *meta)


WEIGHT_ORDER = ["meta_tokens", "mix_norm_w", "w_in", "conv_w", "conv_b", "lru_wa", "lru_ba", "lru_wx", "lru_bx", "lru_lambda",
                "w_branch_ret", "w_branch_lru", "w_out", "ffn_norm_w", "w_ffn_in", "w_ffn_out", "final_norm_w"]


def kernel(x, meta_tokens, mix_norm_w, w_in, conv_w, conv_b, lru_wa, lru_ba, lru_wx, lru_bx, lru_lambda, w_branch_ret, w_branch_lru, w_out, ffn_norm_w, w_ffn_in, w_ffn_out, final_norm_w, loss_target, m_meta_tokens, m_mix_norm_w, m_w_in, m_conv_w, m_conv_b, m_lru_wa, m_lru_ba, m_lru_wx, m_lru_bx, m_lru_lambda, m_w_branch_ret, m_w_branch_lru, m_w_out, m_ffn_norm_w, m_w_ffn_in, m_w_ffn_out, m_final_norm_w, v_meta_tokens, v_mix_norm_w, v_w_in, v_conv_w, v_conv_b, v_lru_wa, v_lru_ba, v_lru_wx, v_lru_bx, v_lru_lambda, v_w_branch_ret, v_w_branch_lru, v_w_out, v_ffn_norm_w, v_w_ffn_in, v_w_ffn_out, v_final_norm_w):
    args = locals()
    wts = {n: args[n] for n in WEIGHT_ORDER}
    mom = {n: args["m_" + n] for n in WEIGHT_ORDER}
    var = {n: args["v_" + n] for n in WEIGHT_ORDER}
    place = jnp.stack([2 * lax.axis_index("x") + lax.axis_index("y"), lax.axis_index("c")]).astype(jnp.int32)

    shards = {n: wts[n][0].astype(BF16) for n in BIG_PIECES}
    shards["meta_tokens"] = wts["meta_tokens"]
    shards["conv_w"] = wts["conv_w"][0]
    plan = _CommPlan(shards, place)
    (first,) = _run_riders("gather_first", [_gather_rider(shards, FIRST_WEIGHTS)])
    w = dict(zip(FIRST_WEIGHTS, first))
    for n in VEC_NAMES:
        w[n] = wts[n].reshape(1, D)
    chips = jnp.bitwise_xor(place[0], jnp.array((0,) + SHARD_ORDER, dtype=jnp.int32))
    w["route"] = jnp.concatenate([place, jnp.stack([2 * chips, 2 * chips + 1], axis=1).reshape(8)])
    w["w_in_shard"] = shards["w_in"]

    grad_x, grad_head, _, stats = _local_step(x[0], loss_target[0], w, plan)
    partial = jnp.concatenate(stats + [grad_head[PAD_ROWS:]], axis=0)
    shard_grads, blocks = plan.finish(partial)

    out = {}
    for n in BIG_PIECES:
        shape2d = (-1, wts[n].shape[-1])
        res = _adamw(n, *[a.reshape(shape2d) for a in (shard_grads[n], wts[n], mom[n], var[n])])
        out[n] = [r.reshape(wts[n].shape) for r in res]

    vecs = [tuple(a[n].reshape(1, D) for a in (wts, mom, var)) for n in VEC_NAMES]
    conv = tuple(a["conv_w"][0] for a in (wts, mom, var))
    meta = tuple(a["meta_tokens"] for a in (wts, mom, var))
    res = _small_update(blocks, place, vecs, conv, meta)
    loss = res[0].reshape(())
    for j, n in enumerate(VEC_NAMES):
        out[n] = [r[j].reshape(wts[n].shape) for r in res[1:5]]
    out["conv_w"] = [r.reshape(wts["conv_w"].shape) for r in res[5:9]]
    out["meta_tokens"] = list(res[9:13])

    return (loss, grad_x.reshape(x.shape)) + tuple(out[n][kind] for kind in range(4) for n in WEIGHT_ORDER)
```

```python
import functools
import math

import jax
import jax.numpy as jnp
from jax import lax
from jax.experimental import pallas as pl
from jax.experimental.pallas import tpu as pltpu

F32 = jnp.float32
BF16 = jnp.bfloat16

LANES = 128
BF16_ROWS = 16

D = 1024
HEADS = 8
DH = 128
CHUNK = 256
N_META = 16
FRONT = 256
PAD_ROWS = FRONT - N_META
LRU_BLOCKS = 4
LRU_BLOCK = 256
LRU_C = 8.0
FFN = 2816
FFN_HALF = FFN // 2
IN_COLS = 8 * D
ROPE_BASE = 10000.0
EPS = 1e-6
QK_SCALE = DH ** -0.5

ADAM_LR = 0.001
ADAM_B1 = 0.9
ADAM_B2 = 0.999
ADAM_EPS = 1e-08
ADAM_WD = 0.01
ADAM_STEP = 10

TM = 256
TM_HEAVY = 768
MXU_TILE = 256
FFN_SUB = 2 * MXU_TILE
_FFN_SUBS = [(a, min(a + FFN_SUB, FFN)) for a in range(0, FFN, FFN_SUB)]
TM_MIX_BWD = 384
TM_LRU = 384
TM_MIX_FWD = 384
TM_IN_PROJ = 1408
VMEM_LIMIT = 60 * 1024 * 1024
TN_VMEM_BUDGET = 40 * 1024 * 1024

NT_DIMS = (((1,), (1,)), ((), ()))
TN_DIMS = (((0,), (0,)), ((), ()))
MESH = pl.DeviceIdType.MESH


def _params(sem=None):
    if sem is None:
        return pltpu.CompilerParams(vmem_limit_bytes=VMEM_LIMIT)
    return pltpu.CompilerParams(dimension_semantics=sem, vmem_limit_bytes=VMEM_LIMIT)


def _dot(a, b):
    return jnp.dot(a, b, preferred_element_type=F32)


def _dot_nt(a, b):
    return lax.dot_general(a, b, NT_DIMS, preferred_element_type=F32)


def _dot_tn(a, b):
    return lax.dot_general(a, b, TN_DIMS, preferred_element_type=F32)


def _sigmoid(z):
    return 1.0 / (1.0 + jnp.exp(-z))


def _log1p(x):
    return jnp.where(x < 1e-3, x * (1.0 - x * (0.5 - x * (1.0 / 3.0))), jnp.log(1.0 + x))


def _softplus(x):
    return jnp.maximum(x, 0.0) + _log1p(jnp.exp(-jnp.abs(x)))


def _one_minus_square(a, log_a):
    x = 2.0 * log_a
    series = -x * (1.0 + x * (0.5 + x * (1.0 / 6.0)))
    return jnp.where(x > -0.02, series, 1.0 - a * a)


_GELU_K = math.sqrt(2.0 / math.pi)


def _gelu_and_grad(x):
    inner = _GELU_K * (x + 0.044715 * x * x * x)
    t = jnp.tanh(inner)
    val = 0.5 * x * (1.0 + t)
    grad = 0.5 * (1.0 + t) + 0.5 * x * (1.0 - t * t) * _GELU_K * (1.0 + 3.0 * 0.044715 * x * x)
    return val, grad


def _row_ids(i, rows, shape):
    return i * rows + lax.broadcasted_iota(jnp.int32, shape, 0)


def _rope_tables(rows):
    inv_freq = ROPE_BASE ** (-jnp.arange(0, DH, 2, dtype=F32) / DH)
    block_pos = jnp.arange(rows // FRONT, dtype=jnp.int32) * FRONT - PAD_ROWS
    coarse = block_pos.astype(F32)[:, None] * inv_freq[None, :]
    fine = jnp.arange(FRONT, dtype=F32)[:, None] * inv_freq[None, :]
    ca, sa = jnp.cos(coarse)[:, None, :], jnp.sin(coarse)[:, None, :]
    cb, sb = jnp.cos(fine)[None, :, :], jnp.sin(fine)[None, :, :]
    cos = (ca * cb - sa * sb).reshape(rows, DH // 2)
    sin = (sa * cb + ca * sb).reshape(rows, DH // 2)
    return jnp.concatenate([cos, cos], axis=1), jnp.concatenate([-sin, sin], axis=1)


def _decay_tables():
    log_g = jnp.log(1.0 - 2.0 ** (-5.0 - jnp.arange(HEADS, dtype=F32)))
    idx = jnp.arange(CHUNK, dtype=F32)
    diff = idx[:, None] - idx[None, :]
    intra = jnp.where(diff[None] >= 0, jnp.exp(jnp.maximum(diff, 0.0)[None] * log_g[:, None, None]), 0.0)
    q_decay = jnp.exp((idx + 1.0)[:, None] * log_g[None, :])
    k_decay = jnp.exp((CHUNK - 1.0 - idx)[:, None] * log_g[None, :])
    chunk_decay = jnp.exp(CHUNK * log_g)
    wide = lambda a: jnp.repeat(a, DH, axis=-1)
    return intra, jnp.swapaxes(intra, 1, 2), wide(q_decay), wide(k_decay), wide(chunk_decay[None, :])


def _norm1(head, x2d, norm_w, riders=()):
    rows = FRONT + x2d.shape[0]
    tm = TM_IN_PROJ if rows % TM_IN_PROJ == 0 else TM_MIX_FWD
    nt = rows // tm

    def body(head_hbm, x_hbm, nw_ref, u_ref, rstd_ref, h0_sc, h0_sem):
        slot = _frame_rows(head_hbm, x_hbm, h0_sc, h0_sem, pl.program_id(0), tm, nt)
        hv = h0_sc[slot]
        rs = lax.rsqrt(jnp.mean(hv * hv, axis=-1, keepdims=True) + EPS)
        u_ref[...] = ((hv * rs) * nw_ref[...]).astype(BF16)
        rstd_ref[...] = rs

    return _hosted_call(
        body,
        name="norm1",
        grid=(nt,),
        in_specs=[_ANY, _ANY, pl.BlockSpec((1, D), lambda i: (0, 0))],
        out_specs=[pl.BlockSpec((tm, D), lambda i: (i, 0)), pl.BlockSpec((tm, 1), lambda i: (i, 0))],
        out_shape=[jax.ShapeDtypeStruct((rows, D), BF16), jax.ShapeDtypeStruct((rows, 1), F32)],
        scratch_shapes=[pltpu.VMEM((2, tm, D), F32), pltpu.SemaphoreType.DMA((3,))],
        args=(head, x2d, norm_w),
        riders=riders,
    )


def _heavy_tile(rows):
    return TM_HEAVY if rows % TM_HEAVY == 0 else TM


def _lru_tile(rows):
    return TM_LRU if rows % TM_LRU == 0 else TM


def _swap_3_4(group):
    return jnp.where(group == 3, 4, jnp.where(group == 4, 3, group))


SHARD_ORDER = (2, 3, 1)
LRU_IN_COL = 3
GATES_COL = 1


def _in_proj(u, w_shard, route, cos_t, sin_t, riders=()):
    rows = u.shape[0]
    tm = TM_IN_PROJ if rows % TM_IN_PROJ == 0 else _heavy_tile(rows)
    nt = rows // tm
    kind, shard = BIG_PIECES["w_in"]

    def body(route_ref, u_hbm, wsh_ref, cos_ref, sin_ref, proj_ref, wfull_ref,
             u_sc, w_sc, u_sem, w_sem, local_sem, ici_send, ici_recv, fwd_send, fwd_recv):
        g, i = pl.program_id(0), pl.program_id(1)
        s, c = route_ref[0], route_ref[1]
        gid = route_ref[2 + g]
        sibling = (s // 2, s % 2, 1 - c)
        local = pltpu.make_async_copy(wsh_ref, _full_region(wfull_ref, kind, shard, s, None), local_sem)
        u_copies = [pltpu.make_async_copy(u_hbm.at[pl.ds(t * tm, tm)], u_sc.at[t], u_sem.at[t]) for t in range(nt)]
        sends, arrivals = [], []
        for k in (1, 2, 3):
            s2, dev = _other_chip(s, c, k)
            sends.append(_remote(_shard_region(wsh_ref, shard, c), _full_region(wfull_ref, kind, shard, s, c),
                                 ici_send.at[k - 1], ici_recv.at[k - 1], dev))
            mine = _full_region(wfull_ref, kind, shard, s2, c)
            theirs = _full_region(wfull_ref, kind, shard, s2, 1 - c)
            arrivals.append((_remote(mine, mine, ici_send.at[k - 1], ici_recv.at[k - 1], dev),
                             _remote(mine, mine, fwd_send.at[k - 1], fwd_recv.at[k - 1], sibling),
                             _remote(theirs, theirs, fwd_send.at[k - 1], fwd_recv.at[k - 1], sibling)))

        slot = g % 2
        last_tile = i == nt - 1

        def block_copy(src, col, to_slot):
            return pltpu.make_async_copy(src.at[:, pl.ds(pl.multiple_of(col * D, D), D)], w_sc.at[to_slot],
                                         w_sem.at[to_slot])

        @pl.when(jnp.logical_and(g == 0, i == 0))
        def _():
            for k in SHARD_ORDER:
                sends[k - 1].start()
            local.start()
            for cp in u_copies:
                cp.start()
            cp = block_copy(wsh_ref, 0, 0)
            cp.start()
            cp.wait()

        @pl.when(jnp.logical_and(last_tile, g == 0))
        def _():
            block_copy(wsh_ref, 1, 1).start()

        for pos, k in enumerate(SHARD_ORDER, start=1):
            arrived, forward, forwarded = arrivals[k - 1]

            @pl.when(jnp.logical_and(last_tile, g == 2 * pos - 1))
            def _():
                arrived.wait_recv()
                forward.start()
                forwarded.wait_recv()
                block_copy(wfull_ref, route_ref[2 + 2 * pos], 0).start()

            @pl.when(jnp.logical_and(last_tile, g == 2 * pos))
            def _():
                block_copy(wfull_ref, route_ref[3 + 2 * pos], 1).start()

        @pl.when(jnp.logical_and(i == 0, g > 0))
        def _():
            block_copy(wfull_ref, 0, slot).wait()

        for t in range(nt):
            @pl.when(jnp.logical_and(g == 0, i == t))
            def _():
                u_copies[t].wait()

        halves = range(0, D, D // 2)

        @pl.when(gid < 2)
        def _():
            scale = jnp.where(gid == 1, QK_SCALE, 1.0).astype(F32)
            for a in halves:
                acc = _dot(u_sc[i], w_sc[slot, :, a:a + D // 2])
                for h in range(HEADS // 2):
                    blk = acc[:, h * DH:(h + 1) * DH]
                    out = (blk * cos_ref[...] + pltpu.roll(blk, DH // 2, axis=1) * sin_ref[...]) * scale
                    proj_ref[:, a + h * DH:a + (h + 1) * DH] = out.astype(BF16)

        @pl.when(gid >= 2)
        def _():
            for a in halves:
                proj_ref[:, a:a + D // 2] = _dot(u_sc[i], w_sc[slot, :, a:a + D // 2]).astype(BF16)

        @pl.when(jnp.logical_and(g == 7, i == nt - 1))
        def _():
            local.wait()
            for cp in sends:
                cp.wait_send()
            for _, forward, _ in arrivals:
                forward.wait_send()

    return _hosted_call(
        body,
        name="in_proj",
        grid=(8, nt),
        in_specs=[
            _ANY, _ANY,
            pl.BlockSpec((tm, DH), lambda g, i, rt: (i, 0)),
            pl.BlockSpec((tm, DH), lambda g, i, rt: (i, 0)),
        ],
        out_specs=[pl.BlockSpec((tm, D), lambda g, i, rt: (i, _swap_3_4(rt[2 + g]))), _ANY],
        out_shape=[jax.ShapeDtypeStruct((rows, IN_COLS), BF16), jax.ShapeDtypeStruct((D, IN_COLS), BF16)],
        scratch_shapes=[
            pltpu.VMEM((nt, tm, D), BF16), pltpu.VMEM((2, D, D), BF16),
            pltpu.SemaphoreType.DMA((nt,)), pltpu.SemaphoreType.DMA((2,)), pltpu.SemaphoreType.DMA,
            pltpu.SemaphoreType.DMA((3,)), pltpu.SemaphoreType.DMA((3,)),
            pltpu.SemaphoreType.DMA((3,)), pltpu.SemaphoreType.DMA((3,)),
        ],
        args=(u, w_shard, cos_t, sin_t),
        riders=riders,
        prefetch=route,
        riders_after_body=True,
    )


def _retention_fwd(proj_bf, intra, q_dec, k_dec, c_dec, riders=()):
    rows = proj_bf.shape[0]
    nc = rows // CHUNK

    def body(q_ref, k_ref, v_ref, m_ref, qd_ref, kd_ref, cd_ref, o_ref, sprev_ref, s_sc):
        @pl.when(pl.program_id(0) == 0)
        def _():
            s_sc[...] = jnp.zeros_like(s_sc)

        for h in range(HEADS):
            sl = slice(h * DH, (h + 1) * DH)
            q, k, v = q_ref[:, sl], k_ref[:, sl], v_ref[:, sl]
            state = s_sc[h]
            state_b = state.astype(BF16)
            sprev_ref[0, h] = state_b
            s = _dot_nt(q, k) * m_ref[h]
            inner = _dot(s.astype(BF16), v)
            cross = _dot(q, state_b) * qd_ref[:, sl]
            o_ref[:, sl] = (inner + cross).astype(BF16)
            k_scaled = (k.astype(F32) * kd_ref[:, sl]).astype(BF16)
            s_sc[h] = state * cd_ref[:, sl] + _dot_tn(k_scaled, v)

    chunk_spec = lambda col: pl.BlockSpec((CHUNK, D), lambda c: (c, col))
    const2 = lambda shape: pl.BlockSpec(shape, lambda c: (0, 0))
    return _hosted_call(
        body,
        name="retention_fwd",
        grid=(nc,),
        in_specs=[
            chunk_spec(0), chunk_spec(1), chunk_spec(2),
            pl.BlockSpec((HEADS, CHUNK, CHUNK), lambda c: (0, 0, 0)),
            const2((CHUNK, D)), const2((CHUNK, D)), const2((1, D)),
        ],
        out_specs=[
            pl.BlockSpec((CHUNK, D), lambda c: (c, 0)),
            pl.BlockSpec((1, HEADS, DH, DH), lambda c: (c, 0, 0, 0)),
        ],
        out_shape=[
            jax.ShapeDtypeStruct((rows, D), BF16),
            jax.ShapeDtypeStruct((nc, HEADS, DH, DH), BF16),
        ],
        scratch_shapes=[pltpu.VMEM((HEADS, DH, DH), F32)],
        args=(proj_bf, proj_bf, proj_bf, intra, q_dec, k_dec, c_dec),
        riders=riders,
    )


def _shift_down(x, first8_prev, d):
    rolled = pltpu.roll(x, d, axis=0)
    head = pltpu.roll(jnp.concatenate([first8_prev, x[0:8]], axis=0), d, axis=0)[8:16]
    return rolled, head


def _conv_and_gates(x, prev8, cw_ref, cb_ref, wa_ref, wx_ref, ba_ref, bx_ref, lam_ref, c_sc):
    cw = cw_ref[...]
    conv = cb_ref[...] + cw[3:4] * x
    head = cb_ref[...] + cw[3:4] * x[0:8]
    for d in (1, 2, 3):
        rolled, hd = _shift_down(x, prev8, d)
        conv = conv + cw[3 - d:4 - d] * rolled
        head = head + cw[3 - d:4 - d] * hd
    c_sc[...] = conv
    c_sc[0:8, :] = head
    c = c_sc[...]
    zr, zi = [], []
    for g in range(LRU_BLOCKS):
        sl = slice(g * LRU_BLOCK, (g + 1) * LRU_BLOCK)
        cg = c[:, sl].astype(BF16)
        zr.append(_dot(cg, wa_ref[g]))
        zi.append(_dot(cg, wx_ref[g]))
    r = _sigmoid(jnp.concatenate(zr, axis=1) + ba_ref[...])
    gate_i = _sigmoid(jnp.concatenate(zi, axis=1) + bx_ref[...])
    sp = _softplus(-lam_ref[...])
    log_a = (-LRU_C) * r * sp
    a = jnp.exp(log_a)
    mult = jnp.sqrt(_one_minus_square(a, log_a))
    return c, r, gate_i, a, mult, log_a


def _lru_fwd(proj, conv_w, conv_b, wa, wx, ba, bx, lam, riders=()):
    rows = proj.shape[0]
    TM = _lru_tile(rows)
    nt = rows // TM

    def body(x_ref, cw_ref, cb_ref, wa_ref, wx_ref, ba_ref, bx_ref, lam_ref,
             h_ref, c_ref, r_ref, i_ref, la_ref, mult_ref, prev_sc, carry_sc, c_sc, a_sc, u_sc, h_sc):
        i = pl.program_id(0)

        @pl.when(i == 0)
        def _():
            prev_sc[...] = jnp.zeros_like(prev_sc)
            carry_sc[...] = jnp.zeros_like(carry_sc)

        x = x_ref[...].astype(F32)
        c, r, gate_i, a, mult, log_a = _conv_and_gates(x, prev_sc[...], cw_ref, cb_ref, wa_ref, wx_ref, ba_ref, bx_ref,
                                                       lam_ref, c_sc)
        for ref, val in ((c_ref, c), (r_ref, r), (i_ref, gate_i), (la_ref, log_a), (mult_ref, mult)):
            ref[...] = val.astype(BF16)
        prev_sc[...] = x[TM - 8:TM]
        valid = _row_ids(i, TM, (TM, D)) >= PAD_ROWS
        a_sc[...] = a
        u_sc[...] = jnp.where(valid, mult * gate_i * c, 0.0)
        row8 = lax.broadcasted_iota(jnp.int32, (8, D), 0)

        def group(gi, hprev):
            r0 = pl.multiple_of(gi * 8, 8)
            aa = a_sc[pl.ds(r0, 8), :]
            uu = u_sc[pl.ds(r0, 8), :]
            for d in (1, 2, 4):
                a_sh = jnp.where(row8 >= d, pltpu.roll(aa, d, axis=0), 1.0)
                u_sh = jnp.where(row8 >= d, pltpu.roll(uu, d, axis=0), 0.0)
                uu = uu + aa * u_sh
                aa = aa * a_sh
            hb = aa * hprev + uu
            h_sc[pl.ds(r0, 8), :] = hb
            return hb[7:8, :]

        hlast = lax.fori_loop(0, TM // 8, group, carry_sc[0:1, :])
        carry_sc[0:1, :] = hlast
        h_ref[...] = h_sc[...].astype(BF16)

    vec = pl.BlockSpec((1, D), lambda i: (0, 0))
    wspec = pl.BlockSpec((LRU_BLOCKS, LRU_BLOCK, LRU_BLOCK), lambda i: (0, 0, 0))
    return _hosted_call(
        body,
        name="lru_fwd",
        grid=(nt,),
        in_specs=[
            pl.BlockSpec((TM, D), lambda i: (i, LRU_IN_COL)),
            pl.BlockSpec((4, D), lambda i: (0, 0)),
            vec, wspec, wspec, vec, vec, vec,
        ],
        out_specs=[pl.BlockSpec((TM, D), lambda i: (i, 0)) for _ in range(6)],
        out_shape=[jax.ShapeDtypeStruct((rows, D), BF16) for _ in range(6)],
        scratch_shapes=[
            pltpu.VMEM((8, D), F32), pltpu.VMEM((8, D), F32),
            pltpu.VMEM((TM, D), F32), pltpu.VMEM((TM, D), F32), pltpu.VMEM((TM, D), F32), pltpu.VMEM((TM, D), F32),
        ],
        args=(proj, conv_w, conv_b, wa, wx, ba, bx, lam),
        riders=riders,
    )


def _group_norm(o):
    outs, rstds = [], []
    for h in range(HEADS):
        oh = o[:, h * DH:(h + 1) * DH]
        rs = lax.rsqrt(jnp.mean(oh * oh, axis=-1, keepdims=True) + EPS)
        outs.append(oh * rs)
        rstds.append(rs)
    return jnp.concatenate(outs, axis=1), rstds


def _frame_rows(head_hbm, x_hbm, buf, sems, i, tm, nt):
    slot = i % 2

    @pl.when(i == 0)
    def _():
        first = [pltpu.make_async_copy(head_hbm, buf.at[0, pl.ds(0, FRONT)], sems.at[2]),
                 pltpu.make_async_copy(x_hbm.at[pl.ds(0, tm - FRONT)], buf.at[0, pl.ds(FRONT, tm - FRONT)], sems.at[0])]
        for cp in first:
            cp.start()
        for cp in first:
            cp.wait()

    @pl.when(i + 1 < nt)
    def _():
        start = pl.multiple_of((i + 1) * tm - FRONT, LANES)
        pltpu.make_async_copy(x_hbm.at[pl.ds(start, tm)], buf.at[1 - slot], sems.at[1 - slot]).start()

    @pl.when(i > 0)
    def _():
        pltpu.make_async_copy(x_hbm.at[pl.ds(0, tm)], buf.at[slot], sems.at[slot]).wait()

    return slot


def _mix_fwd(head, x2d, o, proj, h_lru, w_br, w_bl, w_o, ffn_norm_w, riders=()):
    rows = o.shape[0]
    tm = TM_MIX_FWD
    assert rows % tm == 0 and tm > FRONT
    nt = rows // tm

    def body(head_hbm, x_hbm, o_ref, gates_ref, hl_ref, wbr_ref, wbl_ref, wo_ref, nw_ref,
             yret_ref, ylru_ref, h1_ref, u2_ref, rstd_ref, h0_sc, h0_sem):
        i = pl.program_id(0)
        slot = _frame_rows(head_hbm, x_hbm, h0_sc, h0_sem, i, tm, nt)
        gate = lambda j: gates_ref[:, j * D:(j + 1) * D].astype(F32)
        on, _ = _group_norm(o_ref[...].astype(F32))
        gret = gate(0)
        a_ret = (gret * _sigmoid(gret) * on).astype(BF16)
        y_ret = _dot(a_ret, wbr_ref[...])
        gl, _ = _gelu_and_grad(gate(1))
        a_lru = (gl * hl_ref[...].astype(F32)).astype(BF16)
        y_lru = _dot(a_lru, wbl_ref[...])
        mixed = (_sigmoid(gate(2)) * y_ret + _sigmoid(gate(3)) * y_lru).astype(BF16)
        delta = _dot(mixed, wo_ref[...])
        yret_ref[...] = y_ret.astype(BF16)
        ylru_ref[...] = y_lru.astype(BF16)
        h1 = h0_sc[slot] + delta
        rs = lax.rsqrt(jnp.mean(h1 * h1, axis=-1, keepdims=True) + EPS)
        h1_ref[...] = h1
        u2_ref[...] = ((h1 * rs) * nw_ref[...]).astype(BF16)
        rstd_ref[...] = rs

    tile = lambda col: pl.BlockSpec((tm, D), lambda i: (i, col))
    wspec = pl.BlockSpec((D, D), lambda i: (0, 0))
    return _hosted_call(
        body,
        name="mix_fwd",
        grid=(nt,),
        in_specs=[
            _ANY, _ANY,
            tile(0), pl.BlockSpec((tm, 4 * D), lambda i: (i, GATES_COL)), tile(0),
            wspec, wspec, wspec,
            pl.BlockSpec((1, D), lambda i: (0, 0)),
        ],
        out_specs=[tile(0), tile(0), tile(0), tile(0), pl.BlockSpec((tm, 1), lambda i: (i, 0))],
        out_shape=[
            jax.ShapeDtypeStruct((rows, D), BF16), jax.ShapeDtypeStruct((rows, D), BF16),
            jax.ShapeDtypeStruct((rows, D), F32), jax.ShapeDtypeStruct((rows, D), BF16),
            jax.ShapeDtypeStruct((rows, 1), F32),
        ],
        scratch_shapes=[pltpu.VMEM((2, tm, D), F32), pltpu.SemaphoreType.DMA((3,))],
        args=(head, x2d, o, proj, h_lru, w_br, w_bl, w_o, ffn_norm_w),
        riders=riders,
    )


def _ffn_fwd_gate(u2, w_ffn_in, riders=()):
    rows = u2.shape[0]
    tm = _heavy_tile(rows)
    hid = lambda: pl.BlockSpec((tm, FFN), lambda i: (i, 0))

    def body(u2_ref, w_hbm, silu_ref, dsilu_ref, act_ref, w_sc, w_sem):
        @pl.when(pl.program_id(0) == 0)
        def _():
            cp = pltpu.make_async_copy(w_hbm, w_sc, w_sem)
            cp.start()
            cp.wait()

        u2 = u2_ref[...]
        for a, b in _FFN_SUBS:
            g = _dot(u2, w_sc[:, a:b])
            up = _dot(u2, w_sc[:, FFN + a:FFN + b])
            sg = _sigmoid(g)
            silu = g * sg
            silu_ref[:, a:b] = silu.astype(BF16)
            dsilu_ref[:, a:b] = (up * (sg * (1.0 + g * (1.0 - sg)))).astype(BF16)
            act_ref[:, a:b] = (silu * up).astype(BF16)

    return _hosted_call(
        body,
        name="ffn_fwd_gate",
        grid=(rows // tm,),
        in_specs=[pl.BlockSpec((tm, D), lambda i: (i, 0)), _ANY],
        out_specs=[hid(), hid(), hid()],
        out_shape=[jax.ShapeDtypeStruct((rows, FFN), BF16)] * 3,
        scratch_shapes=[pltpu.VMEM((D, 2 * FFN), BF16), pltpu.SemaphoreType.DMA],
        args=(u2, w_ffn_in),
        riders=riders,
    )


def _ffn_out_loss(act, h1, w_ffn_out, target, final_norm_w):
    rows = act.shape[0]
    tm = _heavy_tile(rows)
    nt = rows // tm

    def body(act_ref, h1_ref, wo_ref, fnw_ref, tgt_hbm, dh2_ref, stats_ref, tgt_sc, tgt_sem):
        i = pl.program_id(0)
        slot = i % 2

        @pl.when(i == 0)
        def _():
            stats_ref[...] = jnp.zeros_like(stats_ref)
            tgt_sc[0, 0:FRONT, :] = jnp.zeros((FRONT, D), F32)
            cp = pltpu.make_async_copy(tgt_hbm.at[pl.ds(0, tm - FRONT)], tgt_sc.at[0, pl.ds(FRONT, tm - FRONT)], tgt_sem.at[0])
            cp.start()
            cp.wait()

        @pl.when(i + 1 < nt)
        def _():
            start = pl.multiple_of((i + 1) * tm - FRONT, FRONT)
            pltpu.make_async_copy(tgt_hbm.at[pl.ds(start, tm)], tgt_sc.at[1 - slot], tgt_sem.at[1 - slot]).start()

        @pl.when(i > 0)
        def _():
            pltpu.make_async_copy(tgt_hbm.at[pl.ds(0, tm)], tgt_sc.at[slot], tgt_sem.at[slot]).wait()

        fnw = fnw_ref[...]
        half = tm // 2
        for r in (0, half):
            sl = pl.ds(r, half)
            h2 = h1_ref[sl, :] + _dot(act_ref[sl, :], wo_ref[...])
            rs = lax.rsqrt(jnp.mean(h2 * h2, axis=-1, keepdims=True) + EPS)
            n = h2 * rs
            valid = i * tm + r + lax.broadcasted_iota(jnp.int32, (half, D), 0) >= FRONT
            diff = jnp.where(valid, n * fnw - tgt_sc[slot, sl, :], 0.0)
            dy = diff * (1.0 / D)
            stats_ref[0:1, :] += (0.5 / D) * jnp.sum(diff * diff, axis=0, keepdims=True)
            stats_ref[1:2, :] += jnp.sum(dy * n, axis=0, keepdims=True)
            dn = dy * fnw
            dh2_ref[sl, :] = rs * (dn - n * jnp.mean(dn * n, axis=-1, keepdims=True))

    return pl.pallas_call(
        body,
        name="ffn_out_loss",
        grid=(nt,),
        in_specs=[
            pl.BlockSpec((tm, FFN), lambda i: (i, 0)),
            pl.BlockSpec((tm, D), lambda i: (i, 0)),
            pl.BlockSpec((FFN, D), lambda i: (0, 0)),
            pl.BlockSpec((1, D), lambda i: (0, 0)),
            _ANY,
        ],
        out_specs=[pl.BlockSpec((tm, D), lambda i: (i, 0)), pl.BlockSpec((8, D), lambda i: (0, 0))],
        out_shape=[jax.ShapeDtypeStruct((rows, D), F32), jax.ShapeDtypeStruct((8, D), F32)],
        scratch_shapes=[pltpu.VMEM((2, tm, D), F32), pltpu.SemaphoreType.DMA((2,))],
        compiler_params=_params(("arbitrary",)),
    )(act, h1, w_ffn_out, final_norm_w, target)


def _ffn_bwd(dh2, silu, dsilu, h1, rstd2, w_ffn_in, w_ffn_out, ffn_norm_w):
    rows = dh2.shape[0]
    tm = _heavy_tile(rows)
    blk = lambda: pl.BlockSpec((tm, FFN), lambda i: (i, 0))

    def gate_body(dh2_ref, silu_ref, dsilu_ref, wo_hbm, dg_ref, dup_ref, dh2b_ref, wo_sc, wo_sem):
        @pl.when(pl.program_id(0) == 0)
        def _():
            cp = pltpu.make_async_copy(wo_hbm, wo_sc, wo_sem)
            cp.start()
            cp.wait()

        dh2b = dh2_ref[...].astype(BF16)
        dh2b_ref[...] = dh2b
        for a, b in _FFN_SUBS:
            dact = _dot_nt(dh2b, wo_sc[a:b, :])
            dup_ref[:, a:b] = (dact * silu_ref[:, a:b].astype(F32)).astype(BF16)
            dg_ref[:, a:b] = (dact * dsilu_ref[:, a:b].astype(F32)).astype(BF16)

    dg, dup, dh2b = pl.pallas_call(
        gate_body,
        name="ffn_bwd_gate",
        grid=(rows // tm,),
        in_specs=[pl.BlockSpec((tm, D), lambda i: (i, 0)), blk(), blk(), _ANY],
        out_specs=[blk(), blk(), pl.BlockSpec((tm, D), lambda i: (i, 0))],
        out_shape=[jax.ShapeDtypeStruct((rows, FFN), BF16)] * 2 + [jax.ShapeDtypeStruct((rows, D), BF16)],
        scratch_shapes=[pltpu.VMEM((FFN, D), BF16), pltpu.SemaphoreType.DMA],
        compiler_params=_params(("arbitrary",)),
    )(dh2, silu, dsilu, w_ffn_out)

    def body(dg_ref, dup_ref, dh2_ref, h1_ref, rstd_ref, w_hbm, nw_ref, dh1_ref, stats_ref, w_sc, w_sem):
        @pl.when(pl.program_id(0) == 0)
        def _():
            stats_ref[...] = jnp.zeros_like(stats_ref)
            cp = pltpu.make_async_copy(w_hbm, w_sc, w_sem)
            cp.start()
            cp.wait()

        half = tm // 2
        for r in (0, half):
            sl = pl.ds(r, half)
            du = _dot_nt(dg_ref[sl, :], w_sc[:, 0:FFN]) + _dot_nt(dup_ref[sl, :], w_sc[:, FFN:2 * FFN])
            rs = rstd_ref[sl, :]
            n = h1_ref[sl, :] * rs
            stats_ref[0:1, :] += jnp.sum(du * n, axis=0, keepdims=True)
            dn = du * nw_ref[...]
            dh1_ref[sl, :] = dh2_ref[sl, :] + rs * (dn - n * jnp.mean(dn * n, axis=-1, keepdims=True))

    row = lambda width: pl.BlockSpec((tm, width), lambda i: (i, 0))
    dh1, stats = pl.pallas_call(
        body,
        name="ffn_bwd_in",
        grid=(rows // tm,),
        in_specs=[row(FFN), row(FFN), row(D), row(D), row(1), _ANY, pl.BlockSpec((1, D), lambda i: (0, 0))],
        out_specs=[row(D), pl.BlockSpec((8, D), lambda i: (0, 0))],
        out_shape=[jax.ShapeDtypeStruct((rows, D), F32), jax.ShapeDtypeStruct((8, D), F32)],
        scratch_shapes=[pltpu.VMEM((D, 2 * FFN), BF16), pltpu.SemaphoreType.DMA],
        compiler_params=_params(("arbitrary",)),
    )(dg, dup, dh2, h1, rstd2, w_ffn_in, ffn_norm_w)
    return dg, dup, dh2b, dh1, stats


def _mix_bwd(dh1, o, proj, h_lru, y_ret, y_lru, w_br, w_bl, w_o, riders=()):
    rows = dh1.shape[0]
    tm = TM_MIX_BWD
    nt = rows // tm

    def body(dh1_ref, o_ref, gates_ref, hl_ref, yret_ref, ylru_ref, wbr_hbm, wbl_hbm, wo_hbm,
             dproj_ref, do_ref, dhl_ref, mixed_ref, aret_ref, alru_ref, dyret_ref, dylru_ref, w_sc, w_sem):
        @pl.when(pl.program_id(0) == 0)
        def _():
            copies = [pltpu.make_async_copy(src, w_sc.at[q], w_sem.at[q]) for q, src in enumerate((wbr_hbm, wbl_hbm, wo_hbm))]
            for cp in copies:
                cp.start()
            for cp in copies:
                cp.wait()

        wbr_ref, wbl_ref, wo_ref = w_sc.at[0], w_sc.at[1], w_sc.at[2]
        gate = lambda j: gates_ref[:, j * D:(j + 1) * D].astype(F32)
        dmixed = _dot_nt(dh1_ref[...].astype(BF16), wo_ref[...])
        y_ret, y_lru = yret_ref[...].astype(F32), ylru_ref[...].astype(F32)
        sa, sb = _sigmoid(gate(2)), _sigmoid(gate(3))
        mixed_ref[...] = (sa * y_ret + sb * y_lru).astype(BF16)
        dga = dmixed * y_ret * sa * (1.0 - sa)
        dgb = dmixed * y_lru * sb * (1.0 - sb)
        dy_ret = (dmixed * sa).astype(BF16)
        dy_lru = (dmixed * sb).astype(BF16)
        dyret_ref[...] = dy_ret
        dylru_ref[...] = dy_lru
        da_ret = _dot_nt(dy_ret, wbr_ref[...])
        da_lru = _dot_nt(dy_lru, wbl_ref[...])

        gret = gate(0)
        sg = _sigmoid(gret)
        silu = gret * sg
        on, rstds = _group_norm(o_ref[...].astype(F32))
        aret_ref[...] = (silu * on).astype(BF16)
        dgret = da_ret * on * (sg * (1.0 + gret * (1.0 - sg)))
        don = da_ret * silu
        for h in range(HEADS):
            sl = slice(h * DH, (h + 1) * DH)
            onh, donh = on[:, sl], don[:, sl]
            do_ref[:, sl] = (rstds[h] * (donh - onh * jnp.mean(donh * onh, axis=-1, keepdims=True))).astype(BF16)

        gl, gl_grad = _gelu_and_grad(gate(1))
        hl = hl_ref[...].astype(F32)
        alru_ref[...] = (gl * hl).astype(BF16)
        dlgate = da_lru * hl * gl_grad
        dhl_ref[...] = (da_lru * gl).astype(BF16)

        for j, val in enumerate((dgret, dlgate, dga, dgb)):
            dproj_ref[:, j * D:(j + 1) * D] = val.astype(BF16)

    tile = lambda col: pl.BlockSpec((tm, D), lambda i: (i, col))
    gates = pl.BlockSpec((tm, 4 * D), lambda i: (i, GATES_COL))
    bf = lambda: jax.ShapeDtypeStruct((rows, D), BF16)
    return _hosted_call(
        body,
        name="mix_bwd",
        grid=(nt,),
        in_specs=[tile(0), tile(0), gates, tile(0), tile(0), tile(0), _ANY, _ANY, _ANY],
        out_specs=[pl.BlockSpec((tm, 4 * D), lambda i: (i, GATES_COL))] + [tile(0)] * 7,
        out_shape=[jax.ShapeDtypeStruct((rows, IN_COLS), BF16)] + [bf() for _ in range(7)],
        scratch_shapes=[pltpu.VMEM((3, D, D), BF16), pltpu.SemaphoreType.DMA((3,))],
        args=(dh1, o, proj, h_lru, y_ret, y_lru, w_br, w_bl, w_o),
        riders=riders,
    )


def _retention_bwd(dproj, proj_bf, do, sprev, intra, intra_t, q_dec, k_dec, c_dec, cos_t, sin_t, riders=()):
    rows = proj_bf.shape[0]
    nc = rows // CHUNK

    def body(dproj_in_ref, q_ref, k_ref, v_ref, do_ref, sprev_ref, m_ref, mt_ref, qd_ref, kd_ref, cd_ref, cos_ref, sin_ref,
             dproj_ref, ds_sc):
        @pl.when(pl.program_id(0) == 0)
        def _():
            ds_sc[...] = jnp.zeros_like(ds_sc)

        cos, sin = cos_ref[...], sin_ref[...]

        def unrotate(dy):
            return dy * cos - pltpu.roll(dy, DH // 2, axis=1) * sin

        for h in range(HEADS):
            sl = slice(h * DH, (h + 1) * DH)
            q, k, v = q_ref[:, sl], k_ref[:, sl], v_ref[:, sl]
            do = do_ref[:, sl]
            dob = do.astype(BF16)
            doq = (do * qd_ref[:, sl]).astype(BF16)
            state_prev = sprev_ref[0, h]
            dstate = ds_sc[h]
            dstate_b = dstate.astype(BF16)
            s_t = (_dot_nt(k, q) * mt_ref[h]).astype(BF16)
            ds_t = (_dot_nt(v, dob) * mt_ref[h]).astype(BF16)
            ds = (_dot_nt(dob, v) * m_ref[h]).astype(BF16)
            kd = kd_ref[:, sl]
            dq = _dot(ds, k) + _dot_nt(doq, state_prev)
            dk = _dot(ds_t, q) + _dot_nt(v, dstate_b) * kd
            k_scaled = (k.astype(F32) * kd).astype(BF16)
            dv = _dot(s_t, dob) + _dot(k_scaled, dstate_b)
            ds_sc[h] = dstate * cd_ref[:, sl] + _dot_tn(q, doq)
            for part, val in enumerate((unrotate(dq), unrotate(dk) * QK_SCALE, dv)):
                dproj_ref[:, part * D + h * DH:part * D + (h + 1) * DH] = val.astype(BF16)

    rev = lambda c: nc - 1 - c
    chunk_spec = lambda col: pl.BlockSpec((CHUNK, D), lambda c: (rev(c), col))
    const2 = lambda shape: pl.BlockSpec(shape, lambda c: (0, 0))
    const3 = pl.BlockSpec((HEADS, CHUNK, CHUNK), lambda c: (0, 0, 0))
    return _hosted_call(
        body,
        name="retention_bwd",
        grid=(nc,),
        in_specs=[
            pl.BlockSpec(memory_space=pl.ANY),
            chunk_spec(0), chunk_spec(1), chunk_spec(2), chunk_spec(0),
            pl.BlockSpec((1, HEADS, DH, DH), lambda c: (rev(c), 0, 0, 0)),
            const3, const3,
            const2((CHUNK, D)), const2((CHUNK, D)), const2((1, D)),
            pl.BlockSpec((CHUNK, DH), lambda c: (rev(c), 0)),
            pl.BlockSpec((CHUNK, DH), lambda c: (rev(c), 0)),
        ],
        out_specs=[pl.BlockSpec((CHUNK, 3 * D), lambda c: (rev(c), 0))],
        out_shape=[jax.ShapeDtypeStruct(dproj.shape, BF16)],
        aliases={0: 0},
        scratch_shapes=[pltpu.VMEM((HEADS, DH, DH), F32)],
        args=(dproj, proj_bf, proj_bf, proj_bf, do, sprev, intra, intra_t, q_dec, k_dec, c_dec, cos_t, sin_t),
        riders=riders,
    )


def _lru_bwd(dproj, proj, saved, dhl, conv_w, wa, wx, lam, riders=()):
    rows = proj.shape[0]
    TM = _lru_tile(rows)
    nt = rows // TM
    per8 = TM // 8

    def body(dproj_in_ref, x_ref, xprev_ref, h_ref, hprev_ref, c_ref, r_ref, i_ref, la_ref, mult_ref, dhl_ref,
             cw_ref, wa_ref, wx_ref, lam_ref,
             dproj_ref, dwa_ref, dwx_ref, stats_ref, anext_sc, dhnext_sc, dcnext_sc, c_sc, b_sc, dh_sc):
        step = pl.program_id(0)
        i = nt - 1 - step

        @pl.when(step == 0)
        def _():
            anext_sc[...] = jnp.zeros_like(anext_sc)
            dhnext_sc[...] = jnp.zeros_like(dhnext_sc)
            dcnext_sc[...] = jnp.zeros_like(dcnext_sc)
            dwa_ref[...] = jnp.zeros_like(dwa_ref)
            dwx_ref[...] = jnp.zeros_like(dwx_ref)
            stats_ref[...] = jnp.zeros_like(stats_ref)

        first = i == 0
        x = x_ref[...].astype(F32)
        prev8 = jnp.where(first, 0.0, xprev_ref[8:16, :].astype(F32))
        c_b = c_ref[...]
        c, r, gate_i, mult = (ref[...].astype(F32) for ref in (c_ref, r_ref, i_ref, mult_ref))
        a = jnp.exp(la_ref[...].astype(F32))
        sp = _softplus(-lam_ref[...])
        dh_sc[...] = dhl_ref[...].astype(F32)

        b_sc[...] = pltpu.roll(a, TM - 1, axis=0)
        b_sc[TM - 1:TM, :] = anext_sc[0:1, :]
        anext_sc[0:1, :] = a[0:1, :]
        row8 = lax.broadcasted_iota(jnp.int32, (8, D), 0)

        def group(gi, dhnext):
            r0 = pl.multiple_of((per8 - 1 - gi) * 8, 8)
            bb = b_sc[pl.ds(r0, 8), :]
            uu = dh_sc[pl.ds(r0, 8), :]
            for d in (1, 2, 4):
                b_sh = jnp.where(row8 < 8 - d, pltpu.roll(bb, 8 - d, axis=0), 1.0)
                u_sh = jnp.where(row8 < 8 - d, pltpu.roll(uu, 8 - d, axis=0), 0.0)
                uu = uu + bb * u_sh
                bb = bb * b_sh
            dhb = bb * dhnext + uu
            dh_sc[pl.ds(r0, 8), :] = dhb
            return dhb[0:1, :]

        dhfirst = lax.fori_loop(0, per8, group, dhnext_sc[0:1, :])
        dhnext_sc[0:1, :] = dhfirst
        dh = dh_sc[...]

        h = h_ref[...].astype(F32)
        hprev8 = jnp.where(first, 0.0, hprev_ref[8:16, :].astype(F32))
        h_dn, h_head = _shift_down(h, hprev8, 1)
        c_sc[...] = h_dn
        c_sc[0:8, :] = h_head
        h_before = c_sc[...]

        valid = _row_ids(i, TM, (TM, D)) >= PAD_ROWS
        da = dh * h_before
        du = jnp.where(valid, dh, 0.0)
        dmult = du * gate_i * c
        dgate_i = du * mult * c
        dc = du * mult * gate_i
        dla = da * a - dmult * (a * a) / mult
        dla = jnp.where(valid, dla, 0.0)
        dr = dla * ((-LRU_C) * sp)
        dzr = dr * r * (1.0 - r)
        dzi = dgate_i * gate_i * (1.0 - gate_i)
        stats_ref[1:2, :] += jnp.sum(dzr, axis=0, keepdims=True)
        stats_ref[2:3, :] += jnp.sum(dzi, axis=0, keepdims=True)
        stats_ref[3:4, :] += jnp.sum(dla * ((-LRU_C) * r), axis=0, keepdims=True)
        dc_gate = []
        for g in range(LRU_BLOCKS):
            sl = slice(g * LRU_BLOCK, (g + 1) * LRU_BLOCK)
            cg = c_b[:, sl]
            dzr_g = dzr[:, sl].astype(BF16)
            dzi_g = dzi[:, sl].astype(BF16)
            dc_gate.append(_dot_nt(dzr_g, wa_ref[g]) + _dot_nt(dzi_g, wx_ref[g]))
            dwa_ref[g] += _dot_tn(cg, dzr_g)
            dwx_ref[g] += _dot_tn(cg, dzi_g)
        dc = dc + jnp.concatenate(dc_gate, axis=1)

        cw = cw_ref[...]
        stats_ref[0:1, :] += jnp.sum(dc, axis=0, keepdims=True)
        stats_ref[7:8, :] += jnp.sum(dc * x, axis=0, keepdims=True)
        dx = cw[3:4] * dc
        tail_src = jnp.concatenate([dc[TM - 8:TM], dcnext_sc[...]], axis=0)
        dx_tail = cw[3:4] * dc[TM - 8:TM]
        for d in (1, 2, 3):
            dx = dx + cw[3 - d:4 - d] * pltpu.roll(dc, TM - d, axis=0)
            dx_tail = dx_tail + cw[3 - d:4 - d] * pltpu.roll(tail_src, 16 - d, axis=0)[0:8]
            rolled, hd = _shift_down(x, prev8, d)
            b_sc[...] = rolled
            b_sc[0:8, :] = hd
            stats_ref[7 - d:8 - d, :] += jnp.sum(dc * b_sc[...], axis=0, keepdims=True)
        dcnext_sc[...] = dc[0:8]
        dproj_ref[...] = dx.astype(BF16)
        dproj_ref[TM - 8:TM, :] = dx_tail.astype(BF16)

    rev = lambda s: nt - 1 - s
    vec = pl.BlockSpec((1, D), lambda s: (0, 0))
    wspec = pl.BlockSpec((LRU_BLOCKS, LRU_BLOCK, LRU_BLOCK), lambda s: (0, 0, 0))
    prev16 = lambda col: pl.BlockSpec((BF16_ROWS, D), lambda s: (jnp.maximum(rev(s) * (TM // BF16_ROWS) - 1, 0), col))
    tile = lambda: pl.BlockSpec((TM, D), lambda s: (rev(s), 0))
    h_lru, c_sv, r_sv, i_sv, la_sv, mult_sv = saved
    return _hosted_call(
        body,
        name="lru_bwd",
        grid=(nt,),
        in_specs=[
            pl.BlockSpec(memory_space=pl.ANY),
            pl.BlockSpec((TM, D), lambda s: (rev(s), LRU_IN_COL)), prev16(LRU_IN_COL),
            tile(), prev16(0),
            tile(), tile(), tile(), tile(), tile(), tile(),
            pl.BlockSpec((4, D), lambda s: (0, 0)),
            wspec, wspec, vec,
        ],
        out_specs=[
            pl.BlockSpec((TM, D), lambda s: (rev(s), LRU_IN_COL)),
            wspec, wspec,
            pl.BlockSpec((8, D), lambda s: (0, 0)),
        ],
        out_shape=[
            jax.ShapeDtypeStruct(dproj.shape, BF16),
            jax.ShapeDtypeStruct((LRU_BLOCKS, LRU_BLOCK, LRU_BLOCK), F32),
            jax.ShapeDtypeStruct((LRU_BLOCKS, LRU_BLOCK, LRU_BLOCK), F32),
            jax.ShapeDtypeStruct((8, D), F32),
        ],
        aliases={0: 0},
        scratch_shapes=[
            pltpu.VMEM((8, D), F32), pltpu.VMEM((8, D), F32), pltpu.VMEM((8, D), F32),
            pltpu.VMEM((TM, D), F32), pltpu.VMEM((TM, D), F32), pltpu.VMEM((TM, D), F32),
        ],
        args=(dproj, proj, proj, h_lru, h_lru, c_sv, r_sv, i_sv, la_sv, mult_sv, dhl, conv_w, wa, wx, lam),
        riders=riders,
    )


def _in_proj_bwd(dproj, w_in, part, prev=None, riders=()):
    rows = dproj.shape[0]
    tm = _heavy_tile(rows)
    nt = rows // tm
    split = max(1, nt // 4 + 1) if nt > 1 else 1
    first = 0 if part == 0 else split
    count = split if part == 0 else nt - split

    def body(*refs):
        dproj_ref, w_hbm, du_ref, w_sc, w_sem = refs[-5:]

        @pl.when(pl.program_id(0) == 0)
        def _():
            moves = [((0, 3), 0), ((4, 1), 3), ((3, 1), 4), ((5, 3), 5)]
            copies = [pltpu.make_async_copy(w_hbm.at[:, pl.ds(src * D, n * D)], w_sc.at[:, pl.ds(dst * D, n * D)], w_sem.at[q])
                      for q, ((src, n), dst) in enumerate(moves)]
            for cp in copies:
                cp.start()
            for cp in copies:
                cp.wait()

        for a in range(0, D, D // 2):
            du_ref[:, a:a + D // 2] = _dot_nt(dproj_ref[...], w_sc[a:a + D // 2, :])

    in_specs = [pl.BlockSpec((tm, IN_COLS), lambda i: (first + i, 0)), _ANY]
    args = (dproj, w_in)
    if prev is not None:
        in_specs = [_ANY] + in_specs
        args = (prev,) + args
    return _hosted_call(
        body,
        name="in_proj_bwd_%d" % part,
        grid=(count,),
        in_specs=in_specs,
        out_specs=[pl.BlockSpec((tm, D), lambda i: (first + i, 0))],
        out_shape=[jax.ShapeDtypeStruct((rows, D), F32)],
        scratch_shapes=[pltpu.VMEM((D, IN_COLS), BF16), pltpu.SemaphoreType.DMA((4,))],
        aliases={0: 0} if prev is not None else None,
        args=args,
        riders=riders,
    )


def _norm1_bwd(du, dh1, head, x2d, rstd1, norm_w, riders=()):
    rows = du.shape[0]

    def body(du_ref, dh1_ref, head_ref, x_ref, rstd_ref, nw_ref, gx_ref, ghead_ref, stats_ref):
        i = pl.program_id(0)

        @pl.when(i == 0)
        def _():
            stats_ref[...] = jnp.zeros_like(stats_ref)

        def finish(h0, out_ref):
            du = du_ref[...]
            rs = rstd_ref[...]
            n = h0 * rs
            stats_ref[0:1, :] += jnp.sum(du * n, axis=0, keepdims=True)
            dn = du * nw_ref[...]
            out_ref[...] = dh1_ref[...] + rs * (dn - n * jnp.mean(dn * n, axis=-1, keepdims=True))

        @pl.when(i == 0)
        def _():
            finish(head_ref[...], ghead_ref)

        @pl.when(i > 0)
        def _():
            finish(x_ref[...], gx_ref)

    tile = pl.BlockSpec((TM, D), lambda i: (i, 0))
    return _hosted_call(
        body,
        name="norm1_bwd",
        grid=(rows // TM,),
        in_specs=[
            tile, tile,
            pl.BlockSpec((FRONT, D), lambda i: (0, 0)),
            pl.BlockSpec((TM, D), lambda i: (jnp.maximum(i - 1, 0), 0)),
            pl.BlockSpec((TM, 1), lambda i: (i, 0)),
            pl.BlockSpec((1, D), lambda i: (0, 0)),
        ],
        out_specs=[
            pl.BlockSpec((TM, D), lambda i: (jnp.maximum(i - 1, 0), 0)),
            pl.BlockSpec((FRONT, D), lambda i: (0, 0)),
            pl.BlockSpec((8, D), lambda i: (0, 0)),
        ],
        out_shape=[
            jax.ShapeDtypeStruct(x2d.shape, F32),
            jax.ShapeDtypeStruct((FRONT, D), F32),
            jax.ShapeDtypeStruct((8, D), F32),
        ],
        scratch_shapes=[],
        args=(du, dh1, head, x2d, rstd1, norm_w),
        riders=riders,
    )


def _matmul_tn(name, x, dy, out_cols, col0=0, prev=None, k_block=None, n_block=None, riders=(), col_map=None):
    col_map = col_map or (lambda n: n)
    rows, kdim = x.shape
    ndim = dy.shape[1]
    kb = k_block or kdim
    nb = n_block or ndim
    step_bytes = lambda t: 2 * t * (kb * x.dtype.itemsize + nb * dy.dtype.itemsize) + 2 * kb * nb * 4
    tr = next(t for t in (2816, 1408, TM_HEAVY, TM) if rows % t == 0 and (t == TM or step_bytes(t) <= TN_VMEM_BUDGET))
    nr, nk, nn = rows // tr, kdim // kb, ndim // nb
    cb0 = col0 // nb

    def body(*refs):
        x_ref, dy_ref, out_ref = refs[-3], refs[-2], refs[-1]
        cut = (nb // 2) // LANES * LANES
        halves = ((0, cut), (cut, nb)) if cut else ((0, nb),)

        @pl.when(pl.program_id(2) == 0)
        def _():
            for a, b in halves:
                out_ref[:, a:b] = _dot_tn(x_ref[...].astype(BF16), dy_ref[:, a:b].astype(BF16))

        @pl.when(pl.program_id(2) > 0)
        def _():
            for a, b in halves:
                out_ref[:, a:b] += _dot_tn(x_ref[...].astype(BF16), dy_ref[:, a:b].astype(BF16))

    in_specs = [
        pl.BlockSpec((tr, kb), lambda n, k, r: (r, k)),
        pl.BlockSpec((tr, nb), lambda n, k, r: (r, n)),
    ]
    args = [x, dy]
    aliases = {}
    if prev is not None:
        in_specs = [pl.BlockSpec(memory_space=pl.ANY)] + in_specs
        args = [prev] + args
        aliases = {0: 0}
    (out,), rider_outs = _hosted_call(
        body,
        name=name,
        grid=(nn, nk, nr),
        in_specs=in_specs,
        out_specs=[pl.BlockSpec((kb, nb), lambda n, k, r: (k, cb0 + col_map(n)))],
        out_shape=[jax.ShapeDtypeStruct((kdim, out_cols), F32)],
        scratch_shapes=[],
        aliases=aliases,
        args=args,
        riders=riders,
    )
    return (out, rider_outs) if riders else out


def _local_step(x2d, target, w, plan):
    rows = FRONT + x2d.shape[0]
    head = jnp.concatenate([jnp.zeros((PAD_ROWS, D), F32), w["meta_tokens"]], axis=0)
    cos_t, sin_t = _rope_tables(rows)
    intra, intra_t, q_dec, k_dec, c_dec = _decay_tables()
    grads = {}

    def hosted(host, fn, *args, **kwargs):
        outs, rider_outs = fn(*args, riders=plan.riders(host, w, grads), **kwargs)
        plan.after(host, rider_outs, w, grads)
        return outs

    (u1, rstd1), _ = _norm1(head, x2d, w["mix_norm_w"])
    proj, w["w_in"] = hosted("in_proj", _in_proj, u1, w["w_in_shard"], w["route"], cos_t, sin_t)
    o, sprev = hosted("retention_fwd", _retention_fwd, proj, intra, q_dec, k_dec, c_dec)
    lru_args = (w["conv_w"], w["conv_b"], w["lru_wa"], w["lru_wx"], w["lru_ba"], w["lru_bx"], w["lru_lambda"])
    lru_saved = hosted("lru_fwd", _lru_fwd, proj, *lru_args)
    h_lru = lru_saved[0]
    y_ret, y_lru, h1, u2, rstd2 = hosted("mix_fwd", _mix_fwd, head, x2d, o, proj, h_lru, w["w_branch_ret"],
                                         w["w_branch_lru"], w["w_out"], w["ffn_norm_w"])
    silu, dsilu, act = hosted("ffn_fwd_gate", _ffn_fwd_gate, u2, w["w_ffn_in"])
    dh2, stats_loss = _ffn_out_loss(act, h1, w["w_ffn_out"], target, w["final_norm_w"])

    dg, dup, dh2b, dh1, stats_ffn = _ffn_bwd(dh2, silu, dsilu, h1, rstd2, w["w_ffn_in"], w["w_ffn_out"], w["ffn_norm_w"])
    grads["w_ffn_in"] = _matmul_tn("dw_ffn_up", u2, dup, 2 * FFN, col0=FFN, n_block=FFN_HALF,
                                   prev=_matmul_tn("dw_ffn_gate", u2, dg, 2 * FFN, n_block=FFN_HALF))
    grads["w_ffn_out"] = _matmul_tn("dw_ffn_out", act, dh2b, D, k_block=FFN_HALF)
    (dproj, do, dhl, mixed, a_ret, a_lru, dy_ret, dy_lru) = hosted(
        "mix_bwd", _mix_bwd, dh1, o, proj, h_lru, y_ret, y_lru, w["w_branch_ret"], w["w_branch_lru"], w["w_out"])
    grads["w_out"] = _matmul_tn("dw_out", mixed, dh1, D)
    grads["w_branch_ret"] = _matmul_tn("dw_branch_ret", a_ret, dy_ret, D)
    grads["w_branch_lru"] = _matmul_tn("dw_branch_lru", a_lru, dy_lru, D)
    (dproj,) = hosted("retention_bwd", _retention_bwd, dproj, proj, do, sprev, intra, intra_t, q_dec, k_dec, c_dec,
                      cos_t, sin_t)
    dproj, grads["lru_wa"], grads["lru_wx"], stats_lru = hosted(
        "lru_bwd", _lru_bwd, dproj, proj, lru_saved, dhl, w["conv_w"], w["lru_wa"], w["lru_wx"], w["lru_lambda"])
    grads["w_in"], rider_outs = _matmul_tn("dw_in", u1, dproj, IN_COLS, n_block=D, col_map=_swap_3_4,
                                           riders=plan.riders("dw_in", w, grads))
    plan.after("dw_in", rider_outs, w, grads)
    (du1,) = hosted("in_proj_bwd_0", _in_proj_bwd, dproj, w["w_in"], 0)
    (du1,) = hosted("in_proj_bwd_1", _in_proj_bwd, dproj, w["w_in"], 1, du1)
    (grad_x, grad_head, stats_in), _ = _norm1_bwd(du1, dh1, head, x2d, rstd1, w["mix_norm_w"])
    return grad_x, grad_head, grads, [stats_loss, stats_ffn, stats_in, stats_lru]


BIG_PIECES = {
    "w_in": ("col", (D, 2 * D)),
    "w_ffn_in": ("col", (D, FFN_HALF)),
    "w_ffn_out": ("row", (FFN // 4, D)),
    "w_branch_ret": ("row", (D // 4, D)),
    "w_branch_lru": ("row", (D // 4, D)),
    "w_out": ("row", (D // 4, D)),
    "lru_wa": ("lru", (LRU_BLOCKS, LRU_BLOCK // 4, LRU_BLOCK)),
    "lru_wx": ("lru", (LRU_BLOCKS, LRU_BLOCK // 4, LRU_BLOCK)),
}
SMALL_PIECES = {"meta_tokens": ("col", (N_META, D // 4)), "conv_w": ("col", (4, D // 4))}


def _full_shape(kind, shard):
    if kind == "col":
        return (shard[0], 4 * shard[1])
    if kind == "row":
        return (4 * shard[0], shard[1])
    return (shard[0], 4 * shard[1], shard[2])


def _half_shape(kind, shard):
    return (shard[0] // 2,) + tuple(shard[1:])


def _aligned(start, multiple):
    return start if isinstance(start, int) else pl.multiple_of(start, multiple)


def _lead(h, size):
    if h is None:
        return pl.ds(0, size)
    return pl.ds(_aligned(h * (size // 2), size // 2), size // 2)


def _full_region(ref, kind, shard, s, h):
    if kind == "col":
        return ref.at[_lead(h, shard[0]), pl.ds(_aligned(s * shard[1], shard[1]), shard[1])]
    if kind == "row":
        size = shard[0] if h is None else shard[0] // 2
        start = s * shard[0] + (0 if h is None else h * (shard[0] // 2))
        return ref.at[pl.ds(_aligned(start, BF16_ROWS), size), :]
    return ref.at[_lead(h, shard[0]), pl.ds(_aligned(s * shard[1], shard[1]), shard[1]), :]


def _shard_region(ref, shard, h):
    return ref.at[_lead(h, shard[0])]


def _place():
    x, y, c = lax.axis_index("x"), lax.axis_index("y"), lax.axis_index("c")
    return x, y, c, 2 * x + y


def _other_chip(s, c, k):
    s2 = jnp.bitwise_xor(s, k)
    return s2, (s2 // 2, s2 % 2, c)


def _remote(src, dst, send_sem, recv_sem, dev):
    return pltpu.make_async_remote_copy(src_ref=src, dst_ref=dst, send_sem=send_sem, recv_sem=recv_sem,
                                        device_id=dev, device_id_type=MESH)


_ANY = pl.BlockSpec(memory_space=pl.ANY)


class _Rider:
    def __init__(self, ins, out_shapes, sem_shapes, build, aliased=False):
        self.ins, self.out_shapes, self.sem_shapes, self.build, self.aliased = ins, out_shapes, sem_shapes, build, aliased


def _hosted_call(body, *, name, grid, in_specs, out_specs, out_shape, scratch_shapes, args, riders=(), aliases=None,
                 prefetch=None, riders_after_body=False):
    n_in, n_out, n_sc = len(in_specs), len(out_shape), len(scratch_shapes)
    r_in = [a for r in riders for a in r.ins]
    r_out = [s for r in riders for s in r.out_shapes]
    r_sem = [s for r in riders for s in r.sem_shapes]
    lead = () if prefetch is None else (prefetch,)
    assert prefetch is None or not (aliases or any(r.aliased for r in riders))

    def full_body(*refs):
        head, refs = refs[:len(lead)], refs[len(lead):]
        ins, rin = refs[:n_in], refs[n_in:n_in + len(r_in)]
        o0 = n_in + len(r_in)
        outs, rout = refs[o0:o0 + n_out], refs[o0 + n_out:o0 + n_out + len(r_out)]
        s0 = o0 + n_out + len(r_out)
        scratch, rsem = refs[s0:s0 + n_sc], refs[s0 + n_sc:]
        starts, waits = [], []
        pi = po = ps = 0
        for r in riders:
            st, wt = r.build(rin[pi:pi + len(r.ins)], rout[po:po + len(r.out_shapes)], rsem[ps:ps + len(r.sem_shapes)])
            starts += st
            waits += wt
            pi, po, ps = pi + len(r.ins), po + len(r.out_shapes), ps + len(r.sem_shapes)
        first = functools.reduce(jnp.logical_and, [pl.program_id(d) == 0 for d in range(len(grid))])
        last = functools.reduce(jnp.logical_and, [pl.program_id(d) == grid[d] - 1 for d in range(len(grid))])

        def start_riders():
            @pl.when(first)
            def _():
                for cp in starts:
                    cp.start()

        if riders and not riders_after_body:
            start_riders()
        body(*head, *ins, *outs, *scratch)
        if riders and riders_after_body:
            start_riders()
        if riders:
            @pl.when(last)
            def _():
                for wait in waits:
                    wait()

    io_aliases = dict(aliases or {})
    pi = po = 0
    for r in riders:
        if r.aliased:
            for q in range(len(r.ins)):
                io_aliases[n_in + pi + q] = n_out + po + q
        pi, po = pi + len(r.ins), po + len(r.out_shapes)
    specs = dict(
        grid=grid,
        in_specs=list(in_specs) + [_ANY] * len(r_in),
        out_specs=list(out_specs) + [_ANY] * len(r_out),
        scratch_shapes=list(scratch_shapes) + r_sem,
    )
    if prefetch is not None:
        specs = dict(grid_spec=pltpu.PrefetchScalarGridSpec(num_scalar_prefetch=1, **specs))
    res = pl.pallas_call(
        full_body,
        name=name,
        out_shape=list(out_shape) + r_out,
        input_output_aliases=io_aliases,
        compiler_params=_params(("arbitrary",) * len(grid)),
        **specs,
    )(*lead, *args, *r_in)
    rider_outs, po = [], n_out
    for r in riders:
        rider_outs.append(list(res[po:po + len(r.out_shapes)]))
        po += len(r.out_shapes)
    return list(res[:n_out]), rider_outs


def _run_riders(name, riders):
    r_in = [a for r in riders for a in r.ins]
    r_out = [s for r in riders for s in r.out_shapes]
    r_sem = [s for r in riders for s in r.sem_shapes]

    def body(*refs):
        rin, rout, rsem = refs[:len(r_in)], refs[len(r_in):len(r_in) + len(r_out)], refs[len(r_in) + len(r_out):]
        pi = po = ps = 0
        all_waits = []
        for r in riders:
            starts, waits = r.build(rin[pi:pi + len(r.ins)], rout[po:po + len(r.out_shapes)], rsem[ps:ps + len(r.sem_shapes)])
            for cp in starts:
                cp.start()
            all_waits += waits
            pi, po, ps = pi + len(r.ins), po + len(r.out_shapes), ps + len(r.sem_shapes)
        for wait in all_waits:
            wait()

    io_aliases = {}
    pi = po = 0
    for r in riders:
        if r.aliased:
            for q in range(len(r.ins)):
                io_aliases[pi + q] = po + q
        pi, po = pi + len(r.ins), po + len(r.out_shapes)
    res = pl.pallas_call(
        body,
        name=name,
        in_specs=[_ANY] * len(r_in),
        out_specs=[_ANY] * len(r_out),
        out_shape=r_out,
        scratch_shapes=r_sem,
        input_output_aliases=io_aliases,
    )(*r_in)
    outs, po = [], 0
    for r in riders:
        outs.append(list(res[po:po + len(r.out_shapes)]))
        po += len(r.out_shapes)
    return outs


def _piece(name):
    if name in BIG_PIECES:
        return (name, *BIG_PIECES[name], True)
    return (name, *SMALL_PIECES[name], False)


def _gather_rider(shards, names):
    pieces = [_piece(n) for n in names]
    n = len(pieces)

    def build(ins, outs, sems):
        local_sem, ici_send, ici_recv = sems
        _, _, c, s = _place()
        starts, waits = [], []
        for p, (_, kind, shard, split) in enumerate(pieces):
            cp = pltpu.make_async_copy(ins[p], _full_region(outs[p], kind, shard, s, None), local_sem.at[p])
            starts.append(cp)
            waits.append(cp.wait)
            h = c if split else None
            for k in (1, 2, 3):
                s2, dev = _other_chip(s, c, k)
                cp = _remote(_shard_region(ins[p], shard, h), _full_region(outs[p], kind, shard, s, h),
                             ici_send.at[p, k - 1], ici_recv.at[p, k - 1], dev)
                starts.append(cp)
                waits.append(cp.wait_send)
                region = _full_region(outs[p], kind, shard, s2, h)
                waits.append(_remote(region, region, ici_send.at[p, k - 1], ici_recv.at[p, k - 1], dev).wait_recv)
        return starts, waits

    return _Rider(
        [shards[name] for name in names],
        [jax.ShapeDtypeStruct(_full_shape(kind, shard), shards[name].dtype) for name, kind, shard, _ in pieces],
        [pltpu.SemaphoreType.DMA((n,)), pltpu.SemaphoreType.DMA((n, 3)), pltpu.SemaphoreType.DMA((n, 3))],
        build)


def _forward_rider(gathered, names):
    pieces = [_piece(n) for n in names]
    n = len(pieces)

    def build(ins, outs, sems):
        fwd_send, fwd_recv = sems
        x, y, c, s = _place()
        sibling = (x, y, 1 - c)
        starts, waits = [], []
        for p, (_, kind, shard, _) in enumerate(pieces):
            for k in (1, 2, 3):
                s2, _ = _other_chip(s, c, k)
                mine = _full_region(outs[p], kind, shard, s2, c)
                theirs = _full_region(outs[p], kind, shard, s2, 1 - c)
                cp = _remote(mine, mine, fwd_send.at[p, k - 1], fwd_recv.at[p, k - 1], sibling)
                starts.append(cp)
                waits.append(cp.wait_send)
                waits.append(_remote(theirs, theirs, fwd_send.at[p, k - 1], fwd_recv.at[p, k - 1], sibling).wait_recv)
        return starts, waits

    return _Rider(
        [gathered[name] for name in names],
        [jax.ShapeDtypeStruct(gathered[name].shape, gathered[name].dtype) for name in names],
        [pltpu.SemaphoreType.DMA((n, 3)), pltpu.SemaphoreType.DMA((n, 3))],
        build, aliased=True)


def _pair_exchange_rider(grads, names):
    n = len(names)

    def build(ins, outs, sems):
        send_sem, recv_sem = sems
        x, y, c, _ = _place()
        sibling = (x, y, 1 - c)
        starts, waits = [], []
        for p, name in enumerate(names):
            kind, shard = BIG_PIECES[name]
            for s2 in range(4):
                cp = _remote(_full_region(ins[p], kind, shard, s2, 1 - c), outs[p].at[s2], send_sem.at[p, s2],
                             recv_sem.at[p, s2], sibling)
                starts.append(cp)
                waits.append(cp.wait_send)
                waits.append(_remote(outs[p].at[s2], outs[p].at[s2], send_sem.at[p, s2], recv_sem.at[p, s2], sibling).wait_recv)
        return starts, waits

    return _Rider(
        [grads[name] for name in names],
        [jax.ShapeDtypeStruct((4,) + _half_shape(*BIG_PIECES[name]), F32) for name in names],
        [pltpu.SemaphoreType.DMA((n, 4))] * 2,
        build)


def _half_specs(kind, shard):
    half = shard[0] // 2
    if kind == "col":
        full = pl.BlockSpec((half, shard[1]), lambda j, pr: (pr[1], j))
        buf = pl.BlockSpec((None, half, shard[1]), lambda j, pr: (j, 0, 0))
    elif kind == "row":
        full = pl.BlockSpec((half, shard[1]), lambda j, pr: (2 * j + pr[1], 0))
        buf = pl.BlockSpec((None, half, shard[1]), lambda j, pr: (j, 0, 0))
    else:
        full = pl.BlockSpec((half, shard[1], shard[2]), lambda j, pr: (pr[1], j, 0))
        buf = pl.BlockSpec((None, half, shard[1], shard[2]), lambda j, pr: (j, 0, 0, 0))
    return full, buf


def _pair_sum(name, grad, recv, place):
    kind, shard = BIG_PIECES[name]
    full, buf = _half_specs(kind, shard)

    def body(pr, g_ref, r_ref, o_ref):
        o_ref[...] = (g_ref[...] + r_ref[...]).astype(BF16)

    return pl.pallas_call(
        body,
        name="pair_sum_" + name,
        grid_spec=pltpu.PrefetchScalarGridSpec(num_scalar_prefetch=1, grid=(4,), in_specs=[full, buf], out_specs=buf),
        out_shape=jax.ShapeDtypeStruct((4,) + _half_shape(kind, shard), BF16),
        compiler_params=_params(("arbitrary",)),
    )(place, grad, recv)


def _chip_exchange_rider(sums, names):
    n = len(names)

    def build(ins, outs, sems):
        send_sem, recv_sem = sems
        _, _, c, s = _place()
        starts, waits = [], []
        for p in range(n):
            for k in (1, 2, 3):
                s2, dev = _other_chip(s, c, k)
                cp = _remote(ins[p].at[s2], outs[p].at[k - 1], send_sem.at[p, k - 1], recv_sem.at[p, k - 1], dev)
                starts.append(cp)
                waits.append(cp.wait_send)
                waits.append(_remote(outs[p].at[k - 1], outs[p].at[k - 1], send_sem.at[p, k - 1], recv_sem.at[p, k - 1],
                                     dev).wait_recv)
        return starts, waits

    return _Rider(
        [sums[name] for name in names],
        [jax.ShapeDtypeStruct((3,) + _half_shape(*BIG_PIECES[name]), BF16) for name in names],
        [pltpu.SemaphoreType.DMA((n, 3))] * 2,
        build)


def _chip_sum(name, grad, recv_pair, recv_chip, place):
    kind, shard = BIG_PIECES[name]
    half = shard[0] // 2
    tail = tuple(shard[1:])
    zeros = (0,) * len(tail)
    nt = 4 if kind != "lru" and half % (4 * BF16_ROWS) == 0 else 1
    rows = half // nt
    if kind == "col":
        full = pl.BlockSpec((rows,) + tail, lambda j, pr: (pr[1] * nt + j, pr[0]))
    elif kind == "row":
        full = pl.BlockSpec((rows,) + tail, lambda j, pr: ((2 * pr[0] + pr[1]) * nt + j, 0))
    else:
        full = pl.BlockSpec((rows,) + tail, lambda j, pr: (pr[1], pr[0], 0))
    pair = pl.BlockSpec((None, rows) + tail, lambda j, pr: (pr[0], j) + zeros)
    chip = pl.BlockSpec((3, rows) + tail, lambda j, pr: (0, j) + zeros)
    out = pl.BlockSpec((rows,) + tail, lambda j, pr: (pr[1] * nt + j,) + zeros)

    def body(pr, g_ref, rp_ref, rc_ref, o_ref):
        total = g_ref[...] + rp_ref[...]
        for k in range(3):
            total = total + rc_ref[k].astype(F32)
        o_ref[...] = total

    return pl.pallas_call(
        body,
        name="chip_sum_" + name,
        grid_spec=pltpu.PrefetchScalarGridSpec(num_scalar_prefetch=1, grid=(nt,), in_specs=[full, pair, chip], out_specs=out),
        out_shape=jax.ShapeDtypeStruct(shard, F32),
        compiler_params=_params(("arbitrary",)),
    )(place, grad, recv_pair, recv_chip)


def _sibling_exchange_rider(halves, names):
    n = len(names)

    def build(ins, outs, sems):
        send_sem, recv_sem = sems
        x, y, c, _ = _place()
        sibling = (x, y, 1 - c)
        starts, waits = [], []
        for p, name in enumerate(names):
            shard = BIG_PIECES[name][1]
            mine = _shard_region(outs[p], shard, c)
            theirs = _shard_region(outs[p], shard, 1 - c)
            cp = _remote(mine, mine, send_sem.at[p], recv_sem.at[p], sibling)
            starts.append(cp)
            waits.append(cp.wait_send)
            waits.append(_remote(theirs, theirs, send_sem.at[p], recv_sem.at[p], sibling).wait_recv)
        return starts, waits

    return _Rider(
        [halves[name] for name in names],
        [jax.ShapeDtypeStruct(BIG_PIECES[name][1], F32) for name in names],
        [pltpu.SemaphoreType.DMA((n,))] * 2,
        build, aliased=True)


FIRST_WEIGHTS = ["meta_tokens", "conv_w"]
WEIGHT_GROUPS = {
    "lru": ["lru_wa", "lru_wx"],
    "branch": ["w_branch_ret", "w_branch_lru", "w_out"],
    "ffn_in": ["w_ffn_in"],
    "ffn_out": ["w_ffn_out"],
}
WEIGHT_SCHEDULE = {
    "in_proj": [("gather", "lru")],
    "retention_fwd": [("forward", "lru"), ("gather", "branch")],
    "lru_fwd": [("forward", "branch"), ("gather", "ffn_in")],
    "mix_fwd": [("forward", "ffn_in"), ("gather", "ffn_out")],
    "ffn_fwd_gate": [("forward", "ffn_out")],
}
GRAD_GROUPS = {
    "ffn_in": ["w_ffn_in"],
    "ffn_out": ["w_ffn_out"],
    "mixer": ["w_out", "w_branch_ret", "w_branch_lru", "lru_wa", "lru_wx"],
    "in": ["w_in"],
}
GRAD_SCHEDULE = {
    "mix_bwd": [("pair", "ffn_in"), ("pair", "ffn_out")],
    "retention_bwd": [("chip", "ffn_out")],
    "lru_bwd": [("chip", "ffn_in"), ("sibling", "ffn_out")],
    "dw_in": [("sibling", "ffn_in"), ("pair", "mixer")],
    "in_proj_bwd_0": [("chip", "mixer"), ("pair", "in")],
    "in_proj_bwd_1": [("sibling", "mixer"), ("chip", "in")],
}


class _CommPlan:
    def __init__(self, shards, place):
        self.shards, self.place = shards, place
        self.late = {}
        self.recv_pair, self.sums, self.recv_chip, self.halves, self.final = {}, {}, {}, {}, {}

    def _grad_rider(self, stage, group, grads):
        names = GRAD_GROUPS[group]
        if stage == "pair":
            return _pair_exchange_rider(grads, names)
        if stage == "chip":
            return _chip_exchange_rider(self.sums, names)
        return _sibling_exchange_rider(self.halves, names)

    def _grad_after(self, stage, group, outs, grads):
        names = GRAD_GROUPS[group]
        if stage == "pair":
            for n, o in zip(names, outs):
                self.recv_pair[n] = o
                self.sums[n] = _pair_sum(n, grads[n], o, self.place)
        elif stage == "chip":
            for n, o in zip(names, outs):
                self.halves[n] = _chip_sum(n, grads[n], self.recv_pair[n], o, self.place)
        else:
            self.final.update(zip(names, outs))

    def riders(self, host, w, grads):
        if host in WEIGHT_SCHEDULE:
            return [_gather_rider(self.shards, WEIGHT_GROUPS[group]) if stage == "gather"
                    else _forward_rider(self.late, WEIGHT_GROUPS[group]) for stage, group in WEIGHT_SCHEDULE[host]]
        return [self._grad_rider(stage, group, grads) for stage, group in GRAD_SCHEDULE.get(host, [])]

    def after(self, host, rider_outs, w, grads):
        for (stage, group), outs in zip(WEIGHT_SCHEDULE.get(host, []), rider_outs):
            (self.late if stage == "gather" else w).update(zip(WEIGHT_GROUPS[group], outs))
        for (stage, group), outs in zip(GRAD_SCHEDULE.get(host, []), rider_outs):
            self._grad_after(stage, group, outs, grads)

    def finish(self, partial):
        outs, (blocks,) = _run_riders("tail_exchange", [_sibling_exchange_rider(self.halves, GRAD_GROUPS["in"]),
                                                        _small_exchange_rider(partial)])
        self.final.update(zip(GRAD_GROUPS["in"], outs))
        return self.final, blocks


def _adamw_math(w, g, m, v):
    m = ADAM_B1 * m + (1.0 - ADAM_B1) * g
    v = ADAM_B2 * v + (1.0 - ADAM_B2) * (g * g)
    m_hat = m / (1.0 - ADAM_B1 ** ADAM_STEP)
    v_hat = v / (1.0 - ADAM_B2 ** ADAM_STEP)
    delta = -ADAM_LR * (m_hat / (jnp.sqrt(v_hat) + ADAM_EPS) + ADAM_WD * w)
    return delta, m, v


def _adamw(name, g, w, m, v):
    rows, cols = g.shape
    tr = rows // 4 if rows % 32 == 0 else rows

    def body(g_ref, w_ref, m_ref, v_ref, go_ref, d_ref, mo_ref, vo_ref):
        gv = g_ref[...]
        delta, m2, v2 = _adamw_math(w_ref[...], gv, m_ref[...], v_ref[...])
        go_ref[...] = gv
        d_ref[...] = delta
        mo_ref[...] = m2
        vo_ref[...] = v2

    spec = pl.BlockSpec((tr, cols), lambda i: (i, 0))
    return pl.pallas_call(
        body,
        name="adamw_" + name,
        grid=(rows // tr,),
        in_specs=[spec] * 4,
        out_specs=[spec] * 4,
        out_shape=[jax.ShapeDtypeStruct((rows, cols), F32)] * 4,
        compiler_params=_params(("arbitrary",)),
    )(g, w, m, v)


SMALL_ROWS = 48
VEC_ROWS = {"final_norm_w": 1, "ffn_norm_w": 8, "mix_norm_w": 16, "conv_b": 24, "lru_ba": 25, "lru_bx": 26, "lru_lambda": 27}
VEC_NAMES = list(VEC_ROWS)
CONV_W_ROW = 28
META_ROW = 32


def _small_exchange_rider(partial):
    def build(ins, outs, sems):
        local_sem, send_sem, recv_sem = sems
        _, _, c, s = _place()
        me = 2 * s + c
        cp = pltpu.make_async_copy(ins[0], outs[0].at[me], local_sem)
        starts, waits = [cp], [cp.wait]
        for k in range(1, 8):
            peer = jnp.bitwise_xor(me, k)
            dev = (peer // 4, (peer // 2) % 2, peer % 2)
            rd = _remote(ins[0], outs[0].at[me], send_sem.at[k - 1], recv_sem.at[k - 1], dev)
            starts.append(rd)
            waits.append(rd.wait_send)
            waits.append(_remote(ins[0], outs[0].at[peer], send_sem.at[k - 1], recv_sem.at[k - 1], dev).wait_recv)
        return starts, waits

    return _Rider([partial], [jax.ShapeDtypeStruct((8, SMALL_ROWS, D), F32)],
                  [pltpu.SemaphoreType.DMA, pltpu.SemaphoreType.DMA((7,)), pltpu.SemaphoreType.DMA((7,))], build)


def _small_update(blocks, place, vecs, conv, meta):
    nvec = len(VEC_NAMES)
    qcols = D // 4

    def in_order(ref):
        total = ref[0]
        for d in range(1, 8):
            total = total + ref[d]
        return total

    def body(pr, blocks_ref, cols_ref, *refs):
        vec_refs = refs[:3 * nvec]
        conv_refs = refs[3 * nvec:3 * nvec + 3]
        meta_refs = refs[3 * nvec + 3:3 * nvec + 6]
        outs = refs[3 * nvec + 6:]
        loss_ref, vec_out, conv_out, meta_out = outs[0], outs[1:5], outs[5:9], outs[9:13]
        tot, col = in_order(blocks_ref), in_order(cols_ref)
        loss_ref[...] = jnp.sum(tot[0:1, :], axis=1, keepdims=True)
        for o in vec_out:
            o[...] = jnp.zeros_like(o)
        for j, name in enumerate(VEC_NAMES):
            w, m, v = (r[...] for r in vec_refs[3 * j:3 * j + 3])
            g = tot[VEC_ROWS[name]:VEC_ROWS[name] + 1, :]
            if name == "lru_lambda":
                g = -g / (1.0 + jnp.exp(w))
            for o, val in zip(vec_out, (g,) + _adamw_math(w, g, m, v)):
                o[j:j + 1, :] = val
        for row, n_rows, ins, group in ((CONV_W_ROW, 4, conv_refs, conv_out), (META_ROW, N_META, meta_refs, meta_out)):
            g = col[row:row + n_rows, :]
            w, m, v = (r[...] for r in ins)
            for o, val in zip(group, (g,) + _adamw_math(w, g, m, v)):
                o[...] = val

    whole = lambda shape: pl.BlockSpec(shape, lambda i, pr: (0,) * len(shape))
    in_specs = [whole((8, SMALL_ROWS, D)), pl.BlockSpec((8, SMALL_ROWS, qcols), lambda i, pr: (0, 0, pr[0]))]
    in_specs += [whole((1, D))] * (3 * nvec) + [whole((4, qcols))] * 3 + [whole((N_META, qcols))] * 3
    out_shapes = [(1, 1)] + [(8, D)] * 4 + [(4, qcols)] * 4 + [(N_META, qcols)] * 4
    return pl.pallas_call(
        body,
        name="small_update",
        grid_spec=pltpu.PrefetchScalarGridSpec(num_scalar_prefetch=1, grid=(1,), in_specs=in_specs,
                                               out_specs=[whole(s) for s in out_shapes]),
        out_shape=[jax.ShapeDtypeStruct(s, F32) for s in out_shapes],
        compiler_params=_params(("arbitrary",)),
    )(place, blocks, blocks, *[a for t in vecs for a in t], *conv, *meta)


WEIGHT_ORDER = ["meta_tokens", "mix_norm_w", "w_in", "conv_w", "conv_b", "lru_wa", "lru_ba", "lru_wx", "lru_bx", "lru_lambda",
                "w_branch_ret", "w_branch_lru", "w_out", "ffn_norm_w", "w_ffn_in", "w_ffn_out", "final_norm_w"]


def kernel(x, meta_tokens, mix_norm_w, w_in, conv_w, conv_b, lru_wa, lru_ba, lru_wx, lru_bx, lru_lambda, w_branch_ret, w_branch_lru, w_out, ffn_norm_w, w_ffn_in, w_ffn_out, final_norm_w, loss_target, m_meta_tokens, m_mix_norm_w, m_w_in, m_conv_w, m_conv_b, m_lru_wa, m_lru_ba, m_lru_wx, m_lru_bx, m_lru_lambda, m_w_branch_ret, m_w_branch_lru, m_w_out, m_ffn_norm_w, m_w_ffn_in, m_w_ffn_out, m_final_norm_w, v_meta_tokens, v_mix_norm_w, v_w_in, v_conv_w, v_conv_b, v_lru_wa, v_lru_ba, v_lru_wx, v_lru_bx, v_lru_lambda, v_w_branch_ret, v_w_branch_lru, v_w_out, v_ffn_norm_w, v_w_ffn_in, v_w_ffn_out, v_final_norm_w):
    args = locals()
    wts = {n: args[n] for n in WEIGHT_ORDER}
    mom = {n: args["m_" + n] for n in WEIGHT_ORDER}
    var = {n: args["v_" + n] for n in WEIGHT_ORDER}
    place = jnp.stack([2 * lax.axis_index("x") + lax.axis_index("y"), lax.axis_index("c")]).astype(jnp.int32)

    shards = {n: wts[n][0].astype(BF16) for n in BIG_PIECES}
    shards["meta_tokens"] = wts["meta_tokens"]
    shards["conv_w"] = wts["conv_w"][0]
    plan = _CommPlan(shards, place)
    (first,) = _run_riders("gather_first", [_gather_rider(shards, FIRST_WEIGHTS)])
    w = dict(zip(FIRST_WEIGHTS, first))
    for n in VEC_NAMES:
        w[n] = wts[n].reshape(1, D)
    chips = jnp.bitwise_xor(place[0], jnp.array((0,) + SHARD_ORDER, dtype=jnp.int32))
    w["route"] = jnp.concatenate([place, jnp.stack([2 * chips, 2 * chips + 1], axis=1).reshape(8)])
    w["w_in_shard"] = shards["w_in"]

    grad_x, grad_head, _, stats = _local_step(x[0], loss_target[0], w, plan)
    partial = jnp.concatenate(stats + [grad_head[PAD_ROWS:]], axis=0)
    shard_grads, blocks = plan.finish(partial)

    out = {}
    for n in BIG_PIECES:
        shape2d = (-1, wts[n].shape[-1])
        res = _adamw(n, *[a.reshape(shape2d) for a in (shard_grads[n], wts[n], mom[n], var[n])])
        out[n] = [r.reshape(wts[n].shape) for r in res]

    vecs = [tuple(a[n].reshape(1, D) for a in (wts, mom, var)) for n in VEC_NAMES]
    conv = tuple(a["conv_w"][0] for a in (wts, mom, var))
    meta = tuple(a["meta_tokens"] for a in (wts, mom, var))
    res = _small_update(blocks, place, vecs, conv, meta)
    loss = res[0].reshape(())
    for j, n in enumerate(VEC_NAMES):
        out[n] = [r[j].reshape(wts[n].shape) for r in res[1:5]]
    out["conv_w"] = [r.reshape(wts["conv_w"].shape) for r in res[5:9]]
    out["meta_tokens"] = list(res[9:13])

    return (loss, grad_x.reshape(x.shape)) + tuple(out[n][kind] for kind in range(4) for n in WEIGHT_ORDER)
```

```python
import functools
import math

import jax
import jax.numpy as jnp
from jax import lax
from jax.experimental import pallas as pl
from jax.experimental.pallas import tpu as pltpu

F32 = jnp.float32
BF16 = jnp.bfloat16

LANES = 128
BF16_ROWS = 16

D = 1024
HEADS = 8
DH = 128
CHUNK = 256
N_META = 16
FRONT = 256
PAD_ROWS = FRONT - N_META
LRU_BLOCKS = 4
LRU_BLOCK = 256
LRU_C = 8.0
FFN = 2816
FFN_HALF = FFN // 2
IN_COLS = 8 * D
ROPE_BASE = 10000.0
EPS = 1e-6
QK_SCALE = DH ** -0.5

ADAM_LR = 0.001
ADAM_B1 = 0.9
ADAM_B2 = 0.999
ADAM_EPS = 1e-08
ADAM_WD = 0.01
ADAM_STEP = 10

TM = 256
TM_HEAVY = 768
MXU_TILE = 256
FFN_SUB = 2 * MXU_TILE
_FFN_SUBS = [(a, min(a + FFN_SUB, FFN)) for a in range(0, FFN, FFN_SUB)]
TM_MIX_BWD = 384
TM_LRU = 384
TM_MIX_FWD = 384
TM_IN_PROJ = 1408
VMEM_LIMIT = 60 * 1024 * 1024
TN_VMEM_BUDGET = 40 * 1024 * 1024

NT_DIMS = (((1,), (1,)), ((), ()))
TN_DIMS = (((0,), (0,)), ((), ()))
MESH = pl.DeviceIdType.MESH


def _params(sem=None):
    if sem is None:
        return pltpu.CompilerParams(vmem_limit_bytes=VMEM_LIMIT)
    return pltpu.CompilerParams(dimension_semantics=sem, vmem_limit_bytes=VMEM_LIMIT)


def _dot(a, b):
    return jnp.dot(a, b, preferred_element_type=F32)


def _dot_nt(a, b):
    return lax.dot_general(a, b, NT_DIMS, preferred_element_type=F32)


def _dot_tn(a, b):
    return lax.dot_general(a, b, TN_DIMS, preferred_element_type=F32)


def _sigmoid(z):
    return 1.0 / (1.0 + jnp.exp(-z))


def _log1p(x):
    return jnp.where(x < 1e-3, x * (1.0 - x * (0.5 - x * (1.0 / 3.0))), jnp.log(1.0 + x))


def _softplus(x):
    return jnp.maximum(x, 0.0) + _log1p(jnp.exp(-jnp.abs(x)))


def _one_minus_square(a, log_a):
    x = 2.0 * log_a
    series = -x * (1.0 + x * (0.5 + x * (1.0 / 6.0)))
    return jnp.where(x > -0.02, series, 1.0 - a * a)


_GELU_K = math.sqrt(2.0 / math.pi)


def _gelu_and_grad(x):
    inner = _GELU_K * (x + 0.044715 * x * x * x)
    t = jnp.tanh(inner)
    val = 0.5 * x * (1.0 + t)
    grad = 0.5 * (1.0 + t) + 0.5 * x * (1.0 - t * t) * _GELU_K * (1.0 + 3.0 * 0.044715 * x * x)
    return val, grad


def _row_ids(i, rows, shape):
    return i * rows + lax.broadcasted_iota(jnp.int32, shape, 0)


def _rope_tables(rows):
    inv_freq = ROPE_BASE ** (-jnp.arange(0, DH, 2, dtype=F32) / DH)
    block_pos = jnp.arange(rows // FRONT, dtype=jnp.int32) * FRONT - PAD_ROWS
    coarse = block_pos.astype(F32)[:, None] * inv_freq[None, :]
    fine = jnp.arange(FRONT, dtype=F32)[:, None] * inv_freq[None, :]
    ca, sa = jnp.cos(coarse)[:, None, :], jnp.sin(coarse)[:, None, :]
    cb, sb = jnp.cos(fine)[None, :, :], jnp.sin(fine)[None, :, :]
    cos = (ca * cb - sa * sb).reshape(rows, DH // 2)
    sin = (sa * cb + ca * sb).reshape(rows, DH // 2)
    return jnp.concatenate([cos, cos], axis=1), jnp.concatenate([-sin, sin], axis=1)


def _decay_tables():
    log_g = jnp.log(1.0 - 2.0 ** (-5.0 - jnp.arange(HEADS, dtype=F32)))
    idx = jnp.arange(CHUNK, dtype=F32)
    diff = idx[:, None] - idx[None, :]
    intra = jnp.where(diff[None] >= 0, jnp.exp(jnp.maximum(diff, 0.0)[None] * log_g[:, None, None]), 0.0)
    q_decay = jnp.exp((idx + 1.0)[:, None] * log_g[None, :])
    k_decay = jnp.exp((CHUNK - 1.0 - idx)[:, None] * log_g[None, :])
    chunk_decay = jnp.exp(CHUNK * log_g)
    wide = lambda a: jnp.repeat(a, DH, axis=-1)
    return intra, jnp.swapaxes(intra, 1, 2), wide(q_decay), wide(k_decay), wide(chunk_decay[None, :])


def _norm1(head, x2d, norm_w, riders=()):
    rows = FRONT + x2d.shape[0]
    tm = TM_IN_PROJ if rows % TM_IN_PROJ == 0 else TM_MIX_FWD
    nt = rows // tm

    def body(head_hbm, x_hbm, nw_ref, u_ref, rstd_ref, h0_sc, h0_sem):
        slot = _frame_rows(head_hbm, x_hbm, h0_sc, h0_sem, pl.program_id(0), tm, nt)
        hv = h0_sc[slot]
        rs = lax.rsqrt(jnp.mean(hv * hv, axis=-1, keepdims=True) + EPS)
        u_ref[...] = ((hv * rs) * nw_ref[...]).astype(BF16)
        rstd_ref[...] = rs

    return _hosted_call(
        body,
        name="norm1",
        grid=(nt,),
        in_specs=[_ANY, _ANY, pl.BlockSpec((1, D), lambda i: (0, 0))],
        out_specs=[pl.BlockSpec((tm, D), lambda i: (i, 0)), pl.BlockSpec((tm, 1), lambda i: (i, 0))],
        out_shape=[jax.ShapeDtypeStruct((rows, D), BF16), jax.ShapeDtypeStruct((rows, 1), F32)],
        scratch_shapes=[pltpu.VMEM((2, tm, D), F32), pltpu.SemaphoreType.DMA((3,))],
        args=(head, x2d, norm_w),
        riders=riders,
    )


def _heavy_tile(rows):
    return TM_HEAVY if rows % TM_HEAVY == 0 else TM


def _lru_tile(rows):
    return TM_LRU if rows % TM_LRU == 0 else TM


def _swap_3_4(group):
    return jnp.where(group == 3, 4, jnp.where(group == 4, 3, group))


SHARD_ORDER = (2, 3, 1)
LRU_IN_COL = 3
GATES_COL = 1


def _in_proj(u, w_shard, route, cos_t, sin_t, riders=()):
    rows = u.shape[0]
    tm = TM_IN_PROJ if rows % TM_IN_PROJ == 0 else _heavy_tile(rows)
    nt = rows // tm
    kind, shard = BIG_PIECES["w_in"]

    def body(route_ref, u_hbm, wsh_ref, cos_ref, sin_ref, proj_ref, wfull_ref,
             u_sc, w_sc, u_sem, w_sem, local_sem, ici_send, ici_recv, fwd_send, fwd_recv):
        g, i = pl.program_id(0), pl.program_id(1)
        s, c = route_ref[0], route_ref[1]
        gid = route_ref[2 + g]
        sibling = (s // 2, s % 2, 1 - c)
        local = pltpu.make_async_copy(wsh_ref, _full_region(wfull_ref, kind, shard, s, None), local_sem)
        u_copies = [pltpu.make_async_copy(u_hbm.at[pl.ds(t * tm, tm)], u_sc.at[t], u_sem.at[t]) for t in range(nt)]
        sends, arrivals = [], []
        for k in (1, 2, 3):
            s2, dev = _other_chip(s, c, k)
            sends.append(_remote(_shard_region(wsh_ref, shard, c), _full_region(wfull_ref, kind, shard, s, c),
                                 ici_send.at[k - 1], ici_recv.at[k - 1], dev))
            mine = _full_region(wfull_ref, kind, shard, s2, c)
            theirs = _full_region(wfull_ref, kind, shard, s2, 1 - c)
            arrivals.append((_remote(mine, mine, ici_send.at[k - 1], ici_recv.at[k - 1], dev),
                             _remote(mine, mine, fwd_send.at[k - 1], fwd_recv.at[k - 1], sibling),
                             _remote(theirs, theirs, fwd_send.at[k - 1], fwd_recv.at[k - 1], sibling)))

        slot = g % 2
        last_tile = i == nt - 1

        def block_copy(src, col, to_slot):
            return pltpu.make_async_copy(src.at[:, pl.ds(pl.multiple_of(col * D, D), D)], w_sc.at[to_slot],
                                         w_sem.at[to_slot])

        @pl.when(jnp.logical_and(g == 0, i == 0))
        def _():
            for k in SHARD_ORDER:
                sends[k - 1].start()
            local.start()
            for cp in u_copies:
                cp.start()
            cp = block_copy(wsh_ref, 0, 0)
            cp.start()
            cp.wait()

        @pl.when(jnp.logical_and(last_tile, g == 0))
        def _():
            block_copy(wsh_ref, 1, 1).start()

        for pos, k in enumerate(SHARD_ORDER, start=1):
            arrived, forward, forwarded = arrivals[k - 1]

            @pl.when(jnp.logical_and(last_tile, g == 2 * pos - 1))
            def _():
                arrived.wait_recv()
                forward.start()
                forwarded.wait_recv()
                block_copy(wfull_ref, route_ref[2 + 2 * pos], 0).start()

            @pl.when(jnp.logical_and(last_tile, g == 2 * pos))
            def _():
                block_copy(wfull_ref, route_ref[3 + 2 * pos], 1).start()

        @pl.when(jnp.logical_and(i == 0, g > 0))
        def _():
            block_copy(wfull_ref, 0, slot).wait()

        for t in range(nt):
            @pl.when(jnp.logical_and(g == 0, i == t))
            def _():
                u_copies[t].wait()

        part = MXU_TILE
        parts = range(0, D, part)

        @pl.when(gid < 2)
        def _():
            scale = jnp.where(gid == 1, QK_SCALE, 1.0).astype(F32)
            for a in parts:
                acc = _dot(u_sc[i], w_sc[slot, :, a:a + part])
                for h in range(part // DH):
                    blk = acc[:, h * DH:(h + 1) * DH]
                    out = (blk * cos_ref[...] + pltpu.roll(blk, DH // 2, axis=1) * sin_ref[...]) * scale
                    proj_ref[:, a + h * DH:a + (h + 1) * DH] = out.astype(BF16)

        @pl.when(gid >= 2)
        def _():
            for a in parts:
                proj_ref[:, a:a + part] = _dot(u_sc[i], w_sc[slot, :, a:a + part]).astype(BF16)

        @pl.when(jnp.logical_and(g == 7, i == nt - 1))
        def _():
            local.wait()
            for cp in sends:
                cp.wait_send()
            for _, forward, _ in arrivals:
                forward.wait_send()

    return _hosted_call(
        body,
        name="in_proj",
        grid=(8, nt),
        in_specs=[
            _ANY, _ANY,
            pl.BlockSpec((tm, DH), lambda g, i, rt: (i, 0)),
            pl.BlockSpec((tm, DH), lambda g, i, rt: (i, 0)),
        ],
        out_specs=[pl.BlockSpec((tm, D), lambda g, i, rt: (i, _swap_3_4(rt[2 + g]))), _ANY],
        out_shape=[jax.ShapeDtypeStruct((rows, IN_COLS), BF16), jax.ShapeDtypeStruct((D, IN_COLS), BF16)],
        scratch_shapes=[
            pltpu.VMEM((nt, tm, D), BF16), pltpu.VMEM((2, D, D), BF16),
            pltpu.SemaphoreType.DMA((nt,)), pltpu.SemaphoreType.DMA((2,)), pltpu.SemaphoreType.DMA,
            pltpu.SemaphoreType.DMA((3,)), pltpu.SemaphoreType.DMA((3,)),
            pltpu.SemaphoreType.DMA((3,)), pltpu.SemaphoreType.DMA((3,)),
        ],
        args=(u, w_shard, cos_t, sin_t),
        riders=riders,
        prefetch=route,
        riders_after_body=True,
    )


def _retention_fwd(proj_bf, intra, q_dec, k_dec, c_dec, riders=()):
    rows = proj_bf.shape[0]
    nc = rows // CHUNK

    def body(q_ref, k_ref, v_ref, m_ref, qd_ref, kd_ref, cd_ref, o_ref, sprev_ref, s_sc):
        @pl.when(pl.program_id(0) == 0)
        def _():
            s_sc[...] = jnp.zeros_like(s_sc)

        for h in range(HEADS):
            sl = slice(h * DH, (h + 1) * DH)
            q, k, v = q_ref[:, sl], k_ref[:, sl], v_ref[:, sl]
            state = s_sc[h]
            state_b = state.astype(BF16)
            sprev_ref[0, h] = state_b
            s = _dot_nt(q, k) * m_ref[h]
            inner = _dot(s.astype(BF16), v)
            cross = _dot(q, state_b) * qd_ref[:, sl]
            o_ref[:, sl] = (inner + cross).astype(BF16)
            k_scaled = (k.astype(F32) * kd_ref[:, sl]).astype(BF16)
            s_sc[h] = state * cd_ref[:, sl] + _dot_tn(k_scaled, v)

    chunk_spec = lambda col: pl.BlockSpec((CHUNK, D), lambda c: (c, col))
    const2 = lambda shape: pl.BlockSpec(shape, lambda c: (0, 0))
    return _hosted_call(
        body,
        name="retention_fwd",
        grid=(nc,),
        in_specs=[
            chunk_spec(0), chunk_spec(1), chunk_spec(2),
            pl.BlockSpec((HEADS, CHUNK, CHUNK), lambda c: (0, 0, 0)),
            const2((CHUNK, D)), const2((CHUNK, D)), const2((1, D)),
        ],
        out_specs=[
            pl.BlockSpec((CHUNK, D), lambda c: (c, 0)),
            pl.BlockSpec((1, HEADS, DH, DH), lambda c: (c, 0, 0, 0)),
        ],
        out_shape=[
            jax.ShapeDtypeStruct((rows, D), BF16),
            jax.ShapeDtypeStruct((nc, HEADS, DH, DH), BF16),
        ],
        scratch_shapes=[pltpu.VMEM((HEADS, DH, DH), F32)],
        args=(proj_bf, proj_bf, proj_bf, intra, q_dec, k_dec, c_dec),
        riders=riders,
    )


def _shift_down(x, first8_prev, d):
    rolled = pltpu.roll(x, d, axis=0)
    head = pltpu.roll(jnp.concatenate([first8_prev, x[0:8]], axis=0), d, axis=0)[8:16]
    return rolled, head


def _conv_and_gates(x, prev8, cw_ref, cb_ref, wa_ref, wx_ref, ba_ref, bx_ref, lam_ref, c_sc):
    cw = cw_ref[...]
    conv = cb_ref[...] + cw[3:4] * x
    head = cb_ref[...] + cw[3:4] * x[0:8]
    for d in (1, 2, 3):
        rolled, hd = _shift_down(x, prev8, d)
        conv = conv + cw[3 - d:4 - d] * rolled
        head = head + cw[3 - d:4 - d] * hd
    c_sc[...] = conv
    c_sc[0:8, :] = head
    c = c_sc[...]
    zr, zi = [], []
    for g in range(LRU_BLOCKS):
        sl = slice(g * LRU_BLOCK, (g + 1) * LRU_BLOCK)
        cg = c[:, sl].astype(BF16)
        zr.append(_dot(cg, wa_ref[g]))
        zi.append(_dot(cg, wx_ref[g]))
    r = _sigmoid(jnp.concatenate(zr, axis=1) + ba_ref[...])
    gate_i = _sigmoid(jnp.concatenate(zi, axis=1) + bx_ref[...])
    sp = _softplus(-lam_ref[...])
    log_a = (-LRU_C) * r * sp
    a = jnp.exp(log_a)
    mult = jnp.sqrt(_one_minus_square(a, log_a))
    return c, r, gate_i, a, mult, log_a


def _lru_fwd(proj, conv_w, conv_b, wa, wx, ba, bx, lam, riders=()):
    rows = proj.shape[0]
    TM = _lru_tile(rows)
    nt = rows // TM

    def body(x_ref, cw_ref, cb_ref, wa_ref, wx_ref, ba_ref, bx_ref, lam_ref,
             h_ref, c_ref, r_ref, i_ref, la_ref, mult_ref, prev_sc, carry_sc, c_sc, a_sc, u_sc, h_sc):
        i = pl.program_id(0)

        @pl.when(i == 0)
        def _():
            prev_sc[...] = jnp.zeros_like(prev_sc)
            carry_sc[...] = jnp.zeros_like(carry_sc)

        x = x_ref[...].astype(F32)
        c, r, gate_i, a, mult, log_a = _conv_and_gates(x, prev_sc[...], cw_ref, cb_ref, wa_ref, wx_ref, ba_ref, bx_ref,
                                                       lam_ref, c_sc)
        for ref, val in ((c_ref, c), (r_ref, r), (i_ref, gate_i), (la_ref, log_a), (mult_ref, mult)):
            ref[...] = val.astype(BF16)
        prev_sc[...] = x[TM - 8:TM]
        valid = _row_ids(i, TM, (TM, D)) >= PAD_ROWS
        a_sc[...] = a
        u_sc[...] = jnp.where(valid, mult * gate_i * c, 0.0)
        row8 = lax.broadcasted_iota(jnp.int32, (8, D), 0)

        def group(gi, hprev):
            r0 = pl.multiple_of(gi * 8, 8)
            aa = a_sc[pl.ds(r0, 8), :]
            uu = u_sc[pl.ds(r0, 8), :]
            for d in (1, 2, 4):
                a_sh = jnp.where(row8 >= d, pltpu.roll(aa, d, axis=0), 1.0)
                u_sh = jnp.where(row8 >= d, pltpu.roll(uu, d, axis=0), 0.0)
                uu = uu + aa * u_sh
                aa = aa * a_sh
            hb = aa * hprev + uu
            h_sc[pl.ds(r0, 8), :] = hb
            return hb[7:8, :]

        hlast = lax.fori_loop(0, TM // 8, group, carry_sc[0:1, :])
        carry_sc[0:1, :] = hlast
        h_ref[...] = h_sc[...].astype(BF16)

    vec = pl.BlockSpec((1, D), lambda i: (0, 0))
    wspec = pl.BlockSpec((LRU_BLOCKS, LRU_BLOCK, LRU_BLOCK), lambda i: (0, 0, 0))
    return _hosted_call(
        body,
        name="lru_fwd",
        grid=(nt,),
        in_specs=[
            pl.BlockSpec((TM, D), lambda i: (i, LRU_IN_COL)),
            pl.BlockSpec((4, D), lambda i: (0, 0)),
            vec, wspec, wspec, vec, vec, vec,
        ],
        out_specs=[pl.BlockSpec((TM, D), lambda i: (i, 0)) for _ in range(6)],
        out_shape=[jax.ShapeDtypeStruct((rows, D), BF16) for _ in range(6)],
        scratch_shapes=[
            pltpu.VMEM((8, D), F32), pltpu.VMEM((8, D), F32),
            pltpu.VMEM((TM, D), F32), pltpu.VMEM((TM, D), F32), pltpu.VMEM((TM, D), F32), pltpu.VMEM((TM, D), F32),
        ],
        args=(proj, conv_w, conv_b, wa, wx, ba, bx, lam),
        riders=riders,
    )


def _group_norm(o):
    outs, rstds = [], []
    for h in range(HEADS):
        oh = o[:, h * DH:(h + 1) * DH]
        rs = lax.rsqrt(jnp.mean(oh * oh, axis=-1, keepdims=True) + EPS)
        outs.append(oh * rs)
        rstds.append(rs)
    return jnp.concatenate(outs, axis=1), rstds


def _frame_rows(head_hbm, x_hbm, buf, sems, i, tm, nt):
    slot = i % 2

    @pl.when(i == 0)
    def _():
        first = [pltpu.make_async_copy(head_hbm, buf.at[0, pl.ds(0, FRONT)], sems.at[2]),
                 pltpu.make_async_copy(x_hbm.at[pl.ds(0, tm - FRONT)], buf.at[0, pl.ds(FRONT, tm - FRONT)], sems.at[0])]
        for cp in first:
            cp.start()
        for cp in first:
            cp.wait()

    @pl.when(i + 1 < nt)
    def _():
        start = pl.multiple_of((i + 1) * tm - FRONT, LANES)
        pltpu.make_async_copy(x_hbm.at[pl.ds(start, tm)], buf.at[1 - slot], sems.at[1 - slot]).start()

    @pl.when(i > 0)
    def _():
        pltpu.make_async_copy(x_hbm.at[pl.ds(0, tm)], buf.at[slot], sems.at[slot]).wait()

    return slot


def _mix_fwd(head, x2d, o, proj, h_lru, w_br, w_bl, w_o, ffn_norm_w, riders=()):
    rows = o.shape[0]
    tm = TM_MIX_FWD
    assert rows % tm == 0 and tm > FRONT
    nt = rows // tm

    def body(head_hbm, x_hbm, o_ref, gates_ref, hl_ref, wbr_ref, wbl_ref, wo_ref, nw_ref,
             yret_ref, ylru_ref, h1_ref, u2_ref, rstd_ref, h0_sc, h0_sem):
        i = pl.program_id(0)
        slot = _frame_rows(head_hbm, x_hbm, h0_sc, h0_sem, i, tm, nt)
        gate = lambda j: gates_ref[:, j * D:(j + 1) * D].astype(F32)
        on, _ = _group_norm(o_ref[...].astype(F32))
        gret = gate(0)
        a_ret = (gret * _sigmoid(gret) * on).astype(BF16)
        y_ret = _dot(a_ret, wbr_ref[...])
        gl, _ = _gelu_and_grad(gate(1))
        a_lru = (gl * hl_ref[...].astype(F32)).astype(BF16)
        y_lru = _dot(a_lru, wbl_ref[...])
        mixed = (_sigmoid(gate(2)) * y_ret + _sigmoid(gate(3)) * y_lru).astype(BF16)
        delta = _dot(mixed, wo_ref[...])
        yret_ref[...] = y_ret.astype(BF16)
        ylru_ref[...] = y_lru.astype(BF16)
        h1 = h0_sc[slot] + delta
        rs = lax.rsqrt(jnp.mean(h1 * h1, axis=-1, keepdims=True) + EPS)
        h1_ref[...] = h1
        u2_ref[...] = ((h1 * rs) * nw_ref[...]).astype(BF16)
        rstd_ref[...] = rs

    tile = lambda col: pl.BlockSpec((tm, D), lambda i: (i, col))
    wspec = pl.BlockSpec((D, D), lambda i: (0, 0))
    return _hosted_call(
        body,
        name="mix_fwd",
        grid=(nt,),
        in_specs=[
            _ANY, _ANY,
            tile(0), pl.BlockSpec((tm, 4 * D), lambda i: (i, GATES_COL)), tile(0),
            wspec, wspec, wspec,
            pl.BlockSpec((1, D), lambda i: (0, 0)),
        ],
        out_specs=[tile(0), tile(0), tile(0), tile(0), pl.BlockSpec((tm, 1), lambda i: (i, 0))],
        out_shape=[
            jax.ShapeDtypeStruct((rows, D), BF16), jax.ShapeDtypeStruct((rows, D), BF16),
            jax.ShapeDtypeStruct((rows, D), F32), jax.ShapeDtypeStruct((rows, D), BF16),
            jax.ShapeDtypeStruct((rows, 1), F32),
        ],
        scratch_shapes=[pltpu.VMEM((2, tm, D), F32), pltpu.SemaphoreType.DMA((3,))],
        args=(head, x2d, o, proj, h_lru, w_br, w_bl, w_o, ffn_norm_w),
        riders=riders,
    )


def _ffn_fwd_gate(u2, w_ffn_in, riders=()):
    rows = u2.shape[0]
    tm = _heavy_tile(rows)
    hid = lambda: pl.BlockSpec((tm, FFN), lambda i: (i, 0))

    def body(u2_ref, w_hbm, silu_ref, dsilu_ref, act_ref, w_sc, w_sem):
        @pl.when(pl.program_id(0) == 0)
        def _():
            cp = pltpu.make_async_copy(w_hbm, w_sc, w_sem)
            cp.start()
            cp.wait()

        u2 = u2_ref[...]
        for a, b in _FFN_SUBS:
            g = _dot(u2, w_sc[:, a:b])
            up = _dot(u2, w_sc[:, FFN + a:FFN + b])
            sg = _sigmoid(g)
            silu = g * sg
            silu_ref[:, a:b] = silu.astype(BF16)
            dsilu_ref[:, a:b] = (up * (sg * (1.0 + g * (1.0 - sg)))).astype(BF16)
            act_ref[:, a:b] = (silu * up).astype(BF16)

    return _hosted_call(
        body,
        name="ffn_fwd_gate",
        grid=(rows // tm,),
        in_specs=[pl.BlockSpec((tm, D), lambda i: (i, 0)), _ANY],
        out_specs=[hid(), hid(), hid()],
        out_shape=[jax.ShapeDtypeStruct((rows, FFN), BF16)] * 3,
        scratch_shapes=[pltpu.VMEM((D, 2 * FFN), BF16), pltpu.SemaphoreType.DMA],
        args=(u2, w_ffn_in),
        riders=riders,
    )


def _ffn_out_loss(act, h1, w_ffn_out, target, final_norm_w):
    rows = act.shape[0]
    tm = _heavy_tile(rows)
    nt = rows // tm

    def body(act_ref, h1_ref, wo_ref, fnw_ref, tgt_hbm, dh2_ref, stats_ref, tgt_sc, tgt_sem):
        i = pl.program_id(0)
        slot = i % 2

        @pl.when(i == 0)
        def _():
            stats_ref[...] = jnp.zeros_like(stats_ref)
            tgt_sc[0, 0:FRONT, :] = jnp.zeros((FRONT, D), F32)
            cp = pltpu.make_async_copy(tgt_hbm.at[pl.ds(0, tm - FRONT)], tgt_sc.at[0, pl.ds(FRONT, tm - FRONT)], tgt_sem.at[0])
            cp.start()
            cp.wait()

        @pl.when(i + 1 < nt)
        def _():
            start = pl.multiple_of((i + 1) * tm - FRONT, FRONT)
            pltpu.make_async_copy(tgt_hbm.at[pl.ds(start, tm)], tgt_sc.at[1 - slot], tgt_sem.at[1 - slot]).start()

        @pl.when(i > 0)
        def _():
            pltpu.make_async_copy(tgt_hbm.at[pl.ds(0, tm)], tgt_sc.at[slot], tgt_sem.at[slot]).wait()

        fnw = fnw_ref[...]
        half = tm // 2
        for r in (0, half):
            sl = pl.ds(r, half)
            h2 = h1_ref[sl, :] + _dot(act_ref[sl, :], wo_ref[...])
            rs = lax.rsqrt(jnp.mean(h2 * h2, axis=-1, keepdims=True) + EPS)
            n = h2 * rs
            valid = i * tm + r + lax.broadcasted_iota(jnp.int32, (half, D), 0) >= FRONT
            diff = jnp.where(valid, n * fnw - tgt_sc[slot, sl, :], 0.0)
            dy = diff * (1.0 / D)
            stats_ref[0:1, :] += (0.5 / D) * jnp.sum(diff * diff, axis=0, keepdims=True)
            stats_ref[1:2, :] += jnp.sum(dy * n, axis=0, keepdims=True)
            dn = dy * fnw
            dh2_ref[sl, :] = rs * (dn - n * jnp.mean(dn * n, axis=-1, keepdims=True))

    return pl.pallas_call(
        body,
        name="ffn_out_loss",
        grid=(nt,),
        in_specs=[
            pl.BlockSpec((tm, FFN), lambda i: (i, 0)),
            pl.BlockSpec((tm, D), lambda i: (i, 0)),
            pl.BlockSpec((FFN, D), lambda i: (0, 0)),
            pl.BlockSpec((1, D), lambda i: (0, 0)),
            _ANY,
        ],
        out_specs=[pl.BlockSpec((tm, D), lambda i: (i, 0)), pl.BlockSpec((8, D), lambda i: (0, 0))],
        out_shape=[jax.ShapeDtypeStruct((rows, D), F32), jax.ShapeDtypeStruct((8, D), F32)],
        scratch_shapes=[pltpu.VMEM((2, tm, D), F32), pltpu.SemaphoreType.DMA((2,))],
        compiler_params=_params(("arbitrary",)),
    )(act, h1, w_ffn_out, final_norm_w, target)


def _ffn_bwd(dh2, silu, dsilu, h1, rstd2, w_ffn_in, w_ffn_out, ffn_norm_w):
    rows = dh2.shape[0]
    tm = _heavy_tile(rows)
    blk = lambda: pl.BlockSpec((tm, FFN), lambda i: (i, 0))

    def gate_body(dh2_ref, silu_ref, dsilu_ref, wo_hbm, dg_ref, dup_ref, dh2b_ref, wo_sc, wo_sem):
        @pl.when(pl.program_id(0) == 0)
        def _():
            cp = pltpu.make_async_copy(wo_hbm, wo_sc, wo_sem)
            cp.start()
            cp.wait()

        dh2b = dh2_ref[...].astype(BF16)
        dh2b_ref[...] = dh2b
        for a, b in _FFN_SUBS:
            dact = _dot_nt(dh2b, wo_sc[a:b, :])
            dup_ref[:, a:b] = (dact * silu_ref[:, a:b].astype(F32)).astype(BF16)
            dg_ref[:, a:b] = (dact * dsilu_ref[:, a:b].astype(F32)).astype(BF16)

    dg, dup, dh2b = pl.pallas_call(
        gate_body,
        name="ffn_bwd_gate",
        grid=(rows // tm,),
        in_specs=[pl.BlockSpec((tm, D), lambda i: (i, 0)), blk(), blk(), _ANY],
        out_specs=[blk(), blk(), pl.BlockSpec((tm, D), lambda i: (i, 0))],
        out_shape=[jax.ShapeDtypeStruct((rows, FFN), BF16)] * 2 + [jax.ShapeDtypeStruct((rows, D), BF16)],
        scratch_shapes=[pltpu.VMEM((FFN, D), BF16), pltpu.SemaphoreType.DMA],
        compiler_params=_params(("arbitrary",)),
    )(dh2, silu, dsilu, w_ffn_out)

    def body(dg_ref, dup_ref, dh2_ref, h1_ref, rstd_ref, w_hbm, nw_ref, dh1_ref, stats_ref, w_sc, w_sem):
        @pl.when(pl.program_id(0) == 0)
        def _():
            stats_ref[...] = jnp.zeros_like(stats_ref)
            cp = pltpu.make_async_copy(w_hbm, w_sc, w_sem)
            cp.start()
            cp.wait()

        half = tm // 2
        for r in (0, half):
            sl = pl.ds(r, half)
            du = _dot_nt(dg_ref[sl, :], w_sc[:, 0:FFN]) + _dot_nt(dup_ref[sl, :], w_sc[:, FFN:2 * FFN])
            rs = rstd_ref[sl, :]
            n = h1_ref[sl, :] * rs
            stats_ref[0:1, :] += jnp.sum(du * n, axis=0, keepdims=True)
            dn = du * nw_ref[...]
            dh1_ref[sl, :] = dh2_ref[sl, :] + rs * (dn - n * jnp.mean(dn * n, axis=-1, keepdims=True))

    row = lambda width: pl.BlockSpec((tm, width), lambda i: (i, 0))
    dh1, stats = pl.pallas_call(
        body,
        name="ffn_bwd_in",
        grid=(rows // tm,),
        in_specs=[row(FFN), row(FFN), row(D), row(D), row(1), _ANY, pl.BlockSpec((1, D), lambda i: (0, 0))],
        out_specs=[row(D), pl.BlockSpec((8, D), lambda i: (0, 0))],
        out_shape=[jax.ShapeDtypeStruct((rows, D), F32), jax.ShapeDtypeStruct((8, D), F32)],
        scratch_shapes=[pltpu.VMEM((D, 2 * FFN), BF16), pltpu.SemaphoreType.DMA],
        compiler_params=_params(("arbitrary",)),
    )(dg, dup, dh2, h1, rstd2, w_ffn_in, ffn_norm_w)
    return dg, dup, dh2b, dh1, stats


def _mix_bwd(dh1, o, proj, h_lru, y_ret, y_lru, w_br, w_bl, w_o, riders=()):
    rows = dh1.shape[0]
    tm = TM_MIX_BWD
    nt = rows // tm

    def body(dh1_ref, o_ref, gates_ref, hl_ref, yret_ref, ylru_ref, wbr_hbm, wbl_hbm, wo_hbm,
             dproj_ref, do_ref, dhl_ref, mixed_ref, aret_ref, alru_ref, dyret_ref, dylru_ref, w_sc, w_sem):
        @pl.when(pl.program_id(0) == 0)
        def _():
            copies = [pltpu.make_async_copy(src, w_sc.at[q], w_sem.at[q]) for q, src in enumerate((wbr_hbm, wbl_hbm, wo_hbm))]
            for cp in copies:
                cp.start()
            for cp in copies:
                cp.wait()

        wbr_ref, wbl_ref, wo_ref = w_sc.at[0], w_sc.at[1], w_sc.at[2]
        gate = lambda j: gates_ref[:, j * D:(j + 1) * D].astype(F32)
        dmixed = _dot_nt(dh1_ref[...].astype(BF16), wo_ref[...])
        y_ret, y_lru = yret_ref[...].astype(F32), ylru_ref[...].astype(F32)
        sa, sb = _sigmoid(gate(2)), _sigmoid(gate(3))
        mixed_ref[...] = (sa * y_ret + sb * y_lru).astype(BF16)
        dga = dmixed * y_ret * sa * (1.0 - sa)
        dgb = dmixed * y_lru * sb * (1.0 - sb)
        dy_ret = (dmixed * sa).astype(BF16)
        dy_lru = (dmixed * sb).astype(BF16)
        dyret_ref[...] = dy_ret
        dylru_ref[...] = dy_lru
        da_ret = _dot_nt(dy_ret, wbr_ref[...])
        da_lru = _dot_nt(dy_lru, wbl_ref[...])

        gret = gate(0)
        sg = _sigmoid(gret)
        silu = gret * sg
        on, rstds = _group_norm(o_ref[...].astype(F32))
        aret_ref[...] = (silu * on).astype(BF16)
        dgret = da_ret * on * (sg * (1.0 + gret * (1.0 - sg)))
        don = da_ret * silu
        for h in range(HEADS):
            sl = slice(h * DH, (h + 1) * DH)
            onh, donh = on[:, sl], don[:, sl]
            do_ref[:, sl] = (rstds[h] * (donh - onh * jnp.mean(donh * onh, axis=-1, keepdims=True))).astype(BF16)

        gl, gl_grad = _gelu_and_grad(gate(1))
        hl = hl_ref[...].astype(F32)
        alru_ref[...] = (gl * hl).astype(BF16)
        dlgate = da_lru * hl * gl_grad
        dhl_ref[...] = (da_lru * gl).astype(BF16)

        for j, val in enumerate((dgret, dlgate, dga, dgb)):
            dproj_ref[:, j * D:(j + 1) * D] = val.astype(BF16)

    tile = lambda col: pl.BlockSpec((tm, D), lambda i: (i, col))
    gates = pl.BlockSpec((tm, 4 * D), lambda i: (i, GATES_COL))
    bf = lambda: jax.ShapeDtypeStruct((rows, D), BF16)
    return _hosted_call(
        body,
        name="mix_bwd",
        grid=(nt,),
        in_specs=[tile(0), tile(0), gates, tile(0), tile(0), tile(0), _ANY, _ANY, _ANY],
        out_specs=[pl.BlockSpec((tm, 4 * D), lambda i: (i, GATES_COL))] + [tile(0)] * 7,
        out_shape=[jax.ShapeDtypeStruct((rows, IN_COLS), BF16)] + [bf() for _ in range(7)],
        scratch_shapes=[pltpu.VMEM((3, D, D), BF16), pltpu.SemaphoreType.DMA((3,))],
        args=(dh1, o, proj, h_lru, y_ret, y_lru, w_br, w_bl, w_o),
        riders=riders,
    )


def _retention_bwd(dproj, proj_bf, do, sprev, intra, intra_t, q_dec, k_dec, c_dec, cos_t, sin_t, riders=()):
    rows = proj_bf.shape[0]
    nc = rows // CHUNK

    def body(dproj_in_ref, q_ref, k_ref, v_ref, do_ref, sprev_ref, m_ref, mt_ref, qd_ref, kd_ref, cd_ref, cos_ref, sin_ref,
             dproj_ref, ds_sc):
        @pl.when(pl.program_id(0) == 0)
        def _():
            ds_sc[...] = jnp.zeros_like(ds_sc)

        cos, sin = cos_ref[...], sin_ref[...]

        def unrotate(dy):
            return dy * cos - pltpu.roll(dy, DH // 2, axis=1) * sin

        for h in range(HEADS):
            sl = slice(h * DH, (h + 1) * DH)
            q, k, v = q_ref[:, sl], k_ref[:, sl], v_ref[:, sl]
            do = do_ref[:, sl]
            dob = do.astype(BF16)
            doq = (do * qd_ref[:, sl]).astype(BF16)
            state_prev = sprev_ref[0, h]
            dstate = ds_sc[h]
            dstate_b = dstate.astype(BF16)
            s_t = (_dot_nt(k, q) * mt_ref[h]).astype(BF16)
            ds_t = (_dot_nt(v, dob) * mt_ref[h]).astype(BF16)
            ds = (_dot_nt(dob, v) * m_ref[h]).astype(BF16)
            kd = kd_ref[:, sl]
            dq = _dot(ds, k) + _dot_nt(doq, state_prev)
            dk = _dot(ds_t, q) + _dot_nt(v, dstate_b) * kd
            k_scaled = (k.astype(F32) * kd).astype(BF16)
            dv = _dot(s_t, dob) + _dot(k_scaled, dstate_b)
            ds_sc[h] = dstate * cd_ref[:, sl] + _dot_tn(q, doq)
            for part, val in enumerate((unrotate(dq), unrotate(dk) * QK_SCALE, dv)):
                dproj_ref[:, part * D + h * DH:part * D + (h + 1) * DH] = val.astype(BF16)

    rev = lambda c: nc - 1 - c
    chunk_spec = lambda col: pl.BlockSpec((CHUNK, D), lambda c: (rev(c), col))
    const2 = lambda shape: pl.BlockSpec(shape, lambda c: (0, 0))
    const3 = pl.BlockSpec((HEADS, CHUNK, CHUNK), lambda c: (0, 0, 0))
    return _hosted_call(
        body,
        name="retention_bwd",
        grid=(nc,),
        in_specs=[
            pl.BlockSpec(memory_space=pl.ANY),
            chunk_spec(0), chunk_spec(1), chunk_spec(2), chunk_spec(0),
            pl.BlockSpec((1, HEADS, DH, DH), lambda c: (rev(c), 0, 0, 0)),
            const3, const3,
            const2((CHUNK, D)), const2((CHUNK, D)), const2((1, D)),
            pl.BlockSpec((CHUNK, DH), lambda c: (rev(c), 0)),
            pl.BlockSpec((CHUNK, DH), lambda c: (rev(c), 0)),
        ],
        out_specs=[pl.BlockSpec((CHUNK, 3 * D), lambda c: (rev(c), 0))],
        out_shape=[jax.ShapeDtypeStruct(dproj.shape, BF16)],
        aliases={0: 0},
        scratch_shapes=[pltpu.VMEM((HEADS, DH, DH), F32)],
        args=(dproj, proj_bf, proj_bf, proj_bf, do, sprev, intra, intra_t, q_dec, k_dec, c_dec, cos_t, sin_t),
        riders=riders,
    )


def _lru_bwd(dproj, proj, saved, dhl, conv_w, wa, wx, lam, riders=()):
    rows = proj.shape[0]
    TM = _lru_tile(rows)
    nt = rows // TM
    per8 = TM // 8

    def body(dproj_in_ref, x_ref, xprev_ref, h_ref, hprev_ref, c_ref, r_ref, i_ref, la_ref, mult_ref, dhl_ref,
             cw_ref, wa_ref, wx_ref, lam_ref,
             dproj_ref, dwa_ref, dwx_ref, stats_ref, anext_sc, dhnext_sc, dcnext_sc, c_sc, b_sc, dh_sc):
        step = pl.program_id(0)
        i = nt - 1 - step

        @pl.when(step == 0)
        def _():
            anext_sc[...] = jnp.zeros_like(anext_sc)
            dhnext_sc[...] = jnp.zeros_like(dhnext_sc)
            dcnext_sc[...] = jnp.zeros_like(dcnext_sc)
            dwa_ref[...] = jnp.zeros_like(dwa_ref)
            dwx_ref[...] = jnp.zeros_like(dwx_ref)
            stats_ref[...] = jnp.zeros_like(stats_ref)

        first = i == 0
        x = x_ref[...].astype(F32)
        prev8 = jnp.where(first, 0.0, xprev_ref[8:16, :].astype(F32))
        c_b = c_ref[...]
        c, r, gate_i, mult = (ref[...].astype(F32) for ref in (c_ref, r_ref, i_ref, mult_ref))
        a = jnp.exp(la_ref[...].astype(F32))
        sp = _softplus(-lam_ref[...])
        dh_sc[...] = dhl_ref[...].astype(F32)

        b_sc[...] = pltpu.roll(a, TM - 1, axis=0)
        b_sc[TM - 1:TM, :] = anext_sc[0:1, :]
        anext_sc[0:1, :] = a[0:1, :]
        row8 = lax.broadcasted_iota(jnp.int32, (8, D), 0)

        def group(gi, dhnext):
            r0 = pl.multiple_of((per8 - 1 - gi) * 8, 8)
            bb = b_sc[pl.ds(r0, 8), :]
            uu = dh_sc[pl.ds(r0, 8), :]
            for d in (1, 2, 4):
                b_sh = jnp.where(row8 < 8 - d, pltpu.roll(bb, 8 - d, axis=0), 1.0)
                u_sh = jnp.where(row8 < 8 - d, pltpu.roll(uu, 8 - d, axis=0), 0.0)
                uu = uu + bb * u_sh
                bb = bb * b_sh
            dhb = bb * dhnext + uu
            dh_sc[pl.ds(r0, 8), :] = dhb
            return dhb[0:1, :]

        dhfirst = lax.fori_loop(0, per8, group, dhnext_sc[0:1, :])
        dhnext_sc[0:1, :] = dhfirst
        dh = dh_sc[...]

        h = h_ref[...].astype(F32)
        hprev8 = jnp.where(first, 0.0, hprev_ref[8:16, :].astype(F32))
        h_dn, h_head = _shift_down(h, hprev8, 1)
        c_sc[...] = h_dn
        c_sc[0:8, :] = h_head
        h_before = c_sc[...]

        valid = _row_ids(i, TM, (TM, D)) >= PAD_ROWS
        da = dh * h_before
        du = jnp.where(valid, dh, 0.0)
        dmult = du * gate_i * c
        dgate_i = du * mult * c
        dc = du * mult * gate_i
        dla = da * a - dmult * (a * a) / mult
        dla = jnp.where(valid, dla, 0.0)
        dr = dla * ((-LRU_C) * sp)
        dzr = dr * r * (1.0 - r)
        dzi = dgate_i * gate_i * (1.0 - gate_i)
        stats_ref[1:2, :] += jnp.sum(dzr, axis=0, keepdims=True)
        stats_ref[2:3, :] += jnp.sum(dzi, axis=0, keepdims=True)
        stats_ref[3:4, :] += jnp.sum(dla * ((-LRU_C) * r), axis=0, keepdims=True)
        dc_gate = []
        for g in range(LRU_BLOCKS):
            sl = slice(g * LRU_BLOCK, (g + 1) * LRU_BLOCK)
            cg = c_b[:, sl]
            dzr_g = dzr[:, sl].astype(BF16)
            dzi_g = dzi[:, sl].astype(BF16)
            dc_gate.append(_dot_nt(dzr_g, wa_ref[g]) + _dot_nt(dzi_g, wx_ref[g]))
            dwa_ref[g] += _dot_tn(cg, dzr_g)
            dwx_ref[g] += _dot_tn(cg, dzi_g)
        dc = dc + jnp.concatenate(dc_gate, axis=1)

        cw = cw_ref[...]
        stats_ref[0:1, :] += jnp.sum(dc, axis=0, keepdims=True)
        stats_ref[7:8, :] += jnp.sum(dc * x, axis=0, keepdims=True)
        dx = cw[3:4] * dc
        tail_src = jnp.concatenate([dc[TM - 8:TM], dcnext_sc[...]], axis=0)
        dx_tail = cw[3:4] * dc[TM - 8:TM]
        for d in (1, 2, 3):
            dx = dx + cw[3 - d:4 - d] * pltpu.roll(dc, TM - d, axis=0)
            dx_tail = dx_tail + cw[3 - d:4 - d] * pltpu.roll(tail_src, 16 - d, axis=0)[0:8]
            rolled, hd = _shift_down(x, prev8, d)
            b_sc[...] = rolled
            b_sc[0:8, :] = hd
            stats_ref[7 - d:8 - d, :] += jnp.sum(dc * b_sc[...], axis=0, keepdims=True)
        dcnext_sc[...] = dc[0:8]
        dproj_ref[...] = dx.astype(BF16)
        dproj_ref[TM - 8:TM, :] = dx_tail.astype(BF16)

    rev = lambda s: nt - 1 - s
    vec = pl.BlockSpec((1, D), lambda s: (0, 0))
    wspec = pl.BlockSpec((LRU_BLOCKS, LRU_BLOCK, LRU_BLOCK), lambda s: (0, 0, 0))
    prev16 = lambda col: pl.BlockSpec((BF16_ROWS, D), lambda s: (jnp.maximum(rev(s) * (TM // BF16_ROWS) - 1, 0), col))
    tile = lambda: pl.BlockSpec((TM, D), lambda s: (rev(s), 0))
    h_lru, c_sv, r_sv, i_sv, la_sv, mult_sv = saved
    return _hosted_call(
        body,
        name="lru_bwd",
        grid=(nt,),
        in_specs=[
            pl.BlockSpec(memory_space=pl.ANY),
            pl.BlockSpec((TM, D), lambda s: (rev(s), LRU_IN_COL)), prev16(LRU_IN_COL),
            tile(), prev16(0),
            tile(), tile(), tile(), tile(), tile(), tile(),
            pl.BlockSpec((4, D), lambda s: (0, 0)),
            wspec, wspec, vec,
        ],
        out_specs=[
            pl.BlockSpec((TM, D), lambda s: (rev(s), LRU_IN_COL)),
            wspec, wspec,
            pl.BlockSpec((8, D), lambda s: (0, 0)),
        ],
        out_shape=[
            jax.ShapeDtypeStruct(dproj.shape, BF16),
            jax.ShapeDtypeStruct((LRU_BLOCKS, LRU_BLOCK, LRU_BLOCK), F32),
            jax.ShapeDtypeStruct((LRU_BLOCKS, LRU_BLOCK, LRU_BLOCK), F32),
            jax.ShapeDtypeStruct((8, D), F32),
        ],
        aliases={0: 0},
        scratch_shapes=[
            pltpu.VMEM((8, D), F32), pltpu.VMEM((8, D), F32), pltpu.VMEM((8, D), F32),
            pltpu.VMEM((TM, D), F32), pltpu.VMEM((TM, D), F32), pltpu.VMEM((TM, D), F32),
        ],
        args=(dproj, proj, proj, h_lru, h_lru, c_sv, r_sv, i_sv, la_sv, mult_sv, dhl, conv_w, wa, wx, lam),
        riders=riders,
    )


def _in_proj_bwd(dproj, w_in, part, prev=None, riders=()):
    rows = dproj.shape[0]
    tm = _heavy_tile(rows)
    nt = rows // tm
    split = max(1, nt // 4 + 1) if nt > 1 else 1
    first = 0 if part == 0 else split
    count = split if part == 0 else nt - split

    def body(*refs):
        dproj_ref, w_hbm, du_ref, w_sc, w_sem = refs[-5:]

        @pl.when(pl.program_id(0) == 0)
        def _():
            moves = [((0, 3), 0), ((4, 1), 3), ((3, 1), 4), ((5, 3), 5)]
            copies = [pltpu.make_async_copy(w_hbm.at[:, pl.ds(src * D, n * D)], w_sc.at[:, pl.ds(dst * D, n * D)], w_sem.at[q])
                      for q, ((src, n), dst) in enumerate(moves)]
            for cp in copies:
                cp.start()
            for cp in copies:
                cp.wait()

        for a in range(0, D, D // 2):
            du_ref[:, a:a + D // 2] = _dot_nt(dproj_ref[...], w_sc[a:a + D // 2, :])

    in_specs = [pl.BlockSpec((tm, IN_COLS), lambda i: (first + i, 0)), _ANY]
    args = (dproj, w_in)
    if prev is not None:
        in_specs = [_ANY] + in_specs
        args = (prev,) + args
    return _hosted_call(
        body,
        name="in_proj_bwd_%d" % part,
        grid=(count,),
        in_specs=in_specs,
        out_specs=[pl.BlockSpec((tm, D), lambda i: (first + i, 0))],
        out_shape=[jax.ShapeDtypeStruct((rows, D), F32)],
        scratch_shapes=[pltpu.VMEM((D, IN_COLS), BF16), pltpu.SemaphoreType.DMA((4,))],
        aliases={0: 0} if prev is not None else None,
        args=args,
        riders=riders,
    )


def _norm1_bwd(du, dh1, head, x2d, rstd1, norm_w, riders=()):
    rows = du.shape[0]

    def body(du_ref, dh1_ref, head_ref, x_ref, rstd_ref, nw_ref, gx_ref, ghead_ref, stats_ref):
        i = pl.program_id(0)

        @pl.when(i == 0)
        def _():
            stats_ref[...] = jnp.zeros_like(stats_ref)

        def finish(h0, out_ref):
            du = du_ref[...]
            rs = rstd_ref[...]
            n = h0 * rs
            stats_ref[0:1, :] += jnp.sum(du * n, axis=0, keepdims=True)
            dn = du * nw_ref[...]
            out_ref[...] = dh1_ref[...] + rs * (dn - n * jnp.mean(dn * n, axis=-1, keepdims=True))

        @pl.when(i == 0)
        def _():
            finish(head_ref[...], ghead_ref)

        @pl.when(i > 0)
        def _():
            finish(x_ref[...], gx_ref)

    tile = pl.BlockSpec((TM, D), lambda i: (i, 0))
    return _hosted_call(
        body,
        name="norm1_bwd",
        grid=(rows // TM,),
        in_specs=[
            tile, tile,
            pl.BlockSpec((FRONT, D), lambda i: (0, 0)),
            pl.BlockSpec((TM, D), lambda i: (jnp.maximum(i - 1, 0), 0)),
            pl.BlockSpec((TM, 1), lambda i: (i, 0)),
            pl.BlockSpec((1, D), lambda i: (0, 0)),
        ],
        out_specs=[
            pl.BlockSpec((TM, D), lambda i: (jnp.maximum(i - 1, 0), 0)),
            pl.BlockSpec((FRONT, D), lambda i: (0, 0)),
            pl.BlockSpec((8, D), lambda i: (0, 0)),
        ],
        out_shape=[
            jax.ShapeDtypeStruct(x2d.shape, F32),
            jax.ShapeDtypeStruct((FRONT, D), F32),
            jax.ShapeDtypeStruct((8, D), F32),
        ],
        scratch_shapes=[],
        args=(du, dh1, head, x2d, rstd1, norm_w),
        riders=riders,
    )


def _matmul_tn(name, x, dy, out_cols, col0=0, prev=None, k_block=None, n_block=None, riders=(), col_map=None):
    col_map = col_map or (lambda n: n)
    rows, kdim = x.shape
    ndim = dy.shape[1]
    kb = k_block or kdim
    nb = n_block or ndim
    step_bytes = lambda t: 2 * t * (kb * x.dtype.itemsize + nb * dy.dtype.itemsize) + 2 * kb * nb * 4
    tr = next(t for t in (2816, 1408, TM_HEAVY, TM) if rows % t == 0 and (t == TM or step_bytes(t) <= TN_VMEM_BUDGET))
    nr, nk, nn = rows // tr, kdim // kb, ndim // nb
    cb0 = col0 // nb

    def body(*refs):
        x_ref, dy_ref, out_ref = refs[-3], refs[-2], refs[-1]
        cut = (nb // 2) // LANES * LANES
        halves = ((0, cut), (cut, nb)) if cut else ((0, nb),)

        @pl.when(pl.program_id(2) == 0)
        def _():
            for a, b in halves:
                out_ref[:, a:b] = _dot_tn(x_ref[...].astype(BF16), dy_ref[:, a:b].astype(BF16))

        @pl.when(pl.program_id(2) > 0)
        def _():
            for a, b in halves:
                out_ref[:, a:b] += _dot_tn(x_ref[...].astype(BF16), dy_ref[:, a:b].astype(BF16))

    in_specs = [
        pl.BlockSpec((tr, kb), lambda n, k, r: (r, k)),
        pl.BlockSpec((tr, nb), lambda n, k, r: (r, n)),
    ]
    args = [x, dy]
    aliases = {}
    if prev is not None:
        in_specs = [pl.BlockSpec(memory_space=pl.ANY)] + in_specs
        args = [prev] + args
        aliases = {0: 0}
    (out,), rider_outs = _hosted_call(
        body,
        name=name,
        grid=(nn, nk, nr),
        in_specs=in_specs,
        out_specs=[pl.BlockSpec((kb, nb), lambda n, k, r: (k, cb0 + col_map(n)))],
        out_shape=[jax.ShapeDtypeStruct((kdim, out_cols), F32)],
        scratch_shapes=[],
        aliases=aliases,
        args=args,
        riders=riders,
    )
    return (out, rider_outs) if riders else out


def _local_step(x2d, target, w, plan):
    rows = FRONT + x2d.shape[0]
    head = jnp.concatenate([jnp.zeros((PAD_ROWS, D), F32), w["meta_tokens"]], axis=0)
    cos_t, sin_t = _rope_tables(rows)
    intra, intra_t, q_dec, k_dec, c_dec = _decay_tables()
    grads = {}

    def hosted(host, fn, *args, **kwargs):
        outs, rider_outs = fn(*args, riders=plan.riders(host, w, grads), **kwargs)
        plan.after(host, rider_outs, w, grads)
        return outs

    (u1, rstd1), _ = _norm1(head, x2d, w["mix_norm_w"])
    proj, w["w_in"] = hosted("in_proj", _in_proj, u1, w["w_in_shard"], w["route"], cos_t, sin_t)
    o, sprev = hosted("retention_fwd", _retention_fwd, proj, intra, q_dec, k_dec, c_dec)
    lru_args = (w["conv_w"], w["conv_b"], w["lru_wa"], w["lru_wx"], w["lru_ba"], w["lru_bx"], w["lru_lambda"])
    lru_saved = hosted("lru_fwd", _lru_fwd, proj, *lru_args)
    h_lru = lru_saved[0]
    y_ret, y_lru, h1, u2, rstd2 = hosted("mix_fwd", _mix_fwd, head, x2d, o, proj, h_lru, w["w_branch_ret"],
                                         w["w_branch_lru"], w["w_out"], w["ffn_norm_w"])
    silu, dsilu, act = hosted("ffn_fwd_gate", _ffn_fwd_gate, u2, w["w_ffn_in"])
    dh2, stats_loss = _ffn_out_loss(act, h1, w["w_ffn_out"], target, w["final_norm_w"])

    dg, dup, dh2b, dh1, stats_ffn = _ffn_bwd(dh2, silu, dsilu, h1, rstd2, w["w_ffn_in"], w["w_ffn_out"], w["ffn_norm_w"])
    grads["w_ffn_in"] = _matmul_tn("dw_ffn_up", u2, dup, 2 * FFN, col0=FFN, n_block=FFN_HALF,
                                   prev=_matmul_tn("dw_ffn_gate", u2, dg, 2 * FFN, n_block=FFN_HALF))
    grads["w_ffn_out"] = _matmul_tn("dw_ffn_out", act, dh2b, D, k_block=FFN_HALF)
    (dproj, do, dhl, mixed, a_ret, a_lru, dy_ret, dy_lru) = hosted(
        "mix_bwd", _mix_bwd, dh1, o, proj, h_lru, y_ret, y_lru, w["w_branch_ret"], w["w_branch_lru"], w["w_out"])
    grads["w_out"] = _matmul_tn("dw_out", mixed, dh1, D)
    grads["w_branch_ret"] = _matmul_tn("dw_branch_ret", a_ret, dy_ret, D)
    grads["w_branch_lru"] = _matmul_tn("dw_branch_lru", a_lru, dy_lru, D)
    (dproj,) = hosted("retention_bwd", _retention_bwd, dproj, proj, do, sprev, intra, intra_t, q_dec, k_dec, c_dec,
                      cos_t, sin_t)
    dproj, grads["lru_wa"], grads["lru_wx"], stats_lru = hosted(
        "lru_bwd", _lru_bwd, dproj, proj, lru_saved, dhl, w["conv_w"], w["lru_wa"], w["lru_wx"], w["lru_lambda"])
    grads["w_in"], rider_outs = _matmul_tn("dw_in", u1, dproj, IN_COLS, n_block=D, col_map=_swap_3_4,
                                           riders=plan.riders("dw_in", w, grads))
    plan.after("dw_in", rider_outs, w, grads)
    (du1,) = hosted("in_proj_bwd_0", _in_proj_bwd, dproj, w["w_in"], 0)
    (du1,) = hosted("in_proj_bwd_1", _in_proj_bwd, dproj, w["w_in"], 1, du1)
    (grad_x, grad_head, stats_in), _ = _norm1_bwd(du1, dh1, head, x2d, rstd1, w["mix_norm_w"])
    return grad_x, grad_head, grads, [stats_loss, stats_ffn, stats_in, stats_lru]


BIG_PIECES = {
    "w_in": ("col", (D, 2 * D)),
    "w_ffn_in": ("col", (D, FFN_HALF)),
    "w_ffn_out": ("row", (FFN // 4, D)),
    "w_branch_ret": ("row", (D // 4, D)),
    "w_branch_lru": ("row", (D // 4, D)),
    "w_out": ("row", (D // 4, D)),
    "lru_wa": ("lru", (LRU_BLOCKS, LRU_BLOCK // 4, LRU_BLOCK)),
    "lru_wx": ("lru", (LRU_BLOCKS, LRU_BLOCK // 4, LRU_BLOCK)),
}
SMALL_PIECES = {"meta_tokens": ("col", (N_META, D // 4)), "conv_w": ("col", (4, D // 4))}


def _full_shape(kind, shard):
    if kind == "col":
        return (shard[0], 4 * shard[1])
    if kind == "row":
        return (4 * shard[0], shard[1])
    return (shard[0], 4 * shard[1], shard[2])


def _half_shape(kind, shard):
    return (shard[0] // 2,) + tuple(shard[1:])


def _aligned(start, multiple):
    return start if isinstance(start, int) else pl.multiple_of(start, multiple)


def _lead(h, size):
    if h is None:
        return pl.ds(0, size)
    return pl.ds(_aligned(h * (size // 2), size // 2), size // 2)


def _full_region(ref, kind, shard, s, h):
    if kind == "col":
        return ref.at[_lead(h, shard[0]), pl.ds(_aligned(s * shard[1], shard[1]), shard[1])]
    if kind == "row":
        size = shard[0] if h is None else shard[0] // 2
        start = s * shard[0] + (0 if h is None else h * (shard[0] // 2))
        return ref.at[pl.ds(_aligned(start, BF16_ROWS), size), :]
    return ref.at[_lead(h, shard[0]), pl.ds(_aligned(s * shard[1], shard[1]), shard[1]), :]


def _shard_region(ref, shard, h):
    return ref.at[_lead(h, shard[0])]


def _place():
    x, y, c = lax.axis_index("x"), lax.axis_index("y"), lax.axis_index("c")
    return x, y, c, 2 * x + y


def _other_chip(s, c, k):
    s2 = jnp.bitwise_xor(s, k)
    return s2, (s2 // 2, s2 % 2, c)


def _remote(src, dst, send_sem, recv_sem, dev):
    return pltpu.make_async_remote_copy(src_ref=src, dst_ref=dst, send_sem=send_sem, recv_sem=recv_sem,
                                        device_id=dev, device_id_type=MESH)


_ANY = pl.BlockSpec(memory_space=pl.ANY)


class _Rider:
    def __init__(self, ins, out_shapes, sem_shapes, build, aliased=False):
        self.ins, self.out_shapes, self.sem_shapes, self.build, self.aliased = ins, out_shapes, sem_shapes, build, aliased


def _hosted_call(body, *, name, grid, in_specs, out_specs, out_shape, scratch_shapes, args, riders=(), aliases=None,
                 prefetch=None, riders_after_body=False):
    n_in, n_out, n_sc = len(in_specs), len(out_shape), len(scratch_shapes)
    r_in = [a for r in riders for a in r.ins]
    r_out = [s for r in riders for s in r.out_shapes]
    r_sem = [s for r in riders for s in r.sem_shapes]
    lead = () if prefetch is None else (prefetch,)
    assert prefetch is None or not (aliases or any(r.aliased for r in riders))

    def full_body(*refs):
        head, refs = refs[:len(lead)], refs[len(lead):]
        ins, rin = refs[:n_in], refs[n_in:n_in + len(r_in)]
        o0 = n_in + len(r_in)
        outs, rout = refs[o0:o0 + n_out], refs[o0 + n_out:o0 + n_out + len(r_out)]
        s0 = o0 + n_out + len(r_out)
        scratch, rsem = refs[s0:s0 + n_sc], refs[s0 + n_sc:]
        starts, waits = [], []
        pi = po = ps = 0
        for r in riders:
            st, wt = r.build(rin[pi:pi + len(r.ins)], rout[po:po + len(r.out_shapes)], rsem[ps:ps + len(r.sem_shapes)])
            starts += st
            waits += wt
            pi, po, ps = pi + len(r.ins), po + len(r.out_shapes), ps + len(r.sem_shapes)
        first = functools.reduce(jnp.logical_and, [pl.program_id(d) == 0 for d in range(len(grid))])
        last = functools.reduce(jnp.logical_and, [pl.program_id(d) == grid[d] - 1 for d in range(len(grid))])

        def start_riders():
            @pl.when(first)
            def _():
                for cp in starts:
                    cp.start()

        if riders and not riders_after_body:
            start_riders()
        body(*head, *ins, *outs, *scratch)
        if riders and riders_after_body:
            start_riders()
        if riders:
            @pl.when(last)
            def _():
                for wait in waits:
                    wait()

    io_aliases = dict(aliases or {})
    pi = po = 0
    for r in riders:
        if r.aliased:
            for q in range(len(r.ins)):
                io_aliases[n_in + pi + q] = n_out + po + q
        pi, po = pi + len(r.ins), po + len(r.out_shapes)
    specs = dict(
        grid=grid,
        in_specs=list(in_specs) + [_ANY] * len(r_in),
        out_specs=list(out_specs) + [_ANY] * len(r_out),
        scratch_shapes=list(scratch_shapes) + r_sem,
    )
    if prefetch is not None:
        specs = dict(grid_spec=pltpu.PrefetchScalarGridSpec(num_scalar_prefetch=1, **specs))
    res = pl.pallas_call(
        full_body,
        name=name,
        out_shape=list(out_shape) + r_out,
        input_output_aliases=io_aliases,
        compiler_params=_params(("arbitrary",) * len(grid)),
        **specs,
    )(*lead, *args, *r_in)
    rider_outs, po = [], n_out
    for r in riders:
        rider_outs.append(list(res[po:po + len(r.out_shapes)]))
        po += len(r.out_shapes)
    return list(res[:n_out]), rider_outs


def _run_riders(name, riders):
    r_in = [a for r in riders for a in r.ins]
    r_out = [s for r in riders for s in r.out_shapes]
    r_sem = [s for r in riders for s in r.sem_shapes]

    def body(*refs):
        rin, rout, rsem = refs[:len(r_in)], refs[len(r_in):len(r_in) + len(r_out)], refs[len(r_in) + len(r_out):]
        pi = po = ps = 0
        all_waits = []
        for r in riders:
            starts, waits = r.build(rin[pi:pi + len(r.ins)], rout[po:po + len(r.out_shapes)], rsem[ps:ps + len(r.sem_shapes)])
            for cp in starts:
                cp.start()
            all_waits += waits
            pi, po, ps = pi + len(r.ins), po + len(r.out_shapes), ps + len(r.sem_shapes)
        for wait in all_waits:
            wait()

    io_aliases = {}
    pi = po = 0
    for r in riders:
        if r.aliased:
            for q in range(len(r.ins)):
                io_aliases[pi + q] = po + q
        pi, po = pi + len(r.ins), po + len(r.out_shapes)
    res = pl.pallas_call(
        body,
        name=name,
        in_specs=[_ANY] * len(r_in),
        out_specs=[_ANY] * len(r_out),
        out_shape=r_out,
        scratch_shapes=r_sem,
        input_output_aliases=io_aliases,
    )(*r_in)
    outs, po = [], 0
    for r in riders:
        outs.append(list(res[po:po + len(r.out_shapes)]))
        po += len(r.out_shapes)
    return outs


def _piece(name):
    if name in BIG_PIECES:
        return (name, *BIG_PIECES[name], True)
    return (name, *SMALL_PIECES[name], False)


def _gather_rider(shards, names):
    pieces = [_piece(n) for n in names]
    n = len(pieces)

    def build(ins, outs, sems):
        local_sem, ici_send, ici_recv = sems
        _, _, c, s = _place()
        starts, waits = [], []
        for p, (_, kind, shard, split) in enumerate(pieces):
            cp = pltpu.make_async_copy(ins[p], _full_region(outs[p], kind, shard, s, None), local_sem.at[p])
            starts.append(cp)
            waits.append(cp.wait)
            h = c if split else None
            for k in (1, 2, 3):
                s2, dev = _other_chip(s, c, k)
                cp = _remote(_shard_region(ins[p], shard, h), _full_region(outs[p], kind, shard, s, h),
                             ici_send.at[p, k - 1], ici_recv.at[p, k - 1], dev)
                starts.append(cp)
                waits.append(cp.wait_send)
                region = _full_region(outs[p], kind, shard, s2, h)
                waits.append(_remote(region, region, ici_send.at[p, k - 1], ici_recv.at[p, k - 1], dev).wait_recv)
        return starts, waits

    return _Rider(
        [shards[name] for name in names],
        [jax.ShapeDtypeStruct(_full_shape(kind, shard), shards[name].dtype) for name, kind, shard, _ in pieces],
        [pltpu.SemaphoreType.DMA((n,)), pltpu.SemaphoreType.DMA((n, 3)), pltpu.SemaphoreType.DMA((n, 3))],
        build)


def _forward_rider(gathered, names):
    pieces = [_piece(n) for n in names]
    n = len(pieces)

    def build(ins, outs, sems):
        fwd_send, fwd_recv = sems
        x, y, c, s = _place()
        sibling = (x, y, 1 - c)
        starts, waits = [], []
        for p, (_, kind, shard, _) in enumerate(pieces):
            for k in (1, 2, 3):
                s2, _ = _other_chip(s, c, k)
                mine = _full_region(outs[p], kind, shard, s2, c)
                theirs = _full_region(outs[p], kind, shard, s2, 1 - c)
                cp = _remote(mine, mine, fwd_send.at[p, k - 1], fwd_recv.at[p, k - 1], sibling)
                starts.append(cp)
                waits.append(cp.wait_send)
                waits.append(_remote(theirs, theirs, fwd_send.at[p, k - 1], fwd_recv.at[p, k - 1], sibling).wait_recv)
        return starts, waits

    return _Rider(
        [gathered[name] for name in names],
        [jax.ShapeDtypeStruct(gathered[name].shape, gathered[name].dtype) for name in names],
        [pltpu.SemaphoreType.DMA((n, 3)), pltpu.SemaphoreType.DMA((n, 3))],
        build, aliased=True)


def _pair_exchange_rider(grads, names):
    n = len(names)

    def build(ins, outs, sems):
        send_sem, recv_sem = sems
        x, y, c, _ = _place()
        sibling = (x, y, 1 - c)
        starts, waits = [], []
        for p, name in enumerate(names):
            kind, shard = BIG_PIECES[name]
            for s2 in range(4):
                cp = _remote(_full_region(ins[p], kind, shard, s2, 1 - c), outs[p].at[s2], send_sem.at[p, s2],
                             recv_sem.at[p, s2], sibling)
                starts.append(cp)
                waits.append(cp.wait_send)
                waits.append(_remote(outs[p].at[s2], outs[p].at[s2], send_sem.at[p, s2], recv_sem.at[p, s2], sibling).wait_recv)
        return starts, waits

    return _Rider(
        [grads[name] for name in names],
        [jax.ShapeDtypeStruct((4,) + _half_shape(*BIG_PIECES[name]), F32) for name in names],
        [pltpu.SemaphoreType.DMA((n, 4))] * 2,
        build)


def _half_specs(kind, shard):
    half = shard[0] // 2
    if kind == "col":
        full = pl.BlockSpec((half, shard[1]), lambda j, pr: (pr[1], j))
        buf = pl.BlockSpec((None, half, shard[1]), lambda j, pr: (j, 0, 0))
    elif kind == "row":
        full = pl.BlockSpec((half, shard[1]), lambda j, pr: (2 * j + pr[1], 0))
        buf = pl.BlockSpec((None, half, shard[1]), lambda j, pr: (j, 0, 0))
    else:
        full = pl.BlockSpec((half, shard[1], shard[2]), lambda j, pr: (pr[1], j, 0))
        buf = pl.BlockSpec((None, half, shard[1], shard[2]), lambda j, pr: (j, 0, 0, 0))
    return full, buf


def _pair_sum(name, grad, recv, place):
    kind, shard = BIG_PIECES[name]
    full, buf = _half_specs(kind, shard)

    def body(pr, g_ref, r_ref, o_ref):
        o_ref[...] = (g_ref[...] + r_ref[...]).astype(BF16)

    return pl.pallas_call(
        body,
        name="pair_sum_" + name,
        grid_spec=pltpu.PrefetchScalarGridSpec(num_scalar_prefetch=1, grid=(4,), in_specs=[full, buf], out_specs=buf),
        out_shape=jax.ShapeDtypeStruct((4,) + _half_shape(kind, shard), BF16),
        compiler_params=_params(("arbitrary",)),
    )(place, grad, recv)


def _chip_exchange_rider(sums, names):
    n = len(names)

    def build(ins, outs, sems):
        send_sem, recv_sem = sems
        _, _, c, s = _place()
        starts, waits = [], []
        for p in range(n):
            for k in (1, 2, 3):
                s2, dev = _other_chip(s, c, k)
                cp = _remote(ins[p].at[s2], outs[p].at[k - 1], send_sem.at[p, k - 1], recv_sem.at[p, k - 1], dev)
                starts.append(cp)
                waits.append(cp.wait_send)
                waits.append(_remote(outs[p].at[k - 1], outs[p].at[k - 1], send_sem.at[p, k - 1], recv_sem.at[p, k - 1],
                                     dev).wait_recv)
        return starts, waits

    return _Rider(
        [sums[name] for name in names],
        [jax.ShapeDtypeStruct((3,) + _half_shape(*BIG_PIECES[name]), BF16) for name in names],
        [pltpu.SemaphoreType.DMA((n, 3))] * 2,
        build)


def _chip_sum(name, grad, recv_pair, recv_chip, place):
    kind, shard = BIG_PIECES[name]
    half = shard[0] // 2
    tail = tuple(shard[1:])
    zeros = (0,) * len(tail)
    nt = 4 if kind != "lru" and half % (4 * BF16_ROWS) == 0 else 1
    rows = half // nt
    if kind == "col":
        full = pl.BlockSpec((rows,) + tail, lambda j, pr: (pr[1] * nt + j, pr[0]))
    elif kind == "row":
        full = pl.BlockSpec((rows,) + tail, lambda j, pr: ((2 * pr[0] + pr[1]) * nt + j, 0))
    else:
        full = pl.BlockSpec((rows,) + tail, lambda j, pr: (pr[1], pr[0], 0))
    pair = pl.BlockSpec((None, rows) + tail, lambda j, pr: (pr[0], j) + zeros)
    chip = pl.BlockSpec((3, rows) + tail, lambda j, pr: (0, j) + zeros)
    out = pl.BlockSpec((rows,) + tail, lambda j, pr: (pr[1] * nt + j,) + zeros)

    def body(pr, g_ref, rp_ref, rc_ref, o_ref):
        total = g_ref[...] + rp_ref[...]
        for k in range(3):
            total = total + rc_ref[k].astype(F32)
        o_ref[...] = total

    return pl.pallas_call(
        body,
        name="chip_sum_" + name,
        grid_spec=pltpu.PrefetchScalarGridSpec(num_scalar_prefetch=1, grid=(nt,), in_specs=[full, pair, chip], out_specs=out),
        out_shape=jax.ShapeDtypeStruct(shard, F32),
        compiler_params=_params(("arbitrary",)),
    )(place, grad, recv_pair, recv_chip)


def _sibling_exchange_rider(halves, names):
    n = len(names)

    def build(ins, outs, sems):
        send_sem, recv_sem = sems
        x, y, c, _ = _place()
        sibling = (x, y, 1 - c)
        starts, waits = [], []
        for p, name in enumerate(names):
            shard = BIG_PIECES[name][1]
            mine = _shard_region(outs[p], shard, c)
            theirs = _shard_region(outs[p], shard, 1 - c)
            cp = _remote(mine, mine, send_sem.at[p], recv_sem.at[p], sibling)
            starts.append(cp)
            waits.append(cp.wait_send)
            waits.append(_remote(theirs, theirs, send_sem.at[p], recv_sem.at[p], sibling).wait_recv)
        return starts, waits

    return _Rider(
        [halves[name] for name in names],
        [jax.ShapeDtypeStruct(BIG_PIECES[name][1], F32) for name in names],
        [pltpu.SemaphoreType.DMA((n,))] * 2,
        build, aliased=True)


FIRST_WEIGHTS = ["meta_tokens", "conv_w"]
WEIGHT_GROUPS = {
    "lru": ["lru_wa", "lru_wx"],
    "branch": ["w_branch_ret", "w_branch_lru", "w_out"],
    "ffn_in": ["w_ffn_in"],
    "ffn_out": ["w_ffn_out"],
}
WEIGHT_SCHEDULE = {
    "in_proj": [("gather", "lru")],
    "retention_fwd": [("forward", "lru"), ("gather", "branch")],
    "lru_fwd": [("forward", "branch"), ("gather", "ffn_in")],
    "mix_fwd": [("forward", "ffn_in"), ("gather", "ffn_out")],
    "ffn_fwd_gate": [("forward", "ffn_out")],
}
GRAD_GROUPS = {
    "ffn_in": ["w_ffn_in"],
    "ffn_out": ["w_ffn_out"],
    "mixer": ["w_out", "w_branch_ret", "w_branch_lru", "lru_wa", "lru_wx"],
    "in": ["w_in"],
}
GRAD_SCHEDULE = {
    "mix_bwd": [("pair", "ffn_in"), ("pair", "ffn_out")],
    "retention_bwd": [("chip", "ffn_out")],
    "lru_bwd": [("chip", "ffn_in"), ("sibling", "ffn_out")],
    "dw_in": [("sibling", "ffn_in"), ("pair", "mixer")],
    "in_proj_bwd_0": [("chip", "mixer"), ("pair", "in")],
    "in_proj_bwd_1": [("sibling", "mixer"), ("chip", "in")],
}


class _CommPlan:
    def __init__(self, shards, place):
        self.shards, self.place = shards, place
        self.late = {}
        self.recv_pair, self.sums, self.recv_chip, self.halves, self.final = {}, {}, {}, {}, {}

    def _grad_rider(self, stage, group, grads):
        names = GRAD_GROUPS[group]
        if stage == "pair":
            return _pair_exchange_rider(grads, names)
        if stage == "chip":
            return _chip_exchange_rider(self.sums, names)
        return _sibling_exchange_rider(self.halves, names)

    def _grad_after(self, stage, group, outs, grads):
        names = GRAD_GROUPS[group]
        if stage == "pair":
            for n, o in zip(names, outs):
                self.recv_pair[n] = o
                self.sums[n] = _pair_sum(n, grads[n], o, self.place)
        elif stage == "chip":
            for n, o in zip(names, outs):
                self.halves[n] = _chip_sum(n, grads[n], self.recv_pair[n], o, self.place)
        else:
            self.final.update(zip(names, outs))

    def riders(self, host, w, grads):
        if host in WEIGHT_SCHEDULE:
            return [_gather_rider(self.shards, WEIGHT_GROUPS[group]) if stage == "gather"
                    else _forward_rider(self.late, WEIGHT_GROUPS[group]) for stage, group in WEIGHT_SCHEDULE[host]]
        return [self._grad_rider(stage, group, grads) for stage, group in GRAD_SCHEDULE.get(host, [])]

    def after(self, host, rider_outs, w, grads):
        for (stage, group), outs in zip(WEIGHT_SCHEDULE.get(host, []), rider_outs):
            (self.late if stage == "gather" else w).update(zip(WEIGHT_GROUPS[group], outs))
        for (stage, group), outs in zip(GRAD_SCHEDULE.get(host, []), rider_outs):
            self._grad_after(stage, group, outs, grads)

    def finish(self, partial):
        outs, (blocks,) = _run_riders("tail_exchange", [_sibling_exchange_rider(self.halves, GRAD_GROUPS["in"]),
                                                        _small_exchange_rider(partial)])
        self.final.update(zip(GRAD_GROUPS["in"], outs))
        return self.final, blocks


def _adamw_math(w, g, m, v):
    m = ADAM_B1 * m + (1.0 - ADAM_B1) * g
    v = ADAM_B2 * v + (1.0 - ADAM_B2) * (g * g)
    m_hat = m / (1.0 - ADAM_B1 ** ADAM_STEP)
    v_hat = v / (1.0 - ADAM_B2 ** ADAM_STEP)
    delta = -ADAM_LR * (m_hat / (jnp.sqrt(v_hat) + ADAM_EPS) + ADAM_WD * w)
    return delta, m, v


def _adamw(name, g, w, m, v):
    rows, cols = g.shape
    tr = rows // 4 if rows % 32 == 0 else rows

    def body(g_ref, w_ref, m_ref, v_ref, go_ref, d_ref, mo_ref, vo_ref):
        gv = g_ref[...]
        delta, m2, v2 = _adamw_math(w_ref[...], gv, m_ref[...], v_ref[...])
        go_ref[...] = gv
        d_ref[...] = delta
        mo_ref[...] = m2
        vo_ref[...] = v2

    spec = pl.BlockSpec((tr, cols), lambda i: (i, 0))
    return pl.pallas_call(
        body,
        name="adamw_" + name,
        grid=(rows // tr,),
        in_specs=[spec] * 4,
        out_specs=[spec] * 4,
        out_shape=[jax.ShapeDtypeStruct((rows, cols), F32)] * 4,
        compiler_params=_params(("arbitrary",)),
    )(g, w, m, v)


SMALL_ROWS = 48
VEC_ROWS = {"final_norm_w": 1, "ffn_norm_w": 8, "mix_norm_w": 16, "conv_b": 24, "lru_ba": 25, "lru_bx": 26, "lru_lambda": 27}
VEC_NAMES = list(VEC_ROWS)
CONV_W_ROW = 28
META_ROW = 32


def _small_exchange_rider(partial):
    def build(ins, outs, sems):
        local_sem, send_sem, recv_sem = sems
        _, _, c, s = _place()
        me = 2 * s + c
        cp = pltpu.make_async_copy(ins[0], outs[0].at[me], local_sem)
        starts, waits = [cp], [cp.wait]
        for k in range(1, 8):
            peer = jnp.bitwise_xor(me, k)
            dev = (peer // 4, (peer // 2) % 2, peer % 2)
            rd = _remote(ins[0], outs[0].at[me], send_sem.at[k - 1], recv_sem.at[k - 1], dev)
            starts.append(rd)
            waits.append(rd.wait_send)
            waits.append(_remote(ins[0], outs[0].at[peer], send_sem.at[k - 1], recv_sem.at[k - 1], dev).wait_recv)
        return starts, waits

    return _Rider([partial], [jax.ShapeDtypeStruct((8, SMALL_ROWS, D), F32)],
                  [pltpu.SemaphoreType.DMA, pltpu.SemaphoreType.DMA((7,)), pltpu.SemaphoreType.DMA((7,))], build)


def _small_update(blocks, place, vecs, conv, meta):
    nvec = len(VEC_NAMES)
    qcols = D // 4

    def in_order(ref):
        total = ref[0]
        for d in range(1, 8):
            total = total + ref[d]
        return total

    def body(pr, blocks_ref, cols_ref, *refs):
        vec_refs = refs[:3 * nvec]
        conv_refs = refs[3 * nvec:3 * nvec + 3]
        meta_refs = refs[3 * nvec + 3:3 * nvec + 6]
        outs = refs[3 * nvec + 6:]
        loss_ref, vec_out, conv_out, meta_out = outs[0], outs[1:5], outs[5:9], outs[9:13]
        tot, col = in_order(blocks_ref), in_order(cols_ref)
        loss_ref[...] = jnp.sum(tot[0:1, :], axis=1, keepdims=True)
        for o in vec_out:
            o[...] = jnp.zeros_like(o)
        for j, name in enumerate(VEC_NAMES):
            w, m, v = (r[...] for r in vec_refs[3 * j:3 * j + 3])
            g = tot[VEC_ROWS[name]:VEC_ROWS[name] + 1, :]
            if name == "lru_lambda":
                g = -g / (1.0 + jnp.exp(w))
            for o, val in zip(vec_out, (g,) + _adamw_math(w, g, m, v)):
                o[j:j + 1, :] = val
        for row, n_rows, ins, group in ((CONV_W_ROW, 4, conv_refs, conv_out), (META_ROW, N_META, meta_refs, meta_out)):
            g = col[row:row + n_rows, :]
            w, m, v = (r[...] for r in ins)
            for o, val in zip(group, (g,) + _adamw_math(w, g, m, v)):
                o[...] = val

    whole = lambda shape: pl.BlockSpec(shape, lambda i, pr: (0,) * len(shape))
    in_specs = [whole((8, SMALL_ROWS, D)), pl.BlockSpec((8, SMALL_ROWS, qcols), lambda i, pr: (0, 0, pr[0]))]
    in_specs += [whole((1, D))] * (3 * nvec) + [whole((4, qcols))] * 3 + [whole((N_META, qcols))] * 3
    out_shapes = [(1, 1)] + [(8, D)] * 4 + [(4, qcols)] * 4 + [(N_META, qcols)] * 4
    return pl.pallas_call(
        body,
        name="small_update",
        grid_spec=pltpu.PrefetchScalarGridSpec(num_scalar_prefetch=1, grid=(1,), in_specs=in_specs,
                                               out_specs=[whole(s) for s in out_shapes]),
        out_shape=[jax.ShapeDtypeStruct(s, F32) for s in out_shapes],
        compiler_params=_params(("arbitrary",)),
    )(place, blocks, blocks, *[a for t in vecs for a in t], *conv, *meta)


WEIGHT_ORDER = ["meta_tokens", "mix_norm_w", "w_in", "conv_w", "conv_b", "lru_wa", "lru_ba", "lru_wx", "lru_bx", "lru_lambda",
                "w_branch_ret", "w_branch_lru", "w_out", "ffn_norm_w", "w_ffn_in", "w_ffn_out", "final_norm_w"]


def kernel(x, meta_tokens, mix_norm_w, w_in, conv_w, conv_b, lru_wa, lru_ba, lru_wx, lru_bx, lru_lambda, w_branch_ret, w_branch_lru, w_out, ffn_norm_w, w_ffn_in, w_ffn_out, final_norm_w, loss_target, m_meta_tokens, m_mix_norm_w, m_w_in, m_conv_w, m_conv_b, m_lru_wa, m_lru_ba, m_lru_wx, m_lru_bx, m_lru_lambda, m_w_branch_ret, m_w_branch_lru, m_w_out, m_ffn_norm_w, m_w_ffn_in, m_w_ffn_out, m_final_norm_w, v_meta_tokens, v_mix_norm_w, v_w_in, v_conv_w, v_conv_b, v_lru_wa, v_lru_ba, v_lru_wx, v_lru_bx, v_lru_lambda, v_w_branch_ret, v_w_branch_lru, v_w_out, v_ffn_norm_w, v_w_ffn_in, v_w_ffn_out, v_final_norm_w):
    args = locals()
    wts = {n: args[n] for n in WEIGHT_ORDER}
    mom = {n: args["m_" + n] for n in WEIGHT_ORDER}
    var = {n: args["v_" + n] for n in WEIGHT_ORDER}
    place = jnp.stack([2 * lax.axis_index("x") + lax.axis_index("y"), lax.axis_index("c")]).astype(jnp.int32)

    shards = {n: wts[n][0].astype(BF16) for n in BIG_PIECES}
    shards["meta_tokens"] = wts["meta_tokens"]
    shards["conv_w"] = wts["conv_w"][0]
    plan = _CommPlan(shards, place)
    (first,) = _run_riders("gather_first", [_gather_rider(shards, FIRST_WEIGHTS)])
    w = dict(zip(FIRST_WEIGHTS, first))
    for n in VEC_NAMES:
        w[n] = wts[n].reshape(1, D)
    chips = jnp.bitwise_xor(place[0], jnp.array((0,) + SHARD_ORDER, dtype=jnp.int32))
    w["route"] = jnp.concatenate([place, jnp.stack([2 * chips, 2 * chips + 1], axis=1).reshape(8)])
    w["w_in_shard"] = shards["w_in"]

    grad_x, grad_head, _, stats = _local_step(x[0], loss_target[0], w, plan)
    partial = jnp.concatenate(stats + [grad_head[PAD_ROWS:]], axis=0)
    shard_grads, blocks = plan.finish(partial)

    out = {}
    for n in BIG_PIECES:
        shape2d = (-1, wts[n].shape[-1])
        res = _adamw(n, *[a.reshape(shape2d) for a in (shard_grads[n], wts[n], mom[n], var[n])])
        out[n] = [r.reshape(wts[n].shape) for r in res]

    vecs = [tuple(a[n].reshape(1, D) for a in (wts, mom, var)) for n in VEC_NAMES]
    conv = tuple(a["conv_w"][0] for a in (wts, mom, var))
    meta = tuple(a["meta_tokens"] for a in (wts, mom, var))
    res = _small_update(blocks, place, vecs, conv, meta)
    loss = res[0].reshape(())
    for j, n in enumerate(VEC_NAMES):
        out[n] = [r[j].reshape(wts[n].shape) for r in res[1:5]]
    out["conv_w"] = [r.reshape(wts["conv_w"].shape) for r in res[5:9]]
    out["meta_tokens"] = list(res[9:13])

    return (loss, grad_x.reshape(x.shape)) + tuple(out[n][kind] for kind in range(4) for n in WEIGHT_ORDER)
```

```python
import functools
import math

import jax
import jax.numpy as jnp
from jax import lax
from jax.experimental import pallas as pl
from jax.experimental.pallas import tpu as pltpu

F32 = jnp.float32
BF16 = jnp.bfloat16

LANES = 128
BF16_ROWS = 16

D = 1024
HEADS = 8
DH = 128
CHUNK = 256
N_META = 16
FRONT = 256
PAD_ROWS = FRONT - N_META
LRU_BLOCKS = 4
LRU_BLOCK = 256
LRU_C = 8.0
FFN = 2816
FFN_HALF = FFN // 2
IN_COLS = 8 * D
ROPE_BASE = 10000.0
EPS = 1e-6
QK_SCALE = DH ** -0.5

ADAM_LR = 0.001
ADAM_B1 = 0.9
ADAM_B2 = 0.999
ADAM_EPS = 1e-08
ADAM_WD = 0.01
ADAM_STEP = 10

TM = 256
TM_HEAVY = 768
MXU_TILE = 256
FFN_SUB = 2 * MXU_TILE
_FFN_SUBS = [(a, min(a + FFN_SUB, FFN)) for a in range(0, FFN, FFN_SUB)]
TM_MIX_BWD = 384
TM_LRU = 384
TM_MIX_FWD = 384
TM_IN_PROJ = 1408
VMEM_LIMIT = 60 * 1024 * 1024
TN_VMEM_BUDGET = 40 * 1024 * 1024

NT_DIMS = (((1,), (1,)), ((), ()))
TN_DIMS = (((0,), (0,)), ((), ()))
MESH = pl.DeviceIdType.MESH


def _params(sem=None):
    if sem is None:
        return pltpu.CompilerParams(vmem_limit_bytes=VMEM_LIMIT)
    return pltpu.CompilerParams(dimension_semantics=sem, vmem_limit_bytes=VMEM_LIMIT)


def _dot(a, b):
    return jnp.dot(a, b, preferred_element_type=F32)


def _dot_nt(a, b):
    return lax.dot_general(a, b, NT_DIMS, preferred_element_type=F32)


def _dot_tn(a, b):
    return lax.dot_general(a, b, TN_DIMS, preferred_element_type=F32)


def _sigmoid(z):
    return 1.0 / (1.0 + jnp.exp(-z))


def _log1p(x):
    return jnp.where(x < 1e-3, x * (1.0 - x * (0.5 - x * (1.0 / 3.0))), jnp.log(1.0 + x))


def _softplus(x):
    return jnp.maximum(x, 0.0) + _log1p(jnp.exp(-jnp.abs(x)))


def _one_minus_square(a, log_a):
    x = 2.0 * log_a
    series = -x * (1.0 + x * (0.5 + x * (1.0 / 6.0)))
    return jnp.where(x > -0.02, series, 1.0 - a * a)


_GELU_K = math.sqrt(2.0 / math.pi)


def _gelu_and_grad(x):
    inner = _GELU_K * (x + 0.044715 * x * x * x)
    t = jnp.tanh(inner)
    val = 0.5 * x * (1.0 + t)
    grad = 0.5 * (1.0 + t) + 0.5 * x * (1.0 - t * t) * _GELU_K * (1.0 + 3.0 * 0.044715 * x * x)
    return val, grad


def _row_ids(i, rows, shape):
    return i * rows + lax.broadcasted_iota(jnp.int32, shape, 0)


def _rope_tables(rows):
    inv_freq = ROPE_BASE ** (-jnp.arange(0, DH, 2, dtype=F32) / DH)
    block_pos = jnp.arange(rows // FRONT, dtype=jnp.int32) * FRONT - PAD_ROWS
    coarse = block_pos.astype(F32)[:, None] * inv_freq[None, :]
    fine = jnp.arange(FRONT, dtype=F32)[:, None] * inv_freq[None, :]
    ca, sa = jnp.cos(coarse)[:, None, :], jnp.sin(coarse)[:, None, :]
    cb, sb = jnp.cos(fine)[None, :, :], jnp.sin(fine)[None, :, :]
    cos = (ca * cb - sa * sb).reshape(rows, DH // 2)
    sin = (sa * cb + ca * sb).reshape(rows, DH // 2)
    return jnp.concatenate([cos, cos], axis=1), jnp.concatenate([-sin, sin], axis=1)


def _decay_tables():
    log_g = jnp.log(1.0 - 2.0 ** (-5.0 - jnp.arange(HEADS, dtype=F32)))
    idx = jnp.arange(CHUNK, dtype=F32)
    diff = idx[:, None] - idx[None, :]
    intra = jnp.where(diff[None] >= 0, jnp.exp(jnp.maximum(diff, 0.0)[None] * log_g[:, None, None]), 0.0)
    q_decay = jnp.exp((idx + 1.0)[:, None] * log_g[None, :])
    k_decay = jnp.exp((CHUNK - 1.0 - idx)[:, None] * log_g[None, :])
    chunk_decay = jnp.exp(CHUNK * log_g)
    wide = lambda a: jnp.repeat(a, DH, axis=-1)
    return intra, jnp.swapaxes(intra, 1, 2), wide(q_decay), wide(k_decay), wide(chunk_decay[None, :])


def _norm1(head, x2d, norm_w, riders=()):
    rows = FRONT + x2d.shape[0]
    tm = TM_IN_PROJ if rows % TM_IN_PROJ == 0 else TM_MIX_FWD
    nt = rows // tm

    def body(head_hbm, x_hbm, nw_ref, u_ref, rstd_ref, h0_sc, h0_sem):
        slot = _frame_rows(head_hbm, x_hbm, h0_sc, h0_sem, pl.program_id(0), tm, nt)
        hv = h0_sc[slot]
        rs = lax.rsqrt(jnp.mean(hv * hv, axis=-1, keepdims=True) + EPS)
        u_ref[...] = ((hv * rs) * nw_ref[...]).astype(BF16)
        rstd_ref[...] = rs

    return _hosted_call(
        body,
        name="norm1",
        grid=(nt,),
        in_specs=[_ANY, _ANY, pl.BlockSpec((1, D), lambda i: (0, 0))],
        out_specs=[pl.BlockSpec((tm, D), lambda i: (i, 0)), pl.BlockSpec((tm, 1), lambda i: (i, 0))],
        out_shape=[jax.ShapeDtypeStruct((rows, D), BF16), jax.ShapeDtypeStruct((rows, 1), F32)],
        scratch_shapes=[pltpu.VMEM((2, tm, D), F32), pltpu.SemaphoreType.DMA((3,))],
        args=(head, x2d, norm_w),
        riders=riders,
    )


def _heavy_tile(rows):
    return TM_HEAVY if rows % TM_HEAVY == 0 else TM


def _lru_tile(rows):
    return TM_LRU if rows % TM_LRU == 0 else TM


def _swap_3_4(group):
    return jnp.where(group == 3, 4, jnp.where(group == 4, 3, group))


SHARD_ORDER = (2, 3, 1)
LRU_IN_COL = 3
GATES_COL = 1


def _in_proj(u, w_shard, route, cos_t, sin_t, riders=()):
    rows = u.shape[0]
    tm = TM_IN_PROJ if rows % TM_IN_PROJ == 0 else _heavy_tile(rows)
    nt = rows // tm
    kind, shard = BIG_PIECES["w_in"]

    def body(route_ref, u_hbm, wsh_ref, cos_ref, sin_ref, proj_ref, wfull_ref,
             u_sc, w_sc, u_sem, w_sem, local_sem, ici_send, ici_recv, fwd_send, fwd_recv):
        g, i = pl.program_id(0), pl.program_id(1)
        s, c = route_ref[0], route_ref[1]
        gid = route_ref[2 + g]
        sibling = (s // 2, s % 2, 1 - c)
        local = pltpu.make_async_copy(wsh_ref, _full_region(wfull_ref, kind, shard, s, None), local_sem)
        u_copies = [pltpu.make_async_copy(u_hbm.at[pl.ds(t * tm, tm)], u_sc.at[t], u_sem.at[t]) for t in range(nt)]
        sends, arrivals = [], []
        for k in (1, 2, 3):
            s2, dev = _other_chip(s, c, k)
            sends.append(_remote(_shard_region(wsh_ref, shard, c), _full_region(wfull_ref, kind, shard, s, c),
                                 ici_send.at[k - 1], ici_recv.at[k - 1], dev))
            mine = _full_region(wfull_ref, kind, shard, s2, c)
            theirs = _full_region(wfull_ref, kind, shard, s2, 1 - c)
            arrivals.append((_remote(mine, mine, ici_send.at[k - 1], ici_recv.at[k - 1], dev),
                             _remote(mine, mine, fwd_send.at[k - 1], fwd_recv.at[k - 1], sibling),
                             _remote(theirs, theirs, fwd_send.at[k - 1], fwd_recv.at[k - 1], sibling)))

        slot = g % 2
        last_tile = i == nt - 1

        def block_copy(src, col, to_slot):
            return pltpu.make_async_copy(src.at[:, pl.ds(pl.multiple_of(col * D, D), D)], w_sc.at[to_slot],
                                         w_sem.at[to_slot])

        @pl.when(jnp.logical_and(g == 0, i == 0))
        def _():
            for k in SHARD_ORDER:
                sends[k - 1].start()
            local.start()
            for cp in u_copies:
                cp.start()
            cp = block_copy(wsh_ref, 0, 0)
            cp.start()
            cp.wait()

        @pl.when(jnp.logical_and(last_tile, g == 0))
        def _():
            block_copy(wsh_ref, 1, 1).start()

        for pos, k in enumerate(SHARD_ORDER, start=1):
            arrived, forward, forwarded = arrivals[k - 1]

            @pl.when(jnp.logical_and(last_tile, g == 2 * pos - 1))
            def _():
                arrived.wait_recv()
                forward.start()
                forwarded.wait_recv()
                block_copy(wfull_ref, route_ref[2 + 2 * pos], 0).start()

            @pl.when(jnp.logical_and(last_tile, g == 2 * pos))
            def _():
                block_copy(wfull_ref, route_ref[3 + 2 * pos], 1).start()

        @pl.when(jnp.logical_and(i == 0, g > 0))
        def _():
            block_copy(wfull_ref, 0, slot).wait()

        for t in range(nt):
            @pl.when(jnp.logical_and(g == 0, i == t))
            def _():
                u_copies[t].wait()

        halves = range(0, D, D // 2)

        @pl.when(gid < 2)
        def _():
            scale = jnp.where(gid == 1, QK_SCALE, 1.0).astype(F32)
            for a in halves:
                acc = _dot(u_sc[i], w_sc[slot, :, a:a + D // 2])
                for h in range(HEADS // 2):
                    blk = acc[:, h * DH:(h + 1) * DH]
                    out = (blk * cos_ref[...] + pltpu.roll(blk, DH // 2, axis=1) * sin_ref[...]) * scale
                    proj_ref[:, a + h * DH:a + (h + 1) * DH] = out.astype(BF16)

        @pl.when(gid >= 2)
        def _():
            for a in halves:
                proj_ref[:, a:a + D // 2] = _dot(u_sc[i], w_sc[slot, :, a:a + D // 2]).astype(BF16)

        @pl.when(jnp.logical_and(g == 7, i == nt - 1))
        def _():
            local.wait()
            for cp in sends:
                cp.wait_send()
            for _, forward, _ in arrivals:
                forward.wait_send()

    return _hosted_call(
        body,
        name="in_proj",
        grid=(8, nt),
        in_specs=[
            _ANY, _ANY,
            pl.BlockSpec((tm, DH), lambda g, i, rt: (i, 0)),
            pl.BlockSpec((tm, DH), lambda g, i, rt: (i, 0)),
        ],
        out_specs=[pl.BlockSpec((tm, D), lambda g, i, rt: (i, _swap_3_4(rt[2 + g]))), _ANY],
        out_shape=[jax.ShapeDtypeStruct((rows, IN_COLS), BF16), jax.ShapeDtypeStruct((D, IN_COLS), BF16)],
        scratch_shapes=[
            pltpu.VMEM((nt, tm, D), BF16), pltpu.VMEM((2, D, D), BF16),
            pltpu.SemaphoreType.DMA((nt,)), pltpu.SemaphoreType.DMA((2,)), pltpu.SemaphoreType.DMA,
            pltpu.SemaphoreType.DMA((3,)), pltpu.SemaphoreType.DMA((3,)),
            pltpu.SemaphoreType.DMA((3,)), pltpu.SemaphoreType.DMA((3,)),
        ],
        args=(u, w_shard, cos_t, sin_t),
        riders=riders,
        prefetch=route,
        riders_after_body=True,
    )


def _retention_fwd(proj_bf, intra, q_dec, k_dec, c_dec, riders=()):
    rows = proj_bf.shape[0]
    nc = rows // CHUNK

    def body(q_ref, k_ref, v_ref, m_ref, qd_ref, kd_ref, cd_ref, o_ref, sprev_ref, s_sc):
        @pl.when(pl.program_id(0) == 0)
        def _():
            s_sc[...] = jnp.zeros_like(s_sc)

        for h in range(HEADS):
            sl = slice(h * DH, (h + 1) * DH)
            q, k, v = q_ref[:, sl], k_ref[:, sl], v_ref[:, sl]
            state = s_sc[h]
            state_b = state.astype(BF16)
            sprev_ref[0, h] = state_b
            s = _dot_nt(q, k) * m_ref[h]
            inner = _dot(s.astype(BF16), v)
            cross = _dot(q, state_b) * qd_ref[:, sl]
            o_ref[:, sl] = (inner + cross).astype(BF16)
            k_scaled = (k.astype(F32) * kd_ref[:, sl]).astype(BF16)
            s_sc[h] = state * cd_ref[:, sl] + _dot_tn(k_scaled, v)

    chunk_spec = lambda col: pl.BlockSpec((CHUNK, D), lambda c: (c, col))
    const2 = lambda shape: pl.BlockSpec(shape, lambda c: (0, 0))
    return _hosted_call(
        body,
        name="retention_fwd",
        grid=(nc,),
        in_specs=[
            chunk_spec(0), chunk_spec(1), chunk_spec(2),
            pl.BlockSpec((HEADS, CHUNK, CHUNK), lambda c: (0, 0, 0)),
            const2((CHUNK, D)), const2((CHUNK, D)), const2((1, D)),
        ],
        out_specs=[
            pl.BlockSpec((CHUNK, D), lambda c: (c, 0)),
            pl.BlockSpec((1, HEADS, DH, DH), lambda c: (c, 0, 0, 0)),
        ],
        out_shape=[
            jax.ShapeDtypeStruct((rows, D), BF16),
            jax.ShapeDtypeStruct((nc, HEADS, DH, DH), BF16),
        ],
        scratch_shapes=[pltpu.VMEM((HEADS, DH, DH), F32)],
        args=(proj_bf, proj_bf, proj_bf, intra, q_dec, k_dec, c_dec),
        riders=riders,
    )


def _shift_down(x, first8_prev, d):
    rolled = pltpu.roll(x, d, axis=0)
    head = pltpu.roll(jnp.concatenate([first8_prev, x[0:8]], axis=0), d, axis=0)[8:16]
    return rolled, head


def _conv_and_gates(x, prev8, cw_ref, cb_ref, wa_ref, wx_ref, ba_ref, bx_ref, lam_ref, c_sc):
    cw = cw_ref[...]
    conv = cb_ref[...] + cw[3:4] * x
    head = cb_ref[...] + cw[3:4] * x[0:8]
    for d in (1, 2, 3):
        rolled, hd = _shift_down(x, prev8, d)
        conv = conv + cw[3 - d:4 - d] * rolled
        head = head + cw[3 - d:4 - d] * hd
    c_sc[...] = conv
    c_sc[0:8, :] = head
    c = c_sc[...]
    zr, zi = [], []
    for g in range(LRU_BLOCKS):
        sl = slice(g * LRU_BLOCK, (g + 1) * LRU_BLOCK)
        cg = c[:, sl].astype(BF16)
        zr.append(_dot(cg, wa_ref[g]))
        zi.append(_dot(cg, wx_ref[g]))
    r = _sigmoid(jnp.concatenate(zr, axis=1) + ba_ref[...])
    gate_i = _sigmoid(jnp.concatenate(zi, axis=1) + bx_ref[...])
    sp = _softplus(-lam_ref[...])
    log_a = (-LRU_C) * r * sp
    a = jnp.exp(log_a)
    mult = jnp.sqrt(_one_minus_square(a, log_a))
    return c, r, gate_i, a, mult, log_a


def _lru_fwd(proj, conv_w, conv_b, wa, wx, ba, bx, lam, riders=()):
    rows = proj.shape[0]
    TM = _lru_tile(rows)
    nt = rows // TM

    def body(x_ref, cw_ref, cb_ref, wa_ref, wx_ref, ba_ref, bx_ref, lam_ref,
             h_ref, c_ref, r_ref, i_ref, la_ref, mult_ref, prev_sc, carry_sc, c_sc, a_sc, u_sc, h_sc):
        i = pl.program_id(0)

        @pl.when(i == 0)
        def _():
            prev_sc[...] = jnp.zeros_like(prev_sc)
            carry_sc[...] = jnp.zeros_like(carry_sc)

        x = x_ref[...].astype(F32)
        c, r, gate_i, a, mult, log_a = _conv_and_gates(x, prev_sc[...], cw_ref, cb_ref, wa_ref, wx_ref, ba_ref, bx_ref,
                                                       lam_ref, c_sc)
        for ref, val in ((c_ref, c), (r_ref, r), (i_ref, gate_i), (la_ref, log_a), (mult_ref, mult)):
            ref[...] = val.astype(BF16)
        prev_sc[...] = x[TM - 8:TM]
        valid = _row_ids(i, TM, (TM, D)) >= PAD_ROWS
        a_sc[...] = a
        u_sc[...] = jnp.where(valid, mult * gate_i * c, 0.0)
        row8 = lax.broadcasted_iota(jnp.int32, (8, D), 0)

        def group(gi, hprev):
            r0 = pl.multiple_of(gi * 8, 8)
            aa = a_sc[pl.ds(r0, 8), :]
            uu = u_sc[pl.ds(r0, 8), :]
            for d in (1, 2, 4):
                a_sh = jnp.where(row8 >= d, pltpu.roll(aa, d, axis=0), 1.0)
                u_sh = jnp.where(row8 >= d, pltpu.roll(uu, d, axis=0), 0.0)
                uu = uu + aa * u_sh
                aa = aa * a_sh
            hb = aa * hprev + uu
            h_sc[pl.ds(r0, 8), :] = hb
            return hb[7:8, :]

        hlast = lax.fori_loop(0, TM // 8, group, carry_sc[0:1, :])
        carry_sc[0:1, :] = hlast
        h_ref[...] = h_sc[...].astype(BF16)

    vec = pl.BlockSpec((1, D), lambda i: (0, 0))
    wspec = pl.BlockSpec((LRU_BLOCKS, LRU_BLOCK, LRU_BLOCK), lambda i: (0, 0, 0))
    return _hosted_call(
        body,
        name="lru_fwd",
        grid=(nt,),
        in_specs=[
            pl.BlockSpec((TM, D), lambda i: (i, LRU_IN_COL)),
            pl.BlockSpec((4, D), lambda i: (0, 0)),
            vec, wspec, wspec, vec, vec, vec,
        ],
        out_specs=[pl.BlockSpec((TM, D), lambda i: (i, 0)) for _ in range(6)],
        out_shape=[jax.ShapeDtypeStruct((rows, D), BF16) for _ in range(6)],
        scratch_shapes=[
            pltpu.VMEM((8, D), F32), pltpu.VMEM((8, D), F32),
            pltpu.VMEM((TM, D), F32), pltpu.VMEM((TM, D), F32), pltpu.VMEM((TM, D), F32), pltpu.VMEM((TM, D), F32),
        ],
        args=(proj, conv_w, conv_b, wa, wx, ba, bx, lam),
        riders=riders,
    )


def _group_norm(o):
    outs, rstds = [], []
    for h in range(HEADS):
        oh = o[:, h * DH:(h + 1) * DH]
        rs = lax.rsqrt(jnp.mean(oh * oh, axis=-1, keepdims=True) + EPS)
        outs.append(oh * rs)
        rstds.append(rs)
    return jnp.concatenate(outs, axis=1), rstds


def _frame_rows(head_hbm, x_hbm, buf, sems, i, tm, nt):
    slot = i % 2

    @pl.when(i == 0)
    def _():
        first = [pltpu.make_async_copy(head_hbm, buf.at[0, pl.ds(0, FRONT)], sems.at[2]),
                 pltpu.make_async_copy(x_hbm.at[pl.ds(0, tm - FRONT)], buf.at[0, pl.ds(FRONT, tm - FRONT)], sems.at[0])]
        for cp in first:
            cp.start()
        for cp in first:
            cp.wait()

    @pl.when(i + 1 < nt)
    def _():
        start = pl.multiple_of((i + 1) * tm - FRONT, LANES)
        pltpu.make_async_copy(x_hbm.at[pl.ds(start, tm)], buf.at[1 - slot], sems.at[1 - slot]).start()

    @pl.when(i > 0)
    def _():
        pltpu.make_async_copy(x_hbm.at[pl.ds(0, tm)], buf.at[slot], sems.at[slot]).wait()

    return slot


def _mix_fwd(head, x2d, o, proj, h_lru, w_br, w_bl, w_o, ffn_norm_w, riders=()):
    rows = o.shape[0]
    tm = TM_MIX_FWD
    assert rows % tm == 0 and tm > FRONT
    nt = rows // tm

    def body(head_hbm, x_hbm, o_ref, gates_ref, hl_ref, wbr_ref, wbl_ref, wo_ref, nw_ref,
             yret_ref, ylru_ref, h1_ref, u2_ref, rstd_ref, h0_sc, h0_sem):
        i = pl.program_id(0)
        slot = _frame_rows(head_hbm, x_hbm, h0_sc, h0_sem, i, tm, nt)
        half = tm // 2
        for r in (0, half):
            sl = pl.ds(r, half)
            gate = lambda j: gates_ref[sl, j * D:(j + 1) * D].astype(F32)
            on, _ = _group_norm(o_ref[sl, :].astype(F32))
            gret = gate(0)
            a_ret = (gret * _sigmoid(gret) * on).astype(BF16)
            y_ret = _dot(a_ret, wbr_ref[...])
            gl, _ = _gelu_and_grad(gate(1))
            a_lru = (gl * hl_ref[sl, :].astype(F32)).astype(BF16)
            y_lru = _dot(a_lru, wbl_ref[...])
            mixed = (_sigmoid(gate(2)) * y_ret + _sigmoid(gate(3)) * y_lru).astype(BF16)
            delta = _dot(mixed, wo_ref[...])
            yret_ref[sl, :] = y_ret.astype(BF16)
            ylru_ref[sl, :] = y_lru.astype(BF16)
            h1 = h0_sc[slot, sl, :] + delta
            rs = lax.rsqrt(jnp.mean(h1 * h1, axis=-1, keepdims=True) + EPS)
            h1_ref[sl, :] = h1
            u2_ref[sl, :] = ((h1 * rs) * nw_ref[...]).astype(BF16)
            rstd_ref[sl, :] = rs

    tile = lambda col: pl.BlockSpec((tm, D), lambda i: (i, col))
    wspec = pl.BlockSpec((D, D), lambda i: (0, 0))
    return _hosted_call(
        body,
        name="mix_fwd",
        grid=(nt,),
        in_specs=[
            _ANY, _ANY,
            tile(0), pl.BlockSpec((tm, 4 * D), lambda i: (i, GATES_COL)), tile(0),
            wspec, wspec, wspec,
            pl.BlockSpec((1, D), lambda i: (0, 0)),
        ],
        out_specs=[tile(0), tile(0), tile(0), tile(0), pl.BlockSpec((tm, 1), lambda i: (i, 0))],
        out_shape=[
            jax.ShapeDtypeStruct((rows, D), BF16), jax.ShapeDtypeStruct((rows, D), BF16),
            jax.ShapeDtypeStruct((rows, D), F32), jax.ShapeDtypeStruct((rows, D), BF16),
            jax.ShapeDtypeStruct((rows, 1), F32),
        ],
        scratch_shapes=[pltpu.VMEM((2, tm, D), F32), pltpu.SemaphoreType.DMA((3,))],
        args=(head, x2d, o, proj, h_lru, w_br, w_bl, w_o, ffn_norm_w),
        riders=riders,
    )


def _ffn_fwd_gate(u2, w_ffn_in, riders=()):
    rows = u2.shape[0]
    tm = _heavy_tile(rows)
    hid = lambda: pl.BlockSpec((tm, FFN), lambda i: (i, 0))

    def body(u2_ref, w_hbm, silu_ref, dsilu_ref, act_ref, w_sc, w_sem):
        @pl.when(pl.program_id(0) == 0)
        def _():
            cp = pltpu.make_async_copy(w_hbm, w_sc, w_sem)
            cp.start()
            cp.wait()

        u2 = u2_ref[...]
        for a, b in _FFN_SUBS:
            g = _dot(u2, w_sc[:, a:b])
            up = _dot(u2, w_sc[:, FFN + a:FFN + b])
            sg = _sigmoid(g)
            silu = g * sg
            silu_ref[:, a:b] = silu.astype(BF16)
            dsilu_ref[:, a:b] = (up * (sg * (1.0 + g * (1.0 - sg)))).astype(BF16)
            act_ref[:, a:b] = (silu * up).astype(BF16)

    return _hosted_call(
        body,
        name="ffn_fwd_gate",
        grid=(rows // tm,),
        in_specs=[pl.BlockSpec((tm, D), lambda i: (i, 0)), _ANY],
        out_specs=[hid(), hid(), hid()],
        out_shape=[jax.ShapeDtypeStruct((rows, FFN), BF16)] * 3,
        scratch_shapes=[pltpu.VMEM((D, 2 * FFN), BF16), pltpu.SemaphoreType.DMA],
        args=(u2, w_ffn_in),
        riders=riders,
    )


def _ffn_out_loss(act, h1, w_ffn_out, target, final_norm_w):
    rows = act.shape[0]
    tm = _heavy_tile(rows)
    nt = rows // tm

    def body(act_ref, h1_ref, wo_ref, fnw_ref, tgt_hbm, dh2_ref, stats_ref, tgt_sc, tgt_sem):
        i = pl.program_id(0)
        slot = i % 2

        @pl.when(i == 0)
        def _():
            stats_ref[...] = jnp.zeros_like(stats_ref)
            tgt_sc[0, 0:FRONT, :] = jnp.zeros((FRONT, D), F32)
            cp = pltpu.make_async_copy(tgt_hbm.at[pl.ds(0, tm - FRONT)], tgt_sc.at[0, pl.ds(FRONT, tm - FRONT)], tgt_sem.at[0])
            cp.start()
            cp.wait()

        @pl.when(i + 1 < nt)
        def _():
            start = pl.multiple_of((i + 1) * tm - FRONT, FRONT)
            pltpu.make_async_copy(tgt_hbm.at[pl.ds(start, tm)], tgt_sc.at[1 - slot], tgt_sem.at[1 - slot]).start()

        @pl.when(i > 0)
        def _():
            pltpu.make_async_copy(tgt_hbm.at[pl.ds(0, tm)], tgt_sc.at[slot], tgt_sem.at[slot]).wait()

        fnw = fnw_ref[...]
        half = tm // 2
        for r in (0, half):
            sl = pl.ds(r, half)
            h2 = h1_ref[sl, :] + _dot(act_ref[sl, :], wo_ref[...])
            rs = lax.rsqrt(jnp.mean(h2 * h2, axis=-1, keepdims=True) + EPS)
            n = h2 * rs
            valid = i * tm + r + lax.broadcasted_iota(jnp.int32, (half, D), 0) >= FRONT
            diff = jnp.where(valid, n * fnw - tgt_sc[slot, sl, :], 0.0)
            dy = diff * (1.0 / D)
            stats_ref[0:1, :] += (0.5 / D) * jnp.sum(diff * diff, axis=0, keepdims=True)
            stats_ref[1:2, :] += jnp.sum(dy * n, axis=0, keepdims=True)
            dn = dy * fnw
            dh2_ref[sl, :] = rs * (dn - n * jnp.mean(dn * n, axis=-1, keepdims=True))

    return pl.pallas_call(
        body,
        name="ffn_out_loss",
        grid=(nt,),
        in_specs=[
            pl.BlockSpec((tm, FFN), lambda i: (i, 0)),
            pl.BlockSpec((tm, D), lambda i: (i, 0)),
            pl.BlockSpec((FFN, D), lambda i: (0, 0)),
            pl.BlockSpec((1, D), lambda i: (0, 0)),
            _ANY,
        ],
        out_specs=[pl.BlockSpec((tm, D), lambda i: (i, 0)), pl.BlockSpec((8, D), lambda i: (0, 0))],
        out_shape=[jax.ShapeDtypeStruct((rows, D), F32), jax.ShapeDtypeStruct((8, D), F32)],
        scratch_shapes=[pltpu.VMEM((2, tm, D), F32), pltpu.SemaphoreType.DMA((2,))],
        compiler_params=_params(("arbitrary",)),
    )(act, h1, w_ffn_out, final_norm_w, target)


def _ffn_bwd(dh2, silu, dsilu, h1, rstd2, w_ffn_in, w_ffn_out, ffn_norm_w):
    rows = dh2.shape[0]
    tm = _heavy_tile(rows)
    blk = lambda: pl.BlockSpec((tm, FFN), lambda i: (i, 0))

    def gate_body(dh2_ref, silu_ref, dsilu_ref, wo_hbm, dg_ref, dup_ref, dh2b_ref, wo_sc, wo_sem):
        @pl.when(pl.program_id(0) == 0)
        def _():
            cp = pltpu.make_async_copy(wo_hbm, wo_sc, wo_sem)
            cp.start()
            cp.wait()

        dh2b = dh2_ref[...].astype(BF16)
        dh2b_ref[...] = dh2b
        for a, b in _FFN_SUBS:
            dact = _dot_nt(dh2b, wo_sc[a:b, :])
            dup_ref[:, a:b] = (dact * silu_ref[:, a:b].astype(F32)).astype(BF16)
            dg_ref[:, a:b] = (dact * dsilu_ref[:, a:b].astype(F32)).astype(BF16)

    dg, dup, dh2b = pl.pallas_call(
        gate_body,
        name="ffn_bwd_gate",
        grid=(rows // tm,),
        in_specs=[pl.BlockSpec((tm, D), lambda i: (i, 0)), blk(), blk(), _ANY],
        out_specs=[blk(), blk(), pl.BlockSpec((tm, D), lambda i: (i, 0))],
        out_shape=[jax.ShapeDtypeStruct((rows, FFN), BF16)] * 2 + [jax.ShapeDtypeStruct((rows, D), BF16)],
        scratch_shapes=[pltpu.VMEM((FFN, D), BF16), pltpu.SemaphoreType.DMA],
        compiler_params=_params(("arbitrary",)),
    )(dh2, silu, dsilu, w_ffn_out)

    def body(dg_ref, dup_ref, dh2_ref, h1_ref, rstd_ref, w_hbm, nw_ref, dh1_ref, stats_ref, w_sc, w_sem):
        @pl.when(pl.program_id(0) == 0)
        def _():
            stats_ref[...] = jnp.zeros_like(stats_ref)
            cp = pltpu.make_async_copy(w_hbm, w_sc, w_sem)
            cp.start()
            cp.wait()

        half = tm // 2
        for r in (0, half):
            sl = pl.ds(r, half)
            du = _dot_nt(dg_ref[sl, :], w_sc[:, 0:FFN]) + _dot_nt(dup_ref[sl, :], w_sc[:, FFN:2 * FFN])
            rs = rstd_ref[sl, :]
            n = h1_ref[sl, :] * rs
            stats_ref[0:1, :] += jnp.sum(du * n, axis=0, keepdims=True)
            dn = du * nw_ref[...]
            dh1_ref[sl, :] = dh2_ref[sl, :] + rs * (dn - n * jnp.mean(dn * n, axis=-1, keepdims=True))

    row = lambda width: pl.BlockSpec((tm, width), lambda i: (i, 0))
    dh1, stats = pl.pallas_call(
        body,
        name="ffn_bwd_in",
        grid=(rows // tm,),
        in_specs=[row(FFN), row(FFN), row(D), row(D), row(1), _ANY, pl.BlockSpec((1, D), lambda i: (0, 0))],
        out_specs=[row(D), pl.BlockSpec((8, D), lambda i: (0, 0))],
        out_shape=[jax.ShapeDtypeStruct((rows, D), F32), jax.ShapeDtypeStruct((8, D), F32)],
        scratch_shapes=[pltpu.VMEM((D, 2 * FFN), BF16), pltpu.SemaphoreType.DMA],
        compiler_params=_params(("arbitrary",)),
    )(dg, dup, dh2, h1, rstd2, w_ffn_in, ffn_norm_w)
    return dg, dup, dh2b, dh1, stats


def _mix_bwd(dh1, o, proj, h_lru, y_ret, y_lru, w_br, w_bl, w_o, riders=()):
    rows = dh1.shape[0]
    tm = TM_MIX_BWD
    nt = rows // tm

    def body(dh1_ref, o_ref, gates_ref, hl_ref, yret_ref, ylru_ref, wbr_hbm, wbl_hbm, wo_hbm,
             dproj_ref, do_ref, dhl_ref, mixed_ref, aret_ref, alru_ref, dyret_ref, dylru_ref, w_sc, w_sem):
        @pl.when(pl.program_id(0) == 0)
        def _():
            copies = [pltpu.make_async_copy(src, w_sc.at[q], w_sem.at[q]) for q, src in enumerate((wbr_hbm, wbl_hbm, wo_hbm))]
            for cp in copies:
                cp.start()
            for cp in copies:
                cp.wait()

        wbr_ref, wbl_ref, wo_ref = w_sc.at[0], w_sc.at[1], w_sc.at[2]
        half = tm // 2
        for r in (0, half):
            rs_ = pl.ds(r, half)
            gate = lambda j: gates_ref[rs_, j * D:(j + 1) * D].astype(F32)
            dmixed = _dot_nt(dh1_ref[rs_, :].astype(BF16), wo_ref[...])
            y_ret, y_lru = yret_ref[rs_, :].astype(F32), ylru_ref[rs_, :].astype(F32)
            sa, sb = _sigmoid(gate(2)), _sigmoid(gate(3))
            mixed_ref[rs_, :] = (sa * y_ret + sb * y_lru).astype(BF16)
            dga = dmixed * y_ret * sa * (1.0 - sa)
            dgb = dmixed * y_lru * sb * (1.0 - sb)
            dy_ret = (dmixed * sa).astype(BF16)
            dy_lru = (dmixed * sb).astype(BF16)
            dyret_ref[rs_, :] = dy_ret
            dylru_ref[rs_, :] = dy_lru
            da_ret = _dot_nt(dy_ret, wbr_ref[...])
            da_lru = _dot_nt(dy_lru, wbl_ref[...])

            gret = gate(0)
            sg = _sigmoid(gret)
            silu = gret * sg
            on, rstds = _group_norm(o_ref[rs_, :].astype(F32))
            aret_ref[rs_, :] = (silu * on).astype(BF16)
            dgret = da_ret * on * (sg * (1.0 + gret * (1.0 - sg)))
            don = da_ret * silu
            for h in range(HEADS):
                sl = slice(h * DH, (h + 1) * DH)
                onh, donh = on[:, sl], don[:, sl]
                do_ref[rs_, sl] = (rstds[h] * (donh - onh * jnp.mean(donh * onh, axis=-1, keepdims=True))).astype(BF16)

            gl, gl_grad = _gelu_and_grad(gate(1))
            hl = hl_ref[rs_, :].astype(F32)
            alru_ref[rs_, :] = (gl * hl).astype(BF16)
            dlgate = da_lru * hl * gl_grad
            dhl_ref[rs_, :] = (da_lru * gl).astype(BF16)

            for j, val in enumerate((dgret, dlgate, dga, dgb)):
                dproj_ref[rs_, j * D:(j + 1) * D] = val.astype(BF16)

    tile = lambda col: pl.BlockSpec((tm, D), lambda i: (i, col))
    gates = pl.BlockSpec((tm, 4 * D), lambda i: (i, GATES_COL))
    bf = lambda: jax.ShapeDtypeStruct((rows, D), BF16)
    return _hosted_call(
        body,
        name="mix_bwd",
        grid=(nt,),
        in_specs=[tile(0), tile(0), gates, tile(0), tile(0), tile(0), _ANY, _ANY, _ANY],
        out_specs=[pl.BlockSpec((tm, 4 * D), lambda i: (i, GATES_COL))] + [tile(0)] * 7,
        out_shape=[jax.ShapeDtypeStruct((rows, IN_COLS), BF16)] + [bf() for _ in range(7)],
        scratch_shapes=[pltpu.VMEM((3, D, D), BF16), pltpu.SemaphoreType.DMA((3,))],
        args=(dh1, o, proj, h_lru, y_ret, y_lru, w_br, w_bl, w_o),
        riders=riders,
    )


def _retention_bwd(dproj, proj_bf, do, sprev, intra, intra_t, q_dec, k_dec, c_dec, cos_t, sin_t, riders=()):
    rows = proj_bf.shape[0]
    nc = rows // CHUNK

    def body(dproj_in_ref, q_ref, k_ref, v_ref, do_ref, sprev_ref, m_ref, mt_ref, qd_ref, kd_ref, cd_ref, cos_ref, sin_ref,
             dproj_ref, ds_sc):
        @pl.when(pl.program_id(0) == 0)
        def _():
            ds_sc[...] = jnp.zeros_like(ds_sc)

        cos, sin = cos_ref[...], sin_ref[...]

        def unrotate(dy):
            return dy * cos - pltpu.roll(dy, DH // 2, axis=1) * sin

        for h in range(HEADS):
            sl = slice(h * DH, (h + 1) * DH)
            q, k, v = q_ref[:, sl], k_ref[:, sl], v_ref[:, sl]
            do = do_ref[:, sl]
            dob = do.astype(BF16)
            doq = (do * qd_ref[:, sl]).astype(BF16)
            state_prev = sprev_ref[0, h]
            dstate = ds_sc[h]
            dstate_b = dstate.astype(BF16)
            s_t = (_dot_nt(k, q) * mt_ref[h]).astype(BF16)
            ds_t = (_dot_nt(v, dob) * mt_ref[h]).astype(BF16)
            ds = (_dot_nt(dob, v) * m_ref[h]).astype(BF16)
            kd = kd_ref[:, sl]
            dq = _dot(ds, k) + _dot_nt(doq, state_prev)
            dk = _dot(ds_t, q) + _dot_nt(v, dstate_b) * kd
            k_scaled = (k.astype(F32) * kd).astype(BF16)
            dv = _dot(s_t, dob) + _dot(k_scaled, dstate_b)
            ds_sc[h] = dstate * cd_ref[:, sl] + _dot_tn(q, doq)
            for part, val in enumerate((unrotate(dq), unrotate(dk) * QK_SCALE, dv)):
                dproj_ref[:, part * D + h * DH:part * D + (h + 1) * DH] = val.astype(BF16)

    rev = lambda c: nc - 1 - c
    chunk_spec = lambda col: pl.BlockSpec((CHUNK, D), lambda c: (rev(c), col))
    const2 = lambda shape: pl.BlockSpec(shape, lambda c: (0, 0))
    const3 = pl.BlockSpec((HEADS, CHUNK, CHUNK), lambda c: (0, 0, 0))
    return _hosted_call(
        body,
        name="retention_bwd",
        grid=(nc,),
        in_specs=[
            pl.BlockSpec(memory_space=pl.ANY),
            chunk_spec(0), chunk_spec(1), chunk_spec(2), chunk_spec(0),
            pl.BlockSpec((1, HEADS, DH, DH), lambda c: (rev(c), 0, 0, 0)),
            const3, const3,
            const2((CHUNK, D)), const2((CHUNK, D)), const2((1, D)),
            pl.BlockSpec((CHUNK, DH), lambda c: (rev(c), 0)),
            pl.BlockSpec((CHUNK, DH), lambda c: (rev(c), 0)),
        ],
        out_specs=[pl.BlockSpec((CHUNK, 3 * D), lambda c: (rev(c), 0))],
        out_shape=[jax.ShapeDtypeStruct(dproj.shape, BF16)],
        aliases={0: 0},
        scratch_shapes=[pltpu.VMEM((HEADS, DH, DH), F32)],
        args=(dproj, proj_bf, proj_bf, proj_bf, do, sprev, intra, intra_t, q_dec, k_dec, c_dec, cos_t, sin_t),
        riders=riders,
    )


def _lru_bwd(dproj, proj, saved, dhl, conv_w, wa, wx, lam, riders=()):
    rows = proj.shape[0]
    TM = _lru_tile(rows)
    nt = rows // TM
    per8 = TM // 8

    def body(dproj_in_ref, x_ref, xprev_ref, h_ref, hprev_ref, c_ref, r_ref, i_ref, la_ref, mult_ref, dhl_ref,
             cw_ref, wa_ref, wx_ref, lam_ref,
             dproj_ref, dwa_ref, dwx_ref, stats_ref, anext_sc, dhnext_sc, dcnext_sc, c_sc, b_sc, dh_sc):
        step = pl.program_id(0)
        i = nt - 1 - step

        @pl.when(step == 0)
        def _():
            anext_sc[...] = jnp.zeros_like(anext_sc)
            dhnext_sc[...] = jnp.zeros_like(dhnext_sc)
            dcnext_sc[...] = jnp.zeros_like(dcnext_sc)
            dwa_ref[...] = jnp.zeros_like(dwa_ref)
            dwx_ref[...] = jnp.zeros_like(dwx_ref)
            stats_ref[...] = jnp.zeros_like(stats_ref)

        first = i == 0
        x = x_ref[...].astype(F32)
        prev8 = jnp.where(first, 0.0, xprev_ref[8:16, :].astype(F32))
        c_b = c_ref[...]
        c, r, gate_i, mult = (ref[...].astype(F32) for ref in (c_ref, r_ref, i_ref, mult_ref))
        a = jnp.exp(la_ref[...].astype(F32))
        sp = _softplus(-lam_ref[...])
        dh_sc[...] = dhl_ref[...].astype(F32)

        b_sc[...] = pltpu.roll(a, TM - 1, axis=0)
        b_sc[TM - 1:TM, :] = anext_sc[0:1, :]
        anext_sc[0:1, :] = a[0:1, :]
        row8 = lax.broadcasted_iota(jnp.int32, (8, D), 0)

        def group(gi, dhnext):
            r0 = pl.multiple_of((per8 - 1 - gi) * 8, 8)
            bb = b_sc[pl.ds(r0, 8), :]
            uu = dh_sc[pl.ds(r0, 8), :]
            for d in (1, 2, 4):
                b_sh = jnp.where(row8 < 8 - d, pltpu.roll(bb, 8 - d, axis=0), 1.0)
                u_sh = jnp.where(row8 < 8 - d, pltpu.roll(uu, 8 - d, axis=0), 0.0)
                uu = uu + bb * u_sh
                bb = bb * b_sh
            dhb = bb * dhnext + uu
            dh_sc[pl.ds(r0, 8), :] = dhb
            return dhb[0:1, :]

        dhfirst = lax.fori_loop(0, per8, group, dhnext_sc[0:1, :])
        dhnext_sc[0:1, :] = dhfirst
        dh = dh_sc[...]

        h = h_ref[...].astype(F32)
        hprev8 = jnp.where(first, 0.0, hprev_ref[8:16, :].astype(F32))
        h_dn, h_head = _shift_down(h, hprev8, 1)
        c_sc[...] = h_dn
        c_sc[0:8, :] = h_head
        h_before = c_sc[...]

        valid = _row_ids(i, TM, (TM, D)) >= PAD_ROWS
        da = dh * h_before
        du = jnp.where(valid, dh, 0.0)
        dmult = du * gate_i * c
        dgate_i = du * mult * c
        dc = du * mult * gate_i
        dla = da * a - dmult * (a * a) / mult
        dla = jnp.where(valid, dla, 0.0)
        dr = dla * ((-LRU_C) * sp)
        dzr = dr * r * (1.0 - r)
        dzi = dgate_i * gate_i * (1.0 - gate_i)
        stats_ref[1:2, :] += jnp.sum(dzr, axis=0, keepdims=True)
        stats_ref[2:3, :] += jnp.sum(dzi, axis=0, keepdims=True)
        stats_ref[3:4, :] += jnp.sum(dla * ((-LRU_C) * r), axis=0, keepdims=True)
        dc_gate = []
        for g in range(LRU_BLOCKS):
            sl = slice(g * LRU_BLOCK, (g + 1) * LRU_BLOCK)
            cg = c_b[:, sl]
            dzr_g = dzr[:, sl].astype(BF16)
            dzi_g = dzi[:, sl].astype(BF16)
            dc_gate.append(_dot_nt(dzr_g, wa_ref[g]) + _dot_nt(dzi_g, wx_ref[g]))
            dwa_ref[g] += _dot_tn(cg, dzr_g)
            dwx_ref[g] += _dot_tn(cg, dzi_g)
        dc = dc + jnp.concatenate(dc_gate, axis=1)

        cw = cw_ref[...]
        stats_ref[0:1, :] += jnp.sum(dc, axis=0, keepdims=True)
        stats_ref[7:8, :] += jnp.sum(dc * x, axis=0, keepdims=True)
        dx = cw[3:4] * dc
        tail_src = jnp.concatenate([dc[TM - 8:TM], dcnext_sc[...]], axis=0)
        dx_tail = cw[3:4] * dc[TM - 8:TM]
        for d in (1, 2, 3):
            dx = dx + cw[3 - d:4 - d] * pltpu.roll(dc, TM - d, axis=0)
            dx_tail = dx_tail + cw[3 - d:4 - d] * pltpu.roll(tail_src, 16 - d, axis=0)[0:8]
            rolled, hd = _shift_down(x, prev8, d)
            b_sc[...] = rolled
            b_sc[0:8, :] = hd
            stats_ref[7 - d:8 - d, :] += jnp.sum(dc * b_sc[...], axis=0, keepdims=True)
        dcnext_sc[...] = dc[0:8]
        dproj_ref[...] = dx.astype(BF16)
        dproj_ref[TM - 8:TM, :] = dx_tail.astype(BF16)

    rev = lambda s: nt - 1 - s
    vec = pl.BlockSpec((1, D), lambda s: (0, 0))
    wspec = pl.BlockSpec((LRU_BLOCKS, LRU_BLOCK, LRU_BLOCK), lambda s: (0, 0, 0))
    prev16 = lambda col: pl.BlockSpec((BF16_ROWS, D), lambda s: (jnp.maximum(rev(s) * (TM // BF16_ROWS) - 1, 0), col))
    tile = lambda: pl.BlockSpec((TM, D), lambda s: (rev(s), 0))
    h_lru, c_sv, r_sv, i_sv, la_sv, mult_sv = saved
    return _hosted_call(
        body,
        name="lru_bwd",
        grid=(nt,),
        in_specs=[
            pl.BlockSpec(memory_space=pl.ANY),
            pl.BlockSpec((TM, D), lambda s: (rev(s), LRU_IN_COL)), prev16(LRU_IN_COL),
            tile(), prev16(0),
            tile(), tile(), tile(), tile(), tile(), tile(),
            pl.BlockSpec((4, D), lambda s: (0, 0)),
            wspec, wspec, vec,
        ],
        out_specs=[
            pl.BlockSpec((TM, D), lambda s: (rev(s), LRU_IN_COL)),
            wspec, wspec,
            pl.BlockSpec((8, D), lambda s: (0, 0)),
        ],
        out_shape=[
            jax.ShapeDtypeStruct(dproj.shape, BF16),
            jax.ShapeDtypeStruct((LRU_BLOCKS, LRU_BLOCK, LRU_BLOCK), F32),
            jax.ShapeDtypeStruct((LRU_BLOCKS, LRU_BLOCK, LRU_BLOCK), F32),
            jax.ShapeDtypeStruct((8, D), F32),
        ],
        aliases={0: 0},
        scratch_shapes=[
            pltpu.VMEM((8, D), F32), pltpu.VMEM((8, D), F32), pltpu.VMEM((8, D), F32),
            pltpu.VMEM((TM, D), F32), pltpu.VMEM((TM, D), F32), pltpu.VMEM((TM, D), F32),
        ],
        args=(dproj, proj, proj, h_lru, h_lru, c_sv, r_sv, i_sv, la_sv, mult_sv, dhl, conv_w, wa, wx, lam),
        riders=riders,
    )


def _in_proj_bwd(dproj, w_in, part, prev=None, riders=()):
    rows = dproj.shape[0]
    tm = _heavy_tile(rows)
    nt = rows // tm
    split = max(1, nt // 4 + 1) if nt > 1 else 1
    first = 0 if part == 0 else split
    count = split if part == 0 else nt - split

    def body(*refs):
        dproj_ref, w_hbm, du_ref, w_sc, w_sem = refs[-5:]

        @pl.when(pl.program_id(0) == 0)
        def _():
            moves = [((0, 3), 0), ((4, 1), 3), ((3, 1), 4), ((5, 3), 5)]
            copies = [pltpu.make_async_copy(w_hbm.at[:, pl.ds(src * D, n * D)], w_sc.at[:, pl.ds(dst * D, n * D)], w_sem.at[q])
                      for q, ((src, n), dst) in enumerate(moves)]
            for cp in copies:
                cp.start()
            for cp in copies:
                cp.wait()

        for a in range(0, D, D // 2):
            du_ref[:, a:a + D // 2] = _dot_nt(dproj_ref[...], w_sc[a:a + D // 2, :])

    in_specs = [pl.BlockSpec((tm, IN_COLS), lambda i: (first + i, 0)), _ANY]
    args = (dproj, w_in)
    if prev is not None:
        in_specs = [_ANY] + in_specs
        args = (prev,) + args
    return _hosted_call(
        body,
        name="in_proj_bwd_%d" % part,
        grid=(count,),
        in_specs=in_specs,
        out_specs=[pl.BlockSpec((tm, D), lambda i: (first + i, 0))],
        out_shape=[jax.ShapeDtypeStruct((rows, D), F32)],
        scratch_shapes=[pltpu.VMEM((D, IN_COLS), BF16), pltpu.SemaphoreType.DMA((4,))],
        aliases={0: 0} if prev is not None else None,
        args=args,
        riders=riders,
    )


def _norm1_bwd(du, dh1, head, x2d, rstd1, norm_w, riders=()):
    rows = du.shape[0]

    def body(du_ref, dh1_ref, head_ref, x_ref, rstd_ref, nw_ref, gx_ref, ghead_ref, stats_ref):
        i = pl.program_id(0)

        @pl.when(i == 0)
        def _():
            stats_ref[...] = jnp.zeros_like(stats_ref)

        def finish(h0, out_ref):
            du = du_ref[...]
            rs = rstd_ref[...]
            n = h0 * rs
            stats_ref[0:1, :] += jnp.sum(du * n, axis=0, keepdims=True)
            dn = du * nw_ref[...]
            out_ref[...] = dh1_ref[...] + rs * (dn - n * jnp.mean(dn * n, axis=-1, keepdims=True))

        @pl.when(i == 0)
        def _():
            finish(head_ref[...], ghead_ref)

        @pl.when(i > 0)
        def _():
            finish(x_ref[...], gx_ref)

    tile = pl.BlockSpec((TM, D), lambda i: (i, 0))
    return _hosted_call(
        body,
        name="norm1_bwd",
        grid=(rows // TM,),
        in_specs=[
            tile, tile,
            pl.BlockSpec((FRONT, D), lambda i: (0, 0)),
            pl.BlockSpec((TM, D), lambda i: (jnp.maximum(i - 1, 0), 0)),
            pl.BlockSpec((TM, 1), lambda i: (i, 0)),
            pl.BlockSpec((1, D), lambda i: (0, 0)),
        ],
        out_specs=[
            pl.BlockSpec((TM, D), lambda i: (jnp.maximum(i - 1, 0), 0)),
            pl.BlockSpec((FRONT, D), lambda i: (0, 0)),
            pl.BlockSpec((8, D), lambda i: (0, 0)),
        ],
        out_shape=[
            jax.ShapeDtypeStruct(x2d.shape, F32),
            jax.ShapeDtypeStruct((FRONT, D), F32),
            jax.ShapeDtypeStruct((8, D), F32),
        ],
        scratch_shapes=[],
        args=(du, dh1, head, x2d, rstd1, norm_w),
        riders=riders,
    )


def _matmul_tn(name, x, dy, out_cols, col0=0, prev=None, k_block=None, n_block=None, riders=(), col_map=None):
    col_map = col_map or (lambda n: n)
    rows, kdim = x.shape
    ndim = dy.shape[1]
    kb = k_block or kdim
    nb = n_block or ndim
    step_bytes = lambda t: 2 * t * (kb * x.dtype.itemsize + nb * dy.dtype.itemsize) + 2 * kb * nb * 4
    tr = next(t for t in (2816, 1408, TM_HEAVY, TM) if rows % t == 0 and (t == TM or step_bytes(t) <= TN_VMEM_BUDGET))
    nr, nk, nn = rows // tr, kdim // kb, ndim // nb
    cb0 = col0 // nb

    def body(*refs):
        x_ref, dy_ref, out_ref = refs[-3], refs[-2], refs[-1]
        cut = (nb // 2) // LANES * LANES
        halves = ((0, cut), (cut, nb)) if cut else ((0, nb),)

        @pl.when(pl.program_id(2) == 0)
        def _():
            for a, b in halves:
                out_ref[:, a:b] = _dot_tn(x_ref[...].astype(BF16), dy_ref[:, a:b].astype(BF16))

        @pl.when(pl.program_id(2) > 0)
        def _():
            for a, b in halves:
                out_ref[:, a:b] += _dot_tn(x_ref[...].astype(BF16), dy_ref[:, a:b].astype(BF16))

    in_specs = [
        pl.BlockSpec((tr, kb), lambda n, k, r: (r, k)),
        pl.BlockSpec((tr, nb), lambda n, k, r: (r, n)),
    ]
    args = [x, dy]
    aliases = {}
    if prev is not None:
        in_specs = [pl.BlockSpec(memory_space=pl.ANY)] + in_specs
        args = [prev] + args
        aliases = {0: 0}
    (out,), rider_outs = _hosted_call(
        body,
        name=name,
        grid=(nn, nk, nr),
        in_specs=in_specs,
        out_specs=[pl.BlockSpec((kb, nb), lambda n, k, r: (k, cb0 + col_map(n)))],
        out_shape=[jax.ShapeDtypeStruct((kdim, out_cols), F32)],
        scratch_shapes=[],
        aliases=aliases,
        args=args,
        riders=riders,
    )
    return (out, rider_outs) if riders else out


def _local_step(x2d, target, w, plan):
    rows = FRONT + x2d.shape[0]
    head = jnp.concatenate([jnp.zeros((PAD_ROWS, D), F32), w["meta_tokens"]], axis=0)
    cos_t, sin_t = _rope_tables(rows)
    intra, intra_t, q_dec, k_dec, c_dec = _decay_tables()
    grads = {}

    def hosted(host, fn, *args, **kwargs):
        outs, rider_outs = fn(*args, riders=plan.riders(host, w, grads), **kwargs)
        plan.after(host, rider_outs, w, grads)
        return outs

    (u1, rstd1), _ = _norm1(head, x2d, w["mix_norm_w"])
    proj, w["w_in"] = hosted("in_proj", _in_proj, u1, w["w_in_shard"], w["route"], cos_t, sin_t)
    o, sprev = hosted("retention_fwd", _retention_fwd, proj, intra, q_dec, k_dec, c_dec)
    lru_args = (w["conv_w"], w["conv_b"], w["lru_wa"], w["lru_wx"], w["lru_ba"], w["lru_bx"], w["lru_lambda"])
    lru_saved = hosted("lru_fwd", _lru_fwd, proj, *lru_args)
    h_lru = lru_saved[0]
    y_ret, y_lru, h1, u2, rstd2 = hosted("mix_fwd", _mix_fwd, head, x2d, o, proj, h_lru, w["w_branch_ret"],
                                         w["w_branch_lru"], w["w_out"], w["ffn_norm_w"])
    silu, dsilu, act = hosted("ffn_fwd_gate", _ffn_fwd_gate, u2, w["w_ffn_in"])
    dh2, stats_loss = _ffn_out_loss(act, h1, w["w_ffn_out"], target, w["final_norm_w"])

    dg, dup, dh2b, dh1, stats_ffn = _ffn_bwd(dh2, silu, dsilu, h1, rstd2, w["w_ffn_in"], w["w_ffn_out"], w["ffn_norm_w"])
    grads["w_ffn_in"] = _matmul_tn("dw_ffn_up", u2, dup, 2 * FFN, col0=FFN, n_block=FFN_HALF,
                                   prev=_matmul_tn("dw_ffn_gate", u2, dg, 2 * FFN, n_block=FFN_HALF))
    grads["w_ffn_out"] = _matmul_tn("dw_ffn_out", act, dh2b, D, k_block=FFN_HALF)
    (dproj, do, dhl, mixed, a_ret, a_lru, dy_ret, dy_lru) = hosted(
        "mix_bwd", _mix_bwd, dh1, o, proj, h_lru, y_ret, y_lru, w["w_branch_ret"], w["w_branch_lru"], w["w_out"])
    grads["w_out"] = _matmul_tn("dw_out", mixed, dh1, D)
    grads["w_branch_ret"] = _matmul_tn("dw_branch_ret", a_ret, dy_ret, D)
    grads["w_branch_lru"] = _matmul_tn("dw_branch_lru", a_lru, dy_lru, D)
    (dproj,) = hosted("retention_bwd", _retention_bwd, dproj, proj, do, sprev, intra, intra_t, q_dec, k_dec, c_dec,
                      cos_t, sin_t)
    dproj, grads["lru_wa"], grads["lru_wx"], stats_lru = hosted(
        "lru_bwd", _lru_bwd, dproj, proj, lru_saved, dhl, w["conv_w"], w["lru_wa"], w["lru_wx"], w["lru_lambda"])
    grads["w_in"], rider_outs = _matmul_tn("dw_in", u1, dproj, IN_COLS, n_block=D, col_map=_swap_3_4,
                                           riders=plan.riders("dw_in", w, grads))
    plan.after("dw_in", rider_outs, w, grads)
    (du1,) = hosted("in_proj_bwd_0", _in_proj_bwd, dproj, w["w_in"], 0)
    (du1,) = hosted("in_proj_bwd_1", _in_proj_bwd, dproj, w["w_in"], 1, du1)
    (grad_x, grad_head, stats_in), _ = _norm1_bwd(du1, dh1, head, x2d, rstd1, w["mix_norm_w"])
    return grad_x, grad_head, grads, [stats_loss, stats_ffn, stats_in, stats_lru]


BIG_PIECES = {
    "w_in": ("col", (D, 2 * D)),
    "w_ffn_in": ("col", (D, FFN_HALF)),
    "w_ffn_out": ("row", (FFN // 4, D)),
    "w_branch_ret": ("row", (D // 4, D)),
    "w_branch_lru": ("row", (D // 4, D)),
    "w_out": ("row", (D // 4, D)),
    "lru_wa": ("lru", (LRU_BLOCKS, LRU_BLOCK // 4, LRU_BLOCK)),
    "lru_wx": ("lru", (LRU_BLOCKS, LRU_BLOCK // 4, LRU_BLOCK)),
}
SMALL_PIECES = {"meta_tokens": ("col", (N_META, D // 4)), "conv_w": ("col", (4, D // 4))}


def _full_shape(kind, shard):
    if kind == "col":
        return (shard[0], 4 * shard[1])
    if kind == "row":
        return (4 * shard[0], shard[1])
    return (shard[0], 4 * shard[1], shard[2])


def _half_shape(kind, shard):
    return (shard[0] // 2,) + tuple(shard[1:])


def _aligned(start, multiple):
    return start if isinstance(start, int) else pl.multiple_of(start, multiple)


def _lead(h, size):
    if h is None:
        return pl.ds(0, size)
    return pl.ds(_aligned(h * (size // 2), size // 2), size // 2)


def _full_region(ref, kind, shard, s, h):
    if kind == "col":
        return ref.at[_lead(h, shard[0]), pl.ds(_aligned(s * shard[1], shard[1]), shard[1])]
    if kind == "row":
        size = shard[0] if h is None else shard[0] // 2
        start = s * shard[0] + (0 if h is None else h * (shard[0] // 2))
        return ref.at[pl.ds(_aligned(start, BF16_ROWS), size), :]
    return ref.at[_lead(h, shard[0]), pl.ds(_aligned(s * shard[1], shard[1]), shard[1]), :]


def _shard_region(ref, shard, h):
    return ref.at[_lead(h, shard[0])]


def _place():
    x, y, c = lax.axis_index("x"), lax.axis_index("y"), lax.axis_index("c")
    return x, y, c, 2 * x + y


def _other_chip(s, c, k):
    s2 = jnp.bitwise_xor(s, k)
    return s2, (s2 // 2, s2 % 2, c)


def _remote(src, dst, send_sem, recv_sem, dev):
    return pltpu.make_async_remote_copy(src_ref=src, dst_ref=dst, send_sem=send_sem, recv_sem=recv_sem,
                                        device_id=dev, device_id_type=MESH)


_ANY = pl.BlockSpec(memory_space=pl.ANY)


class _Rider:
    def __init__(self, ins, out_shapes, sem_shapes, build, aliased=False):
        self.ins, self.out_shapes, self.sem_shapes, self.build, self.aliased = ins, out_shapes, sem_shapes, build, aliased


def _hosted_call(body, *, name, grid, in_specs, out_specs, out_shape, scratch_shapes, args, riders=(), aliases=None,
                 prefetch=None, riders_after_body=False):
    n_in, n_out, n_sc = len(in_specs), len(out_shape), len(scratch_shapes)
    r_in = [a for r in riders for a in r.ins]
    r_out = [s for r in riders for s in r.out_shapes]
    r_sem = [s for r in riders for s in r.sem_shapes]
    lead = () if prefetch is None else (prefetch,)
    assert prefetch is None or not (aliases or any(r.aliased for r in riders))

    def full_body(*refs):
        head, refs = refs[:len(lead)], refs[len(lead):]
        ins, rin = refs[:n_in], refs[n_in:n_in + len(r_in)]
        o0 = n_in + len(r_in)
        outs, rout = refs[o0:o0 + n_out], refs[o0 + n_out:o0 + n_out + len(r_out)]
        s0 = o0 + n_out + len(r_out)
        scratch, rsem = refs[s0:s0 + n_sc], refs[s0 + n_sc:]
        starts, waits = [], []
        pi = po = ps = 0
        for r in riders:
            st, wt = r.build(rin[pi:pi + len(r.ins)], rout[po:po + len(r.out_shapes)], rsem[ps:ps + len(r.sem_shapes)])
            starts += st
            waits += wt
            pi, po, ps = pi + len(r.ins), po + len(r.out_shapes), ps + len(r.sem_shapes)
        first = functools.reduce(jnp.logical_and, [pl.program_id(d) == 0 for d in range(len(grid))])
        last = functools.reduce(jnp.logical_and, [pl.program_id(d) == grid[d] - 1 for d in range(len(grid))])

        def start_riders():
            @pl.when(first)
            def _():
                for cp in starts:
                    cp.start()

        if riders and not riders_after_body:
            start_riders()
        body(*head, *ins, *outs, *scratch)
        if riders and riders_after_body:
            start_riders()
        if riders:
            @pl.when(last)
            def _():
                for wait in waits:
                    wait()

    io_aliases = dict(aliases or {})
    pi = po = 0
    for r in riders:
        if r.aliased:
            for q in range(len(r.ins)):
                io_aliases[n_in + pi + q] = n_out + po + q
        pi, po = pi + len(r.ins), po + len(r.out_shapes)
    specs = dict(
        grid=grid,
        in_specs=list(in_specs) + [_ANY] * len(r_in),
        out_specs=list(out_specs) + [_ANY] * len(r_out),
        scratch_shapes=list(scratch_shapes) + r_sem,
    )
    if prefetch is not None:
        specs = dict(grid_spec=pltpu.PrefetchScalarGridSpec(num_scalar_prefetch=1, **specs))
    res = pl.pallas_call(
        full_body,
        name=name,
        out_shape=list(out_shape) + r_out,
        input_output_aliases=io_aliases,
        compiler_params=_params(("arbitrary",) * len(grid)),
        **specs,
    )(*lead, *args, *r_in)
    rider_outs, po = [], n_out
    for r in riders:
        rider_outs.append(list(res[po:po + len(r.out_shapes)]))
        po += len(r.out_shapes)
    return list(res[:n_out]), rider_outs


def _run_riders(name, riders):
    r_in = [a for r in riders for a in r.ins]
    r_out = [s for r in riders for s in r.out_shapes]
    r_sem = [s for r in riders for s in r.sem_shapes]

    def body(*refs):
        rin, rout, rsem = refs[:len(r_in)], refs[len(r_in):len(r_in) + len(r_out)], refs[len(r_in) + len(r_out):]
        pi = po = ps = 0
        all_waits = []
        for r in riders:
            starts, waits = r.build(rin[pi:pi + len(r.ins)], rout[po:po + len(r.out_shapes)], rsem[ps:ps + len(r.sem_shapes)])
            for cp in starts:
                cp.start()
            all_waits += waits
            pi, po, ps = pi + len(r.ins), po + len(r.out_shapes), ps + len(r.sem_shapes)
        for wait in all_waits:
            wait()

    io_aliases = {}
    pi = po = 0
    for r in riders:
        if r.aliased:
            for q in range(len(r.ins)):
                io_aliases[pi + q] = po + q
        pi, po = pi + len(r.ins), po + len(r.out_shapes)
    res = pl.pallas_call(
        body,
        name=name,
        in_specs=[_ANY] * len(r_in),
        out_specs=[_ANY] * len(r_out),
        out_shape=r_out,
        scratch_shapes=r_sem,
        input_output_aliases=io_aliases,
    )(*r_in)
    outs, po = [], 0
    for r in riders:
        outs.append(list(res[po:po + len(r.out_shapes)]))
        po += len(r.out_shapes)
    return outs


def _piece(name):
    if name in BIG_PIECES:
        return (name, *BIG_PIECES[name], True)
    return (name, *SMALL_PIECES[name], False)


def _gather_rider(shards, names):
    pieces = [_piece(n) for n in names]
    n = len(pieces)

    def build(ins, outs, sems):
        local_sem, ici_send, ici_recv = sems
        _, _, c, s = _place()
        starts, waits = [], []
        for p, (_, kind, shard, split) in enumerate(pieces):
            cp = pltpu.make_async_copy(ins[p], _full_region(outs[p], kind, shard, s, None), local_sem.at[p])
            starts.append(cp)
            waits.append(cp.wait)
            h = c if split else None
            for k in (1, 2, 3):
                s2, dev = _other_chip(s, c, k)
                cp = _remote(_shard_region(ins[p], shard, h), _full_region(outs[p], kind, shard, s, h),
                             ici_send.at[p, k - 1], ici_recv.at[p, k - 1], dev)
                starts.append(cp)
                waits.append(cp.wait_send)
                region = _full_region(outs[p], kind, shard, s2, h)
                waits.append(_remote(region, region, ici_send.at[p, k - 1], ici_recv.at[p, k - 1], dev).wait_recv)
        return starts, waits

    return _Rider(
        [shards[name] for name in names],
        [jax.ShapeDtypeStruct(_full_shape(kind, shard), shards[name].dtype) for name, kind, shard, _ in pieces],
        [pltpu.SemaphoreType.DMA((n,)), pltpu.SemaphoreType.DMA((n, 3)), pltpu.SemaphoreType.DMA((n, 3))],
        build)


def _forward_rider(gathered, names):
    pieces = [_piece(n) for n in names]
    n = len(pieces)

    def build(ins, outs, sems):
        fwd_send, fwd_recv = sems
        x, y, c, s = _place()
        sibling = (x, y, 1 - c)
        starts, waits = [], []
        for p, (_, kind, shard, _) in enumerate(pieces):
            for k in (1, 2, 3):
                s2, _ = _other_chip(s, c, k)
                mine = _full_region(outs[p], kind, shard, s2, c)
                theirs = _full_region(outs[p], kind, shard, s2, 1 - c)
                cp = _remote(mine, mine, fwd_send.at[p, k - 1], fwd_recv.at[p, k - 1], sibling)
                starts.append(cp)
                waits.append(cp.wait_send)
                waits.append(_remote(theirs, theirs, fwd_send.at[p, k - 1], fwd_recv.at[p, k - 1], sibling).wait_recv)
        return starts, waits

    return _Rider(
        [gathered[name] for name in names],
        [jax.ShapeDtypeStruct(gathered[name].shape, gathered[name].dtype) for name in names],
        [pltpu.SemaphoreType.DMA((n, 3)), pltpu.SemaphoreType.DMA((n, 3))],
        build, aliased=True)


def _pair_exchange_rider(grads, names):
    n = len(names)

    def build(ins, outs, sems):
        send_sem, recv_sem = sems
        x, y, c, _ = _place()
        sibling = (x, y, 1 - c)
        starts, waits = [], []
        for p, name in enumerate(names):
            kind, shard = BIG_PIECES[name]
            for s2 in range(4):
                cp = _remote(_full_region(ins[p], kind, shard, s2, 1 - c), outs[p].at[s2], send_sem.at[p, s2],
                             recv_sem.at[p, s2], sibling)
                starts.append(cp)
                waits.append(cp.wait_send)
                waits.append(_remote(outs[p].at[s2], outs[p].at[s2], send_sem.at[p, s2], recv_sem.at[p, s2], sibling).wait_recv)
        return starts, waits

    return _Rider(
        [grads[name] for name in names],
        [jax.ShapeDtypeStruct((4,) + _half_shape(*BIG_PIECES[name]), F32) for name in names],
        [pltpu.SemaphoreType.DMA((n, 4))] * 2,
        build)


def _half_specs(kind, shard):
    half = shard[0] // 2
    if kind == "col":
        full = pl.BlockSpec((half, shard[1]), lambda j, pr: (pr[1], j))
        buf = pl.BlockSpec((None, half, shard[1]), lambda j, pr: (j, 0, 0))
    elif kind == "row":
        full = pl.BlockSpec((half, shard[1]), lambda j, pr: (2 * j + pr[1], 0))
        buf = pl.BlockSpec((None, half, shard[1]), lambda j, pr: (j, 0, 0))
    else:
        full = pl.BlockSpec((half, shard[1], shard[2]), lambda j, pr: (pr[1], j, 0))
        buf = pl.BlockSpec((None, half, shard[1], shard[2]), lambda j, pr: (j, 0, 0, 0))
    return full, buf


def _pair_sum(name, grad, recv, place):
    kind, shard = BIG_PIECES[name]
    full, buf = _half_specs(kind, shard)

    def body(pr, g_ref, r_ref, o_ref):
        o_ref[...] = (g_ref[...] + r_ref[...]).astype(BF16)

    return pl.pallas_call(
        body,
        name="pair_sum_" + name,
        grid_spec=pltpu.PrefetchScalarGridSpec(num_scalar_prefetch=1, grid=(4,), in_specs=[full, buf], out_specs=buf),
        out_shape=jax.ShapeDtypeStruct((4,) + _half_shape(kind, shard), BF16),
        compiler_params=_params(("arbitrary",)),
    )(place, grad, recv)


def _chip_exchange_rider(sums, names):
    n = len(names)

    def build(ins, outs, sems):
        send_sem, recv_sem = sems
        _, _, c, s = _place()
        starts, waits = [], []
        for p in range(n):
            for k in (1, 2, 3):
                s2, dev = _other_chip(s, c, k)
                cp = _remote(ins[p].at[s2], outs[p].at[k - 1], send_sem.at[p, k - 1], recv_sem.at[p, k - 1], dev)
                starts.append(cp)
                waits.append(cp.wait_send)
                waits.append(_remote(outs[p].at[k - 1], outs[p].at[k - 1], send_sem.at[p, k - 1], recv_sem.at[p, k - 1],
                                     dev).wait_recv)
        return starts, waits

    return _Rider(
        [sums[name] for name in names],
        [jax.ShapeDtypeStruct((3,) + _half_shape(*BIG_PIECES[name]), BF16) for name in names],
        [pltpu.SemaphoreType.DMA((n, 3))] * 2,
        build)


def _chip_sum(name, grad, recv_pair, recv_chip, place):
    kind, shard = BIG_PIECES[name]
    half = shard[0] // 2
    tail = tuple(shard[1:])
    zeros = (0,) * len(tail)
    nt = 4 if kind != "lru" and half % (4 * BF16_ROWS) == 0 else 1
    rows = half // nt
    if kind == "col":
        full = pl.BlockSpec((rows,) + tail, lambda j, pr: (pr[1] * nt + j, pr[0]))
    elif kind == "row":
        full = pl.BlockSpec((rows,) + tail, lambda j, pr: ((2 * pr[0] + pr[1]) * nt + j, 0))
    else:
        full = pl.BlockSpec((rows,) + tail, lambda j, pr: (pr[1], pr[0], 0))
    pair = pl.BlockSpec((None, rows) + tail, lambda j, pr: (pr[0], j) + zeros)
    chip = pl.BlockSpec((3, rows) + tail, lambda j, pr: (0, j) + zeros)
    out = pl.BlockSpec((rows,) + tail, lambda j, pr: (pr[1] * nt + j,) + zeros)

    def body(pr, g_ref, rp_ref, rc_ref, o_ref):
        total = g_ref[...] + rp_ref[...]
        for k in range(3):
            total = total + rc_ref[k].astype(F32)
        o_ref[...] = total

    return pl.pallas_call(
        body,
        name="chip_sum_" + name,
        grid_spec=pltpu.PrefetchScalarGridSpec(num_scalar_prefetch=1, grid=(nt,), in_specs=[full, pair, chip], out_specs=out),
        out_shape=jax.ShapeDtypeStruct(shard, F32),
        compiler_params=_params(("arbitrary",)),
    )(place, grad, recv_pair, recv_chip)


def _sibling_exchange_rider(halves, names):
    n = len(names)

    def build(ins, outs, sems):
        send_sem, recv_sem = sems
        x, y, c, _ = _place()
        sibling = (x, y, 1 - c)
        starts, waits = [], []
        for p, name in enumerate(names):
            shard = BIG_PIECES[name][1]
            mine = _shard_region(outs[p], shard, c)
            theirs = _shard_region(outs[p], shard, 1 - c)
            cp = _remote(mine, mine, send_sem.at[p], recv_sem.at[p], sibling)
            starts.append(cp)
            waits.append(cp.wait_send)
            waits.append(_remote(theirs, theirs, send_sem.at[p], recv_sem.at[p], sibling).wait_recv)
        return starts, waits

    return _Rider(
        [halves[name] for name in names],
        [jax.ShapeDtypeStruct(BIG_PIECES[name][1], F32) for name in names],
        [pltpu.SemaphoreType.DMA((n,))] * 2,
        build, aliased=True)


FIRST_WEIGHTS = ["meta_tokens", "conv_w"]
WEIGHT_GROUPS = {
    "lru": ["lru_wa", "lru_wx"],
    "branch": ["w_branch_ret", "w_branch_lru", "w_out"],
    "ffn_in": ["w_ffn_in"],
    "ffn_out": ["w_ffn_out"],
}
WEIGHT_SCHEDULE = {
    "in_proj": [("gather", "lru")],
    "retention_fwd": [("forward", "lru"), ("gather", "branch")],
    "lru_fwd": [("forward", "branch"), ("gather", "ffn_in")],
    "mix_fwd": [("forward", "ffn_in"), ("gather", "ffn_out")],
    "ffn_fwd_gate": [("forward", "ffn_out")],
}
GRAD_GROUPS = {
    "ffn_in": ["w_ffn_in"],
    "ffn_out": ["w_ffn_out"],
    "mixer": ["w_out", "w_branch_ret", "w_branch_lru", "lru_wa", "lru_wx"],
    "in": ["w_in"],
}
GRAD_SCHEDULE = {
    "mix_bwd": [("pair", "ffn_in"), ("pair", "ffn_out")],
    "retention_bwd": [("chip", "ffn_out")],
    "lru_bwd": [("chip", "ffn_in"), ("sibling", "ffn_out")],
    "dw_in": [("sibling", "ffn_in"), ("pair", "mixer")],
    "in_proj_bwd_0": [("chip", "mixer"), ("pair", "in")],
    "in_proj_bwd_1": [("sibling", "mixer"), ("chip", "in")],
}


class _CommPlan:
    def __init__(self, shards, place):
        self.shards, self.place = shards, place
        self.late = {}
        self.recv_pair, self.sums, self.recv_chip, self.halves, self.final = {}, {}, {}, {}, {}

    def _grad_rider(self, stage, group, grads):
        names = GRAD_GROUPS[group]
        if stage == "pair":
            return _pair_exchange_rider(grads, names)
        if stage == "chip":
            return _chip_exchange_rider(self.sums, names)
        return _sibling_exchange_rider(self.halves, names)

    def _grad_after(self, stage, group, outs, grads):
        names = GRAD_GROUPS[group]
        if stage == "pair":
            for n, o in zip(names, outs):
                self.recv_pair[n] = o
                self.sums[n] = _pair_sum(n, grads[n], o, self.place)
        elif stage == "chip":
            for n, o in zip(names, outs):
                self.halves[n] = _chip_sum(n, grads[n], self.recv_pair[n], o, self.place)
        else:
            self.final.update(zip(names, outs))

    def riders(self, host, w, grads):
        if host in WEIGHT_SCHEDULE:
            return [_gather_rider(self.shards, WEIGHT_GROUPS[group]) if stage == "gather"
                    else _forward_rider(self.late, WEIGHT_GROUPS[group]) for stage, group in WEIGHT_SCHEDULE[host]]
        return [self._grad_rider(stage, group, grads) for stage, group in GRAD_SCHEDULE.get(host, [])]

    def after(self, host, rider_outs, w, grads):
        for (stage, group), outs in zip(WEIGHT_SCHEDULE.get(host, []), rider_outs):
            (self.late if stage == "gather" else w).update(zip(WEIGHT_GROUPS[group], outs))
        for (stage, group), outs in zip(GRAD_SCHEDULE.get(host, []), rider_outs):
            self._grad_after(stage, group, outs, grads)

    def finish(self, partial):
        outs, (blocks,) = _run_riders("tail_exchange", [_sibling_exchange_rider(self.halves, GRAD_GROUPS["in"]),
                                                        _small_exchange_rider(partial)])
        self.final.update(zip(GRAD_GROUPS["in"], outs))
        return self.final, blocks


def _adamw_math(w, g, m, v):
    m = ADAM_B1 * m + (1.0 - ADAM_B1) * g
    v = ADAM_B2 * v + (1.0 - ADAM_B2) * (g * g)
    m_hat = m / (1.0 - ADAM_B1 ** ADAM_STEP)
    v_hat = v / (1.0 - ADAM_B2 ** ADAM_STEP)
    delta = -ADAM_LR * (m_hat / (jnp.sqrt(v_hat) + ADAM_EPS) + ADAM_WD * w)
    return delta, m, v


def _adamw(name, g, w, m, v):
    rows, cols = g.shape
    tr = rows // 4 if rows % 32 == 0 else rows

    def body(g_ref, w_ref, m_ref, v_ref, go_ref, d_ref, mo_ref, vo_ref):
        gv = g_ref[...]
        delta, m2, v2 = _adamw_math(w_ref[...], gv, m_ref[...], v_ref[...])
        go_ref[...] = gv
        d_ref[...] = delta
        mo_ref[...] = m2
        vo_ref[...] = v2

    spec = pl.BlockSpec((tr, cols), lambda i: (i, 0))
    return pl.pallas_call(
        body,
        name="adamw_" + name,
        grid=(rows // tr,),
        in_specs=[spec] * 4,
        out_specs=[spec] * 4,
        out_shape=[jax.ShapeDtypeStruct((rows, cols), F32)] * 4,
        compiler_params=_params(("arbitrary",)),
    )(g, w, m, v)


SMALL_ROWS = 48
VEC_ROWS = {"final_norm_w": 1, "ffn_norm_w": 8, "mix_norm_w": 16, "conv_b": 24, "lru_ba": 25, "lru_bx": 26, "lru_lambda": 27}
VEC_NAMES = list(VEC_ROWS)
CONV_W_ROW = 28
META_ROW = 32


def _small_exchange_rider(partial):
    def build(ins, outs, sems):
        local_sem, send_sem, recv_sem = sems
        _, _, c, s = _place()
        me = 2 * s + c
        cp = pltpu.make_async_copy(ins[0], outs[0].at[me], local_sem)
        starts, waits = [cp], [cp.wait]
        for k in range(1, 8):
            peer = jnp.bitwise_xor(me, k)
            dev = (peer // 4, (peer // 2) % 2, peer % 2)
            rd = _remote(ins[0], outs[0].at[me], send_sem.at[k - 1], recv_sem.at[k - 1], dev)
            starts.append(rd)
            waits.append(rd.wait_send)
            waits.append(_remote(ins[0], outs[0].at[peer], send_sem.at[k - 1], recv_sem.at[k - 1], dev).wait_recv)
        return starts, waits

    return _Rider([partial], [jax.ShapeDtypeStruct((8, SMALL_ROWS, D), F32)],
                  [pltpu.SemaphoreType.DMA, pltpu.SemaphoreType.DMA((7,)), pltpu.SemaphoreType.DMA((7,))], build)


def _small_update(blocks, place, vecs, conv, meta):
    nvec = len(VEC_NAMES)
    qcols = D // 4

    def in_order(ref):
        total = ref[0]
        for d in range(1, 8):
            total = total + ref[d]
        return total

    def body(pr, blocks_ref, cols_ref, *refs):
        vec_refs = refs[:3 * nvec]
        conv_refs = refs[3 * nvec:3 * nvec + 3]
        meta_refs = refs[3 * nvec + 3:3 * nvec + 6]
        outs = refs[3 * nvec + 6:]
        loss_ref, vec_out, conv_out, meta_out = outs[0], outs[1:5], outs[5:9], outs[9:13]
        tot, col = in_order(blocks_ref), in_order(cols_ref)
        loss_ref[...] = jnp.sum(tot[0:1, :], axis=1, keepdims=True)
        for o in vec_out:
            o[...] = jnp.zeros_like(o)
        for j, name in enumerate(VEC_NAMES):
            w, m, v = (r[...] for r in vec_refs[3 * j:3 * j + 3])
            g = tot[VEC_ROWS[name]:VEC_ROWS[name] + 1, :]
            if name == "lru_lambda":
                g = -g / (1.0 + jnp.exp(w))
            for o, val in zip(vec_out, (g,) + _adamw_math(w, g, m, v)):
                o[j:j + 1, :] = val
        for row, n_rows, ins, group in ((CONV_W_ROW, 4, conv_refs, conv_out), (META_ROW, N_META, meta_refs, meta_out)):
            g = col[row:row + n_rows, :]
            w, m, v = (r[...] for r in ins)
            for o, val in zip(group, (g,) + _adamw_math(w, g, m, v)):
                o[...] = val

    whole = lambda shape: pl.BlockSpec(shape, lambda i, pr: (0,) * len(shape))
    in_specs = [whole((8, SMALL_ROWS, D)), pl.BlockSpec((8, SMALL_ROWS, qcols), lambda i, pr: (0, 0, pr[0]))]
    in_specs += [whole((1, D))] * (3 * nvec) + [whole((4, qcols))] * 3 + [whole((N_META, qcols))] * 3
    out_shapes = [(1, 1)] + [(8, D)] * 4 + [(4, qcols)] * 4 + [(N_META, qcols)] * 4
    return pl.pallas_call(
        body,
        name="small_update",
        grid_spec=pltpu.PrefetchScalarGridSpec(num_scalar_prefetch=1, grid=(1,), in_specs=in_specs,
                                               out_specs=[whole(s) for s in out_shapes]),
        out_shape=[jax.ShapeDtypeStruct(s, F32) for s in out_shapes],
        compiler_params=_params(("arbitrary",)),
    )(place, blocks, blocks, *[a for t in vecs for a in t], *conv, *meta)


WEIGHT_ORDER = ["meta_tokens", "mix_norm_w", "w_in", "conv_w", "conv_b", "lru_wa", "lru_ba", "lru_wx", "lru_bx", "lru_lambda",
                "w_branch_ret", "w_branch_lru", "w_out", "ffn_norm_w", "w_ffn_in", "w_ffn_out", "final_norm_w"]


def kernel(x, meta_tokens, mix_norm_w, w_in, conv_w, conv_b, lru_wa, lru_ba, lru_wx, lru_bx, lru_lambda, w_branch_ret, w_branch_lru, w_out, ffn_norm_w, w_ffn_in, w_ffn_out, final_norm_w, loss_target, m_meta_tokens, m_mix_norm_w, m_w_in, m_conv_w, m_conv_b, m_lru_wa, m_lru_ba, m_lru_wx, m_lru_bx, m_lru_lambda, m_w_branch_ret, m_w_branch_lru, m_w_out, m_ffn_norm_w, m_w_ffn_in, m_w_ffn_out, m_final_norm_w, v_meta_tokens, v_mix_norm_w, v_w_in, v_conv_w, v_conv_b, v_lru_wa, v_lru_ba, v_lru_wx, v_lru_bx, v_lru_lambda, v_w_branch_ret, v_w_branch_lru, v_w_out, v_ffn_norm_w, v_w_ffn_in, v_w_ffn_out, v_final_norm_w):
    args = locals()
    wts = {n: args[n] for n in WEIGHT_ORDER}
    mom = {n: args["m_" + n] for n in WEIGHT_ORDER}
    var = {n: args["v_" + n] for n in WEIGHT_ORDER}
    place = jnp.stack([2 * lax.axis_index("x") + lax.axis_index("y"), lax.axis_index("c")]).astype(jnp.int32)

    shards = {n: wts[n][0].astype(BF16) for n in BIG_PIECES}
    shards["meta_tokens"] = wts["meta_tokens"]
    shards["conv_w"] = wts["conv_w"][0]
    plan = _CommPlan(shards, place)
    (first,) = _run_riders("gather_first", [_gather_rider(shards, FIRST_WEIGHTS)])
    w = dict(zip(FIRST_WEIGHTS, first))
    for n in VEC_NAMES:
        w[n] = wts[n].reshape(1, D)
    chips = jnp.bitwise_xor(place[0], jnp.array((0,) + SHARD_ORDER, dtype=jnp.int32))
    w["route"] = jnp.concatenate([place, jnp.stack([2 * chips, 2 * chips + 1], axis=1).reshape(8)])
    w["w_in_shard"] = shards["w_in"]

    grad_x, grad_head, _, stats = _local_step(x[0], loss_target[0], w, plan)
    partial = jnp.concatenate(stats + [grad_head[PAD_ROWS:]], axis=0)
    shard_grads, blocks = plan.finish(partial)

    out = {}
    for n in BIG_PIECES:
        shape2d = (-1, wts[n].shape[-1])
        res = _adamw(n, *[a.reshape(shape2d) for a in (shard_grads[n], wts[n], mom[n], var[n])])
        out[n] = [r.reshape(wts[n].shape) for r in res]

    vecs = [tuple(a[n].reshape(1, D) for a in (wts, mom, var)) for n in VEC_NAMES]
    conv = tuple(a["conv_w"][0] for a in (wts, mom, var))
    meta = tuple(a["meta_tokens"] for a in (wts, mom, var))
    res = _small_update(blocks, place, vecs, conv, meta)
    loss = res[0].reshape(())
    for j, n in enumerate(VEC_NAMES):
        out[n] = [r[j].reshape(wts[n].shape) for r in res[1:5]]
    out["conv_w"] = [r.reshape(wts["conv_w"].shape) for r in res[5:9]]
    out["meta_tokens"] = list(res[9:13])

    return (loss, grad_x.reshape(x.shape)) + tuple(out[n][kind] for kind in range(4) for n in WEIGHT_ORDER)
```

```python
import functools
import math

import jax
import jax.numpy as jnp
from jax import lax
from jax.experimental import pallas as pl
from jax.experimental.pallas import tpu as pltpu

F32 = jnp.float32
BF16 = jnp.bfloat16

LANES = 128
BF16_ROWS = 16

D = 1024
HEADS = 8
DH = 128
CHUNK = 256
N_META = 16
FRONT = 256
PAD_ROWS = FRONT - N_META
LRU_BLOCKS = 4
LRU_BLOCK = 256
LRU_C = 8.0
FFN = 2816
FFN_HALF = FFN // 2
IN_COLS = 8 * D
ROPE_BASE = 10000.0
EPS = 1e-6
QK_SCALE = DH ** -0.5

ADAM_LR = 0.001
ADAM_B1 = 0.9
ADAM_B2 = 0.999
ADAM_EPS = 1e-08
ADAM_WD = 0.01
ADAM_STEP = 10

TM = 256
TM_HEAVY = 768
MXU_TILE = 256
FFN_SUB = 2 * MXU_TILE
_FFN_SUBS = [(a, min(a + FFN_SUB, FFN)) for a in range(0, FFN, FFN_SUB)]
TM_MIX_BWD = 384
TM_LRU = 384
TM_MIX_FWD = 384
TM_IN_PROJ = 1408
VMEM_LIMIT = 60 * 1024 * 1024
TN_VMEM_BUDGET = 40 * 1024 * 1024

NT_DIMS = (((1,), (1,)), ((), ()))
TN_DIMS = (((0,), (0,)), ((), ()))
MESH = pl.DeviceIdType.MESH


def _params(sem=None):
    if sem is None:
        return pltpu.CompilerParams(vmem_limit_bytes=VMEM_LIMIT)
    return pltpu.CompilerParams(dimension_semantics=sem, vmem_limit_bytes=VMEM_LIMIT)


def _dot(a, b):
    return jnp.dot(a, b, preferred_element_type=F32)


def _dot_nt(a, b):
    return lax.dot_general(a, b, NT_DIMS, preferred_element_type=F32)


def _dot_tn(a, b):
    return lax.dot_general(a, b, TN_DIMS, preferred_element_type=F32)


def _sigmoid(z):
    return 1.0 / (1.0 + jnp.exp(-z))


def _log1p(x):
    return jnp.where(x < 1e-3, x * (1.0 - x * (0.5 - x * (1.0 / 3.0))), jnp.log(1.0 + x))


def _softplus(x):
    return jnp.maximum(x, 0.0) + _log1p(jnp.exp(-jnp.abs(x)))


def _one_minus_square(a, log_a):
    x = 2.0 * log_a
    series = -x * (1.0 + x * (0.5 + x * (1.0 / 6.0)))
    return jnp.where(x > -0.02, series, 1.0 - a * a)


_GELU_K = math.sqrt(2.0 / math.pi)


def _gelu_and_grad(x):
    inner = _GELU_K * (x + 0.044715 * x * x * x)
    t = jnp.tanh(inner)
    val = 0.5 * x * (1.0 + t)
    grad = 0.5 * (1.0 + t) + 0.5 * x * (1.0 - t * t) * _GELU_K * (1.0 + 3.0 * 0.044715 * x * x)
    return val, grad


def _row_ids(i, rows, shape):
    return i * rows + lax.broadcasted_iota(jnp.int32, shape, 0)


def _rope_tables(rows):
    inv_freq = ROPE_BASE ** (-jnp.arange(0, DH, 2, dtype=F32) / DH)
    block_pos = jnp.arange(rows // FRONT, dtype=jnp.int32) * FRONT - PAD_ROWS
    coarse = block_pos.astype(F32)[:, None] * inv_freq[None, :]
    fine = jnp.arange(FRONT, dtype=F32)[:, None] * inv_freq[None, :]
    ca, sa = jnp.cos(coarse)[:, None, :], jnp.sin(coarse)[:, None, :]
    cb, sb = jnp.cos(fine)[None, :, :], jnp.sin(fine)[None, :, :]
    cos = (ca * cb - sa * sb).reshape(rows, DH // 2)
    sin = (sa * cb + ca * sb).reshape(rows, DH // 2)
    return jnp.concatenate([cos, cos], axis=1), jnp.concatenate([-sin, sin], axis=1)


def _decay_tables():
    log_g = jnp.log(1.0 - 2.0 ** (-5.0 - jnp.arange(HEADS, dtype=F32)))
    idx = jnp.arange(CHUNK, dtype=F32)
    diff = idx[:, None] - idx[None, :]
    intra = jnp.where(diff[None] >= 0, jnp.exp(jnp.maximum(diff, 0.0)[None] * log_g[:, None, None]), 0.0)
    q_decay = jnp.exp((idx + 1.0)[:, None] * log_g[None, :])
    k_decay = jnp.exp((CHUNK - 1.0 - idx)[:, None] * log_g[None, :])
    chunk_decay = jnp.exp(CHUNK * log_g)
    wide = lambda a: jnp.repeat(a, DH, axis=-1)
    return intra, jnp.swapaxes(intra, 1, 2), wide(q_decay), wide(k_decay), wide(chunk_decay[None, :])


def _norm1(head, x2d, norm_w, riders=()):
    rows = FRONT + x2d.shape[0]
    tm = TM_IN_PROJ if rows % TM_IN_PROJ == 0 else TM_MIX_FWD
    nt = rows // tm

    def body(head_hbm, x_hbm, nw_ref, u_ref, rstd_ref, h0_sc, h0_sem):
        slot = _frame_rows(head_hbm, x_hbm, h0_sc, h0_sem, pl.program_id(0), tm, nt)
        hv = h0_sc[slot]
        rs = lax.rsqrt(jnp.mean(hv * hv, axis=-1, keepdims=True) + EPS)
        u_ref[...] = ((hv * rs) * nw_ref[...]).astype(BF16)
        rstd_ref[...] = rs

    return _hosted_call(
        body,
        name="norm1",
        grid=(nt,),
        in_specs=[_ANY, _ANY, pl.BlockSpec((1, D), lambda i: (0, 0))],
        out_specs=[pl.BlockSpec((tm, D), lambda i: (i, 0)), pl.BlockSpec((tm, 1), lambda i: (i, 0))],
        out_shape=[jax.ShapeDtypeStruct((rows, D), BF16), jax.ShapeDtypeStruct((rows, 1), F32)],
        scratch_shapes=[pltpu.VMEM((2, tm, D), F32), pltpu.SemaphoreType.DMA((3,))],
        args=(head, x2d, norm_w),
        riders=riders,
    )


def _heavy_tile(rows):
    return TM_HEAVY if rows % TM_HEAVY == 0 else TM


def _lru_tile(rows):
    return TM_LRU if rows % TM_LRU == 0 else TM


def _swap_3_4(group):
    return jnp.where(group == 3, 4, jnp.where(group == 4, 3, group))


SHARD_ORDER = (2, 3, 1)
LRU_IN_COL = 3
GATES_COL = 1


def _in_proj(u, w_shard, route, cos_t, sin_t, riders=()):
    rows = u.shape[0]
    tm = TM_IN_PROJ if rows % TM_IN_PROJ == 0 else _heavy_tile(rows)
    nt = rows // tm
    kind, shard = BIG_PIECES["w_in"]

    def body(route_ref, u_hbm, wsh_ref, cos_ref, sin_ref, proj_ref, wfull_ref,
             u_sc, w_sc, u_sem, w_sem, local_sem, ici_send, ici_recv, fwd_send, fwd_recv):
        g, i = pl.program_id(0), pl.program_id(1)
        s, c = route_ref[0], route_ref[1]
        gid = route_ref[2 + g]
        sibling = (s // 2, s % 2, 1 - c)
        local = pltpu.make_async_copy(wsh_ref, _full_region(wfull_ref, kind, shard, s, None), local_sem)
        u_copies = [pltpu.make_async_copy(u_hbm.at[pl.ds(t * tm, tm)], u_sc.at[t], u_sem.at[t]) for t in range(nt)]
        sends, arrivals = [], []
        for k in (1, 2, 3):
            s2, dev = _other_chip(s, c, k)
            sends.append(_remote(_shard_region(wsh_ref, shard, c), _full_region(wfull_ref, kind, shard, s, c),
                                 ici_send.at[k - 1], ici_recv.at[k - 1], dev))
            mine = _full_region(wfull_ref, kind, shard, s2, c)
            theirs = _full_region(wfull_ref, kind, shard, s2, 1 - c)
            arrivals.append((_remote(mine, mine, ici_send.at[k - 1], ici_recv.at[k - 1], dev),
                             _remote(mine, mine, fwd_send.at[k - 1], fwd_recv.at[k - 1], sibling),
                             _remote(theirs, theirs, fwd_send.at[k - 1], fwd_recv.at[k - 1], sibling)))

        slot = g % 2
        last_tile = i == nt - 1

        def block_copy(src, col, to_slot):
            return pltpu.make_async_copy(src.at[:, pl.ds(pl.multiple_of(col * D, D), D)], w_sc.at[to_slot],
                                         w_sem.at[to_slot])

        @pl.when(jnp.logical_and(g == 0, i == 0))
        def _():
            for k in SHARD_ORDER:
                sends[k - 1].start()
            local.start()
            for cp in u_copies:
                cp.start()
            cp = block_copy(wsh_ref, 0, 0)
            cp.start()
            cp.wait()

        @pl.when(jnp.logical_and(last_tile, g == 0))
        def _():
            block_copy(wsh_ref, 1, 1).start()

        for pos, k in enumerate(SHARD_ORDER, start=1):
            arrived, forward, forwarded = arrivals[k - 1]

            @pl.when(jnp.logical_and(last_tile, g == 2 * pos - 1))
            def _():
                arrived.wait_recv()
                forward.start()
                forwarded.wait_recv()
                block_copy(wfull_ref, route_ref[2 + 2 * pos], 0).start()

            @pl.when(jnp.logical_and(last_tile, g == 2 * pos))
            def _():
                block_copy(wfull_ref, route_ref[3 + 2 * pos], 1).start()

        @pl.when(jnp.logical_and(i == 0, g > 0))
        def _():
            block_copy(wfull_ref, 0, slot).wait()

        for t in range(nt):
            @pl.when(jnp.logical_and(g == 0, i == t))
            def _():
                u_copies[t].wait()

        halves = range(0, D, D // 2)

        @pl.when(gid < 2)
        def _():
            scale = jnp.where(gid == 1, QK_SCALE, 1.0).astype(F32)
            for a in halves:
                acc = _dot(u_sc[i], w_sc[slot, :, a:a + D // 2])
                for h in range(HEADS // 2):
                    blk = acc[:, h * DH:(h + 1) * DH]
                    out = (blk * cos_ref[...] + pltpu.roll(blk, DH // 2, axis=1) * sin_ref[...]) * scale
                    proj_ref[:, a + h * DH:a + (h + 1) * DH] = out.astype(BF16)

        @pl.when(gid >= 2)
        def _():
            for a in halves:
                proj_ref[:, a:a + D // 2] = _dot(u_sc[i], w_sc[slot, :, a:a + D // 2]).astype(BF16)

        @pl.when(jnp.logical_and(g == 7, i == nt - 1))
        def _():
            local.wait()
            for cp in sends:
                cp.wait_send()
            for _, forward, _ in arrivals:
                forward.wait_send()

    return _hosted_call(
        body,
        name="in_proj",
        grid=(8, nt),
        in_specs=[
            _ANY, _ANY,
            pl.BlockSpec((tm, DH), lambda g, i, rt: (i, 0)),
            pl.BlockSpec((tm, DH), lambda g, i, rt: (i, 0)),
        ],
        out_specs=[pl.BlockSpec((tm, D), lambda g, i, rt: (i, _swap_3_4(rt[2 + g]))), _ANY],
        out_shape=[jax.ShapeDtypeStruct((rows, IN_COLS), BF16), jax.ShapeDtypeStruct((D, IN_COLS), BF16)],
        scratch_shapes=[
            pltpu.VMEM((nt, tm, D), BF16), pltpu.VMEM((2, D, D), BF16),
            pltpu.SemaphoreType.DMA((nt,)), pltpu.SemaphoreType.DMA((2,)), pltpu.SemaphoreType.DMA,
            pltpu.SemaphoreType.DMA((3,)), pltpu.SemaphoreType.DMA((3,)),
            pltpu.SemaphoreType.DMA((3,)), pltpu.SemaphoreType.DMA((3,)),
        ],
        args=(u, w_shard, cos_t, sin_t),
        riders=riders,
        prefetch=route,
        riders_after_body=True,
    )


def _retention_fwd(proj_bf, intra, q_dec, k_dec, c_dec, riders=()):
    rows = proj_bf.shape[0]
    nc = rows // CHUNK

    def body(q_ref, k_ref, v_ref, m_ref, qd_ref, kd_ref, cd_ref, o_ref, sprev_ref, s_sc):
        @pl.when(pl.program_id(0) == 0)
        def _():
            s_sc[...] = jnp.zeros_like(s_sc)

        for h in range(HEADS):
            sl = slice(h * DH, (h + 1) * DH)
            q, k, v = q_ref[:, sl], k_ref[:, sl], v_ref[:, sl]
            state = s_sc[h]
            state_b = state.astype(BF16)
            sprev_ref[0, h] = state_b
            s = _dot_nt(q, k) * m_ref[h]
            inner = _dot(s.astype(BF16), v)
            cross = _dot(q, state_b) * qd_ref[:, sl]
            o_ref[:, sl] = (inner + cross).astype(BF16)
            k_scaled = (k.astype(F32) * kd_ref[:, sl]).astype(BF16)
            s_sc[h] = state * cd_ref[:, sl] + _dot_tn(k_scaled, v)

    chunk_spec = lambda col: pl.BlockSpec((CHUNK, D), lambda c: (c, col))
    const2 = lambda shape: pl.BlockSpec(shape, lambda c: (0, 0))
    return _hosted_call(
        body,
        name="retention_fwd",
        grid=(nc,),
        in_specs=[
            chunk_spec(0), chunk_spec(1), chunk_spec(2),
            pl.BlockSpec((HEADS, CHUNK, CHUNK), lambda c: (0, 0, 0)),
            const2((CHUNK, D)), const2((CHUNK, D)), const2((1, D)),
        ],
        out_specs=[
            pl.BlockSpec((CHUNK, D), lambda c: (c, 0)),
            pl.BlockSpec((1, HEADS, DH, DH), lambda c: (c, 0, 0, 0)),
        ],
        out_shape=[
            jax.ShapeDtypeStruct((rows, D), BF16),
            jax.ShapeDtypeStruct((nc, HEADS, DH, DH), BF16),
        ],
        scratch_shapes=[pltpu.VMEM((HEADS, DH, DH), F32)],
        args=(proj_bf, proj_bf, proj_bf, intra, q_dec, k_dec, c_dec),
        riders=riders,
    )


def _shift_down(x, first8_prev, d):
    rolled = pltpu.roll(x, d, axis=0)
    head = pltpu.roll(jnp.concatenate([first8_prev, x[0:8]], axis=0), d, axis=0)[8:16]
    return rolled, head


def _conv_and_gates(x, prev8, cw_ref, cb_ref, wa_ref, wx_ref, ba_ref, bx_ref, lam_ref, c_sc):
    cw = cw_ref[...]
    conv = cb_ref[...] + cw[3:4] * x
    head = cb_ref[...] + cw[3:4] * x[0:8]
    for d in (1, 2, 3):
        rolled, hd = _shift_down(x, prev8, d)
        conv = conv + cw[3 - d:4 - d] * rolled
        head = head + cw[3 - d:4 - d] * hd
    c_sc[...] = conv
    c_sc[0:8, :] = head
    c = c_sc[...]
    zr, zi = [], []
    for g in range(LRU_BLOCKS):
        sl = slice(g * LRU_BLOCK, (g + 1) * LRU_BLOCK)
        cg = c[:, sl].astype(BF16)
        zr.append(_dot(cg, wa_ref[g]))
        zi.append(_dot(cg, wx_ref[g]))
    r = _sigmoid(jnp.concatenate(zr, axis=1) + ba_ref[...])
    gate_i = _sigmoid(jnp.concatenate(zi, axis=1) + bx_ref[...])
    sp = _softplus(-lam_ref[...])
    log_a = (-LRU_C) * r * sp
    a = jnp.exp(log_a)
    mult = jnp.sqrt(_one_minus_square(a, log_a))
    return c, r, gate_i, a, mult, log_a


def _lru_fwd(proj, conv_w, conv_b, wa, wx, ba, bx, lam, riders=()):
    rows = proj.shape[0]
    TM = _lru_tile(rows)
    nt = rows // TM

    def body(x_ref, cw_ref, cb_ref, wa_ref, wx_ref, ba_ref, bx_ref, lam_ref,
             h_ref, c_ref, r_ref, i_ref, la_ref, mult_ref, prev_sc, carry_sc, c_sc, a_sc, u_sc, h_sc):
        i = pl.program_id(0)

        @pl.when(i == 0)
        def _():
            prev_sc[...] = jnp.zeros_like(prev_sc)
            carry_sc[...] = jnp.zeros_like(carry_sc)

        x = x_ref[...].astype(F32)
        c, r, gate_i, a, mult, log_a = _conv_and_gates(x, prev_sc[...], cw_ref, cb_ref, wa_ref, wx_ref, ba_ref, bx_ref,
                                                       lam_ref, c_sc)
        for ref, val in ((c_ref, c), (r_ref, r), (i_ref, gate_i), (la_ref, log_a), (mult_ref, mult)):
            ref[...] = val.astype(BF16)
        prev_sc[...] = x[TM - 8:TM]
        valid = _row_ids(i, TM, (TM, D)) >= PAD_ROWS
        a_sc[...] = a
        u_sc[...] = jnp.where(valid, mult * gate_i * c, 0.0)
        row8 = lax.broadcasted_iota(jnp.int32, (8, D), 0)

        def group(gi, hprev):
            r0 = pl.multiple_of(gi * 8, 8)
            aa = a_sc[pl.ds(r0, 8), :]
            uu = u_sc[pl.ds(r0, 8), :]
            for d in (1, 2, 4):
                a_sh = jnp.where(row8 >= d, pltpu.roll(aa, d, axis=0), 1.0)
                u_sh = jnp.where(row8 >= d, pltpu.roll(uu, d, axis=0), 0.0)
                uu = uu + aa * u_sh
                aa = aa * a_sh
            hb = aa * hprev + uu
            h_sc[pl.ds(r0, 8), :] = hb
            return hb[7:8, :]

        hlast = lax.fori_loop(0, TM // 8, group, carry_sc[0:1, :])
        carry_sc[0:1, :] = hlast
        h_ref[...] = h_sc[...].astype(BF16)

    vec = pl.BlockSpec((1, D), lambda i: (0, 0))
    wspec = pl.BlockSpec((LRU_BLOCKS, LRU_BLOCK, LRU_BLOCK), lambda i: (0, 0, 0))
    return _hosted_call(
        body,
        name="lru_fwd",
        grid=(nt,),
        in_specs=[
            pl.BlockSpec((TM, D), lambda i: (i, LRU_IN_COL)),
            pl.BlockSpec((4, D), lambda i: (0, 0)),
            vec, wspec, wspec, vec, vec, vec,
        ],
        out_specs=[pl.BlockSpec((TM, D), lambda i: (i, 0)) for _ in range(6)],
        out_shape=[jax.ShapeDtypeStruct((rows, D), BF16) for _ in range(6)],
        scratch_shapes=[
            pltpu.VMEM((8, D), F32), pltpu.VMEM((8, D), F32),
            pltpu.VMEM((TM, D), F32), pltpu.VMEM((TM, D), F32), pltpu.VMEM((TM, D), F32), pltpu.VMEM((TM, D), F32),
        ],
        args=(proj, conv_w, conv_b, wa, wx, ba, bx, lam),
        riders=riders,
    )


def _group_norm(o):
    outs, rstds = [], []
    for h in range(HEADS):
        oh = o[:, h * DH:(h + 1) * DH]
        rs = lax.rsqrt(jnp.mean(oh * oh, axis=-1, keepdims=True) + EPS)
        outs.append(oh * rs)
        rstds.append(rs)
    return jnp.concatenate(outs, axis=1), rstds


def _frame_rows(head_hbm, x_hbm, buf, sems, i, tm, nt):
    slot = i % 2

    @pl.when(i == 0)
    def _():
        first = [pltpu.make_async_copy(head_hbm, buf.at[0, pl.ds(0, FRONT)], sems.at[2]),
                 pltpu.make_async_copy(x_hbm.at[pl.ds(0, tm - FRONT)], buf.at[0, pl.ds(FRONT, tm - FRONT)], sems.at[0])]
        for cp in first:
            cp.start()
        for cp in first:
            cp.wait()

    @pl.when(i + 1 < nt)
    def _():
        start = pl.multiple_of((i + 1) * tm - FRONT, LANES)
        pltpu.make_async_copy(x_hbm.at[pl.ds(start, tm)], buf.at[1 - slot], sems.at[1 - slot]).start()

    @pl.when(i > 0)
    def _():
        pltpu.make_async_copy(x_hbm.at[pl.ds(0, tm)], buf.at[slot], sems.at[slot]).wait()

    return slot


def _mix_fwd(head, x2d, o, proj, h_lru, w_br, w_bl, w_o, ffn_norm_w, riders=()):
    rows = o.shape[0]
    tm = TM_MIX_FWD
    assert rows % tm == 0 and tm > FRONT
    nt = rows // tm

    def body(head_hbm, x_hbm, o_ref, gates_ref, hl_ref, wbr_ref, wbl_ref, wo_ref, nw_ref,
             yret_ref, ylru_ref, h1_ref, u2_ref, rstd_ref, h0_sc, h0_sem):
        i = pl.program_id(0)
        slot = _frame_rows(head_hbm, x_hbm, h0_sc, h0_sem, i, tm, nt)
        gate = lambda j: gates_ref[:, j * D:(j + 1) * D].astype(F32)
        on, _ = _group_norm(o_ref[...].astype(F32))
        gret = gate(0)
        a_ret = (gret * _sigmoid(gret) * on).astype(BF16)
        y_ret = _dot(a_ret, wbr_ref[...])
        gl, _ = _gelu_and_grad(gate(1))
        a_lru = (gl * hl_ref[...].astype(F32)).astype(BF16)
        y_lru = _dot(a_lru, wbl_ref[...])
        mixed = (_sigmoid(gate(2)) * y_ret + _sigmoid(gate(3)) * y_lru).astype(BF16)
        delta = _dot(mixed, wo_ref[...])
        yret_ref[...] = y_ret.astype(BF16)
        ylru_ref[...] = y_lru.astype(BF16)
        h1 = h0_sc[slot] + delta
        rs = lax.rsqrt(jnp.mean(h1 * h1, axis=-1, keepdims=True) + EPS)
        h1_ref[...] = h1
        u2_ref[...] = ((h1 * rs) * nw_ref[...]).astype(BF16)
        rstd_ref[...] = rs

    tile = lambda col: pl.BlockSpec((tm, D), lambda i: (i, col))
    wspec = pl.BlockSpec((D, D), lambda i: (0, 0))
    return _hosted_call(
        body,
        name="mix_fwd",
        grid=(nt,),
        in_specs=[
            _ANY, _ANY,
            tile(0), pl.BlockSpec((tm, 4 * D), lambda i: (i, GATES_COL)), tile(0),
            wspec, wspec, wspec,
            pl.BlockSpec((1, D), lambda i: (0, 0)),
        ],
        out_specs=[tile(0), tile(0), tile(0), tile(0), pl.BlockSpec((tm, 1), lambda i: (i, 0))],
        out_shape=[
            jax.ShapeDtypeStruct((rows, D), BF16), jax.ShapeDtypeStruct((rows, D), BF16),
            jax.ShapeDtypeStruct((rows, D), F32), jax.ShapeDtypeStruct((rows, D), BF16),
            jax.ShapeDtypeStruct((rows, 1), F32),
        ],
        scratch_shapes=[pltpu.VMEM((2, tm, D), F32), pltpu.SemaphoreType.DMA((3,))],
        args=(head, x2d, o, proj, h_lru, w_br, w_bl, w_o, ffn_norm_w),
        riders=riders,
    )


def _ffn_fwd_gate(u2, w_ffn_in, riders=()):
    rows = u2.shape[0]
    tm = _heavy_tile(rows)
    hid = lambda: pl.BlockSpec((tm, FFN), lambda i: (i, 0))

    def body(u2_ref, w_hbm, silu_ref, dsilu_ref, act_ref, w_sc, w_sem):
        @pl.when(pl.program_id(0) == 0)
        def _():
            cp = pltpu.make_async_copy(w_hbm, w_sc, w_sem)
            cp.start()
            cp.wait()

        u2 = u2_ref[...]
        for a, b in _FFN_SUBS:
            g = _dot(u2, w_sc[:, a:b])
            up = _dot(u2, w_sc[:, FFN + a:FFN + b])
            sg = _sigmoid(g)
            silu = g * sg
            silu_ref[:, a:b] = silu.astype(BF16)
            dsilu_ref[:, a:b] = (up * (sg * (1.0 + g * (1.0 - sg)))).astype(BF16)
            act_ref[:, a:b] = (silu * up).astype(BF16)

    return _hosted_call(
        body,
        name="ffn_fwd_gate",
        grid=(rows // tm,),
        in_specs=[pl.BlockSpec((tm, D), lambda i: (i, 0)), _ANY],
        out_specs=[hid(), hid(), hid()],
        out_shape=[jax.ShapeDtypeStruct((rows, FFN), BF16)] * 3,
        scratch_shapes=[pltpu.VMEM((D, 2 * FFN), BF16), pltpu.SemaphoreType.DMA],
        args=(u2, w_ffn_in),
        riders=riders,
    )


def _ffn_out_loss(act, h1, w_ffn_out, target, final_norm_w):
    rows = act.shape[0]
    tm = _heavy_tile(rows)
    nt = rows // tm

    def body(act_ref, h1_ref, wo_ref, fnw_ref, tgt_hbm, dh2_ref, stats_ref, tgt_sc, tgt_sem):
        i = pl.program_id(0)
        slot = i % 2

        @pl.when(i == 0)
        def _():
            stats_ref[...] = jnp.zeros_like(stats_ref)
            tgt_sc[0, 0:FRONT, :] = jnp.zeros((FRONT, D), F32)
            cp = pltpu.make_async_copy(tgt_hbm.at[pl.ds(0, tm - FRONT)], tgt_sc.at[0, pl.ds(FRONT, tm - FRONT)], tgt_sem.at[0])
            cp.start()
            cp.wait()

        @pl.when(i + 1 < nt)
        def _():
            start = pl.multiple_of((i + 1) * tm - FRONT, FRONT)
            pltpu.make_async_copy(tgt_hbm.at[pl.ds(start, tm)], tgt_sc.at[1 - slot], tgt_sem.at[1 - slot]).start()

        @pl.when(i > 0)
        def _():
            pltpu.make_async_copy(tgt_hbm.at[pl.ds(0, tm)], tgt_sc.at[slot], tgt_sem.at[slot]).wait()

        fnw = fnw_ref[...]
        half = tm // 2
        for r in (0, half):
            sl = pl.ds(r, half)
            h2 = h1_ref[sl, :] + _dot(act_ref[sl, :], wo_ref[...])
            rs = lax.rsqrt(jnp.mean(h2 * h2, axis=-1, keepdims=True) + EPS)
            n = h2 * rs
            valid = i * tm + r + lax.broadcasted_iota(jnp.int32, (half, D), 0) >= FRONT
            diff = jnp.where(valid, n * fnw - tgt_sc[slot, sl, :], 0.0)
            dy = diff * (1.0 / D)
            stats_ref[0:1, :] += (0.5 / D) * jnp.sum(diff * diff, axis=0, keepdims=True)
            stats_ref[1:2, :] += jnp.sum(dy * n, axis=0, keepdims=True)
            dn = dy * fnw
            dh2_ref[sl, :] = rs * (dn - n * jnp.mean(dn * n, axis=-1, keepdims=True))

    return pl.pallas_call(
        body,
        name="ffn_out_loss",
        grid=(nt,),
        in_specs=[
            pl.BlockSpec((tm, FFN), lambda i: (i, 0)),
            pl.BlockSpec((tm, D), lambda i: (i, 0)),
            pl.BlockSpec((FFN, D), lambda i: (0, 0)),
            pl.BlockSpec((1, D), lambda i: (0, 0)),
            _ANY,
        ],
        out_specs=[pl.BlockSpec((tm, D), lambda i: (i, 0)), pl.BlockSpec((8, D), lambda i: (0, 0))],
        out_shape=[jax.ShapeDtypeStruct((rows, D), F32), jax.ShapeDtypeStruct((8, D), F32)],
        scratch_shapes=[pltpu.VMEM((2, tm, D), F32), pltpu.SemaphoreType.DMA((2,))],
        compiler_params=_params(("arbitrary",)),
    )(act, h1, w_ffn_out, final_norm_w, target)


def _ffn_bwd(dh2, silu, dsilu, h1, rstd2, w_ffn_in, w_ffn_out, ffn_norm_w):
    rows = dh2.shape[0]
    tm = _heavy_tile(rows)
    blk = lambda: pl.BlockSpec((tm, FFN), lambda i: (i, 0))

    def gate_body(dh2_ref, silu_ref, dsilu_ref, wo_hbm, dg_ref, dup_ref, dh2b_ref, wo_sc, wo_sem):
        @pl.when(pl.program_id(0) == 0)
        def _():
            cp = pltpu.make_async_copy(wo_hbm, wo_sc, wo_sem)
            cp.start()
            cp.wait()

        dh2b = dh2_ref[...].astype(BF16)
        dh2b_ref[...] = dh2b
        for a, b in _FFN_SUBS:
            dact = _dot_nt(dh2b, wo_sc[a:b, :])
            dup_ref[:, a:b] = (dact * silu_ref[:, a:b].astype(F32)).astype(BF16)
            dg_ref[:, a:b] = (dact * dsilu_ref[:, a:b].astype(F32)).astype(BF16)

    dg, dup, dh2b = pl.pallas_call(
        gate_body,
        name="ffn_bwd_gate",
        grid=(rows // tm,),
        in_specs=[pl.BlockSpec((tm, D), lambda i: (i, 0)), blk(), blk(), _ANY],
        out_specs=[blk(), blk(), pl.BlockSpec((tm, D), lambda i: (i, 0))],
        out_shape=[jax.ShapeDtypeStruct((rows, FFN), BF16)] * 2 + [jax.ShapeDtypeStruct((rows, D), BF16)],
        scratch_shapes=[pltpu.VMEM((FFN, D), BF16), pltpu.SemaphoreType.DMA],
        compiler_params=_params(("arbitrary",)),
    )(dh2, silu, dsilu, w_ffn_out)

    def body(dg_ref, dup_ref, dh2_ref, h1_ref, rstd_ref, w_hbm, nw_ref, dh1_ref, stats_ref, w_sc, w_sem):
        @pl.when(pl.program_id(0) == 0)
        def _():
            stats_ref[...] = jnp.zeros_like(stats_ref)
            cp = pltpu.make_async_copy(w_hbm, w_sc, w_sem)
            cp.start()
            cp.wait()

        half = tm // 2
        for r in (0, half):
            sl = pl.ds(r, half)
            du = _dot_nt(dg_ref[sl, :], w_sc[:, 0:FFN]) + _dot_nt(dup_ref[sl, :], w_sc[:, FFN:2 * FFN])
            rs = rstd_ref[sl, :]
            n = h1_ref[sl, :] * rs
            stats_ref[0:1, :] += jnp.sum(du * n, axis=0, keepdims=True)
            dn = du * nw_ref[...]
            dh1_ref[sl, :] = dh2_ref[sl, :] + rs * (dn - n * jnp.mean(dn * n, axis=-1, keepdims=True))

    row = lambda width: pl.BlockSpec((tm, width), lambda i: (i, 0))
    dh1, stats = pl.pallas_call(
        body,
        name="ffn_bwd_in",
        grid=(rows // tm,),
        in_specs=[row(FFN), row(FFN), row(D), row(D), row(1), _ANY, pl.BlockSpec((1, D), lambda i: (0, 0))],
        out_specs=[row(D), pl.BlockSpec((8, D), lambda i: (0, 0))],
        out_shape=[jax.ShapeDtypeStruct((rows, D), F32), jax.ShapeDtypeStruct((8, D), F32)],
        scratch_shapes=[pltpu.VMEM((D, 2 * FFN), BF16), pltpu.SemaphoreType.DMA],
        compiler_params=_params(("arbitrary",)),
    )(dg, dup, dh2, h1, rstd2, w_ffn_in, ffn_norm_w)
    return dg, dup, dh2b, dh1, stats


def _mix_bwd(dh1, o, proj, h_lru, y_ret, y_lru, w_br, w_bl, w_o, riders=()):
    rows = dh1.shape[0]
    tm = TM_MIX_BWD
    nt = rows // tm

    def body(dh1_ref, o_ref, gates_ref, hl_ref, yret_ref, ylru_ref, wbr_hbm, wbl_hbm, wo_hbm,
             dproj_ref, do_ref, dhl_ref, mixed_ref, aret_ref, alru_ref, dyret_ref, dylru_ref, w_sc, w_sem):
        @pl.when(pl.program_id(0) == 0)
        def _():
            copies = [pltpu.make_async_copy(src, w_sc.at[q], w_sem.at[q]) for q, src in enumerate((wbr_hbm, wbl_hbm, wo_hbm))]
            for cp in copies:
                cp.start()
            for cp in copies:
                cp.wait()

        wbr_ref, wbl_ref, wo_ref = w_sc.at[0], w_sc.at[1], w_sc.at[2]
        gate = lambda j: gates_ref[:, j * D:(j + 1) * D].astype(F32)
        dmixed = _dot_nt(dh1_ref[...].astype(BF16), wo_ref[...])
        y_ret, y_lru = yret_ref[...].astype(F32), ylru_ref[...].astype(F32)
        sa, sb = _sigmoid(gate(2)), _sigmoid(gate(3))
        mixed_ref[...] = (sa * y_ret + sb * y_lru).astype(BF16)
        dga = dmixed * y_ret * sa * (1.0 - sa)
        dgb = dmixed * y_lru * sb * (1.0 - sb)
        dy_ret = (dmixed * sa).astype(BF16)
        dy_lru = (dmixed * sb).astype(BF16)
        dyret_ref[...] = dy_ret
        dylru_ref[...] = dy_lru
        da_ret = _dot_nt(dy_ret, wbr_ref[...])
        da_lru = _dot_nt(dy_lru, wbl_ref[...])

        gret = gate(0)
        sg = _sigmoid(gret)
        silu = gret * sg
        on, rstds = _group_norm(o_ref[...].astype(F32))
        aret_ref[...] = (silu * on).astype(BF16)
        dgret = da_ret * on * (sg * (1.0 + gret * (1.0 - sg)))
        don = da_ret * silu
        for h in range(HEADS):
            sl = slice(h * DH, (h + 1) * DH)
            onh, donh = on[:, sl], don[:, sl]
            do_ref[:, sl] = (rstds[h] * (donh - onh * jnp.mean(donh * onh, axis=-1, keepdims=True))).astype(BF16)

        gl, gl_grad = _gelu_and_grad(gate(1))
        hl = hl_ref[...].astype(F32)
        alru_ref[...] = (gl * hl).astype(BF16)
        dlgate = da_lru * hl * gl_grad
        dhl_ref[...] = (da_lru * gl).astype(BF16)

        for j, val in enumerate((dgret, dlgate, dga, dgb)):
            dproj_ref[:, j * D:(j + 1) * D] = val.astype(BF16)

    tile = lambda col: pl.BlockSpec((tm, D), lambda i: (i, col))
    gates = pl.BlockSpec((tm, 4 * D), lambda i: (i, GATES_COL))
    bf = lambda: jax.ShapeDtypeStruct((rows, D), BF16)
    return _hosted_call(
        body,
        name="mix_bwd",
        grid=(nt,),
        in_specs=[tile(0), tile(0), gates, tile(0), tile(0), tile(0), _ANY, _ANY, _ANY],
        out_specs=[pl.BlockSpec((tm, 4 * D), lambda i: (i, GATES_COL))] + [tile(0)] * 7,
        out_shape=[jax.ShapeDtypeStruct((rows, IN_COLS), BF16)] + [bf() for _ in range(7)],
        scratch_shapes=[pltpu.VMEM((3, D, D), BF16), pltpu.SemaphoreType.DMA((3,))],
        args=(dh1, o, proj, h_lru, y_ret, y_lru, w_br, w_bl, w_o),
        riders=riders,
    )


def _retention_bwd(dproj, proj_bf, do, sprev, intra, intra_t, q_dec, k_dec, c_dec, cos_t, sin_t, riders=()):
    rows = proj_bf.shape[0]
    nc = rows // CHUNK

    def body(dproj_in_ref, q_ref, k_ref, v_ref, do_ref, sprev_ref, m_ref, mt_ref, qd_ref, kd_ref, cd_ref, cos_ref, sin_ref,
             dproj_ref, ds_sc):
        @pl.when(pl.program_id(0) == 0)
        def _():
            ds_sc[...] = jnp.zeros_like(ds_sc)

        cos, sin = cos_ref[...], sin_ref[...]

        def unrotate(dy):
            return dy * cos - pltpu.roll(dy, DH // 2, axis=1) * sin

        for h in range(HEADS):
            sl = slice(h * DH, (h + 1) * DH)
            q, k, v = q_ref[:, sl], k_ref[:, sl], v_ref[:, sl]
            do = do_ref[:, sl]
            dob = do.astype(BF16)
            doq = (do * qd_ref[:, sl]).astype(BF16)
            state_prev = sprev_ref[0, h]
            dstate = ds_sc[h]
            dstate_b = dstate.astype(BF16)
            s_t = (_dot_nt(k, q) * mt_ref[h]).astype(BF16)
            ds_t = (_dot_nt(v, dob) * mt_ref[h]).astype(BF16)
            ds = (_dot_nt(dob, v) * m_ref[h]).astype(BF16)
            kd = kd_ref[:, sl]
            dq = _dot(ds, k) + _dot_nt(doq, state_prev)
            dk = _dot(ds_t, q) + _dot_nt(v, dstate_b) * kd
            k_scaled = (k.astype(F32) * kd).astype(BF16)
            dv = _dot(s_t, dob) + _dot(k_scaled, dstate_b)
            ds_sc[h] = dstate * cd_ref[:, sl] + _dot_tn(q, doq)
            for part, val in enumerate((unrotate(dq), unrotate(dk) * QK_SCALE, dv)):
                dproj_ref[:, part * D + h * DH:part * D + (h + 1) * DH] = val.astype(BF16)

    rev = lambda c: nc - 1 - c
    chunk_spec = lambda col: pl.BlockSpec((CHUNK, D), lambda c: (rev(c), col))
    const2 = lambda shape: pl.BlockSpec(shape, lambda c: (0, 0))
    const3 = pl.BlockSpec((HEADS, CHUNK, CHUNK), lambda c: (0, 0, 0))
    return _hosted_call(
        body,
        name="retention_bwd",
        grid=(nc,),
        in_specs=[
            pl.BlockSpec(memory_space=pl.ANY),
            chunk_spec(0), chunk_spec(1), chunk_spec(2), chunk_spec(0),
            pl.BlockSpec((1, HEADS, DH, DH), lambda c: (rev(c), 0, 0, 0)),
            const3, const3,
            const2((CHUNK, D)), const2((CHUNK, D)), const2((1, D)),
            pl.BlockSpec((CHUNK, DH), lambda c: (rev(c), 0)),
            pl.BlockSpec((CHUNK, DH), lambda c: (rev(c), 0)),
        ],
        out_specs=[pl.BlockSpec((CHUNK, 3 * D), lambda c: (rev(c), 0))],
        out_shape=[jax.ShapeDtypeStruct(dproj.shape, BF16)],
        aliases={0: 0},
        scratch_shapes=[pltpu.VMEM((HEADS, DH, DH), F32)],
        args=(dproj, proj_bf, proj_bf, proj_bf, do, sprev, intra, intra_t, q_dec, k_dec, c_dec, cos_t, sin_t),
        riders=riders,
    )


def _lru_bwd(dproj, proj, saved, dhl, conv_w, wa, wx, lam, riders=()):
    rows = proj.shape[0]
    TM = _lru_tile(rows)
    nt = rows // TM
    per8 = TM // 8

    def body(dproj_in_ref, x_ref, xprev_ref, h_ref, hprev_ref, c_ref, r_ref, i_ref, la_ref, mult_ref, dhl_ref,
             cw_ref, wa_ref, wx_ref, lam_ref,
             dproj_ref, dwa_ref, dwx_ref, stats_ref, anext_sc, dhnext_sc, dcnext_sc, c_sc, b_sc, dh_sc):
        step = pl.program_id(0)
        i = nt - 1 - step

        @pl.when(step == 0)
        def _():
            anext_sc[...] = jnp.zeros_like(anext_sc)
            dhnext_sc[...] = jnp.zeros_like(dhnext_sc)
            dcnext_sc[...] = jnp.zeros_like(dcnext_sc)
            dwa_ref[...] = jnp.zeros_like(dwa_ref)
            dwx_ref[...] = jnp.zeros_like(dwx_ref)
            stats_ref[...] = jnp.zeros_like(stats_ref)

        first = i == 0
        x = x_ref[...].astype(F32)
        prev8 = jnp.where(first, 0.0, xprev_ref[8:16, :].astype(F32))
        c_b = c_ref[...]
        c, r, gate_i, mult = (ref[...].astype(F32) for ref in (c_ref, r_ref, i_ref, mult_ref))
        a = jnp.exp(la_ref[...].astype(F32))
        sp = _softplus(-lam_ref[...])
        dh_sc[...] = dhl_ref[...].astype(F32)

        b_sc[...] = pltpu.roll(a, TM - 1, axis=0)
        b_sc[TM - 1:TM, :] = anext_sc[0:1, :]
        anext_sc[0:1, :] = a[0:1, :]
        row8 = lax.broadcasted_iota(jnp.int32, (8, D), 0)

        def group(gi, dhnext):
            r0 = pl.multiple_of((per8 - 1 - gi) * 8, 8)
            bb = b_sc[pl.ds(r0, 8), :]
            uu = dh_sc[pl.ds(r0, 8), :]
            for d in (1, 2, 4):
                b_sh = jnp.where(row8 < 8 - d, pltpu.roll(bb, 8 - d, axis=0), 1.0)
                u_sh = jnp.where(row8 < 8 - d, pltpu.roll(uu, 8 - d, axis=0), 0.0)
                uu = uu + bb * u_sh
                bb = bb * b_sh
            dhb = bb * dhnext + uu
            dh_sc[pl.ds(r0, 8), :] = dhb
            return dhb[0:1, :]

        dhfirst = lax.fori_loop(0, per8, group, dhnext_sc[0:1, :])
        dhnext_sc[0:1, :] = dhfirst
        dh = dh_sc[...]

        h = h_ref[...].astype(F32)
        hprev8 = jnp.where(first, 0.0, hprev_ref[8:16, :].astype(F32))
        h_dn, h_head = _shift_down(h, hprev8, 1)
        c_sc[...] = h_dn
        c_sc[0:8, :] = h_head
        h_before = c_sc[...]

        valid = _row_ids(i, TM, (TM, D)) >= PAD_ROWS
        da = dh * h_before
        du = jnp.where(valid, dh, 0.0)
        dmult = du * gate_i * c
        dgate_i = du * mult * c
        dc = du * mult * gate_i
        dla = da * a - dmult * (a * a) / mult
        dla = jnp.where(valid, dla, 0.0)
        dr = dla * ((-LRU_C) * sp)
        dzr = dr * r * (1.0 - r)
        dzi = dgate_i * gate_i * (1.0 - gate_i)
        stats_ref[1:2, :] += jnp.sum(dzr, axis=0, keepdims=True)
        stats_ref[2:3, :] += jnp.sum(dzi, axis=0, keepdims=True)
        stats_ref[3:4, :] += jnp.sum(dla * ((-LRU_C) * r), axis=0, keepdims=True)
        dc_gate = []
        for g in range(LRU_BLOCKS):
            sl = slice(g * LRU_BLOCK, (g + 1) * LRU_BLOCK)
            cg = c_b[:, sl]
            dzr_g = dzr[:, sl].astype(BF16)
            dzi_g = dzi[:, sl].astype(BF16)
            dc_gate.append(_dot_nt(dzr_g, wa_ref[g]) + _dot_nt(dzi_g, wx_ref[g]))
            dwa_ref[g] += _dot_tn(cg, dzr_g)
            dwx_ref[g] += _dot_tn(cg, dzi_g)
        dc = dc + jnp.concatenate(dc_gate, axis=1)

        cw = cw_ref[...]
        stats_ref[0:1, :] += jnp.sum(dc, axis=0, keepdims=True)
        stats_ref[7:8, :] += jnp.sum(dc * x, axis=0, keepdims=True)
        dx = cw[3:4] * dc
        tail_src = jnp.concatenate([dc[TM - 8:TM], dcnext_sc[...]], axis=0)
        dx_tail = cw[3:4] * dc[TM - 8:TM]
        for d in (1, 2, 3):
            dx = dx + cw[3 - d:4 - d] * pltpu.roll(dc, TM - d, axis=0)
            dx_tail = dx_tail + cw[3 - d:4 - d] * pltpu.roll(tail_src, 16 - d, axis=0)[0:8]
            rolled, hd = _shift_down(x, prev8, d)
            b_sc[...] = rolled
            b_sc[0:8, :] = hd
            stats_ref[7 - d:8 - d, :] += jnp.sum(dc * b_sc[...], axis=0, keepdims=True)
        dcnext_sc[...] = dc[0:8]
        dproj_ref[...] = dx.astype(BF16)
        dproj_ref[TM - 8:TM, :] = dx_tail.astype(BF16)

    rev = lambda s: nt - 1 - s
    vec = pl.BlockSpec((1, D), lambda s: (0, 0))
    wspec = pl.BlockSpec((LRU_BLOCKS, LRU_BLOCK, LRU_BLOCK), lambda s: (0, 0, 0))
    prev16 = lambda col: pl.BlockSpec((BF16_ROWS, D), lambda s: (jnp.maximum(rev(s) * (TM // BF16_ROWS) - 1, 0), col))
    tile = lambda: pl.BlockSpec((TM, D), lambda s: (rev(s), 0))
    h_lru, c_sv, r_sv, i_sv, la_sv, mult_sv = saved
    return _hosted_call(
        body,
        name="lru_bwd",
        grid=(nt,),
        in_specs=[
            pl.BlockSpec(memory_space=pl.ANY),
            pl.BlockSpec((TM, D), lambda s: (rev(s), LRU_IN_COL)), prev16(LRU_IN_COL),
            tile(), prev16(0),
            tile(), tile(), tile(), tile(), tile(), tile(),
            pl.BlockSpec((4, D), lambda s: (0, 0)),
            wspec, wspec, vec,
        ],
        out_specs=[
            pl.BlockSpec((TM, D), lambda s: (rev(s), LRU_IN_COL)),
            wspec, wspec,
            pl.BlockSpec((8, D), lambda s: (0, 0)),
        ],
        out_shape=[
            jax.ShapeDtypeStruct(dproj.shape, BF16),
            jax.ShapeDtypeStruct((LRU_BLOCKS, LRU_BLOCK, LRU_BLOCK), F32),
            jax.ShapeDtypeStruct((LRU_BLOCKS, LRU_BLOCK, LRU_BLOCK), F32),
            jax.ShapeDtypeStruct((8, D), F32),
        ],
        aliases={0: 0},
        scratch_shapes=[
            pltpu.VMEM((8, D), F32), pltpu.VMEM((8, D), F32), pltpu.VMEM((8, D), F32),
            pltpu.VMEM((TM, D), F32), pltpu.VMEM((TM, D), F32), pltpu.VMEM((TM, D), F32),
        ],
        args=(dproj, proj, proj, h_lru, h_lru, c_sv, r_sv, i_sv, la_sv, mult_sv, dhl, conv_w, wa, wx, lam),
        riders=riders,
    )


def _in_proj_bwd(dproj, w_in, part, prev=None, riders=()):
    rows = dproj.shape[0]
    tm = _heavy_tile(rows)
    nt = rows // tm
    split = max(1, nt // 4 + 1) if nt > 1 else 1
    first = 0 if part == 0 else split
    count = split if part == 0 else nt - split

    def body(*refs):
        dproj_ref, w_hbm, du_ref, w_sc, w_sem = refs[-5:]

        @pl.when(pl.program_id(0) == 0)
        def _():
            moves = [((0, 3), 0), ((4, 1), 3), ((3, 1), 4), ((5, 3), 5)]
            copies = [pltpu.make_async_copy(w_hbm.at[:, pl.ds(src * D, n * D)], w_sc.at[:, pl.ds(dst * D, n * D)], w_sem.at[q])
                      for q, ((src, n), dst) in enumerate(moves)]
            for cp in copies:
                cp.start()
            for cp in copies:
                cp.wait()

        for a in range(0, D, D // 2):
            du_ref[:, a:a + D // 2] = _dot_nt(dproj_ref[...], w_sc[a:a + D // 2, :])

    in_specs = [pl.BlockSpec((tm, IN_COLS), lambda i: (first + i, 0)), _ANY]
    args = (dproj, w_in)
    if prev is not None:
        in_specs = [_ANY] + in_specs
        args = (prev,) + args
    return _hosted_call(
        body,
        name="in_proj_bwd_%d" % part,
        grid=(count,),
        in_specs=in_specs,
        out_specs=[pl.BlockSpec((tm, D), lambda i: (first + i, 0))],
        out_shape=[jax.ShapeDtypeStruct((rows, D), F32)],
        scratch_shapes=[pltpu.VMEM((D, IN_COLS), BF16), pltpu.SemaphoreType.DMA((4,))],
        aliases={0: 0} if prev is not None else None,
        args=args,
        riders=riders,
    )


def _norm1_bwd(du, dh1, head, x2d, rstd1, norm_w, riders=()):
    assert not riders
    seq = x2d.shape[0]
    front_tiles = FRONT // TM

    def body(du_any, dh1_any, head_ref, x_any, rstd_any, nw_ref, gx_any, ghead_ref, stats_ref, du_sc, dh1_sc, rstd_sc):
        stats_ref[...] = jnp.zeros_like(stats_ref)

        def finish(du, dh1_rows, h0, rs):
            n = h0 * rs
            stats_ref[0:1, :] += jnp.sum(du * n, axis=0, keepdims=True)
            dn = du * nw_ref[...]
            return dh1_rows + rs * (dn - n * jnp.mean(dn * n, axis=-1, keepdims=True))

        for src, dst in ((du_any, du_sc), (dh1_any, dh1_sc), (rstd_any, rstd_sc)):
            pltpu.sync_copy(src.at[pl.ds(0, FRONT)], dst)
        ghead_ref[...] = finish(du_sc[...], dh1_sc[...], head_ref[...], rstd_sc[...])

        def tile(du_ref, dh1_ref, x_ref, rstd_ref, gx_ref):
            gx_ref[...] = finish(du_ref[...], dh1_ref[...], x_ref[...], rstd_ref[...])

        stream = functools.partial(pl.BlockSpec, (TM, D), pipeline_mode=pl.Buffered(3))
        pltpu.emit_pipeline(
            tile,
            grid=(seq // TM,),
            in_specs=[
                stream(lambda i: (i + front_tiles, 0)), stream(lambda i: (i + front_tiles, 0)), stream(lambda i: (i, 0)),
                pl.BlockSpec((TM, 1), lambda i: (i + front_tiles, 0)),
            ],
            out_specs=[pl.BlockSpec((TM, D), lambda i: (i, 0))],
        )(du_any, dh1_any, x_any, rstd_any, gx_any)

    whole = pl.BlockSpec(memory_space=pltpu.VMEM)
    res = pl.pallas_call(
        body,
        name="norm1_bwd",
        in_specs=[_ANY, _ANY, whole, _ANY, _ANY, whole],
        out_specs=[_ANY, whole, whole],
        out_shape=[
            jax.ShapeDtypeStruct(x2d.shape, F32),
            jax.ShapeDtypeStruct((FRONT, D), F32),
            jax.ShapeDtypeStruct((8, D), F32),
        ],
        scratch_shapes=[pltpu.VMEM((FRONT, D), F32), pltpu.VMEM((FRONT, D), F32), pltpu.VMEM((FRONT, 1), F32)],
        compiler_params=_params(),
    )(du, dh1, head, x2d, rstd1, norm_w)
    return list(res), []


def _matmul_tn(name, x, dy, out_cols, col0=0, prev=None, k_block=None, n_block=None, riders=(), col_map=None):
    col_map = col_map or (lambda n: n)
    rows, kdim = x.shape
    ndim = dy.shape[1]
    kb = k_block or kdim
    nb = n_block or ndim
    step_bytes = lambda t: 2 * t * (kb * x.dtype.itemsize + nb * dy.dtype.itemsize) + 2 * kb * nb * 4
    tr = next(t for t in (2816, 1408, TM_HEAVY, TM) if rows % t == 0 and (t == TM or step_bytes(t) <= TN_VMEM_BUDGET))
    nr, nk, nn = rows // tr, kdim // kb, ndim // nb
    cb0 = col0 // nb

    def body(*refs):
        x_ref, dy_ref, out_ref = refs[-3], refs[-2], refs[-1]
        cut = (nb // 2) // LANES * LANES
        halves = ((0, cut), (cut, nb)) if cut else ((0, nb),)

        @pl.when(pl.program_id(2) == 0)
        def _():
            for a, b in halves:
                out_ref[:, a:b] = _dot_tn(x_ref[...].astype(BF16), dy_ref[:, a:b].astype(BF16))

        @pl.when(pl.program_id(2) > 0)
        def _():
            for a, b in halves:
                out_ref[:, a:b] += _dot_tn(x_ref[...].astype(BF16), dy_ref[:, a:b].astype(BF16))

    in_specs = [
        pl.BlockSpec((tr, kb), lambda n, k, r: (r, k)),
        pl.BlockSpec((tr, nb), lambda n, k, r: (r, n)),
    ]
    args = [x, dy]
    aliases = {}
    if prev is not None:
        in_specs = [pl.BlockSpec(memory_space=pl.ANY)] + in_specs
        args = [prev] + args
        aliases = {0: 0}
    (out,), rider_outs = _hosted_call(
        body,
        name=name,
        grid=(nn, nk, nr),
        in_specs=in_specs,
        out_specs=[pl.BlockSpec((kb, nb), lambda n, k, r: (k, cb0 + col_map(n)))],
        out_shape=[jax.ShapeDtypeStruct((kdim, out_cols), F32)],
        scratch_shapes=[],
        aliases=aliases,
        args=args,
        riders=riders,
    )
    return (out, rider_outs) if riders else out


def _local_step(x2d, target, w, plan):
    rows = FRONT + x2d.shape[0]
    head = jnp.concatenate([jnp.zeros((PAD_ROWS, D), F32), w["meta_tokens"]], axis=0)
    cos_t, sin_t = _rope_tables(rows)
    intra, intra_t, q_dec, k_dec, c_dec = _decay_tables()
    grads = {}

    def hosted(host, fn, *args, **kwargs):
        outs, rider_outs = fn(*args, riders=plan.riders(host, w, grads), **kwargs)
        plan.after(host, rider_outs, w, grads)
        return outs

    (u1, rstd1), _ = _norm1(head, x2d, w["mix_norm_w"])
    proj, w["w_in"] = hosted("in_proj", _in_proj, u1, w["w_in_shard"], w["route"], cos_t, sin_t)
    o, sprev = hosted("retention_fwd", _retention_fwd, proj, intra, q_dec, k_dec, c_dec)
    lru_args = (w["conv_w"], w["conv_b"], w["lru_wa"], w["lru_wx"], w["lru_ba"], w["lru_bx"], w["lru_lambda"])
    lru_saved = hosted("lru_fwd", _lru_fwd, proj, *lru_args)
    h_lru = lru_saved[0]
    y_ret, y_lru, h1, u2, rstd2 = hosted("mix_fwd", _mix_fwd, head, x2d, o, proj, h_lru, w["w_branch_ret"],
                                         w["w_branch_lru"], w["w_out"], w["ffn_norm_w"])
    silu, dsilu, act = hosted("ffn_fwd_gate", _ffn_fwd_gate, u2, w["w_ffn_in"])
    dh2, stats_loss = _ffn_out_loss(act, h1, w["w_ffn_out"], target, w["final_norm_w"])

    dg, dup, dh2b, dh1, stats_ffn = _ffn_bwd(dh2, silu, dsilu, h1, rstd2, w["w_ffn_in"], w["w_ffn_out"], w["ffn_norm_w"])
    grads["w_ffn_in"] = _matmul_tn("dw_ffn_up", u2, dup, 2 * FFN, col0=FFN, n_block=FFN_HALF,
                                   prev=_matmul_tn("dw_ffn_gate", u2, dg, 2 * FFN, n_block=FFN_HALF))
    grads["w_ffn_out"] = _matmul_tn("dw_ffn_out", act, dh2b, D, k_block=FFN_HALF)
    (dproj, do, dhl, mixed, a_ret, a_lru, dy_ret, dy_lru) = hosted(
        "mix_bwd", _mix_bwd, dh1, o, proj, h_lru, y_ret, y_lru, w["w_branch_ret"], w["w_branch_lru"], w["w_out"])
    grads["w_out"] = _matmul_tn("dw_out", mixed, dh1, D)
    grads["w_branch_ret"] = _matmul_tn("dw_branch_ret", a_ret, dy_ret, D)
    grads["w_branch_lru"] = _matmul_tn("dw_branch_lru", a_lru, dy_lru, D)
    (dproj,) = hosted("retention_bwd", _retention_bwd, dproj, proj, do, sprev, intra, intra_t, q_dec, k_dec, c_dec,
                      cos_t, sin_t)
    dproj, grads["lru_wa"], grads["lru_wx"], stats_lru = hosted(
        "lru_bwd", _lru_bwd, dproj, proj, lru_saved, dhl, w["conv_w"], w["lru_wa"], w["lru_wx"], w["lru_lambda"])
    grads["w_in"], rider_outs = _matmul_tn("dw_in", u1, dproj, IN_COLS, n_block=D, col_map=_swap_3_4,
                                           riders=plan.riders("dw_in", w, grads))
    plan.after("dw_in", rider_outs, w, grads)
    (du1,) = hosted("in_proj_bwd_0", _in_proj_bwd, dproj, w["w_in"], 0)
    (du1,) = hosted("in_proj_bwd_1", _in_proj_bwd, dproj, w["w_in"], 1, du1)
    (grad_x, grad_head, stats_in), _ = _norm1_bwd(du1, dh1, head, x2d, rstd1, w["mix_norm_w"])
    return grad_x, grad_head, grads, [stats_loss, stats_ffn, stats_in, stats_lru]


BIG_PIECES = {
    "w_in": ("col", (D, 2 * D)),
    "w_ffn_in": ("col", (D, FFN_HALF)),
    "w_ffn_out": ("row", (FFN // 4, D)),
    "w_branch_ret": ("row", (D // 4, D)),
    "w_branch_lru": ("row", (D // 4, D)),
    "w_out": ("row", (D // 4, D)),
    "lru_wa": ("lru", (LRU_BLOCKS, LRU_BLOCK // 4, LRU_BLOCK)),
    "lru_wx": ("lru", (LRU_BLOCKS, LRU_BLOCK // 4, LRU_BLOCK)),
}
SMALL_PIECES = {"meta_tokens": ("col", (N_META, D // 4)), "conv_w": ("col", (4, D // 4))}


def _full_shape(kind, shard):
    if kind == "col":
        return (shard[0], 4 * shard[1])
    if kind == "row":
        return (4 * shard[0], shard[1])
    return (shard[0], 4 * shard[1], shard[2])


def _half_shape(kind, shard):
    return (shard[0] // 2,) + tuple(shard[1:])


def _aligned(start, multiple):
    return start if isinstance(start, int) else pl.multiple_of(start, multiple)


def _lead(h, size):
    if h is None:
        return pl.ds(0, size)
    return pl.ds(_aligned(h * (size // 2), size // 2), size // 2)


def _full_region(ref, kind, shard, s, h):
    if kind == "col":
        return ref.at[_lead(h, shard[0]), pl.ds(_aligned(s * shard[1], shard[1]), shard[1])]
    if kind == "row":
        size = shard[0] if h is None else shard[0] // 2
        start = s * shard[0] + (0 if h is None else h * (shard[0] // 2))
        return ref.at[pl.ds(_aligned(start, BF16_ROWS), size), :]
    return ref.at[_lead(h, shard[0]), pl.ds(_aligned(s * shard[1], shard[1]), shard[1]), :]


def _shard_region(ref, shard, h):
    return ref.at[_lead(h, shard[0])]


def _place():
    x, y, c = lax.axis_index("x"), lax.axis_index("y"), lax.axis_index("c")
    return x, y, c, 2 * x + y


def _other_chip(s, c, k):
    s2 = jnp.bitwise_xor(s, k)
    return s2, (s2 // 2, s2 % 2, c)


def _remote(src, dst, send_sem, recv_sem, dev):
    return pltpu.make_async_remote_copy(src_ref=src, dst_ref=dst, send_sem=send_sem, recv_sem=recv_sem,
                                        device_id=dev, device_id_type=MESH)


_ANY = pl.BlockSpec(memory_space=pl.ANY)


class _Rider:
    def __init__(self, ins, out_shapes, sem_shapes, build, aliased=False):
        self.ins, self.out_shapes, self.sem_shapes, self.build, self.aliased = ins, out_shapes, sem_shapes, build, aliased


def _hosted_call(body, *, name, grid, in_specs, out_specs, out_shape, scratch_shapes, args, riders=(), aliases=None,
                 prefetch=None, riders_after_body=False):
    n_in, n_out, n_sc = len(in_specs), len(out_shape), len(scratch_shapes)
    r_in = [a for r in riders for a in r.ins]
    r_out = [s for r in riders for s in r.out_shapes]
    r_sem = [s for r in riders for s in r.sem_shapes]
    lead = () if prefetch is None else (prefetch,)
    assert prefetch is None or not (aliases or any(r.aliased for r in riders))

    def full_body(*refs):
        head, refs = refs[:len(lead)], refs[len(lead):]
        ins, rin = refs[:n_in], refs[n_in:n_in + len(r_in)]
        o0 = n_in + len(r_in)
        outs, rout = refs[o0:o0 + n_out], refs[o0 + n_out:o0 + n_out + len(r_out)]
        s0 = o0 + n_out + len(r_out)
        scratch, rsem = refs[s0:s0 + n_sc], refs[s0 + n_sc:]
        starts, waits = [], []
        pi = po = ps = 0
        for r in riders:
            st, wt = r.build(rin[pi:pi + len(r.ins)], rout[po:po + len(r.out_shapes)], rsem[ps:ps + len(r.sem_shapes)])
            starts += st
            waits += wt
            pi, po, ps = pi + len(r.ins), po + len(r.out_shapes), ps + len(r.sem_shapes)
        first = functools.reduce(jnp.logical_and, [pl.program_id(d) == 0 for d in range(len(grid))])
        last = functools.reduce(jnp.logical_and, [pl.program_id(d) == grid[d] - 1 for d in range(len(grid))])

        def start_riders():
            @pl.when(first)
            def _():
                for cp in starts:
                    cp.start()

        if riders and not riders_after_body:
            start_riders()
        body(*head, *ins, *outs, *scratch)
        if riders and riders_after_body:
            start_riders()
        if riders:
            @pl.when(last)
            def _():
                for wait in waits:
                    wait()

    io_aliases = dict(aliases or {})
    pi = po = 0
    for r in riders:
        if r.aliased:
            for q in range(len(r.ins)):
                io_aliases[n_in + pi + q] = n_out + po + q
        pi, po = pi + len(r.ins), po + len(r.out_shapes)
    specs = dict(
        grid=grid,
        in_specs=list(in_specs) + [_ANY] * len(r_in),
        out_specs=list(out_specs) + [_ANY] * len(r_out),
        scratch_shapes=list(scratch_shapes) + r_sem,
    )
    if prefetch is not None:
        specs = dict(grid_spec=pltpu.PrefetchScalarGridSpec(num_scalar_prefetch=1, **specs))
    res = pl.pallas_call(
        full_body,
        name=name,
        out_shape=list(out_shape) + r_out,
        input_output_aliases=io_aliases,
        compiler_params=_params(("arbitrary",) * len(grid)),
        **specs,
    )(*lead, *args, *r_in)
    rider_outs, po = [], n_out
    for r in riders:
        rider_outs.append(list(res[po:po + len(r.out_shapes)]))
        po += len(r.out_shapes)
    return list(res[:n_out]), rider_outs


def _run_riders(name, riders):
    r_in = [a for r in riders for a in r.ins]
    r_out = [s for r in riders for s in r.out_shapes]
    r_sem = [s for r in riders for s in r.sem_shapes]

    def body(*refs):
        rin, rout, rsem = refs[:len(r_in)], refs[len(r_in):len(r_in) + len(r_out)], refs[len(r_in) + len(r_out):]
        pi = po = ps = 0
        all_waits = []
        for r in riders:
            starts, waits = r.build(rin[pi:pi + len(r.ins)], rout[po:po + len(r.out_shapes)], rsem[ps:ps + len(r.sem_shapes)])
            for cp in starts:
                cp.start()
            all_waits += waits
            pi, po, ps = pi + len(r.ins), po + len(r.out_shapes), ps + len(r.sem_shapes)
        for wait in all_waits:
            wait()

    io_aliases = {}
    pi = po = 0
    for r in riders:
        if r.aliased:
            for q in range(len(r.ins)):
                io_aliases[pi + q] = po + q
        pi, po = pi + len(r.ins), po + len(r.out_shapes)
    res = pl.pallas_call(
        body,
        name=name,
        in_specs=[_ANY] * len(r_in),
        out_specs=[_ANY] * len(r_out),
        out_shape=r_out,
        scratch_shapes=r_sem,
        input_output_aliases=io_aliases,
    )(*r_in)
    outs, po = [], 0
    for r in riders:
        outs.append(list(res[po:po + len(r.out_shapes)]))
        po += len(r.out_shapes)
    return outs


def _piece(name):
    if name in BIG_PIECES:
        return (name, *BIG_PIECES[name], True)
    return (name, *SMALL_PIECES[name], False)


def _gather_rider(shards, names):
    pieces = [_piece(n) for n in names]
    n = len(pieces)

    def build(ins, outs, sems):
        local_sem, ici_send, ici_recv = sems
        _, _, c, s = _place()
        starts, waits = [], []
        for p, (_, kind, shard, split) in enumerate(pieces):
            cp = pltpu.make_async_copy(ins[p], _full_region(outs[p], kind, shard, s, None), local_sem.at[p])
            starts.append(cp)
            waits.append(cp.wait)
            h = c if split else None
            for k in (1, 2, 3):
                s2, dev = _other_chip(s, c, k)
                cp = _remote(_shard_region(ins[p], shard, h), _full_region(outs[p], kind, shard, s, h),
                             ici_send.at[p, k - 1], ici_recv.at[p, k - 1], dev)
                starts.append(cp)
                waits.append(cp.wait_send)
                region = _full_region(outs[p], kind, shard, s2, h)
                waits.append(_remote(region, region, ici_send.at[p, k - 1], ici_recv.at[p, k - 1], dev).wait_recv)
        return starts, waits

    return _Rider(
        [shards[name] for name in names],
        [jax.ShapeDtypeStruct(_full_shape(kind, shard), shards[name].dtype) for name, kind, shard, _ in pieces],
        [pltpu.SemaphoreType.DMA((n,)), pltpu.SemaphoreType.DMA((n, 3)), pltpu.SemaphoreType.DMA((n, 3))],
        build)


def _forward_rider(gathered, names):
    pieces = [_piece(n) for n in names]
    n = len(pieces)

    def build(ins, outs, sems):
        fwd_send, fwd_recv = sems
        x, y, c, s = _place()
        sibling = (x, y, 1 - c)
        starts, waits = [], []
        for p, (_, kind, shard, _) in enumerate(pieces):
            for k in (1, 2, 3):
                s2, _ = _other_chip(s, c, k)
                mine = _full_region(outs[p], kind, shard, s2, c)
                theirs = _full_region(outs[p], kind, shard, s2, 1 - c)
                cp = _remote(mine, mine, fwd_send.at[p, k - 1], fwd_recv.at[p, k - 1], sibling)
                starts.append(cp)
                waits.append(cp.wait_send)
                waits.append(_remote(theirs, theirs, fwd_send.at[p, k - 1], fwd_recv.at[p, k - 1], sibling).wait_recv)
        return starts, waits

    return _Rider(
        [gathered[name] for name in names],
        [jax.ShapeDtypeStruct(gathered[name].shape, gathered[name].dtype) for name in names],
        [pltpu.SemaphoreType.DMA((n, 3)), pltpu.SemaphoreType.DMA((n, 3))],
        build, aliased=True)


def _pair_exchange_rider(grads, names):
    n = len(names)

    def build(ins, outs, sems):
        send_sem, recv_sem = sems
        x, y, c, _ = _place()
        sibling = (x, y, 1 - c)
        starts, waits = [], []
        for p, name in enumerate(names):
            kind, shard = BIG_PIECES[name]
            for s2 in range(4):
                cp = _remote(_full_region(ins[p], kind, shard, s2, 1 - c), outs[p].at[s2], send_sem.at[p, s2],
                             recv_sem.at[p, s2], sibling)
                starts.append(cp)
                waits.append(cp.wait_send)
                waits.append(_remote(outs[p].at[s2], outs[p].at[s2], send_sem.at[p, s2], recv_sem.at[p, s2], sibling).wait_recv)
        return starts, waits

    return _Rider(
        [grads[name] for name in names],
        [jax.ShapeDtypeStruct((4,) + _half_shape(*BIG_PIECES[name]), F32) for name in names],
        [pltpu.SemaphoreType.DMA((n, 4))] * 2,
        build)


def _half_specs(kind, shard):
    half = shard[0] // 2
    if kind == "col":
        full = pl.BlockSpec((half, shard[1]), lambda j, pr: (pr[1], j))
        buf = pl.BlockSpec((None, half, shard[1]), lambda j, pr: (j, 0, 0))
    elif kind == "row":
        full = pl.BlockSpec((half, shard[1]), lambda j, pr: (2 * j + pr[1], 0))
        buf = pl.BlockSpec((None, half, shard[1]), lambda j, pr: (j, 0, 0))
    else:
        full = pl.BlockSpec((half, shard[1], shard[2]), lambda j, pr: (pr[1], j, 0))
        buf = pl.BlockSpec((None, half, shard[1], shard[2]), lambda j, pr: (j, 0, 0, 0))
    return full, buf


def _pair_sum(name, grad, recv, place):
    kind, shard = BIG_PIECES[name]
    full, buf = _half_specs(kind, shard)

    def body(pr, g_ref, r_ref, o_ref):
        o_ref[...] = (g_ref[...] + r_ref[...]).astype(BF16)

    return pl.pallas_call(
        body,
        name="pair_sum_" + name,
        grid_spec=pltpu.PrefetchScalarGridSpec(num_scalar_prefetch=1, grid=(4,), in_specs=[full, buf], out_specs=buf),
        out_shape=jax.ShapeDtypeStruct((4,) + _half_shape(kind, shard), BF16),
        compiler_params=_params(("arbitrary",)),
    )(place, grad, recv)


def _chip_exchange_rider(sums, names):
    n = len(names)

    def build(ins, outs, sems):
        send_sem, recv_sem = sems
        _, _, c, s = _place()
        starts, waits = [], []
        for p in range(n):
            for k in (1, 2, 3):
                s2, dev = _other_chip(s, c, k)
                cp = _remote(ins[p].at[s2], outs[p].at[k - 1], send_sem.at[p, k - 1], recv_sem.at[p, k - 1], dev)
                starts.append(cp)
                waits.append(cp.wait_send)
                waits.append(_remote(outs[p].at[k - 1], outs[p].at[k - 1], send_sem.at[p, k - 1], recv_sem.at[p, k - 1],
                                     dev).wait_recv)
        return starts, waits

    return _Rider(
        [sums[name] for name in names],
        [jax.ShapeDtypeStruct((3,) + _half_shape(*BIG_PIECES[name]), BF16) for name in names],
        [pltpu.SemaphoreType.DMA((n, 3))] * 2,
        build)


def _chip_sum(name, grad, recv_pair, recv_chip, place):
    kind, shard = BIG_PIECES[name]
    half = shard[0] // 2
    tail = tuple(shard[1:])
    zeros = (0,) * len(tail)
    nt = 4 if kind != "lru" and half % (4 * BF16_ROWS) == 0 else 1
    rows = half // nt
    if kind == "col":
        full = pl.BlockSpec((rows,) + tail, lambda j, pr: (pr[1] * nt + j, pr[0]))
    elif kind == "row":
        full = pl.BlockSpec((rows,) + tail, lambda j, pr: ((2 * pr[0] + pr[1]) * nt + j, 0))
    else:
        full = pl.BlockSpec((rows,) + tail, lambda j, pr: (pr[1], pr[0], 0))
    pair = pl.BlockSpec((None, rows) + tail, lambda j, pr: (pr[0], j) + zeros)
    chip = pl.BlockSpec((3, rows) + tail, lambda j, pr: (0, j) + zeros)
    out = pl.BlockSpec((rows,) + tail, lambda j, pr: (pr[1] * nt + j,) + zeros)

    def body(pr, g_ref, rp_ref, rc_ref, o_ref):
        total = g_ref[...] + rp_ref[...]
        for k in range(3):
            total = total + rc_ref[k].astype(F32)
        o_ref[...] = total

    return pl.pallas_call(
        body,
        name="chip_sum_" + name,
        grid_spec=pltpu.PrefetchScalarGridSpec(num_scalar_prefetch=1, grid=(nt,), in_specs=[full, pair, chip], out_specs=out),
        out_shape=jax.ShapeDtypeStruct(shard, F32),
        compiler_params=_params(("arbitrary",)),
    )(place, grad, recv_pair, recv_chip)


def _sibling_exchange_rider(halves, names):
    n = len(names)

    def build(ins, outs, sems):
        send_sem, recv_sem = sems
        x, y, c, _ = _place()
        sibling = (x, y, 1 - c)
        starts, waits = [], []
        for p, name in enumerate(names):
            shard = BIG_PIECES[name][1]
            mine = _shard_region(outs[p], shard, c)
            theirs = _shard_region(outs[p], shard, 1 - c)
            cp = _remote(mine, mine, send_sem.at[p], recv_sem.at[p], sibling)
            starts.append(cp)
            waits.append(cp.wait_send)
            waits.append(_remote(theirs, theirs, send_sem.at[p], recv_sem.at[p], sibling).wait_recv)
        return starts, waits

    return _Rider(
        [halves[name] for name in names],
        [jax.ShapeDtypeStruct(BIG_PIECES[name][1], F32) for name in names],
        [pltpu.SemaphoreType.DMA((n,))] * 2,
        build, aliased=True)


FIRST_WEIGHTS = ["meta_tokens", "conv_w"]
WEIGHT_GROUPS = {
    "lru": ["lru_wa", "lru_wx"],
    "branch": ["w_branch_ret", "w_branch_lru", "w_out"],
    "ffn_in": ["w_ffn_in"],
    "ffn_out": ["w_ffn_out"],
}
WEIGHT_SCHEDULE = {
    "in_proj": [("gather", "lru")],
    "retention_fwd": [("forward", "lru"), ("gather", "branch")],
    "lru_fwd": [("forward", "branch"), ("gather", "ffn_in")],
    "mix_fwd": [("forward", "ffn_in"), ("gather", "ffn_out")],
    "ffn_fwd_gate": [("forward", "ffn_out")],
}
GRAD_GROUPS = {
    "ffn_in": ["w_ffn_in"],
    "ffn_out": ["w_ffn_out"],
    "mixer": ["w_out", "w_branch_ret", "w_branch_lru", "lru_wa", "lru_wx"],
    "in": ["w_in"],
}
GRAD_SCHEDULE = {
    "mix_bwd": [("pair", "ffn_in"), ("pair", "ffn_out")],
    "retention_bwd": [("chip", "ffn_out")],
    "lru_bwd": [("chip", "ffn_in"), ("sibling", "ffn_out")],
    "dw_in": [("sibling", "ffn_in"), ("pair", "mixer")],
    "in_proj_bwd_0": [("chip", "mixer"), ("pair", "in")],
    "in_proj_bwd_1": [("sibling", "mixer"), ("chip", "in")],
}


class _CommPlan:
    def __init__(self, shards, place):
        self.shards, self.place = shards, place
        self.late = {}
        self.recv_pair, self.sums, self.recv_chip, self.halves, self.final = {}, {}, {}, {}, {}

    def _grad_rider(self, stage, group, grads):
        names = GRAD_GROUPS[group]
        if stage == "pair":
            return _pair_exchange_rider(grads, names)
        if stage == "chip":
            return _chip_exchange_rider(self.sums, names)
        return _sibling_exchange_rider(self.halves, names)

    def _grad_after(self, stage, group, outs, grads):
        names = GRAD_GROUPS[group]
        if stage == "pair":
            for n, o in zip(names, outs):
                self.recv_pair[n] = o
                self.sums[n] = _pair_sum(n, grads[n], o, self.place)
        elif stage == "chip":
            for n, o in zip(names, outs):
                self.halves[n] = _chip_sum(n, grads[n], self.recv_pair[n], o, self.place)
        else:
            self.final.update(zip(names, outs))

    def riders(self, host, w, grads):
        if host in WEIGHT_SCHEDULE:
            return [_gather_rider(self.shards, WEIGHT_GROUPS[group]) if stage == "gather"
                    else _forward_rider(self.late, WEIGHT_GROUPS[group]) for stage, group in WEIGHT_SCHEDULE[host]]
        return [self._grad_rider(stage, group, grads) for stage, group in GRAD_SCHEDULE.get(host, [])]

    def after(self, host, rider_outs, w, grads):
        for (stage, group), outs in zip(WEIGHT_SCHEDULE.get(host, []), rider_outs):
            (self.late if stage == "gather" else w).update(zip(WEIGHT_GROUPS[group], outs))
        for (stage, group), outs in zip(GRAD_SCHEDULE.get(host, []), rider_outs):
            self._grad_after(stage, group, outs, grads)

    def finish(self, partial):
        outs, (blocks,) = _run_riders("tail_exchange", [_sibling_exchange_rider(self.halves, GRAD_GROUPS["in"]),
                                                        _small_exchange_rider(partial)])
        self.final.update(zip(GRAD_GROUPS["in"], outs))
        return self.final, blocks


def _adamw_math(w, g, m, v):
    m = ADAM_B1 * m + (1.0 - ADAM_B1) * g
    v = ADAM_B2 * v + (1.0 - ADAM_B2) * (g * g)
    m_hat = m / (1.0 - ADAM_B1 ** ADAM_STEP)
    v_hat = v / (1.0 - ADAM_B2 ** ADAM_STEP)
    delta = -ADAM_LR * (m_hat / (jnp.sqrt(v_hat) + ADAM_EPS) + ADAM_WD * w)
    return delta, m, v


def _adamw(name, g, w, m, v):
    rows, cols = g.shape
    tr = rows // 4 if rows % 32 == 0 else rows

    def body(g_ref, w_ref, m_ref, v_ref, go_ref, d_ref, mo_ref, vo_ref):
        gv = g_ref[...]
        delta, m2, v2 = _adamw_math(w_ref[...], gv, m_ref[...], v_ref[...])
        go_ref[...] = gv
        d_ref[...] = delta
        mo_ref[...] = m2
        vo_ref[...] = v2

    spec = pl.BlockSpec((tr, cols), lambda i: (i, 0))
    return pl.pallas_call(
        body,
        name="adamw_" + name,
        grid=(rows // tr,),
        in_specs=[spec] * 4,
        out_specs=[spec] * 4,
        out_shape=[jax.ShapeDtypeStruct((rows, cols), F32)] * 4,
        compiler_params=_params(("arbitrary",)),
    )(g, w, m, v)


SMALL_ROWS = 48
VEC_ROWS = {"final_norm_w": 1, "ffn_norm_w": 8, "mix_norm_w": 16, "conv_b": 24, "lru_ba": 25, "lru_bx": 26, "lru_lambda": 27}
VEC_NAMES = list(VEC_ROWS)
CONV_W_ROW = 28
META_ROW = 32


def _small_exchange_rider(partial):
    def build(ins, outs, sems):
        local_sem, send_sem, recv_sem = sems
        _, _, c, s = _place()
        me = 2 * s + c
        cp = pltpu.make_async_copy(ins[0], outs[0].at[me], local_sem)
        starts, waits = [cp], [cp.wait]
        for k in range(1, 8):
            peer = jnp.bitwise_xor(me, k)
            dev = (peer // 4, (peer // 2) % 2, peer % 2)
            rd = _remote(ins[0], outs[0].at[me], send_sem.at[k - 1], recv_sem.at[k - 1], dev)
            starts.append(rd)
            waits.append(rd.wait_send)
            waits.append(_remote(ins[0], outs[0].at[peer], send_sem.at[k - 1], recv_sem.at[k - 1], dev).wait_recv)
        return starts, waits

    return _Rider([partial], [jax.ShapeDtypeStruct((8, SMALL_ROWS, D), F32)],
                  [pltpu.SemaphoreType.DMA, pltpu.SemaphoreType.DMA((7,)), pltpu.SemaphoreType.DMA((7,))], build)


def _small_update(blocks, place, vecs, conv, meta):
    nvec = len(VEC_NAMES)
    qcols = D // 4

    def in_order(ref):
        total = ref[0]
        for d in range(1, 8):
            total = total + ref[d]
        return total

    def body(pr, blocks_ref, cols_ref, *refs):
        vec_refs = refs[:3 * nvec]
        conv_refs = refs[3 * nvec:3 * nvec + 3]
        meta_refs = refs[3 * nvec + 3:3 * nvec + 6]
        outs = refs[3 * nvec + 6:]
        loss_ref, vec_out, conv_out, meta_out = outs[0], outs[1:5], outs[5:9], outs[9:13]
        tot, col = in_order(blocks_ref), in_order(cols_ref)
        loss_ref[...] = jnp.sum(tot[0:1, :], axis=1, keepdims=True)
        for o in vec_out:
            o[...] = jnp.zeros_like(o)
        for j, name in enumerate(VEC_NAMES):
            w, m, v = (r[...] for r in vec_refs[3 * j:3 * j + 3])
            g = tot[VEC_ROWS[name]:VEC_ROWS[name] + 1, :]
            if name == "lru_lambda":
                g = -g / (1.0 + jnp.exp(w))
            for o, val in zip(vec_out, (g,) + _adamw_math(w, g, m, v)):
                o[j:j + 1, :] = val
        for row, n_rows, ins, group in ((CONV_W_ROW, 4, conv_refs, conv_out), (META_ROW, N_META, meta_refs, meta_out)):
            g = col[row:row + n_rows, :]
            w, m, v = (r[...] for r in ins)
            for o, val in zip(group, (g,) + _adamw_math(w, g, m, v)):
                o[...] = val

    whole = lambda shape: pl.BlockSpec(shape, lambda i, pr: (0,) * len(shape))
    in_specs = [whole((8, SMALL_ROWS, D)), pl.BlockSpec((8, SMALL_ROWS, qcols), lambda i, pr: (0, 0, pr[0]))]
    in_specs += [whole((1, D))] * (3 * nvec) + [whole((4, qcols))] * 3 + [whole((N_META, qcols))] * 3
    out_shapes = [(1, 1)] + [(8, D)] * 4 + [(4, qcols)] * 4 + [(N_META, qcols)] * 4
    return pl.pallas_call(
        body,
        name="small_update",
        grid_spec=pltpu.PrefetchScalarGridSpec(num_scalar_prefetch=1, grid=(1,), in_specs=in_specs,
                                               out_specs=[whole(s) for s in out_shapes]),
        out_shape=[jax.ShapeDtypeStruct(s, F32) for s in out_shapes],
        compiler_params=_params(("arbitrary",)),
    )(place, blocks, blocks, *[a for t in vecs for a in t], *conv, *meta)


WEIGHT_ORDER = ["meta_tokens", "mix_norm_w", "w_in", "conv_w", "conv_b", "lru_wa", "lru_ba", "lru_wx", "lru_bx", "lru_lambda",
                "w_branch_ret", "w_branch_lru", "w_out", "ffn_norm_w", "w_ffn_in", "w_ffn_out", "final_norm_w"]


def kernel(x, meta_tokens, mix_norm_w, w_in, conv_w, conv_b, lru_wa, lru_ba, lru_wx, lru_bx, lru_lambda, w_branch_ret, w_branch_lru, w_out, ffn_norm_w, w_ffn_in, w_ffn_out, final_norm_w, loss_target, m_meta_tokens, m_mix_norm_w, m_w_in, m_conv_w, m_conv_b, m_lru_wa, m_lru_ba, m_lru_wx, m_lru_bx, m_lru_lambda, m_w_branch_ret, m_w_branch_lru, m_w_out, m_ffn_norm_w, m_w_ffn_in, m_w_ffn_out, m_final_norm_w, v_meta_tokens, v_mix_norm_w, v_w_in, v_conv_w, v_conv_b, v_lru_wa, v_lru_ba, v_lru_wx, v_lru_bx, v_lru_lambda, v_w_branch_ret, v_w_branch_lru, v_w_out, v_ffn_norm_w, v_w_ffn_in, v_w_ffn_out, v_final_norm_w):
    args = locals()
    wts = {n: args[n] for n in WEIGHT_ORDER}
    mom = {n: args["m_" + n] for n in WEIGHT_ORDER}
    var = {n: args["v_" + n] for n in WEIGHT_ORDER}
    place = jnp.stack([2 * lax.axis_index("x") + lax.axis_index("y"), lax.axis_index("c")]).astype(jnp.int32)

    shards = {n: wts[n][0].astype(BF16) for n in BIG_PIECES}
    shards["meta_tokens"] = wts["meta_tokens"]
    shards["conv_w"] = wts["conv_w"][0]
    plan = _CommPlan(shards, place)
    (first,) = _run_riders("gather_first", [_gather_rider(shards, FIRST_WEIGHTS)])
    w = dict(zip(FIRST_WEIGHTS, first))
    for n in VEC_NAMES:
        w[n] = wts[n].reshape(1, D)
    chips = jnp.bitwise_xor(place[0], jnp.array((0,) + SHARD_ORDER, dtype=jnp.int32))
    w["route"] = jnp.concatenate([place, jnp.stack([2 * chips, 2 * chips + 1], axis=1).reshape(8)])
    w["w_in_shard"] = shards["w_in"]

    grad_x, grad_head, _, stats = _local_step(x[0], loss_target[0], w, plan)
    partial = jnp.concatenate(stats + [grad_head[PAD_ROWS:]], axis=0)
    shard_grads, blocks = plan.finish(partial)

    out = {}
    for n in BIG_PIECES:
        shape2d = (-1, wts[n].shape[-1])
        res = _adamw(n, *[a.reshape(shape2d) for a in (shard_grads[n], wts[n], mom[n], var[n])])
        out[n] = [r.reshape(wts[n].shape) for r in res]

    vecs = [tuple(a[n].reshape(1, D) for a in (wts, mom, var)) for n in VEC_NAMES]
    conv = tuple(a["conv_w"][0] for a in (wts, mom, var))
    meta = tuple(a["meta_tokens"] for a in (wts, mom, var))
    res = _small_update(blocks, place, vecs, conv, meta)
    loss = res[0].reshape(())
    for j, n in enumerate(VEC_NAMES):
        out[n] = [r[j].reshape(wts[n].shape) for r in res[1:5]]
    out["conv_w"] = [r.reshape(wts["conv_w"].shape) for r in res[5:9]]
    out["meta_tokens"] = list(res[9:13])

    return (loss, grad_x.reshape(x.shape)) + tuple(out[n][kind] for kind in range(4) for n in WEIGHT_ORDER)
```

```python
import functools
import math

import jax
import jax.numpy as jnp
from jax import lax
from jax.experimental import pallas as pl
from jax.experimental.pallas import tpu as pltpu

F32 = jnp.float32
BF16 = jnp.bfloat16

LANES = 128
BF16_ROWS = 16

D = 1024
HEADS = 8
DH = 128
CHUNK = 256
N_META = 16
FRONT = 256
PAD_ROWS = FRONT - N_META
LRU_BLOCKS = 4
LRU_BLOCK = 256
LRU_C = 8.0
FFN = 2816
FFN_HALF = FFN // 2
IN_COLS = 8 * D
ROPE_BASE = 10000.0
EPS = 1e-6
QK_SCALE = DH ** -0.5

ADAM_LR = 0.001
ADAM_B1 = 0.9
ADAM_B2 = 0.999
ADAM_EPS = 1e-08
ADAM_WD = 0.01
ADAM_STEP = 10

TM = 256
TM_HEAVY = 768
MXU_TILE = 256
FFN_SUB = 2 * MXU_TILE
_FFN_SUBS = [(a, min(a + FFN_SUB, FFN)) for a in range(0, FFN, FFN_SUB)]
TM_MIX_BWD = 384
TM_LRU = 384
TM_MIX_FWD = 384
TM_IN_PROJ = 1408
VMEM_LIMIT = 60 * 1024 * 1024
TN_VMEM_BUDGET = 40 * 1024 * 1024

NT_DIMS = (((1,), (1,)), ((), ()))
TN_DIMS = (((0,), (0,)), ((), ()))
MESH = pl.DeviceIdType.MESH


def _params(sem=None):
    if sem is None:
        return pltpu.CompilerParams(vmem_limit_bytes=VMEM_LIMIT)
    return pltpu.CompilerParams(dimension_semantics=sem, vmem_limit_bytes=VMEM_LIMIT)


def _dot(a, b):
    return jnp.dot(a, b, preferred_element_type=F32)


def _dot_nt(a, b):
    return lax.dot_general(a, b, NT_DIMS, preferred_element_type=F32)


def _dot_tn(a, b):
    return lax.dot_general(a, b, TN_DIMS, preferred_element_type=F32)


def _sigmoid(z):
    return 1.0 / (1.0 + jnp.exp(-z))


def _log1p(x):
    return jnp.where(x < 1e-3, x * (1.0 - x * (0.5 - x * (1.0 / 3.0))), jnp.log(1.0 + x))


def _softplus(x):
    return jnp.maximum(x, 0.0) + _log1p(jnp.exp(-jnp.abs(x)))


def _one_minus_square(a, log_a):
    x = 2.0 * log_a
    series = -x * (1.0 + x * (0.5 + x * (1.0 / 6.0)))
    return jnp.where(x > -0.02, series, 1.0 - a * a)


_GELU_K = math.sqrt(2.0 / math.pi)


def _gelu_and_grad(x):
    inner = _GELU_K * (x + 0.044715 * x * x * x)
    t = jnp.tanh(inner)
    val = 0.5 * x * (1.0 + t)
    grad = 0.5 * (1.0 + t) + 0.5 * x * (1.0 - t * t) * _GELU_K * (1.0 + 3.0 * 0.044715 * x * x)
    return val, grad


def _row_ids(i, rows, shape):
    return i * rows + lax.broadcasted_iota(jnp.int32, shape, 0)


def _rope_tables(rows):
    inv_freq = ROPE_BASE ** (-jnp.arange(0, DH, 2, dtype=F32) / DH)
    block_pos = jnp.arange(rows // FRONT, dtype=jnp.int32) * FRONT - PAD_ROWS
    coarse = block_pos.astype(F32)[:, None] * inv_freq[None, :]
    fine = jnp.arange(FRONT, dtype=F32)[:, None] * inv_freq[None, :]
    ca, sa = jnp.cos(coarse)[:, None, :], jnp.sin(coarse)[:, None, :]
    cb, sb = jnp.cos(fine)[None, :, :], jnp.sin(fine)[None, :, :]
    cos = (ca * cb - sa * sb).reshape(rows, DH // 2)
    sin = (sa * cb + ca * sb).reshape(rows, DH // 2)
    return jnp.concatenate([cos, cos], axis=1), jnp.concatenate([-sin, sin], axis=1)


def _decay_tables():
    log_g = jnp.log(1.0 - 2.0 ** (-5.0 - jnp.arange(HEADS, dtype=F32)))
    idx = jnp.arange(CHUNK, dtype=F32)
    diff = idx[:, None] - idx[None, :]
    intra = jnp.where(diff[None] >= 0, jnp.exp(jnp.maximum(diff, 0.0)[None] * log_g[:, None, None]), 0.0)
    q_decay = jnp.exp((idx + 1.0)[:, None] * log_g[None, :])
    k_decay = jnp.exp((CHUNK - 1.0 - idx)[:, None] * log_g[None, :])
    chunk_decay = jnp.exp(CHUNK * log_g)
    wide = lambda a: jnp.repeat(a, DH, axis=-1)
    return intra, jnp.swapaxes(intra, 1, 2), wide(q_decay), wide(k_decay), wide(chunk_decay[None, :])


def _norm1(head, x2d, norm_w, riders=()):
    rows = FRONT + x2d.shape[0]
    tm = TM_IN_PROJ if rows % TM_IN_PROJ == 0 else TM_MIX_FWD
    nt = rows // tm

    def body(head_hbm, x_hbm, nw_ref, u_ref, h0_sc, h0_sem):
        slot = _frame_rows(head_hbm, x_hbm, h0_sc, h0_sem, pl.program_id(0), tm, nt)
        hv = h0_sc[slot]
        rs = lax.rsqrt(jnp.mean(hv * hv, axis=-1, keepdims=True) + EPS)
        u_ref[...] = ((hv * rs) * nw_ref[...]).astype(BF16)

    return _hosted_call(
        body,
        name="norm1",
        grid=(nt,),
        in_specs=[_ANY, _ANY, pl.BlockSpec((1, D), lambda i: (0, 0))],
        out_specs=[pl.BlockSpec((tm, D), lambda i: (i, 0))],
        out_shape=[jax.ShapeDtypeStruct((rows, D), BF16)],
        scratch_shapes=[pltpu.VMEM((2, tm, D), F32), pltpu.SemaphoreType.DMA((3,))],
        args=(head, x2d, norm_w),
        riders=riders,
    )


def _heavy_tile(rows):
    return TM_HEAVY if rows % TM_HEAVY == 0 else TM


def _lru_tile(rows):
    return TM_LRU if rows % TM_LRU == 0 else TM


def _swap_3_4(group):
    return jnp.where(group == 3, 4, jnp.where(group == 4, 3, group))


SHARD_ORDER = (2, 3, 1)
LRU_IN_COL = 3
GATES_COL = 1


def _in_proj(u, w_shard, route, cos_t, sin_t, riders=()):
    rows = u.shape[0]
    tm = TM_IN_PROJ if rows % TM_IN_PROJ == 0 else _heavy_tile(rows)
    nt = rows // tm
    kind, shard = BIG_PIECES["w_in"]

    def body(route_ref, u_hbm, wsh_ref, cos_ref, sin_ref, proj_ref, wfull_ref,
             u_sc, w_sc, u_sem, w_sem, local_sem, ici_send, ici_recv, fwd_send, fwd_recv):
        g, i = pl.program_id(0), pl.program_id(1)
        s, c = route_ref[0], route_ref[1]
        gid = route_ref[2 + g]
        sibling = (s // 2, s % 2, 1 - c)
        local = pltpu.make_async_copy(wsh_ref, _full_region(wfull_ref, kind, shard, s, None), local_sem)
        u_copies = [pltpu.make_async_copy(u_hbm.at[pl.ds(t * tm, tm)], u_sc.at[t], u_sem.at[t]) for t in range(nt)]
        sends, arrivals = [], []
        for k in (1, 2, 3):
            s2, dev = _other_chip(s, c, k)
            sends.append(_remote(_shard_region(wsh_ref, shard, c), _full_region(wfull_ref, kind, shard, s, c),
                                 ici_send.at[k - 1], ici_recv.at[k - 1], dev))
            mine = _full_region(wfull_ref, kind, shard, s2, c)
            theirs = _full_region(wfull_ref, kind, shard, s2, 1 - c)
            arrivals.append((_remote(mine, mine, ici_send.at[k - 1], ici_recv.at[k - 1], dev),
                             _remote(mine, mine, fwd_send.at[k - 1], fwd_recv.at[k - 1], sibling),
                             _remote(theirs, theirs, fwd_send.at[k - 1], fwd_recv.at[k - 1], sibling)))

        slot = g % 2
        last_tile = i == nt - 1

        def block_copy(src, col, to_slot):
            return pltpu.make_async_copy(src.at[:, pl.ds(pl.multiple_of(col * D, D), D)], w_sc.at[to_slot],
                                         w_sem.at[to_slot])

        @pl.when(jnp.logical_and(g == 0, i == 0))
        def _():
            for k in SHARD_ORDER:
                sends[k - 1].start()
            local.start()
            for cp in u_copies:
                cp.start()
            cp = block_copy(wsh_ref, 0, 0)
            cp.start()
            cp.wait()

        @pl.when(jnp.logical_and(last_tile, g == 0))
        def _():
            block_copy(wsh_ref, 1, 1).start()

        for pos, k in enumerate(SHARD_ORDER, start=1):
            arrived, forward, forwarded = arrivals[k - 1]

            @pl.when(jnp.logical_and(last_tile, g == 2 * pos - 1))
            def _():
                arrived.wait_recv()
                forward.start()
                forwarded.wait_recv()
                block_copy(wfull_ref, route_ref[2 + 2 * pos], 0).start()

            @pl.when(jnp.logical_and(last_tile, g == 2 * pos))
            def _():
                block_copy(wfull_ref, route_ref[3 + 2 * pos], 1).start()

        @pl.when(jnp.logical_and(i == 0, g > 0))
        def _():
            block_copy(wfull_ref, 0, slot).wait()

        for t in range(nt):
            @pl.when(jnp.logical_and(g == 0, i == t))
            def _():
                u_copies[t].wait()

        halves = range(0, D, D // 2)

        @pl.when(gid < 2)
        def _():
            scale = jnp.where(gid == 1, QK_SCALE, 1.0).astype(F32)
            for a in halves:
                acc = _dot(u_sc[i], w_sc[slot, :, a:a + D // 2])
                for h in range(HEADS // 2):
                    blk = acc[:, h * DH:(h + 1) * DH]
                    out = (blk * cos_ref[...] + pltpu.roll(blk, DH // 2, axis=1) * sin_ref[...]) * scale
                    proj_ref[:, a + h * DH:a + (h + 1) * DH] = out.astype(BF16)

        @pl.when(gid >= 2)
        def _():
            for a in halves:
                proj_ref[:, a:a + D // 2] = _dot(u_sc[i], w_sc[slot, :, a:a + D // 2]).astype(BF16)

        @pl.when(jnp.logical_and(g == 7, i == nt - 1))
        def _():
            local.wait()
            for cp in sends:
                cp.wait_send()
            for _, forward, _ in arrivals:
                forward.wait_send()

    return _hosted_call(
        body,
        name="in_proj",
        grid=(8, nt),
        in_specs=[
            _ANY, _ANY,
            pl.BlockSpec((tm, DH), lambda g, i, rt: (i, 0)),
            pl.BlockSpec((tm, DH), lambda g, i, rt: (i, 0)),
        ],
        out_specs=[pl.BlockSpec((tm, D), lambda g, i, rt: (i, _swap_3_4(rt[2 + g]))), _ANY],
        out_shape=[jax.ShapeDtypeStruct((rows, IN_COLS), BF16), jax.ShapeDtypeStruct((D, IN_COLS), BF16)],
        scratch_shapes=[
            pltpu.VMEM((nt, tm, D), BF16), pltpu.VMEM((2, D, D), BF16),
            pltpu.SemaphoreType.DMA((nt,)), pltpu.SemaphoreType.DMA((2,)), pltpu.SemaphoreType.DMA,
            pltpu.SemaphoreType.DMA((3,)), pltpu.SemaphoreType.DMA((3,)),
            pltpu.SemaphoreType.DMA((3,)), pltpu.SemaphoreType.DMA((3,)),
        ],
        args=(u, w_shard, cos_t, sin_t),
        riders=riders,
        prefetch=route,
        riders_after_body=True,
    )


def _retention_fwd(proj_bf, intra, q_dec, k_dec, c_dec, riders=()):
    rows = proj_bf.shape[0]
    nc = rows // CHUNK

    def body(q_ref, k_ref, v_ref, m_ref, qd_ref, kd_ref, cd_ref, o_ref, sprev_ref, s_sc):
        @pl.when(pl.program_id(0) == 0)
        def _():
            s_sc[...] = jnp.zeros_like(s_sc)

        for h in range(HEADS):
            sl = slice(h * DH, (h + 1) * DH)
            q, k, v = q_ref[:, sl], k_ref[:, sl], v_ref[:, sl]
            state = s_sc[h]
            state_b = state.astype(BF16)
            sprev_ref[0, h] = state_b
            s = _dot_nt(q, k) * m_ref[h]
            inner = _dot(s.astype(BF16), v)
            cross = _dot(q, state_b) * qd_ref[:, sl]
            o_ref[:, sl] = (inner + cross).astype(BF16)
            k_scaled = (k.astype(F32) * kd_ref[:, sl]).astype(BF16)
            s_sc[h] = state * cd_ref[:, sl] + _dot_tn(k_scaled, v)

    chunk_spec = lambda col: pl.BlockSpec((CHUNK, D), lambda c: (c, col))
    const2 = lambda shape: pl.BlockSpec(shape, lambda c: (0, 0))
    return _hosted_call(
        body,
        name="retention_fwd",
        grid=(nc,),
        in_specs=[
            chunk_spec(0), chunk_spec(1), chunk_spec(2),
            pl.BlockSpec((HEADS, CHUNK, CHUNK), lambda c: (0, 0, 0)),
            const2((CHUNK, D)), const2((CHUNK, D)), const2((1, D)),
        ],
        out_specs=[
            pl.BlockSpec((CHUNK, D), lambda c: (c, 0)),
            pl.BlockSpec((1, HEADS, DH, DH), lambda c: (c, 0, 0, 0)),
        ],
        out_shape=[
            jax.ShapeDtypeStruct((rows, D), BF16),
            jax.ShapeDtypeStruct((nc, HEADS, DH, DH), BF16),
        ],
        scratch_shapes=[pltpu.VMEM((HEADS, DH, DH), F32)],
        args=(proj_bf, proj_bf, proj_bf, intra, q_dec, k_dec, c_dec),
        riders=riders,
    )


def _shift_down(x, first8_prev, d):
    rolled = pltpu.roll(x, d, axis=0)
    head = pltpu.roll(jnp.concatenate([first8_prev, x[0:8]], axis=0), d, axis=0)[8:16]
    return rolled, head


def _conv_and_gates(x, prev8, cw_ref, cb_ref, wa_ref, wx_ref, ba_ref, bx_ref, lam_ref, c_sc):
    cw = cw_ref[...]
    conv = cb_ref[...] + cw[3:4] * x
    head = cb_ref[...] + cw[3:4] * x[0:8]
    for d in (1, 2, 3):
        rolled, hd = _shift_down(x, prev8, d)
        conv = conv + cw[3 - d:4 - d] * rolled
        head = head + cw[3 - d:4 - d] * hd
    c_sc[...] = conv
    c_sc[0:8, :] = head
    c = c_sc[...]
    zr, zi = [], []
    for g in range(LRU_BLOCKS):
        sl = slice(g * LRU_BLOCK, (g + 1) * LRU_BLOCK)
        cg = c[:, sl].astype(BF16)
        zr.append(_dot(cg, wa_ref[g]))
        zi.append(_dot(cg, wx_ref[g]))
    r = _sigmoid(jnp.concatenate(zr, axis=1) + ba_ref[...])
    gate_i = _sigmoid(jnp.concatenate(zi, axis=1) + bx_ref[...])
    sp = _softplus(-lam_ref[...])
    log_a = (-LRU_C) * r * sp
    a = jnp.exp(log_a)
    mult = jnp.sqrt(_one_minus_square(a, log_a))
    return c, r, gate_i, a, mult, log_a


def _lru_fwd(proj, conv_w, conv_b, wa, wx, ba, bx, lam, riders=()):
    rows = proj.shape[0]
    TM = _lru_tile(rows)
    nt = rows // TM

    def body(x_ref, cw_ref, cb_ref, wa_ref, wx_ref, ba_ref, bx_ref, lam_ref,
             h_ref, c_ref, r_ref, i_ref, la_ref, mult_ref, prev_sc, carry_sc, c_sc, a_sc, u_sc, h_sc):
        i = pl.program_id(0)

        @pl.when(i == 0)
        def _():
            prev_sc[...] = jnp.zeros_like(prev_sc)
            carry_sc[...] = jnp.zeros_like(carry_sc)

        x = x_ref[...].astype(F32)
        c, r, gate_i, a, mult, log_a = _conv_and_gates(x, prev_sc[...], cw_ref, cb_ref, wa_ref, wx_ref, ba_ref, bx_ref,
                                                       lam_ref, c_sc)
        for ref, val in ((c_ref, c), (r_ref, r), (i_ref, gate_i), (la_ref, log_a), (mult_ref, mult)):
            ref[...] = val.astype(BF16)
        prev_sc[...] = x[TM - 8:TM]
        valid = _row_ids(i, TM, (TM, D)) >= PAD_ROWS
        a_sc[...] = a
        u_sc[...] = jnp.where(valid, mult * gate_i * c, 0.0)
        row8 = lax.broadcasted_iota(jnp.int32, (8, D), 0)

        def group(gi, hprev):
            r0 = pl.multiple_of(gi * 8, 8)
            aa = a_sc[pl.ds(r0, 8), :]
            uu = u_sc[pl.ds(r0, 8), :]
            for d in (1, 2, 4):
                a_sh = jnp.where(row8 >= d, pltpu.roll(aa, d, axis=0), 1.0)
                u_sh = jnp.where(row8 >= d, pltpu.roll(uu, d, axis=0), 0.0)
                uu = uu + aa * u_sh
                aa = aa * a_sh
            hb = aa * hprev + uu
            h_sc[pl.ds(r0, 8), :] = hb
            return hb[7:8, :]

        hlast = lax.fori_loop(0, TM // 8, group, carry_sc[0:1, :])
        carry_sc[0:1, :] = hlast
        h_ref[...] = h_sc[...].astype(BF16)

    vec = pl.BlockSpec((1, D), lambda i: (0, 0))
    wspec = pl.BlockSpec((LRU_BLOCKS, LRU_BLOCK, LRU_BLOCK), lambda i: (0, 0, 0))
    return _hosted_call(
        body,
        name="lru_fwd",
        grid=(nt,),
        in_specs=[
            pl.BlockSpec((TM, D), lambda i: (i, LRU_IN_COL)),
            pl.BlockSpec((4, D), lambda i: (0, 0)),
            vec, wspec, wspec, vec, vec, vec,
        ],
        out_specs=[pl.BlockSpec((TM, D), lambda i: (i, 0)) for _ in range(6)],
        out_shape=[jax.ShapeDtypeStruct((rows, D), BF16) for _ in range(6)],
        scratch_shapes=[
            pltpu.VMEM((8, D), F32), pltpu.VMEM((8, D), F32),
            pltpu.VMEM((TM, D), F32), pltpu.VMEM((TM, D), F32), pltpu.VMEM((TM, D), F32), pltpu.VMEM((TM, D), F32),
        ],
        args=(proj, conv_w, conv_b, wa, wx, ba, bx, lam),
        riders=riders,
    )


def _group_norm(o):
    outs, rstds = [], []
    for h in range(HEADS):
        oh = o[:, h * DH:(h + 1) * DH]
        rs = lax.rsqrt(jnp.mean(oh * oh, axis=-1, keepdims=True) + EPS)
        outs.append(oh * rs)
        rstds.append(rs)
    return jnp.concatenate(outs, axis=1), rstds


def _frame_rows(head_hbm, x_hbm, buf, sems, i, tm, nt):
    slot = i % 2

    @pl.when(i == 0)
    def _():
        first = [pltpu.make_async_copy(head_hbm, buf.at[0, pl.ds(0, FRONT)], sems.at[2]),
                 pltpu.make_async_copy(x_hbm.at[pl.ds(0, tm - FRONT)], buf.at[0, pl.ds(FRONT, tm - FRONT)], sems.at[0])]
        for cp in first:
            cp.start()
        for cp in first:
            cp.wait()

    @pl.when(i + 1 < nt)
    def _():
        start = pl.multiple_of((i + 1) * tm - FRONT, LANES)
        pltpu.make_async_copy(x_hbm.at[pl.ds(start, tm)], buf.at[1 - slot], sems.at[1 - slot]).start()

    @pl.when(i > 0)
    def _():
        pltpu.make_async_copy(x_hbm.at[pl.ds(0, tm)], buf.at[slot], sems.at[slot]).wait()

    return slot


def _mix_fwd(head, x2d, o, proj, h_lru, w_br, w_bl, w_o, ffn_norm_w, riders=()):
    rows = o.shape[0]
    tm = TM_MIX_FWD
    assert rows % tm == 0 and tm > FRONT
    nt = rows // tm

    def body(head_hbm, x_hbm, o_ref, gates_ref, hl_ref, wbr_ref, wbl_ref, wo_ref, nw_ref,
             yret_ref, ylru_ref, h1_ref, u2_ref, rstd_ref, h0_sc, h0_sem):
        i = pl.program_id(0)
        slot = _frame_rows(head_hbm, x_hbm, h0_sc, h0_sem, i, tm, nt)
        gate = lambda j: gates_ref[:, j * D:(j + 1) * D].astype(F32)
        on, _ = _group_norm(o_ref[...].astype(F32))
        gret = gate(0)
        a_ret = (gret * _sigmoid(gret) * on).astype(BF16)
        y_ret = _dot(a_ret, wbr_ref[...])
        gl, _ = _gelu_and_grad(gate(1))
        a_lru = (gl * hl_ref[...].astype(F32)).astype(BF16)
        y_lru = _dot(a_lru, wbl_ref[...])
        mixed = (_sigmoid(gate(2)) * y_ret + _sigmoid(gate(3)) * y_lru).astype(BF16)
        delta = _dot(mixed, wo_ref[...])
        yret_ref[...] = y_ret.astype(BF16)
        ylru_ref[...] = y_lru.astype(BF16)
        h1 = h0_sc[slot] + delta
        rs = lax.rsqrt(jnp.mean(h1 * h1, axis=-1, keepdims=True) + EPS)
        h1_ref[...] = h1
        u2_ref[...] = ((h1 * rs) * nw_ref[...]).astype(BF16)
        rstd_ref[...] = rs

    tile = lambda col: pl.BlockSpec((tm, D), lambda i: (i, col))
    wspec = pl.BlockSpec((D, D), lambda i: (0, 0))
    return _hosted_call(
        body,
        name="mix_fwd",
        grid=(nt,),
        in_specs=[
            _ANY, _ANY,
            tile(0), pl.BlockSpec((tm, 4 * D), lambda i: (i, GATES_COL)), tile(0),
            wspec, wspec, wspec,
            pl.BlockSpec((1, D), lambda i: (0, 0)),
        ],
        out_specs=[tile(0), tile(0), tile(0), tile(0), pl.BlockSpec((tm, 1), lambda i: (i, 0))],
        out_shape=[
            jax.ShapeDtypeStruct((rows, D), BF16), jax.ShapeDtypeStruct((rows, D), BF16),
            jax.ShapeDtypeStruct((rows, D), F32), jax.ShapeDtypeStruct((rows, D), BF16),
            jax.ShapeDtypeStruct((rows, 1), F32),
        ],
        scratch_shapes=[pltpu.VMEM((2, tm, D), F32), pltpu.SemaphoreType.DMA((3,))],
        args=(head, x2d, o, proj, h_lru, w_br, w_bl, w_o, ffn_norm_w),
        riders=riders,
    )


def _ffn_fwd_gate(u2, w_ffn_in, riders=()):
    rows = u2.shape[0]
    tm = _heavy_tile(rows)
    hid = lambda: pl.BlockSpec((tm, FFN), lambda i: (i, 0))

    def body(u2_ref, w_hbm, silu_ref, dsilu_ref, act_ref, w_sc, w_sem):
        @pl.when(pl.program_id(0) == 0)
        def _():
            cp = pltpu.make_async_copy(w_hbm, w_sc, w_sem)
            cp.start()
            cp.wait()

        u2 = u2_ref[...]
        for a, b in _FFN_SUBS:
            g = _dot(u2, w_sc[:, a:b])
            up = _dot(u2, w_sc[:, FFN + a:FFN + b])
            sg = _sigmoid(g)
            silu = g * sg
            silu_ref[:, a:b] = silu.astype(BF16)
            dsilu_ref[:, a:b] = (up * (sg * (1.0 + g * (1.0 - sg)))).astype(BF16)
            act_ref[:, a:b] = (silu * up).astype(BF16)

    return _hosted_call(
        body,
        name="ffn_fwd_gate",
        grid=(rows // tm,),
        in_specs=[pl.BlockSpec((tm, D), lambda i: (i, 0)), _ANY],
        out_specs=[hid(), hid(), hid()],
        out_shape=[jax.ShapeDtypeStruct((rows, FFN), BF16)] * 3,
        scratch_shapes=[pltpu.VMEM((D, 2 * FFN), BF16), pltpu.SemaphoreType.DMA],
        args=(u2, w_ffn_in),
        riders=riders,
    )


def _ffn_out_loss(act, h1, w_ffn_out, target, final_norm_w):
    rows = act.shape[0]
    tm = _heavy_tile(rows)
    nt = rows // tm

    def body(act_ref, h1_ref, wo_ref, fnw_ref, tgt_hbm, dh2_ref, stats_ref, tgt_sc, tgt_sem):
        i = pl.program_id(0)
        slot = i % 2

        @pl.when(i == 0)
        def _():
            stats_ref[...] = jnp.zeros_like(stats_ref)
            tgt_sc[0, 0:FRONT, :] = jnp.zeros((FRONT, D), F32)
            cp = pltpu.make_async_copy(tgt_hbm.at[pl.ds(0, tm - FRONT)], tgt_sc.at[0, pl.ds(FRONT, tm - FRONT)], tgt_sem.at[0])
            cp.start()
            cp.wait()

        @pl.when(i + 1 < nt)
        def _():
            start = pl.multiple_of((i + 1) * tm - FRONT, FRONT)
            pltpu.make_async_copy(tgt_hbm.at[pl.ds(start, tm)], tgt_sc.at[1 - slot], tgt_sem.at[1 - slot]).start()

        @pl.when(i > 0)
        def _():
            pltpu.make_async_copy(tgt_hbm.at[pl.ds(0, tm)], tgt_sc.at[slot], tgt_sem.at[slot]).wait()

        fnw = fnw_ref[...]
        half = tm // 2
        for r in (0, half):
            sl = pl.ds(r, half)
            h2 = h1_ref[sl, :] + _dot(act_ref[sl, :], wo_ref[...])
            rs = lax.rsqrt(jnp.mean(h2 * h2, axis=-1, keepdims=True) + EPS)
            n = h2 * rs
            valid = i * tm + r + lax.broadcasted_iota(jnp.int32, (half, D), 0) >= FRONT
            diff = jnp.where(valid, n * fnw - tgt_sc[slot, sl, :], 0.0)
            dy = diff * (1.0 / D)
            stats_ref[0:1, :] += (0.5 / D) * jnp.sum(diff * diff, axis=0, keepdims=True)
            stats_ref[1:2, :] += jnp.sum(dy * n, axis=0, keepdims=True)
            dn = dy * fnw
            dh2_ref[sl, :] = rs * (dn - n * jnp.mean(dn * n, axis=-1, keepdims=True))

    return pl.pallas_call(
        body,
        name="ffn_out_loss",
        grid=(nt,),
        in_specs=[
            pl.BlockSpec((tm, FFN), lambda i: (i, 0)),
            pl.BlockSpec((tm, D), lambda i: (i, 0)),
            pl.BlockSpec((FFN, D), lambda i: (0, 0)),
            pl.BlockSpec((1, D), lambda i: (0, 0)),
            _ANY,
        ],
        out_specs=[pl.BlockSpec((tm, D), lambda i: (i, 0)), pl.BlockSpec((8, D), lambda i: (0, 0))],
        out_shape=[jax.ShapeDtypeStruct((rows, D), F32), jax.ShapeDtypeStruct((8, D), F32)],
        scratch_shapes=[pltpu.VMEM((2, tm, D), F32), pltpu.SemaphoreType.DMA((2,))],
        compiler_params=_params(("arbitrary",)),
    )(act, h1, w_ffn_out, final_norm_w, target)


def _ffn_bwd(dh2, silu, dsilu, h1, rstd2, w_ffn_in, w_ffn_out, ffn_norm_w):
    rows = dh2.shape[0]
    tm = _heavy_tile(rows)
    blk = lambda: pl.BlockSpec((tm, FFN), lambda i: (i, 0))

    def gate_body(dh2_ref, silu_ref, dsilu_ref, wo_hbm, dg_ref, dup_ref, dh2b_ref, wo_sc, wo_sem):
        @pl.when(pl.program_id(0) == 0)
        def _():
            cp = pltpu.make_async_copy(wo_hbm, wo_sc, wo_sem)
            cp.start()
            cp.wait()

        dh2b = dh2_ref[...].astype(BF16)
        dh2b_ref[...] = dh2b
        for a, b in _FFN_SUBS:
            dact = _dot_nt(dh2b, wo_sc[a:b, :])
            dup_ref[:, a:b] = (dact * silu_ref[:, a:b].astype(F32)).astype(BF16)
            dg_ref[:, a:b] = (dact * dsilu_ref[:, a:b].astype(F32)).astype(BF16)

    dg, dup, dh2b = pl.pallas_call(
        gate_body,
        name="ffn_bwd_gate",
        grid=(rows // tm,),
        in_specs=[pl.BlockSpec((tm, D), lambda i: (i, 0)), blk(), blk(), _ANY],
        out_specs=[blk(), blk(), pl.BlockSpec((tm, D), lambda i: (i, 0))],
        out_shape=[jax.ShapeDtypeStruct((rows, FFN), BF16)] * 2 + [jax.ShapeDtypeStruct((rows, D), BF16)],
        scratch_shapes=[pltpu.VMEM((FFN, D), BF16), pltpu.SemaphoreType.DMA],
        compiler_params=_params(("arbitrary",)),
    )(dh2, silu, dsilu, w_ffn_out)

    def body(dg_ref, dup_ref, dh2_ref, h1_ref, rstd_ref, w_hbm, nw_ref, dh1_ref, stats_ref, w_sc, w_sem):
        @pl.when(pl.program_id(0) == 0)
        def _():
            stats_ref[...] = jnp.zeros_like(stats_ref)
            cp = pltpu.make_async_copy(w_hbm, w_sc, w_sem)
            cp.start()
            cp.wait()

        half = tm // 2
        for r in (0, half):
            sl = pl.ds(r, half)
            du = _dot_nt(dg_ref[sl, :], w_sc[:, 0:FFN]) + _dot_nt(dup_ref[sl, :], w_sc[:, FFN:2 * FFN])
            rs = rstd_ref[sl, :]
            n = h1_ref[sl, :] * rs
            stats_ref[0:1, :] += jnp.sum(du * n, axis=0, keepdims=True)
            dn = du * nw_ref[...]
            dh1_ref[sl, :] = dh2_ref[sl, :] + rs * (dn - n * jnp.mean(dn * n, axis=-1, keepdims=True))

    row = lambda width: pl.BlockSpec((tm, width), lambda i: (i, 0))
    dh1, stats = pl.pallas_call(
        body,
        name="ffn_bwd_in",
        grid=(rows // tm,),
        in_specs=[row(FFN), row(FFN), row(D), row(D), row(1), _ANY, pl.BlockSpec((1, D), lambda i: (0, 0))],
        out_specs=[row(D), pl.BlockSpec((8, D), lambda i: (0, 0))],
        out_shape=[jax.ShapeDtypeStruct((rows, D), F32), jax.ShapeDtypeStruct((8, D), F32)],
        scratch_shapes=[pltpu.VMEM((D, 2 * FFN), BF16), pltpu.SemaphoreType.DMA],
        compiler_params=_params(("arbitrary",)),
    )(dg, dup, dh2, h1, rstd2, w_ffn_in, ffn_norm_w)
    return dg, dup, dh2b, dh1, stats


def _mix_bwd(dh1, o, proj, h_lru, y_ret, y_lru, w_br, w_bl, w_o, riders=()):
    rows = dh1.shape[0]
    tm = TM_MIX_BWD
    nt = rows // tm

    def body(dh1_ref, o_ref, gates_ref, hl_ref, yret_ref, ylru_ref, wbr_hbm, wbl_hbm, wo_hbm,
             dproj_ref, do_ref, dhl_ref, mixed_ref, aret_ref, alru_ref, dyret_ref, dylru_ref, w_sc, w_sem):
        @pl.when(pl.program_id(0) == 0)
        def _():
            copies = [pltpu.make_async_copy(src, w_sc.at[q], w_sem.at[q]) for q, src in enumerate((wbr_hbm, wbl_hbm, wo_hbm))]
            for cp in copies:
                cp.start()
            for cp in copies:
                cp.wait()

        wbr_ref, wbl_ref, wo_ref = w_sc.at[0], w_sc.at[1], w_sc.at[2]
        gate = lambda j: gates_ref[:, j * D:(j + 1) * D].astype(F32)
        dmixed = _dot_nt(dh1_ref[...].astype(BF16), wo_ref[...])
        y_ret, y_lru = yret_ref[...].astype(F32), ylru_ref[...].astype(F32)
        sa, sb = _sigmoid(gate(2)), _sigmoid(gate(3))
        mixed_ref[...] = (sa * y_ret + sb * y_lru).astype(BF16)
        dga = dmixed * y_ret * sa * (1.0 - sa)
        dgb = dmixed * y_lru * sb * (1.0 - sb)
        dy_ret = (dmixed * sa).astype(BF16)
        dy_lru = (dmixed * sb).astype(BF16)
        dyret_ref[...] = dy_ret
        dylru_ref[...] = dy_lru
        da_ret = _dot_nt(dy_ret, wbr_ref[...])
        da_lru = _dot_nt(dy_lru, wbl_ref[...])

        gret = gate(0)
        sg = _sigmoid(gret)
        silu = gret * sg
        on, rstds = _group_norm(o_ref[...].astype(F32))
        aret_ref[...] = (silu * on).astype(BF16)
        dgret = da_ret * on * (sg * (1.0 + gret * (1.0 - sg)))
        don = da_ret * silu
        for h in range(HEADS):
            sl = slice(h * DH, (h + 1) * DH)
            onh, donh = on[:, sl], don[:, sl]
            do_ref[:, sl] = (rstds[h] * (donh - onh * jnp.mean(donh * onh, axis=-1, keepdims=True))).astype(BF16)

        gl, gl_grad = _gelu_and_grad(gate(1))
        hl = hl_ref[...].astype(F32)
        alru_ref[...] = (gl * hl).astype(BF16)
        dlgate = da_lru * hl * gl_grad
        dhl_ref[...] = (da_lru * gl).astype(BF16)

        for j, val in enumerate((dgret, dlgate, dga, dgb)):
            dproj_ref[:, j * D:(j + 1) * D] = val.astype(BF16)

    tile = lambda col: pl.BlockSpec((tm, D), lambda i: (i, col))
    gates = pl.BlockSpec((tm, 4 * D), lambda i: (i, GATES_COL))
    bf = lambda: jax.ShapeDtypeStruct((rows, D), BF16)
    return _hosted_call(
        body,
        name="mix_bwd",
        grid=(nt,),
        in_specs=[tile(0), tile(0), gates, tile(0), tile(0), tile(0), _ANY, _ANY, _ANY],
        out_specs=[pl.BlockSpec((tm, 4 * D), lambda i: (i, GATES_COL))] + [tile(0)] * 7,
        out_shape=[jax.ShapeDtypeStruct((rows, IN_COLS), BF16)] + [bf() for _ in range(7)],
        scratch_shapes=[pltpu.VMEM((3, D, D), BF16), pltpu.SemaphoreType.DMA((3,))],
        args=(dh1, o, proj, h_lru, y_ret, y_lru, w_br, w_bl, w_o),
        riders=riders,
    )


def _retention_bwd(dproj, proj_bf, do, sprev, intra, intra_t, q_dec, k_dec, c_dec, cos_t, sin_t, riders=()):
    rows = proj_bf.shape[0]
    nc = rows // CHUNK

    def body(dproj_in_ref, q_ref, k_ref, v_ref, do_ref, sprev_ref, m_ref, mt_ref, qd_ref, kd_ref, cd_ref, cos_ref, sin_ref,
             dproj_ref, ds_sc):
        @pl.when(pl.program_id(0) == 0)
        def _():
            ds_sc[...] = jnp.zeros_like(ds_sc)

        cos, sin = cos_ref[...], sin_ref[...]

        def unrotate(dy):
            return dy * cos - pltpu.roll(dy, DH // 2, axis=1) * sin

        for h in range(HEADS):
            sl = slice(h * DH, (h + 1) * DH)
            q, k, v = q_ref[:, sl], k_ref[:, sl], v_ref[:, sl]
            do = do_ref[:, sl]
            dob = do.astype(BF16)
            doq = (do * qd_ref[:, sl]).astype(BF16)
            state_prev = sprev_ref[0, h]
            dstate = ds_sc[h]
            dstate_b = dstate.astype(BF16)
            s_t = (_dot_nt(k, q) * mt_ref[h]).astype(BF16)
            ds_t = (_dot_nt(v, dob) * mt_ref[h]).astype(BF16)
            ds = (_dot_nt(dob, v) * m_ref[h]).astype(BF16)
            kd = kd_ref[:, sl]
            dq = _dot(ds, k) + _dot_nt(doq, state_prev)
            dk = _dot(ds_t, q) + _dot_nt(v, dstate_b) * kd
            k_scaled = (k.astype(F32) * kd).astype(BF16)
            dv = _dot(s_t, dob) + _dot(k_scaled, dstate_b)
            ds_sc[h] = dstate * cd_ref[:, sl] + _dot_tn(q, doq)
            for part, val in enumerate((unrotate(dq), unrotate(dk) * QK_SCALE, dv)):
                dproj_ref[:, part * D + h * DH:part * D + (h + 1) * DH] = val.astype(BF16)

    rev = lambda c: nc - 1 - c
    chunk_spec = lambda col: pl.BlockSpec((CHUNK, D), lambda c: (rev(c), col))
    const2 = lambda shape: pl.BlockSpec(shape, lambda c: (0, 0))
    const3 = pl.BlockSpec((HEADS, CHUNK, CHUNK), lambda c: (0, 0, 0))
    return _hosted_call(
        body,
        name="retention_bwd",
        grid=(nc,),
        in_specs=[
            pl.BlockSpec(memory_space=pl.ANY),
            chunk_spec(0), chunk_spec(1), chunk_spec(2), chunk_spec(0),
            pl.BlockSpec((1, HEADS, DH, DH), lambda c: (rev(c), 0, 0, 0)),
            const3, const3,
            const2((CHUNK, D)), const2((CHUNK, D)), const2((1, D)),
            pl.BlockSpec((CHUNK, DH), lambda c: (rev(c), 0)),
            pl.BlockSpec((CHUNK, DH), lambda c: (rev(c), 0)),
        ],
        out_specs=[pl.BlockSpec((CHUNK, 3 * D), lambda c: (rev(c), 0))],
        out_shape=[jax.ShapeDtypeStruct(dproj.shape, BF16)],
        aliases={0: 0},
        scratch_shapes=[pltpu.VMEM((HEADS, DH, DH), F32)],
        args=(dproj, proj_bf, proj_bf, proj_bf, do, sprev, intra, intra_t, q_dec, k_dec, c_dec, cos_t, sin_t),
        riders=riders,
    )


def _lru_bwd(dproj, proj, saved, dhl, conv_w, wa, wx, lam, riders=()):
    rows = proj.shape[0]
    TM = _lru_tile(rows)
    nt = rows // TM
    per8 = TM // 8

    def body(dproj_in_ref, x_ref, xprev_ref, h_ref, hprev_ref, c_ref, r_ref, i_ref, la_ref, mult_ref, dhl_ref,
             cw_ref, wa_ref, wx_ref, lam_ref,
             dproj_ref, dwa_ref, dwx_ref, stats_ref, anext_sc, dhnext_sc, dcnext_sc, c_sc, b_sc, dh_sc):
        step = pl.program_id(0)
        i = nt - 1 - step

        @pl.when(step == 0)
        def _():
            anext_sc[...] = jnp.zeros_like(anext_sc)
            dhnext_sc[...] = jnp.zeros_like(dhnext_sc)
            dcnext_sc[...] = jnp.zeros_like(dcnext_sc)
            dwa_ref[...] = jnp.zeros_like(dwa_ref)
            dwx_ref[...] = jnp.zeros_like(dwx_ref)
            stats_ref[...] = jnp.zeros_like(stats_ref)

        first = i == 0
        x = x_ref[...].astype(F32)
        prev8 = jnp.where(first, 0.0, xprev_ref[8:16, :].astype(F32))
        c_b = c_ref[...]
        c, r, gate_i, mult = (ref[...].astype(F32) for ref in (c_ref, r_ref, i_ref, mult_ref))
        a = jnp.exp(la_ref[...].astype(F32))
        sp = _softplus(-lam_ref[...])
        dh_sc[...] = dhl_ref[...].astype(F32)

        b_sc[...] = pltpu.roll(a, TM - 1, axis=0)
        b_sc[TM - 1:TM, :] = anext_sc[0:1, :]
        anext_sc[0:1, :] = a[0:1, :]
        row8 = lax.broadcasted_iota(jnp.int32, (8, D), 0)

        def group(gi, dhnext):
            r0 = pl.multiple_of((per8 - 1 - gi) * 8, 8)
            bb = b_sc[pl.ds(r0, 8), :]
            uu = dh_sc[pl.ds(r0, 8), :]
            for d in (1, 2, 4):
                b_sh = jnp.where(row8 < 8 - d, pltpu.roll(bb, 8 - d, axis=0), 1.0)
                u_sh = jnp.where(row8 < 8 - d, pltpu.roll(uu, 8 - d, axis=0), 0.0)
                uu = uu + bb * u_sh
                bb = bb * b_sh
            dhb = bb * dhnext + uu
            dh_sc[pl.ds(r0, 8), :] = dhb
            return dhb[0:1, :]

        dhfirst = lax.fori_loop(0, per8, group, dhnext_sc[0:1, :])
        dhnext_sc[0:1, :] = dhfirst
        dh = dh_sc[...]

        h = h_ref[...].astype(F32)
        hprev8 = jnp.where(first, 0.0, hprev_ref[8:16, :].astype(F32))
        h_dn, h_head = _shift_down(h, hprev8, 1)
        c_sc[...] = h_dn
        c_sc[0:8, :] = h_head
        h_before = c_sc[...]

        valid = _row_ids(i, TM, (TM, D)) >= PAD_ROWS
        da = dh * h_before
        du = jnp.where(valid, dh, 0.0)
        dmult = du * gate_i * c
        dgate_i = du * mult * c
        dc = du * mult * gate_i
        dla = da * a - dmult * (a * a) / mult
        dla = jnp.where(valid, dla, 0.0)
        dr = dla * ((-LRU_C) * sp)
        dzr = dr * r * (1.0 - r)
        dzi = dgate_i * gate_i * (1.0 - gate_i)
        stats_ref[1:2, :] += jnp.sum(dzr, axis=0, keepdims=True)
        stats_ref[2:3, :] += jnp.sum(dzi, axis=0, keepdims=True)
        stats_ref[3:4, :] += jnp.sum(dla * ((-LRU_C) * r), axis=0, keepdims=True)
        dc_gate = []
        for g in range(LRU_BLOCKS):
            sl = slice(g * LRU_BLOCK, (g + 1) * LRU_BLOCK)
            cg = c_b[:, sl]
            dzr_g = dzr[:, sl].astype(BF16)
            dzi_g = dzi[:, sl].astype(BF16)
            dc_gate.append(_dot_nt(dzr_g, wa_ref[g]) + _dot_nt(dzi_g, wx_ref[g]))
            dwa_ref[g] += _dot_tn(cg, dzr_g)
            dwx_ref[g] += _dot_tn(cg, dzi_g)
        dc = dc + jnp.concatenate(dc_gate, axis=1)

        cw = cw_ref[...]
        stats_ref[0:1, :] += jnp.sum(dc, axis=0, keepdims=True)
        stats_ref[7:8, :] += jnp.sum(dc * x, axis=0, keepdims=True)
        dx = cw[3:4] * dc
        tail_src = jnp.concatenate([dc[TM - 8:TM], dcnext_sc[...]], axis=0)
        dx_tail = cw[3:4] * dc[TM - 8:TM]
        for d in (1, 2, 3):
            dx = dx + cw[3 - d:4 - d] * pltpu.roll(dc, TM - d, axis=0)
            dx_tail = dx_tail + cw[3 - d:4 - d] * pltpu.roll(tail_src, 16 - d, axis=0)[0:8]
            rolled, hd = _shift_down(x, prev8, d)
            b_sc[...] = rolled
            b_sc[0:8, :] = hd
            stats_ref[7 - d:8 - d, :] += jnp.sum(dc * b_sc[...], axis=0, keepdims=True)
        dcnext_sc[...] = dc[0:8]
        dproj_ref[...] = dx.astype(BF16)
        dproj_ref[TM - 8:TM, :] = dx_tail.astype(BF16)

    rev = lambda s: nt - 1 - s
    vec = pl.BlockSpec((1, D), lambda s: (0, 0))
    wspec = pl.BlockSpec((LRU_BLOCKS, LRU_BLOCK, LRU_BLOCK), lambda s: (0, 0, 0))
    prev16 = lambda col: pl.BlockSpec((BF16_ROWS, D), lambda s: (jnp.maximum(rev(s) * (TM // BF16_ROWS) - 1, 0), col))
    tile = lambda: pl.BlockSpec((TM, D), lambda s: (rev(s), 0))
    h_lru, c_sv, r_sv, i_sv, la_sv, mult_sv = saved
    return _hosted_call(
        body,
        name="lru_bwd",
        grid=(nt,),
        in_specs=[
            pl.BlockSpec(memory_space=pl.ANY),
            pl.BlockSpec((TM, D), lambda s: (rev(s), LRU_IN_COL)), prev16(LRU_IN_COL),
            tile(), prev16(0),
            tile(), tile(), tile(), tile(), tile(), tile(),
            pl.BlockSpec((4, D), lambda s: (0, 0)),
            wspec, wspec, vec,
        ],
        out_specs=[
            pl.BlockSpec((TM, D), lambda s: (rev(s), LRU_IN_COL)),
            wspec, wspec,
            pl.BlockSpec((8, D), lambda s: (0, 0)),
        ],
        out_shape=[
            jax.ShapeDtypeStruct(dproj.shape, BF16),
            jax.ShapeDtypeStruct((LRU_BLOCKS, LRU_BLOCK, LRU_BLOCK), F32),
            jax.ShapeDtypeStruct((LRU_BLOCKS, LRU_BLOCK, LRU_BLOCK), F32),
            jax.ShapeDtypeStruct((8, D), F32),
        ],
        aliases={0: 0},
        scratch_shapes=[
            pltpu.VMEM((8, D), F32), pltpu.VMEM((8, D), F32), pltpu.VMEM((8, D), F32),
            pltpu.VMEM((TM, D), F32), pltpu.VMEM((TM, D), F32), pltpu.VMEM((TM, D), F32),
        ],
        args=(dproj, proj, proj, h_lru, h_lru, c_sv, r_sv, i_sv, la_sv, mult_sv, dhl, conv_w, wa, wx, lam),
        riders=riders,
    )


def _in_proj_bwd(dproj, w_in, part, prev=None, riders=()):
    rows = dproj.shape[0]
    tm = _heavy_tile(rows)
    nt = rows // tm
    split = max(1, nt // 4 + 1) if nt > 1 else 1
    first = 0 if part == 0 else split
    count = split if part == 0 else nt - split

    def body(*refs):
        dproj_ref, w_hbm, du_ref, w_sc, w_sem = refs[-5:]

        @pl.when(pl.program_id(0) == 0)
        def _():
            moves = [((0, 3), 0), ((4, 1), 3), ((3, 1), 4), ((5, 3), 5)]
            copies = [pltpu.make_async_copy(w_hbm.at[:, pl.ds(src * D, n * D)], w_sc.at[:, pl.ds(dst * D, n * D)], w_sem.at[q])
                      for q, ((src, n), dst) in enumerate(moves)]
            for cp in copies:
                cp.start()
            for cp in copies:
                cp.wait()

        for a in range(0, D, D // 2):
            du_ref[:, a:a + D // 2] = _dot_nt(dproj_ref[...], w_sc[a:a + D // 2, :])

    in_specs = [pl.BlockSpec((tm, IN_COLS), lambda i: (first + i, 0)), _ANY]
    args = (dproj, w_in)
    if prev is not None:
        in_specs = [_ANY] + in_specs
        args = (prev,) + args
    return _hosted_call(
        body,
        name="in_proj_bwd_%d" % part,
        grid=(count,),
        in_specs=in_specs,
        out_specs=[pl.BlockSpec((tm, D), lambda i: (first + i, 0))],
        out_shape=[jax.ShapeDtypeStruct((rows, D), F32)],
        scratch_shapes=[pltpu.VMEM((D, IN_COLS), BF16), pltpu.SemaphoreType.DMA((4,))],
        aliases={0: 0} if prev is not None else None,
        args=args,
        riders=riders,
    )


def _norm1_bwd(du, dh1, head, x2d, norm_w, riders=()):
    rows = du.shape[0]

    def body(du_ref, dh1_ref, head_ref, x_ref, nw_ref, gx_ref, ghead_ref, stats_ref):
        i = pl.program_id(0)

        @pl.when(i == 0)
        def _():
            stats_ref[...] = jnp.zeros_like(stats_ref)

        def finish(h0, out_ref):
            du = du_ref[...]
            rs = lax.rsqrt(jnp.mean(h0 * h0, axis=-1, keepdims=True) + EPS)
            n = h0 * rs
            stats_ref[0:1, :] += jnp.sum(du * n, axis=0, keepdims=True)
            dn = du * nw_ref[...]
            out_ref[...] = dh1_ref[...] + rs * (dn - n * jnp.mean(dn * n, axis=-1, keepdims=True))

        @pl.when(i == 0)
        def _():
            finish(head_ref[...], ghead_ref)

        @pl.when(i > 0)
        def _():
            finish(x_ref[...], gx_ref)

    tile = pl.BlockSpec((TM, D), lambda i: (i, 0))
    return _hosted_call(
        body,
        name="norm1_bwd",
        grid=(rows // TM,),
        in_specs=[
            tile, tile,
            pl.BlockSpec((FRONT, D), lambda i: (0, 0)),
            pl.BlockSpec((TM, D), lambda i: (jnp.maximum(i - 1, 0), 0)),
            pl.BlockSpec((1, D), lambda i: (0, 0)),
        ],
        out_specs=[
            pl.BlockSpec((TM, D), lambda i: (jnp.maximum(i - 1, 0), 0)),
            pl.BlockSpec((FRONT, D), lambda i: (0, 0)),
            pl.BlockSpec((8, D), lambda i: (0, 0)),
        ],
        out_shape=[
            jax.ShapeDtypeStruct(x2d.shape, F32),
            jax.ShapeDtypeStruct((FRONT, D), F32),
            jax.ShapeDtypeStruct((8, D), F32),
        ],
        scratch_shapes=[],
        args=(du, dh1, head, x2d, norm_w),
        riders=riders,
    )


def _matmul_tn(name, x, dy, out_cols, col0=0, prev=None, k_block=None, n_block=None, riders=(), col_map=None):
    col_map = col_map or (lambda n: n)
    rows, kdim = x.shape
    ndim = dy.shape[1]
    kb = k_block or kdim
    nb = n_block or ndim
    step_bytes = lambda t: 2 * t * (kb * x.dtype.itemsize + nb * dy.dtype.itemsize) + 2 * kb * nb * 4
    tr = next(t for t in (2816, 1408, TM_HEAVY, TM) if rows % t == 0 and (t == TM or step_bytes(t) <= TN_VMEM_BUDGET))
    nr, nk, nn = rows // tr, kdim // kb, ndim // nb
    cb0 = col0 // nb

    def body(*refs):
        x_ref, dy_ref, out_ref = refs[-3], refs[-2], refs[-1]
        cut = (nb // 2) // LANES * LANES
        halves = ((0, cut), (cut, nb)) if cut else ((0, nb),)

        @pl.when(pl.program_id(2) == 0)
        def _():
            for a, b in halves:
                out_ref[:, a:b] = _dot_tn(x_ref[...].astype(BF16), dy_ref[:, a:b].astype(BF16))

        @pl.when(pl.program_id(2) > 0)
        def _():
            for a, b in halves:
                out_ref[:, a:b] += _dot_tn(x_ref[...].astype(BF16), dy_ref[:, a:b].astype(BF16))

    in_specs = [
        pl.BlockSpec((tr, kb), lambda n, k, r: (r, k)),
        pl.BlockSpec((tr, nb), lambda n, k, r: (r, n)),
    ]
    args = [x, dy]
    aliases = {}
    if prev is not None:
        in_specs = [pl.BlockSpec(memory_space=pl.ANY)] + in_specs
        args = [prev] + args
        aliases = {0: 0}
    (out,), rider_outs = _hosted_call(
        body,
        name=name,
        grid=(nn, nk, nr),
        in_specs=in_specs,
        out_specs=[pl.BlockSpec((kb, nb), lambda n, k, r: (k, cb0 + col_map(n)))],
        out_shape=[jax.ShapeDtypeStruct((kdim, out_cols), F32)],
        scratch_shapes=[],
        aliases=aliases,
        args=args,
        riders=riders,
    )
    return (out, rider_outs) if riders else out


def _local_step(x2d, target, w, plan):
    rows = FRONT + x2d.shape[0]
    head = jnp.concatenate([jnp.zeros((PAD_ROWS, D), F32), w["meta_tokens"]], axis=0)
    cos_t, sin_t = _rope_tables(rows)
    intra, intra_t, q_dec, k_dec, c_dec = _decay_tables()
    grads = {}

    def hosted(host, fn, *args, **kwargs):
        outs, rider_outs = fn(*args, riders=plan.riders(host, w, grads), **kwargs)
        plan.after(host, rider_outs, w, grads)
        return outs

    (u1,), _ = _norm1(head, x2d, w["mix_norm_w"])
    proj, w["w_in"] = hosted("in_proj", _in_proj, u1, w["w_in_shard"], w["route"], cos_t, sin_t)
    o, sprev = hosted("retention_fwd", _retention_fwd, proj, intra, q_dec, k_dec, c_dec)
    lru_args = (w["conv_w"], w["conv_b"], w["lru_wa"], w["lru_wx"], w["lru_ba"], w["lru_bx"], w["lru_lambda"])
    lru_saved = hosted("lru_fwd", _lru_fwd, proj, *lru_args)
    h_lru = lru_saved[0]
    y_ret, y_lru, h1, u2, rstd2 = hosted("mix_fwd", _mix_fwd, head, x2d, o, proj, h_lru, w["w_branch_ret"],
                                         w["w_branch_lru"], w["w_out"], w["ffn_norm_w"])
    silu, dsilu, act = hosted("ffn_fwd_gate", _ffn_fwd_gate, u2, w["w_ffn_in"])
    dh2, stats_loss = _ffn_out_loss(act, h1, w["w_ffn_out"], target, w["final_norm_w"])

    dg, dup, dh2b, dh1, stats_ffn = _ffn_bwd(dh2, silu, dsilu, h1, rstd2, w["w_ffn_in"], w["w_ffn_out"], w["ffn_norm_w"])
    grads["w_ffn_in"] = _matmul_tn("dw_ffn_up", u2, dup, 2 * FFN, col0=FFN, n_block=FFN_HALF,
                                   prev=_matmul_tn("dw_ffn_gate", u2, dg, 2 * FFN, n_block=FFN_HALF))
    grads["w_ffn_out"] = _matmul_tn("dw_ffn_out", act, dh2b, D, k_block=FFN_HALF)
    (dproj, do, dhl, mixed, a_ret, a_lru, dy_ret, dy_lru) = hosted(
        "mix_bwd", _mix_bwd, dh1, o, proj, h_lru, y_ret, y_lru, w["w_branch_ret"], w["w_branch_lru"], w["w_out"])
    grads["w_out"] = _matmul_tn("dw_out", mixed, dh1, D)
    grads["w_branch_ret"] = _matmul_tn("dw_branch_ret", a_ret, dy_ret, D)
    grads["w_branch_lru"] = _matmul_tn("dw_branch_lru", a_lru, dy_lru, D)
    (dproj,) = hosted("retention_bwd", _retention_bwd, dproj, proj, do, sprev, intra, intra_t, q_dec, k_dec, c_dec,
                      cos_t, sin_t)
    dproj, grads["lru_wa"], grads["lru_wx"], stats_lru = hosted(
        "lru_bwd", _lru_bwd, dproj, proj, lru_saved, dhl, w["conv_w"], w["lru_wa"], w["lru_wx"], w["lru_lambda"])
    grads["w_in"], rider_outs = _matmul_tn("dw_in", u1, dproj, IN_COLS, n_block=D, col_map=_swap_3_4,
                                           riders=plan.riders("dw_in", w, grads))
    plan.after("dw_in", rider_outs, w, grads)
    (du1,) = hosted("in_proj_bwd_0", _in_proj_bwd, dproj, w["w_in"], 0)
    (du1,) = hosted("in_proj_bwd_1", _in_proj_bwd, dproj, w["w_in"], 1, du1)
    (grad_x, grad_head, stats_in), _ = _norm1_bwd(du1, dh1, head, x2d, w["mix_norm_w"])
    return grad_x, grad_head, grads, [stats_loss, stats_ffn, stats_in, stats_lru]


BIG_PIECES = {
    "w_in": ("col", (D, 2 * D)),
    "w_ffn_in": ("col", (D, FFN_HALF)),
    "w_ffn_out": ("row", (FFN // 4, D)),
    "w_branch_ret": ("row", (D // 4, D)),
    "w_branch_lru": ("row", (D // 4, D)),
    "w_out": ("row", (D // 4, D)),
    "lru_wa": ("lru", (LRU_BLOCKS, LRU_BLOCK // 4, LRU_BLOCK)),
    "lru_wx": ("lru", (LRU_BLOCKS, LRU_BLOCK // 4, LRU_BLOCK)),
}
SMALL_PIECES = {"meta_tokens": ("col", (N_META, D // 4)), "conv_w": ("col", (4, D // 4))}


def _full_shape(kind, shard):
    if kind == "col":
        return (shard[0], 4 * shard[1])
    if kind == "row":
        return (4 * shard[0], shard[1])
    return (shard[0], 4 * shard[1], shard[2])


def _half_shape(kind, shard):
    return (shard[0] // 2,) + tuple(shard[1:])


def _aligned(start, multiple):
    return start if isinstance(start, int) else pl.multiple_of(start, multiple)


def _lead(h, size):
    if h is None:
        return pl.ds(0, size)
    return pl.ds(_aligned(h * (size // 2), size // 2), size // 2)


def _full_region(ref, kind, shard, s, h):
    if kind == "col":
        return ref.at[_lead(h, shard[0]), pl.ds(_aligned(s * shard[1], shard[1]), shard[1])]
    if kind == "row":
        size = shard[0] if h is None else shard[0] // 2
        start = s * shard[0] + (0 if h is None else h * (shard[0] // 2))
        return ref.at[pl.ds(_aligned(start, BF16_ROWS), size), :]
    return ref.at[_lead(h, shard[0]), pl.ds(_aligned(s * shard[1], shard[1]), shard[1]), :]


def _shard_region(ref, shard, h):
    return ref.at[_lead(h, shard[0])]


def _place():
    x, y, c = lax.axis_index("x"), lax.axis_index("y"), lax.axis_index("c")
    return x, y, c, 2 * x + y


def _other_chip(s, c, k):
    s2 = jnp.bitwise_xor(s, k)
    return s2, (s2 // 2, s2 % 2, c)


def _remote(src, dst, send_sem, recv_sem, dev):
    return pltpu.make_async_remote_copy(src_ref=src, dst_ref=dst, send_sem=send_sem, recv_sem=recv_sem,
                                        device_id=dev, device_id_type=MESH)


_ANY = pl.BlockSpec(memory_space=pl.ANY)


class _Rider:
    def __init__(self, ins, out_shapes, sem_shapes, build, aliased=False):
        self.ins, self.out_shapes, self.sem_shapes, self.build, self.aliased = ins, out_shapes, sem_shapes, build, aliased


def _hosted_call(body, *, name, grid, in_specs, out_specs, out_shape, scratch_shapes, args, riders=(), aliases=None,
                 prefetch=None, riders_after_body=False):
    n_in, n_out, n_sc = len(in_specs), len(out_shape), len(scratch_shapes)
    r_in = [a for r in riders for a in r.ins]
    r_out = [s for r in riders for s in r.out_shapes]
    r_sem = [s for r in riders for s in r.sem_shapes]
    lead = () if prefetch is None else (prefetch,)
    assert prefetch is None or not (aliases or any(r.aliased for r in riders))

    def full_body(*refs):
        head, refs = refs[:len(lead)], refs[len(lead):]
        ins, rin = refs[:n_in], refs[n_in:n_in + len(r_in)]
        o0 = n_in + len(r_in)
        outs, rout = refs[o0:o0 + n_out], refs[o0 + n_out:o0 + n_out + len(r_out)]
        s0 = o0 + n_out + len(r_out)
        scratch, rsem = refs[s0:s0 + n_sc], refs[s0 + n_sc:]
        starts, waits = [], []
        pi = po = ps = 0
        for r in riders:
            st, wt = r.build(rin[pi:pi + len(r.ins)], rout[po:po + len(r.out_shapes)], rsem[ps:ps + len(r.sem_shapes)])
            starts += st
            waits += wt
            pi, po, ps = pi + len(r.ins), po + len(r.out_shapes), ps + len(r.sem_shapes)
        first = functools.reduce(jnp.logical_and, [pl.program_id(d) == 0 for d in range(len(grid))])
        last = functools.reduce(jnp.logical_and, [pl.program_id(d) == grid[d] - 1 for d in range(len(grid))])

        def start_riders():
            @pl.when(first)
            def _():
                for cp in starts:
                    cp.start()

        if riders and not riders_after_body:
            start_riders()
        body(*head, *ins, *outs, *scratch)
        if riders and riders_after_body:
            start_riders()
        if riders:
            @pl.when(last)
            def _():
                for wait in waits:
                    wait()

    io_aliases = dict(aliases or {})
    pi = po = 0
    for r in riders:
        if r.aliased:
            for q in range(len(r.ins)):
                io_aliases[n_in + pi + q] = n_out + po + q
        pi, po = pi + len(r.ins), po + len(r.out_shapes)
    specs = dict(
        grid=grid,
        in_specs=list(in_specs) + [_ANY] * len(r_in),
        out_specs=list(out_specs) + [_ANY] * len(r_out),
        scratch_shapes=list(scratch_shapes) + r_sem,
    )
    if prefetch is not None:
        specs = dict(grid_spec=pltpu.PrefetchScalarGridSpec(num_scalar_prefetch=1, **specs))
    res = pl.pallas_call(
        full_body,
        name=name,
        out_shape=list(out_shape) + r_out,
        input_output_aliases=io_aliases,
        compiler_params=_params(("arbitrary",) * len(grid)),
        **specs,
    )(*lead, *args, *r_in)
    rider_outs, po = [], n_out
    for r in riders:
        rider_outs.append(list(res[po:po + len(r.out_shapes)]))
        po += len(r.out_shapes)
    return list(res[:n_out]), rider_outs


def _run_riders(name, riders):
    r_in = [a for r in riders for a in r.ins]
    r_out = [s for r in riders for s in r.out_shapes]
    r_sem = [s for r in riders for s in r.sem_shapes]

    def body(*refs):
        rin, rout, rsem = refs[:len(r_in)], refs[len(r_in):len(r_in) + len(r_out)], refs[len(r_in) + len(r_out):]
        pi = po = ps = 0
        all_waits = []
        for r in riders:
            starts, waits = r.build(rin[pi:pi + len(r.ins)], rout[po:po + len(r.out_shapes)], rsem[ps:ps + len(r.sem_shapes)])
            for cp in starts:
                cp.start()
            all_waits += waits
            pi, po, ps = pi + len(r.ins), po + len(r.out_shapes), ps + len(r.sem_shapes)
        for wait in all_waits:
            wait()

    io_aliases = {}
    pi = po = 0
    for r in riders:
        if r.aliased:
            for q in range(len(r.ins)):
                io_aliases[pi + q] = po + q
        pi, po = pi + len(r.ins), po + len(r.out_shapes)
    res = pl.pallas_call(
        body,
        name=name,
        in_specs=[_ANY] * len(r_in),
        out_specs=[_ANY] * len(r_out),
        out_shape=r_out,
        scratch_shapes=r_sem,
        input_output_aliases=io_aliases,
    )(*r_in)
    outs, po = [], 0
    for r in riders:
        outs.append(list(res[po:po + len(r.out_shapes)]))
        po += len(r.out_shapes)
    return outs


def _piece(name):
    if name in BIG_PIECES:
        return (name, *BIG_PIECES[name], True)
    return (name, *SMALL_PIECES[name], False)


def _gather_rider(shards, names):
    pieces = [_piece(n) for n in names]
    n = len(pieces)

    def build(ins, outs, sems):
        local_sem, ici_send, ici_recv = sems
        _, _, c, s = _place()
        starts, waits = [], []
        for p, (_, kind, shard, split) in enumerate(pieces):
            cp = pltpu.make_async_copy(ins[p], _full_region(outs[p], kind, shard, s, None), local_sem.at[p])
            starts.append(cp)
            waits.append(cp.wait)
            h = c if split else None
            for k in (1, 2, 3):
                s2, dev = _other_chip(s, c, k)
                cp = _remote(_shard_region(ins[p], shard, h), _full_region(outs[p], kind, shard, s, h),
                             ici_send.at[p, k - 1], ici_recv.at[p, k - 1], dev)
                starts.append(cp)
                waits.append(cp.wait_send)
                region = _full_region(outs[p], kind, shard, s2, h)
                waits.append(_remote(region, region, ici_send.at[p, k - 1], ici_recv.at[p, k - 1], dev).wait_recv)
        return starts, waits

    return _Rider(
        [shards[name] for name in names],
        [jax.ShapeDtypeStruct(_full_shape(kind, shard), shards[name].dtype) for name, kind, shard, _ in pieces],
        [pltpu.SemaphoreType.DMA((n,)), pltpu.SemaphoreType.DMA((n, 3)), pltpu.SemaphoreType.DMA((n, 3))],
        build)


def _forward_rider(gathered, names):
    pieces = [_piece(n) for n in names]
    n = len(pieces)

    def build(ins, outs, sems):
        fwd_send, fwd_recv = sems
        x, y, c, s = _place()
        sibling = (x, y, 1 - c)
        starts, waits = [], []
        for p, (_, kind, shard, _) in enumerate(pieces):
            for k in (1, 2, 3):
                s2, _ = _other_chip(s, c, k)
                mine = _full_region(outs[p], kind, shard, s2, c)
                theirs = _full_region(outs[p], kind, shard, s2, 1 - c)
                cp = _remote(mine, mine, fwd_send.at[p, k - 1], fwd_recv.at[p, k - 1], sibling)
                starts.append(cp)
                waits.append(cp.wait_send)
                waits.append(_remote(theirs, theirs, fwd_send.at[p, k - 1], fwd_recv.at[p, k - 1], sibling).wait_recv)
        return starts, waits

    return _Rider(
        [gathered[name] for name in names],
        [jax.ShapeDtypeStruct(gathered[name].shape, gathered[name].dtype) for name in names],
        [pltpu.SemaphoreType.DMA((n, 3)), pltpu.SemaphoreType.DMA((n, 3))],
        build, aliased=True)


def _pair_exchange_rider(grads, names):
    n = len(names)

    def build(ins, outs, sems):
        send_sem, recv_sem = sems
        x, y, c, _ = _place()
        sibling = (x, y, 1 - c)
        starts, waits = [], []
        for p, name in enumerate(names):
            kind, shard = BIG_PIECES[name]
            for s2 in range(4):
                cp = _remote(_full_region(ins[p], kind, shard, s2, 1 - c), outs[p].at[s2], send_sem.at[p, s2],
                             recv_sem.at[p, s2], sibling)
                starts.append(cp)
                waits.append(cp.wait_send)
                waits.append(_remote(outs[p].at[s2], outs[p].at[s2], send_sem.at[p, s2], recv_sem.at[p, s2], sibling).wait_recv)
        return starts, waits

    return _Rider(
        [grads[name] for name in names],
        [jax.ShapeDtypeStruct((4,) + _half_shape(*BIG_PIECES[name]), F32) for name in names],
        [pltpu.SemaphoreType.DMA((n, 4))] * 2,
        build)


def _half_specs(kind, shard):
    half = shard[0] // 2
    if kind == "col":
        full = pl.BlockSpec((half, shard[1]), lambda j, pr: (pr[1], j))
        buf = pl.BlockSpec((None, half, shard[1]), lambda j, pr: (j, 0, 0))
    elif kind == "row":
        full = pl.BlockSpec((half, shard[1]), lambda j, pr: (2 * j + pr[1], 0))
        buf = pl.BlockSpec((None, half, shard[1]), lambda j, pr: (j, 0, 0))
    else:
        full = pl.BlockSpec((half, shard[1], shard[2]), lambda j, pr: (pr[1], j, 0))
        buf = pl.BlockSpec((None, half, shard[1], shard[2]), lambda j, pr: (j, 0, 0, 0))
    return full, buf


def _pair_sum(name, grad, recv, place):
    kind, shard = BIG_PIECES[name]
    full, buf = _half_specs(kind, shard)

    def body(pr, g_ref, r_ref, o_ref):
        o_ref[...] = (g_ref[...] + r_ref[...]).astype(BF16)

    return pl.pallas_call(
        body,
        name="pair_sum_" + name,
        grid_spec=pltpu.PrefetchScalarGridSpec(num_scalar_prefetch=1, grid=(4,), in_specs=[full, buf], out_specs=buf),
        out_shape=jax.ShapeDtypeStruct((4,) + _half_shape(kind, shard), BF16),
        compiler_params=_params(("arbitrary",)),
    )(place, grad, recv)


def _chip_exchange_rider(sums, names):
    n = len(names)

    def build(ins, outs, sems):
        send_sem, recv_sem = sems
        _, _, c, s = _place()
        starts, waits = [], []
        for p in range(n):
            for k in (1, 2, 3):
                s2, dev = _other_chip(s, c, k)
                cp = _remote(ins[p].at[s2], outs[p].at[k - 1], send_sem.at[p, k - 1], recv_sem.at[p, k - 1], dev)
                starts.append(cp)
                waits.append(cp.wait_send)
                waits.append(_remote(outs[p].at[k - 1], outs[p].at[k - 1], send_sem.at[p, k - 1], recv_sem.at[p, k - 1],
                                     dev).wait_recv)
        return starts, waits

    return _Rider(
        [sums[name] for name in names],
        [jax.ShapeDtypeStruct((3,) + _half_shape(*BIG_PIECES[name]), BF16) for name in names],
        [pltpu.SemaphoreType.DMA((n, 3))] * 2,
        build)


def _chip_sum(name, grad, recv_pair, recv_chip, place):
    kind, shard = BIG_PIECES[name]
    half = shard[0] // 2
    tail = tuple(shard[1:])
    zeros = (0,) * len(tail)
    nt = 4 if kind != "lru" and half % (4 * BF16_ROWS) == 0 else 1
    rows = half // nt
    if kind == "col":
        full = pl.BlockSpec((rows,) + tail, lambda j, pr: (pr[1] * nt + j, pr[0]))
    elif kind == "row":
        full = pl.BlockSpec((rows,) + tail, lambda j, pr: ((2 * pr[0] + pr[1]) * nt + j, 0))
    else:
        full = pl.BlockSpec((rows,) + tail, lambda j, pr: (pr[1], pr[0], 0))
    pair = pl.BlockSpec((None, rows) + tail, lambda j, pr: (pr[0], j) + zeros)
    chip = pl.BlockSpec((3, rows) + tail, lambda j, pr: (0, j) + zeros)
    out = pl.BlockSpec((rows,) + tail, lambda j, pr: (pr[1] * nt + j,) + zeros)

    def body(pr, g_ref, rp_ref, rc_ref, o_ref):
        total = g_ref[...] + rp_ref[...]
        for k in range(3):
            total = total + rc_ref[k].astype(F32)
        o_ref[...] = total

    return pl.pallas_call(
        body,
        name="chip_sum_" + name,
        grid_spec=pltpu.PrefetchScalarGridSpec(num_scalar_prefetch=1, grid=(nt,), in_specs=[full, pair, chip], out_specs=out),
        out_shape=jax.ShapeDtypeStruct(shard, F32),
        compiler_params=_params(("arbitrary",)),
    )(place, grad, recv_pair, recv_chip)


def _sibling_exchange_rider(halves, names):
    n = len(names)

    def build(ins, outs, sems):
        send_sem, recv_sem = sems
        x, y, c, _ = _place()
        sibling = (x, y, 1 - c)
        starts, waits = [], []
        for p, name in enumerate(names):
            shard = BIG_PIECES[name][1]
            mine = _shard_region(outs[p], shard, c)
            theirs = _shard_region(outs[p], shard, 1 - c)
            cp = _remote(mine, mine, send_sem.at[p], recv_sem.at[p], sibling)
            starts.append(cp)
            waits.append(cp.wait_send)
            waits.append(_remote(theirs, theirs, send_sem.at[p], recv_sem.at[p], sibling).wait_recv)
        return starts, waits

    return _Rider(
        [halves[name] for name in names],
        [jax.ShapeDtypeStruct(BIG_PIECES[name][1], F32) for name in names],
        [pltpu.SemaphoreType.DMA((n,))] * 2,
        build, aliased=True)


FIRST_WEIGHTS = ["meta_tokens", "conv_w"]
WEIGHT_GROUPS = {
    "lru": ["lru_wa", "lru_wx"],
    "branch": ["w_branch_ret", "w_branch_lru", "w_out"],
    "ffn_in": ["w_ffn_in"],
    "ffn_out": ["w_ffn_out"],
}
WEIGHT_SCHEDULE = {
    "in_proj": [("gather", "lru")],
    "retention_fwd": [("forward", "lru"), ("gather", "branch")],
    "lru_fwd": [("forward", "branch"), ("gather", "ffn_in")],
    "mix_fwd": [("forward", "ffn_in"), ("gather", "ffn_out")],
    "ffn_fwd_gate": [("forward", "ffn_out")],
}
GRAD_GROUPS = {
    "ffn_in": ["w_ffn_in"],
    "ffn_out": ["w_ffn_out"],
    "mixer": ["w_out", "w_branch_ret", "w_branch_lru", "lru_wa", "lru_wx"],
    "in": ["w_in"],
}
GRAD_SCHEDULE = {
    "mix_bwd": [("pair", "ffn_in"), ("pair", "ffn_out")],
    "retention_bwd": [("chip", "ffn_out")],
    "lru_bwd": [("chip", "ffn_in"), ("sibling", "ffn_out")],
    "dw_in": [("sibling", "ffn_in"), ("pair", "mixer")],
    "in_proj_bwd_0": [("chip", "mixer"), ("pair", "in")],
    "in_proj_bwd_1": [("sibling", "mixer"), ("chip", "in")],
}


class _CommPlan:
    def __init__(self, shards, place):
        self.shards, self.place = shards, place
        self.late = {}
        self.recv_pair, self.sums, self.recv_chip, self.halves, self.final = {}, {}, {}, {}, {}

    def _grad_rider(self, stage, group, grads):
        names = GRAD_GROUPS[group]
        if stage == "pair":
            return _pair_exchange_rider(grads, names)
        if stage == "chip":
            return _chip_exchange_rider(self.sums, names)
        return _sibling_exchange_rider(self.halves, names)

    def _grad_after(self, stage, group, outs, grads):
        names = GRAD_GROUPS[group]
        if stage == "pair":
            for n, o in zip(names, outs):
                self.recv_pair[n] = o
                self.sums[n] = _pair_sum(n, grads[n], o, self.place)
        elif stage == "chip":
            for n, o in zip(names, outs):
                self.halves[n] = _chip_sum(n, grads[n], self.recv_pair[n], o, self.place)
        else:
            self.final.update(zip(names, outs))

    def riders(self, host, w, grads):
        if host in WEIGHT_SCHEDULE:
            return [_gather_rider(self.shards, WEIGHT_GROUPS[group]) if stage == "gather"
                    else _forward_rider(self.late, WEIGHT_GROUPS[group]) for stage, group in WEIGHT_SCHEDULE[host]]
        return [self._grad_rider(stage, group, grads) for stage, group in GRAD_SCHEDULE.get(host, [])]

    def after(self, host, rider_outs, w, grads):
        for (stage, group), outs in zip(WEIGHT_SCHEDULE.get(host, []), rider_outs):
            (self.late if stage == "gather" else w).update(zip(WEIGHT_GROUPS[group], outs))
        for (stage, group), outs in zip(GRAD_SCHEDULE.get(host, []), rider_outs):
            self._grad_after(stage, group, outs, grads)

    def finish(self, partial):
        outs, (blocks,) = _run_riders("tail_exchange", [_sibling_exchange_rider(self.halves, GRAD_GROUPS["in"]),
                                                        _small_exchange_rider(partial)])
        self.final.update(zip(GRAD_GROUPS["in"], outs))
        return self.final, blocks


def _adamw_math(w, g, m, v):
    m = ADAM_B1 * m + (1.0 - ADAM_B1) * g
    v = ADAM_B2 * v + (1.0 - ADAM_B2) * (g * g)
    m_hat = m / (1.0 - ADAM_B1 ** ADAM_STEP)
    v_hat = v / (1.0 - ADAM_B2 ** ADAM_STEP)
    delta = -ADAM_LR * (m_hat / (jnp.sqrt(v_hat) + ADAM_EPS) + ADAM_WD * w)
    return delta, m, v


def _adamw(name, g, w, m, v):
    rows, cols = g.shape
    tr = rows // 4 if rows % 32 == 0 else rows

    def body(g_ref, w_ref, m_ref, v_ref, go_ref, d_ref, mo_ref, vo_ref):
        gv = g_ref[...]
        delta, m2, v2 = _adamw_math(w_ref[...], gv, m_ref[...], v_ref[...])
        go_ref[...] = gv
        d_ref[...] = delta
        mo_ref[...] = m2
        vo_ref[...] = v2

    spec = pl.BlockSpec((tr, cols), lambda i: (i, 0))
    return pl.pallas_call(
        body,
        name="adamw_" + name,
        grid=(rows // tr,),
        in_specs=[spec] * 4,
        out_specs=[spec] * 4,
        out_shape=[jax.ShapeDtypeStruct((rows, cols), F32)] * 4,
        compiler_params=_params(("arbitrary",)),
    )(g, w, m, v)


SMALL_ROWS = 48
VEC_ROWS = {"final_norm_w": 1, "ffn_norm_w": 8, "mix_norm_w": 16, "conv_b": 24, "lru_ba": 25, "lru_bx": 26, "lru_lambda": 27}
VEC_NAMES = list(VEC_ROWS)
CONV_W_ROW = 28
META_ROW = 32


def _small_exchange_rider(partial):
    def build(ins, outs, sems):
        local_sem, send_sem, recv_sem = sems
        _, _, c, s = _place()
        me = 2 * s + c
        cp = pltpu.make_async_copy(ins[0], outs[0].at[me], local_sem)
        starts, waits = [cp], [cp.wait]
        for k in range(1, 8):
            peer = jnp.bitwise_xor(me, k)
            dev = (peer // 4, (peer // 2) % 2, peer % 2)
            rd = _remote(ins[0], outs[0].at[me], send_sem.at[k - 1], recv_sem.at[k - 1], dev)
            starts.append(rd)
            waits.append(rd.wait_send)
            waits.append(_remote(ins[0], outs[0].at[peer], send_sem.at[k - 1], recv_sem.at[k - 1], dev).wait_recv)
        return starts, waits

    return _Rider([partial], [jax.ShapeDtypeStruct((8, SMALL_ROWS, D), F32)],
                  [pltpu.SemaphoreType.DMA, pltpu.SemaphoreType.DMA((7,)), pltpu.SemaphoreType.DMA((7,))], build)


def _small_update(blocks, place, vecs, conv, meta):
    nvec = len(VEC_NAMES)
    qcols = D // 4

    def in_order(ref):
        total = ref[0]
        for d in range(1, 8):
            total = total + ref[d]
        return total

    def body(pr, blocks_ref, cols_ref, *refs):
        vec_refs = refs[:3 * nvec]
        conv_refs = refs[3 * nvec:3 * nvec + 3]
        meta_refs = refs[3 * nvec + 3:3 * nvec + 6]
        outs = refs[3 * nvec + 6:]
        loss_ref, vec_out, conv_out, meta_out = outs[0], outs[1:5], outs[5:9], outs[9:13]
        tot, col = in_order(blocks_ref), in_order(cols_ref)
        loss_ref[...] = jnp.sum(tot[0:1, :], axis=1, keepdims=True)
        for o in vec_out:
            o[...] = jnp.zeros_like(o)
        for j, name in enumerate(VEC_NAMES):
            w, m, v = (r[...] for r in vec_refs[3 * j:3 * j + 3])
            g = tot[VEC_ROWS[name]:VEC_ROWS[name] + 1, :]
            if name == "lru_lambda":
                g = -g / (1.0 + jnp.exp(w))
            for o, val in zip(vec_out, (g,) + _adamw_math(w, g, m, v)):
                o[j:j + 1, :] = val
        for row, n_rows, ins, group in ((CONV_W_ROW, 4, conv_refs, conv_out), (META_ROW, N_META, meta_refs, meta_out)):
            g = col[row:row + n_rows, :]
            w, m, v = (r[...] for r in ins)
            for o, val in zip(group, (g,) + _adamw_math(w, g, m, v)):
                o[...] = val

    whole = lambda shape: pl.BlockSpec(shape, lambda i, pr: (0,) * len(shape))
    in_specs = [whole((8, SMALL_ROWS, D)), pl.BlockSpec((8, SMALL_ROWS, qcols), lambda i, pr: (0, 0, pr[0]))]
    in_specs += [whole((1, D))] * (3 * nvec) + [whole((4, qcols))] * 3 + [whole((N_META, qcols))] * 3
    out_shapes = [(1, 1)] + [(8, D)] * 4 + [(4, qcols)] * 4 + [(N_META, qcols)] * 4
    return pl.pallas_call(
        body,
        name="small_update",
        grid_spec=pltpu.PrefetchScalarGridSpec(num_scalar_prefetch=1, grid=(1,), in_specs=in_specs,
                                               out_specs=[whole(s) for s in out_shapes]),
        out_shape=[jax.ShapeDtypeStruct(s, F32) for s in out_shapes],
        compiler_params=_params(("arbitrary",)),
    )(place, blocks, blocks, *[a for t in vecs for a in t], *conv, *meta)


WEIGHT_ORDER = ["meta_tokens", "mix_norm_w", "w_in", "conv_w", "conv_b", "lru_wa", "lru_ba", "lru_wx", "lru_bx", "lru_lambda",
                "w_branch_ret", "w_branch_lru", "w_out", "ffn_norm_w", "w_ffn_in", "w_ffn_out", "final_norm_w"]


def kernel(x, meta_tokens, mix_norm_w, w_in, conv_w, conv_b, lru_wa, lru_ba, lru_wx, lru_bx, lru_lambda, w_branch_ret, w_branch_lru, w_out, ffn_norm_w, w_ffn_in, w_ffn_out, final_norm_w, loss_target, m_meta_tokens, m_mix_norm_w, m_w_in, m_conv_w, m_conv_b, m_lru_wa, m_lru_ba, m_lru_wx, m_lru_bx, m_lru_lambda, m_w_branch_ret, m_w_branch_lru, m_w_out, m_ffn_norm_w, m_w_ffn_in, m_w_ffn_out, m_final_norm_w, v_meta_tokens, v_mix_norm_w, v_w_in, v_conv_w, v_conv_b, v_lru_wa, v_lru_ba, v_lru_wx, v_lru_bx, v_lru_lambda, v_w_branch_ret, v_w_branch_lru, v_w_out, v_ffn_norm_w, v_w_ffn_in, v_w_ffn_out, v_final_norm_w):
    args = locals()
    wts = {n: args[n] for n in WEIGHT_ORDER}
    mom = {n: args["m_" + n] for n in WEIGHT_ORDER}
    var = {n: args["v_" + n] for n in WEIGHT_ORDER}
    place = jnp.stack([2 * lax.axis_index("x") + lax.axis_index("y"), lax.axis_index("c")]).astype(jnp.int32)

    shards = {n: wts[n][0].astype(BF16) for n in BIG_PIECES}
    shards["meta_tokens"] = wts["meta_tokens"]
    shards["conv_w"] = wts["conv_w"][0]
    plan = _CommPlan(shards, place)
    (first,) = _run_riders("gather_first", [_gather_rider(shards, FIRST_WEIGHTS)])
    w = dict(zip(FIRST_WEIGHTS, first))
    for n in VEC_NAMES:
        w[n] = wts[n].reshape(1, D)
    chips = jnp.bitwise_xor(place[0], jnp.array((0,) + SHARD_ORDER, dtype=jnp.int32))
    w["route"] = jnp.concatenate([place, jnp.stack([2 * chips, 2 * chips + 1], axis=1).reshape(8)])
    w["w_in_shard"] = shards["w_in"]

    grad_x, grad_head, _, stats = _local_step(x[0], loss_target[0], w, plan)
    partial = jnp.concatenate(stats + [grad_head[PAD_ROWS:]], axis=0)
    shard_grads, blocks = plan.finish(partial)

    out = {}
    for n in BIG_PIECES:
        shape2d = (-1, wts[n].shape[-1])
        res = _adamw(n, *[a.reshape(shape2d) for a in (shard_grads[n], wts[n], mom[n], var[n])])
        out[n] = [r.reshape(wts[n].shape) for r in res]

    vecs = [tuple(a[n].reshape(1, D) for a in (wts, mom, var)) for n in VEC_NAMES]
    conv = tuple(a["conv_w"][0] for a in (wts, mom, var))
    meta = tuple(a["meta_tokens"] for a in (wts, mom, var))
    res = _small_update(blocks, place, vecs, conv, meta)
    loss = res[0].reshape(())
    for j, n in enumerate(VEC_NAMES):
        out[n] = [r[j].reshape(wts[n].shape) for r in res[1:5]]
    out["conv_w"] = [r.reshape(wts["conv_w"].shape) for r in res[5:9]]
    out["meta_tokens"] = list(res[9:13])

    return (loss, grad_x.reshape(x.shape)) + tuple(out[n][kind] for kind in range(4) for n in WEIGHT_ORDER)
```
